```python
import math
import jax, jax.numpy as jnp
from jax import lax
import numpy as np

D_MODEL = 1024
BATCH = 8
SEQ = 2048
DEPTH = 4

D_INNER = 2 * D_MODEL
CONV_WIDTH = 4
EPS = 1e-6
LRU_WIDTH = D_INNER // 4
LRU_HEADS = 8
LRU_HEAD_DIM = LRU_WIDTH // LRU_HEADS
LRU_C = 8.0
HG_WIDTH = D_INNER // 4
HG_HEAD_DIM = 128
HG_HEADS = HG_WIDTH // HG_HEAD_DIM
HG_CHUNK = 64
SSD_WIDTH = D_INNER // 2
SSD_HEAD_DIM = 64
SSD_HEADS = SSD_WIDTH // SSD_HEAD_DIM
SSD_GROUPS = 2
SSD_STATE = 128
SSD_CHUNK = 128
SSD_CONV_DIM = SSD_WIDTH + 2 * SSD_GROUPS * SSD_STATE
SPLIT_SIZES = (LRU_WIDTH, LRU_WIDTH,
               HG_WIDTH, HG_WIDTH, HG_WIDTH, HG_WIDTH,
               SSD_WIDTH, SSD_CONV_DIM, SSD_HEADS)
N_IN = sum(SPLIT_SIZES)

kernel_name = "hybrid_rglru_hgrn2_ssd_parallel_heads"


def _split_points():
    return [int(v) for v in np.cumsum(SPLIT_SIZES)[:-1]]


def rmsnorm(x, w):
    xf = x.astype(jnp.float32)
    inv = lax.rsqrt(jnp.mean(xf * xf, axis=-1, keepdims=True) + EPS)
    return (xf * inv).astype(x.dtype) * w


def causal_conv(x, w, b):
    K = w.shape[0]
    S = x.shape[1]
    xp = jnp.pad(x, ((0, 0), (K - 1, 0), (0, 0)))
    out = b
    for k in range(K):
        out = out + xp[:, k:k + S] * w[k]
    return out


def rg_lru(x, wa, ba, wx, bx, lam):
    B, S, _ = x.shape
    xh = x.reshape(B, S, LRU_HEADS, LRU_HEAD_DIM)
    r = jax.nn.sigmoid(jnp.einsum('bshi,hij->bshj', xh, wa) + ba).reshape(B, S, LRU_WIDTH)
    i = jax.nn.sigmoid(jnp.einsum('bshi,hij->bshj', xh, wx) + bx).reshape(B, S, LRU_WIDTH)
    log_a = -LRU_C * r * jax.nn.softplus(-lam)
    a = jnp.exp(log_a)
    mult = jnp.sqrt(-jnp.expm1(2.0 * log_a))
    b = mult * (i * x)

    def combine(left, right):
        a1, b1 = left
        a2, b2 = right
        return a1 * a2, a2 * b1 + b2

    _, h = lax.associative_scan(combine, (a, b), axis=1)
    return h


def hgrn2_chunked(q, k, v, log_f):
    dtype = v.dtype
    q, k, v, log_f = (t.astype(jnp.float32) for t in (q, k, v, log_f))
    B, S, H, DK = q.shape
    DV = v.shape[-1]
    n = S // HG_CHUNK

    def to_chunks(t):
        return t.reshape(B, n, HG_CHUNK, H, t.shape[-1]).transpose(1, 0, 3, 2, 4)

    qc, kc, vc, lc = (to_chunks(t) for t in (q, k, v, log_f))
    causal = jnp.tril(jnp.ones((HG_CHUNK, HG_CHUNK), bool))[:, :, None]

    def step(state, inp):
        qi, ki, vi, li = inp
        cum = jnp.cumsum(li, axis=2)
        diff = cum[:, :, :, None, :] - cum[:, :, None, :, :]
        decay = jnp.where(causal, jnp.exp(jnp.where(causal, diff, 0.0)), 0.0)
        scores = jnp.einsum('bhtd,bhsd,bhtsd->bhts', qi, ki, decay)
        o = (jnp.einsum('bhts,bhsv->bhtv', scores, vi)
             + jnp.einsum('bhtd,bhdv->bhtv', qi * jnp.exp(cum), state))
        last = cum[:, :, -1:, :]
        state = (state * jnp.exp(last[:, :, 0, :, None])
                 + jnp.einsum('bhsd,bhsv->bhdv', ki * jnp.exp(last - cum), vi))
        return state, o

    s0 = jnp.zeros((B, H, DK, DV), jnp.float32)
    _, o = lax.scan(step, s0, (qc, kc, vc, lc))
    return o.transpose(1, 0, 3, 2, 4).reshape(B, S, H, DV).astype(dtype)


def ssd_chunked(x, dt, A, Bm, Cm):
    dtype = x.dtype
    x, dt, A, Bm, Cm = (t.astype(jnp.float32) for t in (x, dt, A, Bm, Cm))
    B, S, H, P = x.shape
    G, N = Bm.shape[2], Bm.shape[3]
    J = H // G
    C = SSD_CHUNK
    n = S // C
    xdt = (x * dt[..., None]).reshape(B, n, C, G, J, P).transpose(1, 0, 2, 3, 4, 5)
    dA = (dt * A).reshape(B, n, C, G, J).transpose(1, 0, 2, 3, 4)
    Bc = Bm.reshape(B, n, C, G, N).transpose(1, 0, 2, 3, 4)
    Cc = Cm.reshape(B, n, C, G, N).transpose(1, 0, 2, 3, 4)
    causal = jnp.tril(jnp.ones((C, C), bool))[:, :, None, None]

    def step(state, inp):
        xi, ai, bi, ci = inp
        cum = jnp.cumsum(ai, axis=1)
        diff = cum[:, :, None] - cum[:, None, :]
        L = jnp.where(causal, jnp.exp(jnp.where(causal, diff, 0.0)), 0.0)
        cb = jnp.einsum('btgn,bsgn->btsg', ci, bi)
        y = jnp.einsum('btsg,btsgj,bsgjp->btgjp', cb, L, xi)
        y = y + jnp.einsum('btgn,bgjpn,btgj->btgjp', ci, state, jnp.exp(cum))
        last = cum[:, -1]
        state = (state * jnp.exp(last)[..., None, None]
                 + jnp.einsum('bsgn,bsgj,bsgjp->bgjpn', bi, jnp.exp(last[:, None] - cum), xi))
        return state, y

    s0 = jnp.zeros((B, G, J, P, N), jnp.float32)
    _, y = lax.scan(step, s0, (xdt, dA, Bc, Cc))
    return y.transpose(1, 0, 2, 3, 4, 5).reshape(B, S, H, P).astype(dtype)


def _fwd_setup_inputs(seed: int = 0) -> dict:
    key = jax.random.key(seed)
    ks = jax.random.split(key, 24)
    f32 = jnp.float32

    def nrm(k, shape, scale):
        return jax.random.normal(k, shape, f32) * scale

    x = nrm(ks[0], (BATCH, SEQ, D_MODEL), 1.0)
    c = nrm(ks[1], (BATCH, D_MODEL), 1.0)
    norm_w = 1.0 + nrm(ks[2], (DEPTH, D_MODEL), 0.01)
    w_ada = nrm(ks[3], (DEPTH, D_MODEL, 3 * D_MODEL), 0.5 * D_MODEL ** -0.5)
    b_ada = nrm(ks[4], (DEPTH, 3 * D_MODEL), 0.01)
    w_in = nrm(ks[5], (DEPTH, D_MODEL, N_IN), D_MODEL ** -0.5)
    lru_conv_w = nrm(ks[6], (DEPTH, CONV_WIDTH, LRU_WIDTH), CONV_WIDTH ** -0.5)
    lru_conv_b = nrm(ks[7], (DEPTH, LRU_WIDTH), 0.01)
    lru_wa = nrm(ks[8], (DEPTH, LRU_HEADS, LRU_HEAD_DIM, LRU_HEAD_DIM), LRU_HEAD_DIM ** -0.5)
    lru_ba = nrm(ks[9], (DEPTH, LRU_HEADS, LRU_HEAD_DIM), 0.01)
    lru_wx = nrm(ks[10], (DEPTH, LRU_HEADS, LRU_HEAD_DIM, LRU_HEAD_DIM), LRU_HEAD_DIM ** -0.5)
    lru_bx = nrm(ks[11], (DEPTH, LRU_HEADS, LRU_HEAD_DIM), 0.01)
    u = jax.random.uniform(ks[12], (DEPTH, LRU_WIDTH), f32, 0.9, 0.999) ** (1.0 / LRU_C)
    lru_lambda = jnp.log(u) - jnp.log1p(-u)
    hg_lb_logits = nrm(ks[13], (DEPTH, HG_WIDTH), 0.5)
    hg_norm_w = 1.0 + nrm(ks[14], (DEPTH, HG_WIDTH), 0.01)
    ssd_conv_w = nrm(ks[15], (DEPTH, CONV_WIDTH, SSD_CONV_DIM), CONV_WIDTH ** -0.5)
    ssd_conv_b = nrm(ks[16], (DEPTH, SSD_CONV_DIM), 0.01)
    dt0 = jnp.exp(jax.random.uniform(ks[17], (DEPTH, SSD_HEADS), f32, math.log(1e-3), math.log(1e-1)))
    ssd_dt_bias = dt0 + jnp.log(-jnp.expm1(-dt0))
    ssd_a_log = jnp.log(jax.random.uniform(ks[18], (DEPTH, SSD_HEADS), f32, 1.0, 16.0))
    ssd_d = 1.0 + nrm(ks[19], (DEPTH, SSD_HEADS), 0.01)
    ssd_norm_w = 1.0 + nrm(ks[20], (DEPTH, SSD_WIDTH), 0.01)
    w_out = nrm(ks[21], (DEPTH, D_INNER, D_MODEL), D_INNER ** -0.5)
    final_norm_w = 1.0 + nrm(ks[22], (D_MODEL,), 0.01)
    return {"x": x, "c": c, "norm_w": norm_w, "w_ada": w_ada, "b_ada": b_ada,
            "w_in": w_in, "lru_conv_w": lru_conv_w, "lru_conv_b": lru_conv_b,
            "lru_wa": lru_wa, "lru_ba": lru_ba, "lru_wx": lru_wx, "lru_bx": lru_bx,
            "lru_lambda": lru_lambda, "hg_lb_logits": hg_lb_logits, "hg_norm_w": hg_norm_w,
            "ssd_conv_w": ssd_conv_w, "ssd_conv_b": ssd_conv_b, "ssd_dt_bias": ssd_dt_bias,
            "ssd_a_log": ssd_a_log, "ssd_d": ssd_d, "ssd_norm_w": ssd_norm_w,
            "w_out": w_out, "final_norm_w": final_norm_w}


def _fwd_reference(x, c, norm_w, w_ada, b_ada, w_in, lru_conv_w, lru_conv_b, lru_wa, lru_ba,
              lru_wx, lru_bx, lru_lambda, hg_lb_logits, hg_norm_w, ssd_conv_w, ssd_conv_b,
              ssd_dt_bias, ssd_a_log, ssd_d, ssd_norm_w, w_out, final_norm_w):
    B, S, _ = x.shape
    split_points = _split_points()
    cond = jax.nn.silu(c)
    p = jax.nn.softmax(hg_lb_logits.astype(jnp.float32), axis=0)
    lower_bounds = jnp.cumsum(p, axis=0) - p[0]

    for l in range(DEPTH):
        mod = cond @ w_ada[l] + b_ada[l]
        shift, scale, gate = jnp.split(mod, 3, axis=-1)
        h = rmsnorm(x, norm_w[l]) * (1.0 + scale[:, None]) + shift[:, None]
        u = h @ w_in[l]
        a_x, a_g, hg_q, hg_f, hg_i, hg_g, ssd_z, ssd_xbc, ssd_dt = jnp.split(u, split_points, axis=-1)

        xa = causal_conv(a_x, lru_conv_w[l], lru_conv_b[l])
        ya = rg_lru(xa, lru_wa[l], lru_ba[l], lru_wx[l], lru_bx[l], lru_lambda[l]) * jax.nn.silu(a_g)

        lb = lower_bounds[l]
        hf = hg_f.astype(jnp.float32)
        f = lb + (1.0 - lb) * jax.nn.sigmoid(hf)
        log_f = jnp.log(f)
        k_in = (1.0 - lb) * jax.nn.sigmoid(-hf)
        q_h = jax.nn.silu(hg_q).reshape(B, S, HG_HEADS, HG_HEAD_DIM)
        o_b = hgrn2_chunked(q_h, k_in.reshape(B, S, HG_HEADS, HG_HEAD_DIM),
                            hg_i.reshape(B, S, HG_HEADS, HG_HEAD_DIM),
                            log_f.reshape(B, S, HG_HEADS, HG_HEAD_DIM))
        yb = rmsnorm(o_b.reshape(B, S, HG_WIDTH), hg_norm_w[l]) * jax.nn.silu(hg_g)

        xbc = jax.nn.silu(causal_conv(ssd_xbc, ssd_conv_w[l], ssd_conv_b[l]))
        xs, Bm, Cm = jnp.split(xbc, [SSD_WIDTH, SSD_WIDTH + SSD_GROUPS * SSD_STATE], axis=-1)
        dt = jax.nn.softplus(ssd_dt + ssd_dt_bias[l])
        A = -jnp.exp(ssd_a_log[l])
        xs_h = xs.reshape(B, S, SSD_HEADS, SSD_HEAD_DIM)
        yc = ssd_chunked(xs_h, dt, A, Bm.reshape(B, S, SSD_GROUPS, SSD_STATE),
                         Cm.reshape(B, S, SSD_GROUPS, SSD_STATE))
        yc = (yc + ssd_d[l][:, None] * xs_h).reshape(B, S, SSD_WIDTH) * jax.nn.silu(ssd_z)
        yc = rmsnorm(yc.reshape(B, S, SSD_GROUPS, SSD_WIDTH // SSD_GROUPS),
                     ssd_norm_w[l].reshape(SSD_GROUPS, SSD_WIDTH // SSD_GROUPS)).reshape(B, S, SSD_WIDTH)

        y = jnp.concatenate([ya, yb, yc], axis=-1) @ w_out[l]
        x = x + gate[:, None] * y

    return rmsnorm(x, final_norm_w)


import jax as _jax
import jax.numpy as _jnp

TWIN_FORMAT = 'train_step'
FWD_PARAMS = ['x', 'c', 'norm_w', 'w_ada', 'b_ada', 'w_in', 'lru_conv_w', 'lru_conv_b', 'lru_wa', 'lru_ba', 'lru_wx', 'lru_bx', 'lru_lambda', 'hg_lb_logits', 'hg_norm_w', 'ssd_conv_w', 'ssd_conv_b', 'ssd_dt_bias', 'ssd_a_log', 'ssd_d', 'ssd_norm_w', 'w_out', 'final_norm_w']
TWIN_WEIGHTS = ['norm_w', 'w_ada', 'b_ada', 'w_in', 'lru_conv_w', 'lru_conv_b', 'lru_wa', 'lru_ba', 'lru_wx', 'lru_bx', 'lru_lambda', 'hg_lb_logits', 'hg_norm_w', 'ssd_conv_w', 'ssd_conv_b', 'ssd_dt_bias', 'ssd_a_log', 'ssd_d', 'ssd_norm_w', 'w_out', 'final_norm_w']
TWIN_DIFF_INPUT = 'x'
TWIN_INPUTS = ['x', 'c', 'norm_w', 'w_ada', 'b_ada', 'w_in', 'lru_conv_w', 'lru_conv_b', 'lru_wa', 'lru_ba', 'lru_wx', 'lru_bx', 'lru_lambda', 'hg_lb_logits', 'hg_norm_w', 'ssd_conv_w', 'ssd_conv_b', 'ssd_dt_bias', 'ssd_a_log', 'ssd_d', 'ssd_norm_w', 'w_out', 'final_norm_w', 'loss_target', 'm_norm_w', 'm_w_ada', 'm_b_ada', 'm_w_in', 'm_lru_conv_w', 'm_lru_conv_b', 'm_lru_wa', 'm_lru_ba', 'm_lru_wx', 'm_lru_bx', 'm_lru_lambda', 'm_hg_lb_logits', 'm_hg_norm_w', 'm_ssd_conv_w', 'm_ssd_conv_b', 'm_ssd_dt_bias', 'm_ssd_a_log', 'm_ssd_d', 'm_ssd_norm_w', 'm_w_out', 'm_final_norm_w', 'v_norm_w', 'v_w_ada', 'v_b_ada', 'v_w_in', 'v_lru_conv_w', 'v_lru_conv_b', 'v_lru_wa', 'v_lru_ba', 'v_lru_wx', 'v_lru_bx', 'v_lru_lambda', 'v_hg_lb_logits', 'v_hg_norm_w', 'v_ssd_conv_w', 'v_ssd_conv_b', 'v_ssd_dt_bias', 'v_ssd_a_log', 'v_ssd_d', 'v_ssd_norm_w', 'v_w_out', 'v_final_norm_w']
TWIN_OUTPUTS = ['loss', 'grad_x', 'grad_norm_w', 'grad_w_ada', 'grad_b_ada', 'grad_w_in', 'grad_lru_conv_w', 'grad_lru_conv_b', 'grad_lru_wa', 'grad_lru_ba', 'grad_lru_wx', 'grad_lru_bx', 'grad_lru_lambda', 'grad_hg_lb_logits', 'grad_hg_norm_w', 'grad_ssd_conv_w', 'grad_ssd_conv_b', 'grad_ssd_dt_bias', 'grad_ssd_a_log', 'grad_ssd_d', 'grad_ssd_norm_w', 'grad_w_out', 'grad_final_norm_w', 'delta_norm_w', 'delta_w_ada', 'delta_b_ada', 'delta_w_in', 'delta_lru_conv_w', 'delta_lru_conv_b', 'delta_lru_wa', 'delta_lru_ba', 'delta_lru_wx', 'delta_lru_bx', 'delta_lru_lambda', 'delta_hg_lb_logits', 'delta_hg_norm_w', 'delta_ssd_conv_w', 'delta_ssd_conv_b', 'delta_ssd_dt_bias', 'delta_ssd_a_log', 'delta_ssd_d', 'delta_ssd_norm_w', 'delta_w_out', 'delta_final_norm_w', 'new_m_norm_w', 'new_m_w_ada', 'new_m_b_ada', 'new_m_w_in', 'new_m_lru_conv_w', 'new_m_lru_conv_b', 'new_m_lru_wa', 'new_m_lru_ba', 'new_m_lru_wx', 'new_m_lru_bx', 'new_m_lru_lambda', 'new_m_hg_lb_logits', 'new_m_hg_norm_w', 'new_m_ssd_conv_w', 'new_m_ssd_conv_b', 'new_m_ssd_dt_bias', 'new_m_ssd_a_log', 'new_m_ssd_d', 'new_m_ssd_norm_w', 'new_m_w_out', 'new_m_final_norm_w', 'new_v_norm_w', 'new_v_w_ada', 'new_v_b_ada', 'new_v_w_in', 'new_v_lru_conv_w', 'new_v_lru_conv_b', 'new_v_lru_wa', 'new_v_lru_ba', 'new_v_lru_wx', 'new_v_lru_bx', 'new_v_lru_lambda', 'new_v_hg_lb_logits', 'new_v_hg_norm_w', 'new_v_ssd_conv_w', 'new_v_ssd_conv_b', 'new_v_ssd_dt_bias', 'new_v_ssd_a_log', 'new_v_ssd_d', 'new_v_ssd_norm_w', 'new_v_w_out', 'new_v_final_norm_w']
TWIN_LEAF_KINDS = {'loss': 'loss', 'grad_x': 'grad_x', 'grad_norm_w': 'grad_w', 'grad_w_ada': 'grad_w', 'grad_b_ada': 'grad_w', 'grad_w_in': 'grad_w', 'grad_lru_conv_w': 'grad_w', 'grad_lru_conv_b': 'grad_w', 'grad_lru_wa': 'grad_w', 'grad_lru_ba': 'grad_w', 'grad_lru_wx': 'grad_w', 'grad_lru_bx': 'grad_w', 'grad_lru_lambda': 'grad_w', 'grad_hg_lb_logits': 'grad_w', 'grad_hg_norm_w': 'grad_w', 'grad_ssd_conv_w': 'grad_w', 'grad_ssd_conv_b': 'grad_w', 'grad_ssd_dt_bias': 'grad_w', 'grad_ssd_a_log': 'grad_w', 'grad_ssd_d': 'grad_w', 'grad_ssd_norm_w': 'grad_w', 'grad_w_out': 'grad_w', 'grad_final_norm_w': 'grad_w', 'delta_norm_w': 'delta_w', 'delta_w_ada': 'delta_w', 'delta_b_ada': 'delta_w', 'delta_w_in': 'delta_w', 'delta_lru_conv_w': 'delta_w', 'delta_lru_conv_b': 'delta_w', 'delta_lru_wa': 'delta_w', 'delta_lru_ba': 'delta_w', 'delta_lru_wx': 'delta_w', 'delta_lru_bx': 'delta_w', 'delta_lru_lambda': 'delta_w', 'delta_hg_lb_logits': 'delta_w', 'delta_hg_norm_w': 'delta_w', 'delta_ssd_conv_w': 'delta_w', 'delta_ssd_conv_b': 'delta_w', 'delta_ssd_dt_bias': 'delta_w', 'delta_ssd_a_log': 'delta_w', 'delta_ssd_d': 'delta_w', 'delta_ssd_norm_w': 'delta_w', 'delta_w_out': 'delta_w', 'delta_final_norm_w': 'delta_w', 'new_m_norm_w': 'new_m', 'new_m_w_ada': 'new_m', 'new_m_b_ada': 'new_m', 'new_m_w_in': 'new_m', 'new_m_lru_conv_w': 'new_m', 'new_m_lru_conv_b': 'new_m', 'new_m_lru_wa': 'new_m', 'new_m_lru_ba': 'new_m', 'new_m_lru_wx': 'new_m', 'new_m_lru_bx': 'new_m', 'new_m_lru_lambda': 'new_m', 'new_m_hg_lb_logits': 'new_m', 'new_m_hg_norm_w': 'new_m', 'new_m_ssd_conv_w': 'new_m', 'new_m_ssd_conv_b': 'new_m', 'new_m_ssd_dt_bias': 'new_m', 'new_m_ssd_a_log': 'new_m', 'new_m_ssd_d': 'new_m', 'new_m_ssd_norm_w': 'new_m', 'new_m_w_out': 'new_m', 'new_m_final_norm_w': 'new_m', 'new_v_norm_w': 'new_v', 'new_v_w_ada': 'new_v', 'new_v_b_ada': 'new_v', 'new_v_w_in': 'new_v', 'new_v_lru_conv_w': 'new_v', 'new_v_lru_conv_b': 'new_v', 'new_v_lru_wa': 'new_v', 'new_v_lru_ba': 'new_v', 'new_v_lru_wx': 'new_v', 'new_v_lru_bx': 'new_v', 'new_v_lru_lambda': 'new_v', 'new_v_hg_lb_logits': 'new_v', 'new_v_hg_norm_w': 'new_v', 'new_v_ssd_conv_w': 'new_v', 'new_v_ssd_conv_b': 'new_v', 'new_v_ssd_dt_bias': 'new_v', 'new_v_ssd_a_log': 'new_v', 'new_v_ssd_d': 'new_v', 'new_v_ssd_norm_w': 'new_v', 'new_v_w_out': 'new_v', 'new_v_final_norm_w': 'new_v'}


def _forward(args):
    return _fwd_reference(*[args[k] for k in FWD_PARAMS])


def _output_shape():
    out = _jax.eval_shape(lambda: _forward(_fwd_setup_inputs(0)))
    return out.shape, out.dtype

N_MICROBATCH = 1
ADAM_LR = 0.001
ADAM_B1 = 0.9
ADAM_B2 = 0.999
ADAM_EPS = 1e-08
ADAM_WD = 0.01
ADAM_STEP = 10
PER_EXAMPLE_BATCH_AXIS = {'x': 0, 'c': 0, 'loss_target': 0}
SHARED_INPUTS = []
_WEIGHT_DTYPES = {'norm_w': _jnp.float32, 'w_ada': _jnp.float32, 'b_ada': _jnp.float32, 'w_in': _jnp.float32, 'lru_conv_w': _jnp.float32, 'lru_conv_b': _jnp.float32, 'lru_wa': _jnp.float32, 'lru_ba': _jnp.float32, 'lru_wx': _jnp.float32, 'lru_bx': _jnp.float32, 'lru_lambda': _jnp.float32, 'hg_lb_logits': _jnp.float32, 'hg_norm_w': _jnp.float32, 'ssd_conv_w': _jnp.float32, 'ssd_conv_b': _jnp.float32, 'ssd_dt_bias': _jnp.float32, 'ssd_a_log': _jnp.float32, 'ssd_d': _jnp.float32, 'ssd_norm_w': _jnp.float32, 'w_out': _jnp.float32, 'final_norm_w': _jnp.float32}
MOMENT_SCALE = {'norm_w': 4.915057e-02, 'w_ada': 6.360277e-02, 'b_ada': 1.030332e-01, 'w_in': 2.477635e-02, 'lru_conv_w': 4.159676e-02, 'lru_conv_b': 1.217021e-01, 'lru_wa': 5.615252e-03, 'lru_ba': 7.864774e-03, 'lru_wx': 1.112584e-02, 'lru_bx': 1.622064e-02, 'lru_lambda': 2.022701e-02, 'hg_lb_logits': 1.279012e-03, 'hg_norm_w': 1.717441e-02, 'ssd_conv_w': 2.255218e-02, 'ssd_conv_b': 2.740084e-02, 'ssd_dt_bias': 5.281003e-02, 'ssd_a_log': 8.274800e-02, 'ssd_d': 1.303251e-01, 'ssd_norm_w': 2.725797e-02, 'w_out': 3.985332e-02, 'final_norm_w': 1.609039e+01}


def _to_microbatches(a, axis):
    t = _jnp.moveaxis(a, axis, 0)
    t = t.reshape((N_MICROBATCH, t.shape[0] // N_MICROBATCH) + t.shape[1:])
    return _jnp.moveaxis(t, 1, axis + 1)


def setup_inputs(seed: int = 0) -> dict:
    inp = _fwd_setup_inputs(seed)
    key = _jax.random.fold_in(_jax.random.key(seed), 7919)
    shape, _ = _output_shape()
    out = dict(inp)
    out["loss_target"] = _jax.random.normal(_jax.random.fold_in(key, 0), shape, _jnp.float32)
    for i, name in enumerate(TWIN_WEIGHTS):
        w = inp[name].astype(_jnp.float32)
        if MOMENT_SCALE is None:
            s = _jnp.sqrt(_jnp.mean(_jnp.square(w)) + 1e-30)
        else:
            s = MOMENT_SCALE[name]
        km, kv = _jax.random.split(_jax.random.fold_in(key, i + 1))
        out[name] = w
        out["m_" + name] = s * _jax.random.normal(km, w.shape, _jnp.float32)
        out["v_" + name] = (s * s) * _jax.random.uniform(kv, w.shape, _jnp.float32, 0.5, 1.5)
    if N_MICROBATCH > 1:
        for name, axis in PER_EXAMPLE_BATCH_AXIS.items():
            out[name] = _to_microbatches(out[name], axis)
    return {'x': out['x'], 'c': out['c'], 'norm_w': out['norm_w'], 'w_ada': out['w_ada'], 'b_ada': out['b_ada'], 'w_in': out['w_in'], 'lru_conv_w': out['lru_conv_w'], 'lru_conv_b': out['lru_conv_b'], 'lru_wa': out['lru_wa'], 'lru_ba': out['lru_ba'], 'lru_wx': out['lru_wx'], 'lru_bx': out['lru_bx'], 'lru_lambda': out['lru_lambda'], 'hg_lb_logits': out['hg_lb_logits'], 'hg_norm_w': out['hg_norm_w'], 'ssd_conv_w': out['ssd_conv_w'], 'ssd_conv_b': out['ssd_conv_b'], 'ssd_dt_bias': out['ssd_dt_bias'], 'ssd_a_log': out['ssd_a_log'], 'ssd_d': out['ssd_d'], 'ssd_norm_w': out['ssd_norm_w'], 'w_out': out['w_out'], 'final_norm_w': out['final_norm_w'], 'loss_target': out['loss_target'], 'm_norm_w': out['m_norm_w'], 'm_w_ada': out['m_w_ada'], 'm_b_ada': out['m_b_ada'], 'm_w_in': out['m_w_in'], 'm_lru_conv_w': out['m_lru_conv_w'], 'm_lru_conv_b': out['m_lru_conv_b'], 'm_lru_wa': out['m_lru_wa'], 'm_lru_ba': out['m_lru_ba'], 'm_lru_wx': out['m_lru_wx'], 'm_lru_bx': out['m_lru_bx'], 'm_lru_lambda': out['m_lru_lambda'], 'm_hg_lb_logits': out['m_hg_lb_logits'], 'm_hg_norm_w': out['m_hg_norm_w'], 'm_ssd_conv_w': out['m_ssd_conv_w'], 'm_ssd_conv_b': out['m_ssd_conv_b'], 'm_ssd_dt_bias': out['m_ssd_dt_bias'], 'm_ssd_a_log': out['m_ssd_a_log'], 'm_ssd_d': out['m_ssd_d'], 'm_ssd_norm_w': out['m_ssd_norm_w'], 'm_w_out': out['m_w_out'], 'm_final_norm_w': out['m_final_norm_w'], 'v_norm_w': out['v_norm_w'], 'v_w_ada': out['v_w_ada'], 'v_b_ada': out['v_b_ada'], 'v_w_in': out['v_w_in'], 'v_lru_conv_w': out['v_lru_conv_w'], 'v_lru_conv_b': out['v_lru_conv_b'], 'v_lru_wa': out['v_lru_wa'], 'v_lru_ba': out['v_lru_ba'], 'v_lru_wx': out['v_lru_wx'], 'v_lru_bx': out['v_lru_bx'], 'v_lru_lambda': out['v_lru_lambda'], 'v_hg_lb_logits': out['v_hg_lb_logits'], 'v_hg_norm_w': out['v_hg_norm_w'], 'v_ssd_conv_w': out['v_ssd_conv_w'], 'v_ssd_conv_b': out['v_ssd_conv_b'], 'v_ssd_dt_bias': out['v_ssd_dt_bias'], 'v_ssd_a_log': out['v_ssd_a_log'], 'v_ssd_d': out['v_ssd_d'], 'v_ssd_norm_w': out['v_ssd_norm_w'], 'v_w_out': out['v_w_out'], 'v_final_norm_w': out['v_final_norm_w']}


def _loss(weights, diff, rest, loss_target):
    with _jax.named_scope("forward"):
        args = {**rest, TWIN_DIFF_INPUT: diff, **{k: w.astype(_WEIGHT_DTYPES[k]) for k, w in weights.items()}}
        y = _forward(args)
    with _jax.named_scope("loss_head"):
        err = _jnp.square(y.astype(_jnp.float32) - loss_target)
        return 0.5 * _jnp.sum(_jnp.mean(err, axis=-1)) if err.ndim else 0.5 * err


def _adamw(w, g, m, v):
    m = ADAM_B1 * m + (1.0 - ADAM_B1) * g
    v = ADAM_B2 * v + (1.0 - ADAM_B2) * _jnp.square(g)
    m_hat = m / (1.0 - ADAM_B1 ** ADAM_STEP)
    v_hat = v / (1.0 - ADAM_B2 ** ADAM_STEP)
    delta = -ADAM_LR * (m_hat / (_jnp.sqrt(v_hat) + ADAM_EPS) + ADAM_WD * w)
    return delta, m, v


def reference(x, c, norm_w, w_ada, b_ada, w_in, lru_conv_w, lru_conv_b, lru_wa, lru_ba, lru_wx, lru_bx, lru_lambda, hg_lb_logits, hg_norm_w, ssd_conv_w, ssd_conv_b, ssd_dt_bias, ssd_a_log, ssd_d, ssd_norm_w, w_out, final_norm_w, loss_target, m_norm_w, m_w_ada, m_b_ada, m_w_in, m_lru_conv_w, m_lru_conv_b, m_lru_wa, m_lru_ba, m_lru_wx, m_lru_bx, m_lru_lambda, m_hg_lb_logits, m_hg_norm_w, m_ssd_conv_w, m_ssd_conv_b, m_ssd_dt_bias, m_ssd_a_log, m_ssd_d, m_ssd_norm_w, m_w_out, m_final_norm_w, v_norm_w, v_w_ada, v_b_ada, v_w_in, v_lru_conv_w, v_lru_conv_b, v_lru_wa, v_lru_ba, v_lru_wx, v_lru_bx, v_lru_lambda, v_hg_lb_logits, v_hg_norm_w, v_ssd_conv_w, v_ssd_conv_b, v_ssd_dt_bias, v_ssd_a_log, v_ssd_d, v_ssd_norm_w, v_w_out, v_final_norm_w):
    given = dict(x=x, c=c, norm_w=norm_w, w_ada=w_ada, b_ada=b_ada, w_in=w_in, lru_conv_w=lru_conv_w, lru_conv_b=lru_conv_b, lru_wa=lru_wa, lru_ba=lru_ba, lru_wx=lru_wx, lru_bx=lru_bx, lru_lambda=lru_lambda, hg_lb_logits=hg_lb_logits, hg_norm_w=hg_norm_w, ssd_conv_w=ssd_conv_w, ssd_conv_b=ssd_conv_b, ssd_dt_bias=ssd_dt_bias, ssd_a_log=ssd_a_log, ssd_d=ssd_d, ssd_norm_w=ssd_norm_w, w_out=w_out, final_norm_w=final_norm_w, loss_target=loss_target, m_norm_w=m_norm_w, m_w_ada=m_w_ada, m_b_ada=m_b_ada, m_w_in=m_w_in, m_lru_conv_w=m_lru_conv_w, m_lru_conv_b=m_lru_conv_b, m_lru_wa=m_lru_wa, m_lru_ba=m_lru_ba, m_lru_wx=m_lru_wx, m_lru_bx=m_lru_bx, m_lru_lambda=m_lru_lambda, m_hg_lb_logits=m_hg_lb_logits, m_hg_norm_w=m_hg_norm_w, m_ssd_conv_w=m_ssd_conv_w, m_ssd_conv_b=m_ssd_conv_b, m_ssd_dt_bias=m_ssd_dt_bias, m_ssd_a_log=m_ssd_a_log, m_ssd_d=m_ssd_d, m_ssd_norm_w=m_ssd_norm_w, m_w_out=m_w_out, m_final_norm_w=m_final_norm_w, v_norm_w=v_norm_w, v_w_ada=v_w_ada, v_b_ada=v_b_ada, v_w_in=v_w_in, v_lru_conv_w=v_lru_conv_w, v_lru_conv_b=v_lru_conv_b, v_lru_wa=v_lru_wa, v_lru_ba=v_lru_ba, v_lru_wx=v_lru_wx, v_lru_bx=v_lru_bx, v_lru_lambda=v_lru_lambda, v_hg_lb_logits=v_hg_lb_logits, v_hg_norm_w=v_hg_norm_w, v_ssd_conv_w=v_ssd_conv_w, v_ssd_conv_b=v_ssd_conv_b, v_ssd_dt_bias=v_ssd_dt_bias, v_ssd_a_log=v_ssd_a_log, v_ssd_d=v_ssd_d, v_ssd_norm_w=v_ssd_norm_w, v_w_out=v_w_out, v_final_norm_w=v_final_norm_w)
    weights = {n: given[n] for n in TWIN_WEIGHTS}
    shared = {n: given[n] for n in SHARED_INPUTS}
    per_example = {n: given[n] for n in ['x', 'c']}
    grad_fn = _jax.value_and_grad(_loss, argnums=(0, 1))

    def one_microbatch(ex, loss_target):
        ex = dict(ex)
        diff = ex.pop(TWIN_DIFF_INPUT)
        return grad_fn(weights, diff, {**shared, **ex}, loss_target)

    if N_MICROBATCH == 1:
        loss, (grad_w, grad_x) = one_microbatch(per_example, given["loss_target"])
    else:
        def body(carry, xs):
            loss_sum, grad_sum = carry
            l_k, (gw_k, gx_k) = one_microbatch(xs[0], xs[1])
            with _jax.named_scope("update"):
                return (loss_sum + l_k, _jax.tree.map(_jnp.add, grad_sum, gw_k)), gx_k

        init = (_jnp.zeros((), _jnp.float32), _jax.tree.map(_jnp.zeros_like, weights))
        (loss, grad_w), grad_x = _jax.lax.scan(body, init, (per_example, given["loss_target"]))
    with _jax.named_scope("update"):
        delta_w, new_m, new_v = {}, {}, {}
        for n in TWIN_WEIGHTS:
            delta_w[n], new_m[n], new_v[n] = _adamw(weights[n], grad_w[n], given["m_" + n], given["v_" + n])
    return (loss, grad_x, *[grad_w[n] for n in TWIN_WEIGHTS], *[delta_w[n] for n in TWIN_WEIGHTS],
            *[new_m[n] for n in TWIN_WEIGHTS], *[new_v[n] for n in TWIN_WEIGHTS])
```

```python
import functools

import numpy as np
import jax
import jax.numpy as jnp
from jax import lax
from jax.experimental import pallas as pl
from jax.experimental.pallas import tpu as pltpu

F32 = jnp.float32
BF16 = jnp.bfloat16
SDS = jax.ShapeDtypeStruct

N_DEV = 8
DEPTH = 4
D_MODEL = 1024
D_INNER = 2048
EPS = 1e-6
LRU_W = 512
LRU_C = 8.0
HG_W = 512
HG_CHUNK = 64
HG_HEADS = 4
SSD_W = 1024
SSD_HEADS = 16
SSD_P = 64
SSD_N = 128
SSD_CHUNK = 128
SSD_CONV = 1536
N_IN = 5648
N_PAD = 5760
OFF_HG = 0
OFF_LRU = 2048
OFF_XBC = 3072
OFF_Z = 4608
LANE = 128
VMEM_LIMIT = 56 * 1024 * 1024
NEG = -1e30

ADAM_LR = 0.001
ADAM_B1 = 0.9
ADAM_B2 = 0.999
ADAM_EPS = 1e-08
ADAM_WD = 0.01
ADAM_STEP = 10


def _cp(sem=None):
    return pltpu.CompilerParams(dimension_semantics=sem, vmem_limit_bytes=VMEM_LIMIT)


def _dg(a, b, ca, cb):
    return lax.dot_general(a, b, (((ca,), (cb,)), ((), ())), preferred_element_type=F32)


def _mm(a, b):
    return _dg(a, b, 1, 0)


def _mm_nt(a, b):
    return _dg(a, b, 1, 1)


def _mm_tn(a, b):
    return _dg(a, b, 0, 0)


def _bf(x):
    return x.astype(BF16)


def _split3(x):
    hi = x.astype(BF16)
    r = x - hi.astype(F32)
    mid = r.astype(BF16)
    lo = (r - mid.astype(F32)).astype(BF16)
    return hi, mid, lo


def _sel_r(x, m):
    hi, mid, lo = _split3(x)
    return _mm(hi, m) + _mm(mid, m) + _mm(lo, m)


def _sel_l(m, x):
    hi, mid, lo = _split3(x)
    return _mm(m, hi) + _mm(m, mid) + _mm(m, lo)


def _sel_tn(x, m):
    hi, mid, lo = _split3(x)
    return _mm_tn(hi, m) + _mm_tn(mid, m) + _mm_tn(lo, m)


def _sigmoid(x):
    return 1.0 / (1.0 + jnp.exp(-x))


def _silu(x):
    return x * _sigmoid(x)


def _dsilu(x):
    s = _sigmoid(x)
    return s * (1.0 + x * (1.0 - s))


def _softplus(x):
    return jnp.maximum(x, 0.0) + jnp.log(1.0 + jnp.exp(-jnp.abs(x)))


def _expm1(z):
    series = z * (1.0 + z * (1.0 / 2) * (1.0 + z * (1.0 / 3) * (1.0 + z * (1.0 / 4) * (
        1.0 + z * (1.0 / 5) * (1.0 + z * (1.0 / 6) * (1.0 + z * (1.0 / 7)))))))
    return jnp.where(jnp.abs(z) < 0.3, series, jnp.exp(z) - 1.0)


def _iota(shape, dim):
    return lax.broadcasted_iota(jnp.int32, shape, dim)


def _last_row(x, rows):
    return jnp.sum(jnp.where(rows == x.shape[0] - 1, x, 0.0), axis=0, keepdims=True)


def _shift_down(x, d, rows, fill=0.0):
    return jnp.where(rows >= d, pltpu.roll(x, d, 0), fill)


def _shift_up(x, d, rows, fill=0.0):
    n = x.shape[0]
    return jnp.where(rows < n - d, pltpu.roll(x, n - d, 0), fill)


def _conv_fwd(x, cw_ref, cb_ref, rows):
    out = cb_ref[...] + cw_ref[pl.ds(3, 1), :] * x
    for k in range(3):
        out = out + cw_ref[pl.ds(k, 1), :] * _shift_down(x, 3 - k, rows)
    return out


def _conv_bwd(x, dco, cw_ref, rows):
    dx = cw_ref[pl.ds(3, 1), :] * dco
    dws = []
    for k in range(3):
        dx = dx + cw_ref[pl.ds(k, 1), :] * _shift_up(dco, 3 - k, rows)
        dws.append(jnp.sum(dco * _shift_down(x, 3 - k, rows), axis=0, keepdims=True))
    dws.append(jnp.sum(dco * x, axis=0, keepdims=True))
    return dx, dws, jnp.sum(dco, axis=0, keepdims=True)


def _vec(n):
    return pl.BlockSpec((1, n), lambda *_: (0, 0))


def _full(shape):
    nd = len(shape)
    return pl.BlockSpec(shape, lambda *_: (0,) * nd)


def _inproj_fwd(x, nw, scale, shift, w):
    S = x.shape[0]
    tm, tn = min(512, S), 640

    def body(x_ref, nw_ref, sc_ref, sh_ref, w_ref, u_ref, h_ref):
        @pl.when(pl.program_id(1) == 0)
        def _():
            xv = x_ref[...]
            inv = lax.rsqrt(jnp.mean(xv * xv, axis=-1, keepdims=True) + EPS)
            h = (xv * inv) * nw_ref[...] * (1.0 + sc_ref[...]) + sh_ref[...]
            h_ref[...] = h.astype(BF16)

        u_ref[...] = _mm(h_ref[...], w_ref[...])

    return pl.pallas_call(
        body, name="inproj_fwd", grid=(S // tm, N_PAD // tn),
        in_specs=[pl.BlockSpec((tm, D_MODEL), lambda i, j: (i, 0)), _vec(D_MODEL), _vec(D_MODEL), _vec(D_MODEL),
                  pl.BlockSpec((D_MODEL, tn), lambda i, j: (0, j))],
        out_specs=[pl.BlockSpec((tm, tn), lambda i, j: (i, j)), pl.BlockSpec((tm, D_MODEL), lambda i, j: (i, 0))],
        out_shape=[SDS((S, N_PAD), F32), SDS((S, D_MODEL), BF16)],
        compiler_params=_cp(("parallel", "arbitrary")),
    )(x, nw, scale, shift, w)


def _inproj_bwd_x(du, w, x, nw, scale, dxn):
    S = x.shape[0]
    tm, tk = min(512, S), 640
    nk = N_PAD // tk

    def body(du_ref, w_ref, x_ref, nw_ref, sc_ref, dxn_ref, dx_ref, red_ref, acc):
        i, k = pl.program_id(0), pl.program_id(1)

        @pl.when(k == 0)
        def _():
            acc[...] = jnp.zeros_like(acc)

        @pl.when((i == 0) & (k == 0))
        def _():
            red_ref[...] = jnp.zeros_like(red_ref)

        acc[...] += _mm_nt(_bf(du_ref[...]), w_ref[...])

        @pl.when(k == nk - 1)
        def _():
            dh = acc[...]
            xv = x_ref[...]
            inv = lax.rsqrt(jnp.mean(xv * xv, axis=-1, keepdims=True) + EPS)
            xhat = xv * inv
            nwv = nw_ref[...]
            g1 = 1.0 + sc_ref[...]
            dxhat = dh * nwv * g1
            dx = inv * (dxhat - xhat * jnp.mean(dxhat * xhat, axis=-1, keepdims=True))
            dx_ref[...] = dxn_ref[...] + dx
            red_ref[0:1, :] += jnp.sum(dh, axis=0, keepdims=True)
            red_ref[1:2, :] += jnp.sum(dh * xhat * nwv, axis=0, keepdims=True)
            red_ref[2:3, :] += jnp.sum(dh * xhat * g1, axis=0, keepdims=True)

    row = pl.BlockSpec((tm, D_MODEL), lambda i, k: (i, 0))
    return pl.pallas_call(
        body, name="inproj_bwd_x", grid=(S // tm, nk),
        in_specs=[pl.BlockSpec((tm, tk), lambda i, k: (i, k)), pl.BlockSpec((D_MODEL, tk), lambda i, k: (0, k)),
                  row, _vec(D_MODEL), _vec(D_MODEL), row],
        out_specs=[row, pl.BlockSpec((8, D_MODEL), lambda i, k: (0, 0))],
        out_shape=[SDS((S, D_MODEL), F32), SDS((8, D_MODEL), F32)],
        scratch_shapes=[pltpu.VMEM((tm, D_MODEL), F32)],
        compiler_params=_cp(("arbitrary", "arbitrary")),
    )(du, w, x, nw, scale, dxn)


def _inproj_bwd_w(h, du):
    S = h.shape[0]
    tn = 640

    def body(h_ref, du_ref, gw_ref):
        gw_ref[...] = _mm_tn(h_ref[...], _bf(du_ref[...]))

    return pl.pallas_call(
        body, name="inproj_bwd_w", grid=(N_PAD // tn,),
        in_specs=[_full((S, D_MODEL)), pl.BlockSpec((S, tn), lambda j: (0, j))],
        out_specs=pl.BlockSpec((D_MODEL, tn), lambda j: (0, j)),
        out_shape=SDS((D_MODEL, N_PAD), F32),
        compiler_params=_cp(("parallel",)),
    )(h, du)


def _scan_block(a, b, rows):
    d = 1
    while d < a.shape[0]:
        a_s = _shift_down(a, d, rows, 1.0)
        b_s = _shift_down(b, d, rows, 0.0)
        b = a * b_s + b
        a = a * a_s
        d *= 2
    return a, b


def _rscan_block(c, g, rows):
    d = 1
    while d < c.shape[0]:
        c_s = _shift_up(c, d, rows, 1.0)
        g_s = _shift_up(g, d, rows, 0.0)
        g = g + c * g_s
        c = c * c_s
        d *= 2
    return c, g


def _lru_gates(xa, wa_ref, ba_ref, wx_ref, bx_ref, lam_ref):
    sp = _softplus(-lam_ref[...])
    xb = _bf(xa)
    r = _sigmoid(_mm(xb, wa_ref[...]) + ba_ref[...])
    ig = _sigmoid(_mm(xb, wx_ref[...]) + bx_ref[...])
    la = -LRU_C * r * sp
    a = jnp.exp(la)
    mult = jnp.sqrt(-_expm1(2.0 * la))
    return sp, r, ig, la, a, mult


def _lru_specs(S):
    t128 = pl.BlockSpec((1, LANE), lambda t: (0, t))
    return [pl.BlockSpec((S, 2 * LANE), lambda t: (0, OFF_LRU // (2 * LANE) + t)),
            pl.BlockSpec((4, LANE), lambda t: (0, t)), t128,
            pl.BlockSpec((None, LANE, LANE), lambda t: (t, 0, 0)), t128,
            pl.BlockSpec((None, LANE, LANE), lambda t: (t, 0, 0)), t128, t128]


def _lru_fwd(u, cw, cb, wa, ba, wx, bx, lam, ycat):
    S = u.shape[0]
    tb = min(256, S)

    def body(u_ref, cw_ref, cb_ref, wa_ref, ba_ref, wx_ref, bx_ref, lam_ref, ycat_in, ycat_ref, h_ref, a_scr, b_scr):
        del ycat_in
        rows = _iota((S, LANE), 0)
        xa = _conv_fwd(u_ref[:, 0:LANE], cw_ref, cb_ref, rows)
        _, _, ig, _, a, mult = _lru_gates(xa, wa_ref, ba_ref, wx_ref, bx_ref, lam_ref)
        a_scr[...] = a
        b_scr[...] = mult * (ig * xa)
        rows_b = _iota((tb, LANE), 0)

        def blk(j, hprev):
            sl = pl.ds(pl.multiple_of(j * tb, tb), tb)
            acum, hloc = _scan_block(a_scr[sl, :], b_scr[sl, :], rows_b)
            hf = hloc + acum * hprev
            h_ref[sl, :] = hf
            return _last_row(hf, rows_b)

        lax.fori_loop(0, S // tb, blk, jnp.zeros((1, LANE), F32))
        ycat_ref[...] = h_ref[...] * _silu(u_ref[:, LANE:2 * LANE])

    col = pl.BlockSpec((S, LANE), lambda t: (0, t))
    return pl.pallas_call(
        body, name="lru_fwd", grid=(LRU_W // LANE,),
        in_specs=_lru_specs(S) + [pl.BlockSpec(memory_space=pl.ANY)],
        out_specs=[col, col],
        out_shape=[SDS((S, D_INNER), F32), SDS((S, LRU_W), F32)],
        scratch_shapes=[pltpu.VMEM((S, LANE), F32), pltpu.VMEM((S, LANE), F32)],
        input_output_aliases={8: 0},
        compiler_params=_cp(("parallel",)),
    )(u, cw, cb, wa, ba, wx, bx, lam, ycat)


def _lru_bwd(u, cw, cb, wa, ba, wx, bx, lam, h_lru, dycat, du):
    S = u.shape[0]
    tb = min(256, S)

    def body(u_ref, cw_ref, cb_ref, wa_ref, ba_ref, wx_ref, bx_ref, lam_ref, h_ref, dy_ref, du_in,
             du_ref, red_ref, gwa_ref, gwx_ref, c_scr, g_scr, l_scr):
        del du_in
        rows = _iota((S, LANE), 0)
        ax = u_ref[:, 0:LANE]
        ag = u_ref[:, LANE:2 * LANE]
        xa = _conv_fwd(ax, cw_ref, cb_ref, rows)
        sp, r, ig, la, a, mult = _lru_gates(xa, wa_ref, ba_ref, wx_ref, bx_ref, lam_ref)
        h = h_ref[...]
        dy = dy_ref[...]
        du_ref[:, LANE:2 * LANE] = dy * h * _dsilu(ag)
        c_scr[...] = _shift_up(a, 1, rows, 0.0)
        g_scr[...] = dy * _silu(ag)
        rows_b = _iota((tb, LANE), 0)
        nb = S // tb

        def blk(jj, lnext):
            j = nb - 1 - jj
            sl = pl.ds(pl.multiple_of(j * tb, tb), tb)
            ccum, lloc = _rscan_block(c_scr[sl, :], g_scr[sl, :], rows_b)
            lam_t = lloc + ccum * lnext
            l_scr[sl, :] = lam_t
            return jnp.sum(jnp.where(rows_b == 0, lam_t, 0.0), axis=0, keepdims=True)

        lax.fori_loop(0, nb, blk, jnp.zeros((1, LANE), F32))
        db = l_scr[...]
        da = db * _shift_down(h, 1, rows)
        dmult = db * ig * xa
        dig = db * mult * xa
        dxa = db * mult * ig
        dla = da * a - dmult * (a * a) / mult
        dr = -LRU_C * sp * dla
        dsp = jnp.sum(-LRU_C * r * dla, axis=0, keepdims=True)
        dlam = -dsp * _sigmoid(-lam_ref[...])
        dzr = dr * r * (1.0 - r)
        dzi = dig * ig * (1.0 - ig)
        dzr_b, dzi_b, xa_b = _bf(dzr), _bf(dzi), _bf(xa)
        dxa = dxa + _mm_nt(dzr_b, wa_ref[...]) + _mm_nt(dzi_b, wx_ref[...])
        gwa_ref[...] = _mm_tn(xa_b, dzr_b)
        gwx_ref[...] = _mm_tn(xa_b, dzi_b)
        dax, dws, dcb = _conv_bwd(ax, dxa, cw_ref, rows)
        du_ref[:, 0:LANE] = dax
        parts = dws + [dcb, jnp.sum(dzr, axis=0, keepdims=True), jnp.sum(dzi, axis=0, keepdims=True), dlam]
        for n, p in enumerate(parts):
            red_ref[pl.ds(n, 1), :] = p

    col = pl.BlockSpec((S, LANE), lambda t: (0, t))
    gw = pl.BlockSpec((None, LANE, LANE), lambda t: (t, 0, 0))
    return pl.pallas_call(
        body, name="lru_bwd", grid=(LRU_W // LANE,),
        in_specs=_lru_specs(S) + [col, col, pl.BlockSpec(memory_space=pl.ANY)],
        out_specs=[pl.BlockSpec((S, 2 * LANE), lambda t: (0, OFF_LRU // (2 * LANE) + t)),
                   pl.BlockSpec((8, LANE), lambda t: (0, t)), gw, gw],
        out_shape=[SDS((S, N_PAD), F32), SDS((8, LRU_W), F32), SDS((4, LANE, LANE), F32), SDS((4, LANE, LANE), F32)],
        scratch_shapes=[pltpu.VMEM((S, LANE), F32)] * 3,
        input_output_aliases={10: 0},
        compiler_params=_cp(("parallel",)),
    )(u, cw, cb, wa, ba, wx, bx, lam, h_lru, dycat, du)


HG_LEVELS = 6


def _hg_consts():
    C = HG_CHUNK
    t = np.arange(C)[:, None]
    r = np.arange(C)[None, :]
    mats = []
    for side in ("q", "k"):
        for l in range(HG_LEVELS):
            b = 1 << l
            upper = (t % (2 * b)) >= b
            anchor = (t // (2 * b)) * 2 * b + b - 1
            if side == "q":
                mats.append(upper & (r > anchor) & (r <= t))
            else:
                mats.append((~upper) & (r > t) & (r <= anchor))
    mats.append(r <= t)
    mats.append(r > t)
    return np.concatenate(mats, 0).astype(np.float32)


def _hg_factors(hf, lb, mall):
    s = _sigmoid(hf)
    f = lb + (1.0 - lb) * s
    lf = jnp.log(f)
    k = (1.0 - lb) * _sigmoid(-hf)
    e = jnp.exp(_sel_l(mall, lf))
    C = HG_CHUNK
    eq = [e[l * C:(l + 1) * C] for l in range(HG_LEVELS)]
    ek = [e[(HG_LEVELS + l) * C:(HG_LEVELS + l + 1) * C] for l in range(HG_LEVELS)]
    ecum = e[2 * HG_LEVELS * C:(2 * HG_LEVELS + 1) * C]
    erem = e[(2 * HG_LEVELS + 1) * C:(2 * HG_LEVELS + 2) * C]
    return s, f, k, eq, ek, ecum, erem


def _hg_masks():
    C = HG_CHUNK
    ri, ci = _iota((C, C), 0), _iota((C, C), 1)
    rr = _iota((C, LANE), 0)
    gm = [(lax.shift_right_logical(ri, l + 1) == lax.shift_right_logical(ci, l + 1)).astype(F32)
          for l in range(HG_LEVELS)]
    up = [(lax.shift_right_logical(rr, l) & 1) == 1 for l in range(HG_LEVELS)]
    eye = (ri == ci).astype(F32)
    return gm, up, eye, rr


def _hg_scores(qh, kh, eq, ek, sl, gm, up, eye):
    qs, ks = [], []
    p = _mm_nt(_bf(qh), _bf(kh)) * eye
    for l in range(HG_LEVELS):
        ql = jnp.where(up[l], qh * eq[l][:, sl], 0.0)
        kl = jnp.where(up[l], 0.0, kh * ek[l][:, sl])
        p = p + _mm_nt(_bf(ql), _bf(kl)) * gm[l]
        qs.append(ql)
        ks.append(kl)
    return p, qs, ks


def _hg_fwd(u, lb, nw, mall, ycat):
    S = u.shape[0]
    C = HG_CHUNK
    n = S // C

    def body(u_ref, lb_ref, nw_ref, mall_ref, ycat_in, ycat_ref, o_ref, st_ref, st):
        del ycat_in

        @pl.when(pl.program_id(0) == 0)
        def _():
            st[...] = jnp.zeros_like(st)

        q = _silu(u_ref[:, 0:512])
        v = u_ref[:, 1024:1536]
        _, _, k, eq, ek, ecum, erem = _hg_factors(u_ref[:, 512:1024], lb_ref[...], mall_ref[...])
        gm, up, eye, rr = _hg_masks()
        for h in range(HG_HEADS):
            sl = slice(h * LANE, (h + 1) * LANE)
            qh, kh, vh = q[:, sl], k[:, sl], _bf(v[:, sl])
            p, _, _ = _hg_scores(qh, kh, eq, ek, sl, gm, up, eye)
            sth = st[h]
            st_ref[h] = sth
            o_ref[:, sl] = _mm(_bf(p), vh) + _mm_nt(_bf(qh * ecum[:, sl]), _bf(sth))
            st[h] = sth * _last_row(ecum[:, sl], rr) + _mm_tn(vh, _bf(kh * erem[:, sl]))
        o = o_ref[...]
        inv = lax.rsqrt(jnp.mean(o * o, axis=-1, keepdims=True) + EPS)
        ycat_ref[...] = (o * inv) * nw_ref[...] * _silu(u_ref[:, 1536:2048])

    return pl.pallas_call(
        body, name="hg_fwd", grid=(n,),
        in_specs=[pl.BlockSpec((C, 2048), lambda i: (i, 0)), _vec(HG_W), _vec(HG_W), _full(mall.shape),
                  pl.BlockSpec(memory_space=pl.ANY)],
        out_specs=[pl.BlockSpec((C, HG_W), lambda i: (i, 1)), pl.BlockSpec((C, HG_W), lambda i: (i, 0)),
                   pl.BlockSpec((None, HG_HEADS, LANE, LANE), lambda i: (i, 0, 0, 0))],
        out_shape=[SDS((S, D_INNER), F32), SDS((S, HG_W), F32), SDS((n, HG_HEADS, LANE, LANE), F32)],
        scratch_shapes=[pltpu.VMEM((HG_HEADS, LANE, LANE), F32)],
        input_output_aliases={4: 0},
        compiler_params=_cp(("arbitrary",)),
    )(u, lb, nw, mall, ycat)


def _hg_bwd(u, lb, nw, mall, mall_t, o_b, states, dycat, du):
    S = u.shape[0]
    C = HG_CHUNK
    n = S // C
    L2 = 2 * HG_LEVELS

    def body(u_ref, lb_ref, nw_ref, mall_ref, mallt_ref, o_ref, st_ref, dy_ref, du_in, du_ref, red_ref,
             dst, dlast_s, dq_s, dk_s, dex):
        del du_in

        @pl.when(pl.program_id(0) == 0)
        def _():
            dst[...] = jnp.zeros_like(dst)
            red_ref[...] = jnp.zeros_like(red_ref)

        lb = lb_ref[...]
        hq, hf, hg = u_ref[:, 0:512], u_ref[:, 512:1024], u_ref[:, 1536:2048]
        q = _silu(hq)
        v = u_ref[:, 1024:1536]
        s, f, k, eq, ek, ecum, erem = _hg_factors(hf, lb, mall_ref[...])
        gm, up, eye, rr = _hg_masks()
        o = o_ref[...]
        dy = dy_ref[...]
        inv = lax.rsqrt(jnp.mean(o * o, axis=-1, keepdims=True) + EPS)
        ohat = o * inv
        nwv = nw_ref[...]
        du_ref[:, 1536:2048] = dy * ohat * nwv * _dsilu(hg)
        dn = dy * _silu(hg)
        red_ref[0:1, :] += jnp.sum(dn * ohat, axis=0, keepdims=True)
        dohat = dn * nwv
        do = inv * (dohat - ohat * jnp.mean(dohat * ohat, axis=-1, keepdims=True))
        for h in range(HG_HEADS):
            sl = slice(h * LANE, (h + 1) * LANE)
            qh, kh, vh, doh = q[:, sl], k[:, sl], _bf(v[:, sl]), _bf(do[:, sl])
            p, qs, ks = _hg_scores(qh, kh, eq, ek, sl, gm, up, eye)
            st_f = st_ref[h]
            sth = _bf(st_f)
            dsth = dst[h]
            dsth_b = _bf(dsth)
            qt = qh * ecum[:, sl]
            kt = kh * erem[:, sl]
            elast = _last_row(ecum[:, sl], rr)
            dp = _mm_nt(doh, vh)
            du_ref[:, 1024 + h * LANE:1024 + (h + 1) * LANE] = _mm_tn(_bf(p), doh) + _mm_nt(_bf(kt), dsth_b)
            dpe = _bf(dp * eye)
            dqt = _mm(doh, sth)
            dkt = _mm(vh, dsth_b)
            dq = dqt * ecum[:, sl] + _mm(dpe, _bf(kh))
            dk = dkt * erem[:, sl] + _mm_tn(dpe, _bf(qh))
            dex[L2 * C:(L2 + 1) * C, sl] = dqt * qt
            dex[(L2 + 1) * C:(L2 + 2) * C, sl] = dkt * kt
            for l in range(HG_LEVELS):
                dpl = _bf(dp * gm[l])
                dql = _mm(dpl, _bf(ks[l]))
                dkl = _mm_tn(dpl, _bf(qs[l]))
                dq = dq + jnp.where(up[l], dql * eq[l][:, sl], 0.0)
                dk = dk + jnp.where(up[l], 0.0, dkl * ek[l][:, sl])
                dex[l * C:(l + 1) * C, sl] = dql * qs[l]
                dex[(HG_LEVELS + l) * C:(HG_LEVELS + l + 1) * C, sl] = dkl * ks[l]
            dlast_s[:, sl] = jnp.sum(dsth * st_f, axis=0, keepdims=True) * elast
            dst[h] = dsth * elast + _mm_tn(doh, _bf(qt))
            dq_s[:, sl] = dq
            dk_s[:, sl] = dk
        dq = dq_s[...]
        dk = dk_s[...]
        dlf = _sel_l(mallt_ref[...], dex[...]) + dlast_s[...]
        du_ref[:, 0:512] = dq * _dsilu(hq)
        t = (1.0 - s) * (dlf / f - dk)
        du_ref[:, 512:1024] = (1.0 - lb) * s * t
        red_ref[1:2, :] += jnp.sum(t, axis=0, keepdims=True)

    rev = lambda i: (n - 1 - i, 0)
    return pl.pallas_call(
        body, name="hg_bwd", grid=(n,),
        in_specs=[pl.BlockSpec((C, 2048), rev), _vec(HG_W), _vec(HG_W), _full(mall.shape), _full(mall_t.shape),
                  pl.BlockSpec((C, HG_W), rev),
                  pl.BlockSpec((None, HG_HEADS, LANE, LANE), lambda i: (n - 1 - i, 0, 0, 0)),
                  pl.BlockSpec((C, HG_W), lambda i: (n - 1 - i, 1)), pl.BlockSpec(memory_space=pl.ANY)],
        out_specs=[pl.BlockSpec((C, 2048), rev), pl.BlockSpec((8, HG_W), lambda i: (0, 0))],
        out_shape=[SDS((S, N_PAD), F32), SDS((8, HG_W), F32)],
        scratch_shapes=[pltpu.VMEM((HG_HEADS, LANE, LANE), F32), pltpu.VMEM((1, HG_W), F32),
                        pltpu.VMEM((C, HG_W), F32), pltpu.VMEM((C, HG_W), F32), pltpu.VMEM(((L2 + 2) * C, HG_W), F32)],
        input_output_aliases={8: 0},
        compiler_params=_cp(("arbitrary",)),
    )(u, lb, nw, mall, mall_t, o_b, states, dycat, du)


def _ssdconv_fwd(u, cw, cb):
    S = u.shape[0]

    def body(u_ref, cw_ref, cb_ref, out_ref):
        rows = _iota((S, LANE), 0)
        out_ref[...] = _silu(_conv_fwd(u_ref[...], cw_ref, cb_ref, rows))

    return pl.pallas_call(
        body, name="ssdconv_fwd", grid=(SSD_CONV // LANE,),
        in_specs=[pl.BlockSpec((S, LANE), lambda t: (0, OFF_XBC // LANE + t)), pl.BlockSpec((4, LANE), lambda t: (0, t)),
                  pl.BlockSpec((1, LANE), lambda t: (0, t))],
        out_specs=pl.BlockSpec((S, LANE), lambda t: (0, t)),
        out_shape=SDS((S, SSD_CONV), F32),
        compiler_params=_cp(("parallel",)),
    )(u, cw, cb)


def _ssdconv_bwd(u, cw, cb, dxbc, du):
    S = u.shape[0]

    def body(u_ref, cw_ref, cb_ref, d_ref, du_in, du_ref, red_ref):
        del du_in
        rows = _iota((S, LANE), 0)
        x = u_ref[...]
        dco = d_ref[...] * _dsilu(_conv_fwd(x, cw_ref, cb_ref, rows))
        dx, dws, dcb = _conv_bwd(x, dco, cw_ref, rows)
        du_ref[...] = dx
        for n, p in enumerate(dws + [dcb]):
            red_ref[pl.ds(n, 1), :] = p
        red_ref[pl.ds(5, 3), :] = jnp.zeros((3, LANE), F32)

    ucol = pl.BlockSpec((S, LANE), lambda t: (0, OFF_XBC // LANE + t))
    return pl.pallas_call(
        body, name="ssdconv_bwd", grid=(SSD_CONV // LANE,),
        in_specs=[ucol, pl.BlockSpec((4, LANE), lambda t: (0, t)), pl.BlockSpec((1, LANE), lambda t: (0, t)),
                  pl.BlockSpec((S, LANE), lambda t: (0, t)), pl.BlockSpec(memory_space=pl.ANY)],
        out_specs=[ucol, pl.BlockSpec((8, LANE), lambda t: (0, t))],
        out_shape=[SDS((S, N_PAD), F32), SDS((8, SSD_CONV), F32)],
        input_output_aliases={4: 0},
        compiler_params=_cp(("parallel",)),
    )(u, cw, cb, dxbc, du)


def _ssd_consts():
    e64 = np.zeros((LANE, SSD_W), np.float32)
    e128 = np.zeros((LANE, SSD_HEADS * LANE), np.float32)
    for h in range(SSD_HEADS):
        e64[h, h * SSD_P:(h + 1) * SSD_P] = 1.0
        e128[h, h * LANE:(h + 1) * LANE] = 1.0
    T = SSD_CHUNK
    tril = (np.arange(T)[None, :] <= np.arange(T)[:, None]).astype(np.float32)
    return e64, e128, tril, tril.T.copy()


def _ssd_common(zdt, bias_ref, alog_ref, tril, e64, e128):
    T = SSD_CHUNK
    lane = _iota((1, LANE), 1)
    a_neg = jnp.where(lane < SSD_HEADS, -jnp.exp(alog_ref[...]), 0.0)
    dtpre = zdt[:, SSD_W:SSD_W + LANE] + bias_ref[...]
    dt = _softplus(dtpre)
    cum = _sel_l(tril, dt * a_neg)
    rowsT = _iota((T, LANE), 0)
    last = _last_row(cum, rowsT)
    ecum_x = _sel_r(jnp.exp(cum), e64)
    erem_x = _sel_r(jnp.exp(last - cum), e64)
    elast_x = _last_row(ecum_x, _iota((T, SSD_W), 0))
    dt_x = _sel_r(dt, e64)
    cum_e = _sel_r(cum, e128)
    return a_neg, dtpre, dt, cum, ecum_x, erem_x, elast_x, dt_x, cum_e


def _ssd_decay(cum_e, cumt_ref, h, causal):
    diff = cum_e[:, h * LANE:(h + 1) * LANE] - cumt_ref[pl.ds(h, 1), :]
    return jnp.exp(jnp.where(causal, diff, NEG))


def _group_norm_fwd(y1, nwv):
    outs, invs = [], []
    for g in range(2):
        seg = y1[:, g * 512:(g + 1) * 512]
        inv = lax.rsqrt(jnp.mean(seg * seg, axis=-1, keepdims=True) + EPS)
        outs.append(seg * inv * nwv[:, g * 512:(g + 1) * 512])
        invs.append(inv)
    return outs, invs


def _ssd_fwd(u, xbc, bias, alog, dskip_x, nw, consts, ycat):
    S = u.shape[0]
    T = SSD_CHUNK
    n = S // T
    e64, e128, tril, _ = consts

    def body(u_ref, xbc_ref, bias_ref, alog_ref, dx_ref, nw_ref, e64_ref, e128_ref, tril_ref, ycat_in,
             ycat_ref, y_ref, st_ref, st, cumt):
        del ycat_in

        @pl.when(pl.program_id(0) == 0)
        def _():
            st[...] = jnp.zeros_like(st)

        zdt = u_ref[...]
        z = zdt[:, 0:SSD_W]
        xs = xbc_ref[:, 0:SSD_W]
        _, _, _, cum, ecum_x, erem_x, elast_x, dt_x, cum_e = _ssd_common(
            zdt, bias_ref, alog_ref, tril_ref[...], e64_ref[...], e128_ref[...])
        cumt[...] = cum.T
        causal = _iota((T, T), 0) >= _iota((T, T), 1)
        lo = _iota((T, LANE), 1) < SSD_P
        xdt = xs * dt_x
        xrem = xdt * erem_x
        st_ref[...] = st[...]
        for g in range(2):
            gs = slice(g * 512, (g + 1) * 512)
            bg = _bf(xbc_ref[:, SSD_W + g * LANE:SSD_W + (g + 1) * LANE])
            cg = _bf(xbc_ref[:, SSD_W + 256 + g * LANE:SSD_W + 256 + (g + 1) * LANE])
            cb = _mm_nt(cg, bg)
            yin = _mm(cg, _bf(st[:, gs])) * ecum_x[:, gs]
            for j in range(4):
                h0 = 8 * g + 2 * j
                cs = slice(h0 * SSD_P, (h0 + 2) * SSD_P)
                xp = xdt[:, cs]
                s0 = _bf(cb * _ssd_decay(cum_e, cumt, h0, causal))
                s1 = _bf(cb * _ssd_decay(cum_e, cumt, h0 + 1, causal))
                y_ref[:, cs] = (_mm(s0, _bf(jnp.where(lo, xp, 0.0))) + _mm(s1, _bf(jnp.where(lo, 0.0, xp)))
                                + yin[:, j * LANE:(j + 1) * LANE])
            st[:, gs] = st[:, gs] * elast_x[:, gs] + _mm_tn(bg, _bf(xrem[:, gs]))
        y1 = (y_ref[...] + dx_ref[...] * xs) * _silu(z)
        outs, _ = _group_norm_fwd(y1, nw_ref[...])
        for g in range(2):
            ycat_ref[:, g * 512:(g + 1) * 512] = outs[g]

    return pl.pallas_call(
        body, name="ssd_fwd", grid=(n,),
        in_specs=[pl.BlockSpec((T, SSD_W + LANE), lambda i: (i, OFF_Z // (SSD_W + LANE))),
                  pl.BlockSpec((T, SSD_CONV), lambda i: (i, 0)), _vec(LANE), _vec(LANE), _vec(SSD_W), _vec(SSD_W),
                  _full(e64.shape), _full(e128.shape), _full(tril.shape), pl.BlockSpec(memory_space=pl.ANY)],
        out_specs=[pl.BlockSpec((T, SSD_W), lambda i: (i, 1)), pl.BlockSpec((T, SSD_W), lambda i: (i, 0)),
                   pl.BlockSpec((None, SSD_N, SSD_W), lambda i: (i, 0, 0))],
        out_shape=[SDS((S, D_INNER), F32), SDS((S, SSD_W), F32), SDS((n, SSD_N, SSD_W), F32)],
        scratch_shapes=[pltpu.VMEM((SSD_N, SSD_W), F32), pltpu.VMEM((LANE, T), F32)],
        input_output_aliases={9: 0},
        compiler_params=_cp(("arbitrary",)),
    )(u, xbc, bias, alog, dskip_x, nw, _bfc(e64), _bfc(e128), _bfc(tril), ycat)


def _ssd_bwd(u, xbc, bias, alog, dskip_x, nw, consts, y_ssd, states, dycat, du):
    S = u.shape[0]
    T = SSD_CHUNK
    n = S // T
    e64, e128, tril, triu = consts
    e64t = np.ascontiguousarray(e64.T)

    def body(u_ref, xbc_ref, bias_ref, alog_ref, dx_ref, nw_ref, e64_ref, e64t_ref, e128_ref, tril_ref, triu_ref,
             y_ref, st_ref, dy_ref, du_in, du_ref, dxbc_ref, red_ref, dst, dl_s, cumt, dxdt_s, dy0_s, gb_s, gc_s):
        del du_in

        @pl.when(pl.program_id(0) == 0)
        def _():
            dst[...] = jnp.zeros_like(dst)
            red_ref[...] = jnp.zeros_like(red_ref)

        zdt = u_ref[...]
        z = zdt[:, 0:SSD_W]
        xs = xbc_ref[:, 0:SSD_W]
        e64m = e64_ref[...]
        a_neg, dtpre, dt, cum, ecum_x, erem_x, elast_x, dt_x, cum_e = _ssd_common(
            zdt, bias_ref, alog_ref, tril_ref[...], e64m, e128_ref[...])
        cumt[...] = cum.T
        causal = _iota((T, T), 0) >= _iota((T, T), 1)
        lo = _iota((T, LANE), 1) < SSD_P
        xdt = xs * dt_x
        xrem = xdt * erem_x
        y = y_ref[...]
        dxv = dx_ref[...]
        nwv = nw_ref[...]
        sz = _silu(z)
        y0 = y + dxv * xs
        y1 = y0 * sz
        for g in range(2):
            gs = slice(g * 512, (g + 1) * 512)
            seg = y1[:, gs]
            inv = lax.rsqrt(jnp.mean(seg * seg, axis=-1, keepdims=True) + EPS)
            shat = seg * inv
            dyg = dy_ref[:, gs]
            red_ref[0:1, gs] += jnp.sum(dyg * shat, axis=0, keepdims=True)
            dsh = dyg * nwv[:, gs]
            dy1g = inv * (dsh - shat * jnp.mean(dsh * shat, axis=-1, keepdims=True))
            du_ref[:, gs] = dy1g * y0[:, gs] * _dsilu(z[:, gs])
            dy0_s[:, gs] = dy1g * sz[:, gs]
        dy0 = dy0_s[...]
        red_ref[1:2, :] += jnp.sum(dy0 * xs, axis=0, keepdims=True)
        dyin = dy0 * ecum_x
        lane = _iota((T, LANE), 1)
        ones = jnp.ones((T, LANE), BF16)
        dcum = jnp.zeros((T, LANE), F32)

        def row_minus_col(gm):
            hi = _bf(gm)
            lw = _bf(gm - hi.astype(F32))
            return _mm(hi, ones) + _mm(lw, ones) - _mm_tn(hi, ones) - _mm_tn(lw, ones)

        for g in range(2):
            gs = slice(g * 512, (g + 1) * 512)
            bg = _bf(xbc_ref[:, SSD_W + g * LANE:SSD_W + (g + 1) * LANE])
            cg = _bf(xbc_ref[:, SSD_W + 256 + g * LANE:SSD_W + 256 + (g + 1) * LANE])
            cb = _mm_nt(cg, bg)
            dst_f, st_f = dst[:, gs], st_ref[:, gs]
            dstg = _bf(dst_f)
            stg = _bf(st_f)
            dyin_g = _bf(dyin[:, gs])
            xrem_g = _bf(xrem[:, gs])
            dcb = jnp.zeros((T, T), F32)
            dxr = _mm(bg, dstg)
            dxdt_s[:, gs] = dxr * erem_x[:, gs]
            gc_s[:, gs] = dxr * xrem[:, gs]
            gb_s[:, gs] = dyin[:, gs] * _mm(cg, stg)
            dl_s[:, gs] = jnp.sum(dst_f * st_f, axis=0, keepdims=True) * elast_x[:, gs]
            for j in range(4):
                h0 = 8 * g + 2 * j
                cs = slice(h0 * SSD_P, (h0 + 2) * SSD_P)
                xp = xdt[:, cs]
                dyp = dy0[:, cs]
                x_lo, x_hi = _bf(jnp.where(lo, xp, 0.0)), _bf(jnp.where(lo, 0.0, xp))
                d_lo, d_hi = _bf(jnp.where(lo, dyp, 0.0)), _bf(jnp.where(lo, 0.0, dyp))
                s0 = cb * _ssd_decay(cum_e, cumt, h0, causal)
                s1 = cb * _ssd_decay(cum_e, cumt, h0 + 1, causal)
                ds0 = _mm_nt(d_lo, x_lo)
                ds1 = _mm_nt(d_hi, x_hi)
                dcb = dcb + ds0 * _ssd_decay(cum_e, cumt, h0, causal) + ds1 * _ssd_decay(cum_e, cumt, h0 + 1, causal)
                dxdt_s[:, cs] += _mm_tn(_bf(s0), d_lo) + _mm_tn(_bf(s1), d_hi)
                dcum = dcum + jnp.where(lane == h0, row_minus_col(ds0 * s0), 0.0)
                dcum = dcum + jnp.where(lane == h0 + 1, row_minus_col(ds1 * s1), 0.0)
            dcb_b = _bf(dcb)
            dxbc_ref[:, SSD_W + g * LANE:SSD_W + (g + 1) * LANE] = _mm_tn(dcb_b, cg) + _mm_nt(xrem_g, dstg)
            dxbc_ref[:, SSD_W + 256 + g * LANE:SSD_W + 256 + (g + 1) * LANE] = _mm(dcb_b, bg) + _mm_nt(dyin_g, stg)
            dst[:, gs] = dst_f * elast_x[:, gs] + _mm_tn(cg, dyin_g)
        dxdt = dxdt_s[...]
        dxbc_ref[:, 0:SSD_W] = dxdt * dt_x + dy0 * dxv
        e64t = e64t_ref[...]
        hc = _sel_r(gc_s[...], e64t)
        dlast = (jnp.sum(hc, axis=0, keepdims=True)
                 + jnp.max(_sel_r(jnp.broadcast_to(dl_s[...], (8, SSD_W)), e64t), axis=0, keepdims=True))
        dcum = dcum + _sel_r(gb_s[...], e64t) - hc + jnp.where(_iota((T, LANE), 0) == T - 1, dlast, 0.0)
        dda = _sel_l(triu_ref[...], dcum)
        ddt = dda * a_neg + _sel_r(dxdt * xs, e64t)
        ddtpre = ddt * _sigmoid(dtpre)
        du_ref[:, SSD_W:SSD_W + LANE] = jnp.where(lane < SSD_HEADS, ddtpre, 0.0)
        red_ref[2:3, 0:LANE] += jnp.sum(ddtpre, axis=0, keepdims=True)
        red_ref[3:4, 0:LANE] += jnp.sum(dda * dt, axis=0, keepdims=True)

    rev = lambda i: (n - 1 - i, 0)
    return pl.pallas_call(
        body, name="ssd_bwd", grid=(n,),
        in_specs=[pl.BlockSpec((T, SSD_W + LANE), lambda i: (n - 1 - i, OFF_Z // (SSD_W + LANE))),
                  pl.BlockSpec((T, SSD_CONV), rev), _vec(LANE), _vec(LANE), _vec(SSD_W), _vec(SSD_W),
                  _full(e64.shape), _full(e64t.shape), _full(e128.shape), _full(tril.shape), _full(triu.shape),
                  pl.BlockSpec((T, SSD_W), rev), pl.BlockSpec((None, SSD_N, SSD_W), lambda i: (n - 1 - i, 0, 0)),
                  pl.BlockSpec((T, SSD_W), lambda i: (n - 1 - i, 1)), pl.BlockSpec(memory_space=pl.ANY)],
        out_specs=[pl.BlockSpec((T, SSD_W + LANE), lambda i: (n - 1 - i, OFF_Z // (SSD_W + LANE))),
                   pl.BlockSpec((T, SSD_CONV), rev), pl.BlockSpec((8, SSD_W), lambda i: (0, 0))],
        out_shape=[SDS((S, N_PAD), F32), SDS((S, SSD_CONV), F32), SDS((8, SSD_W), F32)],
        scratch_shapes=[pltpu.VMEM((SSD_N, SSD_W), F32), pltpu.VMEM((1, SSD_W), F32), pltpu.VMEM((LANE, T), F32)]
        + [pltpu.VMEM((T, SSD_W), F32)] * 4,
        input_output_aliases={14: 0},
        compiler_params=_cp(("arbitrary",)),
    )(u, xbc, bias, alog, dskip_x, nw, _bfc(e64), _bfc(e64t), _bfc(e128), _bfc(tril), _bfc(triu), y_ssd, states, dycat, du)


def _bfc(a):
    return jnp.asarray(a, BF16)


def _outproj_fwd(ycat, wo, x, gate):
    S = x.shape[0]
    tm = min(512, S)

    def body(yc_ref, wo_ref, x_ref, g_ref, xn_ref, y_ref):
        y = _mm(_bf(yc_ref[...]), wo_ref[...])
        y_ref[...] = y
        xn_ref[...] = x_ref[...] + g_ref[...] * y

    row = pl.BlockSpec((tm, D_MODEL), lambda i: (i, 0))
    return pl.pallas_call(
        body, name="outproj_fwd", grid=(S // tm,),
        in_specs=[pl.BlockSpec((tm, D_INNER), lambda i: (i, 0)), _full((D_INNER, D_MODEL)), row, _vec(D_MODEL)],
        out_specs=[row, row],
        out_shape=[SDS((S, D_MODEL), F32), SDS((S, D_MODEL), F32)],
        compiler_params=_cp(("parallel",)),
    )(ycat, wo, x, gate)


def _outproj_bwd(dxn, y, gate, ycat, wo):
    S = dxn.shape[0]
    tm = min(512, S)

    def body(dx_ref, y_ref, g_ref, yc_ref, wo_ref, dyc_ref, gwo_ref, dg_ref):
        @pl.when(pl.program_id(0) == 0)
        def _():
            gwo_ref[...] = jnp.zeros_like(gwo_ref)
            dg_ref[...] = jnp.zeros_like(dg_ref)

        dxv = dx_ref[...]
        dy = _bf(dxv * g_ref[...])
        dg_ref[0:1, :] += jnp.sum(dxv * y_ref[...], axis=0, keepdims=True)
        dyc_ref[...] = _mm_nt(dy, wo_ref[...])
        gwo_ref[...] += _mm_tn(_bf(yc_ref[...]), dy)

    row = pl.BlockSpec((tm, D_MODEL), lambda i: (i, 0))
    wide = pl.BlockSpec((tm, D_INNER), lambda i: (i, 0))
    return pl.pallas_call(
        body, name="outproj_bwd", grid=(S // tm,),
        in_specs=[row, row, _vec(D_MODEL), wide, _full((D_INNER, D_MODEL))],
        out_specs=[wide, _full((D_INNER, D_MODEL)), _full((8, D_MODEL))],
        out_shape=[SDS((S, D_INNER), F32), SDS((D_INNER, D_MODEL), F32), SDS((8, D_MODEL), F32)],
        compiler_params=_cp(("arbitrary",)),
    )(dxn, y, gate, ycat, wo)


def _loss_head(x, fw, target):
    S = x.shape[0]
    tm = min(512, S)

    def body(x_ref, fw_ref, t_ref, dx_ref, red_ref):
        @pl.when(pl.program_id(0) == 0)
        def _():
            red_ref[...] = jnp.zeros_like(red_ref)

        xv = x_ref[...]
        fwv = fw_ref[...]
        inv = lax.rsqrt(jnp.mean(xv * xv, axis=-1, keepdims=True) + EPS)
        xhat = xv * inv
        err = xhat * fwv - t_ref[...]
        col = jnp.sum(err * err, axis=0, keepdims=True)
        red_ref[1:2, :] += jnp.broadcast_to(jnp.sum(col, axis=1, keepdims=True) * (0.5 / D_MODEL), (1, D_MODEL))
        dy = err * (1.0 / D_MODEL)
        red_ref[0:1, :] += jnp.sum(dy * xhat, axis=0, keepdims=True)
        dxhat = dy * fwv
        dx_ref[...] = inv * (dxhat - xhat * jnp.mean(dxhat * xhat, axis=-1, keepdims=True))

    row = pl.BlockSpec((tm, D_MODEL), lambda i: (i, 0))
    return pl.pallas_call(
        body, name="loss_head", grid=(S // tm,),
        in_specs=[row, _vec(D_MODEL), row],
        out_specs=[row, _full((8, D_MODEL))],
        out_shape=[SDS((S, D_MODEL), F32), SDS((8, D_MODEL), F32)],
        compiler_params=_cp(("arbitrary",)),
    )(x, fw, target)


ADA_COLS = 3 * D_MODEL // N_DEV


def _ada_fwd(c_all, w_ada, b_cols):
    def body(c_ref, w_ref, b_ref, out_ref):
        out_ref[...] = _mm(_bf(_silu(c_ref[...])), _bf(w_ref[...])) + b_ref[...]

    return pl.pallas_call(
        body, name="ada_fwd", grid=(DEPTH,),
        in_specs=[_full((N_DEV, D_MODEL)), pl.BlockSpec((None, D_MODEL, ADA_COLS), lambda l: (l, 0, 0)),
                  pl.BlockSpec((None, 1, ADA_COLS), lambda l: (l, 0, 0))],
        out_specs=pl.BlockSpec((None, N_DEV, ADA_COLS), lambda l: (l, 0, 0)),
        out_shape=SDS((DEPTH, N_DEV, ADA_COLS), F32),
        compiler_params=_cp(("parallel",)),
    )(c_all, w_ada, b_cols)


def _ada_bwd(ct_pad, dmod_pad):
    def body(c_ref, d_ref, out_ref):
        out_ref[...] = _mm(_bf(_silu(c_ref[...])), _bf(d_ref[...]))

    return pl.pallas_call(
        body, name="ada_bwd", grid=(DEPTH,),
        in_specs=[_full((D_MODEL, LANE)), pl.BlockSpec((None, LANE, ADA_COLS), lambda l: (l, 0, 0))],
        out_specs=pl.BlockSpec((None, D_MODEL, ADA_COLS), lambda l: (l, 0, 0)),
        out_shape=SDS((DEPTH, D_MODEL, ADA_COLS), F32),
        compiler_params=_cp(("parallel",)),
    )(ct_pad, dmod_pad)


def _adamw(parts, w, m, v, name):
    n, R, C = parts.shape
    tr = R
    while tr * C * 4 > (1 << 20) and tr % 16 == 0:
        tr //= 2

    def body(p_ref, w_ref, m_ref, v_ref, g_ref, d_ref, mo_ref, vo_ref):
        g = p_ref[0].astype(F32)
        for k in range(1, n):
            g = g + p_ref[k].astype(F32)
        mn = ADAM_B1 * m_ref[...] + (1.0 - ADAM_B1) * g
        vn = ADAM_B2 * v_ref[...] + (1.0 - ADAM_B2) * (g * g)
        m_hat = mn / (1.0 - ADAM_B1 ** ADAM_STEP)
        v_hat = vn / (1.0 - ADAM_B2 ** ADAM_STEP)
        g_ref[...] = g
        d_ref[...] = -ADAM_LR * (m_hat / (jnp.sqrt(v_hat) + ADAM_EPS) + ADAM_WD * w_ref[...])
        mo_ref[...] = mn
        vo_ref[...] = vn

    blk = pl.BlockSpec((tr, C), lambda i: (i, 0))
    return pl.pallas_call(
        body, name=name, grid=(R // tr,),
        in_specs=[pl.BlockSpec((n, tr, C), lambda i: (0, i, 0)), blk, blk, blk],
        out_specs=[blk] * 4,
        out_shape=[SDS((R, C), F32)] * 4,
        compiler_params=_cp(("parallel",)),
    )(parts, w, m, v)


MESH = pl.DeviceIdType.MESH
ANY = pl.BlockSpec(memory_space=pl.ANY)


def _all_gather(v, name):
    def body(v_ref, out_ref, send_sems, recv_sems, local_sem):
        x, y, c = lax.axis_index("x"), lax.axis_index("y"), lax.axis_index("c")
        me, sibling = (x, y, c), (x, y, 1 - c)
        chips = [(1 - x, y), (x, 1 - y), (1 - x, 1 - y)]

        def slot(px, py, pc):
            return out_ref.at[4 * px + 2 * py + pc]

        def copy(k, block, to, src=None):
            return pltpu.make_async_remote_copy(
                src_ref=slot(*block) if src is None else src, dst_ref=slot(*block),
                send_sem=send_sems.at[k], recv_sem=recv_sems.at[k], device_id=to, device_id_type=MESH)

        mine = pltpu.make_async_copy(v_ref, slot(*me), local_sem)
        mine.start()
        first = [copy(0, me, sibling, src=v_ref)]
        first += [copy(1 + j, me, (*chip, c), src=v_ref) for j, chip in enumerate(chips)]
        for cp in first:
            cp.start()
        passed = [copy(4 + j, (*chip, c), sibling) for j, chip in enumerate(chips)]
        for j, chip in enumerate(chips):
            copy(1 + j, (*chip, c), me).wait_recv()
            passed[j].start()
        copy(0, sibling, me).wait_recv()
        for j, chip in enumerate(chips):
            copy(4 + j, (*chip, 1 - c), me).wait_recv()
        for cp in first + passed:
            cp.wait_send()
        mine.wait()

    return pl.pallas_call(
        body, name=name, in_specs=[ANY], out_specs=ANY,
        out_shape=SDS((N_DEV,) + v.shape, v.dtype),
        scratch_shapes=[pltpu.SemaphoreType.DMA((7,)), pltpu.SemaphoreType.DMA((7,)), pltpu.SemaphoreType.DMA],
    )(v)


def _all_to_all(v, name):
    def body(v_ref, out_ref, send_sems, recv_sems, local_sem):
        x, y, c = lax.axis_index("x"), lax.axis_index("y"), lax.axis_index("c")
        mine_idx = 4 * x + 2 * y + c
        mine = pltpu.make_async_copy(v_ref.at[mine_idx], out_ref.at[mine_idx], local_sem)
        mine.start()
        sends, recvs = [], []
        for k in range(1, N_DEV):
            px = 1 - x if k & 4 else x
            py = 1 - y if k & 2 else y
            pc = 1 - c if k & 1 else c
            peer_idx = 4 * px + 2 * py + pc
            sems = dict(send_sem=send_sems.at[k - 1], recv_sem=recv_sems.at[k - 1], device_id=(px, py, pc),
                        device_id_type=MESH)
            sends.append(pltpu.make_async_remote_copy(src_ref=v_ref.at[peer_idx], dst_ref=out_ref.at[mine_idx], **sems))
            recvs.append(pltpu.make_async_remote_copy(src_ref=v_ref.at[peer_idx], dst_ref=out_ref.at[peer_idx], **sems))
        for cp in sends:
            cp.start()
        for cp in recvs:
            cp.wait_recv()
        for cp in sends:
            cp.wait_send()
        mine.wait()

    return pl.pallas_call(
        body, name=name, in_specs=[ANY], out_specs=ANY,
        out_shape=SDS(v.shape, v.dtype),
        scratch_shapes=[pltpu.SemaphoreType.DMA((7,)), pltpu.SemaphoreType.DMA((7,)), pltpu.SemaphoreType.DMA],
    )(v)


_IN_PIECES = ([(1024, 3072)]
              + [r for t in range(4) for r in ((LANE * t, LANE * (t + 1)), (512 + LANE * t, 512 + LANE * (t + 1)))]
              + [(4096, 5632), (3072, 4096), (5632, 5648)])


def _permute_in(w):
    pad = jnp.zeros(w.shape[:-1] + (N_PAD - N_IN,), w.dtype)
    return jnp.concatenate([w[..., a:b] for a, b in _IN_PIECES] + [pad], axis=-1)


def _unpermute_in(g):
    ax = [g[..., OFF_LRU + 2 * LANE * t:OFF_LRU + 2 * LANE * t + LANE] for t in range(4)]
    ag = [g[..., OFF_LRU + 2 * LANE * t + LANE:OFF_LRU + 2 * LANE * (t + 1)] for t in range(4)]
    return jnp.concatenate(ax + ag + [g[..., 0:2048], g[..., OFF_Z:OFF_Z + SSD_W], g[..., OFF_XBC:OFF_XBC + SSD_CONV],
                                      g[..., OFF_Z + SSD_W:OFF_Z + SSD_W + SSD_HEADS]], axis=-1)


def _block_diag(w):
    w4 = w.reshape(4, 2, 64, 64)
    z = jnp.zeros((4, 64, 64), w.dtype)
    top = jnp.concatenate([w4[:, 0], z], axis=-1)
    bot = jnp.concatenate([z, w4[:, 1]], axis=-1)
    return jnp.concatenate([top, bot], axis=1).astype(BF16)


def _diag_blocks(g):
    return jnp.stack([g[:, :64, :64], g[:, 64:, 64:]], axis=1).reshape(8, 64, 64)


def _pad_lanes(v):
    return jnp.pad(v, (0, LANE - v.shape[0]))[None, :]


def _lower_bounds(logits):
    p = jax.nn.softmax(logits, axis=0)
    return p, jnp.cumsum(p, axis=0) - p[0]


def _lower_bounds_bwd(p, dlb):
    dp = jnp.cumsum(dlb[::-1], axis=0)[::-1]
    dp = dp.at[0].add(-jnp.sum(dlb, axis=0))
    return p * (dp - jnp.sum(dp * p, axis=0, keepdims=True))


SMALL = ["norm_w", "b_ada", "lru_conv_b", "lru_wa", "lru_ba", "lru_wx", "lru_bx", "lru_lambda", "hg_lb_logits",
         "hg_norm_w", "ssd_conv_b", "ssd_dt_bias", "ssd_a_log", "ssd_d", "ssd_norm_w", "final_norm_w"]
WEIGHTS = ["norm_w", "w_ada", "b_ada", "w_in", "lru_conv_w", "lru_conv_b", "lru_wa", "lru_ba", "lru_wx", "lru_bx",
           "lru_lambda", "hg_lb_logits", "hg_norm_w", "ssd_conv_w", "ssd_conv_b", "ssd_dt_bias", "ssd_a_log", "ssd_d",
           "ssd_norm_w", "w_out", "final_norm_w"]
INPUTS = ["x", "c"] + WEIGHTS + ["loss_target"] + ["m_" + n for n in WEIGHTS] + ["v_" + n for n in WEIGHTS]
SMALL_ROW = 1024


def _flatten_small(d, prefix=""):
    flat = jnp.concatenate([d[prefix + n].reshape(-1) for n in SMALL])
    return jnp.pad(flat, (0, -flat.shape[0] % (8 * SMALL_ROW)))


def _split_small(flat, like):
    out, off = {}, 0
    for n in SMALL:
        size = int(np.prod(like[n].shape))
        out[n] = flat[off:off + size].reshape(like[n].shape)
        off += size
    return out


def _local_step(x, mod, target, w):
    S = x.shape[0]
    mall = _bfc(_hg_consts())
    mall_t = _bfc(_hg_consts().T)
    consts = _ssd_consts()
    p_lb, lbs = _lower_bounds(w["hg_lb_logits"])
    saved = []
    for l in range(DEPTH):
        shift, scale, gate = (mod[l:l + 1, k * D_MODEL:(k + 1) * D_MODEL] for k in range(3))
        prm = dict(
            nw=w["norm_w"][l:l + 1], cw=w["lru_conv_w"][l], cb=w["lru_conv_b"][l:l + 1],
            wa=_block_diag(w["lru_wa"][l]), ba=w["lru_ba"][l].reshape(1, LRU_W),
            wx=_block_diag(w["lru_wx"][l]), bx=w["lru_bx"][l].reshape(1, LRU_W), lam=w["lru_lambda"][l:l + 1],
            lb=lbs[l:l + 1], hnw=w["hg_norm_w"][l:l + 1], scw=w["ssd_conv_w"][l], scb=w["ssd_conv_b"][l:l + 1],
            bias=_pad_lanes(w["ssd_dt_bias"][l]), alog=_pad_lanes(w["ssd_a_log"][l]),
            dskip=jnp.repeat(w["ssd_d"][l], SSD_P)[None, :], snw=w["ssd_norm_w"][l:l + 1],
            w_in=w["w_in"][l], w_out=w["w_out"][l], scale=scale, gate=gate)
        u, h = _inproj_fwd(x, prm["nw"], scale, shift, prm["w_in"])
        ycat = lax.empty((S, D_INNER), F32)
        lru_args = (u, prm["cw"], prm["cb"], prm["wa"], prm["ba"], prm["wx"], prm["bx"], prm["lam"])
        ycat, h_lru = _lru_fwd(*lru_args, ycat)
        ycat, o_b, hg_st = _hg_fwd(u, prm["lb"], prm["hnw"], mall, ycat)
        xbc = _ssdconv_fwd(u, prm["scw"], prm["scb"])
        ssd_args = (u, xbc, prm["bias"], prm["alog"], prm["dskip"], prm["snw"], consts)
        ycat, y_ssd, ssd_st = _ssd_fwd(*ssd_args, ycat)
        x_new, y = _outproj_fwd(ycat, prm["w_out"], x, gate)
        saved.append((prm, x, u, h, ycat, lru_args, h_lru, o_b, hg_st, ssd_args, y_ssd, ssd_st, y))
        x = x_new
    dx, red = _loss_head(x, w["final_norm_w"][None, :], target)
    loss = red[1, 0]
    g = {n: [None] * DEPTH for n in WEIGHTS}
    g["final_norm_w"] = red[0]
    dmod, dlb = [None] * DEPTH, [None] * DEPTH
    for l in reversed(range(DEPTH)):
        prm, x, u, h, ycat, lru_args, h_lru, o_b, hg_st, ssd_args, y_ssd, ssd_st, y = saved[l]
        dycat, g["w_out"][l], dgate = _outproj_bwd(dx, y, prm["gate"], ycat, prm["w_out"])
        du = lax.empty((S, N_PAD), F32)
        du, dxbc, sred = _ssd_bwd(*ssd_args, y_ssd, ssd_st, dycat, du)
        du, cred = _ssdconv_bwd(u, prm["scw"], prm["scb"], dxbc, du)
        du, hred = _hg_bwd(u, prm["lb"], prm["hnw"], mall, mall_t, o_b, hg_st, dycat, du)
        du, lred, gwa, gwx = _lru_bwd(*lru_args, h_lru, dycat, du)
        dx, ired = _inproj_bwd_x(du, prm["w_in"], x, prm["nw"], prm["scale"], dx)
        g["w_in"][l] = _inproj_bwd_w(h, du)
        g["norm_w"][l] = ired[2]
        dmod[l] = jnp.concatenate([ired[0], ired[1], dgate[0]])
        g["lru_conv_w"][l], g["lru_conv_b"][l] = lred[0:4], lred[4]
        g["lru_ba"][l], g["lru_bx"][l], g["lru_lambda"][l] = lred[5].reshape(8, 64), lred[6].reshape(8, 64), lred[7]
        g["lru_wa"][l], g["lru_wx"][l] = _diag_blocks(gwa), _diag_blocks(gwx)
        g["hg_norm_w"][l], dlb[l] = hred[0], hred[1]
        g["ssd_conv_w"][l], g["ssd_conv_b"][l] = cred[0:4], cred[4]
        g["ssd_norm_w"][l] = sred[0]
        g["ssd_d"][l] = sred[1].reshape(SSD_HEADS, SSD_P).sum(-1)
        g["ssd_dt_bias"][l] = sred[2, :SSD_HEADS]
        g["ssd_a_log"][l] = -sred[3, :SSD_HEADS] * jnp.exp(w["ssd_a_log"][l])
    g["hg_lb_logits"] = _lower_bounds_bwd(p_lb, jnp.stack(dlb))
    for n in WEIGHTS:
        if isinstance(g[n], list) and g[n][0] is not None:
            g[n] = jnp.stack(g[n])
    return loss, dx, jnp.stack(dmod), g


def kernel(x, c, norm_w, w_ada, b_ada, w_in, lru_conv_w, lru_conv_b, lru_wa, lru_ba, lru_wx, lru_bx, lru_lambda, hg_lb_logits, hg_norm_w, ssd_conv_w, ssd_conv_b, ssd_dt_bias, ssd_a_log, ssd_d, ssd_norm_w, w_out, final_norm_w, loss_target, m_norm_w, m_w_ada, m_b_ada, m_w_in, m_lru_conv_w, m_lru_conv_b, m_lru_wa, m_lru_ba, m_lru_wx, m_lru_bx, m_lru_lambda, m_hg_lb_logits, m_hg_norm_w, m_ssd_conv_w, m_ssd_conv_b, m_ssd_dt_bias, m_ssd_a_log, m_ssd_d, m_ssd_norm_w, m_w_out, m_final_norm_w, v_norm_w, v_w_ada, v_b_ada, v_w_in, v_lru_conv_w, v_lru_conv_b, v_lru_wa, v_lru_ba, v_lru_wx, v_lru_bx, v_lru_lambda, v_hg_lb_logits, v_hg_norm_w, v_ssd_conv_w, v_ssd_conv_b, v_ssd_dt_bias, v_ssd_a_log, v_ssd_d, v_ssd_norm_w, v_w_out, v_final_norm_w):
    return _step(x, c, norm_w, w_ada, b_ada, w_in, lru_conv_w, lru_conv_b, lru_wa, lru_ba, lru_wx, lru_bx, lru_lambda, hg_lb_logits, hg_norm_w, ssd_conv_w, ssd_conv_b, ssd_dt_bias, ssd_a_log, ssd_d, ssd_norm_w, w_out, final_norm_w, loss_target, m_norm_w, m_w_ada, m_b_ada, m_w_in, m_lru_conv_w, m_lru_conv_b, m_lru_wa, m_lru_ba, m_lru_wx, m_lru_bx, m_lru_lambda, m_hg_lb_logits, m_hg_norm_w, m_ssd_conv_w, m_ssd_conv_b, m_ssd_dt_bias, m_ssd_a_log, m_ssd_d, m_ssd_norm_w, m_w_out, m_final_norm_w, v_norm_w, v_w_ada, v_b_ada, v_w_in, v_lru_conv_w, v_lru_conv_b, v_lru_wa, v_lru_ba, v_lru_wx, v_lru_bx, v_lru_lambda, v_hg_lb_logits, v_hg_norm_w, v_ssd_conv_w, v_ssd_conv_b, v_ssd_dt_bias, v_ssd_a_log, v_ssd_d, v_ssd_norm_w, v_w_out, v_final_norm_w)


def _step(*args):
    a = dict(zip(INPUTS, args, strict=True))
    me = 4 * lax.axis_index("x") + 2 * lax.axis_index("y") + lax.axis_index("c")
    x, target = a["x"][0], a["loss_target"][0]

    c_all = _all_gather(a["c"], "gather_c")[:, 0, :]
    b_cols = lax.dynamic_slice_in_dim(a["b_ada"], me * ADA_COLS, ADA_COLS, axis=1)[:, None, :]
    mod_parts = _all_gather(_ada_fwd(c_all, a["w_ada"], b_cols), "gather_mod")
    mod = lax.dynamic_index_in_dim(mod_parts, me, axis=2, keepdims=False)
    mod = mod.transpose(1, 0, 2).reshape(DEPTH, 3 * D_MODEL)

    w = {n: a[n] for n in SMALL}
    w_in = _all_gather(a["w_in"].astype(BF16), "gather_w_in")
    w["w_in"] = _permute_in(w_in.transpose(1, 2, 0, 3).reshape(DEPTH, D_MODEL, N_IN))
    w_out = _all_gather(a["w_out"].astype(BF16), "gather_w_out")
    w["w_out"] = w_out.transpose(1, 0, 2, 3).reshape(DEPTH, D_INNER, D_MODEL)
    conv = _all_gather(jnp.concatenate([a["lru_conv_w"], a["ssd_conv_w"]], axis=-1), "gather_conv")
    conv = conv.transpose(1, 2, 0, 3)
    w["lru_conv_w"] = conv[..., :64].reshape(DEPTH, 4, LRU_W)
    w["ssd_conv_w"] = conv[..., 64:].reshape(DEPTH, 4, SSD_CONV)

    loss, dx, dmod, g = _local_step(x, mod, target, w)
    loss = lax.psum(loss, ("x", "y", "c"))

    g["b_ada"] = dmod
    small = _all_gather(_flatten_small(g).reshape(-1, SMALL_ROW), "gather_small")
    rows = small.shape[1]
    fl = lambda prefix: _flatten_small(a, prefix).reshape(rows, SMALL_ROW)
    outs = _adamw(small, fl(""), fl("m_"), fl("v_"), "adamw_small")
    res = [_split_small(o.reshape(-1), a) for o in outs]

    off = DEPTH * D_MODEL
    dmod_all = small.reshape(N_DEV, -1)[:, off:off + DEPTH * 3 * D_MODEL]
    dmod_all = dmod_all.reshape(N_DEV, DEPTH, 3 * D_MODEL).transpose(1, 0, 2)
    dmod_cols = lax.dynamic_slice_in_dim(dmod_all, me * ADA_COLS, ADA_COLS, axis=2)
    dmod_pad = jnp.pad(dmod_cols, ((0, 0), (0, LANE - N_DEV), (0, 0)))
    ct_pad = jnp.pad(c_all.T, ((0, 0), (0, LANE - N_DEV)))
    g_ada = _ada_bwd(ct_pad, dmod_pad)

    def sharded(name, parts):
        shape = a[name].shape
        r2 = lambda t: t.reshape(-1, shape[-1])
        o = _adamw(parts.reshape(parts.shape[0], -1, shape[-1]), r2(a[name]), r2(a["m_" + name]), r2(a["v_" + name]),
                   "adamw_" + name)
        return [t.reshape(shape) for t in o]

    big = {"w_ada": sharded("w_ada", g_ada[None])}
    g_in = _unpermute_in(g["w_in"]).reshape(DEPTH, D_MODEL, N_DEV, N_IN // N_DEV).transpose(2, 0, 1, 3)
    big["w_in"] = sharded("w_in", _all_to_all(g_in.astype(BF16), "scatter_w_in"))
    g_out = g["w_out"].reshape(DEPTH, N_DEV, D_INNER // N_DEV, D_MODEL).transpose(1, 0, 2, 3)
    big["w_out"] = sharded("w_out", _all_to_all(g_out.astype(BF16), "scatter_w_out"))
    g_conv = jnp.concatenate([g["lru_conv_w"].reshape(DEPTH, 4, N_DEV, 64), g["ssd_conv_w"].reshape(DEPTH, 4, N_DEV, 192)],
                             axis=-1).transpose(2, 0, 1, 3)
    conv_parts = _all_to_all(g_conv, "scatter_conv")
    big["lru_conv_w"] = sharded("lru_conv_w", conv_parts[..., :64])
    big["ssd_conv_w"] = sharded("ssd_conv_w", conv_parts[..., 64:])

    out = [loss, dx[None]]
    for k in range(4):
        out += [big[n][k] if n in big else res[k][n] for n in WEIGHTS]
    return tuple(out)
```

```python
import functools

import numpy as np
import jax
import jax.numpy as jnp
from jax import lax
from jax.experimental import pallas as pl
from jax.experimental.pallas import tpu as pltpu

F32 = jnp.float32
BF16 = jnp.bfloat16
SDS = jax.ShapeDtypeStruct

N_DEV = 8
DEPTH = 4
D_MODEL = 1024
D_INNER = 2048
EPS = 1e-6
LRU_W = 512
LRU_C = 8.0
HG_W = 512
HG_CHUNK = 64
HG_HEADS = 4
SSD_W = 1024
SSD_HEADS = 16
SSD_P = 64
SSD_N = 128
SSD_CHUNK = 128
SSD_CONV = 1536
N_IN = 5648
N_PAD = 5760
OFF_HG = 0
OFF_LRU = 2048
OFF_XBC = 3072
OFF_Z = 4608
LANE = 128
VMEM_LIMIT = 56 * 1024 * 1024
NEG = -1e30

ADAM_LR = 0.001
ADAM_B1 = 0.9
ADAM_B2 = 0.999
ADAM_EPS = 1e-08
ADAM_WD = 0.01
ADAM_STEP = 10


def _cp(sem=None):
    return pltpu.CompilerParams(dimension_semantics=sem, vmem_limit_bytes=VMEM_LIMIT)


def _dg(a, b, ca, cb):
    return lax.dot_general(a, b, (((ca,), (cb,)), ((), ())), preferred_element_type=F32)


def _mm(a, b):
    return _dg(a, b, 1, 0)


def _mm_nt(a, b):
    return _dg(a, b, 1, 1)


def _mm_tn(a, b):
    return _dg(a, b, 0, 0)


def _bf(x):
    return x.astype(BF16)


def _split3(x):
    hi = x.astype(BF16)
    r = x - hi.astype(F32)
    mid = r.astype(BF16)
    lo = (r - mid.astype(F32)).astype(BF16)
    return hi, mid, lo


def _sel_r(x, m):
    hi, mid, lo = _split3(x)
    return _mm(hi, m) + _mm(mid, m) + _mm(lo, m)


def _sel_l(m, x):
    hi, mid, lo = _split3(x)
    return _mm(m, hi) + _mm(m, mid) + _mm(m, lo)


def _sel_tn(x, m):
    hi, mid, lo = _split3(x)
    return _mm_tn(hi, m) + _mm_tn(mid, m) + _mm_tn(lo, m)


def _sigmoid(x):
    return 1.0 / (1.0 + jnp.exp(-x))


def _silu(x):
    return x * _sigmoid(x)


def _dsilu(x):
    s = _sigmoid(x)
    return s * (1.0 + x * (1.0 - s))


def _softplus(x):
    return jnp.maximum(x, 0.0) + jnp.log(1.0 + jnp.exp(-jnp.abs(x)))


def _expm1(z):
    series = z * (1.0 + z * (1.0 / 2) * (1.0 + z * (1.0 / 3) * (1.0 + z * (1.0 / 4) * (
        1.0 + z * (1.0 / 5) * (1.0 + z * (1.0 / 6) * (1.0 + z * (1.0 / 7)))))))
    return jnp.where(jnp.abs(z) < 0.3, series, jnp.exp(z) - 1.0)


def _iota(shape, dim):
    return lax.broadcasted_iota(jnp.int32, shape, dim)


def _last_row(x, rows):
    return jnp.sum(jnp.where(rows == x.shape[0] - 1, x, 0.0), axis=0, keepdims=True)


def _shift_down(x, d, rows, fill=0.0):
    return jnp.where(rows >= d, pltpu.roll(x, d, 0), fill)


def _shift_up(x, d, rows, fill=0.0):
    n = x.shape[0]
    return jnp.where(rows < n - d, pltpu.roll(x, n - d, 0), fill)


def _conv_fwd(x, cw_ref, cb_ref, rows):
    out = cb_ref[...] + cw_ref[pl.ds(3, 1), :] * x
    for k in range(3):
        out = out + cw_ref[pl.ds(k, 1), :] * _shift_down(x, 3 - k, rows)
    return out


def _conv_bwd(x, dco, cw_ref, rows):
    dx = cw_ref[pl.ds(3, 1), :] * dco
    dws = []
    for k in range(3):
        dx = dx + cw_ref[pl.ds(k, 1), :] * _shift_up(dco, 3 - k, rows)
        dws.append(jnp.sum(dco * _shift_down(x, 3 - k, rows), axis=0, keepdims=True))
    dws.append(jnp.sum(dco * x, axis=0, keepdims=True))
    return dx, dws, jnp.sum(dco, axis=0, keepdims=True)


def _vec(n):
    return pl.BlockSpec((1, n), lambda *_: (0, 0))


def _full(shape):
    nd = len(shape)
    return pl.BlockSpec(shape, lambda *_: (0,) * nd)


def _inproj_fwd(x, nw, scale, shift, w):
    S = x.shape[0]
    tm, tn = min(512, S), 640

    def body(x_ref, nw_ref, sc_ref, sh_ref, w_ref, u_ref, h_ref):
        @pl.when(pl.program_id(1) == 0)
        def _():
            xv = x_ref[...]
            inv = lax.rsqrt(jnp.mean(xv * xv, axis=-1, keepdims=True) + EPS)
            h = (xv * inv) * nw_ref[...] * (1.0 + sc_ref[...]) + sh_ref[...]
            h_ref[...] = h.astype(BF16)

        u_ref[...] = _mm(h_ref[...], w_ref[...])

    return pl.pallas_call(
        body, name="inproj_fwd", grid=(S // tm, N_PAD // tn),
        in_specs=[pl.BlockSpec((tm, D_MODEL), lambda i, j: (i, 0)), _vec(D_MODEL), _vec(D_MODEL), _vec(D_MODEL),
                  pl.BlockSpec((D_MODEL, tn), lambda i, j: (0, j))],
        out_specs=[pl.BlockSpec((tm, tn), lambda i, j: (i, j)), pl.BlockSpec((tm, D_MODEL), lambda i, j: (i, 0))],
        out_shape=[SDS((S, N_PAD), F32), SDS((S, D_MODEL), BF16)],
        compiler_params=_cp(("parallel", "arbitrary")),
    )(x, nw, scale, shift, w)


def _inproj_bwd_x(du, w, x, nw, scale, dxn):
    S = x.shape[0]
    tm, tk = min(512, S), 640
    nk = N_PAD // tk

    def body(du_ref, w_ref, x_ref, nw_ref, sc_ref, dxn_ref, dx_ref, red_ref, acc):
        i, k = pl.program_id(0), pl.program_id(1)

        @pl.when(k == 0)
        def _():
            acc[...] = jnp.zeros_like(acc)

        @pl.when((i == 0) & (k == 0))
        def _():
            red_ref[...] = jnp.zeros_like(red_ref)

        acc[...] += _mm_nt(_bf(du_ref[...]), w_ref[...])

        @pl.when(k == nk - 1)
        def _():
            dh = acc[...]
            xv = x_ref[...]
            inv = lax.rsqrt(jnp.mean(xv * xv, axis=-1, keepdims=True) + EPS)
            xhat = xv * inv
            nwv = nw_ref[...]
            g1 = 1.0 + sc_ref[...]
            dxhat = dh * nwv * g1
            dx = inv * (dxhat - xhat * jnp.mean(dxhat * xhat, axis=-1, keepdims=True))
            dx_ref[...] = dxn_ref[...] + dx
            red_ref[0:1, :] += jnp.sum(dh, axis=0, keepdims=True)
            red_ref[1:2, :] += jnp.sum(dh * xhat * nwv, axis=0, keepdims=True)
            red_ref[2:3, :] += jnp.sum(dh * xhat * g1, axis=0, keepdims=True)

    row = pl.BlockSpec((tm, D_MODEL), lambda i, k: (i, 0))
    return pl.pallas_call(
        body, name="inproj_bwd_x", grid=(S // tm, nk),
        in_specs=[pl.BlockSpec((tm, tk), lambda i, k: (i, k)), pl.BlockSpec((D_MODEL, tk), lambda i, k: (0, k)),
                  row, _vec(D_MODEL), _vec(D_MODEL), row],
        out_specs=[row, pl.BlockSpec((8, D_MODEL), lambda i, k: (0, 0))],
        out_shape=[SDS((S, D_MODEL), F32), SDS((8, D_MODEL), F32)],
        scratch_shapes=[pltpu.VMEM((tm, D_MODEL), F32)],
        compiler_params=_cp(("arbitrary", "arbitrary")),
    )(du, w, x, nw, scale, dxn)


def _inproj_bwd_w(h, du):
    S = h.shape[0]
    tn = 640

    def body(h_ref, du_ref, gw_ref):
        gw_ref[...] = _mm_tn(h_ref[...], _bf(du_ref[...]))

    return pl.pallas_call(
        body, name="inproj_bwd_w", grid=(N_PAD // tn,),
        in_specs=[_full((S, D_MODEL)), pl.BlockSpec((S, tn), lambda j: (0, j))],
        out_specs=pl.BlockSpec((D_MODEL, tn), lambda j: (0, j)),
        out_shape=SDS((D_MODEL, N_PAD), F32),
        compiler_params=_cp(("parallel",)),
    )(h, du)


def _scan_block(a, b, rows):
    d = 1
    while d < a.shape[0]:
        a_s = _shift_down(a, d, rows, 1.0)
        b_s = _shift_down(b, d, rows, 0.0)
        b = a * b_s + b
        a = a * a_s
        d *= 2
    return a, b


def _rscan_block(c, g, rows):
    d = 1
    while d < c.shape[0]:
        c_s = _shift_up(c, d, rows, 1.0)
        g_s = _shift_up(g, d, rows, 0.0)
        g = g + c * g_s
        c = c * c_s
        d *= 2
    return c, g


def _lru_gates(xa, wa_ref, ba_ref, wx_ref, bx_ref, lam_ref):
    sp = _softplus(-lam_ref[...])
    xb = _bf(xa)
    r = _sigmoid(_mm(xb, wa_ref[...]) + ba_ref[...])
    ig = _sigmoid(_mm(xb, wx_ref[...]) + bx_ref[...])
    la = -LRU_C * r * sp
    a = jnp.exp(la)
    mult = jnp.sqrt(-_expm1(2.0 * la))
    return sp, r, ig, la, a, mult


def _lru_specs(S):
    t128 = pl.BlockSpec((1, LANE), lambda t: (0, t))
    return [pl.BlockSpec((S, 2 * LANE), lambda t: (0, OFF_LRU // (2 * LANE) + t)),
            pl.BlockSpec((4, LANE), lambda t: (0, t)), t128,
            pl.BlockSpec((None, LANE, LANE), lambda t: (t, 0, 0)), t128,
            pl.BlockSpec((None, LANE, LANE), lambda t: (t, 0, 0)), t128, t128]


def _lru_fwd(u, cw, cb, wa, ba, wx, bx, lam, ycat):
    S = u.shape[0]
    tb = min(256, S)

    def body(u_ref, cw_ref, cb_ref, wa_ref, ba_ref, wx_ref, bx_ref, lam_ref, ycat_in, ycat_ref, h_ref, a_scr, b_scr):
        del ycat_in
        rows = _iota((S, LANE), 0)
        xa = _conv_fwd(u_ref[:, 0:LANE], cw_ref, cb_ref, rows)
        _, _, ig, _, a, mult = _lru_gates(xa, wa_ref, ba_ref, wx_ref, bx_ref, lam_ref)
        a_scr[...] = a
        b_scr[...] = mult * (ig * xa)
        rows_b = _iota((tb, LANE), 0)

        def blk(j, hprev):
            sl = pl.ds(pl.multiple_of(j * tb, tb), tb)
            acum, hloc = _scan_block(a_scr[sl, :], b_scr[sl, :], rows_b)
            hf = hloc + acum * hprev
            h_ref[sl, :] = hf
            return _last_row(hf, rows_b)

        lax.fori_loop(0, S // tb, blk, jnp.zeros((1, LANE), F32))
        ycat_ref[...] = h_ref[...] * _silu(u_ref[:, LANE:2 * LANE])

    col = pl.BlockSpec((S, LANE), lambda t: (0, t))
    return pl.pallas_call(
        body, name="lru_fwd", grid=(LRU_W // LANE,),
        in_specs=_lru_specs(S) + [pl.BlockSpec(memory_space=pl.ANY)],
        out_specs=[col, col],
        out_shape=[SDS((S, D_INNER), F32), SDS((S, LRU_W), F32)],
        scratch_shapes=[pltpu.VMEM((S, LANE), F32), pltpu.VMEM((S, LANE), F32)],
        input_output_aliases={8: 0},
        compiler_params=_cp(("parallel",)),
    )(u, cw, cb, wa, ba, wx, bx, lam, ycat)


def _lru_bwd(u, cw, cb, wa, ba, wx, bx, lam, h_lru, dycat, du):
    S = u.shape[0]
    tb = min(256, S)

    def body(u_ref, cw_ref, cb_ref, wa_ref, ba_ref, wx_ref, bx_ref, lam_ref, h_ref, dy_ref, du_in,
             du_ref, red_ref, gwa_ref, gwx_ref, c_scr, g_scr, l_scr):
        del du_in
        rows = _iota((S, LANE), 0)
        ax = u_ref[:, 0:LANE]
        ag = u_ref[:, LANE:2 * LANE]
        xa = _conv_fwd(ax, cw_ref, cb_ref, rows)
        sp, r, ig, la, a, mult = _lru_gates(xa, wa_ref, ba_ref, wx_ref, bx_ref, lam_ref)
        h = h_ref[...]
        dy = dy_ref[...]
        du_ref[:, LANE:2 * LANE] = dy * h * _dsilu(ag)
        c_scr[...] = _shift_up(a, 1, rows, 0.0)
        g_scr[...] = dy * _silu(ag)
        rows_b = _iota((tb, LANE), 0)
        nb = S // tb

        def blk(jj, lnext):
            j = nb - 1 - jj
            sl = pl.ds(pl.multiple_of(j * tb, tb), tb)
            ccum, lloc = _rscan_block(c_scr[sl, :], g_scr[sl, :], rows_b)
            lam_t = lloc + ccum * lnext
            l_scr[sl, :] = lam_t
            return jnp.sum(jnp.where(rows_b == 0, lam_t, 0.0), axis=0, keepdims=True)

        lax.fori_loop(0, nb, blk, jnp.zeros((1, LANE), F32))
        db = l_scr[...]
        da = db * _shift_down(h, 1, rows)
        dmult = db * ig * xa
        dig = db * mult * xa
        dxa = db * mult * ig
        dla = da * a - dmult * (a * a) / mult
        dr = -LRU_C * sp * dla
        dsp = jnp.sum(-LRU_C * r * dla, axis=0, keepdims=True)
        dlam = -dsp * _sigmoid(-lam_ref[...])
        dzr = dr * r * (1.0 - r)
        dzi = dig * ig * (1.0 - ig)
        dzr_b, dzi_b, xa_b = _bf(dzr), _bf(dzi), _bf(xa)
        dxa = dxa + _mm_nt(dzr_b, wa_ref[...]) + _mm_nt(dzi_b, wx_ref[...])
        gwa_ref[...] = _mm_tn(xa_b, dzr_b)
        gwx_ref[...] = _mm_tn(xa_b, dzi_b)
        dax, dws, dcb = _conv_bwd(ax, dxa, cw_ref, rows)
        du_ref[:, 0:LANE] = dax
        parts = dws + [dcb, jnp.sum(dzr, axis=0, keepdims=True), jnp.sum(dzi, axis=0, keepdims=True), dlam]
        for n, p in enumerate(parts):
            red_ref[pl.ds(n, 1), :] = p

    col = pl.BlockSpec((S, LANE), lambda t: (0, t))
    gw = pl.BlockSpec((None, LANE, LANE), lambda t: (t, 0, 0))
    return pl.pallas_call(
        body, name="lru_bwd", grid=(LRU_W // LANE,),
        in_specs=_lru_specs(S) + [col, col, pl.BlockSpec(memory_space=pl.ANY)],
        out_specs=[pl.BlockSpec((S, 2 * LANE), lambda t: (0, OFF_LRU // (2 * LANE) + t)),
                   pl.BlockSpec((8, LANE), lambda t: (0, t)), gw, gw],
        out_shape=[SDS((S, N_PAD), F32), SDS((8, LRU_W), F32), SDS((4, LANE, LANE), F32), SDS((4, LANE, LANE), F32)],
        scratch_shapes=[pltpu.VMEM((S, LANE), F32)] * 3,
        input_output_aliases={10: 0},
        compiler_params=_cp(("parallel",)),
    )(u, cw, cb, wa, ba, wx, bx, lam, h_lru, dycat, du)


HG_LEVELS = 6


def _hg_consts():
    C = HG_CHUNK
    t = np.arange(C)[:, None]
    r = np.arange(C)[None, :]
    mats = []
    for side in ("q", "k"):
        for l in range(HG_LEVELS):
            b = 1 << l
            upper = (t % (2 * b)) >= b
            anchor = (t // (2 * b)) * 2 * b + b - 1
            if side == "q":
                mats.append(upper & (r > anchor) & (r <= t))
            else:
                mats.append((~upper) & (r > t) & (r <= anchor))
    mats.append(r <= t)
    mats.append(r > t)
    return np.concatenate(mats, 0).astype(np.float32)


def _hg_factors(hf, lb, mall):
    s = _sigmoid(hf)
    f = lb + (1.0 - lb) * s
    lf = jnp.log(f)
    k = (1.0 - lb) * _sigmoid(-hf)
    e = jnp.exp(_sel_l(mall, lf))
    C = HG_CHUNK
    eq = [e[l * C:(l + 1) * C] for l in range(HG_LEVELS)]
    ek = [e[(HG_LEVELS + l) * C:(HG_LEVELS + l + 1) * C] for l in range(HG_LEVELS)]
    ecum = e[2 * HG_LEVELS * C:(2 * HG_LEVELS + 1) * C]
    erem = e[(2 * HG_LEVELS + 1) * C:(2 * HG_LEVELS + 2) * C]
    return s, f, k, eq, ek, ecum, erem


def _hg_masks():
    C = HG_CHUNK
    ri, ci = _iota((C, C), 0), _iota((C, C), 1)
    rr = _iota((C, LANE), 0)
    gm = [(lax.shift_right_logical(ri, l + 1) == lax.shift_right_logical(ci, l + 1)).astype(F32)
          for l in range(HG_LEVELS)]
    up = [(lax.shift_right_logical(rr, l) & 1) == 1 for l in range(HG_LEVELS)]
    eye = (ri == ci).astype(F32)
    return gm, up, eye, rr


def _hg_scores(qh, kh, eq, ek, sl, gm, up, eye):
    qs, ks = [], []
    p = _mm_nt(_bf(qh), _bf(kh)) * eye
    for l in range(HG_LEVELS):
        ql = jnp.where(up[l], qh * eq[l][:, sl], 0.0)
        kl = jnp.where(up[l], 0.0, kh * ek[l][:, sl])
        p = p + _mm_nt(_bf(ql), _bf(kl)) * gm[l]
        qs.append(ql)
        ks.append(kl)
    return p, qs, ks


def _hg_fwd(u, lb, nw, mall, ycat):
    S = u.shape[0]
    C = HG_CHUNK
    n = S // C

    def body(u_ref, lb_ref, nw_ref, mall_ref, ycat_in, ycat_ref, o_ref, st_ref, st):
        del ycat_in

        @pl.when(pl.program_id(0) == 0)
        def _():
            st[...] = jnp.zeros_like(st)

        q = _silu(u_ref[:, 0:512])
        v = u_ref[:, 1024:1536]
        _, _, k, eq, ek, ecum, erem = _hg_factors(u_ref[:, 512:1024], lb_ref[...], mall_ref[...])
        gm, up, eye, rr = _hg_masks()
        for h in range(HG_HEADS):
            sl = slice(h * LANE, (h + 1) * LANE)
            qh, kh, vh = q[:, sl], k[:, sl], _bf(v[:, sl])
            p, _, _ = _hg_scores(qh, kh, eq, ek, sl, gm, up, eye)
            sth = st[h]
            st_ref[h] = sth
            o_ref[:, sl] = _mm(_bf(p), vh) + _mm_nt(_bf(qh * ecum[:, sl]), _bf(sth))
            st[h] = sth * _last_row(ecum[:, sl], rr) + _mm_tn(vh, _bf(kh * erem[:, sl]))
        o = o_ref[...]
        inv = lax.rsqrt(jnp.mean(o * o, axis=-1, keepdims=True) + EPS)
        ycat_ref[...] = (o * inv) * nw_ref[...] * _silu(u_ref[:, 1536:2048])

    return pl.pallas_call(
        body, name="hg_fwd", grid=(n,),
        in_specs=[pl.BlockSpec((C, 2048), lambda i: (i, 0)), _vec(HG_W), _vec(HG_W), _full(mall.shape),
                  pl.BlockSpec(memory_space=pl.ANY)],
        out_specs=[pl.BlockSpec((C, HG_W), lambda i: (i, 1)), pl.BlockSpec((C, HG_W), lambda i: (i, 0)),
                   pl.BlockSpec((None, HG_HEADS, LANE, LANE), lambda i: (i, 0, 0, 0))],
        out_shape=[SDS((S, D_INNER), F32), SDS((S, HG_W), F32), SDS((n, HG_HEADS, LANE, LANE), F32)],
        scratch_shapes=[pltpu.VMEM((HG_HEADS, LANE, LANE), F32)],
        input_output_aliases={4: 0},
        compiler_params=_cp(("arbitrary",)),
    )(u, lb, nw, mall, ycat)


def _hg_bwd(u, lb, nw, mall, mall_t, o_b, states, dycat, du):
    S = u.shape[0]
    C = HG_CHUNK
    n = S // C
    L2 = 2 * HG_LEVELS

    def body(u_ref, lb_ref, nw_ref, mall_ref, mallt_ref, o_ref, st_ref, dy_ref, du_in, du_ref, red_ref,
             dst, dlast_s, dq_s, dk_s, dex):
        del du_in

        @pl.when(pl.program_id(0) == 0)
        def _():
            dst[...] = jnp.zeros_like(dst)
            red_ref[...] = jnp.zeros_like(red_ref)

        lb = lb_ref[...]
        hq, hf, hg = u_ref[:, 0:512], u_ref[:, 512:1024], u_ref[:, 1536:2048]
        q = _silu(hq)
        v = u_ref[:, 1024:1536]
        s, f, k, eq, ek, ecum, erem = _hg_factors(hf, lb, mall_ref[...])
        gm, up, eye, rr = _hg_masks()
        o = o_ref[...]
        dy = dy_ref[...]
        inv = lax.rsqrt(jnp.mean(o * o, axis=-1, keepdims=True) + EPS)
        ohat = o * inv
        nwv = nw_ref[...]
        du_ref[:, 1536:2048] = dy * ohat * nwv * _dsilu(hg)
        dn = dy * _silu(hg)
        red_ref[0:1, :] += jnp.sum(dn * ohat, axis=0, keepdims=True)
        dohat = dn * nwv
        do = inv * (dohat - ohat * jnp.mean(dohat * ohat, axis=-1, keepdims=True))
        for h in range(HG_HEADS):
            sl = slice(h * LANE, (h + 1) * LANE)
            qh, kh, vh, doh = q[:, sl], k[:, sl], _bf(v[:, sl]), _bf(do[:, sl])
            p, qs, ks = _hg_scores(qh, kh, eq, ek, sl, gm, up, eye)
            st_f = st_ref[h]
            sth = _bf(st_f)
            dsth = dst[h]
            dsth_b = _bf(dsth)
            qt = qh * ecum[:, sl]
            kt = kh * erem[:, sl]
            elast = _last_row(ecum[:, sl], rr)
            dp = _mm_nt(doh, vh)
            du_ref[:, 1024 + h * LANE:1024 + (h + 1) * LANE] = _mm_tn(_bf(p), doh) + _mm_nt(_bf(kt), dsth_b)
            dpe = _bf(dp * eye)
            dqt = _mm(doh, sth)
            dkt = _mm(vh, dsth_b)
            dq = dqt * ecum[:, sl] + _mm(dpe, _bf(kh))
            dk = dkt * erem[:, sl] + _mm_tn(dpe, _bf(qh))
            dex[L2 * C:(L2 + 1) * C, sl] = dqt * qt
            dex[(L2 + 1) * C:(L2 + 2) * C, sl] = dkt * kt
            for l in range(HG_LEVELS):
                dpl = _bf(dp * gm[l])
                dql = _mm(dpl, _bf(ks[l]))
                dkl = _mm_tn(dpl, _bf(qs[l]))
                dq = dq + jnp.where(up[l], dql * eq[l][:, sl], 0.0)
                dk = dk + jnp.where(up[l], 0.0, dkl * ek[l][:, sl])
                dex[l * C:(l + 1) * C, sl] = dql * qs[l]
                dex[(HG_LEVELS + l) * C:(HG_LEVELS + l + 1) * C, sl] = dkl * ks[l]
            dlast_s[:, sl] = jnp.sum(dsth * st_f, axis=0, keepdims=True) * elast
            dst[h] = dsth * elast + _mm_tn(doh, _bf(qt))
            dq_s[:, sl] = dq
            dk_s[:, sl] = dk
        dq = dq_s[...]
        dk = dk_s[...]
        dlf = _sel_l(mallt_ref[...], dex[...]) + dlast_s[...]
        du_ref[:, 0:512] = dq * _dsilu(hq)
        t = (1.0 - s) * (dlf / f - dk)
        du_ref[:, 512:1024] = (1.0 - lb) * s * t
        red_ref[1:2, :] += jnp.sum(t, axis=0, keepdims=True)

    rev = lambda i: (n - 1 - i, 0)
    return pl.pallas_call(
        body, name="hg_bwd", grid=(n,),
        in_specs=[pl.BlockSpec((C, 2048), rev), _vec(HG_W), _vec(HG_W), _full(mall.shape), _full(mall_t.shape),
                  pl.BlockSpec((C, HG_W), rev),
                  pl.BlockSpec((None, HG_HEADS, LANE, LANE), lambda i: (n - 1 - i, 0, 0, 0)),
                  pl.BlockSpec((C, HG_W), lambda i: (n - 1 - i, 1)), pl.BlockSpec(memory_space=pl.ANY)],
        out_specs=[pl.BlockSpec((C, 2048), rev), pl.BlockSpec((8, HG_W), lambda i: (0, 0))],
        out_shape=[SDS((S, N_PAD), F32), SDS((8, HG_W), F32)],
        scratch_shapes=[pltpu.VMEM((HG_HEADS, LANE, LANE), F32), pltpu.VMEM((1, HG_W), F32),
                        pltpu.VMEM((C, HG_W), F32), pltpu.VMEM((C, HG_W), F32), pltpu.VMEM(((L2 + 2) * C, HG_W), F32)],
        input_output_aliases={8: 0},
        compiler_params=_cp(("arbitrary",)),
    )(u, lb, nw, mall, mall_t, o_b, states, dycat, du)


def _ssdconv_fwd(u, cw, cb):
    S = u.shape[0]

    def body(u_ref, cw_ref, cb_ref, out_ref):
        rows = _iota((S, LANE), 0)
        out_ref[...] = _silu(_conv_fwd(u_ref[...], cw_ref, cb_ref, rows))

    return pl.pallas_call(
        body, name="ssdconv_fwd", grid=(SSD_CONV // LANE,),
        in_specs=[pl.BlockSpec((S, LANE), lambda t: (0, OFF_XBC // LANE + t)), pl.BlockSpec((4, LANE), lambda t: (0, t)),
                  pl.BlockSpec((1, LANE), lambda t: (0, t))],
        out_specs=pl.BlockSpec((S, LANE), lambda t: (0, t)),
        out_shape=SDS((S, SSD_CONV), F32),
        compiler_params=_cp(("parallel",)),
    )(u, cw, cb)


def _ssdconv_bwd(u, cw, cb, dxbc, du):
    S = u.shape[0]

    def body(u_ref, cw_ref, cb_ref, d_ref, du_in, du_ref, red_ref):
        del du_in
        rows = _iota((S, LANE), 0)
        x = u_ref[...]
        dco = d_ref[...] * _dsilu(_conv_fwd(x, cw_ref, cb_ref, rows))
        dx, dws, dcb = _conv_bwd(x, dco, cw_ref, rows)
        du_ref[...] = dx
        for n, p in enumerate(dws + [dcb]):
            red_ref[pl.ds(n, 1), :] = p
        red_ref[pl.ds(5, 3), :] = jnp.zeros((3, LANE), F32)

    ucol = pl.BlockSpec((S, LANE), lambda t: (0, OFF_XBC // LANE + t))
    return pl.pallas_call(
        body, name="ssdconv_bwd", grid=(SSD_CONV // LANE,),
        in_specs=[ucol, pl.BlockSpec((4, LANE), lambda t: (0, t)), pl.BlockSpec((1, LANE), lambda t: (0, t)),
                  pl.BlockSpec((S, LANE), lambda t: (0, t)), pl.BlockSpec(memory_space=pl.ANY)],
        out_specs=[ucol, pl.BlockSpec((8, LANE), lambda t: (0, t))],
        out_shape=[SDS((S, N_PAD), F32), SDS((8, SSD_CONV), F32)],
        input_output_aliases={4: 0},
        compiler_params=_cp(("parallel",)),
    )(u, cw, cb, dxbc, du)


def _ssd_consts():
    e64 = np.zeros((LANE, SSD_W), np.float32)
    e128 = np.zeros((LANE, SSD_HEADS * LANE), np.float32)
    for h in range(SSD_HEADS):
        e64[h, h * SSD_P:(h + 1) * SSD_P] = 1.0
        e128[h, h * LANE:(h + 1) * LANE] = 1.0
    T = SSD_CHUNK
    tril = (np.arange(T)[None, :] <= np.arange(T)[:, None]).astype(np.float32)
    return e64, e128, tril, tril.T.copy()


def _ssd_common(zdt, bias_ref, alog_ref, tril, e64, e128):
    T = SSD_CHUNK
    lane = _iota((1, LANE), 1)
    a_neg = jnp.where(lane < SSD_HEADS, -jnp.exp(alog_ref[...]), 0.0)
    dtpre = zdt[:, SSD_W:SSD_W + LANE] + bias_ref[...]
    dt = _softplus(dtpre)
    cum = _sel_l(tril, dt * a_neg)
    rowsT = _iota((T, LANE), 0)
    last = _last_row(cum, rowsT)
    ecum_x = _sel_r(jnp.exp(cum), e64)
    erem_x = _sel_r(jnp.exp(last - cum), e64)
    elast_x = _last_row(ecum_x, _iota((T, SSD_W), 0))
    dt_x = _sel_r(dt, e64)
    cum_e = _sel_r(cum, e128)
    return a_neg, dtpre, dt, cum, ecum_x, erem_x, elast_x, dt_x, cum_e


def _ssd_decay(cum_e, cumt_ref, h, causal):
    diff = cum_e[:, h * LANE:(h + 1) * LANE] - cumt_ref[pl.ds(h, 1), :]
    return jnp.exp(jnp.where(causal, diff, NEG))


def _group_norm_fwd(y1, nwv):
    outs, invs = [], []
    for g in range(2):
        seg = y1[:, g * 512:(g + 1) * 512]
        inv = lax.rsqrt(jnp.mean(seg * seg, axis=-1, keepdims=True) + EPS)
        outs.append(seg * inv * nwv[:, g * 512:(g + 1) * 512])
        invs.append(inv)
    return outs, invs


def _ssd_fwd(u, xbc, bias, alog, dskip_x, nw, consts, ycat):
    S = u.shape[0]
    T = SSD_CHUNK
    n = S // T
    e64, e128, tril, _ = consts

    def body(u_ref, xbc_ref, bias_ref, alog_ref, dx_ref, nw_ref, e64_ref, e128_ref, tril_ref, ycat_in,
             ycat_ref, y_ref, st_ref, st, cumt):
        del ycat_in

        @pl.when(pl.program_id(0) == 0)
        def _():
            st[...] = jnp.zeros_like(st)

        zdt = u_ref[...]
        z = zdt[:, 0:SSD_W]
        xs = xbc_ref[:, 0:SSD_W]
        _, _, _, cum, ecum_x, erem_x, elast_x, dt_x, cum_e = _ssd_common(
            zdt, bias_ref, alog_ref, tril_ref[...], e64_ref[...], e128_ref[...])
        cumt[...] = cum.T
        causal = _iota((T, T), 0) >= _iota((T, T), 1)
        lo = _iota((T, LANE), 1) < SSD_P
        xdt = xs * dt_x
        xrem = xdt * erem_x
        st_ref[...] = st[...]
        for g in range(2):
            gs = slice(g * 512, (g + 1) * 512)
            bg = _bf(xbc_ref[:, SSD_W + g * LANE:SSD_W + (g + 1) * LANE])
            cg = _bf(xbc_ref[:, SSD_W + 256 + g * LANE:SSD_W + 256 + (g + 1) * LANE])
            cb = _mm_nt(cg, bg)
            yin = _mm(cg, _bf(st[:, gs])) * ecum_x[:, gs]
            for j in range(4):
                h0 = 8 * g + 2 * j
                cs = slice(h0 * SSD_P, (h0 + 2) * SSD_P)
                xp = xdt[:, cs]
                s0 = _bf(cb * _ssd_decay(cum_e, cumt, h0, causal))
                s1 = _bf(cb * _ssd_decay(cum_e, cumt, h0 + 1, causal))
                y_ref[:, cs] = (_mm(s0, _bf(jnp.where(lo, xp, 0.0))) + _mm(s1, _bf(jnp.where(lo, 0.0, xp)))
                                + yin[:, j * LANE:(j + 1) * LANE])
            st[:, gs] = st[:, gs] * elast_x[:, gs] + _mm_tn(bg, _bf(xrem[:, gs]))
        y1 = (y_ref[...] + dx_ref[...] * xs) * _silu(z)
        outs, _ = _group_norm_fwd(y1, nw_ref[...])
        for g in range(2):
            ycat_ref[:, g * 512:(g + 1) * 512] = outs[g]

    return pl.pallas_call(
        body, name="ssd_fwd", grid=(n,),
        in_specs=[pl.BlockSpec((T, SSD_W + LANE), lambda i: (i, OFF_Z // (SSD_W + LANE))),
                  pl.BlockSpec((T, SSD_CONV), lambda i: (i, 0)), _vec(LANE), _vec(LANE), _vec(SSD_W), _vec(SSD_W),
                  _full(e64.shape), _full(e128.shape), _full(tril.shape), pl.BlockSpec(memory_space=pl.ANY)],
        out_specs=[pl.BlockSpec((T, SSD_W), lambda i: (i, 1)), pl.BlockSpec((T, SSD_W), lambda i: (i, 0)),
                   pl.BlockSpec((None, SSD_N, SSD_W), lambda i: (i, 0, 0))],
        out_shape=[SDS((S, D_INNER), F32), SDS((S, SSD_W), F32), SDS((n, SSD_N, SSD_W), F32)],
        scratch_shapes=[pltpu.VMEM((SSD_N, SSD_W), F32), pltpu.VMEM((LANE, T), F32)],
        input_output_aliases={9: 0},
        compiler_params=_cp(("arbitrary",)),
    )(u, xbc, bias, alog, dskip_x, nw, _bfc(e64), _bfc(e128), _bfc(tril), ycat)


def _ssd_bwd(u, xbc, bias, alog, dskip_x, nw, consts, y_ssd, states, dycat, du):
    S = u.shape[0]
    T = SSD_CHUNK
    n = S // T
    e64, e128, tril, triu = consts
    e64t = np.ascontiguousarray(e64.T)

    def body(u_ref, xbc_ref, bias_ref, alog_ref, dx_ref, nw_ref, e64_ref, e64t_ref, e128_ref, tril_ref, triu_ref,
             y_ref, st_ref, dy_ref, du_in, du_ref, dxbc_ref, red_ref, dst, dl_s, cumt, dxdt_s, dy0_s, gb_s, gc_s):
        del du_in

        @pl.when(pl.program_id(0) == 0)
        def _():
            dst[...] = jnp.zeros_like(dst)
            red_ref[...] = jnp.zeros_like(red_ref)

        zdt = u_ref[...]
        z = zdt[:, 0:SSD_W]
        xs = xbc_ref[:, 0:SSD_W]
        e64m = e64_ref[...]
        a_neg, dtpre, dt, cum, ecum_x, erem_x, elast_x, dt_x, cum_e = _ssd_common(
            zdt, bias_ref, alog_ref, tril_ref[...], e64m, e128_ref[...])
        cumt[...] = cum.T
        causal = _iota((T, T), 0) >= _iota((T, T), 1)
        lo = _iota((T, LANE), 1) < SSD_P
        xdt = xs * dt_x
        xrem = xdt * erem_x
        y = y_ref[...]
        dxv = dx_ref[...]
        nwv = nw_ref[...]
        sz = _silu(z)
        y0 = y + dxv * xs
        y1 = y0 * sz
        for g in range(2):
            gs = slice(g * 512, (g + 1) * 512)
            seg = y1[:, gs]
            inv = lax.rsqrt(jnp.mean(seg * seg, axis=-1, keepdims=True) + EPS)
            shat = seg * inv
            dyg = dy_ref[:, gs]
            red_ref[0:1, gs] += jnp.sum(dyg * shat, axis=0, keepdims=True)
            dsh = dyg * nwv[:, gs]
            dy1g = inv * (dsh - shat * jnp.mean(dsh * shat, axis=-1, keepdims=True))
            du_ref[:, gs] = dy1g * y0[:, gs] * _dsilu(z[:, gs])
            dy0_s[:, gs] = dy1g * sz[:, gs]
        dy0 = dy0_s[...]
        red_ref[1:2, :] += jnp.sum(dy0 * xs, axis=0, keepdims=True)
        dyin = dy0 * ecum_x
        lane = _iota((T, LANE), 1)
        ones = jnp.ones((T, LANE), BF16)
        dcum = jnp.zeros((T, LANE), F32)

        def row_minus_col(gm):
            hi = _bf(gm)
            lw = _bf(gm - hi.astype(F32))
            return _mm(hi, ones) + _mm(lw, ones) - _mm_tn(hi, ones) - _mm_tn(lw, ones)

        for g in range(2):
            gs = slice(g * 512, (g + 1) * 512)
            bg = _bf(xbc_ref[:, SSD_W + g * LANE:SSD_W + (g + 1) * LANE])
            cg = _bf(xbc_ref[:, SSD_W + 256 + g * LANE:SSD_W + 256 + (g + 1) * LANE])
            cb = _mm_nt(cg, bg)
            dst_f, st_f = dst[:, gs], st_ref[:, gs]
            dstg = _bf(dst_f)
            stg = _bf(st_f)
            dyin_g = _bf(dyin[:, gs])
            xrem_g = _bf(xrem[:, gs])
            dcb = jnp.zeros((T, T), F32)
            dxr = _mm(bg, dstg)
            dxdt_s[:, gs] = dxr * erem_x[:, gs]
            gc_s[:, gs] = dxr * xrem[:, gs]
            gb_s[:, gs] = dyin[:, gs] * _mm(cg, stg)
            dl_s[:, gs] = jnp.sum(dst_f * st_f, axis=0, keepdims=True) * elast_x[:, gs]
            for j in range(4):
                h0 = 8 * g + 2 * j
                cs = slice(h0 * SSD_P, (h0 + 2) * SSD_P)
                xp = xdt[:, cs]
                dyp = dy0[:, cs]
                x_lo, x_hi = _bf(jnp.where(lo, xp, 0.0)), _bf(jnp.where(lo, 0.0, xp))
                d_lo, d_hi = _bf(jnp.where(lo, dyp, 0.0)), _bf(jnp.where(lo, 0.0, dyp))
                s0 = cb * _ssd_decay(cum_e, cumt, h0, causal)
                s1 = cb * _ssd_decay(cum_e, cumt, h0 + 1, causal)
                ds0 = _mm_nt(d_lo, x_lo)
                ds1 = _mm_nt(d_hi, x_hi)
                dcb = dcb + ds0 * _ssd_decay(cum_e, cumt, h0, causal) + ds1 * _ssd_decay(cum_e, cumt, h0 + 1, causal)
                dxdt_s[:, cs] += _mm_tn(_bf(s0), d_lo) + _mm_tn(_bf(s1), d_hi)
                dcum = dcum + jnp.where(lane == h0, row_minus_col(ds0 * s0), 0.0)
                dcum = dcum + jnp.where(lane == h0 + 1, row_minus_col(ds1 * s1), 0.0)
            dcb_b = _bf(dcb)
            dxbc_ref[:, SSD_W + g * LANE:SSD_W + (g + 1) * LANE] = _mm_tn(dcb_b, cg) + _mm_nt(xrem_g, dstg)
            dxbc_ref[:, SSD_W + 256 + g * LANE:SSD_W + 256 + (g + 1) * LANE] = _mm(dcb_b, bg) + _mm_nt(dyin_g, stg)
            dst[:, gs] = dst_f * elast_x[:, gs] + _mm_tn(cg, dyin_g)
        dxdt = dxdt_s[...]
        dxbc_ref[:, 0:SSD_W] = dxdt * dt_x + dy0 * dxv
        e64t = e64t_ref[...]
        hc = _sel_r(gc_s[...], e64t)
        dlast = (jnp.sum(hc, axis=0, keepdims=True)
                 + jnp.max(_sel_r(jnp.broadcast_to(dl_s[...], (8, SSD_W)), e64t), axis=0, keepdims=True))
        dcum = dcum + _sel_r(gb_s[...], e64t) - hc + jnp.where(_iota((T, LANE), 0) == T - 1, dlast, 0.0)
        dda = _sel_l(triu_ref[...], dcum)
        ddt = dda * a_neg + _sel_r(dxdt * xs, e64t)
        ddtpre = ddt * _sigmoid(dtpre)
        du_ref[:, SSD_W:SSD_W + LANE] = jnp.where(lane < SSD_HEADS, ddtpre, 0.0)
        red_ref[2:3, 0:LANE] += jnp.sum(ddtpre, axis=0, keepdims=True)
        red_ref[3:4, 0:LANE] += jnp.sum(dda * dt, axis=0, keepdims=True)

    rev = lambda i: (n - 1 - i, 0)
    return pl.pallas_call(
        body, name="ssd_bwd", grid=(n,),
        in_specs=[pl.BlockSpec((T, SSD_W + LANE), lambda i: (n - 1 - i, OFF_Z // (SSD_W + LANE))),
                  pl.BlockSpec((T, SSD_CONV), rev), _vec(LANE), _vec(LANE), _vec(SSD_W), _vec(SSD_W),
                  _full(e64.shape), _full(e64t.shape), _full(e128.shape), _full(tril.shape), _full(triu.shape),
                  pl.BlockSpec((T, SSD_W), rev), pl.BlockSpec((None, SSD_N, SSD_W), lambda i: (n - 1 - i, 0, 0)),
                  pl.BlockSpec((T, SSD_W), lambda i: (n - 1 - i, 1)), pl.BlockSpec(memory_space=pl.ANY)],
        out_specs=[pl.BlockSpec((T, SSD_W + LANE), lambda i: (n - 1 - i, OFF_Z // (SSD_W + LANE))),
                   pl.BlockSpec((T, SSD_CONV), rev), pl.BlockSpec((8, SSD_W), lambda i: (0, 0))],
        out_shape=[SDS((S, N_PAD), F32), SDS((S, SSD_CONV), F32), SDS((8, SSD_W), F32)],
        scratch_shapes=[pltpu.VMEM((SSD_N, SSD_W), F32), pltpu.VMEM((1, SSD_W), F32), pltpu.VMEM((LANE, T), F32)]
        + [pltpu.VMEM((T, SSD_W), F32)] * 4,
        input_output_aliases={14: 0},
        compiler_params=_cp(("arbitrary",)),
    )(u, xbc, bias, alog, dskip_x, nw, _bfc(e64), _bfc(e64t), _bfc(e128), _bfc(tril), _bfc(triu), y_ssd, states, dycat, du)


def _bfc(a):
    return jnp.asarray(a, BF16)


def _outproj_fwd(ycat, wo, x, gate):
    S = x.shape[0]
    tm = min(512, S)

    def body(yc_ref, wo_ref, x_ref, g_ref, xn_ref, y_ref):
        y = _mm(_bf(yc_ref[...]), wo_ref[...])
        y_ref[...] = y
        xn_ref[...] = x_ref[...] + g_ref[...] * y

    row = pl.BlockSpec((tm, D_MODEL), lambda i: (i, 0))
    return pl.pallas_call(
        body, name="outproj_fwd", grid=(S // tm,),
        in_specs=[pl.BlockSpec((tm, D_INNER), lambda i: (i, 0)), _full((D_INNER, D_MODEL)), row, _vec(D_MODEL)],
        out_specs=[row, row],
        out_shape=[SDS((S, D_MODEL), F32), SDS((S, D_MODEL), F32)],
        compiler_params=_cp(("parallel",)),
    )(ycat, wo, x, gate)


def _outproj_bwd(dxn, y, gate, ycat, wo):
    S = dxn.shape[0]
    tm = min(512, S)

    def body(dx_ref, y_ref, g_ref, yc_ref, wo_ref, dyc_ref, gwo_ref, dg_ref):
        @pl.when(pl.program_id(0) == 0)
        def _():
            gwo_ref[...] = jnp.zeros_like(gwo_ref)
            dg_ref[...] = jnp.zeros_like(dg_ref)

        dxv = dx_ref[...]
        dy = _bf(dxv * g_ref[...])
        dg_ref[0:1, :] += jnp.sum(dxv * y_ref[...], axis=0, keepdims=True)
        dyc_ref[...] = _mm_nt(dy, wo_ref[...])
        gwo_ref[...] += _mm_tn(_bf(yc_ref[...]), dy)

    row = pl.BlockSpec((tm, D_MODEL), lambda i: (i, 0))
    wide = pl.BlockSpec((tm, D_INNER), lambda i: (i, 0))
    return pl.pallas_call(
        body, name="outproj_bwd", grid=(S // tm,),
        in_specs=[row, row, _vec(D_MODEL), wide, _full((D_INNER, D_MODEL))],
        out_specs=[wide, _full((D_INNER, D_MODEL)), _full((8, D_MODEL))],
        out_shape=[SDS((S, D_INNER), F32), SDS((D_INNER, D_MODEL), F32), SDS((8, D_MODEL), F32)],
        compiler_params=_cp(("arbitrary",)),
    )(dxn, y, gate, ycat, wo)


def _loss_head(x, fw, target):
    S = x.shape[0]
    tm = min(512, S)

    def body(x_ref, fw_ref, t_ref, dx_ref, red_ref):
        @pl.when(pl.program_id(0) == 0)
        def _():
            red_ref[...] = jnp.zeros_like(red_ref)

        xv = x_ref[...]
        fwv = fw_ref[...]
        inv = lax.rsqrt(jnp.mean(xv * xv, axis=-1, keepdims=True) + EPS)
        xhat = xv * inv
        err = xhat * fwv - t_ref[...]
        col = jnp.sum(err * err, axis=0, keepdims=True)
        red_ref[1:2, :] += jnp.broadcast_to(jnp.sum(col, axis=1, keepdims=True) * (0.5 / D_MODEL), (1, D_MODEL))
        dy = err * (1.0 / D_MODEL)
        red_ref[0:1, :] += jnp.sum(dy * xhat, axis=0, keepdims=True)
        dxhat = dy * fwv
        dx_ref[...] = inv * (dxhat - xhat * jnp.mean(dxhat * xhat, axis=-1, keepdims=True))

    row = pl.BlockSpec((tm, D_MODEL), lambda i: (i, 0))
    return pl.pallas_call(
        body, name="loss_head", grid=(S // tm,),
        in_specs=[row, _vec(D_MODEL), row],
        out_specs=[row, _full((8, D_MODEL))],
        out_shape=[SDS((S, D_MODEL), F32), SDS((8, D_MODEL), F32)],
        compiler_params=_cp(("arbitrary",)),
    )(x, fw, target)


ADA_COLS = 3 * D_MODEL // N_DEV


def _ada_fwd(c_all, w_ada, b_cols):
    def body(c_ref, w_ref, b_ref, out_ref):
        out_ref[...] = _mm(_bf(_silu(c_ref[...])), _bf(w_ref[...])) + b_ref[...]

    return pl.pallas_call(
        body, name="ada_fwd", grid=(DEPTH,),
        in_specs=[_full((N_DEV, D_MODEL)), pl.BlockSpec((None, D_MODEL, ADA_COLS), lambda l: (l, 0, 0)),
                  pl.BlockSpec((None, 1, ADA_COLS), lambda l: (l, 0, 0))],
        out_specs=pl.BlockSpec((None, N_DEV, ADA_COLS), lambda l: (l, 0, 0)),
        out_shape=SDS((DEPTH, N_DEV, ADA_COLS), F32),
        compiler_params=_cp(("parallel",)),
    )(c_all, w_ada, b_cols)


def _ada_bwd(ct_pad, dmod_pad):
    def body(c_ref, d_ref, out_ref):
        out_ref[...] = _mm(_bf(_silu(c_ref[...])), _bf(d_ref[...]))

    return pl.pallas_call(
        body, name="ada_bwd", grid=(DEPTH,),
        in_specs=[_full((D_MODEL, LANE)), pl.BlockSpec((None, LANE, ADA_COLS), lambda l: (l, 0, 0))],
        out_specs=pl.BlockSpec((None, D_MODEL, ADA_COLS), lambda l: (l, 0, 0)),
        out_shape=SDS((DEPTH, D_MODEL, ADA_COLS), F32),
        compiler_params=_cp(("parallel",)),
    )(ct_pad, dmod_pad)


def _adamw(parts, w, m, v, name):
    n, R, C = parts.shape
    tr = R
    while tr * C * 4 > (1 << 20) and tr % 16 == 0:
        tr //= 2

    def body(p_ref, w_ref, m_ref, v_ref, g_ref, d_ref, mo_ref, vo_ref):
        g = p_ref[0].astype(F32)
        for k in range(1, n):
            g = g + p_ref[k].astype(F32)
        mn = ADAM_B1 * m_ref[...] + (1.0 - ADAM_B1) * g
        vn = ADAM_B2 * v_ref[...] + (1.0 - ADAM_B2) * (g * g)
        m_hat = mn / (1.0 - ADAM_B1 ** ADAM_STEP)
        v_hat = vn / (1.0 - ADAM_B2 ** ADAM_STEP)
        g_ref[...] = g
        d_ref[...] = -ADAM_LR * (m_hat / (jnp.sqrt(v_hat) + ADAM_EPS) + ADAM_WD * w_ref[...])
        mo_ref[...] = mn
        vo_ref[...] = vn

    blk = pl.BlockSpec((tr, C), lambda i: (i, 0))
    return pl.pallas_call(
        body, name=name, grid=(R // tr,),
        in_specs=[pl.BlockSpec((n, tr, C), lambda i: (0, i, 0)), blk, blk, blk],
        out_specs=[blk] * 4,
        out_shape=[SDS((R, C), F32)] * 4,
        compiler_params=_cp(("parallel",)),
    )(parts, w, m, v)


MESH = pl.DeviceIdType.MESH
ANY = pl.BlockSpec(memory_space=pl.ANY)


def _all_gather(v, name):
    def body(v_ref, out_ref, send_sems, recv_sems, local_sem):
        x, y, c = lax.axis_index("x"), lax.axis_index("y"), lax.axis_index("c")
        me, sibling = (x, y, c), (x, y, 1 - c)
        chips = [(1 - x, y), (x, 1 - y), (1 - x, 1 - y)]

        def slot(px, py, pc):
            return out_ref.at[4 * px + 2 * py + pc]

        def copy(k, block, to, src=None):
            return pltpu.make_async_remote_copy(
                src_ref=slot(*block) if src is None else src, dst_ref=slot(*block),
                send_sem=send_sems.at[k], recv_sem=recv_sems.at[k], device_id=to, device_id_type=MESH)

        mine = pltpu.make_async_copy(v_ref, slot(*me), local_sem)
        mine.start()
        first = [copy(0, me, sibling, src=v_ref)]
        first += [copy(1 + j, me, (*chip, c), src=v_ref) for j, chip in enumerate(chips)]
        for cp in first:
            cp.start()
        passed = [copy(4 + j, (*chip, c), sibling) for j, chip in enumerate(chips)]
        for j, chip in enumerate(chips):
            copy(1 + j, (*chip, c), me).wait_recv()
            passed[j].start()
        copy(0, sibling, me).wait_recv()
        for j, chip in enumerate(chips):
            copy(4 + j, (*chip, 1 - c), me).wait_recv()
        for cp in first + passed:
            cp.wait_send()
        mine.wait()

    return pl.pallas_call(
        body, name=name, in_specs=[ANY], out_specs=ANY,
        out_shape=SDS((N_DEV,) + v.shape, v.dtype),
        scratch_shapes=[pltpu.SemaphoreType.DMA((7,)), pltpu.SemaphoreType.DMA((7,)), pltpu.SemaphoreType.DMA],
    )(v)


def _all_to_all(v, name):
    def body(v_ref, out_ref, send_sems, recv_sems, local_sem):
        x, y, c = lax.axis_index("x"), lax.axis_index("y"), lax.axis_index("c")
        mine_idx = 4 * x + 2 * y + c
        mine = pltpu.make_async_copy(v_ref.at[mine_idx], out_ref.at[mine_idx], local_sem)
        mine.start()
        sends, recvs = [], []
        for k in range(1, N_DEV):
            px = 1 - x if k & 4 else x
            py = 1 - y if k & 2 else y
            pc = 1 - c if k & 1 else c
            peer_idx = 4 * px + 2 * py + pc
            sems = dict(send_sem=send_sems.at[k - 1], recv_sem=recv_sems.at[k - 1], device_id=(px, py, pc),
                        device_id_type=MESH)
            sends.append(pltpu.make_async_remote_copy(src_ref=v_ref.at[peer_idx], dst_ref=out_ref.at[mine_idx], **sems))
            recvs.append(pltpu.make_async_remote_copy(src_ref=v_ref.at[peer_idx], dst_ref=out_ref.at[peer_idx], **sems))
        for cp in sends:
            cp.start()
        for cp in recvs:
            cp.wait_recv()
        for cp in sends:
            cp.wait_send()
        mine.wait()

    return pl.pallas_call(
        body, name=name, in_specs=[ANY], out_specs=ANY,
        out_shape=SDS(v.shape, v.dtype),
        scratch_shapes=[pltpu.SemaphoreType.DMA((7,)), pltpu.SemaphoreType.DMA((7,)), pltpu.SemaphoreType.DMA],
    )(v)


HBM_SPEC = pl.BlockSpec(memory_space=pltpu.HBM)
SEM_SPEC = pl.BlockSpec(memory_space=pltpu.SEMAPHORE)
EFFECT = pltpu.SideEffectType.DATAFLOW_SIDE_EFFECTING


def _exchange_copies(srcs, lands, send_sems, recv_sems, scatter, layer):
    x, y, c = lax.axis_index("x"), lax.axis_index("y"), lax.axis_index("c")
    me = 4 * x + 2 * y + c
    copies = []
    for a, (src, land) in enumerate(zip(srcs, lands)):
        for k in range(1, N_DEV):
            px = 1 - x if k & 4 else x
            py = 1 - y if k & 2 else y
            pc = 1 - c if k & 1 else c
            n = 7 * a + k - 1
            copies.append(pltpu.make_async_remote_copy(
                src_ref=src.at[4 * px + 2 * py + pc] if scatter else src,
                dst_ref=land.at[me, layer] if scatter else land.at[me],
                send_sem=send_sems.at[n], recv_sem=recv_sems.at[n], device_id=(px, py, pc), device_id_type=MESH))
    return copies


def _exchange_start(name, srcs, lands, scatter, layer=0, after=None):
    n = len(srcs)

    def body(*refs):
        send_sems, recv_sems = refs[-2 * n - 3], refs[-2 * n - 2]
        for cp in _exchange_copies(refs[:n], refs[n:2 * n], send_sems, recv_sems, scatter, layer):
            cp.start()
        refs[-1][...] = jnp.zeros_like(refs[-1])

    arrays = list(srcs) + list(lands)
    sems = pltpu.SemaphoreType.DMA((7 * n,))
    out = pl.pallas_call(
        body, name=name,
        out_shape=(sems, sems, *[pltpu.HBM(v.shape, v.dtype) for v in arrays], SDS((8, LANE), F32)),
        in_specs=[HBM_SPEC] * (2 * n) + ([ANY] if after is not None else []),
        out_specs=(SEM_SPEC, SEM_SPEC, *[HBM_SPEC] * (2 * n), pl.BlockSpec(memory_space=pltpu.VMEM)),
        input_output_aliases={i: 2 + i for i in range(2 * n)},
        compiler_params=pltpu.CompilerParams(has_side_effects=EFFECT),
    )(*[pltpu.with_memory_space_constraint(v, pltpu.HBM) for v in arrays], *([after] if after is not None else []))
    return dict(sems=out[:2], srcs=out[2:2 + n], lands=out[2 + n:2 + 2 * n], token=out[-1][0, 0], scatter=scatter,
                layer=layer)


def _exchange_wait(name, st, after):
    n = len(st["srcs"])

    def body(*refs):
        send_sems, recv_sems = refs[2 * n], refs[2 * n + 1]
        for cp in _exchange_copies(refs[:n], refs[n:2 * n], send_sems, recv_sems, st["scatter"], st["layer"]):
            cp.wait_send()
            cp.wait_recv()

    arrays = list(st["srcs"]) + list(st["lands"])
    out = pl.pallas_call(
        body, name=name,
        out_shape=tuple(pltpu.HBM(v.shape, v.dtype) for v in arrays),
        in_specs=[HBM_SPEC] * (2 * n) + [SEM_SPEC, SEM_SPEC, ANY],
        out_specs=tuple([HBM_SPEC] * (2 * n)),
        input_output_aliases={i: i for i in range(2 * n)},
        compiler_params=pltpu.CompilerParams(has_side_effects=EFFECT),
    )(*arrays, *st["sems"], after)
    return out[n:]


_IN_PIECES = ([(1024, 3072)]
              + [r for t in range(4) for r in ((LANE * t, LANE * (t + 1)), (512 + LANE * t, 512 + LANE * (t + 1)))]
              + [(4096, 5632), (3072, 4096), (5632, 5648)])


def _permute_in(w):
    pad = jnp.zeros(w.shape[:-1] + (N_PAD - N_IN,), w.dtype)
    return jnp.concatenate([w[..., a:b] for a, b in _IN_PIECES] + [pad], axis=-1)


def _unpermute_in(g):
    ax = [g[..., OFF_LRU + 2 * LANE * t:OFF_LRU + 2 * LANE * t + LANE] for t in range(4)]
    ag = [g[..., OFF_LRU + 2 * LANE * t + LANE:OFF_LRU + 2 * LANE * (t + 1)] for t in range(4)]
    return jnp.concatenate(ax + ag + [g[..., 0:2048], g[..., OFF_Z:OFF_Z + SSD_W], g[..., OFF_XBC:OFF_XBC + SSD_CONV],
                                      g[..., OFF_Z + SSD_W:OFF_Z + SSD_W + SSD_HEADS]], axis=-1)


def _block_diag(w):
    w4 = w.reshape(4, 2, 64, 64)
    z = jnp.zeros((4, 64, 64), w.dtype)
    top = jnp.concatenate([w4[:, 0], z], axis=-1)
    bot = jnp.concatenate([z, w4[:, 1]], axis=-1)
    return jnp.concatenate([top, bot], axis=1).astype(BF16)


def _diag_blocks(g):
    return jnp.stack([g[:, :64, :64], g[:, 64:, 64:]], axis=1).reshape(8, 64, 64)


def _pad_lanes(v):
    return jnp.pad(v, (0, LANE - v.shape[0]))[None, :]


def _lower_bounds(logits):
    p = jax.nn.softmax(logits, axis=0)
    return p, jnp.cumsum(p, axis=0) - p[0]


def _lower_bounds_bwd(p, dlb):
    dp = jnp.cumsum(dlb[::-1], axis=0)[::-1]
    dp = dp.at[0].add(-jnp.sum(dlb, axis=0))
    return p * (dp - jnp.sum(dp * p, axis=0, keepdims=True))


SMALL = ["norm_w", "b_ada", "lru_conv_b", "lru_wa", "lru_ba", "lru_wx", "lru_bx", "lru_lambda", "hg_lb_logits",
         "hg_norm_w", "ssd_conv_b", "ssd_dt_bias", "ssd_a_log", "ssd_d", "ssd_norm_w", "final_norm_w"]
WEIGHTS = ["norm_w", "w_ada", "b_ada", "w_in", "lru_conv_w", "lru_conv_b", "lru_wa", "lru_ba", "lru_wx", "lru_bx",
           "lru_lambda", "hg_lb_logits", "hg_norm_w", "ssd_conv_w", "ssd_conv_b", "ssd_dt_bias", "ssd_a_log", "ssd_d",
           "ssd_norm_w", "w_out", "final_norm_w"]
INPUTS = ["x", "c"] + WEIGHTS + ["loss_target"] + ["m_" + n for n in WEIGHTS] + ["v_" + n for n in WEIGHTS]
SMALL_ROW = 1024


def _flatten_small(d, prefix=""):
    flat = jnp.concatenate([d[prefix + n].reshape(-1) for n in SMALL])
    return jnp.pad(flat, (0, -flat.shape[0] % (8 * SMALL_ROW)))


def _split_small(flat, like):
    out, off = {}, 0
    for n in SMALL:
        size = int(np.prod(like[n].shape))
        out[n] = flat[off:off + size].reshape(like[n].shape)
        off += size
    return out


def _local_step(x, mod, target, w, fetch, emit):
    S = x.shape[0]
    mall = _bfc(_hg_consts())
    mall_t = _bfc(_hg_consts().T)
    consts = _ssd_consts()
    p_lb, lbs = _lower_bounds(w["hg_lb_logits"])
    saved = []
    for l in range(DEPTH):
        w_in_l, w_out_l, token = fetch(l, x)
        shift, scale, gate = (mod[l:l + 1, k * D_MODEL:(k + 1) * D_MODEL] for k in range(3))
        shift = shift + token
        prm = dict(
            nw=w["norm_w"][l:l + 1], cw=w["lru_conv_w"][l], cb=w["lru_conv_b"][l:l + 1],
            wa=_block_diag(w["lru_wa"][l]), ba=w["lru_ba"][l].reshape(1, LRU_W),
            wx=_block_diag(w["lru_wx"][l]), bx=w["lru_bx"][l].reshape(1, LRU_W), lam=w["lru_lambda"][l:l + 1],
            lb=lbs[l:l + 1], hnw=w["hg_norm_w"][l:l + 1], scw=w["ssd_conv_w"][l], scb=w["ssd_conv_b"][l:l + 1],
            bias=_pad_lanes(w["ssd_dt_bias"][l]), alog=_pad_lanes(w["ssd_a_log"][l]),
            dskip=jnp.repeat(w["ssd_d"][l], SSD_P)[None, :], snw=w["ssd_norm_w"][l:l + 1],
            w_in=w_in_l, w_out=w_out_l, scale=scale, gate=gate)
        u, h = _inproj_fwd(x, prm["nw"], scale, shift, prm["w_in"])
        ycat = lax.empty((S, D_INNER), F32)
        lru_args = (u, prm["cw"], prm["cb"], prm["wa"], prm["ba"], prm["wx"], prm["bx"], prm["lam"])
        ycat, h_lru = _lru_fwd(*lru_args, ycat)
        ycat, o_b, hg_st = _hg_fwd(u, prm["lb"], prm["hnw"], mall, ycat)
        xbc = _ssdconv_fwd(u, prm["scw"], prm["scb"])
        ssd_args = (u, xbc, prm["bias"], prm["alog"], prm["dskip"], prm["snw"], consts)
        ycat, y_ssd, ssd_st = _ssd_fwd(*ssd_args, ycat)
        x_new, y = _outproj_fwd(ycat, prm["w_out"], x, gate)
        saved.append((prm, x, u, h, ycat, lru_args, h_lru, o_b, hg_st, ssd_args, y_ssd, ssd_st, y))
        x = x_new
    dx, red = _loss_head(x, w["final_norm_w"][None, :], target)
    loss = red[1, 0]
    g = {n: [None] * DEPTH for n in WEIGHTS}
    g["final_norm_w"] = red[0]
    dmod, dlb = [None] * DEPTH, [None] * DEPTH
    token = 0.0
    for l in reversed(range(DEPTH)):
        prm, x, u, h, ycat, lru_args, h_lru, o_b, hg_st, ssd_args, y_ssd, ssd_st, y = saved[l]
        dycat, g_out, dgate = _outproj_bwd(dx, y, prm["gate"] + token, ycat, prm["w_out"])
        du = lax.empty((S, N_PAD), F32)
        du, dxbc, sred = _ssd_bwd(*ssd_args, y_ssd, ssd_st, dycat, du)
        du, cred = _ssdconv_bwd(u, prm["scw"], prm["scb"], dxbc, du)
        du, hred = _hg_bwd(u, prm["lb"], prm["hnw"], mall, mall_t, o_b, hg_st, dycat, du)
        du, lred, gwa, gwx = _lru_bwd(*lru_args, h_lru, dycat, du)
        dx, ired = _inproj_bwd_x(du, prm["w_in"], x, prm["nw"], prm["scale"], dx)
        token = emit(l, _inproj_bwd_w(h, du), g_out)
        g["norm_w"][l] = ired[2]
        dmod[l] = jnp.concatenate([ired[0], ired[1], dgate[0]])
        g["lru_conv_w"][l], g["lru_conv_b"][l] = lred[0:4], lred[4]
        g["lru_ba"][l], g["lru_bx"][l], g["lru_lambda"][l] = lred[5].reshape(8, 64), lred[6].reshape(8, 64), lred[7]
        g["lru_wa"][l], g["lru_wx"][l] = _diag_blocks(gwa), _diag_blocks(gwx)
        g["hg_norm_w"][l], dlb[l] = hred[0], hred[1]
        g["ssd_conv_w"][l], g["ssd_conv_b"][l] = cred[0:4], cred[4]
        g["ssd_norm_w"][l] = sred[0]
        g["ssd_d"][l] = sred[1].reshape(SSD_HEADS, SSD_P).sum(-1)
        g["ssd_dt_bias"][l] = sred[2, :SSD_HEADS]
        g["ssd_a_log"][l] = -sred[3, :SSD_HEADS] * jnp.exp(w["ssd_a_log"][l])
    g["hg_lb_logits"] = _lower_bounds_bwd(p_lb, jnp.stack(dlb))
    for n in WEIGHTS:
        if isinstance(g[n], list) and g[n][0] is not None:
            g[n] = jnp.stack(g[n])
    return loss, dx, jnp.stack(dmod), g


def kernel(x, c, norm_w, w_ada, b_ada, w_in, lru_conv_w, lru_conv_b, lru_wa, lru_ba, lru_wx, lru_bx, lru_lambda, hg_lb_logits, hg_norm_w, ssd_conv_w, ssd_conv_b, ssd_dt_bias, ssd_a_log, ssd_d, ssd_norm_w, w_out, final_norm_w, loss_target, m_norm_w, m_w_ada, m_b_ada, m_w_in, m_lru_conv_w, m_lru_conv_b, m_lru_wa, m_lru_ba, m_lru_wx, m_lru_bx, m_lru_lambda, m_hg_lb_logits, m_hg_norm_w, m_ssd_conv_w, m_ssd_conv_b, m_ssd_dt_bias, m_ssd_a_log, m_ssd_d, m_ssd_norm_w, m_w_out, m_final_norm_w, v_norm_w, v_w_ada, v_b_ada, v_w_in, v_lru_conv_w, v_lru_conv_b, v_lru_wa, v_lru_ba, v_lru_wx, v_lru_bx, v_lru_lambda, v_hg_lb_logits, v_hg_norm_w, v_ssd_conv_w, v_ssd_conv_b, v_ssd_dt_bias, v_ssd_a_log, v_ssd_d, v_ssd_norm_w, v_w_out, v_final_norm_w):
    return _step(x, c, norm_w, w_ada, b_ada, w_in, lru_conv_w, lru_conv_b, lru_wa, lru_ba, lru_wx, lru_bx, lru_lambda, hg_lb_logits, hg_norm_w, ssd_conv_w, ssd_conv_b, ssd_dt_bias, ssd_a_log, ssd_d, ssd_norm_w, w_out, final_norm_w, loss_target, m_norm_w, m_w_ada, m_b_ada, m_w_in, m_lru_conv_w, m_lru_conv_b, m_lru_wa, m_lru_ba, m_lru_wx, m_lru_bx, m_lru_lambda, m_hg_lb_logits, m_hg_norm_w, m_ssd_conv_w, m_ssd_conv_b, m_ssd_dt_bias, m_ssd_a_log, m_ssd_d, m_ssd_norm_w, m_w_out, m_final_norm_w, v_norm_w, v_w_ada, v_b_ada, v_w_in, v_lru_conv_w, v_lru_conv_b, v_lru_wa, v_lru_ba, v_lru_wx, v_lru_bx, v_lru_lambda, v_hg_lb_logits, v_hg_norm_w, v_ssd_conv_w, v_ssd_conv_b, v_ssd_dt_bias, v_ssd_a_log, v_ssd_d, v_ssd_norm_w, v_w_out, v_final_norm_w)


def _step(*args):
    a = dict(zip(INPUTS, args, strict=True))
    me = 4 * lax.axis_index("x") + 2 * lax.axis_index("y") + lax.axis_index("c")
    x, target = a["x"][0], a["loss_target"][0]

    c_all = _all_gather(a["c"], "gather_c")[:, 0, :]
    b_cols = lax.dynamic_slice_in_dim(a["b_ada"], me * ADA_COLS, ADA_COLS, axis=1)[:, None, :]
    mod_parts = _all_gather(_ada_fwd(c_all, a["w_ada"], b_cols), "gather_mod")
    mod = lax.dynamic_index_in_dim(mod_parts, me, axis=2, keepdims=False)
    mod = mod.transpose(1, 0, 2).reshape(DEPTH, 3 * D_MODEL)

    w = {n: a[n] for n in SMALL}
    conv = _all_gather(jnp.concatenate([a["lru_conv_w"], a["ssd_conv_w"]], axis=-1), "gather_conv")
    conv = conv.transpose(1, 2, 0, 3)
    w["lru_conv_w"] = conv[..., :64].reshape(DEPTH, 4, LRU_W)
    w["ssd_conv_w"] = conv[..., 64:].reshape(DEPTH, 4, SSD_CONV)

    w_in_b, w_out_b = a["w_in"].astype(BF16), a["w_out"].astype(BF16)
    cols, rows_out = N_IN // N_DEV, D_INNER // N_DEV

    def gather_start(l, after=None):
        lands = [lax.empty((N_DEV, D_MODEL, cols), BF16), lax.empty((N_DEV, rows_out, D_MODEL), BF16)]
        return _exchange_start(f"gather_start_{l}", [w_in_b[l], w_out_b[l]], lands, False, after=after)

    gathers = {0: gather_start(0), 1: gather_start(1)}

    def fetch(l, x_l):
        if l >= 1 and l + 1 < DEPTH:
            gathers[l + 1] = gather_start(l + 1, after=x_l)
        land_in, land_out = _exchange_wait(f"gather_wait_{l}", gathers[l], x_l)
        land_in = lax.dynamic_update_index_in_dim(land_in, w_in_b[l], me, 0)
        land_out = lax.dynamic_update_index_in_dim(land_out, w_out_b[l], me, 0)
        w_in_l = _permute_in(land_in.transpose(1, 0, 2).reshape(D_MODEL, N_IN))
        token = gathers[l + 1]["token"] if l + 1 < DEPTH else 0.0
        return w_in_l, land_out.reshape(D_INNER, D_MODEL), token

    scatters = {}
    lands = [lax.empty((N_DEV, DEPTH, D_MODEL, cols), BF16), lax.empty((N_DEV, DEPTH, rows_out, D_MODEL), BF16)]
    own = {}

    def emit(l, g_in, g_out):
        g_in = _unpermute_in(g_in).reshape(D_MODEL, N_DEV, cols).transpose(1, 0, 2).astype(BF16)
        g_out = g_out.reshape(N_DEV, rows_out, D_MODEL).astype(BF16)
        own[l] = (lax.dynamic_index_in_dim(g_in, me, 0, keepdims=False),
                  lax.dynamic_index_in_dim(g_out, me, 0, keepdims=False))
        scatters[l] = _exchange_start(f"scatter_start_{l}", [g_in, g_out], lands, True, layer=l)
        lands[:] = scatters[l]["lands"]
        return scatters[l]["token"]

    loss, dx, dmod, g = _local_step(x, mod, target, w, fetch, emit)
    loss = lax.psum(loss, ("x", "y", "c"))
    for l in reversed(range(DEPTH)):
        scatters[l]["lands"] = lands
        lands[:] = _exchange_wait(f"scatter_wait_{l}", scatters[l], dx)
    g_in_parts, g_out_parts = lands
    for l in range(DEPTH):
        g_in_parts = lax.dynamic_update_slice(g_in_parts, own[l][0][None, None], (me, l, 0, 0))
        g_out_parts = lax.dynamic_update_slice(g_out_parts, own[l][1][None, None], (me, l, 0, 0))

    g["b_ada"] = dmod
    small = _all_gather(_flatten_small(g).reshape(-1, SMALL_ROW), "gather_small")
    rows = small.shape[1]
    fl = lambda prefix: _flatten_small(a, prefix).reshape(rows, SMALL_ROW)
    outs = _adamw(small, fl(""), fl("m_"), fl("v_"), "adamw_small")
    res = [_split_small(o.reshape(-1), a) for o in outs]

    off = DEPTH * D_MODEL
    dmod_all = small.reshape(N_DEV, -1)[:, off:off + DEPTH * 3 * D_MODEL]
    dmod_all = dmod_all.reshape(N_DEV, DEPTH, 3 * D_MODEL).transpose(1, 0, 2)
    dmod_cols = lax.dynamic_slice_in_dim(dmod_all, me * ADA_COLS, ADA_COLS, axis=2)
    dmod_pad = jnp.pad(dmod_cols, ((0, 0), (0, LANE - N_DEV), (0, 0)))
    ct_pad = jnp.pad(c_all.T, ((0, 0), (0, LANE - N_DEV)))
    g_ada = _ada_bwd(ct_pad, dmod_pad)

    def sharded(name, parts):
        shape = a[name].shape
        r2 = lambda t: t.reshape(-1, shape[-1])
        o = _adamw(parts.reshape(parts.shape[0], -1, shape[-1]), r2(a[name]), r2(a["m_" + name]), r2(a["v_" + name]),
                   "adamw_" + name)
        return [t.reshape(shape) for t in o]

    big = {"w_ada": sharded("w_ada", g_ada[None])}
    big["w_in"] = sharded("w_in", g_in_parts)
    big["w_out"] = sharded("w_out", g_out_parts)
    g_conv = jnp.concatenate([g["lru_conv_w"].reshape(DEPTH, 4, N_DEV, 64), g["ssd_conv_w"].reshape(DEPTH, 4, N_DEV, 192)],
                             axis=-1).transpose(2, 0, 1, 3)
    conv_parts = _all_to_all(g_conv, "scatter_conv")
    big["lru_conv_w"] = sharded("lru_conv_w", conv_parts[..., :64])
    big["ssd_conv_w"] = sharded("ssd_conv_w", conv_parts[..., 64:])

    out = [loss, dx[None]]
    for k in range(4):
        out += [big[n][k] if n in big else res[k][n] for n in WEIGHTS]
    return tuple(out)
```

```python
import functools

import numpy as np
import jax
import jax.numpy as jnp
from jax import lax
from jax.experimental import pallas as pl
from jax.experimental.pallas import tpu as pltpu

F32 = jnp.float32
BF16 = jnp.bfloat16
SDS = jax.ShapeDtypeStruct

N_DEV = 8
DEPTH = 4
D_MODEL = 1024
D_INNER = 2048
EPS = 1e-6
LRU_W = 512
LRU_C = 8.0
HG_W = 512
HG_CHUNK = 64
HG_HEADS = 4
SSD_W = 1024
SSD_HEADS = 16
SSD_P = 64
SSD_N = 128
SSD_CHUNK = 128
SSD_CONV = 1536
N_IN = 5648
N_PAD = 5760
OFF_HG = 0
OFF_LRU = 2048
OFF_XBC = 3072
OFF_Z = 4608
LANE = 128
VMEM_LIMIT = 56 * 1024 * 1024
NEG = -1e30

ADAM_LR = 0.001
ADAM_B1 = 0.9
ADAM_B2 = 0.999
ADAM_EPS = 1e-08
ADAM_WD = 0.01
ADAM_STEP = 10


def _cp(sem=None):
    return pltpu.CompilerParams(dimension_semantics=sem, vmem_limit_bytes=VMEM_LIMIT)


def _dg(a, b, ca, cb):
    return lax.dot_general(a, b, (((ca,), (cb,)), ((), ())), preferred_element_type=F32)


def _mm(a, b):
    return _dg(a, b, 1, 0)


def _mm_nt(a, b):
    return _dg(a, b, 1, 1)


def _mm_tn(a, b):
    return _dg(a, b, 0, 0)


def _bf(x):
    return x.astype(BF16)


def _split3(x):
    hi = x.astype(BF16)
    r = x - hi.astype(F32)
    mid = r.astype(BF16)
    lo = (r - mid.astype(F32)).astype(BF16)
    return hi, mid, lo


def _sel_r(x, m):
    hi, mid, lo = _split3(x)
    return _mm(hi, m) + _mm(mid, m) + _mm(lo, m)


def _sel_l(m, x):
    hi, mid, lo = _split3(x)
    return _mm(m, hi) + _mm(m, mid) + _mm(m, lo)


def _sel_tn(x, m):
    hi, mid, lo = _split3(x)
    return _mm_tn(hi, m) + _mm_tn(mid, m) + _mm_tn(lo, m)


def _sigmoid(x):
    return 1.0 / (1.0 + jnp.exp(-x))


def _silu(x):
    return x * _sigmoid(x)


def _dsilu(x):
    s = _sigmoid(x)
    return s * (1.0 + x * (1.0 - s))


def _softplus(x):
    return jnp.maximum(x, 0.0) + jnp.log(1.0 + jnp.exp(-jnp.abs(x)))


def _expm1(z):
    series = z * (1.0 + z * (1.0 / 2) * (1.0 + z * (1.0 / 3) * (1.0 + z * (1.0 / 4) * (
        1.0 + z * (1.0 / 5) * (1.0 + z * (1.0 / 6) * (1.0 + z * (1.0 / 7)))))))
    return jnp.where(jnp.abs(z) < 0.3, series, jnp.exp(z) - 1.0)


def _iota(shape, dim):
    return lax.broadcasted_iota(jnp.int32, shape, dim)


def _last_row(x, rows):
    return jnp.sum(jnp.where(rows == x.shape[0] - 1, x, 0.0), axis=0, keepdims=True)


def _shift_down(x, d, rows, fill=0.0):
    return jnp.where(rows >= d, pltpu.roll(x, d, 0), fill)


def _shift_up(x, d, rows, fill=0.0):
    n = x.shape[0]
    return jnp.where(rows < n - d, pltpu.roll(x, n - d, 0), fill)


def _conv_fwd(x, cw_ref, cb_ref, rows):
    out = cb_ref[...] + cw_ref[pl.ds(3, 1), :] * x
    for k in range(3):
        out = out + cw_ref[pl.ds(k, 1), :] * _shift_down(x, 3 - k, rows)
    return out


def _conv_bwd(x, dco, cw_ref, rows):
    dx = cw_ref[pl.ds(3, 1), :] * dco
    dws = []
    for k in range(3):
        dx = dx + cw_ref[pl.ds(k, 1), :] * _shift_up(dco, 3 - k, rows)
        dws.append(jnp.sum(dco * _shift_down(x, 3 - k, rows), axis=0, keepdims=True))
    dws.append(jnp.sum(dco * x, axis=0, keepdims=True))
    return dx, dws, jnp.sum(dco, axis=0, keepdims=True)


def _vec(n):
    return pl.BlockSpec((1, n), lambda *_: (0, 0))


def _full(shape):
    nd = len(shape)
    return pl.BlockSpec(shape, lambda *_: (0,) * nd)


def _inproj_fwd(x, nw, scale, shift, w):
    S = x.shape[0]
    tm, tn = min(512, S), 640

    def body(x_ref, nw_ref, sc_ref, sh_ref, w_ref, u_ref, h_ref):
        @pl.when(pl.program_id(1) == 0)
        def _():
            xv = x_ref[...]
            inv = lax.rsqrt(jnp.mean(xv * xv, axis=-1, keepdims=True) + EPS)
            h = (xv * inv) * nw_ref[...] * (1.0 + sc_ref[...]) + sh_ref[...]
            h_ref[...] = h.astype(BF16)

        u_ref[...] = _mm(h_ref[...], w_ref[...])

    return pl.pallas_call(
        body, name="inproj_fwd", grid=(S // tm, N_PAD // tn),
        in_specs=[pl.BlockSpec((tm, D_MODEL), lambda i, j: (i, 0)), _vec(D_MODEL), _vec(D_MODEL), _vec(D_MODEL),
                  pl.BlockSpec((D_MODEL, tn), lambda i, j: (0, j))],
        out_specs=[pl.BlockSpec((tm, tn), lambda i, j: (i, j)), pl.BlockSpec((tm, D_MODEL), lambda i, j: (i, 0))],
        out_shape=[SDS((S, N_PAD), F32), SDS((S, D_MODEL), BF16)],
        compiler_params=_cp(("parallel", "arbitrary")),
    )(x, nw, scale, shift, w)


def _inproj_bwd_x(du, w, x, nw, scale, dxn):
    S = x.shape[0]
    tm, tk = min(512, S), 640
    nk = N_PAD // tk

    def body(du_ref, w_ref, x_ref, nw_ref, sc_ref, dxn_ref, dx_ref, red_ref, acc):
        i, k = pl.program_id(0), pl.program_id(1)

        @pl.when(k == 0)
        def _():
            acc[...] = jnp.zeros_like(acc)

        @pl.when((i == 0) & (k == 0))
        def _():
            red_ref[...] = jnp.zeros_like(red_ref)

        acc[...] += _mm_nt(_bf(du_ref[...]), w_ref[...])

        @pl.when(k == nk - 1)
        def _():
            dh = acc[...]
            xv = x_ref[...]
            inv = lax.rsqrt(jnp.mean(xv * xv, axis=-1, keepdims=True) + EPS)
            xhat = xv * inv
            nwv = nw_ref[...]
            g1 = 1.0 + sc_ref[...]
            dxhat = dh * nwv * g1
            dx = inv * (dxhat - xhat * jnp.mean(dxhat * xhat, axis=-1, keepdims=True))
            dx_ref[...] = dxn_ref[...] + dx
            red_ref[0:1, :] += jnp.sum(dh, axis=0, keepdims=True)
            red_ref[1:2, :] += jnp.sum(dh * xhat * nwv, axis=0, keepdims=True)
            red_ref[2:3, :] += jnp.sum(dh * xhat * g1, axis=0, keepdims=True)

    row = pl.BlockSpec((tm, D_MODEL), lambda i, k: (i, 0))
    return pl.pallas_call(
        body, name="inproj_bwd_x", grid=(S // tm, nk),
        in_specs=[pl.BlockSpec((tm, tk), lambda i, k: (i, k)), pl.BlockSpec((D_MODEL, tk), lambda i, k: (0, k)),
                  row, _vec(D_MODEL), _vec(D_MODEL), row],
        out_specs=[row, pl.BlockSpec((8, D_MODEL), lambda i, k: (0, 0))],
        out_shape=[SDS((S, D_MODEL), F32), SDS((8, D_MODEL), F32)],
        scratch_shapes=[pltpu.VMEM((tm, D_MODEL), F32)],
        compiler_params=_cp(("arbitrary", "arbitrary")),
    )(du, w, x, nw, scale, dxn)


def _inproj_bwd_w(h, du):
    S = h.shape[0]
    tn = 640

    def body(h_ref, du_ref, gw_ref):
        gw_ref[...] = _mm_tn(h_ref[...], _bf(du_ref[...]))

    return pl.pallas_call(
        body, name="inproj_bwd_w", grid=(N_PAD // tn,),
        in_specs=[_full((S, D_MODEL)), pl.BlockSpec((S, tn), lambda j: (0, j))],
        out_specs=pl.BlockSpec((D_MODEL, tn), lambda j: (0, j)),
        out_shape=SDS((D_MODEL, N_PAD), F32),
        compiler_params=_cp(("parallel",)),
    )(h, du)


def _scan_block(a, b, rows):
    d = 1
    while d < a.shape[0]:
        a_s = _shift_down(a, d, rows, 1.0)
        b_s = _shift_down(b, d, rows, 0.0)
        b = a * b_s + b
        a = a * a_s
        d *= 2
    return a, b


def _rscan_block(c, g, rows):
    d = 1
    while d < c.shape[0]:
        c_s = _shift_up(c, d, rows, 1.0)
        g_s = _shift_up(g, d, rows, 0.0)
        g = g + c * g_s
        c = c * c_s
        d *= 2
    return c, g


def _lru_gates(xa, wa_ref, ba_ref, wx_ref, bx_ref, lam_ref):
    sp = _softplus(-lam_ref[...])
    xb = _bf(xa)
    r = _sigmoid(_mm(xb, wa_ref[...]) + ba_ref[...])
    ig = _sigmoid(_mm(xb, wx_ref[...]) + bx_ref[...])
    la = -LRU_C * r * sp
    a = jnp.exp(la)
    mult = jnp.sqrt(-_expm1(2.0 * la))
    return sp, r, ig, la, a, mult


def _lru_specs(S):
    t128 = pl.BlockSpec((1, LANE), lambda t: (0, t))
    return [pl.BlockSpec((S, 2 * LANE), lambda t: (0, OFF_LRU // (2 * LANE) + t)),
            pl.BlockSpec((4, LANE), lambda t: (0, t)), t128,
            pl.BlockSpec((None, LANE, LANE), lambda t: (t, 0, 0)), t128,
            pl.BlockSpec((None, LANE, LANE), lambda t: (t, 0, 0)), t128, t128]


def _lru_fwd(u, cw, cb, wa, ba, wx, bx, lam, ycat):
    S = u.shape[0]
    tb = min(256, S)

    def body(u_ref, cw_ref, cb_ref, wa_ref, ba_ref, wx_ref, bx_ref, lam_ref, ycat_in, ycat_ref, h_ref, a_scr, b_scr):
        del ycat_in
        rows = _iota((S, LANE), 0)
        xa = _conv_fwd(u_ref[:, 0:LANE], cw_ref, cb_ref, rows)
        _, _, ig, _, a, mult = _lru_gates(xa, wa_ref, ba_ref, wx_ref, bx_ref, lam_ref)
        a_scr[...] = a
        b_scr[...] = mult * (ig * xa)
        rows_b = _iota((tb, LANE), 0)

        def blk(j, hprev):
            sl = pl.ds(pl.multiple_of(j * tb, tb), tb)
            acum, hloc = _scan_block(a_scr[sl, :], b_scr[sl, :], rows_b)
            hf = hloc + acum * hprev
            h_ref[sl, :] = hf
            return _last_row(hf, rows_b)

        lax.fori_loop(0, S // tb, blk, jnp.zeros((1, LANE), F32))
        ycat_ref[...] = h_ref[...] * _silu(u_ref[:, LANE:2 * LANE])

    col = pl.BlockSpec((S, LANE), lambda t: (0, t))
    return pl.pallas_call(
        body, name="lru_fwd", grid=(LRU_W // LANE,),
        in_specs=_lru_specs(S) + [pl.BlockSpec(memory_space=pl.ANY)],
        out_specs=[col, col],
        out_shape=[SDS((S, D_INNER), F32), SDS((S, LRU_W), F32)],
        scratch_shapes=[pltpu.VMEM((S, LANE), F32), pltpu.VMEM((S, LANE), F32)],
        input_output_aliases={8: 0},
        compiler_params=_cp(("parallel",)),
    )(u, cw, cb, wa, ba, wx, bx, lam, ycat)


def _lru_bwd(u, cw, cb, wa, ba, wx, bx, lam, h_lru, dycat, du):
    S = u.shape[0]
    tb = min(256, S)

    def body(u_ref, cw_ref, cb_ref, wa_ref, ba_ref, wx_ref, bx_ref, lam_ref, h_ref, dy_ref, du_in,
             du_ref, red_ref, gwa_ref, gwx_ref, c_scr, g_scr, l_scr):
        del du_in
        rows = _iota((S, LANE), 0)
        ax = u_ref[:, 0:LANE]
        ag = u_ref[:, LANE:2 * LANE]
        xa = _conv_fwd(ax, cw_ref, cb_ref, rows)
        sp, r, ig, la, a, mult = _lru_gates(xa, wa_ref, ba_ref, wx_ref, bx_ref, lam_ref)
        h = h_ref[...]
        dy = dy_ref[...]
        du_ref[:, LANE:2 * LANE] = dy * h * _dsilu(ag)
        c_scr[...] = _shift_up(a, 1, rows, 0.0)
        g_scr[...] = dy * _silu(ag)
        rows_b = _iota((tb, LANE), 0)
        nb = S // tb

        def blk(jj, lnext):
            j = nb - 1 - jj
            sl = pl.ds(pl.multiple_of(j * tb, tb), tb)
            ccum, lloc = _rscan_block(c_scr[sl, :], g_scr[sl, :], rows_b)
            lam_t = lloc + ccum * lnext
            l_scr[sl, :] = lam_t
            return jnp.sum(jnp.where(rows_b == 0, lam_t, 0.0), axis=0, keepdims=True)

        lax.fori_loop(0, nb, blk, jnp.zeros((1, LANE), F32))
        db = l_scr[...]
        da = db * _shift_down(h, 1, rows)
        dmult = db * ig * xa
        dig = db * mult * xa
        dxa = db * mult * ig
        dla = da * a - dmult * (a * a) / mult
        dr = -LRU_C * sp * dla
        dsp = jnp.sum(-LRU_C * r * dla, axis=0, keepdims=True)
        dlam = -dsp * _sigmoid(-lam_ref[...])
        dzr = dr * r * (1.0 - r)
        dzi = dig * ig * (1.0 - ig)
        dzr_b, dzi_b, xa_b = _bf(dzr), _bf(dzi), _bf(xa)
        dxa = dxa + _mm_nt(dzr_b, wa_ref[...]) + _mm_nt(dzi_b, wx_ref[...])
        gwa_ref[...] = _mm_tn(xa_b, dzr_b)
        gwx_ref[...] = _mm_tn(xa_b, dzi_b)
        dax, dws, dcb = _conv_bwd(ax, dxa, cw_ref, rows)
        du_ref[:, 0:LANE] = dax
        parts = dws + [dcb, jnp.sum(dzr, axis=0, keepdims=True), jnp.sum(dzi, axis=0, keepdims=True), dlam]
        for n, p in enumerate(parts):
            red_ref[pl.ds(n, 1), :] = p

    col = pl.BlockSpec((S, LANE), lambda t: (0, t))
    gw = pl.BlockSpec((None, LANE, LANE), lambda t: (t, 0, 0))
    return pl.pallas_call(
        body, name="lru_bwd", grid=(LRU_W // LANE,),
        in_specs=_lru_specs(S) + [col, col, pl.BlockSpec(memory_space=pl.ANY)],
        out_specs=[pl.BlockSpec((S, 2 * LANE), lambda t: (0, OFF_LRU // (2 * LANE) + t)),
                   pl.BlockSpec((8, LANE), lambda t: (0, t)), gw, gw],
        out_shape=[SDS((S, N_PAD), F32), SDS((8, LRU_W), F32), SDS((4, LANE, LANE), F32), SDS((4, LANE, LANE), F32)],
        scratch_shapes=[pltpu.VMEM((S, LANE), F32)] * 3,
        input_output_aliases={10: 0},
        compiler_params=_cp(("parallel",)),
    )(u, cw, cb, wa, ba, wx, bx, lam, h_lru, dycat, du)


HG_LEVELS = 6


def _hg_consts():
    C = HG_CHUNK
    t = np.arange(C)[:, None]
    r = np.arange(C)[None, :]
    mats = []
    for side in ("q", "k"):
        for l in range(HG_LEVELS):
            b = 1 << l
            upper = (t % (2 * b)) >= b
            anchor = (t // (2 * b)) * 2 * b + b - 1
            if side == "q":
                mats.append(upper & (r > anchor) & (r <= t))
            else:
                mats.append((~upper) & (r > t) & (r <= anchor))
    mats.append(r <= t)
    mats.append(r > t)
    return np.concatenate(mats, 0).astype(np.float32)


def _hg_factors(hf, lb, mall):
    s = _sigmoid(hf)
    f = lb + (1.0 - lb) * s
    lf = jnp.log(f)
    k = (1.0 - lb) * _sigmoid(-hf)
    e = jnp.exp(_sel_l(mall, lf))
    C = HG_CHUNK
    eq = [e[l * C:(l + 1) * C] for l in range(HG_LEVELS)]
    ek = [e[(HG_LEVELS + l) * C:(HG_LEVELS + l + 1) * C] for l in range(HG_LEVELS)]
    ecum = e[2 * HG_LEVELS * C:(2 * HG_LEVELS + 1) * C]
    erem = e[(2 * HG_LEVELS + 1) * C:(2 * HG_LEVELS + 2) * C]
    return s, f, k, eq, ek, ecum, erem


def _hg_masks():
    C = HG_CHUNK
    ri, ci = _iota((C, C), 0), _iota((C, C), 1)
    rr = _iota((C, LANE), 0)
    gm = [(lax.shift_right_logical(ri, l + 1) == lax.shift_right_logical(ci, l + 1)).astype(F32)
          for l in range(HG_LEVELS)]
    up = [(lax.shift_right_logical(rr, l) & 1) == 1 for l in range(HG_LEVELS)]
    eye = (ri == ci).astype(F32)
    return gm, up, eye, rr


def _hg_scores(qh, kh, eq, ek, sl, gm, up, eye):
    qs, ks = [], []
    p = _mm_nt(_bf(qh), _bf(kh)) * eye
    for l in range(HG_LEVELS):
        ql = jnp.where(up[l], qh * eq[l][:, sl], 0.0)
        kl = jnp.where(up[l], 0.0, kh * ek[l][:, sl])
        p = p + _mm_nt(_bf(ql), _bf(kl)) * gm[l]
        qs.append(ql)
        ks.append(kl)
    return p, qs, ks


def _hg_fwd(u, lb, nw, mall, ycat):
    S = u.shape[0]
    C = HG_CHUNK
    n = S // C

    def body(u_ref, lb_ref, nw_ref, mall_ref, ycat_in, ycat_ref, o_ref, st_ref, st):
        del ycat_in

        @pl.when(pl.program_id(0) == 0)
        def _():
            st[...] = jnp.zeros_like(st)

        q = _silu(u_ref[:, 0:512])
        v = u_ref[:, 1024:1536]
        _, _, k, eq, ek, ecum, erem = _hg_factors(u_ref[:, 512:1024], lb_ref[...], mall_ref[...])
        gm, up, eye, rr = _hg_masks()
        for h in range(HG_HEADS):
            sl = slice(h * LANE, (h + 1) * LANE)
            qh, kh, vh = q[:, sl], k[:, sl], _bf(v[:, sl])
            p, _, _ = _hg_scores(qh, kh, eq, ek, sl, gm, up, eye)
            sth = st[h]
            st_ref[h] = sth
            o_ref[:, sl] = _mm(_bf(p), vh) + _mm_nt(_bf(qh * ecum[:, sl]), _bf(sth))
            st[h] = sth * _last_row(ecum[:, sl], rr) + _mm_tn(vh, _bf(kh * erem[:, sl]))
        o = o_ref[...]
        inv = lax.rsqrt(jnp.mean(o * o, axis=-1, keepdims=True) + EPS)
        ycat_ref[...] = (o * inv) * nw_ref[...] * _silu(u_ref[:, 1536:2048])

    return pl.pallas_call(
        body, name="hg_fwd", grid=(n,),
        in_specs=[pl.BlockSpec((C, 2048), lambda i: (i, 0)), _vec(HG_W), _vec(HG_W), _full(mall.shape),
                  pl.BlockSpec(memory_space=pl.ANY)],
        out_specs=[pl.BlockSpec((C, HG_W), lambda i: (i, 1)), pl.BlockSpec((C, HG_W), lambda i: (i, 0)),
                   pl.BlockSpec((None, HG_HEADS, LANE, LANE), lambda i: (i, 0, 0, 0))],
        out_shape=[SDS((S, D_INNER), F32), SDS((S, HG_W), F32), SDS((n, HG_HEADS, LANE, LANE), F32)],
        scratch_shapes=[pltpu.VMEM((HG_HEADS, LANE, LANE), F32)],
        input_output_aliases={4: 0},
        compiler_params=_cp(("arbitrary",)),
    )(u, lb, nw, mall, ycat)


def _hg_bwd(u, lb, nw, mall, mall_t, o_b, states, dycat, du):
    S = u.shape[0]
    C = HG_CHUNK
    n = S // C
    L2 = 2 * HG_LEVELS

    def body(u_ref, lb_ref, nw_ref, mall_ref, mallt_ref, o_ref, st_ref, dy_ref, du_in, du_ref, red_ref,
             dst, dlast_s, dq_s, dk_s, dex):
        del du_in

        @pl.when(pl.program_id(0) == 0)
        def _():
            dst[...] = jnp.zeros_like(dst)
            red_ref[...] = jnp.zeros_like(red_ref)

        lb = lb_ref[...]
        hq, hf, hg = u_ref[:, 0:512], u_ref[:, 512:1024], u_ref[:, 1536:2048]
        q = _silu(hq)
        v = u_ref[:, 1024:1536]
        s, f, k, eq, ek, ecum, erem = _hg_factors(hf, lb, mall_ref[...])
        gm, up, eye, rr = _hg_masks()
        o = o_ref[...]
        dy = dy_ref[...]
        inv = lax.rsqrt(jnp.mean(o * o, axis=-1, keepdims=True) + EPS)
        ohat = o * inv
        nwv = nw_ref[...]
        du_ref[:, 1536:2048] = dy * ohat * nwv * _dsilu(hg)
        dn = dy * _silu(hg)
        red_ref[0:1, :] += jnp.sum(dn * ohat, axis=0, keepdims=True)
        dohat = dn * nwv
        do = inv * (dohat - ohat * jnp.mean(dohat * ohat, axis=-1, keepdims=True))
        for h in range(HG_HEADS):
            sl = slice(h * LANE, (h + 1) * LANE)
            qh, kh, vh, doh = q[:, sl], k[:, sl], _bf(v[:, sl]), _bf(do[:, sl])
            p, qs, ks = _hg_scores(qh, kh, eq, ek, sl, gm, up, eye)
            st_f = st_ref[h]
            sth = _bf(st_f)
            dsth = dst[h]
            dsth_b = _bf(dsth)
            qt = qh * ecum[:, sl]
            kt = kh * erem[:, sl]
            elast = _last_row(ecum[:, sl], rr)
            dp = _mm_nt(doh, vh)
            du_ref[:, 1024 + h * LANE:1024 + (h + 1) * LANE] = _mm_tn(_bf(p), doh) + _mm_nt(_bf(kt), dsth_b)
            dpe = _bf(dp * eye)
            dqt = _mm(doh, sth)
            dkt = _mm(vh, dsth_b)
            dq = dqt * ecum[:, sl] + _mm(dpe, _bf(kh))
            dk = dkt * erem[:, sl] + _mm_tn(dpe, _bf(qh))
            dex[L2 * C:(L2 + 1) * C, sl] = dqt * qt
            dex[(L2 + 1) * C:(L2 + 2) * C, sl] = dkt * kt
            for l in range(HG_LEVELS):
                dpl = _bf(dp * gm[l])
                dql = _mm(dpl, _bf(ks[l]))
                dkl = _mm_tn(dpl, _bf(qs[l]))
                dq = dq + jnp.where(up[l], dql * eq[l][:, sl], 0.0)
                dk = dk + jnp.where(up[l], 0.0, dkl * ek[l][:, sl])
                dex[l * C:(l + 1) * C, sl] = dql * qs[l]
                dex[(HG_LEVELS + l) * C:(HG_LEVELS + l + 1) * C, sl] = dkl * ks[l]
            dlast_s[:, sl] = jnp.sum(dsth * st_f, axis=0, keepdims=True) * elast
            dst[h] = dsth * elast + _mm_tn(doh, _bf(qt))
            dq_s[:, sl] = dq
            dk_s[:, sl] = dk
        dq = dq_s[...]
        dk = dk_s[...]
        dlf = _sel_l(mallt_ref[...], dex[...]) + dlast_s[...]
        du_ref[:, 0:512] = dq * _dsilu(hq)
        t = (1.0 - s) * (dlf / f - dk)
        du_ref[:, 512:1024] = (1.0 - lb) * s * t
        red_ref[1:2, :] += jnp.sum(t, axis=0, keepdims=True)

    rev = lambda i: (n - 1 - i, 0)
    return pl.pallas_call(
        body, name="hg_bwd", grid=(n,),
        in_specs=[pl.BlockSpec((C, 2048), rev), _vec(HG_W), _vec(HG_W), _full(mall.shape), _full(mall_t.shape),
                  pl.BlockSpec((C, HG_W), rev),
                  pl.BlockSpec((None, HG_HEADS, LANE, LANE), lambda i: (n - 1 - i, 0, 0, 0)),
                  pl.BlockSpec((C, HG_W), lambda i: (n - 1 - i, 1)), pl.BlockSpec(memory_space=pl.ANY)],
        out_specs=[pl.BlockSpec((C, 2048), rev), pl.BlockSpec((8, HG_W), lambda i: (0, 0))],
        out_shape=[SDS((S, N_PAD), F32), SDS((8, HG_W), F32)],
        scratch_shapes=[pltpu.VMEM((HG_HEADS, LANE, LANE), F32), pltpu.VMEM((1, HG_W), F32),
                        pltpu.VMEM((C, HG_W), F32), pltpu.VMEM((C, HG_W), F32), pltpu.VMEM(((L2 + 2) * C, HG_W), F32)],
        input_output_aliases={8: 0},
        compiler_params=_cp(("arbitrary",)),
    )(u, lb, nw, mall, mall_t, o_b, states, dycat, du)


def _ssdconv_fwd(u, cw, cb):
    S = u.shape[0]

    def body(u_ref, cw_ref, cb_ref, out_ref):
        rows = _iota((S, LANE), 0)
        out_ref[...] = _silu(_conv_fwd(u_ref[...], cw_ref, cb_ref, rows))

    return pl.pallas_call(
        body, name="ssdconv_fwd", grid=(SSD_CONV // LANE,),
        in_specs=[pl.BlockSpec((S, LANE), lambda t: (0, OFF_XBC // LANE + t)), pl.BlockSpec((4, LANE), lambda t: (0, t)),
                  pl.BlockSpec((1, LANE), lambda t: (0, t))],
        out_specs=pl.BlockSpec((S, LANE), lambda t: (0, t)),
        out_shape=SDS((S, SSD_CONV), F32),
        compiler_params=_cp(("parallel",)),
    )(u, cw, cb)


def _ssdconv_bwd(u, cw, cb, dxbc, du):
    S = u.shape[0]

    def body(u_ref, cw_ref, cb_ref, d_ref, du_in, du_ref, red_ref):
        del du_in
        rows = _iota((S, LANE), 0)
        x = u_ref[...]
        dco = d_ref[...] * _dsilu(_conv_fwd(x, cw_ref, cb_ref, rows))
        dx, dws, dcb = _conv_bwd(x, dco, cw_ref, rows)
        du_ref[...] = dx
        for n, p in enumerate(dws + [dcb]):
            red_ref[pl.ds(n, 1), :] = p
        red_ref[pl.ds(5, 3), :] = jnp.zeros((3, LANE), F32)

    ucol = pl.BlockSpec((S, LANE), lambda t: (0, OFF_XBC // LANE + t))
    return pl.pallas_call(
        body, name="ssdconv_bwd", grid=(SSD_CONV // LANE,),
        in_specs=[ucol, pl.BlockSpec((4, LANE), lambda t: (0, t)), pl.BlockSpec((1, LANE), lambda t: (0, t)),
                  pl.BlockSpec((S, LANE), lambda t: (0, t)), pl.BlockSpec(memory_space=pl.ANY)],
        out_specs=[ucol, pl.BlockSpec((8, LANE), lambda t: (0, t))],
        out_shape=[SDS((S, N_PAD), F32), SDS((8, SSD_CONV), F32)],
        input_output_aliases={4: 0},
        compiler_params=_cp(("parallel",)),
    )(u, cw, cb, dxbc, du)


def _ssd_consts():
    e64 = np.zeros((LANE, SSD_W), np.float32)
    e128 = np.zeros((LANE, SSD_HEADS * LANE), np.float32)
    for h in range(SSD_HEADS):
        e64[h, h * SSD_P:(h + 1) * SSD_P] = 1.0
        e128[h, h * LANE:(h + 1) * LANE] = 1.0
    T = SSD_CHUNK
    tril = (np.arange(T)[None, :] <= np.arange(T)[:, None]).astype(np.float32)
    return e64, e128, tril, tril.T.copy()


def _ssd_common(zdt, bias_ref, alog_ref, tril, e64, e128):
    T = SSD_CHUNK
    lane = _iota((1, LANE), 1)
    a_neg = jnp.where(lane < SSD_HEADS, -jnp.exp(alog_ref[...]), 0.0)
    dtpre = zdt[:, SSD_W:SSD_W + LANE] + bias_ref[...]
    dt = _softplus(dtpre)
    cum = _sel_l(tril, dt * a_neg)
    rowsT = _iota((T, LANE), 0)
    last = _last_row(cum, rowsT)
    ecum_x = _sel_r(jnp.exp(cum), e64)
    erem_x = _sel_r(jnp.exp(last - cum), e64)
    elast_x = _last_row(ecum_x, _iota((T, SSD_W), 0))
    dt_x = _sel_r(dt, e64)
    cum_e = _sel_r(cum, e128)
    return a_neg, dtpre, dt, cum, ecum_x, erem_x, elast_x, dt_x, cum_e


def _ssd_decay(cum_e, cumt_ref, h, causal):
    diff = cum_e[:, h * LANE:(h + 1) * LANE] - cumt_ref[pl.ds(h, 1), :]
    return jnp.exp(jnp.where(causal, diff, NEG))


def _group_norm_fwd(y1, nwv):
    outs, invs = [], []
    for g in range(2):
        seg = y1[:, g * 512:(g + 1) * 512]
        inv = lax.rsqrt(jnp.mean(seg * seg, axis=-1, keepdims=True) + EPS)
        outs.append(seg * inv * nwv[:, g * 512:(g + 1) * 512])
        invs.append(inv)
    return outs, invs


def _ssd_fwd(u, xbc, bias, alog, dskip_x, nw, consts, ycat):
    S = u.shape[0]
    T = SSD_CHUNK
    n = S // T
    e64, e128, tril, _ = consts

    def body(u_ref, xbc_ref, bias_ref, alog_ref, dx_ref, nw_ref, e64_ref, e128_ref, tril_ref, ycat_in,
             ycat_ref, y_ref, st_ref, st, cumt):
        del ycat_in

        @pl.when(pl.program_id(0) == 0)
        def _():
            st[...] = jnp.zeros_like(st)

        zdt = u_ref[...]
        z = zdt[:, 0:SSD_W]
        xs = xbc_ref[:, 0:SSD_W]
        _, _, _, cum, ecum_x, erem_x, elast_x, dt_x, cum_e = _ssd_common(
            zdt, bias_ref, alog_ref, tril_ref[...], e64_ref[...], e128_ref[...])
        cumt[...] = cum.T
        causal = _iota((T, T), 0) >= _iota((T, T), 1)
        lo = _iota((T, LANE), 1) < SSD_P
        xdt = xs * dt_x
        xrem = xdt * erem_x
        st_ref[...] = st[...]
        for g in range(2):
            gs = slice(g * 512, (g + 1) * 512)
            bg = _bf(xbc_ref[:, SSD_W + g * LANE:SSD_W + (g + 1) * LANE])
            cg = _bf(xbc_ref[:, SSD_W + 256 + g * LANE:SSD_W + 256 + (g + 1) * LANE])
            cb = _mm_nt(cg, bg)
            yin = _mm(cg, _bf(st[:, gs])) * ecum_x[:, gs]
            for j in range(4):
                h0 = 8 * g + 2 * j
                cs = slice(h0 * SSD_P, (h0 + 2) * SSD_P)
                xp = xdt[:, cs]
                s0 = _bf(cb * _ssd_decay(cum_e, cumt, h0, causal))
                s1 = _bf(cb * _ssd_decay(cum_e, cumt, h0 + 1, causal))
                y_ref[:, cs] = (_mm(s0, _bf(jnp.where(lo, xp, 0.0))) + _mm(s1, _bf(jnp.where(lo, 0.0, xp)))
                                + yin[:, j * LANE:(j + 1) * LANE])
            st[:, gs] = st[:, gs] * elast_x[:, gs] + _mm_tn(bg, _bf(xrem[:, gs]))
        y1 = (y_ref[...] + dx_ref[...] * xs) * _silu(z)
        outs, _ = _group_norm_fwd(y1, nw_ref[...])
        for g in range(2):
            ycat_ref[:, g * 512:(g + 1) * 512] = outs[g]

    return pl.pallas_call(
        body, name="ssd_fwd", grid=(n,),
        in_specs=[pl.BlockSpec((T, SSD_W + LANE), lambda i: (i, OFF_Z // (SSD_W + LANE))),
                  pl.BlockSpec((T, SSD_CONV), lambda i: (i, 0)), _vec(LANE), _vec(LANE), _vec(SSD_W), _vec(SSD_W),
                  _full(e64.shape), _full(e128.shape), _full(tril.shape), pl.BlockSpec(memory_space=pl.ANY)],
        out_specs=[pl.BlockSpec((T, SSD_W), lambda i: (i, 1)), pl.BlockSpec((T, SSD_W), lambda i: (i, 0)),
                   pl.BlockSpec((None, SSD_N, SSD_W), lambda i: (i, 0, 0))],
        out_shape=[SDS((S, D_INNER), F32), SDS((S, SSD_W), F32), SDS((n, SSD_N, SSD_W), F32)],
        scratch_shapes=[pltpu.VMEM((SSD_N, SSD_W), F32), pltpu.VMEM((LANE, T), F32)],
        input_output_aliases={9: 0},
        compiler_params=_cp(("arbitrary",)),
    )(u, xbc, bias, alog, dskip_x, nw, _bfc(e64), _bfc(e128), _bfc(tril), ycat)


def _ssd_bwd(u, xbc, bias, alog, dskip_x, nw, consts, y_ssd, states, dycat, du):
    S = u.shape[0]
    T = SSD_CHUNK
    n = S // T
    e64, e128, tril, triu = consts
    e64t = np.ascontiguousarray(e64.T)

    def body(u_ref, xbc_ref, bias_ref, alog_ref, dx_ref, nw_ref, e64_ref, e64t_ref, e128_ref, tril_ref, triu_ref,
             y_ref, st_ref, dy_ref, du_in, du_ref, dxbc_ref, red_ref, dst, dl_s, cumt, dxdt_s, dy0_s, gb_s, gc_s):
        del du_in

        @pl.when(pl.program_id(0) == 0)
        def _():
            dst[...] = jnp.zeros_like(dst)
            red_ref[...] = jnp.zeros_like(red_ref)

        zdt = u_ref[...]
        z = zdt[:, 0:SSD_W]
        xs = xbc_ref[:, 0:SSD_W]
        e64m = e64_ref[...]
        a_neg, dtpre, dt, cum, ecum_x, erem_x, elast_x, dt_x, cum_e = _ssd_common(
            zdt, bias_ref, alog_ref, tril_ref[...], e64m, e128_ref[...])
        cumt[...] = cum.T
        causal = _iota((T, T), 0) >= _iota((T, T), 1)
        lo = _iota((T, LANE), 1) < SSD_P
        xdt = xs * dt_x
        xrem = xdt * erem_x
        y = y_ref[...]
        dxv = dx_ref[...]
        nwv = nw_ref[...]
        sz = _silu(z)
        y0 = y + dxv * xs
        y1 = y0 * sz
        for g in range(2):
            gs = slice(g * 512, (g + 1) * 512)
            seg = y1[:, gs]
            inv = lax.rsqrt(jnp.mean(seg * seg, axis=-1, keepdims=True) + EPS)
            shat = seg * inv
            dyg = dy_ref[:, gs]
            red_ref[0:1, gs] += jnp.sum(dyg * shat, axis=0, keepdims=True)
            dsh = dyg * nwv[:, gs]
            dy1g = inv * (dsh - shat * jnp.mean(dsh * shat, axis=-1, keepdims=True))
            du_ref[:, gs] = dy1g * y0[:, gs] * _dsilu(z[:, gs])
            dy0_s[:, gs] = dy1g * sz[:, gs]
        dy0 = dy0_s[...]
        red_ref[1:2, :] += jnp.sum(dy0 * xs, axis=0, keepdims=True)
        dyin = dy0 * ecum_x
        lane = _iota((T, LANE), 1)
        ones = jnp.ones((T, LANE), BF16)
        dcum = jnp.zeros((T, LANE), F32)

        def row_minus_col(gm):
            hi = _bf(gm)
            lw = _bf(gm - hi.astype(F32))
            return _mm(hi, ones) + _mm(lw, ones) - _mm_tn(hi, ones) - _mm_tn(lw, ones)

        for g in range(2):
            gs = slice(g * 512, (g + 1) * 512)
            bg = _bf(xbc_ref[:, SSD_W + g * LANE:SSD_W + (g + 1) * LANE])
            cg = _bf(xbc_ref[:, SSD_W + 256 + g * LANE:SSD_W + 256 + (g + 1) * LANE])
            cb = _mm_nt(cg, bg)
            dst_f, st_f = dst[:, gs], st_ref[:, gs]
            dstg = _bf(dst_f)
            stg = _bf(st_f)
            dyin_g = _bf(dyin[:, gs])
            xrem_g = _bf(xrem[:, gs])
            dcb = jnp.zeros((T, T), F32)
            dxr = _mm(bg, dstg)
            dxdt_s[:, gs] = dxr * erem_x[:, gs]
            gc_s[:, gs] = dxr * xrem[:, gs]
            gb_s[:, gs] = dyin[:, gs] * _mm(cg, stg)
            dl_s[:, gs] = jnp.sum(dst_f * st_f, axis=0, keepdims=True) * elast_x[:, gs]
            for j in range(4):
                h0 = 8 * g + 2 * j
                cs = slice(h0 * SSD_P, (h0 + 2) * SSD_P)
                xp = xdt[:, cs]
                dyp = dy0[:, cs]
                x_lo, x_hi = _bf(jnp.where(lo, xp, 0.0)), _bf(jnp.where(lo, 0.0, xp))
                d_lo, d_hi = _bf(jnp.where(lo, dyp, 0.0)), _bf(jnp.where(lo, 0.0, dyp))
                s0 = cb * _ssd_decay(cum_e, cumt, h0, causal)
                s1 = cb * _ssd_decay(cum_e, cumt, h0 + 1, causal)
                ds0 = _mm_nt(d_lo, x_lo)
                ds1 = _mm_nt(d_hi, x_hi)
                dcb = dcb + ds0 * _ssd_decay(cum_e, cumt, h0, causal) + ds1 * _ssd_decay(cum_e, cumt, h0 + 1, causal)
                dxdt_s[:, cs] += _mm_tn(_bf(s0), d_lo) + _mm_tn(_bf(s1), d_hi)
                dcum = dcum + jnp.where(lane == h0, row_minus_col(ds0 * s0), 0.0)
                dcum = dcum + jnp.where(lane == h0 + 1, row_minus_col(ds1 * s1), 0.0)
            dcb_b = _bf(dcb)
            dxbc_ref[:, SSD_W + g * LANE:SSD_W + (g + 1) * LANE] = _mm_tn(dcb_b, cg) + _mm_nt(xrem_g, dstg)
            dxbc_ref[:, SSD_W + 256 + g * LANE:SSD_W + 256 + (g + 1) * LANE] = _mm(dcb_b, bg) + _mm_nt(dyin_g, stg)
            dst[:, gs] = dst_f * elast_x[:, gs] + _mm_tn(cg, dyin_g)
        dxdt = dxdt_s[...]
        dxbc_ref[:, 0:SSD_W] = dxdt * dt_x + dy0 * dxv
        e64t = e64t_ref[...]
        hc = _sel_r(gc_s[...], e64t)
        dlast = (jnp.sum(hc, axis=0, keepdims=True)
                 + jnp.max(_sel_r(jnp.broadcast_to(dl_s[...], (8, SSD_W)), e64t), axis=0, keepdims=True))
        dcum = dcum + _sel_r(gb_s[...], e64t) - hc + jnp.where(_iota((T, LANE), 0) == T - 1, dlast, 0.0)
        dda = _sel_l(triu_ref[...], dcum)
        ddt = dda * a_neg + _sel_r(dxdt * xs, e64t)
        ddtpre = ddt * _sigmoid(dtpre)
        du_ref[:, SSD_W:SSD_W + LANE] = jnp.where(lane < SSD_HEADS, ddtpre, 0.0)
        red_ref[2:3, 0:LANE] += jnp.sum(ddtpre, axis=0, keepdims=True)
        red_ref[3:4, 0:LANE] += jnp.sum(dda * dt, axis=0, keepdims=True)

    rev = lambda i: (n - 1 - i, 0)
    return pl.pallas_call(
        body, name="ssd_bwd", grid=(n,),
        in_specs=[pl.BlockSpec((T, SSD_W + LANE), lambda i: (n - 1 - i, OFF_Z // (SSD_W + LANE))),
                  pl.BlockSpec((T, SSD_CONV), rev), _vec(LANE), _vec(LANE), _vec(SSD_W), _vec(SSD_W),
                  _full(e64.shape), _full(e64t.shape), _full(e128.shape), _full(tril.shape), _full(triu.shape),
                  pl.BlockSpec((T, SSD_W), rev), pl.BlockSpec((None, SSD_N, SSD_W), lambda i: (n - 1 - i, 0, 0)),
                  pl.BlockSpec((T, SSD_W), lambda i: (n - 1 - i, 1)), pl.BlockSpec(memory_space=pl.ANY)],
        out_specs=[pl.BlockSpec((T, SSD_W + LANE), lambda i: (n - 1 - i, OFF_Z // (SSD_W + LANE))),
                   pl.BlockSpec((T, SSD_CONV), rev), pl.BlockSpec((8, SSD_W), lambda i: (0, 0))],
        out_shape=[SDS((S, N_PAD), F32), SDS((S, SSD_CONV), F32), SDS((8, SSD_W), F32)],
        scratch_shapes=[pltpu.VMEM((SSD_N, SSD_W), F32), pltpu.VMEM((1, SSD_W), F32), pltpu.VMEM((LANE, T), F32)]
        + [pltpu.VMEM((T, SSD_W), F32)] * 4,
        input_output_aliases={14: 0},
        compiler_params=_cp(("arbitrary",)),
    )(u, xbc, bias, alog, dskip_x, nw, _bfc(e64), _bfc(e64t), _bfc(e128), _bfc(tril), _bfc(triu), y_ssd, states, dycat, du)


def _bfc(a):
    return jnp.asarray(a, BF16)


def _outproj_fwd(ycat, wo, x, gate):
    S = x.shape[0]
    tm = min(512, S)

    def body(yc_ref, wo_ref, x_ref, g_ref, xn_ref, y_ref):
        y = _mm(_bf(yc_ref[...]), wo_ref[...])
        y_ref[...] = y
        xn_ref[...] = x_ref[...] + g_ref[...] * y

    row = pl.BlockSpec((tm, D_MODEL), lambda i: (i, 0))
    return pl.pallas_call(
        body, name="outproj_fwd", grid=(S // tm,),
        in_specs=[pl.BlockSpec((tm, D_INNER), lambda i: (i, 0)), _full((D_INNER, D_MODEL)), row, _vec(D_MODEL)],
        out_specs=[row, row],
        out_shape=[SDS((S, D_MODEL), F32), SDS((S, D_MODEL), F32)],
        compiler_params=_cp(("parallel",)),
    )(ycat, wo, x, gate)


def _outproj_bwd(dxn, y, gate, ycat, wo):
    S = dxn.shape[0]
    tm = min(512, S)

    def body(dx_ref, y_ref, g_ref, yc_ref, wo_ref, dyc_ref, gwo_ref, dg_ref, acc):
        @pl.when(pl.program_id(0) == 0)
        def _():
            acc[...] = jnp.zeros_like(acc)
            dg_ref[...] = jnp.zeros_like(dg_ref)

        dxv = dx_ref[...]
        dy = _bf(dxv * g_ref[...])
        dg_ref[0:1, :] += jnp.sum(dxv * y_ref[...], axis=0, keepdims=True)
        dyc_ref[...] = _mm_nt(dy, wo_ref[...])
        acc[...] += _mm_tn(_bf(yc_ref[...]), dy)

        @pl.when(pl.program_id(0) == pl.num_programs(0) - 1)
        def _():
            gwo_ref[...] = acc[...].astype(BF16)

    row = pl.BlockSpec((tm, D_MODEL), lambda i: (i, 0))
    wide = pl.BlockSpec((tm, D_INNER), lambda i: (i, 0))
    return pl.pallas_call(
        body, name="outproj_bwd", grid=(S // tm,),
        in_specs=[row, row, _vec(D_MODEL), wide, _full((D_INNER, D_MODEL))],
        out_specs=[wide, _full((D_INNER, D_MODEL)), _full((8, D_MODEL))],
        out_shape=[SDS((S, D_INNER), F32), SDS((D_INNER, D_MODEL), BF16), SDS((8, D_MODEL), F32)],
        scratch_shapes=[pltpu.VMEM((D_INNER, D_MODEL), F32)],
        compiler_params=_cp(("arbitrary",)),
    )(dxn, y, gate, ycat, wo)


def _loss_head(x, fw, target):
    S = x.shape[0]
    tm = min(512, S)

    def body(x_ref, fw_ref, t_ref, dx_ref, red_ref):
        @pl.when(pl.program_id(0) == 0)
        def _():
            red_ref[...] = jnp.zeros_like(red_ref)

        xv = x_ref[...]
        fwv = fw_ref[...]
        inv = lax.rsqrt(jnp.mean(xv * xv, axis=-1, keepdims=True) + EPS)
        xhat = xv * inv
        err = xhat * fwv - t_ref[...]
        col = jnp.sum(err * err, axis=0, keepdims=True)
        red_ref[1:2, :] += jnp.broadcast_to(jnp.sum(col, axis=1, keepdims=True) * (0.5 / D_MODEL), (1, D_MODEL))
        dy = err * (1.0 / D_MODEL)
        red_ref[0:1, :] += jnp.sum(dy * xhat, axis=0, keepdims=True)
        dxhat = dy * fwv
        dx_ref[...] = inv * (dxhat - xhat * jnp.mean(dxhat * xhat, axis=-1, keepdims=True))

    row = pl.BlockSpec((tm, D_MODEL), lambda i: (i, 0))
    return pl.pallas_call(
        body, name="loss_head", grid=(S // tm,),
        in_specs=[row, _vec(D_MODEL), row],
        out_specs=[row, _full((8, D_MODEL))],
        out_shape=[SDS((S, D_MODEL), F32), SDS((8, D_MODEL), F32)],
        compiler_params=_cp(("arbitrary",)),
    )(x, fw, target)


ADA_COLS = 3 * D_MODEL // N_DEV


def _ada_fwd(c_all, w_ada, b_cols):
    def body(c_ref, w_ref, b_ref, out_ref):
        out_ref[...] = _mm(_bf(_silu(c_ref[...])), _bf(w_ref[...])) + b_ref[...]

    return pl.pallas_call(
        body, name="ada_fwd", grid=(DEPTH,),
        in_specs=[_full((N_DEV, D_MODEL)), pl.BlockSpec((None, D_MODEL, ADA_COLS), lambda l: (l, 0, 0)),
                  pl.BlockSpec((None, 1, ADA_COLS), lambda l: (l, 0, 0))],
        out_specs=pl.BlockSpec((None, N_DEV, ADA_COLS), lambda l: (l, 0, 0)),
        out_shape=SDS((DEPTH, N_DEV, ADA_COLS), F32),
        compiler_params=_cp(("parallel",)),
    )(c_all, w_ada, b_cols)


def _ada_bwd(ct_pad, dmod_pad):
    def body(c_ref, d_ref, out_ref):
        out_ref[...] = _mm(_bf(_silu(c_ref[...])), _bf(d_ref[...]))

    return pl.pallas_call(
        body, name="ada_bwd", grid=(DEPTH,),
        in_specs=[_full((D_MODEL, LANE)), pl.BlockSpec((None, LANE, ADA_COLS), lambda l: (l, 0, 0))],
        out_specs=pl.BlockSpec((None, D_MODEL, ADA_COLS), lambda l: (l, 0, 0)),
        out_shape=SDS((DEPTH, D_MODEL, ADA_COLS), F32),
        compiler_params=_cp(("parallel",)),
    )(ct_pad, dmod_pad)


def _adamw(parts, w, m, v, name, own=None):
    n, R, C = parts.shape
    tr = R
    while tr * C * 4 > (1 << 20) and tr % 16 == 0:
        tr //= 2

    def body(*refs):
        p_ref, w_ref, m_ref, v_ref, g_ref, d_ref, mo_ref, vo_ref = refs[:1] + refs[-7:]

        def part(k):
            if own is None:
                return p_ref[k].astype(F32)
            me = 4 * lax.axis_index("x") + 2 * lax.axis_index("y") + lax.axis_index("c")
            return jnp.where(me == k, refs[1][...], p_ref[k]).astype(F32)

        g = part(0)
        for k in range(1, n):
            g = g + part(k)
        mn = ADAM_B1 * m_ref[...] + (1.0 - ADAM_B1) * g
        vn = ADAM_B2 * v_ref[...] + (1.0 - ADAM_B2) * (g * g)
        m_hat = mn / (1.0 - ADAM_B1 ** ADAM_STEP)
        v_hat = vn / (1.0 - ADAM_B2 ** ADAM_STEP)
        g_ref[...] = g
        d_ref[...] = -ADAM_LR * (m_hat / (jnp.sqrt(v_hat) + ADAM_EPS) + ADAM_WD * w_ref[...])
        mo_ref[...] = mn
        vo_ref[...] = vn

    blk = pl.BlockSpec((tr, C), lambda i: (i, 0))
    return pl.pallas_call(
        body, name=name, grid=(R // tr,),
        in_specs=[pl.BlockSpec((n, tr, C), lambda i: (0, i, 0))] + [blk] * (3 if own is None else 4),
        out_specs=[blk] * 4,
        out_shape=[SDS((R, C), F32)] * 4,
        compiler_params=_cp(("parallel",)),
    )(parts, *([] if own is None else [own]), w, m, v)


MESH = pl.DeviceIdType.MESH
ANY = pl.BlockSpec(memory_space=pl.ANY)


def _all_gather(v, name):
    def body(v_ref, out_ref, send_sems, recv_sems, local_sem):
        x, y, c = lax.axis_index("x"), lax.axis_index("y"), lax.axis_index("c")
        me, sibling = (x, y, c), (x, y, 1 - c)
        chips = [(1 - x, y), (x, 1 - y), (1 - x, 1 - y)]

        def slot(px, py, pc):
            return out_ref.at[4 * px + 2 * py + pc]

        def copy(k, block, to, src=None):
            return pltpu.make_async_remote_copy(
                src_ref=slot(*block) if src is None else src, dst_ref=slot(*block),
                send_sem=send_sems.at[k], recv_sem=recv_sems.at[k], device_id=to, device_id_type=MESH)

        mine = pltpu.make_async_copy(v_ref, slot(*me), local_sem)
        mine.start()
        first = [copy(0, me, sibling, src=v_ref)]
        first += [copy(1 + j, me, (*chip, c), src=v_ref) for j, chip in enumerate(chips)]
        for cp in first:
            cp.start()
        passed = [copy(4 + j, (*chip, c), sibling) for j, chip in enumerate(chips)]
        for j, chip in enumerate(chips):
            copy(1 + j, (*chip, c), me).wait_recv()
            passed[j].start()
        copy(0, sibling, me).wait_recv()
        for j, chip in enumerate(chips):
            copy(4 + j, (*chip, 1 - c), me).wait_recv()
        for cp in first + passed:
            cp.wait_send()
        mine.wait()

    return pl.pallas_call(
        body, name=name, in_specs=[ANY], out_specs=ANY,
        out_shape=SDS((N_DEV,) + v.shape, v.dtype),
        scratch_shapes=[pltpu.SemaphoreType.DMA((7,)), pltpu.SemaphoreType.DMA((7,)), pltpu.SemaphoreType.DMA],
    )(v)


def _all_to_all(v, name):
    def body(v_ref, out_ref, send_sems, recv_sems, local_sem):
        x, y, c = lax.axis_index("x"), lax.axis_index("y"), lax.axis_index("c")
        mine_idx = 4 * x + 2 * y + c
        mine = pltpu.make_async_copy(v_ref.at[mine_idx], out_ref.at[mine_idx], local_sem)
        mine.start()
        sends, recvs = [], []
        for k in range(1, N_DEV):
            px = 1 - x if k & 4 else x
            py = 1 - y if k & 2 else y
            pc = 1 - c if k & 1 else c
            peer_idx = 4 * px + 2 * py + pc
            sems = dict(send_sem=send_sems.at[k - 1], recv_sem=recv_sems.at[k - 1], device_id=(px, py, pc),
                        device_id_type=MESH)
            sends.append(pltpu.make_async_remote_copy(src_ref=v_ref.at[peer_idx], dst_ref=out_ref.at[mine_idx], **sems))
            recvs.append(pltpu.make_async_remote_copy(src_ref=v_ref.at[peer_idx], dst_ref=out_ref.at[peer_idx], **sems))
        for cp in sends:
            cp.start()
        for cp in recvs:
            cp.wait_recv()
        for cp in sends:
            cp.wait_send()
        mine.wait()

    return pl.pallas_call(
        body, name=name, in_specs=[ANY], out_specs=ANY,
        out_shape=SDS(v.shape, v.dtype),
        scratch_shapes=[pltpu.SemaphoreType.DMA((7,)), pltpu.SemaphoreType.DMA((7,)), pltpu.SemaphoreType.DMA],
    )(v)


HBM_SPEC = pl.BlockSpec(memory_space=pltpu.HBM)
SEM_SPEC = pl.BlockSpec(memory_space=pltpu.SEMAPHORE)
EFFECT = pltpu.SideEffectType.DATAFLOW_SIDE_EFFECTING


def _exchange_copies(srcs, lands, send_sems, recv_sems, scatter, layer):
    x, y, c = lax.axis_index("x"), lax.axis_index("y"), lax.axis_index("c")
    me = 4 * x + 2 * y + c
    copies = []
    for a, (src, land) in enumerate(zip(srcs, lands)):
        for k in range(1, N_DEV):
            px = 1 - x if k & 4 else x
            py = 1 - y if k & 2 else y
            pc = 1 - c if k & 1 else c
            n = 7 * a + k - 1
            copies.append(pltpu.make_async_remote_copy(
                src_ref=src.at[4 * px + 2 * py + pc] if scatter else src,
                dst_ref=land.at[me, layer] if scatter else land.at[me],
                send_sem=send_sems.at[n], recv_sem=recv_sems.at[n], device_id=(px, py, pc), device_id_type=MESH))
    return copies


def _exchange_start(name, srcs, lands, scatter, layer=0, after=None):
    n = len(srcs)

    def body(*refs):
        send_sems, recv_sems = refs[-2 * n - 3], refs[-2 * n - 2]
        for cp in _exchange_copies(refs[:n], refs[n:2 * n], send_sems, recv_sems, scatter, layer):
            cp.start()
        refs[-1][...] = jnp.zeros_like(refs[-1])

    arrays = list(srcs) + list(lands)
    sems = pltpu.SemaphoreType.DMA((7 * n,))
    out = pl.pallas_call(
        body, name=name,
        out_shape=(sems, sems, *[pltpu.HBM(v.shape, v.dtype) for v in arrays], SDS((8, LANE), F32)),
        in_specs=[HBM_SPEC] * (2 * n) + ([ANY] if after is not None else []),
        out_specs=(SEM_SPEC, SEM_SPEC, *[HBM_SPEC] * (2 * n), pl.BlockSpec(memory_space=pltpu.VMEM)),
        input_output_aliases={i: 2 + i for i in range(2 * n)},
        compiler_params=pltpu.CompilerParams(has_side_effects=EFFECT),
    )(*[pltpu.with_memory_space_constraint(v, pltpu.HBM) for v in arrays], *([after] if after is not None else []))
    return dict(sems=out[:2], srcs=out[2:2 + n], lands=out[2 + n:2 + 2 * n], token=out[-1][0, 0], scatter=scatter,
                layer=layer)


def _exchange_wait(name, st, after):
    n = len(st["srcs"])

    def body(*refs):
        send_sems, recv_sems = refs[2 * n], refs[2 * n + 1]
        for cp in _exchange_copies(refs[:n], refs[n:2 * n], send_sems, recv_sems, st["scatter"], st["layer"]):
            cp.wait_send()
            cp.wait_recv()

    arrays = list(st["srcs"]) + list(st["lands"])
    out = pl.pallas_call(
        body, name=name,
        out_shape=tuple(pltpu.HBM(v.shape, v.dtype) for v in arrays),
        in_specs=[HBM_SPEC] * (2 * n) + [SEM_SPEC, SEM_SPEC, ANY],
        out_specs=tuple([HBM_SPEC] * (2 * n)),
        input_output_aliases={i: i for i in range(2 * n)},
        compiler_params=pltpu.CompilerParams(has_side_effects=EFFECT),
    )(*arrays, *st["sems"], after)
    return out[n:]


_IN_PIECES = ([(1024, 3072)]
              + [r for t in range(4) for r in ((LANE * t, LANE * (t + 1)), (512 + LANE * t, 512 + LANE * (t + 1)))]
              + [(4096, 5632), (3072, 4096), (5632, 5648)])


def _permute_in(w):
    pad = jnp.zeros(w.shape[:-1] + (N_PAD - N_IN,), w.dtype)
    return jnp.concatenate([w[..., a:b] for a, b in _IN_PIECES] + [pad], axis=-1)


def _unpermute_in(g):
    ax = [g[..., OFF_LRU + 2 * LANE * t:OFF_LRU + 2 * LANE * t + LANE] for t in range(4)]
    ag = [g[..., OFF_LRU + 2 * LANE * t + LANE:OFF_LRU + 2 * LANE * (t + 1)] for t in range(4)]
    return jnp.concatenate(ax + ag + [g[..., 0:2048], g[..., OFF_Z:OFF_Z + SSD_W], g[..., OFF_XBC:OFF_XBC + SSD_CONV],
                                      g[..., OFF_Z + SSD_W:OFF_Z + SSD_W + SSD_HEADS]], axis=-1)


SHARD_COLS = N_IN // N_DEV


def _in_segments():
    segs, pos = [], 0
    for a, b in _IN_PIECES:
        for i in range(N_DEV):
            lo, hi = max(a, SHARD_COLS * i), min(b, SHARD_COLS * (i + 1))
            if lo < hi:
                segs.append((i, lo - SHARD_COLS * i, hi - lo, pos + lo - a))
        pos += b - a
    return segs


RELAYOUT_ROWS = 256


def _relayout_in(land, own):
    def body(land_ref, own_ref, out_ref):
        me = 4 * lax.axis_index("x") + 2 * lax.axis_index("y") + lax.axis_index("c")
        out_ref[:, N_IN:N_PAD] = jnp.zeros((RELAYOUT_ROWS, N_PAD - N_IN), BF16)
        for i, j, wd, p in _in_segments():
            out_ref[:, p:p + wd] = jnp.where(me == i, own_ref[:, j:j + wd], land_ref[i, :, j:j + wd])

    return pl.pallas_call(
        body, name="relayout_in", grid=(D_MODEL // RELAYOUT_ROWS,),
        in_specs=[pl.BlockSpec((N_DEV, RELAYOUT_ROWS, SHARD_COLS), lambda r: (0, r, 0)),
                  pl.BlockSpec((RELAYOUT_ROWS, SHARD_COLS), lambda r: (r, 0))],
        out_specs=pl.BlockSpec((RELAYOUT_ROWS, N_PAD), lambda r: (r, 0)),
        out_shape=SDS((D_MODEL, N_PAD), BF16),
        compiler_params=_cp(("parallel",)),
    )(land, own)


def _relayout_grad(g):
    def body(g_ref, out_ref):
        for i, j, wd, p in _in_segments():
            out_ref[i, :, j:j + wd] = g_ref[:, p:p + wd].astype(BF16)

    return pl.pallas_call(
        body, name="relayout_grad", grid=(D_MODEL // RELAYOUT_ROWS,),
        in_specs=[pl.BlockSpec((RELAYOUT_ROWS, N_PAD), lambda r: (r, 0))],
        out_specs=pl.BlockSpec((N_DEV, RELAYOUT_ROWS, SHARD_COLS), lambda r: (0, r, 0)),
        out_shape=SDS((N_DEV, D_MODEL, SHARD_COLS), BF16),
        compiler_params=_cp(("parallel",)),
    )(g)


def _block_diag(w):
    w4 = w.reshape(4, 2, 64, 64)
    z = jnp.zeros((4, 64, 64), w.dtype)
    top = jnp.concatenate([w4[:, 0], z], axis=-1)
    bot = jnp.concatenate([z, w4[:, 1]], axis=-1)
    return jnp.concatenate([top, bot], axis=1).astype(BF16)


def _diag_blocks(g):
    return jnp.stack([g[:, :64, :64], g[:, 64:, 64:]], axis=1).reshape(8, 64, 64)


def _pad_lanes(v):
    return jnp.pad(v, (0, LANE - v.shape[0]))[None, :]


def _lower_bounds(logits):
    p = jax.nn.softmax(logits, axis=0)
    return p, jnp.cumsum(p, axis=0) - p[0]


def _lower_bounds_bwd(p, dlb):
    dp = jnp.cumsum(dlb[::-1], axis=0)[::-1]
    dp = dp.at[0].add(-jnp.sum(dlb, axis=0))
    return p * (dp - jnp.sum(dp * p, axis=0, keepdims=True))


SMALL = ["norm_w", "b_ada", "lru_conv_b", "lru_wa", "lru_ba", "lru_wx", "lru_bx", "lru_lambda", "hg_lb_logits",
         "hg_norm_w", "ssd_conv_b", "ssd_dt_bias", "ssd_a_log", "ssd_d", "ssd_norm_w", "final_norm_w"]
WEIGHTS = ["norm_w", "w_ada", "b_ada", "w_in", "lru_conv_w", "lru_conv_b", "lru_wa", "lru_ba", "lru_wx", "lru_bx",
           "lru_lambda", "hg_lb_logits", "hg_norm_w", "ssd_conv_w", "ssd_conv_b", "ssd_dt_bias", "ssd_a_log", "ssd_d",
           "ssd_norm_w", "w_out", "final_norm_w"]
INPUTS = ["x", "c"] + WEIGHTS + ["loss_target"] + ["m_" + n for n in WEIGHTS] + ["v_" + n for n in WEIGHTS]
SMALL_ROW = 1024


def _flatten_small(d, prefix=""):
    flat = jnp.concatenate([d[prefix + n].reshape(-1) for n in SMALL])
    return jnp.pad(flat, (0, -flat.shape[0] % (8 * SMALL_ROW)))


def _split_small(flat, like):
    out, off = {}, 0
    for n in SMALL:
        size = int(np.prod(like[n].shape))
        out[n] = flat[off:off + size].reshape(like[n].shape)
        off += size
    return out


def _local_step(x, mod, target, w, fetch, emit):
    S = x.shape[0]
    mall = _bfc(_hg_consts())
    mall_t = _bfc(_hg_consts().T)
    consts = _ssd_consts()
    p_lb, lbs = _lower_bounds(w["hg_lb_logits"])
    saved = []
    for l in range(DEPTH):
        w_in_l, w_out_l, token = fetch(l, x)
        shift, scale, gate = (mod[l:l + 1, k * D_MODEL:(k + 1) * D_MODEL] for k in range(3))
        shift = shift + token
        prm = dict(
            nw=w["norm_w"][l:l + 1], cw=w["lru_conv_w"][l], cb=w["lru_conv_b"][l:l + 1],
            wa=_block_diag(w["lru_wa"][l]), ba=w["lru_ba"][l].reshape(1, LRU_W),
            wx=_block_diag(w["lru_wx"][l]), bx=w["lru_bx"][l].reshape(1, LRU_W), lam=w["lru_lambda"][l:l + 1],
            lb=lbs[l:l + 1], hnw=w["hg_norm_w"][l:l + 1], scw=w["ssd_conv_w"][l], scb=w["ssd_conv_b"][l:l + 1],
            bias=_pad_lanes(w["ssd_dt_bias"][l]), alog=_pad_lanes(w["ssd_a_log"][l]),
            dskip=jnp.repeat(w["ssd_d"][l], SSD_P)[None, :], snw=w["ssd_norm_w"][l:l + 1],
            w_in=w_in_l, w_out=w_out_l, scale=scale, gate=gate)
        u, h = _inproj_fwd(x, prm["nw"], scale, shift, prm["w_in"])
        ycat = lax.empty((S, D_INNER), F32)
        lru_args = (u, prm["cw"], prm["cb"], prm["wa"], prm["ba"], prm["wx"], prm["bx"], prm["lam"])
        ycat, h_lru = _lru_fwd(*lru_args, ycat)
        ycat, o_b, hg_st = _hg_fwd(u, prm["lb"], prm["hnw"], mall, ycat)
        xbc = _ssdconv_fwd(u, prm["scw"], prm["scb"])
        ssd_args = (u, xbc, prm["bias"], prm["alog"], prm["dskip"], prm["snw"], consts)
        ycat, y_ssd, ssd_st = _ssd_fwd(*ssd_args, ycat)
        x_new, y = _outproj_fwd(ycat, prm["w_out"], x, gate)
        saved.append((prm, x, u, h, ycat, lru_args, h_lru, o_b, hg_st, ssd_args, y_ssd, ssd_st, y))
        x = x_new
    dx, red = _loss_head(x, w["final_norm_w"][None, :], target)
    loss = red[1, 0]
    g = {n: [None] * DEPTH for n in WEIGHTS}
    g["final_norm_w"] = red[0]
    dmod, dlb = [None] * DEPTH, [None] * DEPTH
    token = 0.0
    for l in reversed(range(DEPTH)):
        prm, x, u, h, ycat, lru_args, h_lru, o_b, hg_st, ssd_args, y_ssd, ssd_st, y = saved[l]
        dycat, g_out, dgate = _outproj_bwd(dx, y, prm["gate"] + token, ycat, prm["w_out"])
        du = lax.empty((S, N_PAD), F32)
        du, dxbc, sred = _ssd_bwd(*ssd_args, y_ssd, ssd_st, dycat, du)
        du, cred = _ssdconv_bwd(u, prm["scw"], prm["scb"], dxbc, du)
        du, hred = _hg_bwd(u, prm["lb"], prm["hnw"], mall, mall_t, o_b, hg_st, dycat, du)
        du, lred, gwa, gwx = _lru_bwd(*lru_args, h_lru, dycat, du)
        dx, ired = _inproj_bwd_x(du, prm["w_in"], x, prm["nw"], prm["scale"], dx)
        token = emit(l, _inproj_bwd_w(h, du), g_out)
        g["norm_w"][l] = ired[2]
        dmod[l] = jnp.concatenate([ired[0], ired[1], dgate[0]])
        g["lru_conv_w"][l], g["lru_conv_b"][l] = lred[0:4], lred[4]
        g["lru_ba"][l], g["lru_bx"][l], g["lru_lambda"][l] = lred[5].reshape(8, 64), lred[6].reshape(8, 64), lred[7]
        g["lru_wa"][l], g["lru_wx"][l] = _diag_blocks(gwa), _diag_blocks(gwx)
        g["hg_norm_w"][l], dlb[l] = hred[0], hred[1]
        g["ssd_conv_w"][l], g["ssd_conv_b"][l] = cred[0:4], cred[4]
        g["ssd_norm_w"][l] = sred[0]
        g["ssd_d"][l] = sred[1].reshape(SSD_HEADS, SSD_P).sum(-1)
        g["ssd_dt_bias"][l] = sred[2, :SSD_HEADS]
        g["ssd_a_log"][l] = -sred[3, :SSD_HEADS] * jnp.exp(w["ssd_a_log"][l])
    g["hg_lb_logits"] = _lower_bounds_bwd(p_lb, jnp.stack(dlb))
    for n in WEIGHTS:
        if isinstance(g[n], list) and g[n][0] is not None:
            g[n] = jnp.stack(g[n])
    return loss, dx, jnp.stack(dmod), g


def kernel(x, c, norm_w, w_ada, b_ada, w_in, lru_conv_w, lru_conv_b, lru_wa, lru_ba, lru_wx, lru_bx, lru_lambda, hg_lb_logits, hg_norm_w, ssd_conv_w, ssd_conv_b, ssd_dt_bias, ssd_a_log, ssd_d, ssd_norm_w, w_out, final_norm_w, loss_target, m_norm_w, m_w_ada, m_b_ada, m_w_in, m_lru_conv_w, m_lru_conv_b, m_lru_wa, m_lru_ba, m_lru_wx, m_lru_bx, m_lru_lambda, m_hg_lb_logits, m_hg_norm_w, m_ssd_conv_w, m_ssd_conv_b, m_ssd_dt_bias, m_ssd_a_log, m_ssd_d, m_ssd_norm_w, m_w_out, m_final_norm_w, v_norm_w, v_w_ada, v_b_ada, v_w_in, v_lru_conv_w, v_lru_conv_b, v_lru_wa, v_lru_ba, v_lru_wx, v_lru_bx, v_lru_lambda, v_hg_lb_logits, v_hg_norm_w, v_ssd_conv_w, v_ssd_conv_b, v_ssd_dt_bias, v_ssd_a_log, v_ssd_d, v_ssd_norm_w, v_w_out, v_final_norm_w):
    return _step(x, c, norm_w, w_ada, b_ada, w_in, lru_conv_w, lru_conv_b, lru_wa, lru_ba, lru_wx, lru_bx, lru_lambda, hg_lb_logits, hg_norm_w, ssd_conv_w, ssd_conv_b, ssd_dt_bias, ssd_a_log, ssd_d, ssd_norm_w, w_out, final_norm_w, loss_target, m_norm_w, m_w_ada, m_b_ada, m_w_in, m_lru_conv_w, m_lru_conv_b, m_lru_wa, m_lru_ba, m_lru_wx, m_lru_bx, m_lru_lambda, m_hg_lb_logits, m_hg_norm_w, m_ssd_conv_w, m_ssd_conv_b, m_ssd_dt_bias, m_ssd_a_log, m_ssd_d, m_ssd_norm_w, m_w_out, m_final_norm_w, v_norm_w, v_w_ada, v_b_ada, v_w_in, v_lru_conv_w, v_lru_conv_b, v_lru_wa, v_lru_ba, v_lru_wx, v_lru_bx, v_lru_lambda, v_hg_lb_logits, v_hg_norm_w, v_ssd_conv_w, v_ssd_conv_b, v_ssd_dt_bias, v_ssd_a_log, v_ssd_d, v_ssd_norm_w, v_w_out, v_final_norm_w)


def _step(*args):
    a = dict(zip(INPUTS, args, strict=True))
    me = 4 * lax.axis_index("x") + 2 * lax.axis_index("y") + lax.axis_index("c")
    x, target = a["x"][0], a["loss_target"][0]

    c_all = _all_gather(a["c"], "gather_c")[:, 0, :]
    b_cols = lax.dynamic_slice_in_dim(a["b_ada"], me * ADA_COLS, ADA_COLS, axis=1)[:, None, :]
    mod_parts = _all_gather(_ada_fwd(c_all, a["w_ada"], b_cols), "gather_mod")
    mod = lax.dynamic_index_in_dim(mod_parts, me, axis=2, keepdims=False)
    mod = mod.transpose(1, 0, 2).reshape(DEPTH, 3 * D_MODEL)

    w = {n: a[n] for n in SMALL}
    conv = _all_gather(jnp.concatenate([a["lru_conv_w"], a["ssd_conv_w"]], axis=-1), "gather_conv")
    conv = conv.transpose(1, 2, 0, 3)
    w["lru_conv_w"] = conv[..., :64].reshape(DEPTH, 4, LRU_W)
    w["ssd_conv_w"] = conv[..., 64:].reshape(DEPTH, 4, SSD_CONV)

    w_in_b, w_out_b = a["w_in"].astype(BF16), a["w_out"].astype(BF16)
    cols, rows_out = N_IN // N_DEV, D_INNER // N_DEV

    def gather_start(l, after=None):
        lands = [lax.empty((N_DEV, D_MODEL, cols), BF16), lax.empty((N_DEV, rows_out, D_MODEL), BF16)]
        return _exchange_start(f"gather_start_{l}", [w_in_b[l], w_out_b[l]], lands, False, after=after)

    gathers = {0: gather_start(0), 1: gather_start(1)}

    def fetch(l, x_l):
        if l >= 1 and l + 1 < DEPTH:
            gathers[l + 1] = gather_start(l + 1, after=x_l)
        land_in, land_out = _exchange_wait(f"gather_wait_{l}", gathers[l], x_l)
        land_out = lax.dynamic_update_index_in_dim(land_out, w_out_b[l], me, 0)
        token = gathers[l + 1]["token"] if l + 1 < DEPTH else 0.0
        return _relayout_in(land_in, w_in_b[l]), land_out.reshape(D_INNER, D_MODEL), token

    scatters = {}
    lands = [lax.empty((N_DEV, DEPTH, D_MODEL, cols), BF16), lax.empty((N_DEV, DEPTH, rows_out, D_MODEL), BF16)]
    own_in, own_out = [None] * DEPTH, [None] * DEPTH

    def emit(l, g_in, g_out):
        g_in = _relayout_grad(g_in)
        g_out = g_out.reshape(N_DEV, rows_out, D_MODEL)
        own_in[l] = lax.dynamic_index_in_dim(g_in, me, 0, keepdims=False)
        own_out[l] = lax.dynamic_index_in_dim(g_out, me, 0, keepdims=False)
        scatters[l] = _exchange_start(f"scatter_start_{l}", [g_in, g_out], lands, True, layer=l)
        lands[:] = scatters[l]["lands"]
        return scatters[l]["token"]

    loss, dx, dmod, g = _local_step(x, mod, target, w, fetch, emit)
    loss = lax.psum(loss, ("x", "y", "c"))

    g["b_ada"] = dmod
    small = _all_gather(_flatten_small(g).reshape(-1, SMALL_ROW), "gather_small")
    rows = small.shape[1]
    fl = lambda prefix: _flatten_small(a, prefix).reshape(rows, SMALL_ROW)
    outs = _adamw(small, fl(""), fl("m_"), fl("v_"), "adamw_small")
    res = [_split_small(o.reshape(-1), a) for o in outs]

    off = DEPTH * D_MODEL
    dmod_all = small.reshape(N_DEV, -1)[:, off:off + DEPTH * 3 * D_MODEL]
    dmod_all = dmod_all.reshape(N_DEV, DEPTH, 3 * D_MODEL).transpose(1, 0, 2)
    dmod_cols = lax.dynamic_slice_in_dim(dmod_all, me * ADA_COLS, ADA_COLS, axis=2)
    dmod_pad = jnp.pad(dmod_cols, ((0, 0), (0, LANE - N_DEV), (0, 0)))
    ct_pad = jnp.pad(c_all.T, ((0, 0), (0, LANE - N_DEV)))
    g_ada = _ada_bwd(ct_pad, dmod_pad)

    def sharded(name, parts, own=None):
        shape = a[name].shape
        r2 = lambda t: t.reshape(-1, shape[-1])
        o = _adamw(parts.reshape(parts.shape[0], -1, shape[-1]), r2(a[name]), r2(a["m_" + name]), r2(a["v_" + name]),
                   "adamw_" + name, own=None if own is None else r2(own))
        return [t.reshape(shape) for t in o]

    big = {"w_ada": sharded("w_ada", g_ada[None])}
    g_conv = jnp.concatenate([g["lru_conv_w"].reshape(DEPTH, 4, N_DEV, 64), g["ssd_conv_w"].reshape(DEPTH, 4, N_DEV, 192)],
                             axis=-1).transpose(2, 0, 1, 3)
    conv_parts = _all_to_all(g_conv, "scatter_conv")
    big["lru_conv_w"] = sharded("lru_conv_w", conv_parts[..., :64])
    big["ssd_conv_w"] = sharded("ssd_conv_w", conv_parts[..., 64:])

    after = big["w_ada"][1]
    for l in reversed(range(DEPTH)):
        scatters[l]["lands"] = lands
        lands[:] = _exchange_wait(f"scatter_wait_{l}", scatters[l], after)
    big["w_in"] = sharded("w_in", lands[0], jnp.stack(own_in))
    big["w_out"] = sharded("w_out", lands[1], jnp.stack(own_out))

    out = [loss, dx[None]]
    for k in range(4):
        out += [big[n][k] if n in big else res[k][n] for n in WEIGHTS]
    return tuple(out)
```

```python
import functools

import numpy as np
import jax
import jax.numpy as jnp
from jax import lax
from jax.experimental import pallas as pl
from jax.experimental.pallas import tpu as pltpu

F32 = jnp.float32
BF16 = jnp.bfloat16
SDS = jax.ShapeDtypeStruct

N_DEV = 8
DEPTH = 4
D_MODEL = 1024
D_INNER = 2048
EPS = 1e-6
LRU_W = 512
LRU_C = 8.0
HG_W = 512
HG_CHUNK = 64
HG_HEADS = 4
SSD_W = 1024
SSD_HEADS = 16
SSD_P = 64
SSD_N = 128
SSD_CHUNK = 128
SSD_CONV = 1536
N_IN = 5648
N_PAD = 5760
OFF_HG = 0
OFF_LRU = 2048
OFF_XBC = 3072
OFF_Z = 4608
LANE = 128
VMEM_LIMIT = 56 * 1024 * 1024
NEG = -1e30

ADAM_LR = 0.001
ADAM_B1 = 0.9
ADAM_B2 = 0.999
ADAM_EPS = 1e-08
ADAM_WD = 0.01
ADAM_STEP = 10


def _cp(sem=None):
    return pltpu.CompilerParams(dimension_semantics=sem, vmem_limit_bytes=VMEM_LIMIT)


def _dg(a, b, ca, cb):
    return lax.dot_general(a, b, (((ca,), (cb,)), ((), ())), preferred_element_type=F32)


def _mm(a, b):
    return _dg(a, b, 1, 0)


def _mm_nt(a, b):
    return _dg(a, b, 1, 1)


def _mm_tn(a, b):
    return _dg(a, b, 0, 0)


def _bf(x):
    return x.astype(BF16)


def _split3(x):
    hi = x.astype(BF16)
    r = x - hi.astype(F32)
    mid = r.astype(BF16)
    lo = (r - mid.astype(F32)).astype(BF16)
    return hi, mid, lo


def _sel_r(x, m):
    hi, mid, lo = _split3(x)
    return _mm(hi, m) + _mm(mid, m) + _mm(lo, m)


def _sel_l(m, x):
    hi, mid, lo = _split3(x)
    return _mm(m, hi) + _mm(m, mid) + _mm(m, lo)


def _sel_tn(x, m):
    hi, mid, lo = _split3(x)
    return _mm_tn(hi, m) + _mm_tn(mid, m) + _mm_tn(lo, m)


def _sigmoid(x):
    return 1.0 / (1.0 + jnp.exp(-x))


def _silu(x):
    return x * _sigmoid(x)


def _dsilu(x):
    s = _sigmoid(x)
    return s * (1.0 + x * (1.0 - s))


def _softplus(x):
    return jnp.maximum(x, 0.0) + jnp.log(1.0 + jnp.exp(-jnp.abs(x)))


def _expm1(z):
    series = z * (1.0 + z * (1.0 / 2) * (1.0 + z * (1.0 / 3) * (1.0 + z * (1.0 / 4) * (
        1.0 + z * (1.0 / 5) * (1.0 + z * (1.0 / 6) * (1.0 + z * (1.0 / 7)))))))
    return jnp.where(jnp.abs(z) < 0.3, series, jnp.exp(z) - 1.0)


def _iota(shape, dim):
    return lax.broadcasted_iota(jnp.int32, shape, dim)


def _last_row(x, rows):
    return jnp.sum(jnp.where(rows == x.shape[0] - 1, x, 0.0), axis=0, keepdims=True)


def _shift_down(x, d, rows, fill=0.0):
    return jnp.where(rows >= d, pltpu.roll(x, d, 0), fill)


def _shift_up(x, d, rows, fill=0.0):
    n = x.shape[0]
    return jnp.where(rows < n - d, pltpu.roll(x, n - d, 0), fill)


def _conv_fwd(x, cw_ref, cb_ref, rows):
    out = cb_ref[...] + cw_ref[pl.ds(3, 1), :] * x
    for k in range(3):
        out = out + cw_ref[pl.ds(k, 1), :] * _shift_down(x, 3 - k, rows)
    return out


def _conv_bwd(x, dco, cw_ref, rows):
    dx = cw_ref[pl.ds(3, 1), :] * dco
    dws = []
    for k in range(3):
        dx = dx + cw_ref[pl.ds(k, 1), :] * _shift_up(dco, 3 - k, rows)
        dws.append(jnp.sum(dco * _shift_down(x, 3 - k, rows), axis=0, keepdims=True))
    dws.append(jnp.sum(dco * x, axis=0, keepdims=True))
    return dx, dws, jnp.sum(dco, axis=0, keepdims=True)


def _vec(n):
    return pl.BlockSpec((1, n), lambda *_: (0, 0))


def _full(shape):
    nd = len(shape)
    return pl.BlockSpec(shape, lambda *_: (0,) * nd)


def _inproj_fwd(x, nw, scale, shift, w):
    S = x.shape[0]
    tm, tn = min(512, S), 640

    def body(x_ref, nw_ref, sc_ref, sh_ref, w_ref, u_ref, h_ref):
        @pl.when(pl.program_id(1) == 0)
        def _():
            xv = x_ref[...]
            inv = lax.rsqrt(jnp.mean(xv * xv, axis=-1, keepdims=True) + EPS)
            h = (xv * inv) * nw_ref[...] * (1.0 + sc_ref[...]) + sh_ref[...]
            h_ref[...] = h.astype(BF16)

        u_ref[...] = _mm(h_ref[...], w_ref[...])

    return pl.pallas_call(
        body, name="inproj_fwd", grid=(S // tm, N_PAD // tn),
        in_specs=[pl.BlockSpec((tm, D_MODEL), lambda i, j: (i, 0)), _vec(D_MODEL), _vec(D_MODEL), _vec(D_MODEL),
                  pl.BlockSpec((D_MODEL, tn), lambda i, j: (0, j))],
        out_specs=[pl.BlockSpec((tm, tn), lambda i, j: (i, j)), pl.BlockSpec((tm, D_MODEL), lambda i, j: (i, 0))],
        out_shape=[SDS((S, N_PAD), F32), SDS((S, D_MODEL), BF16)],
        compiler_params=_cp(("parallel", "arbitrary")),
    )(x, nw, scale, shift, w)


def _inproj_bwd_x(du, w, x, nw, scale, dxn):
    S = x.shape[0]
    tm, tk = min(512, S), 640
    nk = N_PAD // tk

    def body(du_ref, w_ref, x_ref, nw_ref, sc_ref, dxn_ref, dx_ref, red_ref, acc):
        i, k = pl.program_id(0), pl.program_id(1)

        @pl.when(k == 0)
        def _():
            acc[...] = jnp.zeros_like(acc)

        @pl.when((i == 0) & (k == 0))
        def _():
            red_ref[...] = jnp.zeros_like(red_ref)

        acc[...] += _mm_nt(_bf(du_ref[...]), w_ref[...])

        @pl.when(k == nk - 1)
        def _():
            dh = acc[...]
            xv = x_ref[...]
            inv = lax.rsqrt(jnp.mean(xv * xv, axis=-1, keepdims=True) + EPS)
            xhat = xv * inv
            nwv = nw_ref[...]
            g1 = 1.0 + sc_ref[...]
            dxhat = dh * nwv * g1
            dx = inv * (dxhat - xhat * jnp.mean(dxhat * xhat, axis=-1, keepdims=True))
            dx_ref[...] = dxn_ref[...] + dx
            red_ref[0:1, :] += jnp.sum(dh, axis=0, keepdims=True)
            red_ref[1:2, :] += jnp.sum(dh * xhat * nwv, axis=0, keepdims=True)
            red_ref[2:3, :] += jnp.sum(dh * xhat * g1, axis=0, keepdims=True)

    row = pl.BlockSpec((tm, D_MODEL), lambda i, k: (i, 0))
    return pl.pallas_call(
        body, name="inproj_bwd_x", grid=(S // tm, nk),
        in_specs=[pl.BlockSpec((tm, tk), lambda i, k: (i, k)), pl.BlockSpec((D_MODEL, tk), lambda i, k: (0, k)),
                  row, _vec(D_MODEL), _vec(D_MODEL), row],
        out_specs=[row, pl.BlockSpec((8, D_MODEL), lambda i, k: (0, 0))],
        out_shape=[SDS((S, D_MODEL), F32), SDS((8, D_MODEL), F32)],
        scratch_shapes=[pltpu.VMEM((tm, D_MODEL), F32)],
        compiler_params=_cp(("arbitrary", "arbitrary")),
    )(du, w, x, nw, scale, dxn)


def _inproj_bwd_w(h, du):
    S = h.shape[0]
    tn = 640

    def body(h_ref, du_ref, gw_ref):
        gw_ref[...] = _mm_tn(h_ref[...], _bf(du_ref[...]))

    return pl.pallas_call(
        body, name="inproj_bwd_w", grid=(N_PAD // tn,),
        in_specs=[_full((S, D_MODEL)), pl.BlockSpec((S, tn), lambda j: (0, j))],
        out_specs=pl.BlockSpec((D_MODEL, tn), lambda j: (0, j)),
        out_shape=SDS((D_MODEL, N_PAD), F32),
        compiler_params=_cp(("parallel",)),
    )(h, du)


def _scan_block(a, b, rows):
    d = 1
    while d < a.shape[0]:
        a_s = _shift_down(a, d, rows, 1.0)
        b_s = _shift_down(b, d, rows, 0.0)
        b = a * b_s + b
        a = a * a_s
        d *= 2
    return a, b


def _rscan_block(c, g, rows):
    d = 1
    while d < c.shape[0]:
        c_s = _shift_up(c, d, rows, 1.0)
        g_s = _shift_up(g, d, rows, 0.0)
        g = g + c * g_s
        c = c * c_s
        d *= 2
    return c, g


def _lru_gates(xa, wa_ref, ba_ref, wx_ref, bx_ref, lam_ref):
    sp = _softplus(-lam_ref[...])
    xb = _bf(xa)
    r = _sigmoid(_mm(xb, wa_ref[...]) + ba_ref[...])
    ig = _sigmoid(_mm(xb, wx_ref[...]) + bx_ref[...])
    la = -LRU_C * r * sp
    a = jnp.exp(la)
    mult = jnp.sqrt(-_expm1(2.0 * la))
    return sp, r, ig, la, a, mult


def _lru_specs(S):
    t128 = pl.BlockSpec((1, LANE), lambda t: (0, t))
    return [pl.BlockSpec((S, 2 * LANE), lambda t: (0, OFF_LRU // (2 * LANE) + t)),
            pl.BlockSpec((4, LANE), lambda t: (0, t)), t128,
            pl.BlockSpec((None, LANE, LANE), lambda t: (t, 0, 0)), t128,
            pl.BlockSpec((None, LANE, LANE), lambda t: (t, 0, 0)), t128, t128]


def _lru_fwd(u, cw, cb, wa, ba, wx, bx, lam, ycat):
    S = u.shape[0]
    tb = min(256, S)

    def body(u_ref, cw_ref, cb_ref, wa_ref, ba_ref, wx_ref, bx_ref, lam_ref, ycat_in, ycat_ref, h_ref, a_scr, b_scr):
        del ycat_in
        rows = _iota((S, LANE), 0)
        xa = _conv_fwd(u_ref[:, 0:LANE], cw_ref, cb_ref, rows)
        _, _, ig, _, a, mult = _lru_gates(xa, wa_ref, ba_ref, wx_ref, bx_ref, lam_ref)
        a_scr[...] = a
        b_scr[...] = mult * (ig * xa)
        rows_b = _iota((tb, LANE), 0)

        def blk(j, hprev):
            sl = pl.ds(pl.multiple_of(j * tb, tb), tb)
            acum, hloc = _scan_block(a_scr[sl, :], b_scr[sl, :], rows_b)
            hf = hloc + acum * hprev
            h_ref[sl, :] = hf
            return _last_row(hf, rows_b)

        lax.fori_loop(0, S // tb, blk, jnp.zeros((1, LANE), F32))
        ycat_ref[...] = _bf(h_ref[...] * _silu(u_ref[:, LANE:2 * LANE]))

    col = pl.BlockSpec((S, LANE), lambda t: (0, t))
    return pl.pallas_call(
        body, name="lru_fwd", grid=(LRU_W // LANE,),
        in_specs=_lru_specs(S) + [pl.BlockSpec(memory_space=pl.ANY)],
        out_specs=[col, col],
        out_shape=[SDS((S, D_INNER), BF16), SDS((S,LRU_W), F32)],
        scratch_shapes=[pltpu.VMEM((S, LANE), F32), pltpu.VMEM((S, LANE), F32)],
        input_output_aliases={8: 0},
        compiler_params=_cp(("parallel",)),
    )(u, cw, cb, wa, ba, wx, bx, lam, ycat)


def _lru_bwd(u, cw, cb, wa, ba, wx, bx, lam, h_lru, dycat, du):
    S = u.shape[0]
    tb = min(256, S)

    def body(u_ref, cw_ref, cb_ref, wa_ref, ba_ref, wx_ref, bx_ref, lam_ref, h_ref, dy_ref, du_in,
             du_ref, red_ref, gwa_ref, gwx_ref, c_scr, g_scr, l_scr):
        del du_in
        rows = _iota((S, LANE), 0)
        ax = u_ref[:, 0:LANE]
        ag = u_ref[:, LANE:2 * LANE]
        xa = _conv_fwd(ax, cw_ref, cb_ref, rows)
        sp, r, ig, la, a, mult = _lru_gates(xa, wa_ref, ba_ref, wx_ref, bx_ref, lam_ref)
        h = h_ref[...]
        dy = dy_ref[...]
        du_ref[:, LANE:2 * LANE] = _bf(dy * h * _dsilu(ag))
        c_scr[...] = _shift_up(a, 1, rows, 0.0)
        g_scr[...] = dy * _silu(ag)
        rows_b = _iota((tb, LANE), 0)
        nb = S // tb

        def blk(jj, lnext):
            j = nb - 1 - jj
            sl = pl.ds(pl.multiple_of(j * tb, tb), tb)
            ccum, lloc = _rscan_block(c_scr[sl, :], g_scr[sl, :], rows_b)
            lam_t = lloc + ccum * lnext
            l_scr[sl, :] = lam_t
            return jnp.sum(jnp.where(rows_b == 0, lam_t, 0.0), axis=0, keepdims=True)

        lax.fori_loop(0, nb, blk, jnp.zeros((1, LANE), F32))
        db = l_scr[...]
        da = db * _shift_down(h, 1, rows)
        dmult = db * ig * xa
        dig = db * mult * xa
        dxa = db * mult * ig
        dla = da * a - dmult * (a * a) / mult
        dr = -LRU_C * sp * dla
        dsp = jnp.sum(-LRU_C * r * dla, axis=0, keepdims=True)
        dlam = -dsp * _sigmoid(-lam_ref[...])
        dzr = dr * r * (1.0 - r)
        dzi = dig * ig * (1.0 - ig)
        dzr_b, dzi_b, xa_b = _bf(dzr), _bf(dzi), _bf(xa)
        dxa = dxa + _mm_nt(dzr_b, wa_ref[...]) + _mm_nt(dzi_b, wx_ref[...])
        gwa_ref[...] = _mm_tn(xa_b, dzr_b)
        gwx_ref[...] = _mm_tn(xa_b, dzi_b)
        dax, dws, dcb = _conv_bwd(ax, dxa, cw_ref, rows)
        du_ref[:, 0:LANE] = _bf(dax)
        parts = dws + [dcb, jnp.sum(dzr, axis=0, keepdims=True), jnp.sum(dzi, axis=0, keepdims=True), dlam]
        for n, p in enumerate(parts):
            red_ref[pl.ds(n, 1), :] = p

    col = pl.BlockSpec((S, LANE), lambda t: (0, t))
    gw = pl.BlockSpec((None, LANE, LANE), lambda t: (t, 0, 0))
    return pl.pallas_call(
        body, name="lru_bwd", grid=(LRU_W // LANE,),
        in_specs=_lru_specs(S) + [col, col, pl.BlockSpec(memory_space=pl.ANY)],
        out_specs=[pl.BlockSpec((S, 2 * LANE), lambda t: (0, OFF_LRU // (2 * LANE) + t)),
                   pl.BlockSpec((8, LANE), lambda t: (0, t)), gw, gw],
        out_shape=[SDS((S, N_PAD), BF16), SDS((8, LRU_W), F32), SDS((4, LANE, LANE), F32), SDS((4, LANE, LANE), F32)],
        scratch_shapes=[pltpu.VMEM((S, LANE), F32)] * 3,
        input_output_aliases={10: 0},
        compiler_params=_cp(("parallel",)),
    )(u, cw, cb, wa, ba, wx, bx, lam, h_lru, dycat, du)


HG_LEVELS = 6


def _hg_consts():
    C = HG_CHUNK
    t = np.arange(C)[:, None]
    r = np.arange(C)[None, :]
    mats = []
    for side in ("q", "k"):
        for l in range(HG_LEVELS):
            b = 1 << l
            upper = (t % (2 * b)) >= b
            anchor = (t // (2 * b)) * 2 * b + b - 1
            if side == "q":
                mats.append(upper & (r > anchor) & (r <= t))
            else:
                mats.append((~upper) & (r > t) & (r <= anchor))
    mats.append(r <= t)
    mats.append(r > t)
    return np.concatenate(mats, 0).astype(np.float32)


def _hg_factors(hf, lb, mall):
    s = _sigmoid(hf)
    f = lb + (1.0 - lb) * s
    lf = jnp.log(f)
    k = (1.0 - lb) * _sigmoid(-hf)
    e = jnp.exp(_sel_l(mall, lf))
    C = HG_CHUNK
    eq = [e[l * C:(l + 1) * C] for l in range(HG_LEVELS)]
    ek = [e[(HG_LEVELS + l) * C:(HG_LEVELS + l + 1) * C] for l in range(HG_LEVELS)]
    ecum = e[2 * HG_LEVELS * C:(2 * HG_LEVELS + 1) * C]
    erem = e[(2 * HG_LEVELS + 1) * C:(2 * HG_LEVELS + 2) * C]
    return s, f, k, eq, ek, ecum, erem


def _hg_masks():
    C = HG_CHUNK
    ri, ci = _iota((C, C), 0), _iota((C, C), 1)
    rr = _iota((C, LANE), 0)
    gm = [(lax.shift_right_logical(ri, l + 1) == lax.shift_right_logical(ci, l + 1)).astype(F32)
          for l in range(HG_LEVELS)]
    up = [(lax.shift_right_logical(rr, l) & 1) == 1 for l in range(HG_LEVELS)]
    eye = (ri == ci).astype(F32)
    return gm, up, eye, rr


def _hg_scores(qh, kh, eq, ek, sl, gm, up, eye):
    qs, ks = [], []
    p = _mm_nt(_bf(qh), _bf(kh)) * eye
    for l in range(HG_LEVELS):
        ql = jnp.where(up[l], qh * eq[l][:, sl], 0.0)
        kl = jnp.where(up[l], 0.0, kh * ek[l][:, sl])
        p = p + _mm_nt(_bf(ql), _bf(kl)) * gm[l]
        qs.append(ql)
        ks.append(kl)
    return p, qs, ks


def _hg_fwd(u, lb, nw, mall, ycat):
    S = u.shape[0]
    C = HG_CHUNK
    n = S // C

    def body(u_ref, lb_ref, nw_ref, mall_ref, ycat_in, ycat_ref, o_ref, st_ref, st):
        del ycat_in

        @pl.when(pl.program_id(0) == 0)
        def _():
            st[...] = jnp.zeros_like(st)

        q = _silu(u_ref[:, 0:512])
        v = u_ref[:, 1024:1536]
        _, _, k, eq, ek, ecum, erem = _hg_factors(u_ref[:, 512:1024], lb_ref[...], mall_ref[...])
        gm, up, eye, rr = _hg_masks()
        for h in range(HG_HEADS):
            sl = slice(h * LANE, (h + 1) * LANE)
            qh, kh, vh = q[:, sl], k[:, sl], _bf(v[:, sl])
            p, _, _ = _hg_scores(qh, kh, eq, ek, sl, gm, up, eye)
            sth = st[h]
            st_ref[h] = sth
            o_ref[:, sl] = _mm(_bf(p), vh) + _mm_nt(_bf(qh * ecum[:, sl]), _bf(sth))
            st[h] = sth * _last_row(ecum[:, sl], rr) + _mm_tn(vh, _bf(kh * erem[:, sl]))
        o = o_ref[...]
        inv = lax.rsqrt(jnp.mean(o * o, axis=-1, keepdims=True) + EPS)
        ycat_ref[...] = _bf((o * inv) * nw_ref[...] * _silu(u_ref[:, 1536:2048]))

    return pl.pallas_call(
        body, name="hg_fwd", grid=(n,),
        in_specs=[pl.BlockSpec((C, 2048), lambda i: (i, 0)), _vec(HG_W), _vec(HG_W), _full(mall.shape),
                  pl.BlockSpec(memory_space=pl.ANY)],
        out_specs=[pl.BlockSpec((C, HG_W), lambda i: (i, 1)), pl.BlockSpec((C, HG_W), lambda i: (i, 0)),
                   pl.BlockSpec((None, HG_HEADS, LANE, LANE), lambda i: (i, 0, 0, 0))],
        out_shape=[SDS((S, D_INNER), BF16), SDS((S,HG_W), F32), SDS((n, HG_HEADS, LANE, LANE), F32)],
        scratch_shapes=[pltpu.VMEM((HG_HEADS, LANE, LANE), F32)],
        input_output_aliases={4: 0},
        compiler_params=_cp(("arbitrary",)),
    )(u, lb, nw, mall, ycat)


def _hg_bwd(u, lb, nw, mall, mall_t, o_b, states, dycat, du):
    S = u.shape[0]
    C = HG_CHUNK
    n = S // C
    L2 = 2 * HG_LEVELS

    def body(u_ref, lb_ref, nw_ref, mall_ref, mallt_ref, o_ref, st_ref, dy_ref, du_in, du_ref, red_ref,
             dst, dlast_s, dq_s, dk_s, dex):
        del du_in

        @pl.when(pl.program_id(0) == 0)
        def _():
            dst[...] = jnp.zeros_like(dst)
            red_ref[...] = jnp.zeros_like(red_ref)

        lb = lb_ref[...]
        hq, hf, hg = u_ref[:, 0:512], u_ref[:, 512:1024], u_ref[:, 1536:2048]
        q = _silu(hq)
        v = u_ref[:, 1024:1536]
        s, f, k, eq, ek, ecum, erem = _hg_factors(hf, lb, mall_ref[...])
        gm, up, eye, rr = _hg_masks()
        o = o_ref[...]
        dy = dy_ref[...]
        inv = lax.rsqrt(jnp.mean(o * o, axis=-1, keepdims=True) + EPS)
        ohat = o * inv
        nwv = nw_ref[...]
        du_ref[:, 1536:2048] = _bf(dy * ohat * nwv * _dsilu(hg))
        dn = dy * _silu(hg)
        red_ref[0:1, :] += jnp.sum(dn * ohat, axis=0, keepdims=True)
        dohat = dn * nwv
        do = inv * (dohat - ohat * jnp.mean(dohat * ohat, axis=-1, keepdims=True))
        for h in range(HG_HEADS):
            sl = slice(h * LANE, (h + 1) * LANE)
            qh, kh, vh, doh = q[:, sl], k[:, sl], _bf(v[:, sl]), _bf(do[:, sl])
            p, qs, ks = _hg_scores(qh, kh, eq, ek, sl, gm, up, eye)
            st_f = st_ref[h]
            sth = _bf(st_f)
            dsth = dst[h]
            dsth_b = _bf(dsth)
            qt = qh * ecum[:, sl]
            kt = kh * erem[:, sl]
            elast = _last_row(ecum[:, sl], rr)
            dp = _mm_nt(doh, vh)
            du_ref[:, 1024 + h * LANE:1024 + (h + 1) * LANE] = _bf(_mm_tn(_bf(p), doh) + _mm_nt(_bf(kt), dsth_b))
            dpe = _bf(dp * eye)
            dqt = _mm(doh, sth)
            dkt = _mm(vh, dsth_b)
            dq = dqt * ecum[:, sl] + _mm(dpe, _bf(kh))
            dk = dkt * erem[:, sl] + _mm_tn(dpe, _bf(qh))
            dex[L2 * C:(L2 + 1) * C, sl] = dqt * qt
            dex[(L2 + 1) * C:(L2 + 2) * C, sl] = dkt * kt
            for l in range(HG_LEVELS):
                dpl = _bf(dp * gm[l])
                dql = _mm(dpl, _bf(ks[l]))
                dkl = _mm_tn(dpl, _bf(qs[l]))
                dq = dq + jnp.where(up[l], dql * eq[l][:, sl], 0.0)
                dk = dk + jnp.where(up[l], 0.0, dkl * ek[l][:, sl])
                dex[l * C:(l + 1) * C, sl] = dql * qs[l]
                dex[(HG_LEVELS + l) * C:(HG_LEVELS + l + 1) * C, sl] = dkl * ks[l]
            dlast_s[:, sl] = jnp.sum(dsth * st_f, axis=0, keepdims=True) * elast
            dst[h] = dsth * elast + _mm_tn(doh, _bf(qt))
            dq_s[:, sl] = dq
            dk_s[:, sl] = dk
        dq = dq_s[...]
        dk = dk_s[...]
        dlf = _sel_l(mallt_ref[...], dex[...]) + dlast_s[...]
        du_ref[:, 0:512] = _bf(dq * _dsilu(hq))
        t = (1.0 - s) * (dlf / f - dk)
        du_ref[:, 512:1024] = _bf((1.0 - lb) * s * t)
        red_ref[1:2, :] += jnp.sum(t, axis=0, keepdims=True)

    rev = lambda i: (n - 1 - i, 0)
    return pl.pallas_call(
        body, name="hg_bwd", grid=(n,),
        in_specs=[pl.BlockSpec((C, 2048), rev), _vec(HG_W), _vec(HG_W), _full(mall.shape), _full(mall_t.shape),
                  pl.BlockSpec((C, HG_W), rev),
                  pl.BlockSpec((None, HG_HEADS, LANE, LANE), lambda i: (n - 1 - i, 0, 0, 0)),
                  pl.BlockSpec((C, HG_W), lambda i: (n - 1 - i, 1)), pl.BlockSpec(memory_space=pl.ANY)],
        out_specs=[pl.BlockSpec((C, 2048), rev), pl.BlockSpec((8, HG_W), lambda i: (0, 0))],
        out_shape=[SDS((S, N_PAD), BF16), SDS((8, HG_W), F32)],
        scratch_shapes=[pltpu.VMEM((HG_HEADS, LANE, LANE), F32), pltpu.VMEM((1, HG_W), F32),
                        pltpu.VMEM((C, HG_W), F32), pltpu.VMEM((C, HG_W), F32), pltpu.VMEM(((L2 + 2) * C, HG_W), F32)],
        input_output_aliases={8: 0},
        compiler_params=_cp(("arbitrary",)),
    )(u, lb, nw, mall, mall_t, o_b, states, dycat, du)


def _ssdconv_fwd(u, cw, cb):
    S = u.shape[0]

    def body(u_ref, cw_ref, cb_ref, out_ref):
        rows = _iota((S, LANE), 0)
        out_ref[...] = _silu(_conv_fwd(u_ref[...], cw_ref, cb_ref, rows))

    return pl.pallas_call(
        body, name="ssdconv_fwd", grid=(SSD_CONV // LANE,),
        in_specs=[pl.BlockSpec((S, LANE), lambda t: (0, OFF_XBC // LANE + t)), pl.BlockSpec((4, LANE), lambda t: (0, t)),
                  pl.BlockSpec((1, LANE), lambda t: (0, t))],
        out_specs=pl.BlockSpec((S, LANE), lambda t: (0, t)),
        out_shape=SDS((S, SSD_CONV), F32),
        compiler_params=_cp(("parallel",)),
    )(u, cw, cb)


def _ssdconv_bwd(u, cw, cb, dxbc, du):
    S = u.shape[0]

    def body(u_ref, cw_ref, cb_ref, d_ref, du_in, du_ref, red_ref):
        del du_in
        rows = _iota((S, LANE), 0)
        x = u_ref[...]
        dco = d_ref[...] * _dsilu(_conv_fwd(x, cw_ref, cb_ref, rows))
        dx, dws, dcb = _conv_bwd(x, dco, cw_ref, rows)
        du_ref[...] = _bf(dx)
        for n, p in enumerate(dws + [dcb]):
            red_ref[pl.ds(n, 1), :] = p
        red_ref[pl.ds(5, 3), :] = jnp.zeros((3, LANE), F32)

    ucol = pl.BlockSpec((S, LANE), lambda t: (0, OFF_XBC // LANE + t))
    return pl.pallas_call(
        body, name="ssdconv_bwd", grid=(SSD_CONV // LANE,),
        in_specs=[ucol, pl.BlockSpec((4, LANE), lambda t: (0, t)), pl.BlockSpec((1, LANE), lambda t: (0, t)),
                  pl.BlockSpec((S, LANE), lambda t: (0, t)), pl.BlockSpec(memory_space=pl.ANY)],
        out_specs=[ucol, pl.BlockSpec((8, LANE), lambda t: (0, t))],
        out_shape=[SDS((S, N_PAD), BF16), SDS((8, SSD_CONV), F32)],
        input_output_aliases={4: 0},
        compiler_params=_cp(("parallel",)),
    )(u, cw, cb, dxbc, du)


def _ssd_consts():
    e64 = np.zeros((LANE, SSD_W), np.float32)
    e128 = np.zeros((LANE, SSD_HEADS * LANE), np.float32)
    for h in range(SSD_HEADS):
        e64[h, h * SSD_P:(h + 1) * SSD_P] = 1.0
        e128[h, h * LANE:(h + 1) * LANE] = 1.0
    T = SSD_CHUNK
    tril = (np.arange(T)[None, :] <= np.arange(T)[:, None]).astype(np.float32)
    return e64, e128, tril, tril.T.copy()


def _ssd_common(zdt, bias_ref, alog_ref, tril, e64, e128):
    T = SSD_CHUNK
    lane = _iota((1, LANE), 1)
    a_neg = jnp.where(lane < SSD_HEADS, -jnp.exp(alog_ref[...]), 0.0)
    dtpre = zdt[:, SSD_W:SSD_W + LANE] + bias_ref[...]
    dt = _softplus(dtpre)
    cum = _sel_l(tril, dt * a_neg)
    rowsT = _iota((T, LANE), 0)
    last = _last_row(cum, rowsT)
    ecum_x = _sel_r(jnp.exp(cum), e64)
    erem_x = _sel_r(jnp.exp(last - cum), e64)
    elast_x = _last_row(ecum_x, _iota((T, SSD_W), 0))
    dt_x = _sel_r(dt, e64)
    cum_e = _sel_r(cum, e128)
    return a_neg, dtpre, dt, cum, ecum_x, erem_x, elast_x, dt_x, cum_e


def _ssd_decay(cum_e, cumt_ref, h, causal):
    diff = cum_e[:, h * LANE:(h + 1) * LANE] - cumt_ref[pl.ds(h, 1), :]
    return jnp.exp(jnp.where(causal, diff, NEG))


def _group_norm_fwd(y1, nwv):
    outs, invs = [], []
    for g in range(2):
        seg = y1[:, g * 512:(g + 1) * 512]
        inv = lax.rsqrt(jnp.mean(seg * seg, axis=-1, keepdims=True) + EPS)
        outs.append(seg * inv * nwv[:, g * 512:(g + 1) * 512])
        invs.append(inv)
    return outs, invs


def _ssd_fwd(u, xbc, bias, alog, dskip_x, nw, consts, ycat):
    S = u.shape[0]
    T = SSD_CHUNK
    n = S // T
    e64, e128, tril, _ = consts

    def body(u_ref, xbc_ref, bias_ref, alog_ref, dx_ref, nw_ref, e64_ref, e128_ref, tril_ref, ycat_in,
             ycat_ref, y_ref, st_ref, st, cumt):
        del ycat_in

        @pl.when(pl.program_id(0) == 0)
        def _():
            st[...] = jnp.zeros_like(st)

        zdt = u_ref[...]
        z = zdt[:, 0:SSD_W]
        xs = xbc_ref[:, 0:SSD_W]
        _, _, _, cum, ecum_x, erem_x, elast_x, dt_x, cum_e = _ssd_common(
            zdt, bias_ref, alog_ref, tril_ref[...], e64_ref[...], e128_ref[...])
        cumt[...] = cum.T
        causal = _iota((T, T), 0) >= _iota((T, T), 1)
        lo = _iota((T, LANE), 1) < SSD_P
        xdt = xs * dt_x
        xrem = xdt * erem_x
        st_ref[...] = st[...]
        for g in range(2):
            gs = slice(g * 512, (g + 1) * 512)
            bg = _bf(xbc_ref[:, SSD_W + g * LANE:SSD_W + (g + 1) * LANE])
            cg = _bf(xbc_ref[:, SSD_W + 256 + g * LANE:SSD_W + 256 + (g + 1) * LANE])
            cb = _mm_nt(cg, bg)
            yin = _mm(cg, _bf(st[:, gs])) * ecum_x[:, gs]
            for j in range(4):
                h0 = 8 * g + 2 * j
                cs = slice(h0 * SSD_P, (h0 + 2) * SSD_P)
                xp = xdt[:, cs]
                s0 = _bf(cb * _ssd_decay(cum_e, cumt, h0, causal))
                s1 = _bf(cb * _ssd_decay(cum_e, cumt, h0 + 1, causal))
                y_ref[:, cs] = (_mm(s0, _bf(jnp.where(lo, xp, 0.0))) + _mm(s1, _bf(jnp.where(lo, 0.0, xp)))
                                + yin[:, j * LANE:(j + 1) * LANE])
            st[:, gs] = st[:, gs] * elast_x[:, gs] + _mm_tn(bg, _bf(xrem[:, gs]))
        y1 = (y_ref[...] + dx_ref[...] * xs) * _silu(z)
        outs, _ = _group_norm_fwd(y1, nw_ref[...])
        for g in range(2):
            ycat_ref[:, g * 512:(g + 1) * 512] = _bf(outs[g])

    return pl.pallas_call(
        body, name="ssd_fwd", grid=(n,),
        in_specs=[pl.BlockSpec((T, SSD_W + LANE), lambda i: (i, OFF_Z // (SSD_W + LANE))),
                  pl.BlockSpec((T, SSD_CONV), lambda i: (i, 0)), _vec(LANE), _vec(LANE), _vec(SSD_W), _vec(SSD_W),
                  _full(e64.shape), _full(e128.shape), _full(tril.shape), pl.BlockSpec(memory_space=pl.ANY)],
        out_specs=[pl.BlockSpec((T, SSD_W), lambda i: (i, 1)), pl.BlockSpec((T, SSD_W), lambda i: (i, 0)),
                   pl.BlockSpec((None, SSD_N, SSD_W), lambda i: (i, 0, 0))],
        out_shape=[SDS((S, D_INNER), BF16), SDS((S,SSD_W), F32), SDS((n, SSD_N, SSD_W), F32)],
        scratch_shapes=[pltpu.VMEM((SSD_N, SSD_W), F32), pltpu.VMEM((LANE, T), F32)],
        input_output_aliases={9: 0},
        compiler_params=_cp(("arbitrary",)),
    )(u, xbc, bias, alog, dskip_x, nw, _bfc(e64), _bfc(e128), _bfc(tril), ycat)


def _ssd_bwd(u, xbc, bias, alog, dskip_x, nw, consts, y_ssd, states, dycat, du):
    S = u.shape[0]
    T = SSD_CHUNK
    n = S // T
    e64, e128, tril, triu = consts
    e64t = np.ascontiguousarray(e64.T)

    def body(u_ref, xbc_ref, bias_ref, alog_ref, dx_ref, nw_ref, e64_ref, e64t_ref, e128_ref, tril_ref, triu_ref,
             y_ref, st_ref, dy_ref, du_in, du_ref, dxbc_ref, red_ref, dst, dl_s, cumt, dxdt_s, dy0_s, gb_s, gc_s):
        del du_in

        @pl.when(pl.program_id(0) == 0)
        def _():
            dst[...] = jnp.zeros_like(dst)
            red_ref[...] = jnp.zeros_like(red_ref)

        zdt = u_ref[...]
        z = zdt[:, 0:SSD_W]
        xs = xbc_ref[:, 0:SSD_W]
        e64m = e64_ref[...]
        a_neg, dtpre, dt, cum, ecum_x, erem_x, elast_x, dt_x, cum_e = _ssd_common(
            zdt, bias_ref, alog_ref, tril_ref[...], e64m, e128_ref[...])
        cumt[...] = cum.T
        causal = _iota((T, T), 0) >= _iota((T, T), 1)
        lo = _iota((T, LANE), 1) < SSD_P
        xdt = xs * dt_x
        xrem = xdt * erem_x
        y = y_ref[...]
        dxv = dx_ref[...]
        nwv = nw_ref[...]
        sz = _silu(z)
        y0 = y + dxv * xs
        y1 = y0 * sz
        for g in range(2):
            gs = slice(g * 512, (g + 1) * 512)
            seg = y1[:, gs]
            inv = lax.rsqrt(jnp.mean(seg * seg, axis=-1, keepdims=True) + EPS)
            shat = seg * inv
            dyg = dy_ref[:, gs]
            red_ref[0:1, gs] += jnp.sum(dyg * shat, axis=0, keepdims=True)
            dsh = dyg * nwv[:, gs]
            dy1g = inv * (dsh - shat * jnp.mean(dsh * shat, axis=-1, keepdims=True))
            du_ref[:, gs] = _bf(dy1g * y0[:, gs] * _dsilu(z[:, gs]))
            dy0_s[:, gs] = dy1g * sz[:, gs]
        dy0 = dy0_s[...]
        red_ref[1:2, :] += jnp.sum(dy0 * xs, axis=0, keepdims=True)
        dyin = dy0 * ecum_x
        lane = _iota((T, LANE), 1)
        ones = jnp.ones((T, LANE), BF16)
        dcum = jnp.zeros((T, LANE), F32)

        def row_minus_col(gm):
            hi = _bf(gm)
            lw = _bf(gm - hi.astype(F32))
            return _mm(hi, ones) + _mm(lw, ones) - _mm_tn(hi, ones) - _mm_tn(lw, ones)

        for g in range(2):
            gs = slice(g * 512, (g + 1) * 512)
            bg = _bf(xbc_ref[:, SSD_W + g * LANE:SSD_W + (g + 1) * LANE])
            cg = _bf(xbc_ref[:, SSD_W + 256 + g * LANE:SSD_W + 256 + (g + 1) * LANE])
            cb = _mm_nt(cg, bg)
            dst_f, st_f = dst[:, gs], st_ref[:, gs]
            dstg = _bf(dst_f)
            stg = _bf(st_f)
            dyin_g = _bf(dyin[:, gs])
            xrem_g = _bf(xrem[:, gs])
            dcb = jnp.zeros((T, T), F32)
            dxr = _mm(bg, dstg)
            dxdt_s[:, gs] = dxr * erem_x[:, gs]
            gc_s[:, gs] = dxr * xrem[:, gs]
            gb_s[:, gs] = dyin[:, gs] * _mm(cg, stg)
            dl_s[:, gs] = jnp.sum(dst_f * st_f, axis=0, keepdims=True) * elast_x[:, gs]
            for j in range(4):
                h0 = 8 * g + 2 * j
                cs = slice(h0 * SSD_P, (h0 + 2) * SSD_P)
                xp = xdt[:, cs]
                dyp = dy0[:, cs]
                x_lo, x_hi = _bf(jnp.where(lo, xp, 0.0)), _bf(jnp.where(lo, 0.0, xp))
                d_lo, d_hi = _bf(jnp.where(lo, dyp, 0.0)), _bf(jnp.where(lo, 0.0, dyp))
                s0 = cb * _ssd_decay(cum_e, cumt, h0, causal)
                s1 = cb * _ssd_decay(cum_e, cumt, h0 + 1, causal)
                ds0 = _mm_nt(d_lo, x_lo)
                ds1 = _mm_nt(d_hi, x_hi)
                dcb = dcb + ds0 * _ssd_decay(cum_e, cumt, h0, causal) + ds1 * _ssd_decay(cum_e, cumt, h0 + 1, causal)
                dxdt_s[:, cs] += _mm_tn(_bf(s0), d_lo) + _mm_tn(_bf(s1), d_hi)
                dcum = dcum + jnp.where(lane == h0, row_minus_col(ds0 * s0), 0.0)
                dcum = dcum + jnp.where(lane == h0 + 1, row_minus_col(ds1 * s1), 0.0)
            dcb_b = _bf(dcb)
            dxbc_ref[:, SSD_W + g * LANE:SSD_W + (g + 1) * LANE] = _mm_tn(dcb_b, cg) + _mm_nt(xrem_g, dstg)
            dxbc_ref[:, SSD_W + 256 + g * LANE:SSD_W + 256 + (g + 1) * LANE] = _mm(dcb_b, bg) + _mm_nt(dyin_g, stg)
            dst[:, gs] = dst_f * elast_x[:, gs] + _mm_tn(cg, dyin_g)
        dxdt = dxdt_s[...]
        dxbc_ref[:, 0:SSD_W] = dxdt * dt_x + dy0 * dxv
        e64t = e64t_ref[...]
        hc = _sel_r(gc_s[...], e64t)
        dlast = (jnp.sum(hc, axis=0, keepdims=True)
                 + jnp.max(_sel_r(jnp.broadcast_to(dl_s[...], (8, SSD_W)), e64t), axis=0, keepdims=True))
        dcum = dcum + _sel_r(gb_s[...], e64t) - hc + jnp.where(_iota((T, LANE), 0) == T - 1, dlast, 0.0)
        dda = _sel_l(triu_ref[...], dcum)
        ddt = dda * a_neg + _sel_r(dxdt * xs, e64t)
        ddtpre = ddt * _sigmoid(dtpre)
        du_ref[:, SSD_W:SSD_W + LANE] = _bf(jnp.where(lane < SSD_HEADS, ddtpre, 0.0))
        red_ref[2:3, 0:LANE] += jnp.sum(ddtpre, axis=0, keepdims=True)
        red_ref[3:4, 0:LANE] += jnp.sum(dda * dt, axis=0, keepdims=True)

    rev = lambda i: (n - 1 - i, 0)
    return pl.pallas_call(
        body, name="ssd_bwd", grid=(n,),
        in_specs=[pl.BlockSpec((T, SSD_W + LANE), lambda i: (n - 1 - i, OFF_Z // (SSD_W + LANE))),
                  pl.BlockSpec((T, SSD_CONV), rev), _vec(LANE), _vec(LANE), _vec(SSD_W), _vec(SSD_W),
                  _full(e64.shape), _full(e64t.shape), _full(e128.shape), _full(tril.shape), _full(triu.shape),
                  pl.BlockSpec((T, SSD_W), rev), pl.BlockSpec((None, SSD_N, SSD_W), lambda i: (n - 1 - i, 0, 0)),
                  pl.BlockSpec((T, SSD_W), lambda i: (n - 1 - i, 1)), pl.BlockSpec(memory_space=pl.ANY)],
        out_specs=[pl.BlockSpec((T, SSD_W + LANE), lambda i: (n - 1 - i, OFF_Z // (SSD_W + LANE))),
                   pl.BlockSpec((T, SSD_CONV), rev), pl.BlockSpec((8, SSD_W), lambda i: (0, 0))],
        out_shape=[SDS((S, N_PAD), BF16), SDS((S, SSD_CONV), F32), SDS((8, SSD_W), F32)],
        scratch_shapes=[pltpu.VMEM((SSD_N, SSD_W), F32), pltpu.VMEM((1, SSD_W), F32), pltpu.VMEM((LANE, T), F32)]
        + [pltpu.VMEM((T, SSD_W), F32)] * 4,
        input_output_aliases={14: 0},
        compiler_params=_cp(("arbitrary",)),
    )(u, xbc, bias, alog, dskip_x, nw, _bfc(e64), _bfc(e64t), _bfc(e128), _bfc(tril), _bfc(triu), y_ssd, states, dycat, du)


def _bfc(a):
    return jnp.asarray(a, BF16)


def _outproj_fwd(ycat, wo, x, gate):
    S = x.shape[0]
    tm = min(512, S)

    def body(yc_ref, wo_ref, x_ref, g_ref, xn_ref, y_ref):
        y = _mm(_bf(yc_ref[...]), wo_ref[...])
        y_ref[...] = y
        xn_ref[...] = x_ref[...] + g_ref[...] * y

    row = pl.BlockSpec((tm, D_MODEL), lambda i: (i, 0))
    return pl.pallas_call(
        body, name="outproj_fwd", grid=(S // tm,),
        in_specs=[pl.BlockSpec((tm, D_INNER), lambda i: (i, 0)), _full((D_INNER, D_MODEL)), row, _vec(D_MODEL)],
        out_specs=[row, row],
        out_shape=[SDS((S, D_MODEL), F32), SDS((S, D_MODEL), F32)],
        compiler_params=_cp(("parallel",)),
    )(ycat, wo, x, gate)


def _outproj_bwd(dxn, y, gate, ycat, wo):
    S = dxn.shape[0]
    tm = min(512, S)

    def body(dx_ref, y_ref, g_ref, yc_ref, wo_ref, dyc_ref, gwo_ref, dg_ref, acc):
        @pl.when(pl.program_id(0) == 0)
        def _():
            acc[...] = jnp.zeros_like(acc)
            dg_ref[...] = jnp.zeros_like(dg_ref)

        dxv = dx_ref[...]
        dy = _bf(dxv * g_ref[...])
        dg_ref[0:1, :] += jnp.sum(dxv * y_ref[...], axis=0, keepdims=True)
        dyc_ref[...] = _mm_nt(dy, wo_ref[...])
        acc[...] += _mm_tn(_bf(yc_ref[...]), dy)

        @pl.when(pl.program_id(0) == pl.num_programs(0) - 1)
        def _():
            gwo_ref[...] = acc[...].astype(BF16)

    row = pl.BlockSpec((tm, D_MODEL), lambda i: (i, 0))
    wide = pl.BlockSpec((tm, D_INNER), lambda i: (i, 0))
    return pl.pallas_call(
        body, name="outproj_bwd", grid=(S // tm,),
        in_specs=[row, row, _vec(D_MODEL), wide, _full((D_INNER, D_MODEL))],
        out_specs=[wide, _full((D_INNER, D_MODEL)), _full((8, D_MODEL))],
        out_shape=[SDS((S, D_INNER), F32), SDS((D_INNER, D_MODEL), BF16), SDS((8, D_MODEL), F32)],
        scratch_shapes=[pltpu.VMEM((D_INNER, D_MODEL), F32)],
        compiler_params=_cp(("arbitrary",)),
    )(dxn, y, gate, ycat, wo)


def _loss_head(x, fw, target):
    S = x.shape[0]
    tm = min(512, S)

    def body(x_ref, fw_ref, t_ref, dx_ref, red_ref):
        @pl.when(pl.program_id(0) == 0)
        def _():
            red_ref[...] = jnp.zeros_like(red_ref)

        xv = x_ref[...]
        fwv = fw_ref[...]
        inv = lax.rsqrt(jnp.mean(xv * xv, axis=-1, keepdims=True) + EPS)
        xhat = xv * inv
        err = xhat * fwv - t_ref[...]
        col = jnp.sum(err * err, axis=0, keepdims=True)
        red_ref[1:2, :] += jnp.broadcast_to(jnp.sum(col, axis=1, keepdims=True) * (0.5 / D_MODEL), (1, D_MODEL))
        dy = err * (1.0 / D_MODEL)
        red_ref[0:1, :] += jnp.sum(dy * xhat, axis=0, keepdims=True)
        dxhat = dy * fwv
        dx_ref[...] = inv * (dxhat - xhat * jnp.mean(dxhat * xhat, axis=-1, keepdims=True))

    row = pl.BlockSpec((tm, D_MODEL), lambda i: (i, 0))
    return pl.pallas_call(
        body, name="loss_head", grid=(S // tm,),
        in_specs=[row, _vec(D_MODEL), row],
        out_specs=[row, _full((8, D_MODEL))],
        out_shape=[SDS((S, D_MODEL), F32), SDS((8, D_MODEL), F32)],
        compiler_params=_cp(("arbitrary",)),
    )(x, fw, target)


ADA_COLS = 3 * D_MODEL // N_DEV


def _ada_fwd(c_all, w_ada, b_cols):
    def body(c_ref, w_ref, b_ref, out_ref):
        out_ref[...] = _mm(_bf(_silu(c_ref[...])), _bf(w_ref[...])) + b_ref[...]

    return pl.pallas_call(
        body, name="ada_fwd", grid=(DEPTH,),
        in_specs=[_full((N_DEV, D_MODEL)), pl.BlockSpec((None, D_MODEL, ADA_COLS), lambda l: (l, 0, 0)),
                  pl.BlockSpec((None, 1, ADA_COLS), lambda l: (l, 0, 0))],
        out_specs=pl.BlockSpec((None, N_DEV, ADA_COLS), lambda l: (l, 0, 0)),
        out_shape=SDS((DEPTH, N_DEV, ADA_COLS), F32),
        compiler_params=_cp(("parallel",)),
    )(c_all, w_ada, b_cols)


def _ada_bwd(ct_pad, dmod_pad):
    def body(c_ref, d_ref, out_ref):
        out_ref[...] = _mm(_bf(_silu(c_ref[...])), _bf(d_ref[...]))

    return pl.pallas_call(
        body, name="ada_bwd", grid=(DEPTH,),
        in_specs=[_full((D_MODEL, LANE)), pl.BlockSpec((None, LANE, ADA_COLS), lambda l: (l, 0, 0))],
        out_specs=pl.BlockSpec((None, D_MODEL, ADA_COLS), lambda l: (l, 0, 0)),
        out_shape=SDS((DEPTH, D_MODEL, ADA_COLS), F32),
        compiler_params=_cp(("parallel",)),
    )(ct_pad, dmod_pad)


def _adamw(parts, w, m, v, name, own=None):
    n, L, R, C = parts.shape
    tr = R
    while tr * C * 4 > (1 << 20) and tr % 16 == 0:
        tr //= 2

    def body(*refs):
        p_ref, w_ref, m_ref, v_ref, g_ref, d_ref, mo_ref, vo_ref = refs[:1] + refs[-7:]

        def part(k):
            if own is None:
                return p_ref[k].astype(F32)
            me = 4 * lax.axis_index("x") + 2 * lax.axis_index("y") + lax.axis_index("c")
            return jnp.where(me == k, refs[1][...], p_ref[k]).astype(F32)

        g = part(0)
        for k in range(1, n):
            g = g + part(k)
        mn = ADAM_B1 * m_ref[...] + (1.0 - ADAM_B1) * g
        vn = ADAM_B2 * v_ref[...] + (1.0 - ADAM_B2) * (g * g)
        m_hat = mn / (1.0 - ADAM_B1 ** ADAM_STEP)
        v_hat = vn / (1.0 - ADAM_B2 ** ADAM_STEP)
        g_ref[...] = g
        d_ref[...] = -ADAM_LR * (m_hat / (jnp.sqrt(v_hat) + ADAM_EPS) + ADAM_WD * w_ref[...])
        mo_ref[...] = mn
        vo_ref[...] = vn

    blk = pl.BlockSpec((None, tr, C), lambda l, i: (l, i, 0))
    return pl.pallas_call(
        body, name=name, grid=(L, R // tr),
        in_specs=[pl.BlockSpec((n, None, tr, C), lambda l, i: (0, l, i, 0))] + [blk] * (3 if own is None else 4),
        out_specs=[blk] * 4,
        out_shape=[SDS((L, R, C), F32)] * 4,
        compiler_params=_cp(("parallel", "parallel")),
    )(parts, *([] if own is None else [own]), w, m, v)


MESH = pl.DeviceIdType.MESH
ANY = pl.BlockSpec(memory_space=pl.ANY)


def _all_gather(v, name):
    def body(v_ref, out_ref, send_sems, recv_sems, local_sem):
        x, y, c = lax.axis_index("x"), lax.axis_index("y"), lax.axis_index("c")
        me, sibling = (x, y, c), (x, y, 1 - c)
        chips = [(1 - x, y), (x, 1 - y), (1 - x, 1 - y)]

        def slot(px, py, pc):
            return out_ref.at[4 * px + 2 * py + pc]

        def copy(k, block, to, src=None):
            return pltpu.make_async_remote_copy(
                src_ref=slot(*block) if src is None else src, dst_ref=slot(*block),
                send_sem=send_sems.at[k], recv_sem=recv_sems.at[k], device_id=to, device_id_type=MESH)

        mine = pltpu.make_async_copy(v_ref, slot(*me), local_sem)
        mine.start()
        first = [copy(0, me, sibling, src=v_ref)]
        first += [copy(1 + j, me, (*chip, c), src=v_ref) for j, chip in enumerate(chips)]
        for cp in first:
            cp.start()
        passed = [copy(4 + j, (*chip, c), sibling) for j, chip in enumerate(chips)]
        for j, chip in enumerate(chips):
            copy(1 + j, (*chip, c), me).wait_recv()
            passed[j].start()
        copy(0, sibling, me).wait_recv()
        for j, chip in enumerate(chips):
            copy(4 + j, (*chip, 1 - c), me).wait_recv()
        for cp in first + passed:
            cp.wait_send()
        mine.wait()

    return pl.pallas_call(
        body, name=name, in_specs=[ANY], out_specs=ANY,
        out_shape=SDS((N_DEV,) + v.shape, v.dtype),
        scratch_shapes=[pltpu.SemaphoreType.DMA((7,)), pltpu.SemaphoreType.DMA((7,)), pltpu.SemaphoreType.DMA],
    )(v)


def _all_to_all(v, name):
    def body(v_ref, out_ref, send_sems, recv_sems, local_sem):
        x, y, c = lax.axis_index("x"), lax.axis_index("y"), lax.axis_index("c")
        mine_idx = 4 * x + 2 * y + c
        mine = pltpu.make_async_copy(v_ref.at[mine_idx], out_ref.at[mine_idx], local_sem)
        mine.start()
        sends, recvs = [], []
        for k in range(1, N_DEV):
            px = 1 - x if k & 4 else x
            py = 1 - y if k & 2 else y
            pc = 1 - c if k & 1 else c
            peer_idx = 4 * px + 2 * py + pc
            sems = dict(send_sem=send_sems.at[k - 1], recv_sem=recv_sems.at[k - 1], device_id=(px, py, pc),
                        device_id_type=MESH)
            sends.append(pltpu.make_async_remote_copy(src_ref=v_ref.at[peer_idx], dst_ref=out_ref.at[mine_idx], **sems))
            recvs.append(pltpu.make_async_remote_copy(src_ref=v_ref.at[peer_idx], dst_ref=out_ref.at[peer_idx], **sems))
        for cp in sends:
            cp.start()
        for cp in recvs:
            cp.wait_recv()
        for cp in sends:
            cp.wait_send()
        mine.wait()

    return pl.pallas_call(
        body, name=name, in_specs=[ANY], out_specs=ANY,
        out_shape=SDS(v.shape, v.dtype),
        scratch_shapes=[pltpu.SemaphoreType.DMA((7,)), pltpu.SemaphoreType.DMA((7,)), pltpu.SemaphoreType.DMA],
    )(v)


HBM_SPEC = pl.BlockSpec(memory_space=pltpu.HBM)
SEM_SPEC = pl.BlockSpec(memory_space=pltpu.SEMAPHORE)
EFFECT = pltpu.SideEffectType.DATAFLOW_SIDE_EFFECTING


def _exchange_copies(srcs, lands, send_sems, recv_sems, scatter, layer):
    x, y, c = lax.axis_index("x"), lax.axis_index("y"), lax.axis_index("c")
    me = 4 * x + 2 * y + c
    copies = []
    for a, (src, land) in enumerate(zip(srcs, lands)):
        for k in range(1, N_DEV):
            px = 1 - x if k & 4 else x
            py = 1 - y if k & 2 else y
            pc = 1 - c if k & 1 else c
            n = 7 * a + k - 1
            copies.append(pltpu.make_async_remote_copy(
                src_ref=src.at[4 * px + 2 * py + pc] if scatter else src,
                dst_ref=land.at[me, layer] if scatter else land.at[me],
                send_sem=send_sems.at[n], recv_sem=recv_sems.at[n], device_id=(px, py, pc), device_id_type=MESH))
    return copies


def _exchange_start(name, srcs, lands, scatter, layer=0, after=None):
    n = len(srcs)

    def body(*refs):
        send_sems, recv_sems = refs[-2 * n - 3], refs[-2 * n - 2]
        for cp in _exchange_copies(refs[:n], refs[n:2 * n], send_sems, recv_sems, scatter, layer):
            cp.start()
        refs[-1][...] = jnp.zeros_like(refs[-1])

    arrays = list(srcs) + list(lands)
    sems = pltpu.SemaphoreType.DMA((7 * n,))
    out = pl.pallas_call(
        body, name=name,
        out_shape=(sems, sems, *[pltpu.HBM(v.shape, v.dtype) for v in arrays], SDS((8, LANE), F32)),
        in_specs=[HBM_SPEC] * (2 * n) + ([ANY] if after is not None else []),
        out_specs=(SEM_SPEC, SEM_SPEC, *[HBM_SPEC] * (2 * n), pl.BlockSpec(memory_space=pltpu.VMEM)),
        input_output_aliases={i: 2 + i for i in range(2 * n)},
        compiler_params=pltpu.CompilerParams(has_side_effects=EFFECT),
    )(*[pltpu.with_memory_space_constraint(v, pltpu.HBM) for v in arrays], *([after] if after is not None else []))
    return dict(sems=out[:2], srcs=out[2:2 + n], lands=out[2 + n:2 + 2 * n], token=out[-1][0, 0], scatter=scatter,
                layer=layer)


def _exchange_wait(name, st, after):
    n = len(st["srcs"])

    def body(*refs):
        send_sems, recv_sems = refs[2 * n], refs[2 * n + 1]
        for cp in _exchange_copies(refs[:n], refs[n:2 * n], send_sems, recv_sems, st["scatter"], st["layer"]):
            cp.wait_send()
            cp.wait_recv()

    arrays = list(st["srcs"]) + list(st["lands"])
    out = pl.pallas_call(
        body, name=name,
        out_shape=tuple(pltpu.HBM(v.shape, v.dtype) for v in arrays),
        in_specs=[HBM_SPEC] * (2 * n) + [SEM_SPEC, SEM_SPEC, ANY],
        out_specs=tuple([HBM_SPEC] * (2 * n)),
        input_output_aliases={i: i for i in range(2 * n)},
        compiler_params=pltpu.CompilerParams(has_side_effects=EFFECT),
    )(*arrays, *st["sems"], after)
    return out[n:]


_IN_PIECES = ([(1024, 3072)]
              + [r for t in range(4) for r in ((LANE * t, LANE * (t + 1)), (512 + LANE * t, 512 + LANE * (t + 1)))]
              + [(4096, 5632), (3072, 4096), (5632, 5648)])


def _permute_in(w):
    pad = jnp.zeros(w.shape[:-1] + (N_PAD - N_IN,), w.dtype)
    return jnp.concatenate([w[..., a:b] for a, b in _IN_PIECES] + [pad], axis=-1)


def _unpermute_in(g):
    ax = [g[..., OFF_LRU + 2 * LANE * t:OFF_LRU + 2 * LANE * t + LANE] for t in range(4)]
    ag = [g[..., OFF_LRU + 2 * LANE * t + LANE:OFF_LRU + 2 * LANE * (t + 1)] for t in range(4)]
    return jnp.concatenate(ax + ag + [g[..., 0:2048], g[..., OFF_Z:OFF_Z + SSD_W], g[..., OFF_XBC:OFF_XBC + SSD_CONV],
                                      g[..., OFF_Z + SSD_W:OFF_Z + SSD_W + SSD_HEADS]], axis=-1)


SHARD_COLS = N_IN // N_DEV


def _in_segments():
    segs, pos = [], 0
    for a, b in _IN_PIECES:
        for i in range(N_DEV):
            lo, hi = max(a, SHARD_COLS * i), min(b, SHARD_COLS * (i + 1))
            if lo < hi:
                segs.append((i, lo - SHARD_COLS * i, hi - lo, pos + lo - a))
        pos += b - a
    return segs


RELAYOUT_ROWS = 256


def _relayout_in(land, own):
    def body(land_ref, own_ref, out_ref):
        me = 4 * lax.axis_index("x") + 2 * lax.axis_index("y") + lax.axis_index("c")
        out_ref[:, N_IN:N_PAD] = jnp.zeros((RELAYOUT_ROWS, N_PAD - N_IN), BF16)
        for i, j, wd, p in _in_segments():
            out_ref[:, p:p + wd] = jnp.where(me == i, own_ref[:, j:j + wd], land_ref[i, :, j:j + wd])

    return pl.pallas_call(
        body, name="relayout_in", grid=(D_MODEL // RELAYOUT_ROWS,),
        in_specs=[pl.BlockSpec((N_DEV, RELAYOUT_ROWS, SHARD_COLS), lambda r: (0, r, 0)),
                  pl.BlockSpec((RELAYOUT_ROWS, SHARD_COLS), lambda r: (r, 0))],
        out_specs=pl.BlockSpec((RELAYOUT_ROWS, N_PAD), lambda r: (r, 0)),
        out_shape=SDS((D_MODEL, N_PAD), BF16),
        compiler_params=_cp(("parallel",)),
    )(land, own)


def _relayout_grad(g):
    def body(g_ref, out_ref):
        for i, j, wd, p in _in_segments():
            out_ref[i, :, j:j + wd] = g_ref[:, p:p + wd].astype(BF16)

    return pl.pallas_call(
        body, name="relayout_grad", grid=(D_MODEL // RELAYOUT_ROWS,),
        in_specs=[pl.BlockSpec((RELAYOUT_ROWS, N_PAD), lambda r: (r, 0))],
        out_specs=pl.BlockSpec((N_DEV, RELAYOUT_ROWS, SHARD_COLS), lambda r: (0, r, 0)),
        out_shape=SDS((N_DEV, D_MODEL, SHARD_COLS), BF16),
        compiler_params=_cp(("parallel",)),
    )(g)


def _block_diag(w):
    w4 = w.reshape(4, 2, 64, 64)
    z = jnp.zeros((4, 64, 64), w.dtype)
    top = jnp.concatenate([w4[:, 0], z], axis=-1)
    bot = jnp.concatenate([z, w4[:, 1]], axis=-1)
    return jnp.concatenate([top, bot], axis=1).astype(BF16)


def _diag_blocks(g):
    return jnp.stack([g[:, :64, :64], g[:, 64:, 64:]], axis=1).reshape(8, 64, 64)


def _pad_lanes(v):
    return jnp.pad(v, (0, LANE - v.shape[0]))[None, :]


def _lower_bounds(logits):
    p = jax.nn.softmax(logits, axis=0)
    return p, jnp.cumsum(p, axis=0) - p[0]


def _lower_bounds_bwd(p, dlb):
    dp = jnp.cumsum(dlb[::-1], axis=0)[::-1]
    dp = dp.at[0].add(-jnp.sum(dlb, axis=0))
    return p * (dp - jnp.sum(dp * p, axis=0, keepdims=True))


SMALL = ["norm_w", "b_ada", "lru_conv_b", "lru_wa", "lru_ba", "lru_wx", "lru_bx", "lru_lambda", "hg_lb_logits",
         "hg_norm_w", "ssd_conv_b", "ssd_dt_bias", "ssd_a_log", "ssd_d", "ssd_norm_w", "final_norm_w"]
WEIGHTS = ["norm_w", "w_ada", "b_ada", "w_in", "lru_conv_w", "lru_conv_b", "lru_wa", "lru_ba", "lru_wx", "lru_bx",
           "lru_lambda", "hg_lb_logits", "hg_norm_w", "ssd_conv_w", "ssd_conv_b", "ssd_dt_bias", "ssd_a_log", "ssd_d",
           "ssd_norm_w", "w_out", "final_norm_w"]
INPUTS = ["x", "c"] + WEIGHTS + ["loss_target"] + ["m_" + n for n in WEIGHTS] + ["v_" + n for n in WEIGHTS]
SMALL_ROW = 1024


def _small_rows(like):
    out, off = {}, 0
    for n in SMALL:
        rows = -(-int(np.prod(like[n].shape)) // (8 * SMALL_ROW)) * 8
        out[n] = (off, rows)
        off += rows
    return out, off


def _flatten_small(d, prefix=""):
    table, _ = _small_rows({n: d[prefix + n] for n in SMALL})
    pieces = []
    for n in SMALL:
        flat = d[prefix + n].reshape(-1)
        pieces.append(jnp.pad(flat, (0, table[n][1] * SMALL_ROW - flat.shape[0])).reshape(-1, SMALL_ROW))
    return jnp.concatenate(pieces, axis=0)


def _split_small(packed, like):
    table, _ = _small_rows(like)
    out = {}
    for n in SMALL:
        off, rows = table[n]
        size = int(np.prod(like[n].shape))
        out[n] = packed[off:off + rows].reshape(-1)[:size].reshape(like[n].shape)
    return out


def _local_step(x, mod, target, w, fetch, emit):
    S = x.shape[0]
    mall = _bfc(_hg_consts())
    mall_t = _bfc(_hg_consts().T)
    consts = _ssd_consts()
    p_lb, lbs = _lower_bounds(w["hg_lb_logits"])
    saved = []
    for l in range(DEPTH):
        w_in_l, w_out_l, token = fetch(l, x)
        shift, scale, gate = (mod[l:l + 1, k * D_MODEL:(k + 1) * D_MODEL] for k in range(3))
        shift = shift + token
        prm = dict(
            nw=w["norm_w"][l:l + 1], cw=w["lru_conv_w"][l], cb=w["lru_conv_b"][l:l + 1],
            wa=_block_diag(w["lru_wa"][l]), ba=w["lru_ba"][l].reshape(1, LRU_W),
            wx=_block_diag(w["lru_wx"][l]), bx=w["lru_bx"][l].reshape(1, LRU_W), lam=w["lru_lambda"][l:l + 1],
            lb=lbs[l:l + 1], hnw=w["hg_norm_w"][l:l + 1], scw=w["ssd_conv_w"][l], scb=w["ssd_conv_b"][l:l + 1],
            bias=_pad_lanes(w["ssd_dt_bias"][l]), alog=_pad_lanes(w["ssd_a_log"][l]),
            dskip=jnp.repeat(w["ssd_d"][l], SSD_P)[None, :], snw=w["ssd_norm_w"][l:l + 1],
            w_in=w_in_l, w_out=w_out_l, scale=scale, gate=gate)
        u, h = _inproj_fwd(x, prm["nw"], scale, shift, prm["w_in"])
        ycat = lax.empty((S, D_INNER), BF16)
        lru_args = (u, prm["cw"], prm["cb"], prm["wa"], prm["ba"], prm["wx"], prm["bx"], prm["lam"])
        ycat, h_lru = _lru_fwd(*lru_args, ycat)
        ycat, o_b, hg_st = _hg_fwd(u, prm["lb"], prm["hnw"], mall, ycat)
        xbc = _ssdconv_fwd(u, prm["scw"], prm["scb"])
        ssd_args = (u, xbc, prm["bias"], prm["alog"], prm["dskip"], prm["snw"], consts)
        ycat, y_ssd, ssd_st = _ssd_fwd(*ssd_args, ycat)
        x_new, y = _outproj_fwd(ycat, prm["w_out"], x, gate)
        saved.append((prm, x, u, h, ycat, lru_args, h_lru, o_b, hg_st, ssd_args, y_ssd, ssd_st, y))
        x = x_new
    dx, red = _loss_head(x, w["final_norm_w"][None, :], target)
    loss = red[1, 0]
    g = {n: [None] * DEPTH for n in WEIGHTS}
    g["final_norm_w"] = red[0]
    dmod, dlb = [None] * DEPTH, [None] * DEPTH
    for l in reversed(range(DEPTH)):
        prm, x, u, h, ycat, lru_args, h_lru, o_b, hg_st, ssd_args, y_ssd, ssd_st, y = saved[l]
        dycat, g_out, dgate = _outproj_bwd(dx, y, prm["gate"], ycat, prm["w_out"])
        token = emit(l, "w_out", g_out)
        du = lax.empty((S, N_PAD), BF16)
        ssd_args = ssd_args[:5] + (ssd_args[5] + token,) + ssd_args[6:]
        du, dxbc, sred = _ssd_bwd(*ssd_args, y_ssd, ssd_st, dycat, du)
        du, cred = _ssdconv_bwd(u, prm["scw"], prm["scb"], dxbc, du)
        du, hred = _hg_bwd(u, prm["lb"], prm["hnw"], mall, mall_t, o_b, hg_st, dycat, du)
        du, lred, gwa, gwx = _lru_bwd(*lru_args, h_lru, dycat, du)
        token = emit(l, "w_in", _inproj_bwd_w(h, du))
        dx, ired = _inproj_bwd_x(du, prm["w_in"], x, prm["nw"], prm["scale"] + token, dx)
        g["norm_w"][l] = ired[2]
        dmod[l] = jnp.concatenate([ired[0], ired[1], dgate[0]])
        g["lru_conv_w"][l], g["lru_conv_b"][l] = lred[0:4], lred[4]
        g["lru_ba"][l], g["lru_bx"][l], g["lru_lambda"][l] = lred[5].reshape(8, 64), lred[6].reshape(8, 64), lred[7]
        g["lru_wa"][l], g["lru_wx"][l] = _diag_blocks(gwa), _diag_blocks(gwx)
        g["hg_norm_w"][l], dlb[l] = hred[0], hred[1]
        g["ssd_conv_w"][l], g["ssd_conv_b"][l] = cred[0:4], cred[4]
        g["ssd_norm_w"][l] = sred[0]
        g["ssd_d"][l] = sred[1].reshape(SSD_HEADS, SSD_P).sum(-1)
        g["ssd_dt_bias"][l] = sred[2, :SSD_HEADS]
        g["ssd_a_log"][l] = -sred[3, :SSD_HEADS] * jnp.exp(w["ssd_a_log"][l])
    g["hg_lb_logits"] = _lower_bounds_bwd(p_lb, jnp.stack(dlb))
    for n in WEIGHTS:
        if isinstance(g[n], list) and g[n][0] is not None:
            g[n] = jnp.stack(g[n])
    return loss, dx, jnp.stack(dmod), g


def kernel(x, c, norm_w, w_ada, b_ada, w_in, lru_conv_w, lru_conv_b, lru_wa, lru_ba, lru_wx, lru_bx, lru_lambda, hg_lb_logits, hg_norm_w, ssd_conv_w, ssd_conv_b, ssd_dt_bias, ssd_a_log, ssd_d, ssd_norm_w, w_out, final_norm_w, loss_target, m_norm_w, m_w_ada, m_b_ada, m_w_in, m_lru_conv_w, m_lru_conv_b, m_lru_wa, m_lru_ba, m_lru_wx, m_lru_bx, m_lru_lambda, m_hg_lb_logits, m_hg_norm_w, m_ssd_conv_w, m_ssd_conv_b, m_ssd_dt_bias, m_ssd_a_log, m_ssd_d, m_ssd_norm_w, m_w_out, m_final_norm_w, v_norm_w, v_w_ada, v_b_ada, v_w_in, v_lru_conv_w, v_lru_conv_b, v_lru_wa, v_lru_ba, v_lru_wx, v_lru_bx, v_lru_lambda, v_hg_lb_logits, v_hg_norm_w, v_ssd_conv_w, v_ssd_conv_b, v_ssd_dt_bias, v_ssd_a_log, v_ssd_d, v_ssd_norm_w, v_w_out, v_final_norm_w):
    return _step(x, c, norm_w, w_ada, b_ada, w_in, lru_conv_w, lru_conv_b, lru_wa, lru_ba, lru_wx, lru_bx, lru_lambda, hg_lb_logits, hg_norm_w, ssd_conv_w, ssd_conv_b, ssd_dt_bias, ssd_a_log, ssd_d, ssd_norm_w, w_out, final_norm_w, loss_target, m_norm_w, m_w_ada, m_b_ada, m_w_in, m_lru_conv_w, m_lru_conv_b, m_lru_wa, m_lru_ba, m_lru_wx, m_lru_bx, m_lru_lambda, m_hg_lb_logits, m_hg_norm_w, m_ssd_conv_w, m_ssd_conv_b, m_ssd_dt_bias, m_ssd_a_log, m_ssd_d, m_ssd_norm_w, m_w_out, m_final_norm_w, v_norm_w, v_w_ada, v_b_ada, v_w_in, v_lru_conv_w, v_lru_conv_b, v_lru_wa, v_lru_ba, v_lru_wx, v_lru_bx, v_lru_lambda, v_hg_lb_logits, v_hg_norm_w, v_ssd_conv_w, v_ssd_conv_b, v_ssd_dt_bias, v_ssd_a_log, v_ssd_d, v_ssd_norm_w, v_w_out, v_final_norm_w)


def _step(*args):
    a = dict(zip(INPUTS, args, strict=True))
    me = 4 * lax.axis_index("x") + 2 * lax.axis_index("y") + lax.axis_index("c")
    x, target = a["x"][0], a["loss_target"][0]

    c_all = _all_gather(a["c"], "gather_c")[:, 0, :]
    b_cols = lax.dynamic_slice_in_dim(a["b_ada"], me * ADA_COLS, ADA_COLS, axis=1)[:, None, :]
    mod_parts = _all_gather(_ada_fwd(c_all, a["w_ada"], b_cols), "gather_mod")
    mod = lax.dynamic_index_in_dim(mod_parts, me, axis=2, keepdims=False)
    mod = mod.transpose(1, 0, 2).reshape(DEPTH, 3 * D_MODEL)

    w = {n: a[n] for n in SMALL}
    conv = _all_gather(jnp.concatenate([a["lru_conv_w"], a["ssd_conv_w"]], axis=-1), "gather_conv")
    conv = conv.transpose(1, 2, 0, 3)
    w["lru_conv_w"] = conv[..., :64].reshape(DEPTH, 4, LRU_W)
    w["ssd_conv_w"] = conv[..., 64:].reshape(DEPTH, 4, SSD_CONV)

    w_in_b, w_out_b = a["w_in"].astype(BF16), a["w_out"].astype(BF16)
    cols, rows_out = N_IN // N_DEV, D_INNER // N_DEV

    def gather_start(l, after=None):
        lands = [lax.empty((N_DEV, D_MODEL, cols), BF16), lax.empty((N_DEV, rows_out, D_MODEL), BF16)]
        return _exchange_start(f"gather_start_{l}", [w_in_b[l], w_out_b[l]], lands, False, after=after)

    gathers = {0: gather_start(0), 1: gather_start(1)}

    def fetch(l, x_l):
        if l >= 1 and l + 1 < DEPTH:
            gathers[l + 1] = gather_start(l + 1, after=x_l)
        land_in, land_out = _exchange_wait(f"gather_wait_{l}", gathers[l], x_l)
        land_out = lax.dynamic_update_index_in_dim(land_out, w_out_b[l], me, 0)
        token = gathers[l + 1]["token"] if l + 1 < DEPTH else 0.0
        return _relayout_in(land_in, w_in_b[l]), land_out.reshape(D_INNER, D_MODEL), token

    scatters = {"w_in": {}, "w_out": {}}
    lands = {"w_in": lax.empty((N_DEV, DEPTH, D_MODEL, cols), BF16),
             "w_out": lax.empty((N_DEV, DEPTH, rows_out, D_MODEL), BF16)}
    own = {"w_in": [None] * DEPTH, "w_out": [None] * DEPTH}

    def emit(l, name, grad):
        grad = _relayout_grad(grad) if name == "w_in" else grad.reshape(N_DEV, rows_out, D_MODEL)
        own[name][l] = lax.dynamic_index_in_dim(grad, me, 0, keepdims=False)
        st = _exchange_start(f"scatter_start_{name}_{l}", [grad], [lands[name]], True, layer=l)
        scatters[name][l] = st
        lands[name] = st["lands"][0]
        return st["token"]

    loss, dx, dmod, g = _local_step(x, mod, target, w, fetch, emit)
    loss = lax.psum(loss, ("x", "y", "c"))

    g["b_ada"] = dmod
    small = _all_gather(_flatten_small(g), "gather_small")
    outs = _adamw(small[:, None], *[_flatten_small(a, p)[None] for p in ("", "m_", "v_")], "adamw_small")
    res = [_split_small(o[0], a) for o in outs]

    off = _small_rows(a)[0]["b_ada"][0]
    dmod_all = small[:, off:off + DEPTH * 3 * D_MODEL // SMALL_ROW]
    dmod_all = dmod_all.reshape(N_DEV, DEPTH, 3 * D_MODEL).transpose(1, 0, 2)
    dmod_cols = lax.dynamic_slice_in_dim(dmod_all, me * ADA_COLS, ADA_COLS, axis=2)
    dmod_pad = jnp.pad(dmod_cols, ((0, 0), (0, LANE - N_DEV), (0, 0)))
    ct_pad = jnp.pad(c_all.T, ((0, 0), (0, LANE - N_DEV)))
    g_ada = _ada_bwd(ct_pad, dmod_pad)

    def sharded(name, parts, own=None):
        return _adamw(parts, a[name], a["m_" + name], a["v_" + name], "adamw_" + name, own=own)

    big = {"w_ada": sharded("w_ada", g_ada[None])}
    g_conv = jnp.concatenate([g["lru_conv_w"].reshape(DEPTH, 4, N_DEV, 64), g["ssd_conv_w"].reshape(DEPTH, 4, N_DEV, 192)],
                             axis=-1).transpose(2, 0, 1, 3)
    conv_parts = _all_to_all(g_conv, "scatter_conv")
    big["lru_conv_w"] = sharded("lru_conv_w", conv_parts[..., :64])
    big["ssd_conv_w"] = sharded("ssd_conv_w", conv_parts[..., 64:])

    after = big["w_ada"][1]
    for name in ("w_out", "w_in"):
        for l in reversed(range(DEPTH)):
            scatters[name][l]["lands"] = [lands[name]]
            lands[name] = _exchange_wait(f"scatter_wait_{name}_{l}", scatters[name][l], after)[0]
        big[name] = sharded(name, lands[name], jnp.stack(own[name]))

    out = [loss, dx[None]]
    for k in range(4):
        out += [big[n][k] if n in big else res[k][n] for n in WEIGHTS]
    return tuple(out)
```

```python
import functools

import numpy as np
import jax
import jax.numpy as jnp
from jax import lax
from jax.experimental import pallas as pl
from jax.experimental.pallas import tpu as pltpu

F32 = jnp.float32
BF16 = jnp.bfloat16
SDS = jax.ShapeDtypeStruct

N_DEV = 8
DEPTH = 4
D_MODEL = 1024
D_INNER = 2048
EPS = 1e-6
LRU_W = 512
LRU_C = 8.0
HG_W = 512
HG_CHUNK = 64
HG_HEADS = 4
SSD_W = 1024
SSD_HEADS = 16
SSD_P = 64
SSD_N = 128
SSD_CHUNK = 128
SSD_CONV = 1536
N_IN = 5648
N_PAD = 5760
OFF_HG = 0
OFF_LRU = 2048
OFF_XBC = 3072
OFF_Z = 4608
LANE = 128
VMEM_LIMIT = 56 * 1024 * 1024
NEG = -1e30

ADAM_LR = 0.001
ADAM_B1 = 0.9
ADAM_B2 = 0.999
ADAM_EPS = 1e-08
ADAM_WD = 0.01
ADAM_STEP = 10


def _cp(sem=None):
    return pltpu.CompilerParams(dimension_semantics=sem, vmem_limit_bytes=VMEM_LIMIT)


def _dg(a, b, ca, cb):
    return lax.dot_general(a, b, (((ca,), (cb,)), ((), ())), preferred_element_type=F32)


def _mm(a, b):
    return _dg(a, b, 1, 0)


def _mm_nt(a, b):
    return _dg(a, b, 1, 1)


def _mm_tn(a, b):
    return _dg(a, b, 0, 0)


def _bf(x):
    return x.astype(BF16)


def _split3(x):
    hi = x.astype(BF16)
    r = x - hi.astype(F32)
    mid = r.astype(BF16)
    lo = (r - mid.astype(F32)).astype(BF16)
    return hi, mid, lo


def _sel_r(x, m):
    hi, mid, lo = _split3(x)
    return _mm(hi, m) + _mm(mid, m) + _mm(lo, m)


def _sel_l(m, x):
    hi, mid, lo = _split3(x)
    return _mm(m, hi) + _mm(m, mid) + _mm(m, lo)


def _sel_tn(x, m):
    hi, mid, lo = _split3(x)
    return _mm_tn(hi, m) + _mm_tn(mid, m) + _mm_tn(lo, m)


def _sigmoid(x):
    return 1.0 / (1.0 + jnp.exp(-x))


def _silu(x):
    return x * _sigmoid(x)


def _dsilu(x):
    s = _sigmoid(x)
    return s * (1.0 + x * (1.0 - s))


def _softplus(x):
    return jnp.maximum(x, 0.0) + jnp.log(1.0 + jnp.exp(-jnp.abs(x)))


def _expm1(z):
    series = z * (1.0 + z * (1.0 / 2) * (1.0 + z * (1.0 / 3) * (1.0 + z * (1.0 / 4) * (
        1.0 + z * (1.0 / 5) * (1.0 + z * (1.0 / 6) * (1.0 + z * (1.0 / 7)))))))
    return jnp.where(jnp.abs(z) < 0.3, series, jnp.exp(z) - 1.0)


def _iota(shape, dim):
    return lax.broadcasted_iota(jnp.int32, shape, dim)


def _last_row(x, rows):
    return jnp.sum(jnp.where(rows == x.shape[0] - 1, x, 0.0), axis=0, keepdims=True)


def _shift_down(x, d, rows, fill=0.0):
    return jnp.where(rows >= d, pltpu.roll(x, d, 0), fill)


def _shift_up(x, d, rows, fill=0.0):
    n = x.shape[0]
    return jnp.where(rows < n - d, pltpu.roll(x, n - d, 0), fill)


def _conv_fwd(x, cw_ref, cb_ref, rows):
    out = cb_ref[...] + cw_ref[pl.ds(3, 1), :] * x
    for k in range(3):
        out = out + cw_ref[pl.ds(k, 1), :] * _shift_down(x, 3 - k, rows)
    return out


def _conv_bwd(x, dco, cw_ref, rows):
    dx = cw_ref[pl.ds(3, 1), :] * dco
    dws = []
    for k in range(3):
        dx = dx + cw_ref[pl.ds(k, 1), :] * _shift_up(dco, 3 - k, rows)
        dws.append(jnp.sum(dco * _shift_down(x, 3 - k, rows), axis=0, keepdims=True))
    dws.append(jnp.sum(dco * x, axis=0, keepdims=True))
    return dx, dws, jnp.sum(dco, axis=0, keepdims=True)


def _vec(n):
    return pl.BlockSpec((1, n), lambda *_: (0, 0))


def _full(shape):
    nd = len(shape)
    return pl.BlockSpec(shape, lambda *_: (0,) * nd)


def _inproj_fwd(x, nw, scale, shift, w):
    S = x.shape[0]
    tm, tn = min(512, S), 640

    def body(x_ref, nw_ref, sc_ref, sh_ref, w_ref, u_ref, h_ref):
        @pl.when(pl.program_id(1) == 0)
        def _():
            xv = x_ref[...]
            inv = lax.rsqrt(jnp.mean(xv * xv, axis=-1, keepdims=True) + EPS)
            h = (xv * inv) * nw_ref[...] * (1.0 + sc_ref[...]) + sh_ref[...]
            h_ref[...] = h.astype(BF16)

        u_ref[...] = _mm(h_ref[...], w_ref[...])

    return pl.pallas_call(
        body, name="inproj_fwd", grid=(S // tm, N_PAD // tn),
        in_specs=[pl.BlockSpec((tm, D_MODEL), lambda i, j: (i, 0)), _vec(D_MODEL), _vec(D_MODEL), _vec(D_MODEL),
                  pl.BlockSpec((D_MODEL, tn), lambda i, j: (0, j))],
        out_specs=[pl.BlockSpec((tm, tn), lambda i, j: (i, j)), pl.BlockSpec((tm, D_MODEL), lambda i, j: (i, 0))],
        out_shape=[SDS((S, N_PAD), F32), SDS((S, D_MODEL), BF16)],
        compiler_params=_cp(("parallel", "arbitrary")),
    )(x, nw, scale, shift, w)


def _inproj_bwd_x(du, w, x, nw, scale, dxn):
    S = x.shape[0]
    tm, tk = min(512, S), 640
    nk = N_PAD // tk

    def body(du_ref, w_ref, x_ref, nw_ref, sc_ref, dxn_ref, dx_ref, red_ref, acc):
        i, k = pl.program_id(0), pl.program_id(1)

        @pl.when(k == 0)
        def _():
            acc[...] = jnp.zeros_like(acc)

        @pl.when((i == 0) & (k == 0))
        def _():
            red_ref[...] = jnp.zeros_like(red_ref)

        acc[...] += _mm_nt(_bf(du_ref[...]), w_ref[...])

        @pl.when(k == nk - 1)
        def _():
            dh = acc[...]
            xv = x_ref[...]
            inv = lax.rsqrt(jnp.mean(xv * xv, axis=-1, keepdims=True) + EPS)
            xhat = xv * inv
            nwv = nw_ref[...]
            g1 = 1.0 + sc_ref[...]
            dxhat = dh * nwv * g1
            dx = inv * (dxhat - xhat * jnp.mean(dxhat * xhat, axis=-1, keepdims=True))
            dx_ref[...] = dxn_ref[...] + dx
            red_ref[0:1, :] += jnp.sum(dh, axis=0, keepdims=True)
            red_ref[1:2, :] += jnp.sum(dh * xhat * nwv, axis=0, keepdims=True)
            red_ref[2:3, :] += jnp.sum(dh * xhat * g1, axis=0, keepdims=True)

    row = pl.BlockSpec((tm, D_MODEL), lambda i, k: (i, 0))
    return pl.pallas_call(
        body, name="inproj_bwd_x", grid=(S // tm, nk),
        in_specs=[pl.BlockSpec((tm, tk), lambda i, k: (i, k)), pl.BlockSpec((D_MODEL, tk), lambda i, k: (0, k)),
                  row, _vec(D_MODEL), _vec(D_MODEL), row],
        out_specs=[row, pl.BlockSpec((8, D_MODEL), lambda i, k: (0, 0))],
        out_shape=[SDS((S, D_MODEL), F32), SDS((8, D_MODEL), F32)],
        scratch_shapes=[pltpu.VMEM((tm, D_MODEL), F32)],
        compiler_params=_cp(("arbitrary", "arbitrary")),
    )(du, w, x, nw, scale, dxn)


def _inproj_bwd_w(h, du):
    S = h.shape[0]
    tn = 640

    def body(h_ref, du_ref, gw_ref):
        gw_ref[...] = _mm_tn(h_ref[...], _bf(du_ref[...]))

    return pl.pallas_call(
        body, name="inproj_bwd_w", grid=(N_PAD // tn,),
        in_specs=[_full((S, D_MODEL)), pl.BlockSpec((S, tn), lambda j: (0, j))],
        out_specs=pl.BlockSpec((D_MODEL, tn), lambda j: (0, j)),
        out_shape=SDS((D_MODEL, N_PAD), F32),
        compiler_params=_cp(("parallel",)),
    )(h, du)


def _scan_block(a, b, rows):
    d = 1
    while d < a.shape[0]:
        a_s = _shift_down(a, d, rows, 1.0)
        b_s = _shift_down(b, d, rows, 0.0)
        b = a * b_s + b
        a = a * a_s
        d *= 2
    return a, b


def _rscan_block(c, g, rows):
    d = 1
    while d < c.shape[0]:
        c_s = _shift_up(c, d, rows, 1.0)
        g_s = _shift_up(g, d, rows, 0.0)
        g = g + c * g_s
        c = c * c_s
        d *= 2
    return c, g


def _lru_gates(xa, wa_ref, ba_ref, wx_ref, bx_ref, lam_ref):
    sp = _softplus(-lam_ref[...])
    xb = _bf(xa)
    r = _sigmoid(_mm(xb, wa_ref[...]) + ba_ref[...])
    ig = _sigmoid(_mm(xb, wx_ref[...]) + bx_ref[...])
    la = -LRU_C * r * sp
    a = jnp.exp(la)
    mult = jnp.sqrt(-_expm1(2.0 * la))
    return sp, r, ig, la, a, mult


def _lru_specs(S):
    t128 = pl.BlockSpec((1, LANE), lambda t: (0, t))
    return [pl.BlockSpec((S, 2 * LANE), lambda t: (0, OFF_LRU // (2 * LANE) + t)),
            pl.BlockSpec((4, LANE), lambda t: (0, t)), t128,
            pl.BlockSpec((None, LANE, LANE), lambda t: (t, 0, 0)), t128,
            pl.BlockSpec((None, LANE, LANE), lambda t: (t, 0, 0)), t128, t128]


def _lru_fwd(u, cw, cb, wa, ba, wx, bx, lam, ycat):
    S = u.shape[0]
    tb = min(256, S)

    def body(u_ref, cw_ref, cb_ref, wa_ref, ba_ref, wx_ref, bx_ref, lam_ref, ycat_in, ycat_ref, h_ref, a_scr, b_scr):
        del ycat_in
        rows = _iota((S, LANE), 0)
        xa = _conv_fwd(u_ref[:, 0:LANE], cw_ref, cb_ref, rows)
        _, _, ig, _, a, mult = _lru_gates(xa, wa_ref, ba_ref, wx_ref, bx_ref, lam_ref)
        a_scr[...] = a
        b_scr[...] = mult * (ig * xa)
        rows_b = _iota((tb, LANE), 0)

        def blk(j, hprev):
            sl = pl.ds(pl.multiple_of(j * tb, tb), tb)
            acum, hloc = _scan_block(a_scr[sl, :], b_scr[sl, :], rows_b)
            hf = hloc + acum * hprev
            h_ref[sl, :] = hf
            return _last_row(hf, rows_b)

        lax.fori_loop(0, S // tb, blk, jnp.zeros((1, LANE), F32))
        ycat_ref[...] = _bf(h_ref[...] * _silu(u_ref[:, LANE:2 * LANE]))

    col = pl.BlockSpec((S, LANE), lambda t: (0, t))
    return pl.pallas_call(
        body, name="lru_fwd", grid=(LRU_W // LANE,),
        in_specs=_lru_specs(S) + [pl.BlockSpec(memory_space=pl.ANY)],
        out_specs=[col, col],
        out_shape=[SDS((S, D_INNER), BF16), SDS((S,LRU_W), F32)],
        scratch_shapes=[pltpu.VMEM((S, LANE), F32), pltpu.VMEM((S, LANE), F32)],
        input_output_aliases={8: 0},
        compiler_params=_cp(("parallel",)),
    )(u, cw, cb, wa, ba, wx, bx, lam, ycat)


def _lru_bwd(u, cw, cb, wa, ba, wx, bx, lam, h_lru, dycat, du):
    S = u.shape[0]
    tb = min(256, S)

    def body(u_ref, cw_ref, cb_ref, wa_ref, ba_ref, wx_ref, bx_ref, lam_ref, h_ref, dy_ref, du_in,
             du_ref, red_ref, gwa_ref, gwx_ref, c_scr, g_scr, l_scr):
        del du_in
        rows = _iota((S, LANE), 0)
        ax = u_ref[:, 0:LANE]
        ag = u_ref[:, LANE:2 * LANE]
        xa = _conv_fwd(ax, cw_ref, cb_ref, rows)
        sp, r, ig, la, a, mult = _lru_gates(xa, wa_ref, ba_ref, wx_ref, bx_ref, lam_ref)
        h = h_ref[...]
        dy = dy_ref[...]
        du_ref[:, LANE:2 * LANE] = _bf(dy * h * _dsilu(ag))
        c_scr[...] = _shift_up(a, 1, rows, 0.0)
        g_scr[...] = dy * _silu(ag)
        rows_b = _iota((tb, LANE), 0)
        nb = S // tb

        def blk(jj, lnext):
            j = nb - 1 - jj
            sl = pl.ds(pl.multiple_of(j * tb, tb), tb)
            ccum, lloc = _rscan_block(c_scr[sl, :], g_scr[sl, :], rows_b)
            lam_t = lloc + ccum * lnext
            l_scr[sl, :] = lam_t
            return jnp.sum(jnp.where(rows_b == 0, lam_t, 0.0), axis=0, keepdims=True)

        lax.fori_loop(0, nb, blk, jnp.zeros((1, LANE), F32))
        db = l_scr[...]
        da = db * _shift_down(h, 1, rows)
        dmult = db * ig * xa
        dig = db * mult * xa
        dxa = db * mult * ig
        dla = da * a - dmult * (a * a) / mult
        dr = -LRU_C * sp * dla
        dsp = jnp.sum(-LRU_C * r * dla, axis=0, keepdims=True)
        dlam = -dsp * _sigmoid(-lam_ref[...])
        dzr = dr * r * (1.0 - r)
        dzi = dig * ig * (1.0 - ig)
        dzr_b, dzi_b, xa_b = _bf(dzr), _bf(dzi), _bf(xa)
        dxa = dxa + _mm_nt(dzr_b, wa_ref[...]) + _mm_nt(dzi_b, wx_ref[...])
        gwa_ref[...] = _mm_tn(xa_b, dzr_b)
        gwx_ref[...] = _mm_tn(xa_b, dzi_b)
        dax, dws, dcb = _conv_bwd(ax, dxa, cw_ref, rows)
        du_ref[:, 0:LANE] = _bf(dax)
        parts = dws + [dcb, jnp.sum(dzr, axis=0, keepdims=True), jnp.sum(dzi, axis=0, keepdims=True), dlam]
        for n, p in enumerate(parts):
            red_ref[pl.ds(n, 1), :] = p

    col = pl.BlockSpec((S, LANE), lambda t: (0, t))
    gw = pl.BlockSpec((None, LANE, LANE), lambda t: (t, 0, 0))
    return pl.pallas_call(
        body, name="lru_bwd", grid=(LRU_W // LANE,),
        in_specs=_lru_specs(S) + [col, col, pl.BlockSpec(memory_space=pl.ANY)],
        out_specs=[pl.BlockSpec((S, 2 * LANE), lambda t: (0, OFF_LRU // (2 * LANE) + t)),
                   pl.BlockSpec((8, LANE), lambda t: (0, t)), gw, gw],
        out_shape=[SDS((S, N_PAD), BF16), SDS((8, LRU_W), F32), SDS((4, LANE, LANE), F32), SDS((4, LANE, LANE), F32)],
        scratch_shapes=[pltpu.VMEM((S, LANE), F32)] * 3,
        input_output_aliases={10: 0},
        compiler_params=_cp(("parallel",)),
    )(u, cw, cb, wa, ba, wx, bx, lam, h_lru, dycat, du)


HG_LEVELS = 6


def _hg_consts():
    C = HG_CHUNK
    t = np.arange(C)[:, None]
    r = np.arange(C)[None, :]
    mats = []
    for side in ("q", "k"):
        for l in range(HG_LEVELS):
            b = 1 << l
            upper = (t % (2 * b)) >= b
            anchor = (t // (2 * b)) * 2 * b + b - 1
            if side == "q":
                mats.append(upper & (r > anchor) & (r <= t))
            else:
                mats.append((~upper) & (r > t) & (r <= anchor))
    mats.append(r <= t)
    mats.append(r > t)
    return np.concatenate(mats, 0).astype(np.float32)


def _hg_factors(hf, lb, mall):
    s = _sigmoid(hf)
    f = lb + (1.0 - lb) * s
    lf = jnp.log(f)
    k = (1.0 - lb) * _sigmoid(-hf)
    e = jnp.exp(_sel_l(mall, lf))
    C = HG_CHUNK
    eq = [e[l * C:(l + 1) * C] for l in range(HG_LEVELS)]
    ek = [e[(HG_LEVELS + l) * C:(HG_LEVELS + l + 1) * C] for l in range(HG_LEVELS)]
    ecum = e[2 * HG_LEVELS * C:(2 * HG_LEVELS + 1) * C]
    erem = e[(2 * HG_LEVELS + 1) * C:(2 * HG_LEVELS + 2) * C]
    return s, f, k, eq, ek, ecum, erem


def _hg_masks():
    C = HG_CHUNK
    ri, ci = _iota((C, C), 0), _iota((C, C), 1)
    rr = _iota((C, LANE), 0)
    gm = [(lax.shift_right_logical(ri, l + 1) == lax.shift_right_logical(ci, l + 1)).astype(F32)
          for l in range(HG_LEVELS)]
    up = [(lax.shift_right_logical(rr, l) & 1) == 1 for l in range(HG_LEVELS)]
    eye = (ri == ci).astype(F32)
    return gm, up, eye, rr


def _hg_scores(qh, kh, eq, ek, sl, gm, up, eye):
    qs, ks = [], []
    p = _mm_nt(_bf(qh), _bf(kh)) * eye
    for l in range(HG_LEVELS):
        ql = jnp.where(up[l], qh * eq[l][:, sl], 0.0)
        kl = jnp.where(up[l], 0.0, kh * ek[l][:, sl])
        p = p + _mm_nt(_bf(ql), _bf(kl)) * gm[l]
        qs.append(ql)
        ks.append(kl)
    return p, qs, ks


def _hg_fwd(u, lb, nw, mall, ycat):
    S = u.shape[0]
    C = HG_CHUNK
    n = S // C

    def body(u_ref, lb_ref, nw_ref, mall_ref, ycat_in, ycat_ref, o_ref, st_ref, st):
        del ycat_in

        @pl.when(pl.program_id(0) == 0)
        def _():
            st[...] = jnp.zeros_like(st)

        q = _silu(u_ref[:, 0:512])
        v = u_ref[:, 1024:1536]
        _, _, k, eq, ek, ecum, erem = _hg_factors(u_ref[:, 512:1024], lb_ref[...], mall_ref[...])
        gm, up, eye, rr = _hg_masks()
        for h in range(HG_HEADS):
            sl = slice(h * LANE, (h + 1) * LANE)
            qh, kh, vh = q[:, sl], k[:, sl], _bf(v[:, sl])
            p, _, _ = _hg_scores(qh, kh, eq, ek, sl, gm, up, eye)
            sth = st[h]
            st_ref[h] = sth
            o_ref[:, sl] = _mm(_bf(p), vh) + _mm_nt(_bf(qh * ecum[:, sl]), _bf(sth))
            st[h] = sth * _last_row(ecum[:, sl], rr) + _mm_tn(vh, _bf(kh * erem[:, sl]))
        o = o_ref[...]
        inv = lax.rsqrt(jnp.mean(o * o, axis=-1, keepdims=True) + EPS)
        ycat_ref[...] = _bf((o * inv) * nw_ref[...] * _silu(u_ref[:, 1536:2048]))

    return pl.pallas_call(
        body, name="hg_fwd", grid=(n,),
        in_specs=[pl.BlockSpec((C, 2048), lambda i: (i, 0)), _vec(HG_W), _vec(HG_W), _full(mall.shape),
                  pl.BlockSpec(memory_space=pl.ANY)],
        out_specs=[pl.BlockSpec((C, HG_W), lambda i: (i, 1)), pl.BlockSpec((C, HG_W), lambda i: (i, 0)),
                   pl.BlockSpec((None, HG_HEADS, LANE, LANE), lambda i: (i, 0, 0, 0))],
        out_shape=[SDS((S, D_INNER), BF16), SDS((S,HG_W), F32), SDS((n, HG_HEADS, LANE, LANE), F32)],
        scratch_shapes=[pltpu.VMEM((HG_HEADS, LANE, LANE), F32)],
        input_output_aliases={4: 0},
        compiler_params=_cp(("arbitrary",)),
    )(u, lb, nw, mall, ycat)


def _hg_bwd(u, lb, nw, mall, mall_t, o_b, states, dycat, du):
    S = u.shape[0]
    C = HG_CHUNK
    n = S // C
    L2 = 2 * HG_LEVELS

    def body(u_ref, lb_ref, nw_ref, mall_ref, mallt_ref, o_ref, st_ref, dy_ref, du_in, du_ref, red_ref,
             dst, dlast_s, dq_s, dk_s, dex):
        del du_in

        @pl.when(pl.program_id(0) == 0)
        def _():
            dst[...] = jnp.zeros_like(dst)
            red_ref[...] = jnp.zeros_like(red_ref)

        lb = lb_ref[...]
        hq, hf, hg = u_ref[:, 0:512], u_ref[:, 512:1024], u_ref[:, 1536:2048]
        q = _silu(hq)
        v = u_ref[:, 1024:1536]
        s, f, k, eq, ek, ecum, erem = _hg_factors(hf, lb, mall_ref[...])
        gm, up, eye, rr = _hg_masks()
        o = o_ref[...]
        dy = dy_ref[...]
        inv = lax.rsqrt(jnp.mean(o * o, axis=-1, keepdims=True) + EPS)
        ohat = o * inv
        nwv = nw_ref[...]
        du_ref[:, 1536:2048] = _bf(dy * ohat * nwv * _dsilu(hg))
        dn = dy * _silu(hg)
        red_ref[0:1, :] += jnp.sum(dn * ohat, axis=0, keepdims=True)
        dohat = dn * nwv
        do = inv * (dohat - ohat * jnp.mean(dohat * ohat, axis=-1, keepdims=True))
        for h in range(HG_HEADS):
            sl = slice(h * LANE, (h + 1) * LANE)
            qh, kh, vh, doh = q[:, sl], k[:, sl], _bf(v[:, sl]), _bf(do[:, sl])
            p, qs, ks = _hg_scores(qh, kh, eq, ek, sl, gm, up, eye)
            st_f = st_ref[h]
            sth = _bf(st_f)
            dsth = dst[h]
            dsth_b = _bf(dsth)
            qt = qh * ecum[:, sl]
            kt = kh * erem[:, sl]
            elast = _last_row(ecum[:, sl], rr)
            dp = _mm_nt(doh, vh)
            du_ref[:, 1024 + h * LANE:1024 + (h + 1) * LANE] = _bf(_mm_tn(_bf(p), doh) + _mm_nt(_bf(kt), dsth_b))
            dpe = _bf(dp * eye)
            dqt = _mm(doh, sth)
            dkt = _mm(vh, dsth_b)
            dq = dqt * ecum[:, sl] + _mm(dpe, _bf(kh))
            dk = dkt * erem[:, sl] + _mm_tn(dpe, _bf(qh))
            dex[L2 * C:(L2 + 1) * C, sl] = dqt * qt
            dex[(L2 + 1) * C:(L2 + 2) * C, sl] = dkt * kt
            for l in range(HG_LEVELS):
                dpl = _bf(dp * gm[l])
                dql = _mm(dpl, _bf(ks[l]))
                dkl = _mm_tn(dpl, _bf(qs[l]))
                dq = dq + jnp.where(up[l], dql * eq[l][:, sl], 0.0)
                dk = dk + jnp.where(up[l], 0.0, dkl * ek[l][:, sl])
                dex[l * C:(l + 1) * C, sl] = dql * qs[l]
                dex[(HG_LEVELS + l) * C:(HG_LEVELS + l + 1) * C, sl] = dkl * ks[l]
            dlast_s[:, sl] = jnp.sum(dsth * st_f, axis=0, keepdims=True) * elast
            dst[h] = dsth * elast + _mm_tn(doh, _bf(qt))
            dq_s[:, sl] = dq
            dk_s[:, sl] = dk
        dq = dq_s[...]
        dk = dk_s[...]
        dlf = _sel_l(mallt_ref[...], dex[...]) + dlast_s[...]
        du_ref[:, 0:512] = _bf(dq * _dsilu(hq))
        t = (1.0 - s) * (dlf / f - dk)
        du_ref[:, 512:1024] = _bf((1.0 - lb) * s * t)
        red_ref[1:2, :] += jnp.sum(t, axis=0, keepdims=True)

    rev = lambda i: (n - 1 - i, 0)
    return pl.pallas_call(
        body, name="hg_bwd", grid=(n,),
        in_specs=[pl.BlockSpec((C, 2048), rev), _vec(HG_W), _vec(HG_W), _full(mall.shape), _full(mall_t.shape),
                  pl.BlockSpec((C, HG_W), rev),
                  pl.BlockSpec((None, HG_HEADS, LANE, LANE), lambda i: (n - 1 - i, 0, 0, 0)),
                  pl.BlockSpec((C, HG_W), lambda i: (n - 1 - i, 1)), pl.BlockSpec(memory_space=pl.ANY)],
        out_specs=[pl.BlockSpec((C, 2048), rev), pl.BlockSpec((8, HG_W), lambda i: (0, 0))],
        out_shape=[SDS((S, N_PAD), BF16), SDS((8, HG_W), F32)],
        scratch_shapes=[pltpu.VMEM((HG_HEADS, LANE, LANE), F32), pltpu.VMEM((1, HG_W), F32),
                        pltpu.VMEM((C, HG_W), F32), pltpu.VMEM((C, HG_W), F32), pltpu.VMEM(((L2 + 2) * C, HG_W), F32)],
        input_output_aliases={8: 0},
        compiler_params=_cp(("arbitrary",)),
    )(u, lb, nw, mall, mall_t, o_b, states, dycat, du)


def _ssdconv_fwd(u, cw, cb):
    S = u.shape[0]

    def body(u_ref, cw_ref, cb_ref, out_ref):
        rows = _iota((S, LANE), 0)
        out_ref[...] = _silu(_conv_fwd(u_ref[...], cw_ref, cb_ref, rows))

    return pl.pallas_call(
        body, name="ssdconv_fwd", grid=(SSD_CONV // LANE,),
        in_specs=[pl.BlockSpec((S, LANE), lambda t: (0, OFF_XBC // LANE + t)), pl.BlockSpec((4, LANE), lambda t: (0, t)),
                  pl.BlockSpec((1, LANE), lambda t: (0, t))],
        out_specs=pl.BlockSpec((S, LANE), lambda t: (0, t)),
        out_shape=SDS((S, SSD_CONV), F32),
        compiler_params=_cp(("parallel",)),
    )(u, cw, cb)


def _ssdconv_bwd(u, cw, cb, dxbc, du):
    S = u.shape[0]

    def body(u_ref, cw_ref, cb_ref, d_ref, du_in, du_ref, red_ref):
        del du_in
        rows = _iota((S, LANE), 0)
        x = u_ref[...]
        dco = d_ref[...] * _dsilu(_conv_fwd(x, cw_ref, cb_ref, rows))
        dx, dws, dcb = _conv_bwd(x, dco, cw_ref, rows)
        du_ref[...] = _bf(dx)
        for n, p in enumerate(dws + [dcb]):
            red_ref[pl.ds(n, 1), :] = p
        red_ref[pl.ds(5, 3), :] = jnp.zeros((3, LANE), F32)

    ucol = pl.BlockSpec((S, LANE), lambda t: (0, OFF_XBC // LANE + t))
    return pl.pallas_call(
        body, name="ssdconv_bwd", grid=(SSD_CONV // LANE,),
        in_specs=[ucol, pl.BlockSpec((4, LANE), lambda t: (0, t)), pl.BlockSpec((1, LANE), lambda t: (0, t)),
                  pl.BlockSpec((S, LANE), lambda t: (0, t)), pl.BlockSpec(memory_space=pl.ANY)],
        out_specs=[ucol, pl.BlockSpec((8, LANE), lambda t: (0, t))],
        out_shape=[SDS((S, N_PAD), BF16), SDS((8, SSD_CONV), F32)],
        input_output_aliases={4: 0},
        compiler_params=_cp(("parallel",)),
    )(u, cw, cb, dxbc, du)


def _ssd_consts():
    e64 = np.zeros((LANE, SSD_W), np.float32)
    e128 = np.zeros((LANE, SSD_HEADS * LANE), np.float32)
    for h in range(SSD_HEADS):
        e64[h, h * SSD_P:(h + 1) * SSD_P] = 1.0
        e128[h, h * LANE:(h + 1) * LANE] = 1.0
    T = SSD_CHUNK
    tril = (np.arange(T)[None, :] <= np.arange(T)[:, None]).astype(np.float32)
    return e64, e128, tril, tril.T.copy()


def _ssd_common(zdt, bias_ref, alog_ref, tril, e64, e128):
    T = SSD_CHUNK
    lane = _iota((1, LANE), 1)
    a_neg = jnp.where(lane < SSD_HEADS, -jnp.exp(alog_ref[...]), 0.0)
    dtpre = zdt[:, SSD_W:SSD_W + LANE] + bias_ref[...]
    dt = _softplus(dtpre)
    cum = _sel_l(tril, dt * a_neg)
    rowsT = _iota((T, LANE), 0)
    last = _last_row(cum, rowsT)
    ecum_x = _sel_r(jnp.exp(cum), e64)
    erem_x = _sel_r(jnp.exp(last - cum), e64)
    elast_x = _last_row(ecum_x, _iota((T, SSD_W), 0))
    dt_x = _sel_r(dt, e64)
    cum_e = _sel_r(cum, e128)
    return a_neg, dtpre, dt, cum, ecum_x, erem_x, elast_x, dt_x, cum_e


def _ssd_decay(cum_e, cumt_ref, h, causal):
    diff = cum_e[:, h * LANE:(h + 1) * LANE] - cumt_ref[pl.ds(h, 1), :]
    return jnp.exp(jnp.where(causal, diff, NEG))


def _group_norm_fwd(y1, nwv):
    outs, invs = [], []
    for g in range(2):
        seg = y1[:, g * 512:(g + 1) * 512]
        inv = lax.rsqrt(jnp.mean(seg * seg, axis=-1, keepdims=True) + EPS)
        outs.append(seg * inv * nwv[:, g * 512:(g + 1) * 512])
        invs.append(inv)
    return outs, invs


def _ssd_fwd(u, xbc, bias, alog, dskip_x, nw, consts, ycat):
    S = u.shape[0]
    T = SSD_CHUNK
    n = S // T
    e64, e128, tril, _ = consts

    def body(u_ref, xbc_ref, bias_ref, alog_ref, dx_ref, nw_ref, e64_ref, e128_ref, tril_ref, ycat_in,
             ycat_ref, y_ref, st_ref, st, cumt):
        del ycat_in

        @pl.when(pl.program_id(0) == 0)
        def _():
            st[...] = jnp.zeros_like(st)

        zdt = u_ref[...]
        z = zdt[:, 0:SSD_W]
        xs = xbc_ref[:, 0:SSD_W]
        _, _, _, cum, ecum_x, erem_x, elast_x, dt_x, cum_e = _ssd_common(
            zdt, bias_ref, alog_ref, tril_ref[...], e64_ref[...], e128_ref[...])
        cumt[...] = cum.T
        causal = _iota((T, T), 0) >= _iota((T, T), 1)
        lo = _iota((T, LANE), 1) < SSD_P
        xdt = xs * dt_x
        xrem = xdt * erem_x
        st_ref[...] = st[...]
        for g in range(2):
            gs = slice(g * 512, (g + 1) * 512)
            bg = _bf(xbc_ref[:, SSD_W + g * LANE:SSD_W + (g + 1) * LANE])
            cg = _bf(xbc_ref[:, SSD_W + 256 + g * LANE:SSD_W + 256 + (g + 1) * LANE])
            cb = _mm_nt(cg, bg)
            yin = _mm(cg, _bf(st[:, gs])) * ecum_x[:, gs]
            for j in range(4):
                h0 = 8 * g + 2 * j
                cs = slice(h0 * SSD_P, (h0 + 2) * SSD_P)
                xp = xdt[:, cs]
                s0 = _bf(cb * _ssd_decay(cum_e, cumt, h0, causal))
                s1 = _bf(cb * _ssd_decay(cum_e, cumt, h0 + 1, causal))
                y_ref[:, cs] = (_mm(s0, _bf(jnp.where(lo, xp, 0.0))) + _mm(s1, _bf(jnp.where(lo, 0.0, xp)))
                                + yin[:, j * LANE:(j + 1) * LANE])
            st[:, gs] = st[:, gs] * elast_x[:, gs] + _mm_tn(bg, _bf(xrem[:, gs]))
        y1 = (y_ref[...] + dx_ref[...] * xs) * _silu(z)
        outs, _ = _group_norm_fwd(y1, nw_ref[...])
        for g in range(2):
            ycat_ref[:, g * 512:(g + 1) * 512] = _bf(outs[g])

    return pl.pallas_call(
        body, name="ssd_fwd", grid=(n,),
        in_specs=[pl.BlockSpec((T, SSD_W + LANE), lambda i: (i, OFF_Z // (SSD_W + LANE))),
                  pl.BlockSpec((T, SSD_CONV), lambda i: (i, 0)), _vec(LANE), _vec(LANE), _vec(SSD_W), _vec(SSD_W),
                  _full(e64.shape), _full(e128.shape), _full(tril.shape), pl.BlockSpec(memory_space=pl.ANY)],
        out_specs=[pl.BlockSpec((T, SSD_W), lambda i: (i, 1)), pl.BlockSpec((T, SSD_W), lambda i: (i, 0)),
                   pl.BlockSpec((None, SSD_N, SSD_W), lambda i: (i, 0, 0))],
        out_shape=[SDS((S, D_INNER), BF16), SDS((S,SSD_W), F32), SDS((n, SSD_N, SSD_W), F32)],
        scratch_shapes=[pltpu.VMEM((SSD_N, SSD_W), F32), pltpu.VMEM((LANE, T), F32)],
        input_output_aliases={9: 0},
        compiler_params=_cp(("arbitrary",)),
    )(u, xbc, bias, alog, dskip_x, nw, _bfc(e64), _bfc(e128), _bfc(tril), ycat)


def _ssd_bwd(u, xbc, bias, alog, dskip_x, nw, consts, y_ssd, states, dycat, du):
    S = u.shape[0]
    T = SSD_CHUNK
    n = S // T
    e64, e128, tril, triu = consts
    e64t = np.ascontiguousarray(e64.T)

    def body(u_ref, xbc_ref, bias_ref, alog_ref, dx_ref, nw_ref, e64_ref, e64t_ref, e128_ref, tril_ref, triu_ref,
             y_ref, st_ref, dy_ref, du_in, du_ref, dxbc_ref, red_ref, dst, dl_s, cumt, dxdt_s, dy0_s, gb_s, gc_s):
        del du_in

        @pl.when(pl.program_id(0) == 0)
        def _():
            dst[...] = jnp.zeros_like(dst)
            red_ref[...] = jnp.zeros_like(red_ref)

        zdt = u_ref[...]
        z = zdt[:, 0:SSD_W]
        xs = xbc_ref[:, 0:SSD_W]
        e64m = e64_ref[...]
        a_neg, dtpre, dt, cum, ecum_x, erem_x, elast_x, dt_x, cum_e = _ssd_common(
            zdt, bias_ref, alog_ref, tril_ref[...], e64m, e128_ref[...])
        cumt[...] = cum.T
        causal = _iota((T, T), 0) >= _iota((T, T), 1)
        lo = _iota((T, LANE), 1) < SSD_P
        xdt = xs * dt_x
        xrem = xdt * erem_x
        y = y_ref[...]
        dxv = dx_ref[...]
        nwv = nw_ref[...]
        sz = _silu(z)
        y0 = y + dxv * xs
        y1 = y0 * sz
        for g in range(2):
            gs = slice(g * 512, (g + 1) * 512)
            seg = y1[:, gs]
            inv = lax.rsqrt(jnp.mean(seg * seg, axis=-1, keepdims=True) + EPS)
            shat = seg * inv
            dyg = dy_ref[:, gs]
            red_ref[0:1, gs] += jnp.sum(dyg * shat, axis=0, keepdims=True)
            dsh = dyg * nwv[:, gs]
            dy1g = inv * (dsh - shat * jnp.mean(dsh * shat, axis=-1, keepdims=True))
            du_ref[:, gs] = _bf(dy1g * y0[:, gs] * _dsilu(z[:, gs]))
            dy0_s[:, gs] = dy1g * sz[:, gs]
        dy0 = dy0_s[...]
        red_ref[1:2, :] += jnp.sum(dy0 * xs, axis=0, keepdims=True)
        dyin = dy0 * ecum_x
        lane = _iota((T, LANE), 1)
        ones = jnp.ones((T, LANE), BF16)
        dcum = jnp.zeros((T, LANE), F32)

        def row_minus_col(gm):
            hi = _bf(gm)
            lw = _bf(gm - hi.astype(F32))
            return _mm(hi, ones) + _mm(lw, ones) - _mm_tn(hi, ones) - _mm_tn(lw, ones)

        for g in range(2):
            gs = slice(g * 512, (g + 1) * 512)
            bg = _bf(xbc_ref[:, SSD_W + g * LANE:SSD_W + (g + 1) * LANE])
            cg = _bf(xbc_ref[:, SSD_W + 256 + g * LANE:SSD_W + 256 + (g + 1) * LANE])
            cb = _mm_nt(cg, bg)
            dst_f, st_f = dst[:, gs], st_ref[:, gs]
            dstg = _bf(dst_f)
            stg = _bf(st_f)
            dyin_g = _bf(dyin[:, gs])
            xrem_g = _bf(xrem[:, gs])
            dcb = jnp.zeros((T, T), F32)
            dxr = _mm(bg, dstg)
            dxdt_s[:, gs] = dxr * erem_x[:, gs]
            gc_s[:, gs] = dxr * xrem[:, gs]
            gb_s[:, gs] = dyin[:, gs] * _mm(cg, stg)
            dl_s[:, gs] = jnp.sum(dst_f * st_f, axis=0, keepdims=True) * elast_x[:, gs]
            for j in range(4):
                h0 = 8 * g + 2 * j
                cs = slice(h0 * SSD_P, (h0 + 2) * SSD_P)
                xp = xdt[:, cs]
                dyp = dy0[:, cs]
                x_lo, x_hi = _bf(jnp.where(lo, xp, 0.0)), _bf(jnp.where(lo, 0.0, xp))
                d_lo, d_hi = _bf(jnp.where(lo, dyp, 0.0)), _bf(jnp.where(lo, 0.0, dyp))
                s0 = cb * _ssd_decay(cum_e, cumt, h0, causal)
                s1 = cb * _ssd_decay(cum_e, cumt, h0 + 1, causal)
                ds0 = _mm_nt(d_lo, x_lo)
                ds1 = _mm_nt(d_hi, x_hi)
                dcb = dcb + ds0 * _ssd_decay(cum_e, cumt, h0, causal) + ds1 * _ssd_decay(cum_e, cumt, h0 + 1, causal)
                dxdt_s[:, cs] += _mm_tn(_bf(s0), d_lo) + _mm_tn(_bf(s1), d_hi)
                dcum = dcum + jnp.where(lane == h0, row_minus_col(ds0 * s0), 0.0)
                dcum = dcum + jnp.where(lane == h0 + 1, row_minus_col(ds1 * s1), 0.0)
            dcb_b = _bf(dcb)
            dxbc_ref[:, SSD_W + g * LANE:SSD_W + (g + 1) * LANE] = _mm_tn(dcb_b, cg) + _mm_nt(xrem_g, dstg)
            dxbc_ref[:, SSD_W + 256 + g * LANE:SSD_W + 256 + (g + 1) * LANE] = _mm(dcb_b, bg) + _mm_nt(dyin_g, stg)
            dst[:, gs] = dst_f * elast_x[:, gs] + _mm_tn(cg, dyin_g)
        dxdt = dxdt_s[...]
        dxbc_ref[:, 0:SSD_W] = dxdt * dt_x + dy0 * dxv
        e64t = e64t_ref[...]
        hc = _sel_r(gc_s[...], e64t)
        dlast = (jnp.sum(hc, axis=0, keepdims=True)
                 + jnp.max(_sel_r(jnp.broadcast_to(dl_s[...], (8, SSD_W)), e64t), axis=0, keepdims=True))
        dcum = dcum + _sel_r(gb_s[...], e64t) - hc + jnp.where(_iota((T, LANE), 0) == T - 1, dlast, 0.0)
        dda = _sel_l(triu_ref[...], dcum)
        ddt = dda * a_neg + _sel_r(dxdt * xs, e64t)
        ddtpre = ddt * _sigmoid(dtpre)
        du_ref[:, SSD_W:SSD_W + LANE] = _bf(jnp.where(lane < SSD_HEADS, ddtpre, 0.0))
        red_ref[2:3, 0:LANE] += jnp.sum(ddtpre, axis=0, keepdims=True)
        red_ref[3:4, 0:LANE] += jnp.sum(dda * dt, axis=0, keepdims=True)

    rev = lambda i: (n - 1 - i, 0)
    return pl.pallas_call(
        body, name="ssd_bwd", grid=(n,),
        in_specs=[pl.BlockSpec((T, SSD_W + LANE), lambda i: (n - 1 - i, OFF_Z // (SSD_W + LANE))),
                  pl.BlockSpec((T, SSD_CONV), rev), _vec(LANE), _vec(LANE), _vec(SSD_W), _vec(SSD_W),
                  _full(e64.shape), _full(e64t.shape), _full(e128.shape), _full(tril.shape), _full(triu.shape),
                  pl.BlockSpec((T, SSD_W), rev), pl.BlockSpec((None, SSD_N, SSD_W), lambda i: (n - 1 - i, 0, 0)),
                  pl.BlockSpec((T, SSD_W), lambda i: (n - 1 - i, 1)), pl.BlockSpec(memory_space=pl.ANY)],
        out_specs=[pl.BlockSpec((T, SSD_W + LANE), lambda i: (n - 1 - i, OFF_Z // (SSD_W + LANE))),
                   pl.BlockSpec((T, SSD_CONV), rev), pl.BlockSpec((8, SSD_W), lambda i: (0, 0))],
        out_shape=[SDS((S, N_PAD), BF16), SDS((S, SSD_CONV), F32), SDS((8, SSD_W), F32)],
        scratch_shapes=[pltpu.VMEM((SSD_N, SSD_W), F32), pltpu.VMEM((1, SSD_W), F32), pltpu.VMEM((LANE, T), F32)]
        + [pltpu.VMEM((T, SSD_W), F32)] * 4,
        input_output_aliases={14: 0},
        compiler_params=_cp(("arbitrary",)),
    )(u, xbc, bias, alog, dskip_x, nw, _bfc(e64), _bfc(e64t), _bfc(e128), _bfc(tril), _bfc(triu), y_ssd, states, dycat, du)


def _bfc(a):
    return jnp.asarray(a, BF16)


def _outproj_fwd(ycat, wo, x, gate):
    S = x.shape[0]
    tm = min(512, S)

    def body(yc_ref, wo_ref, x_ref, g_ref, xn_ref, y_ref):
        y = _mm(_bf(yc_ref[...]), wo_ref[...])
        y_ref[...] = y
        xn_ref[...] = x_ref[...] + g_ref[...] * y

    row = pl.BlockSpec((tm, D_MODEL), lambda i: (i, 0))
    return pl.pallas_call(
        body, name="outproj_fwd", grid=(S // tm,),
        in_specs=[pl.BlockSpec((tm, D_INNER), lambda i: (i, 0)), _full((D_INNER, D_MODEL)), row, _vec(D_MODEL)],
        out_specs=[row, row],
        out_shape=[SDS((S, D_MODEL), F32), SDS((S, D_MODEL), F32)],
        compiler_params=_cp(("parallel",)),
    )(ycat, wo, x, gate)


def _outproj_bwd(dxn, y, gate, ycat, wo):
    S = dxn.shape[0]
    tm = min(512, S)

    def body(dx_ref, y_ref, g_ref, yc_ref, wo_ref, dyc_ref, gwo_ref, dg_ref, acc):
        @pl.when(pl.program_id(0) == 0)
        def _():
            acc[...] = jnp.zeros_like(acc)
            dg_ref[...] = jnp.zeros_like(dg_ref)

        dxv = dx_ref[...]
        dy = _bf(dxv * g_ref[...])
        dg_ref[0:1, :] += jnp.sum(dxv * y_ref[...], axis=0, keepdims=True)
        dyc_ref[...] = _mm_nt(dy, wo_ref[...])
        acc[...] += _mm_tn(_bf(yc_ref[...]), dy)

        @pl.when(pl.program_id(0) == pl.num_programs(0) - 1)
        def _():
            gwo_ref[...] = acc[...].astype(BF16)

    row = pl.BlockSpec((tm, D_MODEL), lambda i: (i, 0))
    wide = pl.BlockSpec((tm, D_INNER), lambda i: (i, 0))
    return pl.pallas_call(
        body, name="outproj_bwd", grid=(S // tm,),
        in_specs=[row, row, _vec(D_MODEL), wide, _full((D_INNER, D_MODEL))],
        out_specs=[wide, _full((D_INNER, D_MODEL)), _full((8, D_MODEL))],
        out_shape=[SDS((S, D_INNER), F32), SDS((D_INNER, D_MODEL), BF16), SDS((8, D_MODEL), F32)],
        scratch_shapes=[pltpu.VMEM((D_INNER, D_MODEL), F32)],
        compiler_params=_cp(("arbitrary",)),
    )(dxn, y, gate, ycat, wo)


def _loss_head(x, fw, target):
    S = x.shape[0]
    tm = min(512, S)

    def body(x_ref, fw_ref, t_ref, dx_ref, red_ref):
        @pl.when(pl.program_id(0) == 0)
        def _():
            red_ref[...] = jnp.zeros_like(red_ref)

        xv = x_ref[...]
        fwv = fw_ref[...]
        inv = lax.rsqrt(jnp.mean(xv * xv, axis=-1, keepdims=True) + EPS)
        xhat = xv * inv
        err = xhat * fwv - t_ref[...]
        col = jnp.sum(err * err, axis=0, keepdims=True)
        red_ref[1:2, :] += jnp.broadcast_to(jnp.sum(col, axis=1, keepdims=True) * (0.5 / D_MODEL), (1, D_MODEL))
        dy = err * (1.0 / D_MODEL)
        red_ref[0:1, :] += jnp.sum(dy * xhat, axis=0, keepdims=True)
        dxhat = dy * fwv
        dx_ref[...] = inv * (dxhat - xhat * jnp.mean(dxhat * xhat, axis=-1, keepdims=True))

    row = pl.BlockSpec((tm, D_MODEL), lambda i: (i, 0))
    return pl.pallas_call(
        body, name="loss_head", grid=(S // tm,),
        in_specs=[row, _vec(D_MODEL), row],
        out_specs=[row, _full((8, D_MODEL))],
        out_shape=[SDS((S, D_MODEL), F32), SDS((8, D_MODEL), F32)],
        compiler_params=_cp(("arbitrary",)),
    )(x, fw, target)


ADA_COLS = 3 * D_MODEL // N_DEV


def _ada_fwd(c_all, w_ada, b_cols):
    def body(c_ref, w_ref, b_ref, out_ref):
        out_ref[...] = _mm(_bf(_silu(c_ref[...])), _bf(w_ref[...])) + b_ref[...]

    return pl.pallas_call(
        body, name="ada_fwd", grid=(DEPTH,),
        in_specs=[_full((N_DEV, D_MODEL)), pl.BlockSpec((None, D_MODEL, ADA_COLS), lambda l: (l, 0, 0)),
                  pl.BlockSpec((None, 1, ADA_COLS), lambda l: (l, 0, 0))],
        out_specs=pl.BlockSpec((None, N_DEV, ADA_COLS), lambda l: (l, 0, 0)),
        out_shape=SDS((DEPTH, N_DEV, ADA_COLS), F32),
        compiler_params=_cp(("parallel",)),
    )(c_all, w_ada, b_cols)


def _ada_bwd(ct_pad, dmod_pad):
    def body(c_ref, d_ref, out_ref):
        out_ref[...] = _mm(_bf(_silu(c_ref[...])), _bf(d_ref[...]))

    return pl.pallas_call(
        body, name="ada_bwd", grid=(DEPTH,),
        in_specs=[_full((D_MODEL, LANE)), pl.BlockSpec((None, LANE, ADA_COLS), lambda l: (l, 0, 0))],
        out_specs=pl.BlockSpec((None, D_MODEL, ADA_COLS), lambda l: (l, 0, 0)),
        out_shape=SDS((DEPTH, D_MODEL, ADA_COLS), F32),
        compiler_params=_cp(("parallel",)),
    )(ct_pad, dmod_pad)


def _adamw(parts, w, m, v, name, own=None):
    n, L, R, C = parts.shape
    tr = R
    while tr * C * 4 > (1 << 20) and tr % 16 == 0:
        tr //= 2

    def body(*refs):
        p_ref, w_ref, m_ref, v_ref, g_ref, d_ref, mo_ref, vo_ref = refs[:1] + refs[-7:]

        def part(k):
            if own is None:
                return p_ref[k].astype(F32)
            me = 4 * lax.axis_index("x") + 2 * lax.axis_index("y") + lax.axis_index("c")
            return jnp.where(me == k, refs[1][...], p_ref[k]).astype(F32)

        g = part(0)
        for k in range(1, n):
            g = g + part(k)
        mn = ADAM_B1 * m_ref[...] + (1.0 - ADAM_B1) * g
        vn = ADAM_B2 * v_ref[...] + (1.0 - ADAM_B2) * (g * g)
        m_hat = mn / (1.0 - ADAM_B1 ** ADAM_STEP)
        v_hat = vn / (1.0 - ADAM_B2 ** ADAM_STEP)
        g_ref[...] = g
        d_ref[...] = -ADAM_LR * (m_hat / (jnp.sqrt(v_hat) + ADAM_EPS) + ADAM_WD * w_ref[...])
        mo_ref[...] = mn
        vo_ref[...] = vn

    blk = pl.BlockSpec((None, tr, C), lambda l, i: (l, i, 0))
    return pl.pallas_call(
        body, name=name, grid=(L, R // tr),
        in_specs=[pl.BlockSpec((n, None, tr, C), lambda l, i: (0, l, i, 0))] + [blk] * (3 if own is None else 4),
        out_specs=[blk] * 4,
        out_shape=[SDS((L, R, C), F32)] * 4,
        compiler_params=_cp(("parallel", "parallel")),
    )(parts, *([] if own is None else [own]), w, m, v)


MESH = pl.DeviceIdType.MESH
ANY = pl.BlockSpec(memory_space=pl.ANY)


def _all_gather(v, name):
    def body(v_ref, out_ref, send_sems, recv_sems, local_sem):
        x, y, c = lax.axis_index("x"), lax.axis_index("y"), lax.axis_index("c")
        me, sibling = (x, y, c), (x, y, 1 - c)
        chips = [(1 - x, y), (x, 1 - y), (1 - x, 1 - y)]

        def slot(px, py, pc):
            return out_ref.at[4 * px + 2 * py + pc]

        def copy(k, block, to, src=None):
            return pltpu.make_async_remote_copy(
                src_ref=slot(*block) if src is None else src, dst_ref=slot(*block),
                send_sem=send_sems.at[k], recv_sem=recv_sems.at[k], device_id=to, device_id_type=MESH)

        mine = pltpu.make_async_copy(v_ref, slot(*me), local_sem)
        mine.start()
        first = [copy(0, me, sibling, src=v_ref)]
        first += [copy(1 + j, me, (*chip, c), src=v_ref) for j, chip in enumerate(chips)]
        for cp in first:
            cp.start()
        passed = [copy(4 + j, (*chip, c), sibling) for j, chip in enumerate(chips)]
        for j, chip in enumerate(chips):
            copy(1 + j, (*chip, c), me).wait_recv()
            passed[j].start()
        copy(0, sibling, me).wait_recv()
        for j, chip in enumerate(chips):
            copy(4 + j, (*chip, 1 - c), me).wait_recv()
        for cp in first + passed:
            cp.wait_send()
        mine.wait()

    return pl.pallas_call(
        body, name=name, in_specs=[ANY], out_specs=ANY,
        out_shape=SDS((N_DEV,) + v.shape, v.dtype),
        scratch_shapes=[pltpu.SemaphoreType.DMA((7,)), pltpu.SemaphoreType.DMA((7,)), pltpu.SemaphoreType.DMA],
    )(v)


def _all_to_all(v, name):
    def body(v_ref, out_ref, send_sems, recv_sems, local_sem):
        x, y, c = lax.axis_index("x"), lax.axis_index("y"), lax.axis_index("c")
        mine_idx = 4 * x + 2 * y + c
        mine = pltpu.make_async_copy(v_ref.at[mine_idx], out_ref.at[mine_idx], local_sem)
        mine.start()
        sends, recvs = [], []
        for k in range(1, N_DEV):
            px = 1 - x if k & 4 else x
            py = 1 - y if k & 2 else y
            pc = 1 - c if k & 1 else c
            peer_idx = 4 * px + 2 * py + pc
            sems = dict(send_sem=send_sems.at[k - 1], recv_sem=recv_sems.at[k - 1], device_id=(px, py, pc),
                        device_id_type=MESH)
            sends.append(pltpu.make_async_remote_copy(src_ref=v_ref.at[peer_idx], dst_ref=out_ref.at[mine_idx], **sems))
            recvs.append(pltpu.make_async_remote_copy(src_ref=v_ref.at[peer_idx], dst_ref=out_ref.at[peer_idx], **sems))
        for cp in sends:
            cp.start()
        for cp in recvs:
            cp.wait_recv()
        for cp in sends:
            cp.wait_send()
        mine.wait()

    return pl.pallas_call(
        body, name=name, in_specs=[ANY], out_specs=ANY,
        out_shape=SDS(v.shape, v.dtype),
        scratch_shapes=[pltpu.SemaphoreType.DMA((7,)), pltpu.SemaphoreType.DMA((7,)), pltpu.SemaphoreType.DMA],
    )(v)


HBM_SPEC = pl.BlockSpec(memory_space=pltpu.HBM)
SEM_SPEC = pl.BlockSpec(memory_space=pltpu.SEMAPHORE)
EFFECT = pltpu.SideEffectType.DATAFLOW_SIDE_EFFECTING


def _exchange_copies(srcs, lands, send_sems, recv_sems, scatter, layer):
    x, y, c = lax.axis_index("x"), lax.axis_index("y"), lax.axis_index("c")
    me = 4 * x + 2 * y + c
    copies = []
    for a, (src, land) in enumerate(zip(srcs, lands)):
        for k in range(1, N_DEV):
            px = 1 - x if k & 4 else x
            py = 1 - y if k & 2 else y
            pc = 1 - c if k & 1 else c
            n = 7 * a + k - 1
            copies.append(pltpu.make_async_remote_copy(
                src_ref=src.at[4 * px + 2 * py + pc] if scatter else src,
                dst_ref=land.at[me, layer] if scatter else land.at[me],
                send_sem=send_sems.at[n], recv_sem=recv_sems.at[n], device_id=(px, py, pc), device_id_type=MESH))
    return copies


def _exchange_start(name, srcs, lands, scatter, layer=0, after=None):
    n = len(srcs)

    def body(*refs):
        send_sems, recv_sems = refs[-2 * n - 3], refs[-2 * n - 2]
        for cp in _exchange_copies(refs[:n], refs[n:2 * n], send_sems, recv_sems, scatter, layer):
            cp.start()
        refs[-1][...] = jnp.zeros_like(refs[-1])

    arrays = list(srcs) + list(lands)
    sems = pltpu.SemaphoreType.DMA((7 * n,))
    out = pl.pallas_call(
        body, name=name,
        out_shape=(sems, sems, *[pltpu.HBM(v.shape, v.dtype) for v in arrays], SDS((8, LANE), F32)),
        in_specs=[HBM_SPEC] * (2 * n) + ([ANY] if after is not None else []),
        out_specs=(SEM_SPEC, SEM_SPEC, *[HBM_SPEC] * (2 * n), pl.BlockSpec(memory_space=pltpu.VMEM)),
        input_output_aliases={i: 2 + i for i in range(2 * n)},
        compiler_params=pltpu.CompilerParams(has_side_effects=EFFECT),
    )(*[pltpu.with_memory_space_constraint(v, pltpu.HBM) for v in arrays], *([after] if after is not None else []))
    return dict(sems=out[:2], srcs=out[2:2 + n], lands=out[2 + n:2 + 2 * n], token=out[-1][0, 0], scatter=scatter,
                layer=layer)


def _exchange_wait(name, st, after):
    n = len(st["srcs"])

    def body(*refs):
        send_sems, recv_sems = refs[2 * n], refs[2 * n + 1]
        for cp in _exchange_copies(refs[:n], refs[n:2 * n], send_sems, recv_sems, st["scatter"], st["layer"]):
            cp.wait_send()
            cp.wait_recv()

    arrays = list(st["srcs"]) + list(st["lands"])
    out = pl.pallas_call(
        body, name=name,
        out_shape=tuple(pltpu.HBM(v.shape, v.dtype) for v in arrays),
        in_specs=[HBM_SPEC] * (2 * n) + [SEM_SPEC, SEM_SPEC, ANY],
        out_specs=tuple([HBM_SPEC] * (2 * n)),
        input_output_aliases={i: i for i in range(2 * n)},
        compiler_params=pltpu.CompilerParams(has_side_effects=EFFECT),
    )(*arrays, *st["sems"], after)
    return out[n:]


_IN_PIECES = ([(1024, 3072)]
              + [r for t in range(4) for r in ((LANE * t, LANE * (t + 1)), (512 + LANE * t, 512 + LANE * (t + 1)))]
              + [(4096, 5632), (3072, 4096), (5632, 5648)])


def _permute_in(w):
    pad = jnp.zeros(w.shape[:-1] + (N_PAD - N_IN,), w.dtype)
    return jnp.concatenate([w[..., a:b] for a, b in _IN_PIECES] + [pad], axis=-1)


def _unpermute_in(g):
    ax = [g[..., OFF_LRU + 2 * LANE * t:OFF_LRU + 2 * LANE * t + LANE] for t in range(4)]
    ag = [g[..., OFF_LRU + 2 * LANE * t + LANE:OFF_LRU + 2 * LANE * (t + 1)] for t in range(4)]
    return jnp.concatenate(ax + ag + [g[..., 0:2048], g[..., OFF_Z:OFF_Z + SSD_W], g[..., OFF_XBC:OFF_XBC + SSD_CONV],
                                      g[..., OFF_Z + SSD_W:OFF_Z + SSD_W + SSD_HEADS]], axis=-1)


SHARD_COLS = N_IN // N_DEV


def _in_segments():
    segs, pos = [], 0
    for a, b in _IN_PIECES:
        for i in range(N_DEV):
            lo, hi = max(a, SHARD_COLS * i), min(b, SHARD_COLS * (i + 1))
            if lo < hi:
                segs.append((i, lo - SHARD_COLS * i, hi - lo, pos + lo - a))
        pos += b - a
    return segs


RELAYOUT_ROWS = 256


def _relayout_in(land, own):
    def body(land_ref, own_ref, out_ref):
        me = 4 * lax.axis_index("x") + 2 * lax.axis_index("y") + lax.axis_index("c")
        out_ref[:, N_IN:N_PAD] = jnp.zeros((RELAYOUT_ROWS, N_PAD - N_IN), BF16)
        for i, j, wd, p in _in_segments():
            out_ref[:, p:p + wd] = jnp.where(me == i, own_ref[:, j:j + wd], land_ref[i, :, j:j + wd])

    return pl.pallas_call(
        body, name="relayout_in", grid=(D_MODEL // RELAYOUT_ROWS,),
        in_specs=[pl.BlockSpec((N_DEV, RELAYOUT_ROWS, SHARD_COLS), lambda r: (0, r, 0)),
                  pl.BlockSpec((RELAYOUT_ROWS, SHARD_COLS), lambda r: (r, 0))],
        out_specs=pl.BlockSpec((RELAYOUT_ROWS, N_PAD), lambda r: (r, 0)),
        out_shape=SDS((D_MODEL, N_PAD), BF16),
        compiler_params=_cp(("parallel",)),
    )(land, own)


def _relayout_grad(g):
    def body(g_ref, out_ref):
        for i, j, wd, p in _in_segments():
            out_ref[i, :, j:j + wd] = g_ref[:, p:p + wd].astype(BF16)

    return pl.pallas_call(
        body, name="relayout_grad", grid=(D_MODEL // RELAYOUT_ROWS,),
        in_specs=[pl.BlockSpec((RELAYOUT_ROWS, N_PAD), lambda r: (r, 0))],
        out_specs=pl.BlockSpec((N_DEV, RELAYOUT_ROWS, SHARD_COLS), lambda r: (0, r, 0)),
        out_shape=SDS((N_DEV, D_MODEL, SHARD_COLS), BF16),
        compiler_params=_cp(("parallel",)),
    )(g)


def _block_diag(w):
    w4 = w.reshape(4, 2, 64, 64)
    z = jnp.zeros((4, 64, 64), w.dtype)
    top = jnp.concatenate([w4[:, 0], z], axis=-1)
    bot = jnp.concatenate([z, w4[:, 1]], axis=-1)
    return jnp.concatenate([top, bot], axis=1).astype(BF16)


def _diag_blocks(g):
    return jnp.stack([g[:, :64, :64], g[:, 64:, 64:]], axis=1).reshape(8, 64, 64)


def _pad_lanes(v):
    return jnp.pad(v, (0, LANE - v.shape[0]))[None, :]


def _lower_bounds(logits):
    p = jax.nn.softmax(logits, axis=0)
    return p, jnp.cumsum(p, axis=0) - p[0]


def _lower_bounds_bwd(p, dlb):
    dp = jnp.cumsum(dlb[::-1], axis=0)[::-1]
    dp = dp.at[0].add(-jnp.sum(dlb, axis=0))
    return p * (dp - jnp.sum(dp * p, axis=0, keepdims=True))


SMALL = ["norm_w", "b_ada", "lru_conv_b", "lru_wa", "lru_ba", "lru_wx", "lru_bx", "lru_lambda", "hg_lb_logits",
         "hg_norm_w", "ssd_conv_b", "ssd_dt_bias", "ssd_a_log", "ssd_d", "ssd_norm_w", "final_norm_w"]
WEIGHTS = ["norm_w", "w_ada", "b_ada", "w_in", "lru_conv_w", "lru_conv_b", "lru_wa", "lru_ba", "lru_wx", "lru_bx",
           "lru_lambda", "hg_lb_logits", "hg_norm_w", "ssd_conv_w", "ssd_conv_b", "ssd_dt_bias", "ssd_a_log", "ssd_d",
           "ssd_norm_w", "w_out", "final_norm_w"]
INPUTS = ["x", "c"] + WEIGHTS + ["loss_target"] + ["m_" + n for n in WEIGHTS] + ["v_" + n for n in WEIGHTS]
SMALL_ROW = 1024


def _small_rows(like):
    out, off = {}, 0
    for n in SMALL:
        rows = -(-int(np.prod(like[n].shape)) // (8 * SMALL_ROW)) * 8
        out[n] = (off, rows)
        off += rows
    return out, off


def _flatten_small(d, prefix=""):
    table, _ = _small_rows({n: d[prefix + n] for n in SMALL})
    pieces = []
    for n in SMALL:
        flat = d[prefix + n].reshape(-1)
        pieces.append(jnp.pad(flat, (0, table[n][1] * SMALL_ROW - flat.shape[0])).reshape(-1, SMALL_ROW))
    return jnp.concatenate(pieces, axis=0)


def _split_small(packed, like):
    table, _ = _small_rows(like)
    out = {}
    for n in SMALL:
        off, rows = table[n]
        size = int(np.prod(like[n].shape))
        out[n] = packed[off:off + rows].reshape(-1)[:size].reshape(like[n].shape)
    return out


def _local_step(x, mod, target, w, fetch, emit):
    S = x.shape[0]
    mall = _bfc(_hg_consts())
    mall_t = _bfc(_hg_consts().T)
    consts = _ssd_consts()
    p_lb, lbs = _lower_bounds(w["hg_lb_logits"])
    saved = []
    for l in range(DEPTH):
        w_in_l, w_out_l, token = fetch(l, x)
        shift, scale, gate = (mod[l:l + 1, k * D_MODEL:(k + 1) * D_MODEL] for k in range(3))
        shift = shift + token
        prm = dict(
            nw=w["norm_w"][l:l + 1], cw=w["lru_conv_w"][l], cb=w["lru_conv_b"][l:l + 1],
            wa=_block_diag(w["lru_wa"][l]), ba=w["lru_ba"][l].reshape(1, LRU_W),
            wx=_block_diag(w["lru_wx"][l]), bx=w["lru_bx"][l].reshape(1, LRU_W), lam=w["lru_lambda"][l:l + 1],
            lb=lbs[l:l + 1], hnw=w["hg_norm_w"][l:l + 1], scw=w["ssd_conv_w"][l], scb=w["ssd_conv_b"][l:l + 1],
            bias=_pad_lanes(w["ssd_dt_bias"][l]), alog=_pad_lanes(w["ssd_a_log"][l]),
            dskip=jnp.repeat(w["ssd_d"][l], SSD_P)[None, :], snw=w["ssd_norm_w"][l:l + 1],
            w_in=w_in_l, w_out=w_out_l, scale=scale, gate=gate)
        u, h = _inproj_fwd(x, prm["nw"], scale, shift, prm["w_in"])
        ycat = lax.empty((S, D_INNER), BF16)
        lru_args = (u, prm["cw"], prm["cb"], prm["wa"], prm["ba"], prm["wx"], prm["bx"], prm["lam"])
        ycat, h_lru = _lru_fwd(*lru_args, ycat)
        ycat, o_b, hg_st = _hg_fwd(u, prm["lb"], prm["hnw"], mall, ycat)
        xbc = _ssdconv_fwd(u, prm["scw"], prm["scb"])
        ssd_args = (u, xbc, prm["bias"], prm["alog"], prm["dskip"], prm["snw"], consts)
        ycat, y_ssd, ssd_st = _ssd_fwd(*ssd_args, ycat)
        x_new, y = _outproj_fwd(ycat, prm["w_out"], x, gate)
        saved.append((prm, x, u, h, ycat, lru_args, h_lru, o_b, hg_st, ssd_args, y_ssd, ssd_st, y))
        x = x_new
    dx, red = _loss_head(x, w["final_norm_w"][None, :], target)
    loss = red[1, 0]
    g = {n: [None] * DEPTH for n in WEIGHTS}
    g["final_norm_w"] = red[0]
    dmod, dlb = [None] * DEPTH, [None] * DEPTH
    for l in reversed(range(DEPTH)):
        prm, x, u, h, ycat, lru_args, h_lru, o_b, hg_st, ssd_args, y_ssd, ssd_st, y = saved[l]
        dycat, g_out, dgate = _outproj_bwd(dx, y, prm["gate"], ycat, prm["w_out"])
        token = emit(l, "w_out", g_out)
        du = lax.empty((S, N_PAD), BF16)
        ssd_args = ssd_args[:5] + (ssd_args[5] + token,) + ssd_args[6:]
        du, dxbc, sred = _ssd_bwd(*ssd_args, y_ssd, ssd_st, dycat, du)
        du, cred = _ssdconv_bwd(u, prm["scw"], prm["scb"], dxbc, du)
        du, hred = _hg_bwd(u, prm["lb"], prm["hnw"], mall, mall_t, o_b, hg_st, dycat, du)
        du, lred, gwa, gwx = _lru_bwd(*lru_args, h_lru, dycat, du)
        token = emit(l, "w_in", _inproj_bwd_w(h, du))
        dx, ired = _inproj_bwd_x(du, prm["w_in"], x, prm["nw"], prm["scale"] + token, dx)
        g["norm_w"][l] = ired[2]
        dmod[l] = jnp.concatenate([ired[0], ired[1], dgate[0]])
        g["lru_conv_w"][l], g["lru_conv_b"][l] = lred[0:4], lred[4]
        g["lru_ba"][l], g["lru_bx"][l], g["lru_lambda"][l] = lred[5].reshape(8, 64), lred[6].reshape(8, 64), lred[7]
        g["lru_wa"][l], g["lru_wx"][l] = _diag_blocks(gwa), _diag_blocks(gwx)
        g["hg_norm_w"][l], dlb[l] = hred[0], hred[1]
        g["ssd_conv_w"][l], g["ssd_conv_b"][l] = cred[0:4], cred[4]
        g["ssd_norm_w"][l] = sred[0]
        g["ssd_d"][l] = sred[1].reshape(SSD_HEADS, SSD_P).sum(-1)
        g["ssd_dt_bias"][l] = sred[2, :SSD_HEADS]
        g["ssd_a_log"][l] = -sred[3, :SSD_HEADS] * jnp.exp(w["ssd_a_log"][l])
    g["hg_lb_logits"] = _lower_bounds_bwd(p_lb, jnp.stack(dlb))
    for n in WEIGHTS:
        if isinstance(g[n], list) and g[n][0] is not None:
            g[n] = jnp.stack(g[n])
    return loss, dx, jnp.stack(dmod), g


def kernel(x, c, norm_w, w_ada, b_ada, w_in, lru_conv_w, lru_conv_b, lru_wa, lru_ba, lru_wx, lru_bx, lru_lambda, hg_lb_logits, hg_norm_w, ssd_conv_w, ssd_conv_b, ssd_dt_bias, ssd_a_log, ssd_d, ssd_norm_w, w_out, final_norm_w, loss_target, m_norm_w, m_w_ada, m_b_ada, m_w_in, m_lru_conv_w, m_lru_conv_b, m_lru_wa, m_lru_ba, m_lru_wx, m_lru_bx, m_lru_lambda, m_hg_lb_logits, m_hg_norm_w, m_ssd_conv_w, m_ssd_conv_b, m_ssd_dt_bias, m_ssd_a_log, m_ssd_d, m_ssd_norm_w, m_w_out, m_final_norm_w, v_norm_w, v_w_ada, v_b_ada, v_w_in, v_lru_conv_w, v_lru_conv_b, v_lru_wa, v_lru_ba, v_lru_wx, v_lru_bx, v_lru_lambda, v_hg_lb_logits, v_hg_norm_w, v_ssd_conv_w, v_ssd_conv_b, v_ssd_dt_bias, v_ssd_a_log, v_ssd_d, v_ssd_norm_w, v_w_out, v_final_norm_w):
    return _step(x, c, norm_w, w_ada, b_ada, w_in, lru_conv_w, lru_conv_b, lru_wa, lru_ba, lru_wx, lru_bx, lru_lambda, hg_lb_logits, hg_norm_w, ssd_conv_w, ssd_conv_b, ssd_dt_bias, ssd_a_log, ssd_d, ssd_norm_w, w_out, final_norm_w, loss_target, m_norm_w, m_w_ada, m_b_ada, m_w_in, m_lru_conv_w, m_lru_conv_b, m_lru_wa, m_lru_ba, m_lru_wx, m_lru_bx, m_lru_lambda, m_hg_lb_logits, m_hg_norm_w, m_ssd_conv_w, m_ssd_conv_b, m_ssd_dt_bias, m_ssd_a_log, m_ssd_d, m_ssd_norm_w, m_w_out, m_final_norm_w, v_norm_w, v_w_ada, v_b_ada, v_w_in, v_lru_conv_w, v_lru_conv_b, v_lru_wa, v_lru_ba, v_lru_wx, v_lru_bx, v_lru_lambda, v_hg_lb_logits, v_hg_norm_w, v_ssd_conv_w, v_ssd_conv_b, v_ssd_dt_bias, v_ssd_a_log, v_ssd_d, v_ssd_norm_w, v_w_out, v_final_norm_w)


def _step(*args):
    a = dict(zip(INPUTS, args, strict=True))
    me = 4 * lax.axis_index("x") + 2 * lax.axis_index("y") + lax.axis_index("c")
    x, target = a["x"][0], a["loss_target"][0]

    c_all = _all_gather(a["c"], "gather_c")[:, 0, :]
    b_cols = lax.dynamic_slice_in_dim(a["b_ada"], me * ADA_COLS, ADA_COLS, axis=1)[:, None, :]
    mod_parts = _all_gather(_ada_fwd(c_all, a["w_ada"], b_cols), "gather_mod")
    mod = lax.dynamic_index_in_dim(mod_parts, me, axis=2, keepdims=False)
    mod = mod.transpose(1, 0, 2).reshape(DEPTH, 3 * D_MODEL)

    w = {n: a[n] for n in SMALL}

    w_in_b, w_out_b = a["w_in"].astype(BF16), a["w_out"].astype(BF16)
    conv_own = jnp.concatenate([a["lru_conv_w"], a["ssd_conv_w"]], axis=-1)
    cols, rows_out = N_IN // N_DEV, D_INNER // N_DEV

    def gather_start(l, after):
        srcs = [w_in_b[l], w_out_b[l]] + ([conv_own] if l == 0 else [])
        lands = [lax.empty((N_DEV,) + s.shape, s.dtype) for s in srcs]
        return _exchange_start(f"gather_start_{l}", srcs, lands, False, after=after)

    gathers = {0: gather_start(0, mod)}

    def fetch(l, x_l):
        landed = _exchange_wait(f"gather_wait_{l}", gathers[l], x_l)
        land_out = lax.dynamic_update_index_in_dim(landed[1], w_out_b[l], me, 0)
        if l == 0:
            conv = lax.dynamic_update_index_in_dim(landed[2], conv_own, me, 0).transpose(1, 2, 0, 3)
            w["lru_conv_w"] = conv[..., :64].reshape(DEPTH, 4, LRU_W)
            w["ssd_conv_w"] = conv[..., 64:].reshape(DEPTH, 4, SSD_CONV)
        token = 0.0
        if l + 1 < DEPTH:
            gathers[l + 1] = gather_start(l + 1, land_out)
            token = gathers[l + 1]["token"]
        return _relayout_in(landed[0], w_in_b[l]), land_out.reshape(D_INNER, D_MODEL), token

    scatters = {"w_in": {}, "w_out": {}}
    lands = {"w_in": lax.empty((N_DEV, DEPTH, D_MODEL, cols), BF16),
             "w_out": lax.empty((N_DEV, DEPTH, rows_out, D_MODEL), BF16)}
    own = {"w_in": [None] * DEPTH, "w_out": [None] * DEPTH}

    def emit(l, name, grad):
        grad = _relayout_grad(grad) if name == "w_in" else grad.reshape(N_DEV, rows_out, D_MODEL)
        own[name][l] = lax.dynamic_index_in_dim(grad, me, 0, keepdims=False)
        st = _exchange_start(f"scatter_start_{name}_{l}", [grad], [lands[name]], True, layer=l)
        scatters[name][l] = st
        lands[name] = st["lands"][0]
        return st["token"]

    loss, dx, dmod, g = _local_step(x, mod, target, w, fetch, emit)
    loss = lax.psum(loss, ("x", "y", "c"))

    g["b_ada"] = dmod
    small = _all_gather(_flatten_small(g), "gather_small")
    outs = _adamw(small[:, None], *[_flatten_small(a, p)[None] for p in ("", "m_", "v_")], "adamw_small")
    res = [_split_small(o[0], a) for o in outs]

    off = _small_rows(a)[0]["b_ada"][0]
    dmod_all = small[:, off:off + DEPTH * 3 * D_MODEL // SMALL_ROW]
    dmod_all = dmod_all.reshape(N_DEV, DEPTH, 3 * D_MODEL).transpose(1, 0, 2)
    dmod_cols = lax.dynamic_slice_in_dim(dmod_all, me * ADA_COLS, ADA_COLS, axis=2)
    dmod_pad = jnp.pad(dmod_cols, ((0, 0), (0, LANE - N_DEV), (0, 0)))
    ct_pad = jnp.pad(c_all.T, ((0, 0), (0, LANE - N_DEV)))
    g_ada = _ada_bwd(ct_pad, dmod_pad)

    def sharded(name, parts, own=None):
        return _adamw(parts, a[name], a["m_" + name], a["v_" + name], "adamw_" + name, own=own)

    big = {"w_ada": sharded("w_ada", g_ada[None])}
    g_conv = jnp.concatenate([g["lru_conv_w"].reshape(DEPTH, 4, N_DEV, 64), g["ssd_conv_w"].reshape(DEPTH, 4, N_DEV, 192)],
                             axis=-1).transpose(2, 0, 1, 3)
    conv_parts = _all_to_all(g_conv, "scatter_conv")
    big["lru_conv_w"] = sharded("lru_conv_w", conv_parts[..., :64])
    big["ssd_conv_w"] = sharded("ssd_conv_w", conv_parts[..., 64:])

    after = big["w_ada"][1]
    for name in ("w_out", "w_in"):
        for l in reversed(range(DEPTH)):
            scatters[name][l]["lands"] = [lands[name]]
            lands[name] = _exchange_wait(f"scatter_wait_{name}_{l}", scatters[name][l], after)[0]
        big[name] = sharded(name, lands[name], jnp.stack(own[name]))

    out = [loss, dx[None]]
    for k in range(4):
        out += [big[n][k] if n in big else res[k][n] for n in WEIGHTS]
    return tuple(out)
```

```python
import functools

import numpy as np
import jax
import jax.numpy as jnp
from jax import lax
from jax.experimental import pallas as pl
from jax.experimental.pallas import tpu as pltpu

F32 = jnp.float32
BF16 = jnp.bfloat16
SDS = jax.ShapeDtypeStruct

N_DEV = 8
DEPTH = 4
D_MODEL = 1024
D_INNER = 2048
EPS = 1e-6
LRU_W = 512
LRU_C = 8.0
HG_W = 512
HG_CHUNK = 64
HG_HEADS = 4
SSD_W = 1024
SSD_HEADS = 16
SSD_P = 64
SSD_N = 128
SSD_CHUNK = 128
SSD_CONV = 1536
N_IN = 5648
N_PAD = 5760
OFF_HG = 0
OFF_LRU = 2048
OFF_XBC = 3072
OFF_Z = 4608
LANE = 128
VMEM_LIMIT = 56 * 1024 * 1024
NEG = -1e30

ADAM_LR = 0.001
ADAM_B1 = 0.9
ADAM_B2 = 0.999
ADAM_EPS = 1e-08
ADAM_WD = 0.01
ADAM_STEP = 10


def _cp(sem=None):
    return pltpu.CompilerParams(dimension_semantics=sem, vmem_limit_bytes=VMEM_LIMIT)


def _dg(a, b, ca, cb):
    return lax.dot_general(a, b, (((ca,), (cb,)), ((), ())), preferred_element_type=F32)


def _mm(a, b):
    return _dg(a, b, 1, 0)


def _mm_nt(a, b):
    return _dg(a, b, 1, 1)


def _mm_tn(a, b):
    return _dg(a, b, 0, 0)


def _bf(x):
    return x.astype(BF16)


def _split3(x):
    hi = x.astype(BF16)
    r = x - hi.astype(F32)
    mid = r.astype(BF16)
    lo = (r - mid.astype(F32)).astype(BF16)
    return hi, mid, lo


def _sel_r(x, m):
    hi, mid, lo = _split3(x)
    return _mm(hi, m) + _mm(mid, m) + _mm(lo, m)


def _sel_l(m, x):
    hi, mid, lo = _split3(x)
    return _mm(m, hi) + _mm(m, mid) + _mm(m, lo)


def _sel_tn(x, m):
    hi, mid, lo = _split3(x)
    return _mm_tn(hi, m) + _mm_tn(mid, m) + _mm_tn(lo, m)


def _sigmoid(x):
    return 1.0 / (1.0 + jnp.exp(-x))


def _silu(x):
    return x * _sigmoid(x)


def _dsilu(x):
    s = _sigmoid(x)
    return s * (1.0 + x * (1.0 - s))


def _softplus(x):
    return jnp.maximum(x, 0.0) + jnp.log(1.0 + jnp.exp(-jnp.abs(x)))


def _expm1(z):
    series = z * (1.0 + z * (1.0 / 2) * (1.0 + z * (1.0 / 3) * (1.0 + z * (1.0 / 4) * (
        1.0 + z * (1.0 / 5) * (1.0 + z * (1.0 / 6) * (1.0 + z * (1.0 / 7)))))))
    return jnp.where(jnp.abs(z) < 0.3, series, jnp.exp(z) - 1.0)


def _iota(shape, dim):
    return lax.broadcasted_iota(jnp.int32, shape, dim)


def _last_row(x, rows):
    return jnp.sum(jnp.where(rows == x.shape[0] - 1, x, 0.0), axis=0, keepdims=True)


def _shift_down(x, d, rows, fill=0.0):
    return jnp.where(rows >= d, pltpu.roll(x, d, 0), fill)


def _shift_up(x, d, rows, fill=0.0):
    n = x.shape[0]
    return jnp.where(rows < n - d, pltpu.roll(x, n - d, 0), fill)


def _conv_fwd(x, cw_ref, cb_ref, rows):
    out = cb_ref[...] + cw_ref[pl.ds(3, 1), :] * x
    for k in range(3):
        out = out + cw_ref[pl.ds(k, 1), :] * _shift_down(x, 3 - k, rows)
    return out


def _conv_bwd(x, dco, cw_ref, rows):
    dx = cw_ref[pl.ds(3, 1), :] * dco
    dws = []
    for k in range(3):
        dx = dx + cw_ref[pl.ds(k, 1), :] * _shift_up(dco, 3 - k, rows)
        dws.append(jnp.sum(dco * _shift_down(x, 3 - k, rows), axis=0, keepdims=True))
    dws.append(jnp.sum(dco * x, axis=0, keepdims=True))
    return dx, dws, jnp.sum(dco, axis=0, keepdims=True)


def _vec(n):
    return pl.BlockSpec((1, n), lambda *_: (0, 0))


def _full(shape):
    nd = len(shape)
    return pl.BlockSpec(shape, lambda *_: (0,) * nd)


def _inproj_fwd(x, nw, scale, shift, w):
    S = x.shape[0]
    tm, tn = min(512, S), 640

    def body(x_ref, nw_ref, sc_ref, sh_ref, w_ref, u_ref, h_ref):
        @pl.when(pl.program_id(1) == 0)
        def _():
            xv = x_ref[...]
            inv = lax.rsqrt(jnp.mean(xv * xv, axis=-1, keepdims=True) + EPS)
            h = (xv * inv) * nw_ref[...] * (1.0 + sc_ref[...]) + sh_ref[...]
            h_ref[...] = h.astype(BF16)

        u_ref[...] = _mm(h_ref[...], w_ref[...])

    return pl.pallas_call(
        body, name="inproj_fwd", grid=(S // tm, N_PAD // tn),
        in_specs=[pl.BlockSpec((tm, D_MODEL), lambda i, j: (i, 0)), _vec(D_MODEL), _vec(D_MODEL), _vec(D_MODEL),
                  pl.BlockSpec((D_MODEL, tn), lambda i, j: (0, j))],
        out_specs=[pl.BlockSpec((tm, tn), lambda i, j: (i, j)), pl.BlockSpec((tm, D_MODEL), lambda i, j: (i, 0))],
        out_shape=[SDS((S, N_PAD), F32), SDS((S, D_MODEL), BF16)],
        compiler_params=_cp(("parallel", "arbitrary")),
    )(x, nw, scale, shift, w)


def _inproj_bwd_x(du, w, x, nw, scale, dxn):
    S = x.shape[0]
    tm, tk = min(512, S), 640
    nk = N_PAD // tk

    def body(du_ref, w_ref, x_ref, nw_ref, sc_ref, dxn_ref, dx_ref, red_ref, acc):
        i, k = pl.program_id(0), pl.program_id(1)

        @pl.when(k == 0)
        def _():
            acc[...] = jnp.zeros_like(acc)

        @pl.when((i == 0) & (k == 0))
        def _():
            red_ref[...] = jnp.zeros_like(red_ref)

        acc[...] += _mm_nt(_bf(du_ref[...]), w_ref[...])

        @pl.when(k == nk - 1)
        def _():
            dh = acc[...]
            xv = x_ref[...]
            inv = lax.rsqrt(jnp.mean(xv * xv, axis=-1, keepdims=True) + EPS)
            xhat = xv * inv
            nwv = nw_ref[...]
            g1 = 1.0 + sc_ref[...]
            dxhat = dh * nwv * g1
            dx = inv * (dxhat - xhat * jnp.mean(dxhat * xhat, axis=-1, keepdims=True))
            dx_ref[...] = dxn_ref[...] + dx
            red_ref[0:1, :] += jnp.sum(dh, axis=0, keepdims=True)
            red_ref[1:2, :] += jnp.sum(dh * xhat * nwv, axis=0, keepdims=True)
            red_ref[2:3, :] += jnp.sum(dh * xhat * g1, axis=0, keepdims=True)

    row = pl.BlockSpec((tm, D_MODEL), lambda i, k: (i, 0))
    return pl.pallas_call(
        body, name="inproj_bwd_x", grid=(S // tm, nk),
        in_specs=[pl.BlockSpec((tm, tk), lambda i, k: (i, k)), pl.BlockSpec((D_MODEL, tk), lambda i, k: (0, k)),
                  row, _vec(D_MODEL), _vec(D_MODEL), row],
        out_specs=[row, pl.BlockSpec((8, D_MODEL), lambda i, k: (0, 0))],
        out_shape=[SDS((S, D_MODEL), F32), SDS((8, D_MODEL), F32)],
        scratch_shapes=[pltpu.VMEM((tm, D_MODEL), F32)],
        compiler_params=_cp(("arbitrary", "arbitrary")),
    )(du, w, x, nw, scale, dxn)


def _inproj_bwd_w(h, du):
    S = h.shape[0]
    tn = 640

    def body(h_ref, du_ref, gw_ref):
        gw_ref[...] = _mm_tn(h_ref[...], _bf(du_ref[...]))

    return pl.pallas_call(
        body, name="inproj_bwd_w", grid=(N_PAD // tn,),
        in_specs=[_full((S, D_MODEL)), pl.BlockSpec((S, tn), lambda j: (0, j))],
        out_specs=pl.BlockSpec((D_MODEL, tn), lambda j: (0, j)),
        out_shape=SDS((D_MODEL, N_PAD), F32),
        compiler_params=_cp(("parallel",)),
    )(h, du)


def _scan_block(a, b, rows):
    d = 1
    while d < a.shape[0]:
        a_s = _shift_down(a, d, rows, 1.0)
        b_s = _shift_down(b, d, rows, 0.0)
        b = a * b_s + b
        a = a * a_s
        d *= 2
    return a, b


def _rscan_block(c, g, rows):
    d = 1
    while d < c.shape[0]:
        c_s = _shift_up(c, d, rows, 1.0)
        g_s = _shift_up(g, d, rows, 0.0)
        g = g + c * g_s
        c = c * c_s
        d *= 2
    return c, g


def _lru_gates(xa, wa_ref, ba_ref, wx_ref, bx_ref, lam_ref):
    sp = _softplus(-lam_ref[...])
    xb = _bf(xa)
    r = _sigmoid(_mm(xb, wa_ref[...]) + ba_ref[...])
    ig = _sigmoid(_mm(xb, wx_ref[...]) + bx_ref[...])
    la = -LRU_C * r * sp
    a = jnp.exp(la)
    mult = jnp.sqrt(-_expm1(2.0 * la))
    return sp, r, ig, la, a, mult


def _lru_specs(S):
    t128 = pl.BlockSpec((1, LANE), lambda t: (0, t))
    return [pl.BlockSpec((S, 2 * LANE), lambda t: (0, OFF_LRU // (2 * LANE) + t)),
            pl.BlockSpec((4, LANE), lambda t: (0, t)), t128,
            pl.BlockSpec((None, LANE, LANE), lambda t: (t, 0, 0)), t128,
            pl.BlockSpec((None, LANE, LANE), lambda t: (t, 0, 0)), t128, t128]


def _lru_fwd(u, cw, cb, wa, ba, wx, bx, lam, ycat):
    S = u.shape[0]
    tb = min(256, S)

    def body(u_ref, cw_ref, cb_ref, wa_ref, ba_ref, wx_ref, bx_ref, lam_ref, ycat_in, ycat_ref, h_ref, a_scr, b_scr):
        del ycat_in
        rows = _iota((S, LANE), 0)
        xa = _conv_fwd(u_ref[:, 0:LANE], cw_ref, cb_ref, rows)
        _, _, ig, _, a, mult = _lru_gates(xa, wa_ref, ba_ref, wx_ref, bx_ref, lam_ref)
        a_scr[...] = a
        b_scr[...] = mult * (ig * xa)
        rows_b = _iota((tb, LANE), 0)

        def blk(j, hprev):
            sl = pl.ds(pl.multiple_of(j * tb, tb), tb)
            acum, hloc = _scan_block(a_scr[sl, :], b_scr[sl, :], rows_b)
            hf = hloc + acum * hprev
            h_ref[sl, :] = hf
            return _last_row(hf, rows_b)

        lax.fori_loop(0, S // tb, blk, jnp.zeros((1, LANE), F32))
        ycat_ref[...] = _bf(h_ref[...] * _silu(u_ref[:, LANE:2 * LANE]))

    col = pl.BlockSpec((S, LANE), lambda t: (0, t))
    return pl.pallas_call(
        body, name="lru_fwd", grid=(LRU_W // LANE,),
        in_specs=_lru_specs(S) + [pl.BlockSpec(memory_space=pl.ANY)],
        out_specs=[col, col],
        out_shape=[SDS((S, D_INNER), BF16), SDS((S,LRU_W), F32)],
        scratch_shapes=[pltpu.VMEM((S, LANE), F32), pltpu.VMEM((S, LANE), F32)],
        input_output_aliases={8: 0},
        compiler_params=_cp(("parallel",)),
    )(u, cw, cb, wa, ba, wx, bx, lam, ycat)


def _lru_bwd(u, cw, cb, wa, ba, wx, bx, lam, h_lru, dycat, du):
    S = u.shape[0]
    tb = min(256, S)

    def body(u_ref, cw_ref, cb_ref, wa_ref, ba_ref, wx_ref, bx_ref, lam_ref, h_ref, dy_ref, du_in,
             du_ref, red_ref, gwa_ref, gwx_ref, c_scr, g_scr, l_scr):
        del du_in
        rows = _iota((S, LANE), 0)
        ax = u_ref[:, 0:LANE]
        ag = u_ref[:, LANE:2 * LANE]
        xa = _conv_fwd(ax, cw_ref, cb_ref, rows)
        sp, r, ig, la, a, mult = _lru_gates(xa, wa_ref, ba_ref, wx_ref, bx_ref, lam_ref)
        h = h_ref[...]
        dy = dy_ref[...]
        du_ref[:, LANE:2 * LANE] = _bf(dy * h * _dsilu(ag))
        c_scr[...] = _shift_up(a, 1, rows, 0.0)
        g_scr[...] = dy * _silu(ag)
        rows_b = _iota((tb, LANE), 0)
        nb = S // tb

        def blk(jj, lnext):
            j = nb - 1 - jj
            sl = pl.ds(pl.multiple_of(j * tb, tb), tb)
            ccum, lloc = _rscan_block(c_scr[sl, :], g_scr[sl, :], rows_b)
            lam_t = lloc + ccum * lnext
            l_scr[sl, :] = lam_t
            return jnp.sum(jnp.where(rows_b == 0, lam_t, 0.0), axis=0, keepdims=True)

        lax.fori_loop(0, nb, blk, jnp.zeros((1, LANE), F32))
        db = l_scr[...]
        da = db * _shift_down(h, 1, rows)
        dmult = db * ig * xa
        dig = db * mult * xa
        dxa = db * mult * ig
        dla = da * a - dmult * (a * a) / mult
        dr = -LRU_C * sp * dla
        dsp = jnp.sum(-LRU_C * r * dla, axis=0, keepdims=True)
        dlam = -dsp * _sigmoid(-lam_ref[...])
        dzr = dr * r * (1.0 - r)
        dzi = dig * ig * (1.0 - ig)
        dzr_b, dzi_b, xa_b = _bf(dzr), _bf(dzi), _bf(xa)
        dxa = dxa + _mm_nt(dzr_b, wa_ref[...]) + _mm_nt(dzi_b, wx_ref[...])
        gwa_ref[...] = _mm_tn(xa_b, dzr_b)
        gwx_ref[...] = _mm_tn(xa_b, dzi_b)
        dax, dws, dcb = _conv_bwd(ax, dxa, cw_ref, rows)
        du_ref[:, 0:LANE] = _bf(dax)
        parts = dws + [dcb, jnp.sum(dzr, axis=0, keepdims=True), jnp.sum(dzi, axis=0, keepdims=True), dlam]
        for n, p in enumerate(parts):
            red_ref[pl.ds(n, 1), :] = p

    col = pl.BlockSpec((S, LANE), lambda t: (0, t))
    gw = pl.BlockSpec((None, LANE, LANE), lambda t: (t, 0, 0))
    return pl.pallas_call(
        body, name="lru_bwd", grid=(LRU_W // LANE,),
        in_specs=_lru_specs(S) + [col, col, pl.BlockSpec(memory_space=pl.ANY)],
        out_specs=[pl.BlockSpec((S, 2 * LANE), lambda t: (0, OFF_LRU // (2 * LANE) + t)),
                   pl.BlockSpec((8, LANE), lambda t: (0, t)), gw, gw],
        out_shape=[SDS((S, N_PAD), BF16), SDS((8, LRU_W), F32), SDS((4, LANE, LANE), F32), SDS((4, LANE, LANE), F32)],
        scratch_shapes=[pltpu.VMEM((S, LANE), F32)] * 3,
        input_output_aliases={10: 0},
        compiler_params=_cp(("parallel",)),
    )(u, cw, cb, wa, ba, wx, bx, lam, h_lru, dycat, du)


HG_LEVELS = 6


def _hg_consts():
    C = HG_CHUNK
    t = np.arange(C)[:, None]
    r = np.arange(C)[None, :]
    mats = []
    for side in ("q", "k"):
        for l in range(HG_LEVELS):
            b = 1 << l
            upper = (t % (2 * b)) >= b
            anchor = (t // (2 * b)) * 2 * b + b - 1
            if side == "q":
                mats.append(upper & (r > anchor) & (r <= t))
            else:
                mats.append((~upper) & (r > t) & (r <= anchor))
    mats.append(r <= t)
    mats.append(r > t)
    return np.concatenate(mats, 0).astype(np.float32)


def _hg_factors(hf, lb, mall):
    s = _sigmoid(hf)
    f = lb + (1.0 - lb) * s
    lf = jnp.log(f)
    k = (1.0 - lb) * _sigmoid(-hf)
    e = jnp.exp(_sel_l(mall, lf))
    C = HG_CHUNK
    eq = [e[l * C:(l + 1) * C] for l in range(HG_LEVELS)]
    ek = [e[(HG_LEVELS + l) * C:(HG_LEVELS + l + 1) * C] for l in range(HG_LEVELS)]
    ecum = e[2 * HG_LEVELS * C:(2 * HG_LEVELS + 1) * C]
    erem = e[(2 * HG_LEVELS + 1) * C:(2 * HG_LEVELS + 2) * C]
    return s, f, k, eq, ek, ecum, erem


def _hg_masks():
    C = HG_CHUNK
    ri, ci = _iota((C, C), 0), _iota((C, C), 1)
    rr = _iota((C, LANE), 0)
    gm = [(lax.shift_right_logical(ri, l + 1) == lax.shift_right_logical(ci, l + 1)).astype(F32)
          for l in range(HG_LEVELS)]
    up = [(lax.shift_right_logical(rr, l) & 1) == 1 for l in range(HG_LEVELS)]
    eye = (ri == ci).astype(F32)
    return gm, up, eye, rr


def _hg_scores(qh, kh, eq, ek, sl, gm, up, eye):
    qs, ks = [], []
    p = _mm_nt(_bf(qh), _bf(kh)) * eye
    for l in range(HG_LEVELS):
        ql = jnp.where(up[l], qh * eq[l][:, sl], 0.0)
        kl = jnp.where(up[l], 0.0, kh * ek[l][:, sl])
        p = p + _mm_nt(_bf(ql), _bf(kl)) * gm[l]
        qs.append(ql)
        ks.append(kl)
    return p, qs, ks


def _hg_fwd(u, lb, nw, mall, ycat):
    S = u.shape[0]
    C = HG_CHUNK
    n = S // C

    def body(u_ref, lb_ref, nw_ref, mall_ref, ycat_in, ycat_ref, o_ref, st_ref, st):
        del ycat_in

        @pl.when(pl.program_id(0) == 0)
        def _():
            st[...] = jnp.zeros_like(st)

        q = _silu(u_ref[:, 0:512])
        v = u_ref[:, 1024:1536]
        _, _, k, eq, ek, ecum, erem = _hg_factors(u_ref[:, 512:1024], lb_ref[...], mall_ref[...])
        gm, up, eye, rr = _hg_masks()
        for h in range(HG_HEADS):
            sl = slice(h * LANE, (h + 1) * LANE)
            qh, kh, vh = q[:, sl], k[:, sl], _bf(v[:, sl])
            p, _, _ = _hg_scores(qh, kh, eq, ek, sl, gm, up, eye)
            sth = st[h]
            st_ref[h] = sth
            o_ref[:, sl] = _mm(_bf(p), vh) + _mm_nt(_bf(qh * ecum[:, sl]), _bf(sth))
            st[h] = sth * _last_row(ecum[:, sl], rr) + _mm_tn(vh, _bf(kh * erem[:, sl]))
        o = o_ref[...]
        inv = lax.rsqrt(jnp.mean(o * o, axis=-1, keepdims=True) + EPS)
        ycat_ref[...] = _bf((o * inv) * nw_ref[...] * _silu(u_ref[:, 1536:2048]))

    return pl.pallas_call(
        body, name="hg_fwd", grid=(n,),
        in_specs=[pl.BlockSpec((C, 2048), lambda i: (i, 0)), _vec(HG_W), _vec(HG_W), _full(mall.shape),
                  pl.BlockSpec(memory_space=pl.ANY)],
        out_specs=[pl.BlockSpec((C, HG_W), lambda i: (i, 1)), pl.BlockSpec((C, HG_W), lambda i: (i, 0)),
                   pl.BlockSpec((None, HG_HEADS, LANE, LANE), lambda i: (i, 0, 0, 0))],
        out_shape=[SDS((S, D_INNER), BF16), SDS((S,HG_W), F32), SDS((n, HG_HEADS, LANE, LANE), F32)],
        scratch_shapes=[pltpu.VMEM((HG_HEADS, LANE, LANE), F32)],
        input_output_aliases={4: 0},
        compiler_params=_cp(("arbitrary",)),
    )(u, lb, nw, mall, ycat)


def _hg_bwd(u, lb, nw, mall, mall_t, o_b, states, dycat, du):
    S = u.shape[0]
    C = HG_CHUNK
    n = S // C
    L2 = 2 * HG_LEVELS

    def body(u_ref, lb_ref, nw_ref, mall_ref, mallt_ref, o_ref, st_ref, dy_ref, du_in, du_ref, red_ref,
             dst, dlast_s, dq_s, dk_s, dex):
        del du_in

        @pl.when(pl.program_id(0) == 0)
        def _():
            dst[...] = jnp.zeros_like(dst)
            red_ref[...] = jnp.zeros_like(red_ref)

        lb = lb_ref[...]
        hq, hf, hg = u_ref[:, 0:512], u_ref[:, 512:1024], u_ref[:, 1536:2048]
        q = _silu(hq)
        v = u_ref[:, 1024:1536]
        s, f, k, eq, ek, ecum, erem = _hg_factors(hf, lb, mall_ref[...])
        gm, up, eye, rr = _hg_masks()
        o = o_ref[...]
        dy = dy_ref[...]
        inv = lax.rsqrt(jnp.mean(o * o, axis=-1, keepdims=True) + EPS)
        ohat = o * inv
        nwv = nw_ref[...]
        du_ref[:, 1536:2048] = _bf(dy * ohat * nwv * _dsilu(hg))
        dn = dy * _silu(hg)
        red_ref[0:1, :] += jnp.sum(dn * ohat, axis=0, keepdims=True)
        dohat = dn * nwv
        do = inv * (dohat - ohat * jnp.mean(dohat * ohat, axis=-1, keepdims=True))
        for h in range(HG_HEADS):
            sl = slice(h * LANE, (h + 1) * LANE)
            qh, kh, vh, doh = q[:, sl], k[:, sl], _bf(v[:, sl]), _bf(do[:, sl])
            p, qs, ks = _hg_scores(qh, kh, eq, ek, sl, gm, up, eye)
            st_f = st_ref[h]
            sth = _bf(st_f)
            dsth = dst[h]
            dsth_b = _bf(dsth)
            qt = qh * ecum[:, sl]
            kt = kh * erem[:, sl]
            elast = _last_row(ecum[:, sl], rr)
            dp = _mm_nt(doh, vh)
            du_ref[:, 1024 + h * LANE:1024 + (h + 1) * LANE] = _bf(_mm_tn(_bf(p), doh) + _mm_nt(_bf(kt), dsth_b))
            dpe = _bf(dp * eye)
            dqt = _mm(doh, sth)
            dkt = _mm(vh, dsth_b)
            dq = dqt * ecum[:, sl] + _mm(dpe, _bf(kh))
            dk = dkt * erem[:, sl] + _mm_tn(dpe, _bf(qh))
            dex[L2 * C:(L2 + 1) * C, sl] = dqt * qt
            dex[(L2 + 1) * C:(L2 + 2) * C, sl] = dkt * kt
            for l in range(HG_LEVELS):
                dpl = _bf(dp * gm[l])
                dql = _mm(dpl, _bf(ks[l]))
                dkl = _mm_tn(dpl, _bf(qs[l]))
                dq = dq + jnp.where(up[l], dql * eq[l][:, sl], 0.0)
                dk = dk + jnp.where(up[l], 0.0, dkl * ek[l][:, sl])
                dex[l * C:(l + 1) * C, sl] = dql * qs[l]
                dex[(HG_LEVELS + l) * C:(HG_LEVELS + l + 1) * C, sl] = dkl * ks[l]
            dlast_s[:, sl] = jnp.sum(dsth * st_f, axis=0, keepdims=True) * elast
            dst[h] = dsth * elast + _mm_tn(doh, _bf(qt))
            dq_s[:, sl] = dq
            dk_s[:, sl] = dk
        dq = dq_s[...]
        dk = dk_s[...]
        dlf = _sel_l(mallt_ref[...], dex[...]) + dlast_s[...]
        du_ref[:, 0:512] = _bf(dq * _dsilu(hq))
        t = (1.0 - s) * (dlf / f - dk)
        du_ref[:, 512:1024] = _bf((1.0 - lb) * s * t)
        red_ref[1:2, :] += jnp.sum(t, axis=0, keepdims=True)

    rev = lambda i: (n - 1 - i, 0)
    return pl.pallas_call(
        body, name="hg_bwd", grid=(n,),
        in_specs=[pl.BlockSpec((C, 2048), rev), _vec(HG_W), _vec(HG_W), _full(mall.shape), _full(mall_t.shape),
                  pl.BlockSpec((C, HG_W), rev),
                  pl.BlockSpec((None, HG_HEADS, LANE, LANE), lambda i: (n - 1 - i, 0, 0, 0)),
                  pl.BlockSpec((C, HG_W), lambda i: (n - 1 - i, 1)), pl.BlockSpec(memory_space=pl.ANY)],
        out_specs=[pl.BlockSpec((C, 2048), rev), pl.BlockSpec((8, HG_W), lambda i: (0, 0))],
        out_shape=[SDS((S, N_PAD), BF16), SDS((8, HG_W), F32)],
        scratch_shapes=[pltpu.VMEM((HG_HEADS, LANE, LANE), F32), pltpu.VMEM((1, HG_W), F32),
                        pltpu.VMEM((C, HG_W), F32), pltpu.VMEM((C, HG_W), F32), pltpu.VMEM(((L2 + 2) * C, HG_W), F32)],
        input_output_aliases={8: 0},
        compiler_params=_cp(("arbitrary",)),
    )(u, lb, nw, mall, mall_t, o_b, states, dycat, du)


def _ssdconv_fwd(u, cw, cb):
    S = u.shape[0]

    def body(u_ref, cw_ref, cb_ref, out_ref):
        rows = _iota((S, LANE), 0)
        out_ref[...] = _silu(_conv_fwd(u_ref[...], cw_ref, cb_ref, rows))

    return pl.pallas_call(
        body, name="ssdconv_fwd", grid=(SSD_CONV // LANE,),
        in_specs=[pl.BlockSpec((S, LANE), lambda t: (0, OFF_XBC // LANE + t)), pl.BlockSpec((4, LANE), lambda t: (0, t)),
                  pl.BlockSpec((1, LANE), lambda t: (0, t))],
        out_specs=pl.BlockSpec((S, LANE), lambda t: (0, t)),
        out_shape=SDS((S, SSD_CONV), F32),
        compiler_params=_cp(("parallel",)),
    )(u, cw, cb)


def _ssdconv_bwd(u, cw, cb, dxbc, du):
    S = u.shape[0]

    def body(u_ref, cw_ref, cb_ref, d_ref, du_in, du_ref, red_ref):
        del du_in
        rows = _iota((S, LANE), 0)
        x = u_ref[...]
        dco = d_ref[...] * _dsilu(_conv_fwd(x, cw_ref, cb_ref, rows))
        dx, dws, dcb = _conv_bwd(x, dco, cw_ref, rows)
        du_ref[...] = _bf(dx)
        for n, p in enumerate(dws + [dcb]):
            red_ref[pl.ds(n, 1), :] = p
        red_ref[pl.ds(5, 3), :] = jnp.zeros((3, LANE), F32)

    ucol = pl.BlockSpec((S, LANE), lambda t: (0, OFF_XBC // LANE + t))
    return pl.pallas_call(
        body, name="ssdconv_bwd", grid=(SSD_CONV // LANE,),
        in_specs=[ucol, pl.BlockSpec((4, LANE), lambda t: (0, t)), pl.BlockSpec((1, LANE), lambda t: (0, t)),
                  pl.BlockSpec((S, LANE), lambda t: (0, t)), pl.BlockSpec(memory_space=pl.ANY)],
        out_specs=[ucol, pl.BlockSpec((8, LANE), lambda t: (0, t))],
        out_shape=[SDS((S, N_PAD), BF16), SDS((8, SSD_CONV), F32)],
        input_output_aliases={4: 0},
        compiler_params=_cp(("parallel",)),
    )(u, cw, cb, dxbc, du)


def _ssd_consts():
    e64 = np.zeros((LANE, SSD_W), np.float32)
    e128 = np.zeros((LANE, SSD_HEADS * LANE), np.float32)
    for h in range(SSD_HEADS):
        e64[h, h * SSD_P:(h + 1) * SSD_P] = 1.0
        e128[h, h * LANE:(h + 1) * LANE] = 1.0
    T = SSD_CHUNK
    tril = (np.arange(T)[None, :] <= np.arange(T)[:, None]).astype(np.float32)
    return e64, e128, tril, tril.T.copy()


def _ssd_common(zdt, bias_ref, alog_ref, tril, e64, e128):
    T = SSD_CHUNK
    lane = _iota((1, LANE), 1)
    a_neg = jnp.where(lane < SSD_HEADS, -jnp.exp(alog_ref[...]), 0.0)
    dtpre = zdt[:, SSD_W:SSD_W + LANE] + bias_ref[...]
    dt = _softplus(dtpre)
    cum = _sel_l(tril, dt * a_neg)
    rowsT = _iota((T, LANE), 0)
    last = _last_row(cum, rowsT)
    ecum_x = _sel_r(jnp.exp(cum), e64)
    erem_x = _sel_r(jnp.exp(last - cum), e64)
    elast_x = _last_row(ecum_x, _iota((T, SSD_W), 0))
    dt_x = _sel_r(dt, e64)
    cum_e = _sel_r(cum, e128)
    return a_neg, dtpre, dt, cum, ecum_x, erem_x, elast_x, dt_x, cum_e


def _ssd_decay(cum_e, cumt_ref, h, causal):
    diff = cum_e[:, h * LANE:(h + 1) * LANE] - cumt_ref[pl.ds(h, 1), :]
    return jnp.exp(jnp.where(causal, diff, NEG))


def _group_norm_fwd(y1, nwv):
    outs, invs = [], []
    for g in range(2):
        seg = y1[:, g * 512:(g + 1) * 512]
        inv = lax.rsqrt(jnp.mean(seg * seg, axis=-1, keepdims=True) + EPS)
        outs.append(seg * inv * nwv[:, g * 512:(g + 1) * 512])
        invs.append(inv)
    return outs, invs


def _ssd_fwd(u, xbc, bias, alog, dskip_x, nw, consts, ycat):
    S = u.shape[0]
    T = SSD_CHUNK
    n = S // T
    e64, e128, tril, _ = consts

    def body(u_ref, xbc_ref, bias_ref, alog_ref, dx_ref, nw_ref, e64_ref, e128_ref, tril_ref, ycat_in,
             ycat_ref, y_ref, st_ref, st, cumt):
        del ycat_in

        @pl.when(pl.program_id(0) == 0)
        def _():
            st[...] = jnp.zeros_like(st)

        zdt = u_ref[...]
        z = zdt[:, 0:SSD_W]
        xs = xbc_ref[:, 0:SSD_W]
        _, _, _, cum, ecum_x, erem_x, elast_x, dt_x, cum_e = _ssd_common(
            zdt, bias_ref, alog_ref, tril_ref[...], e64_ref[...], e128_ref[...])
        cumt[...] = cum.T
        causal = _iota((T, T), 0) >= _iota((T, T), 1)
        lo = _iota((T, LANE), 1) < SSD_P
        xdt = xs * dt_x
        xrem = xdt * erem_x
        st_ref[...] = st[...]
        for g in range(2):
            gs = slice(g * 512, (g + 1) * 512)
            bg = _bf(xbc_ref[:, SSD_W + g * LANE:SSD_W + (g + 1) * LANE])
            cg = _bf(xbc_ref[:, SSD_W + 256 + g * LANE:SSD_W + 256 + (g + 1) * LANE])
            cb = _mm_nt(cg, bg)
            yin = _mm(cg, _bf(st[:, gs])) * ecum_x[:, gs]
            for j in range(4):
                h0 = 8 * g + 2 * j
                cs = slice(h0 * SSD_P, (h0 + 2) * SSD_P)
                xp = xdt[:, cs]
                s0 = _bf(cb * _ssd_decay(cum_e, cumt, h0, causal))
                s1 = _bf(cb * _ssd_decay(cum_e, cumt, h0 + 1, causal))
                y_ref[:, cs] = (_mm(s0, _bf(jnp.where(lo, xp, 0.0))) + _mm(s1, _bf(jnp.where(lo, 0.0, xp)))
                                + yin[:, j * LANE:(j + 1) * LANE])
            st[:, gs] = st[:, gs] * elast_x[:, gs] + _mm_tn(bg, _bf(xrem[:, gs]))
        y1 = (y_ref[...] + dx_ref[...] * xs) * _silu(z)
        outs, _ = _group_norm_fwd(y1, nw_ref[...])
        for g in range(2):
            ycat_ref[:, g * 512:(g + 1) * 512] = _bf(outs[g])

    return pl.pallas_call(
        body, name="ssd_fwd", grid=(n,),
        in_specs=[pl.BlockSpec((T, SSD_W + LANE), lambda i: (i, OFF_Z // (SSD_W + LANE))),
                  pl.BlockSpec((T, SSD_CONV), lambda i: (i, 0)), _vec(LANE), _vec(LANE), _vec(SSD_W), _vec(SSD_W),
                  _full(e64.shape), _full(e128.shape), _full(tril.shape), pl.BlockSpec(memory_space=pl.ANY)],
        out_specs=[pl.BlockSpec((T, SSD_W), lambda i: (i, 1)), pl.BlockSpec((T, SSD_W), lambda i: (i, 0)),
                   pl.BlockSpec((None, SSD_N, SSD_W), lambda i: (i, 0, 0))],
        out_shape=[SDS((S, D_INNER), BF16), SDS((S,SSD_W), F32), SDS((n, SSD_N, SSD_W), F32)],
        scratch_shapes=[pltpu.VMEM((SSD_N, SSD_W), F32), pltpu.VMEM((LANE, T), F32)],
        input_output_aliases={9: 0},
        compiler_params=_cp(("arbitrary",)),
    )(u, xbc, bias, alog, dskip_x, nw, _bfc(e64), _bfc(e128), _bfc(tril), ycat)


def _ssd_bwd(u, xbc, bias, alog, dskip_x, nw, consts, y_ssd, states, dycat, du):
    S = u.shape[0]
    T = SSD_CHUNK
    n = S // T
    e64, e128, tril, triu = consts
    e64t = np.ascontiguousarray(e64.T)

    def body(u_ref, xbc_ref, bias_ref, alog_ref, dx_ref, nw_ref, e64_ref, e64t_ref, e128_ref, tril_ref, triu_ref,
             y_ref, st_ref, dy_ref, du_in, du_ref, dxbc_ref, red_ref, dst, dl_s, cumt, dxdt_s, dy0_s, gb_s, gc_s):
        del du_in

        @pl.when(pl.program_id(0) == 0)
        def _():
            dst[...] = jnp.zeros_like(dst)
            red_ref[...] = jnp.zeros_like(red_ref)

        zdt = u_ref[...]
        z = zdt[:, 0:SSD_W]
        xs = xbc_ref[:, 0:SSD_W]
        e64m = e64_ref[...]
        a_neg, dtpre, dt, cum, ecum_x, erem_x, elast_x, dt_x, cum_e = _ssd_common(
            zdt, bias_ref, alog_ref, tril_ref[...], e64m, e128_ref[...])
        cumt[...] = cum.T
        causal = _iota((T, T), 0) >= _iota((T, T), 1)
        lo = _iota((T, LANE), 1) < SSD_P
        xdt = xs * dt_x
        xrem = xdt * erem_x
        y = y_ref[...]
        dxv = dx_ref[...]
        nwv = nw_ref[...]
        sz = _silu(z)
        y0 = y + dxv * xs
        y1 = y0 * sz
        for g in range(2):
            gs = slice(g * 512, (g + 1) * 512)
            seg = y1[:, gs]
            inv = lax.rsqrt(jnp.mean(seg * seg, axis=-1, keepdims=True) + EPS)
            shat = seg * inv
            dyg = dy_ref[:, gs]
            red_ref[0:1, gs] += jnp.sum(dyg * shat, axis=0, keepdims=True)
            dsh = dyg * nwv[:, gs]
            dy1g = inv * (dsh - shat * jnp.mean(dsh * shat, axis=-1, keepdims=True))
            du_ref[:, gs] = _bf(dy1g * y0[:, gs] * _dsilu(z[:, gs]))
            dy0_s[:, gs] = dy1g * sz[:, gs]
        dy0 = dy0_s[...]
        red_ref[1:2, :] += jnp.sum(dy0 * xs, axis=0, keepdims=True)
        dyin = dy0 * ecum_x
        lane = _iota((T, LANE), 1)
        ones = jnp.ones((T, LANE), BF16)
        dcum = jnp.zeros((T, LANE), F32)

        def row_minus_col(gm):
            hi = _bf(gm)
            lw = _bf(gm - hi.astype(F32))
            return _mm(hi, ones) + _mm(lw, ones) - _mm_tn(hi, ones) - _mm_tn(lw, ones)

        for g in range(2):
            gs = slice(g * 512, (g + 1) * 512)
            bg = _bf(xbc_ref[:, SSD_W + g * LANE:SSD_W + (g + 1) * LANE])
            cg = _bf(xbc_ref[:, SSD_W + 256 + g * LANE:SSD_W + 256 + (g + 1) * LANE])
            cb = _mm_nt(cg, bg)
            dst_f, st_f = dst[:, gs], st_ref[:, gs]
            dstg = _bf(dst_f)
            stg = _bf(st_f)
            dyin_g = _bf(dyin[:, gs])
            xrem_g = _bf(xrem[:, gs])
            dcb = jnp.zeros((T, T), F32)
            dxr = _mm(bg, dstg)
            dxdt_s[:, gs] = dxr * erem_x[:, gs]
            gc_s[:, gs] = dxr * xrem[:, gs]
            gb_s[:, gs] = dyin[:, gs] * _mm(cg, stg)
            dl_s[:, gs] = jnp.sum(dst_f * st_f, axis=0, keepdims=True) * elast_x[:, gs]
            for j in range(4):
                h0 = 8 * g + 2 * j
                cs = slice(h0 * SSD_P, (h0 + 2) * SSD_P)
                xp = xdt[:, cs]
                dyp = dy0[:, cs]
                x_lo, x_hi = _bf(jnp.where(lo, xp, 0.0)), _bf(jnp.where(lo, 0.0, xp))
                d_lo, d_hi = _bf(jnp.where(lo, dyp, 0.0)), _bf(jnp.where(lo, 0.0, dyp))
                s0 = cb * _ssd_decay(cum_e, cumt, h0, causal)
                s1 = cb * _ssd_decay(cum_e, cumt, h0 + 1, causal)
                ds0 = _mm_nt(d_lo, x_lo)
                ds1 = _mm_nt(d_hi, x_hi)
                dcb = dcb + ds0 * _ssd_decay(cum_e, cumt, h0, causal) + ds1 * _ssd_decay(cum_e, cumt, h0 + 1, causal)
                dxdt_s[:, cs] += _mm_tn(_bf(s0), d_lo) + _mm_tn(_bf(s1), d_hi)
                dcum = dcum + jnp.where(lane == h0, row_minus_col(ds0 * s0), 0.0)
                dcum = dcum + jnp.where(lane == h0 + 1, row_minus_col(ds1 * s1), 0.0)
            dcb_b = _bf(dcb)
            dxbc_ref[:, SSD_W + g * LANE:SSD_W + (g + 1) * LANE] = _mm_tn(dcb_b, cg) + _mm_nt(xrem_g, dstg)
            dxbc_ref[:, SSD_W + 256 + g * LANE:SSD_W + 256 + (g + 1) * LANE] = _mm(dcb_b, bg) + _mm_nt(dyin_g, stg)
            dst[:, gs] = dst_f * elast_x[:, gs] + _mm_tn(cg, dyin_g)
        dxdt = dxdt_s[...]
        dxbc_ref[:, 0:SSD_W] = dxdt * dt_x + dy0 * dxv
        e64t = e64t_ref[...]
        hc = _sel_r(gc_s[...], e64t)
        dlast = (jnp.sum(hc, axis=0, keepdims=True)
                 + jnp.max(_sel_r(jnp.broadcast_to(dl_s[...], (8, SSD_W)), e64t), axis=0, keepdims=True))
        dcum = dcum + _sel_r(gb_s[...], e64t) - hc + jnp.where(_iota((T, LANE), 0) == T - 1, dlast, 0.0)
        dda = _sel_l(triu_ref[...], dcum)
        ddt = dda * a_neg + _sel_r(dxdt * xs, e64t)
        ddtpre = ddt * _sigmoid(dtpre)
        du_ref[:, SSD_W:SSD_W + LANE] = _bf(jnp.where(lane < SSD_HEADS, ddtpre, 0.0))
        red_ref[2:3, 0:LANE] += jnp.sum(ddtpre, axis=0, keepdims=True)
        red_ref[3:4, 0:LANE] += jnp.sum(dda * dt, axis=0, keepdims=True)

    rev = lambda i: (n - 1 - i, 0)
    return pl.pallas_call(
        body, name="ssd_bwd", grid=(n,),
        in_specs=[pl.BlockSpec((T, SSD_W + LANE), lambda i: (n - 1 - i, OFF_Z // (SSD_W + LANE))),
                  pl.BlockSpec((T, SSD_CONV), rev), _vec(LANE), _vec(LANE), _vec(SSD_W), _vec(SSD_W),
                  _full(e64.shape), _full(e64t.shape), _full(e128.shape), _full(tril.shape), _full(triu.shape),
                  pl.BlockSpec((T, SSD_W), rev), pl.BlockSpec((None, SSD_N, SSD_W), lambda i: (n - 1 - i, 0, 0)),
                  pl.BlockSpec((T, SSD_W), lambda i: (n - 1 - i, 1)), pl.BlockSpec(memory_space=pl.ANY)],
        out_specs=[pl.BlockSpec((T, SSD_W + LANE), lambda i: (n - 1 - i, OFF_Z // (SSD_W + LANE))),
                   pl.BlockSpec((T, SSD_CONV), rev), pl.BlockSpec((8, SSD_W), lambda i: (0, 0))],
        out_shape=[SDS((S, N_PAD), BF16), SDS((S, SSD_CONV), F32), SDS((8, SSD_W), F32)],
        scratch_shapes=[pltpu.VMEM((SSD_N, SSD_W), F32), pltpu.VMEM((1, SSD_W), F32), pltpu.VMEM((LANE, T), F32)]
        + [pltpu.VMEM((T, SSD_W), F32)] * 4,
        input_output_aliases={14: 0},
        compiler_params=_cp(("arbitrary",)),
    )(u, xbc, bias, alog, dskip_x, nw, _bfc(e64), _bfc(e64t), _bfc(e128), _bfc(tril), _bfc(triu), y_ssd, states, dycat, du)


def _bfc(a):
    return jnp.asarray(a, BF16)


def _outproj_fwd(ycat, wo, x, gate):
    S = x.shape[0]
    tm = min(512, S)

    def body(yc_ref, wo_ref, x_ref, g_ref, xn_ref, y_ref):
        y = _mm(_bf(yc_ref[...]), wo_ref[...])
        y_ref[...] = y
        xn_ref[...] = x_ref[...] + g_ref[...] * y

    row = pl.BlockSpec((tm, D_MODEL), lambda i: (i, 0))
    return pl.pallas_call(
        body, name="outproj_fwd", grid=(S // tm,),
        in_specs=[pl.BlockSpec((tm, D_INNER), lambda i: (i, 0)), _full((D_INNER, D_MODEL)), row, _vec(D_MODEL)],
        out_specs=[row, row],
        out_shape=[SDS((S, D_MODEL), F32), SDS((S, D_MODEL), F32)],
        compiler_params=_cp(("parallel",)),
    )(ycat, wo, x, gate)


def _outproj_bwd(dxn, y, gate, ycat, wo):
    S = dxn.shape[0]
    tm = min(512, S)

    def body(dx_ref, y_ref, g_ref, yc_ref, wo_ref, dyc_ref, gwo_ref, dg_ref, acc):
        @pl.when(pl.program_id(0) == 0)
        def _():
            acc[...] = jnp.zeros_like(acc)
            dg_ref[...] = jnp.zeros_like(dg_ref)

        dxv = dx_ref[...]
        dy = _bf(dxv * g_ref[...])
        dg_ref[0:1, :] += jnp.sum(dxv * y_ref[...], axis=0, keepdims=True)
        dyc_ref[...] = _mm_nt(dy, wo_ref[...])
        acc[...] += _mm_tn(_bf(yc_ref[...]), dy)

        @pl.when(pl.program_id(0) == pl.num_programs(0) - 1)
        def _():
            gwo_ref[...] = acc[...].astype(BF16)

    row = pl.BlockSpec((tm, D_MODEL), lambda i: (i, 0))
    wide = pl.BlockSpec((tm, D_INNER), lambda i: (i, 0))
    return pl.pallas_call(
        body, name="outproj_bwd", grid=(S // tm,),
        in_specs=[row, row, _vec(D_MODEL), wide, _full((D_INNER, D_MODEL))],
        out_specs=[wide, _full((D_INNER, D_MODEL)), _full((8, D_MODEL))],
        out_shape=[SDS((S, D_INNER), F32), SDS((D_INNER, D_MODEL), BF16), SDS((8, D_MODEL), F32)],
        scratch_shapes=[pltpu.VMEM((D_INNER, D_MODEL), F32)],
        compiler_params=_cp(("arbitrary",)),
    )(dxn, y, gate, ycat, wo)


def _loss_head(x, fw, target):
    S = x.shape[0]
    tm = min(512, S)

    def body(x_ref, fw_ref, t_ref, dx_ref, red_ref):
        @pl.when(pl.program_id(0) == 0)
        def _():
            red_ref[...] = jnp.zeros_like(red_ref)

        xv = x_ref[...]
        fwv = fw_ref[...]
        inv = lax.rsqrt(jnp.mean(xv * xv, axis=-1, keepdims=True) + EPS)
        xhat = xv * inv
        err = xhat * fwv - t_ref[...]
        col = jnp.sum(err * err, axis=0, keepdims=True)
        red_ref[1:2, :] += jnp.broadcast_to(jnp.sum(col, axis=1, keepdims=True) * (0.5 / D_MODEL), (1, D_MODEL))
        dy = err * (1.0 / D_MODEL)
        red_ref[0:1, :] += jnp.sum(dy * xhat, axis=0, keepdims=True)
        dxhat = dy * fwv
        dx_ref[...] = inv * (dxhat - xhat * jnp.mean(dxhat * xhat, axis=-1, keepdims=True))

    row = pl.BlockSpec((tm, D_MODEL), lambda i: (i, 0))
    return pl.pallas_call(
        body, name="loss_head", grid=(S // tm,),
        in_specs=[row, _vec(D_MODEL), row],
        out_specs=[row, _full((8, D_MODEL))],
        out_shape=[SDS((S, D_MODEL), F32), SDS((8, D_MODEL), F32)],
        compiler_params=_cp(("arbitrary",)),
    )(x, fw, target)


ADA_COLS = 3 * D_MODEL // N_DEV


def _ada_fwd(c_all, w_ada, b_cols):
    def body(c_ref, w_ref, b_ref, out_ref):
        out_ref[...] = _mm(_bf(_silu(c_ref[...])), _bf(w_ref[...])) + b_ref[...]

    return pl.pallas_call(
        body, name="ada_fwd", grid=(DEPTH,),
        in_specs=[_full((N_DEV, D_MODEL)), pl.BlockSpec((None, D_MODEL, ADA_COLS), lambda l: (l, 0, 0)),
                  pl.BlockSpec((None, 1, ADA_COLS), lambda l: (l, 0, 0))],
        out_specs=pl.BlockSpec((None, N_DEV, ADA_COLS), lambda l: (l, 0, 0)),
        out_shape=SDS((DEPTH, N_DEV, ADA_COLS), F32),
        compiler_params=_cp(("parallel",)),
    )(c_all, w_ada, b_cols)


def _ada_bwd(ct_pad, dmod_pad):
    def body(c_ref, d_ref, out_ref):
        out_ref[...] = _mm(_bf(_silu(c_ref[...])), _bf(d_ref[...]))

    return pl.pallas_call(
        body, name="ada_bwd", grid=(DEPTH,),
        in_specs=[_full((D_MODEL, LANE)), pl.BlockSpec((None, LANE, ADA_COLS), lambda l: (l, 0, 0))],
        out_specs=pl.BlockSpec((None, D_MODEL, ADA_COLS), lambda l: (l, 0, 0)),
        out_shape=SDS((DEPTH, D_MODEL, ADA_COLS), F32),
        compiler_params=_cp(("parallel",)),
    )(ct_pad, dmod_pad)


def _adamw(parts, w, m, v, name, own=None):
    n, L, R, C = parts.shape
    tr = R
    while tr * C * 4 > (1 << 20) and tr % 16 == 0:
        tr //= 2

    def body(*refs):
        p_ref, w_ref, m_ref, v_ref, g_ref, d_ref, mo_ref, vo_ref = refs[:1] + refs[-7:]

        def part(k):
            if own is None:
                return p_ref[k].astype(F32)
            me = 4 * lax.axis_index("x") + 2 * lax.axis_index("y") + lax.axis_index("c")
            return jnp.where(me == k, refs[1][...], p_ref[k]).astype(F32)

        g = part(0)
        for k in range(1, n):
            g = g + part(k)
        mn = ADAM_B1 * m_ref[...] + (1.0 - ADAM_B1) * g
        vn = ADAM_B2 * v_ref[...] + (1.0 - ADAM_B2) * (g * g)
        m_hat = mn / (1.0 - ADAM_B1 ** ADAM_STEP)
        v_hat = vn / (1.0 - ADAM_B2 ** ADAM_STEP)
        g_ref[...] = g
        d_ref[...] = -ADAM_LR * (m_hat / (jnp.sqrt(v_hat) + ADAM_EPS) + ADAM_WD * w_ref[...])
        mo_ref[...] = mn
        vo_ref[...] = vn

    blk = pl.BlockSpec((None, tr, C), lambda l, i: (l, i, 0))
    return pl.pallas_call(
        body, name=name, grid=(L, R // tr),
        in_specs=[pl.BlockSpec((n, None, tr, C), lambda l, i: (0, l, i, 0))] + [blk] * (3 if own is None else 4),
        out_specs=[blk] * 4,
        out_shape=[SDS((L, R, C), F32)] * 4,
        compiler_params=_cp(("parallel", "parallel")),
    )(parts, *([] if own is None else [own]), w, m, v)


MESH = pl.DeviceIdType.MESH
ANY = pl.BlockSpec(memory_space=pl.ANY)


def _all_gather(v, name):
    def body(v_ref, out_ref, send_sems, recv_sems, local_sem):
        x, y, c = lax.axis_index("x"), lax.axis_index("y"), lax.axis_index("c")
        me, sibling = (x, y, c), (x, y, 1 - c)
        chips = [(1 - x, y), (x, 1 - y), (1 - x, 1 - y)]

        def slot(px, py, pc):
            return out_ref.at[4 * px + 2 * py + pc]

        def copy(k, block, to, src=None):
            return pltpu.make_async_remote_copy(
                src_ref=slot(*block) if src is None else src, dst_ref=slot(*block),
                send_sem=send_sems.at[k], recv_sem=recv_sems.at[k], device_id=to, device_id_type=MESH)

        mine = pltpu.make_async_copy(v_ref, slot(*me), local_sem)
        mine.start()
        first = [copy(0, me, sibling, src=v_ref)]
        first += [copy(1 + j, me, (*chip, c), src=v_ref) for j, chip in enumerate(chips)]
        for cp in first:
            cp.start()
        passed = [copy(4 + j, (*chip, c), sibling) for j, chip in enumerate(chips)]
        for j, chip in enumerate(chips):
            copy(1 + j, (*chip, c), me).wait_recv()
            passed[j].start()
        copy(0, sibling, me).wait_recv()
        for j, chip in enumerate(chips):
            copy(4 + j, (*chip, 1 - c), me).wait_recv()
        for cp in first + passed:
            cp.wait_send()
        mine.wait()

    return pl.pallas_call(
        body, name=name, in_specs=[ANY], out_specs=ANY,
        out_shape=SDS((N_DEV,) + v.shape, v.dtype),
        scratch_shapes=[pltpu.SemaphoreType.DMA((7,)), pltpu.SemaphoreType.DMA((7,)), pltpu.SemaphoreType.DMA],
    )(v)


def _all_to_all(v, name):
    def body(v_ref, out_ref, send_sems, recv_sems, local_sem):
        x, y, c = lax.axis_index("x"), lax.axis_index("y"), lax.axis_index("c")
        mine_idx = 4 * x + 2 * y + c
        mine = pltpu.make_async_copy(v_ref.at[mine_idx], out_ref.at[mine_idx], local_sem)
        mine.start()
        sends, recvs = [], []
        for k in range(1, N_DEV):
            px = 1 - x if k & 4 else x
            py = 1 - y if k & 2 else y
            pc = 1 - c if k & 1 else c
            peer_idx = 4 * px + 2 * py + pc
            sems = dict(send_sem=send_sems.at[k - 1], recv_sem=recv_sems.at[k - 1], device_id=(px, py, pc),
                        device_id_type=MESH)
            sends.append(pltpu.make_async_remote_copy(src_ref=v_ref.at[peer_idx], dst_ref=out_ref.at[mine_idx], **sems))
            recvs.append(pltpu.make_async_remote_copy(src_ref=v_ref.at[peer_idx], dst_ref=out_ref.at[peer_idx], **sems))
        for cp in sends:
            cp.start()
        for cp in recvs:
            cp.wait_recv()
        for cp in sends:
            cp.wait_send()
        mine.wait()

    return pl.pallas_call(
        body, name=name, in_specs=[ANY], out_specs=ANY,
        out_shape=SDS(v.shape, v.dtype),
        scratch_shapes=[pltpu.SemaphoreType.DMA((7,)), pltpu.SemaphoreType.DMA((7,)), pltpu.SemaphoreType.DMA],
    )(v)


HBM_SPEC = pl.BlockSpec(memory_space=pltpu.HBM)
SEM_SPEC = pl.BlockSpec(memory_space=pltpu.SEMAPHORE)
EFFECT = pltpu.SideEffectType.DATAFLOW_SIDE_EFFECTING


EXCHANGE_PEERS = {"gather": range(1, N_DEV), "scatter": range(1, N_DEV), "chip": (1, 2, 4, 6), "pass": (2, 4, 6)}


def _exchange_copies(srcs, lands, send_sems, recv_sems, mode, layer):
    x, y, c = lax.axis_index("x"), lax.axis_index("y"), lax.axis_index("c")
    me = 4 * x + 2 * y + c
    copies = []
    for a, (src, land) in enumerate(zip(srcs, lands)):
        for k in EXCHANGE_PEERS[mode]:
            px = 1 - x if k & 4 else x
            py = 1 - y if k & 2 else y
            pc = 1 - c if k & 1 else c
            peer = 4 * px + 2 * py + pc
            if mode == "scatter":
                s, d, to = src.at[peer], land.at[me, layer], (px, py, pc)
            elif mode == "pass":
                s, d, to = land.at[peer], land.at[peer], (x, y, 1 - c)
            else:
                s, d, to = src, land.at[me], (px, py, pc)
            n = 7 * a + k - 1
            copies.append(pltpu.make_async_remote_copy(
                src_ref=s, dst_ref=d, send_sem=send_sems.at[n], recv_sem=recv_sems.at[n], device_id=to,
                device_id_type=MESH))
    return copies


def _exchange_start(name, srcs, lands, mode, layer=0, after=None):
    n = len(srcs)

    def body(*refs):
        send_sems, recv_sems = refs[-2 * n - 3], refs[-2 * n - 2]
        for cp in _exchange_copies(refs[:n], refs[n:2 * n], send_sems, recv_sems, mode, layer):
            cp.start()
        refs[-1][...] = jnp.zeros_like(refs[-1])

    arrays = list(srcs) + list(lands)
    sems = pltpu.SemaphoreType.DMA((7 * n,))
    out = pl.pallas_call(
        body, name=name,
        out_shape=(sems, sems, *[pltpu.HBM(v.shape, v.dtype) for v in arrays], SDS((8, LANE), F32)),
        in_specs=[HBM_SPEC] * (2 * n) + ([ANY] if after is not None else []),
        out_specs=(SEM_SPEC, SEM_SPEC, *[HBM_SPEC] * (2 * n), pl.BlockSpec(memory_space=pltpu.VMEM)),
        input_output_aliases={i: 2 + i for i in range(2 * n)},
        compiler_params=pltpu.CompilerParams(has_side_effects=EFFECT),
    )(*[pltpu.with_memory_space_constraint(v, pltpu.HBM) for v in arrays], *([after] if after is not None else []))
    return dict(sems=out[:2], srcs=out[2:2 + n], lands=out[2 + n:2 + 2 * n], token=out[-1][0, 0], mode=mode,
                layer=layer)


def _exchange_wait(name, st, after):
    n = len(st["srcs"])

    def body(*refs):
        send_sems, recv_sems = refs[2 * n], refs[2 * n + 1]
        for cp in _exchange_copies(refs[:n], refs[n:2 * n], send_sems, recv_sems, st["mode"], st["layer"]):
            cp.wait_send()
            cp.wait_recv()

    arrays = list(st["srcs"]) + list(st["lands"])
    out = pl.pallas_call(
        body, name=name,
        out_shape=tuple(pltpu.HBM(v.shape, v.dtype) for v in arrays),
        in_specs=[HBM_SPEC] * (2 * n) + [SEM_SPEC, SEM_SPEC, ANY],
        out_specs=tuple([HBM_SPEC] * (2 * n)),
        input_output_aliases={i: i for i in range(2 * n)},
        compiler_params=pltpu.CompilerParams(has_side_effects=EFFECT),
    )(*arrays, *st["sems"], after)
    st["srcs"] = out[:n]
    return out[n:]


_IN_PIECES = ([(1024, 3072)]
              + [r for t in range(4) for r in ((LANE * t, LANE * (t + 1)), (512 + LANE * t, 512 + LANE * (t + 1)))]
              + [(4096, 5632), (3072, 4096), (5632, 5648)])


def _permute_in(w):
    pad = jnp.zeros(w.shape[:-1] + (N_PAD - N_IN,), w.dtype)
    return jnp.concatenate([w[..., a:b] for a, b in _IN_PIECES] + [pad], axis=-1)


def _unpermute_in(g):
    ax = [g[..., OFF_LRU + 2 * LANE * t:OFF_LRU + 2 * LANE * t + LANE] for t in range(4)]
    ag = [g[..., OFF_LRU + 2 * LANE * t + LANE:OFF_LRU + 2 * LANE * (t + 1)] for t in range(4)]
    return jnp.concatenate(ax + ag + [g[..., 0:2048], g[..., OFF_Z:OFF_Z + SSD_W], g[..., OFF_XBC:OFF_XBC + SSD_CONV],
                                      g[..., OFF_Z + SSD_W:OFF_Z + SSD_W + SSD_HEADS]], axis=-1)


SHARD_COLS = N_IN // N_DEV


def _in_segments():
    segs, pos = [], 0
    for a, b in _IN_PIECES:
        for i in range(N_DEV):
            lo, hi = max(a, SHARD_COLS * i), min(b, SHARD_COLS * (i + 1))
            if lo < hi:
                segs.append((i, lo - SHARD_COLS * i, hi - lo, pos + lo - a))
        pos += b - a
    return segs


RELAYOUT_ROWS = 256


def _relayout_in(land, own):
    def body(land_ref, own_ref, out_ref):
        me = 4 * lax.axis_index("x") + 2 * lax.axis_index("y") + lax.axis_index("c")
        out_ref[:, N_IN:N_PAD] = jnp.zeros((RELAYOUT_ROWS, N_PAD - N_IN), BF16)
        for i, j, wd, p in _in_segments():
            out_ref[:, p:p + wd] = jnp.where(me == i, own_ref[:, j:j + wd], land_ref[i, :, j:j + wd])

    return pl.pallas_call(
        body, name="relayout_in", grid=(D_MODEL // RELAYOUT_ROWS,),
        in_specs=[pl.BlockSpec((N_DEV, RELAYOUT_ROWS, SHARD_COLS), lambda r: (0, r, 0)),
                  pl.BlockSpec((RELAYOUT_ROWS, SHARD_COLS), lambda r: (r, 0))],
        out_specs=pl.BlockSpec((RELAYOUT_ROWS, N_PAD), lambda r: (r, 0)),
        out_shape=SDS((D_MODEL, N_PAD), BF16),
        compiler_params=_cp(("parallel",)),
    )(land, own)


def _relayout_grad(g):
    def body(g_ref, out_ref):
        for i, j, wd, p in _in_segments():
            out_ref[i, :, j:j + wd] = g_ref[:, p:p + wd].astype(BF16)

    return pl.pallas_call(
        body, name="relayout_grad", grid=(D_MODEL // RELAYOUT_ROWS,),
        in_specs=[pl.BlockSpec((RELAYOUT_ROWS, N_PAD), lambda r: (r, 0))],
        out_specs=pl.BlockSpec((N_DEV, RELAYOUT_ROWS, SHARD_COLS), lambda r: (0, r, 0)),
        out_shape=SDS((N_DEV, D_MODEL, SHARD_COLS), BF16),
        compiler_params=_cp(("parallel",)),
    )(g)


def _block_diag(w):
    w4 = w.reshape(4, 2, 64, 64)
    z = jnp.zeros((4, 64, 64), w.dtype)
    top = jnp.concatenate([w4[:, 0], z], axis=-1)
    bot = jnp.concatenate([z, w4[:, 1]], axis=-1)
    return jnp.concatenate([top, bot], axis=1).astype(BF16)


def _diag_blocks(g):
    return jnp.stack([g[:, :64, :64], g[:, 64:, 64:]], axis=1).reshape(8, 64, 64)


def _pad_lanes(v):
    return jnp.pad(v, (0, LANE - v.shape[0]))[None, :]


def _lower_bounds(logits):
    p = jax.nn.softmax(logits, axis=0)
    return p, jnp.cumsum(p, axis=0) - p[0]


def _lower_bounds_bwd(p, dlb):
    dp = jnp.cumsum(dlb[::-1], axis=0)[::-1]
    dp = dp.at[0].add(-jnp.sum(dlb, axis=0))
    return p * (dp - jnp.sum(dp * p, axis=0, keepdims=True))


SMALL = ["norm_w", "b_ada", "lru_conv_b", "lru_wa", "lru_ba", "lru_wx", "lru_bx", "lru_lambda", "hg_lb_logits",
         "hg_norm_w", "ssd_conv_b", "ssd_dt_bias", "ssd_a_log", "ssd_d", "ssd_norm_w", "final_norm_w"]
WEIGHTS = ["norm_w", "w_ada", "b_ada", "w_in", "lru_conv_w", "lru_conv_b", "lru_wa", "lru_ba", "lru_wx", "lru_bx",
           "lru_lambda", "hg_lb_logits", "hg_norm_w", "ssd_conv_w", "ssd_conv_b", "ssd_dt_bias", "ssd_a_log", "ssd_d",
           "ssd_norm_w", "w_out", "final_norm_w"]
INPUTS = ["x", "c"] + WEIGHTS + ["loss_target"] + ["m_" + n for n in WEIGHTS] + ["v_" + n for n in WEIGHTS]
SMALL_ROW = 1024


def _small_rows(like):
    out, off = {}, 0
    for n in SMALL:
        rows = -(-int(np.prod(like[n].shape)) // (8 * SMALL_ROW)) * 8
        out[n] = (off, rows)
        off += rows
    return out, off


def _flatten_small(d, prefix=""):
    table, _ = _small_rows({n: d[prefix + n] for n in SMALL})
    pieces = []
    for n in SMALL:
        flat = d[prefix + n].reshape(-1)
        pieces.append(jnp.pad(flat, (0, table[n][1] * SMALL_ROW - flat.shape[0])).reshape(-1, SMALL_ROW))
    return jnp.concatenate(pieces, axis=0)


def _split_small(packed, like):
    table, _ = _small_rows(like)
    out = {}
    for n in SMALL:
        off, rows = table[n]
        size = int(np.prod(like[n].shape))
        out[n] = packed[off:off + rows].reshape(-1)[:size].reshape(like[n].shape)
    return out


def _local_step(x, mod, target, w, fetch, emit):
    S = x.shape[0]
    mall = _bfc(_hg_consts())
    mall_t = _bfc(_hg_consts().T)
    consts = _ssd_consts()
    p_lb, lbs = _lower_bounds(w["hg_lb_logits"])
    saved = []
    for l in range(DEPTH):
        w_in_l, w_out_l, token = fetch(l, x)
        shift, scale, gate = (mod[l:l + 1, k * D_MODEL:(k + 1) * D_MODEL] for k in range(3))
        shift = shift + token
        prm = dict(
            nw=w["norm_w"][l:l + 1], cw=w["lru_conv_w"][l], cb=w["lru_conv_b"][l:l + 1],
            wa=_block_diag(w["lru_wa"][l]), ba=w["lru_ba"][l].reshape(1, LRU_W),
            wx=_block_diag(w["lru_wx"][l]), bx=w["lru_bx"][l].reshape(1, LRU_W), lam=w["lru_lambda"][l:l + 1],
            lb=lbs[l:l + 1], hnw=w["hg_norm_w"][l:l + 1], scw=w["ssd_conv_w"][l], scb=w["ssd_conv_b"][l:l + 1],
            bias=_pad_lanes(w["ssd_dt_bias"][l]), alog=_pad_lanes(w["ssd_a_log"][l]),
            dskip=jnp.repeat(w["ssd_d"][l], SSD_P)[None, :], snw=w["ssd_norm_w"][l:l + 1],
            w_in=w_in_l, w_out=w_out_l, scale=scale, gate=gate)
        u, h = _inproj_fwd(x, prm["nw"], scale, shift, prm["w_in"])
        ycat = lax.empty((S, D_INNER), BF16)
        lru_args = (u, prm["cw"], prm["cb"], prm["wa"], prm["ba"], prm["wx"], prm["bx"], prm["lam"])
        ycat, h_lru = _lru_fwd(*lru_args, ycat)
        ycat, o_b, hg_st = _hg_fwd(u, prm["lb"], prm["hnw"], mall, ycat)
        xbc = _ssdconv_fwd(u, prm["scw"], prm["scb"])
        ssd_args = (u, xbc, prm["bias"], prm["alog"], prm["dskip"], prm["snw"], consts)
        ycat, y_ssd, ssd_st = _ssd_fwd(*ssd_args, ycat)
        x_new, y = _outproj_fwd(ycat, prm["w_out"], x, gate)
        saved.append((prm, x, u, h, ycat, lru_args, h_lru, o_b, hg_st, ssd_args, y_ssd, ssd_st, y))
        x = x_new
    dx, red = _loss_head(x, w["final_norm_w"][None, :], target)
    loss = red[1, 0]
    g = {n: [None] * DEPTH for n in WEIGHTS}
    g["final_norm_w"] = red[0]
    dmod, dlb = [None] * DEPTH, [None] * DEPTH
    for l in reversed(range(DEPTH)):
        prm, x, u, h, ycat, lru_args, h_lru, o_b, hg_st, ssd_args, y_ssd, ssd_st, y = saved[l]
        dycat, g_out, dgate = _outproj_bwd(dx, y, prm["gate"], ycat, prm["w_out"])
        token = emit(l, "w_out", g_out)
        du = lax.empty((S, N_PAD), BF16)
        ssd_args = ssd_args[:5] + (ssd_args[5] + token,) + ssd_args[6:]
        du, dxbc, sred = _ssd_bwd(*ssd_args, y_ssd, ssd_st, dycat, du)
        du, cred = _ssdconv_bwd(u, prm["scw"], prm["scb"], dxbc, du)
        du, hred = _hg_bwd(u, prm["lb"], prm["hnw"], mall, mall_t, o_b, hg_st, dycat, du)
        du, lred, gwa, gwx = _lru_bwd(*lru_args, h_lru, dycat, du)
        token = emit(l, "w_in", _inproj_bwd_w(h, du))
        dx, ired = _inproj_bwd_x(du, prm["w_in"], x, prm["nw"], prm["scale"] + token, dx)
        g["norm_w"][l] = ired[2]
        dmod[l] = jnp.concatenate([ired[0], ired[1], dgate[0]])
        g["lru_conv_w"][l], g["lru_conv_b"][l] = lred[0:4], lred[4]
        g["lru_ba"][l], g["lru_bx"][l], g["lru_lambda"][l] = lred[5].reshape(8, 64), lred[6].reshape(8, 64), lred[7]
        g["lru_wa"][l], g["lru_wx"][l] = _diag_blocks(gwa), _diag_blocks(gwx)
        g["hg_norm_w"][l], dlb[l] = hred[0], hred[1]
        g["ssd_conv_w"][l], g["ssd_conv_b"][l] = cred[0:4], cred[4]
        g["ssd_norm_w"][l] = sred[0]
        g["ssd_d"][l] = sred[1].reshape(SSD_HEADS, SSD_P).sum(-1)
        g["ssd_dt_bias"][l] = sred[2, :SSD_HEADS]
        g["ssd_a_log"][l] = -sred[3, :SSD_HEADS] * jnp.exp(w["ssd_a_log"][l])
    g["hg_lb_logits"] = _lower_bounds_bwd(p_lb, jnp.stack(dlb))
    for n in WEIGHTS:
        if isinstance(g[n], list) and g[n][0] is not None:
            g[n] = jnp.stack(g[n])
    return loss, dx, jnp.stack(dmod), g


def kernel(x, c, norm_w, w_ada, b_ada, w_in, lru_conv_w, lru_conv_b, lru_wa, lru_ba, lru_wx, lru_bx, lru_lambda, hg_lb_logits, hg_norm_w, ssd_conv_w, ssd_conv_b, ssd_dt_bias, ssd_a_log, ssd_d, ssd_norm_w, w_out, final_norm_w, loss_target, m_norm_w, m_w_ada, m_b_ada, m_w_in, m_lru_conv_w, m_lru_conv_b, m_lru_wa, m_lru_ba, m_lru_wx, m_lru_bx, m_lru_lambda, m_hg_lb_logits, m_hg_norm_w, m_ssd_conv_w, m_ssd_conv_b, m_ssd_dt_bias, m_ssd_a_log, m_ssd_d, m_ssd_norm_w, m_w_out, m_final_norm_w, v_norm_w, v_w_ada, v_b_ada, v_w_in, v_lru_conv_w, v_lru_conv_b, v_lru_wa, v_lru_ba, v_lru_wx, v_lru_bx, v_lru_lambda, v_hg_lb_logits, v_hg_norm_w, v_ssd_conv_w, v_ssd_conv_b, v_ssd_dt_bias, v_ssd_a_log, v_ssd_d, v_ssd_norm_w, v_w_out, v_final_norm_w):
    return _step(x, c, norm_w, w_ada, b_ada, w_in, lru_conv_w, lru_conv_b, lru_wa, lru_ba, lru_wx, lru_bx, lru_lambda, hg_lb_logits, hg_norm_w, ssd_conv_w, ssd_conv_b, ssd_dt_bias, ssd_a_log, ssd_d, ssd_norm_w, w_out, final_norm_w, loss_target, m_norm_w, m_w_ada, m_b_ada, m_w_in, m_lru_conv_w, m_lru_conv_b, m_lru_wa, m_lru_ba, m_lru_wx, m_lru_bx, m_lru_lambda, m_hg_lb_logits, m_hg_norm_w, m_ssd_conv_w, m_ssd_conv_b, m_ssd_dt_bias, m_ssd_a_log, m_ssd_d, m_ssd_norm_w, m_w_out, m_final_norm_w, v_norm_w, v_w_ada, v_b_ada, v_w_in, v_lru_conv_w, v_lru_conv_b, v_lru_wa, v_lru_ba, v_lru_wx, v_lru_bx, v_lru_lambda, v_hg_lb_logits, v_hg_norm_w, v_ssd_conv_w, v_ssd_conv_b, v_ssd_dt_bias, v_ssd_a_log, v_ssd_d, v_ssd_norm_w, v_w_out, v_final_norm_w)


def _step(*args):
    a = dict(zip(INPUTS, args, strict=True))
    me = 4 * lax.axis_index("x") + 2 * lax.axis_index("y") + lax.axis_index("c")
    x, target = a["x"][0], a["loss_target"][0]

    c_all = _all_gather(a["c"], "gather_c")[:, 0, :]
    b_cols = lax.dynamic_slice_in_dim(a["b_ada"], me * ADA_COLS, ADA_COLS, axis=1)[:, None, :]
    mod_parts = _all_gather(_ada_fwd(c_all, a["w_ada"], b_cols), "gather_mod")
    mod = lax.dynamic_index_in_dim(mod_parts, me, axis=2, keepdims=False)
    mod = mod.transpose(1, 0, 2).reshape(DEPTH, 3 * D_MODEL)

    w = {n: a[n] for n in SMALL}

    w_in_b, w_out_b = a["w_in"].astype(BF16), a["w_out"].astype(BF16)
    conv_own = jnp.concatenate([a["lru_conv_w"], a["ssd_conv_w"]], axis=-1)
    cols, rows_out = N_IN // N_DEV, D_INNER // N_DEV

    def gather_start(l, after):
        srcs = [w_in_b[l], w_out_b[l]] + ([conv_own] if l == 0 else [])
        lands = [lax.empty((N_DEV,) + s.shape, s.dtype) for s in srcs]
        return _exchange_start(f"gather_start_{l}", srcs, lands, "chip" if l == 0 else "gather", after=after)

    def gather_pass(name, st, after):
        landed = _exchange_wait(name + "_wait", st, after)
        st2 = _exchange_start(name + "_pass", st["srcs"], landed, "pass")
        return _exchange_wait(name + "_passed", st2, after)

    gathers = {0: gather_start(0, mod)}

    def fetch(l, x_l):
        if l == 0:
            landed = gather_pass("gather_0", gathers[0], x_l)
        else:
            landed = _exchange_wait(f"gather_wait_{l}", gathers[l], x_l)
        land_out = lax.dynamic_update_index_in_dim(landed[1], w_out_b[l], me, 0)
        if l == 0:
            conv = lax.dynamic_update_index_in_dim(landed[2], conv_own, me, 0).transpose(1, 2, 0, 3)
            w["lru_conv_w"] = conv[..., :64].reshape(DEPTH, 4, LRU_W)
            w["ssd_conv_w"] = conv[..., 64:].reshape(DEPTH, 4, SSD_CONV)
        token = 0.0
        if l + 1 < DEPTH:
            gathers[l + 1] = gather_start(l + 1, land_out)
            token = gathers[l + 1]["token"]
        return _relayout_in(landed[0], w_in_b[l]), land_out.reshape(D_INNER, D_MODEL), token

    scatters = {"w_in": {}, "w_out": {}}
    lands = {"w_in": lax.empty((N_DEV, DEPTH, D_MODEL, cols), BF16),
             "w_out": lax.empty((N_DEV, DEPTH, rows_out, D_MODEL), BF16)}
    own = {"w_in": [None] * DEPTH, "w_out": [None] * DEPTH}

    def emit(l, name, grad):
        grad = _relayout_grad(grad) if name == "w_in" else grad.reshape(N_DEV, rows_out, D_MODEL)
        own[name][l] = lax.dynamic_index_in_dim(grad, me, 0, keepdims=False)
        st = _exchange_start(f"scatter_start_{name}_{l}", [grad], [lands[name]], "scatter", layer=l)
        scatters[name][l] = st
        lands[name] = st["lands"][0]
        return st["token"]

    loss, dx, dmod, g = _local_step(x, mod, target, w, fetch, emit)
    loss = lax.psum(loss, ("x", "y", "c"))

    def sharded(name, parts, own=None):
        return _adamw(parts, a[name], a["m_" + name], a["v_" + name], "adamw_" + name, own=own)

    g["b_ada"] = dmod
    small_own = _flatten_small(g)
    small_st = _exchange_start("gather_small", [small_own], [lax.empty((N_DEV,) + small_own.shape, F32)], "chip",
                               after=dx)
    big = {}
    after = small_st["token"] + dx[0:8, 0:LANE]
    for name in ("w_out", "w_in"):
        for l in reversed(range(DEPTH)):
            scatters[name][l]["lands"] = [lands[name]]
            lands[name] = _exchange_wait(f"scatter_wait_{name}_{l}", scatters[name][l], after)[0]
        big[name] = sharded(name, lands[name], jnp.stack(own[name]))
        after = big[name][1]
    small = gather_pass("gather_small", small_st, after)[0]
    outs = _adamw(small[:, None], *[_flatten_small(a, p)[None] for p in ("", "m_", "v_")], "adamw_small",
                  own=small_own[None])
    res = [_split_small(o[0], a) for o in outs]

    off = _small_rows(a)[0]["b_ada"][0]
    dmod_all = lax.dynamic_update_index_in_dim(small[:, off:off + DEPTH * 3 * D_MODEL // SMALL_ROW],
                                               dmod.reshape(-1, SMALL_ROW), me, 0)
    dmod_all = dmod_all.reshape(N_DEV, DEPTH, 3 * D_MODEL).transpose(1, 0, 2)
    dmod_cols = lax.dynamic_slice_in_dim(dmod_all, me * ADA_COLS, ADA_COLS, axis=2)
    dmod_pad = jnp.pad(dmod_cols, ((0, 0), (0, LANE - N_DEV), (0, 0)))
    ct_pad = jnp.pad(c_all.T, ((0, 0), (0, LANE - N_DEV)))
    big["w_ada"] = sharded("w_ada", _ada_bwd(ct_pad, dmod_pad)[None])
    g_conv = jnp.concatenate([g["lru_conv_w"].reshape(DEPTH, 4, N_DEV, 64), g["ssd_conv_w"].reshape(DEPTH, 4, N_DEV, 192)],
                             axis=-1).transpose(2, 0, 1, 3)
    conv_parts = _all_to_all(g_conv, "scatter_conv")
    big["lru_conv_w"] = sharded("lru_conv_w", conv_parts[..., :64])
    big["ssd_conv_w"] = sharded("ssd_conv_w", conv_parts[..., 64:])

    out = [loss, dx[None]]
    for k in range(4):
        out += [big[n][k] if n in big else res[k][n] for n in WEIGHTS]
    return tuple(out)
```

```python
import functools

import numpy as np
import jax
import jax.numpy as jnp
from jax import lax
from jax.experimental import pallas as pl
from jax.experimental.pallas import tpu as pltpu

F32 = jnp.float32
BF16 = jnp.bfloat16
SDS = jax.ShapeDtypeStruct

N_DEV = 8
DEPTH = 4
D_MODEL = 1024
D_INNER = 2048
EPS = 1e-6
LRU_W = 512
LRU_C = 8.0
HG_W = 512
HG_CHUNK = 64
HG_HEADS = 4
SSD_W = 1024
SSD_HEADS = 16
SSD_P = 64
SSD_N = 128
SSD_CHUNK = 128
SSD_CONV = 1536
N_IN = 5648
N_PAD = 5760
OFF_HG = 0
OFF_LRU = 2048
OFF_XBC = 3072
OFF_Z = 4608
LANE = 128
VMEM_LIMIT = 56 * 1024 * 1024
NEG = -1e30

ADAM_LR = 0.001
ADAM_B1 = 0.9
ADAM_B2 = 0.999
ADAM_EPS = 1e-08
ADAM_WD = 0.01
ADAM_STEP = 10


def _cp(sem=None):
    return pltpu.CompilerParams(dimension_semantics=sem, vmem_limit_bytes=VMEM_LIMIT)


def _dg(a, b, ca, cb):
    return lax.dot_general(a, b, (((ca,), (cb,)), ((), ())), preferred_element_type=F32)


def _mm(a, b):
    return _dg(a, b, 1, 0)


def _mm_nt(a, b):
    return _dg(a, b, 1, 1)


def _mm_tn(a, b):
    return _dg(a, b, 0, 0)


def _bf(x):
    return x.astype(BF16)


def _split3(x):
    hi = x.astype(BF16)
    r = x - hi.astype(F32)
    mid = r.astype(BF16)
    lo = (r - mid.astype(F32)).astype(BF16)
    return hi, mid, lo


def _sel_r(x, m):
    hi, mid, lo = _split3(x)
    return _mm(hi, m) + _mm(mid, m) + _mm(lo, m)


def _sel_l(m, x):
    hi, mid, lo = _split3(x)
    return _mm(m, hi) + _mm(m, mid) + _mm(m, lo)


def _sel_l2(m, x):
    hi = x.astype(BF16)
    lo = (x - hi.astype(F32)).astype(BF16)
    return _mm(m, hi) + _mm(m, lo)


def _sel_tn(x, m):
    hi, mid, lo = _split3(x)
    return _mm_tn(hi, m) + _mm_tn(mid, m) + _mm_tn(lo, m)


def _sigmoid(x):
    return 1.0 / (1.0 + jnp.exp(-x))


def _silu(x):
    return x * _sigmoid(x)


def _dsilu(x):
    s = _sigmoid(x)
    return s * (1.0 + x * (1.0 - s))


def _softplus(x):
    return jnp.maximum(x, 0.0) + jnp.log(1.0 + jnp.exp(-jnp.abs(x)))


def _expm1(z):
    series = z * (1.0 + z * (1.0 / 2) * (1.0 + z * (1.0 / 3) * (1.0 + z * (1.0 / 4) * (
        1.0 + z * (1.0 / 5) * (1.0 + z * (1.0 / 6) * (1.0 + z * (1.0 / 7)))))))
    return jnp.where(jnp.abs(z) < 0.3, series, jnp.exp(z) - 1.0)


def _iota(shape, dim):
    return lax.broadcasted_iota(jnp.int32, shape, dim)


def _last_row(x, rows):
    return jnp.sum(jnp.where(rows == x.shape[0] - 1, x, 0.0), axis=0, keepdims=True)


def _shift_down(x, d, rows, fill=0.0):
    return jnp.where(rows >= d, pltpu.roll(x, d, 0), fill)


def _shift_up(x, d, rows, fill=0.0):
    n = x.shape[0]
    return jnp.where(rows < n - d, pltpu.roll(x, n - d, 0), fill)


def _conv_fwd(x, cw_ref, cb_ref, rows):
    out = cb_ref[...] + cw_ref[pl.ds(3, 1), :] * x
    for k in range(3):
        out = out + cw_ref[pl.ds(k, 1), :] * _shift_down(x, 3 - k, rows)
    return out


def _conv_bwd(x, dco, cw_ref, rows):
    dx = cw_ref[pl.ds(3, 1), :] * dco
    dws = []
    for k in range(3):
        dx = dx + cw_ref[pl.ds(k, 1), :] * _shift_up(dco, 3 - k, rows)
        dws.append(jnp.sum(dco * _shift_down(x, 3 - k, rows), axis=0, keepdims=True))
    dws.append(jnp.sum(dco * x, axis=0, keepdims=True))
    return dx, dws, jnp.sum(dco, axis=0, keepdims=True)


def _vec(n):
    return pl.BlockSpec((1, n), lambda *_: (0, 0))


def _full(shape):
    nd = len(shape)
    return pl.BlockSpec(shape, lambda *_: (0,) * nd)


def _inproj_fwd(x, nw, scale, shift, w):
    S = x.shape[0]
    tm, tn = min(512, S), 640

    def body(x_ref, nw_ref, sc_ref, sh_ref, w_ref, u_ref, h_ref):
        @pl.when(pl.program_id(1) == 0)
        def _():
            xv = x_ref[...]
            inv = lax.rsqrt(jnp.mean(xv * xv, axis=-1, keepdims=True) + EPS)
            h = (xv * inv) * nw_ref[...] * (1.0 + sc_ref[...]) + sh_ref[...]
            h_ref[...] = h.astype(BF16)

        u_ref[...] = _mm(h_ref[...], w_ref[...])

    return pl.pallas_call(
        body, name="inproj_fwd", grid=(S // tm, N_PAD // tn),
        in_specs=[pl.BlockSpec((tm, D_MODEL), lambda i, j: (i, 0)), _vec(D_MODEL), _vec(D_MODEL), _vec(D_MODEL),
                  pl.BlockSpec((D_MODEL, tn), lambda i, j: (0, j))],
        out_specs=[pl.BlockSpec((tm, tn), lambda i, j: (i, j)), pl.BlockSpec((tm, D_MODEL), lambda i, j: (i, 0))],
        out_shape=[SDS((S, N_PAD), F32), SDS((S, D_MODEL), BF16)],
        compiler_params=_cp(("parallel", "arbitrary")),
    )(x, nw, scale, shift, w)


def _inproj_bwd_x(du, w, x, nw, scale, dxn):
    S = x.shape[0]
    tm, tk = min(512, S), 640
    nk = N_PAD // tk

    def body(du_ref, w_ref, x_ref, nw_ref, sc_ref, dxn_ref, dx_ref, red_ref, acc):
        i, k = pl.program_id(0), pl.program_id(1)

        @pl.when(k == 0)
        def _():
            acc[...] = jnp.zeros_like(acc)

        @pl.when((i == 0) & (k == 0))
        def _():
            red_ref[...] = jnp.zeros_like(red_ref)

        acc[...] += _mm_nt(_bf(du_ref[...]), w_ref[...])

        @pl.when(k == nk - 1)
        def _():
            dh = acc[...]
            xv = x_ref[...]
            inv = lax.rsqrt(jnp.mean(xv * xv, axis=-1, keepdims=True) + EPS)
            xhat = xv * inv
            nwv = nw_ref[...]
            g1 = 1.0 + sc_ref[...]
            dxhat = dh * nwv * g1
            dx = inv * (dxhat - xhat * jnp.mean(dxhat * xhat, axis=-1, keepdims=True))
            dx_ref[...] = dxn_ref[...] + dx
            red_ref[0:1, :] += jnp.sum(dh, axis=0, keepdims=True)
            red_ref[1:2, :] += jnp.sum(dh * xhat * nwv, axis=0, keepdims=True)
            red_ref[2:3, :] += jnp.sum(dh * xhat * g1, axis=0, keepdims=True)

    row = pl.BlockSpec((tm, D_MODEL), lambda i, k: (i, 0))
    return pl.pallas_call(
        body, name="inproj_bwd_x", grid=(S // tm, nk),
        in_specs=[pl.BlockSpec((tm, tk), lambda i, k: (i, k)), pl.BlockSpec((D_MODEL, tk), lambda i, k: (0, k)),
                  row, _vec(D_MODEL), _vec(D_MODEL), row],
        out_specs=[row, pl.BlockSpec((8, D_MODEL), lambda i, k: (0, 0))],
        out_shape=[SDS((S, D_MODEL), F32), SDS((8, D_MODEL), F32)],
        scratch_shapes=[pltpu.VMEM((tm, D_MODEL), F32)],
        compiler_params=_cp(("arbitrary", "arbitrary")),
    )(du, w, x, nw, scale, dxn)


def _inproj_bwd_w(h, du):
    S = h.shape[0]
    tn = 640

    def body(h_ref, du_ref, gw_ref):
        gw_ref[...] = _mm_tn(h_ref[...], _bf(du_ref[...]))

    return pl.pallas_call(
        body, name="inproj_bwd_w", grid=(N_PAD // tn,),
        in_specs=[_full((S, D_MODEL)), pl.BlockSpec((S, tn), lambda j: (0, j))],
        out_specs=pl.BlockSpec((D_MODEL, tn), lambda j: (0, j)),
        out_shape=SDS((D_MODEL, N_PAD), F32),
        compiler_params=_cp(("parallel",)),
    )(h, du)


def _scan_block(a, b, rows):
    d = 1
    while d < a.shape[0]:
        a_s = _shift_down(a, d, rows, 1.0)
        b_s = _shift_down(b, d, rows, 0.0)
        b = a * b_s + b
        a = a * a_s
        d *= 2
    return a, b


def _rscan_block(c, g, rows):
    d = 1
    while d < c.shape[0]:
        c_s = _shift_up(c, d, rows, 1.0)
        g_s = _shift_up(g, d, rows, 0.0)
        g = g + c * g_s
        c = c * c_s
        d *= 2
    return c, g


def _lru_gates(xa, wa_ref, ba_ref, wx_ref, bx_ref, lam_ref):
    sp = _softplus(-lam_ref[...])
    xb = _bf(xa)
    r = _sigmoid(_mm(xb, wa_ref[...]) + ba_ref[...])
    ig = _sigmoid(_mm(xb, wx_ref[...]) + bx_ref[...])
    la = -LRU_C * r * sp
    a = jnp.exp(la)
    mult = jnp.sqrt(-_expm1(2.0 * la))
    return sp, r, ig, la, a, mult


def _lru_specs(S):
    t128 = pl.BlockSpec((1, LANE), lambda t: (0, t))
    return [pl.BlockSpec((S, 2 * LANE), lambda t: (0, OFF_LRU // (2 * LANE) + t)),
            pl.BlockSpec((4, LANE), lambda t: (0, t)), t128,
            pl.BlockSpec((None, LANE, LANE), lambda t: (t, 0, 0)), t128,
            pl.BlockSpec((None, LANE, LANE), lambda t: (t, 0, 0)), t128, t128]


def _lru_fwd(u, cw, cb, wa, ba, wx, bx, lam, ycat):
    S = u.shape[0]
    tb = min(256, S)

    def body(u_ref, cw_ref, cb_ref, wa_ref, ba_ref, wx_ref, bx_ref, lam_ref, ycat_in, ycat_ref, h_ref, a_scr, b_scr):
        del ycat_in
        rows = _iota((S, LANE), 0)
        xa = _conv_fwd(u_ref[:, 0:LANE], cw_ref, cb_ref, rows)
        _, _, ig, _, a, mult = _lru_gates(xa, wa_ref, ba_ref, wx_ref, bx_ref, lam_ref)
        a_scr[...] = a
        b_scr[...] = mult * (ig * xa)
        rows_b = _iota((tb, LANE), 0)

        def blk(j, hprev):
            sl = pl.ds(pl.multiple_of(j * tb, tb), tb)
            acum, hloc = _scan_block(a_scr[sl, :], b_scr[sl, :], rows_b)
            hf = hloc + acum * hprev
            h_ref[sl, :] = hf
            return _last_row(hf, rows_b)

        lax.fori_loop(0, S // tb, blk, jnp.zeros((1, LANE), F32))
        ycat_ref[...] = _bf(h_ref[...] * _silu(u_ref[:, LANE:2 * LANE]))

    col = pl.BlockSpec((S, LANE), lambda t: (0, t))
    return pl.pallas_call(
        body, name="lru_fwd", grid=(LRU_W // LANE,),
        in_specs=_lru_specs(S) + [pl.BlockSpec(memory_space=pl.ANY)],
        out_specs=[col, col],
        out_shape=[SDS((S, D_INNER), BF16), SDS((S,LRU_W), F32)],
        scratch_shapes=[pltpu.VMEM((S, LANE), F32), pltpu.VMEM((S, LANE), F32)],
        input_output_aliases={8: 0},
        compiler_params=_cp(("parallel",)),
    )(u, cw, cb, wa, ba, wx, bx, lam, ycat)


def _lru_bwd(u, cw, cb, wa, ba, wx, bx, lam, h_lru, dycat, du):
    S = u.shape[0]
    tb = min(256, S)

    def body(u_ref, cw_ref, cb_ref, wa_ref, ba_ref, wx_ref, bx_ref, lam_ref, h_ref, dy_ref, du_in,
             du_ref, red_ref, gwa_ref, gwx_ref, c_scr, g_scr, l_scr):
        del du_in
        rows = _iota((S, LANE), 0)
        ax = u_ref[:, 0:LANE]
        ag = u_ref[:, LANE:2 * LANE]
        xa = _conv_fwd(ax, cw_ref, cb_ref, rows)
        sp, r, ig, la, a, mult = _lru_gates(xa, wa_ref, ba_ref, wx_ref, bx_ref, lam_ref)
        h = h_ref[...]
        dy = dy_ref[...]
        du_ref[:, LANE:2 * LANE] = _bf(dy * h * _dsilu(ag))
        c_scr[...] = _shift_up(a, 1, rows, 0.0)
        g_scr[...] = dy * _silu(ag)
        rows_b = _iota((tb, LANE), 0)
        nb = S // tb

        def blk(jj, lnext):
            j = nb - 1 - jj
            sl = pl.ds(pl.multiple_of(j * tb, tb), tb)
            ccum, lloc = _rscan_block(c_scr[sl, :], g_scr[sl, :], rows_b)
            lam_t = lloc + ccum * lnext
            l_scr[sl, :] = lam_t
            return jnp.sum(jnp.where(rows_b == 0, lam_t, 0.0), axis=0, keepdims=True)

        lax.fori_loop(0, nb, blk, jnp.zeros((1, LANE), F32))
        db = l_scr[...]
        da = db * _shift_down(h, 1, rows)
        dmult = db * ig * xa
        dig = db * mult * xa
        dxa = db * mult * ig
        dla = da * a - dmult * (a * a) / mult
        dr = -LRU_C * sp * dla
        dsp = jnp.sum(-LRU_C * r * dla, axis=0, keepdims=True)
        dlam = -dsp * _sigmoid(-lam_ref[...])
        dzr = dr * r * (1.0 - r)
        dzi = dig * ig * (1.0 - ig)
        dzr_b, dzi_b, xa_b = _bf(dzr), _bf(dzi), _bf(xa)
        dxa = dxa + _mm_nt(dzr_b, wa_ref[...]) + _mm_nt(dzi_b, wx_ref[...])
        gwa_ref[...] = _mm_tn(xa_b, dzr_b)
        gwx_ref[...] = _mm_tn(xa_b, dzi_b)
        dax, dws, dcb = _conv_bwd(ax, dxa, cw_ref, rows)
        du_ref[:, 0:LANE] = _bf(dax)
        parts = dws + [dcb, jnp.sum(dzr, axis=0, keepdims=True), jnp.sum(dzi, axis=0, keepdims=True), dlam]
        for n, p in enumerate(parts):
            red_ref[pl.ds(n, 1), :] = p

    col = pl.BlockSpec((S, LANE), lambda t: (0, t))
    gw = pl.BlockSpec((None, LANE, LANE), lambda t: (t, 0, 0))
    return pl.pallas_call(
        body, name="lru_bwd", grid=(LRU_W // LANE,),
        in_specs=_lru_specs(S) + [col, col, pl.BlockSpec(memory_space=pl.ANY)],
        out_specs=[pl.BlockSpec((S, 2 * LANE), lambda t: (0, OFF_LRU // (2 * LANE) + t)),
                   pl.BlockSpec((8, LANE), lambda t: (0, t)), gw, gw],
        out_shape=[SDS((S, N_PAD), BF16), SDS((8, LRU_W), F32), SDS((4, LANE, LANE), F32), SDS((4, LANE, LANE), F32)],
        scratch_shapes=[pltpu.VMEM((S, LANE), F32)] * 3,
        input_output_aliases={10: 0},
        compiler_params=_cp(("parallel",)),
    )(u, cw, cb, wa, ba, wx, bx, lam, h_lru, dycat, du)


HG_LEVELS = 6


def _hg_consts():
    C = HG_CHUNK
    t = np.arange(C)[:, None]
    r = np.arange(C)[None, :]
    mats = []
    for side in ("q", "k"):
        for l in range(HG_LEVELS):
            b = 1 << l
            upper = (t % (2 * b)) >= b
            anchor = (t // (2 * b)) * 2 * b + b - 1
            if side == "q":
                mats.append(upper & (r > anchor) & (r <= t))
            else:
                mats.append((~upper) & (r > t) & (r <= anchor))
    mats.append(r <= t)
    mats.append(r > t)
    return np.concatenate(mats, 0).astype(np.float32)


def _hg_factors(hf, lb, mall):
    s = _sigmoid(hf)
    f = lb + (1.0 - lb) * s
    lf = jnp.log(f)
    k = (1.0 - lb) * _sigmoid(-hf)
    e = jnp.exp(_sel_l(mall, lf))
    C = HG_CHUNK
    eq = [e[l * C:(l + 1) * C] for l in range(HG_LEVELS)]
    ek = [e[(HG_LEVELS + l) * C:(HG_LEVELS + l + 1) * C] for l in range(HG_LEVELS)]
    ecum = e[2 * HG_LEVELS * C:(2 * HG_LEVELS + 1) * C]
    erem = e[(2 * HG_LEVELS + 1) * C:(2 * HG_LEVELS + 2) * C]
    return s, f, k, eq, ek, ecum, erem


def _hg_masks():
    C = HG_CHUNK
    ri, ci = _iota((C, C), 0), _iota((C, C), 1)
    rr = _iota((C, LANE), 0)
    gm = [(lax.shift_right_logical(ri, l + 1) == lax.shift_right_logical(ci, l + 1)).astype(F32)
          for l in range(HG_LEVELS)]
    up = [(lax.shift_right_logical(rr, l) & 1) == 1 for l in range(HG_LEVELS)]
    eye = (ri == ci).astype(F32)
    return gm, up, eye, rr


def _hg_scores(qh, kh, eq, ek, sl, gm, up, eye):
    qs, ks = [], []
    p = _mm_nt(_bf(qh), _bf(kh)) * eye
    for l in range(HG_LEVELS):
        ql = jnp.where(up[l], qh * eq[l][:, sl], 0.0)
        kl = jnp.where(up[l], 0.0, kh * ek[l][:, sl])
        p = p + _mm_nt(_bf(ql), _bf(kl)) * gm[l]
        qs.append(ql)
        ks.append(kl)
    return p, qs, ks


def _hg_fwd(u, lb, nw, mall, ycat):
    S = u.shape[0]
    C = HG_CHUNK
    n = S // C

    def body(u_ref, lb_ref, nw_ref, mall_ref, ycat_in, ycat_ref, o_ref, st_ref, st):
        del ycat_in

        @pl.when(pl.program_id(0) == 0)
        def _():
            st[...] = jnp.zeros_like(st)

        q = _silu(u_ref[:, 0:512])
        v = u_ref[:, 1024:1536]
        _, _, k, eq, ek, ecum, erem = _hg_factors(u_ref[:, 512:1024], lb_ref[...], mall_ref[...])
        gm, up, eye, rr = _hg_masks()
        for h in range(HG_HEADS):
            sl = slice(h * LANE, (h + 1) * LANE)
            qh, kh, vh = q[:, sl], k[:, sl], _bf(v[:, sl])
            p, _, _ = _hg_scores(qh, kh, eq, ek, sl, gm, up, eye)
            sth = st[h]
            st_ref[h] = sth
            o_ref[:, sl] = _mm(_bf(p), vh) + _mm_nt(_bf(qh * ecum[:, sl]), _bf(sth))
            st[h] = sth * _last_row(ecum[:, sl], rr) + _mm_tn(vh, _bf(kh * erem[:, sl]))
        o = o_ref[...]
        inv = lax.rsqrt(jnp.mean(o * o, axis=-1, keepdims=True) + EPS)
        ycat_ref[...] = _bf((o * inv) * nw_ref[...] * _silu(u_ref[:, 1536:2048]))

    return pl.pallas_call(
        body, name="hg_fwd", grid=(n,),
        in_specs=[pl.BlockSpec((C, 2048), lambda i: (i, 0)), _vec(HG_W), _vec(HG_W), _full(mall.shape),
                  pl.BlockSpec(memory_space=pl.ANY)],
        out_specs=[pl.BlockSpec((C, HG_W), lambda i: (i, 1)), pl.BlockSpec((C, HG_W), lambda i: (i, 0)),
                   pl.BlockSpec((None, HG_HEADS, LANE, LANE), lambda i: (i, 0, 0, 0))],
        out_shape=[SDS((S, D_INNER), BF16), SDS((S,HG_W), F32), SDS((n, HG_HEADS, LANE, LANE), F32)],
        scratch_shapes=[pltpu.VMEM((HG_HEADS, LANE, LANE), F32)],
        input_output_aliases={4: 0},
        compiler_params=_cp(("arbitrary",)),
    )(u, lb, nw, mall, ycat)


def _hg_bwd(u, lb, nw, mall, mall_t, o_b, states, dycat, du):
    S = u.shape[0]
    C = HG_CHUNK
    n = S // C
    L2 = 2 * HG_LEVELS

    def body(u_ref, lb_ref, nw_ref, mall_ref, mallt_ref, o_ref, st_ref, dy_ref, du_in, du_ref, red_ref,
             dst, dlast_s, dq_s, dk_s, dex):
        del du_in

        @pl.when(pl.program_id(0) == 0)
        def _():
            dst[...] = jnp.zeros_like(dst)
            red_ref[...] = jnp.zeros_like(red_ref)

        lb = lb_ref[...]
        hq, hf, hg = u_ref[:, 0:512], u_ref[:, 512:1024], u_ref[:, 1536:2048]
        q = _silu(hq)
        v = u_ref[:, 1024:1536]
        s, f, k, eq, ek, ecum, erem = _hg_factors(hf, lb, mall_ref[...])
        gm, up, eye, rr = _hg_masks()
        o = o_ref[...]
        dy = dy_ref[...]
        inv = lax.rsqrt(jnp.mean(o * o, axis=-1, keepdims=True) + EPS)
        ohat = o * inv
        nwv = nw_ref[...]
        du_ref[:, 1536:2048] = _bf(dy * ohat * nwv * _dsilu(hg))
        dn = dy * _silu(hg)
        red_ref[0:1, :] += jnp.sum(dn * ohat, axis=0, keepdims=True)
        dohat = dn * nwv
        do = inv * (dohat - ohat * jnp.mean(dohat * ohat, axis=-1, keepdims=True))
        for h in range(HG_HEADS):
            sl = slice(h * LANE, (h + 1) * LANE)
            qh, kh, vh, doh = q[:, sl], k[:, sl], _bf(v[:, sl]), _bf(do[:, sl])
            p, qs, ks = _hg_scores(qh, kh, eq, ek, sl, gm, up, eye)
            st_f = st_ref[h]
            sth = _bf(st_f)
            dsth = dst[h]
            dsth_b = _bf(dsth)
            qt = qh * ecum[:, sl]
            kt = kh * erem[:, sl]
            elast = _last_row(ecum[:, sl], rr)
            dp = _mm_nt(doh, vh)
            du_ref[:, 1024 + h * LANE:1024 + (h + 1) * LANE] = _bf(_mm_tn(_bf(p), doh) + _mm_nt(_bf(kt), dsth_b))
            dpe = _bf(dp * eye)
            dqt = _mm(doh, sth)
            dkt = _mm(vh, dsth_b)
            dq = dqt * ecum[:, sl] + _mm(dpe, _bf(kh))
            dk = dkt * erem[:, sl] + _mm_tn(dpe, _bf(qh))
            dex[L2 * C:(L2 + 1) * C, sl] = dqt * qt
            dex[(L2 + 1) * C:(L2 + 2) * C, sl] = dkt * kt
            for l in range(HG_LEVELS):
                dpl = _bf(dp * gm[l])
                dql = _mm(dpl, _bf(ks[l]))
                dkl = _mm_tn(dpl, _bf(qs[l]))
                dq = dq + jnp.where(up[l], dql * eq[l][:, sl], 0.0)
                dk = dk + jnp.where(up[l], 0.0, dkl * ek[l][:, sl])
                dex[l * C:(l + 1) * C, sl] = dql * qs[l]
                dex[(HG_LEVELS + l) * C:(HG_LEVELS + l + 1) * C, sl] = dkl * ks[l]
            dlast_s[:, sl] = jnp.sum(dsth * st_f, axis=0, keepdims=True) * elast
            dst[h] = dsth * elast + _mm_tn(doh, _bf(qt))
            dq_s[:, sl] = dq
            dk_s[:, sl] = dk
        dq = dq_s[...]
        dk = dk_s[...]
        dlf = _sel_l2(mallt_ref[...], dex[...]) + dlast_s[...]
        du_ref[:, 0:512] = _bf(dq * _dsilu(hq))
        t = (1.0 - s) * (dlf / f - dk)
        du_ref[:, 512:1024] = _bf((1.0 - lb) * s * t)
        red_ref[1:2, :] += jnp.sum(t, axis=0, keepdims=True)

    rev = lambda i: (n - 1 - i, 0)
    return pl.pallas_call(
        body, name="hg_bwd", grid=(n,),
        in_specs=[pl.BlockSpec((C, 2048), rev), _vec(HG_W), _vec(HG_W), _full(mall.shape), _full(mall_t.shape),
                  pl.BlockSpec((C, HG_W), rev),
                  pl.BlockSpec((None, HG_HEADS, LANE, LANE), lambda i: (n - 1 - i, 0, 0, 0)),
                  pl.BlockSpec((C, HG_W), lambda i: (n - 1 - i, 1)), pl.BlockSpec(memory_space=pl.ANY)],
        out_specs=[pl.BlockSpec((C, 2048), rev), pl.BlockSpec((8, HG_W), lambda i: (0, 0))],
        out_shape=[SDS((S, N_PAD), BF16), SDS((8, HG_W), F32)],
        scratch_shapes=[pltpu.VMEM((HG_HEADS, LANE, LANE), F32), pltpu.VMEM((1, HG_W), F32),
                        pltpu.VMEM((C, HG_W), F32), pltpu.VMEM((C, HG_W), F32), pltpu.VMEM(((L2 + 2) * C, HG_W), F32)],
        input_output_aliases={8: 0},
        compiler_params=_cp(("arbitrary",)),
    )(u, lb, nw, mall, mall_t, o_b, states, dycat, du)


def _ssdconv_fwd(u, cw, cb):
    S = u.shape[0]

    def body(u_ref, cw_ref, cb_ref, out_ref):
        rows = _iota((S, LANE), 0)
        out_ref[...] = _silu(_conv_fwd(u_ref[...], cw_ref, cb_ref, rows))

    return pl.pallas_call(
        body, name="ssdconv_fwd", grid=(SSD_CONV // LANE,),
        in_specs=[pl.BlockSpec((S, LANE), lambda t: (0, OFF_XBC // LANE + t)), pl.BlockSpec((4, LANE), lambda t: (0, t)),
                  pl.BlockSpec((1, LANE), lambda t: (0, t))],
        out_specs=pl.BlockSpec((S, LANE), lambda t: (0, t)),
        out_shape=SDS((S, SSD_CONV), F32),
        compiler_params=_cp(("parallel",)),
    )(u, cw, cb)


def _ssdconv_bwd(u, cw, cb, dxbc, du):
    S = u.shape[0]

    def body(u_ref, cw_ref, cb_ref, d_ref, du_in, du_ref, red_ref):
        del du_in
        rows = _iota((S, LANE), 0)
        x = u_ref[...]
        dco = d_ref[...] * _dsilu(_conv_fwd(x, cw_ref, cb_ref, rows))
        dx, dws, dcb = _conv_bwd(x, dco, cw_ref, rows)
        du_ref[...] = _bf(dx)
        for n, p in enumerate(dws + [dcb]):
            red_ref[pl.ds(n, 1), :] = p
        red_ref[pl.ds(5, 3), :] = jnp.zeros((3, LANE), F32)

    ucol = pl.BlockSpec((S, LANE), lambda t: (0, OFF_XBC // LANE + t))
    return pl.pallas_call(
        body, name="ssdconv_bwd", grid=(SSD_CONV // LANE,),
        in_specs=[ucol, pl.BlockSpec((4, LANE), lambda t: (0, t)), pl.BlockSpec((1, LANE), lambda t: (0, t)),
                  pl.BlockSpec((S, LANE), lambda t: (0, t)), pl.BlockSpec(memory_space=pl.ANY)],
        out_specs=[ucol, pl.BlockSpec((8, LANE), lambda t: (0, t))],
        out_shape=[SDS((S, N_PAD), BF16), SDS((8, SSD_CONV), F32)],
        input_output_aliases={4: 0},
        compiler_params=_cp(("parallel",)),
    )(u, cw, cb, dxbc, du)


def _ssd_consts():
    e64 = np.zeros((LANE, SSD_W), np.float32)
    e128 = np.zeros((LANE, SSD_HEADS * LANE), np.float32)
    for h in range(SSD_HEADS):
        e64[h, h * SSD_P:(h + 1) * SSD_P] = 1.0
        e128[h, h * LANE:(h + 1) * LANE] = 1.0
    T = SSD_CHUNK
    tril = (np.arange(T)[None, :] <= np.arange(T)[:, None]).astype(np.float32)
    return e64, e128, tril, tril.T.copy()


def _ssd_common(zdt, bias_ref, alog_ref, tril, e64, e128):
    T = SSD_CHUNK
    lane = _iota((1, LANE), 1)
    a_neg = jnp.where(lane < SSD_HEADS, -jnp.exp(alog_ref[...]), 0.0)
    dtpre = zdt[:, SSD_W:SSD_W + LANE] + bias_ref[...]
    dt = _softplus(dtpre)
    cum = _sel_l(tril, dt * a_neg)
    rowsT = _iota((T, LANE), 0)
    last = _last_row(cum, rowsT)
    ecum_x = _sel_r(jnp.exp(cum), e64)
    erem_x = _sel_r(jnp.exp(last - cum), e64)
    elast_x = _last_row(ecum_x, _iota((T, SSD_W), 0))
    dt_x = _sel_r(dt, e64)
    cum_e = _sel_r(cum, e128)
    return a_neg, dtpre, dt, cum, ecum_x, erem_x, elast_x, dt_x, cum_e


def _ssd_decay(cum_e, cumt_ref, h, causal):
    diff = cum_e[:, h * LANE:(h + 1) * LANE] - cumt_ref[pl.ds(h, 1), :]
    return jnp.exp(jnp.where(causal, diff, NEG))


def _group_norm_fwd(y1, nwv):
    outs, invs = [], []
    for g in range(2):
        seg = y1[:, g * 512:(g + 1) * 512]
        inv = lax.rsqrt(jnp.mean(seg * seg, axis=-1, keepdims=True) + EPS)
        outs.append(seg * inv * nwv[:, g * 512:(g + 1) * 512])
        invs.append(inv)
    return outs, invs


def _ssd_fwd(u, xbc, bias, alog, dskip_x, nw, consts, ycat):
    S = u.shape[0]
    T = SSD_CHUNK
    n = S // T
    e64, e128, tril, _ = consts

    def body(u_ref, xbc_ref, bias_ref, alog_ref, dx_ref, nw_ref, e64_ref, e128_ref, tril_ref, ycat_in,
             ycat_ref, y_ref, st_ref, st, cumt):
        del ycat_in

        @pl.when(pl.program_id(0) == 0)
        def _():
            st[...] = jnp.zeros_like(st)

        zdt = u_ref[...]
        z = zdt[:, 0:SSD_W]
        xs = xbc_ref[:, 0:SSD_W]
        _, _, _, cum, ecum_x, erem_x, elast_x, dt_x, cum_e = _ssd_common(
            zdt, bias_ref, alog_ref, tril_ref[...], e64_ref[...], e128_ref[...])
        cumt[...] = cum.T
        causal = _iota((T, T), 0) >= _iota((T, T), 1)
        lo = _iota((T, LANE), 1) < SSD_P
        xdt = xs * dt_x
        xrem = xdt * erem_x
        st_ref[...] = st[...]
        for g in range(2):
            gs = slice(g * 512, (g + 1) * 512)
            bg = _bf(xbc_ref[:, SSD_W + g * LANE:SSD_W + (g + 1) * LANE])
            cg = _bf(xbc_ref[:, SSD_W + 256 + g * LANE:SSD_W + 256 + (g + 1) * LANE])
            cb = _mm_nt(cg, bg)
            yin = _mm(cg, _bf(st[:, gs])) * ecum_x[:, gs]
            for j in range(4):
                h0 = 8 * g + 2 * j
                cs = slice(h0 * SSD_P, (h0 + 2) * SSD_P)
                xp = xdt[:, cs]
                s0 = _bf(cb * _ssd_decay(cum_e, cumt, h0, causal))
                s1 = _bf(cb * _ssd_decay(cum_e, cumt, h0 + 1, causal))
                y_ref[:, cs] = (_mm(s0, _bf(jnp.where(lo, xp, 0.0))) + _mm(s1, _bf(jnp.where(lo, 0.0, xp)))
                                + yin[:, j * LANE:(j + 1) * LANE])
            st[:, gs] = st[:, gs] * elast_x[:, gs] + _mm_tn(bg, _bf(xrem[:, gs]))
        y1 = (y_ref[...] + dx_ref[...] * xs) * _silu(z)
        outs, _ = _group_norm_fwd(y1, nw_ref[...])
        for g in range(2):
            ycat_ref[:, g * 512:(g + 1) * 512] = _bf(outs[g])

    return pl.pallas_call(
        body, name="ssd_fwd", grid=(n,),
        in_specs=[pl.BlockSpec((T, SSD_W + LANE), lambda i: (i, OFF_Z // (SSD_W + LANE))),
                  pl.BlockSpec((T, SSD_CONV), lambda i: (i, 0)), _vec(LANE), _vec(LANE), _vec(SSD_W), _vec(SSD_W),
                  _full(e64.shape), _full(e128.shape), _full(tril.shape), pl.BlockSpec(memory_space=pl.ANY)],
        out_specs=[pl.BlockSpec((T, SSD_W), lambda i: (i, 1)), pl.BlockSpec((T, SSD_W), lambda i: (i, 0)),
                   pl.BlockSpec((None, SSD_N, SSD_W), lambda i: (i, 0, 0))],
        out_shape=[SDS((S, D_INNER), BF16), SDS((S,SSD_W), F32), SDS((n, SSD_N, SSD_W), F32)],
        scratch_shapes=[pltpu.VMEM((SSD_N, SSD_W), F32), pltpu.VMEM((LANE, T), F32)],
        input_output_aliases={9: 0},
        compiler_params=_cp(("arbitrary",)),
    )(u, xbc, bias, alog, dskip_x, nw, _bfc(e64), _bfc(e128), _bfc(tril), ycat)


def _ssd_bwd(u, xbc, bias, alog, dskip_x, nw, consts, y_ssd, states, dycat, du):
    S = u.shape[0]
    T = SSD_CHUNK
    n = S // T
    e64, e128, tril, triu = consts
    e64t = np.ascontiguousarray(e64.T)

    def body(u_ref, xbc_ref, bias_ref, alog_ref, dx_ref, nw_ref, e64_ref, e64t_ref, e128_ref, tril_ref, triu_ref,
             y_ref, st_ref, dy_ref, du_in, du_ref, dxbc_ref, red_ref, dst, dl_s, cumt, dxdt_s, dy0_s, gb_s, gc_s):
        del du_in

        @pl.when(pl.program_id(0) == 0)
        def _():
            dst[...] = jnp.zeros_like(dst)
            red_ref[...] = jnp.zeros_like(red_ref)

        zdt = u_ref[...]
        z = zdt[:, 0:SSD_W]
        xs = xbc_ref[:, 0:SSD_W]
        e64m = e64_ref[...]
        a_neg, dtpre, dt, cum, ecum_x, erem_x, elast_x, dt_x, cum_e = _ssd_common(
            zdt, bias_ref, alog_ref, tril_ref[...], e64m, e128_ref[...])
        cumt[...] = cum.T
        causal = _iota((T, T), 0) >= _iota((T, T), 1)
        lo = _iota((T, LANE), 1) < SSD_P
        xdt = xs * dt_x
        xrem = xdt * erem_x
        y = y_ref[...]
        dxv = dx_ref[...]
        nwv = nw_ref[...]
        sz = _silu(z)
        y0 = y + dxv * xs
        y1 = y0 * sz
        for g in range(2):
            gs = slice(g * 512, (g + 1) * 512)
            seg = y1[:, gs]
            inv = lax.rsqrt(jnp.mean(seg * seg, axis=-1, keepdims=True) + EPS)
            shat = seg * inv
            dyg = dy_ref[:, gs]
            red_ref[0:1, gs] += jnp.sum(dyg * shat, axis=0, keepdims=True)
            dsh = dyg * nwv[:, gs]
            dy1g = inv * (dsh - shat * jnp.mean(dsh * shat, axis=-1, keepdims=True))
            du_ref[:, gs] = _bf(dy1g * y0[:, gs] * _dsilu(z[:, gs]))
            dy0_s[:, gs] = dy1g * sz[:, gs]
        dy0 = dy0_s[...]
        red_ref[1:2, :] += jnp.sum(dy0 * xs, axis=0, keepdims=True)
        dyin = dy0 * ecum_x
        lane = _iota((T, LANE), 1)
        ones = jnp.ones((T, LANE), BF16)
        dcum = jnp.zeros((T, LANE), F32)

        def row_minus_col(gm):
            hi = _bf(gm)
            lw = _bf(gm - hi.astype(F32))
            return _mm(hi, ones) + _mm(lw, ones) - _mm_tn(hi, ones) - _mm_tn(lw, ones)

        for g in range(2):
            gs = slice(g * 512, (g + 1) * 512)
            bg = _bf(xbc_ref[:, SSD_W + g * LANE:SSD_W + (g + 1) * LANE])
            cg = _bf(xbc_ref[:, SSD_W + 256 + g * LANE:SSD_W + 256 + (g + 1) * LANE])
            cb = _mm_nt(cg, bg)
            dst_f, st_f = dst[:, gs], st_ref[:, gs]
            dstg = _bf(dst_f)
            stg = _bf(st_f)
            dyin_g = _bf(dyin[:, gs])
            xrem_g = _bf(xrem[:, gs])
            dcb = jnp.zeros((T, T), F32)
            dxr = _mm(bg, dstg)
            dxdt_s[:, gs] = dxr * erem_x[:, gs]
            gc_s[:, gs] = dxr * xrem[:, gs]
            gb_s[:, gs] = dyin[:, gs] * _mm(cg, stg)
            dl_s[:, gs] = jnp.sum(dst_f * st_f, axis=0, keepdims=True) * elast_x[:, gs]
            for j in range(4):
                h0 = 8 * g + 2 * j
                cs = slice(h0 * SSD_P, (h0 + 2) * SSD_P)
                xp = xdt[:, cs]
                dyp = dy0[:, cs]
                x_lo, x_hi = _bf(jnp.where(lo, xp, 0.0)), _bf(jnp.where(lo, 0.0, xp))
                d_lo, d_hi = _bf(jnp.where(lo, dyp, 0.0)), _bf(jnp.where(lo, 0.0, dyp))
                s0 = cb * _ssd_decay(cum_e, cumt, h0, causal)
                s1 = cb * _ssd_decay(cum_e, cumt, h0 + 1, causal)
                ds0 = _mm_nt(d_lo, x_lo)
                ds1 = _mm_nt(d_hi, x_hi)
                dcb = dcb + ds0 * _ssd_decay(cum_e, cumt, h0, causal) + ds1 * _ssd_decay(cum_e, cumt, h0 + 1, causal)
                dxdt_s[:, cs] += _mm_tn(_bf(s0), d_lo) + _mm_tn(_bf(s1), d_hi)
                dcum = dcum + jnp.where(lane == h0, row_minus_col(ds0 * s0), 0.0)
                dcum = dcum + jnp.where(lane == h0 + 1, row_minus_col(ds1 * s1), 0.0)
            dcb_b = _bf(dcb)
            dxbc_ref[:, SSD_W + g * LANE:SSD_W + (g + 1) * LANE] = _mm_tn(dcb_b, cg) + _mm_nt(xrem_g, dstg)
            dxbc_ref[:, SSD_W + 256 + g * LANE:SSD_W + 256 + (g + 1) * LANE] = _mm(dcb_b, bg) + _mm_nt(dyin_g, stg)
            dst[:, gs] = dst_f * elast_x[:, gs] + _mm_tn(cg, dyin_g)
        dxdt = dxdt_s[...]
        dxbc_ref[:, 0:SSD_W] = dxdt * dt_x + dy0 * dxv
        e64t = e64t_ref[...]
        hc = _sel_r(gc_s[...], e64t)
        dlast = (jnp.sum(hc, axis=0, keepdims=True)
                 + jnp.max(_sel_r(jnp.broadcast_to(dl_s[...], (8, SSD_W)), e64t), axis=0, keepdims=True))
        dcum = dcum + _sel_r(gb_s[...], e64t) - hc + jnp.where(_iota((T, LANE), 0) == T - 1, dlast, 0.0)
        dda = _sel_l(triu_ref[...], dcum)
        ddt = dda * a_neg + _sel_r(dxdt * xs, e64t)
        ddtpre = ddt * _sigmoid(dtpre)
        du_ref[:, SSD_W:SSD_W + LANE] = _bf(jnp.where(lane < SSD_HEADS, ddtpre, 0.0))
        red_ref[2:3, 0:LANE] += jnp.sum(ddtpre, axis=0, keepdims=True)
        red_ref[3:4, 0:LANE] += jnp.sum(dda * dt, axis=0, keepdims=True)

    rev = lambda i: (n - 1 - i, 0)
    return pl.pallas_call(
        body, name="ssd_bwd", grid=(n,),
        in_specs=[pl.BlockSpec((T, SSD_W + LANE), lambda i: (n - 1 - i, OFF_Z // (SSD_W + LANE))),
                  pl.BlockSpec((T, SSD_CONV), rev), _vec(LANE), _vec(LANE), _vec(SSD_W), _vec(SSD_W),
                  _full(e64.shape), _full(e64t.shape), _full(e128.shape), _full(tril.shape), _full(triu.shape),
                  pl.BlockSpec((T, SSD_W), rev), pl.BlockSpec((None, SSD_N, SSD_W), lambda i: (n - 1 - i, 0, 0)),
                  pl.BlockSpec((T, SSD_W), lambda i: (n - 1 - i, 1)), pl.BlockSpec(memory_space=pl.ANY)],
        out_specs=[pl.BlockSpec((T, SSD_W + LANE), lambda i: (n - 1 - i, OFF_Z // (SSD_W + LANE))),
                   pl.BlockSpec((T, SSD_CONV), rev), pl.BlockSpec((8, SSD_W), lambda i: (0, 0))],
        out_shape=[SDS((S, N_PAD), BF16), SDS((S, SSD_CONV), F32), SDS((8, SSD_W), F32)],
        scratch_shapes=[pltpu.VMEM((SSD_N, SSD_W), F32), pltpu.VMEM((1, SSD_W), F32), pltpu.VMEM((LANE, T), F32)]
        + [pltpu.VMEM((T, SSD_W), F32)] * 4,
        input_output_aliases={14: 0},
        compiler_params=_cp(("arbitrary",)),
    )(u, xbc, bias, alog, dskip_x, nw, _bfc(e64), _bfc(e64t), _bfc(e128), _bfc(tril), _bfc(triu), y_ssd, states, dycat, du)


def _bfc(a):
    return jnp.asarray(a, BF16)


def _outproj_fwd(ycat, wo, x, gate):
    S = x.shape[0]
    tm = min(512, S)

    def body(yc_ref, wo_ref, x_ref, g_ref, xn_ref, y_ref):
        y = _mm(_bf(yc_ref[...]), wo_ref[...])
        y_ref[...] = y
        xn_ref[...] = x_ref[...] + g_ref[...] * y

    row = pl.BlockSpec((tm, D_MODEL), lambda i: (i, 0))
    return pl.pallas_call(
        body, name="outproj_fwd", grid=(S // tm,),
        in_specs=[pl.BlockSpec((tm, D_INNER), lambda i: (i, 0)), _full((D_INNER, D_MODEL)), row, _vec(D_MODEL)],
        out_specs=[row, row],
        out_shape=[SDS((S, D_MODEL), F32), SDS((S, D_MODEL), F32)],
        compiler_params=_cp(("parallel",)),
    )(ycat, wo, x, gate)


def _outproj_bwd(dxn, y, gate, ycat, wo):
    S = dxn.shape[0]
    tm = min(512, S)

    def body(dx_ref, y_ref, g_ref, yc_ref, wo_ref, dyc_ref, gwo_ref, dg_ref, acc):
        @pl.when(pl.program_id(0) == 0)
        def _():
            acc[...] = jnp.zeros_like(acc)
            dg_ref[...] = jnp.zeros_like(dg_ref)

        dxv = dx_ref[...]
        dy = _bf(dxv * g_ref[...])
        dg_ref[0:1, :] += jnp.sum(dxv * y_ref[...], axis=0, keepdims=True)
        dyc_ref[...] = _mm_nt(dy, wo_ref[...])
        acc[...] += _mm_tn(_bf(yc_ref[...]), dy)

        @pl.when(pl.program_id(0) == pl.num_programs(0) - 1)
        def _():
            gwo_ref[...] = acc[...].astype(BF16)

    row = pl.BlockSpec((tm, D_MODEL), lambda i: (i, 0))
    wide = pl.BlockSpec((tm, D_INNER), lambda i: (i, 0))
    return pl.pallas_call(
        body, name="outproj_bwd", grid=(S // tm,),
        in_specs=[row, row, _vec(D_MODEL), wide, _full((D_INNER, D_MODEL))],
        out_specs=[wide, _full((D_INNER, D_MODEL)), _full((8, D_MODEL))],
        out_shape=[SDS((S, D_INNER), F32), SDS((D_INNER, D_MODEL), BF16), SDS((8, D_MODEL), F32)],
        scratch_shapes=[pltpu.VMEM((D_INNER, D_MODEL), F32)],
        compiler_params=_cp(("arbitrary",)),
    )(dxn, y, gate, ycat, wo)


def _loss_head(x, fw, target):
    S = x.shape[0]
    tm = min(512, S)

    def body(x_ref, fw_ref, t_ref, dx_ref, red_ref):
        @pl.when(pl.program_id(0) == 0)
        def _():
            red_ref[...] = jnp.zeros_like(red_ref)

        xv = x_ref[...]
        fwv = fw_ref[...]
        inv = lax.rsqrt(jnp.mean(xv * xv, axis=-1, keepdims=True) + EPS)
        xhat = xv * inv
        err = xhat * fwv - t_ref[...]
        col = jnp.sum(err * err, axis=0, keepdims=True)
        red_ref[1:2, :] += jnp.broadcast_to(jnp.sum(col, axis=1, keepdims=True) * (0.5 / D_MODEL), (1, D_MODEL))
        dy = err * (1.0 / D_MODEL)
        red_ref[0:1, :] += jnp.sum(dy * xhat, axis=0, keepdims=True)
        dxhat = dy * fwv
        dx_ref[...] = inv * (dxhat - xhat * jnp.mean(dxhat * xhat, axis=-1, keepdims=True))

    row = pl.BlockSpec((tm, D_MODEL), lambda i: (i, 0))
    return pl.pallas_call(
        body, name="loss_head", grid=(S // tm,),
        in_specs=[row, _vec(D_MODEL), row],
        out_specs=[row, _full((8, D_MODEL))],
        out_shape=[SDS((S, D_MODEL), F32), SDS((8, D_MODEL), F32)],
        compiler_params=_cp(("arbitrary",)),
    )(x, fw, target)


ADA_COLS = 3 * D_MODEL // N_DEV


def _ada_fwd(c_all, w_ada, b_cols):
    def body(c_ref, w_ref, b_ref, out_ref):
        out_ref[...] = _mm(_bf(_silu(c_ref[...])), _bf(w_ref[...])) + b_ref[...]

    return pl.pallas_call(
        body, name="ada_fwd", grid=(DEPTH,),
        in_specs=[_full((N_DEV, D_MODEL)), pl.BlockSpec((None, D_MODEL, ADA_COLS), lambda l: (l, 0, 0)),
                  pl.BlockSpec((None, 1, ADA_COLS), lambda l: (l, 0, 0))],
        out_specs=pl.BlockSpec((None, N_DEV, ADA_COLS), lambda l: (l, 0, 0)),
        out_shape=SDS((DEPTH, N_DEV, ADA_COLS), F32),
        compiler_params=_cp(("parallel",)),
    )(c_all, w_ada, b_cols)


def _ada_bwd(ct_pad, dmod_pad):
    def body(c_ref, d_ref, out_ref):
        out_ref[...] = _mm(_bf(_silu(c_ref[...])), _bf(d_ref[...]))

    return pl.pallas_call(
        body, name="ada_bwd", grid=(DEPTH,),
        in_specs=[_full((D_MODEL, LANE)), pl.BlockSpec((None, LANE, ADA_COLS), lambda l: (l, 0, 0))],
        out_specs=pl.BlockSpec((None, D_MODEL, ADA_COLS), lambda l: (l, 0, 0)),
        out_shape=SDS((DEPTH, D_MODEL, ADA_COLS), F32),
        compiler_params=_cp(("parallel",)),
    )(ct_pad, dmod_pad)


def _adamw(parts, w, m, v, name, own=None, layers=None, prev=None):
    n, L, R, C = parts.shape
    lo, hi = layers or (0, L)
    tr = R
    while tr * C * 4 > (1 << 20) and tr % 16 == 0:
        tr //= 2
    first = 1 if own is None else 2

    def body(*refs):
        p_ref = refs[0]
        w_ref, m_ref, v_ref = refs[first:first + 3]
        g_ref, d_ref, mo_ref, vo_ref = refs[-4:]

        def part(k):
            if own is None:
                return p_ref[k].astype(F32)
            me = 4 * lax.axis_index("x") + 2 * lax.axis_index("y") + lax.axis_index("c")
            return jnp.where(me == k, refs[1][...], p_ref[k]).astype(F32)

        g = part(0)
        for k in range(1, n):
            g = g + part(k)
        mn = ADAM_B1 * m_ref[...] + (1.0 - ADAM_B1) * g
        vn = ADAM_B2 * v_ref[...] + (1.0 - ADAM_B2) * (g * g)
        m_hat = mn / (1.0 - ADAM_B1 ** ADAM_STEP)
        v_hat = vn / (1.0 - ADAM_B2 ** ADAM_STEP)
        g_ref[...] = g
        d_ref[...] = -ADAM_LR * (m_hat / (jnp.sqrt(v_hat) + ADAM_EPS) + ADAM_WD * w_ref[...])
        mo_ref[...] = mn
        vo_ref[...] = vn

    blk = pl.BlockSpec((None, tr, C), lambda l, i: (lo + l, i, 0))
    n_blk = 3 if own is None else 4
    return pl.pallas_call(
        body, name=name, grid=(hi - lo, R // tr),
        in_specs=[pl.BlockSpec((n, None, tr, C), lambda l, i: (0, lo + l, i, 0))] + [blk] * n_blk
        + ([] if prev is None else [ANY] * 4),
        out_specs=[blk] * 4,
        out_shape=[SDS((L, R, C), F32)] * 4,
        input_output_aliases={} if prev is None else {1 + n_blk + k: k for k in range(4)},
        compiler_params=_cp(("parallel", "parallel")),
    )(parts, *([] if own is None else [own]), w, m, v, *([] if prev is None else prev))


MESH = pl.DeviceIdType.MESH
ANY = pl.BlockSpec(memory_space=pl.ANY)


def _all_gather(v, name):
    def body(v_ref, out_ref, send_sems, recv_sems, local_sem):
        x, y, c = lax.axis_index("x"), lax.axis_index("y"), lax.axis_index("c")
        me, sibling = (x, y, c), (x, y, 1 - c)
        chips = [(1 - x, y), (x, 1 - y), (1 - x, 1 - y)]

        def slot(px, py, pc):
            return out_ref.at[4 * px + 2 * py + pc]

        def copy(k, block, to, src=None):
            return pltpu.make_async_remote_copy(
                src_ref=slot(*block) if src is None else src, dst_ref=slot(*block),
                send_sem=send_sems.at[k], recv_sem=recv_sems.at[k], device_id=to, device_id_type=MESH)

        mine = pltpu.make_async_copy(v_ref, slot(*me), local_sem)
        mine.start()
        first = [copy(0, me, sibling, src=v_ref)]
        first += [copy(1 + j, me, (*chip, c), src=v_ref) for j, chip in enumerate(chips)]
        for cp in first:
            cp.start()
        passed = [copy(4 + j, (*chip, c), sibling) for j, chip in enumerate(chips)]
        for j, chip in enumerate(chips):
            copy(1 + j, (*chip, c), me).wait_recv()
            passed[j].start()
        copy(0, sibling, me).wait_recv()
        for j, chip in enumerate(chips):
            copy(4 + j, (*chip, 1 - c), me).wait_recv()
        for cp in first + passed:
            cp.wait_send()
        mine.wait()

    return pl.pallas_call(
        body, name=name, in_specs=[ANY], out_specs=ANY,
        out_shape=SDS((N_DEV,) + v.shape, v.dtype),
        scratch_shapes=[pltpu.SemaphoreType.DMA((7,)), pltpu.SemaphoreType.DMA((7,)), pltpu.SemaphoreType.DMA],
    )(v)


def _all_to_all(v, name):
    def body(v_ref, out_ref, send_sems, recv_sems, local_sem):
        x, y, c = lax.axis_index("x"), lax.axis_index("y"), lax.axis_index("c")
        mine_idx = 4 * x + 2 * y + c
        mine = pltpu.make_async_copy(v_ref.at[mine_idx], out_ref.at[mine_idx], local_sem)
        mine.start()
        sends, recvs = [], []
        for k in range(1, N_DEV):
            px = 1 - x if k & 4 else x
            py = 1 - y if k & 2 else y
            pc = 1 - c if k & 1 else c
            peer_idx = 4 * px + 2 * py + pc
            sems = dict(send_sem=send_sems.at[k - 1], recv_sem=recv_sems.at[k - 1], device_id=(px, py, pc),
                        device_id_type=MESH)
            sends.append(pltpu.make_async_remote_copy(src_ref=v_ref.at[peer_idx], dst_ref=out_ref.at[mine_idx], **sems))
            recvs.append(pltpu.make_async_remote_copy(src_ref=v_ref.at[peer_idx], dst_ref=out_ref.at[peer_idx], **sems))
        for cp in sends:
            cp.start()
        for cp in recvs:
            cp.wait_recv()
        for cp in sends:
            cp.wait_send()
        mine.wait()

    return pl.pallas_call(
        body, name=name, in_specs=[ANY], out_specs=ANY,
        out_shape=SDS(v.shape, v.dtype),
        scratch_shapes=[pltpu.SemaphoreType.DMA((7,)), pltpu.SemaphoreType.DMA((7,)), pltpu.SemaphoreType.DMA],
    )(v)


HBM_SPEC = pl.BlockSpec(memory_space=pltpu.HBM)
SEM_SPEC = pl.BlockSpec(memory_space=pltpu.SEMAPHORE)
EFFECT = pltpu.SideEffectType.DATAFLOW_SIDE_EFFECTING


EXCHANGE_PEERS = {"gather": range(1, N_DEV), "scatter": range(1, N_DEV), "chip": (1, 2, 4, 6), "pass": (2, 4, 6)}


def _exchange_copies(srcs, lands, send_sems, recv_sems, mode, layer):
    x, y, c = lax.axis_index("x"), lax.axis_index("y"), lax.axis_index("c")
    me = 4 * x + 2 * y + c
    copies = []
    for a, (src, land) in enumerate(zip(srcs, lands)):
        for k in EXCHANGE_PEERS[mode]:
            px = 1 - x if k & 4 else x
            py = 1 - y if k & 2 else y
            pc = 1 - c if k & 1 else c
            peer = 4 * px + 2 * py + pc
            if mode == "scatter":
                s, d, to = src.at[peer], land.at[me, layer], (px, py, pc)
            elif mode == "pass":
                s, d, to = land.at[peer], land.at[peer], (x, y, 1 - c)
            else:
                s, d, to = src, land.at[me], (px, py, pc)
            n = 7 * a + k - 1
            copies.append(pltpu.make_async_remote_copy(
                src_ref=s, dst_ref=d, send_sem=send_sems.at[n], recv_sem=recv_sems.at[n], device_id=to,
                device_id_type=MESH))
    return copies


def _exchange_start(name, srcs, lands, mode, layer=0, after=None):
    n = len(srcs)

    def body(*refs):
        send_sems, recv_sems = refs[-2 * n - 3], refs[-2 * n - 2]
        for cp in _exchange_copies(refs[:n], refs[n:2 * n], send_sems, recv_sems, mode, layer):
            cp.start()
        refs[-1][...] = jnp.zeros_like(refs[-1])

    arrays = list(srcs) + list(lands)
    sems = pltpu.SemaphoreType.DMA((7 * n,))
    out = pl.pallas_call(
        body, name=name,
        out_shape=(sems, sems, *[pltpu.HBM(v.shape, v.dtype) for v in arrays], SDS((8, LANE), F32)),
        in_specs=[HBM_SPEC] * (2 * n) + ([ANY] if after is not None else []),
        out_specs=(SEM_SPEC, SEM_SPEC, *[HBM_SPEC] * (2 * n), pl.BlockSpec(memory_space=pltpu.VMEM)),
        input_output_aliases={i: 2 + i for i in range(2 * n)},
        compiler_params=pltpu.CompilerParams(has_side_effects=EFFECT),
    )(*[pltpu.with_memory_space_constraint(v, pltpu.HBM) for v in arrays], *([after] if after is not None else []))
    return dict(sems=out[:2], srcs=out[2:2 + n], lands=out[2 + n:2 + 2 * n], token=out[-1][0, 0], mode=mode,
                layer=layer)


def _exchange_wait(name, st, after, also=()):
    n = len(st["srcs"])

    def body(*refs):
        send_sems, recv_sems = refs[2 * n], refs[2 * n + 1]
        for cp in _exchange_copies(refs[:n], refs[n:2 * n], send_sems, recv_sems, st["mode"], st["layer"]):
            cp.wait_send()
            cp.wait_recv()

    arrays = list(st["srcs"]) + list(st["lands"])
    out = pl.pallas_call(
        body, name=name,
        out_shape=tuple(pltpu.HBM(v.shape, v.dtype) for v in arrays),
        in_specs=[HBM_SPEC] * (2 * n) + [SEM_SPEC, SEM_SPEC] + [ANY] * (1 + len(also)),
        out_specs=tuple([HBM_SPEC] * (2 * n)),
        input_output_aliases={i: i for i in range(2 * n)},
        compiler_params=pltpu.CompilerParams(has_side_effects=EFFECT),
    )(*arrays, *st["sems"], after, *also)
    st["srcs"] = out[:n]
    return out[n:]


_IN_PIECES = ([(1024, 3072)]
              + [r for t in range(4) for r in ((LANE * t, LANE * (t + 1)), (512 + LANE * t, 512 + LANE * (t + 1)))]
              + [(4096, 5632), (3072, 4096), (5632, 5648)])


def _permute_in(w):
    pad = jnp.zeros(w.shape[:-1] + (N_PAD - N_IN,), w.dtype)
    return jnp.concatenate([w[..., a:b] for a, b in _IN_PIECES] + [pad], axis=-1)


def _unpermute_in(g):
    ax = [g[..., OFF_LRU + 2 * LANE * t:OFF_LRU + 2 * LANE * t + LANE] for t in range(4)]
    ag = [g[..., OFF_LRU + 2 * LANE * t + LANE:OFF_LRU + 2 * LANE * (t + 1)] for t in range(4)]
    return jnp.concatenate(ax + ag + [g[..., 0:2048], g[..., OFF_Z:OFF_Z + SSD_W], g[..., OFF_XBC:OFF_XBC + SSD_CONV],
                                      g[..., OFF_Z + SSD_W:OFF_Z + SSD_W + SSD_HEADS]], axis=-1)


SHARD_COLS = N_IN // N_DEV


def _in_segments():
    segs, pos = [], 0
    for a, b in _IN_PIECES:
        for i in range(N_DEV):
            lo, hi = max(a, SHARD_COLS * i), min(b, SHARD_COLS * (i + 1))
            if lo < hi:
                segs.append((i, lo - SHARD_COLS * i, hi - lo, pos + lo - a))
        pos += b - a
    return segs


RELAYOUT_ROWS = 256


def _relayout_in(land, own):
    def body(land_ref, own_ref, out_ref):
        me = 4 * lax.axis_index("x") + 2 * lax.axis_index("y") + lax.axis_index("c")
        out_ref[:, N_IN:N_PAD] = jnp.zeros((RELAYOUT_ROWS, N_PAD - N_IN), BF16)
        for i, j, wd, p in _in_segments():
            out_ref[:, p:p + wd] = jnp.where(me == i, own_ref[:, j:j + wd], land_ref[i, :, j:j + wd])

    return pl.pallas_call(
        body, name="relayout_in", grid=(D_MODEL // RELAYOUT_ROWS,),
        in_specs=[pl.BlockSpec((N_DEV, RELAYOUT_ROWS, SHARD_COLS), lambda r: (0, r, 0)),
                  pl.BlockSpec((RELAYOUT_ROWS, SHARD_COLS), lambda r: (r, 0))],
        out_specs=pl.BlockSpec((RELAYOUT_ROWS, N_PAD), lambda r: (r, 0)),
        out_shape=SDS((D_MODEL, N_PAD), BF16),
        compiler_params=_cp(("parallel",)),
    )(land, own)


def _relayout_grad(g):
    def body(g_ref, out_ref):
        for i, j, wd, p in _in_segments():
            out_ref[i, :, j:j + wd] = g_ref[:, p:p + wd].astype(BF16)

    return pl.pallas_call(
        body, name="relayout_grad", grid=(D_MODEL // RELAYOUT_ROWS,),
        in_specs=[pl.BlockSpec((RELAYOUT_ROWS, N_PAD), lambda r: (r, 0))],
        out_specs=pl.BlockSpec((N_DEV, RELAYOUT_ROWS, SHARD_COLS), lambda r: (0, r, 0)),
        out_shape=SDS((N_DEV, D_MODEL, SHARD_COLS), BF16),
        compiler_params=_cp(("parallel",)),
    )(g)


def _block_diag(w):
    w4 = w.reshape(4, 2, 64, 64)
    z = jnp.zeros((4, 64, 64), w.dtype)
    top = jnp.concatenate([w4[:, 0], z], axis=-1)
    bot = jnp.concatenate([z, w4[:, 1]], axis=-1)
    return jnp.concatenate([top, bot], axis=1).astype(BF16)


def _diag_blocks(g):
    return jnp.stack([g[:, :64, :64], g[:, 64:, 64:]], axis=1).reshape(8, 64, 64)


def _pad_lanes(v):
    return jnp.pad(v, (0, LANE - v.shape[0]))[None, :]


def _lower_bounds(logits):
    p = jax.nn.softmax(logits, axis=0)
    return p, jnp.cumsum(p, axis=0) - p[0]


def _lower_bounds_bwd(p, dlb):
    dp = jnp.cumsum(dlb[::-1], axis=0)[::-1]
    dp = dp.at[0].add(-jnp.sum(dlb, axis=0))
    return p * (dp - jnp.sum(dp * p, axis=0, keepdims=True))


SMALL = ["norm_w", "b_ada", "lru_conv_b", "lru_wa", "lru_ba", "lru_wx", "lru_bx", "lru_lambda", "hg_lb_logits",
         "hg_norm_w", "ssd_conv_b", "ssd_dt_bias", "ssd_a_log", "ssd_d", "ssd_norm_w", "final_norm_w"]
WEIGHTS = ["norm_w", "w_ada", "b_ada", "w_in", "lru_conv_w", "lru_conv_b", "lru_wa", "lru_ba", "lru_wx", "lru_bx",
           "lru_lambda", "hg_lb_logits", "hg_norm_w", "ssd_conv_w", "ssd_conv_b", "ssd_dt_bias", "ssd_a_log", "ssd_d",
           "ssd_norm_w", "w_out", "final_norm_w"]
INPUTS = ["x", "c"] + WEIGHTS + ["loss_target"] + ["m_" + n for n in WEIGHTS] + ["v_" + n for n in WEIGHTS]
SMALL_ROW = 1024


def _small_rows(like):
    out, off = {}, 0
    for n in SMALL:
        rows = -(-int(np.prod(like[n].shape)) // (8 * SMALL_ROW)) * 8
        out[n] = (off, rows)
        off += rows
    return out, off


def _flatten_small(d, prefix="", last=0.0):
    table, _ = _small_rows({n: d[prefix + n] for n in SMALL})
    pieces = []
    for n in SMALL:
        flat = d[prefix + n].reshape(-1)
        pieces.append(jnp.pad(flat, (0, table[n][1] * SMALL_ROW - flat.shape[0])).reshape(-1, SMALL_ROW))
    return jnp.concatenate(pieces + [jnp.full((8, SMALL_ROW), last, F32)], axis=0)


def _split_small(packed, like):
    table, _ = _small_rows(like)
    out = {}
    for n in SMALL:
        off, rows = table[n]
        size = int(np.prod(like[n].shape))
        out[n] = packed[off:off + rows].reshape(-1)[:size].reshape(like[n].shape)
    return out


def _local_step(x, mod, target, w, fetch, emit):
    S = x.shape[0]
    mall = _bfc(_hg_consts())
    mall_t = _bfc(_hg_consts().T)
    consts = _ssd_consts()
    p_lb, lbs = _lower_bounds(w["hg_lb_logits"])
    saved = []
    for l in range(DEPTH):
        w_in_l, w_out_l, token = fetch(l, x)
        shift, scale, gate = (mod[l:l + 1, k * D_MODEL:(k + 1) * D_MODEL] for k in range(3))
        shift = shift + token
        prm = dict(
            nw=w["norm_w"][l:l + 1], cw=w["lru_conv_w"][l], cb=w["lru_conv_b"][l:l + 1],
            wa=_block_diag(w["lru_wa"][l]), ba=w["lru_ba"][l].reshape(1, LRU_W),
            wx=_block_diag(w["lru_wx"][l]), bx=w["lru_bx"][l].reshape(1, LRU_W), lam=w["lru_lambda"][l:l + 1],
            lb=lbs[l:l + 1], hnw=w["hg_norm_w"][l:l + 1], scw=w["ssd_conv_w"][l], scb=w["ssd_conv_b"][l:l + 1],
            bias=_pad_lanes(w["ssd_dt_bias"][l]), alog=_pad_lanes(w["ssd_a_log"][l]),
            dskip=jnp.repeat(w["ssd_d"][l], SSD_P)[None, :], snw=w["ssd_norm_w"][l:l + 1],
            w_in=w_in_l, w_out=w_out_l, scale=scale, gate=gate)
        u, h = _inproj_fwd(x, prm["nw"], scale, shift, prm["w_in"])
        ycat = lax.empty((S, D_INNER), BF16)
        lru_args = (u, prm["cw"], prm["cb"], prm["wa"], prm["ba"], prm["wx"], prm["bx"], prm["lam"])
        ycat, h_lru = _lru_fwd(*lru_args, ycat)
        ycat, o_b, hg_st = _hg_fwd(u, prm["lb"], prm["hnw"], mall, ycat)
        xbc = _ssdconv_fwd(u, prm["scw"], prm["scb"])
        ssd_args = (u, xbc, prm["bias"], prm["alog"], prm["dskip"], prm["snw"], consts)
        ycat, y_ssd, ssd_st = _ssd_fwd(*ssd_args, ycat)
        x_new, y = _outproj_fwd(ycat, prm["w_out"], x, gate)
        saved.append((prm, x, u, h, ycat, lru_args, h_lru, o_b, hg_st, ssd_args, y_ssd, ssd_st, y))
        x = x_new
    dx, red = _loss_head(x, w["final_norm_w"][None, :], target)
    loss = red[1, 0]
    g = {n: [None] * DEPTH for n in WEIGHTS}
    g["final_norm_w"] = red[0]
    dmod, dlb = [None] * DEPTH, [None] * DEPTH
    for l in reversed(range(DEPTH)):
        prm, x, u, h, ycat, lru_args, h_lru, o_b, hg_st, ssd_args, y_ssd, ssd_st, y = saved[l]
        dycat, g_out, dgate = _outproj_bwd(dx, y, prm["gate"], ycat, prm["w_out"])
        token = emit(l, "w_out", g_out)
        du = lax.empty((S, N_PAD), BF16)
        ssd_args = ssd_args[:5] + (ssd_args[5] + token,) + ssd_args[6:]
        du, dxbc, sred = _ssd_bwd(*ssd_args, y_ssd, ssd_st, dycat, du)
        du, cred = _ssdconv_bwd(u, prm["scw"], prm["scb"], dxbc, du)
        du, hred = _hg_bwd(u, prm["lb"], prm["hnw"], mall, mall_t, o_b, hg_st, dycat, du)
        du, lred, gwa, gwx = _lru_bwd(*lru_args, h_lru, dycat, du)
        token = emit(l, "w_in", _inproj_bwd_w(h, du))
        dx, ired = _inproj_bwd_x(du, prm["w_in"], x, prm["nw"], prm["scale"] + token, dx)
        g["norm_w"][l] = ired[2]
        dmod[l] = jnp.concatenate([ired[0], ired[1], dgate[0]])
        g["lru_conv_w"][l], g["lru_conv_b"][l] = lred[0:4], lred[4]
        g["lru_ba"][l], g["lru_bx"][l], g["lru_lambda"][l] = lred[5].reshape(8, 64), lred[6].reshape(8, 64), lred[7]
        g["lru_wa"][l], g["lru_wx"][l] = _diag_blocks(gwa), _diag_blocks(gwx)
        g["hg_norm_w"][l], dlb[l] = hred[0], hred[1]
        g["ssd_conv_w"][l], g["ssd_conv_b"][l] = cred[0:4], cred[4]
        g["ssd_norm_w"][l] = sred[0]
        g["ssd_d"][l] = sred[1].reshape(SSD_HEADS, SSD_P).sum(-1)
        g["ssd_dt_bias"][l] = sred[2, :SSD_HEADS]
        g["ssd_a_log"][l] = -sred[3, :SSD_HEADS] * jnp.exp(w["ssd_a_log"][l])
    g["hg_lb_logits"] = _lower_bounds_bwd(p_lb, jnp.stack(dlb))
    for n in WEIGHTS:
        if isinstance(g[n], list) and g[n][0] is not None:
            g[n] = jnp.stack(g[n])
    return loss, dx, jnp.stack(dmod), g


def kernel(x, c, norm_w, w_ada, b_ada, w_in, lru_conv_w, lru_conv_b, lru_wa, lru_ba, lru_wx, lru_bx, lru_lambda, hg_lb_logits, hg_norm_w, ssd_conv_w, ssd_conv_b, ssd_dt_bias, ssd_a_log, ssd_d, ssd_norm_w, w_out, final_norm_w, loss_target, m_norm_w, m_w_ada, m_b_ada, m_w_in, m_lru_conv_w, m_lru_conv_b, m_lru_wa, m_lru_ba, m_lru_wx, m_lru_bx, m_lru_lambda, m_hg_lb_logits, m_hg_norm_w, m_ssd_conv_w, m_ssd_conv_b, m_ssd_dt_bias, m_ssd_a_log, m_ssd_d, m_ssd_norm_w, m_w_out, m_final_norm_w, v_norm_w, v_w_ada, v_b_ada, v_w_in, v_lru_conv_w, v_lru_conv_b, v_lru_wa, v_lru_ba, v_lru_wx, v_lru_bx, v_lru_lambda, v_hg_lb_logits, v_hg_norm_w, v_ssd_conv_w, v_ssd_conv_b, v_ssd_dt_bias, v_ssd_a_log, v_ssd_d, v_ssd_norm_w, v_w_out, v_final_norm_w):
    return _step(x, c, norm_w, w_ada, b_ada, w_in, lru_conv_w, lru_conv_b, lru_wa, lru_ba, lru_wx, lru_bx, lru_lambda, hg_lb_logits, hg_norm_w, ssd_conv_w, ssd_conv_b, ssd_dt_bias, ssd_a_log, ssd_d, ssd_norm_w, w_out, final_norm_w, loss_target, m_norm_w, m_w_ada, m_b_ada, m_w_in, m_lru_conv_w, m_lru_conv_b, m_lru_wa, m_lru_ba, m_lru_wx, m_lru_bx, m_lru_lambda, m_hg_lb_logits, m_hg_norm_w, m_ssd_conv_w, m_ssd_conv_b, m_ssd_dt_bias, m_ssd_a_log, m_ssd_d, m_ssd_norm_w, m_w_out, m_final_norm_w, v_norm_w, v_w_ada, v_b_ada, v_w_in, v_lru_conv_w, v_lru_conv_b, v_lru_wa, v_lru_ba, v_lru_wx, v_lru_bx, v_lru_lambda, v_hg_lb_logits, v_hg_norm_w, v_ssd_conv_w, v_ssd_conv_b, v_ssd_dt_bias, v_ssd_a_log, v_ssd_d, v_ssd_norm_w, v_w_out, v_final_norm_w)


def _step(*args):
    a = dict(zip(INPUTS, args, strict=True))
    me = 4 * lax.axis_index("x") + 2 * lax.axis_index("y") + lax.axis_index("c")
    x, target = a["x"][0], a["loss_target"][0]

    c_all = _all_gather(a["c"], "gather_c")[:, 0, :]
    b_cols = lax.dynamic_slice_in_dim(a["b_ada"], me * ADA_COLS, ADA_COLS, axis=1)[:, None, :]
    mod_parts = _all_gather(_ada_fwd(c_all, a["w_ada"], b_cols), "gather_mod")
    mod = lax.dynamic_index_in_dim(mod_parts, me, axis=2, keepdims=False)
    mod = mod.transpose(1, 0, 2).reshape(DEPTH, 3 * D_MODEL)

    w = {n: a[n] for n in SMALL}

    w_in_b, w_out_b = a["w_in"].astype(BF16), a["w_out"].astype(BF16)
    conv_own = jnp.concatenate([a["lru_conv_w"], a["ssd_conv_w"]], axis=-1)
    cols, rows_out = N_IN // N_DEV, D_INNER // N_DEV

    def gather_start(l, after):
        srcs = [w_in_b[l], w_out_b[l]] + ([conv_own] if l == 0 else [])
        lands = [lax.empty((N_DEV,) + s.shape, s.dtype) for s in srcs]
        return _exchange_start(f"gather_start_{l}", srcs, lands, "chip" if l == 0 else "gather", after=after)

    def gather_pass(name, st, after, also=()):
        landed = _exchange_wait(name + "_wait", st, after, also)
        st2 = _exchange_start(name + "_pass", st["srcs"], landed, "pass")
        return _exchange_wait(name + "_passed", st2, after)

    gathers = {0: gather_start(0, mod)}

    def fetch(l, x_l):
        if l == 0:
            landed = gather_pass("gather_0", gathers[0], x_l, also=(a["m_w_in"], a["v_w_in"]))
        else:
            landed = _exchange_wait(f"gather_wait_{l}", gathers[l], x_l)
        land_out = lax.dynamic_update_index_in_dim(landed[1], w_out_b[l], me, 0)
        if l == 0:
            conv = lax.dynamic_update_index_in_dim(landed[2], conv_own, me, 0).transpose(1, 2, 0, 3)
            w["lru_conv_w"] = conv[..., :64].reshape(DEPTH, 4, LRU_W)
            w["ssd_conv_w"] = conv[..., 64:].reshape(DEPTH, 4, SSD_CONV)
        token = 0.0
        if l + 1 < DEPTH:
            gathers[l + 1] = gather_start(l + 1, land_out)
            token = gathers[l + 1]["token"]
        return _relayout_in(landed[0], w_in_b[l]), land_out.reshape(D_INNER, D_MODEL), token

    scatters = {"w_in": {}, "w_out": {}}
    lands = {"w_in": lax.empty((N_DEV, DEPTH, D_MODEL, cols), BF16),
             "w_out": lax.empty((N_DEV, DEPTH, rows_out, D_MODEL), BF16)}
    own = {"w_in": [None] * DEPTH, "w_out": [None] * DEPTH}

    def emit(l, name, grad):
        grad = _relayout_grad(grad) if name == "w_in" else grad.reshape(N_DEV, rows_out, D_MODEL)
        own[name][l] = lax.dynamic_index_in_dim(grad, me, 0, keepdims=False)
        st = _exchange_start(f"scatter_start_{name}_{l}", [grad], [lands[name]], "scatter", layer=l)
        scatters[name][l] = st
        lands[name] = st["lands"][0]
        return st["token"]

    loss_own, dx, dmod, g = _local_step(x, mod, target, w, fetch, emit)

    def sharded(name, parts, own=None, **kw):
        return _adamw(parts, a[name], a["m_" + name], a["v_" + name], "adamw_" + name + kw.pop("tag", ""), own=own, **kw)

    g["b_ada"] = dmod
    small_own = _flatten_small(g, last=loss_own)
    small_st = _exchange_start("gather_small", [small_own], [lax.empty((N_DEV,) + small_own.shape, F32)], "chip",
                               after=dx)
    big = {}
    after = small_st["token"] + dx[0:8, 0:LANE]
    for name in ("w_out", "w_in"):
        own_all = jnp.stack(own[name])
        for l in reversed(range(1, DEPTH)):
            scatters[name][l]["lands"] = [lands[name]]
            lands[name] = _exchange_wait(f"scatter_wait_{name}_{l}", scatters[name][l], after)[0]
        upper = sharded(name, lands[name], own_all, layers=(1, DEPTH), tag="_upper")
        scatters[name][0]["lands"] = [lands[name]]
        lands[name] = _exchange_wait(f"scatter_wait_{name}_0", scatters[name][0], upper[1])[0]
        big[name] = sharded(name, lands[name], own_all, layers=(0, 1), prev=upper)
        after = big[name][1]
    small = gather_pass("gather_small", small_st, after)[0]
    outs = _adamw(small[:, None], *[_flatten_small(a, p)[None] for p in ("", "m_", "v_")], "adamw_small",
                  own=small_own[None])
    res = [_split_small(o[0], a) for o in outs]
    losses = lax.dynamic_update_index_in_dim(small[:, -1, 0], loss_own, me, 0)
    loss = jnp.sum(losses)

    off = _small_rows(a)[0]["b_ada"][0]
    dmod_all = lax.dynamic_update_index_in_dim(small[:, off:off + DEPTH * 3 * D_MODEL // SMALL_ROW],
                                               dmod.reshape(-1, SMALL_ROW), me, 0)
    dmod_all = dmod_all.reshape(N_DEV, DEPTH, 3 * D_MODEL).transpose(1, 0, 2)
    dmod_cols = lax.dynamic_slice_in_dim(dmod_all, me * ADA_COLS, ADA_COLS, axis=2)
    dmod_pad = jnp.pad(dmod_cols, ((0, 0), (0, LANE - N_DEV), (0, 0)))
    ct_pad = jnp.pad(c_all.T, ((0, 0), (0, LANE - N_DEV)))
    big["w_ada"] = sharded("w_ada", _ada_bwd(ct_pad, dmod_pad)[None])
    g_conv = jnp.concatenate([g["lru_conv_w"].reshape(DEPTH, 4, N_DEV, 64), g["ssd_conv_w"].reshape(DEPTH, 4, N_DEV, 192)],
                             axis=-1).transpose(2, 0, 1, 3)
    conv_parts = _all_to_all(g_conv, "scatter_conv")
    big["lru_conv_w"] = sharded("lru_conv_w", conv_parts[..., :64])
    big["ssd_conv_w"] = sharded("ssd_conv_w", conv_parts[..., 64:])

    out = [loss, dx[None]]
    for k in range(4):
        out += [big[n][k] if n in big else res[k][n] for n in WEIGHTS]
    return tuple(out)
```

```python
import functools

import numpy as np
import jax
import jax.numpy as jnp
from jax import lax
from jax.experimental import pallas as pl
from jax.experimental.pallas import tpu as pltpu

F32 = jnp.float32
BF16 = jnp.bfloat16
SDS = jax.ShapeDtypeStruct

N_DEV = 8
DEPTH = 4
D_MODEL = 1024
D_INNER = 2048
EPS = 1e-6
LRU_W = 512
LRU_C = 8.0
HG_W = 512
HG_CHUNK = 64
HG_HEADS = 4
SSD_W = 1024
SSD_HEADS = 16
SSD_P = 64
SSD_N = 128
SSD_CHUNK = 128
SSD_CONV = 1536
N_IN = 5648
N_PAD = 5760
OFF_HG = 0
OFF_LRU = 2048
OFF_XBC = 3072
OFF_Z = 4608
LANE = 128
VMEM_LIMIT = 56 * 1024 * 1024
NEG = -1e30

ADAM_LR = 0.001
ADAM_B1 = 0.9
ADAM_B2 = 0.999
ADAM_EPS = 1e-08
ADAM_WD = 0.01
ADAM_STEP = 10


def _cp(sem=None):
    return pltpu.CompilerParams(dimension_semantics=sem, vmem_limit_bytes=VMEM_LIMIT)


def _dg(a, b, ca, cb):
    return lax.dot_general(a, b, (((ca,), (cb,)), ((), ())), preferred_element_type=F32)


def _mm(a, b):
    return _dg(a, b, 1, 0)


def _mm_nt(a, b):
    return _dg(a, b, 1, 1)


def _mm_tn(a, b):
    return _dg(a, b, 0, 0)


def _bf(x):
    return x.astype(BF16)


def _split3(x):
    hi = x.astype(BF16)
    r = x - hi.astype(F32)
    mid = r.astype(BF16)
    lo = (r - mid.astype(F32)).astype(BF16)
    return hi, mid, lo


def _sel_r(x, m):
    hi, mid, lo = _split3(x)
    return _mm(hi, m) + _mm(mid, m) + _mm(lo, m)


def _sel_l(m, x):
    hi, mid, lo = _split3(x)
    return _mm(m, hi) + _mm(m, mid) + _mm(m, lo)


def _sel_l2(m, x):
    hi = x.astype(BF16)
    lo = (x - hi.astype(F32)).astype(BF16)
    return _mm(m, hi) + _mm(m, lo)


def _sel_tn(x, m):
    hi, mid, lo = _split3(x)
    return _mm_tn(hi, m) + _mm_tn(mid, m) + _mm_tn(lo, m)


def _sigmoid(x):
    return 1.0 / (1.0 + jnp.exp(-x))


def _silu(x):
    return x * _sigmoid(x)


def _dsilu(x):
    s = _sigmoid(x)
    return s * (1.0 + x * (1.0 - s))


def _softplus(x):
    return jnp.maximum(x, 0.0) + jnp.log(1.0 + jnp.exp(-jnp.abs(x)))


def _expm1(z):
    series = z * (1.0 + z * (1.0 / 2) * (1.0 + z * (1.0 / 3) * (1.0 + z * (1.0 / 4) * (
        1.0 + z * (1.0 / 5) * (1.0 + z * (1.0 / 6) * (1.0 + z * (1.0 / 7)))))))
    return jnp.where(jnp.abs(z) < 0.3, series, jnp.exp(z) - 1.0)


def _iota(shape, dim):
    return lax.broadcasted_iota(jnp.int32, shape, dim)


def _last_row(x, rows):
    return jnp.sum(jnp.where(rows == x.shape[0] - 1, x, 0.0), axis=0, keepdims=True)


def _shift_down(x, d, rows, fill=0.0):
    return jnp.where(rows >= d, pltpu.roll(x, d, 0), fill)


def _shift_up(x, d, rows, fill=0.0):
    n = x.shape[0]
    return jnp.where(rows < n - d, pltpu.roll(x, n - d, 0), fill)


def _conv_fwd(x, cw_ref, cb_ref, rows):
    out = cb_ref[...] + cw_ref[pl.ds(3, 1), :] * x
    for k in range(3):
        out = out + cw_ref[pl.ds(k, 1), :] * _shift_down(x, 3 - k, rows)
    return out


def _conv_bwd(x, dco, cw_ref, rows):
    dx = cw_ref[pl.ds(3, 1), :] * dco
    dws = []
    for k in range(3):
        dx = dx + cw_ref[pl.ds(k, 1), :] * _shift_up(dco, 3 - k, rows)
        dws.append(jnp.sum(dco * _shift_down(x, 3 - k, rows), axis=0, keepdims=True))
    dws.append(jnp.sum(dco * x, axis=0, keepdims=True))
    return dx, dws, jnp.sum(dco, axis=0, keepdims=True)


def _vec(n):
    return pl.BlockSpec((1, n), lambda *_: (0, 0))


def _full(shape):
    nd = len(shape)
    return pl.BlockSpec(shape, lambda *_: (0,) * nd)


def _inproj_fwd(x, nw, scale, shift, w):
    S = x.shape[0]
    tm, tn = min(512, S), 640

    def body(x_ref, nw_ref, sc_ref, sh_ref, w_ref, u_ref, h_ref):
        @pl.when(pl.program_id(1) == 0)
        def _():
            xv = x_ref[...]
            inv = lax.rsqrt(jnp.mean(xv * xv, axis=-1, keepdims=True) + EPS)
            h = (xv * inv) * nw_ref[...] * (1.0 + sc_ref[...]) + sh_ref[...]
            h_ref[...] = h.astype(BF16)

        u_ref[...] = _mm(h_ref[...], w_ref[...])

    return pl.pallas_call(
        body, name="inproj_fwd", grid=(S // tm, N_PAD // tn),
        in_specs=[pl.BlockSpec((tm, D_MODEL), lambda i, j: (i, 0)), _vec(D_MODEL), _vec(D_MODEL), _vec(D_MODEL),
                  pl.BlockSpec((D_MODEL, tn), lambda i, j: (0, j))],
        out_specs=[pl.BlockSpec((tm, tn), lambda i, j: (i, j)), pl.BlockSpec((tm, D_MODEL), lambda i, j: (i, 0))],
        out_shape=[SDS((S, N_PAD), F32), SDS((S, D_MODEL), BF16)],
        compiler_params=_cp(("parallel", "arbitrary")),
    )(x, nw, scale, shift, w)


def _inproj_bwd_x(du, w, x, nw, scale, dxn):
    S = x.shape[0]
    tm, tk = min(512, S), 640
    nk = N_PAD // tk

    def body(du_ref, w_ref, x_ref, nw_ref, sc_ref, dxn_ref, dx_ref, red_ref, acc):
        i, k = pl.program_id(0), pl.program_id(1)

        @pl.when(k == 0)
        def _():
            acc[...] = jnp.zeros_like(acc)

        @pl.when((i == 0) & (k == 0))
        def _():
            red_ref[...] = jnp.zeros_like(red_ref)

        acc[...] += _mm_nt(_bf(du_ref[...]), w_ref[...])

        @pl.when(k == nk - 1)
        def _():
            dh = acc[...]
            xv = x_ref[...]
            inv = lax.rsqrt(jnp.mean(xv * xv, axis=-1, keepdims=True) + EPS)
            xhat = xv * inv
            nwv = nw_ref[...]
            g1 = 1.0 + sc_ref[...]
            dxhat = dh * nwv * g1
            dx = inv * (dxhat - xhat * jnp.mean(dxhat * xhat, axis=-1, keepdims=True))
            dx_ref[...] = dxn_ref[...] + dx
            red_ref[0:1, :] += jnp.sum(dh, axis=0, keepdims=True)
            red_ref[1:2, :] += jnp.sum(dh * xhat * nwv, axis=0, keepdims=True)
            red_ref[2:3, :] += jnp.sum(dh * xhat * g1, axis=0, keepdims=True)

    row = pl.BlockSpec((tm, D_MODEL), lambda i, k: (i, 0))
    return pl.pallas_call(
        body, name="inproj_bwd_x", grid=(S // tm, nk),
        in_specs=[pl.BlockSpec((tm, tk), lambda i, k: (i, k)), pl.BlockSpec((D_MODEL, tk), lambda i, k: (0, k)),
                  row, _vec(D_MODEL), _vec(D_MODEL), row],
        out_specs=[row, pl.BlockSpec((8, D_MODEL), lambda i, k: (0, 0))],
        out_shape=[SDS((S, D_MODEL), F32), SDS((8, D_MODEL), F32)],
        scratch_shapes=[pltpu.VMEM((tm, D_MODEL), F32)],
        compiler_params=_cp(("arbitrary", "arbitrary")),
    )(du, w, x, nw, scale, dxn)


def _inproj_bwd_w(h, du):
    S = h.shape[0]
    tn = 640

    def body(h_ref, du_ref, gw_ref):
        gw_ref[...] = _mm_tn(h_ref[...], _bf(du_ref[...]))

    return pl.pallas_call(
        body, name="inproj_bwd_w", grid=(N_PAD // tn,),
        in_specs=[_full((S, D_MODEL)), pl.BlockSpec((S, tn), lambda j: (0, j))],
        out_specs=pl.BlockSpec((D_MODEL, tn), lambda j: (0, j)),
        out_shape=SDS((D_MODEL, N_PAD), F32),
        compiler_params=_cp(("parallel",)),
    )(h, du)


def _scan_block(a, b, rows):
    d = 1
    while d < a.shape[0]:
        a_s = _shift_down(a, d, rows, 1.0)
        b_s = _shift_down(b, d, rows, 0.0)
        b = a * b_s + b
        a = a * a_s
        d *= 2
    return a, b


def _rscan_block(c, g, rows):
    d = 1
    while d < c.shape[0]:
        c_s = _shift_up(c, d, rows, 1.0)
        g_s = _shift_up(g, d, rows, 0.0)
        g = g + c * g_s
        c = c * c_s
        d *= 2
    return c, g


def _lru_gates(xa, wa_ref, ba_ref, wx_ref, bx_ref, lam_ref):
    sp = _softplus(-lam_ref[...])
    xb = _bf(xa)
    r = _sigmoid(_mm(xb, wa_ref[...]) + ba_ref[...])
    ig = _sigmoid(_mm(xb, wx_ref[...]) + bx_ref[...])
    la = -LRU_C * r * sp
    a = jnp.exp(la)
    mult = jnp.sqrt(-_expm1(2.0 * la))
    return sp, r, ig, la, a, mult


def _lru_specs(S):
    t128 = pl.BlockSpec((1, LANE), lambda t: (0, t))
    return [pl.BlockSpec((S, 2 * LANE), lambda t: (0, OFF_LRU // (2 * LANE) + t)),
            pl.BlockSpec((4, LANE), lambda t: (0, t)), t128,
            pl.BlockSpec((None, LANE, LANE), lambda t: (t, 0, 0)), t128,
            pl.BlockSpec((None, LANE, LANE), lambda t: (t, 0, 0)), t128, t128]


def _lru_fwd(u, cw, cb, wa, ba, wx, bx, lam, ycat):
    S = u.shape[0]
    tb = min(256, S)

    def body(u_ref, cw_ref, cb_ref, wa_ref, ba_ref, wx_ref, bx_ref, lam_ref, ycat_in, ycat_ref, h_ref, a_scr, b_scr):
        del ycat_in
        rows = _iota((S, LANE), 0)
        xa = _conv_fwd(u_ref[:, 0:LANE], cw_ref, cb_ref, rows)
        _, _, ig, _, a, mult = _lru_gates(xa, wa_ref, ba_ref, wx_ref, bx_ref, lam_ref)
        a_scr[...] = a
        b_scr[...] = mult * (ig * xa)
        rows_b = _iota((tb, LANE), 0)

        def blk(j, hprev):
            sl = pl.ds(pl.multiple_of(j * tb, tb), tb)
            acum, hloc = _scan_block(a_scr[sl, :], b_scr[sl, :], rows_b)
            hf = hloc + acum * hprev
            h_ref[sl, :] = hf
            return _last_row(hf, rows_b)

        lax.fori_loop(0, S // tb, blk, jnp.zeros((1, LANE), F32))
        ycat_ref[...] = _bf(h_ref[...] * _silu(u_ref[:, LANE:2 * LANE]))

    col = pl.BlockSpec((S, LANE), lambda t: (0, t))
    return pl.pallas_call(
        body, name="lru_fwd", grid=(LRU_W // LANE,),
        in_specs=_lru_specs(S) + [pl.BlockSpec(memory_space=pl.ANY)],
        out_specs=[col, col],
        out_shape=[SDS((S, D_INNER), BF16), SDS((S,LRU_W), F32)],
        scratch_shapes=[pltpu.VMEM((S, LANE), F32), pltpu.VMEM((S, LANE), F32)],
        input_output_aliases={8: 0},
        compiler_params=_cp(("parallel",)),
    )(u, cw, cb, wa, ba, wx, bx, lam, ycat)


def _lru_bwd(u, cw, cb, wa, ba, wx, bx, lam, h_lru, dycat, du):
    S = u.shape[0]
    tb = min(256, S)

    def body(u_ref, cw_ref, cb_ref, wa_ref, ba_ref, wx_ref, bx_ref, lam_ref, h_ref, dy_ref, du_in,
             du_ref, red_ref, gwa_ref, gwx_ref, c_scr, g_scr, l_scr):
        del du_in
        rows = _iota((S, LANE), 0)
        ax = u_ref[:, 0:LANE]
        ag = u_ref[:, LANE:2 * LANE]
        xa = _conv_fwd(ax, cw_ref, cb_ref, rows)
        sp, r, ig, la, a, mult = _lru_gates(xa, wa_ref, ba_ref, wx_ref, bx_ref, lam_ref)
        h = h_ref[...]
        dy = dy_ref[...]
        du_ref[:, LANE:2 * LANE] = _bf(dy * h * _dsilu(ag))
        c_scr[...] = _shift_up(a, 1, rows, 0.0)
        g_scr[...] = dy * _silu(ag)
        rows_b = _iota((tb, LANE), 0)
        nb = S // tb

        def blk(jj, lnext):
            j = nb - 1 - jj
            sl = pl.ds(pl.multiple_of(j * tb, tb), tb)
            ccum, lloc = _rscan_block(c_scr[sl, :], g_scr[sl, :], rows_b)
            lam_t = lloc + ccum * lnext
            l_scr[sl, :] = lam_t
            return jnp.sum(jnp.where(rows_b == 0, lam_t, 0.0), axis=0, keepdims=True)

        lax.fori_loop(0, nb, blk, jnp.zeros((1, LANE), F32))
        db = l_scr[...]
        da = db * _shift_down(h, 1, rows)
        dmult = db * ig * xa
        dig = db * mult * xa
        dxa = db * mult * ig
        dla = da * a - dmult * (a * a) / mult
        dr = -LRU_C * sp * dla
        dsp = jnp.sum(-LRU_C * r * dla, axis=0, keepdims=True)
        dlam = -dsp * _sigmoid(-lam_ref[...])
        dzr = dr * r * (1.0 - r)
        dzi = dig * ig * (1.0 - ig)
        dzr_b, dzi_b, xa_b = _bf(dzr), _bf(dzi), _bf(xa)
        dxa = dxa + _mm_nt(dzr_b, wa_ref[...]) + _mm_nt(dzi_b, wx_ref[...])
        gwa_ref[...] = _mm_tn(xa_b, dzr_b)
        gwx_ref[...] = _mm_tn(xa_b, dzi_b)
        dax, dws, dcb = _conv_bwd(ax, dxa, cw_ref, rows)
        du_ref[:, 0:LANE] = _bf(dax)
        parts = dws + [dcb, jnp.sum(dzr, axis=0, keepdims=True), jnp.sum(dzi, axis=0, keepdims=True), dlam]
        for n, p in enumerate(parts):
            red_ref[pl.ds(n, 1), :] = p

    col = pl.BlockSpec((S, LANE), lambda t: (0, t))
    gw = pl.BlockSpec((None, LANE, LANE), lambda t: (t, 0, 0))
    return pl.pallas_call(
        body, name="lru_bwd", grid=(LRU_W // LANE,),
        in_specs=_lru_specs(S) + [col, col, pl.BlockSpec(memory_space=pl.ANY)],
        out_specs=[pl.BlockSpec((S, 2 * LANE), lambda t: (0, OFF_LRU // (2 * LANE) + t)),
                   pl.BlockSpec((8, LANE), lambda t: (0, t)), gw, gw],
        out_shape=[SDS((S, N_PAD), BF16), SDS((8, LRU_W), F32), SDS((4, LANE, LANE), F32), SDS((4, LANE, LANE), F32)],
        scratch_shapes=[pltpu.VMEM((S, LANE), F32)] * 3,
        input_output_aliases={10: 0},
        compiler_params=_cp(("parallel",)),
    )(u, cw, cb, wa, ba, wx, bx, lam, h_lru, dycat, du)


HG_LEVELS = 6


def _hg_consts():
    C = HG_CHUNK
    t = np.arange(C)[:, None]
    r = np.arange(C)[None, :]
    mats = []
    for l in range(HG_LEVELS):
        b = 1 << l
        upper = (t % (2 * b)) >= b
        anchor = (t // (2 * b)) * 2 * b + b - 1
        mats.append((upper & (r > anchor) & (r <= t)) | ((~upper) & (r > t) & (r <= anchor)))
    mats.append(r <= t)
    mats.append(r > t)
    return np.concatenate(mats, 0).astype(np.float32)


def _hg_factors(hf, lb, mall):
    s = _sigmoid(hf)
    f = lb + (1.0 - lb) * s
    lf = jnp.log(f)
    k = (1.0 - lb) * _sigmoid(-hf)
    e = jnp.exp(_sel_l(mall, lf))
    C = HG_CHUNK
    eq = [e[l * C:(l + 1) * C] for l in range(HG_LEVELS)]
    ecum = e[HG_LEVELS * C:(HG_LEVELS + 1) * C]
    erem = e[(HG_LEVELS + 1) * C:(HG_LEVELS + 2) * C]
    return s, f, k, eq, eq, ecum, erem


def _hg_masks():
    C = HG_CHUNK
    ri, ci = _iota((C, C), 0), _iota((C, C), 1)
    rr = _iota((C, LANE), 0)
    gm = [(lax.shift_right_logical(ri, l + 1) == lax.shift_right_logical(ci, l + 1)).astype(F32)
          for l in range(HG_LEVELS)]
    up = [(lax.shift_right_logical(rr, l) & 1) == 1 for l in range(HG_LEVELS)]
    eye = (ri == ci).astype(F32)
    return gm, up, eye, rr


def _hg_scores(qh, kh, eq, ek, sl, gm, up, eye):
    qs, ks = [], []
    p = _mm_nt(_bf(qh), _bf(kh)) * eye
    for l in range(HG_LEVELS):
        ql = jnp.where(up[l], qh * eq[l][:, sl], 0.0)
        kl = jnp.where(up[l], 0.0, kh * ek[l][:, sl])
        p = p + _mm_nt(_bf(ql), _bf(kl)) * gm[l]
        qs.append(ql)
        ks.append(kl)
    return p, qs, ks


HG_SUB = 2


def _hg_fwd(u, lb, nw, mall, ycat):
    S = u.shape[0]
    C = HG_CHUNK
    n = S // C
    rows = HG_SUB * C

    def body(u_ref, lb_ref, nw_ref, mall_ref, ycat_in, ycat_ref, o_ref, st_ref, st):
        del ycat_in

        @pl.when(pl.program_id(0) == 0)
        def _():
            st[...] = jnp.zeros_like(st)

        gm, up, eye, rr = _hg_masks()
        for sub in range(HG_SUB):
            r = slice(sub * C, (sub + 1) * C)
            q = _silu(u_ref[r, 0:512])
            v = u_ref[r, 1024:1536]
            _, _, k, eq, ek, ecum, erem = _hg_factors(u_ref[r, 512:1024], lb_ref[...], mall_ref[...])
            for h in range(HG_HEADS):
                sl = slice(h * LANE, (h + 1) * LANE)
                qh, kh, vh = q[:, sl], k[:, sl], _bf(v[:, sl])
                p, _, _ = _hg_scores(qh, kh, eq, ek, sl, gm, up, eye)
                sth = st[h]
                st_ref[sub, h] = sth
                o_ref[r, sl] = _mm(_bf(p), vh) + _mm_nt(_bf(qh * ecum[:, sl]), _bf(sth))
                st[h] = sth * _last_row(ecum[:, sl], rr) + _mm_tn(vh, _bf(kh * erem[:, sl]))
            o = o_ref[r, :]
            inv = lax.rsqrt(jnp.mean(o * o, axis=-1, keepdims=True) + EPS)
            ycat_ref[r, :] = _bf((o * inv) * nw_ref[...] * _silu(u_ref[r, 1536:2048]))

    return pl.pallas_call(
        body, name="hg_fwd", grid=(n // HG_SUB,),
        in_specs=[pl.BlockSpec((rows, 2048), lambda i: (i, 0)), _vec(HG_W), _vec(HG_W), _full(mall.shape),
                  pl.BlockSpec(memory_space=pl.ANY)],
        out_specs=[pl.BlockSpec((rows, HG_W), lambda i: (i, 1)), pl.BlockSpec((rows, HG_W), lambda i: (i, 0)),
                   pl.BlockSpec((HG_SUB, HG_HEADS, LANE, LANE), lambda i: (i, 0, 0, 0))],
        out_shape=[SDS((S, D_INNER), BF16), SDS((S,HG_W), F32), SDS((n, HG_HEADS, LANE, LANE), F32)],
        scratch_shapes=[pltpu.VMEM((HG_HEADS, LANE, LANE), F32)],
        input_output_aliases={4: 0},
        compiler_params=_cp(("arbitrary",)),
    )(u, lb, nw, mall, ycat)


def _hg_bwd(u, lb, nw, mall, mall_t, o_b, states, dycat, du):
    S = u.shape[0]
    C = HG_CHUNK
    n = S // C
    nb = n // HG_SUB
    rows = HG_SUB * C
    L2 = HG_LEVELS

    def body(u_ref, lb_ref, nw_ref, mall_ref, mallt_ref, o_ref, st_ref, dy_ref, du_in, du_ref, red_ref,
             dst, dlast_s, dq_s, dk_s, dex):
        del du_in

        @pl.when(pl.program_id(0) == 0)
        def _():
            dst[...] = jnp.zeros_like(dst)
            red_ref[...] = jnp.zeros_like(red_ref)

        lb = lb_ref[...]
        nwv = nw_ref[...]
        gm, up, eye, rr = _hg_masks()
        for sub in reversed(range(HG_SUB)):
            r = slice(sub * C, (sub + 1) * C)
            hq, hf, hg = u_ref[r, 0:512], u_ref[r, 512:1024], u_ref[r, 1536:2048]
            q = _silu(hq)
            v = u_ref[r, 1024:1536]
            s, f, k, eq, ek, ecum, erem = _hg_factors(hf, lb, mall_ref[...])
            o = o_ref[r, :]
            dy = dy_ref[r, :]
            inv = lax.rsqrt(jnp.mean(o * o, axis=-1, keepdims=True) + EPS)
            ohat = o * inv
            du_ref[r, 1536:2048] = _bf(dy * ohat * nwv * _dsilu(hg))
            dn = dy * _silu(hg)
            red_ref[0:1, :] += jnp.sum(dn * ohat, axis=0, keepdims=True)
            dohat = dn * nwv
            do = inv * (dohat - ohat * jnp.mean(dohat * ohat, axis=-1, keepdims=True))
            for h in range(HG_HEADS):
                sl = slice(h * LANE, (h + 1) * LANE)
                qh, kh, vh, doh = q[:, sl], k[:, sl], _bf(v[:, sl]), _bf(do[:, sl])
                p, qs, ks = _hg_scores(qh, kh, eq, ek, sl, gm, up, eye)
                st_f = st_ref[sub, h]
                sth = _bf(st_f)
                dsth = dst[h]
                dsth_b = _bf(dsth)
                qt = qh * ecum[:, sl]
                kt = kh * erem[:, sl]
                elast = _last_row(ecum[:, sl], rr)
                dp = _mm_nt(doh, vh)
                du_ref[r, 1024 + h * LANE:1024 + (h + 1) * LANE] = _bf(_mm_tn(_bf(p), doh) + _mm_nt(_bf(kt), dsth_b))
                dpe = _bf(dp * eye)
                dqt = _mm(doh, sth)
                dkt = _mm(vh, dsth_b)
                dq = dqt * ecum[:, sl] + _mm(dpe, _bf(kh))
                dk = dkt * erem[:, sl] + _mm_tn(dpe, _bf(qh))
                dex[sub, L2 * C:(L2 + 1) * C, sl] = dqt * qt
                dex[sub, (L2 + 1) * C:(L2 + 2) * C, sl] = dkt * kt
                for l in range(HG_LEVELS):
                    dpl = _bf(dp * gm[l])
                    dql = _mm(dpl, _bf(ks[l]))
                    dkl = _mm_tn(dpl, _bf(qs[l]))
                    dq = dq + jnp.where(up[l], dql * eq[l][:, sl], 0.0)
                    dk = dk + jnp.where(up[l], 0.0, dkl * ek[l][:, sl])
                    dex[sub, l * C:(l + 1) * C, sl] = dql * qs[l] + dkl * ks[l]
                dlast_s[sub, :, sl] = jnp.sum(dsth * st_f, axis=0, keepdims=True) * elast
                dst[h] = dsth * elast + _mm_tn(doh, _bf(qt))
                dq_s[sub, :, sl] = dq
                dk_s[sub, :, sl] = dk
            dq = dq_s[sub]
            dk = dk_s[sub]
            dlf = _sel_l2(mallt_ref[...], dex[sub]) + dlast_s[sub]
            du_ref[r, 0:512] = _bf(dq * _dsilu(hq))
            t = (1.0 - s) * (dlf / f - dk)
            du_ref[r, 512:1024] = _bf((1.0 - lb) * s * t)
            red_ref[1:2, :] += jnp.sum(t, axis=0, keepdims=True)

    rev = lambda i: (nb - 1 - i, 0)
    return pl.pallas_call(
        body, name="hg_bwd", grid=(nb,),
        in_specs=[pl.BlockSpec((rows, 2048), rev), _vec(HG_W), _vec(HG_W), _full(mall.shape), _full(mall_t.shape),
                  pl.BlockSpec((rows, HG_W), rev),
                  pl.BlockSpec((HG_SUB, HG_HEADS, LANE, LANE), lambda i: (nb - 1 - i, 0, 0, 0)),
                  pl.BlockSpec((rows, HG_W), lambda i: (nb - 1 - i, 1)), pl.BlockSpec(memory_space=pl.ANY)],
        out_specs=[pl.BlockSpec((rows, 2048), rev), pl.BlockSpec((8, HG_W), lambda i: (0, 0))],
        out_shape=[SDS((S, N_PAD), BF16), SDS((8, HG_W), F32)],
        scratch_shapes=[pltpu.VMEM((HG_HEADS, LANE, LANE), F32), pltpu.VMEM((HG_SUB, 1, HG_W), F32),
                        pltpu.VMEM((HG_SUB, C, HG_W), F32), pltpu.VMEM((HG_SUB, C, HG_W), F32),
                        pltpu.VMEM((HG_SUB, (L2 + 2) * C, HG_W), F32)],
        input_output_aliases={8: 0},
        compiler_params=_cp(("arbitrary",)),
    )(u, lb, nw, mall, mall_t, o_b, states, dycat, du)


def _ssdconv_fwd(u, cw, cb):
    S = u.shape[0]

    def body(u_ref, cw_ref, cb_ref, out_ref):
        rows = _iota((S, LANE), 0)
        out_ref[...] = _silu(_conv_fwd(u_ref[...], cw_ref, cb_ref, rows))

    return pl.pallas_call(
        body, name="ssdconv_fwd", grid=(SSD_CONV // LANE,),
        in_specs=[pl.BlockSpec((S, LANE), lambda t: (0, OFF_XBC // LANE + t)), pl.BlockSpec((4, LANE), lambda t: (0, t)),
                  pl.BlockSpec((1, LANE), lambda t: (0, t))],
        out_specs=pl.BlockSpec((S, LANE), lambda t: (0, t)),
        out_shape=SDS((S, SSD_CONV), F32),
        compiler_params=_cp(("parallel",)),
    )(u, cw, cb)


def _ssdconv_bwd(u, cw, cb, dxbc, du):
    S = u.shape[0]

    def body(u_ref, cw_ref, cb_ref, d_ref, du_in, du_ref, red_ref):
        del du_in
        rows = _iota((S, LANE), 0)
        x = u_ref[...]
        dco = d_ref[...] * _dsilu(_conv_fwd(x, cw_ref, cb_ref, rows))
        dx, dws, dcb = _conv_bwd(x, dco, cw_ref, rows)
        du_ref[...] = _bf(dx)
        for n, p in enumerate(dws + [dcb]):
            red_ref[pl.ds(n, 1), :] = p
        red_ref[pl.ds(5, 3), :] = jnp.zeros((3, LANE), F32)

    ucol = pl.BlockSpec((S, LANE), lambda t: (0, OFF_XBC // LANE + t))
    return pl.pallas_call(
        body, name="ssdconv_bwd", grid=(SSD_CONV // LANE,),
        in_specs=[ucol, pl.BlockSpec((4, LANE), lambda t: (0, t)), pl.BlockSpec((1, LANE), lambda t: (0, t)),
                  pl.BlockSpec((S, LANE), lambda t: (0, t)), pl.BlockSpec(memory_space=pl.ANY)],
        out_specs=[ucol, pl.BlockSpec((8, LANE), lambda t: (0, t))],
        out_shape=[SDS((S, N_PAD), BF16), SDS((8, SSD_CONV), F32)],
        input_output_aliases={4: 0},
        compiler_params=_cp(("parallel",)),
    )(u, cw, cb, dxbc, du)


def _ssd_consts():
    e64 = np.zeros((LANE, SSD_W), np.float32)
    e128 = np.zeros((LANE, SSD_HEADS * LANE), np.float32)
    for h in range(SSD_HEADS):
        e64[h, h * SSD_P:(h + 1) * SSD_P] = 1.0
        e128[h, h * LANE:(h + 1) * LANE] = 1.0
    T = SSD_CHUNK
    tril = (np.arange(T)[None, :] <= np.arange(T)[:, None]).astype(np.float32)
    return e64, e128, tril, tril.T.copy()


def _ssd_common(zdt, bias_ref, alog_ref, tril, e64, e128):
    T = SSD_CHUNK
    lane = _iota((1, LANE), 1)
    a_neg = jnp.where(lane < SSD_HEADS, -jnp.exp(alog_ref[...]), 0.0)
    dtpre = zdt[:, SSD_W:SSD_W + LANE] + bias_ref[...]
    dt = _softplus(dtpre)
    cum = _sel_l(tril, dt * a_neg)
    rowsT = _iota((T, LANE), 0)
    last = _last_row(cum, rowsT)
    ecum_x = _sel_r(jnp.exp(cum), e64)
    erem_x = _sel_r(jnp.exp(last - cum), e64)
    elast_x = _last_row(ecum_x, _iota((T, SSD_W), 0))
    dt_x = _sel_r(dt, e64)
    cum_e = _sel_r(cum, e128)
    return a_neg, dtpre, dt, cum, ecum_x, erem_x, elast_x, dt_x, cum_e


def _ssd_decay(cum_e, cumt_ref, h, causal):
    diff = cum_e[:, h * LANE:(h + 1) * LANE] - cumt_ref[pl.ds(h, 1), :]
    return jnp.exp(jnp.where(causal, diff, NEG))


def _group_norm_fwd(y1, nwv):
    outs, invs = [], []
    for g in range(2):
        seg = y1[:, g * 512:(g + 1) * 512]
        inv = lax.rsqrt(jnp.mean(seg * seg, axis=-1, keepdims=True) + EPS)
        outs.append(seg * inv * nwv[:, g * 512:(g + 1) * 512])
        invs.append(inv)
    return outs, invs


def _ssd_fwd(u, xbc, bias, alog, dskip_x, nw, consts, ycat):
    S = u.shape[0]
    T = SSD_CHUNK
    n = S // T
    e64, e128, tril, _ = consts

    def body(u_ref, xbc_ref, bias_ref, alog_ref, dx_ref, nw_ref, e64_ref, e128_ref, tril_ref, ycat_in,
             ycat_ref, y_ref, st_ref, st, cumt):
        del ycat_in

        @pl.when(pl.program_id(0) == 0)
        def _():
            st[...] = jnp.zeros_like(st)

        zdt = u_ref[...]
        z = zdt[:, 0:SSD_W]
        xs = xbc_ref[:, 0:SSD_W]
        _, _, _, cum, ecum_x, erem_x, elast_x, dt_x, cum_e = _ssd_common(
            zdt, bias_ref, alog_ref, tril_ref[...], e64_ref[...], e128_ref[...])
        cumt[...] = cum.T
        causal = _iota((T, T), 0) >= _iota((T, T), 1)
        lo = _iota((T, LANE), 1) < SSD_P
        xdt = xs * dt_x
        xrem = xdt * erem_x
        st_ref[...] = st[...]
        for g in range(2):
            gs = slice(g * 512, (g + 1) * 512)
            bg = _bf(xbc_ref[:, SSD_W + g * LANE:SSD_W + (g + 1) * LANE])
            cg = _bf(xbc_ref[:, SSD_W + 256 + g * LANE:SSD_W + 256 + (g + 1) * LANE])
            cb = _mm_nt(cg, bg)
            yin = _mm(cg, _bf(st[:, gs])) * ecum_x[:, gs]
            for j in range(4):
                h0 = 8 * g + 2 * j
                cs = slice(h0 * SSD_P, (h0 + 2) * SSD_P)
                xp = xdt[:, cs]
                s0 = _bf(cb * _ssd_decay(cum_e, cumt, h0, causal))
                s1 = _bf(cb * _ssd_decay(cum_e, cumt, h0 + 1, causal))
                y_ref[:, cs] = (_mm(s0, _bf(jnp.where(lo, xp, 0.0))) + _mm(s1, _bf(jnp.where(lo, 0.0, xp)))
                                + yin[:, j * LANE:(j + 1) * LANE])
            st[:, gs] = st[:, gs] * elast_x[:, gs] + _mm_tn(bg, _bf(xrem[:, gs]))
        y1 = (y_ref[...] + dx_ref[...] * xs) * _silu(z)
        outs, _ = _group_norm_fwd(y1, nw_ref[...])
        for g in range(2):
            ycat_ref[:, g * 512:(g + 1) * 512] = _bf(outs[g])

    return pl.pallas_call(
        body, name="ssd_fwd", grid=(n,),
        in_specs=[pl.BlockSpec((T, SSD_W + LANE), lambda i: (i, OFF_Z // (SSD_W + LANE))),
                  pl.BlockSpec((T, SSD_CONV), lambda i: (i, 0)), _vec(LANE), _vec(LANE), _vec(SSD_W), _vec(SSD_W),
                  _full(e64.shape), _full(e128.shape), _full(tril.shape), pl.BlockSpec(memory_space=pl.ANY)],
        out_specs=[pl.BlockSpec((T, SSD_W), lambda i: (i, 1)), pl.BlockSpec((T, SSD_W), lambda i: (i, 0)),
                   pl.BlockSpec((None, SSD_N, SSD_W), lambda i: (i, 0, 0))],
        out_shape=[SDS((S, D_INNER), BF16), SDS((S,SSD_W), F32), SDS((n, SSD_N, SSD_W), F32)],
        scratch_shapes=[pltpu.VMEM((SSD_N, SSD_W), F32), pltpu.VMEM((LANE, T), F32)],
        input_output_aliases={9: 0},
        compiler_params=_cp(("arbitrary",)),
    )(u, xbc, bias, alog, dskip_x, nw, _bfc(e64), _bfc(e128), _bfc(tril), ycat)


def _ssd_bwd(u, xbc, bias, alog, dskip_x, nw, consts, y_ssd, states, dycat, du):
    S = u.shape[0]
    T = SSD_CHUNK
    n = S // T
    e64, e128, tril, triu = consts
    e64t = np.ascontiguousarray(e64.T)

    def body(u_ref, xbc_ref, bias_ref, alog_ref, dx_ref, nw_ref, e64_ref, e64t_ref, e128_ref, tril_ref, triu_ref,
             y_ref, st_ref, dy_ref, du_in, du_ref, dxbc_ref, red_ref, dst, dl_s, cumt, dxdt_s, dy0_s, gb_s, gc_s):
        del du_in

        @pl.when(pl.program_id(0) == 0)
        def _():
            dst[...] = jnp.zeros_like(dst)
            red_ref[...] = jnp.zeros_like(red_ref)

        zdt = u_ref[...]
        z = zdt[:, 0:SSD_W]
        xs = xbc_ref[:, 0:SSD_W]
        e64m = e64_ref[...]
        a_neg, dtpre, dt, cum, ecum_x, erem_x, elast_x, dt_x, cum_e = _ssd_common(
            zdt, bias_ref, alog_ref, tril_ref[...], e64m, e128_ref[...])
        cumt[...] = cum.T
        causal = _iota((T, T), 0) >= _iota((T, T), 1)
        lo = _iota((T, LANE), 1) < SSD_P
        xdt = xs * dt_x
        xrem = xdt * erem_x
        y = y_ref[...]
        dxv = dx_ref[...]
        nwv = nw_ref[...]
        sz = _silu(z)
        y0 = y + dxv * xs
        y1 = y0 * sz
        for g in range(2):
            gs = slice(g * 512, (g + 1) * 512)
            seg = y1[:, gs]
            inv = lax.rsqrt(jnp.mean(seg * seg, axis=-1, keepdims=True) + EPS)
            shat = seg * inv
            dyg = dy_ref[:, gs]
            red_ref[0:1, gs] += jnp.sum(dyg * shat, axis=0, keepdims=True)
            dsh = dyg * nwv[:, gs]
            dy1g = inv * (dsh - shat * jnp.mean(dsh * shat, axis=-1, keepdims=True))
            du_ref[:, gs] = _bf(dy1g * y0[:, gs] * _dsilu(z[:, gs]))
            dy0_s[:, gs] = dy1g * sz[:, gs]
        dy0 = dy0_s[...]
        red_ref[1:2, :] += jnp.sum(dy0 * xs, axis=0, keepdims=True)
        dyin = dy0 * ecum_x
        lane = _iota((T, LANE), 1)
        ones = jnp.ones((T, LANE), BF16)
        dcum = jnp.zeros((T, LANE), F32)

        def row_minus_col(gm):
            hi = _bf(gm)
            lw = _bf(gm - hi.astype(F32))
            return _mm(hi, ones) + _mm(lw, ones) - _mm_tn(hi, ones) - _mm_tn(lw, ones)

        for g in range(2):
            gs = slice(g * 512, (g + 1) * 512)
            bg = _bf(xbc_ref[:, SSD_W + g * LANE:SSD_W + (g + 1) * LANE])
            cg = _bf(xbc_ref[:, SSD_W + 256 + g * LANE:SSD_W + 256 + (g + 1) * LANE])
            cb = _mm_nt(cg, bg)
            dst_f, st_f = dst[:, gs], st_ref[:, gs]
            dstg = _bf(dst_f)
            stg = _bf(st_f)
            dyin_g = _bf(dyin[:, gs])
            xrem_g = _bf(xrem[:, gs])
            dcb = jnp.zeros((T, T), F32)
            dxr = _mm(bg, dstg)
            dxdt_s[:, gs] = dxr * erem_x[:, gs]
            gc_s[:, gs] = dxr * xrem[:, gs]
            gb_s[:, gs] = dyin[:, gs] * _mm(cg, stg)
            dl_s[:, gs] = jnp.sum(dst_f * st_f, axis=0, keepdims=True) * elast_x[:, gs]
            for j in range(4):
                h0 = 8 * g + 2 * j
                cs = slice(h0 * SSD_P, (h0 + 2) * SSD_P)
                xp = xdt[:, cs]
                dyp = dy0[:, cs]
                x_lo, x_hi = _bf(jnp.where(lo, xp, 0.0)), _bf(jnp.where(lo, 0.0, xp))
                d_lo, d_hi = _bf(jnp.where(lo, dyp, 0.0)), _bf(jnp.where(lo, 0.0, dyp))
                s0 = cb * _ssd_decay(cum_e, cumt, h0, causal)
                s1 = cb * _ssd_decay(cum_e, cumt, h0 + 1, causal)
                ds0 = _mm_nt(d_lo, x_lo)
                ds1 = _mm_nt(d_hi, x_hi)
                dcb = dcb + ds0 * _ssd_decay(cum_e, cumt, h0, causal) + ds1 * _ssd_decay(cum_e, cumt, h0 + 1, causal)
                dxdt_s[:, cs] += _mm_tn(_bf(s0), d_lo) + _mm_tn(_bf(s1), d_hi)
                dcum = dcum + jnp.where(lane == h0, row_minus_col(ds0 * s0), 0.0)
                dcum = dcum + jnp.where(lane == h0 + 1, row_minus_col(ds1 * s1), 0.0)
            dcb_b = _bf(dcb)
            dxbc_ref[:, SSD_W + g * LANE:SSD_W + (g + 1) * LANE] = _mm_tn(dcb_b, cg) + _mm_nt(xrem_g, dstg)
            dxbc_ref[:, SSD_W + 256 + g * LANE:SSD_W + 256 + (g + 1) * LANE] = _mm(dcb_b, bg) + _mm_nt(dyin_g, stg)
            dst[:, gs] = dst_f * elast_x[:, gs] + _mm_tn(cg, dyin_g)
        dxdt = dxdt_s[...]
        dxbc_ref[:, 0:SSD_W] = dxdt * dt_x + dy0 * dxv
        e64t = e64t_ref[...]
        hc = _sel_r(gc_s[...], e64t)
        dlast = (jnp.sum(hc, axis=0, keepdims=True)
                 + jnp.max(_sel_r(jnp.broadcast_to(dl_s[...], (8, SSD_W)), e64t), axis=0, keepdims=True))
        dcum = dcum + _sel_r(gb_s[...], e64t) - hc + jnp.where(_iota((T, LANE), 0) == T - 1, dlast, 0.0)
        dda = _sel_l(triu_ref[...], dcum)
        ddt = dda * a_neg + _sel_r(dxdt * xs, e64t)
        ddtpre = ddt * _sigmoid(dtpre)
        du_ref[:, SSD_W:SSD_W + LANE] = _bf(jnp.where(lane < SSD_HEADS, ddtpre, 0.0))
        red_ref[2:3, 0:LANE] += jnp.sum(ddtpre, axis=0, keepdims=True)
        red_ref[3:4, 0:LANE] += jnp.sum(dda * dt, axis=0, keepdims=True)

    rev = lambda i: (n - 1 - i, 0)
    return pl.pallas_call(
        body, name="ssd_bwd", grid=(n,),
        in_specs=[pl.BlockSpec((T, SSD_W + LANE), lambda i: (n - 1 - i, OFF_Z // (SSD_W + LANE))),
                  pl.BlockSpec((T, SSD_CONV), rev), _vec(LANE), _vec(LANE), _vec(SSD_W), _vec(SSD_W),
                  _full(e64.shape), _full(e64t.shape), _full(e128.shape), _full(tril.shape), _full(triu.shape),
                  pl.BlockSpec((T, SSD_W), rev), pl.BlockSpec((None, SSD_N, SSD_W), lambda i: (n - 1 - i, 0, 0)),
                  pl.BlockSpec((T, SSD_W), lambda i: (n - 1 - i, 1)), pl.BlockSpec(memory_space=pl.ANY)],
        out_specs=[pl.BlockSpec((T, SSD_W + LANE), lambda i: (n - 1 - i, OFF_Z // (SSD_W + LANE))),
                   pl.BlockSpec((T, SSD_CONV), rev), pl.BlockSpec((8, SSD_W), lambda i: (0, 0))],
        out_shape=[SDS((S, N_PAD), BF16), SDS((S, SSD_CONV), F32), SDS((8, SSD_W), F32)],
        scratch_shapes=[pltpu.VMEM((SSD_N, SSD_W), F32), pltpu.VMEM((1, SSD_W), F32), pltpu.VMEM((LANE, T), F32)]
        + [pltpu.VMEM((T, SSD_W), F32)] * 4,
        input_output_aliases={14: 0},
        compiler_params=_cp(("arbitrary",)),
    )(u, xbc, bias, alog, dskip_x, nw, _bfc(e64), _bfc(e64t), _bfc(e128), _bfc(tril), _bfc(triu), y_ssd, states, dycat, du)


def _bfc(a):
    return jnp.asarray(a, BF16)


def _outproj_fwd(ycat, wo, x, gate):
    S = x.shape[0]
    tm = min(512, S)

    def body(yc_ref, wo_ref, x_ref, g_ref, xn_ref, y_ref):
        y = _mm(_bf(yc_ref[...]), wo_ref[...])
        y_ref[...] = y
        xn_ref[...] = x_ref[...] + g_ref[...] * y

    row = pl.BlockSpec((tm, D_MODEL), lambda i: (i, 0))
    return pl.pallas_call(
        body, name="outproj_fwd", grid=(S // tm,),
        in_specs=[pl.BlockSpec((tm, D_INNER), lambda i: (i, 0)), _full((D_INNER, D_MODEL)), row, _vec(D_MODEL)],
        out_specs=[row, row],
        out_shape=[SDS((S, D_MODEL), F32), SDS((S, D_MODEL), F32)],
        compiler_params=_cp(("parallel",)),
    )(ycat, wo, x, gate)


def _outproj_bwd(dxn, y, gate, ycat, wo):
    S = dxn.shape[0]
    tm = min(512, S)

    def body(dx_ref, y_ref, g_ref, yc_ref, wo_ref, dyc_ref, gwo_ref, dg_ref, acc):
        @pl.when(pl.program_id(0) == 0)
        def _():
            acc[...] = jnp.zeros_like(acc)
            dg_ref[...] = jnp.zeros_like(dg_ref)

        dxv = dx_ref[...]
        dy = _bf(dxv * g_ref[...])
        dg_ref[0:1, :] += jnp.sum(dxv * y_ref[...], axis=0, keepdims=True)
        dyc_ref[...] = _mm_nt(dy, wo_ref[...])
        acc[...] += _mm_tn(_bf(yc_ref[...]), dy)

        @pl.when(pl.program_id(0) == pl.num_programs(0) - 1)
        def _():
            gwo_ref[...] = acc[...].astype(BF16)

    row = pl.BlockSpec((tm, D_MODEL), lambda i: (i, 0))
    wide = pl.BlockSpec((tm, D_INNER), lambda i: (i, 0))
    return pl.pallas_call(
        body, name="outproj_bwd", grid=(S // tm,),
        in_specs=[row, row, _vec(D_MODEL), wide, _full((D_INNER, D_MODEL))],
        out_specs=[wide, _full((D_INNER, D_MODEL)), _full((8, D_MODEL))],
        out_shape=[SDS((S, D_INNER), F32), SDS((D_INNER, D_MODEL), BF16), SDS((8, D_MODEL), F32)],
        scratch_shapes=[pltpu.VMEM((D_INNER, D_MODEL), F32)],
        compiler_params=_cp(("arbitrary",)),
    )(dxn, y, gate, ycat, wo)


def _loss_head(x, fw, target):
    S = x.shape[0]
    tm = min(512, S)

    def body(x_ref, fw_ref, t_ref, dx_ref, red_ref):
        @pl.when(pl.program_id(0) == 0)
        def _():
            red_ref[...] = jnp.zeros_like(red_ref)

        xv = x_ref[...]
        fwv = fw_ref[...]
        inv = lax.rsqrt(jnp.mean(xv * xv, axis=-1, keepdims=True) + EPS)
        xhat = xv * inv
        err = xhat * fwv - t_ref[...]
        col = jnp.sum(err * err, axis=0, keepdims=True)
        red_ref[1:2, :] += jnp.broadcast_to(jnp.sum(col, axis=1, keepdims=True) * (0.5 / D_MODEL), (1, D_MODEL))
        dy = err * (1.0 / D_MODEL)
        red_ref[0:1, :] += jnp.sum(dy * xhat, axis=0, keepdims=True)
        dxhat = dy * fwv
        dx_ref[...] = inv * (dxhat - xhat * jnp.mean(dxhat * xhat, axis=-1, keepdims=True))

    row = pl.BlockSpec((tm, D_MODEL), lambda i: (i, 0))
    return pl.pallas_call(
        body, name="loss_head", grid=(S // tm,),
        in_specs=[row, _vec(D_MODEL), row],
        out_specs=[row, _full((8, D_MODEL))],
        out_shape=[SDS((S, D_MODEL), F32), SDS((8, D_MODEL), F32)],
        compiler_params=_cp(("arbitrary",)),
    )(x, fw, target)


ADA_COLS = 3 * D_MODEL // N_DEV


def _ada_fwd(c_all, w_ada, b_cols):
    def body(c_ref, w_ref, b_ref, out_ref):
        out_ref[...] = _mm(_bf(_silu(c_ref[...])), _bf(w_ref[...])) + b_ref[...]

    return pl.pallas_call(
        body, name="ada_fwd", grid=(DEPTH,),
        in_specs=[_full((N_DEV, D_MODEL)), pl.BlockSpec((None, D_MODEL, ADA_COLS), lambda l: (l, 0, 0)),
                  pl.BlockSpec((None, 1, ADA_COLS), lambda l: (l, 0, 0))],
        out_specs=pl.BlockSpec((None, N_DEV, ADA_COLS), lambda l: (l, 0, 0)),
        out_shape=SDS((DEPTH, N_DEV, ADA_COLS), F32),
        compiler_params=_cp(("parallel",)),
    )(c_all, w_ada, b_cols)


def _ada_bwd(ct_pad, dmod_pad):
    def body(c_ref, d_ref, out_ref):
        out_ref[...] = _mm(_bf(_silu(c_ref[...])), _bf(d_ref[...]))

    return pl.pallas_call(
        body, name="ada_bwd", grid=(DEPTH,),
        in_specs=[_full((D_MODEL, LANE)), pl.BlockSpec((None, LANE, ADA_COLS), lambda l: (l, 0, 0))],
        out_specs=pl.BlockSpec((None, D_MODEL, ADA_COLS), lambda l: (l, 0, 0)),
        out_shape=SDS((DEPTH, D_MODEL, ADA_COLS), F32),
        compiler_params=_cp(("parallel",)),
    )(ct_pad, dmod_pad)


def _adamw(parts, w, m, v, name, own=None, layers=None, prev=None):
    n, L, R, C = parts.shape
    lo, hi = layers or (0, L)
    tr = R
    while tr * C * 4 > (1 << 20) and tr % 16 == 0:
        tr //= 2
    first = 1 if own is None else 2

    def body(*refs):
        p_ref = refs[0]
        w_ref, m_ref, v_ref = refs[first:first + 3]
        g_ref, d_ref, mo_ref, vo_ref = refs[-4:]

        def part(k):
            if own is None:
                return p_ref[k].astype(F32)
            me = 4 * lax.axis_index("x") + 2 * lax.axis_index("y") + lax.axis_index("c")
            return jnp.where(me == k, refs[1][...], p_ref[k]).astype(F32)

        g = part(0)
        for k in range(1, n):
            g = g + part(k)
        mn = ADAM_B1 * m_ref[...] + (1.0 - ADAM_B1) * g
        vn = ADAM_B2 * v_ref[...] + (1.0 - ADAM_B2) * (g * g)
        m_hat = mn / (1.0 - ADAM_B1 ** ADAM_STEP)
        v_hat = vn / (1.0 - ADAM_B2 ** ADAM_STEP)
        g_ref[...] = g
        d_ref[...] = -ADAM_LR * (m_hat / (jnp.sqrt(v_hat) + ADAM_EPS) + ADAM_WD * w_ref[...])
        mo_ref[...] = mn
        vo_ref[...] = vn

    blk = pl.BlockSpec((None, tr, C), lambda l, i: (lo + l, i, 0))
    n_blk = 3 if own is None else 4
    return pl.pallas_call(
        body, name=name, grid=(hi - lo, R // tr),
        in_specs=[pl.BlockSpec((n, None, tr, C), lambda l, i: (0, lo + l, i, 0))] + [blk] * n_blk
        + ([] if prev is None else [ANY] * 4),
        out_specs=[blk] * 4,
        out_shape=[SDS((L, R, C), F32)] * 4,
        input_output_aliases={} if prev is None else {1 + n_blk + k: k for k in range(4)},
        compiler_params=_cp(("parallel", "parallel")),
    )(parts, *([] if own is None else [own]), w, m, v, *([] if prev is None else prev))


MESH = pl.DeviceIdType.MESH
ANY = pl.BlockSpec(memory_space=pl.ANY)


def _all_gather(v, name):
    def body(v_ref, out_ref, send_sems, recv_sems, local_sem):
        x, y, c = lax.axis_index("x"), lax.axis_index("y"), lax.axis_index("c")
        me, sibling = (x, y, c), (x, y, 1 - c)
        chips = [(1 - x, y), (x, 1 - y), (1 - x, 1 - y)]

        def slot(px, py, pc):
            return out_ref.at[4 * px + 2 * py + pc]

        def copy(k, block, to, src=None):
            return pltpu.make_async_remote_copy(
                src_ref=slot(*block) if src is None else src, dst_ref=slot(*block),
                send_sem=send_sems.at[k], recv_sem=recv_sems.at[k], device_id=to, device_id_type=MESH)

        mine = pltpu.make_async_copy(v_ref, slot(*me), local_sem)
        mine.start()
        first = [copy(0, me, sibling, src=v_ref)]
        first += [copy(1 + j, me, (*chip, c), src=v_ref) for j, chip in enumerate(chips)]
        for cp in first:
            cp.start()
        passed = [copy(4 + j, (*chip, c), sibling) for j, chip in enumerate(chips)]
        for j, chip in enumerate(chips):
            copy(1 + j, (*chip, c), me).wait_recv()
            passed[j].start()
        copy(0, sibling, me).wait_recv()
        for j, chip in enumerate(chips):
            copy(4 + j, (*chip, 1 - c), me).wait_recv()
        for cp in first + passed:
            cp.wait_send()
        mine.wait()

    return pl.pallas_call(
        body, name=name, in_specs=[ANY], out_specs=ANY,
        out_shape=SDS((N_DEV,) + v.shape, v.dtype),
        scratch_shapes=[pltpu.SemaphoreType.DMA((7,)), pltpu.SemaphoreType.DMA((7,)), pltpu.SemaphoreType.DMA],
    )(v)


def _all_to_all(v, name):
    def body(v_ref, out_ref, send_sems, recv_sems, local_sem):
        x, y, c = lax.axis_index("x"), lax.axis_index("y"), lax.axis_index("c")
        mine_idx = 4 * x + 2 * y + c
        mine = pltpu.make_async_copy(v_ref.at[mine_idx], out_ref.at[mine_idx], local_sem)
        mine.start()
        sends, recvs = [], []
        for k in range(1, N_DEV):
            px = 1 - x if k & 4 else x
            py = 1 - y if k & 2 else y
            pc = 1 - c if k & 1 else c
            peer_idx = 4 * px + 2 * py + pc
            sems = dict(send_sem=send_sems.at[k - 1], recv_sem=recv_sems.at[k - 1], device_id=(px, py, pc),
                        device_id_type=MESH)
            sends.append(pltpu.make_async_remote_copy(src_ref=v_ref.at[peer_idx], dst_ref=out_ref.at[mine_idx], **sems))
            recvs.append(pltpu.make_async_remote_copy(src_ref=v_ref.at[peer_idx], dst_ref=out_ref.at[peer_idx], **sems))
        for cp in sends:
            cp.start()
        for cp in recvs:
            cp.wait_recv()
        for cp in sends:
            cp.wait_send()
        mine.wait()

    return pl.pallas_call(
        body, name=name, in_specs=[ANY], out_specs=ANY,
        out_shape=SDS(v.shape, v.dtype),
        scratch_shapes=[pltpu.SemaphoreType.DMA((7,)), pltpu.SemaphoreType.DMA((7,)), pltpu.SemaphoreType.DMA],
    )(v)


HBM_SPEC = pl.BlockSpec(memory_space=pltpu.HBM)
SEM_SPEC = pl.BlockSpec(memory_space=pltpu.SEMAPHORE)
EFFECT = pltpu.SideEffectType.DATAFLOW_SIDE_EFFECTING


EXCHANGE_PEERS = {"gather": range(1, N_DEV), "scatter": range(1, N_DEV), "chip": (1, 2, 4, 6), "pass": (2, 4, 6)}


def _exchange_copies(srcs, lands, send_sems, recv_sems, mode, layer):
    x, y, c = lax.axis_index("x"), lax.axis_index("y"), lax.axis_index("c")
    me = 4 * x + 2 * y + c
    copies = []
    for a, (src, land) in enumerate(zip(srcs, lands)):
        for k in EXCHANGE_PEERS[mode]:
            px = 1 - x if k & 4 else x
            py = 1 - y if k & 2 else y
            pc = 1 - c if k & 1 else c
            peer = 4 * px + 2 * py + pc
            if mode == "scatter":
                s, d, to = src.at[peer], land.at[me, layer], (px, py, pc)
            elif mode == "pass":
                s, d, to = land.at[peer], land.at[peer], (x, y, 1 - c)
            else:
                s, d, to = src, land.at[me], (px, py, pc)
            n = 7 * a + k - 1
            copies.append(pltpu.make_async_remote_copy(
                src_ref=s, dst_ref=d, send_sem=send_sems.at[n], recv_sem=recv_sems.at[n], device_id=to,
                device_id_type=MESH))
    return copies


def _exchange_start(name, srcs, lands, mode, layer=0, after=None):
    n = len(srcs)

    def body(*refs):
        send_sems, recv_sems = refs[-2 * n - 3], refs[-2 * n - 2]
        for cp in _exchange_copies(refs[:n], refs[n:2 * n], send_sems, recv_sems, mode, layer):
            cp.start()
        refs[-1][...] = jnp.zeros_like(refs[-1])

    arrays = list(srcs) + list(lands)
    sems = pltpu.SemaphoreType.DMA((7 * n,))
    out = pl.pallas_call(
        body, name=name,
        out_shape=(sems, sems, *[pltpu.HBM(v.shape, v.dtype) for v in arrays], SDS((8, LANE), F32)),
        in_specs=[HBM_SPEC] * (2 * n) + ([ANY] if after is not None else []),
        out_specs=(SEM_SPEC, SEM_SPEC, *[HBM_SPEC] * (2 * n), pl.BlockSpec(memory_space=pltpu.VMEM)),
        input_output_aliases={i: 2 + i for i in range(2 * n)},
        compiler_params=pltpu.CompilerParams(has_side_effects=EFFECT),
    )(*[pltpu.with_memory_space_constraint(v, pltpu.HBM) for v in arrays], *([after] if after is not None else []))
    return dict(sems=out[:2], srcs=out[2:2 + n], lands=out[2 + n:2 + 2 * n], token=out[-1][0, 0], mode=mode,
                layer=layer)


def _exchange_wait(name, st, after, also=()):
    n = len(st["srcs"])

    def body(*refs):
        send_sems, recv_sems = refs[2 * n], refs[2 * n + 1]
        for cp in _exchange_copies(refs[:n], refs[n:2 * n], send_sems, recv_sems, st["mode"], st["layer"]):
            cp.wait_send()
            cp.wait_recv()

    arrays = list(st["srcs"]) + list(st["lands"])
    out = pl.pallas_call(
        body, name=name,
        out_shape=tuple(pltpu.HBM(v.shape, v.dtype) for v in arrays),
        in_specs=[HBM_SPEC] * (2 * n) + [SEM_SPEC, SEM_SPEC] + [ANY] * (1 + len(also)),
        out_specs=tuple([HBM_SPEC] * (2 * n)),
        input_output_aliases={i: i for i in range(2 * n)},
        compiler_params=pltpu.CompilerParams(has_side_effects=EFFECT),
    )(*arrays, *st["sems"], after, *also)
    st["srcs"] = out[:n]
    return out[n:]


_IN_PIECES = ([(1024, 3072)]
              + [r for t in range(4) for r in ((LANE * t, LANE * (t + 1)), (512 + LANE * t, 512 + LANE * (t + 1)))]
              + [(4096, 5632), (3072, 4096), (5632, 5648)])


def _permute_in(w):
    pad = jnp.zeros(w.shape[:-1] + (N_PAD - N_IN,), w.dtype)
    return jnp.concatenate([w[..., a:b] for a, b in _IN_PIECES] + [pad], axis=-1)


def _unpermute_in(g):
    ax = [g[..., OFF_LRU + 2 * LANE * t:OFF_LRU + 2 * LANE * t + LANE] for t in range(4)]
    ag = [g[..., OFF_LRU + 2 * LANE * t + LANE:OFF_LRU + 2 * LANE * (t + 1)] for t in range(4)]
    return jnp.concatenate(ax + ag + [g[..., 0:2048], g[..., OFF_Z:OFF_Z + SSD_W], g[..., OFF_XBC:OFF_XBC + SSD_CONV],
                                      g[..., OFF_Z + SSD_W:OFF_Z + SSD_W + SSD_HEADS]], axis=-1)


SHARD_COLS = N_IN // N_DEV


def _in_segments():
    segs, pos = [], 0
    for a, b in _IN_PIECES:
        for i in range(N_DEV):
            lo, hi = max(a, SHARD_COLS * i), min(b, SHARD_COLS * (i + 1))
            if lo < hi:
                segs.append((i, lo - SHARD_COLS * i, hi - lo, pos + lo - a))
        pos += b - a
    return segs


RELAYOUT_ROWS = 256


def _relayout_in(land, own):
    def body(land_ref, own_ref, out_ref):
        me = 4 * lax.axis_index("x") + 2 * lax.axis_index("y") + lax.axis_index("c")
        out_ref[:, N_IN:N_PAD] = jnp.zeros((RELAYOUT_ROWS, N_PAD - N_IN), BF16)
        for i, j, wd, p in _in_segments():
            out_ref[:, p:p + wd] = jnp.where(me == i, own_ref[:, j:j + wd], land_ref[i, :, j:j + wd])

    return pl.pallas_call(
        body, name="relayout_in", grid=(D_MODEL // RELAYOUT_ROWS,),
        in_specs=[pl.BlockSpec((N_DEV, RELAYOUT_ROWS, SHARD_COLS), lambda r: (0, r, 0)),
                  pl.BlockSpec((RELAYOUT_ROWS, SHARD_COLS), lambda r: (r, 0))],
        out_specs=pl.BlockSpec((RELAYOUT_ROWS, N_PAD), lambda r: (r, 0)),
        out_shape=SDS((D_MODEL, N_PAD), BF16),
        compiler_params=_cp(("parallel",)),
    )(land, own)


def _relayout_grad(g):
    def body(g_ref, out_ref):
        for i, j, wd, p in _in_segments():
            out_ref[i, :, j:j + wd] = g_ref[:, p:p + wd].astype(BF16)

    return pl.pallas_call(
        body, name="relayout_grad", grid=(D_MODEL // RELAYOUT_ROWS,),
        in_specs=[pl.BlockSpec((RELAYOUT_ROWS, N_PAD), lambda r: (r, 0))],
        out_specs=pl.BlockSpec((N_DEV, RELAYOUT_ROWS, SHARD_COLS), lambda r: (0, r, 0)),
        out_shape=SDS((N_DEV, D_MODEL, SHARD_COLS), BF16),
        compiler_params=_cp(("parallel",)),
    )(g)


def _block_diag(w):
    w4 = w.reshape(4, 2, 64, 64)
    z = jnp.zeros((4, 64, 64), w.dtype)
    top = jnp.concatenate([w4[:, 0], z], axis=-1)
    bot = jnp.concatenate([z, w4[:, 1]], axis=-1)
    return jnp.concatenate([top, bot], axis=1).astype(BF16)


def _diag_blocks(g):
    return jnp.stack([g[:, :64, :64], g[:, 64:, 64:]], axis=1).reshape(8, 64, 64)


def _pad_lanes(v):
    return jnp.pad(v, (0, LANE - v.shape[0]))[None, :]


def _lower_bounds(logits):
    p = jax.nn.softmax(logits, axis=0)
    return p, jnp.cumsum(p, axis=0) - p[0]


def _lower_bounds_bwd(p, dlb):
    dp = jnp.cumsum(dlb[::-1], axis=0)[::-1]
    dp = dp.at[0].add(-jnp.sum(dlb, axis=0))
    return p * (dp - jnp.sum(dp * p, axis=0, keepdims=True))


SMALL = ["norm_w", "b_ada", "lru_conv_b", "lru_wa", "lru_ba", "lru_wx", "lru_bx", "lru_lambda", "hg_lb_logits",
         "hg_norm_w", "ssd_conv_b", "ssd_dt_bias", "ssd_a_log", "ssd_d", "ssd_norm_w", "final_norm_w"]
WEIGHTS = ["norm_w", "w_ada", "b_ada", "w_in", "lru_conv_w", "lru_conv_b", "lru_wa", "lru_ba", "lru_wx", "lru_bx",
           "lru_lambda", "hg_lb_logits", "hg_norm_w", "ssd_conv_w", "ssd_conv_b", "ssd_dt_bias", "ssd_a_log", "ssd_d",
           "ssd_norm_w", "w_out", "final_norm_w"]
INPUTS = ["x", "c"] + WEIGHTS + ["loss_target"] + ["m_" + n for n in WEIGHTS] + ["v_" + n for n in WEIGHTS]
SMALL_ROW = 1024


def _small_rows(like):
    out, off = {}, 0
    for n in SMALL:
        rows = -(-int(np.prod(like[n].shape)) // (8 * SMALL_ROW)) * 8
        out[n] = (off, rows)
        off += rows
    return out, off


def _flatten_small(d, prefix="", last=0.0):
    table, _ = _small_rows({n: d[prefix + n] for n in SMALL})
    pieces = []
    for n in SMALL:
        flat = d[prefix + n].reshape(-1)
        pieces.append(jnp.pad(flat, (0, table[n][1] * SMALL_ROW - flat.shape[0])).reshape(-1, SMALL_ROW))
    return jnp.concatenate(pieces + [jnp.full((8, SMALL_ROW), last, F32)], axis=0)


def _split_small(packed, like):
    table, _ = _small_rows(like)
    out = {}
    for n in SMALL:
        off, rows = table[n]
        size = int(np.prod(like[n].shape))
        out[n] = packed[off:off + rows].reshape(-1)[:size].reshape(like[n].shape)
    return out


def _local_step(x, mod, target, w, fetch, emit):
    S = x.shape[0]
    mall = _bfc(_hg_consts())
    mall_t = _bfc(_hg_consts().T)
    consts = _ssd_consts()
    p_lb, lbs = _lower_bounds(w["hg_lb_logits"])
    saved = []
    for l in range(DEPTH):
        w_in_l, w_out_l, token = fetch(l, x)
        shift, scale, gate = (mod[l:l + 1, k * D_MODEL:(k + 1) * D_MODEL] for k in range(3))
        shift = shift + token
        prm = dict(
            nw=w["norm_w"][l:l + 1], cw=w["lru_conv_w"][l], cb=w["lru_conv_b"][l:l + 1],
            wa=_block_diag(w["lru_wa"][l]), ba=w["lru_ba"][l].reshape(1, LRU_W),
            wx=_block_diag(w["lru_wx"][l]), bx=w["lru_bx"][l].reshape(1, LRU_W), lam=w["lru_lambda"][l:l + 1],
            lb=lbs[l:l + 1], hnw=w["hg_norm_w"][l:l + 1], scw=w["ssd_conv_w"][l], scb=w["ssd_conv_b"][l:l + 1],
            bias=_pad_lanes(w["ssd_dt_bias"][l]), alog=_pad_lanes(w["ssd_a_log"][l]),
            dskip=jnp.repeat(w["ssd_d"][l], SSD_P)[None, :], snw=w["ssd_norm_w"][l:l + 1],
            w_in=w_in_l, w_out=w_out_l, scale=scale, gate=gate)
        u, h = _inproj_fwd(x, prm["nw"], scale, shift, prm["w_in"])
        ycat = lax.empty((S, D_INNER), BF16)
        lru_args = (u, prm["cw"], prm["cb"], prm["wa"], prm["ba"], prm["wx"], prm["bx"], prm["lam"])
        ycat, h_lru = _lru_fwd(*lru_args, ycat)
        ycat, o_b, hg_st = _hg_fwd(u, prm["lb"], prm["hnw"], mall, ycat)
        xbc = _ssdconv_fwd(u, prm["scw"], prm["scb"])
        ssd_args = (u, xbc, prm["bias"], prm["alog"], prm["dskip"], prm["snw"], consts)
        ycat, y_ssd, ssd_st = _ssd_fwd(*ssd_args, ycat)
        x_new, y = _outproj_fwd(ycat, prm["w_out"], x, gate)
        saved.append((prm, x, u, h, ycat, lru_args, h_lru, o_b, hg_st, ssd_args, y_ssd, ssd_st, y))
        x = x_new
    dx, red = _loss_head(x, w["final_norm_w"][None, :], target)
    loss = red[1, 0]
    g = {n: [None] * DEPTH for n in WEIGHTS}
    g["final_norm_w"] = red[0]
    dmod, dlb = [None] * DEPTH, [None] * DEPTH
    for l in reversed(range(DEPTH)):
        prm, x, u, h, ycat, lru_args, h_lru, o_b, hg_st, ssd_args, y_ssd, ssd_st, y = saved[l]
        dycat, g_out, dgate = _outproj_bwd(dx, y, prm["gate"], ycat, prm["w_out"])
        token = emit(l, "w_out", g_out)
        du = lax.empty((S, N_PAD), BF16)
        ssd_args = ssd_args[:5] + (ssd_args[5] + token,) + ssd_args[6:]
        du, dxbc, sred = _ssd_bwd(*ssd_args, y_ssd, ssd_st, dycat, du)
        du, cred = _ssdconv_bwd(u, prm["scw"], prm["scb"], dxbc, du)
        du, hred = _hg_bwd(u, prm["lb"], prm["hnw"], mall, mall_t, o_b, hg_st, dycat, du)
        du, lred, gwa, gwx = _lru_bwd(*lru_args, h_lru, dycat, du)
        token = emit(l, "w_in", _inproj_bwd_w(h, du))
        dx, ired = _inproj_bwd_x(du, prm["w_in"], x, prm["nw"], prm["scale"] + token, dx)
        g["norm_w"][l] = ired[2]
        dmod[l] = jnp.concatenate([ired[0], ired[1], dgate[0]])
        g["lru_conv_w"][l], g["lru_conv_b"][l] = lred[0:4], lred[4]
        g["lru_ba"][l], g["lru_bx"][l], g["lru_lambda"][l] = lred[5].reshape(8, 64), lred[6].reshape(8, 64), lred[7]
        g["lru_wa"][l], g["lru_wx"][l] = _diag_blocks(gwa), _diag_blocks(gwx)
        g["hg_norm_w"][l], dlb[l] = hred[0], hred[1]
        g["ssd_conv_w"][l], g["ssd_conv_b"][l] = cred[0:4], cred[4]
        g["ssd_norm_w"][l] = sred[0]
        g["ssd_d"][l] = sred[1].reshape(SSD_HEADS, SSD_P).sum(-1)
        g["ssd_dt_bias"][l] = sred[2, :SSD_HEADS]
        g["ssd_a_log"][l] = -sred[3, :SSD_HEADS] * jnp.exp(w["ssd_a_log"][l])
    g["hg_lb_logits"] = _lower_bounds_bwd(p_lb, jnp.stack(dlb))
    for n in WEIGHTS:
        if isinstance(g[n], list) and g[n][0] is not None:
            g[n] = jnp.stack(g[n])
    return loss, dx, jnp.stack(dmod), g


def kernel(x, c, norm_w, w_ada, b_ada, w_in, lru_conv_w, lru_conv_b, lru_wa, lru_ba, lru_wx, lru_bx, lru_lambda, hg_lb_logits, hg_norm_w, ssd_conv_w, ssd_conv_b, ssd_dt_bias, ssd_a_log, ssd_d, ssd_norm_w, w_out, final_norm_w, loss_target, m_norm_w, m_w_ada, m_b_ada, m_w_in, m_lru_conv_w, m_lru_conv_b, m_lru_wa, m_lru_ba, m_lru_wx, m_lru_bx, m_lru_lambda, m_hg_lb_logits, m_hg_norm_w, m_ssd_conv_w, m_ssd_conv_b, m_ssd_dt_bias, m_ssd_a_log, m_ssd_d, m_ssd_norm_w, m_w_out, m_final_norm_w, v_norm_w, v_w_ada, v_b_ada, v_w_in, v_lru_conv_w, v_lru_conv_b, v_lru_wa, v_lru_ba, v_lru_wx, v_lru_bx, v_lru_lambda, v_hg_lb_logits, v_hg_norm_w, v_ssd_conv_w, v_ssd_conv_b, v_ssd_dt_bias, v_ssd_a_log, v_ssd_d, v_ssd_norm_w, v_w_out, v_final_norm_w):
    return _step(x, c, norm_w, w_ada, b_ada, w_in, lru_conv_w, lru_conv_b, lru_wa, lru_ba, lru_wx, lru_bx, lru_lambda, hg_lb_logits, hg_norm_w, ssd_conv_w, ssd_conv_b, ssd_dt_bias, ssd_a_log, ssd_d, ssd_norm_w, w_out, final_norm_w, loss_target, m_norm_w, m_w_ada, m_b_ada, m_w_in, m_lru_conv_w, m_lru_conv_b, m_lru_wa, m_lru_ba, m_lru_wx, m_lru_bx, m_lru_lambda, m_hg_lb_logits, m_hg_norm_w, m_ssd_conv_w, m_ssd_conv_b, m_ssd_dt_bias, m_ssd_a_log, m_ssd_d, m_ssd_norm_w, m_w_out, m_final_norm_w, v_norm_w, v_w_ada, v_b_ada, v_w_in, v_lru_conv_w, v_lru_conv_b, v_lru_wa, v_lru_ba, v_lru_wx, v_lru_bx, v_lru_lambda, v_hg_lb_logits, v_hg_norm_w, v_ssd_conv_w, v_ssd_conv_b, v_ssd_dt_bias, v_ssd_a_log, v_ssd_d, v_ssd_norm_w, v_w_out, v_final_norm_w)


def _step(*args):
    a = dict(zip(INPUTS, args, strict=True))
    me = 4 * lax.axis_index("x") + 2 * lax.axis_index("y") + lax.axis_index("c")
    x, target = a["x"][0], a["loss_target"][0]

    c_all = _all_gather(a["c"], "gather_c")[:, 0, :]
    b_cols = lax.dynamic_slice_in_dim(a["b_ada"], me * ADA_COLS, ADA_COLS, axis=1)[:, None, :]
    mod_parts = _all_gather(_ada_fwd(c_all, a["w_ada"], b_cols), "gather_mod")
    mod = lax.dynamic_index_in_dim(mod_parts, me, axis=2, keepdims=False)
    mod = mod.transpose(1, 0, 2).reshape(DEPTH, 3 * D_MODEL)

    w = {n: a[n] for n in SMALL}

    w_in_b, w_out_b = a["w_in"].astype(BF16), a["w_out"].astype(BF16)
    conv_own = jnp.concatenate([a["lru_conv_w"], a["ssd_conv_w"]], axis=-1)
    cols, rows_out = N_IN // N_DEV, D_INNER // N_DEV

    def gather_start(l, after):
        srcs = [w_in_b[l], w_out_b[l]] + ([conv_own] if l == 0 else [])
        lands = [lax.empty((N_DEV,) + s.shape, s.dtype) for s in srcs]
        return _exchange_start(f"gather_start_{l}", srcs, lands, "chip" if l == 0 else "gather", after=after)

    def gather_pass(name, st, after, also=()):
        landed = _exchange_wait(name + "_wait", st, after, also)
        st2 = _exchange_start(name + "_pass", st["srcs"], landed, "pass")
        return _exchange_wait(name + "_passed", st2, after)

    gathers = {0: gather_start(0, mod)}

    def fetch(l, x_l):
        if l == 0:
            landed = gather_pass("gather_0", gathers[0], x_l, also=(a["m_w_in"], a["v_w_in"]))
        else:
            landed = _exchange_wait(f"gather_wait_{l}", gathers[l], x_l)
        land_out = lax.dynamic_update_index_in_dim(landed[1], w_out_b[l], me, 0)
        if l == 0:
            conv = lax.dynamic_update_index_in_dim(landed[2], conv_own, me, 0).transpose(1, 2, 0, 3)
            w["lru_conv_w"] = conv[..., :64].reshape(DEPTH, 4, LRU_W)
            w["ssd_conv_w"] = conv[..., 64:].reshape(DEPTH, 4, SSD_CONV)
        token = 0.0
        if l + 1 < DEPTH:
            gathers[l + 1] = gather_start(l + 1, land_out)
            token = gathers[l + 1]["token"]
        return _relayout_in(landed[0], w_in_b[l]), land_out.reshape(D_INNER, D_MODEL), token

    scatters = {"w_in": {}, "w_out": {}}
    lands = {"w_in": lax.empty((N_DEV, DEPTH, D_MODEL, cols), BF16),
             "w_out": lax.empty((N_DEV, DEPTH, rows_out, D_MODEL), BF16)}
    own = {"w_in": [None] * DEPTH, "w_out": [None] * DEPTH}

    def emit(l, name, grad):
        grad = _relayout_grad(grad) if name == "w_in" else grad.reshape(N_DEV, rows_out, D_MODEL)
        own[name][l] = lax.dynamic_index_in_dim(grad, me, 0, keepdims=False)
        st = _exchange_start(f"scatter_start_{name}_{l}", [grad], [lands[name]], "scatter", layer=l)
        scatters[name][l] = st
        lands[name] = st["lands"][0]
        return st["token"]

    loss_own, dx, dmod, g = _local_step(x, mod, target, w, fetch, emit)

    def sharded(name, parts, own=None, **kw):
        return _adamw(parts, a[name], a["m_" + name], a["v_" + name], "adamw_" + name + kw.pop("tag", ""), own=own, **kw)

    g["b_ada"] = dmod
    small_own = _flatten_small(g, last=loss_own)
    small_st = _exchange_start("gather_small", [small_own], [lax.empty((N_DEV,) + small_own.shape, F32)], "chip",
                               after=dx)
    big = {}
    after = small_st["token"] + dx[0:8, 0:LANE]
    for name in ("w_out", "w_in"):
        own_all = jnp.stack(own[name])
        for l in reversed(range(1, DEPTH)):
            scatters[name][l]["lands"] = [lands[name]]
            lands[name] = _exchange_wait(f"scatter_wait_{name}_{l}", scatters[name][l], after)[0]
        upper = sharded(name, lands[name], own_all, layers=(1, DEPTH), tag="_upper")
        scatters[name][0]["lands"] = [lands[name]]
        lands[name] = _exchange_wait(f"scatter_wait_{name}_0", scatters[name][0], upper[1])[0]
        big[name] = sharded(name, lands[name], own_all, layers=(0, 1), prev=upper)
        after = big[name][1]
    small = gather_pass("gather_small", small_st, after)[0]
    outs = _adamw(small[:, None], *[_flatten_small(a, p)[None] for p in ("", "m_", "v_")], "adamw_small",
                  own=small_own[None])
    res = [_split_small(o[0], a) for o in outs]
    losses = lax.dynamic_update_index_in_dim(small[:, -1, 0], loss_own, me, 0)
    loss = jnp.sum(losses)

    off = _small_rows(a)[0]["b_ada"][0]
    dmod_all = lax.dynamic_update_index_in_dim(small[:, off:off + DEPTH * 3 * D_MODEL // SMALL_ROW],
                                               dmod.reshape(-1, SMALL_ROW), me, 0)
    dmod_all = dmod_all.reshape(N_DEV, DEPTH, 3 * D_MODEL).transpose(1, 0, 2)
    dmod_cols = lax.dynamic_slice_in_dim(dmod_all, me * ADA_COLS, ADA_COLS, axis=2)
    dmod_pad = jnp.pad(dmod_cols, ((0, 0), (0, LANE - N_DEV), (0, 0)))
    ct_pad = jnp.pad(c_all.T, ((0, 0), (0, LANE - N_DEV)))
    big["w_ada"] = sharded("w_ada", _ada_bwd(ct_pad, dmod_pad)[None])
    g_conv = jnp.concatenate([g["lru_conv_w"].reshape(DEPTH, 4, N_DEV, 64), g["ssd_conv_w"].reshape(DEPTH, 4, N_DEV, 192)],
                             axis=-1).transpose(2, 0, 1, 3)
    conv_parts = _all_to_all(g_conv, "scatter_conv")
    big["lru_conv_w"] = sharded("lru_conv_w", conv_parts[..., :64])
    big["ssd_conv_w"] = sharded("ssd_conv_w", conv_parts[..., 64:])

    out = [loss, dx[None]]
    for k in range(4):
        out += [big[n][k] if n in big else res[k][n] for n in WEIGHTS]
    return tuple(out)
```

```python
import functools

import numpy as np
import jax
import jax.numpy as jnp
from jax import lax
from jax.experimental import pallas as pl
from jax.experimental.pallas import tpu as pltpu

F32 = jnp.float32
BF16 = jnp.bfloat16
SDS = jax.ShapeDtypeStruct

N_DEV = 8
DEPTH = 4
D_MODEL = 1024
D_INNER = 2048
EPS = 1e-6
LRU_W = 512
LRU_C = 8.0
HG_W = 512
HG_CHUNK = 64
HG_HEADS = 4
SSD_W = 1024
SSD_HEADS = 16
SSD_P = 64
SSD_N = 128
SSD_CHUNK = 128
SSD_CONV = 1536
N_IN = 5648
N_PAD = 5760
OFF_HG = 0
OFF_LRU = 2048
OFF_XBC = 3072
OFF_Z = 4608
LANE = 128
VMEM_LIMIT = 56 * 1024 * 1024
NEG = -1e30

ADAM_LR = 0.001
ADAM_B1 = 0.9
ADAM_B2 = 0.999
ADAM_EPS = 1e-08
ADAM_WD = 0.01
ADAM_STEP = 10


def _cp(sem=None):
    return pltpu.CompilerParams(dimension_semantics=sem, vmem_limit_bytes=VMEM_LIMIT)


def _dg(a, b, ca, cb):
    return lax.dot_general(a, b, (((ca,), (cb,)), ((), ())), preferred_element_type=F32)


def _mm(a, b):
    return _dg(a, b, 1, 0)


def _mm_nt(a, b):
    return _dg(a, b, 1, 1)


def _mm_tn(a, b):
    return _dg(a, b, 0, 0)


def _bf(x):
    return x.astype(BF16)


def _split3(x):
    hi = x.astype(BF16)
    r = x - hi.astype(F32)
    mid = r.astype(BF16)
    lo = (r - mid.astype(F32)).astype(BF16)
    return hi, mid, lo


def _sel_r(x, m):
    hi, mid, lo = _split3(x)
    return _mm(hi, m) + _mm(mid, m) + _mm(lo, m)


def _sel_l(m, x):
    hi, mid, lo = _split3(x)
    return _mm(m, hi) + _mm(m, mid) + _mm(m, lo)


def _sel_l2(m, x):
    hi = x.astype(BF16)
    lo = (x - hi.astype(F32)).astype(BF16)
    return _mm(m, hi) + _mm(m, lo)


def _sel_tn(x, m):
    hi, mid, lo = _split3(x)
    return _mm_tn(hi, m) + _mm_tn(mid, m) + _mm_tn(lo, m)


def _sigmoid(x):
    return 1.0 / (1.0 + jnp.exp(-x))


def _silu(x):
    return x * _sigmoid(x)


def _dsilu(x):
    s = _sigmoid(x)
    return s * (1.0 + x * (1.0 - s))


def _softplus(x):
    return jnp.maximum(x, 0.0) + jnp.log(1.0 + jnp.exp(-jnp.abs(x)))


def _expm1(z):
    series = z * (1.0 + z * (1.0 / 2) * (1.0 + z * (1.0 / 3) * (1.0 + z * (1.0 / 4) * (
        1.0 + z * (1.0 / 5) * (1.0 + z * (1.0 / 6) * (1.0 + z * (1.0 / 7)))))))
    return jnp.where(jnp.abs(z) < 0.3, series, jnp.exp(z) - 1.0)


def _iota(shape, dim):
    return lax.broadcasted_iota(jnp.int32, shape, dim)


def _last_row(x, rows):
    return jnp.sum(jnp.where(rows == x.shape[0] - 1, x, 0.0), axis=0, keepdims=True)


def _shift_down(x, d, rows, fill=0.0):
    return jnp.where(rows >= d, pltpu.roll(x, d, 0), fill)


def _shift_up(x, d, rows, fill=0.0):
    n = x.shape[0]
    return jnp.where(rows < n - d, pltpu.roll(x, n - d, 0), fill)


def _conv_fwd(x, cw_ref, cb_ref, rows):
    out = cb_ref[...] + cw_ref[pl.ds(3, 1), :] * x
    for k in range(3):
        out = out + cw_ref[pl.ds(k, 1), :] * _shift_down(x, 3 - k, rows)
    return out


def _conv_bwd(x, dco, cw_ref, rows):
    dx = cw_ref[pl.ds(3, 1), :] * dco
    dws = []
    for k in range(3):
        dx = dx + cw_ref[pl.ds(k, 1), :] * _shift_up(dco, 3 - k, rows)
        dws.append(jnp.sum(dco * _shift_down(x, 3 - k, rows), axis=0, keepdims=True))
    dws.append(jnp.sum(dco * x, axis=0, keepdims=True))
    return dx, dws, jnp.sum(dco, axis=0, keepdims=True)


def _vec(n):
    return pl.BlockSpec((1, n), lambda *_: (0, 0))


def _full(shape):
    nd = len(shape)
    return pl.BlockSpec(shape, lambda *_: (0,) * nd)


def _inproj_fwd(x, nw, scale, shift, w):
    S = x.shape[0]
    tm = min(256, S)

    def body(x_ref, nw_ref, sc_ref, sh_ref, w_ref, u_ref, h_ref):
        xv = x_ref[...]
        inv = lax.rsqrt(jnp.mean(xv * xv, axis=-1, keepdims=True) + EPS)
        h = ((xv * inv) * nw_ref[...] * (1.0 + sc_ref[...]) + sh_ref[...]).astype(BF16)
        h_ref[...] = h
        u_ref[...] = _mm(h, w_ref[...])

    return pl.pallas_call(
        body, name="inproj_fwd", grid=(S // tm,),
        in_specs=[pl.BlockSpec((tm, D_MODEL), lambda i: (i, 0)), _vec(D_MODEL), _vec(D_MODEL), _vec(D_MODEL),
                  _full((D_MODEL, N_PAD))],
        out_specs=[pl.BlockSpec((tm, N_PAD), lambda i: (i, 0)), pl.BlockSpec((tm, D_MODEL), lambda i: (i, 0))],
        out_shape=[SDS((S, N_PAD), F32), SDS((S, D_MODEL), BF16)],
        compiler_params=_cp(("parallel",)),
    )(x, nw, scale, shift, w)


def _inproj_bwd_x(du, w, x, nw, scale, dxn):
    S = x.shape[0]
    tm = min(256, S)

    def body(du_ref, w_ref, x_ref, nw_ref, sc_ref, dxn_ref, dx_ref, red_ref):
        @pl.when(pl.program_id(0) == 0)
        def _():
            red_ref[...] = jnp.zeros_like(red_ref)

        dh = _mm_nt(du_ref[...], w_ref[...])
        xv = x_ref[...]
        inv = lax.rsqrt(jnp.mean(xv * xv, axis=-1, keepdims=True) + EPS)
        xhat = xv * inv
        nwv = nw_ref[...]
        g1 = 1.0 + sc_ref[...]
        dxhat = dh * nwv * g1
        dx = inv * (dxhat - xhat * jnp.mean(dxhat * xhat, axis=-1, keepdims=True))
        dx_ref[...] = dxn_ref[...] + dx
        red_ref[0:1, :] += jnp.sum(dh, axis=0, keepdims=True)
        red_ref[1:2, :] += jnp.sum(dh * xhat * nwv, axis=0, keepdims=True)
        red_ref[2:3, :] += jnp.sum(dh * xhat * g1, axis=0, keepdims=True)

    row = pl.BlockSpec((tm, D_MODEL), lambda i: (i, 0))
    return pl.pallas_call(
        body, name="inproj_bwd_x", grid=(S // tm,),
        in_specs=[pl.BlockSpec((tm, N_PAD), lambda i: (i, 0)), _full((D_MODEL, N_PAD)), row, _vec(D_MODEL),
                  _vec(D_MODEL), row],
        out_specs=[row, _full((8, D_MODEL))],
        out_shape=[SDS((S, D_MODEL), F32), SDS((8, D_MODEL), F32)],
        compiler_params=_cp(("arbitrary",)),
    )(du, w, x, nw, scale, dxn)


def _inproj_bwd_w(h, du):
    S = h.shape[0]
    tn = 640

    def body(h_ref, du_ref, gw_ref):
        gw_ref[...] = _mm_tn(h_ref[...], _bf(du_ref[...]))

    return pl.pallas_call(
        body, name="inproj_bwd_w", grid=(N_PAD // tn,),
        in_specs=[_full((S, D_MODEL)), pl.BlockSpec((S, tn), lambda j: (0, j))],
        out_specs=pl.BlockSpec((D_MODEL, tn), lambda j: (0, j)),
        out_shape=SDS((D_MODEL, N_PAD), F32),
        compiler_params=_cp(("parallel",)),
    )(h, du)


def _scan_block(a, b, rows):
    d = 1
    while d < a.shape[0]:
        a_s = _shift_down(a, d, rows, 1.0)
        b_s = _shift_down(b, d, rows, 0.0)
        b = a * b_s + b
        a = a * a_s
        d *= 2
    return a, b


def _rscan_block(c, g, rows):
    d = 1
    while d < c.shape[0]:
        c_s = _shift_up(c, d, rows, 1.0)
        g_s = _shift_up(g, d, rows, 0.0)
        g = g + c * g_s
        c = c * c_s
        d *= 2
    return c, g


SUBLANES = 8


def _group_expand(S):
    return (np.arange(S)[:, None] // SUBLANES == np.arange(S // SUBLANES)[None, :]).astype(np.float32)


def _scan_full(a, b, a_scr, b_scr, expand, rows):
    S = a.shape[0]
    for d in (1, 2, 4):
        inside = (rows & (SUBLANES - 1)) >= d
        a_s = jnp.where(inside, pltpu.roll(a, d, 0), 1.0)
        b_s = jnp.where(inside, pltpu.roll(b, d, 0), 0.0)
        b = a * b_s + b
        a = a * a_s
    a_scr[...] = a
    b_scr[...] = b
    groups = S // SUBLANES
    rows_g = _iota((groups, LANE), 0)
    _, hg = _scan_block(a_scr[pl.ds(SUBLANES - 1, groups, stride=SUBLANES), :],
                        b_scr[pl.ds(SUBLANES - 1, groups, stride=SUBLANES), :], rows_g)
    return b + a * _sel_l(expand, _shift_down(hg, 1, rows_g))


def _rscan_full(c, g, c_scr, g_scr, expand, rows):
    S = c.shape[0]
    for d in (1, 2, 4):
        inside = (rows & (SUBLANES - 1)) < SUBLANES - d
        c_s = jnp.where(inside, pltpu.roll(c, S - d, 0), 1.0)
        g_s = jnp.where(inside, pltpu.roll(g, S - d, 0), 0.0)
        g = g + c * g_s
        c = c * c_s
    c_scr[...] = c
    g_scr[...] = g
    groups = S // SUBLANES
    rows_g = _iota((groups, LANE), 0)
    _, lg = _rscan_block(c_scr[pl.ds(0, groups, stride=SUBLANES), :], g_scr[pl.ds(0, groups, stride=SUBLANES), :], rows_g)
    return g + c * _sel_l(expand, _shift_up(lg, 1, rows_g))


def _lru_gates(xa, wa_ref, ba_ref, wx_ref, bx_ref, lam_ref):
    sp = _softplus(-lam_ref[...])
    xb = _bf(xa)
    r = _sigmoid(_mm(xb, wa_ref[...]) + ba_ref[...])
    ig = _sigmoid(_mm(xb, wx_ref[...]) + bx_ref[...])
    la = -LRU_C * r * sp
    a = jnp.exp(la)
    mult = jnp.sqrt(-_expm1(2.0 * la))
    return sp, r, ig, la, a, mult


def _lru_specs(S):
    t128 = pl.BlockSpec((1, LANE), lambda t: (0, t))
    return [pl.BlockSpec((S, 2 * LANE), lambda t: (0, OFF_LRU // (2 * LANE) + t)),
            pl.BlockSpec((4, LANE), lambda t: (0, t)), t128,
            pl.BlockSpec((None, LANE, LANE), lambda t: (t, 0, 0)), t128,
            pl.BlockSpec((None, LANE, LANE), lambda t: (t, 0, 0)), t128, t128]


def _lru_fwd(u, cw, cb, wa, ba, wx, bx, lam, ycat):
    S = u.shape[0]
    expand = _bfc(_group_expand(S))

    def body(u_ref, cw_ref, cb_ref, wa_ref, ba_ref, wx_ref, bx_ref, lam_ref, ex_ref, ycat_in, ycat_ref, h_ref,
             a_scr, b_scr):
        del ycat_in
        rows = _iota((S, LANE), 0)
        xa = _conv_fwd(u_ref[:, 0:LANE], cw_ref, cb_ref, rows)
        _, _, ig, _, a, mult = _lru_gates(xa, wa_ref, ba_ref, wx_ref, bx_ref, lam_ref)
        h = _scan_full(a, mult * (ig * xa), a_scr, b_scr, ex_ref[...], rows)
        h_ref[...] = h
        ycat_ref[...] = _bf(h * _silu(u_ref[:, LANE:2 * LANE]))

    col = pl.BlockSpec((S, LANE), lambda t: (0, t))
    return pl.pallas_call(
        body, name="lru_fwd", grid=(LRU_W // LANE,),
        in_specs=_lru_specs(S) + [_full(expand.shape), pl.BlockSpec(memory_space=pl.ANY)],
        out_specs=[col, col],
        out_shape=[SDS((S, D_INNER), BF16), SDS((S,LRU_W), F32)],
        scratch_shapes=[pltpu.VMEM((S, LANE), F32), pltpu.VMEM((S, LANE), F32)],
        input_output_aliases={9: 0},
        compiler_params=_cp(("parallel",)),
    )(u, cw, cb, wa, ba, wx, bx, lam, expand, ycat)


def _lru_bwd(u, cw, cb, wa, ba, wx, bx, lam, h_lru, dycat, du):
    S = u.shape[0]
    expand = _bfc(_group_expand(S))

    def body(u_ref, cw_ref, cb_ref, wa_ref, ba_ref, wx_ref, bx_ref, lam_ref, ex_ref, h_ref, dy_ref, du_in,
             du_ref, red_ref, gwa_ref, gwx_ref, c_scr, g_scr):
        del du_in
        rows = _iota((S, LANE), 0)
        ax = u_ref[:, 0:LANE]
        ag = u_ref[:, LANE:2 * LANE]
        xa = _conv_fwd(ax, cw_ref, cb_ref, rows)
        sp, r, ig, la, a, mult = _lru_gates(xa, wa_ref, ba_ref, wx_ref, bx_ref, lam_ref)
        h = h_ref[...]
        dy = dy_ref[...]
        du_ref[:, LANE:2 * LANE] = _bf(dy * h * _dsilu(ag))
        db = _rscan_full(_shift_up(a, 1, rows, 0.0), dy * _silu(ag), c_scr, g_scr, ex_ref[...], rows)
        da = db * _shift_down(h, 1, rows)
        dmult = db * ig * xa
        dig = db * mult * xa
        dxa = db * mult * ig
        dla = da * a - dmult * (a * a) / mult
        dr = -LRU_C * sp * dla
        dsp = jnp.sum(-LRU_C * r * dla, axis=0, keepdims=True)
        dlam = -dsp * _sigmoid(-lam_ref[...])
        dzr = dr * r * (1.0 - r)
        dzi = dig * ig * (1.0 - ig)
        dzr_b, dzi_b, xa_b = _bf(dzr), _bf(dzi), _bf(xa)
        dxa = dxa + _mm_nt(dzr_b, wa_ref[...]) + _mm_nt(dzi_b, wx_ref[...])
        gwa_ref[...] = _mm_tn(xa_b, dzr_b)
        gwx_ref[...] = _mm_tn(xa_b, dzi_b)
        dax, dws, dcb = _conv_bwd(ax, dxa, cw_ref, rows)
        du_ref[:, 0:LANE] = _bf(dax)
        parts = dws + [dcb, jnp.sum(dzr, axis=0, keepdims=True), jnp.sum(dzi, axis=0, keepdims=True), dlam]
        for n, p in enumerate(parts):
            red_ref[pl.ds(n, 1), :] = p

    col = pl.BlockSpec((S, LANE), lambda t: (0, t))
    gw = pl.BlockSpec((None, LANE, LANE), lambda t: (t, 0, 0))
    return pl.pallas_call(
        body, name="lru_bwd", grid=(LRU_W // LANE,),
        in_specs=_lru_specs(S) + [_full(expand.shape), col, col, pl.BlockSpec(memory_space=pl.ANY)],
        out_specs=[pl.BlockSpec((S, 2 * LANE), lambda t: (0, OFF_LRU // (2 * LANE) + t)),
                   pl.BlockSpec((8, LANE), lambda t: (0, t)), gw, gw],
        out_shape=[SDS((S, N_PAD), BF16), SDS((8, LRU_W), F32), SDS((4, LANE, LANE), F32), SDS((4, LANE, LANE), F32)],
        scratch_shapes=[pltpu.VMEM((S, LANE), F32)] * 2,
        input_output_aliases={11: 0},
        compiler_params=_cp(("parallel",)),
    )(u, cw, cb, wa, ba, wx, bx, lam, expand, h_lru, dycat, du)


HG_LEVELS = 6


def _hg_consts():
    C = HG_CHUNK
    t = np.arange(C)[:, None]
    r = np.arange(C)[None, :]
    mats = []
    for l in range(HG_LEVELS):
        b = 1 << l
        upper = (t % (2 * b)) >= b
        anchor = (t // (2 * b)) * 2 * b + b - 1
        mats.append((upper & (r > anchor) & (r <= t)) | ((~upper) & (r > t) & (r <= anchor)))
    mats.append(r <= t)
    mats.append(r > t)
    return np.concatenate(mats, 0).astype(np.float32)


def _hg_factors(hf, lb, mall):
    s = _sigmoid(hf)
    f = lb + (1.0 - lb) * s
    lf = jnp.log(f)
    k = (1.0 - lb) * _sigmoid(-hf)
    e = jnp.exp(_sel_l(mall, lf))
    C = HG_CHUNK
    eq = [e[l * C:(l + 1) * C] for l in range(HG_LEVELS)]
    ecum = e[HG_LEVELS * C:(HG_LEVELS + 1) * C]
    erem = e[(HG_LEVELS + 1) * C:(HG_LEVELS + 2) * C]
    return s, f, k, eq, eq, ecum, erem


def _hg_masks():
    C = HG_CHUNK
    ri, ci = _iota((C, C), 0), _iota((C, C), 1)
    rr = _iota((C, LANE), 0)
    gm = [(lax.shift_right_logical(ri, l + 1) == lax.shift_right_logical(ci, l + 1)).astype(F32)
          for l in range(HG_LEVELS)]
    up = [(lax.shift_right_logical(rr, l) & 1) == 1 for l in range(HG_LEVELS)]
    eye = (ri == ci).astype(F32)
    return gm, up, eye, rr


def _hg_scores(qh, kh, eq, ek, sl, gm, up, eye):
    qs, ks = [], []
    p = _mm_nt(_bf(qh), _bf(kh)) * eye
    for l in range(HG_LEVELS):
        ql = jnp.where(up[l], qh * eq[l][:, sl], 0.0)
        kl = jnp.where(up[l], 0.0, kh * ek[l][:, sl])
        p = p + _mm_nt(_bf(ql), _bf(kl)) * gm[l]
        qs.append(ql)
        ks.append(kl)
    return p, qs, ks


HG_SUB = 2


def _hg_fwd(u, lb, nw, mall, ycat):
    S = u.shape[0]
    C = HG_CHUNK
    n = S // C
    rows = HG_SUB * C

    def body(u_ref, lb_ref, nw_ref, mall_ref, ycat_in, ycat_ref, o_ref, st_ref, st):
        del ycat_in

        @pl.when(pl.program_id(0) == 0)
        def _():
            st[...] = jnp.zeros_like(st)

        gm, up, eye, rr = _hg_masks()
        for sub in range(HG_SUB):
            r = slice(sub * C, (sub + 1) * C)
            q = _silu(u_ref[r, 0:512])
            v = u_ref[r, 1024:1536]
            _, _, k, eq, ek, ecum, erem = _hg_factors(u_ref[r, 512:1024], lb_ref[...], mall_ref[...])
            for h in range(HG_HEADS):
                sl = slice(h * LANE, (h + 1) * LANE)
                qh, kh, vh = q[:, sl], k[:, sl], _bf(v[:, sl])
                p, _, _ = _hg_scores(qh, kh, eq, ek, sl, gm, up, eye)
                sth = st[h]
                st_ref[sub, h] = sth
                o_ref[r, sl] = _mm(_bf(p), vh) + _mm_nt(_bf(qh * ecum[:, sl]), _bf(sth))
                st[h] = sth * _last_row(ecum[:, sl], rr) + _mm_tn(vh, _bf(kh * erem[:, sl]))
            o = o_ref[r, :]
            inv = lax.rsqrt(jnp.mean(o * o, axis=-1, keepdims=True) + EPS)
            ycat_ref[r, :] = _bf((o * inv) * nw_ref[...] * _silu(u_ref[r, 1536:2048]))

    return pl.pallas_call(
        body, name="hg_fwd", grid=(n // HG_SUB,),
        in_specs=[pl.BlockSpec((rows, 2048), lambda i: (i, 0)), _vec(HG_W), _vec(HG_W), _full(mall.shape),
                  pl.BlockSpec(memory_space=pl.ANY)],
        out_specs=[pl.BlockSpec((rows, HG_W), lambda i: (i, 1)), pl.BlockSpec((rows, HG_W), lambda i: (i, 0)),
                   pl.BlockSpec((HG_SUB, HG_HEADS, LANE, LANE), lambda i: (i, 0, 0, 0))],
        out_shape=[SDS((S, D_INNER), BF16), SDS((S,HG_W), F32), SDS((n, HG_HEADS, LANE, LANE), F32)],
        scratch_shapes=[pltpu.VMEM((HG_HEADS, LANE, LANE), F32)],
        input_output_aliases={4: 0},
        compiler_params=_cp(("arbitrary",)),
    )(u, lb, nw, mall, ycat)


def _hg_bwd(u, lb, nw, mall, mall_t, o_b, states, dycat, du):
    S = u.shape[0]
    C = HG_CHUNK
    n = S // C
    nb = n // HG_SUB
    rows = HG_SUB * C
    L2 = HG_LEVELS

    def body(u_ref, lb_ref, nw_ref, mall_ref, mallt_ref, o_ref, st_ref, dy_ref, du_in, du_ref, red_ref,
             dst, dlast_s, dq_s, dk_s, dex):
        del du_in

        @pl.when(pl.program_id(0) == 0)
        def _():
            dst[...] = jnp.zeros_like(dst)
            red_ref[...] = jnp.zeros_like(red_ref)

        lb = lb_ref[...]
        nwv = nw_ref[...]
        gm, up, eye, rr = _hg_masks()
        for sub in reversed(range(HG_SUB)):
            r = slice(sub * C, (sub + 1) * C)
            hq, hf, hg = u_ref[r, 0:512], u_ref[r, 512:1024], u_ref[r, 1536:2048]
            q = _silu(hq)
            v = u_ref[r, 1024:1536]
            s, f, k, eq, ek, ecum, erem = _hg_factors(hf, lb, mall_ref[...])
            o = o_ref[r, :]
            dy = dy_ref[r, :]
            inv = lax.rsqrt(jnp.mean(o * o, axis=-1, keepdims=True) + EPS)
            ohat = o * inv
            du_ref[r, 1536:2048] = _bf(dy * ohat * nwv * _dsilu(hg))
            dn = dy * _silu(hg)
            red_ref[0:1, :] += jnp.sum(dn * ohat, axis=0, keepdims=True)
            dohat = dn * nwv
            do = inv * (dohat - ohat * jnp.mean(dohat * ohat, axis=-1, keepdims=True))
            for h in range(HG_HEADS):
                sl = slice(h * LANE, (h + 1) * LANE)
                qh, kh, vh, doh = q[:, sl], k[:, sl], _bf(v[:, sl]), _bf(do[:, sl])
                p, qs, ks = _hg_scores(qh, kh, eq, ek, sl, gm, up, eye)
                st_f = st_ref[sub, h]
                sth = _bf(st_f)
                dsth = dst[h]
                dsth_b = _bf(dsth)
                qt = qh * ecum[:, sl]
                kt = kh * erem[:, sl]
                elast = _last_row(ecum[:, sl], rr)
                dp = _mm_nt(doh, vh)
                du_ref[r, 1024 + h * LANE:1024 + (h + 1) * LANE] = _bf(_mm_tn(_bf(p), doh) + _mm_nt(_bf(kt), dsth_b))
                dpe = _bf(dp * eye)
                dqt = _mm(doh, sth)
                dkt = _mm(vh, dsth_b)
                dq = dqt * ecum[:, sl] + _mm(dpe, _bf(kh))
                dk = dkt * erem[:, sl] + _mm_tn(dpe, _bf(qh))
                dex[sub, L2 * C:(L2 + 1) * C, sl] = dqt * qt
                dex[sub, (L2 + 1) * C:(L2 + 2) * C, sl] = dkt * kt
                for l in range(HG_LEVELS):
                    dpl = _bf(dp * gm[l])
                    dql = _mm(dpl, _bf(ks[l]))
                    dkl = _mm_tn(dpl, _bf(qs[l]))
                    dq = dq + jnp.where(up[l], dql * eq[l][:, sl], 0.0)
                    dk = dk + jnp.where(up[l], 0.0, dkl * ek[l][:, sl])
                    dex[sub, l * C:(l + 1) * C, sl] = dql * qs[l] + dkl * ks[l]
                dlast_s[sub, :, sl] = jnp.sum(dsth * st_f, axis=0, keepdims=True) * elast
                dst[h] = dsth * elast + _mm_tn(doh, _bf(qt))
                dq_s[sub, :, sl] = dq
                dk_s[sub, :, sl] = dk
            dq = dq_s[sub]
            dk = dk_s[sub]
            dlf = _sel_l2(mallt_ref[...], dex[sub]) + dlast_s[sub]
            du_ref[r, 0:512] = _bf(dq * _dsilu(hq))
            t = (1.0 - s) * (dlf / f - dk)
            du_ref[r, 512:1024] = _bf((1.0 - lb) * s * t)
            red_ref[1:2, :] += jnp.sum(t, axis=0, keepdims=True)

    rev = lambda i: (nb - 1 - i, 0)
    return pl.pallas_call(
        body, name="hg_bwd", grid=(nb,),
        in_specs=[pl.BlockSpec((rows, 2048), rev), _vec(HG_W), _vec(HG_W), _full(mall.shape), _full(mall_t.shape),
                  pl.BlockSpec((rows, HG_W), rev),
                  pl.BlockSpec((HG_SUB, HG_HEADS, LANE, LANE), lambda i: (nb - 1 - i, 0, 0, 0)),
                  pl.BlockSpec((rows, HG_W), lambda i: (nb - 1 - i, 1)), pl.BlockSpec(memory_space=pl.ANY)],
        out_specs=[pl.BlockSpec((rows, 2048), rev), pl.BlockSpec((8, HG_W), lambda i: (0, 0))],
        out_shape=[SDS((S, N_PAD), BF16), SDS((8, HG_W), F32)],
        scratch_shapes=[pltpu.VMEM((HG_HEADS, LANE, LANE), F32), pltpu.VMEM((HG_SUB, 1, HG_W), F32),
                        pltpu.VMEM((HG_SUB, C, HG_W), F32), pltpu.VMEM((HG_SUB, C, HG_W), F32),
                        pltpu.VMEM((HG_SUB, (L2 + 2) * C, HG_W), F32)],
        input_output_aliases={8: 0},
        compiler_params=_cp(("arbitrary",)),
    )(u, lb, nw, mall, mall_t, o_b, states, dycat, du)


def _ssdconv_fwd(u, cw, cb):
    S = u.shape[0]

    def body(u_ref, cw_ref, cb_ref, out_ref):
        rows = _iota((S, LANE), 0)
        out_ref[...] = _silu(_conv_fwd(u_ref[...], cw_ref, cb_ref, rows))

    return pl.pallas_call(
        body, name="ssdconv_fwd", grid=(SSD_CONV // LANE,),
        in_specs=[pl.BlockSpec((S, LANE), lambda t: (0, OFF_XBC // LANE + t)), pl.BlockSpec((4, LANE), lambda t: (0, t)),
                  pl.BlockSpec((1, LANE), lambda t: (0, t))],
        out_specs=pl.BlockSpec((S, LANE), lambda t: (0, t)),
        out_shape=SDS((S, SSD_CONV), F32),
        compiler_params=_cp(("parallel",)),
    )(u, cw, cb)


def _ssdconv_bwd(u, cw, cb, dxbc, du):
    S = u.shape[0]

    def body(u_ref, cw_ref, cb_ref, d_ref, du_in, du_ref, red_ref):
        del du_in
        rows = _iota((S, LANE), 0)
        x = u_ref[...]
        dco = d_ref[...] * _dsilu(_conv_fwd(x, cw_ref, cb_ref, rows))
        dx, dws, dcb = _conv_bwd(x, dco, cw_ref, rows)
        du_ref[...] = _bf(dx)
        for n, p in enumerate(dws + [dcb]):
            red_ref[pl.ds(n, 1), :] = p
        red_ref[pl.ds(5, 3), :] = jnp.zeros((3, LANE), F32)

    ucol = pl.BlockSpec((S, LANE), lambda t: (0, OFF_XBC // LANE + t))
    return pl.pallas_call(
        body, name="ssdconv_bwd", grid=(SSD_CONV // LANE,),
        in_specs=[ucol, pl.BlockSpec((4, LANE), lambda t: (0, t)), pl.BlockSpec((1, LANE), lambda t: (0, t)),
                  pl.BlockSpec((S, LANE), lambda t: (0, t)), pl.BlockSpec(memory_space=pl.ANY)],
        out_specs=[ucol, pl.BlockSpec((8, LANE), lambda t: (0, t))],
        out_shape=[SDS((S, N_PAD), BF16), SDS((8, SSD_CONV), F32)],
        input_output_aliases={4: 0},
        compiler_params=_cp(("parallel",)),
    )(u, cw, cb, dxbc, du)


def _ssd_consts():
    e64 = np.zeros((LANE, SSD_W), np.float32)
    for h in range(SSD_HEADS):
        e64[h, h * SSD_P:(h + 1) * SSD_P] = 1.0
    T = SSD_CHUNK
    tril = (np.arange(T)[None, :] <= np.arange(T)[:, None]).astype(np.float32)
    return e64, tril, tril.T.copy()


def _ssd_common(zdt, bias_ref, alog_ref, tril, e64, cum_ref, cumt_ref):
    T = SSD_CHUNK
    lane = _iota((1, LANE), 1)
    a_neg = jnp.where(lane < SSD_HEADS, -jnp.exp(alog_ref[...]), 0.0)
    dtpre = zdt[:, SSD_W:SSD_W + LANE] + bias_ref[...]
    dt = _softplus(dtpre)
    cum = _sel_l(tril, dt * a_neg)
    cum_ref[...] = cum
    cumt_ref[...] = cum.T
    cum_x = _sel_r(cum, e64)
    last_x = _last_row(cum_x, _iota((T, SSD_W), 0))
    ecum_x = jnp.exp(cum_x)
    erem_x = jnp.exp(last_x - cum_x)
    elast_x = jnp.exp(last_x)
    dt_x = _sel_r(dt, e64)
    return a_neg, dtpre, dt, ecum_x, erem_x, elast_x, dt_x


def _ssd_decay(cum_ref, cumt_ref, h, causal):
    T = SSD_CHUNK
    diff = jnp.broadcast_to(cum_ref[:, pl.ds(h, 1)], (T, T)) - cumt_ref[pl.ds(h, 1), :]
    return jnp.exp(jnp.where(causal, diff, NEG))


def _group_norm_fwd(y1, nwv):
    outs, invs = [], []
    for g in range(2):
        seg = y1[:, g * 512:(g + 1) * 512]
        inv = lax.rsqrt(jnp.mean(seg * seg, axis=-1, keepdims=True) + EPS)
        outs.append(seg * inv * nwv[:, g * 512:(g + 1) * 512])
        invs.append(inv)
    return outs, invs


def _ssd_fwd(u, xbc, bias, alog, dskip_x, nw, consts, ycat):
    S = u.shape[0]
    T = SSD_CHUNK
    n = S // T
    e64, tril, _ = consts

    def body(u_ref, xbc_ref, bias_ref, alog_ref, dx_ref, nw_ref, e64_ref, tril_ref, ycat_in,
             ycat_ref, y_ref, st_ref, st, cumt, cum_e):
        del ycat_in

        @pl.when(pl.program_id(0) == 0)
        def _():
            st[...] = jnp.zeros_like(st)

        zdt = u_ref[...]
        z = zdt[:, 0:SSD_W]
        xs = xbc_ref[:, 0:SSD_W]
        _, _, _, ecum_x, erem_x, elast_x, dt_x = _ssd_common(
            zdt, bias_ref, alog_ref, tril_ref[...], e64_ref[...], cum_e, cumt)
        causal = _iota((T, T), 0) >= _iota((T, T), 1)
        lo = _iota((T, LANE), 1) < SSD_P
        xdt = xs * dt_x
        xrem = xdt * erem_x
        st_ref[...] = st[...]
        for g in range(2):
            gs = slice(g * 512, (g + 1) * 512)
            bg = _bf(xbc_ref[:, SSD_W + g * LANE:SSD_W + (g + 1) * LANE])
            cg = _bf(xbc_ref[:, SSD_W + 256 + g * LANE:SSD_W + 256 + (g + 1) * LANE])
            cb = _mm_nt(cg, bg)
            yin = _mm(cg, _bf(st[:, gs])) * ecum_x[:, gs]
            for j in range(4):
                h0 = 8 * g + 2 * j
                cs = slice(h0 * SSD_P, (h0 + 2) * SSD_P)
                xp = xdt[:, cs]
                s0 = _bf(cb * _ssd_decay(cum_e, cumt, h0, causal))
                s1 = _bf(cb * _ssd_decay(cum_e, cumt, h0 + 1, causal))
                y_ref[:, cs] = (_mm(s0, _bf(jnp.where(lo, xp, 0.0))) + _mm(s1, _bf(jnp.where(lo, 0.0, xp)))
                                + yin[:, j * LANE:(j + 1) * LANE])
            st[:, gs] = st[:, gs] * elast_x[:, gs] + _mm_tn(bg, _bf(xrem[:, gs]))
        y1 = (y_ref[...] + dx_ref[...] * xs) * _silu(z)
        outs, _ = _group_norm_fwd(y1, nw_ref[...])
        for g in range(2):
            ycat_ref[:, g * 512:(g + 1) * 512] = _bf(outs[g])

    return pl.pallas_call(
        body, name="ssd_fwd", grid=(n,),
        in_specs=[pl.BlockSpec((T, SSD_W + LANE), lambda i: (i, OFF_Z // (SSD_W + LANE))),
                  pl.BlockSpec((T, SSD_CONV), lambda i: (i, 0)), _vec(LANE), _vec(LANE), _vec(SSD_W), _vec(SSD_W),
                  _full(e64.shape), _full(tril.shape), pl.BlockSpec(memory_space=pl.ANY)],
        out_specs=[pl.BlockSpec((T, SSD_W), lambda i: (i, 1)), pl.BlockSpec((T, SSD_W), lambda i: (i, 0)),
                   pl.BlockSpec((None, SSD_N, SSD_W), lambda i: (i, 0, 0))],
        out_shape=[SDS((S, D_INNER), BF16), SDS((S,SSD_W), F32), SDS((n, SSD_N, SSD_W), F32)],
        scratch_shapes=[pltpu.VMEM((SSD_N, SSD_W), F32), pltpu.VMEM((LANE, T), F32), pltpu.VMEM((T, LANE), F32)],
        input_output_aliases={8: 0},
        compiler_params=_cp(("arbitrary",)),
    )(u, xbc, bias, alog, dskip_x, nw, _bfc(e64), _bfc(tril), ycat)


def _ssd_bwd(u, xbc, bias, alog, dskip_x, nw, consts, y_ssd, states, dycat, du):
    S = u.shape[0]
    T = SSD_CHUNK
    n = S // T
    e64, tril, triu = consts
    e64t = np.ascontiguousarray(e64.T)

    def body(u_ref, xbc_ref, bias_ref, alog_ref, dx_ref, nw_ref, e64_ref, e64t_ref, tril_ref, triu_ref,
             y_ref, st_ref, dy_ref, du_in, du_ref, dxbc_ref, red_ref, dst, dl_s, cumt, dxdt_s, dy0_s, gb_s, gc_s,
             cum_e, cs_s):
        del du_in

        @pl.when(pl.program_id(0) == 0)
        def _():
            dst[...] = jnp.zeros_like(dst)
            red_ref[...] = jnp.zeros_like(red_ref)
            cs_s[...] = jnp.zeros_like(cs_s)

        zdt = u_ref[...]
        z = zdt[:, 0:SSD_W]
        xs = xbc_ref[:, 0:SSD_W]
        a_neg, dtpre, dt, ecum_x, erem_x, elast_x, dt_x = _ssd_common(
            zdt, bias_ref, alog_ref, tril_ref[...], e64_ref[...], cum_e, cumt)
        causal = _iota((T, T), 0) >= _iota((T, T), 1)
        lo = _iota((T, LANE), 1) < SSD_P
        xdt = xs * dt_x
        xrem = xdt * erem_x
        y = y_ref[...]
        dxv = dx_ref[...]
        nwv = nw_ref[...]
        sz = _silu(z)
        y0 = y + dxv * xs
        y1 = y0 * sz
        for g in range(2):
            gs = slice(g * 512, (g + 1) * 512)
            seg = y1[:, gs]
            inv = lax.rsqrt(jnp.mean(seg * seg, axis=-1, keepdims=True) + EPS)
            shat = seg * inv
            dyg = dy_ref[:, gs]
            red_ref[0:1, gs] += jnp.sum(dyg * shat, axis=0, keepdims=True)
            dsh = dyg * nwv[:, gs]
            dy1g = inv * (dsh - shat * jnp.mean(dsh * shat, axis=-1, keepdims=True))
            du_ref[:, gs] = _bf(dy1g * y0[:, gs] * _dsilu(z[:, gs]))
            dy0_s[:, gs] = dy1g * sz[:, gs]
        dy0 = dy0_s[...]
        red_ref[1:2, :] += jnp.sum(dy0 * xs, axis=0, keepdims=True)
        dyin = dy0 * ecum_x
        lane = _iota((T, LANE), 1)
        dcum = jnp.zeros((T, LANE), F32)

        def decay_grad(h, gm):
            cs_s[pl.ds(h, 1), :] = jnp.sum(gm, axis=0, keepdims=True)
            return jnp.where(lane == h, jnp.sum(gm, axis=1, keepdims=True), 0.0)

        for g in range(2):
            gs = slice(g * 512, (g + 1) * 512)
            bg = _bf(xbc_ref[:, SSD_W + g * LANE:SSD_W + (g + 1) * LANE])
            cg = _bf(xbc_ref[:, SSD_W + 256 + g * LANE:SSD_W + 256 + (g + 1) * LANE])
            cb = _mm_nt(cg, bg)
            dst_f, st_f = dst[:, gs], st_ref[:, gs]
            dstg = _bf(dst_f)
            stg = _bf(st_f)
            dyin_g = _bf(dyin[:, gs])
            xrem_g = _bf(xrem[:, gs])
            dcb = jnp.zeros((T, T), F32)
            dxr = _mm(bg, dstg)
            dxdt_s[:, gs] = dxr * erem_x[:, gs]
            gc_s[:, gs] = dxr * xrem[:, gs]
            gb_s[:, gs] = dyin[:, gs] * _mm(cg, stg)
            dl_s[:, gs] = jnp.sum(dst_f * st_f, axis=0, keepdims=True) * elast_x[:, gs]
            for j in range(4):
                h0 = 8 * g + 2 * j
                cs = slice(h0 * SSD_P, (h0 + 2) * SSD_P)
                xp = xdt[:, cs]
                dyp = dy0[:, cs]
                x_lo, x_hi = _bf(jnp.where(lo, xp, 0.0)), _bf(jnp.where(lo, 0.0, xp))
                d_lo, d_hi = _bf(jnp.where(lo, dyp, 0.0)), _bf(jnp.where(lo, 0.0, dyp))
                l0 = _ssd_decay(cum_e, cumt, h0, causal)
                l1 = _ssd_decay(cum_e, cumt, h0 + 1, causal)
                s0 = cb * l0
                s1 = cb * l1
                ds0 = _mm_nt(d_lo, x_lo)
                ds1 = _mm_nt(d_hi, x_hi)
                dcb = dcb + ds0 * l0 + ds1 * l1
                dxdt_s[:, cs] += _mm_tn(_bf(s0), d_lo) + _mm_tn(_bf(s1), d_hi)
                dcum = dcum + decay_grad(h0, ds0 * s0) + decay_grad(h0 + 1, ds1 * s1)
            dcb_b = _bf(dcb)
            dxbc_ref[:, SSD_W + g * LANE:SSD_W + (g + 1) * LANE] = _mm_tn(dcb_b, cg) + _mm_nt(xrem_g, dstg)
            dxbc_ref[:, SSD_W + 256 + g * LANE:SSD_W + 256 + (g + 1) * LANE] = _mm(dcb_b, bg) + _mm_nt(dyin_g, stg)
            dst[:, gs] = dst_f * elast_x[:, gs] + _mm_tn(cg, dyin_g)
        dxdt = dxdt_s[...]
        dxbc_ref[:, 0:SSD_W] = dxdt * dt_x + dy0 * dxv
        e64t = e64t_ref[...]
        gc = gc_s[...]
        dlast_x = jnp.sum(gc, axis=0, keepdims=True) + dl_s[...]
        dlast = jnp.max(_sel_r(jnp.broadcast_to(dlast_x, (8, SSD_W)), e64t), axis=0, keepdims=True)
        dcum = (dcum - cs_s[...].T + _sel_r(gb_s[...] - gc, e64t)
                + jnp.where(_iota((T, LANE), 0) == T - 1, dlast, 0.0))
        dda = _sel_l(triu_ref[...], dcum)
        ddt = dda * a_neg + _sel_r(dxdt * xs, e64t)
        ddtpre = ddt * _sigmoid(dtpre)
        du_ref[:, SSD_W:SSD_W + LANE] = _bf(jnp.where(lane < SSD_HEADS, ddtpre, 0.0))
        red_ref[2:3, 0:LANE] += jnp.sum(ddtpre, axis=0, keepdims=True)
        red_ref[3:4, 0:LANE] += jnp.sum(dda * dt, axis=0, keepdims=True)

    rev = lambda i: (n - 1 - i, 0)
    return pl.pallas_call(
        body, name="ssd_bwd", grid=(n,),
        in_specs=[pl.BlockSpec((T, SSD_W + LANE), lambda i: (n - 1 - i, OFF_Z // (SSD_W + LANE))),
                  pl.BlockSpec((T, SSD_CONV), rev), _vec(LANE), _vec(LANE), _vec(SSD_W), _vec(SSD_W),
                  _full(e64.shape), _full(e64t.shape), _full(tril.shape), _full(triu.shape),
                  pl.BlockSpec((T, SSD_W), rev), pl.BlockSpec((None, SSD_N, SSD_W), lambda i: (n - 1 - i, 0, 0)),
                  pl.BlockSpec((T, SSD_W), lambda i: (n - 1 - i, 1)), pl.BlockSpec(memory_space=pl.ANY)],
        out_specs=[pl.BlockSpec((T, SSD_W + LANE), lambda i: (n - 1 - i, OFF_Z // (SSD_W + LANE))),
                   pl.BlockSpec((T, SSD_CONV), rev), pl.BlockSpec((8, SSD_W), lambda i: (0, 0))],
        out_shape=[SDS((S, N_PAD), BF16), SDS((S, SSD_CONV), F32), SDS((8, SSD_W), F32)],
        scratch_shapes=[pltpu.VMEM((SSD_N, SSD_W), F32), pltpu.VMEM((1, SSD_W), F32), pltpu.VMEM((LANE, T), F32)]
        + [pltpu.VMEM((T, SSD_W), F32)] * 4 + [pltpu.VMEM((T, LANE), F32), pltpu.VMEM((LANE, T), F32)],
        input_output_aliases={13: 0},
        compiler_params=_cp(("arbitrary",)),
    )(u, xbc, bias, alog, dskip_x, nw, _bfc(e64), _bfc(e64t), _bfc(tril), _bfc(triu), y_ssd, states, dycat, du)


def _bfc(a):
    return jnp.asarray(a, BF16)


def _outproj_fwd(ycat, wo, x, gate):
    S = x.shape[0]
    tm = min(512, S)

    def body(yc_ref, wo_ref, x_ref, g_ref, xn_ref, y_ref):
        y = _mm(_bf(yc_ref[...]), wo_ref[...])
        y_ref[...] = y
        xn_ref[...] = x_ref[...] + g_ref[...] * y

    row = pl.BlockSpec((tm, D_MODEL), lambda i: (i, 0))
    return pl.pallas_call(
        body, name="outproj_fwd", grid=(S // tm,),
        in_specs=[pl.BlockSpec((tm, D_INNER), lambda i: (i, 0)), _full((D_INNER, D_MODEL)), row, _vec(D_MODEL)],
        out_specs=[row, row],
        out_shape=[SDS((S, D_MODEL), F32), SDS((S, D_MODEL), F32)],
        compiler_params=_cp(("parallel",)),
    )(ycat, wo, x, gate)


def _outproj_bwd(dxn, y, gate, ycat, wo):
    S = dxn.shape[0]
    tm = min(512, S)

    def body(dx_ref, y_ref, g_ref, yc_ref, wo_ref, dyc_ref, gwo_ref, dg_ref, acc):
        @pl.when(pl.program_id(0) == 0)
        def _():
            acc[...] = jnp.zeros_like(acc)
            dg_ref[...] = jnp.zeros_like(dg_ref)

        dxv = dx_ref[...]
        dy = _bf(dxv * g_ref[...])
        dg_ref[0:1, :] += jnp.sum(dxv * y_ref[...], axis=0, keepdims=True)
        dyc_ref[...] = _mm_nt(dy, wo_ref[...])
        acc[...] += _mm_tn(_bf(yc_ref[...]), dy)

        @pl.when(pl.program_id(0) == pl.num_programs(0) - 1)
        def _():
            gwo_ref[...] = acc[...].astype(BF16)

    row = pl.BlockSpec((tm, D_MODEL), lambda i: (i, 0))
    wide = pl.BlockSpec((tm, D_INNER), lambda i: (i, 0))
    return pl.pallas_call(
        body, name="outproj_bwd", grid=(S // tm,),
        in_specs=[row, row, _vec(D_MODEL), wide, _full((D_INNER, D_MODEL))],
        out_specs=[wide, _full((D_INNER, D_MODEL)), _full((8, D_MODEL))],
        out_shape=[SDS((S, D_INNER), F32), SDS((D_INNER, D_MODEL), BF16), SDS((8, D_MODEL), F32)],
        scratch_shapes=[pltpu.VMEM((D_INNER, D_MODEL), F32)],
        compiler_params=_cp(("arbitrary",)),
    )(dxn, y, gate, ycat, wo)


def _loss_head(x, fw, target):
    S = x.shape[0]
    tm = min(512, S)

    def body(x_ref, fw_ref, t_ref, dx_ref, red_ref):
        @pl.when(pl.program_id(0) == 0)
        def _():
            red_ref[...] = jnp.zeros_like(red_ref)

        xv = x_ref[...]
        fwv = fw_ref[...]
        inv = lax.rsqrt(jnp.mean(xv * xv, axis=-1, keepdims=True) + EPS)
        xhat = xv * inv
        err = xhat * fwv - t_ref[...]
        col = jnp.sum(err * err, axis=0, keepdims=True)
        red_ref[1:2, :] += jnp.broadcast_to(jnp.sum(col, axis=1, keepdims=True) * (0.5 / D_MODEL), (1, D_MODEL))
        dy = err * (1.0 / D_MODEL)
        red_ref[0:1, :] += jnp.sum(dy * xhat, axis=0, keepdims=True)
        dxhat = dy * fwv
        dx_ref[...] = inv * (dxhat - xhat * jnp.mean(dxhat * xhat, axis=-1, keepdims=True))

    row = pl.BlockSpec((tm, D_MODEL), lambda i: (i, 0))
    return pl.pallas_call(
        body, name="loss_head", grid=(S // tm,),
        in_specs=[row, _vec(D_MODEL), row],
        out_specs=[row, _full((8, D_MODEL))],
        out_shape=[SDS((S, D_MODEL), F32), SDS((8, D_MODEL), F32)],
        compiler_params=_cp(("arbitrary",)),
    )(x, fw, target)


ADA_COLS = 3 * D_MODEL // N_DEV


def _ada_fwd(c_all, w_ada, b_cols):
    def body(c_ref, w_ref, b_ref, out_ref):
        out_ref[...] = _mm(_bf(_silu(c_ref[...])), _bf(w_ref[...])) + b_ref[...]

    return pl.pallas_call(
        body, name="ada_fwd", grid=(DEPTH,),
        in_specs=[_full((N_DEV, D_MODEL)), pl.BlockSpec((None, D_MODEL, ADA_COLS), lambda l: (l, 0, 0)),
                  pl.BlockSpec((None, 1, ADA_COLS), lambda l: (l, 0, 0))],
        out_specs=pl.BlockSpec((None, N_DEV, ADA_COLS), lambda l: (l, 0, 0)),
        out_shape=SDS((DEPTH, N_DEV, ADA_COLS), F32),
        compiler_params=_cp(("parallel",)),
    )(c_all, w_ada, b_cols)


def _ada_bwd(ct_pad, dmod_pad):
    def body(c_ref, d_ref, out_ref):
        out_ref[...] = _mm(_bf(_silu(c_ref[...])), _bf(d_ref[...]))

    return pl.pallas_call(
        body, name="ada_bwd", grid=(DEPTH,),
        in_specs=[_full((D_MODEL, LANE)), pl.BlockSpec((None, LANE, ADA_COLS), lambda l: (l, 0, 0))],
        out_specs=pl.BlockSpec((None, D_MODEL, ADA_COLS), lambda l: (l, 0, 0)),
        out_shape=SDS((DEPTH, D_MODEL, ADA_COLS), F32),
        compiler_params=_cp(("parallel",)),
    )(ct_pad, dmod_pad)


def _adamw(parts, w, m, v, name, own=None, layers=None, prev=None):
    n, L, R, C = parts.shape
    lo, hi = layers or (0, L)
    tr = R
    while tr * C * 4 > (1 << 20) and tr % 16 == 0:
        tr //= 2
    first = 1 if own is None else 2

    def body(*refs):
        p_ref = refs[0]
        w_ref, m_ref, v_ref = refs[first:first + 3]
        g_ref, d_ref, mo_ref, vo_ref = refs[-4:]

        def part(k):
            if own is None:
                return p_ref[k].astype(F32)
            me = 4 * lax.axis_index("x") + 2 * lax.axis_index("y") + lax.axis_index("c")
            return jnp.where(me == k, refs[1][...], p_ref[k]).astype(F32)

        g = part(0)
        for k in range(1, n):
            g = g + part(k)
        mn = ADAM_B1 * m_ref[...] + (1.0 - ADAM_B1) * g
        vn = ADAM_B2 * v_ref[...] + (1.0 - ADAM_B2) * (g * g)
        m_hat = mn / (1.0 - ADAM_B1 ** ADAM_STEP)
        v_hat = vn / (1.0 - ADAM_B2 ** ADAM_STEP)
        g_ref[...] = g
        d_ref[...] = -ADAM_LR * (m_hat / (jnp.sqrt(v_hat) + ADAM_EPS) + ADAM_WD * w_ref[...])
        mo_ref[...] = mn
        vo_ref[...] = vn

    blk = pl.BlockSpec((None, tr, C), lambda l, i: (lo + l, i, 0))
    n_blk = 3 if own is None else 4
    return pl.pallas_call(
        body, name=name, grid=(hi - lo, R // tr),
        in_specs=[pl.BlockSpec((n, None, tr, C), lambda l, i: (0, lo + l, i, 0))] + [blk] * n_blk
        + ([] if prev is None else [ANY] * 4),
        out_specs=[blk] * 4,
        out_shape=[SDS((L, R, C), F32)] * 4,
        input_output_aliases={} if prev is None else {1 + n_blk + k: k for k in range(4)},
        compiler_params=_cp(("parallel", "parallel")),
    )(parts, *([] if own is None else [own]), w, m, v, *([] if prev is None else prev))


MESH = pl.DeviceIdType.MESH
ANY = pl.BlockSpec(memory_space=pl.ANY)


def _all_gather(v, name):
    def body(v_ref, out_ref, send_sems, recv_sems, local_sem):
        x, y, c = lax.axis_index("x"), lax.axis_index("y"), lax.axis_index("c")
        me, sibling = (x, y, c), (x, y, 1 - c)
        chips = [(1 - x, y), (x, 1 - y), (1 - x, 1 - y)]

        def slot(px, py, pc):
            return out_ref.at[4 * px + 2 * py + pc]

        def copy(k, block, to, src=None):
            return pltpu.make_async_remote_copy(
                src_ref=slot(*block) if src is None else src, dst_ref=slot(*block),
                send_sem=send_sems.at[k], recv_sem=recv_sems.at[k], device_id=to, device_id_type=MESH)

        mine = pltpu.make_async_copy(v_ref, slot(*me), local_sem)
        mine.start()
        first = [copy(0, me, sibling, src=v_ref)]
        first += [copy(1 + j, me, (*chip, c), src=v_ref) for j, chip in enumerate(chips)]
        for cp in first:
            cp.start()
        passed = [copy(4 + j, (*chip, c), sibling) for j, chip in enumerate(chips)]
        for j, chip in enumerate(chips):
            copy(1 + j, (*chip, c), me).wait_recv()
            passed[j].start()
        copy(0, sibling, me).wait_recv()
        for j, chip in enumerate(chips):
            copy(4 + j, (*chip, 1 - c), me).wait_recv()
        for cp in first + passed:
            cp.wait_send()
        mine.wait()

    return pl.pallas_call(
        body, name=name, in_specs=[ANY], out_specs=ANY,
        out_shape=SDS((N_DEV,) + v.shape, v.dtype),
        scratch_shapes=[pltpu.SemaphoreType.DMA((7,)), pltpu.SemaphoreType.DMA((7,)), pltpu.SemaphoreType.DMA],
    )(v)


def _all_to_all(v, name):
    def body(v_ref, out_ref, send_sems, recv_sems, local_sem):
        x, y, c = lax.axis_index("x"), lax.axis_index("y"), lax.axis_index("c")
        mine_idx = 4 * x + 2 * y + c
        mine = pltpu.make_async_copy(v_ref.at[mine_idx], out_ref.at[mine_idx], local_sem)
        mine.start()
        sends, recvs = [], []
        for k in range(1, N_DEV):
            px = 1 - x if k & 4 else x
            py = 1 - y if k & 2 else y
            pc = 1 - c if k & 1 else c
            peer_idx = 4 * px + 2 * py + pc
            sems = dict(send_sem=send_sems.at[k - 1], recv_sem=recv_sems.at[k - 1], device_id=(px, py, pc),
                        device_id_type=MESH)
            sends.append(pltpu.make_async_remote_copy(src_ref=v_ref.at[peer_idx], dst_ref=out_ref.at[mine_idx], **sems))
            recvs.append(pltpu.make_async_remote_copy(src_ref=v_ref.at[peer_idx], dst_ref=out_ref.at[peer_idx], **sems))
        for cp in sends:
            cp.start()
        for cp in recvs:
            cp.wait_recv()
        for cp in sends:
            cp.wait_send()
        mine.wait()

    return pl.pallas_call(
        body, name=name, in_specs=[ANY], out_specs=ANY,
        out_shape=SDS(v.shape, v.dtype),
        scratch_shapes=[pltpu.SemaphoreType.DMA((7,)), pltpu.SemaphoreType.DMA((7,)), pltpu.SemaphoreType.DMA],
    )(v)


HBM_SPEC = pl.BlockSpec(memory_space=pltpu.HBM)
SEM_SPEC = pl.BlockSpec(memory_space=pltpu.SEMAPHORE)
EFFECT = pltpu.SideEffectType.DATAFLOW_SIDE_EFFECTING


EXCHANGE_PEERS = {"gather": range(1, N_DEV), "scatter": range(1, N_DEV), "chip": (1, 2, 4, 6), "pass": (2, 4, 6)}


def _exchange_copies(srcs, lands, send_sems, recv_sems, mode, layer):
    x, y, c = lax.axis_index("x"), lax.axis_index("y"), lax.axis_index("c")
    me = 4 * x + 2 * y + c
    copies = []
    for a, (src, land) in enumerate(zip(srcs, lands)):
        for k in EXCHANGE_PEERS[mode]:
            px = 1 - x if k & 4 else x
            py = 1 - y if k & 2 else y
            pc = 1 - c if k & 1 else c
            peer = 4 * px + 2 * py + pc
            if mode == "scatter":
                s, d, to = src.at[peer], land.at[me, layer], (px, py, pc)
            elif mode == "pass":
                s, d, to = land.at[peer], land.at[peer], (x, y, 1 - c)
            else:
                s, d, to = src, land.at[me], (px, py, pc)
            n = 7 * a + k - 1
            copies.append(pltpu.make_async_remote_copy(
                src_ref=s, dst_ref=d, send_sem=send_sems.at[n], recv_sem=recv_sems.at[n], device_id=to,
                device_id_type=MESH))
    return copies


def _exchange_start(name, srcs, lands, mode, layer=0, after=None):
    n = len(srcs)

    def body(*refs):
        send_sems, recv_sems = refs[-2 * n - 3], refs[-2 * n - 2]
        for cp in _exchange_copies(refs[:n], refs[n:2 * n], send_sems, recv_sems, mode, layer):
            cp.start()
        refs[-1][...] = jnp.zeros_like(refs[-1])

    arrays = list(srcs) + list(lands)
    sems = pltpu.SemaphoreType.DMA((7 * n,))
    out = pl.pallas_call(
        body, name=name,
        out_shape=(sems, sems, *[pltpu.HBM(v.shape, v.dtype) for v in arrays], SDS((8, LANE), F32)),
        in_specs=[HBM_SPEC] * (2 * n) + ([ANY] if after is not None else []),
        out_specs=(SEM_SPEC, SEM_SPEC, *[HBM_SPEC] * (2 * n), pl.BlockSpec(memory_space=pltpu.VMEM)),
        input_output_aliases={i: 2 + i for i in range(2 * n)},
        compiler_params=pltpu.CompilerParams(has_side_effects=EFFECT),
    )(*[pltpu.with_memory_space_constraint(v, pltpu.HBM) for v in arrays], *([after] if after is not None else []))
    return dict(sems=out[:2], srcs=out[2:2 + n], lands=out[2 + n:2 + 2 * n], token=out[-1][0, 0], mode=mode,
                layer=layer)


def _exchange_wait(name, st, after, also=()):
    n = len(st["srcs"])

    def body(*refs):
        send_sems, recv_sems = refs[2 * n], refs[2 * n + 1]
        for cp in _exchange_copies(refs[:n], refs[n:2 * n], send_sems, recv_sems, st["mode"], st["layer"]):
            cp.wait_send()
            cp.wait_recv()

    arrays = list(st["srcs"]) + list(st["lands"])
    out = pl.pallas_call(
        body, name=name,
        out_shape=tuple(pltpu.HBM(v.shape, v.dtype) for v in arrays),
        in_specs=[HBM_SPEC] * (2 * n) + [SEM_SPEC, SEM_SPEC] + [ANY] * (1 + len(also)),
        out_specs=tuple([HBM_SPEC] * (2 * n)),
        input_output_aliases={i: i for i in range(2 * n)},
        compiler_params=pltpu.CompilerParams(has_side_effects=EFFECT),
    )(*arrays, *st["sems"], after, *also)
    st["srcs"] = out[:n]
    return out[n:]


_IN_PIECES = ([(1024, 3072)]
              + [r for t in range(4) for r in ((LANE * t, LANE * (t + 1)), (512 + LANE * t, 512 + LANE * (t + 1)))]
              + [(4096, 5632), (3072, 4096), (5632, 5648)])


def _permute_in(w):
    pad = jnp.zeros(w.shape[:-1] + (N_PAD - N_IN,), w.dtype)
    return jnp.concatenate([w[..., a:b] for a, b in _IN_PIECES] + [pad], axis=-1)


def _unpermute_in(g):
    ax = [g[..., OFF_LRU + 2 * LANE * t:OFF_LRU + 2 * LANE * t + LANE] for t in range(4)]
    ag = [g[..., OFF_LRU + 2 * LANE * t + LANE:OFF_LRU + 2 * LANE * (t + 1)] for t in range(4)]
    return jnp.concatenate(ax + ag + [g[..., 0:2048], g[..., OFF_Z:OFF_Z + SSD_W], g[..., OFF_XBC:OFF_XBC + SSD_CONV],
                                      g[..., OFF_Z + SSD_W:OFF_Z + SSD_W + SSD_HEADS]], axis=-1)


SHARD_COLS = N_IN // N_DEV


def _in_segments():
    segs, pos = [], 0
    for a, b in _IN_PIECES:
        for i in range(N_DEV):
            lo, hi = max(a, SHARD_COLS * i), min(b, SHARD_COLS * (i + 1))
            if lo < hi:
                segs.append((i, lo - SHARD_COLS * i, hi - lo, pos + lo - a))
        pos += b - a
    return segs


RELAYOUT_ROWS = 256


def _relayout_in(land, own):
    def body(land_ref, own_ref, out_ref):
        me = 4 * lax.axis_index("x") + 2 * lax.axis_index("y") + lax.axis_index("c")
        out_ref[:, N_IN:N_PAD] = jnp.zeros((RELAYOUT_ROWS, N_PAD - N_IN), BF16)
        for i, j, wd, p in _in_segments():
            out_ref[:, p:p + wd] = jnp.where(me == i, own_ref[:, j:j + wd], land_ref[i, :, j:j + wd])

    return pl.pallas_call(
        body, name="relayout_in", grid=(D_MODEL // RELAYOUT_ROWS,),
        in_specs=[pl.BlockSpec((N_DEV, RELAYOUT_ROWS, SHARD_COLS), lambda r: (0, r, 0)),
                  pl.BlockSpec((RELAYOUT_ROWS, SHARD_COLS), lambda r: (r, 0))],
        out_specs=pl.BlockSpec((RELAYOUT_ROWS, N_PAD), lambda r: (r, 0)),
        out_shape=SDS((D_MODEL, N_PAD), BF16),
        compiler_params=_cp(("parallel",)),
    )(land, own)


def _relayout_grad(g):
    def body(g_ref, out_ref):
        for i, j, wd, p in _in_segments():
            out_ref[i, :, j:j + wd] = g_ref[:, p:p + wd].astype(BF16)

    return pl.pallas_call(
        body, name="relayout_grad", grid=(D_MODEL // RELAYOUT_ROWS,),
        in_specs=[pl.BlockSpec((RELAYOUT_ROWS, N_PAD), lambda r: (r, 0))],
        out_specs=pl.BlockSpec((N_DEV, RELAYOUT_ROWS, SHARD_COLS), lambda r: (0, r, 0)),
        out_shape=SDS((N_DEV, D_MODEL, SHARD_COLS), BF16),
        compiler_params=_cp(("parallel",)),
    )(g)


def _block_diag(w):
    w4 = w.reshape(4, 2, 64, 64)
    z = jnp.zeros((4, 64, 64), w.dtype)
    top = jnp.concatenate([w4[:, 0], z], axis=-1)
    bot = jnp.concatenate([z, w4[:, 1]], axis=-1)
    return jnp.concatenate([top, bot], axis=1).astype(BF16)


def _diag_blocks(g):
    return jnp.stack([g[:, :64, :64], g[:, 64:, 64:]], axis=1).reshape(8, 64, 64)


def _pad_lanes(v):
    return jnp.pad(v, (0, LANE - v.shape[0]))[None, :]


def _lower_bounds(logits):
    p = jax.nn.softmax(logits, axis=0)
    return p, jnp.cumsum(p, axis=0) - p[0]


def _lower_bounds_bwd(p, dlb):
    dp = jnp.cumsum(dlb[::-1], axis=0)[::-1]
    dp = dp.at[0].add(-jnp.sum(dlb, axis=0))
    return p * (dp - jnp.sum(dp * p, axis=0, keepdims=True))


SMALL = ["norm_w", "b_ada", "lru_conv_b", "lru_wa", "lru_ba", "lru_wx", "lru_bx", "lru_lambda", "hg_lb_logits",
         "hg_norm_w", "ssd_conv_b", "ssd_dt_bias", "ssd_a_log", "ssd_d", "ssd_norm_w", "final_norm_w"]
WEIGHTS = ["norm_w", "w_ada", "b_ada", "w_in", "lru_conv_w", "lru_conv_b", "lru_wa", "lru_ba", "lru_wx", "lru_bx",
           "lru_lambda", "hg_lb_logits", "hg_norm_w", "ssd_conv_w", "ssd_conv_b", "ssd_dt_bias", "ssd_a_log", "ssd_d",
           "ssd_norm_w", "w_out", "final_norm_w"]
INPUTS = ["x", "c"] + WEIGHTS + ["loss_target"] + ["m_" + n for n in WEIGHTS] + ["v_" + n for n in WEIGHTS]
SMALL_ROW = 1024


def _small_rows(like):
    out, off = {}, 0
    for n in SMALL:
        rows = -(-int(np.prod(like[n].shape)) // (8 * SMALL_ROW)) * 8
        out[n] = (off, rows)
        off += rows
    return out, off


def _flatten_small(d, prefix="", last=0.0):
    table, _ = _small_rows({n: d[prefix + n] for n in SMALL})
    pieces = []
    for n in SMALL:
        flat = d[prefix + n].reshape(-1)
        pieces.append(jnp.pad(flat, (0, table[n][1] * SMALL_ROW - flat.shape[0])).reshape(-1, SMALL_ROW))
    return jnp.concatenate(pieces + [jnp.full((8, SMALL_ROW), last, F32)], axis=0)


def _split_small(packed, like):
    table, _ = _small_rows(like)
    out = {}
    for n in SMALL:
        off, rows = table[n]
        size = int(np.prod(like[n].shape))
        out[n] = packed[off:off + rows].reshape(-1)[:size].reshape(like[n].shape)
    return out


def _local_step(x, mod, target, w, fetch, emit):
    S = x.shape[0]
    mall = _bfc(_hg_consts())
    mall_t = _bfc(_hg_consts().T)
    consts = _ssd_consts()
    p_lb, lbs = _lower_bounds(w["hg_lb_logits"])
    saved = []
    for l in range(DEPTH):
        w_in_l, w_out_l, token = fetch(l, x)
        shift, scale, gate = (mod[l:l + 1, k * D_MODEL:(k + 1) * D_MODEL] for k in range(3))
        shift = shift + token
        prm = dict(
            nw=w["norm_w"][l:l + 1], cw=w["lru_conv_w"][l], cb=w["lru_conv_b"][l:l + 1],
            wa=_block_diag(w["lru_wa"][l]), ba=w["lru_ba"][l].reshape(1, LRU_W),
            wx=_block_diag(w["lru_wx"][l]), bx=w["lru_bx"][l].reshape(1, LRU_W), lam=w["lru_lambda"][l:l + 1],
            lb=lbs[l:l + 1], hnw=w["hg_norm_w"][l:l + 1], scw=w["ssd_conv_w"][l], scb=w["ssd_conv_b"][l:l + 1],
            bias=_pad_lanes(w["ssd_dt_bias"][l]), alog=_pad_lanes(w["ssd_a_log"][l]),
            dskip=jnp.repeat(w["ssd_d"][l], SSD_P)[None, :], snw=w["ssd_norm_w"][l:l + 1],
            w_in=w_in_l, w_out=w_out_l, scale=scale, gate=gate)
        u, h = _inproj_fwd(x, prm["nw"], scale, shift, prm["w_in"])
        ycat = lax.empty((S, D_INNER), BF16)
        lru_args = (u, prm["cw"], prm["cb"], prm["wa"], prm["ba"], prm["wx"], prm["bx"], prm["lam"])
        ycat, h_lru = _lru_fwd(*lru_args, ycat)
        ycat, o_b, hg_st = _hg_fwd(u, prm["lb"], prm["hnw"], mall, ycat)
        xbc = _ssdconv_fwd(u, prm["scw"], prm["scb"])
        ssd_args = (u, xbc, prm["bias"], prm["alog"], prm["dskip"], prm["snw"], consts)
        ycat, y_ssd, ssd_st = _ssd_fwd(*ssd_args, ycat)
        x_new, y = _outproj_fwd(ycat, prm["w_out"], x, gate)
        saved.append((prm, x, u, h, ycat, lru_args, h_lru, o_b, hg_st, ssd_args, y_ssd, ssd_st, y))
        x = x_new
    dx, red = _loss_head(x, w["final_norm_w"][None, :], target)
    loss = red[1, 0]
    g = {n: [None] * DEPTH for n in WEIGHTS}
    g["final_norm_w"] = red[0]
    dmod, dlb = [None] * DEPTH, [None] * DEPTH
    for l in reversed(range(DEPTH)):
        prm, x, u, h, ycat, lru_args, h_lru, o_b, hg_st, ssd_args, y_ssd, ssd_st, y = saved[l]
        dycat, g_out, dgate = _outproj_bwd(dx, y, prm["gate"], ycat, prm["w_out"])
        token = emit(l, "w_out", g_out)
        du = lax.empty((S, N_PAD), BF16)
        ssd_args = ssd_args[:5] + (ssd_args[5] + token,) + ssd_args[6:]
        du, dxbc, sred = _ssd_bwd(*ssd_args, y_ssd, ssd_st, dycat, du)
        du, cred = _ssdconv_bwd(u, prm["scw"], prm["scb"], dxbc, du)
        du, hred = _hg_bwd(u, prm["lb"], prm["hnw"], mall, mall_t, o_b, hg_st, dycat, du)
        du, lred, gwa, gwx = _lru_bwd(*lru_args, h_lru, dycat, du)
        token = emit(l, "w_in", _inproj_bwd_w(h, du))
        dx, ired = _inproj_bwd_x(du, prm["w_in"], x, prm["nw"], prm["scale"] + token, dx)
        g["norm_w"][l] = ired[2]
        dmod[l] = jnp.concatenate([ired[0], ired[1], dgate[0]])
        g["lru_conv_w"][l], g["lru_conv_b"][l] = lred[0:4], lred[4]
        g["lru_ba"][l], g["lru_bx"][l], g["lru_lambda"][l] = lred[5].reshape(8, 64), lred[6].reshape(8, 64), lred[7]
        g["lru_wa"][l], g["lru_wx"][l] = _diag_blocks(gwa), _diag_blocks(gwx)
        g["hg_norm_w"][l], dlb[l] = hred[0], hred[1]
        g["ssd_conv_w"][l], g["ssd_conv_b"][l] = cred[0:4], cred[4]
        g["ssd_norm_w"][l] = sred[0]
        g["ssd_d"][l] = sred[1].reshape(SSD_HEADS, SSD_P).sum(-1)
        g["ssd_dt_bias"][l] = sred[2, :SSD_HEADS]
        g["ssd_a_log"][l] = -sred[3, :SSD_HEADS] * jnp.exp(w["ssd_a_log"][l])
    g["hg_lb_logits"] = _lower_bounds_bwd(p_lb, jnp.stack(dlb))
    for n in WEIGHTS:
        if isinstance(g[n], list) and g[n][0] is not None:
            g[n] = jnp.stack(g[n])
    return loss, dx, jnp.stack(dmod), g


def kernel(x, c, norm_w, w_ada, b_ada, w_in, lru_conv_w, lru_conv_b, lru_wa, lru_ba, lru_wx, lru_bx, lru_lambda, hg_lb_logits, hg_norm_w, ssd_conv_w, ssd_conv_b, ssd_dt_bias, ssd_a_log, ssd_d, ssd_norm_w, w_out, final_norm_w, loss_target, m_norm_w, m_w_ada, m_b_ada, m_w_in, m_lru_conv_w, m_lru_conv_b, m_lru_wa, m_lru_ba, m_lru_wx, m_lru_bx, m_lru_lambda, m_hg_lb_logits, m_hg_norm_w, m_ssd_conv_w, m_ssd_conv_b, m_ssd_dt_bias, m_ssd_a_log, m_ssd_d, m_ssd_norm_w, m_w_out, m_final_norm_w, v_norm_w, v_w_ada, v_b_ada, v_w_in, v_lru_conv_w, v_lru_conv_b, v_lru_wa, v_lru_ba, v_lru_wx, v_lru_bx, v_lru_lambda, v_hg_lb_logits, v_hg_norm_w, v_ssd_conv_w, v_ssd_conv_b, v_ssd_dt_bias, v_ssd_a_log, v_ssd_d, v_ssd_norm_w, v_w_out, v_final_norm_w):
    return _step(x, c, norm_w, w_ada, b_ada, w_in, lru_conv_w, lru_conv_b, lru_wa, lru_ba, lru_wx, lru_bx, lru_lambda, hg_lb_logits, hg_norm_w, ssd_conv_w, ssd_conv_b, ssd_dt_bias, ssd_a_log, ssd_d, ssd_norm_w, w_out, final_norm_w, loss_target, m_norm_w, m_w_ada, m_b_ada, m_w_in, m_lru_conv_w, m_lru_conv_b, m_lru_wa, m_lru_ba, m_lru_wx, m_lru_bx, m_lru_lambda, m_hg_lb_logits, m_hg_norm_w, m_ssd_conv_w, m_ssd_conv_b, m_ssd_dt_bias, m_ssd_a_log, m_ssd_d, m_ssd_norm_w, m_w_out, m_final_norm_w, v_norm_w, v_w_ada, v_b_ada, v_w_in, v_lru_conv_w, v_lru_conv_b, v_lru_wa, v_lru_ba, v_lru_wx, v_lru_bx, v_lru_lambda, v_hg_lb_logits, v_hg_norm_w, v_ssd_conv_w, v_ssd_conv_b, v_ssd_dt_bias, v_ssd_a_log, v_ssd_d, v_ssd_norm_w, v_w_out, v_final_norm_w)


def _step(*args):
    a = dict(zip(INPUTS, args, strict=True))
    me = 4 * lax.axis_index("x") + 2 * lax.axis_index("y") + lax.axis_index("c")
    x, target = a["x"][0], a["loss_target"][0]

    c_all = _all_gather(a["c"], "gather_c")[:, 0, :]
    b_cols = lax.dynamic_slice_in_dim(a["b_ada"], me * ADA_COLS, ADA_COLS, axis=1)[:, None, :]
    mod_parts = _all_gather(_ada_fwd(c_all, a["w_ada"], b_cols), "gather_mod")
    mod = lax.dynamic_index_in_dim(mod_parts, me, axis=2, keepdims=False)
    mod = mod.transpose(1, 0, 2).reshape(DEPTH, 3 * D_MODEL)

    w = {n: a[n] for n in SMALL}

    w_in_b, w_out_b = a["w_in"].astype(BF16), a["w_out"].astype(BF16)
    conv_own = jnp.concatenate([a["lru_conv_w"], a["ssd_conv_w"]], axis=-1)
    cols, rows_out = N_IN // N_DEV, D_INNER // N_DEV

    def gather_start(l, after):
        srcs = [w_in_b[l], w_out_b[l]] + ([conv_own] if l == 0 else [])
        lands = [lax.empty((N_DEV,) + s.shape, s.dtype) for s in srcs]
        return _exchange_start(f"gather_start_{l}", srcs, lands, "chip" if l == 0 else "gather", after=after)

    def gather_pass(name, st, after, also=()):
        landed = _exchange_wait(name + "_wait", st, after, also)
        st2 = _exchange_start(name + "_pass", st["srcs"], landed, "pass")
        return _exchange_wait(name + "_passed", st2, after)

    gathers = {0: gather_start(0, mod)}

    def fetch(l, x_l):
        if l == 0:
            landed = gather_pass("gather_0", gathers[0], x_l, also=(a["m_w_in"], a["v_w_in"]))
        else:
            landed = _exchange_wait(f"gather_wait_{l}", gathers[l], x_l)
        land_out = lax.dynamic_update_index_in_dim(landed[1], w_out_b[l], me, 0)
        if l == 0:
            conv = lax.dynamic_update_index_in_dim(landed[2], conv_own, me, 0).transpose(1, 2, 0, 3)
            w["lru_conv_w"] = conv[..., :64].reshape(DEPTH, 4, LRU_W)
            w["ssd_conv_w"] = conv[..., 64:].reshape(DEPTH, 4, SSD_CONV)
        token = 0.0
        if l + 1 < DEPTH:
            gathers[l + 1] = gather_start(l + 1, land_out)
            token = gathers[l + 1]["token"]
        return _relayout_in(landed[0], w_in_b[l]), land_out.reshape(D_INNER, D_MODEL), token

    scatters = {"w_in": {}, "w_out": {}}
    lands = {"w_in": lax.empty((N_DEV, DEPTH, D_MODEL, cols), BF16),
             "w_out": lax.empty((N_DEV, DEPTH, rows_out, D_MODEL), BF16)}
    own = {"w_in": [None] * DEPTH, "w_out": [None] * DEPTH}

    def emit(l, name, grad):
        grad = _relayout_grad(grad) if name == "w_in" else grad.reshape(N_DEV, rows_out, D_MODEL)
        own[name][l] = lax.dynamic_index_in_dim(grad, me, 0, keepdims=False)
        st = _exchange_start(f"scatter_start_{name}_{l}", [grad], [lands[name]], "scatter", layer=l)
        scatters[name][l] = st
        lands[name] = st["lands"][0]
        return st["token"]

    loss_own, dx, dmod, g = _local_step(x, mod, target, w, fetch, emit)

    def sharded(name, parts, own=None, **kw):
        return _adamw(parts, a[name], a["m_" + name], a["v_" + name], "adamw_" + name + kw.pop("tag", ""), own=own, **kw)

    g["b_ada"] = dmod
    small_own = _flatten_small(g, last=loss_own)
    small_st = _exchange_start("gather_small", [small_own], [lax.empty((N_DEV,) + small_own.shape, F32)], "chip",
                               after=dx)
    big = {}
    after = small_st["token"] + dx[0:8, 0:LANE]
    for name in ("w_out", "w_in"):
        own_all = jnp.stack(own[name])
        for l in reversed(range(1, DEPTH)):
            scatters[name][l]["lands"] = [lands[name]]
            lands[name] = _exchange_wait(f"scatter_wait_{name}_{l}", scatters[name][l], after)[0]
        upper = sharded(name, lands[name], own_all, layers=(1, DEPTH), tag="_upper")
        scatters[name][0]["lands"] = [lands[name]]
        lands[name] = _exchange_wait(f"scatter_wait_{name}_0", scatters[name][0], upper[1])[0]
        big[name] = sharded(name, lands[name], own_all, layers=(0, 1), prev=upper)
        after = big[name][1]
    small = gather_pass("gather_small", small_st, after)[0]
    outs = _adamw(small[:, None], *[_flatten_small(a, p)[None] for p in ("", "m_", "v_")], "adamw_small",
                  own=small_own[None])
    res = [_split_small(o[0], a) for o in outs]
    losses = lax.dynamic_update_index_in_dim(small[:, -1, 0], loss_own, me, 0)
    loss = jnp.sum(losses)

    off = _small_rows(a)[0]["b_ada"][0]
    dmod_all = lax.dynamic_update_index_in_dim(small[:, off:off + DEPTH * 3 * D_MODEL // SMALL_ROW],
                                               dmod.reshape(-1, SMALL_ROW), me, 0)
    dmod_all = dmod_all.reshape(N_DEV, DEPTH, 3 * D_MODEL).transpose(1, 0, 2)
    dmod_cols = lax.dynamic_slice_in_dim(dmod_all, me * ADA_COLS, ADA_COLS, axis=2)
    dmod_pad = jnp.pad(dmod_cols, ((0, 0), (0, LANE - N_DEV), (0, 0)))
    ct_pad = jnp.pad(c_all.T, ((0, 0), (0, LANE - N_DEV)))
    big["w_ada"] = sharded("w_ada", _ada_bwd(ct_pad, dmod_pad)[None])
    g_conv = jnp.concatenate([g["lru_conv_w"].reshape(DEPTH, 4, N_DEV, 64), g["ssd_conv_w"].reshape(DEPTH, 4, N_DEV, 192)],
                             axis=-1).transpose(2, 0, 1, 3)
    conv_parts = _all_to_all(g_conv, "scatter_conv")
    big["lru_conv_w"] = sharded("lru_conv_w", conv_parts[..., :64])
    big["ssd_conv_w"] = sharded("ssd_conv_w", conv_parts[..., 64:])

    out = [loss, dx[None]]
    for k in range(4):
        out += [big[n][k] if n in big else res[k][n] for n in WEIGHTS]
    return tuple(out)
```

```python
import functools

import numpy as np
import jax
import jax.numpy as jnp
from jax import lax
from jax.experimental import pallas as pl
from jax.experimental.pallas import tpu as pltpu

F32 = jnp.float32
BF16 = jnp.bfloat16
SDS = jax.ShapeDtypeStruct

N_DEV = 8
DEPTH = 4
D_MODEL = 1024
D_INNER = 2048
EPS = 1e-6
LRU_W = 512
LRU_C = 8.0
HG_W = 512
HG_CHUNK = 64
HG_HEADS = 4
SSD_W = 1024
SSD_HEADS = 16
SSD_P = 64
SSD_N = 128
SSD_CHUNK = 128
SSD_CONV = 1536
N_IN = 5648
N_PAD = 5760
OFF_HG = 0
OFF_LRU = 2048
OFF_XBC = 3072
OFF_Z = 4608
LANE = 128
VMEM_LIMIT = 56 * 1024 * 1024
NEG = -1e30

ADAM_LR = 0.001
ADAM_B1 = 0.9
ADAM_B2 = 0.999
ADAM_EPS = 1e-08
ADAM_WD = 0.01
ADAM_STEP = 10


def _cp(sem=None):
    return pltpu.CompilerParams(dimension_semantics=sem, vmem_limit_bytes=VMEM_LIMIT)


def _dg(a, b, ca, cb):
    return lax.dot_general(a, b, (((ca,), (cb,)), ((), ())), preferred_element_type=F32)


def _mm(a, b):
    return _dg(a, b, 1, 0)


def _mm_nt(a, b):
    return _dg(a, b, 1, 1)


def _mm_tn(a, b):
    return _dg(a, b, 0, 0)


def _bf(x):
    return x.astype(BF16)


def _split3(x):
    hi = x.astype(BF16)
    r = x - hi.astype(F32)
    mid = r.astype(BF16)
    lo = (r - mid.astype(F32)).astype(BF16)
    return hi, mid, lo


def _sel_r(x, m):
    hi, mid, lo = _split3(x)
    return _mm(hi, m) + _mm(mid, m) + _mm(lo, m)


def _sel_l(m, x):
    hi, mid, lo = _split3(x)
    return _mm(m, hi) + _mm(m, mid) + _mm(m, lo)


def _sel_l2(m, x):
    hi = x.astype(BF16)
    lo = (x - hi.astype(F32)).astype(BF16)
    return _mm(m, hi) + _mm(m, lo)


def _sel_tn(x, m):
    hi, mid, lo = _split3(x)
    return _mm_tn(hi, m) + _mm_tn(mid, m) + _mm_tn(lo, m)


def _sigmoid(x):
    return 1.0 / (1.0 + jnp.exp(-x))


def _silu(x):
    return x * _sigmoid(x)


def _dsilu(x):
    s = _sigmoid(x)
    return s * (1.0 + x * (1.0 - s))


def _softplus(x):
    return jnp.maximum(x, 0.0) + jnp.log(1.0 + jnp.exp(-jnp.abs(x)))


def _expm1(z):
    series = z * (1.0 + z * (1.0 / 2) * (1.0 + z * (1.0 / 3) * (1.0 + z * (1.0 / 4) * (
        1.0 + z * (1.0 / 5) * (1.0 + z * (1.0 / 6) * (1.0 + z * (1.0 / 7)))))))
    return jnp.where(jnp.abs(z) < 0.3, series, jnp.exp(z) - 1.0)


def _iota(shape, dim):
    return lax.broadcasted_iota(jnp.int32, shape, dim)


def _last_row(x, rows):
    return jnp.sum(jnp.where(rows == x.shape[0] - 1, x, 0.0), axis=0, keepdims=True)


def _shift_down(x, d, rows, fill=0.0):
    return jnp.where(rows >= d, pltpu.roll(x, d, 0), fill)


def _shift_up(x, d, rows, fill=0.0):
    n = x.shape[0]
    return jnp.where(rows < n - d, pltpu.roll(x, n - d, 0), fill)


def _conv_fwd(x, cw_ref, cb_ref, rows):
    out = cb_ref[...] + cw_ref[pl.ds(3, 1), :] * x
    for k in range(3):
        out = out + cw_ref[pl.ds(k, 1), :] * _shift_down(x, 3 - k, rows)
    return out


def _conv_bwd(x, dco, cw_ref, rows):
    dx = cw_ref[pl.ds(3, 1), :] * dco
    dws = []
    for k in range(3):
        dx = dx + cw_ref[pl.ds(k, 1), :] * _shift_up(dco, 3 - k, rows)
        dws.append(jnp.sum(dco * _shift_down(x, 3 - k, rows), axis=0, keepdims=True))
    dws.append(jnp.sum(dco * x, axis=0, keepdims=True))
    return dx, dws, jnp.sum(dco, axis=0, keepdims=True)


def _vec(n):
    return pl.BlockSpec((1, n), lambda *_: (0, 0))


def _full(shape):
    nd = len(shape)
    return pl.BlockSpec(shape, lambda *_: (0,) * nd)


def _inproj_fwd(x, nw, scale, shift, w):
    S = x.shape[0]
    tm = min(256, S)

    def body(x_ref, nw_ref, sc_ref, sh_ref, w_ref, u_ref, h_ref):
        xv = x_ref[...]
        inv = lax.rsqrt(jnp.mean(xv * xv, axis=-1, keepdims=True) + EPS)
        h = ((xv * inv) * nw_ref[...] * (1.0 + sc_ref[...]) + sh_ref[...]).astype(BF16)
        h_ref[...] = h
        u_ref[...] = _mm(h, w_ref[...])

    return pl.pallas_call(
        body, name="inproj_fwd", grid=(S // tm,),
        in_specs=[pl.BlockSpec((tm, D_MODEL), lambda i: (i, 0)), _vec(D_MODEL), _vec(D_MODEL), _vec(D_MODEL),
                  _full((D_MODEL, N_PAD))],
        out_specs=[pl.BlockSpec((tm, N_PAD), lambda i: (i, 0)), pl.BlockSpec((tm, D_MODEL), lambda i: (i, 0))],
        out_shape=[SDS((S, N_PAD), F32), SDS((S, D_MODEL), BF16)],
        compiler_params=_cp(("parallel",)),
    )(x, nw, scale, shift, w)


def _inproj_bwd_x(du, w, x, nw, scale, dxn):
    S = x.shape[0]
    tm = min(256, S)

    def body(du_ref, w_ref, x_ref, nw_ref, sc_ref, dxn_ref, dx_ref, red_ref):
        @pl.when(pl.program_id(0) == 0)
        def _():
            red_ref[...] = jnp.zeros_like(red_ref)

        dh = _mm_nt(du_ref[...], w_ref[...])
        xv = x_ref[...]
        inv = lax.rsqrt(jnp.mean(xv * xv, axis=-1, keepdims=True) + EPS)
        xhat = xv * inv
        nwv = nw_ref[...]
        g1 = 1.0 + sc_ref[...]
        dxhat = dh * nwv * g1
        dx = inv * (dxhat - xhat * jnp.mean(dxhat * xhat, axis=-1, keepdims=True))
        dx_ref[...] = dxn_ref[...] + dx
        red_ref[0:1, :] += jnp.sum(dh, axis=0, keepdims=True)
        red_ref[1:2, :] += jnp.sum(dh * xhat * nwv, axis=0, keepdims=True)
        red_ref[2:3, :] += jnp.sum(dh * xhat * g1, axis=0, keepdims=True)

    row = pl.BlockSpec((tm, D_MODEL), lambda i: (i, 0))
    return pl.pallas_call(
        body, name="inproj_bwd_x", grid=(S // tm,),
        in_specs=[pl.BlockSpec((tm, N_PAD), lambda i: (i, 0)), _full((D_MODEL, N_PAD)), row, _vec(D_MODEL),
                  _vec(D_MODEL), row],
        out_specs=[row, _full((8, D_MODEL))],
        out_shape=[SDS((S, D_MODEL), F32), SDS((8, D_MODEL), F32)],
        compiler_params=_cp(("arbitrary",)),
    )(du, w, x, nw, scale, dxn)


def _inproj_bwd_w(h, du):
    S = h.shape[0]
    tn = 640

    def body(h_ref, du_ref, gw_ref):
        gw_ref[...] = _mm_tn(h_ref[...], _bf(du_ref[...]))

    return pl.pallas_call(
        body, name="inproj_bwd_w", grid=(N_PAD // tn,),
        in_specs=[_full((S, D_MODEL)), pl.BlockSpec((S, tn), lambda j: (0, j))],
        out_specs=pl.BlockSpec((D_MODEL, tn), lambda j: (0, j)),
        out_shape=SDS((D_MODEL, N_PAD), F32),
        compiler_params=_cp(("parallel",)),
    )(h, du)


def _scan_block(a, b, rows):
    d = 1
    while d < a.shape[0]:
        a_s = _shift_down(a, d, rows, 1.0)
        b_s = _shift_down(b, d, rows, 0.0)
        b = a * b_s + b
        a = a * a_s
        d *= 2
    return a, b


def _rscan_block(c, g, rows):
    d = 1
    while d < c.shape[0]:
        c_s = _shift_up(c, d, rows, 1.0)
        g_s = _shift_up(g, d, rows, 0.0)
        g = g + c * g_s
        c = c * c_s
        d *= 2
    return c, g


LRU_BLOCK = 256


def _lru_gates(xa, wa_ref, ba_ref, wx_ref, bx_ref, lam_ref):
    sp = _softplus(-lam_ref[...])
    xb = _bf(xa)
    r = _sigmoid(_mm(xb, wa_ref[...]) + ba_ref[...])
    ig = _sigmoid(_mm(xb, wx_ref[...]) + bx_ref[...])
    la = -LRU_C * r * sp
    a = jnp.exp(la)
    mult = jnp.sqrt(-_expm1(2.0 * la))
    return sp, r, ig, la, a, mult


def _lru_specs(S):
    t128 = pl.BlockSpec((1, LANE), lambda t: (0, t))
    return [pl.BlockSpec((S, 2 * LANE), lambda t: (0, OFF_LRU // (2 * LANE) + t)),
            pl.BlockSpec((4, LANE), lambda t: (0, t)), t128,
            pl.BlockSpec((None, LANE, LANE), lambda t: (t, 0, 0)), t128,
            pl.BlockSpec((None, LANE, LANE), lambda t: (t, 0, 0)), t128, t128]


def _lru_fwd(u, cw, cb, wa, ba, wx, bx, lam, ycat):
    S = u.shape[0]
    tb = min(LRU_BLOCK, S)

    def body(u_ref, cw_ref, cb_ref, wa_ref, ba_ref, wx_ref, bx_ref, lam_ref, ycat_in, ycat_ref, h_ref, a_scr, b_scr):
        del ycat_in
        rows = _iota((S, LANE), 0)
        xa = _conv_fwd(u_ref[:, 0:LANE], cw_ref, cb_ref, rows)
        _, _, ig, _, a, mult = _lru_gates(xa, wa_ref, ba_ref, wx_ref, bx_ref, lam_ref)
        a_scr[...] = a
        b_scr[...] = mult * (ig * xa)
        rows_b = _iota((tb, LANE), 0)

        def blk(j, hprev):
            sl = pl.ds(pl.multiple_of(j * tb, tb), tb)
            acum, hloc = _scan_block(a_scr[sl, :], b_scr[sl, :], rows_b)
            hf = hloc + acum * hprev
            h_ref[sl, :] = hf
            return _last_row(hf, rows_b)

        lax.fori_loop(0, S // tb, blk, jnp.zeros((1, LANE), F32))
        ycat_ref[...] = _bf(h_ref[...] * _silu(u_ref[:, LANE:2 * LANE]))

    col = pl.BlockSpec((S, LANE), lambda t: (0, t))
    return pl.pallas_call(
        body, name="lru_fwd", grid=(LRU_W // LANE,),
        in_specs=_lru_specs(S) + [pl.BlockSpec(memory_space=pl.ANY)],
        out_specs=[col, col],
        out_shape=[SDS((S, D_INNER), BF16), SDS((S,LRU_W), F32)],
        scratch_shapes=[pltpu.VMEM((S, LANE), F32), pltpu.VMEM((S, LANE), F32)],
        input_output_aliases={8: 0},
        compiler_params=_cp(("parallel",)),
    )(u, cw, cb, wa, ba, wx, bx, lam, ycat)


def _lru_bwd(u, cw, cb, wa, ba, wx, bx, lam, h_lru, dycat, du):
    S = u.shape[0]
    tb = min(LRU_BLOCK, S)

    def body(u_ref, cw_ref, cb_ref, wa_ref, ba_ref, wx_ref, bx_ref, lam_ref, h_ref, dy_ref, du_in,
             du_ref, red_ref, gwa_ref, gwx_ref, c_scr, g_scr, l_scr):
        del du_in
        rows = _iota((S, LANE), 0)
        ax = u_ref[:, 0:LANE]
        ag = u_ref[:, LANE:2 * LANE]
        xa = _conv_fwd(ax, cw_ref, cb_ref, rows)
        sp, r, ig, la, a, mult = _lru_gates(xa, wa_ref, ba_ref, wx_ref, bx_ref, lam_ref)
        h = h_ref[...]
        dy = dy_ref[...]
        du_ref[:, LANE:2 * LANE] = _bf(dy * h * _dsilu(ag))
        c_scr[...] = _shift_up(a, 1, rows, 0.0)
        g_scr[...] = dy * _silu(ag)
        rows_b = _iota((tb, LANE), 0)
        nb = S // tb

        def blk(jj, lnext):
            j = nb - 1 - jj
            sl = pl.ds(pl.multiple_of(j * tb, tb), tb)
            ccum, lloc = _rscan_block(c_scr[sl, :], g_scr[sl, :], rows_b)
            lam_t = lloc + ccum * lnext
            l_scr[sl, :] = lam_t
            return jnp.sum(jnp.where(rows_b == 0, lam_t, 0.0), axis=0, keepdims=True)

        lax.fori_loop(0, nb, blk, jnp.zeros((1, LANE), F32))
        db = l_scr[...]
        da = db * _shift_down(h, 1, rows)
        dmult = db * ig * xa
        dig = db * mult * xa
        dxa = db * mult * ig
        dla = da * a - dmult * (a * a) / mult
        dr = -LRU_C * sp * dla
        dsp = jnp.sum(-LRU_C * r * dla, axis=0, keepdims=True)
        dlam = -dsp * _sigmoid(-lam_ref[...])
        dzr = dr * r * (1.0 - r)
        dzi = dig * ig * (1.0 - ig)
        dzr_b, dzi_b, xa_b = _bf(dzr), _bf(dzi), _bf(xa)
        dxa = dxa + _mm_nt(dzr_b, wa_ref[...]) + _mm_nt(dzi_b, wx_ref[...])
        gwa_ref[...] = _mm_tn(xa_b, dzr_b)
        gwx_ref[...] = _mm_tn(xa_b, dzi_b)
        dax, dws, dcb = _conv_bwd(ax, dxa, cw_ref, rows)
        du_ref[:, 0:LANE] = _bf(dax)
        parts = dws + [dcb, jnp.sum(dzr, axis=0, keepdims=True), jnp.sum(dzi, axis=0, keepdims=True), dlam]
        for n, p in enumerate(parts):
            red_ref[pl.ds(n, 1), :] = p

    col = pl.BlockSpec((S, LANE), lambda t: (0, t))
    gw = pl.BlockSpec((None, LANE, LANE), lambda t: (t, 0, 0))
    return pl.pallas_call(
        body, name="lru_bwd", grid=(LRU_W // LANE,),
        in_specs=_lru_specs(S) + [col, col, pl.BlockSpec(memory_space=pl.ANY)],
        out_specs=[pl.BlockSpec((S, 2 * LANE), lambda t: (0, OFF_LRU // (2 * LANE) + t)),
                   pl.BlockSpec((8, LANE), lambda t: (0, t)), gw, gw],
        out_shape=[SDS((S, N_PAD), BF16), SDS((8, LRU_W), F32), SDS((4, LANE, LANE), F32), SDS((4, LANE, LANE), F32)],
        scratch_shapes=[pltpu.VMEM((S, LANE), F32)] * 3,
        input_output_aliases={10: 0},
        compiler_params=_cp(("parallel",)),
    )(u, cw, cb, wa, ba, wx, bx, lam, h_lru, dycat, du)


HG_LEVELS = 6


def _hg_consts():
    C = HG_CHUNK
    t = np.arange(C)[:, None]
    r = np.arange(C)[None, :]
    mats = []
    for l in range(HG_LEVELS):
        b = 1 << l
        upper = (t % (2 * b)) >= b
        anchor = (t // (2 * b)) * 2 * b + b - 1
        mats.append((upper & (r > anchor) & (r <= t)) | ((~upper) & (r > t) & (r <= anchor)))
    mats.append(r <= t)
    mats.append(r > t)
    return np.concatenate(mats, 0).astype(np.float32)


def _hg_factors(hf, lb, mall):
    s = _sigmoid(hf)
    f = lb + (1.0 - lb) * s
    lf = jnp.log(f)
    k = (1.0 - lb) * _sigmoid(-hf)
    e = jnp.exp(_sel_l(mall, lf))
    C = HG_CHUNK
    eq = [e[l * C:(l + 1) * C] for l in range(HG_LEVELS)]
    ecum = e[HG_LEVELS * C:(HG_LEVELS + 1) * C]
    erem = e[(HG_LEVELS + 1) * C:(HG_LEVELS + 2) * C]
    return s, f, k, eq, eq, ecum, erem


def _hg_masks():
    C = HG_CHUNK
    ri, ci = _iota((C, C), 0), _iota((C, C), 1)
    rr = _iota((C, LANE), 0)
    gm = [(lax.shift_right_logical(ri, l + 1) == lax.shift_right_logical(ci, l + 1)).astype(F32)
          for l in range(HG_LEVELS)]
    up = [(lax.shift_right_logical(rr, l) & 1) == 1 for l in range(HG_LEVELS)]
    eye = (ri == ci).astype(F32)
    return gm, up, eye, rr


def _hg_scores(qh, kh, eq, ek, sl, gm, up, eye):
    qs, ks = [], []
    p = _mm_nt(_bf(qh), _bf(kh)) * eye
    for l in range(HG_LEVELS):
        ql = jnp.where(up[l], qh * eq[l][:, sl], 0.0)
        kl = jnp.where(up[l], 0.0, kh * ek[l][:, sl])
        p = p + _mm_nt(_bf(ql), _bf(kl)) * gm[l]
        qs.append(ql)
        ks.append(kl)
    return p, qs, ks


HG_SUB = 2


def _hg_fwd(u, lb, nw, mall, ycat):
    S = u.shape[0]
    C = HG_CHUNK
    n = S // C
    rows = HG_SUB * C

    def body(u_ref, lb_ref, nw_ref, mall_ref, ycat_in, ycat_ref, o_ref, st_ref, st):
        del ycat_in

        @pl.when(pl.program_id(0) == 0)
        def _():
            st[...] = jnp.zeros_like(st)

        gm, up, eye, rr = _hg_masks()
        for sub in range(HG_SUB):
            r = slice(sub * C, (sub + 1) * C)
            q = _silu(u_ref[r, 0:512])
            v = u_ref[r, 1024:1536]
            _, _, k, eq, ek, ecum, erem = _hg_factors(u_ref[r, 512:1024], lb_ref[...], mall_ref[...])
            for h in range(HG_HEADS):
                sl = slice(h * LANE, (h + 1) * LANE)
                qh, kh, vh = q[:, sl], k[:, sl], _bf(v[:, sl])
                p, _, _ = _hg_scores(qh, kh, eq, ek, sl, gm, up, eye)
                sth = st[h]
                st_ref[sub, h] = sth
                o_ref[r, sl] = _mm(_bf(p), vh) + _mm_nt(_bf(qh * ecum[:, sl]), _bf(sth))
                st[h] = sth * _last_row(ecum[:, sl], rr) + _mm_tn(vh, _bf(kh * erem[:, sl]))
            o = o_ref[r, :]
            inv = lax.rsqrt(jnp.mean(o * o, axis=-1, keepdims=True) + EPS)
            ycat_ref[r, :] = _bf((o * inv) * nw_ref[...] * _silu(u_ref[r, 1536:2048]))

    return pl.pallas_call(
        body, name="hg_fwd", grid=(n // HG_SUB,),
        in_specs=[pl.BlockSpec((rows, 2048), lambda i: (i, 0)), _vec(HG_W), _vec(HG_W), _full(mall.shape),
                  pl.BlockSpec(memory_space=pl.ANY)],
        out_specs=[pl.BlockSpec((rows, HG_W), lambda i: (i, 1)), pl.BlockSpec((rows, HG_W), lambda i: (i, 0)),
                   pl.BlockSpec((HG_SUB, HG_HEADS, LANE, LANE), lambda i: (i, 0, 0, 0))],
        out_shape=[SDS((S, D_INNER), BF16), SDS((S,HG_W), F32), SDS((n, HG_HEADS, LANE, LANE), F32)],
        scratch_shapes=[pltpu.VMEM((HG_HEADS, LANE, LANE), F32)],
        input_output_aliases={4: 0},
        compiler_params=_cp(("arbitrary",)),
    )(u, lb, nw, mall, ycat)


def _hg_bwd(u, lb, nw, mall, mall_t, o_b, states, dycat, du):
    S = u.shape[0]
    C = HG_CHUNK
    n = S // C
    nb = n // HG_SUB
    rows = HG_SUB * C
    L2 = HG_LEVELS

    def body(u_ref, lb_ref, nw_ref, mall_ref, mallt_ref, o_ref, st_ref, dy_ref, du_in, du_ref, red_ref,
             dst, dlast_s, dq_s, dk_s, dex):
        del du_in

        @pl.when(pl.program_id(0) == 0)
        def _():
            dst[...] = jnp.zeros_like(dst)
            red_ref[...] = jnp.zeros_like(red_ref)

        lb = lb_ref[...]
        nwv = nw_ref[...]
        gm, up, eye, rr = _hg_masks()
        for sub in reversed(range(HG_SUB)):
            r = slice(sub * C, (sub + 1) * C)
            hq, hf, hg = u_ref[r, 0:512], u_ref[r, 512:1024], u_ref[r, 1536:2048]
            q = _silu(hq)
            v = u_ref[r, 1024:1536]
            s, f, k, eq, ek, ecum, erem = _hg_factors(hf, lb, mall_ref[...])
            o = o_ref[r, :]
            dy = dy_ref[r, :]
            inv = lax.rsqrt(jnp.mean(o * o, axis=-1, keepdims=True) + EPS)
            ohat = o * inv
            du_ref[r, 1536:2048] = _bf(dy * ohat * nwv * _dsilu(hg))
            dn = dy * _silu(hg)
            red_ref[0:1, :] += jnp.sum(dn * ohat, axis=0, keepdims=True)
            dohat = dn * nwv
            do = inv * (dohat - ohat * jnp.mean(dohat * ohat, axis=-1, keepdims=True))
            for h in range(HG_HEADS):
                sl = slice(h * LANE, (h + 1) * LANE)
                qh, kh, vh, doh = q[:, sl], k[:, sl], _bf(v[:, sl]), _bf(do[:, sl])
                p, qs, ks = _hg_scores(qh, kh, eq, ek, sl, gm, up, eye)
                st_f = st_ref[sub, h]
                sth = _bf(st_f)
                dsth = dst[h]
                dsth_b = _bf(dsth)
                qt = qh * ecum[:, sl]
                kt = kh * erem[:, sl]
                elast = _last_row(ecum[:, sl], rr)
                dp = _mm_nt(doh, vh)
                du_ref[r, 1024 + h * LANE:1024 + (h + 1) * LANE] = _bf(_mm_tn(_bf(p), doh) + _mm_nt(_bf(kt), dsth_b))
                dpe = _bf(dp * eye)
                dqt = _mm(doh, sth)
                dkt = _mm(vh, dsth_b)
                dq = dqt * ecum[:, sl] + _mm(dpe, _bf(kh))
                dk = dkt * erem[:, sl] + _mm_tn(dpe, _bf(qh))
                dex[sub, L2 * C:(L2 + 1) * C, sl] = dqt * qt
                dex[sub, (L2 + 1) * C:(L2 + 2) * C, sl] = dkt * kt
                for l in range(HG_LEVELS):
                    dpl = _bf(dp * gm[l])
                    dql = _mm(dpl, _bf(ks[l]))
                    dkl = _mm_tn(dpl, _bf(qs[l]))
                    dq = dq + jnp.where(up[l], dql * eq[l][:, sl], 0.0)
                    dk = dk + jnp.where(up[l], 0.0, dkl * ek[l][:, sl])
                    dex[sub, l * C:(l + 1) * C, sl] = dql * qs[l] + dkl * ks[l]
                dlast_s[sub, :, sl] = jnp.sum(dsth * st_f, axis=0, keepdims=True) * elast
                dst[h] = dsth * elast + _mm_tn(doh, _bf(qt))
                dq_s[sub, :, sl] = dq
                dk_s[sub, :, sl] = dk
            dq = dq_s[sub]
            dk = dk_s[sub]
            dlf = _sel_l2(mallt_ref[...], dex[sub]) + dlast_s[sub]
            du_ref[r, 0:512] = _bf(dq * _dsilu(hq))
            t = (1.0 - s) * (dlf / f - dk)
            du_ref[r, 512:1024] = _bf((1.0 - lb) * s * t)
            red_ref[1:2, :] += jnp.sum(t, axis=0, keepdims=True)

    rev = lambda i: (nb - 1 - i, 0)
    return pl.pallas_call(
        body, name="hg_bwd", grid=(nb,),
        in_specs=[pl.BlockSpec((rows, 2048), rev), _vec(HG_W), _vec(HG_W), _full(mall.shape), _full(mall_t.shape),
                  pl.BlockSpec((rows, HG_W), rev),
                  pl.BlockSpec((HG_SUB, HG_HEADS, LANE, LANE), lambda i: (nb - 1 - i, 0, 0, 0)),
                  pl.BlockSpec((rows, HG_W), lambda i: (nb - 1 - i, 1)), pl.BlockSpec(memory_space=pl.ANY)],
        out_specs=[pl.BlockSpec((rows, 2048), rev), pl.BlockSpec((8, HG_W), lambda i: (0, 0))],
        out_shape=[SDS((S, N_PAD), BF16), SDS((8, HG_W), F32)],
        scratch_shapes=[pltpu.VMEM((HG_HEADS, LANE, LANE), F32), pltpu.VMEM((HG_SUB, 1, HG_W), F32),
                        pltpu.VMEM((HG_SUB, C, HG_W), F32), pltpu.VMEM((HG_SUB, C, HG_W), F32),
                        pltpu.VMEM((HG_SUB, (L2 + 2) * C, HG_W), F32)],
        input_output_aliases={8: 0},
        compiler_params=_cp(("arbitrary",)),
    )(u, lb, nw, mall, mall_t, o_b, states, dycat, du)


def _ssdconv_fwd(u, cw, cb):
    S = u.shape[0]

    def body(u_ref, cw_ref, cb_ref, out_ref):
        rows = _iota((S, LANE), 0)
        out_ref[...] = _silu(_conv_fwd(u_ref[...], cw_ref, cb_ref, rows))

    return pl.pallas_call(
        body, name="ssdconv_fwd", grid=(SSD_CONV // LANE,),
        in_specs=[pl.BlockSpec((S, LANE), lambda t: (0, OFF_XBC // LANE + t)), pl.BlockSpec((4, LANE), lambda t: (0, t)),
                  pl.BlockSpec((1, LANE), lambda t: (0, t))],
        out_specs=pl.BlockSpec((S, LANE), lambda t: (0, t)),
        out_shape=SDS((S, SSD_CONV), F32),
        compiler_params=_cp(("parallel",)),
    )(u, cw, cb)


def _ssdconv_bwd(u, cw, cb, dxbc, du):
    S = u.shape[0]

    def body(u_ref, cw_ref, cb_ref, d_ref, du_in, du_ref, red_ref):
        del du_in
        rows = _iota((S, LANE), 0)
        x = u_ref[...]
        dco = d_ref[...] * _dsilu(_conv_fwd(x, cw_ref, cb_ref, rows))
        dx, dws, dcb = _conv_bwd(x, dco, cw_ref, rows)
        du_ref[...] = _bf(dx)
        for n, p in enumerate(dws + [dcb]):
            red_ref[pl.ds(n, 1), :] = p
        red_ref[pl.ds(5, 3), :] = jnp.zeros((3, LANE), F32)

    ucol = pl.BlockSpec((S, LANE), lambda t: (0, OFF_XBC // LANE + t))
    return pl.pallas_call(
        body, name="ssdconv_bwd", grid=(SSD_CONV // LANE,),
        in_specs=[ucol, pl.BlockSpec((4, LANE), lambda t: (0, t)), pl.BlockSpec((1, LANE), lambda t: (0, t)),
                  pl.BlockSpec((S, LANE), lambda t: (0, t)), pl.BlockSpec(memory_space=pl.ANY)],
        out_specs=[ucol, pl.BlockSpec((8, LANE), lambda t: (0, t))],
        out_shape=[SDS((S, N_PAD), BF16), SDS((8, SSD_CONV), F32)],
        input_output_aliases={4: 0},
        compiler_params=_cp(("parallel",)),
    )(u, cw, cb, dxbc, du)


def _ssd_consts():
    e64 = np.zeros((LANE, SSD_W), np.float32)
    for h in range(SSD_HEADS):
        e64[h, h * SSD_P:(h + 1) * SSD_P] = 1.0
    T = SSD_CHUNK
    tril = (np.arange(T)[None, :] <= np.arange(T)[:, None]).astype(np.float32)
    return e64, tril, tril.T.copy()


def _ssd_common(zdt, bias_ref, alog_ref, tril, e64, cum_ref, cumt_ref):
    T = SSD_CHUNK
    lane = _iota((1, LANE), 1)
    a_neg = jnp.where(lane < SSD_HEADS, -jnp.exp(alog_ref[...]), 0.0)
    dtpre = zdt[:, SSD_W:SSD_W + LANE] + bias_ref[...]
    dt = _softplus(dtpre)
    cum = _sel_l(tril, dt * a_neg)
    cum_ref[...] = cum
    cumt_ref[...] = cum.T
    cum_x = _sel_r(cum, e64)
    last_x = _last_row(cum_x, _iota((T, SSD_W), 0))
    ecum_x = jnp.exp(cum_x)
    erem_x = jnp.exp(last_x - cum_x)
    elast_x = jnp.exp(last_x)
    dt_x = _sel_r(dt, e64)
    return a_neg, dtpre, dt, ecum_x, erem_x, elast_x, dt_x


def _ssd_decay(cum_ref, cumt_ref, h, causal):
    T = SSD_CHUNK
    diff = jnp.broadcast_to(cum_ref[:, pl.ds(h, 1)], (T, T)) - cumt_ref[pl.ds(h, 1), :]
    return jnp.exp(jnp.where(causal, diff, NEG))


def _group_norm_fwd(y1, nwv):
    outs, invs = [], []
    for g in range(2):
        seg = y1[:, g * 512:(g + 1) * 512]
        inv = lax.rsqrt(jnp.mean(seg * seg, axis=-1, keepdims=True) + EPS)
        outs.append(seg * inv * nwv[:, g * 512:(g + 1) * 512])
        invs.append(inv)
    return outs, invs


def _ssd_fwd(u, xbc, bias, alog, dskip_x, nw, consts, ycat):
    S = u.shape[0]
    T = SSD_CHUNK
    n = S // T
    e64, tril, _ = consts

    def body(u_ref, xbc_ref, bias_ref, alog_ref, dx_ref, nw_ref, e64_ref, tril_ref, ycat_in,
             ycat_ref, y_ref, st_ref, st, cumt, cum_e):
        del ycat_in

        @pl.when(pl.program_id(0) == 0)
        def _():
            st[...] = jnp.zeros_like(st)

        zdt = u_ref[...]
        z = zdt[:, 0:SSD_W]
        xs = xbc_ref[:, 0:SSD_W]
        _, _, _, ecum_x, erem_x, elast_x, dt_x = _ssd_common(
            zdt, bias_ref, alog_ref, tril_ref[...], e64_ref[...], cum_e, cumt)
        causal = _iota((T, T), 0) >= _iota((T, T), 1)
        lo = _iota((T, LANE), 1) < SSD_P
        xdt = xs * dt_x
        xrem = xdt * erem_x
        st_ref[...] = st[...]
        for g in range(2):
            gs = slice(g * 512, (g + 1) * 512)
            bg = _bf(xbc_ref[:, SSD_W + g * LANE:SSD_W + (g + 1) * LANE])
            cg = _bf(xbc_ref[:, SSD_W + 256 + g * LANE:SSD_W + 256 + (g + 1) * LANE])
            cb = _mm_nt(cg, bg)
            yin = _mm(cg, _bf(st[:, gs])) * ecum_x[:, gs]
            for j in range(4):
                h0 = 8 * g + 2 * j
                cs = slice(h0 * SSD_P, (h0 + 2) * SSD_P)
                xp = xdt[:, cs]
                s0 = _bf(cb * _ssd_decay(cum_e, cumt, h0, causal))
                s1 = _bf(cb * _ssd_decay(cum_e, cumt, h0 + 1, causal))
                y_ref[:, cs] = (_mm(s0, _bf(jnp.where(lo, xp, 0.0))) + _mm(s1, _bf(jnp.where(lo, 0.0, xp)))
                                + yin[:, j * LANE:(j + 1) * LANE])
            st[:, gs] = st[:, gs] * elast_x[:, gs] + _mm_tn(bg, _bf(xrem[:, gs]))
        y1 = (y_ref[...] + dx_ref[...] * xs) * _silu(z)
        outs, _ = _group_norm_fwd(y1, nw_ref[...])
        for g in range(2):
            ycat_ref[:, g * 512:(g + 1) * 512] = _bf(outs[g])

    return pl.pallas_call(
        body, name="ssd_fwd", grid=(n,),
        in_specs=[pl.BlockSpec((T, SSD_W + LANE), lambda i: (i, OFF_Z // (SSD_W + LANE))),
                  pl.BlockSpec((T, SSD_CONV), lambda i: (i, 0)), _vec(LANE), _vec(LANE), _vec(SSD_W), _vec(SSD_W),
                  _full(e64.shape), _full(tril.shape), pl.BlockSpec(memory_space=pl.ANY)],
        out_specs=[pl.BlockSpec((T, SSD_W), lambda i: (i, 1)), pl.BlockSpec((T, SSD_W), lambda i: (i, 0)),
                   pl.BlockSpec((None, SSD_N, SSD_W), lambda i: (i, 0, 0))],
        out_shape=[SDS((S, D_INNER), BF16), SDS((S,SSD_W), F32), SDS((n, SSD_N, SSD_W), F32)],
        scratch_shapes=[pltpu.VMEM((SSD_N, SSD_W), F32), pltpu.VMEM((LANE, T), F32), pltpu.VMEM((T, LANE), F32)],
        input_output_aliases={8: 0},
        compiler_params=_cp(("arbitrary",)),
    )(u, xbc, bias, alog, dskip_x, nw, _bfc(e64), _bfc(tril), ycat)


def _ssd_bwd(u, xbc, bias, alog, dskip_x, nw, consts, y_ssd, states, dycat, du):
    S = u.shape[0]
    T = SSD_CHUNK
    n = S // T
    e64, tril, triu = consts
    e64t = np.ascontiguousarray(e64.T)

    def body(u_ref, xbc_ref, bias_ref, alog_ref, dx_ref, nw_ref, e64_ref, e64t_ref, tril_ref, triu_ref,
             y_ref, st_ref, dy_ref, du_in, du_ref, dxbc_ref, red_ref, dst, dl_s, cumt, dxdt_s, dy0_s, gb_s, gc_s,
             cum_e, cs_s):
        del du_in

        @pl.when(pl.program_id(0) == 0)
        def _():
            dst[...] = jnp.zeros_like(dst)
            red_ref[...] = jnp.zeros_like(red_ref)
            cs_s[...] = jnp.zeros_like(cs_s)

        zdt = u_ref[...]
        z = zdt[:, 0:SSD_W]
        xs = xbc_ref[:, 0:SSD_W]
        a_neg, dtpre, dt, ecum_x, erem_x, elast_x, dt_x = _ssd_common(
            zdt, bias_ref, alog_ref, tril_ref[...], e64_ref[...], cum_e, cumt)
        causal = _iota((T, T), 0) >= _iota((T, T), 1)
        lo = _iota((T, LANE), 1) < SSD_P
        xdt = xs * dt_x
        xrem = xdt * erem_x
        y = y_ref[...]
        dxv = dx_ref[...]
        nwv = nw_ref[...]
        sz = _silu(z)
        y0 = y + dxv * xs
        y1 = y0 * sz
        for g in range(2):
            gs = slice(g * 512, (g + 1) * 512)
            seg = y1[:, gs]
            inv = lax.rsqrt(jnp.mean(seg * seg, axis=-1, keepdims=True) + EPS)
            shat = seg * inv
            dyg = dy_ref[:, gs]
            red_ref[0:1, gs] += jnp.sum(dyg * shat, axis=0, keepdims=True)
            dsh = dyg * nwv[:, gs]
            dy1g = inv * (dsh - shat * jnp.mean(dsh * shat, axis=-1, keepdims=True))
            du_ref[:, gs] = _bf(dy1g * y0[:, gs] * _dsilu(z[:, gs]))
            dy0_s[:, gs] = dy1g * sz[:, gs]
        dy0 = dy0_s[...]
        red_ref[1:2, :] += jnp.sum(dy0 * xs, axis=0, keepdims=True)
        dyin = dy0 * ecum_x
        lane = _iota((T, LANE), 1)
        dcum = jnp.zeros((T, LANE), F32)

        def decay_grad(h, gm):
            cs_s[pl.ds(h, 1), :] = jnp.sum(gm, axis=0, keepdims=True)
            return jnp.where(lane == h, jnp.sum(gm, axis=1, keepdims=True), 0.0)

        for g in range(2):
            gs = slice(g * 512, (g + 1) * 512)
            bg = _bf(xbc_ref[:, SSD_W + g * LANE:SSD_W + (g + 1) * LANE])
            cg = _bf(xbc_ref[:, SSD_W + 256 + g * LANE:SSD_W + 256 + (g + 1) * LANE])
            cb = _mm_nt(cg, bg)
            dst_f, st_f = dst[:, gs], st_ref[:, gs]
            dstg = _bf(dst_f)
            stg = _bf(st_f)
            dyin_g = _bf(dyin[:, gs])
            xrem_g = _bf(xrem[:, gs])
            dcb = jnp.zeros((T, T), F32)
            dxr = _mm(bg, dstg)
            dxdt_s[:, gs] = dxr * erem_x[:, gs]
            gc_s[:, gs] = dxr * xrem[:, gs]
            gb_s[:, gs] = dyin[:, gs] * _mm(cg, stg)
            dl_s[:, gs] = jnp.sum(dst_f * st_f, axis=0, keepdims=True) * elast_x[:, gs]
            for j in range(4):
                h0 = 8 * g + 2 * j
                cs = slice(h0 * SSD_P, (h0 + 2) * SSD_P)
                xp = xdt[:, cs]
                dyp = dy0[:, cs]
                x_lo, x_hi = _bf(jnp.where(lo, xp, 0.0)), _bf(jnp.where(lo, 0.0, xp))
                d_lo, d_hi = _bf(jnp.where(lo, dyp, 0.0)), _bf(jnp.where(lo, 0.0, dyp))
                l0 = _ssd_decay(cum_e, cumt, h0, causal)
                l1 = _ssd_decay(cum_e, cumt, h0 + 1, causal)
                s0 = cb * l0
                s1 = cb * l1
                ds0 = _mm_nt(d_lo, x_lo)
                ds1 = _mm_nt(d_hi, x_hi)
                dcb = dcb + ds0 * l0 + ds1 * l1
                dxdt_s[:, cs] += _mm_tn(_bf(s0), d_lo) + _mm_tn(_bf(s1), d_hi)
                dcum = dcum + decay_grad(h0, ds0 * s0) + decay_grad(h0 + 1, ds1 * s1)
            dcb_b = _bf(dcb)
            dxbc_ref[:, SSD_W + g * LANE:SSD_W + (g + 1) * LANE] = _mm_tn(dcb_b, cg) + _mm_nt(xrem_g, dstg)
            dxbc_ref[:, SSD_W + 256 + g * LANE:SSD_W + 256 + (g + 1) * LANE] = _mm(dcb_b, bg) + _mm_nt(dyin_g, stg)
            dst[:, gs] = dst_f * elast_x[:, gs] + _mm_tn(cg, dyin_g)
        dxdt = dxdt_s[...]
        dxbc_ref[:, 0:SSD_W] = dxdt * dt_x + dy0 * dxv
        e64t = e64t_ref[...]
        gc = gc_s[...]
        dlast_x = jnp.sum(gc, axis=0, keepdims=True) + dl_s[...]
        dlast = jnp.max(_sel_r(jnp.broadcast_to(dlast_x, (8, SSD_W)), e64t), axis=0, keepdims=True)
        dcum = (dcum - cs_s[...].T + _sel_r(gb_s[...] - gc, e64t)
                + jnp.where(_iota((T, LANE), 0) == T - 1, dlast, 0.0))
        dda = _sel_l(triu_ref[...], dcum)
        ddt = dda * a_neg + _sel_r(dxdt * xs, e64t)
        ddtpre = ddt * _sigmoid(dtpre)
        du_ref[:, SSD_W:SSD_W + LANE] = _bf(jnp.where(lane < SSD_HEADS, ddtpre, 0.0))
        red_ref[2:3, 0:LANE] += jnp.sum(ddtpre, axis=0, keepdims=True)
        red_ref[3:4, 0:LANE] += jnp.sum(dda * dt, axis=0, keepdims=True)

    rev = lambda i: (n - 1 - i, 0)
    return pl.pallas_call(
        body, name="ssd_bwd", grid=(n,),
        in_specs=[pl.BlockSpec((T, SSD_W + LANE), lambda i: (n - 1 - i, OFF_Z // (SSD_W + LANE))),
                  pl.BlockSpec((T, SSD_CONV), rev), _vec(LANE), _vec(LANE), _vec(SSD_W), _vec(SSD_W),
                  _full(e64.shape), _full(e64t.shape), _full(tril.shape), _full(triu.shape),
                  pl.BlockSpec((T, SSD_W), rev), pl.BlockSpec((None, SSD_N, SSD_W), lambda i: (n - 1 - i, 0, 0)),
                  pl.BlockSpec((T, SSD_W), lambda i: (n - 1 - i, 1)), pl.BlockSpec(memory_space=pl.ANY)],
        out_specs=[pl.BlockSpec((T, SSD_W + LANE), lambda i: (n - 1 - i, OFF_Z // (SSD_W + LANE))),
                   pl.BlockSpec((T, SSD_CONV), rev), pl.BlockSpec((8, SSD_W), lambda i: (0, 0))],
        out_shape=[SDS((S, N_PAD), BF16), SDS((S, SSD_CONV), F32), SDS((8, SSD_W), F32)],
        scratch_shapes=[pltpu.VMEM((SSD_N, SSD_W), F32), pltpu.VMEM((1, SSD_W), F32), pltpu.VMEM((LANE, T), F32)]
        + [pltpu.VMEM((T, SSD_W), F32)] * 4 + [pltpu.VMEM((T, LANE), F32), pltpu.VMEM((LANE, T), F32)],
        input_output_aliases={13: 0},
        compiler_params=_cp(("arbitrary",)),
    )(u, xbc, bias, alog, dskip_x, nw, _bfc(e64), _bfc(e64t), _bfc(tril), _bfc(triu), y_ssd, states, dycat, du)


def _bfc(a):
    return jnp.asarray(a, BF16)


def _outproj_fwd(ycat, wo, x, gate):
    S = x.shape[0]
    tm = min(512, S)

    def body(yc_ref, wo_ref, x_ref, g_ref, xn_ref, y_ref):
        y = _mm(_bf(yc_ref[...]), wo_ref[...])
        y_ref[...] = y
        xn_ref[...] = x_ref[...] + g_ref[...] * y

    row = pl.BlockSpec((tm, D_MODEL), lambda i: (i, 0))
    return pl.pallas_call(
        body, name="outproj_fwd", grid=(S // tm,),
        in_specs=[pl.BlockSpec((tm, D_INNER), lambda i: (i, 0)), _full((D_INNER, D_MODEL)), row, _vec(D_MODEL)],
        out_specs=[row, row],
        out_shape=[SDS((S, D_MODEL), F32), SDS((S, D_MODEL), F32)],
        compiler_params=_cp(("parallel",)),
    )(ycat, wo, x, gate)


def _outproj_bwd(dxn, y, gate, ycat, wo):
    S = dxn.shape[0]
    tm = min(512, S)

    def body(dx_ref, y_ref, g_ref, yc_ref, wo_ref, dyc_ref, gwo_ref, dg_ref, acc):
        @pl.when(pl.program_id(0) == 0)
        def _():
            acc[...] = jnp.zeros_like(acc)
            dg_ref[...] = jnp.zeros_like(dg_ref)

        dxv = dx_ref[...]
        dy = _bf(dxv * g_ref[...])
        dg_ref[0:1, :] += jnp.sum(dxv * y_ref[...], axis=0, keepdims=True)
        dyc_ref[...] = _mm_nt(dy, wo_ref[...])
        acc[...] += _mm_tn(_bf(yc_ref[...]), dy)

        @pl.when(pl.program_id(0) == pl.num_programs(0) - 1)
        def _():
            gwo_ref[...] = acc[...].astype(BF16)

    row = pl.BlockSpec((tm, D_MODEL), lambda i: (i, 0))
    wide = pl.BlockSpec((tm, D_INNER), lambda i: (i, 0))
    return pl.pallas_call(
        body, name="outproj_bwd", grid=(S // tm,),
        in_specs=[row, row, _vec(D_MODEL), wide, _full((D_INNER, D_MODEL))],
        out_specs=[wide, _full((D_INNER, D_MODEL)), _full((8, D_MODEL))],
        out_shape=[SDS((S, D_INNER), F32), SDS((D_INNER, D_MODEL), BF16), SDS((8, D_MODEL), F32)],
        scratch_shapes=[pltpu.VMEM((D_INNER, D_MODEL), F32)],
        compiler_params=_cp(("arbitrary",)),
    )(dxn, y, gate, ycat, wo)


def _loss_head(x, fw, target):
    S = x.shape[0]
    tm = min(512, S)

    def body(x_ref, fw_ref, t_ref, dx_ref, red_ref):
        @pl.when(pl.program_id(0) == 0)
        def _():
            red_ref[...] = jnp.zeros_like(red_ref)

        xv = x_ref[...]
        fwv = fw_ref[...]
        inv = lax.rsqrt(jnp.mean(xv * xv, axis=-1, keepdims=True) + EPS)
        xhat = xv * inv
        err = xhat * fwv - t_ref[...]
        col = jnp.sum(err * err, axis=0, keepdims=True)
        red_ref[1:2, :] += jnp.broadcast_to(jnp.sum(col, axis=1, keepdims=True) * (0.5 / D_MODEL), (1, D_MODEL))
        dy = err * (1.0 / D_MODEL)
        red_ref[0:1, :] += jnp.sum(dy * xhat, axis=0, keepdims=True)
        dxhat = dy * fwv
        dx_ref[...] = inv * (dxhat - xhat * jnp.mean(dxhat * xhat, axis=-1, keepdims=True))

    row = pl.BlockSpec((tm, D_MODEL), lambda i: (i, 0))
    return pl.pallas_call(
        body, name="loss_head", grid=(S // tm,),
        in_specs=[row, _vec(D_MODEL), row],
        out_specs=[row, _full((8, D_MODEL))],
        out_shape=[SDS((S, D_MODEL), F32), SDS((8, D_MODEL), F32)],
        compiler_params=_cp(("arbitrary",)),
    )(x, fw, target)


ADA_COLS = 3 * D_MODEL // N_DEV


def _ada_fwd(c_all, w_ada, b_cols):
    def body(c_ref, w_ref, b_ref, out_ref):
        out_ref[...] = _mm(_bf(_silu(c_ref[...])), _bf(w_ref[...])) + b_ref[...]

    return pl.pallas_call(
        body, name="ada_fwd", grid=(DEPTH,),
        in_specs=[_full((N_DEV, D_MODEL)), pl.BlockSpec((None, D_MODEL, ADA_COLS), lambda l: (l, 0, 0)),
                  pl.BlockSpec((None, 1, ADA_COLS), lambda l: (l, 0, 0))],
        out_specs=pl.BlockSpec((None, N_DEV, ADA_COLS), lambda l: (l, 0, 0)),
        out_shape=SDS((DEPTH, N_DEV, ADA_COLS), F32),
        compiler_params=_cp(("parallel",)),
    )(c_all, w_ada, b_cols)


def _ada_bwd(ct_pad, dmod_pad):
    def body(c_ref, d_ref, out_ref):
        out_ref[...] = _mm(_bf(_silu(c_ref[...])), _bf(d_ref[...]))

    return pl.pallas_call(
        body, name="ada_bwd", grid=(DEPTH,),
        in_specs=[_full((D_MODEL, LANE)), pl.BlockSpec((None, LANE, ADA_COLS), lambda l: (l, 0, 0))],
        out_specs=pl.BlockSpec((None, D_MODEL, ADA_COLS), lambda l: (l, 0, 0)),
        out_shape=SDS((DEPTH, D_MODEL, ADA_COLS), F32),
        compiler_params=_cp(("parallel",)),
    )(ct_pad, dmod_pad)


def _adamw(parts, w, m, v, name, own=None, layers=None, prev=None):
    n, L, R, C = parts.shape
    lo, hi = layers or (0, L)
    tr = R
    while tr * C * 4 > (1 << 20) and tr % 16 == 0:
        tr //= 2
    first = 1 if own is None else 2

    def body(*refs):
        p_ref = refs[0]
        w_ref, m_ref, v_ref = refs[first:first + 3]
        g_ref, d_ref, mo_ref, vo_ref = refs[-4:]

        def part(k):
            if own is None:
                return p_ref[k].astype(F32)
            me = 4 * lax.axis_index("x") + 2 * lax.axis_index("y") + lax.axis_index("c")
            return jnp.where(me == k, refs[1][...], p_ref[k]).astype(F32)

        g = part(0)
        for k in range(1, n):
            g = g + part(k)
        mn = ADAM_B1 * m_ref[...] + (1.0 - ADAM_B1) * g
        vn = ADAM_B2 * v_ref[...] + (1.0 - ADAM_B2) * (g * g)
        m_hat = mn / (1.0 - ADAM_B1 ** ADAM_STEP)
        v_hat = vn / (1.0 - ADAM_B2 ** ADAM_STEP)
        g_ref[...] = g
        d_ref[...] = -ADAM_LR * (m_hat / (jnp.sqrt(v_hat) + ADAM_EPS) + ADAM_WD * w_ref[...])
        mo_ref[...] = mn
        vo_ref[...] = vn

    blk = pl.BlockSpec((None, tr, C), lambda l, i: (lo + l, i, 0))
    own_blk = [] if own is None else [pl.BlockSpec((None, tr, C), lambda l, i: (l, i, 0))]
    n_blk = 3 if own is None else 4
    return pl.pallas_call(
        body, name=name, grid=(hi - lo, R // tr),
        in_specs=[pl.BlockSpec((n, None, tr, C), lambda l, i: (0, lo + l, i, 0))] + own_blk + [blk] * 3
        + ([] if prev is None else [ANY] * 4),
        out_specs=[blk] * 4,
        out_shape=[SDS((L, R, C), F32)] * 4,
        input_output_aliases={} if prev is None else {1 + n_blk + k: k for k in range(4)},
        compiler_params=_cp(("parallel", "parallel")),
    )(parts, *([] if own is None else [own]), w, m, v, *([] if prev is None else prev))


MESH = pl.DeviceIdType.MESH
ANY = pl.BlockSpec(memory_space=pl.ANY)


def _all_gather(v, name):
    def body(v_ref, out_ref, send_sems, recv_sems, local_sem):
        x, y, c = lax.axis_index("x"), lax.axis_index("y"), lax.axis_index("c")
        me, sibling = (x, y, c), (x, y, 1 - c)
        chips = [(1 - x, y), (x, 1 - y), (1 - x, 1 - y)]

        def slot(px, py, pc):
            return out_ref.at[4 * px + 2 * py + pc]

        def copy(k, block, to, src=None):
            return pltpu.make_async_remote_copy(
                src_ref=slot(*block) if src is None else src, dst_ref=slot(*block),
                send_sem=send_sems.at[k], recv_sem=recv_sems.at[k], device_id=to, device_id_type=MESH)

        mine = pltpu.make_async_copy(v_ref, slot(*me), local_sem)
        mine.start()
        first = [copy(0, me, sibling, src=v_ref)]
        first += [copy(1 + j, me, (*chip, c), src=v_ref) for j, chip in enumerate(chips)]
        for cp in first:
            cp.start()
        passed = [copy(4 + j, (*chip, c), sibling) for j, chip in enumerate(chips)]
        for j, chip in enumerate(chips):
            copy(1 + j, (*chip, c), me).wait_recv()
            passed[j].start()
        copy(0, sibling, me).wait_recv()
        for j, chip in enumerate(chips):
            copy(4 + j, (*chip, 1 - c), me).wait_recv()
        for cp in first + passed:
            cp.wait_send()
        mine.wait()

    return pl.pallas_call(
        body, name=name, in_specs=[ANY], out_specs=ANY,
        out_shape=SDS((N_DEV,) + v.shape, v.dtype),
        scratch_shapes=[pltpu.SemaphoreType.DMA((7,)), pltpu.SemaphoreType.DMA((7,)), pltpu.SemaphoreType.DMA],
    )(v)


def _all_to_all(v, name):
    def body(v_ref, out_ref, send_sems, recv_sems, local_sem):
        x, y, c = lax.axis_index("x"), lax.axis_index("y"), lax.axis_index("c")
        mine_idx = 4 * x + 2 * y + c
        mine = pltpu.make_async_copy(v_ref.at[mine_idx], out_ref.at[mine_idx], local_sem)
        mine.start()
        sends, recvs = [], []
        for k in range(1, N_DEV):
            px = 1 - x if k & 4 else x
            py = 1 - y if k & 2 else y
            pc = 1 - c if k & 1 else c
            peer_idx = 4 * px + 2 * py + pc
            sems = dict(send_sem=send_sems.at[k - 1], recv_sem=recv_sems.at[k - 1], device_id=(px, py, pc),
                        device_id_type=MESH)
            sends.append(pltpu.make_async_remote_copy(src_ref=v_ref.at[peer_idx], dst_ref=out_ref.at[mine_idx], **sems))
            recvs.append(pltpu.make_async_remote_copy(src_ref=v_ref.at[peer_idx], dst_ref=out_ref.at[peer_idx], **sems))
        for cp in sends:
            cp.start()
        for cp in recvs:
            cp.wait_recv()
        for cp in sends:
            cp.wait_send()
        mine.wait()

    return pl.pallas_call(
        body, name=name, in_specs=[ANY], out_specs=ANY,
        out_shape=SDS(v.shape, v.dtype),
        scratch_shapes=[pltpu.SemaphoreType.DMA((7,)), pltpu.SemaphoreType.DMA((7,)), pltpu.SemaphoreType.DMA],
    )(v)


HBM_SPEC = pl.BlockSpec(memory_space=pltpu.HBM)
SEM_SPEC = pl.BlockSpec(memory_space=pltpu.SEMAPHORE)
EFFECT = pltpu.SideEffectType.DATAFLOW_SIDE_EFFECTING


EXCHANGE_PEERS = {"gather": range(1, N_DEV), "scatter": range(1, N_DEV), "chip": (1, 2, 4, 6), "pass": (2, 4, 6)}


def _exchange_copies(srcs, lands, send_sems, recv_sems, mode, layer):
    x, y, c = lax.axis_index("x"), lax.axis_index("y"), lax.axis_index("c")
    me = 4 * x + 2 * y + c
    copies = []
    for a, (src, land) in enumerate(zip(srcs, lands)):
        for k in EXCHANGE_PEERS[mode]:
            px = 1 - x if k & 4 else x
            py = 1 - y if k & 2 else y
            pc = 1 - c if k & 1 else c
            peer = 4 * px + 2 * py + pc
            if mode == "scatter":
                s, d, to = src.at[peer], land.at[me, layer], (px, py, pc)
            elif mode == "pass":
                s, d, to = land.at[peer], land.at[peer], (x, y, 1 - c)
            else:
                s, d, to = src, land.at[me], (px, py, pc)
            n = 7 * a + k - 1
            copies.append(pltpu.make_async_remote_copy(
                src_ref=s, dst_ref=d, send_sem=send_sems.at[n], recv_sem=recv_sems.at[n], device_id=to,
                device_id_type=MESH))
    return copies


def _exchange_start(name, srcs, lands, mode, layer=0, after=None):
    n = len(srcs)

    def body(*refs):
        send_sems, recv_sems = refs[-2 * n - 3], refs[-2 * n - 2]
        for cp in _exchange_copies(refs[:n], refs[n:2 * n], send_sems, recv_sems, mode, layer):
            cp.start()
        refs[-1][...] = jnp.zeros_like(refs[-1])

    arrays = list(srcs) + list(lands)
    sems = pltpu.SemaphoreType.DMA((7 * n,))
    out = pl.pallas_call(
        body, name=name,
        out_shape=(sems, sems, *[pltpu.HBM(v.shape, v.dtype) for v in arrays], SDS((8, LANE), F32)),
        in_specs=[HBM_SPEC] * (2 * n) + ([ANY] if after is not None else []),
        out_specs=(SEM_SPEC, SEM_SPEC, *[HBM_SPEC] * (2 * n), pl.BlockSpec(memory_space=pltpu.VMEM)),
        input_output_aliases={i: 2 + i for i in range(2 * n)},
        compiler_params=pltpu.CompilerParams(has_side_effects=EFFECT),
    )(*[pltpu.with_memory_space_constraint(v, pltpu.HBM) for v in arrays], *([after] if after is not None else []))
    return dict(sems=out[:2], srcs=out[2:2 + n], lands=out[2 + n:2 + 2 * n], token=out[-1][0, 0], mode=mode,
                layer=layer)


def _exchange_wait(name, st, after, also=()):
    n = len(st["srcs"])

    def body(*refs):
        send_sems, recv_sems = refs[2 * n], refs[2 * n + 1]
        for cp in _exchange_copies(refs[:n], refs[n:2 * n], send_sems, recv_sems, st["mode"], st["layer"]):
            cp.wait_send()
            cp.wait_recv()

    arrays = list(st["srcs"]) + list(st["lands"])
    out = pl.pallas_call(
        body, name=name,
        out_shape=tuple(pltpu.HBM(v.shape, v.dtype) for v in arrays),
        in_specs=[HBM_SPEC] * (2 * n) + [SEM_SPEC, SEM_SPEC] + [ANY] * (1 + len(also)),
        out_specs=tuple([HBM_SPEC] * (2 * n)),
        input_output_aliases={i: i for i in range(2 * n)},
        compiler_params=pltpu.CompilerParams(has_side_effects=EFFECT),
    )(*arrays, *st["sems"], after, *also)
    st["srcs"] = out[:n]
    return out[n:]


_IN_PIECES = ([(1024, 3072)]
              + [r for t in range(4) for r in ((LANE * t, LANE * (t + 1)), (512 + LANE * t, 512 + LANE * (t + 1)))]
              + [(4096, 5632), (3072, 4096), (5632, 5648)])


def _permute_in(w):
    pad = jnp.zeros(w.shape[:-1] + (N_PAD - N_IN,), w.dtype)
    return jnp.concatenate([w[..., a:b] for a, b in _IN_PIECES] + [pad], axis=-1)


def _unpermute_in(g):
    ax = [g[..., OFF_LRU + 2 * LANE * t:OFF_LRU + 2 * LANE * t + LANE] for t in range(4)]
    ag = [g[..., OFF_LRU + 2 * LANE * t + LANE:OFF_LRU + 2 * LANE * (t + 1)] for t in range(4)]
    return jnp.concatenate(ax + ag + [g[..., 0:2048], g[..., OFF_Z:OFF_Z + SSD_W], g[..., OFF_XBC:OFF_XBC + SSD_CONV],
                                      g[..., OFF_Z + SSD_W:OFF_Z + SSD_W + SSD_HEADS]], axis=-1)


SHARD_COLS = N_IN // N_DEV


def _in_segments():
    segs, pos = [], 0
    for a, b in _IN_PIECES:
        for i in range(N_DEV):
            lo, hi = max(a, SHARD_COLS * i), min(b, SHARD_COLS * (i + 1))
            if lo < hi:
                segs.append((i, lo - SHARD_COLS * i, hi - lo, pos + lo - a))
        pos += b - a
    return segs


RELAYOUT_ROWS = 256


def _relayout_in(land, own):
    def body(land_ref, own_ref, out_ref):
        me = 4 * lax.axis_index("x") + 2 * lax.axis_index("y") + lax.axis_index("c")
        out_ref[:, N_IN:N_PAD] = jnp.zeros((RELAYOUT_ROWS, N_PAD - N_IN), BF16)
        for i, j, wd, p in _in_segments():
            out_ref[:, p:p + wd] = jnp.where(me == i, own_ref[:, j:j + wd], land_ref[i, :, j:j + wd])

    return pl.pallas_call(
        body, name="relayout_in", grid=(D_MODEL // RELAYOUT_ROWS,),
        in_specs=[pl.BlockSpec((N_DEV, RELAYOUT_ROWS, SHARD_COLS), lambda r: (0, r, 0)),
                  pl.BlockSpec((RELAYOUT_ROWS, SHARD_COLS), lambda r: (r, 0))],
        out_specs=pl.BlockSpec((RELAYOUT_ROWS, N_PAD), lambda r: (r, 0)),
        out_shape=SDS((D_MODEL, N_PAD), BF16),
        compiler_params=_cp(("parallel",)),
    )(land, own)


def _relayout_grad(g):
    def body(g_ref, out_ref):
        for i, j, wd, p in _in_segments():
            out_ref[i, :, j:j + wd] = g_ref[:, p:p + wd].astype(BF16)

    return pl.pallas_call(
        body, name="relayout_grad", grid=(D_MODEL // RELAYOUT_ROWS,),
        in_specs=[pl.BlockSpec((RELAYOUT_ROWS, N_PAD), lambda r: (r, 0))],
        out_specs=pl.BlockSpec((N_DEV, RELAYOUT_ROWS, SHARD_COLS), lambda r: (0, r, 0)),
        out_shape=SDS((N_DEV, D_MODEL, SHARD_COLS), BF16),
        compiler_params=_cp(("parallel",)),
    )(g)


def _block_diag(w):
    w4 = w.reshape(4, 2, 64, 64)
    z = jnp.zeros((4, 64, 64), w.dtype)
    top = jnp.concatenate([w4[:, 0], z], axis=-1)
    bot = jnp.concatenate([z, w4[:, 1]], axis=-1)
    return jnp.concatenate([top, bot], axis=1).astype(BF16)


def _diag_blocks(g):
    return jnp.stack([g[:, :64, :64], g[:, 64:, 64:]], axis=1).reshape(8, 64, 64)


def _pad_lanes(v):
    return jnp.pad(v, (0, LANE - v.shape[0]))[None, :]


def _lower_bounds(logits):
    p = jax.nn.softmax(logits, axis=0)
    return p, jnp.cumsum(p, axis=0) - p[0]


def _lower_bounds_bwd(p, dlb):
    dp = jnp.cumsum(dlb[::-1], axis=0)[::-1]
    dp = dp.at[0].add(-jnp.sum(dlb, axis=0))
    return p * (dp - jnp.sum(dp * p, axis=0, keepdims=True))


SMALL = ["norm_w", "b_ada", "lru_conv_b", "lru_wa", "lru_ba", "lru_wx", "lru_bx", "lru_lambda", "hg_lb_logits",
         "hg_norm_w", "ssd_conv_b", "ssd_dt_bias", "ssd_a_log", "ssd_d", "ssd_norm_w", "final_norm_w"]
WEIGHTS = ["norm_w", "w_ada", "b_ada", "w_in", "lru_conv_w", "lru_conv_b", "lru_wa", "lru_ba", "lru_wx", "lru_bx",
           "lru_lambda", "hg_lb_logits", "hg_norm_w", "ssd_conv_w", "ssd_conv_b", "ssd_dt_bias", "ssd_a_log", "ssd_d",
           "ssd_norm_w", "w_out", "final_norm_w"]
INPUTS = ["x", "c"] + WEIGHTS + ["loss_target"] + ["m_" + n for n in WEIGHTS] + ["v_" + n for n in WEIGHTS]
SMALL_ROW = 1024


def _small_rows(like):
    out, off = {}, 0
    for n in SMALL:
        rows = -(-int(np.prod(like[n].shape)) // (8 * SMALL_ROW)) * 8
        out[n] = (off, rows)
        off += rows
    return out, off


def _flatten_small(d, prefix="", last=0.0):
    table, _ = _small_rows({n: d[prefix + n] for n in SMALL})
    pieces = []
    for n in SMALL:
        flat = d[prefix + n].reshape(-1)
        pieces.append(jnp.pad(flat, (0, table[n][1] * SMALL_ROW - flat.shape[0])).reshape(-1, SMALL_ROW))
    return jnp.concatenate(pieces + [jnp.full((8, SMALL_ROW), last, F32)], axis=0)


def _split_small(packed, like):
    table, _ = _small_rows(like)
    out = {}
    for n in SMALL:
        off, rows = table[n]
        size = int(np.prod(like[n].shape))
        out[n] = packed[off:off + rows].reshape(-1)[:size].reshape(like[n].shape)
    return out


def _local_step(x, mod, target, w, fetch, emit):
    S = x.shape[0]
    mall = _bfc(_hg_consts())
    mall_t = _bfc(_hg_consts().T)
    consts = _ssd_consts()
    p_lb, lbs = _lower_bounds(w["hg_lb_logits"])
    saved = []
    for l in range(DEPTH):
        w_in_l, w_out_l, token = fetch(l, x)
        shift, scale, gate = (mod[l:l + 1, k * D_MODEL:(k + 1) * D_MODEL] for k in range(3))
        shift = shift + token
        prm = dict(
            nw=w["norm_w"][l:l + 1], cw=w["lru_conv_w"][l], cb=w["lru_conv_b"][l:l + 1],
            wa=_block_diag(w["lru_wa"][l]), ba=w["lru_ba"][l].reshape(1, LRU_W),
            wx=_block_diag(w["lru_wx"][l]), bx=w["lru_bx"][l].reshape(1, LRU_W), lam=w["lru_lambda"][l:l + 1],
            lb=lbs[l:l + 1], hnw=w["hg_norm_w"][l:l + 1], scw=w["ssd_conv_w"][l], scb=w["ssd_conv_b"][l:l + 1],
            bias=_pad_lanes(w["ssd_dt_bias"][l]), alog=_pad_lanes(w["ssd_a_log"][l]),
            dskip=jnp.repeat(w["ssd_d"][l], SSD_P)[None, :], snw=w["ssd_norm_w"][l:l + 1],
            w_in=w_in_l, w_out=w_out_l, scale=scale, gate=gate)
        u, h = _inproj_fwd(x, prm["nw"], scale, shift, prm["w_in"])
        ycat = lax.empty((S, D_INNER), BF16)
        lru_args = (u, prm["cw"], prm["cb"], prm["wa"], prm["ba"], prm["wx"], prm["bx"], prm["lam"])
        ycat, h_lru = _lru_fwd(*lru_args, ycat)
        ycat, o_b, hg_st = _hg_fwd(u, prm["lb"], prm["hnw"], mall, ycat)
        xbc = _ssdconv_fwd(u, prm["scw"], prm["scb"])
        ssd_args = (u, xbc, prm["bias"], prm["alog"], prm["dskip"], prm["snw"], consts)
        ycat, y_ssd, ssd_st = _ssd_fwd(*ssd_args, ycat)
        x_new, y = _outproj_fwd(ycat, prm["w_out"], x, gate)
        saved.append((prm, x, u, h, ycat, lru_args, h_lru, o_b, hg_st, ssd_args, y_ssd, ssd_st, y))
        x = x_new
    dx, red = _loss_head(x, w["final_norm_w"][None, :], target)
    loss = red[1, 0]
    g = {n: [None] * DEPTH for n in WEIGHTS}
    g["final_norm_w"] = red[0]
    dmod, dlb = [None] * DEPTH, [None] * DEPTH
    for l in reversed(range(DEPTH)):
        prm, x, u, h, ycat, lru_args, h_lru, o_b, hg_st, ssd_args, y_ssd, ssd_st, y = saved[l]
        dycat, g_out, dgate = _outproj_bwd(dx, y, prm["gate"], ycat, prm["w_out"])
        token = emit(l, "w_out", g_out)
        du = lax.empty((S, N_PAD), BF16)
        ssd_args = ssd_args[:5] + (ssd_args[5] + token,) + ssd_args[6:]
        du, dxbc, sred = _ssd_bwd(*ssd_args, y_ssd, ssd_st, dycat, du)
        du, cred = _ssdconv_bwd(u, prm["scw"], prm["scb"], dxbc, du)
        du, hred = _hg_bwd(u, prm["lb"], prm["hnw"], mall, mall_t, o_b, hg_st, dycat, du)
        du, lred, gwa, gwx = _lru_bwd(*lru_args, h_lru, dycat, du)
        token = emit(l, "w_in", _inproj_bwd_w(h, du))
        dx, ired = _inproj_bwd_x(du, prm["w_in"], x, prm["nw"], prm["scale"] + token, dx)
        g["norm_w"][l] = ired[2]
        dmod[l] = jnp.concatenate([ired[0], ired[1], dgate[0]])
        g["lru_conv_w"][l], g["lru_conv_b"][l] = lred[0:4], lred[4]
        g["lru_ba"][l], g["lru_bx"][l], g["lru_lambda"][l] = lred[5].reshape(8, 64), lred[6].reshape(8, 64), lred[7]
        g["lru_wa"][l], g["lru_wx"][l] = _diag_blocks(gwa), _diag_blocks(gwx)
        g["hg_norm_w"][l], dlb[l] = hred[0], hred[1]
        g["ssd_conv_w"][l], g["ssd_conv_b"][l] = cred[0:4], cred[4]
        g["ssd_norm_w"][l] = sred[0]
        g["ssd_d"][l] = sred[1].reshape(SSD_HEADS, SSD_P).sum(-1)
        g["ssd_dt_bias"][l] = sred[2, :SSD_HEADS]
        g["ssd_a_log"][l] = -sred[3, :SSD_HEADS] * jnp.exp(w["ssd_a_log"][l])
    g["hg_lb_logits"] = _lower_bounds_bwd(p_lb, jnp.stack(dlb))
    for n in WEIGHTS:
        if isinstance(g[n], list) and g[n][0] is not None:
            g[n] = jnp.stack(g[n])
    return loss, dx, jnp.stack(dmod), g


def kernel(x, c, norm_w, w_ada, b_ada, w_in, lru_conv_w, lru_conv_b, lru_wa, lru_ba, lru_wx, lru_bx, lru_lambda, hg_lb_logits, hg_norm_w, ssd_conv_w, ssd_conv_b, ssd_dt_bias, ssd_a_log, ssd_d, ssd_norm_w, w_out, final_norm_w, loss_target, m_norm_w, m_w_ada, m_b_ada, m_w_in, m_lru_conv_w, m_lru_conv_b, m_lru_wa, m_lru_ba, m_lru_wx, m_lru_bx, m_lru_lambda, m_hg_lb_logits, m_hg_norm_w, m_ssd_conv_w, m_ssd_conv_b, m_ssd_dt_bias, m_ssd_a_log, m_ssd_d, m_ssd_norm_w, m_w_out, m_final_norm_w, v_norm_w, v_w_ada, v_b_ada, v_w_in, v_lru_conv_w, v_lru_conv_b, v_lru_wa, v_lru_ba, v_lru_wx, v_lru_bx, v_lru_lambda, v_hg_lb_logits, v_hg_norm_w, v_ssd_conv_w, v_ssd_conv_b, v_ssd_dt_bias, v_ssd_a_log, v_ssd_d, v_ssd_norm_w, v_w_out, v_final_norm_w):
    return _step(x, c, norm_w, w_ada, b_ada, w_in, lru_conv_w, lru_conv_b, lru_wa, lru_ba, lru_wx, lru_bx, lru_lambda, hg_lb_logits, hg_norm_w, ssd_conv_w, ssd_conv_b, ssd_dt_bias, ssd_a_log, ssd_d, ssd_norm_w, w_out, final_norm_w, loss_target, m_norm_w, m_w_ada, m_b_ada, m_w_in, m_lru_conv_w, m_lru_conv_b, m_lru_wa, m_lru_ba, m_lru_wx, m_lru_bx, m_lru_lambda, m_hg_lb_logits, m_hg_norm_w, m_ssd_conv_w, m_ssd_conv_b, m_ssd_dt_bias, m_ssd_a_log, m_ssd_d, m_ssd_norm_w, m_w_out, m_final_norm_w, v_norm_w, v_w_ada, v_b_ada, v_w_in, v_lru_conv_w, v_lru_conv_b, v_lru_wa, v_lru_ba, v_lru_wx, v_lru_bx, v_lru_lambda, v_hg_lb_logits, v_hg_norm_w, v_ssd_conv_w, v_ssd_conv_b, v_ssd_dt_bias, v_ssd_a_log, v_ssd_d, v_ssd_norm_w, v_w_out, v_final_norm_w)


def _step(*args):
    a = dict(zip(INPUTS, args, strict=True))
    me = 4 * lax.axis_index("x") + 2 * lax.axis_index("y") + lax.axis_index("c")
    x, target = a["x"][0], a["loss_target"][0]

    c_all = _all_gather(a["c"], "gather_c")[:, 0, :]
    b_cols = lax.dynamic_slice_in_dim(a["b_ada"], me * ADA_COLS, ADA_COLS, axis=1)[:, None, :]
    mod_parts = _all_gather(_ada_fwd(c_all, a["w_ada"], b_cols), "gather_mod")
    mod = lax.dynamic_index_in_dim(mod_parts, me, axis=2, keepdims=False)
    mod = mod.transpose(1, 0, 2).reshape(DEPTH, 3 * D_MODEL)

    w = {n: a[n] for n in SMALL}

    w_in_b, w_out_b = a["w_in"].astype(BF16), a["w_out"].astype(BF16)
    conv_own = jnp.concatenate([a["lru_conv_w"], a["ssd_conv_w"]], axis=-1)
    cols, rows_out = N_IN // N_DEV, D_INNER // N_DEV

    def gather_start(l, after):
        srcs = [w_in_b[l], w_out_b[l]] + ([conv_own] if l == 0 else [])
        lands = [lax.empty((N_DEV,) + s.shape, s.dtype) for s in srcs]
        return _exchange_start(f"gather_start_{l}", srcs, lands, "chip" if l == 0 else "gather", after=after)

    def gather_pass(name, st, after, also=()):
        landed = _exchange_wait(name + "_wait", st, after, also)
        st2 = _exchange_start(name + "_pass", st["srcs"], landed, "pass")
        return _exchange_wait(name + "_passed", st2, after)

    gathers = {0: gather_start(0, mod)}

    def fetch(l, x_l):
        if l == 0:
            landed = gather_pass("gather_0", gathers[0], x_l, also=(a["m_w_in"], a["v_w_in"]))
        else:
            landed = _exchange_wait(f"gather_wait_{l}", gathers[l], x_l)
        land_out = lax.dynamic_update_index_in_dim(landed[1], w_out_b[l], me, 0)
        if l == 0:
            conv = lax.dynamic_update_index_in_dim(landed[2], conv_own, me, 0).transpose(1, 2, 0, 3)
            w["lru_conv_w"] = conv[..., :64].reshape(DEPTH, 4, LRU_W)
            w["ssd_conv_w"] = conv[..., 64:].reshape(DEPTH, 4, SSD_CONV)
        token = 0.0
        if l + 1 < DEPTH:
            gathers[l + 1] = gather_start(l + 1, land_out)
            token = gathers[l + 1]["token"]
        return _relayout_in(landed[0], w_in_b[l]), land_out.reshape(D_INNER, D_MODEL), token

    scatters = {"w_in": {}, "w_out": {}}
    lands = {"w_in": lax.empty((N_DEV, DEPTH, D_MODEL, cols), BF16),
             "w_out": lax.empty((N_DEV, DEPTH, rows_out, D_MODEL), BF16)}
    own = {"w_in": [None] * DEPTH, "w_out": [None] * DEPTH}

    def emit(l, name, grad):
        grad = _relayout_grad(grad) if name == "w_in" else grad.reshape(N_DEV, rows_out, D_MODEL)
        st = _exchange_start(f"scatter_start_{name}_{l}", [grad], [lands[name]], "scatter", layer=l)
        scatters[name][l] = st
        lands[name] = st["lands"][0]
        return st["token"]

    loss_own, dx, dmod, g = _local_step(x, mod, target, w, fetch, emit)

    def sharded(name, parts, own=None, **kw):
        return _adamw(parts, a[name], a["m_" + name], a["v_" + name], "adamw_" + name + kw.pop("tag", ""), own=own, **kw)

    g["b_ada"] = dmod
    small_own = _flatten_small(g, last=loss_own)
    small_st = _exchange_start("gather_small", [small_own], [lax.empty((N_DEV,) + small_own.shape, F32)], "chip",
                               after=dx)
    big = {}
    after = small_st["token"] + dx[0:8, 0:LANE]
    def own_slice(st):
        return lax.dynamic_index_in_dim(st["srcs"][0], me, 0, keepdims=False)

    for name in ("w_out", "w_in"):
        for l in reversed(range(1, DEPTH)):
            scatters[name][l]["lands"] = [lands[name]]
            lands[name] = _exchange_wait(f"scatter_wait_{name}_{l}", scatters[name][l], after)[0]
            own[name][l] = own_slice(scatters[name][l])
        upper = sharded(name, lands[name], jnp.stack(own[name][1:]), layers=(1, DEPTH), tag="_upper")
        scatters[name][0]["lands"] = [lands[name]]
        lands[name] = _exchange_wait(f"scatter_wait_{name}_0", scatters[name][0], upper[1])[0]
        big[name] = sharded(name, lands[name], own_slice(scatters[name][0])[None], layers=(0, 1), prev=upper)
        after = big[name][1]
    small = gather_pass("gather_small", small_st, after)[0]
    outs = _adamw(small[:, None], *[_flatten_small(a, p)[None] for p in ("", "m_", "v_")], "adamw_small",
                  own=small_own[None])
    res = [_split_small(o[0], a) for o in outs]
    losses = lax.dynamic_update_index_in_dim(small[:, -1, 0], loss_own, me, 0)
    loss = jnp.sum(losses)

    off = _small_rows(a)[0]["b_ada"][0]
    dmod_all = lax.dynamic_update_index_in_dim(small[:, off:off + DEPTH * 3 * D_MODEL // SMALL_ROW],
                                               dmod.reshape(-1, SMALL_ROW), me, 0)
    dmod_all = dmod_all.reshape(N_DEV, DEPTH, 3 * D_MODEL).transpose(1, 0, 2)
    dmod_cols = lax.dynamic_slice_in_dim(dmod_all, me * ADA_COLS, ADA_COLS, axis=2)
    dmod_pad = jnp.pad(dmod_cols, ((0, 0), (0, LANE - N_DEV), (0, 0)))
    ct_pad = jnp.pad(c_all.T, ((0, 0), (0, LANE - N_DEV)))
    big["w_ada"] = sharded("w_ada", _ada_bwd(ct_pad, dmod_pad)[None])
    g_conv = jnp.concatenate([g["lru_conv_w"].reshape(DEPTH, 4, N_DEV, 64), g["ssd_conv_w"].reshape(DEPTH, 4, N_DEV, 192)],
                             axis=-1).transpose(2, 0, 1, 3)
    conv_parts = _all_to_all(g_conv, "scatter_conv")
    big["lru_conv_w"] = sharded("lru_conv_w", conv_parts[..., :64])
    big["ssd_conv_w"] = sharded("ssd_conv_w", conv_parts[..., 64:])

    out = [loss, dx[None]]
    for k in range(4):
        out += [big[n][k] if n in big else res[k][n] for n in WEIGHTS]
    return tuple(out)
```

```python
import functools

import numpy as np
import jax
import jax.numpy as jnp
from jax import lax
from jax.experimental import pallas as pl
from jax.experimental.pallas import tpu as pltpu

F32 = jnp.float32
BF16 = jnp.bfloat16
SDS = jax.ShapeDtypeStruct

N_DEV = 8
DEPTH = 4
D_MODEL = 1024
D_INNER = 2048
EPS = 1e-6
LRU_W = 512
LRU_C = 8.0
HG_W = 512
HG_CHUNK = 64
HG_HEADS = 4
SSD_W = 1024
SSD_HEADS = 16
SSD_P = 64
SSD_N = 128
SSD_CHUNK = 128
SSD_CONV = 1536
N_IN = 5648
N_PAD = 5760
OFF_HG = 0
OFF_LRU = 2048
OFF_XBC = 3072
OFF_Z = 4608
LANE = 128
VMEM_LIMIT = 56 * 1024 * 1024
NEG = -1e30

ADAM_LR = 0.001
ADAM_B1 = 0.9
ADAM_B2 = 0.999
ADAM_EPS = 1e-08
ADAM_WD = 0.01
ADAM_STEP = 10


def _cp(sem=None):
    return pltpu.CompilerParams(dimension_semantics=sem, vmem_limit_bytes=VMEM_LIMIT)


def _dg(a, b, ca, cb):
    return lax.dot_general(a, b, (((ca,), (cb,)), ((), ())), preferred_element_type=F32)


def _mm(a, b):
    return _dg(a, b, 1, 0)


def _mm_nt(a, b):
    return _dg(a, b, 1, 1)


def _mm_tn(a, b):
    return _dg(a, b, 0, 0)


def _bf(x):
    return x.astype(BF16)


def _f(x):
    return x.astype(F32)


def _split3(x):
    hi = x.astype(BF16)
    r = x - hi.astype(F32)
    mid = r.astype(BF16)
    lo = (r - mid.astype(F32)).astype(BF16)
    return hi, mid, lo


def _sel_r(x, m):
    hi, mid, lo = _split3(x)
    return _mm(hi, m) + _mm(mid, m) + _mm(lo, m)


def _sel_l(m, x):
    hi, mid, lo = _split3(x)
    return _mm(m, hi) + _mm(m, mid) + _mm(m, lo)


def _sel_l2(m, x):
    hi = x.astype(BF16)
    lo = (x - hi.astype(F32)).astype(BF16)
    return _mm(m, hi) + _mm(m, lo)


def _sel_tn(x, m):
    hi, mid, lo = _split3(x)
    return _mm_tn(hi, m) + _mm_tn(mid, m) + _mm_tn(lo, m)


def _sigmoid(x):
    return 1.0 / (1.0 + jnp.exp(-x))


def _silu(x):
    return x * _sigmoid(x)


def _dsilu(x):
    s = _sigmoid(x)
    return s * (1.0 + x * (1.0 - s))


def _softplus(x):
    return jnp.maximum(x, 0.0) + jnp.log(1.0 + jnp.exp(-jnp.abs(x)))


def _expm1(z):
    series = z * (1.0 + z * (1.0 / 2) * (1.0 + z * (1.0 / 3) * (1.0 + z * (1.0 / 4) * (
        1.0 + z * (1.0 / 5) * (1.0 + z * (1.0 / 6) * (1.0 + z * (1.0 / 7)))))))
    return jnp.where(jnp.abs(z) < 0.3, series, jnp.exp(z) - 1.0)


def _iota(shape, dim):
    return lax.broadcasted_iota(jnp.int32, shape, dim)


def _last_row(x, rows):
    return jnp.sum(jnp.where(rows == x.shape[0] - 1, x, 0.0), axis=0, keepdims=True)


def _shift_down(x, d, rows, fill=0.0):
    return jnp.where(rows >= d, pltpu.roll(x, d, 0), fill)


def _shift_up(x, d, rows, fill=0.0):
    n = x.shape[0]
    return jnp.where(rows < n - d, pltpu.roll(x, n - d, 0), fill)


def _conv_fwd(x, cw_ref, cb_ref, rows):
    out = cb_ref[...] + cw_ref[pl.ds(3, 1), :] * x
    for k in range(3):
        out = out + cw_ref[pl.ds(k, 1), :] * _shift_down(x, 3 - k, rows)
    return out


def _conv_bwd(x, dco, cw_ref, rows):
    dx = cw_ref[pl.ds(3, 1), :] * dco
    dws = []
    for k in range(3):
        dx = dx + cw_ref[pl.ds(k, 1), :] * _shift_up(dco, 3 - k, rows)
        dws.append(jnp.sum(dco * _shift_down(x, 3 - k, rows), axis=0, keepdims=True))
    dws.append(jnp.sum(dco * x, axis=0, keepdims=True))
    return dx, dws, jnp.sum(dco, axis=0, keepdims=True)


def _vec(n):
    return pl.BlockSpec((1, n), lambda *_: (0, 0))


def _full(shape):
    nd = len(shape)
    return pl.BlockSpec(shape, lambda *_: (0,) * nd)


def _inproj_fwd(x, nw, scale, shift, w):
    S = x.shape[0]
    tm = min(256, S)

    def body(x_ref, nw_ref, sc_ref, sh_ref, w_ref, u_ref, h_ref):
        xv = x_ref[...]
        inv = lax.rsqrt(jnp.mean(xv * xv, axis=-1, keepdims=True) + EPS)
        h = ((xv * inv) * nw_ref[...] * (1.0 + sc_ref[...]) + sh_ref[...]).astype(BF16)
        h_ref[...] = h
        u_ref[...] = _bf(_mm(h, w_ref[...]))

    return pl.pallas_call(
        body, name="inproj_fwd", grid=(S // tm,),
        in_specs=[pl.BlockSpec((tm, D_MODEL), lambda i: (i, 0)), _vec(D_MODEL), _vec(D_MODEL), _vec(D_MODEL),
                  _full((D_MODEL, N_PAD))],
        out_specs=[pl.BlockSpec((tm, N_PAD), lambda i: (i, 0)), pl.BlockSpec((tm, D_MODEL), lambda i: (i, 0))],
        out_shape=[SDS((S, N_PAD), BF16), SDS((S, D_MODEL), BF16)],
        compiler_params=_cp(("parallel",)),
    )(x, nw, scale, shift, w)


def _inproj_bwd_x(du, w, x, nw, scale, dxn):
    S = x.shape[0]
    tm = min(256, S)

    def body(du_ref, w_ref, x_ref, nw_ref, sc_ref, dxn_ref, dx_ref, red_ref):
        @pl.when(pl.program_id(0) == 0)
        def _():
            red_ref[...] = jnp.zeros_like(red_ref)

        dh = _mm_nt(du_ref[...], w_ref[...])
        xv = x_ref[...]
        inv = lax.rsqrt(jnp.mean(xv * xv, axis=-1, keepdims=True) + EPS)
        xhat = xv * inv
        nwv = nw_ref[...]
        g1 = 1.0 + sc_ref[...]
        dxhat = dh * nwv * g1
        dx = inv * (dxhat - xhat * jnp.mean(dxhat * xhat, axis=-1, keepdims=True))
        dx_ref[...] = dxn_ref[...] + dx
        red_ref[0:1, :] += jnp.sum(dh, axis=0, keepdims=True)
        red_ref[1:2, :] += jnp.sum(dh * xhat * nwv, axis=0, keepdims=True)
        red_ref[2:3, :] += jnp.sum(dh * xhat * g1, axis=0, keepdims=True)

    row = pl.BlockSpec((tm, D_MODEL), lambda i: (i, 0))
    return pl.pallas_call(
        body, name="inproj_bwd_x", grid=(S // tm,),
        in_specs=[pl.BlockSpec((tm, N_PAD), lambda i: (i, 0)), _full((D_MODEL, N_PAD)), row, _vec(D_MODEL),
                  _vec(D_MODEL), row],
        out_specs=[row, _full((8, D_MODEL))],
        out_shape=[SDS((S, D_MODEL), F32), SDS((8, D_MODEL), F32)],
        compiler_params=_cp(("arbitrary",)),
    )(du, w, x, nw, scale, dxn)


def _inproj_bwd_w(h, du):
    S = h.shape[0]
    tn = 640

    def body(h_ref, du_ref, gw_ref):
        gw_ref[...] = _mm_tn(h_ref[...], _bf(du_ref[...]))

    return pl.pallas_call(
        body, name="inproj_bwd_w", grid=(N_PAD // tn,),
        in_specs=[_full((S, D_MODEL)), pl.BlockSpec((S, tn), lambda j: (0, j))],
        out_specs=pl.BlockSpec((D_MODEL, tn), lambda j: (0, j)),
        out_shape=SDS((D_MODEL, N_PAD), F32),
        compiler_params=_cp(("parallel",)),
    )(h, du)


def _scan_block(a, b, rows):
    d = 1
    while d < a.shape[0]:
        a_s = _shift_down(a, d, rows, 1.0)
        b_s = _shift_down(b, d, rows, 0.0)
        b = a * b_s + b
        a = a * a_s
        d *= 2
    return a, b


def _rscan_block(c, g, rows):
    d = 1
    while d < c.shape[0]:
        c_s = _shift_up(c, d, rows, 1.0)
        g_s = _shift_up(g, d, rows, 0.0)
        g = g + c * g_s
        c = c * c_s
        d *= 2
    return c, g


LRU_BLOCK = 256


def _lru_gates(xa, wa_ref, ba_ref, wx_ref, bx_ref, lam_ref):
    sp = _softplus(-lam_ref[...])
    xb = _bf(xa)
    r = _sigmoid(_mm(xb, wa_ref[...]) + ba_ref[...])
    ig = _sigmoid(_mm(xb, wx_ref[...]) + bx_ref[...])
    la = -LRU_C * r * sp
    a = jnp.exp(la)
    mult = jnp.sqrt(-_expm1(2.0 * la))
    return sp, r, ig, la, a, mult


def _lru_specs(S):
    t128 = pl.BlockSpec((1, LANE), lambda t: (0, t))
    return [pl.BlockSpec((S, 2 * LANE), lambda t: (0, OFF_LRU // (2 * LANE) + t)),
            pl.BlockSpec((4, LANE), lambda t: (0, t)), t128,
            pl.BlockSpec((None, LANE, LANE), lambda t: (t, 0, 0)), t128,
            pl.BlockSpec((None, LANE, LANE), lambda t: (t, 0, 0)), t128, t128]


def _lru_fwd(u, cw, cb, wa, ba, wx, bx, lam, ycat):
    S = u.shape[0]
    tb = min(LRU_BLOCK, S)

    def body(u_ref, cw_ref, cb_ref, wa_ref, ba_ref, wx_ref, bx_ref, lam_ref, ycat_in, ycat_ref, h_ref, a_scr, b_scr):
        del ycat_in
        rows = _iota((S, LANE), 0)
        xa = _conv_fwd(_f(u_ref[:, 0:LANE]), cw_ref, cb_ref, rows)
        _, _, ig, _, a, mult = _lru_gates(xa, wa_ref, ba_ref, wx_ref, bx_ref, lam_ref)
        a_scr[...] = a
        b_scr[...] = mult * (ig * xa)
        rows_b = _iota((tb, LANE), 0)

        def blk(j, hprev):
            sl = pl.ds(pl.multiple_of(j * tb, tb), tb)
            acum, hloc = _scan_block(a_scr[sl, :], b_scr[sl, :], rows_b)
            hf = hloc + acum * hprev
            h_ref[sl, :] = hf
            return _last_row(hf, rows_b)

        lax.fori_loop(0, S // tb, blk, jnp.zeros((1, LANE), F32))
        ycat_ref[...] = _bf(h_ref[...] * _silu(_f(u_ref[:, LANE:2 * LANE])))

    col = pl.BlockSpec((S, LANE), lambda t: (0, t))
    return pl.pallas_call(
        body, name="lru_fwd", grid=(LRU_W // LANE,),
        in_specs=_lru_specs(S) + [pl.BlockSpec(memory_space=pl.ANY)],
        out_specs=[col, col],
        out_shape=[SDS((S, D_INNER), BF16), SDS((S,LRU_W), F32)],
        scratch_shapes=[pltpu.VMEM((S, LANE), F32), pltpu.VMEM((S, LANE), F32)],
        input_output_aliases={8: 0},
        compiler_params=_cp(("parallel",)),
    )(u, cw, cb, wa, ba, wx, bx, lam, ycat)


def _lru_bwd(u, cw, cb, wa, ba, wx, bx, lam, h_lru, dycat, du):
    S = u.shape[0]
    tb = min(LRU_BLOCK, S)

    def body(u_ref, cw_ref, cb_ref, wa_ref, ba_ref, wx_ref, bx_ref, lam_ref, h_ref, dy_ref, du_in,
             du_ref, red_ref, gwa_ref, gwx_ref, c_scr, g_scr, l_scr):
        del du_in
        rows = _iota((S, LANE), 0)
        ax = _f(u_ref[:, 0:LANE])
        ag = _f(u_ref[:, LANE:2 * LANE])
        xa = _conv_fwd(ax, cw_ref, cb_ref, rows)
        sp, r, ig, la, a, mult = _lru_gates(xa, wa_ref, ba_ref, wx_ref, bx_ref, lam_ref)
        h = h_ref[...]
        dy = _f(dy_ref[...])
        du_ref[:, LANE:2 * LANE] = _bf(dy * h * _dsilu(ag))
        c_scr[...] = _shift_up(a, 1, rows, 0.0)
        g_scr[...] = dy * _silu(ag)
        rows_b = _iota((tb, LANE), 0)
        nb = S // tb

        def blk(jj, lnext):
            j = nb - 1 - jj
            sl = pl.ds(pl.multiple_of(j * tb, tb), tb)
            ccum, lloc = _rscan_block(c_scr[sl, :], g_scr[sl, :], rows_b)
            lam_t = lloc + ccum * lnext
            l_scr[sl, :] = lam_t
            return jnp.sum(jnp.where(rows_b == 0, lam_t, 0.0), axis=0, keepdims=True)

        lax.fori_loop(0, nb, blk, jnp.zeros((1, LANE), F32))
        db = l_scr[...]
        da = db * _shift_down(h, 1, rows)
        dmult = db * ig * xa
        dig = db * mult * xa
        dxa = db * mult * ig
        dla = da * a - dmult * (a * a) / mult
        dr = -LRU_C * sp * dla
        dsp = jnp.sum(-LRU_C * r * dla, axis=0, keepdims=True)
        dlam = -dsp * _sigmoid(-lam_ref[...])
        dzr = dr * r * (1.0 - r)
        dzi = dig * ig * (1.0 - ig)
        dzr_b, dzi_b, xa_b = _bf(dzr), _bf(dzi), _bf(xa)
        dxa = dxa + _mm_nt(dzr_b, wa_ref[...]) + _mm_nt(dzi_b, wx_ref[...])
        gwa_ref[...] = _mm_tn(xa_b, dzr_b)
        gwx_ref[...] = _mm_tn(xa_b, dzi_b)
        dax, dws, dcb = _conv_bwd(ax, dxa, cw_ref, rows)
        du_ref[:, 0:LANE] = _bf(dax)
        parts = dws + [dcb, jnp.sum(dzr, axis=0, keepdims=True), jnp.sum(dzi, axis=0, keepdims=True), dlam]
        for n, p in enumerate(parts):
            red_ref[pl.ds(n, 1), :] = p

    col = pl.BlockSpec((S, LANE), lambda t: (0, t))
    gw = pl.BlockSpec((None, LANE, LANE), lambda t: (t, 0, 0))
    return pl.pallas_call(
        body, name="lru_bwd", grid=(LRU_W // LANE,),
        in_specs=_lru_specs(S) + [col, col, pl.BlockSpec(memory_space=pl.ANY)],
        out_specs=[pl.BlockSpec((S, 2 * LANE), lambda t: (0, OFF_LRU // (2 * LANE) + t)),
                   pl.BlockSpec((8, LANE), lambda t: (0, t)), gw, gw],
        out_shape=[SDS((S, N_PAD), BF16), SDS((8, LRU_W), F32), SDS((4, LANE, LANE), F32), SDS((4, LANE, LANE), F32)],
        scratch_shapes=[pltpu.VMEM((S, LANE), F32)] * 3,
        input_output_aliases={10: 0},
        compiler_params=_cp(("parallel",)),
    )(u, cw, cb, wa, ba, wx, bx, lam, h_lru, dycat, du)


HG_LEVELS = 6


def _hg_consts():
    C = HG_CHUNK
    t = np.arange(C)[:, None]
    r = np.arange(C)[None, :]
    mats = []
    for l in range(HG_LEVELS):
        b = 1 << l
        upper = (t % (2 * b)) >= b
        anchor = (t // (2 * b)) * 2 * b + b - 1
        mats.append((upper & (r > anchor) & (r <= t)) | ((~upper) & (r > t) & (r <= anchor)))
    mats.append(r <= t)
    mats.append(r > t)
    return np.concatenate(mats, 0).astype(np.float32)


def _hg_factors(hf, lb, mall):
    s = _sigmoid(hf)
    f = lb + (1.0 - lb) * s
    lf = jnp.log(f)
    k = (1.0 - lb) * _sigmoid(-hf)
    e = jnp.exp(_sel_l(mall, lf))
    C = HG_CHUNK
    eq = [e[l * C:(l + 1) * C] for l in range(HG_LEVELS)]
    ecum = e[HG_LEVELS * C:(HG_LEVELS + 1) * C]
    erem = e[(HG_LEVELS + 1) * C:(HG_LEVELS + 2) * C]
    return s, f, k, eq, eq, ecum, erem


def _hg_masks():
    C = HG_CHUNK
    ri, ci = _iota((C, C), 0), _iota((C, C), 1)
    rr = _iota((C, LANE), 0)
    gm = [(lax.shift_right_logical(ri, l + 1) == lax.shift_right_logical(ci, l + 1)).astype(F32)
          for l in range(HG_LEVELS)]
    up = [(lax.shift_right_logical(rr, l) & 1) == 1 for l in range(HG_LEVELS)]
    eye = (ri == ci).astype(F32)
    return gm, up, eye, rr


def _hg_scores(qh, kh, eq, ek, sl, gm, up, eye):
    qs, ks = [], []
    p = _mm_nt(_bf(qh), _bf(kh)) * eye
    for l in range(HG_LEVELS):
        ql = jnp.where(up[l], qh * eq[l][:, sl], 0.0)
        kl = jnp.where(up[l], 0.0, kh * ek[l][:, sl])
        p = p + _mm_nt(_bf(ql), _bf(kl)) * gm[l]
        qs.append(ql)
        ks.append(kl)
    return p, qs, ks


HG_SUB = 2


def _hg_fwd(u, lb, nw, mall, ycat):
    S = u.shape[0]
    C = HG_CHUNK
    n = S // C
    rows = HG_SUB * C

    def body(u_ref, lb_ref, nw_ref, mall_ref, ycat_in, ycat_ref, o_ref, st_ref, st):
        del ycat_in

        @pl.when(pl.program_id(0) == 0)
        def _():
            st[...] = jnp.zeros_like(st)

        gm, up, eye, rr = _hg_masks()
        for sub in range(HG_SUB):
            r = slice(sub * C, (sub + 1) * C)
            q = _silu(_f(u_ref[r, 0:512]))
            v = u_ref[r, 1024:1536]
            _, _, k, eq, ek, ecum, erem = _hg_factors(_f(u_ref[r, 512:1024]), lb_ref[...], mall_ref[...])
            for h in range(HG_HEADS):
                sl = slice(h * LANE, (h + 1) * LANE)
                qh, kh, vh = q[:, sl], k[:, sl], _bf(v[:, sl])
                p, _, _ = _hg_scores(qh, kh, eq, ek, sl, gm, up, eye)
                sth = st[h]
                st_ref[sub, h] = sth
                o_ref[r, sl] = _mm(_bf(p), vh) + _mm_nt(_bf(qh * ecum[:, sl]), _bf(sth))
                st[h] = sth * _last_row(ecum[:, sl], rr) + _mm_tn(vh, _bf(kh * erem[:, sl]))
            o = o_ref[r, :]
            inv = lax.rsqrt(jnp.mean(o * o, axis=-1, keepdims=True) + EPS)
            ycat_ref[r, :] = _bf((o * inv) * nw_ref[...] * _silu(_f(u_ref[r, 1536:2048])))

    return pl.pallas_call(
        body, name="hg_fwd", grid=(n // HG_SUB,),
        in_specs=[pl.BlockSpec((rows, 2048), lambda i: (i, 0)), _vec(HG_W), _vec(HG_W), _full(mall.shape),
                  pl.BlockSpec(memory_space=pl.ANY)],
        out_specs=[pl.BlockSpec((rows, HG_W), lambda i: (i, 1)), pl.BlockSpec((rows, HG_W), lambda i: (i, 0)),
                   pl.BlockSpec((HG_SUB, HG_HEADS, LANE, LANE), lambda i: (i, 0, 0, 0))],
        out_shape=[SDS((S, D_INNER), BF16), SDS((S,HG_W), F32), SDS((n, HG_HEADS, LANE, LANE), F32)],
        scratch_shapes=[pltpu.VMEM((HG_HEADS, LANE, LANE), F32)],
        input_output_aliases={4: 0},
        compiler_params=_cp(("arbitrary",)),
    )(u, lb, nw, mall, ycat)


def _hg_bwd(u, lb, nw, mall, mall_t, o_b, states, dycat, du):
    S = u.shape[0]
    C = HG_CHUNK
    n = S // C
    nb = n // HG_SUB
    rows = HG_SUB * C
    L2 = HG_LEVELS

    def body(u_ref, lb_ref, nw_ref, mall_ref, mallt_ref, o_ref, st_ref, dy_ref, du_in, du_ref, red_ref,
             dst, dlast_s, dq_s, dk_s, dex):
        del du_in

        @pl.when(pl.program_id(0) == 0)
        def _():
            dst[...] = jnp.zeros_like(dst)
            red_ref[...] = jnp.zeros_like(red_ref)

        lb = lb_ref[...]
        nwv = nw_ref[...]
        gm, up, eye, rr = _hg_masks()
        for sub in reversed(range(HG_SUB)):
            r = slice(sub * C, (sub + 1) * C)
            hq, hf, hg = _f(u_ref[r, 0:512]), _f(u_ref[r, 512:1024]), _f(u_ref[r, 1536:2048])
            q = _silu(hq)
            v = u_ref[r, 1024:1536]
            s, f, k, eq, ek, ecum, erem = _hg_factors(hf, lb, mall_ref[...])
            o = o_ref[r, :]
            dy = _f(dy_ref[r, :])
            inv = lax.rsqrt(jnp.mean(o * o, axis=-1, keepdims=True) + EPS)
            ohat = o * inv
            du_ref[r, 1536:2048] = _bf(dy * ohat * nwv * _dsilu(hg))
            dn = dy * _silu(hg)
            red_ref[0:1, :] += jnp.sum(dn * ohat, axis=0, keepdims=True)
            dohat = dn * nwv
            do = inv * (dohat - ohat * jnp.mean(dohat * ohat, axis=-1, keepdims=True))
            for h in range(HG_HEADS):
                sl = slice(h * LANE, (h + 1) * LANE)
                qh, kh, vh, doh = q[:, sl], k[:, sl], _bf(v[:, sl]), _bf(do[:, sl])
                p, qs, ks = _hg_scores(qh, kh, eq, ek, sl, gm, up, eye)
                st_f = st_ref[sub, h]
                sth = _bf(st_f)
                dsth = dst[h]
                dsth_b = _bf(dsth)
                qt = qh * ecum[:, sl]
                kt = kh * erem[:, sl]
                elast = _last_row(ecum[:, sl], rr)
                dp = _mm_nt(doh, vh)
                du_ref[r, 1024 + h * LANE:1024 + (h + 1) * LANE] = _bf(_mm_tn(_bf(p), doh) + _mm_nt(_bf(kt), dsth_b))
                dpe = _bf(dp * eye)
                dqt = _mm(doh, sth)
                dkt = _mm(vh, dsth_b)
                dq = dqt * ecum[:, sl] + _mm(dpe, _bf(kh))
                dk = dkt * erem[:, sl] + _mm_tn(dpe, _bf(qh))
                dex[sub, L2 * C:(L2 + 1) * C, sl] = dqt * qt
                dex[sub, (L2 + 1) * C:(L2 + 2) * C, sl] = dkt * kt
                for l in range(HG_LEVELS):
                    dpl = _bf(dp * gm[l])
                    dql = _mm(dpl, _bf(ks[l]))
                    dkl = _mm_tn(dpl, _bf(qs[l]))
                    dq = dq + jnp.where(up[l], dql * eq[l][:, sl], 0.0)
                    dk = dk + jnp.where(up[l], 0.0, dkl * ek[l][:, sl])
                    dex[sub, l * C:(l + 1) * C, sl] = dql * qs[l] + dkl * ks[l]
                dlast_s[sub, :, sl] = jnp.sum(dsth * st_f, axis=0, keepdims=True) * elast
                dst[h] = dsth * elast + _mm_tn(doh, _bf(qt))
                dq_s[sub, :, sl] = dq
                dk_s[sub, :, sl] = dk
            dq = dq_s[sub]
            dk = dk_s[sub]
            dlf = _sel_l2(mallt_ref[...], dex[sub]) + dlast_s[sub]
            du_ref[r, 0:512] = _bf(dq * _dsilu(hq))
            t = (1.0 - s) * (dlf / f - dk)
            du_ref[r, 512:1024] = _bf((1.0 - lb) * s * t)
            red_ref[1:2, :] += jnp.sum(t, axis=0, keepdims=True)

    rev = lambda i: (nb - 1 - i, 0)
    return pl.pallas_call(
        body, name="hg_bwd", grid=(nb,),
        in_specs=[pl.BlockSpec((rows, 2048), rev), _vec(HG_W), _vec(HG_W), _full(mall.shape), _full(mall_t.shape),
                  pl.BlockSpec((rows, HG_W), rev),
                  pl.BlockSpec((HG_SUB, HG_HEADS, LANE, LANE), lambda i: (nb - 1 - i, 0, 0, 0)),
                  pl.BlockSpec((rows, HG_W), lambda i: (nb - 1 - i, 1)), pl.BlockSpec(memory_space=pl.ANY)],
        out_specs=[pl.BlockSpec((rows, 2048), rev), pl.BlockSpec((8, HG_W), lambda i: (0, 0))],
        out_shape=[SDS((S, N_PAD), BF16), SDS((8, HG_W), F32)],
        scratch_shapes=[pltpu.VMEM((HG_HEADS, LANE, LANE), F32), pltpu.VMEM((HG_SUB, 1, HG_W), F32),
                        pltpu.VMEM((HG_SUB, C, HG_W), F32), pltpu.VMEM((HG_SUB, C, HG_W), F32),
                        pltpu.VMEM((HG_SUB, (L2 + 2) * C, HG_W), F32)],
        input_output_aliases={8: 0},
        compiler_params=_cp(("arbitrary",)),
    )(u, lb, nw, mall, mall_t, o_b, states, dycat, du)


def _ssdconv_fwd(u, cw, cb):
    S = u.shape[0]

    def body(u_ref, cw_ref, cb_ref, out_ref):
        rows = _iota((S, LANE), 0)
        out_ref[...] = _silu(_conv_fwd(_f(u_ref[...]), cw_ref, cb_ref, rows))

    return pl.pallas_call(
        body, name="ssdconv_fwd", grid=(SSD_CONV // LANE,),
        in_specs=[pl.BlockSpec((S, LANE), lambda t: (0, OFF_XBC // LANE + t)), pl.BlockSpec((4, LANE), lambda t: (0, t)),
                  pl.BlockSpec((1, LANE), lambda t: (0, t))],
        out_specs=pl.BlockSpec((S, LANE), lambda t: (0, t)),
        out_shape=SDS((S, SSD_CONV), F32),
        compiler_params=_cp(("parallel",)),
    )(u, cw, cb)


def _ssdconv_bwd(u, cw, cb, dxbc, du):
    S = u.shape[0]

    def body(u_ref, cw_ref, cb_ref, d_ref, du_in, du_ref, red_ref):
        del du_in
        rows = _iota((S, LANE), 0)
        x = _f(u_ref[...])
        dco = d_ref[...] * _dsilu(_conv_fwd(x, cw_ref, cb_ref, rows))
        dx, dws, dcb = _conv_bwd(x, dco, cw_ref, rows)
        du_ref[...] = _bf(dx)
        for n, p in enumerate(dws + [dcb]):
            red_ref[pl.ds(n, 1), :] = p
        red_ref[pl.ds(5, 3), :] = jnp.zeros((3, LANE), F32)

    ucol = pl.BlockSpec((S, LANE), lambda t: (0, OFF_XBC // LANE + t))
    return pl.pallas_call(
        body, name="ssdconv_bwd", grid=(SSD_CONV // LANE,),
        in_specs=[ucol, pl.BlockSpec((4, LANE), lambda t: (0, t)), pl.BlockSpec((1, LANE), lambda t: (0, t)),
                  pl.BlockSpec((S, LANE), lambda t: (0, t)), pl.BlockSpec(memory_space=pl.ANY)],
        out_specs=[ucol, pl.BlockSpec((8, LANE), lambda t: (0, t))],
        out_shape=[SDS((S, N_PAD), BF16), SDS((8, SSD_CONV), F32)],
        input_output_aliases={4: 0},
        compiler_params=_cp(("parallel",)),
    )(u, cw, cb, dxbc, du)


def _ssd_consts():
    e64 = np.zeros((LANE, SSD_W), np.float32)
    for h in range(SSD_HEADS):
        e64[h, h * SSD_P:(h + 1) * SSD_P] = 1.0
    T = SSD_CHUNK
    tril = (np.arange(T)[None, :] <= np.arange(T)[:, None]).astype(np.float32)
    return e64, tril, tril.T.copy()


def _ssd_common(zdt, bias_ref, alog_ref, tril, e64, cum_ref, cumt_ref):
    T = SSD_CHUNK
    lane = _iota((1, LANE), 1)
    a_neg = jnp.where(lane < SSD_HEADS, -jnp.exp(alog_ref[...]), 0.0)
    dtpre = zdt[:, SSD_W:SSD_W + LANE] + bias_ref[...]
    dt = _softplus(dtpre)
    cum = _sel_l(tril, dt * a_neg)
    cum_ref[...] = cum
    cumt_ref[...] = cum.T
    cum_x = _sel_r(cum, e64)
    last_x = _last_row(cum_x, _iota((T, SSD_W), 0))
    ecum_x = jnp.exp(cum_x)
    erem_x = jnp.exp(last_x - cum_x)
    elast_x = jnp.exp(last_x)
    dt_x = _sel_r(dt, e64)
    return a_neg, dtpre, dt, ecum_x, erem_x, elast_x, dt_x


def _ssd_decay(cum_ref, cumt_ref, h, causal):
    T = SSD_CHUNK
    diff = jnp.broadcast_to(cum_ref[:, pl.ds(h, 1)], (T, T)) - cumt_ref[pl.ds(h, 1), :]
    return jnp.exp(jnp.where(causal, diff, NEG))


def _group_norm_fwd(y1, nwv):
    outs, invs = [], []
    for g in range(2):
        seg = y1[:, g * 512:(g + 1) * 512]
        inv = lax.rsqrt(jnp.mean(seg * seg, axis=-1, keepdims=True) + EPS)
        outs.append(seg * inv * nwv[:, g * 512:(g + 1) * 512])
        invs.append(inv)
    return outs, invs


def _ssd_fwd(u, xbc, bias, alog, dskip_x, nw, consts, ycat):
    S = u.shape[0]
    T = SSD_CHUNK
    n = S // T
    e64, tril, _ = consts

    def body(u_ref, xbc_ref, bias_ref, alog_ref, dx_ref, nw_ref, e64_ref, tril_ref, ycat_in,
             ycat_ref, y_ref, st_ref, st, cumt, cum_e):
        del ycat_in

        @pl.when(pl.program_id(0) == 0)
        def _():
            st[...] = jnp.zeros_like(st)

        zdt = _f(u_ref[...])
        z = zdt[:, 0:SSD_W]
        xs = xbc_ref[:, 0:SSD_W]
        _, _, _, ecum_x, erem_x, elast_x, dt_x = _ssd_common(
            zdt, bias_ref, alog_ref, tril_ref[...], e64_ref[...], cum_e, cumt)
        causal = _iota((T, T), 0) >= _iota((T, T), 1)
        lo = _iota((T, LANE), 1) < SSD_P
        xdt = xs * dt_x
        xrem = xdt * erem_x
        st_ref[...] = st[...]
        for g in range(2):
            gs = slice(g * 512, (g + 1) * 512)
            bg = _bf(xbc_ref[:, SSD_W + g * LANE:SSD_W + (g + 1) * LANE])
            cg = _bf(xbc_ref[:, SSD_W + 256 + g * LANE:SSD_W + 256 + (g + 1) * LANE])
            cb = _mm_nt(cg, bg)
            yin = _mm(cg, _bf(st[:, gs])) * ecum_x[:, gs]
            for j in range(4):
                h0 = 8 * g + 2 * j
                cs = slice(h0 * SSD_P, (h0 + 2) * SSD_P)
                xp = xdt[:, cs]
                s0 = _bf(cb * _ssd_decay(cum_e, cumt, h0, causal))
                s1 = _bf(cb * _ssd_decay(cum_e, cumt, h0 + 1, causal))
                y_ref[:, cs] = (_mm(s0, _bf(jnp.where(lo, xp, 0.0))) + _mm(s1, _bf(jnp.where(lo, 0.0, xp)))
                                + yin[:, j * LANE:(j + 1) * LANE])
            st[:, gs] = st[:, gs] * elast_x[:, gs] + _mm_tn(bg, _bf(xrem[:, gs]))
        y1 = (y_ref[...] + dx_ref[...] * xs) * _silu(z)
        outs, _ = _group_norm_fwd(y1, nw_ref[...])
        for g in range(2):
            ycat_ref[:, g * 512:(g + 1) * 512] = _bf(outs[g])

    return pl.pallas_call(
        body, name="ssd_fwd", grid=(n,),
        in_specs=[pl.BlockSpec((T, SSD_W + LANE), lambda i: (i, OFF_Z // (SSD_W + LANE))),
                  pl.BlockSpec((T, SSD_CONV), lambda i: (i, 0)), _vec(LANE), _vec(LANE), _vec(SSD_W), _vec(SSD_W),
                  _full(e64.shape), _full(tril.shape), pl.BlockSpec(memory_space=pl.ANY)],
        out_specs=[pl.BlockSpec((T, SSD_W), lambda i: (i, 1)), pl.BlockSpec((T, SSD_W), lambda i: (i, 0)),
                   pl.BlockSpec((None, SSD_N, SSD_W), lambda i: (i, 0, 0))],
        out_shape=[SDS((S, D_INNER), BF16), SDS((S,SSD_W), F32), SDS((n, SSD_N, SSD_W), F32)],
        scratch_shapes=[pltpu.VMEM((SSD_N, SSD_W), F32), pltpu.VMEM((LANE, T), F32), pltpu.VMEM((T, LANE), F32)],
        input_output_aliases={8: 0},
        compiler_params=_cp(("arbitrary",)),
    )(u, xbc, bias, alog, dskip_x, nw, _bfc(e64), _bfc(tril), ycat)


def _ssd_bwd(u, xbc, bias, alog, dskip_x, nw, consts, y_ssd, states, dycat, du):
    S = u.shape[0]
    T = SSD_CHUNK
    n = S // T
    e64, tril, triu = consts
    e64t = np.ascontiguousarray(e64.T)

    def body(u_ref, xbc_ref, bias_ref, alog_ref, dx_ref, nw_ref, e64_ref, e64t_ref, tril_ref, triu_ref,
             y_ref, st_ref, dy_ref, du_in, du_ref, dxbc_ref, red_ref, dst, dl_s, cumt, dxdt_s, dy0_s, gb_s, gc_s,
             cum_e, cs_s):
        del du_in

        @pl.when(pl.program_id(0) == 0)
        def _():
            dst[...] = jnp.zeros_like(dst)
            red_ref[...] = jnp.zeros_like(red_ref)
            cs_s[...] = jnp.zeros_like(cs_s)

        zdt = _f(u_ref[...])
        z = zdt[:, 0:SSD_W]
        xs = xbc_ref[:, 0:SSD_W]
        a_neg, dtpre, dt, ecum_x, erem_x, elast_x, dt_x = _ssd_common(
            zdt, bias_ref, alog_ref, tril_ref[...], e64_ref[...], cum_e, cumt)
        causal = _iota((T, T), 0) >= _iota((T, T), 1)
        lo = _iota((T, LANE), 1) < SSD_P
        xdt = xs * dt_x
        xrem = xdt * erem_x
        y = y_ref[...]
        dxv = dx_ref[...]
        nwv = nw_ref[...]
        sz = _silu(z)
        y0 = y + dxv * xs
        y1 = y0 * sz
        for g in range(2):
            gs = slice(g * 512, (g + 1) * 512)
            seg = y1[:, gs]
            inv = lax.rsqrt(jnp.mean(seg * seg, axis=-1, keepdims=True) + EPS)
            shat = seg * inv
            dyg = _f(dy_ref[:, gs])
            red_ref[0:1, gs] += jnp.sum(dyg * shat, axis=0, keepdims=True)
            dsh = dyg * nwv[:, gs]
            dy1g = inv * (dsh - shat * jnp.mean(dsh * shat, axis=-1, keepdims=True))
            du_ref[:, gs] = _bf(dy1g * y0[:, gs] * _dsilu(z[:, gs]))
            dy0_s[:, gs] = dy1g * sz[:, gs]
        dy0 = dy0_s[...]
        red_ref[1:2, :] += jnp.sum(dy0 * xs, axis=0, keepdims=True)
        dyin = dy0 * ecum_x
        lane = _iota((T, LANE), 1)
        dcum = jnp.zeros((T, LANE), F32)

        def decay_grad(h, gm):
            cs_s[pl.ds(h, 1), :] = jnp.sum(gm, axis=0, keepdims=True)
            return jnp.where(lane == h, jnp.sum(gm, axis=1, keepdims=True), 0.0)

        for g in range(2):
            gs = slice(g * 512, (g + 1) * 512)
            bg = _bf(xbc_ref[:, SSD_W + g * LANE:SSD_W + (g + 1) * LANE])
            cg = _bf(xbc_ref[:, SSD_W + 256 + g * LANE:SSD_W + 256 + (g + 1) * LANE])
            cb = _mm_nt(cg, bg)
            dst_f, st_f = dst[:, gs], st_ref[:, gs]
            dstg = _bf(dst_f)
            stg = _bf(st_f)
            dyin_g = _bf(dyin[:, gs])
            xrem_g = _bf(xrem[:, gs])
            dcb = jnp.zeros((T, T), F32)
            dxr = _mm(bg, dstg)
            dxdt_s[:, gs] = dxr * erem_x[:, gs]
            gc_s[:, gs] = dxr * xrem[:, gs]
            gb_s[:, gs] = dyin[:, gs] * _mm(cg, stg)
            dl_s[:, gs] = jnp.sum(dst_f * st_f, axis=0, keepdims=True) * elast_x[:, gs]
            for j in range(4):
                h0 = 8 * g + 2 * j
                cs = slice(h0 * SSD_P, (h0 + 2) * SSD_P)
                xp = xdt[:, cs]
                dyp = dy0[:, cs]
                x_lo, x_hi = _bf(jnp.where(lo, xp, 0.0)), _bf(jnp.where(lo, 0.0, xp))
                d_lo, d_hi = _bf(jnp.where(lo, dyp, 0.0)), _bf(jnp.where(lo, 0.0, dyp))
                l0 = _ssd_decay(cum_e, cumt, h0, causal)
                l1 = _ssd_decay(cum_e, cumt, h0 + 1, causal)
                s0 = cb * l0
                s1 = cb * l1
                ds0 = _mm_nt(d_lo, x_lo)
                ds1 = _mm_nt(d_hi, x_hi)
                dcb = dcb + ds0 * l0 + ds1 * l1
                dxdt_s[:, cs] += _mm_tn(_bf(s0), d_lo) + _mm_tn(_bf(s1), d_hi)
                dcum = dcum + decay_grad(h0, ds0 * s0) + decay_grad(h0 + 1, ds1 * s1)
            dcb_b = _bf(dcb)
            dxbc_ref[:, SSD_W + g * LANE:SSD_W + (g + 1) * LANE] = _mm_tn(dcb_b, cg) + _mm_nt(xrem_g, dstg)
            dxbc_ref[:, SSD_W + 256 + g * LANE:SSD_W + 256 + (g + 1) * LANE] = _mm(dcb_b, bg) + _mm_nt(dyin_g, stg)
            dst[:, gs] = dst_f * elast_x[:, gs] + _mm_tn(cg, dyin_g)
        dxdt = dxdt_s[...]
        dxbc_ref[:, 0:SSD_W] = dxdt * dt_x + dy0 * dxv
        e64t = e64t_ref[...]
        gc = gc_s[...]
        dlast_x = jnp.sum(gc, axis=0, keepdims=True) + dl_s[...]
        dlast = jnp.max(_sel_r(jnp.broadcast_to(dlast_x, (8, SSD_W)), e64t), axis=0, keepdims=True)
        dcum = (dcum - cs_s[...].T + _sel_r(gb_s[...] - gc, e64t)
                + jnp.where(_iota((T, LANE), 0) == T - 1, dlast, 0.0))
        dda = _sel_l(triu_ref[...], dcum)
        ddt = dda * a_neg + _sel_r(dxdt * xs, e64t)
        ddtpre = ddt * _sigmoid(dtpre)
        du_ref[:, SSD_W:SSD_W + LANE] = _bf(jnp.where(lane < SSD_HEADS, ddtpre, 0.0))
        red_ref[2:3, 0:LANE] += jnp.sum(ddtpre, axis=0, keepdims=True)
        red_ref[3:4, 0:LANE] += jnp.sum(dda * dt, axis=0, keepdims=True)

    rev = lambda i: (n - 1 - i, 0)
    return pl.pallas_call(
        body, name="ssd_bwd", grid=(n,),
        in_specs=[pl.BlockSpec((T, SSD_W + LANE), lambda i: (n - 1 - i, OFF_Z // (SSD_W + LANE))),
                  pl.BlockSpec((T, SSD_CONV), rev), _vec(LANE), _vec(LANE), _vec(SSD_W), _vec(SSD_W),
                  _full(e64.shape), _full(e64t.shape), _full(tril.shape), _full(triu.shape),
                  pl.BlockSpec((T, SSD_W), rev), pl.BlockSpec((None, SSD_N, SSD_W), lambda i: (n - 1 - i, 0, 0)),
                  pl.BlockSpec((T, SSD_W), lambda i: (n - 1 - i, 1)), pl.BlockSpec(memory_space=pl.ANY)],
        out_specs=[pl.BlockSpec((T, SSD_W + LANE), lambda i: (n - 1 - i, OFF_Z // (SSD_W + LANE))),
                   pl.BlockSpec((T, SSD_CONV), rev), pl.BlockSpec((8, SSD_W), lambda i: (0, 0))],
        out_shape=[SDS((S, N_PAD), BF16), SDS((S, SSD_CONV), F32), SDS((8, SSD_W), F32)],
        scratch_shapes=[pltpu.VMEM((SSD_N, SSD_W), F32), pltpu.VMEM((1, SSD_W), F32), pltpu.VMEM((LANE, T), F32)]
        + [pltpu.VMEM((T, SSD_W), F32)] * 4 + [pltpu.VMEM((T, LANE), F32), pltpu.VMEM((LANE, T), F32)],
        input_output_aliases={13: 0},
        compiler_params=_cp(("arbitrary",)),
    )(u, xbc, bias, alog, dskip_x, nw, _bfc(e64), _bfc(e64t), _bfc(tril), _bfc(triu), y_ssd, states, dycat, du)


def _bfc(a):
    return jnp.asarray(a, BF16)


def _outproj_fwd(ycat, wo, x, gate):
    S = x.shape[0]
    tm = min(512, S)

    def body(yc_ref, wo_ref, x_ref, g_ref, xn_ref, y_ref):
        y = _mm(_bf(yc_ref[...]), wo_ref[...])
        y_ref[...] = y
        xn_ref[...] = x_ref[...] + g_ref[...] * y

    row = pl.BlockSpec((tm, D_MODEL), lambda i: (i, 0))
    return pl.pallas_call(
        body, name="outproj_fwd", grid=(S // tm,),
        in_specs=[pl.BlockSpec((tm, D_INNER), lambda i: (i, 0)), _full((D_INNER, D_MODEL)), row, _vec(D_MODEL)],
        out_specs=[row, row],
        out_shape=[SDS((S, D_MODEL), F32), SDS((S, D_MODEL), F32)],
        compiler_params=_cp(("parallel",)),
    )(ycat, wo, x, gate)


def _outproj_bwd(dxn, y, gate, ycat, wo):
    S = dxn.shape[0]
    tm = min(512, S)

    def body(dx_ref, y_ref, g_ref, yc_ref, wo_ref, dyc_ref, gwo_ref, dg_ref, acc):
        @pl.when(pl.program_id(0) == 0)
        def _():
            acc[...] = jnp.zeros_like(acc)
            dg_ref[...] = jnp.zeros_like(dg_ref)

        dxv = dx_ref[...]
        dy = _bf(dxv * g_ref[...])
        dg_ref[0:1, :] += jnp.sum(dxv * y_ref[...], axis=0, keepdims=True)
        dyc_ref[...] = _bf(_mm_nt(dy, wo_ref[...]))
        acc[...] += _mm_tn(_bf(yc_ref[...]), dy)

        @pl.when(pl.program_id(0) == pl.num_programs(0) - 1)
        def _():
            gwo_ref[...] = acc[...].astype(BF16)

    row = pl.BlockSpec((tm, D_MODEL), lambda i: (i, 0))
    wide = pl.BlockSpec((tm, D_INNER), lambda i: (i, 0))
    return pl.pallas_call(
        body, name="outproj_bwd", grid=(S // tm,),
        in_specs=[row, row, _vec(D_MODEL), wide, _full((D_INNER, D_MODEL))],
        out_specs=[wide, _full((D_INNER, D_MODEL)), _full((8, D_MODEL))],
        out_shape=[SDS((S, D_INNER), BF16), SDS((D_INNER, D_MODEL), BF16), SDS((8, D_MODEL), F32)],
        scratch_shapes=[pltpu.VMEM((D_INNER, D_MODEL), F32)],
        compiler_params=_cp(("arbitrary",)),
    )(dxn, y, gate, ycat, wo)


def _loss_head(x, fw, target):
    S = x.shape[0]
    tm = min(512, S)

    def body(x_ref, fw_ref, t_ref, dx_ref, red_ref):
        @pl.when(pl.program_id(0) == 0)
        def _():
            red_ref[...] = jnp.zeros_like(red_ref)

        xv = x_ref[...]
        fwv = fw_ref[...]
        inv = lax.rsqrt(jnp.mean(xv * xv, axis=-1, keepdims=True) + EPS)
        xhat = xv * inv
        err = xhat * fwv - t_ref[...]
        col = jnp.sum(err * err, axis=0, keepdims=True)
        red_ref[1:2, :] += jnp.broadcast_to(jnp.sum(col, axis=1, keepdims=True) * (0.5 / D_MODEL), (1, D_MODEL))
        dy = err * (1.0 / D_MODEL)
        red_ref[0:1, :] += jnp.sum(dy * xhat, axis=0, keepdims=True)
        dxhat = dy * fwv
        dx_ref[...] = inv * (dxhat - xhat * jnp.mean(dxhat * xhat, axis=-1, keepdims=True))

    row = pl.BlockSpec((tm, D_MODEL), lambda i: (i, 0))
    return pl.pallas_call(
        body, name="loss_head", grid=(S // tm,),
        in_specs=[row, _vec(D_MODEL), row],
        out_specs=[row, _full((8, D_MODEL))],
        out_shape=[SDS((S, D_MODEL), F32), SDS((8, D_MODEL), F32)],
        compiler_params=_cp(("arbitrary",)),
    )(x, fw, target)


ADA_COLS = 3 * D_MODEL // N_DEV


def _ada_fwd(c_all, w_ada, b_cols):
    def body(c_ref, w_ref, b_ref, out_ref):
        out_ref[...] = _mm(_bf(_silu(c_ref[...])), _bf(w_ref[...])) + b_ref[...]

    return pl.pallas_call(
        body, name="ada_fwd", grid=(DEPTH,),
        in_specs=[_full((N_DEV, D_MODEL)), pl.BlockSpec((None, D_MODEL, ADA_COLS), lambda l: (l, 0, 0)),
                  pl.BlockSpec((None, 1, ADA_COLS), lambda l: (l, 0, 0))],
        out_specs=pl.BlockSpec((None, N_DEV, ADA_COLS), lambda l: (l, 0, 0)),
        out_shape=SDS((DEPTH, N_DEV, ADA_COLS), F32),
        compiler_params=_cp(("parallel",)),
    )(c_all, w_ada, b_cols)


def _ada_bwd(ct_pad, dmod_pad):
    def body(c_ref, d_ref, out_ref):
        out_ref[...] = _mm(_bf(_silu(c_ref[...])), _bf(d_ref[...]))

    return pl.pallas_call(
        body, name="ada_bwd", grid=(DEPTH,),
        in_specs=[_full((D_MODEL, LANE)), pl.BlockSpec((None, LANE, ADA_COLS), lambda l: (l, 0, 0))],
        out_specs=pl.BlockSpec((None, D_MODEL, ADA_COLS), lambda l: (l, 0, 0)),
        out_shape=SDS((DEPTH, D_MODEL, ADA_COLS), F32),
        compiler_params=_cp(("parallel",)),
    )(ct_pad, dmod_pad)


def _adamw(parts, w, m, v, name, own=None, layers=None, prev=None):
    n, L, R, C = parts.shape
    lo, hi = layers or (0, L)
    tr = R
    while tr * C * 4 > (1 << 20) and tr % 16 == 0:
        tr //= 2
    first = 1 if own is None else 2

    def body(*refs):
        p_ref = refs[0]
        w_ref, m_ref, v_ref = refs[first:first + 3]
        g_ref, d_ref, mo_ref, vo_ref = refs[-4:]

        def part(k):
            if own is None:
                return p_ref[k].astype(F32)
            me = 4 * lax.axis_index("x") + 2 * lax.axis_index("y") + lax.axis_index("c")
            return jnp.where(me == k, refs[1][...], p_ref[k]).astype(F32)

        g = part(0)
        for k in range(1, n):
            g = g + part(k)
        mn = ADAM_B1 * m_ref[...] + (1.0 - ADAM_B1) * g
        vn = ADAM_B2 * v_ref[...] + (1.0 - ADAM_B2) * (g * g)
        m_hat = mn / (1.0 - ADAM_B1 ** ADAM_STEP)
        v_hat = vn / (1.0 - ADAM_B2 ** ADAM_STEP)
        g_ref[...] = g
        d_ref[...] = -ADAM_LR * (m_hat / (jnp.sqrt(v_hat) + ADAM_EPS) + ADAM_WD * w_ref[...])
        mo_ref[...] = mn
        vo_ref[...] = vn

    blk = pl.BlockSpec((None, tr, C), lambda l, i: (lo + l, i, 0))
    own_blk = [] if own is None else [pl.BlockSpec((None, tr, C), lambda l, i: (l, i, 0))]
    n_blk = 3 if own is None else 4
    return pl.pallas_call(
        body, name=name, grid=(hi - lo, R // tr),
        in_specs=[pl.BlockSpec((n, None, tr, C), lambda l, i: (0, lo + l, i, 0))] + own_blk + [blk] * 3
        + ([] if prev is None else [ANY] * 4),
        out_specs=[blk] * 4,
        out_shape=[SDS((L, R, C), F32)] * 4,
        input_output_aliases={} if prev is None else {1 + n_blk + k: k for k in range(4)},
        compiler_params=_cp(("parallel", "parallel")),
    )(parts, *([] if own is None else [own]), w, m, v, *([] if prev is None else prev))


MESH = pl.DeviceIdType.MESH
ANY = pl.BlockSpec(memory_space=pl.ANY)


def _all_gather(v, name):
    def body(v_ref, out_ref, send_sems, recv_sems, local_sem):
        x, y, c = lax.axis_index("x"), lax.axis_index("y"), lax.axis_index("c")
        me, sibling = (x, y, c), (x, y, 1 - c)
        chips = [(1 - x, y), (x, 1 - y), (1 - x, 1 - y)]

        def slot(px, py, pc):
            return out_ref.at[4 * px + 2 * py + pc]

        def copy(k, block, to, src=None):
            return pltpu.make_async_remote_copy(
                src_ref=slot(*block) if src is None else src, dst_ref=slot(*block),
                send_sem=send_sems.at[k], recv_sem=recv_sems.at[k], device_id=to, device_id_type=MESH)

        mine = pltpu.make_async_copy(v_ref, slot(*me), local_sem)
        mine.start()
        first = [copy(0, me, sibling, src=v_ref)]
        first += [copy(1 + j, me, (*chip, c), src=v_ref) for j, chip in enumerate(chips)]
        for cp in first:
            cp.start()
        passed = [copy(4 + j, (*chip, c), sibling) for j, chip in enumerate(chips)]
        for j, chip in enumerate(chips):
            copy(1 + j, (*chip, c), me).wait_recv()
            passed[j].start()
        copy(0, sibling, me).wait_recv()
        for j, chip in enumerate(chips):
            copy(4 + j, (*chip, 1 - c), me).wait_recv()
        for cp in first + passed:
            cp.wait_send()
        mine.wait()

    return pl.pallas_call(
        body, name=name, in_specs=[ANY], out_specs=ANY,
        out_shape=SDS((N_DEV,) + v.shape, v.dtype),
        scratch_shapes=[pltpu.SemaphoreType.DMA((7,)), pltpu.SemaphoreType.DMA((7,)), pltpu.SemaphoreType.DMA],
    )(v)


def _all_to_all(v, name):
    def body(v_ref, out_ref, send_sems, recv_sems, local_sem):
        x, y, c = lax.axis_index("x"), lax.axis_index("y"), lax.axis_index("c")
        mine_idx = 4 * x + 2 * y + c
        mine = pltpu.make_async_copy(v_ref.at[mine_idx], out_ref.at[mine_idx], local_sem)
        mine.start()
        sends, recvs = [], []
        for k in range(1, N_DEV):
            px = 1 - x if k & 4 else x
            py = 1 - y if k & 2 else y
            pc = 1 - c if k & 1 else c
            peer_idx = 4 * px + 2 * py + pc
            sems = dict(send_sem=send_sems.at[k - 1], recv_sem=recv_sems.at[k - 1], device_id=(px, py, pc),
                        device_id_type=MESH)
            sends.append(pltpu.make_async_remote_copy(src_ref=v_ref.at[peer_idx], dst_ref=out_ref.at[mine_idx], **sems))
            recvs.append(pltpu.make_async_remote_copy(src_ref=v_ref.at[peer_idx], dst_ref=out_ref.at[peer_idx], **sems))
        for cp in sends:
            cp.start()
        for cp in recvs:
            cp.wait_recv()
        for cp in sends:
            cp.wait_send()
        mine.wait()

    return pl.pallas_call(
        body, name=name, in_specs=[ANY], out_specs=ANY,
        out_shape=SDS(v.shape, v.dtype),
        scratch_shapes=[pltpu.SemaphoreType.DMA((7,)), pltpu.SemaphoreType.DMA((7,)), pltpu.SemaphoreType.DMA],
    )(v)


HBM_SPEC = pl.BlockSpec(memory_space=pltpu.HBM)
SEM_SPEC = pl.BlockSpec(memory_space=pltpu.SEMAPHORE)
EFFECT = pltpu.SideEffectType.DATAFLOW_SIDE_EFFECTING


EXCHANGE_PEERS = {"gather": range(1, N_DEV), "scatter": range(1, N_DEV), "chip": (1, 2, 4, 6), "pass": (2, 4, 6)}


def _exchange_copies(srcs, lands, send_sems, recv_sems, mode, layer):
    x, y, c = lax.axis_index("x"), lax.axis_index("y"), lax.axis_index("c")
    me = 4 * x + 2 * y + c
    copies = []
    for a, (src, land) in enumerate(zip(srcs, lands)):
        for k in EXCHANGE_PEERS[mode]:
            px = 1 - x if k & 4 else x
            py = 1 - y if k & 2 else y
            pc = 1 - c if k & 1 else c
            peer = 4 * px + 2 * py + pc
            if mode == "scatter":
                s, d, to = src.at[peer], land.at[me, layer], (px, py, pc)
            elif mode == "pass":
                s, d, to = land.at[peer], land.at[peer], (x, y, 1 - c)
            else:
                s, d, to = src, land.at[me], (px, py, pc)
            n = 7 * a + k - 1
            copies.append(pltpu.make_async_remote_copy(
                src_ref=s, dst_ref=d, send_sem=send_sems.at[n], recv_sem=recv_sems.at[n], device_id=to,
                device_id_type=MESH))
    return copies


def _exchange_start(name, srcs, lands, mode, layer=0, after=None):
    n = len(srcs)

    def body(*refs):
        send_sems, recv_sems = refs[-2 * n - 3], refs[-2 * n - 2]
        for cp in _exchange_copies(refs[:n], refs[n:2 * n], send_sems, recv_sems, mode, layer):
            cp.start()
        refs[-1][...] = jnp.zeros_like(refs[-1])

    arrays = list(srcs) + list(lands)
    sems = pltpu.SemaphoreType.DMA((7 * n,))
    out = pl.pallas_call(
        body, name=name,
        out_shape=(sems, sems, *[pltpu.HBM(v.shape, v.dtype) for v in arrays], SDS((8, LANE), F32)),
        in_specs=[HBM_SPEC] * (2 * n) + ([ANY] if after is not None else []),
        out_specs=(SEM_SPEC, SEM_SPEC, *[HBM_SPEC] * (2 * n), pl.BlockSpec(memory_space=pltpu.VMEM)),
        input_output_aliases={i: 2 + i for i in range(2 * n)},
        compiler_params=pltpu.CompilerParams(has_side_effects=EFFECT),
    )(*[pltpu.with_memory_space_constraint(v, pltpu.HBM) for v in arrays], *([after] if after is not None else []))
    return dict(sems=out[:2], srcs=out[2:2 + n], lands=out[2 + n:2 + 2 * n], token=out[-1][0, 0], mode=mode,
                layer=layer)


def _exchange_wait(name, st, after, also=()):
    n = len(st["srcs"])

    def body(*refs):
        send_sems, recv_sems = refs[2 * n], refs[2 * n + 1]
        for cp in _exchange_copies(refs[:n], refs[n:2 * n], send_sems, recv_sems, st["mode"], st["layer"]):
            cp.wait_send()
            cp.wait_recv()

    arrays = list(st["srcs"]) + list(st["lands"])
    out = pl.pallas_call(
        body, name=name,
        out_shape=tuple(pltpu.HBM(v.shape, v.dtype) for v in arrays),
        in_specs=[HBM_SPEC] * (2 * n) + [SEM_SPEC, SEM_SPEC] + [ANY] * (1 + len(also)),
        out_specs=tuple([HBM_SPEC] * (2 * n)),
        input_output_aliases={i: i for i in range(2 * n)},
        compiler_params=pltpu.CompilerParams(has_side_effects=EFFECT),
    )(*arrays, *st["sems"], after, *also)
    st["srcs"] = out[:n]
    return out[n:]


_IN_PIECES = ([(1024, 3072)]
              + [r for t in range(4) for r in ((LANE * t, LANE * (t + 1)), (512 + LANE * t, 512 + LANE * (t + 1)))]
              + [(4096, 5632), (3072, 4096), (5632, 5648)])


def _permute_in(w):
    pad = jnp.zeros(w.shape[:-1] + (N_PAD - N_IN,), w.dtype)
    return jnp.concatenate([w[..., a:b] for a, b in _IN_PIECES] + [pad], axis=-1)


def _unpermute_in(g):
    ax = [g[..., OFF_LRU + 2 * LANE * t:OFF_LRU + 2 * LANE * t + LANE] for t in range(4)]
    ag = [g[..., OFF_LRU + 2 * LANE * t + LANE:OFF_LRU + 2 * LANE * (t + 1)] for t in range(4)]
    return jnp.concatenate(ax + ag + [g[..., 0:2048], g[..., OFF_Z:OFF_Z + SSD_W], g[..., OFF_XBC:OFF_XBC + SSD_CONV],
                                      g[..., OFF_Z + SSD_W:OFF_Z + SSD_W + SSD_HEADS]], axis=-1)


SHARD_COLS = N_IN // N_DEV


def _in_segments():
    segs, pos = [], 0
    for a, b in _IN_PIECES:
        for i in range(N_DEV):
            lo, hi = max(a, SHARD_COLS * i), min(b, SHARD_COLS * (i + 1))
            if lo < hi:
                segs.append((i, lo - SHARD_COLS * i, hi - lo, pos + lo - a))
        pos += b - a
    return segs


RELAYOUT_ROWS = 256


def _relayout_in(land, own):
    def body(land_ref, own_ref, out_ref):
        me = 4 * lax.axis_index("x") + 2 * lax.axis_index("y") + lax.axis_index("c")
        out_ref[:, N_IN:N_PAD] = jnp.zeros((RELAYOUT_ROWS, N_PAD - N_IN), BF16)
        for i, j, wd, p in _in_segments():
            out_ref[:, p:p + wd] = jnp.where(me == i, own_ref[:, j:j + wd], land_ref[i, :, j:j + wd])

    return pl.pallas_call(
        body, name="relayout_in", grid=(D_MODEL // RELAYOUT_ROWS,),
        in_specs=[pl.BlockSpec((N_DEV, RELAYOUT_ROWS, SHARD_COLS), lambda r: (0, r, 0)),
                  pl.BlockSpec((RELAYOUT_ROWS, SHARD_COLS), lambda r: (r, 0))],
        out_specs=pl.BlockSpec((RELAYOUT_ROWS, N_PAD), lambda r: (r, 0)),
        out_shape=SDS((D_MODEL, N_PAD), BF16),
        compiler_params=_cp(("parallel",)),
    )(land, own)


def _relayout_grad(g):
    def body(g_ref, out_ref):
        for i, j, wd, p in _in_segments():
            out_ref[i, :, j:j + wd] = g_ref[:, p:p + wd].astype(BF16)

    return pl.pallas_call(
        body, name="relayout_grad", grid=(D_MODEL // RELAYOUT_ROWS,),
        in_specs=[pl.BlockSpec((RELAYOUT_ROWS, N_PAD), lambda r: (r, 0))],
        out_specs=pl.BlockSpec((N_DEV, RELAYOUT_ROWS, SHARD_COLS), lambda r: (0, r, 0)),
        out_shape=SDS((N_DEV, D_MODEL, SHARD_COLS), BF16),
        compiler_params=_cp(("parallel",)),
    )(g)


def _block_diag(w):
    w4 = w.reshape(4, 2, 64, 64)
    z = jnp.zeros((4, 64, 64), w.dtype)
    top = jnp.concatenate([w4[:, 0], z], axis=-1)
    bot = jnp.concatenate([z, w4[:, 1]], axis=-1)
    return jnp.concatenate([top, bot], axis=1).astype(BF16)


def _diag_blocks(g):
    return jnp.stack([g[:, :64, :64], g[:, 64:, 64:]], axis=1).reshape(8, 64, 64)


def _pad_lanes(v):
    return jnp.pad(v, (0, LANE - v.shape[0]))[None, :]


def _lower_bounds(logits):
    p = jax.nn.softmax(logits, axis=0)
    return p, jnp.cumsum(p, axis=0) - p[0]


def _lower_bounds_bwd(p, dlb):
    dp = jnp.cumsum(dlb[::-1], axis=0)[::-1]
    dp = dp.at[0].add(-jnp.sum(dlb, axis=0))
    return p * (dp - jnp.sum(dp * p, axis=0, keepdims=True))


SMALL = ["norm_w", "b_ada", "lru_conv_b", "lru_wa", "lru_ba", "lru_wx", "lru_bx", "lru_lambda", "hg_lb_logits",
         "hg_norm_w", "ssd_conv_b", "ssd_dt_bias", "ssd_a_log", "ssd_d", "ssd_norm_w", "final_norm_w"]
WEIGHTS = ["norm_w", "w_ada", "b_ada", "w_in", "lru_conv_w", "lru_conv_b", "lru_wa", "lru_ba", "lru_wx", "lru_bx",
           "lru_lambda", "hg_lb_logits", "hg_norm_w", "ssd_conv_w", "ssd_conv_b", "ssd_dt_bias", "ssd_a_log", "ssd_d",
           "ssd_norm_w", "w_out", "final_norm_w"]
INPUTS = ["x", "c"] + WEIGHTS + ["loss_target"] + ["m_" + n for n in WEIGHTS] + ["v_" + n for n in WEIGHTS]
SMALL_ROW = 1024


def _small_rows(like):
    out, off = {}, 0
    for n in SMALL:
        rows = -(-int(np.prod(like[n].shape)) // (8 * SMALL_ROW)) * 8
        out[n] = (off, rows)
        off += rows
    return out, off


def _flatten_small(d, prefix="", last=0.0):
    table, _ = _small_rows({n: d[prefix + n] for n in SMALL})
    pieces = []
    for n in SMALL:
        flat = d[prefix + n].reshape(-1)
        pieces.append(jnp.pad(flat, (0, table[n][1] * SMALL_ROW - flat.shape[0])).reshape(-1, SMALL_ROW))
    return jnp.concatenate(pieces + [jnp.full((8, SMALL_ROW), last, F32)], axis=0)


def _split_small(packed, like):
    table, _ = _small_rows(like)
    out = {}
    for n in SMALL:
        off, rows = table[n]
        size = int(np.prod(like[n].shape))
        out[n] = packed[off:off + rows].reshape(-1)[:size].reshape(like[n].shape)
    return out


def _local_step(x, mod, target, w, fetch, emit):
    S = x.shape[0]
    mall = _bfc(_hg_consts())
    mall_t = _bfc(_hg_consts().T)
    consts = _ssd_consts()
    p_lb, lbs = _lower_bounds(w["hg_lb_logits"])
    saved = []
    for l in range(DEPTH):
        w_in_l, w_out_l, token = fetch(l, x)
        shift, scale, gate = (mod[l:l + 1, k * D_MODEL:(k + 1) * D_MODEL] for k in range(3))
        shift = shift + token
        prm = dict(
            nw=w["norm_w"][l:l + 1], cw=w["lru_conv_w"][l], cb=w["lru_conv_b"][l:l + 1],
            wa=_block_diag(w["lru_wa"][l]), ba=w["lru_ba"][l].reshape(1, LRU_W),
            wx=_block_diag(w["lru_wx"][l]), bx=w["lru_bx"][l].reshape(1, LRU_W), lam=w["lru_lambda"][l:l + 1],
            lb=lbs[l:l + 1], hnw=w["hg_norm_w"][l:l + 1], scw=w["ssd_conv_w"][l], scb=w["ssd_conv_b"][l:l + 1],
            bias=_pad_lanes(w["ssd_dt_bias"][l]), alog=_pad_lanes(w["ssd_a_log"][l]),
            dskip=jnp.repeat(w["ssd_d"][l], SSD_P)[None, :], snw=w["ssd_norm_w"][l:l + 1],
            w_in=w_in_l, w_out=w_out_l, scale=scale, gate=gate)
        u, h = _inproj_fwd(x, prm["nw"], scale, shift, prm["w_in"])
        ycat = lax.empty((S, D_INNER), BF16)
        lru_args = (u, prm["cw"], prm["cb"], prm["wa"], prm["ba"], prm["wx"], prm["bx"], prm["lam"])
        ycat, h_lru = _lru_fwd(*lru_args, ycat)
        ycat, o_b, hg_st = _hg_fwd(u, prm["lb"], prm["hnw"], mall, ycat)
        xbc = _ssdconv_fwd(u, prm["scw"], prm["scb"])
        ssd_args = (u, xbc, prm["bias"], prm["alog"], prm["dskip"], prm["snw"], consts)
        ycat, y_ssd, ssd_st = _ssd_fwd(*ssd_args, ycat)
        x_new, y = _outproj_fwd(ycat, prm["w_out"], x, gate)
        saved.append((prm, x, u, h, ycat, lru_args, h_lru, o_b, hg_st, ssd_args, y_ssd, ssd_st, y))
        x = x_new
    dx, red = _loss_head(x, w["final_norm_w"][None, :], target)
    loss = red[1, 0]
    g = {n: [None] * DEPTH for n in WEIGHTS}
    g["final_norm_w"] = red[0]
    dmod, dlb = [None] * DEPTH, [None] * DEPTH
    for l in reversed(range(DEPTH)):
        prm, x, u, h, ycat, lru_args, h_lru, o_b, hg_st, ssd_args, y_ssd, ssd_st, y = saved[l]
        dycat, g_out, dgate = _outproj_bwd(dx, y, prm["gate"], ycat, prm["w_out"])
        token = emit(l, "w_out", g_out)
        du = lax.empty((S, N_PAD), BF16)
        ssd_args = ssd_args[:5] + (ssd_args[5] + token,) + ssd_args[6:]
        du, dxbc, sred = _ssd_bwd(*ssd_args, y_ssd, ssd_st, dycat, du)
        du, cred = _ssdconv_bwd(u, prm["scw"], prm["scb"], dxbc, du)
        du, hred = _hg_bwd(u, prm["lb"], prm["hnw"], mall, mall_t, o_b, hg_st, dycat, du)
        du, lred, gwa, gwx = _lru_bwd(*lru_args, h_lru, dycat, du)
        token = emit(l, "w_in", _inproj_bwd_w(h, du))
        dx, ired = _inproj_bwd_x(du, prm["w_in"], x, prm["nw"], prm["scale"] + token, dx)
        g["norm_w"][l] = ired[2]
        dmod[l] = jnp.concatenate([ired[0], ired[1], dgate[0]])
        g["lru_conv_w"][l], g["lru_conv_b"][l] = lred[0:4], lred[4]
        g["lru_ba"][l], g["lru_bx"][l], g["lru_lambda"][l] = lred[5].reshape(8, 64), lred[6].reshape(8, 64), lred[7]
        g["lru_wa"][l], g["lru_wx"][l] = _diag_blocks(gwa), _diag_blocks(gwx)
        g["hg_norm_w"][l], dlb[l] = hred[0], hred[1]
        g["ssd_conv_w"][l], g["ssd_conv_b"][l] = cred[0:4], cred[4]
        g["ssd_norm_w"][l] = sred[0]
        g["ssd_d"][l] = sred[1].reshape(SSD_HEADS, SSD_P).sum(-1)
        g["ssd_dt_bias"][l] = sred[2, :SSD_HEADS]
        g["ssd_a_log"][l] = -sred[3, :SSD_HEADS] * jnp.exp(w["ssd_a_log"][l])
    g["hg_lb_logits"] = _lower_bounds_bwd(p_lb, jnp.stack(dlb))
    for n in WEIGHTS:
        if isinstance(g[n], list) and g[n][0] is not None:
            g[n] = jnp.stack(g[n])
    return loss, dx, jnp.stack(dmod), g


def kernel(x, c, norm_w, w_ada, b_ada, w_in, lru_conv_w, lru_conv_b, lru_wa, lru_ba, lru_wx, lru_bx, lru_lambda, hg_lb_logits, hg_norm_w, ssd_conv_w, ssd_conv_b, ssd_dt_bias, ssd_a_log, ssd_d, ssd_norm_w, w_out, final_norm_w, loss_target, m_norm_w, m_w_ada, m_b_ada, m_w_in, m_lru_conv_w, m_lru_conv_b, m_lru_wa, m_lru_ba, m_lru_wx, m_lru_bx, m_lru_lambda, m_hg_lb_logits, m_hg_norm_w, m_ssd_conv_w, m_ssd_conv_b, m_ssd_dt_bias, m_ssd_a_log, m_ssd_d, m_ssd_norm_w, m_w_out, m_final_norm_w, v_norm_w, v_w_ada, v_b_ada, v_w_in, v_lru_conv_w, v_lru_conv_b, v_lru_wa, v_lru_ba, v_lru_wx, v_lru_bx, v_lru_lambda, v_hg_lb_logits, v_hg_norm_w, v_ssd_conv_w, v_ssd_conv_b, v_ssd_dt_bias, v_ssd_a_log, v_ssd_d, v_ssd_norm_w, v_w_out, v_final_norm_w):
    return _step(x, c, norm_w, w_ada, b_ada, w_in, lru_conv_w, lru_conv_b, lru_wa, lru_ba, lru_wx, lru_bx, lru_lambda, hg_lb_logits, hg_norm_w, ssd_conv_w, ssd_conv_b, ssd_dt_bias, ssd_a_log, ssd_d, ssd_norm_w, w_out, final_norm_w, loss_target, m_norm_w, m_w_ada, m_b_ada, m_w_in, m_lru_conv_w, m_lru_conv_b, m_lru_wa, m_lru_ba, m_lru_wx, m_lru_bx, m_lru_lambda, m_hg_lb_logits, m_hg_norm_w, m_ssd_conv_w, m_ssd_conv_b, m_ssd_dt_bias, m_ssd_a_log, m_ssd_d, m_ssd_norm_w, m_w_out, m_final_norm_w, v_norm_w, v_w_ada, v_b_ada, v_w_in, v_lru_conv_w, v_lru_conv_b, v_lru_wa, v_lru_ba, v_lru_wx, v_lru_bx, v_lru_lambda, v_hg_lb_logits, v_hg_norm_w, v_ssd_conv_w, v_ssd_conv_b, v_ssd_dt_bias, v_ssd_a_log, v_ssd_d, v_ssd_norm_w, v_w_out, v_final_norm_w)


def _step(*args):
    a = dict(zip(INPUTS, args, strict=True))
    me = 4 * lax.axis_index("x") + 2 * lax.axis_index("y") + lax.axis_index("c")
    x, target = a["x"][0], a["loss_target"][0]

    c_all = _all_gather(a["c"], "gather_c")[:, 0, :]
    b_cols = lax.dynamic_slice_in_dim(a["b_ada"], me * ADA_COLS, ADA_COLS, axis=1)[:, None, :]
    mod_parts = _all_gather(_ada_fwd(c_all, a["w_ada"], b_cols), "gather_mod")
    mod = lax.dynamic_index_in_dim(mod_parts, me, axis=2, keepdims=False)
    mod = mod.transpose(1, 0, 2).reshape(DEPTH, 3 * D_MODEL)

    w = {n: a[n] for n in SMALL}

    w_in_b, w_out_b = a["w_in"].astype(BF16), a["w_out"].astype(BF16)
    conv_own = jnp.concatenate([a["lru_conv_w"], a["ssd_conv_w"]], axis=-1)
    cols, rows_out = N_IN // N_DEV, D_INNER // N_DEV

    def gather_start(l, after):
        srcs = [w_in_b[l], w_out_b[l]] + ([conv_own] if l == 0 else [])
        lands = [lax.empty((N_DEV,) + s.shape, s.dtype) for s in srcs]
        return _exchange_start(f"gather_start_{l}", srcs, lands, "chip" if l == 0 else "gather", after=after)

    def gather_pass(name, st, after, also=()):
        landed = _exchange_wait(name + "_wait", st, after, also)
        st2 = _exchange_start(name + "_pass", st["srcs"], landed, "pass")
        return _exchange_wait(name + "_passed", st2, after)

    gathers = {0: gather_start(0, mod)}

    def fetch(l, x_l):
        if l == 0:
            landed = gather_pass("gather_0", gathers[0], x_l, also=(a["m_w_in"], a["v_w_in"]))
        else:
            landed = _exchange_wait(f"gather_wait_{l}", gathers[l], x_l)
        land_out = lax.dynamic_update_index_in_dim(landed[1], w_out_b[l], me, 0)
        if l == 0:
            conv = lax.dynamic_update_index_in_dim(landed[2], conv_own, me, 0).transpose(1, 2, 0, 3)
            w["lru_conv_w"] = conv[..., :64].reshape(DEPTH, 4, LRU_W)
            w["ssd_conv_w"] = conv[..., 64:].reshape(DEPTH, 4, SSD_CONV)
        token = 0.0
        if l + 1 < DEPTH:
            gathers[l + 1] = gather_start(l + 1, land_out)
            token = gathers[l + 1]["token"]
        return _relayout_in(landed[0], w_in_b[l]), land_out.reshape(D_INNER, D_MODEL), token

    scatters = {"w_in": {}, "w_out": {}}
    lands = {"w_in": lax.empty((N_DEV, DEPTH, D_MODEL, cols), BF16),
             "w_out": lax.empty((N_DEV, DEPTH, rows_out, D_MODEL), BF16)}
    own = {"w_in": [None] * DEPTH, "w_out": [None] * DEPTH}

    def emit(l, name, grad):
        grad = _relayout_grad(grad) if name == "w_in" else grad.reshape(N_DEV, rows_out, D_MODEL)
        st = _exchange_start(f"scatter_start_{name}_{l}", [grad], [lands[name]], "scatter", layer=l)
        scatters[name][l] = st
        lands[name] = st["lands"][0]
        return st["token"]

    loss_own, dx, dmod, g = _local_step(x, mod, target, w, fetch, emit)

    def sharded(name, parts, own=None, **kw):
        return _adamw(parts, a[name], a["m_" + name], a["v_" + name], "adamw_" + name + kw.pop("tag", ""), own=own, **kw)

    g["b_ada"] = dmod
    small_own = _flatten_small(g, last=loss_own)
    small_st = _exchange_start("gather_small", [small_own], [lax.empty((N_DEV,) + small_own.shape, F32)], "chip",
                               after=dx)
    big = {}
    after = small_st["token"] + dx[0:8, 0:LANE]
    def own_slice(st):
        return lax.dynamic_index_in_dim(st["srcs"][0], me, 0, keepdims=False)

    for name in ("w_out", "w_in"):
        for l in reversed(range(1, DEPTH)):
            scatters[name][l]["lands"] = [lands[name]]
            lands[name] = _exchange_wait(f"scatter_wait_{name}_{l}", scatters[name][l], after)[0]
            own[name][l] = own_slice(scatters[name][l])
        upper = sharded(name, lands[name], jnp.stack(own[name][1:]), layers=(1, DEPTH), tag="_upper")
        scatters[name][0]["lands"] = [lands[name]]
        lands[name] = _exchange_wait(f"scatter_wait_{name}_0", scatters[name][0], upper[1])[0]
        big[name] = sharded(name, lands[name], own_slice(scatters[name][0])[None], layers=(0, 1), prev=upper)
        after = big[name][1]
    small = gather_pass("gather_small", small_st, after)[0]
    outs = _adamw(small[:, None], *[_flatten_small(a, p)[None] for p in ("", "m_", "v_")], "adamw_small",
                  own=small_own[None])
    res = [_split_small(o[0], a) for o in outs]
    losses = lax.dynamic_update_index_in_dim(small[:, -1, 0], loss_own, me, 0)
    loss = jnp.sum(losses)

    off = _small_rows(a)[0]["b_ada"][0]
    dmod_all = lax.dynamic_update_index_in_dim(small[:, off:off + DEPTH * 3 * D_MODEL // SMALL_ROW],
                                               dmod.reshape(-1, SMALL_ROW), me, 0)
    dmod_all = dmod_all.reshape(N_DEV, DEPTH, 3 * D_MODEL).transpose(1, 0, 2)
    dmod_cols = lax.dynamic_slice_in_dim(dmod_all, me * ADA_COLS, ADA_COLS, axis=2)
    dmod_pad = jnp.pad(dmod_cols, ((0, 0), (0, LANE - N_DEV), (0, 0)))
    ct_pad = jnp.pad(c_all.T, ((0, 0), (0, LANE - N_DEV)))
    big["w_ada"] = sharded("w_ada", _ada_bwd(ct_pad, dmod_pad)[None])
    g_conv = jnp.concatenate([g["lru_conv_w"].reshape(DEPTH, 4, N_DEV, 64), g["ssd_conv_w"].reshape(DEPTH, 4, N_DEV, 192)],
                             axis=-1).transpose(2, 0, 1, 3)
    conv_parts = _all_to_all(g_conv, "scatter_conv")
    big["lru_conv_w"] = sharded("lru_conv_w", conv_parts[..., :64])
    big["ssd_conv_w"] = sharded("ssd_conv_w", conv_parts[..., 64:])

    out = [loss, dx[None]]
    for k in range(4):
        out += [big[n][k] if n in big else res[k][n] for n in WEIGHTS]
    return tuple(out)
```

```python
import functools

import numpy as np
import jax
import jax.numpy as jnp
from jax import lax
from jax.experimental import pallas as pl
from jax.experimental.pallas import tpu as pltpu

F32 = jnp.float32
BF16 = jnp.bfloat16
SDS = jax.ShapeDtypeStruct

N_DEV = 8
DEPTH = 4
D_MODEL = 1024
D_INNER = 2048
EPS = 1e-6
LRU_W = 512
LRU_C = 8.0
HG_W = 512
HG_CHUNK = 64
HG_HEADS = 4
SSD_W = 1024
SSD_HEADS = 16
SSD_P = 64
SSD_N = 128
SSD_CHUNK = 128
SSD_CONV = 1536
N_IN = 5648
N_PAD = 5760
OFF_HG = 0
OFF_LRU = 2048
OFF_XBC = 3072
OFF_Z = 4608
LANE = 128
VMEM_LIMIT = 56 * 1024 * 1024
NEG = -1e30

ADAM_LR = 0.001
ADAM_B1 = 0.9
ADAM_B2 = 0.999
ADAM_EPS = 1e-08
ADAM_WD = 0.01
ADAM_STEP = 10


def _cp(sem=None):
    return pltpu.CompilerParams(dimension_semantics=sem, vmem_limit_bytes=VMEM_LIMIT)


def _dg(a, b, ca, cb):
    return lax.dot_general(a, b, (((ca,), (cb,)), ((), ())), preferred_element_type=F32)


def _mm(a, b):
    return _dg(a, b, 1, 0)


def _mm_nt(a, b):
    return _dg(a, b, 1, 1)


def _mm_tn(a, b):
    return _dg(a, b, 0, 0)


def _bf(x):
    return x.astype(BF16)


def _f(x):
    return x.astype(F32)


def _split3(x):
    hi = x.astype(BF16)
    r = x - hi.astype(F32)
    mid = r.astype(BF16)
    lo = (r - mid.astype(F32)).astype(BF16)
    return hi, mid, lo


def _sel_r(x, m):
    hi, mid, lo = _split3(x)
    return _mm(hi, m) + _mm(mid, m) + _mm(lo, m)


def _sel_l(m, x):
    hi, mid, lo = _split3(x)
    return _mm(m, hi) + _mm(m, mid) + _mm(m, lo)


def _sel_l2(m, x):
    hi = x.astype(BF16)
    lo = (x - hi.astype(F32)).astype(BF16)
    return _mm(m, hi) + _mm(m, lo)


def _sel_tn(x, m):
    hi, mid, lo = _split3(x)
    return _mm_tn(hi, m) + _mm_tn(mid, m) + _mm_tn(lo, m)


def _sigmoid(x):
    return 1.0 / (1.0 + jnp.exp(-x))


def _silu(x):
    return x * _sigmoid(x)


def _dsilu(x):
    s = _sigmoid(x)
    return s * (1.0 + x * (1.0 - s))


def _softplus(x):
    return jnp.maximum(x, 0.0) + jnp.log(1.0 + jnp.exp(-jnp.abs(x)))


def _expm1(z):
    series = z * (1.0 + z * (1.0 / 2) * (1.0 + z * (1.0 / 3) * (1.0 + z * (1.0 / 4) * (
        1.0 + z * (1.0 / 5) * (1.0 + z * (1.0 / 6) * (1.0 + z * (1.0 / 7)))))))
    return jnp.where(jnp.abs(z) < 0.3, series, jnp.exp(z) - 1.0)


def _iota(shape, dim):
    return lax.broadcasted_iota(jnp.int32, shape, dim)


def _last_row(x, rows):
    return jnp.sum(jnp.where(rows == x.shape[0] - 1, x, 0.0), axis=0, keepdims=True)


def _shift_down(x, d, rows, fill=0.0):
    return jnp.where(rows >= d, pltpu.roll(x, d, 0), fill)


def _shift_up(x, d, rows, fill=0.0):
    n = x.shape[0]
    return jnp.where(rows < n - d, pltpu.roll(x, n - d, 0), fill)


def _conv_fwd(x, cw_ref, cb_ref, rows):
    out = cb_ref[...] + cw_ref[pl.ds(3, 1), :] * x
    for k in range(3):
        out = out + cw_ref[pl.ds(k, 1), :] * _shift_down(x, 3 - k, rows)
    return out


def _conv_bwd(x, dco, cw_ref, rows):
    dx = cw_ref[pl.ds(3, 1), :] * dco
    dws = []
    for k in range(3):
        dx = dx + cw_ref[pl.ds(k, 1), :] * _shift_up(dco, 3 - k, rows)
        dws.append(jnp.sum(dco * _shift_down(x, 3 - k, rows), axis=0, keepdims=True))
    dws.append(jnp.sum(dco * x, axis=0, keepdims=True))
    return dx, dws, jnp.sum(dco, axis=0, keepdims=True)


def _vec(n):
    return pl.BlockSpec((1, n), lambda *_: (0, 0))


def _full(shape):
    nd = len(shape)
    return pl.BlockSpec(shape, lambda *_: (0,) * nd)


def _inproj_fwd(x, nw, scale, shift, w):
    S = x.shape[0]
    tm = min(256, S)

    def body(x_ref, nw_ref, sc_ref, sh_ref, w_ref, u_ref, h_ref):
        xv = x_ref[...]
        inv = lax.rsqrt(jnp.mean(xv * xv, axis=-1, keepdims=True) + EPS)
        h = ((xv * inv) * nw_ref[...] * (1.0 + sc_ref[...]) + sh_ref[...]).astype(BF16)
        h_ref[...] = h
        u_ref[...] = _mm(h, w_ref[...])

    return pl.pallas_call(
        body, name="inproj_fwd", grid=(S // tm,),
        in_specs=[pl.BlockSpec((tm, D_MODEL), lambda i: (i, 0)), _vec(D_MODEL), _vec(D_MODEL), _vec(D_MODEL),
                  _full((D_MODEL, N_PAD))],
        out_specs=[pl.BlockSpec((tm, N_PAD), lambda i: (i, 0)), pl.BlockSpec((tm, D_MODEL), lambda i: (i, 0))],
        out_shape=[SDS((S, N_PAD), F32), SDS((S, D_MODEL), BF16)],
        compiler_params=_cp(("parallel",)),
    )(x, nw, scale, shift, w)


def _inproj_bwd_x(du, w, x, nw, scale, dxn):
    S = x.shape[0]
    tm = min(256, S)

    def body(du_ref, w_ref, x_ref, nw_ref, sc_ref, dxn_ref, dx_ref, red_ref):
        @pl.when(pl.program_id(0) == 0)
        def _():
            red_ref[...] = jnp.zeros_like(red_ref)

        dh = _mm_nt(du_ref[...], w_ref[...])
        xv = x_ref[...]
        inv = lax.rsqrt(jnp.mean(xv * xv, axis=-1, keepdims=True) + EPS)
        xhat = xv * inv
        nwv = nw_ref[...]
        g1 = 1.0 + sc_ref[...]
        dxhat = dh * nwv * g1
        dx = inv * (dxhat - xhat * jnp.mean(dxhat * xhat, axis=-1, keepdims=True))
        dx_ref[...] = dxn_ref[...] + dx
        red_ref[0:1, :] += jnp.sum(dh, axis=0, keepdims=True)
        red_ref[1:2, :] += jnp.sum(dh * xhat * nwv, axis=0, keepdims=True)
        red_ref[2:3, :] += jnp.sum(dh * xhat * g1, axis=0, keepdims=True)

    row = pl.BlockSpec((tm, D_MODEL), lambda i: (i, 0))
    return pl.pallas_call(
        body, name="inproj_bwd_x", grid=(S // tm,),
        in_specs=[pl.BlockSpec((tm, N_PAD), lambda i: (i, 0)), _full((D_MODEL, N_PAD)), row, _vec(D_MODEL),
                  _vec(D_MODEL), row],
        out_specs=[row, _full((8, D_MODEL))],
        out_shape=[SDS((S, D_MODEL), F32), SDS((8, D_MODEL), F32)],
        compiler_params=_cp(("arbitrary",)),
    )(du, w, x, nw, scale, dxn)


def _inproj_bwd_w(h, du):
    S = h.shape[0]
    tn = 640

    def body(h_ref, du_ref, gw_ref):
        gw_ref[...] = _mm_tn(h_ref[...], _bf(du_ref[...]))

    return pl.pallas_call(
        body, name="inproj_bwd_w", grid=(N_PAD // tn,),
        in_specs=[_full((S, D_MODEL)), pl.BlockSpec((S, tn), lambda j: (0, j))],
        out_specs=pl.BlockSpec((D_MODEL, tn), lambda j: (0, j)),
        out_shape=SDS((D_MODEL, N_PAD), F32),
        compiler_params=_cp(("parallel",)),
    )(h, du)


def _scan_block(a, b, rows):
    d = 1
    while d < a.shape[0]:
        a_s = _shift_down(a, d, rows, 1.0)
        b_s = _shift_down(b, d, rows, 0.0)
        b = a * b_s + b
        a = a * a_s
        d *= 2
    return a, b


def _rscan_block(c, g, rows):
    d = 1
    while d < c.shape[0]:
        c_s = _shift_up(c, d, rows, 1.0)
        g_s = _shift_up(g, d, rows, 0.0)
        g = g + c * g_s
        c = c * c_s
        d *= 2
    return c, g


LRU_BLOCK = 256


def _lru_gates(xa, wa_ref, ba_ref, wx_ref, bx_ref, lam_ref):
    sp = _softplus(-lam_ref[...])
    xb = _bf(xa)
    r = _sigmoid(_mm(xb, wa_ref[...]) + ba_ref[...])
    ig = _sigmoid(_mm(xb, wx_ref[...]) + bx_ref[...])
    la = -LRU_C * r * sp
    a = jnp.exp(la)
    mult = jnp.sqrt(-_expm1(2.0 * la))
    return sp, r, ig, la, a, mult


def _lru_specs(S):
    t128 = pl.BlockSpec((1, LANE), lambda t: (0, t))
    return [pl.BlockSpec((S, 2 * LANE), lambda t: (0, OFF_LRU // (2 * LANE) + t)),
            pl.BlockSpec((4, LANE), lambda t: (0, t)), t128,
            pl.BlockSpec((None, LANE, LANE), lambda t: (t, 0, 0)), t128,
            pl.BlockSpec((None, LANE, LANE), lambda t: (t, 0, 0)), t128, t128]


def _lru_fwd(u, cw, cb, wa, ba, wx, bx, lam, ycat):
    S = u.shape[0]
    tb = min(LRU_BLOCK, S)

    def body(u_ref, cw_ref, cb_ref, wa_ref, ba_ref, wx_ref, bx_ref, lam_ref, ycat_in, ycat_ref, h_ref, a_scr, b_scr):
        del ycat_in
        rows = _iota((S, LANE), 0)
        xa = _conv_fwd(_f(u_ref[:, 0:LANE]), cw_ref, cb_ref, rows)
        _, _, ig, _, a, mult = _lru_gates(xa, wa_ref, ba_ref, wx_ref, bx_ref, lam_ref)
        a_scr[...] = a
        b_scr[...] = mult * (ig * xa)
        rows_b = _iota((tb, LANE), 0)

        def blk(j, hprev):
            sl = pl.ds(pl.multiple_of(j * tb, tb), tb)
            acum, hloc = _scan_block(a_scr[sl, :], b_scr[sl, :], rows_b)
            hf = hloc + acum * hprev
            h_ref[sl, :] = hf
            return _last_row(hf, rows_b)

        lax.fori_loop(0, S // tb, blk, jnp.zeros((1, LANE), F32))
        ycat_ref[...] = _bf(h_ref[...] * _silu(_f(u_ref[:, LANE:2 * LANE])))

    col = pl.BlockSpec((S, LANE), lambda t: (0, t))
    return pl.pallas_call(
        body, name="lru_fwd", grid=(LRU_W // LANE,),
        in_specs=_lru_specs(S) + [pl.BlockSpec(memory_space=pl.ANY)],
        out_specs=[col, col],
        out_shape=[SDS((S, D_INNER), BF16), SDS((S,LRU_W), F32)],
        scratch_shapes=[pltpu.VMEM((S, LANE), F32), pltpu.VMEM((S, LANE), F32)],
        input_output_aliases={8: 0},
        compiler_params=_cp(("parallel",)),
    )(u, cw, cb, wa, ba, wx, bx, lam, ycat)


def _lru_bwd(u, cw, cb, wa, ba, wx, bx, lam, h_lru, dycat, du):
    S = u.shape[0]
    tb = min(LRU_BLOCK, S)

    def body(u_ref, cw_ref, cb_ref, wa_ref, ba_ref, wx_ref, bx_ref, lam_ref, h_ref, dy_ref, du_in,
             du_ref, red_ref, gwa_ref, gwx_ref, c_scr, g_scr, l_scr):
        del du_in
        rows = _iota((S, LANE), 0)
        ax = _f(u_ref[:, 0:LANE])
        ag = _f(u_ref[:, LANE:2 * LANE])
        xa = _conv_fwd(ax, cw_ref, cb_ref, rows)
        sp, r, ig, la, a, mult = _lru_gates(xa, wa_ref, ba_ref, wx_ref, bx_ref, lam_ref)
        h = h_ref[...]
        dy = _f(dy_ref[...])
        du_ref[:, LANE:2 * LANE] = _bf(dy * h * _dsilu(ag))
        c_scr[...] = _shift_up(a, 1, rows, 0.0)
        g_scr[...] = dy * _silu(ag)
        rows_b = _iota((tb, LANE), 0)
        nb = S // tb

        def blk(jj, lnext):
            j = nb - 1 - jj
            sl = pl.ds(pl.multiple_of(j * tb, tb), tb)
            ccum, lloc = _rscan_block(c_scr[sl, :], g_scr[sl, :], rows_b)
            lam_t = lloc + ccum * lnext
            l_scr[sl, :] = lam_t
            return jnp.sum(jnp.where(rows_b == 0, lam_t, 0.0), axis=0, keepdims=True)

        lax.fori_loop(0, nb, blk, jnp.zeros((1, LANE), F32))
        db = l_scr[...]
        da = db * _shift_down(h, 1, rows)
        dmult = db * ig * xa
        dig = db * mult * xa
        dxa = db * mult * ig
        dla = da * a - dmult * (a * a) / mult
        dr = -LRU_C * sp * dla
        dsp = jnp.sum(-LRU_C * r * dla, axis=0, keepdims=True)
        dlam = -dsp * _sigmoid(-lam_ref[...])
        dzr = dr * r * (1.0 - r)
        dzi = dig * ig * (1.0 - ig)
        dzr_b, dzi_b, xa_b = _bf(dzr), _bf(dzi), _bf(xa)
        dxa = dxa + _mm_nt(dzr_b, wa_ref[...]) + _mm_nt(dzi_b, wx_ref[...])
        gwa_ref[...] = _mm_tn(xa_b, dzr_b)
        gwx_ref[...] = _mm_tn(xa_b, dzi_b)
        dax, dws, dcb = _conv_bwd(ax, dxa, cw_ref, rows)
        du_ref[:, 0:LANE] = _bf(dax)
        parts = dws + [dcb, jnp.sum(dzr, axis=0, keepdims=True), jnp.sum(dzi, axis=0, keepdims=True), dlam]
        for n, p in enumerate(parts):
            red_ref[pl.ds(n, 1), :] = p

    col = pl.BlockSpec((S, LANE), lambda t: (0, t))
    gw = pl.BlockSpec((None, LANE, LANE), lambda t: (t, 0, 0))
    return pl.pallas_call(
        body, name="lru_bwd", grid=(LRU_W // LANE,),
        in_specs=_lru_specs(S) + [col, col, pl.BlockSpec(memory_space=pl.ANY)],
        out_specs=[pl.BlockSpec((S, 2 * LANE), lambda t: (0, OFF_LRU // (2 * LANE) + t)),
                   pl.BlockSpec((8, LANE), lambda t: (0, t)), gw, gw],
        out_shape=[SDS((S, N_PAD), BF16), SDS((8, LRU_W), F32), SDS((4, LANE, LANE), F32), SDS((4, LANE, LANE), F32)],
        scratch_shapes=[pltpu.VMEM((S, LANE), F32)] * 3,
        input_output_aliases={10: 0},
        compiler_params=_cp(("parallel",)),
    )(u, cw, cb, wa, ba, wx, bx, lam, h_lru, dycat, du)


HG_LEVELS = 6


def _hg_consts():
    C = HG_CHUNK
    t = np.arange(C)[:, None]
    r = np.arange(C)[None, :]
    mats = []
    for l in range(HG_LEVELS):
        b = 1 << l
        upper = (t % (2 * b)) >= b
        anchor = (t // (2 * b)) * 2 * b + b - 1
        mats.append((upper & (r > anchor) & (r <= t)) | ((~upper) & (r > t) & (r <= anchor)))
    mats.append(r <= t)
    mats.append(r > t)
    return np.concatenate(mats, 0).astype(np.float32)


def _hg_factors(hf, lb, mall):
    s = _sigmoid(hf)
    f = lb + (1.0 - lb) * s
    lf = jnp.log(f)
    k = (1.0 - lb) * _sigmoid(-hf)
    e = jnp.exp(_sel_l(mall, lf))
    C = HG_CHUNK
    eq = [e[l * C:(l + 1) * C] for l in range(HG_LEVELS)]
    ecum = e[HG_LEVELS * C:(HG_LEVELS + 1) * C]
    erem = e[(HG_LEVELS + 1) * C:(HG_LEVELS + 2) * C]
    return s, f, k, eq, eq, ecum, erem


def _hg_masks():
    C = HG_CHUNK
    ri, ci = _iota((C, C), 0), _iota((C, C), 1)
    rr = _iota((C, LANE), 0)
    gm = [(lax.shift_right_logical(ri, l + 1) == lax.shift_right_logical(ci, l + 1)).astype(F32)
          for l in range(HG_LEVELS)]
    up = [(lax.shift_right_logical(rr, l) & 1) == 1 for l in range(HG_LEVELS)]
    eye = (ri == ci).astype(F32)
    return gm, up, eye, rr


def _hg_scores(qh, kh, eq, ek, sl, gm, up, eye):
    qs, ks = [], []
    p = _mm_nt(_bf(qh), _bf(kh)) * eye
    for l in range(HG_LEVELS):
        ql = jnp.where(up[l], qh * eq[l][:, sl], 0.0)
        kl = jnp.where(up[l], 0.0, kh * ek[l][:, sl])
        p = p + _mm_nt(_bf(ql), _bf(kl)) * gm[l]
        qs.append(ql)
        ks.append(kl)
    return p, qs, ks


HG_SUB = 4


def _hg_fwd(u, lb, nw, mall, ycat):
    S = u.shape[0]
    C = HG_CHUNK
    n = S // C
    rows = HG_SUB * C

    def body(u_ref, lb_ref, nw_ref, mall_ref, ycat_in, ycat_ref, o_ref, st_ref, st):
        del ycat_in

        @pl.when(pl.program_id(0) == 0)
        def _():
            st[...] = jnp.zeros_like(st)

        gm, up, eye, rr = _hg_masks()
        for sub in range(HG_SUB):
            r = slice(sub * C, (sub + 1) * C)
            q = _silu(_f(u_ref[r, 0:512]))
            v = u_ref[r, 1024:1536]
            _, _, k, eq, ek, ecum, erem = _hg_factors(_f(u_ref[r, 512:1024]), lb_ref[...], mall_ref[...])
            for h in range(HG_HEADS):
                sl = slice(h * LANE, (h + 1) * LANE)
                qh, kh, vh = q[:, sl], k[:, sl], _bf(v[:, sl])
                p, _, _ = _hg_scores(qh, kh, eq, ek, sl, gm, up, eye)
                sth = st[h]
                st_ref[sub, h] = sth
                o_ref[r, sl] = _mm(_bf(p), vh) + _mm_nt(_bf(qh * ecum[:, sl]), _bf(sth))
                st[h] = sth * _last_row(ecum[:, sl], rr) + _mm_tn(vh, _bf(kh * erem[:, sl]))
            o = o_ref[r, :]
            inv = lax.rsqrt(jnp.mean(o * o, axis=-1, keepdims=True) + EPS)
            ycat_ref[r, :] = _bf((o * inv) * nw_ref[...] * _silu(_f(u_ref[r, 1536:2048])))

    return pl.pallas_call(
        body, name="hg_fwd", grid=(n // HG_SUB,),
        in_specs=[pl.BlockSpec((rows, 2048), lambda i: (i, 0)), _vec(HG_W), _vec(HG_W), _full(mall.shape),
                  pl.BlockSpec(memory_space=pl.ANY)],
        out_specs=[pl.BlockSpec((rows, HG_W), lambda i: (i, 1)), pl.BlockSpec((rows, HG_W), lambda i: (i, 0)),
                   pl.BlockSpec((HG_SUB, HG_HEADS, LANE, LANE), lambda i: (i, 0, 0, 0))],
        out_shape=[SDS((S, D_INNER), BF16), SDS((S,HG_W), F32), SDS((n, HG_HEADS, LANE, LANE), F32)],
        scratch_shapes=[pltpu.VMEM((HG_HEADS, LANE, LANE), F32)],
        input_output_aliases={4: 0},
        compiler_params=_cp(("arbitrary",)),
    )(u, lb, nw, mall, ycat)


def _hg_bwd(u, lb, nw, mall, mall_t, o_b, states, dycat, du):
    S = u.shape[0]
    C = HG_CHUNK
    n = S // C
    nb = n // HG_SUB
    rows = HG_SUB * C
    L2 = HG_LEVELS

    def body(u_ref, lb_ref, nw_ref, mall_ref, mallt_ref, o_ref, st_ref, dy_ref, du_in, du_ref, red_ref,
             dst, dlast_s, dq_s, dk_s, dex):
        del du_in

        @pl.when(pl.program_id(0) == 0)
        def _():
            dst[...] = jnp.zeros_like(dst)
            red_ref[...] = jnp.zeros_like(red_ref)

        lb = lb_ref[...]
        nwv = nw_ref[...]
        gm, up, eye, rr = _hg_masks()
        for sub in reversed(range(HG_SUB)):
            r = slice(sub * C, (sub + 1) * C)
            hq, hf, hg = _f(u_ref[r, 0:512]), _f(u_ref[r, 512:1024]), _f(u_ref[r, 1536:2048])
            q = _silu(hq)
            v = u_ref[r, 1024:1536]
            s, f, k, eq, ek, ecum, erem = _hg_factors(hf, lb, mall_ref[...])
            o = o_ref[r, :]
            dy = _f(dy_ref[r, :])
            inv = lax.rsqrt(jnp.mean(o * o, axis=-1, keepdims=True) + EPS)
            ohat = o * inv
            du_ref[r, 1536:2048] = _bf(dy * ohat * nwv * _dsilu(hg))
            dn = dy * _silu(hg)
            red_ref[0:1, :] += jnp.sum(dn * ohat, axis=0, keepdims=True)
            dohat = dn * nwv
            do = inv * (dohat - ohat * jnp.mean(dohat * ohat, axis=-1, keepdims=True))
            for h in range(HG_HEADS):
                sl = slice(h * LANE, (h + 1) * LANE)
                qh, kh, vh, doh = q[:, sl], k[:, sl], _bf(v[:, sl]), _bf(do[:, sl])
                p, qs, ks = _hg_scores(qh, kh, eq, ek, sl, gm, up, eye)
                st_f = st_ref[sub, h]
                sth = _bf(st_f)
                dsth = dst[h]
                dsth_b = _bf(dsth)
                qt = qh * ecum[:, sl]
                kt = kh * erem[:, sl]
                elast = _last_row(ecum[:, sl], rr)
                dp = _mm_nt(doh, vh)
                du_ref[r, 1024 + h * LANE:1024 + (h + 1) * LANE] = _bf(_mm_tn(_bf(p), doh) + _mm_nt(_bf(kt), dsth_b))
                dpe = _bf(dp * eye)
                dqt = _mm(doh, sth)
                dkt = _mm(vh, dsth_b)
                dq = dqt * ecum[:, sl] + _mm(dpe, _bf(kh))
                dk = dkt * erem[:, sl] + _mm_tn(dpe, _bf(qh))
                dex[sub, L2 * C:(L2 + 1) * C, sl] = dqt * qt
                dex[sub, (L2 + 1) * C:(L2 + 2) * C, sl] = dkt * kt
                for l in range(HG_LEVELS):
                    dpl = _bf(dp * gm[l])
                    dql = _mm(dpl, _bf(ks[l]))
                    dkl = _mm_tn(dpl, _bf(qs[l]))
                    dq = dq + jnp.where(up[l], dql * eq[l][:, sl], 0.0)
                    dk = dk + jnp.where(up[l], 0.0, dkl * ek[l][:, sl])
                    dex[sub, l * C:(l + 1) * C, sl] = dql * qs[l] + dkl * ks[l]
                dlast_s[sub, :, sl] = jnp.sum(dsth * st_f, axis=0, keepdims=True) * elast
                dst[h] = dsth * elast + _mm_tn(doh, _bf(qt))
                dq_s[sub, :, sl] = dq
                dk_s[sub, :, sl] = dk
            dq = dq_s[sub]
            dk = dk_s[sub]
            dlf = _sel_l2(mallt_ref[...], dex[sub]) + dlast_s[sub]
            du_ref[r, 0:512] = _bf(dq * _dsilu(hq))
            t = (1.0 - s) * (dlf / f - dk)
            du_ref[r, 512:1024] = _bf((1.0 - lb) * s * t)
            red_ref[1:2, :] += jnp.sum(t, axis=0, keepdims=True)

    rev = lambda i: (nb - 1 - i, 0)
    return pl.pallas_call(
        body, name="hg_bwd", grid=(nb,),
        in_specs=[pl.BlockSpec((rows, 2048), rev), _vec(HG_W), _vec(HG_W), _full(mall.shape), _full(mall_t.shape),
                  pl.BlockSpec((rows, HG_W), rev),
                  pl.BlockSpec((HG_SUB, HG_HEADS, LANE, LANE), lambda i: (nb - 1 - i, 0, 0, 0)),
                  pl.BlockSpec((rows, HG_W), lambda i: (nb - 1 - i, 1)), pl.BlockSpec(memory_space=pl.ANY)],
        out_specs=[pl.BlockSpec((rows, 2048), rev), pl.BlockSpec((8, HG_W), lambda i: (0, 0))],
        out_shape=[SDS((S, N_PAD), BF16), SDS((8, HG_W), F32)],
        scratch_shapes=[pltpu.VMEM((HG_HEADS, LANE, LANE), F32), pltpu.VMEM((HG_SUB, 1, HG_W), F32),
                        pltpu.VMEM((HG_SUB, C, HG_W), F32), pltpu.VMEM((HG_SUB, C, HG_W), F32),
                        pltpu.VMEM((HG_SUB, (L2 + 2) * C, HG_W), F32)],
        input_output_aliases={8: 0},
        compiler_params=_cp(("arbitrary",)),
    )(u, lb, nw, mall, mall_t, o_b, states, dycat, du)


def _ssdconv_fwd(u, cw, cb):
    S = u.shape[0]

    def body(u_ref, cw_ref, cb_ref, out_ref):
        rows = _iota((S, LANE), 0)
        out_ref[...] = _silu(_conv_fwd(_f(u_ref[...]), cw_ref, cb_ref, rows))

    return pl.pallas_call(
        body, name="ssdconv_fwd", grid=(SSD_CONV // LANE,),
        in_specs=[pl.BlockSpec((S, LANE), lambda t: (0, OFF_XBC // LANE + t)), pl.BlockSpec((4, LANE), lambda t: (0, t)),
                  pl.BlockSpec((1, LANE), lambda t: (0, t))],
        out_specs=pl.BlockSpec((S, LANE), lambda t: (0, t)),
        out_shape=SDS((S, SSD_CONV), F32),
        compiler_params=_cp(("parallel",)),
    )(u, cw, cb)


def _ssdconv_bwd(u, cw, cb, dxbc, du):
    S = u.shape[0]

    def body(u_ref, cw_ref, cb_ref, d_ref, du_in, du_ref, red_ref):
        del du_in
        rows = _iota((S, LANE), 0)
        x = _f(u_ref[...])
        dco = d_ref[...] * _dsilu(_conv_fwd(x, cw_ref, cb_ref, rows))
        dx, dws, dcb = _conv_bwd(x, dco, cw_ref, rows)
        du_ref[...] = _bf(dx)
        for n, p in enumerate(dws + [dcb]):
            red_ref[pl.ds(n, 1), :] = p
        red_ref[pl.ds(5, 3), :] = jnp.zeros((3, LANE), F32)

    ucol = pl.BlockSpec((S, LANE), lambda t: (0, OFF_XBC // LANE + t))
    return pl.pallas_call(
        body, name="ssdconv_bwd", grid=(SSD_CONV // LANE,),
        in_specs=[ucol, pl.BlockSpec((4, LANE), lambda t: (0, t)), pl.BlockSpec((1, LANE), lambda t: (0, t)),
                  pl.BlockSpec((S, LANE), lambda t: (0, t)), pl.BlockSpec(memory_space=pl.ANY)],
        out_specs=[ucol, pl.BlockSpec((8, LANE), lambda t: (0, t))],
        out_shape=[SDS((S, N_PAD), BF16), SDS((8, SSD_CONV), F32)],
        input_output_aliases={4: 0},
        compiler_params=_cp(("parallel",)),
    )(u, cw, cb, dxbc, du)


def _ssd_consts():
    e64 = np.zeros((LANE, SSD_W), np.float32)
    for h in range(SSD_HEADS):
        e64[h, h * SSD_P:(h + 1) * SSD_P] = 1.0
    T = SSD_CHUNK
    tril = (np.arange(T)[None, :] <= np.arange(T)[:, None]).astype(np.float32)
    return e64, tril, tril.T.copy()


def _ssd_common(zdt, bias_ref, alog_ref, tril, e64, cum_ref, cumt_ref):
    T = SSD_CHUNK
    lane = _iota((1, LANE), 1)
    a_neg = jnp.where(lane < SSD_HEADS, -jnp.exp(alog_ref[...]), 0.0)
    dtpre = zdt[:, SSD_W:SSD_W + LANE] + bias_ref[...]
    dt = _softplus(dtpre)
    cum = _sel_l(tril, dt * a_neg)
    cum_ref[...] = cum
    cumt_ref[...] = cum.T
    cum_x = _sel_r(cum, e64)
    last_x = _last_row(cum_x, _iota((T, SSD_W), 0))
    ecum_x = jnp.exp(cum_x)
    erem_x = jnp.exp(last_x - cum_x)
    elast_x = jnp.exp(last_x)
    dt_x = _sel_r(dt, e64)
    return a_neg, dtpre, dt, ecum_x, erem_x, elast_x, dt_x


def _ssd_decay(cum_ref, cumt_ref, h, causal):
    T = SSD_CHUNK
    diff = jnp.broadcast_to(cum_ref[:, pl.ds(h, 1)], (T, T)) - cumt_ref[pl.ds(h, 1), :]
    return jnp.exp(jnp.where(causal, diff, NEG))


def _group_norm_fwd(y1, nwv):
    outs, invs = [], []
    for g in range(2):
        seg = y1[:, g * 512:(g + 1) * 512]
        inv = lax.rsqrt(jnp.mean(seg * seg, axis=-1, keepdims=True) + EPS)
        outs.append(seg * inv * nwv[:, g * 512:(g + 1) * 512])
        invs.append(inv)
    return outs, invs


def _ssd_fwd(u, xbc, bias, alog, dskip_x, nw, consts, ycat):
    S = u.shape[0]
    T = SSD_CHUNK
    n = S // T
    e64, tril, _ = consts

    def body(u_ref, xbc_ref, bias_ref, alog_ref, dx_ref, nw_ref, e64_ref, tril_ref, ycat_in,
             ycat_ref, y_ref, st_ref, st, cumt, cum_e):
        del ycat_in

        @pl.when(pl.program_id(0) == 0)
        def _():
            st[...] = jnp.zeros_like(st)

        zdt = _f(u_ref[...])
        z = zdt[:, 0:SSD_W]
        xs = xbc_ref[:, 0:SSD_W]
        _, _, _, ecum_x, erem_x, elast_x, dt_x = _ssd_common(
            zdt, bias_ref, alog_ref, tril_ref[...], e64_ref[...], cum_e, cumt)
        causal = _iota((T, T), 0) >= _iota((T, T), 1)
        lo = _iota((T, LANE), 1) < SSD_P
        xdt = xs * dt_x
        xrem = xdt * erem_x
        st_ref[...] = st[...]
        for g in range(2):
            gs = slice(g * 512, (g + 1) * 512)
            bg = _bf(xbc_ref[:, SSD_W + g * LANE:SSD_W + (g + 1) * LANE])
            cg = _bf(xbc_ref[:, SSD_W + 256 + g * LANE:SSD_W + 256 + (g + 1) * LANE])
            cb = _mm_nt(cg, bg)
            yin = _mm(cg, _bf(st[:, gs])) * ecum_x[:, gs]
            for j in range(4):
                h0 = 8 * g + 2 * j
                cs = slice(h0 * SSD_P, (h0 + 2) * SSD_P)
                xp = xdt[:, cs]
                s0 = _bf(cb * _ssd_decay(cum_e, cumt, h0, causal))
                s1 = _bf(cb * _ssd_decay(cum_e, cumt, h0 + 1, causal))
                y_ref[:, cs] = (_mm(s0, _bf(jnp.where(lo, xp, 0.0))) + _mm(s1, _bf(jnp.where(lo, 0.0, xp)))
                                + yin[:, j * LANE:(j + 1) * LANE])
            st[:, gs] = st[:, gs] * elast_x[:, gs] + _mm_tn(bg, _bf(xrem[:, gs]))
        y1 = (y_ref[...] + dx_ref[...] * xs) * _silu(z)
        outs, _ = _group_norm_fwd(y1, nw_ref[...])
        for g in range(2):
            ycat_ref[:, g * 512:(g + 1) * 512] = _bf(outs[g])

    return pl.pallas_call(
        body, name="ssd_fwd", grid=(n,),
        in_specs=[pl.BlockSpec((T, SSD_W + LANE), lambda i: (i, OFF_Z // (SSD_W + LANE))),
                  pl.BlockSpec((T, SSD_CONV), lambda i: (i, 0)), _vec(LANE), _vec(LANE), _vec(SSD_W), _vec(SSD_W),
                  _full(e64.shape), _full(tril.shape), pl.BlockSpec(memory_space=pl.ANY)],
        out_specs=[pl.BlockSpec((T, SSD_W), lambda i: (i, 1)), pl.BlockSpec((T, SSD_W), lambda i: (i, 0)),
                   pl.BlockSpec((None, SSD_N, SSD_W), lambda i: (i, 0, 0))],
        out_shape=[SDS((S, D_INNER), BF16), SDS((S,SSD_W), F32), SDS((n, SSD_N, SSD_W), F32)],
        scratch_shapes=[pltpu.VMEM((SSD_N, SSD_W), F32), pltpu.VMEM((LANE, T), F32), pltpu.VMEM((T, LANE), F32)],
        input_output_aliases={8: 0},
        compiler_params=_cp(("arbitrary",)),
    )(u, xbc, bias, alog, dskip_x, nw, _bfc(e64), _bfc(tril), ycat)


def _ssd_bwd(u, xbc, bias, alog, dskip_x, nw, consts, y_ssd, states, dycat, du):
    S = u.shape[0]
    T = SSD_CHUNK
    n = S // T
    e64, tril, triu = consts
    e64t = np.ascontiguousarray(e64.T)

    def body(u_ref, xbc_ref, bias_ref, alog_ref, dx_ref, nw_ref, e64_ref, e64t_ref, tril_ref, triu_ref,
             y_ref, st_ref, dy_ref, du_in, du_ref, dxbc_ref, red_ref, dst, dl_s, cumt, dxdt_s, dy0_s, gb_s, gc_s,
             cum_e, cs_s):
        del du_in

        @pl.when(pl.program_id(0) == 0)
        def _():
            dst[...] = jnp.zeros_like(dst)
            red_ref[...] = jnp.zeros_like(red_ref)
            cs_s[...] = jnp.zeros_like(cs_s)

        zdt = _f(u_ref[...])
        z = zdt[:, 0:SSD_W]
        xs = xbc_ref[:, 0:SSD_W]
        a_neg, dtpre, dt, ecum_x, erem_x, elast_x, dt_x = _ssd_common(
            zdt, bias_ref, alog_ref, tril_ref[...], e64_ref[...], cum_e, cumt)
        causal = _iota((T, T), 0) >= _iota((T, T), 1)
        lo = _iota((T, LANE), 1) < SSD_P
        xdt = xs * dt_x
        xrem = xdt * erem_x
        y = y_ref[...]
        dxv = dx_ref[...]
        nwv = nw_ref[...]
        sz = _silu(z)
        y0 = y + dxv * xs
        y1 = y0 * sz
        for g in range(2):
            gs = slice(g * 512, (g + 1) * 512)
            seg = y1[:, gs]
            inv = lax.rsqrt(jnp.mean(seg * seg, axis=-1, keepdims=True) + EPS)
            shat = seg * inv
            dyg = _f(dy_ref[:, gs])
            red_ref[0:1, gs] += jnp.sum(dyg * shat, axis=0, keepdims=True)
            dsh = dyg * nwv[:, gs]
            dy1g = inv * (dsh - shat * jnp.mean(dsh * shat, axis=-1, keepdims=True))
            du_ref[:, gs] = _bf(dy1g * y0[:, gs] * _dsilu(z[:, gs]))
            dy0_s[:, gs] = dy1g * sz[:, gs]
        dy0 = dy0_s[...]
        red_ref[1:2, :] += jnp.sum(dy0 * xs, axis=0, keepdims=True)
        dyin = dy0 * ecum_x
        lane = _iota((T, LANE), 1)
        dcum = jnp.zeros((T, LANE), F32)

        def decay_grad(h, gm):
            cs_s[pl.ds(h, 1), :] = jnp.sum(gm, axis=0, keepdims=True)
            return jnp.where(lane == h, jnp.sum(gm, axis=1, keepdims=True), 0.0)

        for g in range(2):
            gs = slice(g * 512, (g + 1) * 512)
            bg = _bf(xbc_ref[:, SSD_W + g * LANE:SSD_W + (g + 1) * LANE])
            cg = _bf(xbc_ref[:, SSD_W + 256 + g * LANE:SSD_W + 256 + (g + 1) * LANE])
            cb = _mm_nt(cg, bg)
            dst_f, st_f = dst[:, gs], st_ref[:, gs]
            dstg = _bf(dst_f)
            stg = _bf(st_f)
            dyin_g = _bf(dyin[:, gs])
            xrem_g = _bf(xrem[:, gs])
            dcb = jnp.zeros((T, T), F32)
            dxr = _mm(bg, dstg)
            dxdt_s[:, gs] = dxr * erem_x[:, gs]
            gc_s[:, gs] = dxr * xrem[:, gs]
            gb_s[:, gs] = dyin[:, gs] * _mm(cg, stg)
            dl_s[:, gs] = jnp.sum(dst_f * st_f, axis=0, keepdims=True) * elast_x[:, gs]
            for j in range(4):
                h0 = 8 * g + 2 * j
                cs = slice(h0 * SSD_P, (h0 + 2) * SSD_P)
                xp = xdt[:, cs]
                dyp = dy0[:, cs]
                x_lo, x_hi = _bf(jnp.where(lo, xp, 0.0)), _bf(jnp.where(lo, 0.0, xp))
                d_lo, d_hi = _bf(jnp.where(lo, dyp, 0.0)), _bf(jnp.where(lo, 0.0, dyp))
                l0 = _ssd_decay(cum_e, cumt, h0, causal)
                l1 = _ssd_decay(cum_e, cumt, h0 + 1, causal)
                s0 = cb * l0
                s1 = cb * l1
                ds0 = _mm_nt(d_lo, x_lo)
                ds1 = _mm_nt(d_hi, x_hi)
                dcb = dcb + ds0 * l0 + ds1 * l1
                dxdt_s[:, cs] += _mm_tn(_bf(s0), d_lo) + _mm_tn(_bf(s1), d_hi)
                dcum = dcum + decay_grad(h0, ds0 * s0) + decay_grad(h0 + 1, ds1 * s1)
            dcb_b = _bf(dcb)
            dxbc_ref[:, SSD_W + g * LANE:SSD_W + (g + 1) * LANE] = _mm_tn(dcb_b, cg) + _mm_nt(xrem_g, dstg)
            dxbc_ref[:, SSD_W + 256 + g * LANE:SSD_W + 256 + (g + 1) * LANE] = _mm(dcb_b, bg) + _mm_nt(dyin_g, stg)
            dst[:, gs] = dst_f * elast_x[:, gs] + _mm_tn(cg, dyin_g)
        dxdt = dxdt_s[...]
        dxbc_ref[:, 0:SSD_W] = dxdt * dt_x + dy0 * dxv
        e64t = e64t_ref[...]
        gc = gc_s[...]
        dlast_x = jnp.sum(gc, axis=0, keepdims=True) + dl_s[...]
        dlast = jnp.max(_sel_r(jnp.broadcast_to(dlast_x, (8, SSD_W)), e64t), axis=0, keepdims=True)
        dcum = (dcum - cs_s[...].T + _sel_r(gb_s[...] - gc, e64t)
                + jnp.where(_iota((T, LANE), 0) == T - 1, dlast, 0.0))
        dda = _sel_l(triu_ref[...], dcum)
        ddt = dda * a_neg + _sel_r(dxdt * xs, e64t)
        ddtpre = ddt * _sigmoid(dtpre)
        du_ref[:, SSD_W:SSD_W + LANE] = _bf(jnp.where(lane < SSD_HEADS, ddtpre, 0.0))
        red_ref[2:3, 0:LANE] += jnp.sum(ddtpre, axis=0, keepdims=True)
        red_ref[3:4, 0:LANE] += jnp.sum(dda * dt, axis=0, keepdims=True)

    rev = lambda i: (n - 1 - i, 0)
    return pl.pallas_call(
        body, name="ssd_bwd", grid=(n,),
        in_specs=[pl.BlockSpec((T, SSD_W + LANE), lambda i: (n - 1 - i, OFF_Z // (SSD_W + LANE))),
                  pl.BlockSpec((T, SSD_CONV), rev), _vec(LANE), _vec(LANE), _vec(SSD_W), _vec(SSD_W),
                  _full(e64.shape), _full(e64t.shape), _full(tril.shape), _full(triu.shape),
                  pl.BlockSpec((T, SSD_W), rev), pl.BlockSpec((None, SSD_N, SSD_W), lambda i: (n - 1 - i, 0, 0)),
                  pl.BlockSpec((T, SSD_W), lambda i: (n - 1 - i, 1)), pl.BlockSpec(memory_space=pl.ANY)],
        out_specs=[pl.BlockSpec((T, SSD_W + LANE), lambda i: (n - 1 - i, OFF_Z // (SSD_W + LANE))),
                   pl.BlockSpec((T, SSD_CONV), rev), pl.BlockSpec((8, SSD_W), lambda i: (0, 0))],
        out_shape=[SDS((S, N_PAD), BF16), SDS((S, SSD_CONV), F32), SDS((8, SSD_W), F32)],
        scratch_shapes=[pltpu.VMEM((SSD_N, SSD_W), F32), pltpu.VMEM((1, SSD_W), F32), pltpu.VMEM((LANE, T), F32)]
        + [pltpu.VMEM((T, SSD_W), F32)] * 4 + [pltpu.VMEM((T, LANE), F32), pltpu.VMEM((LANE, T), F32)],
        input_output_aliases={13: 0},
        compiler_params=_cp(("arbitrary",)),
    )(u, xbc, bias, alog, dskip_x, nw, _bfc(e64), _bfc(e64t), _bfc(tril), _bfc(triu), y_ssd, states, dycat, du)


def _bfc(a):
    return jnp.asarray(a, BF16)


def _outproj_fwd(ycat, wo, x, gate):
    S = x.shape[0]
    tm = min(512, S)

    def body(yc_ref, wo_ref, x_ref, g_ref, xn_ref, y_ref):
        y = _mm(_bf(yc_ref[...]), wo_ref[...])
        y_ref[...] = y
        xn_ref[...] = x_ref[...] + g_ref[...] * y

    row = pl.BlockSpec((tm, D_MODEL), lambda i: (i, 0))
    return pl.pallas_call(
        body, name="outproj_fwd", grid=(S // tm,),
        in_specs=[pl.BlockSpec((tm, D_INNER), lambda i: (i, 0)), _full((D_INNER, D_MODEL)), row, _vec(D_MODEL)],
        out_specs=[row, row],
        out_shape=[SDS((S, D_MODEL), F32), SDS((S, D_MODEL), F32)],
        compiler_params=_cp(("parallel",)),
    )(ycat, wo, x, gate)


def _outproj_bwd(dxn, y, gate, ycat, wo):
    S = dxn.shape[0]
    tm = min(512, S)

    def body(dx_ref, y_ref, g_ref, yc_ref, wo_ref, dyc_ref, gwo_ref, dg_ref, acc):
        @pl.when(pl.program_id(0) == 0)
        def _():
            acc[...] = jnp.zeros_like(acc)
            dg_ref[...] = jnp.zeros_like(dg_ref)

        dxv = dx_ref[...]
        dy = _bf(dxv * g_ref[...])
        dg_ref[0:1, :] += jnp.sum(dxv * y_ref[...], axis=0, keepdims=True)
        dyc_ref[...] = _mm_nt(dy, wo_ref[...])
        acc[...] += _mm_tn(_bf(yc_ref[...]), dy)

        @pl.when(pl.program_id(0) == pl.num_programs(0) - 1)
        def _():
            gwo_ref[...] = acc[...].astype(BF16)

    row = pl.BlockSpec((tm, D_MODEL), lambda i: (i, 0))
    wide = pl.BlockSpec((tm, D_INNER), lambda i: (i, 0))
    return pl.pallas_call(
        body, name="outproj_bwd", grid=(S // tm,),
        in_specs=[row, row, _vec(D_MODEL), wide, _full((D_INNER, D_MODEL))],
        out_specs=[wide, _full((D_INNER, D_MODEL)), _full((8, D_MODEL))],
        out_shape=[SDS((S, D_INNER), F32), SDS((D_INNER, D_MODEL), BF16), SDS((8, D_MODEL), F32)],
        scratch_shapes=[pltpu.VMEM((D_INNER, D_MODEL), F32)],
        compiler_params=_cp(("arbitrary",)),
    )(dxn, y, gate, ycat, wo)


def _loss_head(x, fw, target):
    S = x.shape[0]
    tm = min(512, S)

    def body(x_ref, fw_ref, t_ref, dx_ref, red_ref):
        @pl.when(pl.program_id(0) == 0)
        def _():
            red_ref[...] = jnp.zeros_like(red_ref)

        xv = x_ref[...]
        fwv = fw_ref[...]
        inv = lax.rsqrt(jnp.mean(xv * xv, axis=-1, keepdims=True) + EPS)
        xhat = xv * inv
        err = xhat * fwv - t_ref[...]
        col = jnp.sum(err * err, axis=0, keepdims=True)
        red_ref[1:2, :] += jnp.broadcast_to(jnp.sum(col, axis=1, keepdims=True) * (0.5 / D_MODEL), (1, D_MODEL))
        dy = err * (1.0 / D_MODEL)
        red_ref[0:1, :] += jnp.sum(dy * xhat, axis=0, keepdims=True)
        dxhat = dy * fwv
        dx_ref[...] = inv * (dxhat - xhat * jnp.mean(dxhat * xhat, axis=-1, keepdims=True))

    row = pl.BlockSpec((tm, D_MODEL), lambda i: (i, 0))
    return pl.pallas_call(
        body, name="loss_head", grid=(S // tm,),
        in_specs=[row, _vec(D_MODEL), row],
        out_specs=[row, _full((8, D_MODEL))],
        out_shape=[SDS((S, D_MODEL), F32), SDS((8, D_MODEL), F32)],
        compiler_params=_cp(("arbitrary",)),
    )(x, fw, target)


ADA_COLS = 3 * D_MODEL // N_DEV


def _ada_fwd(c_all, w_ada, b_cols):
    def body(c_ref, w_ref, b_ref, out_ref):
        out_ref[...] = _mm(_bf(_silu(c_ref[...])), _bf(w_ref[...])) + b_ref[...]

    return pl.pallas_call(
        body, name="ada_fwd", grid=(DEPTH,),
        in_specs=[_full((N_DEV, D_MODEL)), pl.BlockSpec((None, D_MODEL, ADA_COLS), lambda l: (l, 0, 0)),
                  pl.BlockSpec((None, 1, ADA_COLS), lambda l: (l, 0, 0))],
        out_specs=pl.BlockSpec((None, N_DEV, ADA_COLS), lambda l: (l, 0, 0)),
        out_shape=SDS((DEPTH, N_DEV, ADA_COLS), F32),
        compiler_params=_cp(("parallel",)),
    )(c_all, w_ada, b_cols)


def _ada_bwd(ct_pad, dmod_pad):
    def body(c_ref, d_ref, out_ref):
        out_ref[...] = _mm(_bf(_silu(c_ref[...])), _bf(d_ref[...]))

    return pl.pallas_call(
        body, name="ada_bwd", grid=(DEPTH,),
        in_specs=[_full((D_MODEL, LANE)), pl.BlockSpec((None, LANE, ADA_COLS), lambda l: (l, 0, 0))],
        out_specs=pl.BlockSpec((None, D_MODEL, ADA_COLS), lambda l: (l, 0, 0)),
        out_shape=SDS((DEPTH, D_MODEL, ADA_COLS), F32),
        compiler_params=_cp(("parallel",)),
    )(ct_pad, dmod_pad)


def _adamw(parts, w, m, v, name, own=None, layers=None, prev=None):
    n, L, R, C = parts.shape
    lo, hi = layers or (0, L)
    tr = R
    while tr * C * 4 > (1 << 20) and tr % 16 == 0:
        tr //= 2
    first = 1 if own is None else 2

    def body(*refs):
        p_ref = refs[0]
        w_ref, m_ref, v_ref = refs[first:first + 3]
        g_ref, d_ref, mo_ref, vo_ref = refs[-4:]

        def part(k):
            if own is None:
                return p_ref[k].astype(F32)
            me = 4 * lax.axis_index("x") + 2 * lax.axis_index("y") + lax.axis_index("c")
            return jnp.where(me == k, refs[1][...], p_ref[k]).astype(F32)

        g = part(0)
        for k in range(1, n):
            g = g + part(k)
        mn = ADAM_B1 * m_ref[...] + (1.0 - ADAM_B1) * g
        vn = ADAM_B2 * v_ref[...] + (1.0 - ADAM_B2) * (g * g)
        m_hat = mn / (1.0 - ADAM_B1 ** ADAM_STEP)
        v_hat = vn / (1.0 - ADAM_B2 ** ADAM_STEP)
        g_ref[...] = g
        d_ref[...] = -ADAM_LR * (m_hat / (jnp.sqrt(v_hat) + ADAM_EPS) + ADAM_WD * w_ref[...])
        mo_ref[...] = mn
        vo_ref[...] = vn

    blk = pl.BlockSpec((None, tr, C), lambda l, i: (lo + l, i, 0))
    own_blk = [] if own is None else [pl.BlockSpec((None, tr, C), lambda l, i: (l, i, 0))]
    n_blk = 3 if own is None else 4
    return pl.pallas_call(
        body, name=name, grid=(hi - lo, R // tr),
        in_specs=[pl.BlockSpec((n, None, tr, C), lambda l, i: (0, lo + l, i, 0))] + own_blk + [blk] * 3
        + ([] if prev is None else [ANY] * 4),
        out_specs=[blk] * 4,
        out_shape=[SDS((L, R, C), F32)] * 4,
        input_output_aliases={} if prev is None else {1 + n_blk + k: k for k in range(4)},
        compiler_params=_cp(("parallel", "parallel")),
    )(parts, *([] if own is None else [own]), w, m, v, *([] if prev is None else prev))


MESH = pl.DeviceIdType.MESH
ANY = pl.BlockSpec(memory_space=pl.ANY)


def _all_gather(v, name):
    def body(v_ref, out_ref, send_sems, recv_sems, local_sem):
        x, y, c = lax.axis_index("x"), lax.axis_index("y"), lax.axis_index("c")
        me, sibling = (x, y, c), (x, y, 1 - c)
        chips = [(1 - x, y), (x, 1 - y), (1 - x, 1 - y)]

        def slot(px, py, pc):
            return out_ref.at[4 * px + 2 * py + pc]

        def copy(k, block, to, src=None):
            return pltpu.make_async_remote_copy(
                src_ref=slot(*block) if src is None else src, dst_ref=slot(*block),
                send_sem=send_sems.at[k], recv_sem=recv_sems.at[k], device_id=to, device_id_type=MESH)

        mine = pltpu.make_async_copy(v_ref, slot(*me), local_sem)
        mine.start()
        first = [copy(0, me, sibling, src=v_ref)]
        first += [copy(1 + j, me, (*chip, c), src=v_ref) for j, chip in enumerate(chips)]
        for cp in first:
            cp.start()
        passed = [copy(4 + j, (*chip, c), sibling) for j, chip in enumerate(chips)]
        for j, chip in enumerate(chips):
            copy(1 + j, (*chip, c), me).wait_recv()
            passed[j].start()
        copy(0, sibling, me).wait_recv()
        for j, chip in enumerate(chips):
            copy(4 + j, (*chip, 1 - c), me).wait_recv()
        for cp in first + passed:
            cp.wait_send()
        mine.wait()

    return pl.pallas_call(
        body, name=name, in_specs=[ANY], out_specs=ANY,
        out_shape=SDS((N_DEV,) + v.shape, v.dtype),
        scratch_shapes=[pltpu.SemaphoreType.DMA((7,)), pltpu.SemaphoreType.DMA((7,)), pltpu.SemaphoreType.DMA],
    )(v)


def _all_to_all(v, name):
    def body(v_ref, out_ref, send_sems, recv_sems, local_sem):
        x, y, c = lax.axis_index("x"), lax.axis_index("y"), lax.axis_index("c")
        mine_idx = 4 * x + 2 * y + c
        mine = pltpu.make_async_copy(v_ref.at[mine_idx], out_ref.at[mine_idx], local_sem)
        mine.start()
        sends, recvs = [], []
        for k in range(1, N_DEV):
            px = 1 - x if k & 4 else x
            py = 1 - y if k & 2 else y
            pc = 1 - c if k & 1 else c
            peer_idx = 4 * px + 2 * py + pc
            sems = dict(send_sem=send_sems.at[k - 1], recv_sem=recv_sems.at[k - 1], device_id=(px, py, pc),
                        device_id_type=MESH)
            sends.append(pltpu.make_async_remote_copy(src_ref=v_ref.at[peer_idx], dst_ref=out_ref.at[mine_idx], **sems))
            recvs.append(pltpu.make_async_remote_copy(src_ref=v_ref.at[peer_idx], dst_ref=out_ref.at[peer_idx], **sems))
        for cp in sends:
            cp.start()
        for cp in recvs:
            cp.wait_recv()
        for cp in sends:
            cp.wait_send()
        mine.wait()

    return pl.pallas_call(
        body, name=name, in_specs=[ANY], out_specs=ANY,
        out_shape=SDS(v.shape, v.dtype),
        scratch_shapes=[pltpu.SemaphoreType.DMA((7,)), pltpu.SemaphoreType.DMA((7,)), pltpu.SemaphoreType.DMA],
    )(v)


HBM_SPEC = pl.BlockSpec(memory_space=pltpu.HBM)
SEM_SPEC = pl.BlockSpec(memory_space=pltpu.SEMAPHORE)
EFFECT = pltpu.SideEffectType.DATAFLOW_SIDE_EFFECTING


EXCHANGE_PEERS = {"gather": range(1, N_DEV), "scatter": range(1, N_DEV), "chip": (1, 2, 4, 6), "pass": (2, 4, 6)}


def _exchange_copies(srcs, lands, send_sems, recv_sems, mode, layer):
    x, y, c = lax.axis_index("x"), lax.axis_index("y"), lax.axis_index("c")
    me = 4 * x + 2 * y + c
    copies = []
    for a, (src, land) in enumerate(zip(srcs, lands)):
        for k in EXCHANGE_PEERS[mode]:
            px = 1 - x if k & 4 else x
            py = 1 - y if k & 2 else y
            pc = 1 - c if k & 1 else c
            peer = 4 * px + 2 * py + pc
            if mode == "scatter":
                s, d, to = src.at[peer], land.at[me, layer], (px, py, pc)
            elif mode == "pass":
                s, d, to = land.at[peer], land.at[peer], (x, y, 1 - c)
            else:
                s, d, to = src, land.at[me], (px, py, pc)
            n = 7 * a + k - 1
            copies.append(pltpu.make_async_remote_copy(
                src_ref=s, dst_ref=d, send_sem=send_sems.at[n], recv_sem=recv_sems.at[n], device_id=to,
                device_id_type=MESH))
    return copies


def _exchange_start(name, srcs, lands, mode, layer=0, after=None):
    n = len(srcs)

    def body(*refs):
        send_sems, recv_sems = refs[-2 * n - 3], refs[-2 * n - 2]
        for cp in _exchange_copies(refs[:n], refs[n:2 * n], send_sems, recv_sems, mode, layer):
            cp.start()
        refs[-1][...] = jnp.zeros_like(refs[-1])

    arrays = list(srcs) + list(lands)
    sems = pltpu.SemaphoreType.DMA((7 * n,))
    out = pl.pallas_call(
        body, name=name,
        out_shape=(sems, sems, *[pltpu.HBM(v.shape, v.dtype) for v in arrays], SDS((8, LANE), F32)),
        in_specs=[HBM_SPEC] * (2 * n) + ([ANY] if after is not None else []),
        out_specs=(SEM_SPEC, SEM_SPEC, *[HBM_SPEC] * (2 * n), pl.BlockSpec(memory_space=pltpu.VMEM)),
        input_output_aliases={i: 2 + i for i in range(2 * n)},
        compiler_params=pltpu.CompilerParams(has_side_effects=EFFECT),
    )(*[pltpu.with_memory_space_constraint(v, pltpu.HBM) for v in arrays], *([after] if after is not None else []))
    return dict(sems=out[:2], srcs=out[2:2 + n], lands=out[2 + n:2 + 2 * n], token=out[-1][0, 0], mode=mode,
                layer=layer)


def _exchange_wait(name, st, after, also=()):
    n = len(st["srcs"])

    def body(*refs):
        send_sems, recv_sems = refs[2 * n], refs[2 * n + 1]
        for cp in _exchange_copies(refs[:n], refs[n:2 * n], send_sems, recv_sems, st["mode"], st["layer"]):
            cp.wait_send()
            cp.wait_recv()

    arrays = list(st["srcs"]) + list(st["lands"])
    out = pl.pallas_call(
        body, name=name,
        out_shape=tuple(pltpu.HBM(v.shape, v.dtype) for v in arrays),
        in_specs=[HBM_SPEC] * (2 * n) + [SEM_SPEC, SEM_SPEC] + [ANY] * (1 + len(also)),
        out_specs=tuple([HBM_SPEC] * (2 * n)),
        input_output_aliases={i: i for i in range(2 * n)},
        compiler_params=pltpu.CompilerParams(has_side_effects=EFFECT),
    )(*arrays, *st["sems"], after, *also)
    st["srcs"] = out[:n]
    return out[n:]


_IN_PIECES = ([(1024, 3072)]
              + [r for t in range(4) for r in ((LANE * t, LANE * (t + 1)), (512 + LANE * t, 512 + LANE * (t + 1)))]
              + [(4096, 5632), (3072, 4096), (5632, 5648)])


def _permute_in(w):
    pad = jnp.zeros(w.shape[:-1] + (N_PAD - N_IN,), w.dtype)
    return jnp.concatenate([w[..., a:b] for a, b in _IN_PIECES] + [pad], axis=-1)


def _unpermute_in(g):
    ax = [g[..., OFF_LRU + 2 * LANE * t:OFF_LRU + 2 * LANE * t + LANE] for t in range(4)]
    ag = [g[..., OFF_LRU + 2 * LANE * t + LANE:OFF_LRU + 2 * LANE * (t + 1)] for t in range(4)]
    return jnp.concatenate(ax + ag + [g[..., 0:2048], g[..., OFF_Z:OFF_Z + SSD_W], g[..., OFF_XBC:OFF_XBC + SSD_CONV],
                                      g[..., OFF_Z + SSD_W:OFF_Z + SSD_W + SSD_HEADS]], axis=-1)


SHARD_COLS = N_IN // N_DEV


def _in_segments():
    segs, pos = [], 0
    for a, b in _IN_PIECES:
        for i in range(N_DEV):
            lo, hi = max(a, SHARD_COLS * i), min(b, SHARD_COLS * (i + 1))
            if lo < hi:
                segs.append((i, lo - SHARD_COLS * i, hi - lo, pos + lo - a))
        pos += b - a
    return segs


RELAYOUT_ROWS = 256


def _relayout_in(land, own):
    def body(land_ref, own_ref, out_ref):
        me = 4 * lax.axis_index("x") + 2 * lax.axis_index("y") + lax.axis_index("c")
        out_ref[:, N_IN:N_PAD] = jnp.zeros((RELAYOUT_ROWS, N_PAD - N_IN), BF16)
        for i, j, wd, p in _in_segments():
            out_ref[:, p:p + wd] = jnp.where(me == i, own_ref[:, j:j + wd], land_ref[i, :, j:j + wd])

    return pl.pallas_call(
        body, name="relayout_in", grid=(D_MODEL // RELAYOUT_ROWS,),
        in_specs=[pl.BlockSpec((N_DEV, RELAYOUT_ROWS, SHARD_COLS), lambda r: (0, r, 0)),
                  pl.BlockSpec((RELAYOUT_ROWS, SHARD_COLS), lambda r: (r, 0))],
        out_specs=pl.BlockSpec((RELAYOUT_ROWS, N_PAD), lambda r: (r, 0)),
        out_shape=SDS((D_MODEL, N_PAD), BF16),
        compiler_params=_cp(("parallel",)),
    )(land, own)


def _relayout_grad(g):
    def body(g_ref, out_ref):
        for i, j, wd, p in _in_segments():
            out_ref[i, :, j:j + wd] = g_ref[:, p:p + wd].astype(BF16)

    return pl.pallas_call(
        body, name="relayout_grad", grid=(D_MODEL // RELAYOUT_ROWS,),
        in_specs=[pl.BlockSpec((RELAYOUT_ROWS, N_PAD), lambda r: (r, 0))],
        out_specs=pl.BlockSpec((N_DEV, RELAYOUT_ROWS, SHARD_COLS), lambda r: (0, r, 0)),
        out_shape=SDS((N_DEV, D_MODEL, SHARD_COLS), BF16),
        compiler_params=_cp(("parallel",)),
    )(g)


def _block_diag(w):
    w4 = w.reshape(4, 2, 64, 64)
    z = jnp.zeros((4, 64, 64), w.dtype)
    top = jnp.concatenate([w4[:, 0], z], axis=-1)
    bot = jnp.concatenate([z, w4[:, 1]], axis=-1)
    return jnp.concatenate([top, bot], axis=1).astype(BF16)


def _diag_blocks(g):
    return jnp.stack([g[:, :64, :64], g[:, 64:, 64:]], axis=1).reshape(8, 64, 64)


def _pad_lanes(v):
    return jnp.pad(v, (0, LANE - v.shape[0]))[None, :]


def _lower_bounds(logits):
    p = jax.nn.softmax(logits, axis=0)
    return p, jnp.cumsum(p, axis=0) - p[0]


def _lower_bounds_bwd(p, dlb):
    dp = jnp.cumsum(dlb[::-1], axis=0)[::-1]
    dp = dp.at[0].add(-jnp.sum(dlb, axis=0))
    return p * (dp - jnp.sum(dp * p, axis=0, keepdims=True))


SMALL = ["norm_w", "b_ada", "lru_conv_b", "lru_wa", "lru_ba", "lru_wx", "lru_bx", "lru_lambda", "hg_lb_logits",
         "hg_norm_w", "ssd_conv_b", "ssd_dt_bias", "ssd_a_log", "ssd_d", "ssd_norm_w", "final_norm_w"]
WEIGHTS = ["norm_w", "w_ada", "b_ada", "w_in", "lru_conv_w", "lru_conv_b", "lru_wa", "lru_ba", "lru_wx", "lru_bx",
           "lru_lambda", "hg_lb_logits", "hg_norm_w", "ssd_conv_w", "ssd_conv_b", "ssd_dt_bias", "ssd_a_log", "ssd_d",
           "ssd_norm_w", "w_out", "final_norm_w"]
INPUTS = ["x", "c"] + WEIGHTS + ["loss_target"] + ["m_" + n for n in WEIGHTS] + ["v_" + n for n in WEIGHTS]
SMALL_ROW = 1024


def _small_rows(like):
    out, off = {}, 0
    for n in SMALL:
        rows = -(-int(np.prod(like[n].shape)) // (8 * SMALL_ROW)) * 8
        out[n] = (off, rows)
        off += rows
    return out, off


def _flatten_small(d, prefix="", last=0.0):
    table, _ = _small_rows({n: d[prefix + n] for n in SMALL})
    pieces = []
    for n in SMALL:
        flat = d[prefix + n].reshape(-1)
        pieces.append(jnp.pad(flat, (0, table[n][1] * SMALL_ROW - flat.shape[0])).reshape(-1, SMALL_ROW))
    return jnp.concatenate(pieces + [jnp.full((8, SMALL_ROW), last, F32)], axis=0)


def _split_small(packed, like):
    table, _ = _small_rows(like)
    out = {}
    for n in SMALL:
        off, rows = table[n]
        size = int(np.prod(like[n].shape))
        out[n] = packed[off:off + rows].reshape(-1)[:size].reshape(like[n].shape)
    return out


def _local_step(x, mod, target, w, fetch, emit):
    S = x.shape[0]
    mall = _bfc(_hg_consts())
    mall_t = _bfc(_hg_consts().T)
    consts = _ssd_consts()
    p_lb, lbs = _lower_bounds(w["hg_lb_logits"])
    saved = []
    for l in range(DEPTH):
        w_in_l, w_out_l, token = fetch(l, x)
        shift, scale, gate = (mod[l:l + 1, k * D_MODEL:(k + 1) * D_MODEL] for k in range(3))
        shift = shift + token
        prm = dict(
            nw=w["norm_w"][l:l + 1], cw=w["lru_conv_w"][l], cb=w["lru_conv_b"][l:l + 1],
            wa=_block_diag(w["lru_wa"][l]), ba=w["lru_ba"][l].reshape(1, LRU_W),
            wx=_block_diag(w["lru_wx"][l]), bx=w["lru_bx"][l].reshape(1, LRU_W), lam=w["lru_lambda"][l:l + 1],
            lb=lbs[l:l + 1], hnw=w["hg_norm_w"][l:l + 1], scw=w["ssd_conv_w"][l], scb=w["ssd_conv_b"][l:l + 1],
            bias=_pad_lanes(w["ssd_dt_bias"][l]), alog=_pad_lanes(w["ssd_a_log"][l]),
            dskip=jnp.repeat(w["ssd_d"][l], SSD_P)[None, :], snw=w["ssd_norm_w"][l:l + 1],
            w_in=w_in_l, w_out=w_out_l, scale=scale, gate=gate)
        u, h = _inproj_fwd(x, prm["nw"], scale, shift, prm["w_in"])
        ycat = lax.empty((S, D_INNER), BF16)
        lru_args = (u, prm["cw"], prm["cb"], prm["wa"], prm["ba"], prm["wx"], prm["bx"], prm["lam"])
        ycat, h_lru = _lru_fwd(*lru_args, ycat)
        ycat, o_b, hg_st = _hg_fwd(u, prm["lb"], prm["hnw"], mall, ycat)
        xbc = _ssdconv_fwd(u, prm["scw"], prm["scb"])
        ssd_args = (u, xbc, prm["bias"], prm["alog"], prm["dskip"], prm["snw"], consts)
        ycat, y_ssd, ssd_st = _ssd_fwd(*ssd_args, ycat)
        x_new, y = _outproj_fwd(ycat, prm["w_out"], x, gate)
        saved.append((prm, x, u, h, ycat, lru_args, h_lru, o_b, hg_st, ssd_args, y_ssd, ssd_st, y))
        x = x_new
    dx, red = _loss_head(x, w["final_norm_w"][None, :], target)
    loss = red[1, 0]
    g = {n: [None] * DEPTH for n in WEIGHTS}
    g["final_norm_w"] = red[0]
    dmod, dlb = [None] * DEPTH, [None] * DEPTH
    for l in reversed(range(DEPTH)):
        prm, x, u, h, ycat, lru_args, h_lru, o_b, hg_st, ssd_args, y_ssd, ssd_st, y = saved[l]
        dycat, g_out, dgate = _outproj_bwd(dx, y, prm["gate"], ycat, prm["w_out"])
        token = emit(l, "w_out", g_out)
        du = lax.empty((S, N_PAD), BF16)
        ssd_args = ssd_args[:5] + (ssd_args[5] + token,) + ssd_args[6:]
        du, dxbc, sred = _ssd_bwd(*ssd_args, y_ssd, ssd_st, dycat, du)
        du, cred = _ssdconv_bwd(u, prm["scw"], prm["scb"], dxbc, du)
        du, hred = _hg_bwd(u, prm["lb"], prm["hnw"], mall, mall_t, o_b, hg_st, dycat, du)
        du, lred, gwa, gwx = _lru_bwd(*lru_args, h_lru, dycat, du)
        token = emit(l, "w_in", _inproj_bwd_w(h, du))
        dx, ired = _inproj_bwd_x(du, prm["w_in"], x, prm["nw"], prm["scale"] + token, dx)
        g["norm_w"][l] = ired[2]
        dmod[l] = jnp.concatenate([ired[0], ired[1], dgate[0]])
        g["lru_conv_w"][l], g["lru_conv_b"][l] = lred[0:4], lred[4]
        g["lru_ba"][l], g["lru_bx"][l], g["lru_lambda"][l] = lred[5].reshape(8, 64), lred[6].reshape(8, 64), lred[7]
        g["lru_wa"][l], g["lru_wx"][l] = _diag_blocks(gwa), _diag_blocks(gwx)
        g["hg_norm_w"][l], dlb[l] = hred[0], hred[1]
        g["ssd_conv_w"][l], g["ssd_conv_b"][l] = cred[0:4], cred[4]
        g["ssd_norm_w"][l] = sred[0]
        g["ssd_d"][l] = sred[1].reshape(SSD_HEADS, SSD_P).sum(-1)
        g["ssd_dt_bias"][l] = sred[2, :SSD_HEADS]
        g["ssd_a_log"][l] = -sred[3, :SSD_HEADS] * jnp.exp(w["ssd_a_log"][l])
    g["hg_lb_logits"] = _lower_bounds_bwd(p_lb, jnp.stack(dlb))
    for n in WEIGHTS:
        if isinstance(g[n], list) and g[n][0] is not None:
            g[n] = jnp.stack(g[n])
    return loss, dx, jnp.stack(dmod), g


def kernel(x, c, norm_w, w_ada, b_ada, w_in, lru_conv_w, lru_conv_b, lru_wa, lru_ba, lru_wx, lru_bx, lru_lambda, hg_lb_logits, hg_norm_w, ssd_conv_w, ssd_conv_b, ssd_dt_bias, ssd_a_log, ssd_d, ssd_norm_w, w_out, final_norm_w, loss_target, m_norm_w, m_w_ada, m_b_ada, m_w_in, m_lru_conv_w, m_lru_conv_b, m_lru_wa, m_lru_ba, m_lru_wx, m_lru_bx, m_lru_lambda, m_hg_lb_logits, m_hg_norm_w, m_ssd_conv_w, m_ssd_conv_b, m_ssd_dt_bias, m_ssd_a_log, m_ssd_d, m_ssd_norm_w, m_w_out, m_final_norm_w, v_norm_w, v_w_ada, v_b_ada, v_w_in, v_lru_conv_w, v_lru_conv_b, v_lru_wa, v_lru_ba, v_lru_wx, v_lru_bx, v_lru_lambda, v_hg_lb_logits, v_hg_norm_w, v_ssd_conv_w, v_ssd_conv_b, v_ssd_dt_bias, v_ssd_a_log, v_ssd_d, v_ssd_norm_w, v_w_out, v_final_norm_w):
    return _step(x, c, norm_w, w_ada, b_ada, w_in, lru_conv_w, lru_conv_b, lru_wa, lru_ba, lru_wx, lru_bx, lru_lambda, hg_lb_logits, hg_norm_w, ssd_conv_w, ssd_conv_b, ssd_dt_bias, ssd_a_log, ssd_d, ssd_norm_w, w_out, final_norm_w, loss_target, m_norm_w, m_w_ada, m_b_ada, m_w_in, m_lru_conv_w, m_lru_conv_b, m_lru_wa, m_lru_ba, m_lru_wx, m_lru_bx, m_lru_lambda, m_hg_lb_logits, m_hg_norm_w, m_ssd_conv_w, m_ssd_conv_b, m_ssd_dt_bias, m_ssd_a_log, m_ssd_d, m_ssd_norm_w, m_w_out, m_final_norm_w, v_norm_w, v_w_ada, v_b_ada, v_w_in, v_lru_conv_w, v_lru_conv_b, v_lru_wa, v_lru_ba, v_lru_wx, v_lru_bx, v_lru_lambda, v_hg_lb_logits, v_hg_norm_w, v_ssd_conv_w, v_ssd_conv_b, v_ssd_dt_bias, v_ssd_a_log, v_ssd_d, v_ssd_norm_w, v_w_out, v_final_norm_w)


def _step(*args):
    a = dict(zip(INPUTS, args, strict=True))
    me = 4 * lax.axis_index("x") + 2 * lax.axis_index("y") + lax.axis_index("c")
    x, target = a["x"][0], a["loss_target"][0]

    c_all = _all_gather(a["c"], "gather_c")[:, 0, :]
    b_cols = lax.dynamic_slice_in_dim(a["b_ada"], me * ADA_COLS, ADA_COLS, axis=1)[:, None, :]
    mod_parts = _all_gather(_ada_fwd(c_all, a["w_ada"], b_cols), "gather_mod")
    mod = lax.dynamic_index_in_dim(mod_parts, me, axis=2, keepdims=False)
    mod = mod.transpose(1, 0, 2).reshape(DEPTH, 3 * D_MODEL)

    w = {n: a[n] for n in SMALL}

    w_in_b = [a["w_in"][l].astype(BF16) for l in range(DEPTH)]
    w_out_b = a["w_out"].astype(BF16)
    conv_own = jnp.concatenate([a["lru_conv_w"], a["ssd_conv_w"]], axis=-1)
    cols, rows_out = N_IN // N_DEV, D_INNER // N_DEV

    def gather_start(l, after):
        srcs = [w_in_b[l], w_out_b[l]] + ([conv_own] if l == 0 else [])
        lands = [lax.empty((N_DEV,) + s.shape, s.dtype) for s in srcs]
        return _exchange_start(f"gather_start_{l}", srcs, lands, "chip" if l == 0 else "gather", after=after)

    def gather_pass(name, st, after, also=()):
        landed = _exchange_wait(name + "_wait", st, after, also)
        st2 = _exchange_start(name + "_pass", st["srcs"], landed, "pass")
        return _exchange_wait(name + "_passed", st2, after)

    gathers = {0: gather_start(0, mod)}

    def fetch(l, x_l):
        if l == 0:
            landed = gather_pass("gather_0", gathers[0], x_l, also=(a["w_in"], a["m_w_in"], a["v_w_in"]))
        else:
            landed = _exchange_wait(f"gather_wait_{l}", gathers[l], x_l)
        land_out = lax.dynamic_update_index_in_dim(landed[1], w_out_b[l], me, 0)
        if l == 0:
            conv = lax.dynamic_update_index_in_dim(landed[2], conv_own, me, 0).transpose(1, 2, 0, 3)
            w["lru_conv_w"] = conv[..., :64].reshape(DEPTH, 4, LRU_W)
            w["ssd_conv_w"] = conv[..., 64:].reshape(DEPTH, 4, SSD_CONV)
        token = 0.0
        if l + 1 < DEPTH:
            gathers[l + 1] = gather_start(l + 1, land_out)
            token = gathers[l + 1]["token"]
        return _relayout_in(landed[0], w_in_b[l]), land_out.reshape(D_INNER, D_MODEL), token

    scatters = {"w_in": {}, "w_out": {}}
    lands = {"w_in": lax.empty((N_DEV, DEPTH, D_MODEL, cols), BF16),
             "w_out": lax.empty((N_DEV, DEPTH, rows_out, D_MODEL), BF16)}
    own = {"w_in": [None] * DEPTH, "w_out": [None] * DEPTH}

    def emit(l, name, grad):
        grad = _relayout_grad(grad) if name == "w_in" else grad.reshape(N_DEV, rows_out, D_MODEL)
        st = _exchange_start(f"scatter_start_{name}_{l}", [grad], [lands[name]], "scatter", layer=l)
        scatters[name][l] = st
        lands[name] = st["lands"][0]
        return st["token"]

    loss_own, dx, dmod, g = _local_step(x, mod, target, w, fetch, emit)

    def sharded(name, parts, own=None, **kw):
        return _adamw(parts, a[name], a["m_" + name], a["v_" + name], "adamw_" + name + kw.pop("tag", ""), own=own, **kw)

    g["b_ada"] = dmod
    small_own = _flatten_small(g, last=loss_own)
    small_st = _exchange_start("gather_small", [small_own], [lax.empty((N_DEV,) + small_own.shape, F32)], "chip",
                               after=dx)
    big = {}
    after = small_st["token"] + dx[0:8, 0:LANE]
    def own_slice(st):
        return lax.dynamic_index_in_dim(st["srcs"][0], me, 0, keepdims=False)

    for name in ("w_out", "w_in"):
        for l in reversed(range(1, DEPTH)):
            scatters[name][l]["lands"] = [lands[name]]
            lands[name] = _exchange_wait(f"scatter_wait_{name}_{l}", scatters[name][l], after)[0]
            own[name][l] = own_slice(scatters[name][l])
        upper = sharded(name, lands[name], jnp.stack(own[name][1:]), layers=(1, DEPTH), tag="_upper")
        scatters[name][0]["lands"] = [lands[name]]
        lands[name] = _exchange_wait(f"scatter_wait_{name}_0", scatters[name][0], upper[1])[0]
        big[name] = sharded(name, lands[name], own_slice(scatters[name][0])[None], layers=(0, 1), prev=upper)
        after = big[name][1]
    small = gather_pass("gather_small", small_st, after)[0]
    outs = _adamw(small[:, None], *[_flatten_small(a, p)[None] for p in ("", "m_", "v_")], "adamw_small",
                  own=small_own[None])
    res = [_split_small(o[0], a) for o in outs]
    losses = lax.dynamic_update_index_in_dim(small[:, -1, 0], loss_own, me, 0)
    loss = jnp.sum(losses)

    off = _small_rows(a)[0]["b_ada"][0]
    dmod_all = lax.dynamic_update_index_in_dim(small[:, off:off + DEPTH * 3 * D_MODEL // SMALL_ROW],
                                               dmod.reshape(-1, SMALL_ROW), me, 0)
    dmod_all = dmod_all.reshape(N_DEV, DEPTH, 3 * D_MODEL).transpose(1, 0, 2)
    dmod_cols = lax.dynamic_slice_in_dim(dmod_all, me * ADA_COLS, ADA_COLS, axis=2)
    dmod_pad = jnp.pad(dmod_cols, ((0, 0), (0, LANE - N_DEV), (0, 0)))
    ct_pad = jnp.pad(c_all.T, ((0, 0), (0, LANE - N_DEV)))
    big["w_ada"] = sharded("w_ada", _ada_bwd(ct_pad, dmod_pad)[None])
    g_conv = jnp.concatenate([g["lru_conv_w"].reshape(DEPTH, 4, N_DEV, 64), g["ssd_conv_w"].reshape(DEPTH, 4, N_DEV, 192)],
                             axis=-1).transpose(2, 0, 1, 3)
    conv_parts = _all_to_all(g_conv, "scatter_conv")
    big["lru_conv_w"] = sharded("lru_conv_w", conv_parts[..., :64])
    big["ssd_conv_w"] = sharded("ssd_conv_w", conv_parts[..., 64:])

    out = [loss, dx[None]]
    for k in range(4):
        out += [big[n][k] if n in big else res[k][n] for n in WEIGHTS]
    return tuple(out)
```

```python
import functools

import numpy as np
import jax
import jax.numpy as jnp
from jax import lax
from jax.experimental import pallas as pl
from jax.experimental.pallas import tpu as pltpu

F32 = jnp.float32
BF16 = jnp.bfloat16
SDS = jax.ShapeDtypeStruct

N_DEV = 8
DEPTH = 4
D_MODEL = 1024
D_INNER = 2048
EPS = 1e-6
LRU_W = 512
LRU_C = 8.0
HG_W = 512
HG_CHUNK = 64
HG_HEADS = 4
SSD_W = 1024
SSD_HEADS = 16
SSD_P = 64
SSD_N = 128
SSD_CHUNK = 128
SSD_CONV = 1536
N_IN = 5648
N_PAD = 5760
OFF_HG = 0
OFF_LRU = 2048
OFF_XBC = 3072
OFF_Z = 4608
LANE = 128
VMEM_LIMIT = 56 * 1024 * 1024
NEG = -1e30

ADAM_LR = 0.001
ADAM_B1 = 0.9
ADAM_B2 = 0.999
ADAM_EPS = 1e-08
ADAM_WD = 0.01
ADAM_STEP = 10


def _cp(sem=None):
    return pltpu.CompilerParams(dimension_semantics=sem, vmem_limit_bytes=VMEM_LIMIT)


def _dg(a, b, ca, cb):
    return lax.dot_general(a, b, (((ca,), (cb,)), ((), ())), preferred_element_type=F32)


def _mm(a, b):
    return _dg(a, b, 1, 0)


def _mm_nt(a, b):
    return _dg(a, b, 1, 1)


def _mm_tn(a, b):
    return _dg(a, b, 0, 0)


def _bf(x):
    return x.astype(BF16)


def _f(x):
    return x.astype(F32)


def _split3(x):
    hi = x.astype(BF16)
    r = x - hi.astype(F32)
    mid = r.astype(BF16)
    lo = (r - mid.astype(F32)).astype(BF16)
    return hi, mid, lo


def _sel_r(x, m):
    hi, mid, lo = _split3(x)
    return _mm(hi, m) + _mm(mid, m) + _mm(lo, m)


def _sel_l(m, x):
    hi, mid, lo = _split3(x)
    return _mm(m, hi) + _mm(m, mid) + _mm(m, lo)


def _sel_l2(m, x):
    hi = x.astype(BF16)
    lo = (x - hi.astype(F32)).astype(BF16)
    return _mm(m, hi) + _mm(m, lo)


def _sel_tn(x, m):
    hi, mid, lo = _split3(x)
    return _mm_tn(hi, m) + _mm_tn(mid, m) + _mm_tn(lo, m)


def _sigmoid(x):
    return 1.0 / (1.0 + jnp.exp(-x))


def _silu(x):
    return x * _sigmoid(x)


def _dsilu(x):
    s = _sigmoid(x)
    return s * (1.0 + x * (1.0 - s))


def _softplus(x):
    return jnp.maximum(x, 0.0) + jnp.log(1.0 + jnp.exp(-jnp.abs(x)))


def _expm1(z):
    series = z * (1.0 + z * (1.0 / 2) * (1.0 + z * (1.0 / 3) * (1.0 + z * (1.0 / 4) * (
        1.0 + z * (1.0 / 5) * (1.0 + z * (1.0 / 6) * (1.0 + z * (1.0 / 7)))))))
    return jnp.where(jnp.abs(z) < 0.3, series, jnp.exp(z) - 1.0)


def _iota(shape, dim):
    return lax.broadcasted_iota(jnp.int32, shape, dim)


def _last_row(x, rows):
    return jnp.sum(jnp.where(rows == x.shape[0] - 1, x, 0.0), axis=0, keepdims=True)


def _shift_down(x, d, rows, fill=0.0):
    return jnp.where(rows >= d, pltpu.roll(x, d, 0), fill)


def _shift_up(x, d, rows, fill=0.0):
    n = x.shape[0]
    return jnp.where(rows < n - d, pltpu.roll(x, n - d, 0), fill)


def _conv_fwd(x, cw_ref, cb_ref, rows):
    out = cb_ref[...] + cw_ref[pl.ds(3, 1), :] * x
    for k in range(3):
        out = out + cw_ref[pl.ds(k, 1), :] * _shift_down(x, 3 - k, rows)
    return out


def _conv_bwd(x, dco, cw_ref, rows):
    dx = cw_ref[pl.ds(3, 1), :] * dco
    dws = []
    for k in range(3):
        dx = dx + cw_ref[pl.ds(k, 1), :] * _shift_up(dco, 3 - k, rows)
        dws.append(jnp.sum(dco * _shift_down(x, 3 - k, rows), axis=0, keepdims=True))
    dws.append(jnp.sum(dco * x, axis=0, keepdims=True))
    return dx, dws, jnp.sum(dco, axis=0, keepdims=True)


def _vec(n):
    return pl.BlockSpec((1, n), lambda *_: (0, 0))


class _Row:
    def __init__(self, arr, l, n=None, c=0):
        self.arr, self.l, self.n, self.c = arr[:, None, :], l, n or arr.shape[1], c


def _spec(v):
    if isinstance(v, _Row):
        return pl.BlockSpec((None, 1, v.n), lambda *_: (v.l, 0, v.c))
    return _vec(v.shape[1])


def _arr(v):
    return v.arr if isinstance(v, _Row) else v


def _full(shape):
    nd = len(shape)
    return pl.BlockSpec(shape, lambda *_: (0,) * nd)


def _inproj_fwd(x, nw, scale, shift, w, tok):
    S = x.shape[0]
    tm = min(256, S)

    def body(x_ref, nw_ref, sc_ref, sh_ref, w_ref, tok_ref, u_ref, h_ref):
        del tok_ref
        xv = x_ref[...]
        inv = lax.rsqrt(jnp.mean(xv * xv, axis=-1, keepdims=True) + EPS)
        h = ((xv * inv) * nw_ref[...] * (1.0 + sc_ref[...]) + sh_ref[...]).astype(BF16)
        h_ref[...] = h
        u_ref[...] = _mm(h, w_ref[...])

    return pl.pallas_call(
        body, name="inproj_fwd", grid=(S // tm,),
        in_specs=[pl.BlockSpec((tm, D_MODEL), lambda i: (i, 0)), _spec(nw), _spec(scale), _spec(shift),
                  _full((D_MODEL, N_PAD)), pl.BlockSpec(memory_space=pl.ANY)],
        out_specs=[pl.BlockSpec((tm, N_PAD), lambda i: (i, 0)), pl.BlockSpec((tm, D_MODEL), lambda i: (i, 0))],
        out_shape=[SDS((S, N_PAD), F32), SDS((S, D_MODEL), BF16)],
        compiler_params=_cp(("parallel",)),
    )(x, _arr(nw), _arr(scale), _arr(shift), w, tok)


def _inproj_bwd_x(du, w, x, nw, scale, dxn, tok):
    S = x.shape[0]
    tm = min(256, S)

    def body(du_ref, w_ref, x_ref, nw_ref, sc_ref, dxn_ref, tok_ref, dx_ref, red_ref):
        del tok_ref

        @pl.when(pl.program_id(0) == 0)
        def _():
            red_ref[...] = jnp.zeros_like(red_ref)

        dh = _mm_nt(du_ref[...], w_ref[...])
        xv = x_ref[...]
        inv = lax.rsqrt(jnp.mean(xv * xv, axis=-1, keepdims=True) + EPS)
        xhat = xv * inv
        nwv = nw_ref[...]
        g1 = 1.0 + sc_ref[...]
        dxhat = dh * nwv * g1
        dx = inv * (dxhat - xhat * jnp.mean(dxhat * xhat, axis=-1, keepdims=True))
        dx_ref[...] = dxn_ref[...] + dx
        red_ref[0:1, :] += jnp.sum(dh, axis=0, keepdims=True)
        red_ref[1:2, :] += jnp.sum(dh * xhat * nwv, axis=0, keepdims=True)
        red_ref[2:3, :] += jnp.sum(dh * xhat * g1, axis=0, keepdims=True)

    row = pl.BlockSpec((tm, D_MODEL), lambda i: (i, 0))
    return pl.pallas_call(
        body, name="inproj_bwd_x", grid=(S // tm,),
        in_specs=[pl.BlockSpec((tm, N_PAD), lambda i: (i, 0)), _full((D_MODEL, N_PAD)), row, _spec(nw),
                  _spec(scale), row, pl.BlockSpec(memory_space=pl.ANY)],
        out_specs=[row, _full((8, D_MODEL))],
        out_shape=[SDS((S, D_MODEL), F32), SDS((8, D_MODEL), F32)],
        compiler_params=_cp(("arbitrary",)),
    )(du, w, x, _arr(nw), _arr(scale), dxn, tok)


def _inproj_bwd_w(h, du):
    S = h.shape[0]
    tn = 640

    def body(h_ref, du_ref, gw_ref):
        gw_ref[...] = _mm_tn(h_ref[...], _bf(du_ref[...]))

    return pl.pallas_call(
        body, name="inproj_bwd_w", grid=(N_PAD // tn,),
        in_specs=[_full((S, D_MODEL)), pl.BlockSpec((S, tn), lambda j: (0, j))],
        out_specs=pl.BlockSpec((D_MODEL, tn), lambda j: (0, j)),
        out_shape=SDS((D_MODEL, N_PAD), F32),
        compiler_params=_cp(("parallel",)),
    )(h, du)


def _scan_block(a, b, rows):
    d = 1
    while d < a.shape[0]:
        a_s = _shift_down(a, d, rows, 1.0)
        b_s = _shift_down(b, d, rows, 0.0)
        b = a * b_s + b
        a = a * a_s
        d *= 2
    return a, b


def _rscan_block(c, g, rows):
    d = 1
    while d < c.shape[0]:
        c_s = _shift_up(c, d, rows, 1.0)
        g_s = _shift_up(g, d, rows, 0.0)
        g = g + c * g_s
        c = c * c_s
        d *= 2
    return c, g


LRU_BLOCK = 256


def _lru_gates(xa, wa_ref, ba_ref, wx_ref, bx_ref, lam_ref):
    sp = _softplus(-lam_ref[...])
    xb = _bf(xa)
    r = _sigmoid(_mm(xb, wa_ref[...]) + ba_ref[...])
    ig = _sigmoid(_mm(xb, wx_ref[...]) + bx_ref[...])
    la = -LRU_C * r * sp
    a = jnp.exp(la)
    mult = jnp.sqrt(-_expm1(2.0 * la))
    return sp, r, ig, la, a, mult


def _lru_specs(S, l):
    t128 = pl.BlockSpec((None, 1, LANE), lambda t: (l, 0, t))
    gate = pl.BlockSpec((None, None, LANE, LANE), lambda t: (l, t, 0, 0))
    return [pl.BlockSpec((S, 2 * LANE), lambda t: (0, OFF_LRU // (2 * LANE) + t)),
            pl.BlockSpec((None, 4, LANE), lambda t: (l, 0, t)), t128, gate, t128, gate, t128, t128]


def _lru_fwd(l, u, cw, cb, wa, ba, wx, bx, lam, ycat):
    S = u.shape[0]
    tb = min(LRU_BLOCK, S)

    def body(u_ref, cw_ref, cb_ref, wa_ref, ba_ref, wx_ref, bx_ref, lam_ref, ycat_in, ycat_ref, h_ref, a_scr, b_scr):
        del ycat_in
        rows = _iota((S, LANE), 0)
        xa = _conv_fwd(_f(u_ref[:, 0:LANE]), cw_ref, cb_ref, rows)
        _, _, ig, _, a, mult = _lru_gates(xa, wa_ref, ba_ref, wx_ref, bx_ref, lam_ref)
        a_scr[...] = a
        b_scr[...] = mult * (ig * xa)
        rows_b = _iota((tb, LANE), 0)

        def blk(j, hprev):
            sl = pl.ds(pl.multiple_of(j * tb, tb), tb)
            acum, hloc = _scan_block(a_scr[sl, :], b_scr[sl, :], rows_b)
            hf = hloc + acum * hprev
            h_ref[sl, :] = hf
            return _last_row(hf, rows_b)

        lax.fori_loop(0, S // tb, blk, jnp.zeros((1, LANE), F32))
        ycat_ref[...] = _bf(h_ref[...] * _silu(_f(u_ref[:, LANE:2 * LANE])))

    col = pl.BlockSpec((S, LANE), lambda t: (0, t))
    return pl.pallas_call(
        body, name="lru_fwd", grid=(LRU_W // LANE,),
        in_specs=_lru_specs(S, l) + [pl.BlockSpec(memory_space=pl.ANY)],
        out_specs=[col, col],
        out_shape=[SDS((S, D_INNER), BF16), SDS((S,LRU_W), F32)],
        scratch_shapes=[pltpu.VMEM((S, LANE), F32), pltpu.VMEM((S, LANE), F32)],
        input_output_aliases={8: 0},
        compiler_params=_cp(("parallel",)),
    )(u, cw, cb, wa, ba, wx, bx, lam, ycat)


def _lru_bwd(l, u, cw, cb, wa, ba, wx, bx, lam, h_lru, dycat, du):
    S = u.shape[0]
    tb = min(LRU_BLOCK, S)

    def body(u_ref, cw_ref, cb_ref, wa_ref, ba_ref, wx_ref, bx_ref, lam_ref, h_ref, dy_ref, du_in,
             du_ref, red_ref, gwa_ref, gwx_ref, c_scr, g_scr, l_scr):
        del du_in
        rows = _iota((S, LANE), 0)
        ax = _f(u_ref[:, 0:LANE])
        ag = _f(u_ref[:, LANE:2 * LANE])
        xa = _conv_fwd(ax, cw_ref, cb_ref, rows)
        sp, r, ig, la, a, mult = _lru_gates(xa, wa_ref, ba_ref, wx_ref, bx_ref, lam_ref)
        h = h_ref[...]
        dy = _f(dy_ref[...])
        du_ref[:, LANE:2 * LANE] = _bf(dy * h * _dsilu(ag))
        c_scr[...] = _shift_up(a, 1, rows, 0.0)
        g_scr[...] = dy * _silu(ag)
        rows_b = _iota((tb, LANE), 0)
        nb = S // tb

        def blk(jj, lnext):
            j = nb - 1 - jj
            sl = pl.ds(pl.multiple_of(j * tb, tb), tb)
            ccum, lloc = _rscan_block(c_scr[sl, :], g_scr[sl, :], rows_b)
            lam_t = lloc + ccum * lnext
            l_scr[sl, :] = lam_t
            return jnp.sum(jnp.where(rows_b == 0, lam_t, 0.0), axis=0, keepdims=True)

        lax.fori_loop(0, nb, blk, jnp.zeros((1, LANE), F32))
        db = l_scr[...]
        da = db * _shift_down(h, 1, rows)
        dmult = db * ig * xa
        dig = db * mult * xa
        dxa = db * mult * ig
        dla = da * a - dmult * (a * a) / mult
        dr = -LRU_C * sp * dla
        dsp = jnp.sum(-LRU_C * r * dla, axis=0, keepdims=True)
        dlam = -dsp * _sigmoid(-lam_ref[...])
        dzr = dr * r * (1.0 - r)
        dzi = dig * ig * (1.0 - ig)
        dzr_b, dzi_b, xa_b = _bf(dzr), _bf(dzi), _bf(xa)
        dxa = dxa + _mm_nt(dzr_b, wa_ref[...]) + _mm_nt(dzi_b, wx_ref[...])
        gwa_ref[...] = _mm_tn(xa_b, dzr_b)
        gwx_ref[...] = _mm_tn(xa_b, dzi_b)
        dax, dws, dcb = _conv_bwd(ax, dxa, cw_ref, rows)
        du_ref[:, 0:LANE] = _bf(dax)
        parts = dws + [dcb, jnp.sum(dzr, axis=0, keepdims=True), jnp.sum(dzi, axis=0, keepdims=True), dlam]
        for n, p in enumerate(parts):
            red_ref[pl.ds(n, 1), :] = p

    col = pl.BlockSpec((S, LANE), lambda t: (0, t))
    gw = pl.BlockSpec((None, LANE, LANE), lambda t: (t, 0, 0))
    return pl.pallas_call(
        body, name="lru_bwd", grid=(LRU_W // LANE,),
        in_specs=_lru_specs(S, l) + [col, col, pl.BlockSpec(memory_space=pl.ANY)],
        out_specs=[pl.BlockSpec((S, 2 * LANE), lambda t: (0, OFF_LRU // (2 * LANE) + t)),
                   pl.BlockSpec((8, LANE), lambda t: (0, t)), gw, gw],
        out_shape=[SDS((S, N_PAD), BF16), SDS((8, LRU_W), F32), SDS((4, LANE, LANE), F32), SDS((4, LANE, LANE), F32)],
        scratch_shapes=[pltpu.VMEM((S, LANE), F32)] * 3,
        input_output_aliases={10: 0},
        compiler_params=_cp(("parallel",)),
    )(u, cw, cb, wa, ba, wx, bx, lam, h_lru, dycat, du)


HG_LEVELS = 6


def _hg_consts():
    C = HG_CHUNK
    t = np.arange(C)[:, None]
    r = np.arange(C)[None, :]
    mats = []
    for l in range(HG_LEVELS):
        b = 1 << l
        upper = (t % (2 * b)) >= b
        anchor = (t // (2 * b)) * 2 * b + b - 1
        mats.append((upper & (r > anchor) & (r <= t)) | ((~upper) & (r > t) & (r <= anchor)))
    mats.append(r <= t)
    mats.append(r > t)
    return np.concatenate(mats, 0).astype(np.float32)


def _hg_factors(hf, lb, mall):
    s = _sigmoid(hf)
    f = lb + (1.0 - lb) * s
    lf = jnp.log(f)
    k = (1.0 - lb) * _sigmoid(-hf)
    e = jnp.exp(_sel_l(mall, lf))
    C = HG_CHUNK
    eq = [e[l * C:(l + 1) * C] for l in range(HG_LEVELS)]
    ecum = e[HG_LEVELS * C:(HG_LEVELS + 1) * C]
    erem = e[(HG_LEVELS + 1) * C:(HG_LEVELS + 2) * C]
    return s, f, k, eq, eq, ecum, erem


def _hg_masks():
    C = HG_CHUNK
    ri, ci = _iota((C, C), 0), _iota((C, C), 1)
    rr = _iota((C, LANE), 0)
    gm = [(lax.shift_right_logical(ri, l + 1) == lax.shift_right_logical(ci, l + 1)).astype(F32)
          for l in range(HG_LEVELS)]
    up = [(lax.shift_right_logical(rr, l) & 1) == 1 for l in range(HG_LEVELS)]
    eye = (ri == ci).astype(F32)
    return gm, up, eye, rr


def _hg_scores(qh, kh, eq, ek, sl, gm, up, eye):
    qs, ks = [], []
    p = _mm_nt(_bf(qh), _bf(kh)) * eye
    for l in range(HG_LEVELS):
        ql = jnp.where(up[l], qh * eq[l][:, sl], 0.0)
        kl = jnp.where(up[l], 0.0, kh * ek[l][:, sl])
        p = p + _mm_nt(_bf(ql), _bf(kl)) * gm[l]
        qs.append(ql)
        ks.append(kl)
    return p, qs, ks


HG_SUB = 4


def _hg_fwd(u, lb, nw, mall, ycat):
    S = u.shape[0]
    C = HG_CHUNK
    n = S // C
    rows = HG_SUB * C

    def body(u_ref, lb_ref, nw_ref, mall_ref, ycat_in, ycat_ref, o_ref, st_ref, st):
        del ycat_in

        @pl.when(pl.program_id(0) == 0)
        def _():
            st[...] = jnp.zeros_like(st)

        gm, up, eye, rr = _hg_masks()
        for sub in range(HG_SUB):
            r = slice(sub * C, (sub + 1) * C)
            q = _silu(_f(u_ref[r, 0:512]))
            v = u_ref[r, 1024:1536]
            _, _, k, eq, ek, ecum, erem = _hg_factors(_f(u_ref[r, 512:1024]), lb_ref[...], mall_ref[...])
            for h in range(HG_HEADS):
                sl = slice(h * LANE, (h + 1) * LANE)
                qh, kh, vh = q[:, sl], k[:, sl], _bf(v[:, sl])
                p, _, _ = _hg_scores(qh, kh, eq, ek, sl, gm, up, eye)
                sth = st[h]
                st_ref[sub, h] = sth
                o_ref[r, sl] = _mm(_bf(p), vh) + _mm_nt(_bf(qh * ecum[:, sl]), _bf(sth))
                st[h] = sth * _last_row(ecum[:, sl], rr) + _mm_tn(vh, _bf(kh * erem[:, sl]))
            o = o_ref[r, :]
            inv = lax.rsqrt(jnp.mean(o * o, axis=-1, keepdims=True) + EPS)
            ycat_ref[r, :] = _bf((o * inv) * nw_ref[...] * _silu(_f(u_ref[r, 1536:2048])))

    return pl.pallas_call(
        body, name="hg_fwd", grid=(n // HG_SUB,),
        in_specs=[pl.BlockSpec((rows, 2048), lambda i: (i, 0)), _spec(lb), _spec(nw), _full(mall.shape),
                  pl.BlockSpec(memory_space=pl.ANY)],
        out_specs=[pl.BlockSpec((rows, HG_W), lambda i: (i, 1)), pl.BlockSpec((rows, HG_W), lambda i: (i, 0)),
                   pl.BlockSpec((HG_SUB, HG_HEADS, LANE, LANE), lambda i: (i, 0, 0, 0))],
        out_shape=[SDS((S, D_INNER), BF16), SDS((S,HG_W), F32), SDS((n, HG_HEADS, LANE, LANE), F32)],
        scratch_shapes=[pltpu.VMEM((HG_HEADS, LANE, LANE), F32)],
        input_output_aliases={4: 0},
        compiler_params=_cp(("arbitrary",)),
    )(u, _arr(lb), _arr(nw), mall, ycat)


def _hg_bwd(u, lb, nw, mall, mall_t, o_b, states, dycat, du):
    S = u.shape[0]
    C = HG_CHUNK
    n = S // C
    nb = n // HG_SUB
    rows = HG_SUB * C
    L2 = HG_LEVELS

    def body(u_ref, lb_ref, nw_ref, mall_ref, mallt_ref, o_ref, st_ref, dy_ref, du_in, du_ref, red_ref,
             dst, dlast_s, dq_s, dk_s, dex):
        del du_in

        @pl.when(pl.program_id(0) == 0)
        def _():
            dst[...] = jnp.zeros_like(dst)
            red_ref[...] = jnp.zeros_like(red_ref)

        lb = lb_ref[...]
        nwv = nw_ref[...]
        gm, up, eye, rr = _hg_masks()
        for sub in reversed(range(HG_SUB)):
            r = slice(sub * C, (sub + 1) * C)
            hq, hf, hg = _f(u_ref[r, 0:512]), _f(u_ref[r, 512:1024]), _f(u_ref[r, 1536:2048])
            q = _silu(hq)
            v = u_ref[r, 1024:1536]
            s, f, k, eq, ek, ecum, erem = _hg_factors(hf, lb, mall_ref[...])
            o = o_ref[r, :]
            dy = _f(dy_ref[r, :])
            inv = lax.rsqrt(jnp.mean(o * o, axis=-1, keepdims=True) + EPS)
            ohat = o * inv
            du_ref[r, 1536:2048] = _bf(dy * ohat * nwv * _dsilu(hg))
            dn = dy * _silu(hg)
            red_ref[0:1, :] += jnp.sum(dn * ohat, axis=0, keepdims=True)
            dohat = dn * nwv
            do = inv * (dohat - ohat * jnp.mean(dohat * ohat, axis=-1, keepdims=True))
            for h in range(HG_HEADS):
                sl = slice(h * LANE, (h + 1) * LANE)
                qh, kh, vh, doh = q[:, sl], k[:, sl], _bf(v[:, sl]), _bf(do[:, sl])
                p, qs, ks = _hg_scores(qh, kh, eq, ek, sl, gm, up, eye)
                st_f = st_ref[sub, h]
                sth = _bf(st_f)
                dsth = dst[h]
                dsth_b = _bf(dsth)
                qt = qh * ecum[:, sl]
                kt = kh * erem[:, sl]
                elast = _last_row(ecum[:, sl], rr)
                dp = _mm_nt(doh, vh)
                du_ref[r, 1024 + h * LANE:1024 + (h + 1) * LANE] = _bf(_mm_tn(_bf(p), doh) + _mm_nt(_bf(kt), dsth_b))
                dpe = _bf(dp * eye)
                dqt = _mm(doh, sth)
                dkt = _mm(vh, dsth_b)
                dq = dqt * ecum[:, sl] + _mm(dpe, _bf(kh))
                dk = dkt * erem[:, sl] + _mm_tn(dpe, _bf(qh))
                dex[sub, L2 * C:(L2 + 1) * C, sl] = dqt * qt
                dex[sub, (L2 + 1) * C:(L2 + 2) * C, sl] = dkt * kt
                for l in range(HG_LEVELS):
                    dpl = _bf(dp * gm[l])
                    dql = _mm(dpl, _bf(ks[l]))
                    dkl = _mm_tn(dpl, _bf(qs[l]))
                    dq = dq + jnp.where(up[l], dql * eq[l][:, sl], 0.0)
                    dk = dk + jnp.where(up[l], 0.0, dkl * ek[l][:, sl])
                    dex[sub, l * C:(l + 1) * C, sl] = dql * qs[l] + dkl * ks[l]
                dlast_s[sub, :, sl] = jnp.sum(dsth * st_f, axis=0, keepdims=True) * elast
                dst[h] = dsth * elast + _mm_tn(doh, _bf(qt))
                dq_s[sub, :, sl] = dq
                dk_s[sub, :, sl] = dk
            dq = dq_s[sub]
            dk = dk_s[sub]
            dlf = _sel_l2(mallt_ref[...], dex[sub]) + dlast_s[sub]
            du_ref[r, 0:512] = _bf(dq * _dsilu(hq))
            t = (1.0 - s) * (dlf / f - dk)
            du_ref[r, 512:1024] = _bf((1.0 - lb) * s * t)
            red_ref[1:2, :] += jnp.sum(t, axis=0, keepdims=True)

    rev = lambda i: (nb - 1 - i, 0)
    return pl.pallas_call(
        body, name="hg_bwd", grid=(nb,),
        in_specs=[pl.BlockSpec((rows, 2048), rev), _spec(lb), _spec(nw), _full(mall.shape), _full(mall_t.shape),
                  pl.BlockSpec((rows, HG_W), rev),
                  pl.BlockSpec((HG_SUB, HG_HEADS, LANE, LANE), lambda i: (nb - 1 - i, 0, 0, 0)),
                  pl.BlockSpec((rows, HG_W), lambda i: (nb - 1 - i, 1)), pl.BlockSpec(memory_space=pl.ANY)],
        out_specs=[pl.BlockSpec((rows, 2048), rev), pl.BlockSpec((8, HG_W), lambda i: (0, 0))],
        out_shape=[SDS((S, N_PAD), BF16), SDS((8, HG_W), F32)],
        scratch_shapes=[pltpu.VMEM((HG_HEADS, LANE, LANE), F32), pltpu.VMEM((HG_SUB, 1, HG_W), F32),
                        pltpu.VMEM((HG_SUB, C, HG_W), F32), pltpu.VMEM((HG_SUB, C, HG_W), F32),
                        pltpu.VMEM((HG_SUB, (L2 + 2) * C, HG_W), F32)],
        input_output_aliases={8: 0},
        compiler_params=_cp(("arbitrary",)),
    )(u, _arr(lb), _arr(nw), mall, mall_t, o_b, states, dycat, du)


def _ssdconv_fwd(l, u, cw, cb):
    S = u.shape[0]

    def body(u_ref, cw_ref, cb_ref, out_ref):
        rows = _iota((S, LANE), 0)
        out_ref[...] = _silu(_conv_fwd(_f(u_ref[...]), cw_ref, cb_ref, rows))

    return pl.pallas_call(
        body, name="ssdconv_fwd", grid=(SSD_CONV // LANE,),
        in_specs=[pl.BlockSpec((S, LANE), lambda t: (0, OFF_XBC // LANE + t)),
                  pl.BlockSpec((None, 4, LANE), lambda t: (l, 0, t)), pl.BlockSpec((None, 1, LANE), lambda t: (l, 0, t))],
        out_specs=pl.BlockSpec((S, LANE), lambda t: (0, t)),
        out_shape=SDS((S, SSD_CONV), F32),
        compiler_params=_cp(("parallel",)),
    )(u, cw, cb)


def _ssdconv_bwd(l, u, cw, cb, dxbc, du):
    S = u.shape[0]

    def body(u_ref, cw_ref, cb_ref, d_ref, du_in, du_ref, red_ref):
        del du_in
        rows = _iota((S, LANE), 0)
        x = _f(u_ref[...])
        dco = d_ref[...] * _dsilu(_conv_fwd(x, cw_ref, cb_ref, rows))
        dx, dws, dcb = _conv_bwd(x, dco, cw_ref, rows)
        du_ref[...] = _bf(dx)
        for n, p in enumerate(dws + [dcb]):
            red_ref[pl.ds(n, 1), :] = p
        red_ref[pl.ds(5, 3), :] = jnp.zeros((3, LANE), F32)

    ucol = pl.BlockSpec((S, LANE), lambda t: (0, OFF_XBC // LANE + t))
    return pl.pallas_call(
        body, name="ssdconv_bwd", grid=(SSD_CONV // LANE,),
        in_specs=[ucol, pl.BlockSpec((None, 4, LANE), lambda t: (l, 0, t)),
                  pl.BlockSpec((None, 1, LANE), lambda t: (l, 0, t)),
                  pl.BlockSpec((S, LANE), lambda t: (0, t)), pl.BlockSpec(memory_space=pl.ANY)],
        out_specs=[ucol, pl.BlockSpec((8, LANE), lambda t: (0, t))],
        out_shape=[SDS((S, N_PAD), BF16), SDS((8, SSD_CONV), F32)],
        input_output_aliases={4: 0},
        compiler_params=_cp(("parallel",)),
    )(u, cw, cb, dxbc, du)


def _ssd_consts():
    e64 = np.zeros((LANE, SSD_W), np.float32)
    for h in range(SSD_HEADS):
        e64[h, h * SSD_P:(h + 1) * SSD_P] = 1.0
    T = SSD_CHUNK
    tril = (np.arange(T)[None, :] <= np.arange(T)[:, None]).astype(np.float32)
    return e64, tril, tril.T.copy()


def _ssd_common(zdt, bias_ref, alog_ref, tril, e64, cum_ref, cumt_ref):
    T = SSD_CHUNK
    lane = _iota((1, LANE), 1)
    a_neg = jnp.where(lane < SSD_HEADS, -jnp.exp(alog_ref[...]), 0.0)
    dtpre = zdt[:, SSD_W:SSD_W + LANE] + bias_ref[...]
    dt = _softplus(dtpre)
    cum = _sel_l(tril, dt * a_neg)
    cum_ref[...] = cum
    cumt_ref[...] = cum.T
    cum_x = _sel_r(cum, e64)
    last_x = _last_row(cum_x, _iota((T, SSD_W), 0))
    ecum_x = jnp.exp(cum_x)
    erem_x = jnp.exp(last_x - cum_x)
    elast_x = jnp.exp(last_x)
    dt_x = _sel_r(dt, e64)
    return a_neg, dtpre, dt, ecum_x, erem_x, elast_x, dt_x


def _ssd_decay(cum_ref, cumt_ref, h, causal):
    T = SSD_CHUNK
    diff = jnp.broadcast_to(cum_ref[:, pl.ds(h, 1)], (T, T)) - cumt_ref[pl.ds(h, 1), :]
    return jnp.exp(jnp.where(causal, diff, NEG))


def _group_norm_fwd(y1, nwv):
    outs, invs = [], []
    for g in range(2):
        seg = y1[:, g * 512:(g + 1) * 512]
        inv = lax.rsqrt(jnp.mean(seg * seg, axis=-1, keepdims=True) + EPS)
        outs.append(seg * inv * nwv[:, g * 512:(g + 1) * 512])
        invs.append(inv)
    return outs, invs


def _ssd_fwd(u, xbc, bias, alog, dskip_x, nw, consts, ycat):
    S = u.shape[0]
    T = SSD_CHUNK
    n = S // T
    e64, tril, _ = consts

    def body(u_ref, xbc_ref, bias_ref, alog_ref, dx_ref, nw_ref, e64_ref, tril_ref, ycat_in,
             ycat_ref, y_ref, st_ref, st, cumt, cum_e):
        del ycat_in

        @pl.when(pl.program_id(0) == 0)
        def _():
            st[...] = jnp.zeros_like(st)

        zdt = _f(u_ref[...])
        z = zdt[:, 0:SSD_W]
        xs = xbc_ref[:, 0:SSD_W]
        _, _, _, ecum_x, erem_x, elast_x, dt_x = _ssd_common(
            zdt, bias_ref, alog_ref, tril_ref[...], e64_ref[...], cum_e, cumt)
        causal = _iota((T, T), 0) >= _iota((T, T), 1)
        lo = _iota((T, LANE), 1) < SSD_P
        xdt = xs * dt_x
        xrem = xdt * erem_x
        st_ref[...] = st[...]
        for g in range(2):
            gs = slice(g * 512, (g + 1) * 512)
            bg = _bf(xbc_ref[:, SSD_W + g * LANE:SSD_W + (g + 1) * LANE])
            cg = _bf(xbc_ref[:, SSD_W + 256 + g * LANE:SSD_W + 256 + (g + 1) * LANE])
            cb = _mm_nt(cg, bg)
            yin = _mm(cg, _bf(st[:, gs])) * ecum_x[:, gs]
            for j in range(4):
                h0 = 8 * g + 2 * j
                cs = slice(h0 * SSD_P, (h0 + 2) * SSD_P)
                xp = xdt[:, cs]
                s0 = _bf(cb * _ssd_decay(cum_e, cumt, h0, causal))
                s1 = _bf(cb * _ssd_decay(cum_e, cumt, h0 + 1, causal))
                y_ref[:, cs] = (_mm(s0, _bf(jnp.where(lo, xp, 0.0))) + _mm(s1, _bf(jnp.where(lo, 0.0, xp)))
                                + yin[:, j * LANE:(j + 1) * LANE])
            st[:, gs] = st[:, gs] * elast_x[:, gs] + _mm_tn(bg, _bf(xrem[:, gs]))
        y1 = (y_ref[...] + dx_ref[...] * xs) * _silu(z)
        outs, _ = _group_norm_fwd(y1, nw_ref[...])
        for g in range(2):
            ycat_ref[:, g * 512:(g + 1) * 512] = _bf(outs[g])

    return pl.pallas_call(
        body, name="ssd_fwd", grid=(n,),
        in_specs=[pl.BlockSpec((T, SSD_W + LANE), lambda i: (i, OFF_Z // (SSD_W + LANE))),
                  pl.BlockSpec((T, SSD_CONV), lambda i: (i, 0)), _spec(bias), _spec(alog), _spec(dskip_x), _spec(nw),
                  _full(e64.shape), _full(tril.shape), pl.BlockSpec(memory_space=pl.ANY)],
        out_specs=[pl.BlockSpec((T, SSD_W), lambda i: (i, 1)), pl.BlockSpec((T, SSD_W), lambda i: (i, 0)),
                   pl.BlockSpec((None, SSD_N, SSD_W), lambda i: (i, 0, 0))],
        out_shape=[SDS((S, D_INNER), BF16), SDS((S,SSD_W), F32), SDS((n, SSD_N, SSD_W), F32)],
        scratch_shapes=[pltpu.VMEM((SSD_N, SSD_W), F32), pltpu.VMEM((LANE, T), F32), pltpu.VMEM((T, LANE), F32)],
        input_output_aliases={8: 0},
        compiler_params=_cp(("arbitrary",)),
    )(u, xbc, _arr(bias), _arr(alog), _arr(dskip_x), _arr(nw), _bfc(e64), _bfc(tril), ycat)


def _ssd_bwd(u, xbc, bias, alog, dskip_x, nw, consts, y_ssd, states, dycat, du, tok):
    S = u.shape[0]
    T = SSD_CHUNK
    n = S // T
    e64, tril, triu = consts
    e64t = np.ascontiguousarray(e64.T)

    def body(u_ref, xbc_ref, bias_ref, alog_ref, dx_ref, nw_ref, e64_ref, e64t_ref, tril_ref, triu_ref,
             y_ref, st_ref, dy_ref, du_in, tok_ref, du_ref, dxbc_ref, red_ref, dst, dl_s, cumt, dxdt_s, dy0_s, gb_s,
             gc_s, cum_e, cs_s):
        del du_in, tok_ref

        @pl.when(pl.program_id(0) == 0)
        def _():
            dst[...] = jnp.zeros_like(dst)
            red_ref[...] = jnp.zeros_like(red_ref)
            cs_s[...] = jnp.zeros_like(cs_s)

        zdt = _f(u_ref[...])
        z = zdt[:, 0:SSD_W]
        xs = xbc_ref[:, 0:SSD_W]
        a_neg, dtpre, dt, ecum_x, erem_x, elast_x, dt_x = _ssd_common(
            zdt, bias_ref, alog_ref, tril_ref[...], e64_ref[...], cum_e, cumt)
        causal = _iota((T, T), 0) >= _iota((T, T), 1)
        lo = _iota((T, LANE), 1) < SSD_P
        xdt = xs * dt_x
        xrem = xdt * erem_x
        y = y_ref[...]
        dxv = dx_ref[...]
        nwv = nw_ref[...]
        sz = _silu(z)
        y0 = y + dxv * xs
        y1 = y0 * sz
        for g in range(2):
            gs = slice(g * 512, (g + 1) * 512)
            seg = y1[:, gs]
            inv = lax.rsqrt(jnp.mean(seg * seg, axis=-1, keepdims=True) + EPS)
            shat = seg * inv
            dyg = _f(dy_ref[:, gs])
            red_ref[0:1, gs] += jnp.sum(dyg * shat, axis=0, keepdims=True)
            dsh = dyg * nwv[:, gs]
            dy1g = inv * (dsh - shat * jnp.mean(dsh * shat, axis=-1, keepdims=True))
            du_ref[:, gs] = _bf(dy1g * y0[:, gs] * _dsilu(z[:, gs]))
            dy0_s[:, gs] = dy1g * sz[:, gs]
        dy0 = dy0_s[...]
        red_ref[1:2, :] += jnp.sum(dy0 * xs, axis=0, keepdims=True)
        dyin = dy0 * ecum_x
        lane = _iota((T, LANE), 1)
        dcum = jnp.zeros((T, LANE), F32)

        def decay_grad(h, gm):
            cs_s[pl.ds(h, 1), :] = jnp.sum(gm, axis=0, keepdims=True)
            return jnp.where(lane == h, jnp.sum(gm, axis=1, keepdims=True), 0.0)

        for g in range(2):
            gs = slice(g * 512, (g + 1) * 512)
            bg = _bf(xbc_ref[:, SSD_W + g * LANE:SSD_W + (g + 1) * LANE])
            cg = _bf(xbc_ref[:, SSD_W + 256 + g * LANE:SSD_W + 256 + (g + 1) * LANE])
            cb = _mm_nt(cg, bg)
            dst_f, st_f = dst[:, gs], st_ref[:, gs]
            dstg = _bf(dst_f)
            stg = _bf(st_f)
            dyin_g = _bf(dyin[:, gs])
            xrem_g = _bf(xrem[:, gs])
            dcb = jnp.zeros((T, T), F32)
            dxr = _mm(bg, dstg)
            dxdt_s[:, gs] = dxr * erem_x[:, gs]
            gc_s[:, gs] = dxr * xrem[:, gs]
            gb_s[:, gs] = dyin[:, gs] * _mm(cg, stg)
            dl_s[:, gs] = jnp.sum(dst_f * st_f, axis=0, keepdims=True) * elast_x[:, gs]
            for j in range(4):
                h0 = 8 * g + 2 * j
                cs = slice(h0 * SSD_P, (h0 + 2) * SSD_P)
                xp = xdt[:, cs]
                dyp = dy0[:, cs]
                x_lo, x_hi = _bf(jnp.where(lo, xp, 0.0)), _bf(jnp.where(lo, 0.0, xp))
                d_lo, d_hi = _bf(jnp.where(lo, dyp, 0.0)), _bf(jnp.where(lo, 0.0, dyp))
                l0 = _ssd_decay(cum_e, cumt, h0, causal)
                l1 = _ssd_decay(cum_e, cumt, h0 + 1, causal)
                s0 = cb * l0
                s1 = cb * l1
                ds0 = _mm_nt(d_lo, x_lo)
                ds1 = _mm_nt(d_hi, x_hi)
                dcb = dcb + ds0 * l0 + ds1 * l1
                dxdt_s[:, cs] += _mm_tn(_bf(s0), d_lo) + _mm_tn(_bf(s1), d_hi)
                dcum = dcum + decay_grad(h0, ds0 * s0) + decay_grad(h0 + 1, ds1 * s1)
            dcb_b = _bf(dcb)
            dxbc_ref[:, SSD_W + g * LANE:SSD_W + (g + 1) * LANE] = _mm_tn(dcb_b, cg) + _mm_nt(xrem_g, dstg)
            dxbc_ref[:, SSD_W + 256 + g * LANE:SSD_W + 256 + (g + 1) * LANE] = _mm(dcb_b, bg) + _mm_nt(dyin_g, stg)
            dst[:, gs] = dst_f * elast_x[:, gs] + _mm_tn(cg, dyin_g)
        dxdt = dxdt_s[...]
        dxbc_ref[:, 0:SSD_W] = dxdt * dt_x + dy0 * dxv
        e64t = e64t_ref[...]
        gc = gc_s[...]
        dlast_x = jnp.sum(gc, axis=0, keepdims=True) + dl_s[...]
        dlast = jnp.max(_sel_r(jnp.broadcast_to(dlast_x, (8, SSD_W)), e64t), axis=0, keepdims=True)
        dcum = (dcum - cs_s[...].T + _sel_r(gb_s[...] - gc, e64t)
                + jnp.where(_iota((T, LANE), 0) == T - 1, dlast, 0.0))
        dda = _sel_l(triu_ref[...], dcum)
        ddt = dda * a_neg + _sel_r(dxdt * xs, e64t)
        ddtpre = ddt * _sigmoid(dtpre)
        du_ref[:, SSD_W:SSD_W + LANE] = _bf(jnp.where(lane < SSD_HEADS, ddtpre, 0.0))
        red_ref[2:3, 0:LANE] += jnp.sum(ddtpre, axis=0, keepdims=True)
        red_ref[3:4, 0:LANE] += jnp.sum(dda * dt, axis=0, keepdims=True)

    rev = lambda i: (n - 1 - i, 0)
    return pl.pallas_call(
        body, name="ssd_bwd", grid=(n,),
        in_specs=[pl.BlockSpec((T, SSD_W + LANE), lambda i: (n - 1 - i, OFF_Z // (SSD_W + LANE))),
                  pl.BlockSpec((T, SSD_CONV), rev), _spec(bias), _spec(alog), _spec(dskip_x), _spec(nw),
                  _full(e64.shape), _full(e64t.shape), _full(tril.shape), _full(triu.shape),
                  pl.BlockSpec((T, SSD_W), rev), pl.BlockSpec((None, SSD_N, SSD_W), lambda i: (n - 1 - i, 0, 0)),
                  pl.BlockSpec((T, SSD_W), lambda i: (n - 1 - i, 1)), pl.BlockSpec(memory_space=pl.ANY),
                  pl.BlockSpec(memory_space=pl.ANY)],
        out_specs=[pl.BlockSpec((T, SSD_W + LANE), lambda i: (n - 1 - i, OFF_Z // (SSD_W + LANE))),
                   pl.BlockSpec((T, SSD_CONV), rev), pl.BlockSpec((8, SSD_W), lambda i: (0, 0))],
        out_shape=[SDS((S, N_PAD), BF16), SDS((S, SSD_CONV), F32), SDS((8, SSD_W), F32)],
        scratch_shapes=[pltpu.VMEM((SSD_N, SSD_W), F32), pltpu.VMEM((1, SSD_W), F32), pltpu.VMEM((LANE, T), F32)]
        + [pltpu.VMEM((T, SSD_W), F32)] * 4 + [pltpu.VMEM((T, LANE), F32), pltpu.VMEM((LANE, T), F32)],
        input_output_aliases={13: 0},
        compiler_params=_cp(("arbitrary",)),
    )(u, xbc, _arr(bias), _arr(alog), _arr(dskip_x), _arr(nw), _bfc(e64), _bfc(e64t), _bfc(tril), _bfc(triu), y_ssd,
      states, dycat, du, tok)


def _bfc(a):
    return jnp.asarray(a, BF16)


def _outproj_fwd(ycat, wo, x, gate):
    S = x.shape[0]
    tm = min(512, S)

    def body(yc_ref, wo_ref, x_ref, g_ref, xn_ref, y_ref):
        y = _mm(_bf(yc_ref[...]), wo_ref[...])
        y_ref[...] = y
        xn_ref[...] = x_ref[...] + g_ref[...] * y

    row = pl.BlockSpec((tm, D_MODEL), lambda i: (i, 0))
    return pl.pallas_call(
        body, name="outproj_fwd", grid=(S // tm,),
        in_specs=[pl.BlockSpec((tm, D_INNER), lambda i: (i, 0)), _full((D_INNER, D_MODEL)), row, _spec(gate)],
        out_specs=[row, row],
        out_shape=[SDS((S, D_MODEL), F32), SDS((S, D_MODEL), F32)],
        compiler_params=_cp(("parallel",)),
    )(ycat, wo, x, _arr(gate))


def _outproj_bwd(dxn, y, gate, ycat, wo):
    S = dxn.shape[0]
    tm = min(512, S)

    def body(dx_ref, y_ref, g_ref, yc_ref, wo_ref, dyc_ref, gwo_ref, dg_ref, acc):
        @pl.when(pl.program_id(0) == 0)
        def _():
            acc[...] = jnp.zeros_like(acc)
            dg_ref[...] = jnp.zeros_like(dg_ref)

        dxv = dx_ref[...]
        dy = _bf(dxv * g_ref[...])
        dg_ref[0:1, :] += jnp.sum(dxv * y_ref[...], axis=0, keepdims=True)
        dyc_ref[...] = _mm_nt(dy, wo_ref[...])
        acc[...] += _mm_tn(_bf(yc_ref[...]), dy)

        @pl.when(pl.program_id(0) == pl.num_programs(0) - 1)
        def _():
            gwo_ref[...] = acc[...].astype(BF16)

    row = pl.BlockSpec((tm, D_MODEL), lambda i: (i, 0))
    wide = pl.BlockSpec((tm, D_INNER), lambda i: (i, 0))
    return pl.pallas_call(
        body, name="outproj_bwd", grid=(S // tm,),
        in_specs=[row, row, _spec(gate), wide, _full((D_INNER, D_MODEL))],
        out_specs=[wide, _full((D_INNER, D_MODEL)), _full((8, D_MODEL))],
        out_shape=[SDS((S, D_INNER), F32), SDS((D_INNER, D_MODEL), BF16), SDS((8, D_MODEL), F32)],
        scratch_shapes=[pltpu.VMEM((D_INNER, D_MODEL), F32)],
        compiler_params=_cp(("arbitrary",)),
    )(dxn, y, _arr(gate), ycat, wo)


def _loss_head(x, fw, target):
    S = x.shape[0]
    tm = min(512, S)

    def body(x_ref, fw_ref, t_ref, dx_ref, red_ref):
        @pl.when(pl.program_id(0) == 0)
        def _():
            red_ref[...] = jnp.zeros_like(red_ref)

        xv = x_ref[...]
        fwv = fw_ref[...]
        inv = lax.rsqrt(jnp.mean(xv * xv, axis=-1, keepdims=True) + EPS)
        xhat = xv * inv
        err = xhat * fwv - t_ref[...]
        col = jnp.sum(err * err, axis=0, keepdims=True)
        red_ref[1:2, :] += jnp.broadcast_to(jnp.sum(col, axis=1, keepdims=True) * (0.5 / D_MODEL), (1, D_MODEL))
        dy = err * (1.0 / D_MODEL)
        red_ref[0:1, :] += jnp.sum(dy * xhat, axis=0, keepdims=True)
        dxhat = dy * fwv
        dx_ref[...] = inv * (dxhat - xhat * jnp.mean(dxhat * xhat, axis=-1, keepdims=True))

    row = pl.BlockSpec((tm, D_MODEL), lambda i: (i, 0))
    return pl.pallas_call(
        body, name="loss_head", grid=(S // tm,),
        in_specs=[row, _vec(D_MODEL), row],
        out_specs=[row, _full((8, D_MODEL))],
        out_shape=[SDS((S, D_MODEL), F32), SDS((8, D_MODEL), F32)],
        compiler_params=_cp(("arbitrary",)),
    )(x, fw, target)


ADA_COLS = 3 * D_MODEL // N_DEV


def _ada_fwd(c_all, w_ada, b_cols):
    def body(c_ref, w_ref, b_ref, out_ref):
        out_ref[...] = _mm(_bf(_silu(c_ref[...])), _bf(w_ref[...])) + b_ref[...]

    return pl.pallas_call(
        body, name="ada_fwd", grid=(DEPTH,),
        in_specs=[_full((N_DEV, D_MODEL)), pl.BlockSpec((None, D_MODEL, ADA_COLS), lambda l: (l, 0, 0)),
                  pl.BlockSpec((None, 1, ADA_COLS), lambda l: (l, 0, 0))],
        out_specs=pl.BlockSpec((None, N_DEV, ADA_COLS), lambda l: (l, 0, 0)),
        out_shape=SDS((DEPTH, N_DEV, ADA_COLS), F32),
        compiler_params=_cp(("parallel",)),
    )(c_all, w_ada, b_cols)


def _ada_bwd(ct_pad, dmod_pad):
    def body(c_ref, d_ref, out_ref):
        out_ref[...] = _mm(_bf(_silu(c_ref[...])), _bf(d_ref[...]))

    return pl.pallas_call(
        body, name="ada_bwd", grid=(DEPTH,),
        in_specs=[_full((D_MODEL, LANE)), pl.BlockSpec((None, LANE, ADA_COLS), lambda l: (l, 0, 0))],
        out_specs=pl.BlockSpec((None, D_MODEL, ADA_COLS), lambda l: (l, 0, 0)),
        out_shape=SDS((DEPTH, D_MODEL, ADA_COLS), F32),
        compiler_params=_cp(("parallel",)),
    )(ct_pad, dmod_pad)


def _adamw(parts, w, m, v, name, own=None, layers=None, prev=None):
    n, L, R, C = parts.shape
    lo, hi = layers or (0, L)
    tr = R
    while tr * C * 4 > (1 << 20) and tr % 16 == 0:
        tr //= 2
    first = 1 if own is None else 2

    def body(*refs):
        p_ref = refs[0]
        w_ref, m_ref, v_ref = refs[first:first + 3]
        g_ref, d_ref, mo_ref, vo_ref = refs[-4:]

        def part(k):
            if own is None:
                return p_ref[k].astype(F32)
            me = 4 * lax.axis_index("x") + 2 * lax.axis_index("y") + lax.axis_index("c")
            return jnp.where(me == k, refs[1][...], p_ref[k]).astype(F32)

        g = part(0)
        for k in range(1, n):
            g = g + part(k)
        mn = ADAM_B1 * m_ref[...] + (1.0 - ADAM_B1) * g
        vn = ADAM_B2 * v_ref[...] + (1.0 - ADAM_B2) * (g * g)
        m_hat = mn / (1.0 - ADAM_B1 ** ADAM_STEP)
        v_hat = vn / (1.0 - ADAM_B2 ** ADAM_STEP)
        g_ref[...] = g
        d_ref[...] = -ADAM_LR * (m_hat / (jnp.sqrt(v_hat) + ADAM_EPS) + ADAM_WD * w_ref[...])
        mo_ref[...] = mn
        vo_ref[...] = vn

    blk = pl.BlockSpec((None, tr, C), lambda l, i: (lo + l, i, 0))
    own_blk = [] if own is None else [pl.BlockSpec((None, tr, C), lambda l, i: (l, i, 0))]
    n_blk = 3 if own is None else 4
    return pl.pallas_call(
        body, name=name, grid=(hi - lo, R // tr),
        in_specs=[pl.BlockSpec((n, None, tr, C), lambda l, i: (0, lo + l, i, 0))] + own_blk + [blk] * 3
        + ([] if prev is None else [ANY] * 4),
        out_specs=[blk] * 4,
        out_shape=[SDS((L, R, C), F32)] * 4,
        input_output_aliases={} if prev is None else {1 + n_blk + k: k for k in range(4)},
        compiler_params=_cp(("parallel", "parallel")),
    )(parts, *([] if own is None else [own]), w, m, v, *([] if prev is None else prev))


MESH = pl.DeviceIdType.MESH
ANY = pl.BlockSpec(memory_space=pl.ANY)


def _all_gather(v, name):
    def body(v_ref, out_ref, send_sems, recv_sems, local_sem):
        x, y, c = lax.axis_index("x"), lax.axis_index("y"), lax.axis_index("c")
        me, sibling = (x, y, c), (x, y, 1 - c)
        chips = [(1 - x, y), (x, 1 - y), (1 - x, 1 - y)]

        def slot(px, py, pc):
            return out_ref.at[4 * px + 2 * py + pc]

        def copy(k, block, to, src=None):
            return pltpu.make_async_remote_copy(
                src_ref=slot(*block) if src is None else src, dst_ref=slot(*block),
                send_sem=send_sems.at[k], recv_sem=recv_sems.at[k], device_id=to, device_id_type=MESH)

        mine = pltpu.make_async_copy(v_ref, slot(*me), local_sem)
        mine.start()
        first = [copy(0, me, sibling, src=v_ref)]
        first += [copy(1 + j, me, (*chip, c), src=v_ref) for j, chip in enumerate(chips)]
        for cp in first:
            cp.start()
        passed = [copy(4 + j, (*chip, c), sibling) for j, chip in enumerate(chips)]
        for j, chip in enumerate(chips):
            copy(1 + j, (*chip, c), me).wait_recv()
            passed[j].start()
        copy(0, sibling, me).wait_recv()
        for j, chip in enumerate(chips):
            copy(4 + j, (*chip, 1 - c), me).wait_recv()
        for cp in first + passed:
            cp.wait_send()
        mine.wait()

    return pl.pallas_call(
        body, name=name, in_specs=[ANY], out_specs=ANY,
        out_shape=SDS((N_DEV,) + v.shape, v.dtype),
        scratch_shapes=[pltpu.SemaphoreType.DMA((7,)), pltpu.SemaphoreType.DMA((7,)), pltpu.SemaphoreType.DMA],
    )(v)


def _all_to_all(v, name):
    def body(v_ref, out_ref, send_sems, recv_sems, local_sem):
        x, y, c = lax.axis_index("x"), lax.axis_index("y"), lax.axis_index("c")
        mine_idx = 4 * x + 2 * y + c
        mine = pltpu.make_async_copy(v_ref.at[mine_idx], out_ref.at[mine_idx], local_sem)
        mine.start()
        sends, recvs = [], []
        for k in range(1, N_DEV):
            px = 1 - x if k & 4 else x
            py = 1 - y if k & 2 else y
            pc = 1 - c if k & 1 else c
            peer_idx = 4 * px + 2 * py + pc
            sems = dict(send_sem=send_sems.at[k - 1], recv_sem=recv_sems.at[k - 1], device_id=(px, py, pc),
                        device_id_type=MESH)
            sends.append(pltpu.make_async_remote_copy(src_ref=v_ref.at[peer_idx], dst_ref=out_ref.at[mine_idx], **sems))
            recvs.append(pltpu.make_async_remote_copy(src_ref=v_ref.at[peer_idx], dst_ref=out_ref.at[peer_idx], **sems))
        for cp in sends:
            cp.start()
        for cp in recvs:
            cp.wait_recv()
        for cp in sends:
            cp.wait_send()
        mine.wait()

    return pl.pallas_call(
        body, name=name, in_specs=[ANY], out_specs=ANY,
        out_shape=SDS(v.shape, v.dtype),
        scratch_shapes=[pltpu.SemaphoreType.DMA((7,)), pltpu.SemaphoreType.DMA((7,)), pltpu.SemaphoreType.DMA],
    )(v)


HBM_SPEC = pl.BlockSpec(memory_space=pltpu.HBM)
SEM_SPEC = pl.BlockSpec(memory_space=pltpu.SEMAPHORE)
EFFECT = pltpu.SideEffectType.DATAFLOW_SIDE_EFFECTING


EXCHANGE_PEERS = {"gather": range(1, N_DEV), "scatter": range(1, N_DEV), "chip": (1, 2, 4, 6), "pass": (2, 4, 6)}


def _exchange_copies(srcs, lands, send_sems, recv_sems, mode, layer):
    x, y, c = lax.axis_index("x"), lax.axis_index("y"), lax.axis_index("c")
    me = 4 * x + 2 * y + c
    copies = []
    for a, (src, land) in enumerate(zip(srcs, lands)):
        for k in EXCHANGE_PEERS[mode]:
            px = 1 - x if k & 4 else x
            py = 1 - y if k & 2 else y
            pc = 1 - c if k & 1 else c
            peer = 4 * px + 2 * py + pc
            if mode == "scatter":
                s, d, to = src.at[peer], land.at[me, layer], (px, py, pc)
            elif mode == "pass":
                s, d, to = land.at[peer], land.at[peer], (x, y, 1 - c)
            else:
                s, d, to = src, land.at[me], (px, py, pc)
            n = 7 * a + k - 1
            copies.append(pltpu.make_async_remote_copy(
                src_ref=s, dst_ref=d, send_sem=send_sems.at[n], recv_sem=recv_sems.at[n], device_id=to,
                device_id_type=MESH))
    return copies


def _exchange_start(name, srcs, lands, mode, layer=0, after=None):
    n = len(srcs)

    def body(*refs):
        send_sems, recv_sems = refs[-2 * n - 3], refs[-2 * n - 2]
        for cp in _exchange_copies(refs[:n], refs[n:2 * n], send_sems, recv_sems, mode, layer):
            cp.start()
        refs[-1][...] = jnp.zeros_like(refs[-1])

    arrays = list(srcs) + list(lands)
    sems = pltpu.SemaphoreType.DMA((7 * n,))
    out = pl.pallas_call(
        body, name=name,
        out_shape=(sems, sems, *[pltpu.HBM(v.shape, v.dtype) for v in arrays], SDS((8, LANE), F32)),
        in_specs=[HBM_SPEC] * (2 * n) + ([ANY] if after is not None else []),
        out_specs=(SEM_SPEC, SEM_SPEC, *[HBM_SPEC] * (2 * n), pl.BlockSpec(memory_space=pltpu.VMEM)),
        input_output_aliases={i: 2 + i for i in range(2 * n)},
        compiler_params=pltpu.CompilerParams(has_side_effects=EFFECT),
    )(*[pltpu.with_memory_space_constraint(v, pltpu.HBM) for v in arrays], *([after] if after is not None else []))
    return dict(sems=out[:2], srcs=out[2:2 + n], lands=out[2 + n:2 + 2 * n], token=out[-1], mode=mode,
                layer=layer)


def _exchange_wait(name, st, after, also=()):
    n = len(st["srcs"])

    def body(*refs):
        send_sems, recv_sems = refs[2 * n], refs[2 * n + 1]
        for cp in _exchange_copies(refs[:n], refs[n:2 * n], send_sems, recv_sems, st["mode"], st["layer"]):
            cp.wait_send()
            cp.wait_recv()

    arrays = list(st["srcs"]) + list(st["lands"])
    out = pl.pallas_call(
        body, name=name,
        out_shape=tuple(pltpu.HBM(v.shape, v.dtype) for v in arrays),
        in_specs=[HBM_SPEC] * (2 * n) + [SEM_SPEC, SEM_SPEC] + [ANY] * (1 + len(also)),
        out_specs=tuple([HBM_SPEC] * (2 * n)),
        input_output_aliases={i: i for i in range(2 * n)},
        compiler_params=pltpu.CompilerParams(has_side_effects=EFFECT),
    )(*arrays, *st["sems"], after, *also)
    st["srcs"] = out[:n]
    return out[n:]


_IN_PIECES = ([(1024, 3072)]
              + [r for t in range(4) for r in ((LANE * t, LANE * (t + 1)), (512 + LANE * t, 512 + LANE * (t + 1)))]
              + [(4096, 5632), (3072, 4096), (5632, 5648)])


def _permute_in(w):
    pad = jnp.zeros(w.shape[:-1] + (N_PAD - N_IN,), w.dtype)
    return jnp.concatenate([w[..., a:b] for a, b in _IN_PIECES] + [pad], axis=-1)


def _unpermute_in(g):
    ax = [g[..., OFF_LRU + 2 * LANE * t:OFF_LRU + 2 * LANE * t + LANE] for t in range(4)]
    ag = [g[..., OFF_LRU + 2 * LANE * t + LANE:OFF_LRU + 2 * LANE * (t + 1)] for t in range(4)]
    return jnp.concatenate(ax + ag + [g[..., 0:2048], g[..., OFF_Z:OFF_Z + SSD_W], g[..., OFF_XBC:OFF_XBC + SSD_CONV],
                                      g[..., OFF_Z + SSD_W:OFF_Z + SSD_W + SSD_HEADS]], axis=-1)


SHARD_COLS = N_IN // N_DEV


def _in_segments():
    segs, pos = [], 0
    for a, b in _IN_PIECES:
        for i in range(N_DEV):
            lo, hi = max(a, SHARD_COLS * i), min(b, SHARD_COLS * (i + 1))
            if lo < hi:
                segs.append((i, lo - SHARD_COLS * i, hi - lo, pos + lo - a))
        pos += b - a
    return segs


RELAYOUT_ROWS = 256


def _relayout_in(land, own):
    def body(land_ref, own_ref, out_ref):
        me = 4 * lax.axis_index("x") + 2 * lax.axis_index("y") + lax.axis_index("c")
        out_ref[:, N_IN:N_PAD] = jnp.zeros((RELAYOUT_ROWS, N_PAD - N_IN), BF16)
        for i, j, wd, p in _in_segments():
            out_ref[:, p:p + wd] = jnp.where(me == i, own_ref[:, j:j + wd], land_ref[i, :, j:j + wd])

    return pl.pallas_call(
        body, name="relayout_in", grid=(D_MODEL // RELAYOUT_ROWS,),
        in_specs=[pl.BlockSpec((N_DEV, RELAYOUT_ROWS, SHARD_COLS), lambda r: (0, r, 0)),
                  pl.BlockSpec((RELAYOUT_ROWS, SHARD_COLS), lambda r: (r, 0))],
        out_specs=pl.BlockSpec((RELAYOUT_ROWS, N_PAD), lambda r: (r, 0)),
        out_shape=SDS((D_MODEL, N_PAD), BF16),
        compiler_params=_cp(("parallel",)),
    )(land, own)


def _relayout_grad(g):
    def body(g_ref, out_ref):
        for i, j, wd, p in _in_segments():
            out_ref[i, :, j:j + wd] = g_ref[:, p:p + wd].astype(BF16)

    return pl.pallas_call(
        body, name="relayout_grad", grid=(D_MODEL // RELAYOUT_ROWS,),
        in_specs=[pl.BlockSpec((RELAYOUT_ROWS, N_PAD), lambda r: (r, 0))],
        out_specs=pl.BlockSpec((N_DEV, RELAYOUT_ROWS, SHARD_COLS), lambda r: (0, r, 0)),
        out_shape=SDS((N_DEV, D_MODEL, SHARD_COLS), BF16),
        compiler_params=_cp(("parallel",)),
    )(g)


def _block_diag(w):
    w4 = w.reshape(DEPTH, 4, 2, 64, 64)
    z = jnp.zeros((DEPTH, 4, 64, 64), w.dtype)
    top = jnp.concatenate([w4[:, :, 0], z], axis=-1)
    bot = jnp.concatenate([z, w4[:, :, 1]], axis=-1)
    return jnp.concatenate([top, bot], axis=2).astype(BF16)


def _diag_blocks(g):
    return jnp.stack([g[:, :, :64, :64], g[:, :, 64:, 64:]], axis=2).reshape(DEPTH, 8, 64, 64)


def _pad_lanes(v):
    return jnp.pad(v, ((0, 0), (0, LANE - v.shape[1])))


def _lower_bounds(logits):
    p = jax.nn.softmax(logits, axis=0)
    return p, jnp.cumsum(p, axis=0) - p[0]


def _lower_bounds_bwd(p, dlb):
    dp = jnp.cumsum(dlb[::-1], axis=0)[::-1]
    dp = dp.at[0].add(-jnp.sum(dlb, axis=0))
    return p * (dp - jnp.sum(dp * p, axis=0, keepdims=True))


SMALL = ["norm_w", "b_ada", "lru_conv_b", "lru_wa", "lru_ba", "lru_wx", "lru_bx", "lru_lambda", "hg_lb_logits",
         "hg_norm_w", "ssd_conv_b", "ssd_dt_bias", "ssd_a_log", "ssd_d", "ssd_norm_w", "final_norm_w"]
WEIGHTS = ["norm_w", "w_ada", "b_ada", "w_in", "lru_conv_w", "lru_conv_b", "lru_wa", "lru_ba", "lru_wx", "lru_bx",
           "lru_lambda", "hg_lb_logits", "hg_norm_w", "ssd_conv_w", "ssd_conv_b", "ssd_dt_bias", "ssd_a_log", "ssd_d",
           "ssd_norm_w", "w_out", "final_norm_w"]
INPUTS = ["x", "c"] + WEIGHTS + ["loss_target"] + ["m_" + n for n in WEIGHTS] + ["v_" + n for n in WEIGHTS]
SMALL_ROW = 1024


def _small_rows(like):
    out, off = {}, 0
    for n in SMALL:
        rows = -(-int(np.prod(like[n].shape)) // (8 * SMALL_ROW)) * 8
        out[n] = (off, rows)
        off += rows
    return out, off


def _flatten_small(d, prefix="", last=0.0):
    table, _ = _small_rows({n: d[prefix + n] for n in SMALL})
    pieces = []
    for n in SMALL:
        flat = d[prefix + n].reshape(-1)
        pieces.append(jnp.pad(flat, (0, table[n][1] * SMALL_ROW - flat.shape[0])).reshape(-1, SMALL_ROW))
    return jnp.concatenate(pieces + [jnp.full((8, SMALL_ROW), last, F32)], axis=0)


def _split_small(packed, like):
    table, _ = _small_rows(like)
    out = {}
    for n in SMALL:
        off, rows = table[n]
        size = int(np.prod(like[n].shape))
        out[n] = packed[off:off + rows].reshape(-1)[:size].reshape(like[n].shape)
    return out


def _local_step(x, mod, target, w, fetch, emit):
    S = x.shape[0]
    mall = _bfc(_hg_consts())
    mall_t = _bfc(_hg_consts().T)
    consts = _ssd_consts()
    p_lb, lbs = _lower_bounds(w["hg_lb_logits"])
    no_tok = jnp.zeros((8, LANE), F32)
    wa, wx = _block_diag(w["lru_wa"]), _block_diag(w["lru_wx"])
    ba, bx = w["lru_ba"].reshape(DEPTH, 1, LRU_W), w["lru_bx"].reshape(DEPTH, 1, LRU_W)
    lru_cb, lam, ssd_cb = w["lru_conv_b"][:, None], w["lru_lambda"][:, None], w["ssd_conv_b"][:, None]
    bias, alog = _pad_lanes(w["ssd_dt_bias"]), _pad_lanes(w["ssd_a_log"])
    dskip = jnp.repeat(w["ssd_d"], SSD_P, axis=1)
    saved = []
    for l in range(DEPTH):
        w_in_l, w_out_l, token = fetch(l, x)
        shift, scale, gate = (_Row(mod, l, D_MODEL, k) for k in range(3))
        nw = _Row(w["norm_w"], l)
        u, h = _inproj_fwd(x, nw, scale, shift, w_in_l, no_tok if token is None else token)
        ycat = lax.empty((S, D_INNER), BF16)
        lru_args = (l, u, w["lru_conv_w"], lru_cb, wa, ba, wx, bx, lam)
        ycat, h_lru = _lru_fwd(*lru_args, ycat)
        hg_args = (u, _Row(lbs, l), _Row(w["hg_norm_w"], l), mall)
        ycat, o_b, hg_st = _hg_fwd(*hg_args, ycat)
        xbc = _ssdconv_fwd(l, u, w["ssd_conv_w"], ssd_cb)
        ssd_args = (u, xbc, _Row(bias, l), _Row(alog, l), _Row(dskip, l), _Row(w["ssd_norm_w"], l), consts)
        ycat, y_ssd, ssd_st = _ssd_fwd(*ssd_args, ycat)
        x_new, y = _outproj_fwd(ycat, w_out_l, x, gate)
        saved.append((x, u, h, ycat, nw, scale, gate, w_in_l, w_out_l, lru_args, h_lru, hg_args, o_b, hg_st, ssd_args,
                      y_ssd, ssd_st, y))
        x = x_new
    dx, red = _loss_head(x, w["final_norm_w"][None, :], target)
    loss = red[1, 0]
    reds = {k: [None] * DEPTH for k in ("in", "gate", "lru", "wa", "wx", "hg", "conv", "ssd")}
    for l in reversed(range(DEPTH)):
        (x, u, h, ycat, nw, scale, gate, w_in_l, w_out_l, lru_args, h_lru, hg_args, o_b, hg_st, ssd_args, y_ssd, ssd_st,
         y) = saved[l]
        dycat, g_out, reds["gate"][l] = _outproj_bwd(dx, y, gate, ycat, w_out_l)
        token = emit(l, "w_out", g_out)
        du = lax.empty((S, N_PAD), BF16)
        du, dxbc, reds["ssd"][l] = _ssd_bwd(*ssd_args, y_ssd, ssd_st, dycat, du, no_tok if token is None else token)
        du, reds["conv"][l] = _ssdconv_bwd(l, u, w["ssd_conv_w"], ssd_cb, dxbc, du)
        du, reds["hg"][l] = _hg_bwd(*hg_args, mall_t, o_b, hg_st, dycat, du)
        du, reds["lru"][l], reds["wa"][l], reds["wx"][l] = _lru_bwd(*lru_args, h_lru, dycat, du)
        token = emit(l, "w_in", _inproj_bwd_w(h, du))
        dx, reds["in"][l] = _inproj_bwd_x(du, w_in_l, x, nw, scale, dx, no_tok if token is None else token)
    r = {k: jnp.stack(v) for k, v in reds.items()}
    g = {n: None for n in WEIGHTS}
    g["final_norm_w"] = red[0]
    g["norm_w"] = r["in"][:, 2]
    dmod = jnp.concatenate([r["in"][:, 0], r["in"][:, 1], r["gate"][:, 0]], axis=1)
    g["lru_conv_w"], g["lru_conv_b"] = r["lru"][:, 0:4], r["lru"][:, 4]
    g["lru_ba"], g["lru_bx"] = r["lru"][:, 5].reshape(DEPTH, 8, 64), r["lru"][:, 6].reshape(DEPTH, 8, 64)
    g["lru_lambda"] = r["lru"][:, 7]
    g["lru_wa"], g["lru_wx"] = _diag_blocks(r["wa"]), _diag_blocks(r["wx"])
    g["hg_norm_w"] = r["hg"][:, 0]
    g["hg_lb_logits"] = _lower_bounds_bwd(p_lb, r["hg"][:, 1])
    g["ssd_conv_w"], g["ssd_conv_b"] = r["conv"][:, 0:4], r["conv"][:, 4]
    g["ssd_norm_w"] = r["ssd"][:, 0]
    g["ssd_d"] = r["ssd"][:, 1].reshape(DEPTH, SSD_HEADS, SSD_P).sum(-1)
    g["ssd_dt_bias"] = r["ssd"][:, 2, :SSD_HEADS]
    g["ssd_a_log"] = -r["ssd"][:, 3, :SSD_HEADS] * jnp.exp(w["ssd_a_log"])
    return loss, dx, dmod, g


def kernel(x, c, norm_w, w_ada, b_ada, w_in, lru_conv_w, lru_conv_b, lru_wa, lru_ba, lru_wx, lru_bx, lru_lambda, hg_lb_logits, hg_norm_w, ssd_conv_w, ssd_conv_b, ssd_dt_bias, ssd_a_log, ssd_d, ssd_norm_w, w_out, final_norm_w, loss_target, m_norm_w, m_w_ada, m_b_ada, m_w_in, m_lru_conv_w, m_lru_conv_b, m_lru_wa, m_lru_ba, m_lru_wx, m_lru_bx, m_lru_lambda, m_hg_lb_logits, m_hg_norm_w, m_ssd_conv_w, m_ssd_conv_b, m_ssd_dt_bias, m_ssd_a_log, m_ssd_d, m_ssd_norm_w, m_w_out, m_final_norm_w, v_norm_w, v_w_ada, v_b_ada, v_w_in, v_lru_conv_w, v_lru_conv_b, v_lru_wa, v_lru_ba, v_lru_wx, v_lru_bx, v_lru_lambda, v_hg_lb_logits, v_hg_norm_w, v_ssd_conv_w, v_ssd_conv_b, v_ssd_dt_bias, v_ssd_a_log, v_ssd_d, v_ssd_norm_w, v_w_out, v_final_norm_w):
    return _step(x, c, norm_w, w_ada, b_ada, w_in, lru_conv_w, lru_conv_b, lru_wa, lru_ba, lru_wx, lru_bx, lru_lambda, hg_lb_logits, hg_norm_w, ssd_conv_w, ssd_conv_b, ssd_dt_bias, ssd_a_log, ssd_d, ssd_norm_w, w_out, final_norm_w, loss_target, m_norm_w, m_w_ada, m_b_ada, m_w_in, m_lru_conv_w, m_lru_conv_b, m_lru_wa, m_lru_ba, m_lru_wx, m_lru_bx, m_lru_lambda, m_hg_lb_logits, m_hg_norm_w, m_ssd_conv_w, m_ssd_conv_b, m_ssd_dt_bias, m_ssd_a_log, m_ssd_d, m_ssd_norm_w, m_w_out, m_final_norm_w, v_norm_w, v_w_ada, v_b_ada, v_w_in, v_lru_conv_w, v_lru_conv_b, v_lru_wa, v_lru_ba, v_lru_wx, v_lru_bx, v_lru_lambda, v_hg_lb_logits, v_hg_norm_w, v_ssd_conv_w, v_ssd_conv_b, v_ssd_dt_bias, v_ssd_a_log, v_ssd_d, v_ssd_norm_w, v_w_out, v_final_norm_w)


def _step(*args):
    a = dict(zip(INPUTS, args, strict=True))
    me = 4 * lax.axis_index("x") + 2 * lax.axis_index("y") + lax.axis_index("c")
    x, target = a["x"][0], a["loss_target"][0]

    c_all = _all_gather(a["c"], "gather_c")[:, 0, :]
    b_cols = lax.dynamic_slice_in_dim(a["b_ada"], me * ADA_COLS, ADA_COLS, axis=1)[:, None, :]
    mod_parts = _all_gather(_ada_fwd(c_all, a["w_ada"], b_cols), "gather_mod")
    mod = lax.dynamic_index_in_dim(mod_parts, me, axis=2, keepdims=False)
    mod = mod.transpose(1, 0, 2).reshape(DEPTH, 3 * D_MODEL)

    w = {n: a[n] for n in SMALL}

    w_in_b = [a["w_in"][l].astype(BF16) for l in range(DEPTH)]
    w_out_b = a["w_out"].astype(BF16)
    conv_own = jnp.concatenate([a["lru_conv_w"], a["ssd_conv_w"]], axis=-1)
    cols, rows_out = N_IN // N_DEV, D_INNER // N_DEV

    def gather_start(l, after):
        srcs = [w_in_b[l], w_out_b[l]] + ([conv_own] if l == 0 else [])
        lands = [lax.empty((N_DEV,) + s.shape, s.dtype) for s in srcs]
        return _exchange_start(f"gather_start_{l}", srcs, lands, "chip" if l == 0 else "gather", after=after)

    def gather_pass(name, st, after, also=()):
        landed = _exchange_wait(name + "_wait", st, after, also)
        st2 = _exchange_start(name + "_pass", st["srcs"], landed, "pass")
        return _exchange_wait(name + "_passed", st2, after)

    gathers = {0: gather_start(0, mod)}

    def fetch(l, x_l):
        if l == 0:
            landed = gather_pass("gather_0", gathers[0], x_l, also=(a["w_in"], a["m_w_in"], a["v_w_in"]))
        else:
            landed = _exchange_wait(f"gather_wait_{l}", gathers[l], x_l)
        land_out = lax.dynamic_update_index_in_dim(landed[1], w_out_b[l], me, 0)
        if l == 0:
            conv = lax.dynamic_update_index_in_dim(landed[2], conv_own, me, 0).transpose(1, 2, 0, 3)
            w["lru_conv_w"] = conv[..., :64].reshape(DEPTH, 4, LRU_W)
            w["ssd_conv_w"] = conv[..., 64:].reshape(DEPTH, 4, SSD_CONV)
        token = None
        if l + 1 < DEPTH:
            gathers[l + 1] = gather_start(l + 1, land_out)
            token = gathers[l + 1]["token"]
        return _relayout_in(landed[0], w_in_b[l]), land_out.reshape(D_INNER, D_MODEL), token

    scatters = {"w_in": {}, "w_out": {}}
    lands = {"w_in": lax.empty((N_DEV, DEPTH, D_MODEL, cols), BF16),
             "w_out": lax.empty((N_DEV, DEPTH, rows_out, D_MODEL), BF16)}
    own = {"w_in": [None] * DEPTH, "w_out": [None] * DEPTH}

    def emit(l, name, grad):
        grad = _relayout_grad(grad) if name == "w_in" else grad.reshape(N_DEV, rows_out, D_MODEL)
        st = _exchange_start(f"scatter_start_{name}_{l}", [grad], [lands[name]], "scatter", layer=l)
        scatters[name][l] = st
        lands[name] = st["lands"][0]
        return st["token"]

    loss_own, dx, dmod, g = _local_step(x, mod, target, w, fetch, emit)

    def sharded(name, parts, own=None, **kw):
        return _adamw(parts, a[name], a["m_" + name], a["v_" + name], "adamw_" + name + kw.pop("tag", ""), own=own, **kw)

    g["b_ada"] = dmod
    small_own = _flatten_small(g, last=loss_own)
    small_st = _exchange_start("gather_small", [small_own], [lax.empty((N_DEV,) + small_own.shape, F32)], "chip",
                               after=dx)
    big = {}
    after = small_st["token"] + dx[0:8, 0:LANE]
    def own_slice(st):
        return lax.dynamic_index_in_dim(st["srcs"][0], me, 0, keepdims=False)

    for name in ("w_out", "w_in"):
        for l in reversed(range(1, DEPTH)):
            scatters[name][l]["lands"] = [lands[name]]
            lands[name] = _exchange_wait(f"scatter_wait_{name}_{l}", scatters[name][l], after)[0]
            own[name][l] = own_slice(scatters[name][l])
        upper = sharded(name, lands[name], jnp.stack(own[name][1:]), layers=(1, DEPTH), tag="_upper")
        scatters[name][0]["lands"] = [lands[name]]
        lands[name] = _exchange_wait(f"scatter_wait_{name}_0", scatters[name][0], upper[1])[0]
        big[name] = sharded(name, lands[name], own_slice(scatters[name][0])[None], layers=(0, 1), prev=upper)
        after = big[name][1]
    small = gather_pass("gather_small", small_st, after)[0]
    outs = _adamw(small[:, None], *[_flatten_small(a, p)[None] for p in ("", "m_", "v_")], "adamw_small",
                  own=small_own[None])
    res = [_split_small(o[0], a) for o in outs]
    losses = lax.dynamic_update_index_in_dim(small[:, -1, 0], loss_own, me, 0)
    loss = jnp.sum(losses)

    off = _small_rows(a)[0]["b_ada"][0]
    dmod_all = lax.dynamic_update_index_in_dim(small[:, off:off + DEPTH * 3 * D_MODEL // SMALL_ROW],
                                               dmod.reshape(-1, SMALL_ROW), me, 0)
    dmod_all = dmod_all.reshape(N_DEV, DEPTH, 3 * D_MODEL).transpose(1, 0, 2)
    dmod_cols = lax.dynamic_slice_in_dim(dmod_all, me * ADA_COLS, ADA_COLS, axis=2)
    dmod_pad = jnp.pad(dmod_cols, ((0, 0), (0, LANE - N_DEV), (0, 0)))
    ct_pad = jnp.pad(c_all.T, ((0, 0), (0, LANE - N_DEV)))
    big["w_ada"] = sharded("w_ada", _ada_bwd(ct_pad, dmod_pad)[None])
    g_conv = jnp.concatenate([g["lru_conv_w"].reshape(DEPTH, 4, N_DEV, 64), g["ssd_conv_w"].reshape(DEPTH, 4, N_DEV, 192)],
                             axis=-1).transpose(2, 0, 1, 3)
    conv_parts = _all_to_all(g_conv, "scatter_conv")
    big["lru_conv_w"] = sharded("lru_conv_w", conv_parts[..., :64])
    big["ssd_conv_w"] = sharded("ssd_conv_w", conv_parts[..., 64:])

    out = [loss, dx[None]]
    for k in range(4):
        out += [big[n][k] if n in big else res[k][n] for n in WEIGHTS]
    return tuple(out)
```

```python
import functools

import numpy as np
import jax
import jax.numpy as jnp
from jax import lax
from jax.experimental import pallas as pl
from jax.experimental.pallas import tpu as pltpu

F32 = jnp.float32
BF16 = jnp.bfloat16
SDS = jax.ShapeDtypeStruct

N_DEV = 8
DEPTH = 4
D_MODEL = 1024
D_INNER = 2048
EPS = 1e-6
LRU_W = 512
LRU_C = 8.0
HG_W = 512
HG_CHUNK = 64
HG_HEADS = 4
SSD_W = 1024
SSD_HEADS = 16
SSD_P = 64
SSD_N = 128
SSD_CHUNK = 128
SSD_CONV = 1536
N_IN = 5648
N_PAD = 5760
OFF_HG = 0
OFF_LRU = 2048
OFF_XBC = 3072
OFF_Z = 4608
LANE = 128
VMEM_LIMIT = 56 * 1024 * 1024
NEG = -1e30

ADAM_LR = 0.001
ADAM_B1 = 0.9
ADAM_B2 = 0.999
ADAM_EPS = 1e-08
ADAM_WD = 0.01
ADAM_STEP = 10


def _cp(sem=None):
    return pltpu.CompilerParams(dimension_semantics=sem, vmem_limit_bytes=VMEM_LIMIT)


def _dg(a, b, ca, cb):
    return lax.dot_general(a, b, (((ca,), (cb,)), ((), ())), preferred_element_type=F32)


def _mm(a, b):
    return _dg(a, b, 1, 0)


def _mm_nt(a, b):
    return _dg(a, b, 1, 1)


def _mm_tn(a, b):
    return _dg(a, b, 0, 0)


def _bf(x):
    return x.astype(BF16)


def _f(x):
    return x.astype(F32)


def _split3(x):
    hi = x.astype(BF16)
    r = x - hi.astype(F32)
    mid = r.astype(BF16)
    lo = (r - mid.astype(F32)).astype(BF16)
    return hi, mid, lo


def _sel_r(x, m):
    hi, mid, lo = _split3(x)
    return _mm(hi, m) + _mm(mid, m) + _mm(lo, m)


def _sel_l(m, x):
    hi, mid, lo = _split3(x)
    return _mm(m, hi) + _mm(m, mid) + _mm(m, lo)


def _sel_l2(m, x):
    hi = x.astype(BF16)
    lo = (x - hi.astype(F32)).astype(BF16)
    return _mm(m, hi) + _mm(m, lo)


def _sel_tn(x, m):
    hi, mid, lo = _split3(x)
    return _mm_tn(hi, m) + _mm_tn(mid, m) + _mm_tn(lo, m)


def _sigmoid(x):
    return 1.0 / (1.0 + jnp.exp(-x))


def _silu(x):
    return x * _sigmoid(x)


def _dsilu(x):
    s = _sigmoid(x)
    return s * (1.0 + x * (1.0 - s))


def _softplus(x):
    return jnp.maximum(x, 0.0) + jnp.log(1.0 + jnp.exp(-jnp.abs(x)))


def _expm1(z):
    series = z * (1.0 + z * (1.0 / 2) * (1.0 + z * (1.0 / 3) * (1.0 + z * (1.0 / 4) * (
        1.0 + z * (1.0 / 5) * (1.0 + z * (1.0 / 6) * (1.0 + z * (1.0 / 7)))))))
    return jnp.where(jnp.abs(z) < 0.3, series, jnp.exp(z) - 1.0)


def _iota(shape, dim):
    return lax.broadcasted_iota(jnp.int32, shape, dim)


def _last_row(x, rows):
    return jnp.sum(jnp.where(rows == x.shape[0] - 1, x, 0.0), axis=0, keepdims=True)


def _shift_down(x, d, rows, fill=0.0):
    return jnp.where(rows >= d, pltpu.roll(x, d, 0), fill)


def _shift_up(x, d, rows, fill=0.0):
    n = x.shape[0]
    return jnp.where(rows < n - d, pltpu.roll(x, n - d, 0), fill)


def _conv_fwd(x, cw_ref, cb_ref, rows):
    out = cb_ref[...] + cw_ref[pl.ds(3, 1), :] * x
    for k in range(3):
        out = out + cw_ref[pl.ds(k, 1), :] * _shift_down(x, 3 - k, rows)
    return out


def _conv_bwd(x, dco, cw_ref, rows):
    dx = cw_ref[pl.ds(3, 1), :] * dco
    dws = []
    for k in range(3):
        dx = dx + cw_ref[pl.ds(k, 1), :] * _shift_up(dco, 3 - k, rows)
        dws.append(jnp.sum(dco * _shift_down(x, 3 - k, rows), axis=0, keepdims=True))
    dws.append(jnp.sum(dco * x, axis=0, keepdims=True))
    return dx, dws, jnp.sum(dco, axis=0, keepdims=True)


def _vec(n):
    return pl.BlockSpec((1, n), lambda *_: (0, 0))


class _Row:
    def __init__(self, arr, l, n=None, c=0):
        self.arr, self.l, self.n, self.c = arr[:, None, :], l, n or arr.shape[1], c


def _spec(v):
    if isinstance(v, _Row):
        return pl.BlockSpec((None, 1, v.n), lambda *_: (v.l, 0, v.c))
    return _vec(v.shape[1])


def _arr(v):
    return v.arr if isinstance(v, _Row) else v


def _full(shape):
    nd = len(shape)
    return pl.BlockSpec(shape, lambda *_: (0,) * nd)


def _inproj_fwd(x, nw, scale, shift, w, tok):
    S = x.shape[0]
    tm = min(256, S)

    def body(x_ref, nw_ref, sc_ref, sh_ref, w_ref, tok_ref, u_ref, h_ref):
        del tok_ref
        xv = x_ref[...]
        inv = lax.rsqrt(jnp.mean(xv * xv, axis=-1, keepdims=True) + EPS)
        h = ((xv * inv) * nw_ref[...] * (1.0 + sc_ref[...]) + sh_ref[...]).astype(BF16)
        h_ref[...] = h
        u_ref[...] = _mm(h, w_ref[...])

    return pl.pallas_call(
        body, name="inproj_fwd", grid=(S // tm,),
        in_specs=[pl.BlockSpec((tm, D_MODEL), lambda i: (i, 0)), _spec(nw), _spec(scale), _spec(shift),
                  _full((D_MODEL, N_PAD)), pl.BlockSpec(memory_space=pl.ANY)],
        out_specs=[pl.BlockSpec((tm, N_PAD), lambda i: (i, 0)), pl.BlockSpec((tm, D_MODEL), lambda i: (i, 0))],
        out_shape=[SDS((S, N_PAD), F32), SDS((S, D_MODEL), BF16)],
        compiler_params=_cp(("parallel",)),
    )(x, _arr(nw), _arr(scale), _arr(shift), w, tok)


def _inproj_bwd_x(du, w, x, nw, scale, dxn, tok):
    S = x.shape[0]
    tm = min(256, S)

    def body(du_ref, w_ref, x_ref, nw_ref, sc_ref, dxn_ref, tok_ref, dx_ref, red_ref):
        del tok_ref

        @pl.when(pl.program_id(0) == 0)
        def _():
            red_ref[...] = jnp.zeros_like(red_ref)

        dh = _mm_nt(du_ref[...], w_ref[...])
        xv = x_ref[...]
        inv = lax.rsqrt(jnp.mean(xv * xv, axis=-1, keepdims=True) + EPS)
        xhat = xv * inv
        nwv = nw_ref[...]
        g1 = 1.0 + sc_ref[...]
        dxhat = dh * nwv * g1
        dx = inv * (dxhat - xhat * jnp.mean(dxhat * xhat, axis=-1, keepdims=True))
        dx_ref[...] = dxn_ref[...] + dx
        red_ref[0:1, :] += jnp.sum(dh, axis=0, keepdims=True)
        red_ref[1:2, :] += jnp.sum(dh * xhat * nwv, axis=0, keepdims=True)
        red_ref[2:3, :] += jnp.sum(dh * xhat * g1, axis=0, keepdims=True)

    row = pl.BlockSpec((tm, D_MODEL), lambda i: (i, 0))
    return pl.pallas_call(
        body, name="inproj_bwd_x", grid=(S // tm,),
        in_specs=[pl.BlockSpec((tm, N_PAD), lambda i: (i, 0)), _full((D_MODEL, N_PAD)), row, _spec(nw),
                  _spec(scale), row, pl.BlockSpec(memory_space=pl.ANY)],
        out_specs=[row, _full((8, D_MODEL))],
        out_shape=[SDS((S, D_MODEL), F32), SDS((8, D_MODEL), F32)],
        compiler_params=_cp(("arbitrary",)),
    )(du, w, x, _arr(nw), _arr(scale), dxn, tok)


def _inproj_bwd_w(h, du):
    S = h.shape[0]
    tn = 640

    def body(h_ref, du_ref, gw_ref):
        gw_ref[...] = _mm_tn(h_ref[...], _bf(du_ref[...]))

    return pl.pallas_call(
        body, name="inproj_bwd_w", grid=(N_PAD // tn,),
        in_specs=[_full((S, D_MODEL)), pl.BlockSpec((S, tn), lambda j: (0, j))],
        out_specs=pl.BlockSpec((D_MODEL, tn), lambda j: (0, j)),
        out_shape=SDS((D_MODEL, N_PAD), F32),
        compiler_params=_cp(("parallel",)),
    )(h, du)


def _scan_block(a, b, rows):
    d = 1
    while d < a.shape[0]:
        a_s = _shift_down(a, d, rows, 1.0)
        b_s = _shift_down(b, d, rows, 0.0)
        b = a * b_s + b
        a = a * a_s
        d *= 2
    return a, b


def _rscan_block(c, g, rows):
    d = 1
    while d < c.shape[0]:
        c_s = _shift_up(c, d, rows, 1.0)
        g_s = _shift_up(g, d, rows, 0.0)
        g = g + c * g_s
        c = c * c_s
        d *= 2
    return c, g


LRU_BLOCK = 256


def _lru_gates(xa, wa_ref, ba_ref, wx_ref, bx_ref, lam_ref):
    sp = _softplus(-lam_ref[...])
    xb = _bf(xa)
    r = _sigmoid(_mm(xb, wa_ref[...]) + ba_ref[...])
    ig = _sigmoid(_mm(xb, wx_ref[...]) + bx_ref[...])
    la = -LRU_C * r * sp
    a = jnp.exp(la)
    mult = jnp.sqrt(-_expm1(2.0 * la))
    return sp, r, ig, la, a, mult


def _lru_specs(S, l):
    t128 = pl.BlockSpec((None, 1, LANE), lambda t: (l, 0, t))
    gate = pl.BlockSpec((None, None, LANE, LANE), lambda t: (l, t, 0, 0))
    return [pl.BlockSpec((S, 2 * LANE), lambda t: (0, OFF_LRU // (2 * LANE) + t)),
            pl.BlockSpec((None, 4, LANE), lambda t: (l, 0, t)), t128, gate, t128, gate, t128, t128]


def _lru_fwd(l, u, cw, cb, wa, ba, wx, bx, lam, ycat):
    S = u.shape[0]
    tb = min(LRU_BLOCK, S)

    def body(u_ref, cw_ref, cb_ref, wa_ref, ba_ref, wx_ref, bx_ref, lam_ref, ycat_in, ycat_ref, h_ref, a_scr, b_scr):
        del ycat_in
        rows = _iota((S, LANE), 0)
        xa = _conv_fwd(_f(u_ref[:, 0:LANE]), cw_ref, cb_ref, rows)
        _, _, ig, _, a, mult = _lru_gates(xa, wa_ref, ba_ref, wx_ref, bx_ref, lam_ref)
        a_scr[...] = a
        b_scr[...] = mult * (ig * xa)
        rows_b = _iota((tb, LANE), 0)

        def blk(j, hprev):
            sl = pl.ds(pl.multiple_of(j * tb, tb), tb)
            acum, hloc = _scan_block(a_scr[sl, :], b_scr[sl, :], rows_b)
            hf = hloc + acum * hprev
            h_ref[sl, :] = hf
            return _last_row(hf, rows_b)

        lax.fori_loop(0, S // tb, blk, jnp.zeros((1, LANE), F32))
        ycat_ref[...] = _bf(h_ref[...] * _silu(_f(u_ref[:, LANE:2 * LANE])))

    col = pl.BlockSpec((S, LANE), lambda t: (0, t))
    return pl.pallas_call(
        body, name="lru_fwd", grid=(LRU_W // LANE,),
        in_specs=_lru_specs(S, l) + [pl.BlockSpec(memory_space=pl.ANY)],
        out_specs=[col, col],
        out_shape=[SDS((S, D_INNER), BF16), SDS((S,LRU_W), F32)],
        scratch_shapes=[pltpu.VMEM((S, LANE), F32), pltpu.VMEM((S, LANE), F32)],
        input_output_aliases={8: 0},
        compiler_params=_cp(("parallel",)),
    )(u, cw, cb, wa, ba, wx, bx, lam, ycat)


def _lru_bwd(l, u, cw, cb, wa, ba, wx, bx, lam, h_lru, dycat, du):
    S = u.shape[0]
    tb = min(LRU_BLOCK, S)

    def body(u_ref, cw_ref, cb_ref, wa_ref, ba_ref, wx_ref, bx_ref, lam_ref, h_ref, dy_ref, du_in,
             du_ref, red_ref, gwa_ref, gwx_ref, c_scr, g_scr, l_scr):
        del du_in
        rows = _iota((S, LANE), 0)
        ax = _f(u_ref[:, 0:LANE])
        ag = _f(u_ref[:, LANE:2 * LANE])
        xa = _conv_fwd(ax, cw_ref, cb_ref, rows)
        sp, r, ig, la, a, mult = _lru_gates(xa, wa_ref, ba_ref, wx_ref, bx_ref, lam_ref)
        h = h_ref[...]
        dy = _f(dy_ref[...])
        du_ref[:, LANE:2 * LANE] = _bf(dy * h * _dsilu(ag))
        c_scr[...] = _shift_up(a, 1, rows, 0.0)
        g_scr[...] = dy * _silu(ag)
        rows_b = _iota((tb, LANE), 0)
        nb = S // tb

        def blk(jj, lnext):
            j = nb - 1 - jj
            sl = pl.ds(pl.multiple_of(j * tb, tb), tb)
            ccum, lloc = _rscan_block(c_scr[sl, :], g_scr[sl, :], rows_b)
            lam_t = lloc + ccum * lnext
            l_scr[sl, :] = lam_t
            return jnp.sum(jnp.where(rows_b == 0, lam_t, 0.0), axis=0, keepdims=True)

        lax.fori_loop(0, nb, blk, jnp.zeros((1, LANE), F32))
        db = l_scr[...]
        da = db * _shift_down(h, 1, rows)
        dmult = db * ig * xa
        dig = db * mult * xa
        dxa = db * mult * ig
        dla = da * a - dmult * (a * a) / mult
        dr = -LRU_C * sp * dla
        dsp = jnp.sum(-LRU_C * r * dla, axis=0, keepdims=True)
        dlam = -dsp * _sigmoid(-lam_ref[...])
        dzr = dr * r * (1.0 - r)
        dzi = dig * ig * (1.0 - ig)
        dzr_b, dzi_b, xa_b = _bf(dzr), _bf(dzi), _bf(xa)
        dxa = dxa + _mm_nt(dzr_b, wa_ref[...]) + _mm_nt(dzi_b, wx_ref[...])
        gwa_ref[...] = _mm_tn(xa_b, dzr_b)
        gwx_ref[...] = _mm_tn(xa_b, dzi_b)
        dax, dws, dcb = _conv_bwd(ax, dxa, cw_ref, rows)
        du_ref[:, 0:LANE] = _bf(dax)
        parts = dws + [dcb, jnp.sum(dzr, axis=0, keepdims=True), jnp.sum(dzi, axis=0, keepdims=True), dlam]
        for n, p in enumerate(parts):
            red_ref[pl.ds(n, 1), :] = p

    col = pl.BlockSpec((S, LANE), lambda t: (0, t))
    gw = pl.BlockSpec((None, LANE, LANE), lambda t: (t, 0, 0))
    return pl.pallas_call(
        body, name="lru_bwd", grid=(LRU_W // LANE,),
        in_specs=_lru_specs(S, l) + [col, col, pl.BlockSpec(memory_space=pl.ANY)],
        out_specs=[pl.BlockSpec((S, 2 * LANE), lambda t: (0, OFF_LRU // (2 * LANE) + t)),
                   pl.BlockSpec((8, LANE), lambda t: (0, t)), gw, gw],
        out_shape=[SDS((S, N_PAD), BF16), SDS((8, LRU_W), F32), SDS((4, LANE, LANE), F32), SDS((4, LANE, LANE), F32)],
        scratch_shapes=[pltpu.VMEM((S, LANE), F32)] * 3,
        input_output_aliases={10: 0},
        compiler_params=_cp(("parallel",)),
    )(u, cw, cb, wa, ba, wx, bx, lam, h_lru, dycat, du)


HG_LEVELS = 6


def _hg_consts():
    C = HG_CHUNK
    t = np.arange(C)[:, None]
    r = np.arange(C)[None, :]
    mats = []
    for l in range(HG_LEVELS):
        b = 1 << l
        upper = (t % (2 * b)) >= b
        anchor = (t // (2 * b)) * 2 * b + b - 1
        mats.append((upper & (r > anchor) & (r <= t)) | ((~upper) & (r > t) & (r <= anchor)))
    mats.append(r <= t)
    mats.append(r > t)
    return np.concatenate(mats, 0).astype(np.float32)


def _hg_factors(hf, lb, mall):
    s = _sigmoid(hf)
    f = lb + (1.0 - lb) * s
    lf = jnp.log(f)
    k = (1.0 - lb) * _sigmoid(-hf)
    e = jnp.exp(_sel_l(mall, lf))
    C = HG_CHUNK
    eq = [e[l * C:(l + 1) * C] for l in range(HG_LEVELS)]
    ecum = e[HG_LEVELS * C:(HG_LEVELS + 1) * C]
    erem = e[(HG_LEVELS + 1) * C:(HG_LEVELS + 2) * C]
    return s, f, k, eq, eq, ecum, erem


def _hg_masks():
    C = HG_CHUNK
    ri, ci = _iota((C, C), 0), _iota((C, C), 1)
    rr = _iota((C, LANE), 0)
    gm = [(lax.shift_right_logical(ri, l + 1) == lax.shift_right_logical(ci, l + 1)).astype(F32)
          for l in range(HG_LEVELS)]
    up = [(lax.shift_right_logical(rr, l) & 1) == 1 for l in range(HG_LEVELS)]
    eye = (ri == ci).astype(F32)
    return gm, up, eye, rr


def _hg_scores(qh, kh, eq, ek, sl, gm, up, eye):
    qs, ks = [], []
    p = _mm_nt(_bf(qh), _bf(kh)) * eye
    for l in range(HG_LEVELS):
        ql = jnp.where(up[l], qh * eq[l][:, sl], 0.0)
        kl = jnp.where(up[l], 0.0, kh * ek[l][:, sl])
        p = p + _mm_nt(_bf(ql), _bf(kl)) * gm[l]
        qs.append(ql)
        ks.append(kl)
    return p, qs, ks


HG_SUB = 4


def _hg_fwd(u, lb, nw, mall, ycat):
    S = u.shape[0]
    C = HG_CHUNK
    n = S // C
    rows = HG_SUB * C

    def body(u_ref, lb_ref, nw_ref, mall_ref, ycat_in, ycat_ref, o_ref, st_ref, st):
        del ycat_in

        @pl.when(pl.program_id(0) == 0)
        def _():
            st[...] = jnp.zeros_like(st)

        gm, up, eye, rr = _hg_masks()
        for sub in range(HG_SUB):
            r = slice(sub * C, (sub + 1) * C)
            q = _silu(_f(u_ref[r, 0:512]))
            v = u_ref[r, 1024:1536]
            _, _, k, eq, ek, ecum, erem = _hg_factors(_f(u_ref[r, 512:1024]), lb_ref[...], mall_ref[...])
            for h in range(HG_HEADS):
                sl = slice(h * LANE, (h + 1) * LANE)
                qh, kh, vh = q[:, sl], k[:, sl], _bf(v[:, sl])
                p, _, _ = _hg_scores(qh, kh, eq, ek, sl, gm, up, eye)
                sth = st[h]
                st_ref[sub, h] = sth
                o_ref[r, sl] = _mm(_bf(p), vh) + _mm_nt(_bf(qh * ecum[:, sl]), _bf(sth))
                st[h] = sth * _last_row(ecum[:, sl], rr) + _mm_tn(vh, _bf(kh * erem[:, sl]))
            o = o_ref[r, :]
            inv = lax.rsqrt(jnp.mean(o * o, axis=-1, keepdims=True) + EPS)
            ycat_ref[r, :] = _bf((o * inv) * nw_ref[...] * _silu(_f(u_ref[r, 1536:2048])))

    return pl.pallas_call(
        body, name="hg_fwd", grid=(n // HG_SUB,),
        in_specs=[pl.BlockSpec((rows, 2048), lambda i: (i, 0)), _spec(lb), _spec(nw), _full(mall.shape),
                  pl.BlockSpec(memory_space=pl.ANY)],
        out_specs=[pl.BlockSpec((rows, HG_W), lambda i: (i, 1)), pl.BlockSpec((rows, HG_W), lambda i: (i, 0)),
                   pl.BlockSpec((HG_SUB, HG_HEADS, LANE, LANE), lambda i: (i, 0, 0, 0))],
        out_shape=[SDS((S, D_INNER), BF16), SDS((S,HG_W), F32), SDS((n, HG_HEADS, LANE, LANE), F32)],
        scratch_shapes=[pltpu.VMEM((HG_HEADS, LANE, LANE), F32)],
        input_output_aliases={4: 0},
        compiler_params=_cp(("arbitrary",)),
    )(u, _arr(lb), _arr(nw), mall, ycat)


def _hg_bwd(u, lb, nw, mall, mall_t, o_b, states, dycat, du):
    S = u.shape[0]
    C = HG_CHUNK
    n = S // C
    nb = n // HG_SUB
    rows = HG_SUB * C
    L2 = HG_LEVELS

    def body(u_ref, lb_ref, nw_ref, mall_ref, mallt_ref, o_ref, st_ref, dy_ref, du_in, du_ref, red_ref,
             dst, dlast_s, dq_s, dk_s, dex):
        del du_in

        @pl.when(pl.program_id(0) == 0)
        def _():
            dst[...] = jnp.zeros_like(dst)
            red_ref[...] = jnp.zeros_like(red_ref)

        lb = lb_ref[...]
        nwv = nw_ref[...]
        gm, up, eye, rr = _hg_masks()
        for sub in reversed(range(HG_SUB)):
            r = slice(sub * C, (sub + 1) * C)
            hq, hf, hg = _f(u_ref[r, 0:512]), _f(u_ref[r, 512:1024]), _f(u_ref[r, 1536:2048])
            q = _silu(hq)
            v = u_ref[r, 1024:1536]
            s, f, k, eq, ek, ecum, erem = _hg_factors(hf, lb, mall_ref[...])
            o = o_ref[r, :]
            dy = _f(dy_ref[r, :])
            inv = lax.rsqrt(jnp.mean(o * o, axis=-1, keepdims=True) + EPS)
            ohat = o * inv
            du_ref[r, 1536:2048] = _bf(dy * ohat * nwv * _dsilu(hg))
            dn = dy * _silu(hg)
            red_ref[0:1, :] += jnp.sum(dn * ohat, axis=0, keepdims=True)
            dohat = dn * nwv
            do = inv * (dohat - ohat * jnp.mean(dohat * ohat, axis=-1, keepdims=True))
            for h in range(HG_HEADS):
                sl = slice(h * LANE, (h + 1) * LANE)
                qh, kh, vh, doh = q[:, sl], k[:, sl], _bf(v[:, sl]), _bf(do[:, sl])
                p, qs, ks = _hg_scores(qh, kh, eq, ek, sl, gm, up, eye)
                st_f = st_ref[sub, h]
                sth = _bf(st_f)
                dsth = dst[h]
                dsth_b = _bf(dsth)
                qt = qh * ecum[:, sl]
                kt = kh * erem[:, sl]
                elast = _last_row(ecum[:, sl], rr)
                dp = _mm_nt(doh, vh)
                du_ref[r, 1024 + h * LANE:1024 + (h + 1) * LANE] = _bf(_mm_tn(_bf(p), doh) + _mm_nt(_bf(kt), dsth_b))
                dpe = _bf(dp * eye)
                dqt = _mm(doh, sth)
                dkt = _mm(vh, dsth_b)
                dq = dqt * ecum[:, sl] + _mm(dpe, _bf(kh))
                dk = dkt * erem[:, sl] + _mm_tn(dpe, _bf(qh))
                dex[sub, L2 * C:(L2 + 1) * C, sl] = dqt * qt
                dex[sub, (L2 + 1) * C:(L2 + 2) * C, sl] = dkt * kt
                for l in range(HG_LEVELS):
                    dpl = _bf(dp * gm[l])
                    dql = _mm(dpl, _bf(ks[l]))
                    dkl = _mm_tn(dpl, _bf(qs[l]))
                    dq = dq + jnp.where(up[l], dql * eq[l][:, sl], 0.0)
                    dk = dk + jnp.where(up[l], 0.0, dkl * ek[l][:, sl])
                    dex[sub, l * C:(l + 1) * C, sl] = dql * qs[l] + dkl * ks[l]
                dlast_s[sub, :, sl] = jnp.sum(dsth * st_f, axis=0, keepdims=True) * elast
                dst[h] = dsth * elast + _mm_tn(doh, _bf(qt))
                dq_s[sub, :, sl] = dq
                dk_s[sub, :, sl] = dk
            dq = dq_s[sub]
            dk = dk_s[sub]
            dlf = _sel_l2(mallt_ref[...], dex[sub]) + dlast_s[sub]
            du_ref[r, 0:512] = _bf(dq * _dsilu(hq))
            t = (1.0 - s) * (dlf / f - dk)
            du_ref[r, 512:1024] = _bf((1.0 - lb) * s * t)
            red_ref[1:2, :] += jnp.sum(t, axis=0, keepdims=True)

    rev = lambda i: (nb - 1 - i, 0)
    return pl.pallas_call(
        body, name="hg_bwd", grid=(nb,),
        in_specs=[pl.BlockSpec((rows, 2048), rev), _spec(lb), _spec(nw), _full(mall.shape), _full(mall_t.shape),
                  pl.BlockSpec((rows, HG_W), rev),
                  pl.BlockSpec((HG_SUB, HG_HEADS, LANE, LANE), lambda i: (nb - 1 - i, 0, 0, 0)),
                  pl.BlockSpec((rows, HG_W), lambda i: (nb - 1 - i, 1)), pl.BlockSpec(memory_space=pl.ANY)],
        out_specs=[pl.BlockSpec((rows, 2048), rev), pl.BlockSpec((8, HG_W), lambda i: (0, 0))],
        out_shape=[SDS((S, N_PAD), BF16), SDS((8, HG_W), F32)],
        scratch_shapes=[pltpu.VMEM((HG_HEADS, LANE, LANE), F32), pltpu.VMEM((HG_SUB, 1, HG_W), F32),
                        pltpu.VMEM((HG_SUB, C, HG_W), F32), pltpu.VMEM((HG_SUB, C, HG_W), F32),
                        pltpu.VMEM((HG_SUB, (L2 + 2) * C, HG_W), F32)],
        input_output_aliases={8: 0},
        compiler_params=_cp(("arbitrary",)),
    )(u, _arr(lb), _arr(nw), mall, mall_t, o_b, states, dycat, du)


def _ssdconv_fwd(l, u, cw, cb):
    S = u.shape[0]

    def body(u_ref, cw_ref, cb_ref, out_ref):
        rows = _iota((S, LANE), 0)
        out_ref[...] = _silu(_conv_fwd(_f(u_ref[...]), cw_ref, cb_ref, rows))

    return pl.pallas_call(
        body, name="ssdconv_fwd", grid=(SSD_CONV // LANE,),
        in_specs=[pl.BlockSpec((S, LANE), lambda t: (0, OFF_XBC // LANE + t)),
                  pl.BlockSpec((None, 4, LANE), lambda t: (l, 0, t)), pl.BlockSpec((None, 1, LANE), lambda t: (l, 0, t))],
        out_specs=pl.BlockSpec((S, LANE), lambda t: (0, t)),
        out_shape=SDS((S, SSD_CONV), F32),
        compiler_params=_cp(("parallel",)),
    )(u, cw, cb)


def _ssdconv_bwd(l, u, cw, cb, dxbc, du):
    S = u.shape[0]

    def body(u_ref, cw_ref, cb_ref, d_ref, du_in, du_ref, red_ref):
        del du_in
        rows = _iota((S, LANE), 0)
        x = _f(u_ref[...])
        dco = d_ref[...] * _dsilu(_conv_fwd(x, cw_ref, cb_ref, rows))
        dx, dws, dcb = _conv_bwd(x, dco, cw_ref, rows)
        du_ref[...] = _bf(dx)
        for n, p in enumerate(dws + [dcb]):
            red_ref[pl.ds(n, 1), :] = p
        red_ref[pl.ds(5, 3), :] = jnp.zeros((3, LANE), F32)

    ucol = pl.BlockSpec((S, LANE), lambda t: (0, OFF_XBC // LANE + t))
    return pl.pallas_call(
        body, name="ssdconv_bwd", grid=(SSD_CONV // LANE,),
        in_specs=[ucol, pl.BlockSpec((None, 4, LANE), lambda t: (l, 0, t)),
                  pl.BlockSpec((None, 1, LANE), lambda t: (l, 0, t)),
                  pl.BlockSpec((S, LANE), lambda t: (0, t)), pl.BlockSpec(memory_space=pl.ANY)],
        out_specs=[ucol, pl.BlockSpec((8, LANE), lambda t: (0, t))],
        out_shape=[SDS((S, N_PAD), BF16), SDS((8, SSD_CONV), F32)],
        input_output_aliases={4: 0},
        compiler_params=_cp(("parallel",)),
    )(u, cw, cb, dxbc, du)


def _ssd_consts():
    e64 = np.zeros((LANE, SSD_W), np.float32)
    for h in range(SSD_HEADS):
        e64[h, h * SSD_P:(h + 1) * SSD_P] = 1.0
    T = SSD_CHUNK
    tril = (np.arange(T)[None, :] <= np.arange(T)[:, None]).astype(np.float32)
    return e64, tril, tril.T.copy()


def _ssd_common(zdt, bias_ref, alog_ref, tril, e64, cum_ref, cumt_ref):
    T = SSD_CHUNK
    lane = _iota((1, LANE), 1)
    a_neg = jnp.where(lane < SSD_HEADS, -jnp.exp(alog_ref[...]), 0.0)
    dtpre = zdt[:, SSD_W:SSD_W + LANE] + bias_ref[...]
    dt = _softplus(dtpre)
    cum = _sel_l(tril, dt * a_neg)
    cum_ref[...] = cum
    cumt_ref[...] = cum.T
    cum_x = _sel_r(cum, e64)
    last_x = _last_row(cum_x, _iota((T, SSD_W), 0))
    ecum_x = jnp.exp(cum_x)
    erem_x = jnp.exp(last_x - cum_x)
    elast_x = jnp.exp(last_x)
    dt_x = _sel_r(dt, e64)
    return a_neg, dtpre, dt, ecum_x, erem_x, elast_x, dt_x


def _ssd_decay(cum_ref, cumt_ref, h, causal):
    T = SSD_CHUNK
    diff = jnp.broadcast_to(cum_ref[:, pl.ds(h, 1)], (T, T)) - cumt_ref[pl.ds(h, 1), :]
    return jnp.exp(jnp.where(causal, diff, NEG))


def _group_norm_fwd(y1, nwv):
    outs, invs = [], []
    for g in range(2):
        seg = y1[:, g * 512:(g + 1) * 512]
        inv = lax.rsqrt(jnp.mean(seg * seg, axis=-1, keepdims=True) + EPS)
        outs.append(seg * inv * nwv[:, g * 512:(g + 1) * 512])
        invs.append(inv)
    return outs, invs


def _ssd_fwd(u, xbc, bias, alog, dskip_x, nw, consts, ycat):
    S = u.shape[0]
    T = SSD_CHUNK
    n = S // T
    e64, tril, _ = consts

    def body(u_ref, xbc_ref, bias_ref, alog_ref, dx_ref, nw_ref, e64_ref, tril_ref, ycat_in,
             ycat_ref, y_ref, st_ref, st, cumt, cum_e):
        del ycat_in

        @pl.when(pl.program_id(0) == 0)
        def _():
            st[...] = jnp.zeros_like(st)

        zdt = _f(u_ref[...])
        z = zdt[:, 0:SSD_W]
        xs = xbc_ref[:, 0:SSD_W]
        _, _, _, ecum_x, erem_x, elast_x, dt_x = _ssd_common(
            zdt, bias_ref, alog_ref, tril_ref[...], e64_ref[...], cum_e, cumt)
        causal = _iota((T, T), 0) >= _iota((T, T), 1)
        lo = _iota((T, LANE), 1) < SSD_P
        xdt = xs * dt_x
        xrem = xdt * erem_x
        st_ref[...] = st[...]
        for g in range(2):
            gs = slice(g * 512, (g + 1) * 512)
            bg = _bf(xbc_ref[:, SSD_W + g * LANE:SSD_W + (g + 1) * LANE])
            cg = _bf(xbc_ref[:, SSD_W + 256 + g * LANE:SSD_W + 256 + (g + 1) * LANE])
            cb = _mm_nt(cg, bg)
            yin = _mm(cg, _bf(st[:, gs])) * ecum_x[:, gs]
            for j in range(4):
                h0 = 8 * g + 2 * j
                cs = slice(h0 * SSD_P, (h0 + 2) * SSD_P)
                xp = xdt[:, cs]
                s0 = _bf(cb * _ssd_decay(cum_e, cumt, h0, causal))
                s1 = _bf(cb * _ssd_decay(cum_e, cumt, h0 + 1, causal))
                y_ref[:, cs] = (_mm(s0, _bf(jnp.where(lo, xp, 0.0))) + _mm(s1, _bf(jnp.where(lo, 0.0, xp)))
                                + yin[:, j * LANE:(j + 1) * LANE])
            st[:, gs] = st[:, gs] * elast_x[:, gs] + _mm_tn(bg, _bf(xrem[:, gs]))
        y1 = (y_ref[...] + dx_ref[...] * xs) * _silu(z)
        outs, _ = _group_norm_fwd(y1, nw_ref[...])
        for g in range(2):
            ycat_ref[:, g * 512:(g + 1) * 512] = _bf(outs[g])

    return pl.pallas_call(
        body, name="ssd_fwd", grid=(n,),
        in_specs=[pl.BlockSpec((T, SSD_W + LANE), lambda i: (i, OFF_Z // (SSD_W + LANE))),
                  pl.BlockSpec((T, SSD_CONV), lambda i: (i, 0)), _spec(bias), _spec(alog), _spec(dskip_x), _spec(nw),
                  _full(e64.shape), _full(tril.shape), pl.BlockSpec(memory_space=pl.ANY)],
        out_specs=[pl.BlockSpec((T, SSD_W), lambda i: (i, 1)), pl.BlockSpec((T, SSD_W), lambda i: (i, 0)),
                   pl.BlockSpec((None, SSD_N, SSD_W), lambda i: (i, 0, 0))],
        out_shape=[SDS((S, D_INNER), BF16), SDS((S,SSD_W), F32), SDS((n, SSD_N, SSD_W), F32)],
        scratch_shapes=[pltpu.VMEM((SSD_N, SSD_W), F32), pltpu.VMEM((LANE, T), F32), pltpu.VMEM((T, LANE), F32)],
        input_output_aliases={8: 0},
        compiler_params=_cp(("arbitrary",)),
    )(u, xbc, _arr(bias), _arr(alog), _arr(dskip_x), _arr(nw), _bfc(e64), _bfc(tril), ycat)


def _ssd_bwd(u, xbc, bias, alog, dskip_x, nw, consts, y_ssd, states, dycat, du, tok):
    S = u.shape[0]
    T = SSD_CHUNK
    n = S // T
    e64, tril, triu = consts
    e64t = np.ascontiguousarray(e64.T)

    def body(u_ref, xbc_ref, bias_ref, alog_ref, dx_ref, nw_ref, e64_ref, e64t_ref, tril_ref, triu_ref,
             y_ref, st_ref, dy_ref, du_in, tok_ref, du_ref, dxbc_ref, red_ref, dst, dl_s, cumt, dxdt_s, dy0_s, gb_s,
             gc_s, cum_e, cs_s):
        del du_in, tok_ref

        @pl.when(pl.program_id(0) == 0)
        def _():
            dst[...] = jnp.zeros_like(dst)
            red_ref[...] = jnp.zeros_like(red_ref)
            cs_s[...] = jnp.zeros_like(cs_s)

        zdt = _f(u_ref[...])
        z = zdt[:, 0:SSD_W]
        xs = xbc_ref[:, 0:SSD_W]
        a_neg, dtpre, dt, ecum_x, erem_x, elast_x, dt_x = _ssd_common(
            zdt, bias_ref, alog_ref, tril_ref[...], e64_ref[...], cum_e, cumt)
        causal = _iota((T, T), 0) >= _iota((T, T), 1)
        lo = _iota((T, LANE), 1) < SSD_P
        xdt = xs * dt_x
        xrem = xdt * erem_x
        y = y_ref[...]
        dxv = dx_ref[...]
        nwv = nw_ref[...]
        sz = _silu(z)
        y0 = y + dxv * xs
        y1 = y0 * sz
        for g in range(2):
            gs = slice(g * 512, (g + 1) * 512)
            seg = y1[:, gs]
            inv = lax.rsqrt(jnp.mean(seg * seg, axis=-1, keepdims=True) + EPS)
            shat = seg * inv
            dyg = _f(dy_ref[:, gs])
            red_ref[0:1, gs] += jnp.sum(dyg * shat, axis=0, keepdims=True)
            dsh = dyg * nwv[:, gs]
            dy1g = inv * (dsh - shat * jnp.mean(dsh * shat, axis=-1, keepdims=True))
            du_ref[:, gs] = _bf(dy1g * y0[:, gs] * _dsilu(z[:, gs]))
            dy0_s[:, gs] = dy1g * sz[:, gs]
        dy0 = dy0_s[...]
        red_ref[1:2, :] += jnp.sum(dy0 * xs, axis=0, keepdims=True)
        dyin = dy0 * ecum_x
        lane = _iota((T, LANE), 1)
        dcum = jnp.zeros((T, LANE), F32)

        def decay_grad(h, gm):
            cs_s[pl.ds(h, 1), :] = jnp.sum(gm, axis=0, keepdims=True)
            return jnp.where(lane == h, jnp.sum(gm, axis=1, keepdims=True), 0.0)

        for g in range(2):
            gs = slice(g * 512, (g + 1) * 512)
            bg = _bf(xbc_ref[:, SSD_W + g * LANE:SSD_W + (g + 1) * LANE])
            cg = _bf(xbc_ref[:, SSD_W + 256 + g * LANE:SSD_W + 256 + (g + 1) * LANE])
            cb = _mm_nt(cg, bg)
            dst_f, st_f = dst[:, gs], st_ref[:, gs]
            dstg = _bf(dst_f)
            stg = _bf(st_f)
            dyin_g = _bf(dyin[:, gs])
            xrem_g = _bf(xrem[:, gs])
            dcb = jnp.zeros((T, T), F32)
            dxr = _mm(bg, dstg)
            dxdt_s[:, gs] = dxr * erem_x[:, gs]
            gc_s[:, gs] = dxr * xrem[:, gs]
            gb_s[:, gs] = dyin[:, gs] * _mm(cg, stg)
            dl_s[:, gs] = jnp.sum(dst_f * st_f, axis=0, keepdims=True) * elast_x[:, gs]
            for j in range(4):
                h0 = 8 * g + 2 * j
                cs = slice(h0 * SSD_P, (h0 + 2) * SSD_P)
                xp = xdt[:, cs]
                dyp = dy0[:, cs]
                x_lo, x_hi = _bf(jnp.where(lo, xp, 0.0)), _bf(jnp.where(lo, 0.0, xp))
                d_lo, d_hi = _bf(jnp.where(lo, dyp, 0.0)), _bf(jnp.where(lo, 0.0, dyp))
                l0 = _ssd_decay(cum_e, cumt, h0, causal)
                l1 = _ssd_decay(cum_e, cumt, h0 + 1, causal)
                s0 = cb * l0
                s1 = cb * l1
                ds0 = _mm_nt(d_lo, x_lo)
                ds1 = _mm_nt(d_hi, x_hi)
                dcb = dcb + ds0 * l0 + ds1 * l1
                dxdt_s[:, cs] += _mm_tn(_bf(s0), d_lo) + _mm_tn(_bf(s1), d_hi)
                dcum = dcum + decay_grad(h0, ds0 * s0) + decay_grad(h0 + 1, ds1 * s1)
            dcb_b = _bf(dcb)
            dxbc_ref[:, SSD_W + g * LANE:SSD_W + (g + 1) * LANE] = _mm_tn(dcb_b, cg) + _mm_nt(xrem_g, dstg)
            dxbc_ref[:, SSD_W + 256 + g * LANE:SSD_W + 256 + (g + 1) * LANE] = _mm(dcb_b, bg) + _mm_nt(dyin_g, stg)
            dst[:, gs] = dst_f * elast_x[:, gs] + _mm_tn(cg, dyin_g)
        dxdt = dxdt_s[...]
        dxbc_ref[:, 0:SSD_W] = dxdt * dt_x + dy0 * dxv
        e64t = e64t_ref[...]
        gc = gc_s[...]
        dlast_x = jnp.sum(gc, axis=0, keepdims=True) + dl_s[...]
        dlast = jnp.max(_sel_r(jnp.broadcast_to(dlast_x, (8, SSD_W)), e64t), axis=0, keepdims=True)
        dcum = (dcum - cs_s[...].T + _sel_r(gb_s[...] - gc, e64t)
                + jnp.where(_iota((T, LANE), 0) == T - 1, dlast, 0.0))
        dda = _sel_l(triu_ref[...], dcum)
        ddt = dda * a_neg + _sel_r(dxdt * xs, e64t)
        ddtpre = ddt * _sigmoid(dtpre)
        du_ref[:, SSD_W:SSD_W + LANE] = _bf(jnp.where(lane < SSD_HEADS, ddtpre, 0.0))
        red_ref[2:3, 0:LANE] += jnp.sum(ddtpre, axis=0, keepdims=True)
        red_ref[3:4, 0:LANE] += jnp.sum(dda * dt, axis=0, keepdims=True)

    rev = lambda i: (n - 1 - i, 0)
    return pl.pallas_call(
        body, name="ssd_bwd", grid=(n,),
        in_specs=[pl.BlockSpec((T, SSD_W + LANE), lambda i: (n - 1 - i, OFF_Z // (SSD_W + LANE))),
                  pl.BlockSpec((T, SSD_CONV), rev), _spec(bias), _spec(alog), _spec(dskip_x), _spec(nw),
                  _full(e64.shape), _full(e64t.shape), _full(tril.shape), _full(triu.shape),
                  pl.BlockSpec((T, SSD_W), rev), pl.BlockSpec((None, SSD_N, SSD_W), lambda i: (n - 1 - i, 0, 0)),
                  pl.BlockSpec((T, SSD_W), lambda i: (n - 1 - i, 1)), pl.BlockSpec(memory_space=pl.ANY),
                  pl.BlockSpec(memory_space=pl.ANY)],
        out_specs=[pl.BlockSpec((T, SSD_W + LANE), lambda i: (n - 1 - i, OFF_Z // (SSD_W + LANE))),
                   pl.BlockSpec((T, SSD_CONV), rev), pl.BlockSpec((8, SSD_W), lambda i: (0, 0))],
        out_shape=[SDS((S, N_PAD), BF16), SDS((S, SSD_CONV), F32), SDS((8, SSD_W), F32)],
        scratch_shapes=[pltpu.VMEM((SSD_N, SSD_W), F32), pltpu.VMEM((1, SSD_W), F32), pltpu.VMEM((LANE, T), F32)]
        + [pltpu.VMEM((T, SSD_W), F32)] * 4 + [pltpu.VMEM((T, LANE), F32), pltpu.VMEM((LANE, T), F32)],
        input_output_aliases={13: 0},
        compiler_params=_cp(("arbitrary",)),
    )(u, xbc, _arr(bias), _arr(alog), _arr(dskip_x), _arr(nw), _bfc(e64), _bfc(e64t), _bfc(tril), _bfc(triu), y_ssd,
      states, dycat, du, tok)


def _bfc(a):
    return jnp.asarray(a, BF16)


def _outproj_fwd(ycat, wo, x, gate):
    S = x.shape[0]
    tm = min(512, S)

    def body(yc_ref, wo_ref, x_ref, g_ref, xn_ref, y_ref):
        y = _mm(_bf(yc_ref[...]), wo_ref[...])
        y_ref[...] = y
        xn_ref[...] = x_ref[...] + g_ref[...] * y

    row = pl.BlockSpec((tm, D_MODEL), lambda i: (i, 0))
    return pl.pallas_call(
        body, name="outproj_fwd", grid=(S // tm,),
        in_specs=[pl.BlockSpec((tm, D_INNER), lambda i: (i, 0)), _full((D_INNER, D_MODEL)), row, _spec(gate)],
        out_specs=[row, row],
        out_shape=[SDS((S, D_MODEL), F32), SDS((S, D_MODEL), F32)],
        compiler_params=_cp(("parallel",)),
    )(ycat, wo, x, _arr(gate))


def _outproj_bwd(dxn, y, gate, ycat, wo):
    S = dxn.shape[0]
    tm = min(512, S)

    def body(dx_ref, y_ref, g_ref, yc_ref, wo_ref, dyc_ref, gwo_ref, dg_ref, acc):
        @pl.when(pl.program_id(0) == 0)
        def _():
            acc[...] = jnp.zeros_like(acc)
            dg_ref[...] = jnp.zeros_like(dg_ref)

        dxv = dx_ref[...]
        dy = _bf(dxv * g_ref[...])
        dg_ref[0:1, :] += jnp.sum(dxv * y_ref[...], axis=0, keepdims=True)
        dyc_ref[...] = _mm_nt(dy, wo_ref[...])
        acc[...] += _mm_tn(_bf(yc_ref[...]), dy)

        @pl.when(pl.program_id(0) == pl.num_programs(0) - 1)
        def _():
            gwo_ref[...] = acc[...].astype(BF16)

    row = pl.BlockSpec((tm, D_MODEL), lambda i: (i, 0))
    wide = pl.BlockSpec((tm, D_INNER), lambda i: (i, 0))
    return pl.pallas_call(
        body, name="outproj_bwd", grid=(S // tm,),
        in_specs=[row, row, _spec(gate), wide, _full((D_INNER, D_MODEL))],
        out_specs=[wide, _full((D_INNER, D_MODEL)), _full((8, D_MODEL))],
        out_shape=[SDS((S, D_INNER), F32), SDS((D_INNER, D_MODEL), BF16), SDS((8, D_MODEL), F32)],
        scratch_shapes=[pltpu.VMEM((D_INNER, D_MODEL), F32)],
        compiler_params=_cp(("arbitrary",)),
    )(dxn, y, _arr(gate), ycat, wo)


def _loss_head(x, fw, target):
    S = x.shape[0]
    tm = min(512, S)

    def body(x_ref, fw_ref, t_ref, dx_ref, red_ref):
        @pl.when(pl.program_id(0) == 0)
        def _():
            red_ref[...] = jnp.zeros_like(red_ref)

        xv = x_ref[...]
        fwv = fw_ref[...]
        inv = lax.rsqrt(jnp.mean(xv * xv, axis=-1, keepdims=True) + EPS)
        xhat = xv * inv
        err = xhat * fwv - t_ref[...]
        col = jnp.sum(err * err, axis=0, keepdims=True)
        red_ref[1:2, :] += jnp.broadcast_to(jnp.sum(col, axis=1, keepdims=True) * (0.5 / D_MODEL), (1, D_MODEL))
        dy = err * (1.0 / D_MODEL)
        red_ref[0:1, :] += jnp.sum(dy * xhat, axis=0, keepdims=True)
        dxhat = dy * fwv
        dx_ref[...] = inv * (dxhat - xhat * jnp.mean(dxhat * xhat, axis=-1, keepdims=True))

    row = pl.BlockSpec((tm, D_MODEL), lambda i: (i, 0))
    return pl.pallas_call(
        body, name="loss_head", grid=(S // tm,),
        in_specs=[row, _vec(D_MODEL), row],
        out_specs=[row, _full((8, D_MODEL))],
        out_shape=[SDS((S, D_MODEL), F32), SDS((8, D_MODEL), F32)],
        compiler_params=_cp(("arbitrary",)),
    )(x, fw, target)


ADA_COLS = 3 * D_MODEL // N_DEV


def _ada_fwd(c_all, w_ada, b_cols):
    def body(c_ref, w_ref, b_ref, out_ref):
        out_ref[...] = _mm(_bf(_silu(c_ref[...])), _bf(w_ref[...])) + b_ref[...]

    return pl.pallas_call(
        body, name="ada_fwd", grid=(DEPTH,),
        in_specs=[_full((N_DEV, D_MODEL)), pl.BlockSpec((None, D_MODEL, ADA_COLS), lambda l: (l, 0, 0)),
                  pl.BlockSpec((None, 1, ADA_COLS), lambda l: (l, 0, 0))],
        out_specs=pl.BlockSpec((None, N_DEV, ADA_COLS), lambda l: (l, 0, 0)),
        out_shape=SDS((DEPTH, N_DEV, ADA_COLS), F32),
        compiler_params=_cp(("parallel",)),
    )(c_all, w_ada, b_cols)


def _ada_bwd(ct_pad, dmod_pad):
    def body(c_ref, d_ref, out_ref):
        out_ref[...] = _mm(_bf(_silu(c_ref[...])), _bf(d_ref[...]))

    return pl.pallas_call(
        body, name="ada_bwd", grid=(DEPTH,),
        in_specs=[_full((D_MODEL, LANE)), pl.BlockSpec((None, LANE, ADA_COLS), lambda l: (l, 0, 0))],
        out_specs=pl.BlockSpec((None, D_MODEL, ADA_COLS), lambda l: (l, 0, 0)),
        out_shape=SDS((DEPTH, D_MODEL, ADA_COLS), F32),
        compiler_params=_cp(("parallel",)),
    )(ct_pad, dmod_pad)


def _adamw(parts, w, m, v, name, own=None, layers=None, prev=None):
    n, L, R, C = parts.shape
    lo, hi = layers or (0, L)
    tr = R
    while tr * C * 4 > (1 << 20) and tr % 16 == 0:
        tr //= 2
    first = 1 if own is None else 2

    def body(*refs):
        p_ref = refs[0]
        w_ref, m_ref, v_ref = refs[first:first + 3]
        g_ref, d_ref, mo_ref, vo_ref = refs[-4:]

        def part(k):
            if own is None:
                return p_ref[k].astype(F32)
            me = 4 * lax.axis_index("x") + 2 * lax.axis_index("y") + lax.axis_index("c")
            return jnp.where(me == k, refs[1][...], p_ref[k]).astype(F32)

        g = part(0)
        for k in range(1, n):
            g = g + part(k)
        mn = ADAM_B1 * m_ref[...] + (1.0 - ADAM_B1) * g
        vn = ADAM_B2 * v_ref[...] + (1.0 - ADAM_B2) * (g * g)
        m_hat = mn / (1.0 - ADAM_B1 ** ADAM_STEP)
        v_hat = vn / (1.0 - ADAM_B2 ** ADAM_STEP)
        g_ref[...] = g
        d_ref[...] = -ADAM_LR * (m_hat / (jnp.sqrt(v_hat) + ADAM_EPS) + ADAM_WD * w_ref[...])
        mo_ref[...] = mn
        vo_ref[...] = vn

    blk = pl.BlockSpec((None, tr, C), lambda l, i: (lo + l, i, 0))
    own_blk = [] if own is None else [pl.BlockSpec((None, tr, C), lambda l, i: (l, i, 0))]
    n_blk = 3 if own is None else 4
    return pl.pallas_call(
        body, name=name, grid=(hi - lo, R // tr),
        in_specs=[pl.BlockSpec((n, None, tr, C), lambda l, i: (0, lo + l, i, 0))] + own_blk + [blk] * 3
        + ([] if prev is None else [ANY] * 4),
        out_specs=[blk] * 4,
        out_shape=[SDS((L, R, C), F32)] * 4,
        input_output_aliases={} if prev is None else {1 + n_blk + k: k for k in range(4)},
        compiler_params=_cp(("parallel", "parallel")),
    )(parts, *([] if own is None else [own]), w, m, v, *([] if prev is None else prev))


MESH = pl.DeviceIdType.MESH
ANY = pl.BlockSpec(memory_space=pl.ANY)


def _all_gather(v, name):
    def body(v_ref, out_ref, send_sems, recv_sems, local_sem):
        x, y, c = lax.axis_index("x"), lax.axis_index("y"), lax.axis_index("c")
        me, sibling = (x, y, c), (x, y, 1 - c)
        chips = [(1 - x, y), (x, 1 - y), (1 - x, 1 - y)]

        def slot(px, py, pc):
            return out_ref.at[4 * px + 2 * py + pc]

        def copy(k, block, to, src=None):
            return pltpu.make_async_remote_copy(
                src_ref=slot(*block) if src is None else src, dst_ref=slot(*block),
                send_sem=send_sems.at[k], recv_sem=recv_sems.at[k], device_id=to, device_id_type=MESH)

        mine = pltpu.make_async_copy(v_ref, slot(*me), local_sem)
        mine.start()
        first = [copy(0, me, sibling, src=v_ref)]
        first += [copy(1 + j, me, (*chip, c), src=v_ref) for j, chip in enumerate(chips)]
        for cp in first:
            cp.start()
        passed = [copy(4 + j, (*chip, c), sibling) for j, chip in enumerate(chips)]
        for j, chip in enumerate(chips):
            copy(1 + j, (*chip, c), me).wait_recv()
            passed[j].start()
        copy(0, sibling, me).wait_recv()
        for j, chip in enumerate(chips):
            copy(4 + j, (*chip, 1 - c), me).wait_recv()
        for cp in first + passed:
            cp.wait_send()
        mine.wait()

    return pl.pallas_call(
        body, name=name, in_specs=[ANY], out_specs=ANY,
        out_shape=SDS((N_DEV,) + v.shape, v.dtype),
        scratch_shapes=[pltpu.SemaphoreType.DMA((7,)), pltpu.SemaphoreType.DMA((7,)), pltpu.SemaphoreType.DMA],
    )(v)


def _all_to_all(v, name):
    def body(v_ref, out_ref, send_sems, recv_sems, local_sem):
        x, y, c = lax.axis_index("x"), lax.axis_index("y"), lax.axis_index("c")
        mine_idx = 4 * x + 2 * y + c
        mine = pltpu.make_async_copy(v_ref.at[mine_idx], out_ref.at[mine_idx], local_sem)
        mine.start()
        sends, recvs = [], []
        for k in range(1, N_DEV):
            px = 1 - x if k & 4 else x
            py = 1 - y if k & 2 else y
            pc = 1 - c if k & 1 else c
            peer_idx = 4 * px + 2 * py + pc
            sems = dict(send_sem=send_sems.at[k - 1], recv_sem=recv_sems.at[k - 1], device_id=(px, py, pc),
                        device_id_type=MESH)
            sends.append(pltpu.make_async_remote_copy(src_ref=v_ref.at[peer_idx], dst_ref=out_ref.at[mine_idx], **sems))
            recvs.append(pltpu.make_async_remote_copy(src_ref=v_ref.at[peer_idx], dst_ref=out_ref.at[peer_idx], **sems))
        for cp in sends:
            cp.start()
        for cp in recvs:
            cp.wait_recv()
        for cp in sends:
            cp.wait_send()
        mine.wait()

    return pl.pallas_call(
        body, name=name, in_specs=[ANY], out_specs=ANY,
        out_shape=SDS(v.shape, v.dtype),
        scratch_shapes=[pltpu.SemaphoreType.DMA((7,)), pltpu.SemaphoreType.DMA((7,)), pltpu.SemaphoreType.DMA],
    )(v)


HBM_SPEC = pl.BlockSpec(memory_space=pltpu.HBM)
SEM_SPEC = pl.BlockSpec(memory_space=pltpu.SEMAPHORE)
EFFECT = pltpu.SideEffectType.DATAFLOW_SIDE_EFFECTING


EXCHANGE_PEERS = {"gather": range(1, N_DEV), "scatter": range(1, N_DEV), "chip": (1, 2, 4, 6), "pass": (2, 4, 6)}


def _exchange_copies(srcs, lands, send_sems, recv_sems, mode, layer):
    x, y, c = lax.axis_index("x"), lax.axis_index("y"), lax.axis_index("c")
    me = 4 * x + 2 * y + c
    copies = []
    for a, (src, land) in enumerate(zip(srcs, lands)):
        for k in EXCHANGE_PEERS[mode]:
            px = 1 - x if k & 4 else x
            py = 1 - y if k & 2 else y
            pc = 1 - c if k & 1 else c
            peer = 4 * px + 2 * py + pc
            if mode == "scatter":
                s, d, to = src.at[peer], land.at[me, layer], (px, py, pc)
            elif mode == "pass":
                s, d, to = land.at[peer], land.at[peer], (x, y, 1 - c)
            else:
                s, d, to = src, land.at[me], (px, py, pc)
            n = 7 * a + k - 1
            copies.append(pltpu.make_async_remote_copy(
                src_ref=s, dst_ref=d, send_sem=send_sems.at[n], recv_sem=recv_sems.at[n], device_id=to,
                device_id_type=MESH))
    return copies


def _exchange_start(name, srcs, lands, mode, layer=0, after=None):
    n = len(srcs)

    def body(*refs):
        send_sems, recv_sems = refs[-2 * n - 3], refs[-2 * n - 2]
        for cp in _exchange_copies(refs[:n], refs[n:2 * n], send_sems, recv_sems, mode, layer):
            cp.start()
        refs[-1][...] = jnp.zeros_like(refs[-1])

    arrays = list(srcs) + list(lands)
    sems = pltpu.SemaphoreType.DMA((7 * n,))
    out = pl.pallas_call(
        body, name=name,
        out_shape=(sems, sems, *[pltpu.HBM(v.shape, v.dtype) for v in arrays], SDS((8, LANE), F32)),
        in_specs=[HBM_SPEC] * (2 * n) + ([ANY] if after is not None else []),
        out_specs=(SEM_SPEC, SEM_SPEC, *[HBM_SPEC] * (2 * n), pl.BlockSpec(memory_space=pltpu.VMEM)),
        input_output_aliases={i: 2 + i for i in range(2 * n)},
        compiler_params=pltpu.CompilerParams(has_side_effects=EFFECT),
    )(*[pltpu.with_memory_space_constraint(v, pltpu.HBM) for v in arrays], *([after] if after is not None else []))
    return dict(sems=out[:2], srcs=out[2:2 + n], lands=out[2 + n:2 + 2 * n], token=out[-1], mode=mode,
                layer=layer)


def _exchange_wait(name, st, after, also=()):
    n = len(st["srcs"])

    def body(*refs):
        send_sems, recv_sems = refs[2 * n], refs[2 * n + 1]
        for cp in _exchange_copies(refs[:n], refs[n:2 * n], send_sems, recv_sems, st["mode"], st["layer"]):
            cp.wait_send()
            cp.wait_recv()

    arrays = list(st["srcs"]) + list(st["lands"])
    out = pl.pallas_call(
        body, name=name,
        out_shape=tuple(pltpu.HBM(v.shape, v.dtype) for v in arrays),
        in_specs=[HBM_SPEC] * (2 * n) + [SEM_SPEC, SEM_SPEC] + [ANY] * (1 + len(also)),
        out_specs=tuple([HBM_SPEC] * (2 * n)),
        input_output_aliases={i: i for i in range(2 * n)},
        compiler_params=pltpu.CompilerParams(has_side_effects=EFFECT),
    )(*arrays, *st["sems"], after, *also)
    st["srcs"] = out[:n]
    return out[n:]


_IN_PIECES = ([(1024, 3072)]
              + [r for t in range(4) for r in ((LANE * t, LANE * (t + 1)), (512 + LANE * t, 512 + LANE * (t + 1)))]
              + [(4096, 5632), (3072, 4096), (5632, 5648)])


def _permute_in(w):
    pad = jnp.zeros(w.shape[:-1] + (N_PAD - N_IN,), w.dtype)
    return jnp.concatenate([w[..., a:b] for a, b in _IN_PIECES] + [pad], axis=-1)


def _unpermute_in(g):
    ax = [g[..., OFF_LRU + 2 * LANE * t:OFF_LRU + 2 * LANE * t + LANE] for t in range(4)]
    ag = [g[..., OFF_LRU + 2 * LANE * t + LANE:OFF_LRU + 2 * LANE * (t + 1)] for t in range(4)]
    return jnp.concatenate(ax + ag + [g[..., 0:2048], g[..., OFF_Z:OFF_Z + SSD_W], g[..., OFF_XBC:OFF_XBC + SSD_CONV],
                                      g[..., OFF_Z + SSD_W:OFF_Z + SSD_W + SSD_HEADS]], axis=-1)


SHARD_COLS = N_IN // N_DEV


def _in_segments():
    segs, pos = [], 0
    for a, b in _IN_PIECES:
        for i in range(N_DEV):
            lo, hi = max(a, SHARD_COLS * i), min(b, SHARD_COLS * (i + 1))
            if lo < hi:
                segs.append((i, lo - SHARD_COLS * i, hi - lo, pos + lo - a))
        pos += b - a
    return segs


RELAYOUT_ROWS = 256


def _relayout_in(land, own):
    def body(land_ref, own_ref, out_ref):
        me = 4 * lax.axis_index("x") + 2 * lax.axis_index("y") + lax.axis_index("c")
        out_ref[:, N_IN:N_PAD] = jnp.zeros((RELAYOUT_ROWS, N_PAD - N_IN), BF16)
        for i, j, wd, p in _in_segments():
            out_ref[:, p:p + wd] = jnp.where(me == i, own_ref[:, j:j + wd], land_ref[i, :, j:j + wd])

    return pl.pallas_call(
        body, name="relayout_in", grid=(D_MODEL // RELAYOUT_ROWS,),
        in_specs=[pl.BlockSpec((N_DEV, RELAYOUT_ROWS, SHARD_COLS), lambda r: (0, r, 0)),
                  pl.BlockSpec((RELAYOUT_ROWS, SHARD_COLS), lambda r: (r, 0))],
        out_specs=pl.BlockSpec((RELAYOUT_ROWS, N_PAD), lambda r: (r, 0)),
        out_shape=SDS((D_MODEL, N_PAD), BF16),
        compiler_params=_cp(("parallel",)),
    )(land, own)


def _relayout_grad(g):
    def body(g_ref, out_ref):
        for i, j, wd, p in _in_segments():
            out_ref[i, :, j:j + wd] = g_ref[:, p:p + wd].astype(BF16)

    return pl.pallas_call(
        body, name="relayout_grad", grid=(D_MODEL // RELAYOUT_ROWS,),
        in_specs=[pl.BlockSpec((RELAYOUT_ROWS, N_PAD), lambda r: (r, 0))],
        out_specs=pl.BlockSpec((N_DEV, RELAYOUT_ROWS, SHARD_COLS), lambda r: (0, r, 0)),
        out_shape=SDS((N_DEV, D_MODEL, SHARD_COLS), BF16),
        compiler_params=_cp(("parallel",)),
    )(g)


def _block_diag(w):
    w4 = w.reshape(DEPTH, 4, 2, 64, 64)
    z = jnp.zeros((DEPTH, 4, 64, 64), w.dtype)
    top = jnp.concatenate([w4[:, :, 0], z], axis=-1)
    bot = jnp.concatenate([z, w4[:, :, 1]], axis=-1)
    return jnp.concatenate([top, bot], axis=2).astype(BF16)


def _diag_blocks(g):
    return jnp.stack([g[:, :, :64, :64], g[:, :, 64:, 64:]], axis=2).reshape(DEPTH, 8, 64, 64)


def _pad_lanes(v):
    return jnp.pad(v, ((0, 0), (0, LANE - v.shape[1])))


def _lower_bounds(logits):
    p = jax.nn.softmax(logits, axis=0)
    return p, jnp.cumsum(p, axis=0) - p[0]


def _lower_bounds_bwd(p, dlb):
    dp = jnp.cumsum(dlb[::-1], axis=0)[::-1]
    dp = dp.at[0].add(-jnp.sum(dlb, axis=0))
    return p * (dp - jnp.sum(dp * p, axis=0, keepdims=True))


SMALL = ["norm_w", "b_ada", "lru_conv_b", "lru_wa", "lru_ba", "lru_wx", "lru_bx", "lru_lambda", "hg_lb_logits",
         "hg_norm_w", "ssd_conv_b", "ssd_dt_bias", "ssd_a_log", "ssd_d", "ssd_norm_w", "final_norm_w"]
WEIGHTS = ["norm_w", "w_ada", "b_ada", "w_in", "lru_conv_w", "lru_conv_b", "lru_wa", "lru_ba", "lru_wx", "lru_bx",
           "lru_lambda", "hg_lb_logits", "hg_norm_w", "ssd_conv_w", "ssd_conv_b", "ssd_dt_bias", "ssd_a_log", "ssd_d",
           "ssd_norm_w", "w_out", "final_norm_w"]
INPUTS = ["x", "c"] + WEIGHTS + ["loss_target"] + ["m_" + n for n in WEIGHTS] + ["v_" + n for n in WEIGHTS]
SMALL_ROW = 1024


def _small_rows(like):
    out, off = {}, 0
    for n in SMALL:
        rows = -(-int(np.prod(like[n].shape)) // (8 * SMALL_ROW)) * 8
        out[n] = (off, rows)
        off += rows
    return out, off


def _flatten_small(d, prefix="", last=0.0):
    table, _ = _small_rows({n: d[prefix + n] for n in SMALL})
    pieces = []
    for n in SMALL:
        flat = d[prefix + n].reshape(-1)
        pieces.append(jnp.pad(flat, (0, table[n][1] * SMALL_ROW - flat.shape[0])).reshape(-1, SMALL_ROW))
    return jnp.concatenate(pieces + [jnp.full((8, SMALL_ROW), last, F32)], axis=0)


def _split_small(packed, like):
    table, _ = _small_rows(like)
    out = {}
    for n in SMALL:
        off, rows = table[n]
        size = int(np.prod(like[n].shape))
        out[n] = packed[off:off + rows].reshape(-1)[:size].reshape(like[n].shape)
    return out


def _local_step(x, mod, target, w, fetch, emit):
    S = x.shape[0]
    mall = _bfc(_hg_consts())
    mall_t = _bfc(_hg_consts().T)
    consts = _ssd_consts()
    p_lb, lbs = _lower_bounds(w["hg_lb_logits"])
    no_tok = jnp.zeros((8, LANE), F32)
    wa, wx = _block_diag(w["lru_wa"]), _block_diag(w["lru_wx"])
    ba, bx = w["lru_ba"].reshape(DEPTH, 1, LRU_W), w["lru_bx"].reshape(DEPTH, 1, LRU_W)
    lru_cb, lam, ssd_cb = w["lru_conv_b"][:, None], w["lru_lambda"][:, None], w["ssd_conv_b"][:, None]
    bias, alog = _pad_lanes(w["ssd_dt_bias"]), _pad_lanes(w["ssd_a_log"])
    dskip = jnp.repeat(w["ssd_d"], SSD_P, axis=1)
    saved = []
    for l in range(DEPTH):
        w_in_l, w_out_l, token = fetch(l, x)
        shift, scale, gate = (_Row(mod, l, D_MODEL, k) for k in range(3))
        nw = _Row(w["norm_w"], l)
        u, h = _inproj_fwd(x, nw, scale, shift, w_in_l, no_tok if token is None else token)
        ycat = lax.empty((S, D_INNER), BF16)
        lru_args = (l, u, w["lru_conv_w"], lru_cb, wa, ba, wx, bx, lam)
        ycat, h_lru = _lru_fwd(*lru_args, ycat)
        hg_args = (u, _Row(lbs, l), _Row(w["hg_norm_w"], l), mall)
        ycat, o_b, hg_st = _hg_fwd(*hg_args, ycat)
        xbc = _ssdconv_fwd(l, u, w["ssd_conv_w"], ssd_cb)
        ssd_args = (u, xbc, _Row(bias, l), _Row(alog, l), _Row(dskip, l), _Row(w["ssd_norm_w"], l), consts)
        ycat, y_ssd, ssd_st = _ssd_fwd(*ssd_args, ycat)
        x_new, y = _outproj_fwd(ycat, w_out_l, x, gate)
        saved.append((x, u, h, ycat, nw, scale, gate, w_in_l, w_out_l, lru_args, h_lru, hg_args, o_b, hg_st, ssd_args,
                      y_ssd, ssd_st, y))
        x = x_new
    dx, red = _loss_head(x, w["final_norm_w"][None, :], target)
    loss = red[1, 0]
    reds = {k: [None] * DEPTH for k in ("in", "gate", "lru", "wa", "wx", "hg", "conv", "ssd")}
    for l in reversed(range(DEPTH)):
        (x, u, h, ycat, nw, scale, gate, w_in_l, w_out_l, lru_args, h_lru, hg_args, o_b, hg_st, ssd_args, y_ssd, ssd_st,
         y) = saved[l]
        dycat, g_out, reds["gate"][l] = _outproj_bwd(dx, y, gate, ycat, w_out_l)
        token = emit(l, "w_out", g_out)
        du = lax.empty((S, N_PAD), BF16)
        du, dxbc, reds["ssd"][l] = _ssd_bwd(*ssd_args, y_ssd, ssd_st, dycat, du, no_tok if token is None else token)
        du, reds["conv"][l] = _ssdconv_bwd(l, u, w["ssd_conv_w"], ssd_cb, dxbc, du)
        du, reds["hg"][l] = _hg_bwd(*hg_args, mall_t, o_b, hg_st, dycat, du)
        du, reds["lru"][l], reds["wa"][l], reds["wx"][l] = _lru_bwd(*lru_args, h_lru, dycat, du)
        token = emit(l, "w_in", _inproj_bwd_w(h, du))
        dx, reds["in"][l] = _inproj_bwd_x(du, w_in_l, x, nw, scale, dx, no_tok if token is None else token)
    r = {k: jnp.stack(v) for k, v in reds.items()}
    g = {n: None for n in WEIGHTS}
    g["final_norm_w"] = red[0]
    g["norm_w"] = r["in"][:, 2]
    dmod = jnp.concatenate([r["in"][:, 0], r["in"][:, 1], r["gate"][:, 0]], axis=1)
    g["lru_conv_w"], g["lru_conv_b"] = r["lru"][:, 0:4], r["lru"][:, 4]
    g["lru_ba"], g["lru_bx"] = r["lru"][:, 5].reshape(DEPTH, 8, 64), r["lru"][:, 6].reshape(DEPTH, 8, 64)
    g["lru_lambda"] = r["lru"][:, 7]
    g["lru_wa"], g["lru_wx"] = _diag_blocks(r["wa"]), _diag_blocks(r["wx"])
    g["hg_norm_w"] = r["hg"][:, 0]
    g["hg_lb_logits"] = _lower_bounds_bwd(p_lb, r["hg"][:, 1])
    g["ssd_conv_w"], g["ssd_conv_b"] = r["conv"][:, 0:4], r["conv"][:, 4]
    g["ssd_norm_w"] = r["ssd"][:, 0]
    g["ssd_d"] = r["ssd"][:, 1].reshape(DEPTH, SSD_HEADS, SSD_P).sum(-1)
    g["ssd_dt_bias"] = r["ssd"][:, 2, :SSD_HEADS]
    g["ssd_a_log"] = -r["ssd"][:, 3, :SSD_HEADS] * jnp.exp(w["ssd_a_log"])
    return loss, dx, dmod, g


def kernel(x, c, norm_w, w_ada, b_ada, w_in, lru_conv_w, lru_conv_b, lru_wa, lru_ba, lru_wx, lru_bx, lru_lambda, hg_lb_logits, hg_norm_w, ssd_conv_w, ssd_conv_b, ssd_dt_bias, ssd_a_log, ssd_d, ssd_norm_w, w_out, final_norm_w, loss_target, m_norm_w, m_w_ada, m_b_ada, m_w_in, m_lru_conv_w, m_lru_conv_b, m_lru_wa, m_lru_ba, m_lru_wx, m_lru_bx, m_lru_lambda, m_hg_lb_logits, m_hg_norm_w, m_ssd_conv_w, m_ssd_conv_b, m_ssd_dt_bias, m_ssd_a_log, m_ssd_d, m_ssd_norm_w, m_w_out, m_final_norm_w, v_norm_w, v_w_ada, v_b_ada, v_w_in, v_lru_conv_w, v_lru_conv_b, v_lru_wa, v_lru_ba, v_lru_wx, v_lru_bx, v_lru_lambda, v_hg_lb_logits, v_hg_norm_w, v_ssd_conv_w, v_ssd_conv_b, v_ssd_dt_bias, v_ssd_a_log, v_ssd_d, v_ssd_norm_w, v_w_out, v_final_norm_w):
    return _step(x, c, norm_w, w_ada, b_ada, w_in, lru_conv_w, lru_conv_b, lru_wa, lru_ba, lru_wx, lru_bx, lru_lambda, hg_lb_logits, hg_norm_w, ssd_conv_w, ssd_conv_b, ssd_dt_bias, ssd_a_log, ssd_d, ssd_norm_w, w_out, final_norm_w, loss_target, m_norm_w, m_w_ada, m_b_ada, m_w_in, m_lru_conv_w, m_lru_conv_b, m_lru_wa, m_lru_ba, m_lru_wx, m_lru_bx, m_lru_lambda, m_hg_lb_logits, m_hg_norm_w, m_ssd_conv_w, m_ssd_conv_b, m_ssd_dt_bias, m_ssd_a_log, m_ssd_d, m_ssd_norm_w, m_w_out, m_final_norm_w, v_norm_w, v_w_ada, v_b_ada, v_w_in, v_lru_conv_w, v_lru_conv_b, v_lru_wa, v_lru_ba, v_lru_wx, v_lru_bx, v_lru_lambda, v_hg_lb_logits, v_hg_norm_w, v_ssd_conv_w, v_ssd_conv_b, v_ssd_dt_bias, v_ssd_a_log, v_ssd_d, v_ssd_norm_w, v_w_out, v_final_norm_w)


def _step(*args):
    a = dict(zip(INPUTS, args, strict=True))
    me = 4 * lax.axis_index("x") + 2 * lax.axis_index("y") + lax.axis_index("c")
    x, target = a["x"][0], a["loss_target"][0]

    c_all = _all_gather(a["c"], "gather_c")[:, 0, :]
    b_cols = lax.dynamic_slice_in_dim(a["b_ada"], me * ADA_COLS, ADA_COLS, axis=1)[:, None, :]
    mod_parts = _all_gather(_ada_fwd(c_all, a["w_ada"], b_cols), "gather_mod")
    mod = lax.dynamic_index_in_dim(mod_parts, me, axis=2, keepdims=False)
    mod = mod.transpose(1, 0, 2).reshape(DEPTH, 3 * D_MODEL)

    w = {n: a[n] for n in SMALL}

    w_in_b = [a["w_in"][l].astype(BF16) for l in range(DEPTH)]
    w_out_b = a["w_out"].astype(BF16)
    conv_own = jnp.concatenate([a["lru_conv_w"], a["ssd_conv_w"]], axis=-1)
    cols, rows_out = N_IN // N_DEV, D_INNER // N_DEV

    def gather_start(l, after):
        srcs = [w_in_b[l], w_out_b[l]] + ([conv_own] if l == 0 else [])
        lands = [lax.empty((N_DEV,) + s.shape, s.dtype) for s in srcs]
        return _exchange_start(f"gather_start_{l}", srcs, lands, "chip" if l == 0 else "gather", after=after)

    def gather_pass(name, st, after, also=()):
        landed = _exchange_wait(name + "_wait", st, after, also)
        st2 = _exchange_start(name + "_pass", st["srcs"], landed, "pass")
        return _exchange_wait(name + "_passed", st2, after)

    gathers = {0: gather_start(0, mod)}

    def fetch(l, x_l):
        if l == 0:
            landed = gather_pass("gather_0", gathers[0], x_l, also=(a["w_in"], a["m_w_in"], a["v_w_in"]))
        else:
            landed = _exchange_wait(f"gather_wait_{l}", gathers[l], x_l)
        land_out = lax.dynamic_update_index_in_dim(landed[1], w_out_b[l], me, 0)
        if l == 0:
            conv = lax.dynamic_update_index_in_dim(landed[2], conv_own, me, 0).transpose(1, 2, 0, 3)
            w["lru_conv_w"] = conv[..., :64].reshape(DEPTH, 4, LRU_W)
            w["ssd_conv_w"] = conv[..., 64:].reshape(DEPTH, 4, SSD_CONV)
        token = None
        if l + 1 < DEPTH:
            gathers[l + 1] = gather_start(l + 1, land_out)
            token = gathers[l + 1]["token"]
        return _relayout_in(landed[0], w_in_b[l]), land_out.reshape(D_INNER, D_MODEL), token

    scatters = {"w_in": {}, "w_out": {}}
    lands = {"w_in": lax.empty((N_DEV, DEPTH, D_MODEL, cols), BF16),
             "w_out": lax.empty((N_DEV, DEPTH, rows_out, D_MODEL), BF16)}
    own = {"w_in": [None] * DEPTH, "w_out": [None] * DEPTH}

    deferred = {}

    def emit(l, name, grad, after=None):
        if name == "w_in" and l == 0 and after is None:
            deferred["w_in"] = grad
            return None
        grad = _relayout_grad(grad) if name == "w_in" else grad.reshape(N_DEV, rows_out, D_MODEL)
        st = _exchange_start(f"scatter_start_{name}_{l}", [grad], [lands[name]], "scatter", layer=l, after=after)
        scatters[name][l] = st
        lands[name] = st["lands"][0]
        return st["token"]

    loss_own, dx, dmod, g = _local_step(x, mod, target, w, fetch, emit)

    def sharded(name, parts, own=None, **kw):
        return _adamw(parts, a[name], a["m_" + name], a["v_" + name], "adamw_" + name + kw.pop("tag", ""), own=own, **kw)

    g["b_ada"] = dmod
    small_own = _flatten_small(g, last=loss_own)
    small_st = _exchange_start("gather_small", [small_own], [lax.empty((N_DEV,) + small_own.shape, F32)], "chip",
                               after=dx)
    big = {}
    after = emit(0, "w_in", deferred["w_in"], after=small_st["token"]) + dx[0:8, 0:LANE]

    def own_slice(st):
        return lax.dynamic_index_in_dim(st["srcs"][0], me, 0, keepdims=False)

    upper = {}
    for name in ("w_out", "w_in"):
        for l in reversed(range(1, DEPTH)):
            scatters[name][l]["lands"] = [lands[name]]
            lands[name] = _exchange_wait(f"scatter_wait_{name}_{l}", scatters[name][l], after)[0]
            own[name][l] = own_slice(scatters[name][l])
        upper[name] = sharded(name, lands[name], jnp.stack(own[name][1:]), layers=(1, DEPTH), tag="_upper")
        after = upper[name][1]
    small = gather_pass("gather_small", small_st, after)[0]
    outs = _adamw(small[:, None], *[_flatten_small(a, p)[None] for p in ("", "m_", "v_")], "adamw_small",
                  own=small_own[None])
    res = [_split_small(o[0], a) for o in outs]
    losses = lax.dynamic_update_index_in_dim(small[:, -1, 0], loss_own, me, 0)
    loss = jnp.sum(losses)

    off = _small_rows(a)[0]["b_ada"][0]
    dmod_all = lax.dynamic_update_index_in_dim(small[:, off:off + DEPTH * 3 * D_MODEL // SMALL_ROW],
                                               dmod.reshape(-1, SMALL_ROW), me, 0)
    dmod_all = dmod_all.reshape(N_DEV, DEPTH, 3 * D_MODEL).transpose(1, 0, 2)
    dmod_cols = lax.dynamic_slice_in_dim(dmod_all, me * ADA_COLS, ADA_COLS, axis=2)
    dmod_pad = jnp.pad(dmod_cols, ((0, 0), (0, LANE - N_DEV), (0, 0)))
    ct_pad = jnp.pad(c_all.T, ((0, 0), (0, LANE - N_DEV)))
    big["w_ada"] = sharded("w_ada", _ada_bwd(ct_pad, dmod_pad)[None])
    g_conv = jnp.concatenate([g["lru_conv_w"].reshape(DEPTH, 4, N_DEV, 64), g["ssd_conv_w"].reshape(DEPTH, 4, N_DEV, 192)],
                             axis=-1).transpose(2, 0, 1, 3)
    conv_parts = _all_to_all(g_conv, "scatter_conv")
    big["lru_conv_w"] = sharded("lru_conv_w", conv_parts[..., :64])
    big["ssd_conv_w"] = sharded("ssd_conv_w", conv_parts[..., 64:])

    after = outs[1] + big["w_ada"][1][0, 0:1, 0:1]
    for name in ("w_out", "w_in"):
        scatters[name][0]["lands"] = [lands[name]]
        lands[name] = _exchange_wait(f"scatter_wait_{name}_0", scatters[name][0], after)[0]
        big[name] = sharded(name, lands[name], own_slice(scatters[name][0])[None], layers=(0, 1), prev=upper[name])
        after = big[name][1]

    out = [loss, dx[None]]
    for k in range(4):
        out += [big[n][k] if n in big else res[k][n] for n in WEIGHTS]
    return tuple(out)
```

```python
import functools

import numpy as np
import jax
import jax.numpy as jnp
from jax import lax
from jax.experimental import pallas as pl
from jax.experimental.pallas import tpu as pltpu

F32 = jnp.float32
BF16 = jnp.bfloat16
SDS = jax.ShapeDtypeStruct

N_DEV = 8
DEPTH = 4
D_MODEL = 1024
D_INNER = 2048
EPS = 1e-6
LRU_W = 512
LRU_C = 8.0
HG_W = 512
HG_CHUNK = 64
HG_HEADS = 4
SSD_W = 1024
SSD_HEADS = 16
SSD_P = 64
SSD_N = 128
SSD_CHUNK = 128
SSD_CONV = 1536
N_IN = 5648
N_PAD = 5760
OFF_HG = 0
OFF_LRU = 2048
OFF_XBC = 3072
OFF_Z = 4608
LANE = 128
VMEM_LIMIT = 56 * 1024 * 1024
NEG = -1e30

ADAM_LR = 0.001
ADAM_B1 = 0.9
ADAM_B2 = 0.999
ADAM_EPS = 1e-08
ADAM_WD = 0.01
ADAM_STEP = 10


def _cp(sem=None):
    return pltpu.CompilerParams(dimension_semantics=sem, vmem_limit_bytes=VMEM_LIMIT)


def _dg(a, b, ca, cb):
    return lax.dot_general(a, b, (((ca,), (cb,)), ((), ())), preferred_element_type=F32)


def _mm(a, b):
    return _dg(a, b, 1, 0)


def _mm_nt(a, b):
    return _dg(a, b, 1, 1)


def _mm_tn(a, b):
    return _dg(a, b, 0, 0)


def _bf(x):
    return x.astype(BF16)


def _f(x):
    return x.astype(F32)


def _split3(x):
    hi = x.astype(BF16)
    r = x - hi.astype(F32)
    mid = r.astype(BF16)
    lo = (r - mid.astype(F32)).astype(BF16)
    return hi, mid, lo


def _sel_r(x, m):
    hi, mid, lo = _split3(x)
    return _mm(hi, m) + _mm(mid, m) + _mm(lo, m)


def _sel_l(m, x):
    hi, mid, lo = _split3(x)
    return _mm(m, hi) + _mm(m, mid) + _mm(m, lo)


def _sel_l2(m, x):
    hi = x.astype(BF16)
    lo = (x - hi.astype(F32)).astype(BF16)
    return _mm(m, hi) + _mm(m, lo)


def _sel_tn(x, m):
    hi, mid, lo = _split3(x)
    return _mm_tn(hi, m) + _mm_tn(mid, m) + _mm_tn(lo, m)


def _sigmoid(x):
    return 1.0 / (1.0 + jnp.exp(-x))


def _silu(x):
    return x * _sigmoid(x)


def _dsilu(x):
    s = _sigmoid(x)
    return s * (1.0 + x * (1.0 - s))


def _softplus(x):
    return jnp.maximum(x, 0.0) + jnp.log(1.0 + jnp.exp(-jnp.abs(x)))


def _expm1(z):
    series = z * (1.0 + z * (1.0 / 2) * (1.0 + z * (1.0 / 3) * (1.0 + z * (1.0 / 4) * (
        1.0 + z * (1.0 / 5) * (1.0 + z * (1.0 / 6) * (1.0 + z * (1.0 / 7)))))))
    return jnp.where(jnp.abs(z) < 0.3, series, jnp.exp(z) - 1.0)


def _iota(shape, dim):
    return lax.broadcasted_iota(jnp.int32, shape, dim)


def _last_row(x, rows):
    return jnp.sum(jnp.where(rows == x.shape[0] - 1, x, 0.0), axis=0, keepdims=True)


def _shift_down(x, d, rows, fill=0.0):
    return jnp.where(rows >= d, pltpu.roll(x, d, 0), fill)


def _shift_up(x, d, rows, fill=0.0):
    n = x.shape[0]
    return jnp.where(rows < n - d, pltpu.roll(x, n - d, 0), fill)


def _conv_fwd(x, cw_ref, cb_ref, rows):
    out = cb_ref[...] + cw_ref[pl.ds(3, 1), :] * x
    for k in range(3):
        out = out + cw_ref[pl.ds(k, 1), :] * _shift_down(x, 3 - k, rows)
    return out


def _conv_bwd(x, dco, cw_ref, rows):
    dx = cw_ref[pl.ds(3, 1), :] * dco
    dws = []
    for k in range(3):
        dx = dx + cw_ref[pl.ds(k, 1), :] * _shift_up(dco, 3 - k, rows)
        dws.append(jnp.sum(dco * _shift_down(x, 3 - k, rows), axis=0, keepdims=True))
    dws.append(jnp.sum(dco * x, axis=0, keepdims=True))
    return dx, dws, jnp.sum(dco, axis=0, keepdims=True)


def _vec(n):
    return pl.BlockSpec((1, n), lambda *_: (0, 0))


class _Row:
    def __init__(self, arr, l, n=None, c=0):
        self.arr, self.l, self.n, self.c = arr[:, None, :], l, n or arr.shape[1], c


def _spec(v):
    if isinstance(v, _Row):
        return pl.BlockSpec((None, 1, v.n), lambda *_: (v.l, 0, v.c))
    return _vec(v.shape[1])


def _arr(v):
    return v.arr if isinstance(v, _Row) else v


def _full(shape):
    nd = len(shape)
    return pl.BlockSpec(shape, lambda *_: (0,) * nd)


def _inproj_fwd(x, nw, scale, shift, w, tok):
    S = x.shape[0]
    tm = min(256, S)

    def body(x_ref, nw_ref, sc_ref, sh_ref, w_ref, tok_ref, u_ref, h_ref):
        del tok_ref
        xv = x_ref[...]
        inv = lax.rsqrt(jnp.mean(xv * xv, axis=-1, keepdims=True) + EPS)
        h = ((xv * inv) * nw_ref[...] * (1.0 + sc_ref[...]) + sh_ref[...]).astype(BF16)
        h_ref[...] = h
        u_ref[...] = _mm(h, w_ref[...])

    return pl.pallas_call(
        body, name="inproj_fwd", grid=(S // tm,),
        in_specs=[pl.BlockSpec((tm, D_MODEL), lambda i: (i, 0)), _spec(nw), _spec(scale), _spec(shift),
                  _full((D_MODEL, N_PAD)), pl.BlockSpec(memory_space=pl.ANY)],
        out_specs=[pl.BlockSpec((tm, N_PAD), lambda i: (i, 0)), pl.BlockSpec((tm, D_MODEL), lambda i: (i, 0))],
        out_shape=[SDS((S, N_PAD), F32), SDS((S, D_MODEL), BF16)],
        compiler_params=_cp(("parallel",)),
    )(x, _arr(nw), _arr(scale), _arr(shift), w, tok)


def _inproj_bwd_x(du, w, x, nw, scale, dxn, tok):
    S = x.shape[0]
    tm = min(256, S)

    def body(du_ref, w_ref, x_ref, nw_ref, sc_ref, dxn_ref, tok_ref, dx_ref, red_ref):
        del tok_ref

        @pl.when(pl.program_id(0) == 0)
        def _():
            red_ref[...] = jnp.zeros_like(red_ref)

        dh = _mm_nt(du_ref[...], w_ref[...])
        xv = x_ref[...]
        inv = lax.rsqrt(jnp.mean(xv * xv, axis=-1, keepdims=True) + EPS)
        xhat = xv * inv
        nwv = nw_ref[...]
        g1 = 1.0 + sc_ref[...]
        dxhat = dh * nwv * g1
        dx = inv * (dxhat - xhat * jnp.mean(dxhat * xhat, axis=-1, keepdims=True))
        dx_ref[...] = dxn_ref[...] + dx
        red_ref[0:1, :] += jnp.sum(dh, axis=0, keepdims=True)
        red_ref[1:2, :] += jnp.sum(dh * xhat * nwv, axis=0, keepdims=True)
        red_ref[2:3, :] += jnp.sum(dh * xhat * g1, axis=0, keepdims=True)

    row = pl.BlockSpec((tm, D_MODEL), lambda i: (i, 0))
    return pl.pallas_call(
        body, name="inproj_bwd_x", grid=(S // tm,),
        in_specs=[pl.BlockSpec((tm, N_PAD), lambda i: (i, 0)), _full((D_MODEL, N_PAD)), row, _spec(nw),
                  _spec(scale), row, pl.BlockSpec(memory_space=pl.ANY)],
        out_specs=[row, _full((8, D_MODEL))],
        out_shape=[SDS((S, D_MODEL), F32), SDS((8, D_MODEL), F32)],
        compiler_params=_cp(("arbitrary",)),
    )(du, w, x, _arr(nw), _arr(scale), dxn, tok)


def _inproj_bwd_w(h, du):
    S = h.shape[0]
    tn = 640

    def body(h_ref, du_ref, gw_ref):
        gw_ref[...] = _mm_tn(h_ref[...], _bf(du_ref[...]))

    return pl.pallas_call(
        body, name="inproj_bwd_w", grid=(N_PAD // tn,),
        in_specs=[_full((S, D_MODEL)), pl.BlockSpec((S, tn), lambda j: (0, j))],
        out_specs=pl.BlockSpec((D_MODEL, tn), lambda j: (0, j)),
        out_shape=SDS((D_MODEL, N_PAD), F32),
        compiler_params=_cp(("parallel",)),
    )(h, du)


def _scan_block(a, b, rows):
    d = 1
    while d < a.shape[0]:
        a_s = _shift_down(a, d, rows, 1.0)
        b_s = _shift_down(b, d, rows, 0.0)
        b = a * b_s + b
        a = a * a_s
        d *= 2
    return a, b


def _rscan_block(c, g, rows):
    d = 1
    while d < c.shape[0]:
        c_s = _shift_up(c, d, rows, 1.0)
        g_s = _shift_up(g, d, rows, 0.0)
        g = g + c * g_s
        c = c * c_s
        d *= 2
    return c, g


LRU_BLOCK = 256


def _lru_gates(xa, wa_ref, ba_ref, wx_ref, bx_ref, lam_ref):
    sp = _softplus(-lam_ref[...])
    xb = _bf(xa)
    r = _sigmoid(_mm(xb, wa_ref[...]) + ba_ref[...])
    ig = _sigmoid(_mm(xb, wx_ref[...]) + bx_ref[...])
    la = -LRU_C * r * sp
    a = jnp.exp(la)
    mult = jnp.sqrt(-_expm1(2.0 * la))
    return sp, r, ig, la, a, mult


def _lru_specs(S, l):
    t128 = pl.BlockSpec((None, 1, LANE), lambda t: (l, 0, t))
    gate = pl.BlockSpec((None, None, LANE, LANE), lambda t: (l, t, 0, 0))
    return [pl.BlockSpec((S, 2 * LANE), lambda t: (0, OFF_LRU // (2 * LANE) + t)),
            pl.BlockSpec((None, 4, LANE), lambda t: (l, 0, t)), t128, gate, t128, gate, t128, t128]


def _lru_fwd(l, u, cw, cb, wa, ba, wx, bx, lam, ycat):
    S = u.shape[0]
    tb = min(LRU_BLOCK, S)

    def body(u_ref, cw_ref, cb_ref, wa_ref, ba_ref, wx_ref, bx_ref, lam_ref, ycat_in, ycat_ref, h_ref, a_scr, b_scr):
        del ycat_in
        rows = _iota((S, LANE), 0)
        xa = _conv_fwd(_f(u_ref[:, 0:LANE]), cw_ref, cb_ref, rows)
        _, _, ig, _, a, mult = _lru_gates(xa, wa_ref, ba_ref, wx_ref, bx_ref, lam_ref)
        a_scr[...] = a
        b_scr[...] = mult * (ig * xa)
        rows_b = _iota((tb, LANE), 0)

        def blk(j, hprev):
            sl = pl.ds(pl.multiple_of(j * tb, tb), tb)
            acum, hloc = _scan_block(a_scr[sl, :], b_scr[sl, :], rows_b)
            hf = hloc + acum * hprev
            h_ref[sl, :] = hf
            return _last_row(hf, rows_b)

        lax.fori_loop(0, S // tb, blk, jnp.zeros((1, LANE), F32))
        ycat_ref[...] = _bf(h_ref[...] * _silu(_f(u_ref[:, LANE:2 * LANE])))

    col = pl.BlockSpec((S, LANE), lambda t: (0, t))
    return pl.pallas_call(
        body, name="lru_fwd", grid=(LRU_W // LANE,),
        in_specs=_lru_specs(S, l) + [pl.BlockSpec(memory_space=pl.ANY)],
        out_specs=[col, col],
        out_shape=[SDS((S, D_INNER), BF16), SDS((S,LRU_W), F32)],
        scratch_shapes=[pltpu.VMEM((S, LANE), F32), pltpu.VMEM((S, LANE), F32)],
        input_output_aliases={8: 0},
        compiler_params=_cp(("parallel",)),
    )(u, cw, cb, wa, ba, wx, bx, lam, ycat)


def _lru_bwd(l, u, cw, cb, wa, ba, wx, bx, lam, h_lru, dycat, du):
    S = u.shape[0]
    tb = min(LRU_BLOCK, S)

    def body(u_ref, cw_ref, cb_ref, wa_ref, ba_ref, wx_ref, bx_ref, lam_ref, h_ref, dy_ref, du_in,
             du_ref, red_ref, gwa_ref, gwx_ref, c_scr, g_scr, l_scr):
        del du_in
        rows = _iota((S, LANE), 0)
        ax = _f(u_ref[:, 0:LANE])
        ag = _f(u_ref[:, LANE:2 * LANE])
        xa = _conv_fwd(ax, cw_ref, cb_ref, rows)
        sp, r, ig, la, a, mult = _lru_gates(xa, wa_ref, ba_ref, wx_ref, bx_ref, lam_ref)
        h = h_ref[...]
        dy = _f(dy_ref[...])
        du_ref[:, LANE:2 * LANE] = _bf(dy * h * _dsilu(ag))
        c_scr[...] = _shift_up(a, 1, rows, 0.0)
        g_scr[...] = dy * _silu(ag)
        rows_b = _iota((tb, LANE), 0)
        nb = S // tb

        def blk(jj, lnext):
            j = nb - 1 - jj
            sl = pl.ds(pl.multiple_of(j * tb, tb), tb)
            ccum, lloc = _rscan_block(c_scr[sl, :], g_scr[sl, :], rows_b)
            lam_t = lloc + ccum * lnext
            l_scr[sl, :] = lam_t
            return jnp.sum(jnp.where(rows_b == 0, lam_t, 0.0), axis=0, keepdims=True)

        lax.fori_loop(0, nb, blk, jnp.zeros((1, LANE), F32))
        db = l_scr[...]
        da = db * _shift_down(h, 1, rows)
        dmult = db * ig * xa
        dig = db * mult * xa
        dxa = db * mult * ig
        dla = da * a - dmult * (a * a) / mult
        dr = -LRU_C * sp * dla
        dsp = jnp.sum(-LRU_C * r * dla, axis=0, keepdims=True)
        dlam = -dsp * _sigmoid(-lam_ref[...])
        dzr = dr * r * (1.0 - r)
        dzi = dig * ig * (1.0 - ig)
        dzr_b, dzi_b, xa_b = _bf(dzr), _bf(dzi), _bf(xa)
        dxa = dxa + _mm_nt(dzr_b, wa_ref[...]) + _mm_nt(dzi_b, wx_ref[...])
        gwa_ref[...] = _mm_tn(xa_b, dzr_b)
        gwx_ref[...] = _mm_tn(xa_b, dzi_b)
        dax, dws, dcb = _conv_bwd(ax, dxa, cw_ref, rows)
        du_ref[:, 0:LANE] = _bf(dax)
        parts = dws + [dcb, jnp.sum(dzr, axis=0, keepdims=True), jnp.sum(dzi, axis=0, keepdims=True), dlam]
        for n, p in enumerate(parts):
            red_ref[pl.ds(n, 1), :] = p

    col = pl.BlockSpec((S, LANE), lambda t: (0, t))
    gw = pl.BlockSpec((None, LANE, LANE), lambda t: (t, 0, 0))
    return pl.pallas_call(
        body, name="lru_bwd", grid=(LRU_W // LANE,),
        in_specs=_lru_specs(S, l) + [col, col, pl.BlockSpec(memory_space=pl.ANY)],
        out_specs=[pl.BlockSpec((S, 2 * LANE), lambda t: (0, OFF_LRU // (2 * LANE) + t)),
                   pl.BlockSpec((8, LANE), lambda t: (0, t)), gw, gw],
        out_shape=[SDS((S, N_PAD), BF16), SDS((8, LRU_W), F32), SDS((4, LANE, LANE), F32), SDS((4, LANE, LANE), F32)],
        scratch_shapes=[pltpu.VMEM((S, LANE), F32)] * 3,
        input_output_aliases={10: 0},
        compiler_params=_cp(("parallel",)),
    )(u, cw, cb, wa, ba, wx, bx, lam, h_lru, dycat, du)


HG_LEVELS = 6


def _hg_consts():
    C = HG_CHUNK
    t = np.arange(C)[:, None]
    r = np.arange(C)[None, :]
    mats = []
    for l in range(HG_LEVELS):
        b = 1 << l
        upper = (t % (2 * b)) >= b
        anchor = (t // (2 * b)) * 2 * b + b - 1
        mats.append((upper & (r > anchor) & (r <= t)) | ((~upper) & (r > t) & (r <= anchor)))
    mats.append(r <= t)
    mats.append(r > t)
    return np.concatenate(mats, 0).astype(np.float32)


def _hg_factors(hf, lb, mall):
    s = _sigmoid(hf)
    f = lb + (1.0 - lb) * s
    lf = jnp.log(f)
    k = (1.0 - lb) * _sigmoid(-hf)
    e = jnp.exp(_sel_l(mall, lf))
    C = HG_CHUNK
    eq = [e[l * C:(l + 1) * C] for l in range(HG_LEVELS)]
    ecum = e[HG_LEVELS * C:(HG_LEVELS + 1) * C]
    erem = e[(HG_LEVELS + 1) * C:(HG_LEVELS + 2) * C]
    return s, f, k, eq, eq, ecum, erem


def _hg_masks():
    C = HG_CHUNK
    ri, ci = _iota((C, C), 0), _iota((C, C), 1)
    rr = _iota((C, LANE), 0)
    gm = [(lax.shift_right_logical(ri, l + 1) == lax.shift_right_logical(ci, l + 1)).astype(F32)
          for l in range(HG_LEVELS)]
    up = [(lax.shift_right_logical(rr, l) & 1) == 1 for l in range(HG_LEVELS)]
    eye = (ri == ci).astype(F32)
    return gm, up, eye, rr


def _hg_scores(qh, kh, eq, ek, sl, gm, up, eye):
    qs, ks = [], []
    p = _mm_nt(_bf(qh), _bf(kh)) * eye
    for l in range(HG_LEVELS):
        ql = jnp.where(up[l], qh * eq[l][:, sl], 0.0)
        kl = jnp.where(up[l], 0.0, kh * ek[l][:, sl])
        p = p + _mm_nt(_bf(ql), _bf(kl)) * gm[l]
        qs.append(ql)
        ks.append(kl)
    return p, qs, ks


HG_SUB = 4


def _hg_fwd(u, lb, nw, mall, ycat):
    S = u.shape[0]
    C = HG_CHUNK
    n = S // C
    rows = HG_SUB * C

    def body(u_ref, lb_ref, nw_ref, mall_ref, ycat_in, ycat_ref, o_ref, st_ref, st):
        del ycat_in

        @pl.when(pl.program_id(0) == 0)
        def _():
            st[...] = jnp.zeros_like(st)

        gm, up, eye, rr = _hg_masks()
        for sub in range(HG_SUB):
            r = slice(sub * C, (sub + 1) * C)
            q = _silu(_f(u_ref[r, 0:512]))
            v = u_ref[r, 1024:1536]
            _, _, k, eq, ek, ecum, erem = _hg_factors(_f(u_ref[r, 512:1024]), lb_ref[...], mall_ref[...])
            for h in range(HG_HEADS):
                sl = slice(h * LANE, (h + 1) * LANE)
                qh, kh, vh = q[:, sl], k[:, sl], _bf(v[:, sl])
                p, _, _ = _hg_scores(qh, kh, eq, ek, sl, gm, up, eye)
                sth = st[h]
                st_ref[sub, h] = sth
                o_ref[r, sl] = _mm(_bf(p), vh) + _mm_nt(_bf(qh * ecum[:, sl]), _bf(sth))
                st[h] = sth * _last_row(ecum[:, sl], rr) + _mm_tn(vh, _bf(kh * erem[:, sl]))
            o = o_ref[r, :]
            inv = lax.rsqrt(jnp.mean(o * o, axis=-1, keepdims=True) + EPS)
            ycat_ref[r, :] = _bf((o * inv) * nw_ref[...] * _silu(_f(u_ref[r, 1536:2048])))

    return pl.pallas_call(
        body, name="hg_fwd", grid=(n // HG_SUB,),
        in_specs=[pl.BlockSpec((rows, 2048), lambda i: (i, 0)), _spec(lb), _spec(nw), _full(mall.shape),
                  pl.BlockSpec(memory_space=pl.ANY)],
        out_specs=[pl.BlockSpec((rows, HG_W), lambda i: (i, 1)), pl.BlockSpec((rows, HG_W), lambda i: (i, 0)),
                   pl.BlockSpec((HG_SUB, HG_HEADS, LANE, LANE), lambda i: (i, 0, 0, 0))],
        out_shape=[SDS((S, D_INNER), BF16), SDS((S,HG_W), F32), SDS((n, HG_HEADS, LANE, LANE), F32)],
        scratch_shapes=[pltpu.VMEM((HG_HEADS, LANE, LANE), F32)],
        input_output_aliases={4: 0},
        compiler_params=_cp(("arbitrary",)),
    )(u, _arr(lb), _arr(nw), mall, ycat)


def _hg_bwd(u, lb, nw, mall, mall_t, o_b, states, dycat, du):
    S = u.shape[0]
    C = HG_CHUNK
    n = S // C
    nb = n // HG_SUB
    rows = HG_SUB * C
    L2 = HG_LEVELS

    def body(u_ref, lb_ref, nw_ref, mall_ref, mallt_ref, o_ref, st_ref, dy_ref, du_in, du_ref, red_ref,
             dst, dlast_s, dq_s, dk_s, dex):
        del du_in

        @pl.when(pl.program_id(0) == 0)
        def _():
            dst[...] = jnp.zeros_like(dst)
            red_ref[...] = jnp.zeros_like(red_ref)

        lb = lb_ref[...]
        nwv = nw_ref[...]
        gm, up, eye, rr = _hg_masks()
        for sub in reversed(range(HG_SUB)):
            r = slice(sub * C, (sub + 1) * C)
            hq, hf, hg = _f(u_ref[r, 0:512]), _f(u_ref[r, 512:1024]), _f(u_ref[r, 1536:2048])
            q = _silu(hq)
            v = u_ref[r, 1024:1536]
            s, f, k, eq, ek, ecum, erem = _hg_factors(hf, lb, mall_ref[...])
            o = o_ref[r, :]
            dy = _f(dy_ref[r, :])
            inv = lax.rsqrt(jnp.mean(o * o, axis=-1, keepdims=True) + EPS)
            ohat = o * inv
            du_ref[r, 1536:2048] = _bf(dy * ohat * nwv * _dsilu(hg))
            dn = dy * _silu(hg)
            red_ref[0:1, :] += jnp.sum(dn * ohat, axis=0, keepdims=True)
            dohat = dn * nwv
            do = inv * (dohat - ohat * jnp.mean(dohat * ohat, axis=-1, keepdims=True))
            for h in range(HG_HEADS):
                sl = slice(h * LANE, (h + 1) * LANE)
                qh, kh, vh, doh = q[:, sl], k[:, sl], _bf(v[:, sl]), _bf(do[:, sl])
                p, qs, ks = _hg_scores(qh, kh, eq, ek, sl, gm, up, eye)
                st_f = st_ref[sub, h]
                sth = _bf(st_f)
                dsth = dst[h]
                dsth_b = _bf(dsth)
                qt = qh * ecum[:, sl]
                kt = kh * erem[:, sl]
                elast = _last_row(ecum[:, sl], rr)
                dp = _mm_nt(doh, vh)
                du_ref[r, 1024 + h * LANE:1024 + (h + 1) * LANE] = _bf(_mm_tn(_bf(p), doh) + _mm_nt(_bf(kt), dsth_b))
                dpe = _bf(dp * eye)
                dqt = _mm(doh, sth)
                dkt = _mm(vh, dsth_b)
                dq = dqt * ecum[:, sl] + _mm(dpe, _bf(kh))
                dk = dkt * erem[:, sl] + _mm_tn(dpe, _bf(qh))
                dex[sub, L2 * C:(L2 + 1) * C, sl] = dqt * qt
                dex[sub, (L2 + 1) * C:(L2 + 2) * C, sl] = dkt * kt
                for l in range(HG_LEVELS):
                    dpl = _bf(dp * gm[l])
                    dql = _mm(dpl, _bf(ks[l]))
                    dkl = _mm_tn(dpl, _bf(qs[l]))
                    dq = dq + jnp.where(up[l], dql * eq[l][:, sl], 0.0)
                    dk = dk + jnp.where(up[l], 0.0, dkl * ek[l][:, sl])
                    dex[sub, l * C:(l + 1) * C, sl] = dql * qs[l] + dkl * ks[l]
                dlast_s[sub, :, sl] = jnp.sum(dsth * st_f, axis=0, keepdims=True) * elast
                dst[h] = dsth * elast + _mm_tn(doh, _bf(qt))
                dq_s[sub, :, sl] = dq
                dk_s[sub, :, sl] = dk
            dq = dq_s[sub]
            dk = dk_s[sub]
            dlf = _sel_l2(mallt_ref[...], dex[sub]) + dlast_s[sub]
            du_ref[r, 0:512] = _bf(dq * _dsilu(hq))
            t = (1.0 - s) * (dlf / f - dk)
            du_ref[r, 512:1024] = _bf((1.0 - lb) * s * t)
            red_ref[1:2, :] += jnp.sum(t, axis=0, keepdims=True)

    rev = lambda i: (nb - 1 - i, 0)
    return pl.pallas_call(
        body, name="hg_bwd", grid=(nb,),
        in_specs=[pl.BlockSpec((rows, 2048), rev), _spec(lb), _spec(nw), _full(mall.shape), _full(mall_t.shape),
                  pl.BlockSpec((rows, HG_W), rev),
                  pl.BlockSpec((HG_SUB, HG_HEADS, LANE, LANE), lambda i: (nb - 1 - i, 0, 0, 0)),
                  pl.BlockSpec((rows, HG_W), lambda i: (nb - 1 - i, 1)), pl.BlockSpec(memory_space=pl.ANY)],
        out_specs=[pl.BlockSpec((rows, 2048), rev), pl.BlockSpec((8, HG_W), lambda i: (0, 0))],
        out_shape=[SDS((S, N_PAD), BF16), SDS((8, HG_W), F32)],
        scratch_shapes=[pltpu.VMEM((HG_HEADS, LANE, LANE), F32), pltpu.VMEM((HG_SUB, 1, HG_W), F32),
                        pltpu.VMEM((HG_SUB, C, HG_W), F32), pltpu.VMEM((HG_SUB, C, HG_W), F32),
                        pltpu.VMEM((HG_SUB, (L2 + 2) * C, HG_W), F32)],
        input_output_aliases={8: 0},
        compiler_params=_cp(("arbitrary",)),
    )(u, _arr(lb), _arr(nw), mall, mall_t, o_b, states, dycat, du)


def _ssdconv_fwd(l, u, cw, cb):
    S = u.shape[0]

    def body(u_ref, cw_ref, cb_ref, out_ref):
        rows = _iota((S, LANE), 0)
        out_ref[...] = _silu(_conv_fwd(_f(u_ref[...]), cw_ref, cb_ref, rows))

    return pl.pallas_call(
        body, name="ssdconv_fwd", grid=(SSD_CONV // LANE,),
        in_specs=[pl.BlockSpec((S, LANE), lambda t: (0, OFF_XBC // LANE + t)),
                  pl.BlockSpec((None, 4, LANE), lambda t: (l, 0, t)), pl.BlockSpec((None, 1, LANE), lambda t: (l, 0, t))],
        out_specs=pl.BlockSpec((S, LANE), lambda t: (0, t)),
        out_shape=SDS((S, SSD_CONV), F32),
        compiler_params=_cp(("parallel",)),
    )(u, cw, cb)


def _ssdconv_bwd(l, u, cw, cb, dxbc, du):
    S = u.shape[0]

    def body(u_ref, cw_ref, cb_ref, d_ref, du_in, du_ref, red_ref):
        del du_in
        rows = _iota((S, LANE), 0)
        x = _f(u_ref[...])
        dco = d_ref[...] * _dsilu(_conv_fwd(x, cw_ref, cb_ref, rows))
        dx, dws, dcb = _conv_bwd(x, dco, cw_ref, rows)
        du_ref[...] = _bf(dx)
        for n, p in enumerate(dws + [dcb]):
            red_ref[pl.ds(n, 1), :] = p
        red_ref[pl.ds(5, 3), :] = jnp.zeros((3, LANE), F32)

    ucol = pl.BlockSpec((S, LANE), lambda t: (0, OFF_XBC // LANE + t))
    return pl.pallas_call(
        body, name="ssdconv_bwd", grid=(SSD_CONV // LANE,),
        in_specs=[ucol, pl.BlockSpec((None, 4, LANE), lambda t: (l, 0, t)),
                  pl.BlockSpec((None, 1, LANE), lambda t: (l, 0, t)),
                  pl.BlockSpec((S, LANE), lambda t: (0, t)), pl.BlockSpec(memory_space=pl.ANY)],
        out_specs=[ucol, pl.BlockSpec((8, LANE), lambda t: (0, t))],
        out_shape=[SDS((S, N_PAD), BF16), SDS((8, SSD_CONV), F32)],
        input_output_aliases={4: 0},
        compiler_params=_cp(("parallel",)),
    )(u, cw, cb, dxbc, du)


def _ssd_consts():
    e64 = np.zeros((LANE, SSD_W), np.float32)
    for h in range(SSD_HEADS):
        e64[h, h * SSD_P:(h + 1) * SSD_P] = 1.0
    T = SSD_CHUNK
    tril = (np.arange(T)[None, :] <= np.arange(T)[:, None]).astype(np.float32)
    return e64, tril, tril.T.copy()


def _ssd_common(zdt, bias_ref, alog_ref, tril, e64, cum_ref, cumt_ref):
    T = SSD_CHUNK
    lane = _iota((1, LANE), 1)
    a_neg = jnp.where(lane < SSD_HEADS, -jnp.exp(alog_ref[...]), 0.0)
    dtpre = zdt[:, SSD_W:SSD_W + LANE] + bias_ref[...]
    dt = _softplus(dtpre)
    cum = _sel_l(tril, dt * a_neg)
    cum_ref[...] = cum
    cumt_ref[...] = cum.T
    cum_x = _sel_r(cum, e64)
    last_x = _last_row(cum_x, _iota((T, SSD_W), 0))
    ecum_x = jnp.exp(cum_x)
    erem_x = jnp.exp(last_x - cum_x)
    elast_x = jnp.exp(last_x)
    dt_x = _sel_r(dt, e64)
    return a_neg, dtpre, dt, ecum_x, erem_x, elast_x, dt_x


def _ssd_decay(cum_ref, cumt_ref, h, causal):
    T = SSD_CHUNK
    diff = jnp.broadcast_to(cum_ref[:, pl.ds(h, 1)], (T, T)) - cumt_ref[pl.ds(h, 1), :]
    return jnp.exp(jnp.where(causal, diff, NEG))


def _group_norm_fwd(y1, nwv):
    outs, invs = [], []
    for g in range(2):
        seg = y1[:, g * 512:(g + 1) * 512]
        inv = lax.rsqrt(jnp.mean(seg * seg, axis=-1, keepdims=True) + EPS)
        outs.append(seg * inv * nwv[:, g * 512:(g + 1) * 512])
        invs.append(inv)
    return outs, invs


def _ssd_fwd(u, xbc, bias, alog, dskip_x, nw, consts, ycat):
    S = u.shape[0]
    T = SSD_CHUNK
    n = S // T
    e64, tril, _ = consts

    def body(u_ref, xbc_ref, bias_ref, alog_ref, dx_ref, nw_ref, e64_ref, tril_ref, ycat_in,
             ycat_ref, y_ref, st_ref, st, cumt, cum_e):
        del ycat_in

        @pl.when(pl.program_id(0) == 0)
        def _():
            st[...] = jnp.zeros_like(st)

        zdt = _f(u_ref[...])
        z = zdt[:, 0:SSD_W]
        xs = xbc_ref[:, 0:SSD_W]
        _, _, _, ecum_x, erem_x, elast_x, dt_x = _ssd_common(
            zdt, bias_ref, alog_ref, tril_ref[...], e64_ref[...], cum_e, cumt)
        causal = _iota((T, T), 0) >= _iota((T, T), 1)
        lo = _iota((T, LANE), 1) < SSD_P
        xdt = xs * dt_x
        xrem = xdt * erem_x
        st_ref[...] = st[...]
        for g in range(2):
            gs = slice(g * 512, (g + 1) * 512)
            bg = _bf(xbc_ref[:, SSD_W + g * LANE:SSD_W + (g + 1) * LANE])
            cg = _bf(xbc_ref[:, SSD_W + 256 + g * LANE:SSD_W + 256 + (g + 1) * LANE])
            cb = _mm_nt(cg, bg)
            yin = _mm(cg, _bf(st[:, gs])) * ecum_x[:, gs]
            for j in range(4):
                h0 = 8 * g + 2 * j
                cs = slice(h0 * SSD_P, (h0 + 2) * SSD_P)
                xp = xdt[:, cs]
                s0 = _bf(cb * _ssd_decay(cum_e, cumt, h0, causal))
                s1 = _bf(cb * _ssd_decay(cum_e, cumt, h0 + 1, causal))
                y_ref[:, cs] = (_mm(s0, _bf(jnp.where(lo, xp, 0.0))) + _mm(s1, _bf(jnp.where(lo, 0.0, xp)))
                                + yin[:, j * LANE:(j + 1) * LANE])
            st[:, gs] = st[:, gs] * elast_x[:, gs] + _mm_tn(bg, _bf(xrem[:, gs]))
        y1 = (y_ref[...] + dx_ref[...] * xs) * _silu(z)
        outs, _ = _group_norm_fwd(y1, nw_ref[...])
        for g in range(2):
            ycat_ref[:, g * 512:(g + 1) * 512] = _bf(outs[g])

    return pl.pallas_call(
        body, name="ssd_fwd", grid=(n,),
        in_specs=[pl.BlockSpec((T, SSD_W + LANE), lambda i: (i, OFF_Z // (SSD_W + LANE))),
                  pl.BlockSpec((T, SSD_CONV), lambda i: (i, 0)), _spec(bias), _spec(alog), _spec(dskip_x), _spec(nw),
                  _full(e64.shape), _full(tril.shape), pl.BlockSpec(memory_space=pl.ANY)],
        out_specs=[pl.BlockSpec((T, SSD_W), lambda i: (i, 1)), pl.BlockSpec((T, SSD_W), lambda i: (i, 0)),
                   pl.BlockSpec((None, SSD_N, SSD_W), lambda i: (i, 0, 0))],
        out_shape=[SDS((S, D_INNER), BF16), SDS((S,SSD_W), F32), SDS((n, SSD_N, SSD_W), F32)],
        scratch_shapes=[pltpu.VMEM((SSD_N, SSD_W), F32), pltpu.VMEM((LANE, T), F32), pltpu.VMEM((T, LANE), F32)],
        input_output_aliases={8: 0},
        compiler_params=_cp(("arbitrary",)),
    )(u, xbc, _arr(bias), _arr(alog), _arr(dskip_x), _arr(nw), _bfc(e64), _bfc(tril), ycat)


def _ssd_bwd(u, xbc, bias, alog, dskip_x, nw, consts, y_ssd, states, dycat, du, tok):
    S = u.shape[0]
    T = SSD_CHUNK
    n = S // T
    e64, tril, triu = consts
    e64t = np.ascontiguousarray(e64.T)

    def body(u_ref, xbc_ref, bias_ref, alog_ref, dx_ref, nw_ref, e64_ref, e64t_ref, tril_ref, triu_ref,
             y_ref, st_ref, dy_ref, du_in, tok_ref, du_ref, dxbc_ref, red_ref, dst, dl_s, cumt, dxdt_s, dy0_s, gb_s,
             gc_s, cum_e, cs_s):
        del du_in, tok_ref

        @pl.when(pl.program_id(0) == 0)
        def _():
            dst[...] = jnp.zeros_like(dst)
            red_ref[...] = jnp.zeros_like(red_ref)
            cs_s[...] = jnp.zeros_like(cs_s)

        zdt = _f(u_ref[...])
        z = zdt[:, 0:SSD_W]
        xs = xbc_ref[:, 0:SSD_W]
        a_neg, dtpre, dt, ecum_x, erem_x, elast_x, dt_x = _ssd_common(
            zdt, bias_ref, alog_ref, tril_ref[...], e64_ref[...], cum_e, cumt)
        causal = _iota((T, T), 0) >= _iota((T, T), 1)
        lo = _iota((T, LANE), 1) < SSD_P
        xdt = xs * dt_x
        xrem = xdt * erem_x
        y = y_ref[...]
        dxv = dx_ref[...]
        nwv = nw_ref[...]
        sz = _silu(z)
        y0 = y + dxv * xs
        y1 = y0 * sz
        for g in range(2):
            gs = slice(g * 512, (g + 1) * 512)
            seg = y1[:, gs]
            inv = lax.rsqrt(jnp.mean(seg * seg, axis=-1, keepdims=True) + EPS)
            shat = seg * inv
            dyg = _f(dy_ref[:, gs])
            red_ref[0:1, gs] += jnp.sum(dyg * shat, axis=0, keepdims=True)
            dsh = dyg * nwv[:, gs]
            dy1g = inv * (dsh - shat * jnp.mean(dsh * shat, axis=-1, keepdims=True))
            du_ref[:, gs] = _bf(dy1g * y0[:, gs] * _dsilu(z[:, gs]))
            dy0_s[:, gs] = dy1g * sz[:, gs]
        dy0 = dy0_s[...]
        red_ref[1:2, :] += jnp.sum(dy0 * xs, axis=0, keepdims=True)
        dyin = dy0 * ecum_x
        lane = _iota((T, LANE), 1)
        dcum = jnp.zeros((T, LANE), F32)

        def decay_grad(h, gm):
            cs_s[pl.ds(h, 1), :] = jnp.sum(gm, axis=0, keepdims=True)
            return jnp.where(lane == h, jnp.sum(gm, axis=1, keepdims=True), 0.0)

        for g in range(2):
            gs = slice(g * 512, (g + 1) * 512)
            bg = _bf(xbc_ref[:, SSD_W + g * LANE:SSD_W + (g + 1) * LANE])
            cg = _bf(xbc_ref[:, SSD_W + 256 + g * LANE:SSD_W + 256 + (g + 1) * LANE])
            cb = _mm_nt(cg, bg)
            dst_f, st_f = dst[:, gs], st_ref[:, gs]
            dstg = _bf(dst_f)
            stg = _bf(st_f)
            dyin_g = _bf(dyin[:, gs])
            xrem_g = _bf(xrem[:, gs])
            dcb = jnp.zeros((T, T), F32)
            dxr = _mm(bg, dstg)
            dxdt_s[:, gs] = dxr * erem_x[:, gs]
            gc_s[:, gs] = dxr * xrem[:, gs]
            gb_s[:, gs] = dyin[:, gs] * _mm(cg, stg)
            dl_s[:, gs] = jnp.sum(dst_f * st_f, axis=0, keepdims=True) * elast_x[:, gs]
            for j in range(4):
                h0 = 8 * g + 2 * j
                cs = slice(h0 * SSD_P, (h0 + 2) * SSD_P)
                xp = xdt[:, cs]
                dyp = dy0[:, cs]
                x_lo, x_hi = _bf(jnp.where(lo, xp, 0.0)), _bf(jnp.where(lo, 0.0, xp))
                d_lo, d_hi = _bf(jnp.where(lo, dyp, 0.0)), _bf(jnp.where(lo, 0.0, dyp))
                l0 = _ssd_decay(cum_e, cumt, h0, causal)
                l1 = _ssd_decay(cum_e, cumt, h0 + 1, causal)
                s0 = cb * l0
                s1 = cb * l1
                ds0 = _mm_nt(d_lo, x_lo)
                ds1 = _mm_nt(d_hi, x_hi)
                dcb = dcb + ds0 * l0 + ds1 * l1
                dxdt_s[:, cs] += _mm_tn(_bf(s0), d_lo) + _mm_tn(_bf(s1), d_hi)
                dcum = dcum + decay_grad(h0, ds0 * s0) + decay_grad(h0 + 1, ds1 * s1)
            dcb_b = _bf(dcb)
            dxbc_ref[:, SSD_W + g * LANE:SSD_W + (g + 1) * LANE] = _mm_tn(dcb_b, cg) + _mm_nt(xrem_g, dstg)
            dxbc_ref[:, SSD_W + 256 + g * LANE:SSD_W + 256 + (g + 1) * LANE] = _mm(dcb_b, bg) + _mm_nt(dyin_g, stg)
            dst[:, gs] = dst_f * elast_x[:, gs] + _mm_tn(cg, dyin_g)
        dxdt = dxdt_s[...]
        dxbc_ref[:, 0:SSD_W] = dxdt * dt_x + dy0 * dxv
        e64t = e64t_ref[...]
        gc = gc_s[...]
        dlast_x = jnp.sum(gc, axis=0, keepdims=True) + dl_s[...]
        dlast = jnp.max(_sel_r(jnp.broadcast_to(dlast_x, (8, SSD_W)), e64t), axis=0, keepdims=True)
        dcum = (dcum - cs_s[...].T + _sel_r(gb_s[...] - gc, e64t)
                + jnp.where(_iota((T, LANE), 0) == T - 1, dlast, 0.0))
        dda = _sel_l(triu_ref[...], dcum)
        ddt = dda * a_neg + _sel_r(dxdt * xs, e64t)
        ddtpre = ddt * _sigmoid(dtpre)
        du_ref[:, SSD_W:SSD_W + LANE] = _bf(jnp.where(lane < SSD_HEADS, ddtpre, 0.0))
        red_ref[2:3, 0:LANE] += jnp.sum(ddtpre, axis=0, keepdims=True)
        red_ref[3:4, 0:LANE] += jnp.sum(dda * dt, axis=0, keepdims=True)

    rev = lambda i: (n - 1 - i, 0)
    return pl.pallas_call(
        body, name="ssd_bwd", grid=(n,),
        in_specs=[pl.BlockSpec((T, SSD_W + LANE), lambda i: (n - 1 - i, OFF_Z // (SSD_W + LANE))),
                  pl.BlockSpec((T, SSD_CONV), rev), _spec(bias), _spec(alog), _spec(dskip_x), _spec(nw),
                  _full(e64.shape), _full(e64t.shape), _full(tril.shape), _full(triu.shape),
                  pl.BlockSpec((T, SSD_W), rev), pl.BlockSpec((None, SSD_N, SSD_W), lambda i: (n - 1 - i, 0, 0)),
                  pl.BlockSpec((T, SSD_W), lambda i: (n - 1 - i, 1)), pl.BlockSpec(memory_space=pl.ANY),
                  pl.BlockSpec(memory_space=pl.ANY)],
        out_specs=[pl.BlockSpec((T, SSD_W + LANE), lambda i: (n - 1 - i, OFF_Z // (SSD_W + LANE))),
                   pl.BlockSpec((T, SSD_CONV), rev), pl.BlockSpec((8, SSD_W), lambda i: (0, 0))],
        out_shape=[SDS((S, N_PAD), BF16), SDS((S, SSD_CONV), F32), SDS((8, SSD_W), F32)],
        scratch_shapes=[pltpu.VMEM((SSD_N, SSD_W), F32), pltpu.VMEM((1, SSD_W), F32), pltpu.VMEM((LANE, T), F32)]
        + [pltpu.VMEM((T, SSD_W), F32)] * 4 + [pltpu.VMEM((T, LANE), F32), pltpu.VMEM((LANE, T), F32)],
        input_output_aliases={13: 0},
        compiler_params=_cp(("arbitrary",)),
    )(u, xbc, _arr(bias), _arr(alog), _arr(dskip_x), _arr(nw), _bfc(e64), _bfc(e64t), _bfc(tril), _bfc(triu), y_ssd,
      states, dycat, du, tok)


def _bfc(a):
    return jnp.asarray(a, BF16)


def _outproj_fwd(ycat, wo, x, gate, tok):
    S = x.shape[0]
    tm = min(512, S)

    def body(yc_ref, wo_ref, x_ref, g_ref, tok_ref, xn_ref, y_ref):
        del tok_ref
        y = _mm(_bf(yc_ref[...]), wo_ref[...])
        y_ref[...] = y
        xn_ref[...] = x_ref[...] + g_ref[...] * y

    row = pl.BlockSpec((tm, D_MODEL), lambda i: (i, 0))
    return pl.pallas_call(
        body, name="outproj_fwd", grid=(S // tm,),
        in_specs=[pl.BlockSpec((tm, D_INNER), lambda i: (i, 0)), _full((D_INNER, D_MODEL)), row, _spec(gate),
                  pl.BlockSpec(memory_space=pl.ANY)],
        out_specs=[row, row],
        out_shape=[SDS((S, D_MODEL), F32), SDS((S, D_MODEL), F32)],
        compiler_params=_cp(("parallel",)),
    )(ycat, wo, x, _arr(gate), tok)


def _outproj_bwd(dxn, y, gate, ycat, wo):
    S = dxn.shape[0]
    tm = min(512, S)

    def body(dx_ref, y_ref, g_ref, yc_ref, wo_ref, dyc_ref, gwo_ref, dg_ref, acc):
        @pl.when(pl.program_id(0) == 0)
        def _():
            acc[...] = jnp.zeros_like(acc)
            dg_ref[...] = jnp.zeros_like(dg_ref)

        dxv = dx_ref[...]
        dy = _bf(dxv * g_ref[...])
        dg_ref[0:1, :] += jnp.sum(dxv * y_ref[...], axis=0, keepdims=True)
        dyc_ref[...] = _mm_nt(dy, wo_ref[...])
        acc[...] += _mm_tn(_bf(yc_ref[...]), dy)

        @pl.when(pl.program_id(0) == pl.num_programs(0) - 1)
        def _():
            gwo_ref[...] = acc[...].astype(BF16)

    row = pl.BlockSpec((tm, D_MODEL), lambda i: (i, 0))
    wide = pl.BlockSpec((tm, D_INNER), lambda i: (i, 0))
    return pl.pallas_call(
        body, name="outproj_bwd", grid=(S // tm,),
        in_specs=[row, row, _spec(gate), wide, _full((D_INNER, D_MODEL))],
        out_specs=[wide, _full((D_INNER, D_MODEL)), _full((8, D_MODEL))],
        out_shape=[SDS((S, D_INNER), F32), SDS((D_INNER, D_MODEL), BF16), SDS((8, D_MODEL), F32)],
        scratch_shapes=[pltpu.VMEM((D_INNER, D_MODEL), F32)],
        compiler_params=_cp(("arbitrary",)),
    )(dxn, y, _arr(gate), ycat, wo)


def _loss_head(x, fw, target):
    S = x.shape[0]
    tm = min(512, S)

    def body(x_ref, fw_ref, t_ref, dx_ref, red_ref):
        @pl.when(pl.program_id(0) == 0)
        def _():
            red_ref[...] = jnp.zeros_like(red_ref)

        xv = x_ref[...]
        fwv = fw_ref[...]
        inv = lax.rsqrt(jnp.mean(xv * xv, axis=-1, keepdims=True) + EPS)
        xhat = xv * inv
        err = xhat * fwv - t_ref[...]
        col = jnp.sum(err * err, axis=0, keepdims=True)
        red_ref[1:2, :] += jnp.broadcast_to(jnp.sum(col, axis=1, keepdims=True) * (0.5 / D_MODEL), (1, D_MODEL))
        dy = err * (1.0 / D_MODEL)
        red_ref[0:1, :] += jnp.sum(dy * xhat, axis=0, keepdims=True)
        dxhat = dy * fwv
        dx_ref[...] = inv * (dxhat - xhat * jnp.mean(dxhat * xhat, axis=-1, keepdims=True))

    row = pl.BlockSpec((tm, D_MODEL), lambda i: (i, 0))
    return pl.pallas_call(
        body, name="loss_head", grid=(S // tm,),
        in_specs=[row, _vec(D_MODEL), row],
        out_specs=[row, _full((8, D_MODEL))],
        out_shape=[SDS((S, D_MODEL), F32), SDS((8, D_MODEL), F32)],
        compiler_params=_cp(("arbitrary",)),
    )(x, fw, target)


ADA_COLS = 3 * D_MODEL // N_DEV


def _ada_fwd(c_all, w_ada, b_cols):
    def body(c_ref, w_ref, b_ref, out_ref):
        out_ref[...] = _mm(_bf(_silu(c_ref[...])), _bf(w_ref[...])) + b_ref[...]

    return pl.pallas_call(
        body, name="ada_fwd", grid=(DEPTH,),
        in_specs=[_full((N_DEV, D_MODEL)), pl.BlockSpec((None, D_MODEL, ADA_COLS), lambda l: (l, 0, 0)),
                  pl.BlockSpec((None, 1, ADA_COLS), lambda l: (l, 0, 0))],
        out_specs=pl.BlockSpec((None, N_DEV, ADA_COLS), lambda l: (l, 0, 0)),
        out_shape=SDS((DEPTH, N_DEV, ADA_COLS), F32),
        compiler_params=_cp(("parallel",)),
    )(c_all, w_ada, b_cols)


def _ada_bwd(ct_pad, dmod_pad):
    def body(c_ref, d_ref, out_ref):
        out_ref[...] = _mm(_bf(_silu(c_ref[...])), _bf(d_ref[...]))

    return pl.pallas_call(
        body, name="ada_bwd", grid=(DEPTH,),
        in_specs=[_full((D_MODEL, LANE)), pl.BlockSpec((None, LANE, ADA_COLS), lambda l: (l, 0, 0))],
        out_specs=pl.BlockSpec((None, D_MODEL, ADA_COLS), lambda l: (l, 0, 0)),
        out_shape=SDS((DEPTH, D_MODEL, ADA_COLS), F32),
        compiler_params=_cp(("parallel",)),
    )(ct_pad, dmod_pad)


def _adamw(parts, w, m, v, name, own=None, layers=None, prev=None):
    n, L, R, C = parts.shape
    lo, hi = layers or (0, L)
    tr = R
    while tr * C * 4 > (1 << 20) and tr % 16 == 0:
        tr //= 2
    first = 1 if own is None else 2

    def body(*refs):
        p_ref = refs[0]
        w_ref, m_ref, v_ref = refs[first:first + 3]
        g_ref, d_ref, mo_ref, vo_ref = refs[-4:]

        def part(k):
            if own is None:
                return p_ref[k].astype(F32)
            me = 4 * lax.axis_index("x") + 2 * lax.axis_index("y") + lax.axis_index("c")
            return jnp.where(me == k, refs[1][...], p_ref[k]).astype(F32)

        g = part(0)
        for k in range(1, n):
            g = g + part(k)
        mn = ADAM_B1 * m_ref[...] + (1.0 - ADAM_B1) * g
        vn = ADAM_B2 * v_ref[...] + (1.0 - ADAM_B2) * (g * g)
        m_hat = mn / (1.0 - ADAM_B1 ** ADAM_STEP)
        v_hat = vn / (1.0 - ADAM_B2 ** ADAM_STEP)
        g_ref[...] = g
        d_ref[...] = -ADAM_LR * (m_hat / (jnp.sqrt(v_hat) + ADAM_EPS) + ADAM_WD * w_ref[...])
        mo_ref[...] = mn
        vo_ref[...] = vn

    blk = pl.BlockSpec((None, tr, C), lambda l, i: (lo + l, i, 0))
    own_blk = [] if own is None else [pl.BlockSpec((None, tr, C), lambda l, i: (l, i, 0))]
    n_blk = 3 if own is None else 4
    return pl.pallas_call(
        body, name=name, grid=(hi - lo, R // tr),
        in_specs=[pl.BlockSpec((n, None, tr, C), lambda l, i: (0, lo + l, i, 0))] + own_blk + [blk] * 3
        + ([] if prev is None else [ANY] * 4),
        out_specs=[blk] * 4,
        out_shape=[SDS((L, R, C), F32)] * 4,
        input_output_aliases={} if prev is None else {1 + n_blk + k: k for k in range(4)},
        compiler_params=_cp(("parallel", "parallel")),
    )(parts, *([] if own is None else [own]), w, m, v, *([] if prev is None else prev))


MESH = pl.DeviceIdType.MESH
ANY = pl.BlockSpec(memory_space=pl.ANY)


def _all_gather(v, name):
    def body(v_ref, out_ref, send_sems, recv_sems, local_sem):
        x, y, c = lax.axis_index("x"), lax.axis_index("y"), lax.axis_index("c")
        me, sibling = (x, y, c), (x, y, 1 - c)
        chips = [(1 - x, y), (x, 1 - y), (1 - x, 1 - y)]

        def slot(px, py, pc):
            return out_ref.at[4 * px + 2 * py + pc]

        def copy(k, block, to, src=None):
            return pltpu.make_async_remote_copy(
                src_ref=slot(*block) if src is None else src, dst_ref=slot(*block),
                send_sem=send_sems.at[k], recv_sem=recv_sems.at[k], device_id=to, device_id_type=MESH)

        mine = pltpu.make_async_copy(v_ref, slot(*me), local_sem)
        mine.start()
        first = [copy(0, me, sibling, src=v_ref)]
        first += [copy(1 + j, me, (*chip, c), src=v_ref) for j, chip in enumerate(chips)]
        for cp in first:
            cp.start()
        passed = [copy(4 + j, (*chip, c), sibling) for j, chip in enumerate(chips)]
        for j, chip in enumerate(chips):
            copy(1 + j, (*chip, c), me).wait_recv()
            passed[j].start()
        copy(0, sibling, me).wait_recv()
        for j, chip in enumerate(chips):
            copy(4 + j, (*chip, 1 - c), me).wait_recv()
        for cp in first + passed:
            cp.wait_send()
        mine.wait()

    return pl.pallas_call(
        body, name=name, in_specs=[ANY], out_specs=ANY,
        out_shape=SDS((N_DEV,) + v.shape, v.dtype),
        scratch_shapes=[pltpu.SemaphoreType.DMA((7,)), pltpu.SemaphoreType.DMA((7,)), pltpu.SemaphoreType.DMA],
    )(v)


def _all_to_all(v, name):
    def body(v_ref, out_ref, send_sems, recv_sems, local_sem):
        x, y, c = lax.axis_index("x"), lax.axis_index("y"), lax.axis_index("c")
        mine_idx = 4 * x + 2 * y + c
        mine = pltpu.make_async_copy(v_ref.at[mine_idx], out_ref.at[mine_idx], local_sem)
        mine.start()
        sends, recvs = [], []
        for k in range(1, N_DEV):
            px = 1 - x if k & 4 else x
            py = 1 - y if k & 2 else y
            pc = 1 - c if k & 1 else c
            peer_idx = 4 * px + 2 * py + pc
            sems = dict(send_sem=send_sems.at[k - 1], recv_sem=recv_sems.at[k - 1], device_id=(px, py, pc),
                        device_id_type=MESH)
            sends.append(pltpu.make_async_remote_copy(src_ref=v_ref.at[peer_idx], dst_ref=out_ref.at[mine_idx], **sems))
            recvs.append(pltpu.make_async_remote_copy(src_ref=v_ref.at[peer_idx], dst_ref=out_ref.at[peer_idx], **sems))
        for cp in sends:
            cp.start()
        for cp in recvs:
            cp.wait_recv()
        for cp in sends:
            cp.wait_send()
        mine.wait()

    return pl.pallas_call(
        body, name=name, in_specs=[ANY], out_specs=ANY,
        out_shape=SDS(v.shape, v.dtype),
        scratch_shapes=[pltpu.SemaphoreType.DMA((7,)), pltpu.SemaphoreType.DMA((7,)), pltpu.SemaphoreType.DMA],
    )(v)


HBM_SPEC = pl.BlockSpec(memory_space=pltpu.HBM)
SEM_SPEC = pl.BlockSpec(memory_space=pltpu.SEMAPHORE)
EFFECT = pltpu.SideEffectType.DATAFLOW_SIDE_EFFECTING


EXCHANGE_PEERS = {"gather": range(1, N_DEV), "scatter": range(1, N_DEV), "chip": (1, 2, 4, 6), "pass": (2, 4, 6)}


def _exchange_copies(srcs, lands, send_sems, recv_sems, mode, layer):
    x, y, c = lax.axis_index("x"), lax.axis_index("y"), lax.axis_index("c")
    me = 4 * x + 2 * y + c
    copies = []
    for a, (src, land) in enumerate(zip(srcs, lands)):
        for k in EXCHANGE_PEERS[mode]:
            px = 1 - x if k & 4 else x
            py = 1 - y if k & 2 else y
            pc = 1 - c if k & 1 else c
            peer = 4 * px + 2 * py + pc
            if mode == "scatter":
                s, d, to = src.at[peer], land.at[me, layer], (px, py, pc)
            elif mode == "pass":
                s, d, to = land.at[peer], land.at[peer], (x, y, 1 - c)
            else:
                s, d, to = src, land.at[me], (px, py, pc)
            n = 7 * a + k - 1
            copies.append(pltpu.make_async_remote_copy(
                src_ref=s, dst_ref=d, send_sem=send_sems.at[n], recv_sem=recv_sems.at[n], device_id=to,
                device_id_type=MESH))
    return copies


def _exchange_start(name, srcs, lands, mode, layer=0, after=None):
    n = len(srcs)

    def body(*refs):
        send_sems, recv_sems = refs[-2 * n - 3], refs[-2 * n - 2]
        for cp in _exchange_copies(refs[:n], refs[n:2 * n], send_sems, recv_sems, mode, layer):
            cp.start()
        refs[-1][...] = jnp.zeros_like(refs[-1])

    arrays = list(srcs) + list(lands)
    sems = pltpu.SemaphoreType.DMA((7 * n,))
    out = pl.pallas_call(
        body, name=name,
        out_shape=(sems, sems, *[pltpu.HBM(v.shape, v.dtype) for v in arrays], SDS((8, LANE), F32)),
        in_specs=[HBM_SPEC] * (2 * n) + ([ANY] if after is not None else []),
        out_specs=(SEM_SPEC, SEM_SPEC, *[HBM_SPEC] * (2 * n), pl.BlockSpec(memory_space=pltpu.VMEM)),
        input_output_aliases={i: 2 + i for i in range(2 * n)},
        compiler_params=pltpu.CompilerParams(has_side_effects=EFFECT),
    )(*[pltpu.with_memory_space_constraint(v, pltpu.HBM) for v in arrays], *([after] if after is not None else []))
    return dict(sems=out[:2], srcs=out[2:2 + n], lands=out[2 + n:2 + 2 * n], token=out[-1], mode=mode,
                layer=layer)


def _exchange_wait(name, st, after, also=()):
    n = len(st["srcs"])

    def body(*refs):
        send_sems, recv_sems = refs[2 * n], refs[2 * n + 1]
        for cp in _exchange_copies(refs[:n], refs[n:2 * n], send_sems, recv_sems, st["mode"], st["layer"]):
            cp.wait_send()
            cp.wait_recv()

    arrays = list(st["srcs"]) + list(st["lands"])
    out = pl.pallas_call(
        body, name=name,
        out_shape=tuple(pltpu.HBM(v.shape, v.dtype) for v in arrays),
        in_specs=[HBM_SPEC] * (2 * n) + [SEM_SPEC, SEM_SPEC] + [ANY] * (1 + len(also)),
        out_specs=tuple([HBM_SPEC] * (2 * n)),
        input_output_aliases={i: i for i in range(2 * n)},
        compiler_params=pltpu.CompilerParams(has_side_effects=EFFECT),
    )(*arrays, *st["sems"], after, *also)
    st["srcs"] = out[:n]
    return out[n:]


_IN_PIECES = ([(1024, 3072)]
              + [r for t in range(4) for r in ((LANE * t, LANE * (t + 1)), (512 + LANE * t, 512 + LANE * (t + 1)))]
              + [(4096, 5632), (3072, 4096), (5632, 5648)])


def _permute_in(w):
    pad = jnp.zeros(w.shape[:-1] + (N_PAD - N_IN,), w.dtype)
    return jnp.concatenate([w[..., a:b] for a, b in _IN_PIECES] + [pad], axis=-1)


def _unpermute_in(g):
    ax = [g[..., OFF_LRU + 2 * LANE * t:OFF_LRU + 2 * LANE * t + LANE] for t in range(4)]
    ag = [g[..., OFF_LRU + 2 * LANE * t + LANE:OFF_LRU + 2 * LANE * (t + 1)] for t in range(4)]
    return jnp.concatenate(ax + ag + [g[..., 0:2048], g[..., OFF_Z:OFF_Z + SSD_W], g[..., OFF_XBC:OFF_XBC + SSD_CONV],
                                      g[..., OFF_Z + SSD_W:OFF_Z + SSD_W + SSD_HEADS]], axis=-1)


SHARD_COLS = N_IN // N_DEV


def _in_segments():
    segs, pos = [], 0
    for a, b in _IN_PIECES:
        for i in range(N_DEV):
            lo, hi = max(a, SHARD_COLS * i), min(b, SHARD_COLS * (i + 1))
            if lo < hi:
                segs.append((i, lo - SHARD_COLS * i, hi - lo, pos + lo - a))
        pos += b - a
    return segs


RELAYOUT_ROWS = 256


def _relayout_in(land, own):
    def body(land_ref, own_ref, out_ref):
        me = 4 * lax.axis_index("x") + 2 * lax.axis_index("y") + lax.axis_index("c")
        out_ref[:, N_IN:N_PAD] = jnp.zeros((RELAYOUT_ROWS, N_PAD - N_IN), BF16)
        for i, j, wd, p in _in_segments():
            out_ref[:, p:p + wd] = jnp.where(me == i, own_ref[:, j:j + wd], land_ref[i, :, j:j + wd])

    return pl.pallas_call(
        body, name="relayout_in", grid=(D_MODEL // RELAYOUT_ROWS,),
        in_specs=[pl.BlockSpec((N_DEV, RELAYOUT_ROWS, SHARD_COLS), lambda r: (0, r, 0)),
                  pl.BlockSpec((RELAYOUT_ROWS, SHARD_COLS), lambda r: (r, 0))],
        out_specs=pl.BlockSpec((RELAYOUT_ROWS, N_PAD), lambda r: (r, 0)),
        out_shape=SDS((D_MODEL, N_PAD), BF16),
        compiler_params=_cp(("parallel",)),
    )(land, own)


def _relayout_grad(g):
    def body(g_ref, out_ref):
        for i, j, wd, p in _in_segments():
            out_ref[i, :, j:j + wd] = g_ref[:, p:p + wd].astype(BF16)

    return pl.pallas_call(
        body, name="relayout_grad", grid=(D_MODEL // RELAYOUT_ROWS,),
        in_specs=[pl.BlockSpec((RELAYOUT_ROWS, N_PAD), lambda r: (r, 0))],
        out_specs=pl.BlockSpec((N_DEV, RELAYOUT_ROWS, SHARD_COLS), lambda r: (0, r, 0)),
        out_shape=SDS((N_DEV, D_MODEL, SHARD_COLS), BF16),
        compiler_params=_cp(("parallel",)),
    )(g)


def _block_diag(w):
    w4 = w.reshape(DEPTH, 4, 2, 64, 64)
    z = jnp.zeros((DEPTH, 4, 64, 64), w.dtype)
    top = jnp.concatenate([w4[:, :, 0], z], axis=-1)
    bot = jnp.concatenate([z, w4[:, :, 1]], axis=-1)
    return jnp.concatenate([top, bot], axis=2).astype(BF16)


def _diag_blocks(g):
    return jnp.stack([g[:, :, :64, :64], g[:, :, 64:, 64:]], axis=2).reshape(DEPTH, 8, 64, 64)


def _pad_lanes(v):
    return jnp.pad(v, ((0, 0), (0, LANE - v.shape[1])))


def _lower_bounds(logits):
    p = jax.nn.softmax(logits, axis=0)
    return p, jnp.cumsum(p, axis=0) - p[0]


def _lower_bounds_bwd(p, dlb):
    dp = jnp.cumsum(dlb[::-1], axis=0)[::-1]
    dp = dp.at[0].add(-jnp.sum(dlb, axis=0))
    return p * (dp - jnp.sum(dp * p, axis=0, keepdims=True))


SMALL = ["norm_w", "b_ada", "lru_conv_b", "lru_wa", "lru_ba", "lru_wx", "lru_bx", "lru_lambda", "hg_lb_logits",
         "hg_norm_w", "ssd_conv_b", "ssd_dt_bias", "ssd_a_log", "ssd_d", "ssd_norm_w", "final_norm_w"]
WEIGHTS = ["norm_w", "w_ada", "b_ada", "w_in", "lru_conv_w", "lru_conv_b", "lru_wa", "lru_ba", "lru_wx", "lru_bx",
           "lru_lambda", "hg_lb_logits", "hg_norm_w", "ssd_conv_w", "ssd_conv_b", "ssd_dt_bias", "ssd_a_log", "ssd_d",
           "ssd_norm_w", "w_out", "final_norm_w"]
INPUTS = ["x", "c"] + WEIGHTS + ["loss_target"] + ["m_" + n for n in WEIGHTS] + ["v_" + n for n in WEIGHTS]
SMALL_ROW = 1024


def _small_rows(like):
    out, off = {}, 0
    for n in SMALL:
        rows = -(-int(np.prod(like[n].shape)) // (8 * SMALL_ROW)) * 8
        out[n] = (off, rows)
        off += rows
    return out, off


def _flatten_small(d, prefix="", last=0.0):
    table, _ = _small_rows({n: d[prefix + n] for n in SMALL})
    pieces = []
    for n in SMALL:
        flat = d[prefix + n].reshape(-1)
        pieces.append(jnp.pad(flat, (0, table[n][1] * SMALL_ROW - flat.shape[0])).reshape(-1, SMALL_ROW))
    return jnp.concatenate(pieces + [jnp.full((8, SMALL_ROW), last, F32)], axis=0)


def _split_small(packed, like):
    table, _ = _small_rows(like)
    out = {}
    for n in SMALL:
        off, rows = table[n]
        size = int(np.prod(like[n].shape))
        out[n] = packed[off:off + rows].reshape(-1)[:size].reshape(like[n].shape)
    return out


def _local_step(x, mod, target, w, fetch, emit):
    S = x.shape[0]
    mall = _bfc(_hg_consts())
    mall_t = _bfc(_hg_consts().T)
    consts = _ssd_consts()
    p_lb, lbs = _lower_bounds(w["hg_lb_logits"])
    no_tok = jnp.zeros((8, LANE), F32)
    wa, wx = _block_diag(w["lru_wa"]), _block_diag(w["lru_wx"])
    ba, bx = w["lru_ba"].reshape(DEPTH, 1, LRU_W), w["lru_bx"].reshape(DEPTH, 1, LRU_W)
    lru_cb, lam, ssd_cb = w["lru_conv_b"][:, None], w["lru_lambda"][:, None], w["ssd_conv_b"][:, None]
    bias, alog = _pad_lanes(w["ssd_dt_bias"]), _pad_lanes(w["ssd_a_log"])
    dskip = jnp.repeat(w["ssd_d"], SSD_P, axis=1)
    saved = []
    for l in range(DEPTH):
        w_in_l, w_out_l, token = fetch(l, x)
        shift, scale, gate = (_Row(mod, l, D_MODEL, k) for k in range(3))
        nw = _Row(w["norm_w"], l)
        u, h = _inproj_fwd(x, nw, scale, shift, w_in_l, no_tok if token is None else token)
        ycat = lax.empty((S, D_INNER), BF16)
        lru_args = (l, u, w["lru_conv_w"], lru_cb, wa, ba, wx, bx, lam)
        ycat, h_lru = _lru_fwd(*lru_args, ycat)
        hg_args = (u, _Row(lbs, l), _Row(w["hg_norm_w"], l), mall)
        ycat, o_b, hg_st = _hg_fwd(*hg_args, ycat)
        xbc = _ssdconv_fwd(l, u, w["ssd_conv_w"], ssd_cb)
        ssd_args = (u, xbc, _Row(bias, l), _Row(alog, l), _Row(dskip, l), _Row(w["ssd_norm_w"], l), consts)
        ycat, y_ssd, ssd_st = _ssd_fwd(*ssd_args, ycat)
        token = fetch(l, y_ssd, late=True)
        x_new, y = _outproj_fwd(ycat, w_out_l, x, gate, no_tok if token is None else token)
        saved.append((x, u, h, ycat, nw, scale, gate, w_in_l, w_out_l, lru_args, h_lru, hg_args, o_b, hg_st, ssd_args,
                      y_ssd, ssd_st, y))
        x = x_new
    dx, red = _loss_head(x, w["final_norm_w"][None, :], target)
    loss = red[1, 0]
    reds = {k: [None] * DEPTH for k in ("in", "gate", "lru", "wa", "wx", "hg", "conv", "ssd")}
    for l in reversed(range(DEPTH)):
        (x, u, h, ycat, nw, scale, gate, w_in_l, w_out_l, lru_args, h_lru, hg_args, o_b, hg_st, ssd_args, y_ssd, ssd_st,
         y) = saved[l]
        dycat, g_out, reds["gate"][l] = _outproj_bwd(dx, y, gate, ycat, w_out_l)
        token = emit(l, "w_out", g_out)
        du = lax.empty((S, N_PAD), BF16)
        du, dxbc, reds["ssd"][l] = _ssd_bwd(*ssd_args, y_ssd, ssd_st, dycat, du, no_tok if token is None else token)
        du, reds["conv"][l] = _ssdconv_bwd(l, u, w["ssd_conv_w"], ssd_cb, dxbc, du)
        du, reds["hg"][l] = _hg_bwd(*hg_args, mall_t, o_b, hg_st, dycat, du)
        du, reds["lru"][l], reds["wa"][l], reds["wx"][l] = _lru_bwd(*lru_args, h_lru, dycat, du)
        token = emit(l, "w_in", _inproj_bwd_w(h, du))
        dx, reds["in"][l] = _inproj_bwd_x(du, w_in_l, x, nw, scale, dx, no_tok if token is None else token)
    r = {k: jnp.stack(v) for k, v in reds.items()}
    g = {n: None for n in WEIGHTS}
    g["final_norm_w"] = red[0]
    g["norm_w"] = r["in"][:, 2]
    dmod = jnp.concatenate([r["in"][:, 0], r["in"][:, 1], r["gate"][:, 0]], axis=1)
    g["lru_conv_w"], g["lru_conv_b"] = r["lru"][:, 0:4], r["lru"][:, 4]
    g["lru_ba"], g["lru_bx"] = r["lru"][:, 5].reshape(DEPTH, 8, 64), r["lru"][:, 6].reshape(DEPTH, 8, 64)
    g["lru_lambda"] = r["lru"][:, 7]
    g["lru_wa"], g["lru_wx"] = _diag_blocks(r["wa"]), _diag_blocks(r["wx"])
    g["hg_norm_w"] = r["hg"][:, 0]
    g["hg_lb_logits"] = _lower_bounds_bwd(p_lb, r["hg"][:, 1])
    g["ssd_conv_w"], g["ssd_conv_b"] = r["conv"][:, 0:4], r["conv"][:, 4]
    g["ssd_norm_w"] = r["ssd"][:, 0]
    g["ssd_d"] = r["ssd"][:, 1].reshape(DEPTH, SSD_HEADS, SSD_P).sum(-1)
    g["ssd_dt_bias"] = r["ssd"][:, 2, :SSD_HEADS]
    g["ssd_a_log"] = -r["ssd"][:, 3, :SSD_HEADS] * jnp.exp(w["ssd_a_log"])
    return loss, dx, dmod, g


def kernel(x, c, norm_w, w_ada, b_ada, w_in, lru_conv_w, lru_conv_b, lru_wa, lru_ba, lru_wx, lru_bx, lru_lambda, hg_lb_logits, hg_norm_w, ssd_conv_w, ssd_conv_b, ssd_dt_bias, ssd_a_log, ssd_d, ssd_norm_w, w_out, final_norm_w, loss_target, m_norm_w, m_w_ada, m_b_ada, m_w_in, m_lru_conv_w, m_lru_conv_b, m_lru_wa, m_lru_ba, m_lru_wx, m_lru_bx, m_lru_lambda, m_hg_lb_logits, m_hg_norm_w, m_ssd_conv_w, m_ssd_conv_b, m_ssd_dt_bias, m_ssd_a_log, m_ssd_d, m_ssd_norm_w, m_w_out, m_final_norm_w, v_norm_w, v_w_ada, v_b_ada, v_w_in, v_lru_conv_w, v_lru_conv_b, v_lru_wa, v_lru_ba, v_lru_wx, v_lru_bx, v_lru_lambda, v_hg_lb_logits, v_hg_norm_w, v_ssd_conv_w, v_ssd_conv_b, v_ssd_dt_bias, v_ssd_a_log, v_ssd_d, v_ssd_norm_w, v_w_out, v_final_norm_w):
    return _step(x, c, norm_w, w_ada, b_ada, w_in, lru_conv_w, lru_conv_b, lru_wa, lru_ba, lru_wx, lru_bx, lru_lambda, hg_lb_logits, hg_norm_w, ssd_conv_w, ssd_conv_b, ssd_dt_bias, ssd_a_log, ssd_d, ssd_norm_w, w_out, final_norm_w, loss_target, m_norm_w, m_w_ada, m_b_ada, m_w_in, m_lru_conv_w, m_lru_conv_b, m_lru_wa, m_lru_ba, m_lru_wx, m_lru_bx, m_lru_lambda, m_hg_lb_logits, m_hg_norm_w, m_ssd_conv_w, m_ssd_conv_b, m_ssd_dt_bias, m_ssd_a_log, m_ssd_d, m_ssd_norm_w, m_w_out, m_final_norm_w, v_norm_w, v_w_ada, v_b_ada, v_w_in, v_lru_conv_w, v_lru_conv_b, v_lru_wa, v_lru_ba, v_lru_wx, v_lru_bx, v_lru_lambda, v_hg_lb_logits, v_hg_norm_w, v_ssd_conv_w, v_ssd_conv_b, v_ssd_dt_bias, v_ssd_a_log, v_ssd_d, v_ssd_norm_w, v_w_out, v_final_norm_w)


def _step(*args):
    a = dict(zip(INPUTS, args, strict=True))
    me = 4 * lax.axis_index("x") + 2 * lax.axis_index("y") + lax.axis_index("c")
    x, target = a["x"][0], a["loss_target"][0]

    c_all = _all_gather(a["c"], "gather_c")[:, 0, :]
    b_cols = lax.dynamic_slice_in_dim(a["b_ada"], me * ADA_COLS, ADA_COLS, axis=1)[:, None, :]
    mod_parts = _all_gather(_ada_fwd(c_all, a["w_ada"], b_cols), "gather_mod")
    mod = lax.dynamic_index_in_dim(mod_parts, me, axis=2, keepdims=False)
    mod = mod.transpose(1, 0, 2).reshape(DEPTH, 3 * D_MODEL)

    w = {n: a[n] for n in SMALL}

    w_in_b = [a["w_in"][l].astype(BF16) for l in range(DEPTH)]
    w_out_b = a["w_out"].astype(BF16)
    conv_own = jnp.concatenate([a["lru_conv_w"], a["ssd_conv_w"]], axis=-1)
    cols, rows_out = N_IN // N_DEV, D_INNER // N_DEV

    def gather_start(l, after):
        srcs = [w_in_b[l], w_out_b[l]] + ([conv_own] if l == 0 else [])
        lands = [lax.empty((N_DEV,) + s.shape, s.dtype) for s in srcs]
        return _exchange_start(f"gather_start_{l}", srcs, lands, "chip", after=after)

    def gather_pass(name, st, after, also=()):
        landed = _exchange_wait(name + "_wait", st, after, also)
        st2 = _exchange_start(name + "_pass", st["srcs"], landed, "pass")
        return _exchange_wait(name + "_passed", st2, after)

    gathers = {0: gather_start(0, mod)}
    passing = {}

    def fetch(l, x_l, late=False):
        if late:
            if l + 1 == DEPTH:
                return None
            landed = _exchange_wait(f"gather_{l + 1}_wait", gathers[l + 1], x_l)
            passing[l + 1] = _exchange_start(f"gather_{l + 1}_pass", gathers[l + 1]["srcs"], landed, "pass")
            return passing[l + 1]["token"]
        if l == 0:
            landed = gather_pass("gather_0", gathers[0], x_l, also=(a["w_in"], a["m_w_in"], a["v_w_in"]))
        else:
            landed = _exchange_wait(f"gather_{l}_passed", passing[l], x_l)
        land_out = lax.dynamic_update_index_in_dim(landed[1], w_out_b[l], me, 0)
        if l == 0:
            conv = lax.dynamic_update_index_in_dim(landed[2], conv_own, me, 0).transpose(1, 2, 0, 3)
            w["lru_conv_w"] = conv[..., :64].reshape(DEPTH, 4, LRU_W)
            w["ssd_conv_w"] = conv[..., 64:].reshape(DEPTH, 4, SSD_CONV)
        token = None
        if l + 1 < DEPTH:
            gathers[l + 1] = gather_start(l + 1, land_out)
            token = gathers[l + 1]["token"]
        return _relayout_in(landed[0], w_in_b[l]), land_out.reshape(D_INNER, D_MODEL), token

    scatters = {"w_in": {}, "w_out": {}}
    lands = {"w_in": lax.empty((N_DEV, DEPTH, D_MODEL, cols), BF16),
             "w_out": lax.empty((N_DEV, DEPTH, rows_out, D_MODEL), BF16)}
    own = {"w_in": [None] * DEPTH, "w_out": [None] * DEPTH}

    deferred = {}

    def emit(l, name, grad, after=None):
        if name == "w_in" and l == 0 and after is None:
            deferred["w_in"] = grad
            return None
        grad = _relayout_grad(grad) if name == "w_in" else grad.reshape(N_DEV, rows_out, D_MODEL)
        st = _exchange_start(f"scatter_start_{name}_{l}", [grad], [lands[name]], "scatter", layer=l, after=after)
        scatters[name][l] = st
        lands[name] = st["lands"][0]
        return st["token"]

    loss_own, dx, dmod, g = _local_step(x, mod, target, w, fetch, emit)

    def sharded(name, parts, own=None, **kw):
        return _adamw(parts, a[name], a["m_" + name], a["v_" + name], "adamw_" + name + kw.pop("tag", ""), own=own, **kw)

    g["b_ada"] = dmod
    small_own = _flatten_small(g, last=loss_own)
    small_st = _exchange_start("gather_small", [small_own], [lax.empty((N_DEV,) + small_own.shape, F32)], "chip",
                               after=dx)
    big = {}
    after = emit(0, "w_in", deferred["w_in"], after=small_st["token"]) + dx[0:8, 0:LANE]

    def own_slice(st):
        return lax.dynamic_index_in_dim(st["srcs"][0], me, 0, keepdims=False)

    upper = {}
    for name in ("w_out", "w_in"):
        for l in reversed(range(1, DEPTH)):
            scatters[name][l]["lands"] = [lands[name]]
            lands[name] = _exchange_wait(f"scatter_wait_{name}_{l}", scatters[name][l], after)[0]
            own[name][l] = own_slice(scatters[name][l])
        upper[name] = sharded(name, lands[name], jnp.stack(own[name][1:]), layers=(1, DEPTH), tag="_upper")
        after = upper[name][1]
    small = gather_pass("gather_small", small_st, after)[0]
    outs = _adamw(small[:, None], *[_flatten_small(a, p)[None] for p in ("", "m_", "v_")], "adamw_small",
                  own=small_own[None])
    res = [_split_small(o[0], a) for o in outs]
    losses = lax.dynamic_update_index_in_dim(small[:, -1, 0], loss_own, me, 0)
    loss = jnp.sum(losses)

    off = _small_rows(a)[0]["b_ada"][0]
    dmod_all = lax.dynamic_update_index_in_dim(small[:, off:off + DEPTH * 3 * D_MODEL // SMALL_ROW],
                                               dmod.reshape(-1, SMALL_ROW), me, 0)
    dmod_all = dmod_all.reshape(N_DEV, DEPTH, 3 * D_MODEL).transpose(1, 0, 2)
    dmod_cols = lax.dynamic_slice_in_dim(dmod_all, me * ADA_COLS, ADA_COLS, axis=2)
    dmod_pad = jnp.pad(dmod_cols, ((0, 0), (0, LANE - N_DEV), (0, 0)))
    ct_pad = jnp.pad(c_all.T, ((0, 0), (0, LANE - N_DEV)))
    big["w_ada"] = sharded("w_ada", _ada_bwd(ct_pad, dmod_pad)[None])
    g_conv = jnp.concatenate([g["lru_conv_w"].reshape(DEPTH, 4, N_DEV, 64), g["ssd_conv_w"].reshape(DEPTH, 4, N_DEV, 192)],
                             axis=-1).transpose(2, 0, 1, 3)
    conv_parts = _all_to_all(g_conv, "scatter_conv")
    big["lru_conv_w"] = sharded("lru_conv_w", conv_parts[..., :64])
    big["ssd_conv_w"] = sharded("ssd_conv_w", conv_parts[..., 64:])

    after = outs[1] + big["w_ada"][1][0, 0:1, 0:1]
    for name in ("w_out", "w_in"):
        scatters[name][0]["lands"] = [lands[name]]
        lands[name] = _exchange_wait(f"scatter_wait_{name}_0", scatters[name][0], after)[0]
        big[name] = sharded(name, lands[name], own_slice(scatters[name][0])[None], layers=(0, 1), prev=upper[name])
        after = big[name][1]

    out = [loss, dx[None]]
    for k in range(4):
        out += [big[n][k] if n in big else res[k][n] for n in WEIGHTS]
    return tuple(out)
```

```python
import functools

import numpy as np
import jax
import jax.numpy as jnp
from jax import lax
from jax.experimental import pallas as pl
from jax.experimental.pallas import tpu as pltpu

F32 = jnp.float32
BF16 = jnp.bfloat16
SDS = jax.ShapeDtypeStruct

N_DEV = 8
DEPTH = 4
D_MODEL = 1024
D_INNER = 2048
EPS = 1e-6
LRU_W = 512
LRU_C = 8.0
HG_W = 512
HG_CHUNK = 64
HG_HEADS = 4
SSD_W = 1024
SSD_HEADS = 16
SSD_P = 64
SSD_N = 128
SSD_CHUNK = 128
SSD_CONV = 1536
N_IN = 5648
N_PAD = 5760
OFF_HG = 0
OFF_LRU = 2048
OFF_XBC = 3072
OFF_Z = 4608
LANE = 128
VMEM_LIMIT = 56 * 1024 * 1024
NEG = -1e30

ADAM_LR = 0.001
ADAM_B1 = 0.9
ADAM_B2 = 0.999
ADAM_EPS = 1e-08
ADAM_WD = 0.01
ADAM_STEP = 10


def _cp(sem=None):
    return pltpu.CompilerParams(dimension_semantics=sem, vmem_limit_bytes=VMEM_LIMIT)


def _dg(a, b, ca, cb):
    return lax.dot_general(a, b, (((ca,), (cb,)), ((), ())), preferred_element_type=F32)


def _mm(a, b):
    return _dg(a, b, 1, 0)


def _mm_nt(a, b):
    return _dg(a, b, 1, 1)


def _mm_tn(a, b):
    return _dg(a, b, 0, 0)


def _bf(x):
    return x.astype(BF16)


def _f(x):
    return x.astype(F32)


def _split3(x):
    hi = x.astype(BF16)
    r = x - hi.astype(F32)
    mid = r.astype(BF16)
    lo = (r - mid.astype(F32)).astype(BF16)
    return hi, mid, lo


def _sel_r(x, m):
    hi, mid, lo = _split3(x)
    return _mm(hi, m) + _mm(mid, m) + _mm(lo, m)


def _sel_l(m, x):
    hi, mid, lo = _split3(x)
    return _mm(m, hi) + _mm(m, mid) + _mm(m, lo)


def _sel_l2(m, x):
    hi = x.astype(BF16)
    lo = (x - hi.astype(F32)).astype(BF16)
    return _mm(m, hi) + _mm(m, lo)


def _sel_tn(x, m):
    hi, mid, lo = _split3(x)
    return _mm_tn(hi, m) + _mm_tn(mid, m) + _mm_tn(lo, m)


def _sigmoid(x):
    return 1.0 / (1.0 + jnp.exp(-x))


def _silu(x):
    return x * _sigmoid(x)


def _dsilu(x):
    s = _sigmoid(x)
    return s * (1.0 + x * (1.0 - s))


def _softplus(x):
    return jnp.maximum(x, 0.0) + jnp.log(1.0 + jnp.exp(-jnp.abs(x)))


def _expm1(z):
    series = z * (1.0 + z * (1.0 / 2) * (1.0 + z * (1.0 / 3) * (1.0 + z * (1.0 / 4) * (
        1.0 + z * (1.0 / 5) * (1.0 + z * (1.0 / 6) * (1.0 + z * (1.0 / 7)))))))
    return jnp.where(jnp.abs(z) < 0.3, series, jnp.exp(z) - 1.0)


def _iota(shape, dim):
    return lax.broadcasted_iota(jnp.int32, shape, dim)


def _last_row(x, rows):
    return jnp.sum(jnp.where(rows == x.shape[0] - 1, x, 0.0), axis=0, keepdims=True)


def _shift_down(x, d, rows, fill=0.0):
    return jnp.where(rows >= d, pltpu.roll(x, d, 0), fill)


def _shift_up(x, d, rows, fill=0.0):
    n = x.shape[0]
    return jnp.where(rows < n - d, pltpu.roll(x, n - d, 0), fill)


def _conv_fwd(x, cw_ref, cb_ref, rows):
    out = cb_ref[...] + cw_ref[pl.ds(3, 1), :] * x
    for k in range(3):
        out = out + cw_ref[pl.ds(k, 1), :] * _shift_down(x, 3 - k, rows)
    return out


def _conv_bwd(x, dco, cw_ref, rows):
    dx = cw_ref[pl.ds(3, 1), :] * dco
    dws = []
    for k in range(3):
        dx = dx + cw_ref[pl.ds(k, 1), :] * _shift_up(dco, 3 - k, rows)
        dws.append(jnp.sum(dco * _shift_down(x, 3 - k, rows), axis=0, keepdims=True))
    dws.append(jnp.sum(dco * x, axis=0, keepdims=True))
    return dx, dws, jnp.sum(dco, axis=0, keepdims=True)


def _vec(n):
    return pl.BlockSpec((1, n), lambda *_: (0, 0))


class _Row:
    def __init__(self, arr, l, n=None, c=0):
        self.arr, self.l, self.n, self.c = arr[:, None, :], l, n or arr.shape[1], c


def _spec(v):
    if isinstance(v, _Row):
        return pl.BlockSpec((None, 1, v.n), lambda *_: (v.l, 0, v.c))
    return _vec(v.shape[1])


def _arr(v):
    return v.arr if isinstance(v, _Row) else v


def _full(shape):
    nd = len(shape)
    return pl.BlockSpec(shape, lambda *_: (0,) * nd)


def _inproj_fwd(x, nw, scale, shift, w, tok):
    S = x.shape[0]
    tm = min(256, S)

    def body(x_ref, nw_ref, sc_ref, sh_ref, w_ref, tok_ref, u_ref, h_ref):
        del tok_ref
        xv = x_ref[...]
        inv = lax.rsqrt(jnp.mean(xv * xv, axis=-1, keepdims=True) + EPS)
        h = ((xv * inv) * nw_ref[...] * (1.0 + sc_ref[...]) + sh_ref[...]).astype(BF16)
        h_ref[...] = h
        u_ref[...] = _mm(h, w_ref[...])

    return pl.pallas_call(
        body, name="inproj_fwd", grid=(S // tm,),
        in_specs=[pl.BlockSpec((tm, D_MODEL), lambda i: (i, 0)), _spec(nw), _spec(scale), _spec(shift),
                  _full((D_MODEL, N_PAD)), pl.BlockSpec(memory_space=pl.ANY)],
        out_specs=[pl.BlockSpec((tm, N_PAD), lambda i: (i, 0)), pl.BlockSpec((tm, D_MODEL), lambda i: (i, 0))],
        out_shape=[SDS((S, N_PAD), F32), SDS((S, D_MODEL), BF16)],
        compiler_params=_cp(("parallel",)),
    )(x, _arr(nw), _arr(scale), _arr(shift), w, tok)


def _inproj_bwd_x(du, w, x, nw, scale, dxn, tok):
    S = x.shape[0]
    tm = min(256, S)

    def body(du_ref, w_ref, x_ref, nw_ref, sc_ref, dxn_ref, tok_ref, dx_ref, red_ref):
        del tok_ref

        @pl.when(pl.program_id(0) == 0)
        def _():
            red_ref[...] = jnp.zeros_like(red_ref)

        dh = _mm_nt(du_ref[...], w_ref[...])
        xv = x_ref[...]
        inv = lax.rsqrt(jnp.mean(xv * xv, axis=-1, keepdims=True) + EPS)
        xhat = xv * inv
        nwv = nw_ref[...]
        g1 = 1.0 + sc_ref[...]
        dxhat = dh * nwv * g1
        dx = inv * (dxhat - xhat * jnp.mean(dxhat * xhat, axis=-1, keepdims=True))
        dx_ref[...] = dxn_ref[...] + dx
        red_ref[0:1, :] += jnp.sum(dh, axis=0, keepdims=True)
        red_ref[1:2, :] += jnp.sum(dh * xhat * nwv, axis=0, keepdims=True)
        red_ref[2:3, :] += jnp.sum(dh * xhat * g1, axis=0, keepdims=True)

    row = pl.BlockSpec((tm, D_MODEL), lambda i: (i, 0))
    return pl.pallas_call(
        body, name="inproj_bwd_x", grid=(S // tm,),
        in_specs=[pl.BlockSpec((tm, N_PAD), lambda i: (i, 0)), _full((D_MODEL, N_PAD)), row, _spec(nw),
                  _spec(scale), row, pl.BlockSpec(memory_space=pl.ANY)],
        out_specs=[row, _full((8, D_MODEL))],
        out_shape=[SDS((S, D_MODEL), F32), SDS((8, D_MODEL), F32)],
        compiler_params=_cp(("arbitrary",)),
    )(du, w, x, _arr(nw), _arr(scale), dxn, tok)


def _inproj_bwd_w(h, du):
    S = h.shape[0]
    tn = 640

    def body(h_ref, du_ref, gw_ref):
        gw_ref[...] = _mm_tn(h_ref[...], _bf(du_ref[...]))

    return pl.pallas_call(
        body, name="inproj_bwd_w", grid=(N_PAD // tn,),
        in_specs=[_full((S, D_MODEL)), pl.BlockSpec((S, tn), lambda j: (0, j))],
        out_specs=pl.BlockSpec((D_MODEL, tn), lambda j: (0, j)),
        out_shape=SDS((D_MODEL, N_PAD), F32),
        compiler_params=_cp(("parallel",)),
    )(h, du)


def _scan_block(a, b, rows):
    d = 1
    while d < a.shape[0]:
        a_s = _shift_down(a, d, rows, 1.0)
        b_s = _shift_down(b, d, rows, 0.0)
        b = a * b_s + b
        a = a * a_s
        d *= 2
    return a, b


def _rscan_block(c, g, rows):
    d = 1
    while d < c.shape[0]:
        c_s = _shift_up(c, d, rows, 1.0)
        g_s = _shift_up(g, d, rows, 0.0)
        g = g + c * g_s
        c = c * c_s
        d *= 2
    return c, g


LRU_BLOCK = 256


def _lru_gates(xa, wa_ref, ba_ref, wx_ref, bx_ref, lam_ref):
    sp = _softplus(-lam_ref[...])
    xb = _bf(xa)
    r = _sigmoid(_mm(xb, wa_ref[...]) + ba_ref[...])
    ig = _sigmoid(_mm(xb, wx_ref[...]) + bx_ref[...])
    la = -LRU_C * r * sp
    a = jnp.exp(la)
    mult = jnp.sqrt(-_expm1(2.0 * la))
    return sp, r, ig, la, a, mult


def _lru_specs(S, l):
    t128 = pl.BlockSpec((None, 1, LANE), lambda t: (l, 0, t))
    gate = pl.BlockSpec((None, None, LANE, LANE), lambda t: (l, t, 0, 0))
    return [pl.BlockSpec((S, 2 * LANE), lambda t: (0, OFF_LRU // (2 * LANE) + t)),
            pl.BlockSpec((None, 4, LANE), lambda t: (l, 0, t)), t128, gate, t128, gate, t128, t128]


def _lru_fwd(l, u, cw, cb, wa, ba, wx, bx, lam, ycat):
    S = u.shape[0]
    tb = min(LRU_BLOCK, S)

    def body(u_ref, cw_ref, cb_ref, wa_ref, ba_ref, wx_ref, bx_ref, lam_ref, ycat_in, ycat_ref, h_ref, a_scr, b_scr):
        del ycat_in
        rows = _iota((S, LANE), 0)
        xa = _conv_fwd(_f(u_ref[:, 0:LANE]), cw_ref, cb_ref, rows)
        _, _, ig, _, a, mult = _lru_gates(xa, wa_ref, ba_ref, wx_ref, bx_ref, lam_ref)
        a_scr[...] = a
        b_scr[...] = mult * (ig * xa)
        rows_b = _iota((tb, LANE), 0)

        def blk(j, hprev):
            sl = pl.ds(pl.multiple_of(j * tb, tb), tb)
            acum, hloc = _scan_block(a_scr[sl, :], b_scr[sl, :], rows_b)
            hf = hloc + acum * hprev
            h_ref[sl, :] = hf
            return _last_row(hf, rows_b)

        lax.fori_loop(0, S // tb, blk, jnp.zeros((1, LANE), F32))
        ycat_ref[...] = _bf(h_ref[...] * _silu(_f(u_ref[:, LANE:2 * LANE])))

    col = pl.BlockSpec((S, LANE), lambda t: (0, t))
    return pl.pallas_call(
        body, name="lru_fwd", grid=(LRU_W // LANE,),
        in_specs=_lru_specs(S, l) + [pl.BlockSpec(memory_space=pl.ANY)],
        out_specs=[col, col],
        out_shape=[SDS((S, D_INNER), BF16), SDS((S,LRU_W), F32)],
        scratch_shapes=[pltpu.VMEM((S, LANE), F32), pltpu.VMEM((S, LANE), F32)],
        input_output_aliases={8: 0},
        compiler_params=_cp(("parallel",)),
    )(u, cw, cb, wa, ba, wx, bx, lam, ycat)


def _lru_bwd(l, u, cw, cb, wa, ba, wx, bx, lam, h_lru, dycat, du):
    S = u.shape[0]
    tb = min(LRU_BLOCK, S)

    def body(u_ref, cw_ref, cb_ref, wa_ref, ba_ref, wx_ref, bx_ref, lam_ref, h_ref, dy_ref, du_in,
             du_ref, red_ref, gwa_ref, gwx_ref, c_scr, g_scr, l_scr):
        del du_in
        rows = _iota((S, LANE), 0)
        ax = _f(u_ref[:, 0:LANE])
        ag = _f(u_ref[:, LANE:2 * LANE])
        xa = _conv_fwd(ax, cw_ref, cb_ref, rows)
        sp, r, ig, la, a, mult = _lru_gates(xa, wa_ref, ba_ref, wx_ref, bx_ref, lam_ref)
        h = h_ref[...]
        dy = _f(dy_ref[...])
        du_ref[:, LANE:2 * LANE] = _bf(dy * h * _dsilu(ag))
        c_scr[...] = _shift_up(a, 1, rows, 0.0)
        g_scr[...] = dy * _silu(ag)
        rows_b = _iota((tb, LANE), 0)
        nb = S // tb

        def blk(jj, lnext):
            j = nb - 1 - jj
            sl = pl.ds(pl.multiple_of(j * tb, tb), tb)
            ccum, lloc = _rscan_block(c_scr[sl, :], g_scr[sl, :], rows_b)
            lam_t = lloc + ccum * lnext
            l_scr[sl, :] = lam_t
            return jnp.sum(jnp.where(rows_b == 0, lam_t, 0.0), axis=0, keepdims=True)

        lax.fori_loop(0, nb, blk, jnp.zeros((1, LANE), F32))
        db = l_scr[...]
        da = db * _shift_down(h, 1, rows)
        dmult = db * ig * xa
        dig = db * mult * xa
        dxa = db * mult * ig
        dla = da * a - dmult * (a * a) / mult
        dr = -LRU_C * sp * dla
        dsp = jnp.sum(-LRU_C * r * dla, axis=0, keepdims=True)
        dlam = -dsp * _sigmoid(-lam_ref[...])
        dzr = dr * r * (1.0 - r)
        dzi = dig * ig * (1.0 - ig)
        dzr_b, dzi_b, xa_b = _bf(dzr), _bf(dzi), _bf(xa)
        dxa = dxa + _mm_nt(dzr_b, wa_ref[...]) + _mm_nt(dzi_b, wx_ref[...])
        gwa_ref[...] = _mm_tn(xa_b, dzr_b)
        gwx_ref[...] = _mm_tn(xa_b, dzi_b)
        dax, dws, dcb = _conv_bwd(ax, dxa, cw_ref, rows)
        du_ref[:, 0:LANE] = _bf(dax)
        parts = dws + [dcb, jnp.sum(dzr, axis=0, keepdims=True), jnp.sum(dzi, axis=0, keepdims=True), dlam]
        for n, p in enumerate(parts):
            red_ref[pl.ds(n, 1), :] = p

    col = pl.BlockSpec((S, LANE), lambda t: (0, t))
    gw = pl.BlockSpec((None, LANE, LANE), lambda t: (t, 0, 0))
    return pl.pallas_call(
        body, name="lru_bwd", grid=(LRU_W // LANE,),
        in_specs=_lru_specs(S, l) + [col, col, pl.BlockSpec(memory_space=pl.ANY)],
        out_specs=[pl.BlockSpec((S, 2 * LANE), lambda t: (0, OFF_LRU // (2 * LANE) + t)),
                   pl.BlockSpec((8, LANE), lambda t: (0, t)), gw, gw],
        out_shape=[SDS((S, N_PAD), BF16), SDS((8, LRU_W), F32), SDS((4, LANE, LANE), F32), SDS((4, LANE, LANE), F32)],
        scratch_shapes=[pltpu.VMEM((S, LANE), F32)] * 3,
        input_output_aliases={10: 0},
        compiler_params=_cp(("parallel",)),
    )(u, cw, cb, wa, ba, wx, bx, lam, h_lru, dycat, du)


HG_LEVELS = 6


def _hg_consts():
    C = HG_CHUNK
    t = np.arange(C)[:, None]
    r = np.arange(C)[None, :]
    mats = []
    for l in range(HG_LEVELS):
        b = 1 << l
        upper = (t % (2 * b)) >= b
        anchor = (t // (2 * b)) * 2 * b + b - 1
        mats.append((upper & (r > anchor) & (r <= t)) | ((~upper) & (r > t) & (r <= anchor)))
    mats.append(r <= t)
    mats.append(r > t)
    return np.concatenate(mats, 0).astype(np.float32)


def _hg_factors(hf, lb, mall):
    s = _sigmoid(hf)
    f = lb + (1.0 - lb) * s
    lf = jnp.log(f)
    k = (1.0 - lb) * _sigmoid(-hf)
    e = jnp.exp(_sel_l(mall, lf))
    C = HG_CHUNK
    rows = _iota((C, HG_W), 0)
    eq, ek = [], []
    for l in range(HG_LEVELS):
        el = e[l * C:(l + 1) * C]
        eq.append(jnp.where((lax.shift_right_logical(rows, l) & 1) == 1, el, 0.0))
        ek.append(el - eq[l])
    ecum = e[HG_LEVELS * C:(HG_LEVELS + 1) * C]
    erem = e[(HG_LEVELS + 1) * C:(HG_LEVELS + 2) * C]
    return s, f, k, eq, ek, ecum, erem


def _hg_masks():
    C = HG_CHUNK
    ri, ci = _iota((C, C), 0), _iota((C, C), 1)
    rr = _iota((C, LANE), 0)
    gm = [(lax.shift_right_logical(ri, l + 1) == lax.shift_right_logical(ci, l + 1)).astype(F32)
          for l in range(HG_LEVELS)]
    up = [(lax.shift_right_logical(rr, l) & 1) == 1 for l in range(HG_LEVELS)]
    eye = (ri == ci).astype(F32)
    return gm, up, eye, rr


def _hg_scores(qh, kh, eq, ek, sl, gm, up, eye):
    del up
    qs, ks, qb, kb = [], [], [], []
    p = _mm_nt(_bf(qh), _bf(kh)) * eye
    for l in range(HG_LEVELS):
        qs.append(qh * eq[l][:, sl])
        ks.append(kh * ek[l][:, sl])
        qb.append(_bf(qs[l]))
        kb.append(_bf(ks[l]))
        p = p + _mm_nt(qb[l], kb[l]) * gm[l]
    return p, qs, ks, qb, kb


HG_SUB = 4


def _hg_fwd(u, lb, nw, mall, ycat):
    S = u.shape[0]
    C = HG_CHUNK
    n = S // C
    rows = HG_SUB * C

    def body(u_ref, lb_ref, nw_ref, mall_ref, ycat_in, ycat_ref, o_ref, st_ref, st):
        del ycat_in

        @pl.when(pl.program_id(0) == 0)
        def _():
            st[...] = jnp.zeros_like(st)

        gm, up, eye, rr = _hg_masks()
        for sub in range(HG_SUB):
            r = slice(sub * C, (sub + 1) * C)
            q = _silu(_f(u_ref[r, 0:512]))
            v = u_ref[r, 1024:1536]
            _, _, k, eq, ek, ecum, erem = _hg_factors(_f(u_ref[r, 512:1024]), lb_ref[...], mall_ref[...])
            for h in range(HG_HEADS):
                sl = slice(h * LANE, (h + 1) * LANE)
                qh, kh, vh = q[:, sl], k[:, sl], _bf(v[:, sl])
                p = _hg_scores(qh, kh, eq, ek, sl, gm, up, eye)[0]
                sth = st[h]
                st_ref[sub, h] = sth
                o_ref[r, sl] = _mm(_bf(p), vh) + _mm_nt(_bf(qh * ecum[:, sl]), _bf(sth))
                st[h] = sth * _last_row(ecum[:, sl], rr) + _mm_tn(vh, _bf(kh * erem[:, sl]))
            o = o_ref[r, :]
            inv = lax.rsqrt(jnp.mean(o * o, axis=-1, keepdims=True) + EPS)
            ycat_ref[r, :] = _bf((o * inv) * nw_ref[...] * _silu(_f(u_ref[r, 1536:2048])))

    return pl.pallas_call(
        body, name="hg_fwd", grid=(n // HG_SUB,),
        in_specs=[pl.BlockSpec((rows, 2048), lambda i: (i, 0)), _spec(lb), _spec(nw), _full(mall.shape),
                  pl.BlockSpec(memory_space=pl.ANY)],
        out_specs=[pl.BlockSpec((rows, HG_W), lambda i: (i, 1)), pl.BlockSpec((rows, HG_W), lambda i: (i, 0)),
                   pl.BlockSpec((HG_SUB, HG_HEADS, LANE, LANE), lambda i: (i, 0, 0, 0))],
        out_shape=[SDS((S, D_INNER), BF16), SDS((S,HG_W), F32), SDS((n, HG_HEADS, LANE, LANE), F32)],
        scratch_shapes=[pltpu.VMEM((HG_HEADS, LANE, LANE), F32)],
        input_output_aliases={4: 0},
        compiler_params=_cp(("arbitrary",)),
    )(u, _arr(lb), _arr(nw), mall, ycat)


def _hg_bwd(u, lb, nw, mall, mall_t, o_b, states, dycat, du):
    S = u.shape[0]
    C = HG_CHUNK
    n = S // C
    nb = n // HG_SUB
    rows = HG_SUB * C
    L2 = HG_LEVELS

    def body(u_ref, lb_ref, nw_ref, mall_ref, mallt_ref, o_ref, st_ref, dy_ref, du_in, du_ref, red_ref,
             dst, dlast_s, dq_s, dk_s, dex):
        del du_in

        @pl.when(pl.program_id(0) == 0)
        def _():
            dst[...] = jnp.zeros_like(dst)
            red_ref[...] = jnp.zeros_like(red_ref)

        lb = lb_ref[...]
        nwv = nw_ref[...]
        gm, up, eye, rr = _hg_masks()
        for sub in reversed(range(HG_SUB)):
            r = slice(sub * C, (sub + 1) * C)
            hq, hf, hg = _f(u_ref[r, 0:512]), _f(u_ref[r, 512:1024]), _f(u_ref[r, 1536:2048])
            q = _silu(hq)
            v = u_ref[r, 1024:1536]
            s, f, k, eq, ek, ecum, erem = _hg_factors(hf, lb, mall_ref[...])
            o = o_ref[r, :]
            dy = _f(dy_ref[r, :])
            inv = lax.rsqrt(jnp.mean(o * o, axis=-1, keepdims=True) + EPS)
            ohat = o * inv
            du_ref[r, 1536:2048] = _bf(dy * ohat * nwv * _dsilu(hg))
            dn = dy * _silu(hg)
            red_ref[0:1, :] += jnp.sum(dn * ohat, axis=0, keepdims=True)
            dohat = dn * nwv
            do = inv * (dohat - ohat * jnp.mean(dohat * ohat, axis=-1, keepdims=True))
            for h in range(HG_HEADS):
                sl = slice(h * LANE, (h + 1) * LANE)
                qh, kh, vh, doh = q[:, sl], k[:, sl], _bf(v[:, sl]), _bf(do[:, sl])
                p, qs, ks, qb, kb = _hg_scores(qh, kh, eq, ek, sl, gm, up, eye)
                st_f = st_ref[sub, h]
                sth = _bf(st_f)
                dsth = dst[h]
                dsth_b = _bf(dsth)
                qt = qh * ecum[:, sl]
                kt = kh * erem[:, sl]
                elast = _last_row(ecum[:, sl], rr)
                dp = _mm_nt(doh, vh)
                du_ref[r, 1024 + h * LANE:1024 + (h + 1) * LANE] = _bf(_mm_tn(_bf(p), doh) + _mm_nt(_bf(kt), dsth_b))
                dpe = _bf(dp * eye)
                dqt = _mm(doh, sth)
                dkt = _mm(vh, dsth_b)
                dq = dqt * ecum[:, sl] + _mm(dpe, _bf(kh))
                dk = dkt * erem[:, sl] + _mm_tn(dpe, _bf(qh))
                dex[sub, L2 * C:(L2 + 1) * C, sl] = dqt * qt
                dex[sub, (L2 + 1) * C:(L2 + 2) * C, sl] = dkt * kt
                for l in range(HG_LEVELS):
                    dpl = _bf(dp * gm[l])
                    dql = _mm(dpl, kb[l])
                    dkl = _mm_tn(dpl, qb[l])
                    dq = dq + dql * eq[l][:, sl]
                    dk = dk + dkl * ek[l][:, sl]
                    dex[sub, l * C:(l + 1) * C, sl] = dql * qs[l] + dkl * ks[l]
                dlast_s[sub, :, sl] = jnp.sum(dsth * st_f, axis=0, keepdims=True) * elast
                dst[h] = dsth * elast + _mm_tn(doh, _bf(qt))
                dq_s[sub, :, sl] = dq
                dk_s[sub, :, sl] = dk
            dq = dq_s[sub]
            dk = dk_s[sub]
            dlf = _sel_l2(mallt_ref[...], dex[sub]) + dlast_s[sub]
            du_ref[r, 0:512] = _bf(dq * _dsilu(hq))
            t = (1.0 - s) * (dlf / f - dk)
            du_ref[r, 512:1024] = _bf((1.0 - lb) * s * t)
            red_ref[1:2, :] += jnp.sum(t, axis=0, keepdims=True)

    rev = lambda i: (nb - 1 - i, 0)
    return pl.pallas_call(
        body, name="hg_bwd", grid=(nb,),
        in_specs=[pl.BlockSpec((rows, 2048), rev), _spec(lb), _spec(nw), _full(mall.shape), _full(mall_t.shape),
                  pl.BlockSpec((rows, HG_W), rev),
                  pl.BlockSpec((HG_SUB, HG_HEADS, LANE, LANE), lambda i: (nb - 1 - i, 0, 0, 0)),
                  pl.BlockSpec((rows, HG_W), lambda i: (nb - 1 - i, 1)), pl.BlockSpec(memory_space=pl.ANY)],
        out_specs=[pl.BlockSpec((rows, 2048), rev), pl.BlockSpec((8, HG_W), lambda i: (0, 0))],
        out_shape=[SDS((S, N_PAD), BF16), SDS((8, HG_W), F32)],
        scratch_shapes=[pltpu.VMEM((HG_HEADS, LANE, LANE), F32), pltpu.VMEM((HG_SUB, 1, HG_W), F32),
                        pltpu.VMEM((HG_SUB, C, HG_W), F32), pltpu.VMEM((HG_SUB, C, HG_W), F32),
                        pltpu.VMEM((HG_SUB, (L2 + 2) * C, HG_W), F32)],
        input_output_aliases={8: 0},
        compiler_params=_cp(("arbitrary",)),
    )(u, _arr(lb), _arr(nw), mall, mall_t, o_b, states, dycat, du)


def _ssdconv_fwd(l, u, cw, cb):
    S = u.shape[0]

    def body(u_ref, cw_ref, cb_ref, out_ref):
        rows = _iota((S, LANE), 0)
        out_ref[...] = _silu(_conv_fwd(_f(u_ref[...]), cw_ref, cb_ref, rows))

    return pl.pallas_call(
        body, name="ssdconv_fwd", grid=(SSD_CONV // LANE,),
        in_specs=[pl.BlockSpec((S, LANE), lambda t: (0, OFF_XBC // LANE + t)),
                  pl.BlockSpec((None, 4, LANE), lambda t: (l, 0, t)), pl.BlockSpec((None, 1, LANE), lambda t: (l, 0, t))],
        out_specs=pl.BlockSpec((S, LANE), lambda t: (0, t)),
        out_shape=SDS((S, SSD_CONV), F32),
        compiler_params=_cp(("parallel",)),
    )(u, cw, cb)


def _ssdconv_bwd(l, u, cw, cb, dxbc, du):
    S = u.shape[0]

    def body(u_ref, cw_ref, cb_ref, d_ref, du_in, du_ref, red_ref):
        del du_in
        rows = _iota((S, LANE), 0)
        x = _f(u_ref[...])
        dco = d_ref[...] * _dsilu(_conv_fwd(x, cw_ref, cb_ref, rows))
        dx, dws, dcb = _conv_bwd(x, dco, cw_ref, rows)
        du_ref[...] = _bf(dx)
        for n, p in enumerate(dws + [dcb]):
            red_ref[pl.ds(n, 1), :] = p
        red_ref[pl.ds(5, 3), :] = jnp.zeros((3, LANE), F32)

    ucol = pl.BlockSpec((S, LANE), lambda t: (0, OFF_XBC // LANE + t))
    return pl.pallas_call(
        body, name="ssdconv_bwd", grid=(SSD_CONV // LANE,),
        in_specs=[ucol, pl.BlockSpec((None, 4, LANE), lambda t: (l, 0, t)),
                  pl.BlockSpec((None, 1, LANE), lambda t: (l, 0, t)),
                  pl.BlockSpec((S, LANE), lambda t: (0, t)), pl.BlockSpec(memory_space=pl.ANY)],
        out_specs=[ucol, pl.BlockSpec((8, LANE), lambda t: (0, t))],
        out_shape=[SDS((S, N_PAD), BF16), SDS((8, SSD_CONV), F32)],
        input_output_aliases={4: 0},
        compiler_params=_cp(("parallel",)),
    )(u, cw, cb, dxbc, du)


def _ssd_consts():
    e64 = np.zeros((LANE, SSD_W), np.float32)
    for h in range(SSD_HEADS):
        e64[h, h * SSD_P:(h + 1) * SSD_P] = 1.0
    T = SSD_CHUNK
    tril = (np.arange(T)[None, :] <= np.arange(T)[:, None]).astype(np.float32)
    return e64, tril, tril.T.copy()


def _ssd_common(zdt, bias_ref, alog_ref, tril, e64, cum_ref, cumt_ref):
    T = SSD_CHUNK
    lane = _iota((1, LANE), 1)
    a_neg = jnp.where(lane < SSD_HEADS, -jnp.exp(alog_ref[...]), 0.0)
    dtpre = zdt[:, SSD_W:SSD_W + LANE] + bias_ref[...]
    dt = _softplus(dtpre)
    cum = _sel_l(tril, dt * a_neg)
    cum_ref[...] = cum
    cumt_ref[...] = cum.T
    cum_x = _sel_r(cum, e64)
    last_x = _last_row(cum_x, _iota((T, SSD_W), 0))
    ecum_x = jnp.exp(cum_x)
    erem_x = jnp.exp(last_x - cum_x)
    elast_x = jnp.exp(last_x)
    dt_x = _sel_r(dt, e64)
    return a_neg, dtpre, dt, ecum_x, erem_x, elast_x, dt_x


def _ssd_decay(cum_ref, cumt_ref, h, causal):
    T = SSD_CHUNK
    diff = jnp.broadcast_to(cum_ref[:, pl.ds(h, 1)], (T, T)) - cumt_ref[pl.ds(h, 1), :]
    return jnp.exp(jnp.where(causal, diff, NEG))


def _group_norm_fwd(y1, nwv):
    outs, invs = [], []
    for g in range(2):
        seg = y1[:, g * 512:(g + 1) * 512]
        inv = lax.rsqrt(jnp.mean(seg * seg, axis=-1, keepdims=True) + EPS)
        outs.append(seg * inv * nwv[:, g * 512:(g + 1) * 512])
        invs.append(inv)
    return outs, invs


def _ssd_fwd(u, xbc, bias, alog, dskip_x, nw, consts, ycat):
    S = u.shape[0]
    T = SSD_CHUNK
    n = S // T
    e64, tril, _ = consts

    def body(u_ref, xbc_ref, bias_ref, alog_ref, dx_ref, nw_ref, e64_ref, tril_ref, ycat_in,
             ycat_ref, y_ref, st_ref, st, cumt, cum_e):
        del ycat_in

        @pl.when(pl.program_id(0) == 0)
        def _():
            st[...] = jnp.zeros_like(st)

        zdt = _f(u_ref[...])
        z = zdt[:, 0:SSD_W]
        xs = xbc_ref[:, 0:SSD_W]
        _, _, _, ecum_x, erem_x, elast_x, dt_x = _ssd_common(
            zdt, bias_ref, alog_ref, tril_ref[...], e64_ref[...], cum_e, cumt)
        causal = _iota((T, T), 0) >= _iota((T, T), 1)
        lo = _iota((T, LANE), 1) < SSD_P
        xdt = xs * dt_x
        xrem = xdt * erem_x
        st_ref[...] = st[...]
        for g in range(2):
            gs = slice(g * 512, (g + 1) * 512)
            bg = _bf(xbc_ref[:, SSD_W + g * LANE:SSD_W + (g + 1) * LANE])
            cg = _bf(xbc_ref[:, SSD_W + 256 + g * LANE:SSD_W + 256 + (g + 1) * LANE])
            cb = _mm_nt(cg, bg)
            yin = _mm(cg, _bf(st[:, gs])) * ecum_x[:, gs]
            for j in range(4):
                h0 = 8 * g + 2 * j
                cs = slice(h0 * SSD_P, (h0 + 2) * SSD_P)
                xp = xdt[:, cs]
                s0 = _bf(cb * _ssd_decay(cum_e, cumt, h0, causal))
                s1 = _bf(cb * _ssd_decay(cum_e, cumt, h0 + 1, causal))
                y_ref[:, cs] = (_mm(s0, _bf(jnp.where(lo, xp, 0.0))) + _mm(s1, _bf(jnp.where(lo, 0.0, xp)))
                                + yin[:, j * LANE:(j + 1) * LANE])
            st[:, gs] = st[:, gs] * elast_x[:, gs] + _mm_tn(bg, _bf(xrem[:, gs]))
        y1 = (y_ref[...] + dx_ref[...] * xs) * _silu(z)
        outs, _ = _group_norm_fwd(y1, nw_ref[...])
        for g in range(2):
            ycat_ref[:, g * 512:(g + 1) * 512] = _bf(outs[g])

    return pl.pallas_call(
        body, name="ssd_fwd", grid=(n,),
        in_specs=[pl.BlockSpec((T, SSD_W + LANE), lambda i: (i, OFF_Z // (SSD_W + LANE))),
                  pl.BlockSpec((T, SSD_CONV), lambda i: (i, 0)), _spec(bias), _spec(alog), _spec(dskip_x), _spec(nw),
                  _full(e64.shape), _full(tril.shape), pl.BlockSpec(memory_space=pl.ANY)],
        out_specs=[pl.BlockSpec((T, SSD_W), lambda i: (i, 1)), pl.BlockSpec((T, SSD_W), lambda i: (i, 0)),
                   pl.BlockSpec((None, SSD_N, SSD_W), lambda i: (i, 0, 0))],
        out_shape=[SDS((S, D_INNER), BF16), SDS((S,SSD_W), F32), SDS((n, SSD_N, SSD_W), F32)],
        scratch_shapes=[pltpu.VMEM((SSD_N, SSD_W), F32), pltpu.VMEM((LANE, T), F32), pltpu.VMEM((T, LANE), F32)],
        input_output_aliases={8: 0},
        compiler_params=_cp(("arbitrary",)),
    )(u, xbc, _arr(bias), _arr(alog), _arr(dskip_x), _arr(nw), _bfc(e64), _bfc(tril), ycat)


def _ssd_bwd(u, xbc, bias, alog, dskip_x, nw, consts, y_ssd, states, dycat, du, tok):
    S = u.shape[0]
    T = SSD_CHUNK
    n = S // T
    e64, tril, triu = consts
    e64t = np.ascontiguousarray(e64.T)

    def body(u_ref, xbc_ref, bias_ref, alog_ref, dx_ref, nw_ref, e64_ref, e64t_ref, tril_ref, triu_ref,
             y_ref, st_ref, dy_ref, du_in, tok_ref, du_ref, dxbc_ref, red_ref, dst, dl_s, cumt, dxdt_s, dy0_s, gb_s,
             gc_s, cum_e, cs_s):
        del du_in, tok_ref

        @pl.when(pl.program_id(0) == 0)
        def _():
            dst[...] = jnp.zeros_like(dst)
            red_ref[...] = jnp.zeros_like(red_ref)
            cs_s[...] = jnp.zeros_like(cs_s)

        zdt = _f(u_ref[...])
        z = zdt[:, 0:SSD_W]
        xs = xbc_ref[:, 0:SSD_W]
        a_neg, dtpre, dt, ecum_x, erem_x, elast_x, dt_x = _ssd_common(
            zdt, bias_ref, alog_ref, tril_ref[...], e64_ref[...], cum_e, cumt)
        causal = _iota((T, T), 0) >= _iota((T, T), 1)
        lo = _iota((T, LANE), 1) < SSD_P
        xdt = xs * dt_x
        xrem = xdt * erem_x
        y = y_ref[...]
        dxv = dx_ref[...]
        nwv = nw_ref[...]
        sz = _silu(z)
        y0 = y + dxv * xs
        y1 = y0 * sz
        for g in range(2):
            gs = slice(g * 512, (g + 1) * 512)
            seg = y1[:, gs]
            inv = lax.rsqrt(jnp.mean(seg * seg, axis=-1, keepdims=True) + EPS)
            shat = seg * inv
            dyg = _f(dy_ref[:, gs])
            red_ref[0:1, gs] += jnp.sum(dyg * shat, axis=0, keepdims=True)
            dsh = dyg * nwv[:, gs]
            dy1g = inv * (dsh - shat * jnp.mean(dsh * shat, axis=-1, keepdims=True))
            du_ref[:, gs] = _bf(dy1g * y0[:, gs] * _dsilu(z[:, gs]))
            dy0_s[:, gs] = dy1g * sz[:, gs]
        dy0 = dy0_s[...]
        red_ref[1:2, :] += jnp.sum(dy0 * xs, axis=0, keepdims=True)
        dyin = dy0 * ecum_x
        lane = _iota((T, LANE), 1)
        dcum = jnp.zeros((T, LANE), F32)

        def decay_grad(h, gm):
            cs_s[pl.ds(h, 1), :] = jnp.sum(gm, axis=0, keepdims=True)
            return jnp.where(lane == h, jnp.sum(gm, axis=1, keepdims=True), 0.0)

        for g in range(2):
            gs = slice(g * 512, (g + 1) * 512)
            bg = _bf(xbc_ref[:, SSD_W + g * LANE:SSD_W + (g + 1) * LANE])
            cg = _bf(xbc_ref[:, SSD_W + 256 + g * LANE:SSD_W + 256 + (g + 1) * LANE])
            cb = _mm_nt(cg, bg)
            dst_f, st_f = dst[:, gs], st_ref[:, gs]
            dstg = _bf(dst_f)
            stg = _bf(st_f)
            dyin_g = _bf(dyin[:, gs])
            xrem_g = _bf(xrem[:, gs])
            dcb = jnp.zeros((T, T), F32)
            dxr = _mm(bg, dstg)
            dxdt_s[:, gs] = dxr * erem_x[:, gs]
            gc_s[:, gs] = dxr * xrem[:, gs]
            gb_s[:, gs] = dyin[:, gs] * _mm(cg, stg)
            dl_s[:, gs] = jnp.sum(dst_f * st_f, axis=0, keepdims=True) * elast_x[:, gs]
            for j in range(4):
                h0 = 8 * g + 2 * j
                cs = slice(h0 * SSD_P, (h0 + 2) * SSD_P)
                xp = xdt[:, cs]
                dyp = dy0[:, cs]
                x_lo, x_hi = _bf(jnp.where(lo, xp, 0.0)), _bf(jnp.where(lo, 0.0, xp))
                d_lo, d_hi = _bf(jnp.where(lo, dyp, 0.0)), _bf(jnp.where(lo, 0.0, dyp))
                l0 = _ssd_decay(cum_e, cumt, h0, causal)
                l1 = _ssd_decay(cum_e, cumt, h0 + 1, causal)
                s0 = cb * l0
                s1 = cb * l1
                ds0 = _mm_nt(d_lo, x_lo)
                ds1 = _mm_nt(d_hi, x_hi)
                dcb = dcb + ds0 * l0 + ds1 * l1
                dxdt_s[:, cs] += _mm_tn(_bf(s0), d_lo) + _mm_tn(_bf(s1), d_hi)
                dcum = dcum + decay_grad(h0, ds0 * s0) + decay_grad(h0 + 1, ds1 * s1)
            dcb_b = _bf(dcb)
            dxbc_ref[:, SSD_W + g * LANE:SSD_W + (g + 1) * LANE] = _mm_tn(dcb_b, cg) + _mm_nt(xrem_g, dstg)
            dxbc_ref[:, SSD_W + 256 + g * LANE:SSD_W + 256 + (g + 1) * LANE] = _mm(dcb_b, bg) + _mm_nt(dyin_g, stg)
            dst[:, gs] = dst_f * elast_x[:, gs] + _mm_tn(cg, dyin_g)
        dxdt = dxdt_s[...]
        dxbc_ref[:, 0:SSD_W] = dxdt * dt_x + dy0 * dxv
        e64t = e64t_ref[...]
        gc = gc_s[...]
        dlast_x = jnp.sum(gc, axis=0, keepdims=True) + dl_s[...]
        dlast = jnp.max(_sel_r(jnp.broadcast_to(dlast_x, (8, SSD_W)), e64t), axis=0, keepdims=True)
        dcum = (dcum - cs_s[...].T + _sel_r(gb_s[...] - gc, e64t)
                + jnp.where(_iota((T, LANE), 0) == T - 1, dlast, 0.0))
        dda = _sel_l(triu_ref[...], dcum)
        ddt = dda * a_neg + _sel_r(dxdt * xs, e64t)
        ddtpre = ddt * _sigmoid(dtpre)
        du_ref[:, SSD_W:SSD_W + LANE] = _bf(jnp.where(lane < SSD_HEADS, ddtpre, 0.0))
        red_ref[2:3, 0:LANE] += jnp.sum(ddtpre, axis=0, keepdims=True)
        red_ref[3:4, 0:LANE] += jnp.sum(dda * dt, axis=0, keepdims=True)

    rev = lambda i: (n - 1 - i, 0)
    return pl.pallas_call(
        body, name="ssd_bwd", grid=(n,),
        in_specs=[pl.BlockSpec((T, SSD_W + LANE), lambda i: (n - 1 - i, OFF_Z // (SSD_W + LANE))),
                  pl.BlockSpec((T, SSD_CONV), rev), _spec(bias), _spec(alog), _spec(dskip_x), _spec(nw),
                  _full(e64.shape), _full(e64t.shape), _full(tril.shape), _full(triu.shape),
                  pl.BlockSpec((T, SSD_W), rev), pl.BlockSpec((None, SSD_N, SSD_W), lambda i: (n - 1 - i, 0, 0)),
                  pl.BlockSpec((T, SSD_W), lambda i: (n - 1 - i, 1)), pl.BlockSpec(memory_space=pl.ANY),
                  pl.BlockSpec(memory_space=pl.ANY)],
        out_specs=[pl.BlockSpec((T, SSD_W + LANE), lambda i: (n - 1 - i, OFF_Z // (SSD_W + LANE))),
                   pl.BlockSpec((T, SSD_CONV), rev), pl.BlockSpec((8, SSD_W), lambda i: (0, 0))],
        out_shape=[SDS((S, N_PAD), BF16), SDS((S, SSD_CONV), F32), SDS((8, SSD_W), F32)],
        scratch_shapes=[pltpu.VMEM((SSD_N, SSD_W), F32), pltpu.VMEM((1, SSD_W), F32), pltpu.VMEM((LANE, T), F32)]
        + [pltpu.VMEM((T, SSD_W), F32)] * 4 + [pltpu.VMEM((T, LANE), F32), pltpu.VMEM((LANE, T), F32)],
        input_output_aliases={13: 0},
        compiler_params=_cp(("arbitrary",)),
    )(u, xbc, _arr(bias), _arr(alog), _arr(dskip_x), _arr(nw), _bfc(e64), _bfc(e64t), _bfc(tril), _bfc(triu), y_ssd,
      states, dycat, du, tok)


def _bfc(a):
    return jnp.asarray(a, BF16)


def _outproj_fwd(ycat, wo, x, gate, tok):
    S = x.shape[0]
    tm = min(512, S)

    def body(yc_ref, wo_ref, x_ref, g_ref, tok_ref, xn_ref, y_ref):
        del tok_ref
        y = _mm(_bf(yc_ref[...]), wo_ref[...])
        y_ref[...] = y
        xn_ref[...] = x_ref[...] + g_ref[...] * y

    row = pl.BlockSpec((tm, D_MODEL), lambda i: (i, 0))
    return pl.pallas_call(
        body, name="outproj_fwd", grid=(S // tm,),
        in_specs=[pl.BlockSpec((tm, D_INNER), lambda i: (i, 0)), _full((D_INNER, D_MODEL)), row, _spec(gate),
                  pl.BlockSpec(memory_space=pl.ANY)],
        out_specs=[row, row],
        out_shape=[SDS((S, D_MODEL), F32), SDS((S, D_MODEL), F32)],
        compiler_params=_cp(("parallel",)),
    )(ycat, wo, x, _arr(gate), tok)


def _outproj_bwd(dxn, y, gate, ycat, wo):
    S = dxn.shape[0]
    tm = min(512, S)

    def body(dx_ref, y_ref, g_ref, yc_ref, wo_ref, dyc_ref, gwo_ref, dg_ref, acc):
        @pl.when(pl.program_id(0) == 0)
        def _():
            acc[...] = jnp.zeros_like(acc)
            dg_ref[...] = jnp.zeros_like(dg_ref)

        dxv = dx_ref[...]
        dy = _bf(dxv * g_ref[...])
        dg_ref[0:1, :] += jnp.sum(dxv * y_ref[...], axis=0, keepdims=True)
        dyc_ref[...] = _mm_nt(dy, wo_ref[...])
        acc[...] += _mm_tn(_bf(yc_ref[...]), dy)

        @pl.when(pl.program_id(0) == pl.num_programs(0) - 1)
        def _():
            gwo_ref[...] = acc[...].astype(BF16)

    row = pl.BlockSpec((tm, D_MODEL), lambda i: (i, 0))
    wide = pl.BlockSpec((tm, D_INNER), lambda i: (i, 0))
    return pl.pallas_call(
        body, name="outproj_bwd", grid=(S // tm,),
        in_specs=[row, row, _spec(gate), wide, _full((D_INNER, D_MODEL))],
        out_specs=[wide, _full((D_INNER, D_MODEL)), _full((8, D_MODEL))],
        out_shape=[SDS((S, D_INNER), F32), SDS((D_INNER, D_MODEL), BF16), SDS((8, D_MODEL), F32)],
        scratch_shapes=[pltpu.VMEM((D_INNER, D_MODEL), F32)],
        compiler_params=_cp(("arbitrary",)),
    )(dxn, y, _arr(gate), ycat, wo)


def _loss_head(x, fw, target):
    S = x.shape[0]
    tm = min(512, S)

    def body(x_ref, fw_ref, t_ref, dx_ref, red_ref):
        @pl.when(pl.program_id(0) == 0)
        def _():
            red_ref[...] = jnp.zeros_like(red_ref)

        xv = x_ref[...]
        fwv = fw_ref[...]
        inv = lax.rsqrt(jnp.mean(xv * xv, axis=-1, keepdims=True) + EPS)
        xhat = xv * inv
        err = xhat * fwv - t_ref[...]
        col = jnp.sum(err * err, axis=0, keepdims=True)
        red_ref[1:2, :] += jnp.broadcast_to(jnp.sum(col, axis=1, keepdims=True) * (0.5 / D_MODEL), (1, D_MODEL))
        dy = err * (1.0 / D_MODEL)
        red_ref[0:1, :] += jnp.sum(dy * xhat, axis=0, keepdims=True)
        dxhat = dy * fwv
        dx_ref[...] = inv * (dxhat - xhat * jnp.mean(dxhat * xhat, axis=-1, keepdims=True))

    row = pl.BlockSpec((tm, D_MODEL), lambda i: (i, 0))
    return pl.pallas_call(
        body, name="loss_head", grid=(S // tm,),
        in_specs=[row, _vec(D_MODEL), row],
        out_specs=[row, _full((8, D_MODEL))],
        out_shape=[SDS((S, D_MODEL), F32), SDS((8, D_MODEL), F32)],
        compiler_params=_cp(("arbitrary",)),
    )(x, fw, target)


ADA_COLS = 3 * D_MODEL // N_DEV


def _ada_fwd(c_all, w_ada, b_cols):
    def body(c_ref, w_ref, b_ref, out_ref):
        out_ref[...] = _mm(_bf(_silu(c_ref[...])), _bf(w_ref[...])) + b_ref[...]

    return pl.pallas_call(
        body, name="ada_fwd", grid=(DEPTH,),
        in_specs=[_full((N_DEV, D_MODEL)), pl.BlockSpec((None, D_MODEL, ADA_COLS), lambda l: (l, 0, 0)),
                  pl.BlockSpec((None, 1, ADA_COLS), lambda l: (l, 0, 0))],
        out_specs=pl.BlockSpec((None, N_DEV, ADA_COLS), lambda l: (l, 0, 0)),
        out_shape=SDS((DEPTH, N_DEV, ADA_COLS), F32),
        compiler_params=_cp(("parallel",)),
    )(c_all, w_ada, b_cols)


def _ada_bwd(ct_pad, dmod_pad):
    def body(c_ref, d_ref, out_ref):
        out_ref[...] = _mm(_bf(_silu(c_ref[...])), _bf(d_ref[...]))

    return pl.pallas_call(
        body, name="ada_bwd", grid=(DEPTH,),
        in_specs=[_full((D_MODEL, LANE)), pl.BlockSpec((None, LANE, ADA_COLS), lambda l: (l, 0, 0))],
        out_specs=pl.BlockSpec((None, D_MODEL, ADA_COLS), lambda l: (l, 0, 0)),
        out_shape=SDS((DEPTH, D_MODEL, ADA_COLS), F32),
        compiler_params=_cp(("parallel",)),
    )(ct_pad, dmod_pad)


def _adamw(parts, w, m, v, name, own=None, layers=None, prev=None):
    n, L, R, C = parts.shape
    lo, hi = layers or (0, L)
    tr = R
    while tr * C * 4 > (1 << 20) and tr % 16 == 0:
        tr //= 2
    first = 1 if own is None else 2

    def body(*refs):
        p_ref = refs[0]
        w_ref, m_ref, v_ref = refs[first:first + 3]
        g_ref, d_ref, mo_ref, vo_ref = refs[-4:]

        def part(k):
            if own is None:
                return p_ref[k].astype(F32)
            me = 4 * lax.axis_index("x") + 2 * lax.axis_index("y") + lax.axis_index("c")
            return jnp.where(me == k, refs[1][...], p_ref[k]).astype(F32)

        g = part(0)
        for k in range(1, n):
            g = g + part(k)
        mn = ADAM_B1 * m_ref[...] + (1.0 - ADAM_B1) * g
        vn = ADAM_B2 * v_ref[...] + (1.0 - ADAM_B2) * (g * g)
        m_hat = mn / (1.0 - ADAM_B1 ** ADAM_STEP)
        v_hat = vn / (1.0 - ADAM_B2 ** ADAM_STEP)
        g_ref[...] = g
        d_ref[...] = -ADAM_LR * (m_hat / (jnp.sqrt(v_hat) + ADAM_EPS) + ADAM_WD * w_ref[...])
        mo_ref[...] = mn
        vo_ref[...] = vn

    blk = pl.BlockSpec((None, tr, C), lambda l, i: (lo + l, i, 0))
    own_blk = [] if own is None else [pl.BlockSpec((None, tr, C), lambda l, i: (l, i, 0))]
    n_blk = 3 if own is None else 4
    return pl.pallas_call(
        body, name=name, grid=(hi - lo, R // tr),
        in_specs=[pl.BlockSpec((n, None, tr, C), lambda l, i: (0, lo + l, i, 0))] + own_blk + [blk] * 3
        + ([] if prev is None else [ANY] * 4),
        out_specs=[blk] * 4,
        out_shape=[SDS((L, R, C), F32)] * 4,
        input_output_aliases={} if prev is None else {1 + n_blk + k: k for k in range(4)},
        compiler_params=_cp(("parallel", "parallel")),
    )(parts, *([] if own is None else [own]), w, m, v, *([] if prev is None else prev))


MESH = pl.DeviceIdType.MESH
ANY = pl.BlockSpec(memory_space=pl.ANY)


def _all_gather(v, name):
    def body(v_ref, out_ref, send_sems, recv_sems, local_sem):
        x, y, c = lax.axis_index("x"), lax.axis_index("y"), lax.axis_index("c")
        me, sibling = (x, y, c), (x, y, 1 - c)
        chips = [(1 - x, y), (x, 1 - y), (1 - x, 1 - y)]

        def slot(px, py, pc):
            return out_ref.at[4 * px + 2 * py + pc]

        def copy(k, block, to, src=None):
            return pltpu.make_async_remote_copy(
                src_ref=slot(*block) if src is None else src, dst_ref=slot(*block),
                send_sem=send_sems.at[k], recv_sem=recv_sems.at[k], device_id=to, device_id_type=MESH)

        mine = pltpu.make_async_copy(v_ref, slot(*me), local_sem)
        mine.start()
        first = [copy(0, me, sibling, src=v_ref)]
        first += [copy(1 + j, me, (*chip, c), src=v_ref) for j, chip in enumerate(chips)]
        for cp in first:
            cp.start()
        passed = [copy(4 + j, (*chip, c), sibling) for j, chip in enumerate(chips)]
        for j, chip in enumerate(chips):
            copy(1 + j, (*chip, c), me).wait_recv()
            passed[j].start()
        copy(0, sibling, me).wait_recv()
        for j, chip in enumerate(chips):
            copy(4 + j, (*chip, 1 - c), me).wait_recv()
        for cp in first + passed:
            cp.wait_send()
        mine.wait()

    return pl.pallas_call(
        body, name=name, in_specs=[ANY], out_specs=ANY,
        out_shape=SDS((N_DEV,) + v.shape, v.dtype),
        scratch_shapes=[pltpu.SemaphoreType.DMA((7,)), pltpu.SemaphoreType.DMA((7,)), pltpu.SemaphoreType.DMA],
    )(v)


def _all_to_all(v, name):
    def body(v_ref, out_ref, send_sems, recv_sems, local_sem):
        x, y, c = lax.axis_index("x"), lax.axis_index("y"), lax.axis_index("c")
        mine_idx = 4 * x + 2 * y + c
        mine = pltpu.make_async_copy(v_ref.at[mine_idx], out_ref.at[mine_idx], local_sem)
        mine.start()
        sends, recvs = [], []
        for k in range(1, N_DEV):
            px = 1 - x if k & 4 else x
            py = 1 - y if k & 2 else y
            pc = 1 - c if k & 1 else c
            peer_idx = 4 * px + 2 * py + pc
            sems = dict(send_sem=send_sems.at[k - 1], recv_sem=recv_sems.at[k - 1], device_id=(px, py, pc),
                        device_id_type=MESH)
            sends.append(pltpu.make_async_remote_copy(src_ref=v_ref.at[peer_idx], dst_ref=out_ref.at[mine_idx], **sems))
            recvs.append(pltpu.make_async_remote_copy(src_ref=v_ref.at[peer_idx], dst_ref=out_ref.at[peer_idx], **sems))
        for cp in sends:
            cp.start()
        for cp in recvs:
            cp.wait_recv()
        for cp in sends:
            cp.wait_send()
        mine.wait()

    return pl.pallas_call(
        body, name=name, in_specs=[ANY], out_specs=ANY,
        out_shape=SDS(v.shape, v.dtype),
        scratch_shapes=[pltpu.SemaphoreType.DMA((7,)), pltpu.SemaphoreType.DMA((7,)), pltpu.SemaphoreType.DMA],
    )(v)


HBM_SPEC = pl.BlockSpec(memory_space=pltpu.HBM)
SEM_SPEC = pl.BlockSpec(memory_space=pltpu.SEMAPHORE)
EFFECT = pltpu.SideEffectType.DATAFLOW_SIDE_EFFECTING


EXCHANGE_PEERS = {"gather": range(1, N_DEV), "scatter": range(1, N_DEV), "chip": (1, 2, 4, 6), "pass": (2, 4, 6)}


def _exchange_copies(srcs, lands, send_sems, recv_sems, mode, layer):
    x, y, c = lax.axis_index("x"), lax.axis_index("y"), lax.axis_index("c")
    me = 4 * x + 2 * y + c
    copies = []
    for a, (src, land) in enumerate(zip(srcs, lands)):
        for k in EXCHANGE_PEERS[mode]:
            px = 1 - x if k & 4 else x
            py = 1 - y if k & 2 else y
            pc = 1 - c if k & 1 else c
            peer = 4 * px + 2 * py + pc
            if mode == "scatter":
                s, d, to = src.at[peer], land.at[me, layer], (px, py, pc)
            elif mode == "pass":
                s, d, to = land.at[peer], land.at[peer], (x, y, 1 - c)
            else:
                s, d, to = src, land.at[me], (px, py, pc)
            n = 7 * a + k - 1
            copies.append(pltpu.make_async_remote_copy(
                src_ref=s, dst_ref=d, send_sem=send_sems.at[n], recv_sem=recv_sems.at[n], device_id=to,
                device_id_type=MESH))
    return copies


def _exchange_start(name, srcs, lands, mode, layer=0, after=None):
    n = len(srcs)

    def body(*refs):
        send_sems, recv_sems = refs[-2 * n - 3], refs[-2 * n - 2]
        for cp in _exchange_copies(refs[:n], refs[n:2 * n], send_sems, recv_sems, mode, layer):
            cp.start()
        refs[-1][...] = jnp.zeros_like(refs[-1])

    arrays = list(srcs) + list(lands)
    sems = pltpu.SemaphoreType.DMA((7 * n,))
    out = pl.pallas_call(
        body, name=name,
        out_shape=(sems, sems, *[pltpu.HBM(v.shape, v.dtype) for v in arrays], SDS((8, LANE), F32)),
        in_specs=[HBM_SPEC] * (2 * n) + ([ANY] if after is not None else []),
        out_specs=(SEM_SPEC, SEM_SPEC, *[HBM_SPEC] * (2 * n), pl.BlockSpec(memory_space=pltpu.VMEM)),
        input_output_aliases={i: 2 + i for i in range(2 * n)},
        compiler_params=pltpu.CompilerParams(has_side_effects=EFFECT),
    )(*[pltpu.with_memory_space_constraint(v, pltpu.HBM) for v in arrays], *([after] if after is not None else []))
    return dict(sems=out[:2], srcs=out[2:2 + n], lands=out[2 + n:2 + 2 * n], token=out[-1], mode=mode,
                layer=layer)


def _exchange_wait(name, st, after, also=()):
    n = len(st["srcs"])

    def body(*refs):
        send_sems, recv_sems = refs[2 * n], refs[2 * n + 1]
        for cp in _exchange_copies(refs[:n], refs[n:2 * n], send_sems, recv_sems, st["mode"], st["layer"]):
            cp.wait_send()
            cp.wait_recv()

    arrays = list(st["srcs"]) + list(st["lands"])
    out = pl.pallas_call(
        body, name=name,
        out_shape=tuple(pltpu.HBM(v.shape, v.dtype) for v in arrays),
        in_specs=[HBM_SPEC] * (2 * n) + [SEM_SPEC, SEM_SPEC] + [ANY] * (1 + len(also)),
        out_specs=tuple([HBM_SPEC] * (2 * n)),
        input_output_aliases={i: i for i in range(2 * n)},
        compiler_params=pltpu.CompilerParams(has_side_effects=EFFECT),
    )(*arrays, *st["sems"], after, *also)
    st["srcs"] = out[:n]
    return out[n:]


_IN_PIECES = ([(1024, 3072)]
              + [r for t in range(4) for r in ((LANE * t, LANE * (t + 1)), (512 + LANE * t, 512 + LANE * (t + 1)))]
              + [(4096, 5632), (3072, 4096), (5632, 5648)])


def _permute_in(w):
    pad = jnp.zeros(w.shape[:-1] + (N_PAD - N_IN,), w.dtype)
    return jnp.concatenate([w[..., a:b] for a, b in _IN_PIECES] + [pad], axis=-1)


def _unpermute_in(g):
    ax = [g[..., OFF_LRU + 2 * LANE * t:OFF_LRU + 2 * LANE * t + LANE] for t in range(4)]
    ag = [g[..., OFF_LRU + 2 * LANE * t + LANE:OFF_LRU + 2 * LANE * (t + 1)] for t in range(4)]
    return jnp.concatenate(ax + ag + [g[..., 0:2048], g[..., OFF_Z:OFF_Z + SSD_W], g[..., OFF_XBC:OFF_XBC + SSD_CONV],
                                      g[..., OFF_Z + SSD_W:OFF_Z + SSD_W + SSD_HEADS]], axis=-1)


SHARD_COLS = N_IN // N_DEV


def _in_segments():
    segs, pos = [], 0
    for a, b in _IN_PIECES:
        for i in range(N_DEV):
            lo, hi = max(a, SHARD_COLS * i), min(b, SHARD_COLS * (i + 1))
            if lo < hi:
                segs.append((i, lo - SHARD_COLS * i, hi - lo, pos + lo - a))
        pos += b - a
    return segs


RELAYOUT_ROWS = 256


def _relayout_in(land, own):
    def body(land_ref, own_ref, out_ref):
        me = 4 * lax.axis_index("x") + 2 * lax.axis_index("y") + lax.axis_index("c")
        out_ref[:, N_IN:N_PAD] = jnp.zeros((RELAYOUT_ROWS, N_PAD - N_IN), BF16)
        for i, j, wd, p in _in_segments():
            out_ref[:, p:p + wd] = jnp.where(me == i, own_ref[:, j:j + wd], land_ref[i, :, j:j + wd])

    return pl.pallas_call(
        body, name="relayout_in", grid=(D_MODEL // RELAYOUT_ROWS,),
        in_specs=[pl.BlockSpec((N_DEV, RELAYOUT_ROWS, SHARD_COLS), lambda r: (0, r, 0)),
                  pl.BlockSpec((RELAYOUT_ROWS, SHARD_COLS), lambda r: (r, 0))],
        out_specs=pl.BlockSpec((RELAYOUT_ROWS, N_PAD), lambda r: (r, 0)),
        out_shape=SDS((D_MODEL, N_PAD), BF16),
        compiler_params=_cp(("parallel",)),
    )(land, own)


def _relayout_grad(g):
    def body(g_ref, out_ref):
        for i, j, wd, p in _in_segments():
            out_ref[i, :, j:j + wd] = g_ref[:, p:p + wd].astype(BF16)

    return pl.pallas_call(
        body, name="relayout_grad", grid=(D_MODEL // RELAYOUT_ROWS,),
        in_specs=[pl.BlockSpec((RELAYOUT_ROWS, N_PAD), lambda r: (r, 0))],
        out_specs=pl.BlockSpec((N_DEV, RELAYOUT_ROWS, SHARD_COLS), lambda r: (0, r, 0)),
        out_shape=SDS((N_DEV, D_MODEL, SHARD_COLS), BF16),
        compiler_params=_cp(("parallel",)),
    )(g)


def _block_diag(w):
    w4 = w.reshape(DEPTH, 4, 2, 64, 64)
    z = jnp.zeros((DEPTH, 4, 64, 64), w.dtype)
    top = jnp.concatenate([w4[:, :, 0], z], axis=-1)
    bot = jnp.concatenate([z, w4[:, :, 1]], axis=-1)
    return jnp.concatenate([top, bot], axis=2).astype(BF16)


def _diag_blocks(g):
    return jnp.stack([g[:, :, :64, :64], g[:, :, 64:, 64:]], axis=2).reshape(DEPTH, 8, 64, 64)


def _pad_lanes(v):
    return jnp.pad(v, ((0, 0), (0, LANE - v.shape[1])))


def _lower_bounds(logits):
    p = jax.nn.softmax(logits, axis=0)
    return p, jnp.cumsum(p, axis=0) - p[0]


def _lower_bounds_bwd(p, dlb):
    dp = jnp.cumsum(dlb[::-1], axis=0)[::-1]
    dp = dp.at[0].add(-jnp.sum(dlb, axis=0))
    return p * (dp - jnp.sum(dp * p, axis=0, keepdims=True))


SMALL = ["norm_w", "b_ada", "lru_conv_b", "lru_wa", "lru_ba", "lru_wx", "lru_bx", "lru_lambda", "hg_lb_logits",
         "hg_norm_w", "ssd_conv_b", "ssd_dt_bias", "ssd_a_log", "ssd_d", "ssd_norm_w", "final_norm_w"]
WEIGHTS = ["norm_w", "w_ada", "b_ada", "w_in", "lru_conv_w", "lru_conv_b", "lru_wa", "lru_ba", "lru_wx", "lru_bx",
           "lru_lambda", "hg_lb_logits", "hg_norm_w", "ssd_conv_w", "ssd_conv_b", "ssd_dt_bias", "ssd_a_log", "ssd_d",
           "ssd_norm_w", "w_out", "final_norm_w"]
INPUTS = ["x", "c"] + WEIGHTS + ["loss_target"] + ["m_" + n for n in WEIGHTS] + ["v_" + n for n in WEIGHTS]
SMALL_ROW = 1024


def _small_rows(like):
    out, off = {}, 0
    for n in SMALL:
        rows = -(-int(np.prod(like[n].shape)) // (8 * SMALL_ROW)) * 8
        out[n] = (off, rows)
        off += rows
    return out, off


def _flatten_small(d, prefix="", last=0.0):
    table, _ = _small_rows({n: d[prefix + n] for n in SMALL})
    pieces = []
    for n in SMALL:
        flat = d[prefix + n].reshape(-1)
        pieces.append(jnp.pad(flat, (0, table[n][1] * SMALL_ROW - flat.shape[0])).reshape(-1, SMALL_ROW))
    return jnp.concatenate(pieces + [jnp.full((8, SMALL_ROW), last, F32)], axis=0)


def _split_small(packed, like):
    table, _ = _small_rows(like)
    out = {}
    for n in SMALL:
        off, rows = table[n]
        size = int(np.prod(like[n].shape))
        out[n] = packed[off:off + rows].reshape(-1)[:size].reshape(like[n].shape)
    return out


def _local_step(x, mod, target, w, fetch, emit):
    S = x.shape[0]
    mall = _bfc(_hg_consts())
    mall_t = _bfc(_hg_consts().T)
    consts = _ssd_consts()
    p_lb, lbs = _lower_bounds(w["hg_lb_logits"])
    no_tok = jnp.zeros((8, LANE), F32)
    wa, wx = _block_diag(w["lru_wa"]), _block_diag(w["lru_wx"])
    ba, bx = w["lru_ba"].reshape(DEPTH, 1, LRU_W), w["lru_bx"].reshape(DEPTH, 1, LRU_W)
    lru_cb, lam, ssd_cb = w["lru_conv_b"][:, None], w["lru_lambda"][:, None], w["ssd_conv_b"][:, None]
    bias, alog = _pad_lanes(w["ssd_dt_bias"]), _pad_lanes(w["ssd_a_log"])
    dskip = jnp.repeat(w["ssd_d"], SSD_P, axis=1)
    saved = []
    for l in range(DEPTH):
        w_in_l, w_out_l, token = fetch(l, x)
        shift, scale, gate = (_Row(mod, l, D_MODEL, k) for k in range(3))
        nw = _Row(w["norm_w"], l)
        u, h = _inproj_fwd(x, nw, scale, shift, w_in_l, no_tok if token is None else token)
        ycat = lax.empty((S, D_INNER), BF16)
        lru_args = (l, u, w["lru_conv_w"], lru_cb, wa, ba, wx, bx, lam)
        ycat, h_lru = _lru_fwd(*lru_args, ycat)
        hg_args = (u, _Row(lbs, l), _Row(w["hg_norm_w"], l), mall)
        ycat, o_b, hg_st = _hg_fwd(*hg_args, ycat)
        xbc = _ssdconv_fwd(l, u, w["ssd_conv_w"], ssd_cb)
        ssd_args = (u, xbc, _Row(bias, l), _Row(alog, l), _Row(dskip, l), _Row(w["ssd_norm_w"], l), consts)
        ycat, y_ssd, ssd_st = _ssd_fwd(*ssd_args, ycat)
        token = fetch(l, y_ssd, late=True)
        x_new, y = _outproj_fwd(ycat, w_out_l, x, gate, no_tok if token is None else token)
        saved.append((x, u, h, ycat, nw, scale, gate, w_in_l, w_out_l, lru_args, h_lru, hg_args, o_b, hg_st, ssd_args,
                      y_ssd, ssd_st, y))
        x = x_new
    dx, red = _loss_head(x, w["final_norm_w"][None, :], target)
    loss = red[1, 0]
    reds = {k: [None] * DEPTH for k in ("in", "gate", "lru", "wa", "wx", "hg", "conv", "ssd")}
    for l in reversed(range(DEPTH)):
        (x, u, h, ycat, nw, scale, gate, w_in_l, w_out_l, lru_args, h_lru, hg_args, o_b, hg_st, ssd_args, y_ssd, ssd_st,
         y) = saved[l]
        dycat, g_out, reds["gate"][l] = _outproj_bwd(dx, y, gate, ycat, w_out_l)
        token = emit(l, "w_out", g_out)
        du = lax.empty((S, N_PAD), BF16)
        du, dxbc, reds["ssd"][l] = _ssd_bwd(*ssd_args, y_ssd, ssd_st, dycat, du, no_tok if token is None else token)
        du, reds["conv"][l] = _ssdconv_bwd(l, u, w["ssd_conv_w"], ssd_cb, dxbc, du)
        du, reds["hg"][l] = _hg_bwd(*hg_args, mall_t, o_b, hg_st, dycat, du)
        du, reds["lru"][l], reds["wa"][l], reds["wx"][l] = _lru_bwd(*lru_args, h_lru, dycat, du)
        token = emit(l, "w_in", _inproj_bwd_w(h, du))
        dx, reds["in"][l] = _inproj_bwd_x(du, w_in_l, x, nw, scale, dx, no_tok if token is None else token)
    r = {k: jnp.stack(v) for k, v in reds.items()}
    g = {n: None for n in WEIGHTS}
    g["final_norm_w"] = red[0]
    g["norm_w"] = r["in"][:, 2]
    dmod = jnp.concatenate([r["in"][:, 0], r["in"][:, 1], r["gate"][:, 0]], axis=1)
    g["lru_conv_w"], g["lru_conv_b"] = r["lru"][:, 0:4], r["lru"][:, 4]
    g["lru_ba"], g["lru_bx"] = r["lru"][:, 5].reshape(DEPTH, 8, 64), r["lru"][:, 6].reshape(DEPTH, 8, 64)
    g["lru_lambda"] = r["lru"][:, 7]
    g["lru_wa"], g["lru_wx"] = _diag_blocks(r["wa"]), _diag_blocks(r["wx"])
    g["hg_norm_w"] = r["hg"][:, 0]
    g["hg_lb_logits"] = _lower_bounds_bwd(p_lb, r["hg"][:, 1])
    g["ssd_conv_w"], g["ssd_conv_b"] = r["conv"][:, 0:4], r["conv"][:, 4]
    g["ssd_norm_w"] = r["ssd"][:, 0]
    g["ssd_d"] = r["ssd"][:, 1].reshape(DEPTH, SSD_HEADS, SSD_P).sum(-1)
    g["ssd_dt_bias"] = r["ssd"][:, 2, :SSD_HEADS]
    g["ssd_a_log"] = -r["ssd"][:, 3, :SSD_HEADS] * jnp.exp(w["ssd_a_log"])
    return loss, dx, dmod, g


def kernel(x, c, norm_w, w_ada, b_ada, w_in, lru_conv_w, lru_conv_b, lru_wa, lru_ba, lru_wx, lru_bx, lru_lambda, hg_lb_logits, hg_norm_w, ssd_conv_w, ssd_conv_b, ssd_dt_bias, ssd_a_log, ssd_d, ssd_norm_w, w_out, final_norm_w, loss_target, m_norm_w, m_w_ada, m_b_ada, m_w_in, m_lru_conv_w, m_lru_conv_b, m_lru_wa, m_lru_ba, m_lru_wx, m_lru_bx, m_lru_lambda, m_hg_lb_logits, m_hg_norm_w, m_ssd_conv_w, m_ssd_conv_b, m_ssd_dt_bias, m_ssd_a_log, m_ssd_d, m_ssd_norm_w, m_w_out, m_final_norm_w, v_norm_w, v_w_ada, v_b_ada, v_w_in, v_lru_conv_w, v_lru_conv_b, v_lru_wa, v_lru_ba, v_lru_wx, v_lru_bx, v_lru_lambda, v_hg_lb_logits, v_hg_norm_w, v_ssd_conv_w, v_ssd_conv_b, v_ssd_dt_bias, v_ssd_a_log, v_ssd_d, v_ssd_norm_w, v_w_out, v_final_norm_w):
    return _step(x, c, norm_w, w_ada, b_ada, w_in, lru_conv_w, lru_conv_b, lru_wa, lru_ba, lru_wx, lru_bx, lru_lambda, hg_lb_logits, hg_norm_w, ssd_conv_w, ssd_conv_b, ssd_dt_bias, ssd_a_log, ssd_d, ssd_norm_w, w_out, final_norm_w, loss_target, m_norm_w, m_w_ada, m_b_ada, m_w_in, m_lru_conv_w, m_lru_conv_b, m_lru_wa, m_lru_ba, m_lru_wx, m_lru_bx, m_lru_lambda, m_hg_lb_logits, m_hg_norm_w, m_ssd_conv_w, m_ssd_conv_b, m_ssd_dt_bias, m_ssd_a_log, m_ssd_d, m_ssd_norm_w, m_w_out, m_final_norm_w, v_norm_w, v_w_ada, v_b_ada, v_w_in, v_lru_conv_w, v_lru_conv_b, v_lru_wa, v_lru_ba, v_lru_wx, v_lru_bx, v_lru_lambda, v_hg_lb_logits, v_hg_norm_w, v_ssd_conv_w, v_ssd_conv_b, v_ssd_dt_bias, v_ssd_a_log, v_ssd_d, v_ssd_norm_w, v_w_out, v_final_norm_w)


def _step(*args):
    a = dict(zip(INPUTS, args, strict=True))
    me = 4 * lax.axis_index("x") + 2 * lax.axis_index("y") + lax.axis_index("c")
    x, target = a["x"][0], a["loss_target"][0]

    c_all = _all_gather(a["c"], "gather_c")[:, 0, :]
    b_cols = lax.dynamic_slice_in_dim(a["b_ada"], me * ADA_COLS, ADA_COLS, axis=1)[:, None, :]
    mod_parts = _all_gather(_ada_fwd(c_all, a["w_ada"], b_cols), "gather_mod")
    mod = lax.dynamic_index_in_dim(mod_parts, me, axis=2, keepdims=False)
    mod = mod.transpose(1, 0, 2).reshape(DEPTH, 3 * D_MODEL)

    w = {n: a[n] for n in SMALL}

    w_in_b = [a["w_in"][l].astype(BF16) for l in range(DEPTH)]
    w_out_b = a["w_out"].astype(BF16)
    conv_own = jnp.concatenate([a["lru_conv_w"], a["ssd_conv_w"]], axis=-1)
    cols, rows_out = N_IN // N_DEV, D_INNER // N_DEV

    def gather_start(l, after):
        srcs = [w_in_b[l], w_out_b[l]] + ([conv_own] if l == 0 else [])
        lands = [lax.empty((N_DEV,) + s.shape, s.dtype) for s in srcs]
        return _exchange_start(f"gather_start_{l}", srcs, lands, "chip", after=after)

    def gather_pass(name, st, after, also=()):
        landed = _exchange_wait(name + "_wait", st, after, also)
        st2 = _exchange_start(name + "_pass", st["srcs"], landed, "pass")
        return _exchange_wait(name + "_passed", st2, after)

    gathers = {0: gather_start(0, mod)}
    passing = {}

    def fetch(l, x_l, late=False):
        if late:
            if l + 1 == DEPTH:
                return None
            landed = _exchange_wait(f"gather_{l + 1}_wait", gathers[l + 1], x_l)
            passing[l + 1] = _exchange_start(f"gather_{l + 1}_pass", gathers[l + 1]["srcs"], landed, "pass")
            return passing[l + 1]["token"]
        if l == 0:
            landed = gather_pass("gather_0", gathers[0], x_l, also=(a["w_in"], a["m_w_in"], a["v_w_in"]))
        else:
            landed = _exchange_wait(f"gather_{l}_passed", passing[l], x_l)
        land_out = lax.dynamic_update_index_in_dim(landed[1], w_out_b[l], me, 0)
        if l == 0:
            conv = lax.dynamic_update_index_in_dim(landed[2], conv_own, me, 0).transpose(1, 2, 0, 3)
            w["lru_conv_w"] = conv[..., :64].reshape(DEPTH, 4, LRU_W)
            w["ssd_conv_w"] = conv[..., 64:].reshape(DEPTH, 4, SSD_CONV)
        token = None
        if l + 1 < DEPTH:
            gathers[l + 1] = gather_start(l + 1, land_out)
            token = gathers[l + 1]["token"]
        return _relayout_in(landed[0], w_in_b[l]), land_out.reshape(D_INNER, D_MODEL), token

    PROJ = ("w_in", "w_out")
    scatters = {}
    lands = [lax.empty((N_DEV, DEPTH, D_MODEL, cols), BF16), lax.empty((N_DEV, DEPTH, rows_out, D_MODEL), BF16)]
    own = [None] * DEPTH

    deferred, g_out = {}, {}

    def emit(l, name, grad, after=None):
        if name == "w_out":
            g_out[l] = grad
            return None
        if l == 0 and after is None:
            deferred["w_in"] = grad
            return None
        srcs = [_relayout_grad(grad), g_out[l].reshape(N_DEV, rows_out, D_MODEL)]
        st = _exchange_start(f"scatter_start_{l}", srcs, lands, "scatter", layer=l, after=after)
        scatters[l] = st
        lands[:] = st["lands"]
        return st["token"]

    loss_own, dx, dmod, g = _local_step(x, mod, target, w, fetch, emit)

    def sharded(name, parts, own=None, **kw):
        return _adamw(parts, a[name], a["m_" + name], a["v_" + name], "adamw_" + name + kw.pop("tag", ""), own=own, **kw)

    g["b_ada"] = dmod
    small_own = _flatten_small(g, last=loss_own)
    small_st = _exchange_start("gather_small", [small_own], [lax.empty((N_DEV,) + small_own.shape, F32)], "chip",
                               after=dx)
    big = {}
    after = emit(0, "w_in", deferred["w_in"], after=small_st["token"]) + dx[0:8, 0:LANE]

    def own_slices(st):
        return [lax.dynamic_index_in_dim(s, me, 0, keepdims=False) for s in st["srcs"]]

    for l in reversed(range(1, DEPTH)):
        scatters[l]["lands"] = lands
        lands[:] = _exchange_wait(f"scatter_wait_{l}", scatters[l], after)
        own[l] = own_slices(scatters[l])
    upper = {name: sharded(name, lands[k], jnp.stack([own[l][k] for l in range(1, DEPTH)]), layers=(1, DEPTH),
                           tag="_upper") for k, name in enumerate(PROJ)}
    after = upper["w_in"][1][0, 0:8, 0:LANE] + upper["w_out"][1][0, 0:8, 0:LANE]
    small = gather_pass("gather_small", small_st, after)[0]
    outs = _adamw(small[:, None], *[_flatten_small(a, p)[None] for p in ("", "m_", "v_")], "adamw_small",
                  own=small_own[None])
    res = [_split_small(o[0], a) for o in outs]
    losses = lax.dynamic_update_index_in_dim(small[:, -1, 0], loss_own, me, 0)
    loss = jnp.sum(losses)

    off = _small_rows(a)[0]["b_ada"][0]
    dmod_all = lax.dynamic_update_index_in_dim(small[:, off:off + DEPTH * 3 * D_MODEL // SMALL_ROW],
                                               dmod.reshape(-1, SMALL_ROW), me, 0)
    dmod_all = dmod_all.reshape(N_DEV, DEPTH, 3 * D_MODEL).transpose(1, 0, 2)
    dmod_cols = lax.dynamic_slice_in_dim(dmod_all, me * ADA_COLS, ADA_COLS, axis=2)
    dmod_pad = jnp.pad(dmod_cols, ((0, 0), (0, LANE - N_DEV), (0, 0)))
    ct_pad = jnp.pad(c_all.T, ((0, 0), (0, LANE - N_DEV)))
    big["w_ada"] = sharded("w_ada", _ada_bwd(ct_pad, dmod_pad)[None])
    g_conv = jnp.concatenate([g["lru_conv_w"].reshape(DEPTH, 4, N_DEV, 64), g["ssd_conv_w"].reshape(DEPTH, 4, N_DEV, 192)],
                             axis=-1).transpose(2, 0, 1, 3)
    conv_parts = _all_to_all(g_conv, "scatter_conv")
    big["lru_conv_w"] = sharded("lru_conv_w", conv_parts[..., :64])
    big["ssd_conv_w"] = sharded("ssd_conv_w", conv_parts[..., 64:])

    after = outs[1] + big["w_ada"][1][0, 0:1, 0:1]
    scatters[0]["lands"] = lands
    lands[:] = _exchange_wait("scatter_wait_0", scatters[0], after)
    own[0] = own_slices(scatters[0])
    for k, name in enumerate(PROJ):
        big[name] = sharded(name, lands[k], own[0][k][None], layers=(0, 1), prev=upper[name])

    out = [loss, dx[None]]
    for k in range(4):
        out += [big[n][k] if n in big else res[k][n] for n in WEIGHTS]
    return tuple(out)
```

```python
import functools

import numpy as np
import jax
import jax.numpy as jnp
from jax import lax
from jax.experimental import pallas as pl
from jax.experimental.pallas import tpu as pltpu

F32 = jnp.float32
BF16 = jnp.bfloat16
SDS = jax.ShapeDtypeStruct

N_DEV = 8
DEPTH = 4
D_MODEL = 1024
D_INNER = 2048
EPS = 1e-6
LRU_W = 512
LRU_C = 8.0
HG_W = 512
HG_CHUNK = 64
HG_HEADS = 4
SSD_W = 1024
SSD_HEADS = 16
SSD_P = 64
SSD_N = 128
SSD_CHUNK = 128
SSD_CONV = 1536
N_IN = 5648
N_PAD = 5760
OFF_HG = 0
OFF_LRU = 2048
OFF_XBC = 3072
OFF_Z = 4608
LANE = 128
VMEM_LIMIT = 56 * 1024 * 1024
NEG = -1e30

ADAM_LR = 0.001
ADAM_B1 = 0.9
ADAM_B2 = 0.999
ADAM_EPS = 1e-08
ADAM_WD = 0.01
ADAM_STEP = 10


def _cp(sem=None):
    return pltpu.CompilerParams(dimension_semantics=sem, vmem_limit_bytes=VMEM_LIMIT)


def _dg(a, b, ca, cb):
    return lax.dot_general(a, b, (((ca,), (cb,)), ((), ())), preferred_element_type=F32)


def _mm(a, b):
    return _dg(a, b, 1, 0)


def _mm_nt(a, b):
    return _dg(a, b, 1, 1)


def _mm_tn(a, b):
    return _dg(a, b, 0, 0)


def _bf(x):
    return x.astype(BF16)


def _f(x):
    return x.astype(F32)


def _split3(x):
    hi = x.astype(BF16)
    r = x - hi.astype(F32)
    mid = r.astype(BF16)
    lo = (r - mid.astype(F32)).astype(BF16)
    return hi, mid, lo


def _sel_r(x, m):
    hi, mid, lo = _split3(x)
    return _mm(hi, m) + _mm(mid, m) + _mm(lo, m)


def _sel_l(m, x):
    hi, mid, lo = _split3(x)
    return _mm(m, hi) + _mm(m, mid) + _mm(m, lo)


def _sel_l2(m, x):
    hi = x.astype(BF16)
    lo = (x - hi.astype(F32)).astype(BF16)
    return _mm(m, hi) + _mm(m, lo)


def _sel_tn(x, m):
    hi, mid, lo = _split3(x)
    return _mm_tn(hi, m) + _mm_tn(mid, m) + _mm_tn(lo, m)


def _sigmoid(x):
    return 1.0 / (1.0 + jnp.exp(-x))


def _silu(x):
    return x * _sigmoid(x)


def _dsilu(x):
    s = _sigmoid(x)
    return s * (1.0 + x * (1.0 - s))


def _softplus(x):
    return jnp.maximum(x, 0.0) + jnp.log(1.0 + jnp.exp(-jnp.abs(x)))


def _expm1(z):
    series = z * (1.0 + z * (1.0 / 2) * (1.0 + z * (1.0 / 3) * (1.0 + z * (1.0 / 4) * (
        1.0 + z * (1.0 / 5) * (1.0 + z * (1.0 / 6) * (1.0 + z * (1.0 / 7)))))))
    return jnp.where(jnp.abs(z) < 0.3, series, jnp.exp(z) - 1.0)


def _iota(shape, dim):
    return lax.broadcasted_iota(jnp.int32, shape, dim)


def _last_row(x, rows):
    return jnp.sum(jnp.where(rows == x.shape[0] - 1, x, 0.0), axis=0, keepdims=True)


def _shift_down(x, d, rows, fill=0.0):
    return jnp.where(rows >= d, pltpu.roll(x, d, 0), fill)


def _shift_up(x, d, rows, fill=0.0):
    n = x.shape[0]
    return jnp.where(rows < n - d, pltpu.roll(x, n - d, 0), fill)


def _conv_fwd(x, cw_ref, cb_ref, rows):
    out = cb_ref[...] + cw_ref[pl.ds(3, 1), :] * x
    for k in range(3):
        out = out + cw_ref[pl.ds(k, 1), :] * _shift_down(x, 3 - k, rows)
    return out


def _conv_bwd(x, dco, cw_ref, rows):
    dx = cw_ref[pl.ds(3, 1), :] * dco
    dws = []
    for k in range(3):
        dx = dx + cw_ref[pl.ds(k, 1), :] * _shift_up(dco, 3 - k, rows)
        dws.append(jnp.sum(dco * _shift_down(x, 3 - k, rows), axis=0, keepdims=True))
    dws.append(jnp.sum(dco * x, axis=0, keepdims=True))
    return dx, dws, jnp.sum(dco, axis=0, keepdims=True)


def _vec(n):
    return pl.BlockSpec((1, n), lambda *_: (0, 0))


class _Row:
    def __init__(self, arr, l, n=None, c=0):
        self.arr, self.l, self.n, self.c = arr[:, None, :], l, n or arr.shape[1], c


def _spec(v):
    if isinstance(v, _Row):
        return pl.BlockSpec((None, 1, v.n), lambda *_: (v.l, 0, v.c))
    return _vec(v.shape[1])


def _arr(v):
    return v.arr if isinstance(v, _Row) else v


def _full(shape):
    nd = len(shape)
    return pl.BlockSpec(shape, lambda *_: (0,) * nd)


def _inproj_fwd(x, nw, scale, shift, w, tok):
    S = x.shape[0]
    tm = min(256, S)

    def body(x_ref, nw_ref, sc_ref, sh_ref, w_ref, tok_ref, u_ref, h_ref):
        del tok_ref
        xv = x_ref[...]
        inv = lax.rsqrt(jnp.mean(xv * xv, axis=-1, keepdims=True) + EPS)
        h = ((xv * inv) * nw_ref[...] * (1.0 + sc_ref[...]) + sh_ref[...]).astype(BF16)
        h_ref[...] = h
        u_ref[...] = _mm(h, w_ref[...])

    return pl.pallas_call(
        body, name="inproj_fwd", grid=(S // tm,),
        in_specs=[pl.BlockSpec((tm, D_MODEL), lambda i: (i, 0)), _spec(nw), _spec(scale), _spec(shift),
                  _full((D_MODEL, N_PAD)), pl.BlockSpec(memory_space=pl.ANY)],
        out_specs=[pl.BlockSpec((tm, N_PAD), lambda i: (i, 0)), pl.BlockSpec((tm, D_MODEL), lambda i: (i, 0))],
        out_shape=[SDS((S, N_PAD), F32), SDS((S, D_MODEL), BF16)],
        compiler_params=_cp(("parallel",)),
    )(x, _arr(nw), _arr(scale), _arr(shift), w, tok)


def _inproj_bwd_x(du, w, x, nw, scale, dxn, tok):
    S = x.shape[0]
    tm = min(256, S)

    def body(du_ref, w_ref, x_ref, nw_ref, sc_ref, dxn_ref, tok_ref, dx_ref, red_ref):
        del tok_ref

        @pl.when(pl.program_id(0) == 0)
        def _():
            red_ref[...] = jnp.zeros_like(red_ref)

        dh = _mm_nt(du_ref[...], w_ref[...])
        xv = x_ref[...]
        inv = lax.rsqrt(jnp.mean(xv * xv, axis=-1, keepdims=True) + EPS)
        xhat = xv * inv
        nwv = nw_ref[...]
        g1 = 1.0 + sc_ref[...]
        dxhat = dh * nwv * g1
        dx = inv * (dxhat - xhat * jnp.mean(dxhat * xhat, axis=-1, keepdims=True))
        dx_ref[...] = dxn_ref[...] + dx
        red_ref[0:1, :] += jnp.sum(dh, axis=0, keepdims=True)
        red_ref[1:2, :] += jnp.sum(dh * xhat * nwv, axis=0, keepdims=True)
        red_ref[2:3, :] += jnp.sum(dh * xhat * g1, axis=0, keepdims=True)

    row = pl.BlockSpec((tm, D_MODEL), lambda i: (i, 0))
    return pl.pallas_call(
        body, name="inproj_bwd_x", grid=(S // tm,),
        in_specs=[pl.BlockSpec((tm, N_PAD), lambda i: (i, 0)), _full((D_MODEL, N_PAD)), row, _spec(nw),
                  _spec(scale), row, pl.BlockSpec(memory_space=pl.ANY)],
        out_specs=[row, _full((8, D_MODEL))],
        out_shape=[SDS((S, D_MODEL), F32), SDS((8, D_MODEL), F32)],
        compiler_params=_cp(("arbitrary",)),
    )(du, w, x, _arr(nw), _arr(scale), dxn, tok)


def _inproj_bwd_w(h, du):
    S = h.shape[0]
    tn = 640

    def body(h_ref, du_ref, gw_ref):
        gw_ref[...] = _mm_tn(h_ref[...], _bf(du_ref[...]))

    return pl.pallas_call(
        body, name="inproj_bwd_w", grid=(N_PAD // tn,),
        in_specs=[_full((S, D_MODEL)), pl.BlockSpec((S, tn), lambda j: (0, j))],
        out_specs=pl.BlockSpec((D_MODEL, tn), lambda j: (0, j)),
        out_shape=SDS((D_MODEL, N_PAD), F32),
        compiler_params=_cp(("parallel",)),
    )(h, du)


def _scan_block(a, b, rows):
    d = 1
    while d < a.shape[0]:
        a_s = _shift_down(a, d, rows, 1.0)
        b_s = _shift_down(b, d, rows, 0.0)
        b = a * b_s + b
        a = a * a_s
        d *= 2
    return a, b


def _rscan_block(c, g, rows):
    d = 1
    while d < c.shape[0]:
        c_s = _shift_up(c, d, rows, 1.0)
        g_s = _shift_up(g, d, rows, 0.0)
        g = g + c * g_s
        c = c * c_s
        d *= 2
    return c, g


LRU_BLOCK = 256


def _lru_gates(xa, wa_ref, ba_ref, wx_ref, bx_ref, lam_ref):
    sp = _softplus(-lam_ref[...])
    xb = _bf(xa)
    r = _sigmoid(_mm(xb, wa_ref[...]) + ba_ref[...])
    ig = _sigmoid(_mm(xb, wx_ref[...]) + bx_ref[...])
    la = -LRU_C * r * sp
    a = jnp.exp(la)
    mult = jnp.sqrt(-_expm1(2.0 * la))
    return sp, r, ig, la, a, mult


def _lru_specs(S, l):
    t128 = pl.BlockSpec((None, 1, LANE), lambda t: (l, 0, t))
    gate = pl.BlockSpec((None, None, LANE, LANE), lambda t: (l, t, 0, 0))
    return [pl.BlockSpec((S, 2 * LANE), lambda t: (0, OFF_LRU // (2 * LANE) + t)),
            pl.BlockSpec((None, 4, LANE), lambda t: (l, 0, t)), t128, gate, t128, gate, t128, t128]


def _lru_fwd(l, u, cw, cb, wa, ba, wx, bx, lam, ycat):
    S = u.shape[0]
    tb = min(LRU_BLOCK, S)

    def body(u_ref, cw_ref, cb_ref, wa_ref, ba_ref, wx_ref, bx_ref, lam_ref, ycat_in, ycat_ref, h_ref, a_scr, b_scr):
        del ycat_in
        rows = _iota((S, LANE), 0)
        xa = _conv_fwd(_f(u_ref[:, 0:LANE]), cw_ref, cb_ref, rows)
        _, _, ig, _, a, mult = _lru_gates(xa, wa_ref, ba_ref, wx_ref, bx_ref, lam_ref)
        a_scr[...] = a
        b_scr[...] = mult * (ig * xa)
        rows_b = _iota((tb, LANE), 0)

        def blk(j, hprev):
            sl = pl.ds(pl.multiple_of(j * tb, tb), tb)
            acum, hloc = _scan_block(a_scr[sl, :], b_scr[sl, :], rows_b)
            hf = hloc + acum * hprev
            h_ref[sl, :] = hf
            return _last_row(hf, rows_b)

        lax.fori_loop(0, S // tb, blk, jnp.zeros((1, LANE), F32))
        ycat_ref[...] = _bf(h_ref[...] * _silu(_f(u_ref[:, LANE:2 * LANE])))

    col = pl.BlockSpec((S, LANE), lambda t: (0, t))
    return pl.pallas_call(
        body, name="lru_fwd", grid=(LRU_W // LANE,),
        in_specs=_lru_specs(S, l) + [pl.BlockSpec(memory_space=pl.ANY)],
        out_specs=[col, col],
        out_shape=[SDS((S, D_INNER), BF16), SDS((S,LRU_W), F32)],
        scratch_shapes=[pltpu.VMEM((S, LANE), F32), pltpu.VMEM((S, LANE), F32)],
        input_output_aliases={8: 0},
        compiler_params=_cp(("parallel",)),
    )(u, cw, cb, wa, ba, wx, bx, lam, ycat)


def _lru_bwd(l, u, cw, cb, wa, ba, wx, bx, lam, h_lru, dycat, du):
    S = u.shape[0]
    tb = min(LRU_BLOCK, S)

    def body(u_ref, cw_ref, cb_ref, wa_ref, ba_ref, wx_ref, bx_ref, lam_ref, h_ref, dy_ref, du_in,
             du_ref, red_ref, gwa_ref, gwx_ref, c_scr, g_scr, l_scr):
        del du_in
        rows = _iota((S, LANE), 0)
        ax = _f(u_ref[:, 0:LANE])
        ag = _f(u_ref[:, LANE:2 * LANE])
        xa = _conv_fwd(ax, cw_ref, cb_ref, rows)
        sp, r, ig, la, a, mult = _lru_gates(xa, wa_ref, ba_ref, wx_ref, bx_ref, lam_ref)
        h = h_ref[...]
        dy = _f(dy_ref[...])
        du_ref[:, LANE:2 * LANE] = _bf(dy * h * _dsilu(ag))
        c_scr[...] = _shift_up(a, 1, rows, 0.0)
        g_scr[...] = dy * _silu(ag)
        rows_b = _iota((tb, LANE), 0)
        nb = S // tb

        def blk(jj, lnext):
            j = nb - 1 - jj
            sl = pl.ds(pl.multiple_of(j * tb, tb), tb)
            ccum, lloc = _rscan_block(c_scr[sl, :], g_scr[sl, :], rows_b)
            lam_t = lloc + ccum * lnext
            l_scr[sl, :] = lam_t
            return jnp.sum(jnp.where(rows_b == 0, lam_t, 0.0), axis=0, keepdims=True)

        lax.fori_loop(0, nb, blk, jnp.zeros((1, LANE), F32))
        db = l_scr[...]
        da = db * _shift_down(h, 1, rows)
        dmult = db * ig * xa
        dig = db * mult * xa
        dxa = db * mult * ig
        dla = da * a - dmult * (a * a) / mult
        dr = -LRU_C * sp * dla
        dsp = jnp.sum(-LRU_C * r * dla, axis=0, keepdims=True)
        dlam = -dsp * _sigmoid(-lam_ref[...])
        dzr = dr * r * (1.0 - r)
        dzi = dig * ig * (1.0 - ig)
        dzr_b, dzi_b, xa_b = _bf(dzr), _bf(dzi), _bf(xa)
        dxa = dxa + _mm_nt(dzr_b, wa_ref[...]) + _mm_nt(dzi_b, wx_ref[...])
        gwa_ref[...] = _mm_tn(xa_b, dzr_b)
        gwx_ref[...] = _mm_tn(xa_b, dzi_b)
        dax, dws, dcb = _conv_bwd(ax, dxa, cw_ref, rows)
        du_ref[:, 0:LANE] = _bf(dax)
        parts = dws + [dcb, jnp.sum(dzr, axis=0, keepdims=True), jnp.sum(dzi, axis=0, keepdims=True), dlam]
        for n, p in enumerate(parts):
            red_ref[pl.ds(n, 1), :] = p

    col = pl.BlockSpec((S, LANE), lambda t: (0, t))
    gw = pl.BlockSpec((None, LANE, LANE), lambda t: (t, 0, 0))
    return pl.pallas_call(
        body, name="lru_bwd", grid=(LRU_W // LANE,),
        in_specs=_lru_specs(S, l) + [col, col, pl.BlockSpec(memory_space=pl.ANY)],
        out_specs=[pl.BlockSpec((S, 2 * LANE), lambda t: (0, OFF_LRU // (2 * LANE) + t)),
                   pl.BlockSpec((8, LANE), lambda t: (0, t)), gw, gw],
        out_shape=[SDS((S, N_PAD), BF16), SDS((8, LRU_W), F32), SDS((4, LANE, LANE), F32), SDS((4, LANE, LANE), F32)],
        scratch_shapes=[pltpu.VMEM((S, LANE), F32)] * 3,
        input_output_aliases={10: 0},
        compiler_params=_cp(("parallel",)),
    )(u, cw, cb, wa, ba, wx, bx, lam, h_lru, dycat, du)


HG_LEVELS = 6


def _hg_consts():
    C = HG_CHUNK
    t = np.arange(C)[:, None]
    r = np.arange(C)[None, :]
    mats = []
    for l in range(HG_LEVELS):
        b = 1 << l
        upper = (t % (2 * b)) >= b
        anchor = (t // (2 * b)) * 2 * b + b - 1
        mats.append((upper & (r > anchor) & (r <= t)) | ((~upper) & (r > t) & (r <= anchor)))
    mats.append(r <= t)
    mats.append(r > t)
    return np.concatenate(mats, 0).astype(np.float32)


def _hg_factors(hf, lb, mall):
    s = _sigmoid(hf)
    f = lb + (1.0 - lb) * s
    lf = jnp.log(f)
    k = (1.0 - lb) * _sigmoid(-hf)
    e = jnp.exp(_sel_l(mall, lf))
    C = HG_CHUNK
    rows = _iota((C, HG_W), 0)
    eq, ek = [], []
    for l in range(HG_LEVELS):
        el = e[l * C:(l + 1) * C]
        eq.append(jnp.where((lax.shift_right_logical(rows, l) & 1) == 1, el, 0.0))
        ek.append(el - eq[l])
    ecum = e[HG_LEVELS * C:(HG_LEVELS + 1) * C]
    erem = e[(HG_LEVELS + 1) * C:(HG_LEVELS + 2) * C]
    return s, f, k, eq, ek, ecum, erem


def _hg_masks():
    C = HG_CHUNK
    ri, ci = _iota((C, C), 0), _iota((C, C), 1)
    rr = _iota((C, LANE), 0)
    gm = [(lax.shift_right_logical(ri, l + 1) == lax.shift_right_logical(ci, l + 1)).astype(F32)
          for l in range(HG_LEVELS)]
    up = [(lax.shift_right_logical(rr, l) & 1) == 1 for l in range(HG_LEVELS)]
    eye = (ri == ci).astype(F32)
    return gm, up, eye, rr


def _hg_scores(qh, kh, eq, ek, sl, gm, up, eye):
    del up
    qs, ks, qb, kb = [], [], [], []
    p = _mm_nt(_bf(qh), _bf(kh)) * eye
    for l in range(HG_LEVELS):
        qs.append(qh * eq[l][:, sl])
        ks.append(kh * ek[l][:, sl])
        qb.append(_bf(qs[l]))
        kb.append(_bf(ks[l]))
        p = p + _mm_nt(qb[l], kb[l]) * gm[l]
    return p, qs, ks, qb, kb


HG_SUB = 4


def _hg_fwd(u, lb, nw, mall, ycat):
    S = u.shape[0]
    C = HG_CHUNK
    n = S // C
    rows = HG_SUB * C

    def body(u_ref, lb_ref, nw_ref, mall_ref, ycat_in, ycat_ref, o_ref, st_ref, st):
        del ycat_in

        @pl.when(pl.program_id(0) == 0)
        def _():
            st[...] = jnp.zeros_like(st)

        gm, up, eye, rr = _hg_masks()
        for sub in range(HG_SUB):
            r = slice(sub * C, (sub + 1) * C)
            q = _silu(_f(u_ref[r, 0:512]))
            v = u_ref[r, 1024:1536]
            _, _, k, eq, ek, ecum, erem = _hg_factors(_f(u_ref[r, 512:1024]), lb_ref[...], mall_ref[...])
            for h in range(HG_HEADS):
                sl = slice(h * LANE, (h + 1) * LANE)
                qh, kh, vh = q[:, sl], k[:, sl], _bf(v[:, sl])
                p = _hg_scores(qh, kh, eq, ek, sl, gm, up, eye)[0]
                sth = st[h]
                st_ref[sub, h] = sth
                o_ref[r, sl] = _mm(_bf(p), vh) + _mm_nt(_bf(qh * ecum[:, sl]), _bf(sth))
                st[h] = sth * _last_row(ecum[:, sl], rr) + _mm_tn(vh, _bf(kh * erem[:, sl]))
            o = o_ref[r, :]
            inv = lax.rsqrt(jnp.mean(o * o, axis=-1, keepdims=True) + EPS)
            ycat_ref[r, :] = _bf((o * inv) * nw_ref[...] * _silu(_f(u_ref[r, 1536:2048])))

    return pl.pallas_call(
        body, name="hg_fwd", grid=(n // HG_SUB,),
        in_specs=[pl.BlockSpec((rows, 2048), lambda i: (i, 0)), _spec(lb), _spec(nw), _full(mall.shape),
                  pl.BlockSpec(memory_space=pl.ANY)],
        out_specs=[pl.BlockSpec((rows, HG_W), lambda i: (i, 1)), pl.BlockSpec((rows, HG_W), lambda i: (i, 0)),
                   pl.BlockSpec((HG_SUB, HG_HEADS, LANE, LANE), lambda i: (i, 0, 0, 0))],
        out_shape=[SDS((S, D_INNER), BF16), SDS((S,HG_W), F32), SDS((n, HG_HEADS, LANE, LANE), F32)],
        scratch_shapes=[pltpu.VMEM((HG_HEADS, LANE, LANE), F32)],
        input_output_aliases={4: 0},
        compiler_params=_cp(("arbitrary",)),
    )(u, _arr(lb), _arr(nw), mall, ycat)


def _hg_bwd(u, lb, nw, mall, mall_t, o_b, states, dycat, du):
    S = u.shape[0]
    C = HG_CHUNK
    n = S // C
    nb = n // HG_SUB
    rows = HG_SUB * C
    L2 = HG_LEVELS

    def body(u_ref, lb_ref, nw_ref, mall_ref, mallt_ref, o_ref, st_ref, dy_ref, du_in, du_ref, red_ref,
             dst, dlast_s, dq_s, dk_s, dex):
        del du_in

        @pl.when(pl.program_id(0) == 0)
        def _():
            dst[...] = jnp.zeros_like(dst)
            red_ref[...] = jnp.zeros_like(red_ref)

        lb = lb_ref[...]
        nwv = nw_ref[...]
        gm, up, eye, rr = _hg_masks()
        for sub in reversed(range(HG_SUB)):
            r = slice(sub * C, (sub + 1) * C)
            hq, hf, hg = _f(u_ref[r, 0:512]), _f(u_ref[r, 512:1024]), _f(u_ref[r, 1536:2048])
            q = _silu(hq)
            v = u_ref[r, 1024:1536]
            s, f, k, eq, ek, ecum, erem = _hg_factors(hf, lb, mall_ref[...])
            o = o_ref[r, :]
            dy = _f(dy_ref[r, :])
            inv = lax.rsqrt(jnp.mean(o * o, axis=-1, keepdims=True) + EPS)
            ohat = o * inv
            du_ref[r, 1536:2048] = _bf(dy * ohat * nwv * _dsilu(hg))
            dn = dy * _silu(hg)
            red_ref[0:1, :] += jnp.sum(dn * ohat, axis=0, keepdims=True)
            dohat = dn * nwv
            do = inv * (dohat - ohat * jnp.mean(dohat * ohat, axis=-1, keepdims=True))
            for h in range(HG_HEADS):
                sl = slice(h * LANE, (h + 1) * LANE)
                qh, kh, vh, doh = q[:, sl], k[:, sl], _bf(v[:, sl]), _bf(do[:, sl])
                p, qs, ks, qb, kb = _hg_scores(qh, kh, eq, ek, sl, gm, up, eye)
                st_f = st_ref[sub, h]
                sth = _bf(st_f)
                dsth = dst[h]
                dsth_b = _bf(dsth)
                qt = qh * ecum[:, sl]
                kt = kh * erem[:, sl]
                elast = _last_row(ecum[:, sl], rr)
                dp = _mm_nt(doh, vh)
                du_ref[r, 1024 + h * LANE:1024 + (h + 1) * LANE] = _bf(_mm_tn(_bf(p), doh) + _mm_nt(_bf(kt), dsth_b))
                dpe = _bf(dp * eye)
                dqt = _mm(doh, sth)
                dkt = _mm(vh, dsth_b)
                dq = dqt * ecum[:, sl] + _mm(dpe, _bf(kh))
                dk = dkt * erem[:, sl] + _mm_tn(dpe, _bf(qh))
                dex[sub, L2 * C:(L2 + 1) * C, sl] = dqt * qt
                dex[sub, (L2 + 1) * C:(L2 + 2) * C, sl] = dkt * kt
                for l in range(HG_LEVELS):
                    dpl = _bf(dp * gm[l])
                    dql = _mm(dpl, kb[l])
                    dkl = _mm_tn(dpl, qb[l])
                    dq = dq + dql * eq[l][:, sl]
                    dk = dk + dkl * ek[l][:, sl]
                    dex[sub, l * C:(l + 1) * C, sl] = dql * qs[l] + dkl * ks[l]
                dlast_s[sub, :, sl] = jnp.sum(dsth * st_f, axis=0, keepdims=True) * elast
                dst[h] = dsth * elast + _mm_tn(doh, _bf(qt))
                dq_s[sub, :, sl] = dq
                dk_s[sub, :, sl] = dk
            dq = dq_s[sub]
            dk = dk_s[sub]
            dlf = _sel_l2(mallt_ref[...], dex[sub]) + dlast_s[sub]
            du_ref[r, 0:512] = _bf(dq * _dsilu(hq))
            t = (1.0 - s) * (dlf / f - dk)
            du_ref[r, 512:1024] = _bf((1.0 - lb) * s * t)
            red_ref[1:2, :] += jnp.sum(t, axis=0, keepdims=True)

    rev = lambda i: (nb - 1 - i, 0)
    return pl.pallas_call(
        body, name="hg_bwd", grid=(nb,),
        in_specs=[pl.BlockSpec((rows, 2048), rev), _spec(lb), _spec(nw), _full(mall.shape), _full(mall_t.shape),
                  pl.BlockSpec((rows, HG_W), rev),
                  pl.BlockSpec((HG_SUB, HG_HEADS, LANE, LANE), lambda i: (nb - 1 - i, 0, 0, 0)),
                  pl.BlockSpec((rows, HG_W), lambda i: (nb - 1 - i, 1)), pl.BlockSpec(memory_space=pl.ANY)],
        out_specs=[pl.BlockSpec((rows, 2048), rev), pl.BlockSpec((8, HG_W), lambda i: (0, 0))],
        out_shape=[SDS((S, N_PAD), BF16), SDS((8, HG_W), F32)],
        scratch_shapes=[pltpu.VMEM((HG_HEADS, LANE, LANE), F32), pltpu.VMEM((HG_SUB, 1, HG_W), F32),
                        pltpu.VMEM((HG_SUB, C, HG_W), F32), pltpu.VMEM((HG_SUB, C, HG_W), F32),
                        pltpu.VMEM((HG_SUB, (L2 + 2) * C, HG_W), F32)],
        input_output_aliases={8: 0},
        compiler_params=_cp(("arbitrary",)),
    )(u, _arr(lb), _arr(nw), mall, mall_t, o_b, states, dycat, du)


def _ssdconv_fwd(l, u, cw, cb):
    S = u.shape[0]

    def body(u_ref, cw_ref, cb_ref, out_ref):
        rows = _iota((S, LANE), 0)
        out_ref[...] = _silu(_conv_fwd(_f(u_ref[...]), cw_ref, cb_ref, rows))

    return pl.pallas_call(
        body, name="ssdconv_fwd", grid=(SSD_CONV // LANE,),
        in_specs=[pl.BlockSpec((S, LANE), lambda t: (0, OFF_XBC // LANE + t)),
                  pl.BlockSpec((None, 4, LANE), lambda t: (l, 0, t)), pl.BlockSpec((None, 1, LANE), lambda t: (l, 0, t))],
        out_specs=pl.BlockSpec((S, LANE), lambda t: (0, t)),
        out_shape=SDS((S, SSD_CONV), F32),
        compiler_params=_cp(("parallel",)),
    )(u, cw, cb)


def _ssdconv_bwd(l, u, cw, cb, dxbc, du):
    S = u.shape[0]

    def body(u_ref, cw_ref, cb_ref, d_ref, du_in, du_ref, red_ref):
        del du_in
        rows = _iota((S, LANE), 0)
        x = _f(u_ref[...])
        dco = d_ref[...] * _dsilu(_conv_fwd(x, cw_ref, cb_ref, rows))
        dx, dws, dcb = _conv_bwd(x, dco, cw_ref, rows)
        du_ref[...] = _bf(dx)
        for n, p in enumerate(dws + [dcb]):
            red_ref[pl.ds(n, 1), :] = p
        red_ref[pl.ds(5, 3), :] = jnp.zeros((3, LANE), F32)

    ucol = pl.BlockSpec((S, LANE), lambda t: (0, OFF_XBC // LANE + t))
    return pl.pallas_call(
        body, name="ssdconv_bwd", grid=(SSD_CONV // LANE,),
        in_specs=[ucol, pl.BlockSpec((None, 4, LANE), lambda t: (l, 0, t)),
                  pl.BlockSpec((None, 1, LANE), lambda t: (l, 0, t)),
                  pl.BlockSpec((S, LANE), lambda t: (0, t)), pl.BlockSpec(memory_space=pl.ANY)],
        out_specs=[ucol, pl.BlockSpec((8, LANE), lambda t: (0, t))],
        out_shape=[SDS((S, N_PAD), BF16), SDS((8, SSD_CONV), F32)],
        input_output_aliases={4: 0},
        compiler_params=_cp(("parallel",)),
    )(u, cw, cb, dxbc, du)


def _ssd_consts():
    e64 = np.zeros((LANE, SSD_W), np.float32)
    for h in range(SSD_HEADS):
        e64[h, h * SSD_P:(h + 1) * SSD_P] = 1.0
    T = SSD_CHUNK
    tril = (np.arange(T)[None, :] <= np.arange(T)[:, None]).astype(np.float32)
    return e64, tril, tril.T.copy()


def _ssd_common(zdt, bias_ref, alog_ref, tril, e64, cum_ref, cumt_ref):
    T = SSD_CHUNK
    lane = _iota((1, LANE), 1)
    a_neg = jnp.where(lane < SSD_HEADS, -jnp.exp(alog_ref[...]), 0.0)
    dtpre = zdt[:, SSD_W:SSD_W + LANE] + bias_ref[...]
    dt = _softplus(dtpre)
    cum = _sel_l(tril, dt * a_neg)
    cum_ref[...] = cum
    cumt_ref[...] = cum.T
    cum_x = _sel_r(cum, e64)
    last_x = _last_row(cum_x, _iota((T, SSD_W), 0))
    ecum_x = jnp.exp(cum_x)
    erem_x = jnp.exp(last_x - cum_x)
    elast_x = jnp.exp(last_x)
    dt_x = _sel_r(dt, e64)
    return a_neg, dtpre, dt, ecum_x, erem_x, elast_x, dt_x


def _ssd_decay(cum_ref, cumt_ref, h, causal):
    T = SSD_CHUNK
    diff = jnp.broadcast_to(cum_ref[:, pl.ds(h, 1)], (T, T)) - cumt_ref[pl.ds(h, 1), :]
    return jnp.exp(jnp.where(causal, diff, NEG))


def _group_norm_fwd(y1, nwv):
    outs, invs = [], []
    for g in range(2):
        seg = y1[:, g * 512:(g + 1) * 512]
        inv = lax.rsqrt(jnp.mean(seg * seg, axis=-1, keepdims=True) + EPS)
        outs.append(seg * inv * nwv[:, g * 512:(g + 1) * 512])
        invs.append(inv)
    return outs, invs


def _ssd_fwd(u, xbc, bias, alog, dskip_x, nw, consts, ycat):
    S = u.shape[0]
    T = SSD_CHUNK
    n = S // T
    e64, tril, _ = consts

    def body(u_ref, xbc_ref, bias_ref, alog_ref, dx_ref, nw_ref, e64_ref, tril_ref, ycat_in,
             ycat_ref, y_ref, st_ref, st, cumt, cum_e):
        del ycat_in

        @pl.when(pl.program_id(0) == 0)
        def _():
            st[...] = jnp.zeros_like(st)

        zdt = _f(u_ref[...])
        z = zdt[:, 0:SSD_W]
        xs = xbc_ref[:, 0:SSD_W]
        _, _, _, ecum_x, erem_x, elast_x, dt_x = _ssd_common(
            zdt, bias_ref, alog_ref, tril_ref[...], e64_ref[...], cum_e, cumt)
        causal = _iota((T, T), 0) >= _iota((T, T), 1)
        lo = _iota((T, LANE), 1) < SSD_P
        xdt = xs * dt_x
        xrem = xdt * erem_x
        st_ref[...] = st[...]
        for g in range(2):
            gs = slice(g * 512, (g + 1) * 512)
            bg = _bf(xbc_ref[:, SSD_W + g * LANE:SSD_W + (g + 1) * LANE])
            cg = _bf(xbc_ref[:, SSD_W + 256 + g * LANE:SSD_W + 256 + (g + 1) * LANE])
            cb = _mm_nt(cg, bg)
            yin = _mm(cg, _bf(st[:, gs])) * ecum_x[:, gs]
            for j in range(4):
                h0 = 8 * g + 2 * j
                cs = slice(h0 * SSD_P, (h0 + 2) * SSD_P)
                xp = xdt[:, cs]
                s0 = _bf(cb * _ssd_decay(cum_e, cumt, h0, causal))
                s1 = _bf(cb * _ssd_decay(cum_e, cumt, h0 + 1, causal))
                y_ref[:, cs] = (_mm(s0, _bf(jnp.where(lo, xp, 0.0))) + _mm(s1, _bf(jnp.where(lo, 0.0, xp)))
                                + yin[:, j * LANE:(j + 1) * LANE])
            st[:, gs] = st[:, gs] * elast_x[:, gs] + _mm_tn(bg, _bf(xrem[:, gs]))
        y1 = (y_ref[...] + dx_ref[...] * xs) * _silu(z)
        outs, _ = _group_norm_fwd(y1, nw_ref[...])
        for g in range(2):
            ycat_ref[:, g * 512:(g + 1) * 512] = _bf(outs[g])

    return pl.pallas_call(
        body, name="ssd_fwd", grid=(n,),
        in_specs=[pl.BlockSpec((T, SSD_W + LANE), lambda i: (i, OFF_Z // (SSD_W + LANE))),
                  pl.BlockSpec((T, SSD_CONV), lambda i: (i, 0)), _spec(bias), _spec(alog), _spec(dskip_x), _spec(nw),
                  _full(e64.shape), _full(tril.shape), pl.BlockSpec(memory_space=pl.ANY)],
        out_specs=[pl.BlockSpec((T, SSD_W), lambda i: (i, 1)), pl.BlockSpec((T, SSD_W), lambda i: (i, 0)),
                   pl.BlockSpec((None, SSD_N, SSD_W), lambda i: (i, 0, 0))],
        out_shape=[SDS((S, D_INNER), BF16), SDS((S,SSD_W), F32), SDS((n, SSD_N, SSD_W), F32)],
        scratch_shapes=[pltpu.VMEM((SSD_N, SSD_W), F32), pltpu.VMEM((LANE, T), F32), pltpu.VMEM((T, LANE), F32)],
        input_output_aliases={8: 0},
        compiler_params=_cp(("arbitrary",)),
    )(u, xbc, _arr(bias), _arr(alog), _arr(dskip_x), _arr(nw), _bfc(e64), _bfc(tril), ycat)


def _ssd_bwd(u, xbc, bias, alog, dskip_x, nw, consts, y_ssd, states, dycat, du, tok):
    S = u.shape[0]
    T = SSD_CHUNK
    n = S // T
    e64, tril, triu = consts
    e64t = np.ascontiguousarray(e64.T)

    def body(u_ref, xbc_ref, bias_ref, alog_ref, dx_ref, nw_ref, e64_ref, e64t_ref, tril_ref, triu_ref,
             y_ref, st_ref, dy_ref, du_in, tok_ref, du_ref, dxbc_ref, red_ref, dst, dl_s, cumt, dxdt_s, dy0_s, gb_s,
             gc_s, cum_e, cs_s):
        del du_in, tok_ref

        @pl.when(pl.program_id(0) == 0)
        def _():
            dst[...] = jnp.zeros_like(dst)
            red_ref[...] = jnp.zeros_like(red_ref)
            cs_s[...] = jnp.zeros_like(cs_s)

        zdt = _f(u_ref[...])
        z = zdt[:, 0:SSD_W]
        xs = xbc_ref[:, 0:SSD_W]
        a_neg, dtpre, dt, ecum_x, erem_x, elast_x, dt_x = _ssd_common(
            zdt, bias_ref, alog_ref, tril_ref[...], e64_ref[...], cum_e, cumt)
        causal = _iota((T, T), 0) >= _iota((T, T), 1)
        lo = _iota((T, LANE), 1) < SSD_P
        xdt = xs * dt_x
        xrem = xdt * erem_x
        y = y_ref[...]
        dxv = dx_ref[...]
        nwv = nw_ref[...]
        sz = _silu(z)
        y0 = y + dxv * xs
        y1 = y0 * sz
        for g in range(2):
            gs = slice(g * 512, (g + 1) * 512)
            seg = y1[:, gs]
            inv = lax.rsqrt(jnp.mean(seg * seg, axis=-1, keepdims=True) + EPS)
            shat = seg * inv
            dyg = _f(dy_ref[:, gs])
            red_ref[0:1, gs] += jnp.sum(dyg * shat, axis=0, keepdims=True)
            dsh = dyg * nwv[:, gs]
            dy1g = inv * (dsh - shat * jnp.mean(dsh * shat, axis=-1, keepdims=True))
            du_ref[:, gs] = _bf(dy1g * y0[:, gs] * _dsilu(z[:, gs]))
            dy0_s[:, gs] = dy1g * sz[:, gs]
        dy0 = dy0_s[...]
        red_ref[1:2, :] += jnp.sum(dy0 * xs, axis=0, keepdims=True)
        dyin = dy0 * ecum_x
        lane = _iota((T, LANE), 1)
        dcum = jnp.zeros((T, LANE), F32)

        def decay_grad(h, gm):
            cs_s[pl.ds(h, 1), :] = jnp.sum(gm, axis=0, keepdims=True)
            return jnp.where(lane == h, jnp.sum(gm, axis=1, keepdims=True), 0.0)

        for g in range(2):
            gs = slice(g * 512, (g + 1) * 512)
            bg = _bf(xbc_ref[:, SSD_W + g * LANE:SSD_W + (g + 1) * LANE])
            cg = _bf(xbc_ref[:, SSD_W + 256 + g * LANE:SSD_W + 256 + (g + 1) * LANE])
            cb = _mm_nt(cg, bg)
            dst_f, st_f = dst[:, gs], st_ref[:, gs]
            dstg = _bf(dst_f)
            stg = _bf(st_f)
            dyin_g = _bf(dyin[:, gs])
            xrem_g = _bf(xrem[:, gs])
            dcb = jnp.zeros((T, T), F32)
            dxr = _mm(bg, dstg)
            dxdt_s[:, gs] = dxr * erem_x[:, gs]
            gc_s[:, gs] = dxr * xrem[:, gs]
            gb_s[:, gs] = dyin[:, gs] * _mm(cg, stg)
            dl_s[:, gs] = jnp.sum(dst_f * st_f, axis=0, keepdims=True) * elast_x[:, gs]
            for j in range(4):
                h0 = 8 * g + 2 * j
                cs = slice(h0 * SSD_P, (h0 + 2) * SSD_P)
                xp = xdt[:, cs]
                dyp = dy0[:, cs]
                x_lo, x_hi = _bf(jnp.where(lo, xp, 0.0)), _bf(jnp.where(lo, 0.0, xp))
                d_lo, d_hi = _bf(jnp.where(lo, dyp, 0.0)), _bf(jnp.where(lo, 0.0, dyp))
                l0 = _ssd_decay(cum_e, cumt, h0, causal)
                l1 = _ssd_decay(cum_e, cumt, h0 + 1, causal)
                s0 = cb * l0
                s1 = cb * l1
                ds0 = _mm_nt(d_lo, x_lo)
                ds1 = _mm_nt(d_hi, x_hi)
                dcb = dcb + ds0 * l0 + ds1 * l1
                dxdt_s[:, cs] += _mm_tn(_bf(s0), d_lo) + _mm_tn(_bf(s1), d_hi)
                dcum = dcum + decay_grad(h0, ds0 * s0) + decay_grad(h0 + 1, ds1 * s1)
            dcb_b = _bf(dcb)
            dxbc_ref[:, SSD_W + g * LANE:SSD_W + (g + 1) * LANE] = _mm_tn(dcb_b, cg) + _mm_nt(xrem_g, dstg)
            dxbc_ref[:, SSD_W + 256 + g * LANE:SSD_W + 256 + (g + 1) * LANE] = _mm(dcb_b, bg) + _mm_nt(dyin_g, stg)
            dst[:, gs] = dst_f * elast_x[:, gs] + _mm_tn(cg, dyin_g)
        dxdt = dxdt_s[...]
        dxbc_ref[:, 0:SSD_W] = dxdt * dt_x + dy0 * dxv
        e64t = e64t_ref[...]
        gc = gc_s[...]
        dlast_x = jnp.sum(gc, axis=0, keepdims=True) + dl_s[...]
        dlast = jnp.max(_sel_r(jnp.broadcast_to(dlast_x, (8, SSD_W)), e64t), axis=0, keepdims=True)
        dcum = (dcum - cs_s[...].T + _sel_r(gb_s[...] - gc, e64t)
                + jnp.where(_iota((T, LANE), 0) == T - 1, dlast, 0.0))
        dda = _sel_l(triu_ref[...], dcum)
        ddt = dda * a_neg + _sel_r(dxdt * xs, e64t)
        ddtpre = ddt * _sigmoid(dtpre)
        du_ref[:, SSD_W:SSD_W + LANE] = _bf(jnp.where(lane < SSD_HEADS, ddtpre, 0.0))
        red_ref[2:3, 0:LANE] += jnp.sum(ddtpre, axis=0, keepdims=True)
        red_ref[3:4, 0:LANE] += jnp.sum(dda * dt, axis=0, keepdims=True)

    rev = lambda i: (n - 1 - i, 0)
    return pl.pallas_call(
        body, name="ssd_bwd", grid=(n,),
        in_specs=[pl.BlockSpec((T, SSD_W + LANE), lambda i: (n - 1 - i, OFF_Z // (SSD_W + LANE))),
                  pl.BlockSpec((T, SSD_CONV), rev), _spec(bias), _spec(alog), _spec(dskip_x), _spec(nw),
                  _full(e64.shape), _full(e64t.shape), _full(tril.shape), _full(triu.shape),
                  pl.BlockSpec((T, SSD_W), rev), pl.BlockSpec((None, SSD_N, SSD_W), lambda i: (n - 1 - i, 0, 0)),
                  pl.BlockSpec((T, SSD_W), lambda i: (n - 1 - i, 1)), pl.BlockSpec(memory_space=pl.ANY),
                  pl.BlockSpec(memory_space=pl.ANY)],
        out_specs=[pl.BlockSpec((T, SSD_W + LANE), lambda i: (n - 1 - i, OFF_Z // (SSD_W + LANE))),
                   pl.BlockSpec((T, SSD_CONV), rev), pl.BlockSpec((8, SSD_W), lambda i: (0, 0))],
        out_shape=[SDS((S, N_PAD), BF16), SDS((S, SSD_CONV), F32), SDS((8, SSD_W), F32)],
        scratch_shapes=[pltpu.VMEM((SSD_N, SSD_W), F32), pltpu.VMEM((1, SSD_W), F32), pltpu.VMEM((LANE, T), F32)]
        + [pltpu.VMEM((T, SSD_W), F32)] * 4 + [pltpu.VMEM((T, LANE), F32), pltpu.VMEM((LANE, T), F32)],
        input_output_aliases={13: 0},
        compiler_params=_cp(("arbitrary",)),
    )(u, xbc, _arr(bias), _arr(alog), _arr(dskip_x), _arr(nw), _bfc(e64), _bfc(e64t), _bfc(tril), _bfc(triu), y_ssd,
      states, dycat, du, tok)


def _bfc(a):
    return jnp.asarray(a, BF16)


def _outproj_fwd(ycat, wo, x, gate, tok):
    S = x.shape[0]
    tm = min(512, S)

    def body(yc_ref, wo_ref, x_ref, g_ref, tok_ref, xn_ref, y_ref):
        del tok_ref
        y = _mm(_bf(yc_ref[...]), wo_ref[...])
        y_ref[...] = y
        xn_ref[...] = x_ref[...] + g_ref[...] * y

    row = pl.BlockSpec((tm, D_MODEL), lambda i: (i, 0))
    return pl.pallas_call(
        body, name="outproj_fwd", grid=(S // tm,),
        in_specs=[pl.BlockSpec((tm, D_INNER), lambda i: (i, 0)), _full((D_INNER, D_MODEL)), row, _spec(gate),
                  pl.BlockSpec(memory_space=pl.ANY)],
        out_specs=[row, row],
        out_shape=[SDS((S, D_MODEL), F32), SDS((S, D_MODEL), F32)],
        compiler_params=_cp(("parallel",)),
    )(ycat, wo, x, _arr(gate), tok)


def _outproj_bwd(dxn, y, gate, ycat, wo):
    S = dxn.shape[0]
    tm = min(512, S)

    def body(dx_ref, y_ref, g_ref, yc_ref, wo_ref, dyc_ref, gwo_ref, dg_ref, acc):
        @pl.when(pl.program_id(0) == 0)
        def _():
            acc[...] = jnp.zeros_like(acc)
            dg_ref[...] = jnp.zeros_like(dg_ref)

        dxv = dx_ref[...]
        dy = _bf(dxv * g_ref[...])
        dg_ref[0:1, :] += jnp.sum(dxv * y_ref[...], axis=0, keepdims=True)
        dyc_ref[...] = _mm_nt(dy, wo_ref[...])
        acc[...] += _mm_tn(_bf(yc_ref[...]), dy)

        @pl.when(pl.program_id(0) == pl.num_programs(0) - 1)
        def _():
            gwo_ref[...] = acc[...].astype(BF16)

    row = pl.BlockSpec((tm, D_MODEL), lambda i: (i, 0))
    wide = pl.BlockSpec((tm, D_INNER), lambda i: (i, 0))
    return pl.pallas_call(
        body, name="outproj_bwd", grid=(S // tm,),
        in_specs=[row, row, _spec(gate), wide, _full((D_INNER, D_MODEL))],
        out_specs=[wide, _full((D_INNER, D_MODEL)), _full((8, D_MODEL))],
        out_shape=[SDS((S, D_INNER), F32), SDS((D_INNER, D_MODEL), BF16), SDS((8, D_MODEL), F32)],
        scratch_shapes=[pltpu.VMEM((D_INNER, D_MODEL), F32)],
        compiler_params=_cp(("arbitrary",)),
    )(dxn, y, _arr(gate), ycat, wo)


def _loss_head(x, fw, target):
    S = x.shape[0]
    tm = min(512, S)

    def body(x_ref, fw_ref, t_ref, dx_ref, red_ref):
        @pl.when(pl.program_id(0) == 0)
        def _():
            red_ref[...] = jnp.zeros_like(red_ref)

        xv = x_ref[...]
        fwv = fw_ref[...]
        inv = lax.rsqrt(jnp.mean(xv * xv, axis=-1, keepdims=True) + EPS)
        xhat = xv * inv
        err = xhat * fwv - t_ref[...]
        col = jnp.sum(err * err, axis=0, keepdims=True)
        red_ref[1:2, :] += jnp.broadcast_to(jnp.sum(col, axis=1, keepdims=True) * (0.5 / D_MODEL), (1, D_MODEL))
        dy = err * (1.0 / D_MODEL)
        red_ref[0:1, :] += jnp.sum(dy * xhat, axis=0, keepdims=True)
        dxhat = dy * fwv
        dx_ref[...] = inv * (dxhat - xhat * jnp.mean(dxhat * xhat, axis=-1, keepdims=True))

    row = pl.BlockSpec((tm, D_MODEL), lambda i: (i, 0))
    return pl.pallas_call(
        body, name="loss_head", grid=(S // tm,),
        in_specs=[row, _vec(D_MODEL), row],
        out_specs=[row, _full((8, D_MODEL))],
        out_shape=[SDS((S, D_MODEL), F32), SDS((8, D_MODEL), F32)],
        compiler_params=_cp(("arbitrary",)),
    )(x, fw, target)


ADA_COLS = 3 * D_MODEL // N_DEV


def _ada_fwd(c_all, w_ada, b_cols):
    def body(c_ref, w_ref, b_ref, out_ref):
        out_ref[...] = _mm(_bf(_silu(c_ref[...])), _bf(w_ref[...])) + b_ref[...]

    return pl.pallas_call(
        body, name="ada_fwd", grid=(DEPTH,),
        in_specs=[_full((N_DEV, D_MODEL)), pl.BlockSpec((None, D_MODEL, ADA_COLS), lambda l: (l, 0, 0)),
                  pl.BlockSpec((None, 1, ADA_COLS), lambda l: (l, 0, 0))],
        out_specs=pl.BlockSpec((None, N_DEV, ADA_COLS), lambda l: (l, 0, 0)),
        out_shape=SDS((DEPTH, N_DEV, ADA_COLS), F32),
        compiler_params=_cp(("parallel",)),
    )(c_all, w_ada, b_cols)


def _ada_bwd(ct_pad, dmod_pad):
    def body(c_ref, d_ref, out_ref):
        out_ref[...] = _mm(_bf(_silu(c_ref[...])), _bf(d_ref[...]))

    return pl.pallas_call(
        body, name="ada_bwd", grid=(DEPTH,),
        in_specs=[_full((D_MODEL, LANE)), pl.BlockSpec((None, LANE, ADA_COLS), lambda l: (l, 0, 0))],
        out_specs=pl.BlockSpec((None, D_MODEL, ADA_COLS), lambda l: (l, 0, 0)),
        out_shape=SDS((DEPTH, D_MODEL, ADA_COLS), F32),
        compiler_params=_cp(("parallel",)),
    )(ct_pad, dmod_pad)


def _adamw(parts, w, m, v, name, own=None, layers=None, prev=None):
    n, L, R, C = parts.shape
    lo, hi = layers or (0, L)
    tr = R
    while tr * C * 4 > (1 << 20) and tr % 16 == 0:
        tr //= 2
    first = 1 if own is None else 2

    def body(*refs):
        p_ref = refs[0]
        w_ref, m_ref, v_ref = refs[first:first + 3]
        g_ref, d_ref, mo_ref, vo_ref = refs[-4:]

        def part(k):
            if own is None:
                return p_ref[k].astype(F32)
            me = 4 * lax.axis_index("x") + 2 * lax.axis_index("y") + lax.axis_index("c")
            return jnp.where(me == k, refs[1][...], p_ref[k]).astype(F32)

        g = part(0)
        for k in range(1, n):
            g = g + part(k)
        mn = ADAM_B1 * m_ref[...] + (1.0 - ADAM_B1) * g
        vn = ADAM_B2 * v_ref[...] + (1.0 - ADAM_B2) * (g * g)
        m_hat = mn / (1.0 - ADAM_B1 ** ADAM_STEP)
        v_hat = vn / (1.0 - ADAM_B2 ** ADAM_STEP)
        g_ref[...] = g
        d_ref[...] = -ADAM_LR * (m_hat / (jnp.sqrt(v_hat) + ADAM_EPS) + ADAM_WD * w_ref[...])
        mo_ref[...] = mn
        vo_ref[...] = vn

    blk = pl.BlockSpec((None, tr, C), lambda l, i: (lo + l, i, 0))
    own_blk = [] if own is None else [pl.BlockSpec((None, tr, C), lambda l, i: (l, i, 0))]
    n_blk = 3 if own is None else 4
    return pl.pallas_call(
        body, name=name, grid=(hi - lo, R // tr),
        in_specs=[pl.BlockSpec((n, None, tr, C), lambda l, i: (0, lo + l, i, 0))] + own_blk + [blk] * 3
        + ([] if prev is None else [ANY] * 4),
        out_specs=[blk] * 4,
        out_shape=[SDS((L, R, C), F32)] * 4,
        input_output_aliases={} if prev is None else {1 + n_blk + k: k for k in range(4)},
        compiler_params=_cp(("parallel", "parallel")),
    )(parts, *([] if own is None else [own]), w, m, v, *([] if prev is None else prev))


MESH = pl.DeviceIdType.MESH
ANY = pl.BlockSpec(memory_space=pl.ANY)


def _all_gather(v, name):
    def body(v_ref, out_ref, send_sems, recv_sems, local_sem):
        x, y, c = lax.axis_index("x"), lax.axis_index("y"), lax.axis_index("c")
        me, sibling = (x, y, c), (x, y, 1 - c)
        chips = [(1 - x, y), (x, 1 - y), (1 - x, 1 - y)]

        def slot(px, py, pc):
            return out_ref.at[4 * px + 2 * py + pc]

        def copy(k, block, to, src=None):
            return pltpu.make_async_remote_copy(
                src_ref=slot(*block) if src is None else src, dst_ref=slot(*block),
                send_sem=send_sems.at[k], recv_sem=recv_sems.at[k], device_id=to, device_id_type=MESH)

        mine = pltpu.make_async_copy(v_ref, slot(*me), local_sem)
        mine.start()
        first = [copy(0, me, sibling, src=v_ref)]
        first += [copy(1 + j, me, (*chip, c), src=v_ref) for j, chip in enumerate(chips)]
        for cp in first:
            cp.start()
        passed = [copy(4 + j, (*chip, c), sibling) for j, chip in enumerate(chips)]
        for j, chip in enumerate(chips):
            copy(1 + j, (*chip, c), me).wait_recv()
            passed[j].start()
        copy(0, sibling, me).wait_recv()
        for j, chip in enumerate(chips):
            copy(4 + j, (*chip, 1 - c), me).wait_recv()
        for cp in first + passed:
            cp.wait_send()
        mine.wait()

    return pl.pallas_call(
        body, name=name, in_specs=[ANY], out_specs=ANY,
        out_shape=SDS((N_DEV,) + v.shape, v.dtype),
        scratch_shapes=[pltpu.SemaphoreType.DMA((7,)), pltpu.SemaphoreType.DMA((7,)), pltpu.SemaphoreType.DMA],
    )(v)


def _all_to_all(v, name):
    def body(v_ref, out_ref, send_sems, recv_sems, local_sem):
        x, y, c = lax.axis_index("x"), lax.axis_index("y"), lax.axis_index("c")
        mine_idx = 4 * x + 2 * y + c
        mine = pltpu.make_async_copy(v_ref.at[mine_idx], out_ref.at[mine_idx], local_sem)
        mine.start()
        sends, recvs = [], []
        for k in range(1, N_DEV):
            px = 1 - x if k & 4 else x
            py = 1 - y if k & 2 else y
            pc = 1 - c if k & 1 else c
            peer_idx = 4 * px + 2 * py + pc
            sems = dict(send_sem=send_sems.at[k - 1], recv_sem=recv_sems.at[k - 1], device_id=(px, py, pc),
                        device_id_type=MESH)
            sends.append(pltpu.make_async_remote_copy(src_ref=v_ref.at[peer_idx], dst_ref=out_ref.at[mine_idx], **sems))
            recvs.append(pltpu.make_async_remote_copy(src_ref=v_ref.at[peer_idx], dst_ref=out_ref.at[peer_idx], **sems))
        for cp in sends:
            cp.start()
        for cp in recvs:
            cp.wait_recv()
        for cp in sends:
            cp.wait_send()
        mine.wait()

    return pl.pallas_call(
        body, name=name, in_specs=[ANY], out_specs=ANY,
        out_shape=SDS(v.shape, v.dtype),
        scratch_shapes=[pltpu.SemaphoreType.DMA((7,)), pltpu.SemaphoreType.DMA((7,)), pltpu.SemaphoreType.DMA],
    )(v)


HBM_SPEC = pl.BlockSpec(memory_space=pltpu.HBM)
SEM_SPEC = pl.BlockSpec(memory_space=pltpu.SEMAPHORE)
EFFECT = pltpu.SideEffectType.DATAFLOW_SIDE_EFFECTING


EXCHANGE_PEERS = {"gather": range(1, N_DEV), "scatter": range(1, N_DEV), "chip": (1, 2, 4, 6), "pass": (2, 4, 6)}


def _exchange_copies(srcs, lands, send_sems, recv_sems, mode, layer):
    x, y, c = lax.axis_index("x"), lax.axis_index("y"), lax.axis_index("c")
    me = 4 * x + 2 * y + c
    copies = []
    for a, (src, land) in enumerate(zip(srcs, lands)):
        for k in EXCHANGE_PEERS[mode]:
            px = 1 - x if k & 4 else x
            py = 1 - y if k & 2 else y
            pc = 1 - c if k & 1 else c
            peer = 4 * px + 2 * py + pc
            if mode == "scatter":
                s, d, to = src.at[peer], land.at[me, layer], (px, py, pc)
            elif mode == "pass":
                s, d, to = land.at[peer], land.at[peer], (x, y, 1 - c)
            else:
                s, d, to = src, land.at[me], (px, py, pc)
            n = 7 * a + k - 1
            copies.append(pltpu.make_async_remote_copy(
                src_ref=s, dst_ref=d, send_sem=send_sems.at[n], recv_sem=recv_sems.at[n], device_id=to,
                device_id_type=MESH))
    return copies


def _exchange_start(name, srcs, lands, mode, layer=0, after=None):
    n = len(srcs)

    def body(*refs):
        send_sems, recv_sems = refs[-2 * n - 3], refs[-2 * n - 2]
        for cp in _exchange_copies(refs[:n], refs[n:2 * n], send_sems, recv_sems, mode, layer):
            cp.start()
        refs[-1][...] = jnp.zeros_like(refs[-1])

    arrays = list(srcs) + list(lands)
    sems = pltpu.SemaphoreType.DMA((7 * n,))
    out = pl.pallas_call(
        body, name=name,
        out_shape=(sems, sems, *[pltpu.HBM(v.shape, v.dtype) for v in arrays], SDS((8, LANE), F32)),
        in_specs=[HBM_SPEC] * (2 * n) + ([ANY] if after is not None else []),
        out_specs=(SEM_SPEC, SEM_SPEC, *[HBM_SPEC] * (2 * n), pl.BlockSpec(memory_space=pltpu.VMEM)),
        input_output_aliases={i: 2 + i for i in range(2 * n)},
        compiler_params=pltpu.CompilerParams(has_side_effects=EFFECT),
    )(*[pltpu.with_memory_space_constraint(v, pltpu.HBM) for v in arrays], *([after] if after is not None else []))
    return dict(sems=out[:2], srcs=out[2:2 + n], lands=out[2 + n:2 + 2 * n], token=out[-1], mode=mode,
                layer=layer)


def _exchange_wait(name, st, after, also=()):
    n = len(st["srcs"])

    def body(*refs):
        send_sems, recv_sems = refs[2 * n], refs[2 * n + 1]
        for cp in _exchange_copies(refs[:n], refs[n:2 * n], send_sems, recv_sems, st["mode"], st["layer"]):
            cp.wait_send()
            cp.wait_recv()

    arrays = list(st["srcs"]) + list(st["lands"])
    out = pl.pallas_call(
        body, name=name,
        out_shape=tuple(pltpu.HBM(v.shape, v.dtype) for v in arrays),
        in_specs=[HBM_SPEC] * (2 * n) + [SEM_SPEC, SEM_SPEC] + [ANY] * (1 + len(also)),
        out_specs=tuple([HBM_SPEC] * (2 * n)),
        input_output_aliases={i: i for i in range(2 * n)},
        compiler_params=pltpu.CompilerParams(has_side_effects=EFFECT),
    )(*arrays, *st["sems"], after, *also)
    st["srcs"] = out[:n]
    return out[n:]


_IN_PIECES = ([(1024, 3072)]
              + [r for t in range(4) for r in ((LANE * t, LANE * (t + 1)), (512 + LANE * t, 512 + LANE * (t + 1)))]
              + [(4096, 5632), (3072, 4096), (5632, 5648)])


def _permute_in(w):
    pad = jnp.zeros(w.shape[:-1] + (N_PAD - N_IN,), w.dtype)
    return jnp.concatenate([w[..., a:b] for a, b in _IN_PIECES] + [pad], axis=-1)


def _unpermute_in(g):
    ax = [g[..., OFF_LRU + 2 * LANE * t:OFF_LRU + 2 * LANE * t + LANE] for t in range(4)]
    ag = [g[..., OFF_LRU + 2 * LANE * t + LANE:OFF_LRU + 2 * LANE * (t + 1)] for t in range(4)]
    return jnp.concatenate(ax + ag + [g[..., 0:2048], g[..., OFF_Z:OFF_Z + SSD_W], g[..., OFF_XBC:OFF_XBC + SSD_CONV],
                                      g[..., OFF_Z + SSD_W:OFF_Z + SSD_W + SSD_HEADS]], axis=-1)


SHARD_COLS = N_IN // N_DEV


def _in_segments():
    segs, pos = [], 0
    for a, b in _IN_PIECES:
        for i in range(N_DEV):
            lo, hi = max(a, SHARD_COLS * i), min(b, SHARD_COLS * (i + 1))
            if lo < hi:
                segs.append((i, lo - SHARD_COLS * i, hi - lo, pos + lo - a))
        pos += b - a
    return segs


RELAYOUT_ROWS = 256


def _relayout_in(land, own):
    def body(land_ref, own_ref, out_ref):
        me = 4 * lax.axis_index("x") + 2 * lax.axis_index("y") + lax.axis_index("c")
        out_ref[:, N_IN:N_PAD] = jnp.zeros((RELAYOUT_ROWS, N_PAD - N_IN), BF16)
        for i, j, wd, p in _in_segments():
            out_ref[:, p:p + wd] = jnp.where(me == i, own_ref[:, j:j + wd], land_ref[i, :, j:j + wd])

    return pl.pallas_call(
        body, name="relayout_in", grid=(D_MODEL // RELAYOUT_ROWS,),
        in_specs=[pl.BlockSpec((N_DEV, RELAYOUT_ROWS, SHARD_COLS), lambda r: (0, r, 0)),
                  pl.BlockSpec((RELAYOUT_ROWS, SHARD_COLS), lambda r: (r, 0))],
        out_specs=pl.BlockSpec((RELAYOUT_ROWS, N_PAD), lambda r: (r, 0)),
        out_shape=SDS((D_MODEL, N_PAD), BF16),
        compiler_params=_cp(("parallel",)),
    )(land, own)


def _relayout_grad(g):
    def body(g_ref, out_ref):
        for i, j, wd, p in _in_segments():
            out_ref[i, :, j:j + wd] = g_ref[:, p:p + wd].astype(BF16)

    return pl.pallas_call(
        body, name="relayout_grad", grid=(D_MODEL // RELAYOUT_ROWS,),
        in_specs=[pl.BlockSpec((RELAYOUT_ROWS, N_PAD), lambda r: (r, 0))],
        out_specs=pl.BlockSpec((N_DEV, RELAYOUT_ROWS, SHARD_COLS), lambda r: (0, r, 0)),
        out_shape=SDS((N_DEV, D_MODEL, SHARD_COLS), BF16),
        compiler_params=_cp(("parallel",)),
    )(g)


def _block_diag(w):
    w4 = w.reshape(DEPTH, 4, 2, 64, 64)
    z = jnp.zeros((DEPTH, 4, 64, 64), w.dtype)
    top = jnp.concatenate([w4[:, :, 0], z], axis=-1)
    bot = jnp.concatenate([z, w4[:, :, 1]], axis=-1)
    return jnp.concatenate([top, bot], axis=2).astype(BF16)


def _diag_blocks(g):
    return jnp.stack([g[:, :, :64, :64], g[:, :, 64:, 64:]], axis=2).reshape(DEPTH, 8, 64, 64)


def _pad_lanes(v):
    return jnp.pad(v, ((0, 0), (0, LANE - v.shape[1])))


def _lower_bounds(logits):
    p = jax.nn.softmax(logits, axis=0)
    return p, jnp.cumsum(p, axis=0) - p[0]


def _lower_bounds_bwd(p, dlb):
    dp = jnp.cumsum(dlb[::-1], axis=0)[::-1]
    dp = dp.at[0].add(-jnp.sum(dlb, axis=0))
    return p * (dp - jnp.sum(dp * p, axis=0, keepdims=True))


SMALL = ["norm_w", "b_ada", "lru_conv_b", "lru_wa", "lru_ba", "lru_wx", "lru_bx", "lru_lambda", "hg_lb_logits",
         "hg_norm_w", "ssd_conv_b", "ssd_dt_bias", "ssd_a_log", "ssd_d", "ssd_norm_w", "final_norm_w"]
WEIGHTS = ["norm_w", "w_ada", "b_ada", "w_in", "lru_conv_w", "lru_conv_b", "lru_wa", "lru_ba", "lru_wx", "lru_bx",
           "lru_lambda", "hg_lb_logits", "hg_norm_w", "ssd_conv_w", "ssd_conv_b", "ssd_dt_bias", "ssd_a_log", "ssd_d",
           "ssd_norm_w", "w_out", "final_norm_w"]
INPUTS = ["x", "c"] + WEIGHTS + ["loss_target"] + ["m_" + n for n in WEIGHTS] + ["v_" + n for n in WEIGHTS]
SMALL_ROW = 1024


def _small_rows(like):
    out, off = {}, 0
    for n in SMALL:
        rows = -(-int(np.prod(like[n].shape)) // (8 * SMALL_ROW)) * 8
        out[n] = (off, rows)
        off += rows
    return out, off


def _flatten_small(d, prefix="", last=0.0):
    table, _ = _small_rows({n: d[prefix + n] for n in SMALL})
    pieces = []
    for n in SMALL:
        flat = d[prefix + n].reshape(-1)
        pieces.append(jnp.pad(flat, (0, table[n][1] * SMALL_ROW - flat.shape[0])).reshape(-1, SMALL_ROW))
    return jnp.concatenate(pieces + [jnp.full((8, SMALL_ROW), last, F32)], axis=0)


def _split_small(packed, like):
    table, _ = _small_rows(like)
    out = {}
    for n in SMALL:
        off, rows = table[n]
        size = int(np.prod(like[n].shape))
        out[n] = packed[off:off + rows].reshape(-1)[:size].reshape(like[n].shape)
    return out


def _local_step(x, mod, target, w, fetch, emit):
    S = x.shape[0]
    mall = _bfc(_hg_consts())
    mall_t = _bfc(_hg_consts().T)
    consts = _ssd_consts()
    p_lb, lbs = _lower_bounds(w["hg_lb_logits"])
    no_tok = jnp.zeros((8, LANE), F32)
    wa, wx = _block_diag(w["lru_wa"]), _block_diag(w["lru_wx"])
    ba, bx = w["lru_ba"].reshape(DEPTH, 1, LRU_W), w["lru_bx"].reshape(DEPTH, 1, LRU_W)
    lru_cb, lam, ssd_cb = w["lru_conv_b"][:, None], w["lru_lambda"][:, None], w["ssd_conv_b"][:, None]
    bias, alog = _pad_lanes(w["ssd_dt_bias"]), _pad_lanes(w["ssd_a_log"])
    dskip = jnp.repeat(w["ssd_d"], SSD_P, axis=1)
    saved = []
    for l in range(DEPTH):
        w_in_l, w_out_l, token = fetch(l, x)
        shift, scale, gate = (_Row(mod, l, D_MODEL, k) for k in range(3))
        nw = _Row(w["norm_w"], l)
        u, h = _inproj_fwd(x, nw, scale, shift, w_in_l, no_tok if token is None else token)
        ycat = lax.empty((S, D_INNER), BF16)
        lru_args = (l, u, w["lru_conv_w"], lru_cb, wa, ba, wx, bx, lam)
        ycat, h_lru = _lru_fwd(*lru_args, ycat)
        hg_args = (u, _Row(lbs, l), _Row(w["hg_norm_w"], l), mall)
        ycat, o_b, hg_st = _hg_fwd(*hg_args, ycat)
        xbc = _ssdconv_fwd(l, u, w["ssd_conv_w"], ssd_cb)
        ssd_args = (u, xbc, _Row(bias, l), _Row(alog, l), _Row(dskip, l), _Row(w["ssd_norm_w"], l), consts)
        ycat, y_ssd, ssd_st = _ssd_fwd(*ssd_args, ycat)
        token = fetch(l, y_ssd, late=True)
        x_new, y = _outproj_fwd(ycat, w_out_l, x, gate, no_tok if token is None else token)
        saved.append((x, u, h, ycat, nw, scale, gate, w_in_l, w_out_l, lru_args, h_lru, hg_args, o_b, hg_st, ssd_args,
                      y_ssd, ssd_st, y))
        x = x_new
    dx, red = _loss_head(x, w["final_norm_w"][None, :], target)
    loss = red[1, 0]
    reds = {k: [None] * DEPTH for k in ("in", "gate", "lru", "wa", "wx", "hg", "conv", "ssd")}
    for l in reversed(range(DEPTH)):
        (x, u, h, ycat, nw, scale, gate, w_in_l, w_out_l, lru_args, h_lru, hg_args, o_b, hg_st, ssd_args, y_ssd, ssd_st,
         y) = saved[l]
        dycat, g_out, reds["gate"][l] = _outproj_bwd(dx, y, gate, ycat, w_out_l)
        token = emit(l, "w_out", g_out)
        du = lax.empty((S, N_PAD), BF16)
        du, dxbc, reds["ssd"][l] = _ssd_bwd(*ssd_args, y_ssd, ssd_st, dycat, du, no_tok if token is None else token)
        du, reds["conv"][l] = _ssdconv_bwd(l, u, w["ssd_conv_w"], ssd_cb, dxbc, du)
        du, reds["hg"][l] = _hg_bwd(*hg_args, mall_t, o_b, hg_st, dycat, du)
        du, reds["lru"][l], reds["wa"][l], reds["wx"][l] = _lru_bwd(*lru_args, h_lru, dycat, du)
        token = emit(l, "w_in", _inproj_bwd_w(h, du))
        dx, reds["in"][l] = _inproj_bwd_x(du, w_in_l, x, nw, scale, dx, no_tok if token is None else token)
    r = {k: jnp.stack(v) for k, v in reds.items()}
    g = {n: None for n in WEIGHTS}
    g["final_norm_w"] = red[0]
    g["norm_w"] = r["in"][:, 2]
    dmod = jnp.concatenate([r["in"][:, 0], r["in"][:, 1], r["gate"][:, 0]], axis=1)
    g["lru_conv_w"], g["lru_conv_b"] = r["lru"][:, 0:4], r["lru"][:, 4]
    g["lru_ba"], g["lru_bx"] = r["lru"][:, 5].reshape(DEPTH, 8, 64), r["lru"][:, 6].reshape(DEPTH, 8, 64)
    g["lru_lambda"] = r["lru"][:, 7]
    g["lru_wa"], g["lru_wx"] = _diag_blocks(r["wa"]), _diag_blocks(r["wx"])
    g["hg_norm_w"] = r["hg"][:, 0]
    g["hg_lb_logits"] = _lower_bounds_bwd(p_lb, r["hg"][:, 1])
    g["ssd_conv_w"], g["ssd_conv_b"] = r["conv"][:, 0:4], r["conv"][:, 4]
    g["ssd_norm_w"] = r["ssd"][:, 0]
    g["ssd_d"] = r["ssd"][:, 1].reshape(DEPTH, SSD_HEADS, SSD_P).sum(-1)
    g["ssd_dt_bias"] = r["ssd"][:, 2, :SSD_HEADS]
    g["ssd_a_log"] = -r["ssd"][:, 3, :SSD_HEADS] * jnp.exp(w["ssd_a_log"])
    return loss, dx, dmod, g


def kernel(x, c, norm_w, w_ada, b_ada, w_in, lru_conv_w, lru_conv_b, lru_wa, lru_ba, lru_wx, lru_bx, lru_lambda, hg_lb_logits, hg_norm_w, ssd_conv_w, ssd_conv_b, ssd_dt_bias, ssd_a_log, ssd_d, ssd_norm_w, w_out, final_norm_w, loss_target, m_norm_w, m_w_ada, m_b_ada, m_w_in, m_lru_conv_w, m_lru_conv_b, m_lru_wa, m_lru_ba, m_lru_wx, m_lru_bx, m_lru_lambda, m_hg_lb_logits, m_hg_norm_w, m_ssd_conv_w, m_ssd_conv_b, m_ssd_dt_bias, m_ssd_a_log, m_ssd_d, m_ssd_norm_w, m_w_out, m_final_norm_w, v_norm_w, v_w_ada, v_b_ada, v_w_in, v_lru_conv_w, v_lru_conv_b, v_lru_wa, v_lru_ba, v_lru_wx, v_lru_bx, v_lru_lambda, v_hg_lb_logits, v_hg_norm_w, v_ssd_conv_w, v_ssd_conv_b, v_ssd_dt_bias, v_ssd_a_log, v_ssd_d, v_ssd_norm_w, v_w_out, v_final_norm_w):
    return _step(x, c, norm_w, w_ada, b_ada, w_in, lru_conv_w, lru_conv_b, lru_wa, lru_ba, lru_wx, lru_bx, lru_lambda, hg_lb_logits, hg_norm_w, ssd_conv_w, ssd_conv_b, ssd_dt_bias, ssd_a_log, ssd_d, ssd_norm_w, w_out, final_norm_w, loss_target, m_norm_w, m_w_ada, m_b_ada, m_w_in, m_lru_conv_w, m_lru_conv_b, m_lru_wa, m_lru_ba, m_lru_wx, m_lru_bx, m_lru_lambda, m_hg_lb_logits, m_hg_norm_w, m_ssd_conv_w, m_ssd_conv_b, m_ssd_dt_bias, m_ssd_a_log, m_ssd_d, m_ssd_norm_w, m_w_out, m_final_norm_w, v_norm_w, v_w_ada, v_b_ada, v_w_in, v_lru_conv_w, v_lru_conv_b, v_lru_wa, v_lru_ba, v_lru_wx, v_lru_bx, v_lru_lambda, v_hg_lb_logits, v_hg_norm_w, v_ssd_conv_w, v_ssd_conv_b, v_ssd_dt_bias, v_ssd_a_log, v_ssd_d, v_ssd_norm_w, v_w_out, v_final_norm_w)


def _step(*args):
    a = dict(zip(INPUTS, args, strict=True))
    me = 4 * lax.axis_index("x") + 2 * lax.axis_index("y") + lax.axis_index("c")
    x, target = a["x"][0], a["loss_target"][0]

    c_all = _all_gather(a["c"], "gather_c")[:, 0, :]
    b_cols = lax.dynamic_slice_in_dim(a["b_ada"], me * ADA_COLS, ADA_COLS, axis=1)[:, None, :]
    mod_parts = _all_gather(_ada_fwd(c_all, a["w_ada"], b_cols), "gather_mod")
    mod = lax.dynamic_index_in_dim(mod_parts, me, axis=2, keepdims=False)
    mod = mod.transpose(1, 0, 2).reshape(DEPTH, 3 * D_MODEL)

    w = {n: a[n] for n in SMALL}

    w_in_b = [a["w_in"][l].astype(BF16) for l in range(DEPTH)]
    w_out_b = a["w_out"].astype(BF16)
    conv_own = jnp.concatenate([a["lru_conv_w"], a["ssd_conv_w"]], axis=-1)
    cols, rows_out = N_IN // N_DEV, D_INNER // N_DEV

    def gather_start(l, after):
        srcs = [w_in_b[l], w_out_b[l]] + ([conv_own] if l == 0 else [])
        lands = [lax.empty((N_DEV,) + s.shape, s.dtype) for s in srcs]
        return _exchange_start(f"gather_start_{l}", srcs, lands, "chip", after=after)

    def gather_pass(name, st, after, also=()):
        landed = _exchange_wait(name + "_wait", st, after, also)
        st2 = _exchange_start(name + "_pass", st["srcs"], landed, "pass")
        return _exchange_wait(name + "_passed", st2, after)

    gathers = {0: gather_start(0, mod)}
    passing = {}

    def fetch(l, x_l, late=False):
        if late:
            if l + 1 == DEPTH:
                return None
            landed = _exchange_wait(f"gather_{l + 1}_wait", gathers[l + 1], x_l)
            passing[l + 1] = _exchange_start(f"gather_{l + 1}_pass", gathers[l + 1]["srcs"], landed, "pass")
            return passing[l + 1]["token"]
        if l == 0:
            landed = gather_pass("gather_0", gathers[0], x_l, also=(a["w_in"], a["m_w_in"], a["v_w_in"]))
        else:
            landed = _exchange_wait(f"gather_{l}_passed", passing[l], x_l)
        land_out = lax.dynamic_update_index_in_dim(landed[1], w_out_b[l], me, 0)
        if l == 0:
            conv = lax.dynamic_update_index_in_dim(landed[2], conv_own, me, 0).transpose(1, 2, 0, 3)
            w["lru_conv_w"] = conv[..., :64].reshape(DEPTH, 4, LRU_W)
            w["ssd_conv_w"] = conv[..., 64:].reshape(DEPTH, 4, SSD_CONV)
        token = None
        if l + 1 < DEPTH:
            gathers[l + 1] = gather_start(l + 1, land_out)
            token = gathers[l + 1]["token"]
        return _relayout_in(landed[0], w_in_b[l]), land_out.reshape(D_INNER, D_MODEL), token

    PROJ = ("w_in", "w_out")
    scatters = {}
    lands = [lax.empty((N_DEV, DEPTH, D_MODEL, cols), BF16), lax.empty((N_DEV, DEPTH, rows_out, D_MODEL), BF16)]
    own = [None] * DEPTH

    deferred, g_out = {}, {}

    def emit(l, name, grad, after=None):
        if name == "w_out" and l > 0:
            g_out[l] = grad
            return None
        if name == "w_in" and l == 0 and after is None:
            deferred["w_in"] = grad
            return None
        if l == 0:
            k = PROJ.index(name)
            src = _relayout_grad(grad) if name == "w_in" else grad.reshape(N_DEV, rows_out, D_MODEL)
            st = _exchange_start(f"scatter_start_0_{name}", [src], [lands[k]], "scatter", layer=0, after=after)
            scatters[name] = st
            lands[k] = st["lands"][0]
            return st["token"]
        srcs = [_relayout_grad(grad), g_out[l].reshape(N_DEV, rows_out, D_MODEL)]
        st = _exchange_start(f"scatter_start_{l}", srcs, lands, "scatter", layer=l, after=after)
        scatters[l] = st
        lands[:] = st["lands"]
        return st["token"]

    loss_own, dx, dmod, g = _local_step(x, mod, target, w, fetch, emit)

    def sharded(name, parts, own=None, **kw):
        return _adamw(parts, a[name], a["m_" + name], a["v_" + name], "adamw_" + name + kw.pop("tag", ""), own=own, **kw)

    g["b_ada"] = dmod
    small_own = _flatten_small(g, last=loss_own)
    small_st = _exchange_start("gather_small", [small_own], [lax.empty((N_DEV,) + small_own.shape, F32)], "chip",
                               after=dx)
    big = {}
    after = emit(0, "w_in", deferred["w_in"], after=small_st["token"]) + dx[0:8, 0:LANE]

    def own_slices(st):
        return [lax.dynamic_index_in_dim(s, me, 0, keepdims=False) for s in st["srcs"]]

    for l in reversed(range(1, DEPTH)):
        scatters[l]["lands"] = lands
        lands[:] = _exchange_wait(f"scatter_wait_{l}", scatters[l], after)
        own[l] = own_slices(scatters[l])
    scatters["w_out"]["lands"] = [lands[1]]
    lands[1] = _exchange_wait("scatter_wait_0_w_out", scatters["w_out"], after)[0]
    own[0] = [None, own_slices(scatters["w_out"])[0]]
    big["w_out"] = sharded("w_out", lands[1], jnp.stack([own[l][1] for l in range(DEPTH)]))
    upper = sharded("w_in", lands[0], jnp.stack([own[l][0] for l in range(1, DEPTH)]), layers=(1, DEPTH), tag="_upper")
    after = upper[1][0, 0:8, 0:LANE] + big["w_out"][1][0, 0:8, 0:LANE]
    small = gather_pass("gather_small", small_st, after)[0]
    outs = _adamw(small[:, None], *[_flatten_small(a, p)[None] for p in ("", "m_", "v_")], "adamw_small",
                  own=small_own[None])
    res = [_split_small(o[0], a) for o in outs]
    losses = lax.dynamic_update_index_in_dim(small[:, -1, 0], loss_own, me, 0)
    loss = jnp.sum(losses)

    off = _small_rows(a)[0]["b_ada"][0]
    dmod_all = lax.dynamic_update_index_in_dim(small[:, off:off + DEPTH * 3 * D_MODEL // SMALL_ROW],
                                               dmod.reshape(-1, SMALL_ROW), me, 0)
    dmod_all = dmod_all.reshape(N_DEV, DEPTH, 3 * D_MODEL).transpose(1, 0, 2)
    dmod_cols = lax.dynamic_slice_in_dim(dmod_all, me * ADA_COLS, ADA_COLS, axis=2)
    dmod_pad = jnp.pad(dmod_cols, ((0, 0), (0, LANE - N_DEV), (0, 0)))
    ct_pad = jnp.pad(c_all.T, ((0, 0), (0, LANE - N_DEV)))
    big["w_ada"] = sharded("w_ada", _ada_bwd(ct_pad, dmod_pad)[None])
    g_conv = jnp.concatenate([g["lru_conv_w"].reshape(DEPTH, 4, N_DEV, 64), g["ssd_conv_w"].reshape(DEPTH, 4, N_DEV, 192)],
                             axis=-1).transpose(2, 0, 1, 3)
    conv_parts = _all_to_all(g_conv, "scatter_conv")
    big["lru_conv_w"] = sharded("lru_conv_w", conv_parts[..., :64])
    big["ssd_conv_w"] = sharded("ssd_conv_w", conv_parts[..., 64:])

    after = outs[1] + big["w_ada"][1][0, 0:1, 0:1]
    scatters["w_in"]["lands"] = [lands[0]]
    lands[0] = _exchange_wait("scatter_wait_0_w_in", scatters["w_in"], after)[0]
    big["w_in"] = sharded("w_in", lands[0], own_slices(scatters["w_in"])[0][None], layers=(0, 1), prev=upper)

    out = [loss, dx[None]]
    for k in range(4):
        out += [big[n][k] if n in big else res[k][n] for n in WEIGHTS]
    return tuple(out)
```

```python
import functools

import numpy as np
import jax
import jax.numpy as jnp
from jax import lax
from jax.experimental import pallas as pl
from jax.experimental.pallas import tpu as pltpu

F32 = jnp.float32
BF16 = jnp.bfloat16
SDS = jax.ShapeDtypeStruct

N_DEV = 8
DEPTH = 4
D_MODEL = 1024
D_INNER = 2048
EPS = 1e-6
LRU_W = 512
LRU_C = 8.0
HG_W = 512
HG_CHUNK = 64
HG_HEADS = 4
SSD_W = 1024
SSD_HEADS = 16
SSD_P = 64
SSD_N = 128
SSD_CHUNK = 128
SSD_CONV = 1536
N_IN = 5648
N_PAD = 5760
OFF_HG = 0
OFF_LRU = 2048
OFF_XBC = 3072
OFF_Z = 4608
LANE = 128
VMEM_LIMIT = 56 * 1024 * 1024
NEG = -1e30

ADAM_LR = 0.001
ADAM_B1 = 0.9
ADAM_B2 = 0.999
ADAM_EPS = 1e-08
ADAM_WD = 0.01
ADAM_STEP = 10


def _cp(sem=None):
    return pltpu.CompilerParams(dimension_semantics=sem, vmem_limit_bytes=VMEM_LIMIT)


def _dg(a, b, ca, cb):
    return lax.dot_general(a, b, (((ca,), (cb,)), ((), ())), preferred_element_type=F32)


def _mm(a, b):
    return _dg(a, b, 1, 0)


def _mm_nt(a, b):
    return _dg(a, b, 1, 1)


def _mm_tn(a, b):
    return _dg(a, b, 0, 0)


def _bf(x):
    return x.astype(BF16)


def _f(x):
    return x.astype(F32)


def _split3(x):
    hi = x.astype(BF16)
    r = x - hi.astype(F32)
    mid = r.astype(BF16)
    lo = (r - mid.astype(F32)).astype(BF16)
    return hi, mid, lo


def _sel_r(x, m):
    hi, mid, lo = _split3(x)
    return _mm(hi, m) + _mm(mid, m) + _mm(lo, m)


def _sel_l(m, x):
    hi, mid, lo = _split3(x)
    return _mm(m, hi) + _mm(m, mid) + _mm(m, lo)


def _sel_l2(m, x):
    hi = x.astype(BF16)
    lo = (x - hi.astype(F32)).astype(BF16)
    return _mm(m, hi) + _mm(m, lo)


def _sel_tn(x, m):
    hi, mid, lo = _split3(x)
    return _mm_tn(hi, m) + _mm_tn(mid, m) + _mm_tn(lo, m)


def _sigmoid(x):
    return 1.0 / (1.0 + jnp.exp(-x))


def _silu(x):
    return x * _sigmoid(x)


def _dsilu(x):
    s = _sigmoid(x)
    return s * (1.0 + x * (1.0 - s))


def _softplus(x):
    return jnp.maximum(x, 0.0) + jnp.log(1.0 + jnp.exp(-jnp.abs(x)))


def _expm1(z):
    series = z * (1.0 + z * (1.0 / 2) * (1.0 + z * (1.0 / 3) * (1.0 + z * (1.0 / 4) * (
        1.0 + z * (1.0 / 5) * (1.0 + z * (1.0 / 6) * (1.0 + z * (1.0 / 7)))))))
    return jnp.where(jnp.abs(z) < 0.3, series, jnp.exp(z) - 1.0)


def _iota(shape, dim):
    return lax.broadcasted_iota(jnp.int32, shape, dim)


def _last_row(x, rows):
    return jnp.sum(jnp.where(rows == x.shape[0] - 1, x, 0.0), axis=0, keepdims=True)


def _shift_down(x, d, rows, fill=0.0):
    return jnp.where(rows >= d, pltpu.roll(x, d, 0), fill)


def _shift_up(x, d, rows, fill=0.0):
    n = x.shape[0]
    return jnp.where(rows < n - d, pltpu.roll(x, n - d, 0), fill)


def _conv_fwd(x, cw_ref, cb_ref, rows):
    out = cb_ref[...] + cw_ref[pl.ds(3, 1), :] * x
    for k in range(3):
        out = out + cw_ref[pl.ds(k, 1), :] * _shift_down(x, 3 - k, rows)
    return out


def _conv_bwd(x, dco, cw_ref, rows):
    dx = cw_ref[pl.ds(3, 1), :] * dco
    dws = []
    for k in range(3):
        dx = dx + cw_ref[pl.ds(k, 1), :] * _shift_up(dco, 3 - k, rows)
        dws.append(jnp.sum(dco * _shift_down(x, 3 - k, rows), axis=0, keepdims=True))
    dws.append(jnp.sum(dco * x, axis=0, keepdims=True))
    return dx, dws, jnp.sum(dco, axis=0, keepdims=True)


def _vec(n):
    return pl.BlockSpec((1, n), lambda *_: (0, 0))


class _Row:
    def __init__(self, arr, l, n=None, c=0):
        self.arr, self.l, self.n, self.c = arr[:, None, :], l, n or arr.shape[1], c


def _spec(v):
    if isinstance(v, _Row):
        return pl.BlockSpec((None, 1, v.n), lambda *_: (v.l, 0, v.c))
    return _vec(v.shape[1])


def _arr(v):
    return v.arr if isinstance(v, _Row) else v


def _full(shape):
    nd = len(shape)
    return pl.BlockSpec(shape, lambda *_: (0,) * nd)


def _inproj_fwd(x, nw, scale, shift, w, tok):
    S = x.shape[0]
    tm = min(256, S)

    def body(x_ref, nw_ref, sc_ref, sh_ref, w_ref, tok_ref, u_ref, h_ref):
        del tok_ref
        xv = x_ref[...]
        inv = lax.rsqrt(jnp.mean(xv * xv, axis=-1, keepdims=True) + EPS)
        h = ((xv * inv) * nw_ref[...] * (1.0 + sc_ref[...]) + sh_ref[...]).astype(BF16)
        h_ref[...] = h
        u_ref[...] = _mm(h, w_ref[...])

    return pl.pallas_call(
        body, name="inproj_fwd", grid=(S // tm,),
        in_specs=[pl.BlockSpec((tm, D_MODEL), lambda i: (i, 0)), _spec(nw), _spec(scale), _spec(shift),
                  _full((D_MODEL, N_PAD)), pl.BlockSpec(memory_space=pl.ANY)],
        out_specs=[pl.BlockSpec((tm, N_PAD), lambda i: (i, 0)), pl.BlockSpec((tm, D_MODEL), lambda i: (i, 0))],
        out_shape=[SDS((S, N_PAD), F32), SDS((S, D_MODEL), BF16)],
        compiler_params=_cp(("parallel",)),
    )(x, _arr(nw), _arr(scale), _arr(shift), w, tok)


def _inproj_bwd_x(du, w, x, nw, scale, dxn, tok):
    S = x.shape[0]
    tm = min(256, S)

    def body(du_ref, w_ref, x_ref, nw_ref, sc_ref, dxn_ref, tok_ref, dx_ref, red_ref):
        del tok_ref

        @pl.when(pl.program_id(0) == 0)
        def _():
            red_ref[...] = jnp.zeros_like(red_ref)

        dh = _mm_nt(du_ref[...], w_ref[...])
        xv = x_ref[...]
        inv = lax.rsqrt(jnp.mean(xv * xv, axis=-1, keepdims=True) + EPS)
        xhat = xv * inv
        nwv = nw_ref[...]
        g1 = 1.0 + sc_ref[...]
        dxhat = dh * nwv * g1
        dx = inv * (dxhat - xhat * jnp.mean(dxhat * xhat, axis=-1, keepdims=True))
        dx_ref[...] = dxn_ref[...] + dx
        red_ref[0:1, :] += jnp.sum(dh, axis=0, keepdims=True)
        red_ref[1:2, :] += jnp.sum(dh * xhat * nwv, axis=0, keepdims=True)
        red_ref[2:3, :] += jnp.sum(dh * xhat * g1, axis=0, keepdims=True)

    row = pl.BlockSpec((tm, D_MODEL), lambda i: (i, 0))
    return pl.pallas_call(
        body, name="inproj_bwd_x", grid=(S // tm,),
        in_specs=[pl.BlockSpec((tm, N_PAD), lambda i: (i, 0)), _full((D_MODEL, N_PAD)), row, _spec(nw),
                  _spec(scale), row, pl.BlockSpec(memory_space=pl.ANY)],
        out_specs=[row, _full((8, D_MODEL))],
        out_shape=[SDS((S, D_MODEL), F32), SDS((8, D_MODEL), F32)],
        compiler_params=_cp(("arbitrary",)),
    )(du, w, x, _arr(nw), _arr(scale), dxn, tok)


def _inproj_bwd_w(h, du, tok):
    S = h.shape[0]
    tn = 640

    def body(h_ref, du_ref, tok_ref, gw_ref):
        del tok_ref
        gw_ref[...] = _mm_tn(h_ref[...], _bf(du_ref[...]))

    return pl.pallas_call(
        body, name="inproj_bwd_w", grid=(N_PAD // tn,),
        in_specs=[_full((S, D_MODEL)), pl.BlockSpec((S, tn), lambda j: (0, j)), pl.BlockSpec(memory_space=pl.ANY)],
        out_specs=pl.BlockSpec((D_MODEL, tn), lambda j: (0, j)),
        out_shape=SDS((D_MODEL, N_PAD), F32),
        compiler_params=_cp(("parallel",)),
    )(h, du, tok)


def _scan_block(a, b, rows):
    d = 1
    while d < a.shape[0]:
        a_s = _shift_down(a, d, rows, 1.0)
        b_s = _shift_down(b, d, rows, 0.0)
        b = a * b_s + b
        a = a * a_s
        d *= 2
    return a, b


def _rscan_block(c, g, rows):
    d = 1
    while d < c.shape[0]:
        c_s = _shift_up(c, d, rows, 1.0)
        g_s = _shift_up(g, d, rows, 0.0)
        g = g + c * g_s
        c = c * c_s
        d *= 2
    return c, g


LRU_BLOCK = 256


def _lru_gates(xa, wa_ref, ba_ref, wx_ref, bx_ref, lam_ref):
    sp = _softplus(-lam_ref[...])
    xb = _bf(xa)
    r = _sigmoid(_mm(xb, wa_ref[...]) + ba_ref[...])
    ig = _sigmoid(_mm(xb, wx_ref[...]) + bx_ref[...])
    la = -LRU_C * r * sp
    a = jnp.exp(la)
    mult = jnp.sqrt(-_expm1(2.0 * la))
    return sp, r, ig, la, a, mult


def _lru_specs(S, l):
    t128 = pl.BlockSpec((None, 1, LANE), lambda t: (l, 0, t))
    gate = pl.BlockSpec((None, None, LANE, LANE), lambda t: (l, t, 0, 0))
    return [pl.BlockSpec((S, 2 * LANE), lambda t: (0, OFF_LRU // (2 * LANE) + t)),
            pl.BlockSpec((None, 4, LANE), lambda t: (l, 0, t)), t128, gate, t128, gate, t128, t128]


def _lru_fwd(l, u, cw, cb, wa, ba, wx, bx, lam, ycat):
    S = u.shape[0]
    tb = min(LRU_BLOCK, S)

    def body(u_ref, cw_ref, cb_ref, wa_ref, ba_ref, wx_ref, bx_ref, lam_ref, ycat_in, ycat_ref, h_ref, a_scr, b_scr):
        del ycat_in
        rows = _iota((S, LANE), 0)
        xa = _conv_fwd(_f(u_ref[:, 0:LANE]), cw_ref, cb_ref, rows)
        _, _, ig, _, a, mult = _lru_gates(xa, wa_ref, ba_ref, wx_ref, bx_ref, lam_ref)
        a_scr[...] = a
        b_scr[...] = mult * (ig * xa)
        rows_b = _iota((tb, LANE), 0)

        def blk(j, hprev):
            sl = pl.ds(pl.multiple_of(j * tb, tb), tb)
            acum, hloc = _scan_block(a_scr[sl, :], b_scr[sl, :], rows_b)
            hf = hloc + acum * hprev
            h_ref[sl, :] = hf
            return _last_row(hf, rows_b)

        lax.fori_loop(0, S // tb, blk, jnp.zeros((1, LANE), F32))
        ycat_ref[...] = _bf(h_ref[...] * _silu(_f(u_ref[:, LANE:2 * LANE])))

    col = pl.BlockSpec((S, LANE), lambda t: (0, t))
    return pl.pallas_call(
        body, name="lru_fwd", grid=(LRU_W // LANE,),
        in_specs=_lru_specs(S, l) + [pl.BlockSpec(memory_space=pl.ANY)],
        out_specs=[col, col],
        out_shape=[SDS((S, D_INNER), BF16), SDS((S,LRU_W), F32)],
        scratch_shapes=[pltpu.VMEM((S, LANE), F32), pltpu.VMEM((S, LANE), F32)],
        input_output_aliases={8: 0},
        compiler_params=_cp(("parallel",)),
    )(u, cw, cb, wa, ba, wx, bx, lam, ycat)


def _lru_bwd(l, u, cw, cb, wa, ba, wx, bx, lam, h_lru, dycat, du):
    S = u.shape[0]
    tb = min(LRU_BLOCK, S)

    def body(u_ref, cw_ref, cb_ref, wa_ref, ba_ref, wx_ref, bx_ref, lam_ref, h_ref, dy_ref, du_in,
             du_ref, red_ref, gwa_ref, gwx_ref, c_scr, g_scr, l_scr):
        del du_in
        rows = _iota((S, LANE), 0)
        ax = _f(u_ref[:, 0:LANE])
        ag = _f(u_ref[:, LANE:2 * LANE])
        xa = _conv_fwd(ax, cw_ref, cb_ref, rows)
        sp, r, ig, la, a, mult = _lru_gates(xa, wa_ref, ba_ref, wx_ref, bx_ref, lam_ref)
        h = h_ref[...]
        dy = _f(dy_ref[...])
        du_ref[:, LANE:2 * LANE] = _bf(dy * h * _dsilu(ag))
        c_scr[...] = _shift_up(a, 1, rows, 0.0)
        g_scr[...] = dy * _silu(ag)
        rows_b = _iota((tb, LANE), 0)
        nb = S // tb

        def blk(jj, lnext):
            j = nb - 1 - jj
            sl = pl.ds(pl.multiple_of(j * tb, tb), tb)
            ccum, lloc = _rscan_block(c_scr[sl, :], g_scr[sl, :], rows_b)
            lam_t = lloc + ccum * lnext
            l_scr[sl, :] = lam_t
            return jnp.sum(jnp.where(rows_b == 0, lam_t, 0.0), axis=0, keepdims=True)

        lax.fori_loop(0, nb, blk, jnp.zeros((1, LANE), F32))
        db = l_scr[...]
        da = db * _shift_down(h, 1, rows)
        dmult = db * ig * xa
        dig = db * mult * xa
        dxa = db * mult * ig
        dla = da * a - dmult * (a * a) / mult
        dr = -LRU_C * sp * dla
        dsp = jnp.sum(-LRU_C * r * dla, axis=0, keepdims=True)
        dlam = -dsp * _sigmoid(-lam_ref[...])
        dzr = dr * r * (1.0 - r)
        dzi = dig * ig * (1.0 - ig)
        dzr_b, dzi_b, xa_b = _bf(dzr), _bf(dzi), _bf(xa)
        dxa = dxa + _mm_nt(dzr_b, wa_ref[...]) + _mm_nt(dzi_b, wx_ref[...])
        gwa_ref[...] = _mm_tn(xa_b, dzr_b)
        gwx_ref[...] = _mm_tn(xa_b, dzi_b)
        dax, dws, dcb = _conv_bwd(ax, dxa, cw_ref, rows)
        du_ref[:, 0:LANE] = _bf(dax)
        parts = dws + [dcb, jnp.sum(dzr, axis=0, keepdims=True), jnp.sum(dzi, axis=0, keepdims=True), dlam]
        for n, p in enumerate(parts):
            red_ref[pl.ds(n, 1), :] = p

    col = pl.BlockSpec((S, LANE), lambda t: (0, t))
    gw = pl.BlockSpec((None, LANE, LANE), lambda t: (t, 0, 0))
    return pl.pallas_call(
        body, name="lru_bwd", grid=(LRU_W // LANE,),
        in_specs=_lru_specs(S, l) + [col, col, pl.BlockSpec(memory_space=pl.ANY)],
        out_specs=[pl.BlockSpec((S, 2 * LANE), lambda t: (0, OFF_LRU // (2 * LANE) + t)),
                   pl.BlockSpec((8, LANE), lambda t: (0, t)), gw, gw],
        out_shape=[SDS((S, N_PAD), BF16), SDS((8, LRU_W), F32), SDS((4, LANE, LANE), F32), SDS((4, LANE, LANE), F32)],
        scratch_shapes=[pltpu.VMEM((S, LANE), F32)] * 3,
        input_output_aliases={10: 0},
        compiler_params=_cp(("parallel",)),
    )(u, cw, cb, wa, ba, wx, bx, lam, h_lru, dycat, du)


HG_LEVELS = 6


def _hg_consts():
    C = HG_CHUNK
    t = np.arange(C)[:, None]
    r = np.arange(C)[None, :]
    mats = []
    for l in range(HG_LEVELS):
        b = 1 << l
        upper = (t % (2 * b)) >= b
        anchor = (t // (2 * b)) * 2 * b + b - 1
        mats.append((upper & (r > anchor) & (r <= t)) | ((~upper) & (r > t) & (r <= anchor)))
    mats.append(r <= t)
    mats.append(r > t)
    return np.concatenate(mats, 0).astype(np.float32)


def _hg_factors(hf, lb, mall):
    s = _sigmoid(hf)
    f = lb + (1.0 - lb) * s
    lf = jnp.log(f)
    k = (1.0 - lb) * _sigmoid(-hf)
    e = jnp.exp(_sel_l(mall, lf))
    C = HG_CHUNK
    rows = _iota((C, HG_W), 0)
    eq, ek = [], []
    for l in range(HG_LEVELS):
        el = e[l * C:(l + 1) * C]
        eq.append(jnp.where((lax.shift_right_logical(rows, l) & 1) == 1, el, 0.0))
        ek.append(el - eq[l])
    ecum = e[HG_LEVELS * C:(HG_LEVELS + 1) * C]
    erem = e[(HG_LEVELS + 1) * C:(HG_LEVELS + 2) * C]
    return s, f, k, eq, ek, ecum, erem


def _hg_masks():
    C = HG_CHUNK
    ri, ci = _iota((C, C), 0), _iota((C, C), 1)
    rr = _iota((C, LANE), 0)
    gm = [(lax.shift_right_logical(ri, l + 1) == lax.shift_right_logical(ci, l + 1)).astype(F32)
          for l in range(HG_LEVELS)]
    up = [(lax.shift_right_logical(rr, l) & 1) == 1 for l in range(HG_LEVELS)]
    eye = (ri == ci).astype(F32)
    return gm, up, eye, rr


def _hg_scores(qh, kh, eq, ek, sl, gm, up, eye):
    del up
    qs, ks, qb, kb = [], [], [], []
    p = _mm_nt(_bf(qh), _bf(kh)) * eye
    for l in range(HG_LEVELS):
        qs.append(qh * eq[l][:, sl])
        ks.append(kh * ek[l][:, sl])
        qb.append(_bf(qs[l]))
        kb.append(_bf(ks[l]))
        p = p + _mm_nt(qb[l], kb[l]) * gm[l]
    return p, qs, ks, qb, kb


HG_SUB = 4


def _hg_fwd(u, lb, nw, mall, ycat):
    S = u.shape[0]
    C = HG_CHUNK
    n = S // C
    rows = HG_SUB * C

    def body(u_ref, lb_ref, nw_ref, mall_ref, ycat_in, ycat_ref, o_ref, st_ref, st):
        del ycat_in

        @pl.when(pl.program_id(0) == 0)
        def _():
            st[...] = jnp.zeros_like(st)

        gm, up, eye, rr = _hg_masks()
        for sub in range(HG_SUB):
            r = slice(sub * C, (sub + 1) * C)
            q = _silu(_f(u_ref[r, 0:512]))
            v = u_ref[r, 1024:1536]
            _, _, k, eq, ek, ecum, erem = _hg_factors(_f(u_ref[r, 512:1024]), lb_ref[...], mall_ref[...])
            for h in range(HG_HEADS):
                sl = slice(h * LANE, (h + 1) * LANE)
                qh, kh, vh = q[:, sl], k[:, sl], _bf(v[:, sl])
                p = _hg_scores(qh, kh, eq, ek, sl, gm, up, eye)[0]
                sth = st[h]
                st_ref[sub, h] = sth
                o_ref[r, sl] = _mm(_bf(p), vh) + _mm_nt(_bf(qh * ecum[:, sl]), _bf(sth))
                st[h] = sth * _last_row(ecum[:, sl], rr) + _mm_tn(vh, _bf(kh * erem[:, sl]))
            o = o_ref[r, :]
            inv = lax.rsqrt(jnp.mean(o * o, axis=-1, keepdims=True) + EPS)
            ycat_ref[r, :] = _bf((o * inv) * nw_ref[...] * _silu(_f(u_ref[r, 1536:2048])))

    return pl.pallas_call(
        body, name="hg_fwd", grid=(n // HG_SUB,),
        in_specs=[pl.BlockSpec((rows, 2048), lambda i: (i, 0)), _spec(lb), _spec(nw), _full(mall.shape),
                  pl.BlockSpec(memory_space=pl.ANY)],
        out_specs=[pl.BlockSpec((rows, HG_W), lambda i: (i, 1)), pl.BlockSpec((rows, HG_W), lambda i: (i, 0)),
                   pl.BlockSpec((HG_SUB, HG_HEADS, LANE, LANE), lambda i: (i, 0, 0, 0))],
        out_shape=[SDS((S, D_INNER), BF16), SDS((S,HG_W), F32), SDS((n, HG_HEADS, LANE, LANE), F32)],
        scratch_shapes=[pltpu.VMEM((HG_HEADS, LANE, LANE), F32)],
        input_output_aliases={4: 0},
        compiler_params=_cp(("arbitrary",)),
    )(u, _arr(lb), _arr(nw), mall, ycat)


def _hg_bwd(u, lb, nw, mall, mall_t, o_b, states, dycat, du):
    S = u.shape[0]
    C = HG_CHUNK
    n = S // C
    nb = n // HG_SUB
    rows = HG_SUB * C
    L2 = HG_LEVELS

    def body(u_ref, lb_ref, nw_ref, mall_ref, mallt_ref, o_ref, st_ref, dy_ref, du_in, du_ref, red_ref,
             dst, dlast_s, dq_s, dk_s, dex):
        del du_in

        @pl.when(pl.program_id(0) == 0)
        def _():
            dst[...] = jnp.zeros_like(dst)
            red_ref[...] = jnp.zeros_like(red_ref)

        lb = lb_ref[...]
        nwv = nw_ref[...]
        gm, up, eye, rr = _hg_masks()
        for sub in reversed(range(HG_SUB)):
            r = slice(sub * C, (sub + 1) * C)
            hq, hf, hg = _f(u_ref[r, 0:512]), _f(u_ref[r, 512:1024]), _f(u_ref[r, 1536:2048])
            q = _silu(hq)
            v = u_ref[r, 1024:1536]
            s, f, k, eq, ek, ecum, erem = _hg_factors(hf, lb, mall_ref[...])
            o = o_ref[r, :]
            dy = _f(dy_ref[r, :])
            inv = lax.rsqrt(jnp.mean(o * o, axis=-1, keepdims=True) + EPS)
            ohat = o * inv
            du_ref[r, 1536:2048] = _bf(dy * ohat * nwv * _dsilu(hg))
            dn = dy * _silu(hg)
            red_ref[0:1, :] += jnp.sum(dn * ohat, axis=0, keepdims=True)
            dohat = dn * nwv
            do = inv * (dohat - ohat * jnp.mean(dohat * ohat, axis=-1, keepdims=True))
            for h in range(HG_HEADS):
                sl = slice(h * LANE, (h + 1) * LANE)
                qh, kh, vh, doh = q[:, sl], k[:, sl], _bf(v[:, sl]), _bf(do[:, sl])
                p, qs, ks, qb, kb = _hg_scores(qh, kh, eq, ek, sl, gm, up, eye)
                st_f = st_ref[sub, h]
                sth = _bf(st_f)
                dsth = dst[h]
                dsth_b = _bf(dsth)
                qt = qh * ecum[:, sl]
                kt = kh * erem[:, sl]
                elast = _last_row(ecum[:, sl], rr)
                dp = _mm_nt(doh, vh)
                du_ref[r, 1024 + h * LANE:1024 + (h + 1) * LANE] = _bf(_mm_tn(_bf(p), doh) + _mm_nt(_bf(kt), dsth_b))
                dpe = _bf(dp * eye)
                dqt = _mm(doh, sth)
                dkt = _mm(vh, dsth_b)
                dq = dqt * ecum[:, sl] + _mm(dpe, _bf(kh))
                dk = dkt * erem[:, sl] + _mm_tn(dpe, _bf(qh))
                dex[sub, L2 * C:(L2 + 1) * C, sl] = dqt * qt
                dex[sub, (L2 + 1) * C:(L2 + 2) * C, sl] = dkt * kt
                for l in range(HG_LEVELS):
                    dpl = _bf(dp * gm[l])
                    dql = _mm(dpl, kb[l])
                    dkl = _mm_tn(dpl, qb[l])
                    dq = dq + dql * eq[l][:, sl]
                    dk = dk + dkl * ek[l][:, sl]
                    dex[sub, l * C:(l + 1) * C, sl] = dql * qs[l] + dkl * ks[l]
                dlast_s[sub, :, sl] = jnp.sum(dsth * st_f, axis=0, keepdims=True) * elast
                dst[h] = dsth * elast + _mm_tn(doh, _bf(qt))
                dq_s[sub, :, sl] = dq
                dk_s[sub, :, sl] = dk
            dq = dq_s[sub]
            dk = dk_s[sub]
            dlf = _sel_l2(mallt_ref[...], dex[sub]) + dlast_s[sub]
            du_ref[r, 0:512] = _bf(dq * _dsilu(hq))
            t = (1.0 - s) * (dlf / f - dk)
            du_ref[r, 512:1024] = _bf((1.0 - lb) * s * t)
            red_ref[1:2, :] += jnp.sum(t, axis=0, keepdims=True)

    rev = lambda i: (nb - 1 - i, 0)
    return pl.pallas_call(
        body, name="hg_bwd", grid=(nb,),
        in_specs=[pl.BlockSpec((rows, 2048), rev), _spec(lb), _spec(nw), _full(mall.shape), _full(mall_t.shape),
                  pl.BlockSpec((rows, HG_W), rev),
                  pl.BlockSpec((HG_SUB, HG_HEADS, LANE, LANE), lambda i: (nb - 1 - i, 0, 0, 0)),
                  pl.BlockSpec((rows, HG_W), lambda i: (nb - 1 - i, 1)), pl.BlockSpec(memory_space=pl.ANY)],
        out_specs=[pl.BlockSpec((rows, 2048), rev), pl.BlockSpec((8, HG_W), lambda i: (0, 0))],
        out_shape=[SDS((S, N_PAD), BF16), SDS((8, HG_W), F32)],
        scratch_shapes=[pltpu.VMEM((HG_HEADS, LANE, LANE), F32), pltpu.VMEM((HG_SUB, 1, HG_W), F32),
                        pltpu.VMEM((HG_SUB, C, HG_W), F32), pltpu.VMEM((HG_SUB, C, HG_W), F32),
                        pltpu.VMEM((HG_SUB, (L2 + 2) * C, HG_W), F32)],
        input_output_aliases={8: 0},
        compiler_params=_cp(("arbitrary",)),
    )(u, _arr(lb), _arr(nw), mall, mall_t, o_b, states, dycat, du)


def _ssdconv_fwd(l, u, cw, cb):
    S = u.shape[0]

    def body(u_ref, cw_ref, cb_ref, out_ref):
        rows = _iota((S, LANE), 0)
        out_ref[...] = _silu(_conv_fwd(_f(u_ref[...]), cw_ref, cb_ref, rows))

    return pl.pallas_call(
        body, name="ssdconv_fwd", grid=(SSD_CONV // LANE,),
        in_specs=[pl.BlockSpec((S, LANE), lambda t: (0, OFF_XBC // LANE + t)),
                  pl.BlockSpec((None, 4, LANE), lambda t: (l, 0, t)), pl.BlockSpec((None, 1, LANE), lambda t: (l, 0, t))],
        out_specs=pl.BlockSpec((S, LANE), lambda t: (0, t)),
        out_shape=SDS((S, SSD_CONV), F32),
        compiler_params=_cp(("parallel",)),
    )(u, cw, cb)


def _ssdconv_bwd(l, u, cw, cb, dxbc, du):
    S = u.shape[0]

    def body(u_ref, cw_ref, cb_ref, d_ref, du_in, du_ref, red_ref):
        del du_in
        rows = _iota((S, LANE), 0)
        x = _f(u_ref[...])
        dco = d_ref[...] * _dsilu(_conv_fwd(x, cw_ref, cb_ref, rows))
        dx, dws, dcb = _conv_bwd(x, dco, cw_ref, rows)
        du_ref[...] = _bf(dx)
        for n, p in enumerate(dws + [dcb]):
            red_ref[pl.ds(n, 1), :] = p
        red_ref[pl.ds(5, 3), :] = jnp.zeros((3, LANE), F32)

    ucol = pl.BlockSpec((S, LANE), lambda t: (0, OFF_XBC // LANE + t))
    return pl.pallas_call(
        body, name="ssdconv_bwd", grid=(SSD_CONV // LANE,),
        in_specs=[ucol, pl.BlockSpec((None, 4, LANE), lambda t: (l, 0, t)),
                  pl.BlockSpec((None, 1, LANE), lambda t: (l, 0, t)),
                  pl.BlockSpec((S, LANE), lambda t: (0, t)), pl.BlockSpec(memory_space=pl.ANY)],
        out_specs=[ucol, pl.BlockSpec((8, LANE), lambda t: (0, t))],
        out_shape=[SDS((S, N_PAD), BF16), SDS((8, SSD_CONV), F32)],
        input_output_aliases={4: 0},
        compiler_params=_cp(("parallel",)),
    )(u, cw, cb, dxbc, du)


def _ssd_consts():
    e64 = np.zeros((LANE, SSD_W), np.float32)
    for h in range(SSD_HEADS):
        e64[h, h * SSD_P:(h + 1) * SSD_P] = 1.0
    T = SSD_CHUNK
    tril = (np.arange(T)[None, :] <= np.arange(T)[:, None]).astype(np.float32)
    return e64, tril, tril.T.copy()


def _ssd_common(zdt, bias_ref, alog_ref, tril, e64, cum_ref, cumt_ref):
    T = SSD_CHUNK
    lane = _iota((1, LANE), 1)
    a_neg = jnp.where(lane < SSD_HEADS, -jnp.exp(alog_ref[...]), 0.0)
    dtpre = zdt[:, SSD_W:SSD_W + LANE] + bias_ref[...]
    dt = _softplus(dtpre)
    cum = _sel_l(tril, dt * a_neg)
    cum_ref[...] = cum
    cumt_ref[...] = cum.T
    cum_x = _sel_r(cum, e64)
    last_x = _last_row(cum_x, _iota((T, SSD_W), 0))
    ecum_x = jnp.exp(cum_x)
    erem_x = jnp.exp(last_x - cum_x)
    elast_x = jnp.exp(last_x)
    dt_x = _sel_r(dt, e64)
    return a_neg, dtpre, dt, ecum_x, erem_x, elast_x, dt_x


def _ssd_decay(cum_ref, cumt_ref, h, causal):
    T = SSD_CHUNK
    diff = jnp.broadcast_to(cum_ref[:, pl.ds(h, 1)], (T, T)) - cumt_ref[pl.ds(h, 1), :]
    return jnp.exp(jnp.where(causal, diff, NEG))


def _group_norm_fwd(y1, nwv):
    outs, invs = [], []
    for g in range(2):
        seg = y1[:, g * 512:(g + 1) * 512]
        inv = lax.rsqrt(jnp.mean(seg * seg, axis=-1, keepdims=True) + EPS)
        outs.append(seg * inv * nwv[:, g * 512:(g + 1) * 512])
        invs.append(inv)
    return outs, invs


def _ssd_fwd(u, xbc, bias, alog, dskip_x, nw, consts, ycat):
    S = u.shape[0]
    T = SSD_CHUNK
    n = S // T
    e64, tril, _ = consts

    def body(u_ref, xbc_ref, bias_ref, alog_ref, dx_ref, nw_ref, e64_ref, tril_ref, ycat_in,
             ycat_ref, y_ref, st_ref, st, cumt, cum_e):
        del ycat_in

        @pl.when(pl.program_id(0) == 0)
        def _():
            st[...] = jnp.zeros_like(st)

        zdt = _f(u_ref[...])
        z = zdt[:, 0:SSD_W]
        xs = xbc_ref[:, 0:SSD_W]
        _, _, _, ecum_x, erem_x, elast_x, dt_x = _ssd_common(
            zdt, bias_ref, alog_ref, tril_ref[...], e64_ref[...], cum_e, cumt)
        causal = _iota((T, T), 0) >= _iota((T, T), 1)
        lo = _iota((T, LANE), 1) < SSD_P
        xdt = xs * dt_x
        xrem = xdt * erem_x
        st_ref[...] = st[...]
        for g in range(2):
            gs = slice(g * 512, (g + 1) * 512)
            bg = _bf(xbc_ref[:, SSD_W + g * LANE:SSD_W + (g + 1) * LANE])
            cg = _bf(xbc_ref[:, SSD_W + 256 + g * LANE:SSD_W + 256 + (g + 1) * LANE])
            cb = _mm_nt(cg, bg)
            yin = _mm(cg, _bf(st[:, gs])) * ecum_x[:, gs]
            for j in range(4):
                h0 = 8 * g + 2 * j
                cs = slice(h0 * SSD_P, (h0 + 2) * SSD_P)
                xp = xdt[:, cs]
                s0 = _bf(cb * _ssd_decay(cum_e, cumt, h0, causal))
                s1 = _bf(cb * _ssd_decay(cum_e, cumt, h0 + 1, causal))
                y_ref[:, cs] = (_mm(s0, _bf(jnp.where(lo, xp, 0.0))) + _mm(s1, _bf(jnp.where(lo, 0.0, xp)))
                                + yin[:, j * LANE:(j + 1) * LANE])
            st[:, gs] = st[:, gs] * elast_x[:, gs] + _mm_tn(bg, _bf(xrem[:, gs]))
        y1 = (y_ref[...] + dx_ref[...] * xs) * _silu(z)
        outs, _ = _group_norm_fwd(y1, nw_ref[...])
        for g in range(2):
            ycat_ref[:, g * 512:(g + 1) * 512] = _bf(outs[g])

    return pl.pallas_call(
        body, name="ssd_fwd", grid=(n,),
        in_specs=[pl.BlockSpec((T, SSD_W + LANE), lambda i: (i, OFF_Z // (SSD_W + LANE))),
                  pl.BlockSpec((T, SSD_CONV), lambda i: (i, 0)), _spec(bias), _spec(alog), _spec(dskip_x), _spec(nw),
                  _full(e64.shape), _full(tril.shape), pl.BlockSpec(memory_space=pl.ANY)],
        out_specs=[pl.BlockSpec((T, SSD_W), lambda i: (i, 1)), pl.BlockSpec((T, SSD_W), lambda i: (i, 0)),
                   pl.BlockSpec((None, SSD_N, SSD_W), lambda i: (i, 0, 0))],
        out_shape=[SDS((S, D_INNER), BF16), SDS((S,SSD_W), F32), SDS((n, SSD_N, SSD_W), F32)],
        scratch_shapes=[pltpu.VMEM((SSD_N, SSD_W), F32), pltpu.VMEM((LANE, T), F32), pltpu.VMEM((T, LANE), F32)],
        input_output_aliases={8: 0},
        compiler_params=_cp(("arbitrary",)),
    )(u, xbc, _arr(bias), _arr(alog), _arr(dskip_x), _arr(nw), _bfc(e64), _bfc(tril), ycat)


def _ssd_bwd(u, xbc, bias, alog, dskip_x, nw, consts, y_ssd, states, dycat, du, tok):
    S = u.shape[0]
    T = SSD_CHUNK
    n = S // T
    e64, tril, triu = consts
    e64t = np.ascontiguousarray(e64.T)

    def body(u_ref, xbc_ref, bias_ref, alog_ref, dx_ref, nw_ref, e64_ref, e64t_ref, tril_ref, triu_ref,
             y_ref, st_ref, dy_ref, du_in, tok_ref, du_ref, dxbc_ref, red_ref, dst, dl_s, cumt, dxdt_s, dy0_s, gb_s,
             gc_s, cum_e, cs_s):
        del du_in, tok_ref

        @pl.when(pl.program_id(0) == 0)
        def _():
            dst[...] = jnp.zeros_like(dst)
            red_ref[...] = jnp.zeros_like(red_ref)
            cs_s[...] = jnp.zeros_like(cs_s)

        zdt = _f(u_ref[...])
        z = zdt[:, 0:SSD_W]
        xs = xbc_ref[:, 0:SSD_W]
        a_neg, dtpre, dt, ecum_x, erem_x, elast_x, dt_x = _ssd_common(
            zdt, bias_ref, alog_ref, tril_ref[...], e64_ref[...], cum_e, cumt)
        causal = _iota((T, T), 0) >= _iota((T, T), 1)
        lo = _iota((T, LANE), 1) < SSD_P
        xdt = xs * dt_x
        xrem = xdt * erem_x
        y = y_ref[...]
        dxv = dx_ref[...]
        nwv = nw_ref[...]
        sz = _silu(z)
        y0 = y + dxv * xs
        y1 = y0 * sz
        for g in range(2):
            gs = slice(g * 512, (g + 1) * 512)
            seg = y1[:, gs]
            inv = lax.rsqrt(jnp.mean(seg * seg, axis=-1, keepdims=True) + EPS)
            shat = seg * inv
            dyg = _f(dy_ref[:, gs])
            red_ref[0:1, gs] += jnp.sum(dyg * shat, axis=0, keepdims=True)
            dsh = dyg * nwv[:, gs]
            dy1g = inv * (dsh - shat * jnp.mean(dsh * shat, axis=-1, keepdims=True))
            du_ref[:, gs] = _bf(dy1g * y0[:, gs] * _dsilu(z[:, gs]))
            dy0_s[:, gs] = dy1g * sz[:, gs]
        dy0 = dy0_s[...]
        red_ref[1:2, :] += jnp.sum(dy0 * xs, axis=0, keepdims=True)
        dyin = dy0 * ecum_x
        lane = _iota((T, LANE), 1)
        dcum = jnp.zeros((T, LANE), F32)

        def decay_grad(h, gm):
            cs_s[pl.ds(h, 1), :] = jnp.sum(gm, axis=0, keepdims=True)
            return jnp.where(lane == h, jnp.sum(gm, axis=1, keepdims=True), 0.0)

        for g in range(2):
            gs = slice(g * 512, (g + 1) * 512)
            bg = _bf(xbc_ref[:, SSD_W + g * LANE:SSD_W + (g + 1) * LANE])
            cg = _bf(xbc_ref[:, SSD_W + 256 + g * LANE:SSD_W + 256 + (g + 1) * LANE])
            cb = _mm_nt(cg, bg)
            dst_f, st_f = dst[:, gs], st_ref[:, gs]
            dstg = _bf(dst_f)
            stg = _bf(st_f)
            dyin_g = _bf(dyin[:, gs])
            xrem_g = _bf(xrem[:, gs])
            dcb = jnp.zeros((T, T), F32)
            dxr = _mm(bg, dstg)
            dxdt_s[:, gs] = dxr * erem_x[:, gs]
            gc_s[:, gs] = dxr * xrem[:, gs]
            gb_s[:, gs] = dyin[:, gs] * _mm(cg, stg)
            dl_s[:, gs] = jnp.sum(dst_f * st_f, axis=0, keepdims=True) * elast_x[:, gs]
            for j in range(4):
                h0 = 8 * g + 2 * j
                cs = slice(h0 * SSD_P, (h0 + 2) * SSD_P)
                xp = xdt[:, cs]
                dyp = dy0[:, cs]
                x_lo, x_hi = _bf(jnp.where(lo, xp, 0.0)), _bf(jnp.where(lo, 0.0, xp))
                d_lo, d_hi = _bf(jnp.where(lo, dyp, 0.0)), _bf(jnp.where(lo, 0.0, dyp))
                l0 = _ssd_decay(cum_e, cumt, h0, causal)
                l1 = _ssd_decay(cum_e, cumt, h0 + 1, causal)
                s0 = cb * l0
                s1 = cb * l1
                ds0 = _mm_nt(d_lo, x_lo)
                ds1 = _mm_nt(d_hi, x_hi)
                dcb = dcb + ds0 * l0 + ds1 * l1
                dxdt_s[:, cs] += _mm_tn(_bf(s0), d_lo) + _mm_tn(_bf(s1), d_hi)
                dcum = dcum + decay_grad(h0, ds0 * s0) + decay_grad(h0 + 1, ds1 * s1)
            dcb_b = _bf(dcb)
            dxbc_ref[:, SSD_W + g * LANE:SSD_W + (g + 1) * LANE] = _mm_tn(dcb_b, cg) + _mm_nt(xrem_g, dstg)
            dxbc_ref[:, SSD_W + 256 + g * LANE:SSD_W + 256 + (g + 1) * LANE] = _mm(dcb_b, bg) + _mm_nt(dyin_g, stg)
            dst[:, gs] = dst_f * elast_x[:, gs] + _mm_tn(cg, dyin_g)
        dxdt = dxdt_s[...]
        dxbc_ref[:, 0:SSD_W] = dxdt * dt_x + dy0 * dxv
        e64t = e64t_ref[...]
        gc = gc_s[...]
        dlast_x = jnp.sum(gc, axis=0, keepdims=True) + dl_s[...]
        dlast = jnp.max(_sel_r(jnp.broadcast_to(dlast_x, (8, SSD_W)), e64t), axis=0, keepdims=True)
        dcum = (dcum - cs_s[...].T + _sel_r(gb_s[...] - gc, e64t)
                + jnp.where(_iota((T, LANE), 0) == T - 1, dlast, 0.0))
        dda = _sel_l(triu_ref[...], dcum)
        ddt = dda * a_neg + _sel_r(dxdt * xs, e64t)
        ddtpre = ddt * _sigmoid(dtpre)
        du_ref[:, SSD_W:SSD_W + LANE] = _bf(jnp.where(lane < SSD_HEADS, ddtpre, 0.0))
        red_ref[2:3, 0:LANE] += jnp.sum(ddtpre, axis=0, keepdims=True)
        red_ref[3:4, 0:LANE] += jnp.sum(dda * dt, axis=0, keepdims=True)

    rev = lambda i: (n - 1 - i, 0)
    return pl.pallas_call(
        body, name="ssd_bwd", grid=(n,),
        in_specs=[pl.BlockSpec((T, SSD_W + LANE), lambda i: (n - 1 - i, OFF_Z // (SSD_W + LANE))),
                  pl.BlockSpec((T, SSD_CONV), rev), _spec(bias), _spec(alog), _spec(dskip_x), _spec(nw),
                  _full(e64.shape), _full(e64t.shape), _full(tril.shape), _full(triu.shape),
                  pl.BlockSpec((T, SSD_W), rev), pl.BlockSpec((None, SSD_N, SSD_W), lambda i: (n - 1 - i, 0, 0)),
                  pl.BlockSpec((T, SSD_W), lambda i: (n - 1 - i, 1)), pl.BlockSpec(memory_space=pl.ANY),
                  pl.BlockSpec(memory_space=pl.ANY)],
        out_specs=[pl.BlockSpec((T, SSD_W + LANE), lambda i: (n - 1 - i, OFF_Z // (SSD_W + LANE))),
                   pl.BlockSpec((T, SSD_CONV), rev), pl.BlockSpec((8, SSD_W), lambda i: (0, 0))],
        out_shape=[SDS((S, N_PAD), BF16), SDS((S, SSD_CONV), F32), SDS((8, SSD_W), F32)],
        scratch_shapes=[pltpu.VMEM((SSD_N, SSD_W), F32), pltpu.VMEM((1, SSD_W), F32), pltpu.VMEM((LANE, T), F32)]
        + [pltpu.VMEM((T, SSD_W), F32)] * 4 + [pltpu.VMEM((T, LANE), F32), pltpu.VMEM((LANE, T), F32)],
        input_output_aliases={13: 0},
        compiler_params=_cp(("arbitrary",)),
    )(u, xbc, _arr(bias), _arr(alog), _arr(dskip_x), _arr(nw), _bfc(e64), _bfc(e64t), _bfc(tril), _bfc(triu), y_ssd,
      states, dycat, du, tok)


def _bfc(a):
    return jnp.asarray(a, BF16)


def _outproj_fwd(ycat, wo, x, gate, tok):
    S = x.shape[0]
    tm = min(512, S)

    def body(yc_ref, wo_ref, x_ref, g_ref, tok_ref, xn_ref, y_ref):
        del tok_ref
        y = _mm(_bf(yc_ref[...]), wo_ref[...])
        y_ref[...] = y
        xn_ref[...] = x_ref[...] + g_ref[...] * y

    row = pl.BlockSpec((tm, D_MODEL), lambda i: (i, 0))
    return pl.pallas_call(
        body, name="outproj_fwd", grid=(S // tm,),
        in_specs=[pl.BlockSpec((tm, D_INNER), lambda i: (i, 0)), _full((D_INNER, D_MODEL)), row, _spec(gate),
                  pl.BlockSpec(memory_space=pl.ANY)],
        out_specs=[row, row],
        out_shape=[SDS((S, D_MODEL), F32), SDS((S, D_MODEL), F32)],
        compiler_params=_cp(("parallel",)),
    )(ycat, wo, x, _arr(gate), tok)


def _outproj_bwd(dxn, y, gate, ycat, wo):
    S = dxn.shape[0]
    tm = min(512, S)

    def body(dx_ref, y_ref, g_ref, yc_ref, wo_ref, dyc_ref, gwo_ref, dg_ref, acc):
        @pl.when(pl.program_id(0) == 0)
        def _():
            acc[...] = jnp.zeros_like(acc)
            dg_ref[...] = jnp.zeros_like(dg_ref)

        dxv = dx_ref[...]
        dy = _bf(dxv * g_ref[...])
        dg_ref[0:1, :] += jnp.sum(dxv * y_ref[...], axis=0, keepdims=True)
        dyc_ref[...] = _mm_nt(dy, wo_ref[...])
        acc[...] += _mm_tn(_bf(yc_ref[...]), dy)

        @pl.when(pl.program_id(0) == pl.num_programs(0) - 1)
        def _():
            gwo_ref[...] = acc[...].astype(BF16)

    row = pl.BlockSpec((tm, D_MODEL), lambda i: (i, 0))
    wide = pl.BlockSpec((tm, D_INNER), lambda i: (i, 0))
    return pl.pallas_call(
        body, name="outproj_bwd", grid=(S // tm,),
        in_specs=[row, row, _spec(gate), wide, _full((D_INNER, D_MODEL))],
        out_specs=[wide, _full((D_INNER, D_MODEL)), _full((8, D_MODEL))],
        out_shape=[SDS((S, D_INNER), F32), SDS((D_INNER, D_MODEL), BF16), SDS((8, D_MODEL), F32)],
        scratch_shapes=[pltpu.VMEM((D_INNER, D_MODEL), F32)],
        compiler_params=_cp(("arbitrary",)),
    )(dxn, y, _arr(gate), ycat, wo)


def _loss_head(x, fw, target):
    S = x.shape[0]
    tm = min(512, S)

    def body(x_ref, fw_ref, t_ref, dx_ref, red_ref):
        @pl.when(pl.program_id(0) == 0)
        def _():
            red_ref[...] = jnp.zeros_like(red_ref)

        xv = x_ref[...]
        fwv = fw_ref[...]
        inv = lax.rsqrt(jnp.mean(xv * xv, axis=-1, keepdims=True) + EPS)
        xhat = xv * inv
        err = xhat * fwv - t_ref[...]
        col = jnp.sum(err * err, axis=0, keepdims=True)
        red_ref[1:2, :] += jnp.broadcast_to(jnp.sum(col, axis=1, keepdims=True) * (0.5 / D_MODEL), (1, D_MODEL))
        dy = err * (1.0 / D_MODEL)
        red_ref[0:1, :] += jnp.sum(dy * xhat, axis=0, keepdims=True)
        dxhat = dy * fwv
        dx_ref[...] = inv * (dxhat - xhat * jnp.mean(dxhat * xhat, axis=-1, keepdims=True))

    row = pl.BlockSpec((tm, D_MODEL), lambda i: (i, 0))
    return pl.pallas_call(
        body, name="loss_head", grid=(S // tm,),
        in_specs=[row, _vec(D_MODEL), row],
        out_specs=[row, _full((8, D_MODEL))],
        out_shape=[SDS((S, D_MODEL), F32), SDS((8, D_MODEL), F32)],
        compiler_params=_cp(("arbitrary",)),
    )(x, fw, target)


ADA_COLS = 3 * D_MODEL // N_DEV


def _ada_fwd(c_all, w_ada, b_cols):
    def body(c_ref, w_ref, b_ref, out_ref):
        out_ref[...] = _mm(_bf(_silu(c_ref[...])), _bf(w_ref[...])) + b_ref[...]

    return pl.pallas_call(
        body, name="ada_fwd", grid=(DEPTH,),
        in_specs=[_full((N_DEV, D_MODEL)), pl.BlockSpec((None, D_MODEL, ADA_COLS), lambda l: (l, 0, 0)),
                  pl.BlockSpec((None, 1, ADA_COLS), lambda l: (l, 0, 0))],
        out_specs=pl.BlockSpec((None, N_DEV, ADA_COLS), lambda l: (l, 0, 0)),
        out_shape=SDS((DEPTH, N_DEV, ADA_COLS), F32),
        compiler_params=_cp(("parallel",)),
    )(c_all, w_ada, b_cols)


def _ada_bwd(ct_pad, dmod_pad):
    def body(c_ref, d_ref, out_ref):
        out_ref[...] = _mm(_bf(_silu(c_ref[...])), _bf(d_ref[...]))

    return pl.pallas_call(
        body, name="ada_bwd", grid=(DEPTH,),
        in_specs=[_full((D_MODEL, LANE)), pl.BlockSpec((None, LANE, ADA_COLS), lambda l: (l, 0, 0))],
        out_specs=pl.BlockSpec((None, D_MODEL, ADA_COLS), lambda l: (l, 0, 0)),
        out_shape=SDS((DEPTH, D_MODEL, ADA_COLS), F32),
        compiler_params=_cp(("parallel",)),
    )(ct_pad, dmod_pad)


def _adamw(parts, w, m, v, name, own=None, layers=None, prev=None):
    n, L, R, C = parts.shape
    lo, hi = layers or (0, L)
    tr = R
    while tr * C * 4 > (1 << 20) and tr % 16 == 0:
        tr //= 2
    first = 1 if own is None else 2

    def body(*refs):
        p_ref = refs[0]
        w_ref, m_ref, v_ref = refs[first:first + 3]
        g_ref, d_ref, mo_ref, vo_ref = refs[-4:]

        def part(k):
            if own is None:
                return p_ref[k].astype(F32)
            me = 4 * lax.axis_index("x") + 2 * lax.axis_index("y") + lax.axis_index("c")
            return jnp.where(me == k, refs[1][...], p_ref[k]).astype(F32)

        g = part(0)
        for k in range(1, n):
            g = g + part(k)
        mn = ADAM_B1 * m_ref[...] + (1.0 - ADAM_B1) * g
        vn = ADAM_B2 * v_ref[...] + (1.0 - ADAM_B2) * (g * g)
        m_hat = mn / (1.0 - ADAM_B1 ** ADAM_STEP)
        v_hat = vn / (1.0 - ADAM_B2 ** ADAM_STEP)
        g_ref[...] = g
        d_ref[...] = -ADAM_LR * (m_hat / (jnp.sqrt(v_hat) + ADAM_EPS) + ADAM_WD * w_ref[...])
        mo_ref[...] = mn
        vo_ref[...] = vn

    blk = pl.BlockSpec((None, tr, C), lambda l, i: (lo + l, i, 0))
    own_blk = [] if own is None else [pl.BlockSpec((None, tr, C), lambda l, i: (l, i, 0))]
    n_blk = 3 if own is None else 4
    return pl.pallas_call(
        body, name=name, grid=(hi - lo, R // tr),
        in_specs=[pl.BlockSpec((n, None, tr, C), lambda l, i: (0, lo + l, i, 0))] + own_blk + [blk] * 3
        + ([] if prev is None else [ANY] * 4),
        out_specs=[blk] * 4,
        out_shape=[SDS((L, R, C), F32)] * 4,
        input_output_aliases={} if prev is None else {1 + n_blk + k: k for k in range(4)},
        compiler_params=_cp(("parallel", "parallel")),
    )(parts, *([] if own is None else [own]), w, m, v, *([] if prev is None else prev))


MESH = pl.DeviceIdType.MESH
ANY = pl.BlockSpec(memory_space=pl.ANY)


def _all_gather(v, name):
    def body(v_ref, out_ref, send_sems, recv_sems, local_sem):
        x, y, c = lax.axis_index("x"), lax.axis_index("y"), lax.axis_index("c")
        me, sibling = (x, y, c), (x, y, 1 - c)
        chips = [(1 - x, y), (x, 1 - y), (1 - x, 1 - y)]

        def slot(px, py, pc):
            return out_ref.at[4 * px + 2 * py + pc]

        def copy(k, block, to, src=None):
            return pltpu.make_async_remote_copy(
                src_ref=slot(*block) if src is None else src, dst_ref=slot(*block),
                send_sem=send_sems.at[k], recv_sem=recv_sems.at[k], device_id=to, device_id_type=MESH)

        mine = pltpu.make_async_copy(v_ref, slot(*me), local_sem)
        mine.start()
        first = [copy(0, me, sibling, src=v_ref)]
        first += [copy(1 + j, me, (*chip, c), src=v_ref) for j, chip in enumerate(chips)]
        for cp in first:
            cp.start()
        passed = [copy(4 + j, (*chip, c), sibling) for j, chip in enumerate(chips)]
        for j, chip in enumerate(chips):
            copy(1 + j, (*chip, c), me).wait_recv()
            passed[j].start()
        copy(0, sibling, me).wait_recv()
        for j, chip in enumerate(chips):
            copy(4 + j, (*chip, 1 - c), me).wait_recv()
        for cp in first + passed:
            cp.wait_send()
        mine.wait()

    return pl.pallas_call(
        body, name=name, in_specs=[ANY], out_specs=ANY,
        out_shape=SDS((N_DEV,) + v.shape, v.dtype),
        scratch_shapes=[pltpu.SemaphoreType.DMA((7,)), pltpu.SemaphoreType.DMA((7,)), pltpu.SemaphoreType.DMA],
    )(v)


def _all_to_all(v, name):
    def body(v_ref, out_ref, send_sems, recv_sems, local_sem):
        x, y, c = lax.axis_index("x"), lax.axis_index("y"), lax.axis_index("c")
        mine_idx = 4 * x + 2 * y + c
        mine = pltpu.make_async_copy(v_ref.at[mine_idx], out_ref.at[mine_idx], local_sem)
        mine.start()
        sends, recvs = [], []
        for k in range(1, N_DEV):
            px = 1 - x if k & 4 else x
            py = 1 - y if k & 2 else y
            pc = 1 - c if k & 1 else c
            peer_idx = 4 * px + 2 * py + pc
            sems = dict(send_sem=send_sems.at[k - 1], recv_sem=recv_sems.at[k - 1], device_id=(px, py, pc),
                        device_id_type=MESH)
            sends.append(pltpu.make_async_remote_copy(src_ref=v_ref.at[peer_idx], dst_ref=out_ref.at[mine_idx], **sems))
            recvs.append(pltpu.make_async_remote_copy(src_ref=v_ref.at[peer_idx], dst_ref=out_ref.at[peer_idx], **sems))
        for cp in sends:
            cp.start()
        for cp in recvs:
            cp.wait_recv()
        for cp in sends:
            cp.wait_send()
        mine.wait()

    return pl.pallas_call(
        body, name=name, in_specs=[ANY], out_specs=ANY,
        out_shape=SDS(v.shape, v.dtype),
        scratch_shapes=[pltpu.SemaphoreType.DMA((7,)), pltpu.SemaphoreType.DMA((7,)), pltpu.SemaphoreType.DMA],
    )(v)


HBM_SPEC = pl.BlockSpec(memory_space=pltpu.HBM)
SEM_SPEC = pl.BlockSpec(memory_space=pltpu.SEMAPHORE)
EFFECT = pltpu.SideEffectType.DATAFLOW_SIDE_EFFECTING


EXCHANGE_PEERS = {"gather": range(1, N_DEV), "scatter": range(1, N_DEV), "chip": (1, 2, 4, 6), "pass": (2, 4, 6)}


def _exchange_copies(srcs, lands, send_sems, recv_sems, mode, layer):
    x, y, c = lax.axis_index("x"), lax.axis_index("y"), lax.axis_index("c")
    me = 4 * x + 2 * y + c
    copies = []
    for a, (src, land) in enumerate(zip(srcs, lands)):
        for k in EXCHANGE_PEERS[mode]:
            px = 1 - x if k & 4 else x
            py = 1 - y if k & 2 else y
            pc = 1 - c if k & 1 else c
            peer = 4 * px + 2 * py + pc
            if mode == "scatter":
                s, d, to = src.at[peer], land.at[me, layer], (px, py, pc)
            elif mode == "pass":
                s, d, to = land.at[peer], land.at[peer], (x, y, 1 - c)
            else:
                s, d, to = src, land.at[me], (px, py, pc)
            n = 7 * a + k - 1
            copies.append(pltpu.make_async_remote_copy(
                src_ref=s, dst_ref=d, send_sem=send_sems.at[n], recv_sem=recv_sems.at[n], device_id=to,
                device_id_type=MESH))
    return copies


def _exchange_start(name, srcs, lands, mode, layer=0, after=None):
    n = len(srcs)

    def body(*refs):
        send_sems, recv_sems = refs[-2 * n - 3], refs[-2 * n - 2]
        for cp in _exchange_copies(refs[:n], refs[n:2 * n], send_sems, recv_sems, mode, layer):
            cp.start()
        refs[-1][...] = jnp.zeros_like(refs[-1])

    arrays = list(srcs) + list(lands)
    sems = pltpu.SemaphoreType.DMA((7 * n,))
    out = pl.pallas_call(
        body, name=name,
        out_shape=(sems, sems, *[pltpu.HBM(v.shape, v.dtype) for v in arrays], SDS((8, LANE), F32)),
        in_specs=[HBM_SPEC] * (2 * n) + ([ANY] if after is not None else []),
        out_specs=(SEM_SPEC, SEM_SPEC, *[HBM_SPEC] * (2 * n), pl.BlockSpec(memory_space=pltpu.VMEM)),
        input_output_aliases={i: 2 + i for i in range(2 * n)},
        compiler_params=pltpu.CompilerParams(has_side_effects=EFFECT),
    )(*[pltpu.with_memory_space_constraint(v, pltpu.HBM) for v in arrays], *([after] if after is not None else []))
    return dict(sems=out[:2], srcs=out[2:2 + n], lands=out[2 + n:2 + 2 * n], token=out[-1], mode=mode,
                layer=layer)


def _exchange_wait(name, st, after, also=()):
    n = len(st["srcs"])

    def body(*refs):
        send_sems, recv_sems = refs[2 * n], refs[2 * n + 1]
        for cp in _exchange_copies(refs[:n], refs[n:2 * n], send_sems, recv_sems, st["mode"], st["layer"]):
            cp.wait_send()
            cp.wait_recv()

    arrays = list(st["srcs"]) + list(st["lands"])
    out = pl.pallas_call(
        body, name=name,
        out_shape=tuple(pltpu.HBM(v.shape, v.dtype) for v in arrays),
        in_specs=[HBM_SPEC] * (2 * n) + [SEM_SPEC, SEM_SPEC] + [ANY] * (1 + len(also)),
        out_specs=tuple([HBM_SPEC] * (2 * n)),
        input_output_aliases={i: i for i in range(2 * n)},
        compiler_params=pltpu.CompilerParams(has_side_effects=EFFECT),
    )(*arrays, *st["sems"], after, *also)
    st["srcs"] = out[:n]
    return out[n:]


_IN_PIECES = ([(1024, 3072)]
              + [r for t in range(4) for r in ((LANE * t, LANE * (t + 1)), (512 + LANE * t, 512 + LANE * (t + 1)))]
              + [(4096, 5632), (3072, 4096), (5632, 5648)])


def _permute_in(w):
    pad = jnp.zeros(w.shape[:-1] + (N_PAD - N_IN,), w.dtype)
    return jnp.concatenate([w[..., a:b] for a, b in _IN_PIECES] + [pad], axis=-1)


def _unpermute_in(g):
    ax = [g[..., OFF_LRU + 2 * LANE * t:OFF_LRU + 2 * LANE * t + LANE] for t in range(4)]
    ag = [g[..., OFF_LRU + 2 * LANE * t + LANE:OFF_LRU + 2 * LANE * (t + 1)] for t in range(4)]
    return jnp.concatenate(ax + ag + [g[..., 0:2048], g[..., OFF_Z:OFF_Z + SSD_W], g[..., OFF_XBC:OFF_XBC + SSD_CONV],
                                      g[..., OFF_Z + SSD_W:OFF_Z + SSD_W + SSD_HEADS]], axis=-1)


SHARD_COLS = N_IN // N_DEV


def _in_segments():
    segs, pos = [], 0
    for a, b in _IN_PIECES:
        for i in range(N_DEV):
            lo, hi = max(a, SHARD_COLS * i), min(b, SHARD_COLS * (i + 1))
            if lo < hi:
                segs.append((i, lo - SHARD_COLS * i, hi - lo, pos + lo - a))
        pos += b - a
    return segs


RELAYOUT_ROWS = 256


def _relayout_in(land, own):
    def body(land_ref, own_ref, out_ref):
        me = 4 * lax.axis_index("x") + 2 * lax.axis_index("y") + lax.axis_index("c")
        out_ref[:, N_IN:N_PAD] = jnp.zeros((RELAYOUT_ROWS, N_PAD - N_IN), BF16)
        for i, j, wd, p in _in_segments():
            out_ref[:, p:p + wd] = jnp.where(me == i, own_ref[:, j:j + wd], land_ref[i, :, j:j + wd])

    return pl.pallas_call(
        body, name="relayout_in", grid=(D_MODEL // RELAYOUT_ROWS,),
        in_specs=[pl.BlockSpec((N_DEV, RELAYOUT_ROWS, SHARD_COLS), lambda r: (0, r, 0)),
                  pl.BlockSpec((RELAYOUT_ROWS, SHARD_COLS), lambda r: (r, 0))],
        out_specs=pl.BlockSpec((RELAYOUT_ROWS, N_PAD), lambda r: (r, 0)),
        out_shape=SDS((D_MODEL, N_PAD), BF16),
        compiler_params=_cp(("parallel",)),
    )(land, own)


def _relayout_grad(g):
    def body(g_ref, out_ref):
        for i, j, wd, p in _in_segments():
            out_ref[i, :, j:j + wd] = g_ref[:, p:p + wd].astype(BF16)

    return pl.pallas_call(
        body, name="relayout_grad", grid=(D_MODEL // RELAYOUT_ROWS,),
        in_specs=[pl.BlockSpec((RELAYOUT_ROWS, N_PAD), lambda r: (r, 0))],
        out_specs=pl.BlockSpec((N_DEV, RELAYOUT_ROWS, SHARD_COLS), lambda r: (0, r, 0)),
        out_shape=SDS((N_DEV, D_MODEL, SHARD_COLS), BF16),
        compiler_params=_cp(("parallel",)),
    )(g)


def _block_diag(w):
    w4 = w.reshape(DEPTH, 4, 2, 64, 64)
    z = jnp.zeros((DEPTH, 4, 64, 64), w.dtype)
    top = jnp.concatenate([w4[:, :, 0], z], axis=-1)
    bot = jnp.concatenate([z, w4[:, :, 1]], axis=-1)
    return jnp.concatenate([top, bot], axis=2).astype(BF16)


def _diag_blocks(g):
    return jnp.stack([g[:, :, :64, :64], g[:, :, 64:, 64:]], axis=2).reshape(DEPTH, 8, 64, 64)


def _pad_lanes(v):
    return jnp.pad(v, ((0, 0), (0, LANE - v.shape[1])))


def _lower_bounds(logits):
    p = jax.nn.softmax(logits, axis=0)
    return p, jnp.cumsum(p, axis=0) - p[0]


def _lower_bounds_bwd(p, dlb):
    dp = jnp.cumsum(dlb[::-1], axis=0)[::-1]
    dp = dp.at[0].add(-jnp.sum(dlb, axis=0))
    return p * (dp - jnp.sum(dp * p, axis=0, keepdims=True))


SMALL = ["norm_w", "b_ada", "lru_conv_b", "lru_wa", "lru_ba", "lru_wx", "lru_bx", "lru_lambda", "hg_lb_logits",
         "hg_norm_w", "ssd_conv_b", "ssd_dt_bias", "ssd_a_log", "ssd_d", "ssd_norm_w", "final_norm_w"]
WEIGHTS = ["norm_w", "w_ada", "b_ada", "w_in", "lru_conv_w", "lru_conv_b", "lru_wa", "lru_ba", "lru_wx", "lru_bx",
           "lru_lambda", "hg_lb_logits", "hg_norm_w", "ssd_conv_w", "ssd_conv_b", "ssd_dt_bias", "ssd_a_log", "ssd_d",
           "ssd_norm_w", "w_out", "final_norm_w"]
INPUTS = ["x", "c"] + WEIGHTS + ["loss_target"] + ["m_" + n for n in WEIGHTS] + ["v_" + n for n in WEIGHTS]
SMALL_ROW = 1024


def _small_rows(like):
    out, off = {}, 0
    for n in SMALL:
        rows = -(-int(np.prod(like[n].shape)) // (8 * SMALL_ROW)) * 8
        out[n] = (off, rows)
        off += rows
    return out, off


def _flatten_small(d, prefix="", last=0.0):
    table, _ = _small_rows({n: d[prefix + n] for n in SMALL})
    pieces = []
    for n in SMALL:
        flat = d[prefix + n].reshape(-1)
        pieces.append(jnp.pad(flat, (0, table[n][1] * SMALL_ROW - flat.shape[0])).reshape(-1, SMALL_ROW))
    return jnp.concatenate(pieces + [jnp.full((8, SMALL_ROW), last, F32)], axis=0)


def _split_small(packed, like):
    table, _ = _small_rows(like)
    out = {}
    for n in SMALL:
        off, rows = table[n]
        size = int(np.prod(like[n].shape))
        out[n] = packed[off:off + rows].reshape(-1)[:size].reshape(like[n].shape)
    return out


def _local_step(x, mod, target, w, fetch, emit):
    S = x.shape[0]
    mall = _bfc(_hg_consts())
    mall_t = _bfc(_hg_consts().T)
    consts = _ssd_consts()
    p_lb, lbs = _lower_bounds(w["hg_lb_logits"])
    no_tok = jnp.zeros((8, LANE), F32)
    wa, wx = _block_diag(w["lru_wa"]), _block_diag(w["lru_wx"])
    ba, bx = w["lru_ba"].reshape(DEPTH, 1, LRU_W), w["lru_bx"].reshape(DEPTH, 1, LRU_W)
    lru_cb, lam, ssd_cb = w["lru_conv_b"][:, None], w["lru_lambda"][:, None], w["ssd_conv_b"][:, None]
    bias, alog = _pad_lanes(w["ssd_dt_bias"]), _pad_lanes(w["ssd_a_log"])
    dskip = jnp.repeat(w["ssd_d"], SSD_P, axis=1)
    saved = []
    for l in range(DEPTH):
        w_in_l, w_out_l, token = fetch(l, x)
        shift, scale, gate = (_Row(mod, l, D_MODEL, k) for k in range(3))
        nw = _Row(w["norm_w"], l)
        u, h = _inproj_fwd(x, nw, scale, shift, w_in_l, no_tok if token is None else token)
        ycat = lax.empty((S, D_INNER), BF16)
        lru_args = (l, u, w["lru_conv_w"], lru_cb, wa, ba, wx, bx, lam)
        ycat, h_lru = _lru_fwd(*lru_args, ycat)
        hg_args = (u, _Row(lbs, l), _Row(w["hg_norm_w"], l), mall)
        ycat, o_b, hg_st = _hg_fwd(*hg_args, ycat)
        xbc = _ssdconv_fwd(l, u, w["ssd_conv_w"], ssd_cb)
        ssd_args = (u, xbc, _Row(bias, l), _Row(alog, l), _Row(dskip, l), _Row(w["ssd_norm_w"], l), consts)
        ycat, y_ssd, ssd_st = _ssd_fwd(*ssd_args, ycat)
        token = fetch(l, y_ssd, late=True)
        x_new, y = _outproj_fwd(ycat, w_out_l, x, gate, no_tok if token is None else token)
        saved.append((x, u, h, ycat, nw, scale, gate, w_in_l, w_out_l, lru_args, h_lru, hg_args, o_b, hg_st, ssd_args,
                      y_ssd, ssd_st, y))
        x = x_new
    dx, red = _loss_head(x, w["final_norm_w"][None, :], target)
    loss = red[1, 0]
    reds = {k: [None] * DEPTH for k in ("in", "gate", "lru", "wa", "wx", "hg", "conv", "ssd")}
    for l in reversed(range(DEPTH)):
        (x, u, h, ycat, nw, scale, gate, w_in_l, w_out_l, lru_args, h_lru, hg_args, o_b, hg_st, ssd_args, y_ssd, ssd_st,
         y) = saved[l]
        dycat, g_out, reds["gate"][l] = _outproj_bwd(dx, y, gate, ycat, w_out_l)
        token = emit(l, "w_out", g_out)
        du = lax.empty((S, N_PAD), BF16)
        du, dxbc, reds["ssd"][l] = _ssd_bwd(*ssd_args, y_ssd, ssd_st, dycat, du, no_tok if token is None else token)
        du, reds["conv"][l] = _ssdconv_bwd(l, u, w["ssd_conv_w"], ssd_cb, dxbc, du)
        du, reds["hg"][l] = _hg_bwd(*hg_args, mall_t, o_b, hg_st, dycat, du)
        du, reds["lru"][l], reds["wa"][l], reds["wx"][l] = _lru_bwd(*lru_args, h_lru, dycat, du)
        token = emit(l, "w_in", functools.partial(_inproj_bwd_w, h, du))
        dx, reds["in"][l] = _inproj_bwd_x(du, w_in_l, x, nw, scale, dx, no_tok if token is None else token)
    r = {k: jnp.stack(v) for k, v in reds.items()}
    g = {n: None for n in WEIGHTS}
    g["final_norm_w"] = red[0]
    g["norm_w"] = r["in"][:, 2]
    dmod = jnp.concatenate([r["in"][:, 0], r["in"][:, 1], r["gate"][:, 0]], axis=1)
    g["lru_conv_w"], g["lru_conv_b"] = r["lru"][:, 0:4], r["lru"][:, 4]
    g["lru_ba"], g["lru_bx"] = r["lru"][:, 5].reshape(DEPTH, 8, 64), r["lru"][:, 6].reshape(DEPTH, 8, 64)
    g["lru_lambda"] = r["lru"][:, 7]
    g["lru_wa"], g["lru_wx"] = _diag_blocks(r["wa"]), _diag_blocks(r["wx"])
    g["hg_norm_w"] = r["hg"][:, 0]
    g["hg_lb_logits"] = _lower_bounds_bwd(p_lb, r["hg"][:, 1])
    g["ssd_conv_w"], g["ssd_conv_b"] = r["conv"][:, 0:4], r["conv"][:, 4]
    g["ssd_norm_w"] = r["ssd"][:, 0]
    g["ssd_d"] = r["ssd"][:, 1].reshape(DEPTH, SSD_HEADS, SSD_P).sum(-1)
    g["ssd_dt_bias"] = r["ssd"][:, 2, :SSD_HEADS]
    g["ssd_a_log"] = -r["ssd"][:, 3, :SSD_HEADS] * jnp.exp(w["ssd_a_log"])
    return loss, dx, dmod, g


def kernel(x, c, norm_w, w_ada, b_ada, w_in, lru_conv_w, lru_conv_b, lru_wa, lru_ba, lru_wx, lru_bx, lru_lambda, hg_lb_logits, hg_norm_w, ssd_conv_w, ssd_conv_b, ssd_dt_bias, ssd_a_log, ssd_d, ssd_norm_w, w_out, final_norm_w, loss_target, m_norm_w, m_w_ada, m_b_ada, m_w_in, m_lru_conv_w, m_lru_conv_b, m_lru_wa, m_lru_ba, m_lru_wx, m_lru_bx, m_lru_lambda, m_hg_lb_logits, m_hg_norm_w, m_ssd_conv_w, m_ssd_conv_b, m_ssd_dt_bias, m_ssd_a_log, m_ssd_d, m_ssd_norm_w, m_w_out, m_final_norm_w, v_norm_w, v_w_ada, v_b_ada, v_w_in, v_lru_conv_w, v_lru_conv_b, v_lru_wa, v_lru_ba, v_lru_wx, v_lru_bx, v_lru_lambda, v_hg_lb_logits, v_hg_norm_w, v_ssd_conv_w, v_ssd_conv_b, v_ssd_dt_bias, v_ssd_a_log, v_ssd_d, v_ssd_norm_w, v_w_out, v_final_norm_w):
    return _step(x, c, norm_w, w_ada, b_ada, w_in, lru_conv_w, lru_conv_b, lru_wa, lru_ba, lru_wx, lru_bx, lru_lambda, hg_lb_logits, hg_norm_w, ssd_conv_w, ssd_conv_b, ssd_dt_bias, ssd_a_log, ssd_d, ssd_norm_w, w_out, final_norm_w, loss_target, m_norm_w, m_w_ada, m_b_ada, m_w_in, m_lru_conv_w, m_lru_conv_b, m_lru_wa, m_lru_ba, m_lru_wx, m_lru_bx, m_lru_lambda, m_hg_lb_logits, m_hg_norm_w, m_ssd_conv_w, m_ssd_conv_b, m_ssd_dt_bias, m_ssd_a_log, m_ssd_d, m_ssd_norm_w, m_w_out, m_final_norm_w, v_norm_w, v_w_ada, v_b_ada, v_w_in, v_lru_conv_w, v_lru_conv_b, v_lru_wa, v_lru_ba, v_lru_wx, v_lru_bx, v_lru_lambda, v_hg_lb_logits, v_hg_norm_w, v_ssd_conv_w, v_ssd_conv_b, v_ssd_dt_bias, v_ssd_a_log, v_ssd_d, v_ssd_norm_w, v_w_out, v_final_norm_w)


def _step(*args):
    a = dict(zip(INPUTS, args, strict=True))
    me = 4 * lax.axis_index("x") + 2 * lax.axis_index("y") + lax.axis_index("c")
    x, target = a["x"][0], a["loss_target"][0]

    c_all = _all_gather(a["c"], "gather_c")[:, 0, :]
    b_cols = lax.dynamic_slice_in_dim(a["b_ada"], me * ADA_COLS, ADA_COLS, axis=1)[:, None, :]
    mod_parts = _all_gather(_ada_fwd(c_all, a["w_ada"], b_cols), "gather_mod")
    mod = lax.dynamic_index_in_dim(mod_parts, me, axis=2, keepdims=False)
    mod = mod.transpose(1, 0, 2).reshape(DEPTH, 3 * D_MODEL)

    w = {n: a[n] for n in SMALL}

    w_in_b = [a["w_in"][l].astype(BF16) for l in range(DEPTH)]
    w_out_b = a["w_out"].astype(BF16)
    conv_own = jnp.concatenate([a["lru_conv_w"], a["ssd_conv_w"]], axis=-1)
    cols, rows_out = N_IN // N_DEV, D_INNER // N_DEV

    def gather_start(l, after):
        srcs = [w_in_b[l], w_out_b[l]] + ([conv_own] if l == 0 else [])
        lands = [lax.empty((N_DEV,) + s.shape, s.dtype) for s in srcs]
        return _exchange_start(f"gather_start_{l}", srcs, lands, "chip", after=after)

    def gather_pass(name, st, after, also=()):
        landed = _exchange_wait(name + "_wait", st, after, also)
        st2 = _exchange_start(name + "_pass", st["srcs"], landed, "pass")
        return _exchange_wait(name + "_passed", st2, after)

    gathers = {0: gather_start(0, mod)}
    passing = {}

    def fetch(l, x_l, late=False):
        if late:
            if l + 1 == DEPTH:
                return None
            landed = _exchange_wait(f"gather_{l + 1}_wait", gathers[l + 1], x_l)
            passing[l + 1] = _exchange_start(f"gather_{l + 1}_pass", gathers[l + 1]["srcs"], landed, "pass")
            return passing[l + 1]["token"]
        if l == 0:
            landed = gather_pass("gather_0", gathers[0], x_l, also=(a["w_in"], a["m_w_in"], a["v_w_in"]))
        else:
            landed = _exchange_wait(f"gather_{l}_passed", passing[l], x_l)
        land_out = lax.dynamic_update_index_in_dim(landed[1], w_out_b[l], me, 0)
        if l == 0:
            conv = lax.dynamic_update_index_in_dim(landed[2], conv_own, me, 0).transpose(1, 2, 0, 3)
            w["lru_conv_w"] = conv[..., :64].reshape(DEPTH, 4, LRU_W)
            w["ssd_conv_w"] = conv[..., 64:].reshape(DEPTH, 4, SSD_CONV)
        token = None
        if l + 1 < DEPTH:
            gathers[l + 1] = gather_start(l + 1, land_out)
            token = gathers[l + 1]["token"]
        return _relayout_in(landed[0], w_in_b[l]), land_out.reshape(D_INNER, D_MODEL), token

    PROJ = ("w_in", "w_out")
    scatters = {}
    lands = [lax.empty((N_DEV, DEPTH, D_MODEL, cols), BF16), lax.empty((N_DEV, DEPTH, rows_out, D_MODEL), BF16)]
    own = [None] * DEPTH

    deferred, g_out = {}, {}

    def emit(l, name, grad, after=None):
        if name == "w_out" and l > 0:
            g_out[l] = grad
            return None
        if name == "w_in" and l == 0 and after is None:
            deferred["w_in"] = grad
            return None
        if name == "w_in":
            grad = grad(jnp.zeros((8, LANE), F32) if after is None else after)
        if l == 0:
            k = PROJ.index(name)
            src = _relayout_grad(grad) if name == "w_in" else grad.reshape(N_DEV, rows_out, D_MODEL)
            st = _exchange_start(f"scatter_start_0_{name}", [src], [lands[k]], "scatter", layer=0, after=after)
            scatters[name] = st
            lands[k] = st["lands"][0]
            return st["token"]
        srcs = [_relayout_grad(grad), g_out[l].reshape(N_DEV, rows_out, D_MODEL)]
        st = _exchange_start(f"scatter_start_{l}", srcs, lands, "scatter", layer=l, after=after)
        scatters[l] = st
        lands[:] = st["lands"]
        return st["token"]

    loss_own, dx, dmod, g = _local_step(x, mod, target, w, fetch, emit)

    def sharded(name, parts, own=None, **kw):
        return _adamw(parts, a[name], a["m_" + name], a["v_" + name], "adamw_" + name + kw.pop("tag", ""), own=own, **kw)

    g["b_ada"] = dmod
    small_own = _flatten_small(g, last=loss_own)
    small_st = _exchange_start("gather_small", [small_own], [lax.empty((N_DEV,) + small_own.shape, F32)], "chip",
                               after=dx)
    big = {}
    after = emit(0, "w_in", deferred["w_in"], after=small_st["token"]) + dx[0:8, 0:LANE]

    def own_slices(st):
        return [lax.dynamic_index_in_dim(s, me, 0, keepdims=False) for s in st["srcs"]]

    for l in reversed(range(1, DEPTH)):
        scatters[l]["lands"] = lands
        lands[:] = _exchange_wait(f"scatter_wait_{l}", scatters[l], after)
        own[l] = own_slices(scatters[l])
    scatters["w_out"]["lands"] = [lands[1]]
    lands[1] = _exchange_wait("scatter_wait_0_w_out", scatters["w_out"], after)[0]
    own[0] = [None, own_slices(scatters["w_out"])[0]]
    big["w_out"] = sharded("w_out", lands[1], jnp.stack([own[l][1] for l in range(DEPTH)]))
    upper = sharded("w_in", lands[0], jnp.stack([own[l][0] for l in range(1, DEPTH)]), layers=(1, DEPTH), tag="_upper")
    after = upper[1][0, 0:8, 0:LANE] + big["w_out"][1][0, 0:8, 0:LANE]
    small = gather_pass("gather_small", small_st, after)[0]
    outs = _adamw(small[:, None], *[_flatten_small(a, p)[None] for p in ("", "m_", "v_")], "adamw_small",
                  own=small_own[None])
    res = [_split_small(o[0], a) for o in outs]
    losses = lax.dynamic_update_index_in_dim(small[:, -1, 0], loss_own, me, 0)
    loss = jnp.sum(losses)

    off = _small_rows(a)[0]["b_ada"][0]
    dmod_all = lax.dynamic_update_index_in_dim(small[:, off:off + DEPTH * 3 * D_MODEL // SMALL_ROW],
                                               dmod.reshape(-1, SMALL_ROW), me, 0)
    dmod_all = dmod_all.reshape(N_DEV, DEPTH, 3 * D_MODEL).transpose(1, 0, 2)
    dmod_cols = lax.dynamic_slice_in_dim(dmod_all, me * ADA_COLS, ADA_COLS, axis=2)
    dmod_pad = jnp.pad(dmod_cols, ((0, 0), (0, LANE - N_DEV), (0, 0)))
    ct_pad = jnp.pad(c_all.T, ((0, 0), (0, LANE - N_DEV)))
    big["w_ada"] = sharded("w_ada", _ada_bwd(ct_pad, dmod_pad)[None])
    g_conv = jnp.concatenate([g["lru_conv_w"].reshape(DEPTH, 4, N_DEV, 64), g["ssd_conv_w"].reshape(DEPTH, 4, N_DEV, 192)],
                             axis=-1).transpose(2, 0, 1, 3)
    conv_parts = _all_to_all(g_conv, "scatter_conv")
    big["lru_conv_w"] = sharded("lru_conv_w", conv_parts[..., :64])
    big["ssd_conv_w"] = sharded("ssd_conv_w", conv_parts[..., 64:])

    after = outs[1] + big["w_ada"][1][0, 0:1, 0:1]
    scatters["w_in"]["lands"] = [lands[0]]
    lands[0] = _exchange_wait("scatter_wait_0_w_in", scatters["w_in"], after)[0]
    big["w_in"] = sharded("w_in", lands[0], own_slices(scatters["w_in"])[0][None], layers=(0, 1), prev=upper)

    out = [loss, dx[None]]
    for k in range(4):
        out += [big[n][k] if n in big else res[k][n] for n in WEIGHTS]
    return tuple(out)
```

```python
import functools

import numpy as np
import jax
import jax.numpy as jnp
from jax import lax
from jax.experimental import pallas as pl
from jax.experimental.pallas import tpu as pltpu

F32 = jnp.float32
BF16 = jnp.bfloat16
SDS = jax.ShapeDtypeStruct

N_DEV = 8
DEPTH = 4
D_MODEL = 1024
D_INNER = 2048
EPS = 1e-6
LRU_W = 512
LRU_C = 8.0
HG_W = 512
HG_CHUNK = 64
HG_HEADS = 4
SSD_W = 1024
SSD_HEADS = 16
SSD_P = 64
SSD_N = 128
SSD_CHUNK = 128
SSD_CONV = 1536
N_IN = 5648
N_PAD = 5760
OFF_HG = 0
OFF_LRU = 2048
OFF_XBC = 3072
OFF_Z = 4608
LANE = 128
VMEM_LIMIT = 56 * 1024 * 1024
NEG = -1e30

ADAM_LR = 0.001
ADAM_B1 = 0.9
ADAM_B2 = 0.999
ADAM_EPS = 1e-08
ADAM_WD = 0.01
ADAM_STEP = 10


def _cp(sem=None):
    return pltpu.CompilerParams(dimension_semantics=sem, vmem_limit_bytes=VMEM_LIMIT)


def _dg(a, b, ca, cb):
    return lax.dot_general(a, b, (((ca,), (cb,)), ((), ())), preferred_element_type=F32)


def _mm(a, b):
    return _dg(a, b, 1, 0)


def _mm_nt(a, b):
    return _dg(a, b, 1, 1)


def _mm_tn(a, b):
    return _dg(a, b, 0, 0)


def _bf(x):
    return x.astype(BF16)


def _f(x):
    return x.astype(F32)


def _split3(x):
    hi = x.astype(BF16)
    r = x - hi.astype(F32)
    mid = r.astype(BF16)
    lo = (r - mid.astype(F32)).astype(BF16)
    return hi, mid, lo


def _sel_r(x, m):
    hi, mid, lo = _split3(x)
    return _mm(hi, m) + _mm(mid, m) + _mm(lo, m)


def _sel_l(m, x):
    hi, mid, lo = _split3(x)
    return _mm(m, hi) + _mm(m, mid) + _mm(m, lo)


def _sel_l2(m, x):
    hi = x.astype(BF16)
    lo = (x - hi.astype(F32)).astype(BF16)
    return _mm(m, hi) + _mm(m, lo)


def _sel_tn(x, m):
    hi, mid, lo = _split3(x)
    return _mm_tn(hi, m) + _mm_tn(mid, m) + _mm_tn(lo, m)


def _sigmoid(x):
    return 1.0 / (1.0 + jnp.exp(-x))


def _silu(x):
    return x * _sigmoid(x)


def _dsilu(x):
    s = _sigmoid(x)
    return s * (1.0 + x * (1.0 - s))


def _softplus(x):
    return jnp.maximum(x, 0.0) + jnp.log(1.0 + jnp.exp(-jnp.abs(x)))


def _expm1(z):
    series = z * (1.0 + z * (1.0 / 2) * (1.0 + z * (1.0 / 3) * (1.0 + z * (1.0 / 4) * (
        1.0 + z * (1.0 / 5) * (1.0 + z * (1.0 / 6) * (1.0 + z * (1.0 / 7)))))))
    return jnp.where(jnp.abs(z) < 0.3, series, jnp.exp(z) - 1.0)


def _iota(shape, dim):
    return lax.broadcasted_iota(jnp.int32, shape, dim)


def _last_row(x, rows):
    return jnp.sum(jnp.where(rows == x.shape[0] - 1, x, 0.0), axis=0, keepdims=True)


def _shift_down(x, d, rows, fill=0.0):
    return jnp.where(rows >= d, pltpu.roll(x, d, 0), fill)


def _shift_up(x, d, rows, fill=0.0):
    n = x.shape[0]
    return jnp.where(rows < n - d, pltpu.roll(x, n - d, 0), fill)


def _conv_fwd(x, cw_ref, cb_ref, rows):
    out = cb_ref[...] + cw_ref[pl.ds(3, 1), :] * x
    for k in range(3):
        out = out + cw_ref[pl.ds(k, 1), :] * _shift_down(x, 3 - k, rows)
    return out


def _conv_bwd(x, dco, cw_ref, rows):
    dx = cw_ref[pl.ds(3, 1), :] * dco
    dws = []
    for k in range(3):
        dx = dx + cw_ref[pl.ds(k, 1), :] * _shift_up(dco, 3 - k, rows)
        dws.append(jnp.sum(dco * _shift_down(x, 3 - k, rows), axis=0, keepdims=True))
    dws.append(jnp.sum(dco * x, axis=0, keepdims=True))
    return dx, dws, jnp.sum(dco, axis=0, keepdims=True)


def _vec(n):
    return pl.BlockSpec((1, n), lambda *_: (0, 0))


class _Row:
    def __init__(self, arr, l, n=None, c=0):
        self.arr, self.l, self.n, self.c = arr[:, None, :], l, n or arr.shape[1], c


def _spec(v):
    if isinstance(v, _Row):
        return pl.BlockSpec((None, 1, v.n), lambda *_: (v.l, 0, v.c))
    return _vec(v.shape[1])


def _arr(v):
    return v.arr if isinstance(v, _Row) else v


def _full(shape):
    nd = len(shape)
    return pl.BlockSpec(shape, lambda *_: (0,) * nd)


def _inproj_fwd(x, nw, scale, shift, w, tok):
    S = x.shape[0]
    tm = min(256, S)

    def body(x_ref, nw_ref, sc_ref, sh_ref, w_ref, tok_ref, u_ref, h_ref):
        del tok_ref
        xv = x_ref[...]
        inv = lax.rsqrt(jnp.mean(xv * xv, axis=-1, keepdims=True) + EPS)
        h = ((xv * inv) * nw_ref[...] * (1.0 + sc_ref[...]) + sh_ref[...]).astype(BF16)
        h_ref[...] = h
        u_ref[...] = _mm(h, w_ref[...])

    return pl.pallas_call(
        body, name="inproj_fwd", grid=(S // tm,),
        in_specs=[pl.BlockSpec((tm, D_MODEL), lambda i: (i, 0)), _spec(nw), _spec(scale), _spec(shift),
                  _full((D_MODEL, N_PAD)), pl.BlockSpec(memory_space=pl.ANY)],
        out_specs=[pl.BlockSpec((tm, N_PAD), lambda i: (i, 0)), pl.BlockSpec((tm, D_MODEL), lambda i: (i, 0))],
        out_shape=[SDS((S, N_PAD), F32), SDS((S, D_MODEL), BF16)],
        compiler_params=_cp(("parallel",)),
    )(x, _arr(nw), _arr(scale), _arr(shift), w, tok)


def _inproj_bwd_x(du, w, x, nw, scale, dxn, tok):
    S = x.shape[0]
    tm = min(256, S)

    def body(du_ref, w_ref, x_ref, nw_ref, sc_ref, dxn_ref, tok_ref, dx_ref, red_ref):
        del tok_ref

        @pl.when(pl.program_id(0) == 0)
        def _():
            red_ref[...] = jnp.zeros_like(red_ref)

        dh = _mm_nt(du_ref[...], w_ref[...])
        xv = x_ref[...]
        inv = lax.rsqrt(jnp.mean(xv * xv, axis=-1, keepdims=True) + EPS)
        xhat = xv * inv
        nwv = nw_ref[...]
        g1 = 1.0 + sc_ref[...]
        dxhat = dh * nwv * g1
        dx = inv * (dxhat - xhat * jnp.mean(dxhat * xhat, axis=-1, keepdims=True))
        dx_ref[...] = dxn_ref[...] + dx
        red_ref[0:1, :] += jnp.sum(dh, axis=0, keepdims=True)
        red_ref[1:2, :] += jnp.sum(dh * xhat * nwv, axis=0, keepdims=True)
        red_ref[2:3, :] += jnp.sum(dh * xhat * g1, axis=0, keepdims=True)

    row = pl.BlockSpec((tm, D_MODEL), lambda i: (i, 0))
    return pl.pallas_call(
        body, name="inproj_bwd_x", grid=(S // tm,),
        in_specs=[pl.BlockSpec((tm, N_PAD), lambda i: (i, 0)), _full((D_MODEL, N_PAD)), row, _spec(nw),
                  _spec(scale), row, pl.BlockSpec(memory_space=pl.ANY)],
        out_specs=[row, _full((8, D_MODEL))],
        out_shape=[SDS((S, D_MODEL), F32), SDS((8, D_MODEL), F32)],
        compiler_params=_cp(("arbitrary",)),
    )(du, w, x, _arr(nw), _arr(scale), dxn, tok)


def _inproj_bwd_w(h, du, tok):
    S = h.shape[0]
    tn = 640

    def body(h_ref, du_ref, tok_ref, gw_ref):
        del tok_ref
        gw_ref[...] = _mm_tn(h_ref[...], _bf(du_ref[...]))

    return pl.pallas_call(
        body, name="inproj_bwd_w", grid=(N_PAD // tn,),
        in_specs=[_full((S, D_MODEL)), pl.BlockSpec((S, tn), lambda j: (0, j)), pl.BlockSpec(memory_space=pl.ANY)],
        out_specs=pl.BlockSpec((D_MODEL, tn), lambda j: (0, j)),
        out_shape=SDS((D_MODEL, N_PAD), F32),
        compiler_params=_cp(("parallel",)),
    )(h, du, tok)


def _scan_block(a, b, rows):
    d = 1
    while d < a.shape[0]:
        a_s = _shift_down(a, d, rows, 1.0)
        b_s = _shift_down(b, d, rows, 0.0)
        b = a * b_s + b
        a = a * a_s
        d *= 2
    return a, b


def _rscan_block(c, g, rows):
    d = 1
    while d < c.shape[0]:
        c_s = _shift_up(c, d, rows, 1.0)
        g_s = _shift_up(g, d, rows, 0.0)
        g = g + c * g_s
        c = c * c_s
        d *= 2
    return c, g


LRU_BLOCK = 256


def _lru_gates(xa, wa_ref, ba_ref, wx_ref, bx_ref, lam_ref):
    sp = _softplus(-lam_ref[...])
    xb = _bf(xa)
    r = _sigmoid(_mm(xb, wa_ref[...]) + ba_ref[...])
    ig = _sigmoid(_mm(xb, wx_ref[...]) + bx_ref[...])
    la = -LRU_C * r * sp
    a = jnp.exp(la)
    mult = jnp.sqrt(-_expm1(2.0 * la))
    return sp, r, ig, la, a, mult


def _lru_specs(S, l):
    t128 = pl.BlockSpec((None, 1, LANE), lambda t: (l, 0, t))
    gate = pl.BlockSpec((None, None, LANE, LANE), lambda t: (l, t, 0, 0))
    return [pl.BlockSpec((S, 2 * LANE), lambda t: (0, OFF_LRU // (2 * LANE) + t)),
            pl.BlockSpec((None, 4, LANE), lambda t: (l, 0, t)), t128, gate, t128, gate, t128, t128]


def _lru_fwd(l, u, cw, cb, wa, ba, wx, bx, lam, ycat):
    S = u.shape[0]
    tb = min(LRU_BLOCK, S)

    def body(u_ref, cw_ref, cb_ref, wa_ref, ba_ref, wx_ref, bx_ref, lam_ref, ycat_in, ycat_ref, h_ref, a_scr, b_scr):
        del ycat_in
        rows = _iota((S, LANE), 0)
        xa = _conv_fwd(_f(u_ref[:, 0:LANE]), cw_ref, cb_ref, rows)
        _, _, ig, _, a, mult = _lru_gates(xa, wa_ref, ba_ref, wx_ref, bx_ref, lam_ref)
        a_scr[...] = a
        b_scr[...] = mult * (ig * xa)
        rows_b = _iota((tb, LANE), 0)

        def blk(j, hprev):
            sl = pl.ds(pl.multiple_of(j * tb, tb), tb)
            acum, hloc = _scan_block(a_scr[sl, :], b_scr[sl, :], rows_b)
            hf = hloc + acum * hprev
            h_ref[sl, :] = hf
            return _last_row(hf, rows_b)

        lax.fori_loop(0, S // tb, blk, jnp.zeros((1, LANE), F32))
        ycat_ref[...] = _bf(h_ref[...] * _silu(_f(u_ref[:, LANE:2 * LANE])))

    col = pl.BlockSpec((S, LANE), lambda t: (0, t))
    return pl.pallas_call(
        body, name="lru_fwd", grid=(LRU_W // LANE,),
        in_specs=_lru_specs(S, l) + [pl.BlockSpec(memory_space=pl.ANY)],
        out_specs=[col, col],
        out_shape=[SDS((S, D_INNER), BF16), SDS((S,LRU_W), F32)],
        scratch_shapes=[pltpu.VMEM((S, LANE), F32), pltpu.VMEM((S, LANE), F32)],
        input_output_aliases={8: 0},
        compiler_params=_cp(("parallel",)),
    )(u, cw, cb, wa, ba, wx, bx, lam, ycat)


def _lru_bwd(l, u, cw, cb, wa, ba, wx, bx, lam, h_lru, dycat, du):
    S = u.shape[0]
    tb = min(LRU_BLOCK, S)

    def body(u_ref, cw_ref, cb_ref, wa_ref, ba_ref, wx_ref, bx_ref, lam_ref, h_ref, dy_ref, du_in,
             du_ref, red_ref, gwa_ref, gwx_ref, c_scr, g_scr, l_scr):
        del du_in
        rows = _iota((S, LANE), 0)
        ax = _f(u_ref[:, 0:LANE])
        ag = _f(u_ref[:, LANE:2 * LANE])
        xa = _conv_fwd(ax, cw_ref, cb_ref, rows)
        sp, r, ig, la, a, mult = _lru_gates(xa, wa_ref, ba_ref, wx_ref, bx_ref, lam_ref)
        h = h_ref[...]
        dy = _f(dy_ref[...])
        du_ref[:, LANE:2 * LANE] = _bf(dy * h * _dsilu(ag))
        c_scr[...] = _shift_up(a, 1, rows, 0.0)
        g_scr[...] = dy * _silu(ag)
        rows_b = _iota((tb, LANE), 0)
        nb = S // tb

        def blk(jj, lnext):
            j = nb - 1 - jj
            sl = pl.ds(pl.multiple_of(j * tb, tb), tb)
            ccum, lloc = _rscan_block(c_scr[sl, :], g_scr[sl, :], rows_b)
            lam_t = lloc + ccum * lnext
            l_scr[sl, :] = lam_t
            return jnp.sum(jnp.where(rows_b == 0, lam_t, 0.0), axis=0, keepdims=True)

        lax.fori_loop(0, nb, blk, jnp.zeros((1, LANE), F32))
        db = l_scr[...]
        da = db * _shift_down(h, 1, rows)
        dmult = db * ig * xa
        dig = db * mult * xa
        dxa = db * mult * ig
        dla = da * a - dmult * (a * a) / mult
        dr = -LRU_C * sp * dla
        dsp = jnp.sum(-LRU_C * r * dla, axis=0, keepdims=True)
        dlam = -dsp * _sigmoid(-lam_ref[...])
        dzr = dr * r * (1.0 - r)
        dzi = dig * ig * (1.0 - ig)
        dzr_b, dzi_b, xa_b = _bf(dzr), _bf(dzi), _bf(xa)
        dxa = dxa + _mm_nt(dzr_b, wa_ref[...]) + _mm_nt(dzi_b, wx_ref[...])
        gwa_ref[...] = _mm_tn(xa_b, dzr_b)
        gwx_ref[...] = _mm_tn(xa_b, dzi_b)
        dax, dws, dcb = _conv_bwd(ax, dxa, cw_ref, rows)
        du_ref[:, 0:LANE] = _bf(dax)
        parts = dws + [dcb, jnp.sum(dzr, axis=0, keepdims=True), jnp.sum(dzi, axis=0, keepdims=True), dlam]
        for n, p in enumerate(parts):
            red_ref[pl.ds(n, 1), :] = p

    col = pl.BlockSpec((S, LANE), lambda t: (0, t))
    gw = pl.BlockSpec((None, LANE, LANE), lambda t: (t, 0, 0))
    return pl.pallas_call(
        body, name="lru_bwd", grid=(LRU_W // LANE,),
        in_specs=_lru_specs(S, l) + [col, col, pl.BlockSpec(memory_space=pl.ANY)],
        out_specs=[pl.BlockSpec((S, 2 * LANE), lambda t: (0, OFF_LRU // (2 * LANE) + t)),
                   pl.BlockSpec((8, LANE), lambda t: (0, t)), gw, gw],
        out_shape=[SDS((S, N_PAD), BF16), SDS((8, LRU_W), F32), SDS((4, LANE, LANE), F32), SDS((4, LANE, LANE), F32)],
        scratch_shapes=[pltpu.VMEM((S, LANE), F32)] * 3,
        input_output_aliases={10: 0},
        compiler_params=_cp(("parallel",)),
    )(u, cw, cb, wa, ba, wx, bx, lam, h_lru, dycat, du)


HG_LEVELS = 6


def _hg_consts():
    C = HG_CHUNK
    t = np.arange(C)[:, None]
    r = np.arange(C)[None, :]
    mats = []
    for l in range(HG_LEVELS):
        b = 1 << l
        upper = (t % (2 * b)) >= b
        anchor = (t // (2 * b)) * 2 * b + b - 1
        mats.append((upper & (r > anchor) & (r <= t)) | ((~upper) & (r > t) & (r <= anchor)))
    mats.append(r <= t)
    mats.append(r > t)
    return np.concatenate(mats, 0).astype(np.float32)


def _hg_factors(hf, lb, mall):
    s = _sigmoid(hf)
    f = lb + (1.0 - lb) * s
    lf = jnp.log(f)
    k = (1.0 - lb) * _sigmoid(-hf)
    e = jnp.exp(_sel_l(mall, lf))
    C = HG_CHUNK
    rows = _iota((C, HG_W), 0)
    eq, ek = [], []
    for l in range(HG_LEVELS):
        el = e[l * C:(l + 1) * C]
        eq.append(jnp.where((lax.shift_right_logical(rows, l) & 1) == 1, el, 0.0))
        ek.append(el - eq[l])
    ecum = e[HG_LEVELS * C:(HG_LEVELS + 1) * C]
    erem = e[(HG_LEVELS + 1) * C:(HG_LEVELS + 2) * C]
    return s, f, k, eq, ek, ecum, erem


def _hg_masks():
    C = HG_CHUNK
    ri, ci = _iota((C, C), 0), _iota((C, C), 1)
    rr = _iota((C, LANE), 0)
    gm = [(lax.shift_right_logical(ri, l + 1) == lax.shift_right_logical(ci, l + 1)).astype(F32)
          for l in range(HG_LEVELS)]
    up = [(lax.shift_right_logical(rr, l) & 1) == 1 for l in range(HG_LEVELS)]
    eye = (ri == ci).astype(F32)
    return gm, up, eye, rr


def _hg_scores(qh, kh, eq, ek, sl, gm, up, eye):
    del up
    qs, ks, qb, kb = [], [], [], []
    p = _mm_nt(_bf(qh), _bf(kh)) * eye
    for l in range(HG_LEVELS):
        qs.append(qh * eq[l][:, sl])
        ks.append(kh * ek[l][:, sl])
        qb.append(_bf(qs[l]))
        kb.append(_bf(ks[l]))
        p = p + _mm_nt(qb[l], kb[l]) * gm[l]
    return p, qs, ks, qb, kb


HG_SUB = 4


def _hg_fwd(u, lb, nw, mall, ycat):
    S = u.shape[0]
    C = HG_CHUNK
    n = S // C
    rows = HG_SUB * C

    def body(u_ref, lb_ref, nw_ref, mall_ref, ycat_in, ycat_ref, o_ref, st_ref, st):
        del ycat_in

        @pl.when(pl.program_id(0) == 0)
        def _():
            st[...] = jnp.zeros_like(st)

        gm, up, eye, rr = _hg_masks()
        for sub in range(HG_SUB):
            r = slice(sub * C, (sub + 1) * C)
            q = _silu(_f(u_ref[r, 0:512]))
            v = u_ref[r, 1024:1536]
            _, _, k, eq, ek, ecum, erem = _hg_factors(_f(u_ref[r, 512:1024]), lb_ref[...], mall_ref[...])
            for h in range(HG_HEADS):
                sl = slice(h * LANE, (h + 1) * LANE)
                qh, kh, vh = q[:, sl], k[:, sl], _bf(v[:, sl])
                p = _hg_scores(qh, kh, eq, ek, sl, gm, up, eye)[0]
                sth = st[h]
                st_ref[sub, h] = sth
                o_ref[r, sl] = _mm(_bf(p), vh) + _mm_nt(_bf(qh * ecum[:, sl]), _bf(sth))
                st[h] = sth * _last_row(ecum[:, sl], rr) + _mm_tn(vh, _bf(kh * erem[:, sl]))
            o = o_ref[r, :]
            inv = lax.rsqrt(jnp.mean(o * o, axis=-1, keepdims=True) + EPS)
            ycat_ref[r, :] = _bf((o * inv) * nw_ref[...] * _silu(_f(u_ref[r, 1536:2048])))

    return pl.pallas_call(
        body, name="hg_fwd", grid=(n // HG_SUB,),
        in_specs=[pl.BlockSpec((rows, 2048), lambda i: (i, 0)), _spec(lb), _spec(nw), _full(mall.shape),
                  pl.BlockSpec(memory_space=pl.ANY)],
        out_specs=[pl.BlockSpec((rows, HG_W), lambda i: (i, 1)), pl.BlockSpec((rows, HG_W), lambda i: (i, 0)),
                   pl.BlockSpec((HG_SUB, HG_HEADS, LANE, LANE), lambda i: (i, 0, 0, 0))],
        out_shape=[SDS((S, D_INNER), BF16), SDS((S,HG_W), F32), SDS((n, HG_HEADS, LANE, LANE), F32)],
        scratch_shapes=[pltpu.VMEM((HG_HEADS, LANE, LANE), F32)],
        input_output_aliases={4: 0},
        compiler_params=_cp(("arbitrary",)),
    )(u, _arr(lb), _arr(nw), mall, ycat)


def _hg_bwd(u, lb, nw, mall, mall_t, o_b, states, dycat, du):
    S = u.shape[0]
    C = HG_CHUNK
    n = S // C
    nb = n // HG_SUB
    rows = HG_SUB * C
    L2 = HG_LEVELS

    def body(u_ref, lb_ref, nw_ref, mall_ref, mallt_ref, o_ref, st_ref, dy_ref, du_in, du_ref, red_ref,
             dst, dlast_s, dq_s, dk_s, dex):
        del du_in

        @pl.when(pl.program_id(0) == 0)
        def _():
            dst[...] = jnp.zeros_like(dst)
            red_ref[...] = jnp.zeros_like(red_ref)

        lb = lb_ref[...]
        nwv = nw_ref[...]
        gm, up, eye, rr = _hg_masks()
        for sub in reversed(range(HG_SUB)):
            r = slice(sub * C, (sub + 1) * C)
            hq, hf, hg = _f(u_ref[r, 0:512]), _f(u_ref[r, 512:1024]), _f(u_ref[r, 1536:2048])
            q = _silu(hq)
            v = u_ref[r, 1024:1536]
            s, f, k, eq, ek, ecum, erem = _hg_factors(hf, lb, mall_ref[...])
            o = o_ref[r, :]
            dy = _f(dy_ref[r, :])
            inv = lax.rsqrt(jnp.mean(o * o, axis=-1, keepdims=True) + EPS)
            ohat = o * inv
            du_ref[r, 1536:2048] = _bf(dy * ohat * nwv * _dsilu(hg))
            dn = dy * _silu(hg)
            red_ref[0:1, :] += jnp.sum(dn * ohat, axis=0, keepdims=True)
            dohat = dn * nwv
            do = inv * (dohat - ohat * jnp.mean(dohat * ohat, axis=-1, keepdims=True))
            for h in range(HG_HEADS):
                sl = slice(h * LANE, (h + 1) * LANE)
                qh, kh, vh, doh = q[:, sl], k[:, sl], _bf(v[:, sl]), _bf(do[:, sl])
                p, qs, ks, qb, kb = _hg_scores(qh, kh, eq, ek, sl, gm, up, eye)
                st_f = st_ref[sub, h]
                sth = _bf(st_f)
                dsth = dst[h]
                dsth_b = _bf(dsth)
                qt = qh * ecum[:, sl]
                kt = kh * erem[:, sl]
                elast = _last_row(ecum[:, sl], rr)
                dp = _mm_nt(doh, vh)
                du_ref[r, 1024 + h * LANE:1024 + (h + 1) * LANE] = _bf(_mm_tn(_bf(p), doh) + _mm_nt(_bf(kt), dsth_b))
                dpe = _bf(dp * eye)
                dqt = _mm(doh, sth)
                dkt = _mm(vh, dsth_b)
                dq = dqt * ecum[:, sl] + _mm(dpe, _bf(kh))
                dk = dkt * erem[:, sl] + _mm_tn(dpe, _bf(qh))
                dex[sub, L2 * C:(L2 + 1) * C, sl] = dqt * qt
                dex[sub, (L2 + 1) * C:(L2 + 2) * C, sl] = dkt * kt
                for l in range(HG_LEVELS):
                    dpl = _bf(dp * gm[l])
                    dql = _mm(dpl, kb[l])
                    dkl = _mm_tn(dpl, qb[l])
                    dq = dq + dql * eq[l][:, sl]
                    dk = dk + dkl * ek[l][:, sl]
                    dex[sub, l * C:(l + 1) * C, sl] = dql * qs[l] + dkl * ks[l]
                dlast_s[sub, :, sl] = jnp.sum(dsth * st_f, axis=0, keepdims=True) * elast
                dst[h] = dsth * elast + _mm_tn(doh, _bf(qt))
                dq_s[sub, :, sl] = dq
                dk_s[sub, :, sl] = dk
            dq = dq_s[sub]
            dk = dk_s[sub]
            dlf = _sel_l2(mallt_ref[...], dex[sub]) + dlast_s[sub]
            du_ref[r, 0:512] = _bf(dq * _dsilu(hq))
            t = (1.0 - s) * (dlf / f - dk)
            du_ref[r, 512:1024] = _bf((1.0 - lb) * s * t)
            red_ref[1:2, :] += jnp.sum(t, axis=0, keepdims=True)

    rev = lambda i: (nb - 1 - i, 0)
    return pl.pallas_call(
        body, name="hg_bwd", grid=(nb,),
        in_specs=[pl.BlockSpec((rows, 2048), rev), _spec(lb), _spec(nw), _full(mall.shape), _full(mall_t.shape),
                  pl.BlockSpec((rows, HG_W), rev),
                  pl.BlockSpec((HG_SUB, HG_HEADS, LANE, LANE), lambda i: (nb - 1 - i, 0, 0, 0)),
                  pl.BlockSpec((rows, HG_W), lambda i: (nb - 1 - i, 1)), pl.BlockSpec(memory_space=pl.ANY)],
        out_specs=[pl.BlockSpec((rows, 2048), rev), pl.BlockSpec((8, HG_W), lambda i: (0, 0))],
        out_shape=[SDS((S, N_PAD), BF16), SDS((8, HG_W), F32)],
        scratch_shapes=[pltpu.VMEM((HG_HEADS, LANE, LANE), F32), pltpu.VMEM((HG_SUB, 1, HG_W), F32),
                        pltpu.VMEM((HG_SUB, C, HG_W), F32), pltpu.VMEM((HG_SUB, C, HG_W), F32),
                        pltpu.VMEM((HG_SUB, (L2 + 2) * C, HG_W), F32)],
        input_output_aliases={8: 0},
        compiler_params=_cp(("arbitrary",)),
    )(u, _arr(lb), _arr(nw), mall, mall_t, o_b, states, dycat, du)


def _ssdconv_fwd(l, u, cw, cb):
    S = u.shape[0]

    def body(u_ref, cw_ref, cb_ref, out_ref):
        rows = _iota((S, LANE), 0)
        out_ref[...] = _silu(_conv_fwd(_f(u_ref[...]), cw_ref, cb_ref, rows))

    return pl.pallas_call(
        body, name="ssdconv_fwd", grid=(SSD_CONV // LANE,),
        in_specs=[pl.BlockSpec((S, LANE), lambda t: (0, OFF_XBC // LANE + t)),
                  pl.BlockSpec((None, 4, LANE), lambda t: (l, 0, t)), pl.BlockSpec((None, 1, LANE), lambda t: (l, 0, t))],
        out_specs=pl.BlockSpec((S, LANE), lambda t: (0, t)),
        out_shape=SDS((S, SSD_CONV), F32),
        compiler_params=_cp(("parallel",)),
    )(u, cw, cb)


def _ssdconv_bwd(l, u, cw, cb, dxbc, du):
    S = u.shape[0]

    def body(u_ref, cw_ref, cb_ref, d_ref, du_in, du_ref, red_ref):
        del du_in
        rows = _iota((S, LANE), 0)
        x = _f(u_ref[...])
        dco = d_ref[...] * _dsilu(_conv_fwd(x, cw_ref, cb_ref, rows))
        dx, dws, dcb = _conv_bwd(x, dco, cw_ref, rows)
        du_ref[...] = _bf(dx)
        for n, p in enumerate(dws + [dcb]):
            red_ref[pl.ds(n, 1), :] = p
        red_ref[pl.ds(5, 3), :] = jnp.zeros((3, LANE), F32)

    ucol = pl.BlockSpec((S, LANE), lambda t: (0, OFF_XBC // LANE + t))
    return pl.pallas_call(
        body, name="ssdconv_bwd", grid=(SSD_CONV // LANE,),
        in_specs=[ucol, pl.BlockSpec((None, 4, LANE), lambda t: (l, 0, t)),
                  pl.BlockSpec((None, 1, LANE), lambda t: (l, 0, t)),
                  pl.BlockSpec((S, LANE), lambda t: (0, t)), pl.BlockSpec(memory_space=pl.ANY)],
        out_specs=[ucol, pl.BlockSpec((8, LANE), lambda t: (0, t))],
        out_shape=[SDS((S, N_PAD), BF16), SDS((8, SSD_CONV), F32)],
        input_output_aliases={4: 0},
        compiler_params=_cp(("parallel",)),
    )(u, cw, cb, dxbc, du)


def _ssd_consts():
    e64 = np.zeros((LANE, SSD_W), np.float32)
    for h in range(SSD_HEADS):
        e64[h, h * SSD_P:(h + 1) * SSD_P] = 1.0
    T = SSD_CHUNK
    tril = (np.arange(T)[None, :] <= np.arange(T)[:, None]).astype(np.float32)
    return e64, tril, tril.T.copy()


def _ssd_common(zdt, bias_ref, alog_ref, tril, e64, cum_ref, cumt_ref):
    T = SSD_CHUNK
    lane = _iota((1, LANE), 1)
    a_neg = jnp.where(lane < SSD_HEADS, -jnp.exp(alog_ref[...]), 0.0)
    dtpre = zdt[:, SSD_W:SSD_W + LANE] + bias_ref[...]
    dt = _softplus(dtpre)
    cum = _sel_l(tril, dt * a_neg)
    cum_ref[...] = cum
    cumt_ref[...] = cum.T
    cum_x = _sel_r(cum, e64)
    last_x = _last_row(cum_x, _iota((T, SSD_W), 0))
    ecum_x = jnp.exp(cum_x)
    erem_x = jnp.exp(last_x - cum_x)
    elast_x = jnp.exp(last_x)
    dt_x = _sel_r(dt, e64)
    return a_neg, dtpre, dt, ecum_x, erem_x, elast_x, dt_x


def _ssd_decay(cum_ref, cumt_ref, h, causal):
    T = SSD_CHUNK
    diff = jnp.broadcast_to(cum_ref[:, pl.ds(h, 1)], (T, T)) - cumt_ref[pl.ds(h, 1), :]
    return jnp.exp(jnp.where(causal, diff, NEG))


def _group_norm_fwd(y1, nwv):
    outs, invs = [], []
    for g in range(2):
        seg = y1[:, g * 512:(g + 1) * 512]
        inv = lax.rsqrt(jnp.mean(seg * seg, axis=-1, keepdims=True) + EPS)
        outs.append(seg * inv * nwv[:, g * 512:(g + 1) * 512])
        invs.append(inv)
    return outs, invs


def _ssd_fwd(u, xbc, bias, alog, dskip_x, nw, consts, ycat):
    S = u.shape[0]
    T = SSD_CHUNK
    n = S // T
    e64, tril, _ = consts

    def body(u_ref, xbc_ref, bias_ref, alog_ref, dx_ref, nw_ref, e64_ref, tril_ref, ycat_in,
             ycat_ref, y_ref, st_ref, st, cumt, cum_e):
        del ycat_in

        @pl.when(pl.program_id(0) == 0)
        def _():
            st[...] = jnp.zeros_like(st)

        zdt = _f(u_ref[...])
        z = zdt[:, 0:SSD_W]
        xs = xbc_ref[:, 0:SSD_W]
        _, _, _, ecum_x, erem_x, elast_x, dt_x = _ssd_common(
            zdt, bias_ref, alog_ref, tril_ref[...], e64_ref[...], cum_e, cumt)
        causal = _iota((T, T), 0) >= _iota((T, T), 1)
        lo = _iota((T, LANE), 1) < SSD_P
        xdt = xs * dt_x
        xrem = xdt * erem_x
        st_ref[...] = st[...]
        for g in range(2):
            gs = slice(g * 512, (g + 1) * 512)
            bg = _bf(xbc_ref[:, SSD_W + g * LANE:SSD_W + (g + 1) * LANE])
            cg = _bf(xbc_ref[:, SSD_W + 256 + g * LANE:SSD_W + 256 + (g + 1) * LANE])
            cb = _mm_nt(cg, bg)
            yin = _mm(cg, _bf(st[:, gs])) * ecum_x[:, gs]
            for j in range(4):
                h0 = 8 * g + 2 * j
                cs = slice(h0 * SSD_P, (h0 + 2) * SSD_P)
                xp = xdt[:, cs]
                s0 = _bf(cb * _ssd_decay(cum_e, cumt, h0, causal))
                s1 = _bf(cb * _ssd_decay(cum_e, cumt, h0 + 1, causal))
                y_ref[:, cs] = (_mm(s0, _bf(jnp.where(lo, xp, 0.0))) + _mm(s1, _bf(jnp.where(lo, 0.0, xp)))
                                + yin[:, j * LANE:(j + 1) * LANE])
            st[:, gs] = st[:, gs] * elast_x[:, gs] + _mm_tn(bg, _bf(xrem[:, gs]))
        y1 = (y_ref[...] + dx_ref[...] * xs) * _silu(z)
        outs, _ = _group_norm_fwd(y1, nw_ref[...])
        for g in range(2):
            ycat_ref[:, g * 512:(g + 1) * 512] = _bf(outs[g])

    return pl.pallas_call(
        body, name="ssd_fwd", grid=(n,),
        in_specs=[pl.BlockSpec((T, SSD_W + LANE), lambda i: (i, OFF_Z // (SSD_W + LANE))),
                  pl.BlockSpec((T, SSD_CONV), lambda i: (i, 0)), _spec(bias), _spec(alog), _spec(dskip_x), _spec(nw),
                  _full(e64.shape), _full(tril.shape), pl.BlockSpec(memory_space=pl.ANY)],
        out_specs=[pl.BlockSpec((T, SSD_W), lambda i: (i, 1)), pl.BlockSpec((T, SSD_W), lambda i: (i, 0)),
                   pl.BlockSpec((None, SSD_N, SSD_W), lambda i: (i, 0, 0))],
        out_shape=[SDS((S, D_INNER), BF16), SDS((S,SSD_W), F32), SDS((n, SSD_N, SSD_W), F32)],
        scratch_shapes=[pltpu.VMEM((SSD_N, SSD_W), F32), pltpu.VMEM((LANE, T), F32), pltpu.VMEM((T, LANE), F32)],
        input_output_aliases={8: 0},
        compiler_params=_cp(("arbitrary",)),
    )(u, xbc, _arr(bias), _arr(alog), _arr(dskip_x), _arr(nw), _bfc(e64), _bfc(tril), ycat)


def _ssd_bwd(u, xbc, bias, alog, dskip_x, nw, consts, y_ssd, states, dycat, du, tok):
    S = u.shape[0]
    T = SSD_CHUNK
    n = S // T
    e64, tril, triu = consts
    e64t = np.ascontiguousarray(e64.T)

    def body(u_ref, xbc_ref, bias_ref, alog_ref, dx_ref, nw_ref, e64_ref, e64t_ref, tril_ref, triu_ref,
             y_ref, st_ref, dy_ref, du_in, tok_ref, du_ref, dxbc_ref, red_ref, dst, dl_s, cumt, dxdt_s, dy0_s, gb_s,
             gc_s, cum_e, cs_s):
        del du_in, tok_ref

        @pl.when(pl.program_id(0) == 0)
        def _():
            dst[...] = jnp.zeros_like(dst)
            red_ref[...] = jnp.zeros_like(red_ref)
            cs_s[...] = jnp.zeros_like(cs_s)

        zdt = _f(u_ref[...])
        z = zdt[:, 0:SSD_W]
        xs = xbc_ref[:, 0:SSD_W]
        a_neg, dtpre, dt, ecum_x, erem_x, elast_x, dt_x = _ssd_common(
            zdt, bias_ref, alog_ref, tril_ref[...], e64_ref[...], cum_e, cumt)
        causal = _iota((T, T), 0) >= _iota((T, T), 1)
        lo = _iota((T, LANE), 1) < SSD_P
        xdt = xs * dt_x
        xrem = xdt * erem_x
        y = y_ref[...]
        dxv = dx_ref[...]
        nwv = nw_ref[...]
        sz = _silu(z)
        y0 = y + dxv * xs
        y1 = y0 * sz
        for g in range(2):
            gs = slice(g * 512, (g + 1) * 512)
            seg = y1[:, gs]
            inv = lax.rsqrt(jnp.mean(seg * seg, axis=-1, keepdims=True) + EPS)
            shat = seg * inv
            dyg = _f(dy_ref[:, gs])
            red_ref[0:1, gs] += jnp.sum(dyg * shat, axis=0, keepdims=True)
            dsh = dyg * nwv[:, gs]
            dy1g = inv * (dsh - shat * jnp.mean(dsh * shat, axis=-1, keepdims=True))
            du_ref[:, gs] = _bf(dy1g * y0[:, gs] * _dsilu(z[:, gs]))
            dy0_s[:, gs] = dy1g * sz[:, gs]
        dy0 = dy0_s[...]
        red_ref[1:2, :] += jnp.sum(dy0 * xs, axis=0, keepdims=True)
        dyin = dy0 * ecum_x
        lane = _iota((T, LANE), 1)
        dcum = jnp.zeros((T, LANE), F32)

        def decay_grad(h, gm):
            cs_s[pl.ds(h, 1), :] = jnp.sum(gm, axis=0, keepdims=True)
            return jnp.where(lane == h, jnp.sum(gm, axis=1, keepdims=True), 0.0)

        for g in range(2):
            gs = slice(g * 512, (g + 1) * 512)
            bg = _bf(xbc_ref[:, SSD_W + g * LANE:SSD_W + (g + 1) * LANE])
            cg = _bf(xbc_ref[:, SSD_W + 256 + g * LANE:SSD_W + 256 + (g + 1) * LANE])
            cb = _mm_nt(cg, bg)
            dst_f, st_f = dst[:, gs], st_ref[:, gs]
            dstg = _bf(dst_f)
            stg = _bf(st_f)
            dyin_g = _bf(dyin[:, gs])
            xrem_g = _bf(xrem[:, gs])
            dcb = jnp.zeros((T, T), F32)
            dxr = _mm(bg, dstg)
            dxdt_s[:, gs] = dxr * erem_x[:, gs]
            gc_s[:, gs] = dxr * xrem[:, gs]
            gb_s[:, gs] = dyin[:, gs] * _mm(cg, stg)
            dl_s[:, gs] = jnp.sum(dst_f * st_f, axis=0, keepdims=True) * elast_x[:, gs]
            for j in range(4):
                h0 = 8 * g + 2 * j
                cs = slice(h0 * SSD_P, (h0 + 2) * SSD_P)
                xp = xdt[:, cs]
                dyp = dy0[:, cs]
                x_lo, x_hi = _bf(jnp.where(lo, xp, 0.0)), _bf(jnp.where(lo, 0.0, xp))
                d_lo, d_hi = _bf(jnp.where(lo, dyp, 0.0)), _bf(jnp.where(lo, 0.0, dyp))
                l0 = _ssd_decay(cum_e, cumt, h0, causal)
                l1 = _ssd_decay(cum_e, cumt, h0 + 1, causal)
                s0 = cb * l0
                s1 = cb * l1
                ds0 = _mm_nt(d_lo, x_lo)
                ds1 = _mm_nt(d_hi, x_hi)
                dcb = dcb + ds0 * l0 + ds1 * l1
                dxdt_s[:, cs] += _mm_tn(_bf(s0), d_lo) + _mm_tn(_bf(s1), d_hi)
                dcum = dcum + decay_grad(h0, ds0 * s0) + decay_grad(h0 + 1, ds1 * s1)
            dcb_b = _bf(dcb)
            dxbc_ref[:, SSD_W + g * LANE:SSD_W + (g + 1) * LANE] = _mm_tn(dcb_b, cg) + _mm_nt(xrem_g, dstg)
            dxbc_ref[:, SSD_W + 256 + g * LANE:SSD_W + 256 + (g + 1) * LANE] = _mm(dcb_b, bg) + _mm_nt(dyin_g, stg)
            dst[:, gs] = dst_f * elast_x[:, gs] + _mm_tn(cg, dyin_g)
        dxdt = dxdt_s[...]
        dxbc_ref[:, 0:SSD_W] = dxdt * dt_x + dy0 * dxv
        e64t = e64t_ref[...]
        gc = gc_s[...]
        dlast_x = jnp.sum(gc, axis=0, keepdims=True) + dl_s[...]
        dlast = jnp.max(_sel_r(jnp.broadcast_to(dlast_x, (8, SSD_W)), e64t), axis=0, keepdims=True)
        dcum = (dcum - cs_s[...].T + _sel_r(gb_s[...] - gc, e64t)
                + jnp.where(_iota((T, LANE), 0) == T - 1, dlast, 0.0))
        dda = _sel_l(triu_ref[...], dcum)
        ddt = dda * a_neg + _sel_r(dxdt * xs, e64t)
        ddtpre = ddt * _sigmoid(dtpre)
        du_ref[:, SSD_W:SSD_W + LANE] = _bf(jnp.where(lane < SSD_HEADS, ddtpre, 0.0))
        red_ref[2:3, 0:LANE] += jnp.sum(ddtpre, axis=0, keepdims=True)
        red_ref[3:4, 0:LANE] += jnp.sum(dda * dt, axis=0, keepdims=True)

    rev = lambda i: (n - 1 - i, 0)
    return pl.pallas_call(
        body, name="ssd_bwd", grid=(n,),
        in_specs=[pl.BlockSpec((T, SSD_W + LANE), lambda i: (n - 1 - i, OFF_Z // (SSD_W + LANE))),
                  pl.BlockSpec((T, SSD_CONV), rev), _spec(bias), _spec(alog), _spec(dskip_x), _spec(nw),
                  _full(e64.shape), _full(e64t.shape), _full(tril.shape), _full(triu.shape),
                  pl.BlockSpec((T, SSD_W), rev), pl.BlockSpec((None, SSD_N, SSD_W), lambda i: (n - 1 - i, 0, 0)),
                  pl.BlockSpec((T, SSD_W), lambda i: (n - 1 - i, 1)), pl.BlockSpec(memory_space=pl.ANY),
                  pl.BlockSpec(memory_space=pl.ANY)],
        out_specs=[pl.BlockSpec((T, SSD_W + LANE), lambda i: (n - 1 - i, OFF_Z // (SSD_W + LANE))),
                   pl.BlockSpec((T, SSD_CONV), rev), pl.BlockSpec((8, SSD_W), lambda i: (0, 0))],
        out_shape=[SDS((S, N_PAD), BF16), SDS((S, SSD_CONV), F32), SDS((8, SSD_W), F32)],
        scratch_shapes=[pltpu.VMEM((SSD_N, SSD_W), F32), pltpu.VMEM((1, SSD_W), F32), pltpu.VMEM((LANE, T), F32)]
        + [pltpu.VMEM((T, SSD_W), F32)] * 4 + [pltpu.VMEM((T, LANE), F32), pltpu.VMEM((LANE, T), F32)],
        input_output_aliases={13: 0},
        compiler_params=_cp(("arbitrary",)),
    )(u, xbc, _arr(bias), _arr(alog), _arr(dskip_x), _arr(nw), _bfc(e64), _bfc(e64t), _bfc(tril), _bfc(triu), y_ssd,
      states, dycat, du, tok)


def _bfc(a):
    return jnp.asarray(a, BF16)


def _outproj_fwd(ycat, wo, x, gate, tok):
    S = x.shape[0]
    tm = min(512, S)

    def body(yc_ref, wo_ref, x_ref, g_ref, tok_ref, xn_ref, y_ref):
        del tok_ref
        y = _mm(_bf(yc_ref[...]), wo_ref[...])
        y_ref[...] = y
        xn_ref[...] = x_ref[...] + g_ref[...] * y

    row = pl.BlockSpec((tm, D_MODEL), lambda i: (i, 0))
    return pl.pallas_call(
        body, name="outproj_fwd", grid=(S // tm,),
        in_specs=[pl.BlockSpec((tm, D_INNER), lambda i: (i, 0)), _full((D_INNER, D_MODEL)), row, _spec(gate),
                  pl.BlockSpec(memory_space=pl.ANY)],
        out_specs=[row, row],
        out_shape=[SDS((S, D_MODEL), F32), SDS((S, D_MODEL), F32)],
        compiler_params=_cp(("parallel",)),
    )(ycat, wo, x, _arr(gate), tok)


def _outproj_bwd(dxn, y, gate, ycat, wo):
    S = dxn.shape[0]
    tm = min(512, S)

    def body(dx_ref, y_ref, g_ref, yc_ref, wo_ref, dyc_ref, gwo_ref, dg_ref, acc):
        @pl.when(pl.program_id(0) == 0)
        def _():
            acc[...] = jnp.zeros_like(acc)
            dg_ref[...] = jnp.zeros_like(dg_ref)

        dxv = dx_ref[...]
        dy = _bf(dxv * g_ref[...])
        dg_ref[0:1, :] += jnp.sum(dxv * y_ref[...], axis=0, keepdims=True)
        dyc_ref[...] = _mm_nt(dy, wo_ref[...])
        acc[...] += _mm_tn(_bf(yc_ref[...]), dy)

        @pl.when(pl.program_id(0) == pl.num_programs(0) - 1)
        def _():
            gwo_ref[...] = acc[...].astype(BF16)

    row = pl.BlockSpec((tm, D_MODEL), lambda i: (i, 0))
    wide = pl.BlockSpec((tm, D_INNER), lambda i: (i, 0))
    return pl.pallas_call(
        body, name="outproj_bwd", grid=(S // tm,),
        in_specs=[row, row, _spec(gate), wide, _full((D_INNER, D_MODEL))],
        out_specs=[wide, _full((D_INNER, D_MODEL)), _full((8, D_MODEL))],
        out_shape=[SDS((S, D_INNER), F32), SDS((D_INNER, D_MODEL), BF16), SDS((8, D_MODEL), F32)],
        scratch_shapes=[pltpu.VMEM((D_INNER, D_MODEL), F32)],
        compiler_params=_cp(("arbitrary",)),
    )(dxn, y, _arr(gate), ycat, wo)


def _loss_head(x, fw, target):
    S = x.shape[0]
    tm = min(512, S)

    def body(x_ref, fw_ref, t_ref, dx_ref, red_ref):
        @pl.when(pl.program_id(0) == 0)
        def _():
            red_ref[...] = jnp.zeros_like(red_ref)

        xv = x_ref[...]
        fwv = fw_ref[...]
        inv = lax.rsqrt(jnp.mean(xv * xv, axis=-1, keepdims=True) + EPS)
        xhat = xv * inv
        err = xhat * fwv - t_ref[...]
        col = jnp.sum(err * err, axis=0, keepdims=True)
        red_ref[1:2, :] += jnp.broadcast_to(jnp.sum(col, axis=1, keepdims=True) * (0.5 / D_MODEL), (1, D_MODEL))
        dy = err * (1.0 / D_MODEL)
        red_ref[0:1, :] += jnp.sum(dy * xhat, axis=0, keepdims=True)
        dxhat = dy * fwv
        dx_ref[...] = inv * (dxhat - xhat * jnp.mean(dxhat * xhat, axis=-1, keepdims=True))

    row = pl.BlockSpec((tm, D_MODEL), lambda i: (i, 0))
    return pl.pallas_call(
        body, name="loss_head", grid=(S // tm,),
        in_specs=[row, _vec(D_MODEL), row],
        out_specs=[row, _full((8, D_MODEL))],
        out_shape=[SDS((S, D_MODEL), F32), SDS((8, D_MODEL), F32)],
        compiler_params=_cp(("arbitrary",)),
    )(x, fw, target)


ADA_COLS = 3 * D_MODEL // N_DEV


def _ada_fwd(c_all, w_ada, b_cols):
    def body(c_ref, w_ref, b_ref, out_ref):
        out_ref[...] = _mm(_bf(_silu(c_ref[...])), _bf(w_ref[...])) + b_ref[...]

    return pl.pallas_call(
        body, name="ada_fwd", grid=(DEPTH,),
        in_specs=[_full((N_DEV, D_MODEL)), pl.BlockSpec((None, D_MODEL, ADA_COLS), lambda l: (l, 0, 0)),
                  pl.BlockSpec((None, 1, ADA_COLS), lambda l: (l, 0, 0))],
        out_specs=pl.BlockSpec((None, N_DEV, ADA_COLS), lambda l: (l, 0, 0)),
        out_shape=SDS((DEPTH, N_DEV, ADA_COLS), F32),
        compiler_params=_cp(("parallel",)),
    )(c_all, w_ada, b_cols)


def _ada_bwd(ct_pad, dmod_pad):
    def body(c_ref, d_ref, out_ref):
        out_ref[...] = _mm(_bf(_silu(c_ref[...])), _bf(d_ref[...]))

    return pl.pallas_call(
        body, name="ada_bwd", grid=(DEPTH,),
        in_specs=[_full((D_MODEL, LANE)), pl.BlockSpec((None, LANE, ADA_COLS), lambda l: (l, 0, 0))],
        out_specs=pl.BlockSpec((None, D_MODEL, ADA_COLS), lambda l: (l, 0, 0)),
        out_shape=SDS((DEPTH, D_MODEL, ADA_COLS), F32),
        compiler_params=_cp(("parallel",)),
    )(ct_pad, dmod_pad)


def _adamw(parts, w, m, v, name, own=None, layers=None, prev=None):
    n, L, R, C = parts.shape
    lo, hi = layers or (0, L)
    tr = R
    while tr * C * 4 > (1 << 20) and tr % 16 == 0:
        tr //= 2
    first = 1 if own is None else 2

    def body(*refs):
        p_ref = refs[0]
        w_ref, m_ref, v_ref = refs[first:first + 3]
        g_ref, d_ref, mo_ref, vo_ref = refs[-4:]

        def part(k):
            if own is None:
                return p_ref[k].astype(F32)
            me = 4 * lax.axis_index("x") + 2 * lax.axis_index("y") + lax.axis_index("c")
            return jnp.where(me == k, refs[1][...], p_ref[k]).astype(F32)

        g = part(0)
        for k in range(1, n):
            g = g + part(k)
        mn = ADAM_B1 * m_ref[...] + (1.0 - ADAM_B1) * g
        vn = ADAM_B2 * v_ref[...] + (1.0 - ADAM_B2) * (g * g)
        m_hat = mn / (1.0 - ADAM_B1 ** ADAM_STEP)
        v_hat = vn / (1.0 - ADAM_B2 ** ADAM_STEP)
        g_ref[...] = g
        d_ref[...] = -ADAM_LR * (m_hat / (jnp.sqrt(v_hat) + ADAM_EPS) + ADAM_WD * w_ref[...])
        mo_ref[...] = mn
        vo_ref[...] = vn

    blk = pl.BlockSpec((None, tr, C), lambda l, i: (lo + l, i, 0))
    own_blk = [] if own is None else [pl.BlockSpec((None, tr, C), lambda l, i: (l, i, 0))]
    n_blk = 3 if own is None else 4
    return pl.pallas_call(
        body, name=name, grid=(hi - lo, R // tr),
        in_specs=[pl.BlockSpec((n, None, tr, C), lambda l, i: (0, lo + l, i, 0))] + own_blk + [blk] * 3
        + ([] if prev is None else [ANY] * 4),
        out_specs=[blk] * 4,
        out_shape=[SDS((L, R, C), F32)] * 4,
        input_output_aliases={} if prev is None else {1 + n_blk + k: k for k in range(4)},
        compiler_params=_cp(("parallel", "parallel")),
    )(parts, *([] if own is None else [own]), w, m, v, *([] if prev is None else prev))


def _adamw_small(parts, own, a):
    table, _ = _small_rows({n: a[n] for n in SMALL})

    def two_d(t):
        return t.reshape(-1, SMALL_ROW) if t.size % SMALL_ROW == 0 else t.reshape(1, t.size)

    ins = [two_d(a[p + n]) for n in SMALL for p in ("", "m_", "v_")]
    shapes = [ins[3 * i].shape for i in range(len(SMALL))]

    def body(p_ref, own_ref, *refs):
        me = 4 * lax.axis_index("x") + 2 * lax.axis_index("y") + lax.axis_index("c")
        for i, n in enumerate(SMALL):
            off, (r, c) = table[n][0], shapes[i]
            w_ref, m_ref, v_ref = refs[3 * i:3 * i + 3]
            g_ref, d_ref, mo_ref, vo_ref = refs[3 * len(SMALL) + 4 * i:3 * len(SMALL) + 4 * i + 4]
            g = jnp.zeros((r, c), F32)
            for k in range(N_DEV):
                g = g + jnp.where(me == k, own_ref[off:off + r, 0:c], p_ref[k, off:off + r, 0:c])
            mn = ADAM_B1 * m_ref[...] + (1.0 - ADAM_B1) * g
            vn = ADAM_B2 * v_ref[...] + (1.0 - ADAM_B2) * (g * g)
            m_hat = mn / (1.0 - ADAM_B1 ** ADAM_STEP)
            v_hat = vn / (1.0 - ADAM_B2 ** ADAM_STEP)
            g_ref[...] = g
            d_ref[...] = -ADAM_LR * (m_hat / (jnp.sqrt(v_hat) + ADAM_EPS) + ADAM_WD * w_ref[...])
            mo_ref[...] = mn
            vo_ref[...] = vn

    return pl.pallas_call(
        body, name="adamw_small",
        out_shape=[SDS(s, F32) for s in shapes for _ in range(4)],
        compiler_params=pltpu.CompilerParams(vmem_limit_bytes=VMEM_LIMIT),
    )(parts, own, *ins)


MESH = pl.DeviceIdType.MESH
ANY = pl.BlockSpec(memory_space=pl.ANY)


def _all_gather(v, name):
    def body(v_ref, out_ref, send_sems, recv_sems, local_sem):
        x, y, c = lax.axis_index("x"), lax.axis_index("y"), lax.axis_index("c")
        me, sibling = (x, y, c), (x, y, 1 - c)
        chips = [(1 - x, y), (x, 1 - y), (1 - x, 1 - y)]

        def slot(px, py, pc):
            return out_ref.at[4 * px + 2 * py + pc]

        def copy(k, block, to, src=None):
            return pltpu.make_async_remote_copy(
                src_ref=slot(*block) if src is None else src, dst_ref=slot(*block),
                send_sem=send_sems.at[k], recv_sem=recv_sems.at[k], device_id=to, device_id_type=MESH)

        mine = pltpu.make_async_copy(v_ref, slot(*me), local_sem)
        mine.start()
        first = [copy(0, me, sibling, src=v_ref)]
        first += [copy(1 + j, me, (*chip, c), src=v_ref) for j, chip in enumerate(chips)]
        for cp in first:
            cp.start()
        passed = [copy(4 + j, (*chip, c), sibling) for j, chip in enumerate(chips)]
        for j, chip in enumerate(chips):
            copy(1 + j, (*chip, c), me).wait_recv()
            passed[j].start()
        copy(0, sibling, me).wait_recv()
        for j, chip in enumerate(chips):
            copy(4 + j, (*chip, 1 - c), me).wait_recv()
        for cp in first + passed:
            cp.wait_send()
        mine.wait()

    return pl.pallas_call(
        body, name=name, in_specs=[ANY], out_specs=ANY,
        out_shape=SDS((N_DEV,) + v.shape, v.dtype),
        scratch_shapes=[pltpu.SemaphoreType.DMA((7,)), pltpu.SemaphoreType.DMA((7,)), pltpu.SemaphoreType.DMA],
    )(v)


def _all_to_all(v, name):
    def body(v_ref, out_ref, send_sems, recv_sems, local_sem):
        x, y, c = lax.axis_index("x"), lax.axis_index("y"), lax.axis_index("c")
        mine_idx = 4 * x + 2 * y + c
        mine = pltpu.make_async_copy(v_ref.at[mine_idx], out_ref.at[mine_idx], local_sem)
        mine.start()
        sends, recvs = [], []
        for k in range(1, N_DEV):
            px = 1 - x if k & 4 else x
            py = 1 - y if k & 2 else y
            pc = 1 - c if k & 1 else c
            peer_idx = 4 * px + 2 * py + pc
            sems = dict(send_sem=send_sems.at[k - 1], recv_sem=recv_sems.at[k - 1], device_id=(px, py, pc),
                        device_id_type=MESH)
            sends.append(pltpu.make_async_remote_copy(src_ref=v_ref.at[peer_idx], dst_ref=out_ref.at[mine_idx], **sems))
            recvs.append(pltpu.make_async_remote_copy(src_ref=v_ref.at[peer_idx], dst_ref=out_ref.at[peer_idx], **sems))
        for cp in sends:
            cp.start()
        for cp in recvs:
            cp.wait_recv()
        for cp in sends:
            cp.wait_send()
        mine.wait()

    return pl.pallas_call(
        body, name=name, in_specs=[ANY], out_specs=ANY,
        out_shape=SDS(v.shape, v.dtype),
        scratch_shapes=[pltpu.SemaphoreType.DMA((7,)), pltpu.SemaphoreType.DMA((7,)), pltpu.SemaphoreType.DMA],
    )(v)


HBM_SPEC = pl.BlockSpec(memory_space=pltpu.HBM)
SEM_SPEC = pl.BlockSpec(memory_space=pltpu.SEMAPHORE)
EFFECT = pltpu.SideEffectType.DATAFLOW_SIDE_EFFECTING


EXCHANGE_PEERS = {"gather": range(1, N_DEV), "scatter": range(1, N_DEV), "chip": (1, 2, 4, 6), "pass": (2, 4, 6)}


def _exchange_copies(srcs, lands, send_sems, recv_sems, mode, layer):
    x, y, c = lax.axis_index("x"), lax.axis_index("y"), lax.axis_index("c")
    me = 4 * x + 2 * y + c
    copies = []
    for a, (src, land) in enumerate(zip(srcs, lands)):
        for k in EXCHANGE_PEERS[mode]:
            px = 1 - x if k & 4 else x
            py = 1 - y if k & 2 else y
            pc = 1 - c if k & 1 else c
            peer = 4 * px + 2 * py + pc
            if mode == "scatter":
                s, d, to = src.at[peer], land.at[me, layer], (px, py, pc)
            elif mode == "pass":
                s, d, to = land.at[peer], land.at[peer], (x, y, 1 - c)
            else:
                s, d, to = src, land.at[me], (px, py, pc)
            n = 7 * a + k - 1
            copies.append(pltpu.make_async_remote_copy(
                src_ref=s, dst_ref=d, send_sem=send_sems.at[n], recv_sem=recv_sems.at[n], device_id=to,
                device_id_type=MESH))
    return copies


def _exchange_start(name, srcs, lands, mode, layer=0, after=None):
    n = len(srcs)

    def body(*refs):
        send_sems, recv_sems = refs[-2 * n - 3], refs[-2 * n - 2]
        for cp in _exchange_copies(refs[:n], refs[n:2 * n], send_sems, recv_sems, mode, layer):
            cp.start()
        refs[-1][...] = jnp.zeros_like(refs[-1])

    arrays = list(srcs) + list(lands)
    sems = pltpu.SemaphoreType.DMA((7 * n,))
    out = pl.pallas_call(
        body, name=name,
        out_shape=(sems, sems, *[pltpu.HBM(v.shape, v.dtype) for v in arrays], SDS((8, LANE), F32)),
        in_specs=[HBM_SPEC] * (2 * n) + ([ANY] if after is not None else []),
        out_specs=(SEM_SPEC, SEM_SPEC, *[HBM_SPEC] * (2 * n), pl.BlockSpec(memory_space=pltpu.VMEM)),
        input_output_aliases={i: 2 + i for i in range(2 * n)},
        compiler_params=pltpu.CompilerParams(has_side_effects=EFFECT),
    )(*[pltpu.with_memory_space_constraint(v, pltpu.HBM) for v in arrays], *([after] if after is not None else []))
    return dict(sems=out[:2], srcs=out[2:2 + n], lands=out[2 + n:2 + 2 * n], token=out[-1], mode=mode,
                layer=layer)


def _exchange_wait(name, st, after, also=()):
    n = len(st["srcs"])

    def body(*refs):
        send_sems, recv_sems = refs[2 * n], refs[2 * n + 1]
        for cp in _exchange_copies(refs[:n], refs[n:2 * n], send_sems, recv_sems, st["mode"], st["layer"]):
            cp.wait_send()
            cp.wait_recv()

    arrays = list(st["srcs"]) + list(st["lands"])
    out = pl.pallas_call(
        body, name=name,
        out_shape=tuple(pltpu.HBM(v.shape, v.dtype) for v in arrays),
        in_specs=[HBM_SPEC] * (2 * n) + [SEM_SPEC, SEM_SPEC] + [ANY] * (1 + len(also)),
        out_specs=tuple([HBM_SPEC] * (2 * n)),
        input_output_aliases={i: i for i in range(2 * n)},
        compiler_params=pltpu.CompilerParams(has_side_effects=EFFECT),
    )(*arrays, *st["sems"], after, *also)
    st["srcs"] = out[:n]
    return out[n:]


_IN_PIECES = ([(1024, 3072)]
              + [r for t in range(4) for r in ((LANE * t, LANE * (t + 1)), (512 + LANE * t, 512 + LANE * (t + 1)))]
              + [(4096, 5632), (3072, 4096), (5632, 5648)])


def _permute_in(w):
    pad = jnp.zeros(w.shape[:-1] + (N_PAD - N_IN,), w.dtype)
    return jnp.concatenate([w[..., a:b] for a, b in _IN_PIECES] + [pad], axis=-1)


def _unpermute_in(g):
    ax = [g[..., OFF_LRU + 2 * LANE * t:OFF_LRU + 2 * LANE * t + LANE] for t in range(4)]
    ag = [g[..., OFF_LRU + 2 * LANE * t + LANE:OFF_LRU + 2 * LANE * (t + 1)] for t in range(4)]
    return jnp.concatenate(ax + ag + [g[..., 0:2048], g[..., OFF_Z:OFF_Z + SSD_W], g[..., OFF_XBC:OFF_XBC + SSD_CONV],
                                      g[..., OFF_Z + SSD_W:OFF_Z + SSD_W + SSD_HEADS]], axis=-1)


SHARD_COLS = N_IN // N_DEV


def _in_segments():
    segs, pos = [], 0
    for a, b in _IN_PIECES:
        for i in range(N_DEV):
            lo, hi = max(a, SHARD_COLS * i), min(b, SHARD_COLS * (i + 1))
            if lo < hi:
                segs.append((i, lo - SHARD_COLS * i, hi - lo, pos + lo - a))
        pos += b - a
    return segs


RELAYOUT_ROWS = 256


def _relayout_in(land, own):
    def body(land_ref, own_ref, out_ref):
        me = 4 * lax.axis_index("x") + 2 * lax.axis_index("y") + lax.axis_index("c")
        out_ref[:, N_IN:N_PAD] = jnp.zeros((RELAYOUT_ROWS, N_PAD - N_IN), BF16)
        for i, j, wd, p in _in_segments():
            out_ref[:, p:p + wd] = jnp.where(me == i, own_ref[:, j:j + wd], land_ref[i, :, j:j + wd])

    return pl.pallas_call(
        body, name="relayout_in", grid=(D_MODEL // RELAYOUT_ROWS,),
        in_specs=[pl.BlockSpec((N_DEV, RELAYOUT_ROWS, SHARD_COLS), lambda r: (0, r, 0)),
                  pl.BlockSpec((RELAYOUT_ROWS, SHARD_COLS), lambda r: (r, 0))],
        out_specs=pl.BlockSpec((RELAYOUT_ROWS, N_PAD), lambda r: (r, 0)),
        out_shape=SDS((D_MODEL, N_PAD), BF16),
        compiler_params=_cp(("parallel",)),
    )(land, own)


def _relayout_grad(g):
    def body(g_ref, out_ref):
        for i, j, wd, p in _in_segments():
            out_ref[i, :, j:j + wd] = g_ref[:, p:p + wd].astype(BF16)

    return pl.pallas_call(
        body, name="relayout_grad", grid=(D_MODEL // RELAYOUT_ROWS,),
        in_specs=[pl.BlockSpec((RELAYOUT_ROWS, N_PAD), lambda r: (r, 0))],
        out_specs=pl.BlockSpec((N_DEV, RELAYOUT_ROWS, SHARD_COLS), lambda r: (0, r, 0)),
        out_shape=SDS((N_DEV, D_MODEL, SHARD_COLS), BF16),
        compiler_params=_cp(("parallel",)),
    )(g)


def _block_diag(w):
    w4 = w.reshape(DEPTH, 4, 2, 64, 64)
    z = jnp.zeros((DEPTH, 4, 64, 64), w.dtype)
    top = jnp.concatenate([w4[:, :, 0], z], axis=-1)
    bot = jnp.concatenate([z, w4[:, :, 1]], axis=-1)
    return jnp.concatenate([top, bot], axis=2).astype(BF16)


def _diag_blocks(g):
    return jnp.stack([g[:, :, :64, :64], g[:, :, 64:, 64:]], axis=2).reshape(DEPTH, 8, 64, 64)


def _pad_lanes(v):
    return jnp.pad(v, ((0, 0), (0, LANE - v.shape[1])))


def _lower_bounds(logits):
    p = jax.nn.softmax(logits, axis=0)
    return p, jnp.cumsum(p, axis=0) - p[0]


def _lower_bounds_bwd(p, dlb):
    dp = jnp.cumsum(dlb[::-1], axis=0)[::-1]
    dp = dp.at[0].add(-jnp.sum(dlb, axis=0))
    return p * (dp - jnp.sum(dp * p, axis=0, keepdims=True))


SMALL = ["norm_w", "b_ada", "lru_conv_b", "lru_wa", "lru_ba", "lru_wx", "lru_bx", "lru_lambda", "hg_lb_logits",
         "hg_norm_w", "ssd_conv_b", "ssd_dt_bias", "ssd_a_log", "ssd_d", "ssd_norm_w", "final_norm_w"]
WEIGHTS = ["norm_w", "w_ada", "b_ada", "w_in", "lru_conv_w", "lru_conv_b", "lru_wa", "lru_ba", "lru_wx", "lru_bx",
           "lru_lambda", "hg_lb_logits", "hg_norm_w", "ssd_conv_w", "ssd_conv_b", "ssd_dt_bias", "ssd_a_log", "ssd_d",
           "ssd_norm_w", "w_out", "final_norm_w"]
INPUTS = ["x", "c"] + WEIGHTS + ["loss_target"] + ["m_" + n for n in WEIGHTS] + ["v_" + n for n in WEIGHTS]
SMALL_ROW = 1024


def _small_rows(like):
    out, off = {}, 0
    for n in SMALL:
        rows = -(-int(np.prod(like[n].shape)) // (8 * SMALL_ROW)) * 8
        out[n] = (off, rows)
        off += rows
    return out, off


def _flatten_small(d, prefix="", last=0.0):
    table, _ = _small_rows({n: d[prefix + n] for n in SMALL})
    pieces = []
    for n in SMALL:
        flat = d[prefix + n].reshape(-1)
        pieces.append(jnp.pad(flat, (0, table[n][1] * SMALL_ROW - flat.shape[0])).reshape(-1, SMALL_ROW))
    return jnp.concatenate(pieces + [jnp.full((8, SMALL_ROW), last, F32)], axis=0)


def _local_step(x, mod, target, w, fetch, emit):
    S = x.shape[0]
    mall = _bfc(_hg_consts())
    mall_t = _bfc(_hg_consts().T)
    consts = _ssd_consts()
    p_lb, lbs = _lower_bounds(w["hg_lb_logits"])
    no_tok = jnp.zeros((8, LANE), F32)
    wa, wx = _block_diag(w["lru_wa"]), _block_diag(w["lru_wx"])
    ba, bx = w["lru_ba"].reshape(DEPTH, 1, LRU_W), w["lru_bx"].reshape(DEPTH, 1, LRU_W)
    lru_cb, lam, ssd_cb = w["lru_conv_b"][:, None], w["lru_lambda"][:, None], w["ssd_conv_b"][:, None]
    bias, alog = _pad_lanes(w["ssd_dt_bias"]), _pad_lanes(w["ssd_a_log"])
    dskip = jnp.repeat(w["ssd_d"], SSD_P, axis=1)
    saved = []
    for l in range(DEPTH):
        w_in_l, w_out_l, token = fetch(l, x)
        shift, scale, gate = (_Row(mod, l, D_MODEL, k) for k in range(3))
        nw = _Row(w["norm_w"], l)
        u, h = _inproj_fwd(x, nw, scale, shift, w_in_l, no_tok if token is None else token)
        ycat = lax.empty((S, D_INNER), BF16)
        lru_args = (l, u, w["lru_conv_w"], lru_cb, wa, ba, wx, bx, lam)
        ycat, h_lru = _lru_fwd(*lru_args, ycat)
        hg_args = (u, _Row(lbs, l), _Row(w["hg_norm_w"], l), mall)
        ycat, o_b, hg_st = _hg_fwd(*hg_args, ycat)
        xbc = _ssdconv_fwd(l, u, w["ssd_conv_w"], ssd_cb)
        ssd_args = (u, xbc, _Row(bias, l), _Row(alog, l), _Row(dskip, l), _Row(w["ssd_norm_w"], l), consts)
        ycat, y_ssd, ssd_st = _ssd_fwd(*ssd_args, ycat)
        token = fetch(l, y_ssd, late=True)
        x_new, y = _outproj_fwd(ycat, w_out_l, x, gate, no_tok if token is None else token)
        saved.append((x, u, h, ycat, nw, scale, gate, w_in_l, w_out_l, lru_args, h_lru, hg_args, o_b, hg_st, ssd_args,
                      y_ssd, ssd_st, y))
        x = x_new
    dx, red = _loss_head(x, w["final_norm_w"][None, :], target)
    loss = red[1, 0]
    reds = {k: [None] * DEPTH for k in ("in", "gate", "lru", "wa", "wx", "hg", "conv", "ssd")}
    for l in reversed(range(DEPTH)):
        (x, u, h, ycat, nw, scale, gate, w_in_l, w_out_l, lru_args, h_lru, hg_args, o_b, hg_st, ssd_args, y_ssd, ssd_st,
         y) = saved[l]
        dycat, g_out, reds["gate"][l] = _outproj_bwd(dx, y, gate, ycat, w_out_l)
        token = emit(l, "w_out", g_out)
        du = lax.empty((S, N_PAD), BF16)
        du, dxbc, reds["ssd"][l] = _ssd_bwd(*ssd_args, y_ssd, ssd_st, dycat, du, no_tok if token is None else token)
        du, reds["conv"][l] = _ssdconv_bwd(l, u, w["ssd_conv_w"], ssd_cb, dxbc, du)
        du, reds["hg"][l] = _hg_bwd(*hg_args, mall_t, o_b, hg_st, dycat, du)
        du, reds["lru"][l], reds["wa"][l], reds["wx"][l] = _lru_bwd(*lru_args, h_lru, dycat, du)
        token = emit(l, "w_in", functools.partial(_inproj_bwd_w, h, du))
        dx, reds["in"][l] = _inproj_bwd_x(du, w_in_l, x, nw, scale, dx, no_tok if token is None else token)
    r = {k: jnp.stack(v) for k, v in reds.items()}
    g = {n: None for n in WEIGHTS}
    g["final_norm_w"] = red[0]
    g["norm_w"] = r["in"][:, 2]
    dmod = jnp.concatenate([r["in"][:, 0], r["in"][:, 1], r["gate"][:, 0]], axis=1)
    g["lru_conv_w"], g["lru_conv_b"] = r["lru"][:, 0:4], r["lru"][:, 4]
    g["lru_ba"], g["lru_bx"] = r["lru"][:, 5].reshape(DEPTH, 8, 64), r["lru"][:, 6].reshape(DEPTH, 8, 64)
    g["lru_lambda"] = r["lru"][:, 7]
    g["lru_wa"], g["lru_wx"] = _diag_blocks(r["wa"]), _diag_blocks(r["wx"])
    g["hg_norm_w"] = r["hg"][:, 0]
    g["hg_lb_logits"] = _lower_bounds_bwd(p_lb, r["hg"][:, 1])
    g["ssd_conv_w"], g["ssd_conv_b"] = r["conv"][:, 0:4], r["conv"][:, 4]
    g["ssd_norm_w"] = r["ssd"][:, 0]
    g["ssd_d"] = r["ssd"][:, 1].reshape(DEPTH, SSD_HEADS, SSD_P).sum(-1)
    g["ssd_dt_bias"] = r["ssd"][:, 2, :SSD_HEADS]
    g["ssd_a_log"] = -r["ssd"][:, 3, :SSD_HEADS] * jnp.exp(w["ssd_a_log"])
    return loss, dx, dmod, g


def kernel(x, c, norm_w, w_ada, b_ada, w_in, lru_conv_w, lru_conv_b, lru_wa, lru_ba, lru_wx, lru_bx, lru_lambda, hg_lb_logits, hg_norm_w, ssd_conv_w, ssd_conv_b, ssd_dt_bias, ssd_a_log, ssd_d, ssd_norm_w, w_out, final_norm_w, loss_target, m_norm_w, m_w_ada, m_b_ada, m_w_in, m_lru_conv_w, m_lru_conv_b, m_lru_wa, m_lru_ba, m_lru_wx, m_lru_bx, m_lru_lambda, m_hg_lb_logits, m_hg_norm_w, m_ssd_conv_w, m_ssd_conv_b, m_ssd_dt_bias, m_ssd_a_log, m_ssd_d, m_ssd_norm_w, m_w_out, m_final_norm_w, v_norm_w, v_w_ada, v_b_ada, v_w_in, v_lru_conv_w, v_lru_conv_b, v_lru_wa, v_lru_ba, v_lru_wx, v_lru_bx, v_lru_lambda, v_hg_lb_logits, v_hg_norm_w, v_ssd_conv_w, v_ssd_conv_b, v_ssd_dt_bias, v_ssd_a_log, v_ssd_d, v_ssd_norm_w, v_w_out, v_final_norm_w):
    return _step(x, c, norm_w, w_ada, b_ada, w_in, lru_conv_w, lru_conv_b, lru_wa, lru_ba, lru_wx, lru_bx, lru_lambda, hg_lb_logits, hg_norm_w, ssd_conv_w, ssd_conv_b, ssd_dt_bias, ssd_a_log, ssd_d, ssd_norm_w, w_out, final_norm_w, loss_target, m_norm_w, m_w_ada, m_b_ada, m_w_in, m_lru_conv_w, m_lru_conv_b, m_lru_wa, m_lru_ba, m_lru_wx, m_lru_bx, m_lru_lambda, m_hg_lb_logits, m_hg_norm_w, m_ssd_conv_w, m_ssd_conv_b, m_ssd_dt_bias, m_ssd_a_log, m_ssd_d, m_ssd_norm_w, m_w_out, m_final_norm_w, v_norm_w, v_w_ada, v_b_ada, v_w_in, v_lru_conv_w, v_lru_conv_b, v_lru_wa, v_lru_ba, v_lru_wx, v_lru_bx, v_lru_lambda, v_hg_lb_logits, v_hg_norm_w, v_ssd_conv_w, v_ssd_conv_b, v_ssd_dt_bias, v_ssd_a_log, v_ssd_d, v_ssd_norm_w, v_w_out, v_final_norm_w)


def _step(*args):
    a = dict(zip(INPUTS, args, strict=True))
    me = 4 * lax.axis_index("x") + 2 * lax.axis_index("y") + lax.axis_index("c")
    x, target = a["x"][0], a["loss_target"][0]

    c_all = _all_gather(a["c"], "gather_c")[:, 0, :]
    b_cols = lax.dynamic_slice_in_dim(a["b_ada"], me * ADA_COLS, ADA_COLS, axis=1)[:, None, :]
    mod_parts = _all_gather(_ada_fwd(c_all, a["w_ada"], b_cols), "gather_mod")
    mod = lax.dynamic_index_in_dim(mod_parts, me, axis=2, keepdims=False)
    mod = mod.transpose(1, 0, 2).reshape(DEPTH, 3 * D_MODEL)

    w = {n: a[n] for n in SMALL}

    w_in_b = [a["w_in"][l].astype(BF16) for l in range(DEPTH)]
    w_out_b = a["w_out"].astype(BF16)
    conv_own = jnp.concatenate([a["lru_conv_w"], a["ssd_conv_w"]], axis=-1)
    cols, rows_out = N_IN // N_DEV, D_INNER // N_DEV

    def gather_start(l, after):
        srcs = [w_in_b[l], w_out_b[l]] + ([conv_own] if l == 0 else [])
        lands = [lax.empty((N_DEV,) + s.shape, s.dtype) for s in srcs]
        return _exchange_start(f"gather_start_{l}", srcs, lands, "chip", after=after)

    def gather_pass(name, st, after, also=()):
        landed = _exchange_wait(name + "_wait", st, after, also)
        st2 = _exchange_start(name + "_pass", st["srcs"], landed, "pass")
        return _exchange_wait(name + "_passed", st2, after)

    gathers = {0: gather_start(0, mod)}
    passing = {}

    def fetch(l, x_l, late=False):
        if late:
            if l + 1 == DEPTH:
                return None
            landed = _exchange_wait(f"gather_{l + 1}_wait", gathers[l + 1], x_l)
            passing[l + 1] = _exchange_start(f"gather_{l + 1}_pass", gathers[l + 1]["srcs"], landed, "pass")
            return passing[l + 1]["token"]
        if l == 0:
            landed = gather_pass("gather_0", gathers[0], x_l, also=(a["w_in"], a["m_w_in"], a["v_w_in"]))
        else:
            landed = _exchange_wait(f"gather_{l}_passed", passing[l], x_l)
        land_out = lax.dynamic_update_index_in_dim(landed[1], w_out_b[l], me, 0)
        if l == 0:
            conv = lax.dynamic_update_index_in_dim(landed[2], conv_own, me, 0).transpose(1, 2, 0, 3)
            w["lru_conv_w"] = conv[..., :64].reshape(DEPTH, 4, LRU_W)
            w["ssd_conv_w"] = conv[..., 64:].reshape(DEPTH, 4, SSD_CONV)
        token = None
        if l + 1 < DEPTH:
            gathers[l + 1] = gather_start(l + 1, land_out)
            token = gathers[l + 1]["token"]
        return _relayout_in(landed[0], w_in_b[l]), land_out.reshape(D_INNER, D_MODEL), token

    PROJ = ("w_in", "w_out")
    scatters = {}
    lands = [lax.empty((N_DEV, DEPTH, D_MODEL, cols), BF16), lax.empty((N_DEV, DEPTH, rows_out, D_MODEL), BF16)]
    own = [None] * DEPTH

    deferred, g_out = {}, {}

    def emit(l, name, grad, after=None):
        if name == "w_out" and l > 0:
            g_out[l] = grad
            return None
        if name == "w_in" and l == 0 and after is None:
            deferred["w_in"] = grad
            return None
        if name == "w_in":
            grad = grad(jnp.zeros((8, LANE), F32) if after is None else after)
        if l == 0:
            k = PROJ.index(name)
            src = _relayout_grad(grad) if name == "w_in" else grad.reshape(N_DEV, rows_out, D_MODEL)
            st = _exchange_start(f"scatter_start_0_{name}", [src], [lands[k]], "scatter", layer=0, after=after)
            scatters[name] = st
            lands[k] = st["lands"][0]
            return st["token"]
        srcs = [_relayout_grad(grad), g_out[l].reshape(N_DEV, rows_out, D_MODEL)]
        st = _exchange_start(f"scatter_start_{l}", srcs, lands, "scatter", layer=l, after=after)
        scatters[l] = st
        lands[:] = st["lands"]
        return st["token"]

    loss_own, dx, dmod, g = _local_step(x, mod, target, w, fetch, emit)

    def sharded(name, parts, own=None, **kw):
        return _adamw(parts, a[name], a["m_" + name], a["v_" + name], "adamw_" + name + kw.pop("tag", ""), own=own, **kw)

    g["b_ada"] = dmod
    small_own = _flatten_small(g, last=loss_own)
    small_st = _exchange_start("gather_small", [small_own], [lax.empty((N_DEV,) + small_own.shape, F32)], "chip",
                               after=dx)
    big = {}
    after = emit(0, "w_in", deferred["w_in"], after=small_st["token"]) + dx[0:8, 0:LANE]

    def own_slices(st):
        return [lax.dynamic_index_in_dim(s, me, 0, keepdims=False) for s in st["srcs"]]

    for l in reversed(range(1, DEPTH)):
        scatters[l]["lands"] = lands
        lands[:] = _exchange_wait(f"scatter_wait_{l}", scatters[l], after)
        own[l] = own_slices(scatters[l])
    scatters["w_out"]["lands"] = [lands[1]]
    lands[1] = _exchange_wait("scatter_wait_0_w_out", scatters["w_out"], after)[0]
    own[0] = [None, own_slices(scatters["w_out"])[0]]
    big["w_out"] = sharded("w_out", lands[1], jnp.stack([own[l][1] for l in range(DEPTH)]))
    upper = sharded("w_in", lands[0], jnp.stack([own[l][0] for l in range(1, DEPTH)]), layers=(1, DEPTH), tag="_upper")
    after = upper[1][0, 0:8, 0:LANE] + big["w_out"][1][0, 0:8, 0:LANE]
    small = gather_pass("gather_small", small_st, after)[0]
    outs = _adamw_small(small, small_own, a)
    res = [{n: outs[4 * i + k].reshape(a[n].shape) for i, n in enumerate(SMALL)} for k in range(4)]
    losses = lax.dynamic_update_index_in_dim(small[:, -1, 0], loss_own, me, 0)
    loss = jnp.sum(losses)

    off = _small_rows(a)[0]["b_ada"][0]
    dmod_all = lax.dynamic_update_index_in_dim(small[:, off:off + DEPTH * 3 * D_MODEL // SMALL_ROW],
                                               dmod.reshape(-1, SMALL_ROW), me, 0)
    dmod_all = dmod_all.reshape(N_DEV, DEPTH, 3 * D_MODEL).transpose(1, 0, 2)
    dmod_cols = lax.dynamic_slice_in_dim(dmod_all, me * ADA_COLS, ADA_COLS, axis=2)
    dmod_pad = jnp.pad(dmod_cols, ((0, 0), (0, LANE - N_DEV), (0, 0)))
    ct_pad = jnp.pad(c_all.T, ((0, 0), (0, LANE - N_DEV)))
    big["w_ada"] = sharded("w_ada", _ada_bwd(ct_pad, dmod_pad)[None])
    g_conv = jnp.concatenate([g["lru_conv_w"].reshape(DEPTH, 4, N_DEV, 64), g["ssd_conv_w"].reshape(DEPTH, 4, N_DEV, 192)],
                             axis=-1).transpose(2, 0, 1, 3)
    conv_parts = _all_to_all(g_conv, "scatter_conv")
    big["lru_conv_w"] = sharded("lru_conv_w", conv_parts[..., :64])
    big["ssd_conv_w"] = sharded("ssd_conv_w", conv_parts[..., 64:])

    after = outs[1] + big["w_ada"][1][0, 0:1, 0:1]
    scatters["w_in"]["lands"] = [lands[0]]
    lands[0] = _exchange_wait("scatter_wait_0_w_in", scatters["w_in"], after)[0]
    big["w_in"] = sharded("w_in", lands[0], own_slices(scatters["w_in"])[0][None], layers=(0, 1), prev=upper)

    out = [loss, dx[None]]
    for k in range(4):
        out += [big[n][k] if n in big else res[k][n] for n in WEIGHTS]
    return tuple(out)
```

```python
import functools

import numpy as np
import jax
import jax.numpy as jnp
from jax import lax
from jax.experimental import pallas as pl
from jax.experimental.pallas import tpu as pltpu

F32 = jnp.float32
BF16 = jnp.bfloat16
SDS = jax.ShapeDtypeStruct

N_DEV = 8
DEPTH = 4
D_MODEL = 1024
D_INNER = 2048
EPS = 1e-6
LRU_W = 512
LRU_C = 8.0
HG_W = 512
HG_CHUNK = 64
HG_HEADS = 4
SSD_W = 1024
SSD_HEADS = 16
SSD_P = 64
SSD_N = 128
SSD_CHUNK = 128
SSD_CONV = 1536
N_IN = 5648
N_PAD = 5760
OFF_HG = 0
OFF_LRU = 2048
OFF_XBC = 3072
OFF_Z = 4608
LANE = 128
VMEM_LIMIT = 56 * 1024 * 1024
NEG = -1e30

ADAM_LR = 0.001
ADAM_B1 = 0.9
ADAM_B2 = 0.999
ADAM_EPS = 1e-08
ADAM_WD = 0.01
ADAM_STEP = 10


def _cp(sem=None):
    return pltpu.CompilerParams(dimension_semantics=sem, vmem_limit_bytes=VMEM_LIMIT)


def _dg(a, b, ca, cb):
    return lax.dot_general(a, b, (((ca,), (cb,)), ((), ())), preferred_element_type=F32)


def _mm(a, b):
    return _dg(a, b, 1, 0)


def _mm_nt(a, b):
    return _dg(a, b, 1, 1)


def _mm_tn(a, b):
    return _dg(a, b, 0, 0)


def _bf(x):
    return x.astype(BF16)


def _f(x):
    return x.astype(F32)


def _split3(x):
    hi = x.astype(BF16)
    r = x - hi.astype(F32)
    mid = r.astype(BF16)
    lo = (r - mid.astype(F32)).astype(BF16)
    return hi, mid, lo


def _sel_r(x, m):
    hi, mid, lo = _split3(x)
    return _mm(hi, m) + _mm(mid, m) + _mm(lo, m)


def _sel_l(m, x):
    hi, mid, lo = _split3(x)
    return _mm(m, hi) + _mm(m, mid) + _mm(m, lo)


def _sel_l2(m, x):
    hi = x.astype(BF16)
    lo = (x - hi.astype(F32)).astype(BF16)
    return _mm(m, hi) + _mm(m, lo)


def _sel_tn(x, m):
    hi, mid, lo = _split3(x)
    return _mm_tn(hi, m) + _mm_tn(mid, m) + _mm_tn(lo, m)


def _sigmoid(x):
    return 1.0 / (1.0 + jnp.exp(-x))


def _silu(x):
    return x * _sigmoid(x)


def _dsilu(x):
    s = _sigmoid(x)
    return s * (1.0 + x * (1.0 - s))


def _softplus(x):
    return jnp.maximum(x, 0.0) + jnp.log(1.0 + jnp.exp(-jnp.abs(x)))


def _expm1(z):
    series = z * (1.0 + z * (1.0 / 2) * (1.0 + z * (1.0 / 3) * (1.0 + z * (1.0 / 4) * (
        1.0 + z * (1.0 / 5) * (1.0 + z * (1.0 / 6) * (1.0 + z * (1.0 / 7)))))))
    return jnp.where(jnp.abs(z) < 0.3, series, jnp.exp(z) - 1.0)


def _iota(shape, dim):
    return lax.broadcasted_iota(jnp.int32, shape, dim)


def _last_row(x, rows):
    return jnp.sum(jnp.where(rows == x.shape[0] - 1, x, 0.0), axis=0, keepdims=True)


def _shift_down(x, d, rows, fill=0.0):
    return jnp.where(rows >= d, pltpu.roll(x, d, 0), fill)


def _shift_up(x, d, rows, fill=0.0):
    n = x.shape[0]
    return jnp.where(rows < n - d, pltpu.roll(x, n - d, 0), fill)


def _conv_fwd(x, cw_ref, cb_ref, rows):
    out = cb_ref[...] + cw_ref[pl.ds(3, 1), :] * x
    for k in range(3):
        out = out + cw_ref[pl.ds(k, 1), :] * _shift_down(x, 3 - k, rows)
    return out


def _conv_bwd(x, dco, cw_ref, rows):
    dx = cw_ref[pl.ds(3, 1), :] * dco
    dws = []
    for k in range(3):
        dx = dx + cw_ref[pl.ds(k, 1), :] * _shift_up(dco, 3 - k, rows)
        dws.append(jnp.sum(dco * _shift_down(x, 3 - k, rows), axis=0, keepdims=True))
    dws.append(jnp.sum(dco * x, axis=0, keepdims=True))
    return dx, dws, jnp.sum(dco, axis=0, keepdims=True)


def _vec(n):
    return pl.BlockSpec((1, n), lambda *_: (0, 0))


class _Row:
    def __init__(self, arr, l, n=None, c=0):
        self.arr, self.l, self.n, self.c = arr[:, None, :], l, n or arr.shape[1], c


def _spec(v):
    if isinstance(v, _Row):
        return pl.BlockSpec((None, 1, v.n), lambda *_: (v.l, 0, v.c))
    return _vec(v.shape[1])


def _arr(v):
    return v.arr if isinstance(v, _Row) else v


def _full(shape):
    nd = len(shape)
    return pl.BlockSpec(shape, lambda *_: (0,) * nd)


def _inproj_fwd(x, nw, scale, shift, w, tok):
    S = x.shape[0]
    tm = min(256, S)

    def body(x_ref, nw_ref, sc_ref, sh_ref, w_ref, tok_ref, u_ref, h_ref):
        del tok_ref
        xv = x_ref[...]
        inv = lax.rsqrt(jnp.mean(xv * xv, axis=-1, keepdims=True) + EPS)
        h = ((xv * inv) * nw_ref[...] * (1.0 + sc_ref[...]) + sh_ref[...]).astype(BF16)
        h_ref[...] = h
        u_ref[...] = _mm(h, w_ref[...])

    return pl.pallas_call(
        body, name="inproj_fwd", grid=(S // tm,),
        in_specs=[pl.BlockSpec((tm, D_MODEL), lambda i: (i, 0)), _spec(nw), _spec(scale), _spec(shift),
                  _full((D_MODEL, N_PAD)), pl.BlockSpec(memory_space=pl.ANY)],
        out_specs=[pl.BlockSpec((tm, N_PAD), lambda i: (i, 0)), pl.BlockSpec((tm, D_MODEL), lambda i: (i, 0))],
        out_shape=[SDS((S, N_PAD), F32), SDS((S, D_MODEL), BF16)],
        compiler_params=_cp(("parallel",)),
    )(x, _arr(nw), _arr(scale), _arr(shift), w, tok)


def _inproj_bwd_x(du, w, x, nw, scale, dxn, tok):
    S = x.shape[0]
    tm = min(256, S)

    def body(du_ref, w_ref, x_ref, nw_ref, sc_ref, dxn_ref, tok_ref, dx_ref, red_ref):
        del tok_ref

        @pl.when(pl.program_id(0) == 0)
        def _():
            red_ref[...] = jnp.zeros_like(red_ref)

        dh = _mm_nt(du_ref[...], w_ref[...])
        xv = x_ref[...]
        inv = lax.rsqrt(jnp.mean(xv * xv, axis=-1, keepdims=True) + EPS)
        xhat = xv * inv
        nwv = nw_ref[...]
        g1 = 1.0 + sc_ref[...]
        dxhat = dh * nwv * g1
        dx = inv * (dxhat - xhat * jnp.mean(dxhat * xhat, axis=-1, keepdims=True))
        dx_ref[...] = dxn_ref[...] + dx
        red_ref[0:1, :] += jnp.sum(dh, axis=0, keepdims=True)
        red_ref[1:2, :] += jnp.sum(dh * xhat * nwv, axis=0, keepdims=True)
        red_ref[2:3, :] += jnp.sum(dh * xhat * g1, axis=0, keepdims=True)

    row = pl.BlockSpec((tm, D_MODEL), lambda i: (i, 0))
    return pl.pallas_call(
        body, name="inproj_bwd_x", grid=(S // tm,),
        in_specs=[pl.BlockSpec((tm, N_PAD), lambda i: (i, 0)), _full((D_MODEL, N_PAD)), row, _spec(nw),
                  _spec(scale), row, pl.BlockSpec(memory_space=pl.ANY)],
        out_specs=[row, _full((8, D_MODEL))],
        out_shape=[SDS((S, D_MODEL), F32), SDS((8, D_MODEL), F32)],
        compiler_params=_cp(("arbitrary",)),
    )(du, w, x, _arr(nw), _arr(scale), dxn, tok)


def _inproj_bwd_w(h, du, tok):
    S = h.shape[0]
    tn = 640

    def body(h_ref, du_ref, tok_ref, gw_ref):
        del tok_ref
        gw_ref[...] = _mm_tn(h_ref[...], _bf(du_ref[...]))

    return pl.pallas_call(
        body, name="inproj_bwd_w", grid=(N_PAD // tn,),
        in_specs=[_full((S, D_MODEL)), pl.BlockSpec((S, tn), lambda j: (0, j)), pl.BlockSpec(memory_space=pl.ANY)],
        out_specs=pl.BlockSpec((D_MODEL, tn), lambda j: (0, j)),
        out_shape=SDS((D_MODEL, N_PAD), F32),
        compiler_params=_cp(("parallel",)),
    )(h, du, tok)


def _scan_block(a, b, rows):
    d = 1
    while d < a.shape[0]:
        a_s = _shift_down(a, d, rows, 1.0)
        b_s = _shift_down(b, d, rows, 0.0)
        b = a * b_s + b
        a = a * a_s
        d *= 2
    return a, b


def _rscan_block(c, g, rows):
    d = 1
    while d < c.shape[0]:
        c_s = _shift_up(c, d, rows, 1.0)
        g_s = _shift_up(g, d, rows, 0.0)
        g = g + c * g_s
        c = c * c_s
        d *= 2
    return c, g


LRU_BLOCK = 256


def _lru_gates(xa, wa_ref, ba_ref, wx_ref, bx_ref, lam_ref):
    sp = _softplus(-lam_ref[...])
    xb = _bf(xa)
    r = _sigmoid(_mm(xb, wa_ref[...]) + ba_ref[...])
    ig = _sigmoid(_mm(xb, wx_ref[...]) + bx_ref[...])
    la = -LRU_C * r * sp
    a = jnp.exp(la)
    mult = jnp.sqrt(-_expm1(2.0 * la))
    return sp, r, ig, la, a, mult


def _lru_specs(S, l):
    t128 = pl.BlockSpec((None, 1, LANE), lambda t: (l, 0, t))
    gate = pl.BlockSpec((None, None, LANE, LANE), lambda t: (l, t, 0, 0))
    return [pl.BlockSpec((S, 2 * LANE), lambda t: (0, OFF_LRU // (2 * LANE) + t)),
            pl.BlockSpec((None, 4, LANE), lambda t: (l, 0, t)), t128, gate, t128, gate, t128, t128]


def _lru_fwd(l, u, cw, cb, wa, ba, wx, bx, lam, ycat):
    S = u.shape[0]
    tb = min(LRU_BLOCK, S)

    def body(u_ref, cw_ref, cb_ref, wa_ref, ba_ref, wx_ref, bx_ref, lam_ref, ycat_in, ycat_ref, h_ref, a_scr, b_scr):
        del ycat_in
        rows = _iota((S, LANE), 0)
        xa = _conv_fwd(_f(u_ref[:, 0:LANE]), cw_ref, cb_ref, rows)
        _, _, ig, _, a, mult = _lru_gates(xa, wa_ref, ba_ref, wx_ref, bx_ref, lam_ref)
        a_scr[...] = a
        b_scr[...] = mult * (ig * xa)
        rows_b = _iota((tb, LANE), 0)

        def blk(j, hprev):
            sl = pl.ds(pl.multiple_of(j * tb, tb), tb)
            acum, hloc = _scan_block(a_scr[sl, :], b_scr[sl, :], rows_b)
            hf = hloc + acum * hprev
            h_ref[sl, :] = hf
            return _last_row(hf, rows_b)

        lax.fori_loop(0, S // tb, blk, jnp.zeros((1, LANE), F32))
        ycat_ref[...] = _bf(h_ref[...] * _silu(_f(u_ref[:, LANE:2 * LANE])))

    col = pl.BlockSpec((S, LANE), lambda t: (0, t))
    return pl.pallas_call(
        body, name="lru_fwd", grid=(LRU_W // LANE,),
        in_specs=_lru_specs(S, l) + [pl.BlockSpec(memory_space=pl.ANY)],
        out_specs=[col, col],
        out_shape=[SDS((S, D_INNER), BF16), SDS((S,LRU_W), F32)],
        scratch_shapes=[pltpu.VMEM((S, LANE), F32), pltpu.VMEM((S, LANE), F32)],
        input_output_aliases={8: 0},
        compiler_params=_cp(("parallel",)),
    )(u, cw, cb, wa, ba, wx, bx, lam, ycat)


def _lru_bwd(l, u, cw, cb, wa, ba, wx, bx, lam, h_lru, dycat, du):
    S = u.shape[0]
    tb = min(LRU_BLOCK, S)

    def body(u_ref, cw_ref, cb_ref, wa_ref, ba_ref, wx_ref, bx_ref, lam_ref, h_ref, dy_ref, du_in,
             du_ref, red_ref, gwa_ref, gwx_ref, c_scr, g_scr, l_scr):
        del du_in
        rows = _iota((S, LANE), 0)
        ax = _f(u_ref[:, 0:LANE])
        ag = _f(u_ref[:, LANE:2 * LANE])
        xa = _conv_fwd(ax, cw_ref, cb_ref, rows)
        sp, r, ig, la, a, mult = _lru_gates(xa, wa_ref, ba_ref, wx_ref, bx_ref, lam_ref)
        h = h_ref[...]
        dy = _f(dy_ref[...])
        du_ref[:, LANE:2 * LANE] = _bf(dy * h * _dsilu(ag))
        c_scr[...] = _shift_up(a, 1, rows, 0.0)
        g_scr[...] = dy * _silu(ag)
        rows_b = _iota((tb, LANE), 0)
        nb = S // tb

        def blk(jj, lnext):
            j = nb - 1 - jj
            sl = pl.ds(pl.multiple_of(j * tb, tb), tb)
            ccum, lloc = _rscan_block(c_scr[sl, :], g_scr[sl, :], rows_b)
            lam_t = lloc + ccum * lnext
            l_scr[sl, :] = lam_t
            return jnp.sum(jnp.where(rows_b == 0, lam_t, 0.0), axis=0, keepdims=True)

        lax.fori_loop(0, nb, blk, jnp.zeros((1, LANE), F32))
        db = l_scr[...]
        da = db * _shift_down(h, 1, rows)
        dmult = db * ig * xa
        dig = db * mult * xa
        dxa = db * mult * ig
        dla = da * a - dmult * (a * a) / mult
        dr = -LRU_C * sp * dla
        dsp = jnp.sum(-LRU_C * r * dla, axis=0, keepdims=True)
        dlam = -dsp * _sigmoid(-lam_ref[...])
        dzr = dr * r * (1.0 - r)
        dzi = dig * ig * (1.0 - ig)
        dzr_b, dzi_b, xa_b = _bf(dzr), _bf(dzi), _bf(xa)
        dxa = dxa + _mm_nt(dzr_b, wa_ref[...]) + _mm_nt(dzi_b, wx_ref[...])
        gwa_ref[...] = _mm_tn(xa_b, dzr_b)
        gwx_ref[...] = _mm_tn(xa_b, dzi_b)
        dax, dws, dcb = _conv_bwd(ax, dxa, cw_ref, rows)
        du_ref[:, 0:LANE] = _bf(dax)
        parts = dws + [dcb, jnp.sum(dzr, axis=0, keepdims=True), jnp.sum(dzi, axis=0, keepdims=True), dlam]
        for n, p in enumerate(parts):
            red_ref[pl.ds(n, 1), :] = p

    col = pl.BlockSpec((S, LANE), lambda t: (0, t))
    gw = pl.BlockSpec((None, LANE, LANE), lambda t: (t, 0, 0))
    return pl.pallas_call(
        body, name="lru_bwd", grid=(LRU_W // LANE,),
        in_specs=_lru_specs(S, l) + [col, col, pl.BlockSpec(memory_space=pl.ANY)],
        out_specs=[pl.BlockSpec((S, 2 * LANE), lambda t: (0, OFF_LRU // (2 * LANE) + t)),
                   pl.BlockSpec((8, LANE), lambda t: (0, t)), gw, gw],
        out_shape=[SDS((S, N_PAD), BF16), SDS((8, LRU_W), F32), SDS((4, LANE, LANE), F32), SDS((4, LANE, LANE), F32)],
        scratch_shapes=[pltpu.VMEM((S, LANE), F32)] * 3,
        input_output_aliases={10: 0},
        compiler_params=_cp(("parallel",)),
    )(u, cw, cb, wa, ba, wx, bx, lam, h_lru, dycat, du)


HG_LEVELS = 6


def _hg_consts():
    C = HG_CHUNK
    t = np.arange(C)[:, None]
    r = np.arange(C)[None, :]
    mats = []
    for l in range(HG_LEVELS):
        b = 1 << l
        upper = (t % (2 * b)) >= b
        anchor = (t // (2 * b)) * 2 * b + b - 1
        mats.append((upper & (r > anchor) & (r <= t)) | ((~upper) & (r > t) & (r <= anchor)))
    mats.append(r <= t)
    mats.append(r > t)
    return np.concatenate(mats, 0).astype(np.float32)


def _hg_factors(hf, lb, mall):
    s = _sigmoid(hf)
    f = lb + (1.0 - lb) * s
    lf = jnp.log(f)
    k = (1.0 - lb) * _sigmoid(-hf)
    e = jnp.exp(_sel_l(mall, lf))
    C = HG_CHUNK
    rows = _iota((C, HG_W), 0)
    eq, ek = [], []
    for l in range(HG_LEVELS):
        el = e[l * C:(l + 1) * C]
        eq.append(jnp.where((lax.shift_right_logical(rows, l) & 1) == 1, el, 0.0))
        ek.append(el - eq[l])
    ecum = e[HG_LEVELS * C:(HG_LEVELS + 1) * C]
    erem = e[(HG_LEVELS + 1) * C:(HG_LEVELS + 2) * C]
    return s, f, k, eq, ek, ecum, erem


def _hg_masks():
    C = HG_CHUNK
    ri, ci = _iota((C, C), 0), _iota((C, C), 1)
    rr = _iota((C, LANE), 0)
    gm = [(lax.shift_right_logical(ri, l + 1) == lax.shift_right_logical(ci, l + 1)).astype(F32)
          for l in range(HG_LEVELS)]
    up = [(lax.shift_right_logical(rr, l) & 1) == 1 for l in range(HG_LEVELS)]
    eye = (ri == ci).astype(F32)
    return gm, up, eye, rr


def _hg_scores(qh, kh, eq, ek, sl, gm, up, eye):
    del up
    qs, ks, qb, kb = [], [], [], []
    p = _mm_nt(_bf(qh), _bf(kh)) * eye
    for l in range(HG_LEVELS):
        qs.append(qh * eq[l][:, sl])
        ks.append(kh * ek[l][:, sl])
        qb.append(_bf(qs[l]))
        kb.append(_bf(ks[l]))
        p = p + _mm_nt(qb[l], kb[l]) * gm[l]
    return p, qs, ks, qb, kb


HG_SUB = 4


def _hg_fwd(u, lb, nw, mall, ycat):
    S = u.shape[0]
    C = HG_CHUNK
    n = S // C
    rows = HG_SUB * C

    def body(u_ref, lb_ref, nw_ref, mall_ref, ycat_in, ycat_ref, o_ref, st_ref, st):
        del ycat_in

        @pl.when(pl.program_id(0) == 0)
        def _():
            st[...] = jnp.zeros_like(st)

        gm, up, eye, rr = _hg_masks()
        for sub in range(HG_SUB):
            r = slice(sub * C, (sub + 1) * C)
            q = _silu(_f(u_ref[r, 0:512]))
            v = u_ref[r, 1024:1536]
            _, _, k, eq, ek, ecum, erem = _hg_factors(_f(u_ref[r, 512:1024]), lb_ref[...], mall_ref[...])
            for h in range(HG_HEADS):
                sl = slice(h * LANE, (h + 1) * LANE)
                qh, kh, vh = q[:, sl], k[:, sl], _bf(v[:, sl])
                p = _hg_scores(qh, kh, eq, ek, sl, gm, up, eye)[0]
                sth = st[h]
                st_ref[sub, h] = sth
                o_ref[r, sl] = _mm(_bf(p), vh) + _mm_nt(_bf(qh * ecum[:, sl]), _bf(sth))
                st[h] = sth * _last_row(ecum[:, sl], rr) + _mm_tn(vh, _bf(kh * erem[:, sl]))
            o = o_ref[r, :]
            inv = lax.rsqrt(jnp.mean(o * o, axis=-1, keepdims=True) + EPS)
            ycat_ref[r, :] = _bf((o * inv) * nw_ref[...] * _silu(_f(u_ref[r, 1536:2048])))

    return pl.pallas_call(
        body, name="hg_fwd", grid=(n // HG_SUB,),
        in_specs=[pl.BlockSpec((rows, 2048), lambda i: (i, 0)), _spec(lb), _spec(nw), _full(mall.shape),
                  pl.BlockSpec(memory_space=pl.ANY)],
        out_specs=[pl.BlockSpec((rows, HG_W), lambda i: (i, 1)), pl.BlockSpec((rows, HG_W), lambda i: (i, 0)),
                   pl.BlockSpec((HG_SUB, HG_HEADS, LANE, LANE), lambda i: (i, 0, 0, 0))],
        out_shape=[SDS((S, D_INNER), BF16), SDS((S,HG_W), F32), SDS((n, HG_HEADS, LANE, LANE), F32)],
        scratch_shapes=[pltpu.VMEM((HG_HEADS, LANE, LANE), F32)],
        input_output_aliases={4: 0},
        compiler_params=_cp(("arbitrary",)),
    )(u, _arr(lb), _arr(nw), mall, ycat)


def _hg_bwd(u, lb, nw, mall, mall_t, o_b, states, dycat, du):
    S = u.shape[0]
    C = HG_CHUNK
    n = S // C
    nb = n // HG_SUB
    rows = HG_SUB * C
    L2 = HG_LEVELS

    def body(u_ref, lb_ref, nw_ref, mall_ref, mallt_ref, o_ref, st_ref, dy_ref, du_in, du_ref, red_ref,
             dst, dlast_s, dq_s, dk_s, dex):
        del du_in

        @pl.when(pl.program_id(0) == 0)
        def _():
            dst[...] = jnp.zeros_like(dst)
            red_ref[...] = jnp.zeros_like(red_ref)

        lb = lb_ref[...]
        nwv = nw_ref[...]
        gm, up, eye, rr = _hg_masks()
        for sub in reversed(range(HG_SUB)):
            r = slice(sub * C, (sub + 1) * C)
            hq, hf, hg = _f(u_ref[r, 0:512]), _f(u_ref[r, 512:1024]), _f(u_ref[r, 1536:2048])
            q = _silu(hq)
            v = u_ref[r, 1024:1536]
            s, f, k, eq, ek, ecum, erem = _hg_factors(hf, lb, mall_ref[...])
            o = o_ref[r, :]
            dy = _f(dy_ref[r, :])
            inv = lax.rsqrt(jnp.mean(o * o, axis=-1, keepdims=True) + EPS)
            ohat = o * inv
            du_ref[r, 1536:2048] = _bf(dy * ohat * nwv * _dsilu(hg))
            dn = dy * _silu(hg)
            red_ref[0:1, :] += jnp.sum(dn * ohat, axis=0, keepdims=True)
            dohat = dn * nwv
            do = inv * (dohat - ohat * jnp.mean(dohat * ohat, axis=-1, keepdims=True))
            for h in range(HG_HEADS):
                sl = slice(h * LANE, (h + 1) * LANE)
                qh, kh, vh, doh = q[:, sl], k[:, sl], _bf(v[:, sl]), _bf(do[:, sl])
                p, qs, ks, qb, kb = _hg_scores(qh, kh, eq, ek, sl, gm, up, eye)
                st_f = st_ref[sub, h]
                sth = _bf(st_f)
                dsth = dst[h]
                dsth_b = _bf(dsth)
                qt = qh * ecum[:, sl]
                kt = kh * erem[:, sl]
                elast = _last_row(ecum[:, sl], rr)
                dp = _mm_nt(doh, vh)
                du_ref[r, 1024 + h * LANE:1024 + (h + 1) * LANE] = _bf(_mm_tn(_bf(p), doh) + _mm_nt(_bf(kt), dsth_b))
                dpe = _bf(dp * eye)
                dqt = _mm(doh, sth)
                dkt = _mm(vh, dsth_b)
                dq = dqt * ecum[:, sl] + _mm(dpe, _bf(kh))
                dk = dkt * erem[:, sl] + _mm_tn(dpe, _bf(qh))
                dex[sub, L2 * C:(L2 + 1) * C, sl] = dqt * qt
                dex[sub, (L2 + 1) * C:(L2 + 2) * C, sl] = dkt * kt
                for l in range(HG_LEVELS):
                    dpl = _bf(dp * gm[l])
                    dql = _mm(dpl, kb[l])
                    dkl = _mm_tn(dpl, qb[l])
                    dq = dq + dql * eq[l][:, sl]
                    dk = dk + dkl * ek[l][:, sl]
                    dex[sub, l * C:(l + 1) * C, sl] = dql * qs[l] + dkl * ks[l]
                dlast_s[sub, :, sl] = jnp.sum(dsth * st_f, axis=0, keepdims=True) * elast
                dst[h] = dsth * elast + _mm_tn(doh, _bf(qt))
                dq_s[sub, :, sl] = dq
                dk_s[sub, :, sl] = dk
            dq = dq_s[sub]
            dk = dk_s[sub]
            dlf = _sel_l2(mallt_ref[...], dex[sub]) + dlast_s[sub]
            du_ref[r, 0:512] = _bf(dq * _dsilu(hq))
            t = (1.0 - s) * (dlf / f - dk)
            du_ref[r, 512:1024] = _bf((1.0 - lb) * s * t)
            red_ref[1:2, :] += jnp.sum(t, axis=0, keepdims=True)

    rev = lambda i: (nb - 1 - i, 0)
    return pl.pallas_call(
        body, name="hg_bwd", grid=(nb,),
        in_specs=[pl.BlockSpec((rows, 2048), rev), _spec(lb), _spec(nw), _full(mall.shape), _full(mall_t.shape),
                  pl.BlockSpec((rows, HG_W), rev),
                  pl.BlockSpec((HG_SUB, HG_HEADS, LANE, LANE), lambda i: (nb - 1 - i, 0, 0, 0)),
                  pl.BlockSpec((rows, HG_W), lambda i: (nb - 1 - i, 1)), pl.BlockSpec(memory_space=pl.ANY)],
        out_specs=[pl.BlockSpec((rows, 2048), rev), pl.BlockSpec((8, HG_W), lambda i: (0, 0))],
        out_shape=[SDS((S, N_PAD), BF16), SDS((8, HG_W), F32)],
        scratch_shapes=[pltpu.VMEM((HG_HEADS, LANE, LANE), F32), pltpu.VMEM((HG_SUB, 1, HG_W), F32),
                        pltpu.VMEM((HG_SUB, C, HG_W), F32), pltpu.VMEM((HG_SUB, C, HG_W), F32),
                        pltpu.VMEM((HG_SUB, (L2 + 2) * C, HG_W), F32)],
        input_output_aliases={8: 0},
        compiler_params=_cp(("arbitrary",)),
    )(u, _arr(lb), _arr(nw), mall, mall_t, o_b, states, dycat, du)


def _ssdconv_fwd(l, u, cw, cb):
    S = u.shape[0]

    def body(u_ref, cw_ref, cb_ref, out_ref):
        rows = _iota((S, LANE), 0)
        out_ref[...] = _silu(_conv_fwd(_f(u_ref[...]), cw_ref, cb_ref, rows))

    return pl.pallas_call(
        body, name="ssdconv_fwd", grid=(SSD_CONV // LANE,),
        in_specs=[pl.BlockSpec((S, LANE), lambda t: (0, OFF_XBC // LANE + t)),
                  pl.BlockSpec((None, 4, LANE), lambda t: (l, 0, t)), pl.BlockSpec((None, 1, LANE), lambda t: (l, 0, t))],
        out_specs=pl.BlockSpec((S, LANE), lambda t: (0, t)),
        out_shape=SDS((S, SSD_CONV), F32),
        compiler_params=_cp(("parallel",)),
    )(u, cw, cb)


def _ssdconv_bwd(l, u, cw, cb, dxbc, du):
    S = u.shape[0]

    def body(u_ref, cw_ref, cb_ref, d_ref, du_in, du_ref, red_ref):
        del du_in
        rows = _iota((S, LANE), 0)
        x = _f(u_ref[...])
        dco = d_ref[...] * _dsilu(_conv_fwd(x, cw_ref, cb_ref, rows))
        dx, dws, dcb = _conv_bwd(x, dco, cw_ref, rows)
        du_ref[...] = _bf(dx)
        for n, p in enumerate(dws + [dcb]):
            red_ref[pl.ds(n, 1), :] = p
        red_ref[pl.ds(5, 3), :] = jnp.zeros((3, LANE), F32)

    ucol = pl.BlockSpec((S, LANE), lambda t: (0, OFF_XBC // LANE + t))
    return pl.pallas_call(
        body, name="ssdconv_bwd", grid=(SSD_CONV // LANE,),
        in_specs=[ucol, pl.BlockSpec((None, 4, LANE), lambda t: (l, 0, t)),
                  pl.BlockSpec((None, 1, LANE), lambda t: (l, 0, t)),
                  pl.BlockSpec((S, LANE), lambda t: (0, t)), pl.BlockSpec(memory_space=pl.ANY)],
        out_specs=[ucol, pl.BlockSpec((8, LANE), lambda t: (0, t))],
        out_shape=[SDS((S, N_PAD), BF16), SDS((8, SSD_CONV), F32)],
        input_output_aliases={4: 0},
        compiler_params=_cp(("parallel",)),
    )(u, cw, cb, dxbc, du)


SSD_SUB = 2


def _ssd_consts():
    e64 = np.zeros((LANE, SSD_W), np.float32)
    for h in range(SSD_HEADS):
        e64[h, h * SSD_P:(h + 1) * SSD_P] = 1.0
    T = SSD_CHUNK
    tril = (np.arange(T)[None, :] <= np.arange(T)[:, None]).astype(np.float32)
    return e64, tril, tril.T.copy()


def _ssd_common(zdt, bias_ref, alog_ref, tril, e64, cum_ref, cumt_ref):
    T = SSD_CHUNK
    lane = _iota((1, LANE), 1)
    a_neg = jnp.where(lane < SSD_HEADS, -jnp.exp(alog_ref[...]), 0.0)
    dtpre = zdt[:, SSD_W:SSD_W + LANE] + bias_ref[...]
    dt = _softplus(dtpre)
    cum = _sel_l(tril, dt * a_neg)
    cum_ref[...] = cum
    cumt_ref[...] = cum.T
    cum_x = _sel_r(cum, e64)
    last_x = _last_row(cum_x, _iota((T, SSD_W), 0))
    ecum_x = jnp.exp(cum_x)
    erem_x = jnp.exp(last_x - cum_x)
    elast_x = jnp.exp(last_x)
    dt_x = _sel_r(dt, e64)
    return a_neg, dtpre, dt, ecum_x, erem_x, elast_x, dt_x


def _ssd_decay(cum_ref, cumt_ref, h, causal):
    T = SSD_CHUNK
    diff = jnp.broadcast_to(cum_ref[:, pl.ds(h, 1)], (T, T)) - cumt_ref[pl.ds(h, 1), :]
    return jnp.exp(jnp.where(causal, diff, NEG))


def _group_norm_fwd(y1, nwv):
    outs, invs = [], []
    for g in range(2):
        seg = y1[:, g * 512:(g + 1) * 512]
        inv = lax.rsqrt(jnp.mean(seg * seg, axis=-1, keepdims=True) + EPS)
        outs.append(seg * inv * nwv[:, g * 512:(g + 1) * 512])
        invs.append(inv)
    return outs, invs


def _ssd_fwd(u, xbc, bias, alog, dskip_x, nw, consts, ycat):
    S = u.shape[0]
    T = SSD_CHUNK
    n = S // T
    rows = SSD_SUB * T
    e64, tril, _ = consts

    def body(u_ref, xbc_ref, bias_ref, alog_ref, dx_ref, nw_ref, e64_ref, tril_ref, ycat_in,
             ycat_ref, y_ref, st_ref, st, cumt, cum_e):
        del ycat_in

        @pl.when(pl.program_id(0) == 0)
        def _():
            st[...] = jnp.zeros_like(st)

        causal = _iota((T, T), 0) >= _iota((T, T), 1)
        lo = _iota((T, LANE), 1) < SSD_P
        for sub in range(SSD_SUB):
            r = slice(sub * T, (sub + 1) * T)
            zdt = _f(u_ref[r, :])
            z = zdt[:, 0:SSD_W]
            xs = xbc_ref[r, 0:SSD_W]
            cum_r, cumt_r = cum_e.at[sub], cumt.at[sub]
            _, _, _, ecum_x, erem_x, elast_x, dt_x = _ssd_common(
                zdt, bias_ref, alog_ref, tril_ref[...], e64_ref[...], cum_r, cumt_r)
            xdt = xs * dt_x
            xrem = xdt * erem_x
            st_ref[sub] = st[...]
            for g in range(2):
                gs = slice(g * 512, (g + 1) * 512)
                bg = _bf(xbc_ref[r, SSD_W + g * LANE:SSD_W + (g + 1) * LANE])
                cg = _bf(xbc_ref[r, SSD_W + 256 + g * LANE:SSD_W + 256 + (g + 1) * LANE])
                cb = _mm_nt(cg, bg)
                yin = _mm(cg, _bf(st[:, gs])) * ecum_x[:, gs]
                for j in range(4):
                    h0 = 8 * g + 2 * j
                    cs = slice(h0 * SSD_P, (h0 + 2) * SSD_P)
                    xp = xdt[:, cs]
                    s0 = _bf(cb * _ssd_decay(cum_r, cumt_r, h0, causal))
                    s1 = _bf(cb * _ssd_decay(cum_r, cumt_r, h0 + 1, causal))
                    y_ref[r, cs] = (_mm(s0, _bf(jnp.where(lo, xp, 0.0))) + _mm(s1, _bf(jnp.where(lo, 0.0, xp)))
                                    + yin[:, j * LANE:(j + 1) * LANE])
                st[:, gs] = st[:, gs] * elast_x[:, gs] + _mm_tn(bg, _bf(xrem[:, gs]))
            y1 = (y_ref[r, :] + dx_ref[...] * xs) * _silu(z)
            outs, _ = _group_norm_fwd(y1, nw_ref[...])
            for g in range(2):
                ycat_ref[r, g * 512:(g + 1) * 512] = _bf(outs[g])

    return pl.pallas_call(
        body, name="ssd_fwd", grid=(n // SSD_SUB,),
        in_specs=[pl.BlockSpec((rows, SSD_W + LANE), lambda i: (i, OFF_Z // (SSD_W + LANE))),
                  pl.BlockSpec((rows, SSD_CONV), lambda i: (i, 0)), _spec(bias), _spec(alog), _spec(dskip_x), _spec(nw),
                  _full(e64.shape), _full(tril.shape), pl.BlockSpec(memory_space=pl.ANY)],
        out_specs=[pl.BlockSpec((rows, SSD_W), lambda i: (i, 1)), pl.BlockSpec((rows, SSD_W), lambda i: (i, 0)),
                   pl.BlockSpec((SSD_SUB, SSD_N, SSD_W), lambda i: (i, 0, 0))],
        out_shape=[SDS((S, D_INNER), BF16), SDS((S,SSD_W), F32), SDS((n, SSD_N, SSD_W), F32)],
        scratch_shapes=[pltpu.VMEM((SSD_N, SSD_W), F32), pltpu.VMEM((SSD_SUB, LANE, T), F32),
                        pltpu.VMEM((SSD_SUB, T, LANE), F32)],
        input_output_aliases={8: 0},
        compiler_params=_cp(("arbitrary",)),
    )(u, xbc, _arr(bias), _arr(alog), _arr(dskip_x), _arr(nw), _bfc(e64), _bfc(tril), ycat)


def _ssd_bwd(u, xbc, bias, alog, dskip_x, nw, consts, y_ssd, states, dycat, du, tok):
    S = u.shape[0]
    T = SSD_CHUNK
    n = S // T
    e64, tril, triu = consts
    e64t = np.ascontiguousarray(e64.T)

    def chunk(u_ref, xbc_ref, bias_ref, alog_ref, dx_ref, nw_ref, e64_ref, e64t_ref, tril_ref, triu_ref,
              y_ref, st_ref, dy_ref, du_ref, dxbc_ref, red_ref, dst, dl_s, cumt, dxdt_s, dy0_s, gb_s, gc_s, cum_e, cs_s):
        zdt = _f(u_ref[...])
        z = zdt[:, 0:SSD_W]
        xs = xbc_ref[:, 0:SSD_W]
        a_neg, dtpre, dt, ecum_x, erem_x, elast_x, dt_x = _ssd_common(
            zdt, bias_ref, alog_ref, tril_ref[...], e64_ref[...], cum_e, cumt)
        causal = _iota((T, T), 0) >= _iota((T, T), 1)
        lo = _iota((T, LANE), 1) < SSD_P
        xdt = xs * dt_x
        xrem = xdt * erem_x
        y = y_ref[...]
        dxv = dx_ref[...]
        nwv = nw_ref[...]
        sz = _silu(z)
        y0 = y + dxv * xs
        y1 = y0 * sz
        for g in range(2):
            gs = slice(g * 512, (g + 1) * 512)
            seg = y1[:, gs]
            inv = lax.rsqrt(jnp.mean(seg * seg, axis=-1, keepdims=True) + EPS)
            shat = seg * inv
            dyg = _f(dy_ref[:, gs])
            red_ref[0:1, gs] += jnp.sum(dyg * shat, axis=0, keepdims=True)
            dsh = dyg * nwv[:, gs]
            dy1g = inv * (dsh - shat * jnp.mean(dsh * shat, axis=-1, keepdims=True))
            du_ref[:, gs] = _bf(dy1g * y0[:, gs] * _dsilu(z[:, gs]))
            dy0_s[:, gs] = dy1g * sz[:, gs]
        dy0 = dy0_s[...]
        red_ref[1:2, :] += jnp.sum(dy0 * xs, axis=0, keepdims=True)
        dyin = dy0 * ecum_x
        lane = _iota((T, LANE), 1)
        dcum = jnp.zeros((T, LANE), F32)

        def decay_grad(h, gm):
            cs_s[pl.ds(h, 1), :] = jnp.sum(gm, axis=0, keepdims=True)
            return jnp.where(lane == h, jnp.sum(gm, axis=1, keepdims=True), 0.0)

        for g in range(2):
            gs = slice(g * 512, (g + 1) * 512)
            bg = _bf(xbc_ref[:, SSD_W + g * LANE:SSD_W + (g + 1) * LANE])
            cg = _bf(xbc_ref[:, SSD_W + 256 + g * LANE:SSD_W + 256 + (g + 1) * LANE])
            cb = _mm_nt(cg, bg)
            dst_f, st_f = dst[:, gs], st_ref[:, gs]
            dstg = _bf(dst_f)
            stg = _bf(st_f)
            dyin_g = _bf(dyin[:, gs])
            xrem_g = _bf(xrem[:, gs])
            dcb = jnp.zeros((T, T), F32)
            dxr = _mm(bg, dstg)
            dxdt_s[:, gs] = dxr * erem_x[:, gs]
            gc_s[:, gs] = dxr * xrem[:, gs]
            gb_s[:, gs] = dyin[:, gs] * _mm(cg, stg)
            dl_s[:, gs] = jnp.sum(dst_f * st_f, axis=0, keepdims=True) * elast_x[:, gs]
            for j in range(4):
                h0 = 8 * g + 2 * j
                cs = slice(h0 * SSD_P, (h0 + 2) * SSD_P)
                xp = xdt[:, cs]
                dyp = dy0[:, cs]
                x_lo, x_hi = _bf(jnp.where(lo, xp, 0.0)), _bf(jnp.where(lo, 0.0, xp))
                d_lo, d_hi = _bf(jnp.where(lo, dyp, 0.0)), _bf(jnp.where(lo, 0.0, dyp))
                l0 = _ssd_decay(cum_e, cumt, h0, causal)
                l1 = _ssd_decay(cum_e, cumt, h0 + 1, causal)
                s0 = cb * l0
                s1 = cb * l1
                ds0 = _mm_nt(d_lo, x_lo)
                ds1 = _mm_nt(d_hi, x_hi)
                dcb = dcb + ds0 * l0 + ds1 * l1
                dxdt_s[:, cs] += _mm_tn(_bf(s0), d_lo) + _mm_tn(_bf(s1), d_hi)
                dcum = dcum + decay_grad(h0, ds0 * s0) + decay_grad(h0 + 1, ds1 * s1)
            dcb_b = _bf(dcb)
            dxbc_ref[:, SSD_W + g * LANE:SSD_W + (g + 1) * LANE] = _mm_tn(dcb_b, cg) + _mm_nt(xrem_g, dstg)
            dxbc_ref[:, SSD_W + 256 + g * LANE:SSD_W + 256 + (g + 1) * LANE] = _mm(dcb_b, bg) + _mm_nt(dyin_g, stg)
            dst[:, gs] = dst_f * elast_x[:, gs] + _mm_tn(cg, dyin_g)
        dxdt = dxdt_s[...]
        dxbc_ref[:, 0:SSD_W] = dxdt * dt_x + dy0 * dxv
        e64t = e64t_ref[...]
        gc = gc_s[...]
        dlast_x = jnp.sum(gc, axis=0, keepdims=True) + dl_s[...]
        dlast = jnp.max(_sel_r(jnp.broadcast_to(dlast_x, (8, SSD_W)), e64t), axis=0, keepdims=True)
        dcum = (dcum - cs_s[...].T + _sel_r(gb_s[...] - gc, e64t)
                + jnp.where(_iota((T, LANE), 0) == T - 1, dlast, 0.0))
        dda = _sel_l(triu_ref[...], dcum)
        ddt = dda * a_neg + _sel_r(dxdt * xs, e64t)
        ddtpre = ddt * _sigmoid(dtpre)
        du_ref[:, SSD_W:SSD_W + LANE] = _bf(jnp.where(lane < SSD_HEADS, ddtpre, 0.0))
        red_ref[2:3, 0:LANE] += jnp.sum(ddtpre, axis=0, keepdims=True)
        red_ref[3:4, 0:LANE] += jnp.sum(dda * dt, axis=0, keepdims=True)

    def body(u_ref, xbc_ref, bias_ref, alog_ref, dx_ref, nw_ref, e64_ref, e64t_ref, tril_ref, triu_ref,
             y_ref, st_ref, dy_ref, du_in, tok_ref, du_ref, dxbc_ref, red_ref, dst, *scratch):
        del du_in, tok_ref

        @pl.when(pl.program_id(0) == 0)
        def _():
            dst[...] = jnp.zeros_like(dst)
            red_ref[...] = jnp.zeros_like(red_ref)
            scratch[-1][...] = jnp.zeros_like(scratch[-1])

        for sub in reversed(range(SSD_SUB)):
            rs = pl.ds(sub * T, T)
            chunk(u_ref.at[rs], xbc_ref.at[rs], bias_ref, alog_ref, dx_ref, nw_ref, e64_ref, e64t_ref, tril_ref, triu_ref,
                  y_ref.at[rs], st_ref.at[sub], dy_ref.at[rs], du_ref.at[rs], dxbc_ref.at[rs], red_ref, dst,
                  *[s.at[sub] for s in scratch])

    nb = n // SSD_SUB
    rows = SSD_SUB * T
    rev = lambda i: (nb - 1 - i, 0)
    sub_scratch = [(1, SSD_W), (LANE, T)] + [(T, SSD_W)] * 4 + [(T, LANE), (LANE, T)]
    return pl.pallas_call(
        body, name="ssd_bwd", grid=(nb,),
        in_specs=[pl.BlockSpec((rows, SSD_W + LANE), lambda i: (nb - 1 - i, OFF_Z // (SSD_W + LANE))),
                  pl.BlockSpec((rows, SSD_CONV), rev), _spec(bias), _spec(alog), _spec(dskip_x), _spec(nw),
                  _full(e64.shape), _full(e64t.shape), _full(tril.shape), _full(triu.shape),
                  pl.BlockSpec((rows, SSD_W), rev), pl.BlockSpec((SSD_SUB, SSD_N, SSD_W), lambda i: (nb - 1 - i, 0, 0)),
                  pl.BlockSpec((rows, SSD_W), lambda i: (nb - 1 - i, 1)), pl.BlockSpec(memory_space=pl.ANY),
                  pl.BlockSpec(memory_space=pl.ANY)],
        out_specs=[pl.BlockSpec((rows, SSD_W + LANE), lambda i: (nb - 1 - i, OFF_Z // (SSD_W + LANE))),
                   pl.BlockSpec((rows, SSD_CONV), rev), pl.BlockSpec((8, SSD_W), lambda i: (0, 0))],
        out_shape=[SDS((S, N_PAD), BF16), SDS((S, SSD_CONV), F32), SDS((8, SSD_W), F32)],
        scratch_shapes=[pltpu.VMEM((SSD_N, SSD_W), F32)] + [pltpu.VMEM((SSD_SUB,) + s, F32) for s in sub_scratch],
        input_output_aliases={13: 0},
        compiler_params=_cp(("arbitrary",)),
    )(u, xbc, _arr(bias), _arr(alog), _arr(dskip_x), _arr(nw), _bfc(e64), _bfc(e64t), _bfc(tril), _bfc(triu), y_ssd,
      states, dycat, du, tok)


def _bfc(a):
    return jnp.asarray(a, BF16)


def _outproj_fwd(ycat, wo, x, gate, tok):
    S = x.shape[0]
    tm = min(512, S)

    def body(yc_ref, wo_ref, x_ref, g_ref, tok_ref, xn_ref, y_ref):
        del tok_ref
        y = _mm(_bf(yc_ref[...]), wo_ref[...])
        y_ref[...] = y
        xn_ref[...] = x_ref[...] + g_ref[...] * y

    row = pl.BlockSpec((tm, D_MODEL), lambda i: (i, 0))
    return pl.pallas_call(
        body, name="outproj_fwd", grid=(S // tm,),
        in_specs=[pl.BlockSpec((tm, D_INNER), lambda i: (i, 0)), _full((D_INNER, D_MODEL)), row, _spec(gate),
                  pl.BlockSpec(memory_space=pl.ANY)],
        out_specs=[row, row],
        out_shape=[SDS((S, D_MODEL), F32), SDS((S, D_MODEL), F32)],
        compiler_params=_cp(("parallel",)),
    )(ycat, wo, x, _arr(gate), tok)


def _outproj_bwd(dxn, y, gate, ycat, wo):
    S = dxn.shape[0]
    tm = min(512, S)

    def body(dx_ref, y_ref, g_ref, yc_ref, wo_ref, dyc_ref, gwo_ref, dg_ref, acc):
        @pl.when(pl.program_id(0) == 0)
        def _():
            acc[...] = jnp.zeros_like(acc)
            dg_ref[...] = jnp.zeros_like(dg_ref)

        dxv = dx_ref[...]
        dy = _bf(dxv * g_ref[...])
        dg_ref[0:1, :] += jnp.sum(dxv * y_ref[...], axis=0, keepdims=True)
        dyc_ref[...] = _mm_nt(dy, wo_ref[...])
        acc[...] += _mm_tn(_bf(yc_ref[...]), dy)

        @pl.when(pl.program_id(0) == pl.num_programs(0) - 1)
        def _():
            gwo_ref[...] = acc[...].astype(BF16)

    row = pl.BlockSpec((tm, D_MODEL), lambda i: (i, 0))
    wide = pl.BlockSpec((tm, D_INNER), lambda i: (i, 0))
    return pl.pallas_call(
        body, name="outproj_bwd", grid=(S // tm,),
        in_specs=[row, row, _spec(gate), wide, _full((D_INNER, D_MODEL))],
        out_specs=[wide, _full((D_INNER, D_MODEL)), _full((8, D_MODEL))],
        out_shape=[SDS((S, D_INNER), F32), SDS((D_INNER, D_MODEL), BF16), SDS((8, D_MODEL), F32)],
        scratch_shapes=[pltpu.VMEM((D_INNER, D_MODEL), F32)],
        compiler_params=_cp(("arbitrary",)),
    )(dxn, y, _arr(gate), ycat, wo)


def _loss_head(x, fw, target):
    S = x.shape[0]
    tm = min(512, S)

    def body(x_ref, fw_ref, t_ref, dx_ref, red_ref):
        @pl.when(pl.program_id(0) == 0)
        def _():
            red_ref[...] = jnp.zeros_like(red_ref)

        xv = x_ref[...]
        fwv = fw_ref[...]
        inv = lax.rsqrt(jnp.mean(xv * xv, axis=-1, keepdims=True) + EPS)
        xhat = xv * inv
        err = xhat * fwv - t_ref[...]
        col = jnp.sum(err * err, axis=0, keepdims=True)
        red_ref[1:2, :] += jnp.broadcast_to(jnp.sum(col, axis=1, keepdims=True) * (0.5 / D_MODEL), (1, D_MODEL))
        dy = err * (1.0 / D_MODEL)
        red_ref[0:1, :] += jnp.sum(dy * xhat, axis=0, keepdims=True)
        dxhat = dy * fwv
        dx_ref[...] = inv * (dxhat - xhat * jnp.mean(dxhat * xhat, axis=-1, keepdims=True))

    row = pl.BlockSpec((tm, D_MODEL), lambda i: (i, 0))
    return pl.pallas_call(
        body, name="loss_head", grid=(S // tm,),
        in_specs=[row, _vec(D_MODEL), row],
        out_specs=[row, _full((8, D_MODEL))],
        out_shape=[SDS((S, D_MODEL), F32), SDS((8, D_MODEL), F32)],
        compiler_params=_cp(("arbitrary",)),
    )(x, fw, target)


ADA_COLS = 3 * D_MODEL // N_DEV


def _ada_fwd(c_all, w_ada, b_cols):
    def body(c_ref, w_ref, b_ref, out_ref):
        out_ref[...] = _mm(_bf(_silu(c_ref[...])), _bf(w_ref[...])) + b_ref[...]

    return pl.pallas_call(
        body, name="ada_fwd", grid=(DEPTH,),
        in_specs=[_full((N_DEV, D_MODEL)), pl.BlockSpec((None, D_MODEL, ADA_COLS), lambda l: (l, 0, 0)),
                  pl.BlockSpec((None, 1, ADA_COLS), lambda l: (l, 0, 0))],
        out_specs=pl.BlockSpec((None, N_DEV, ADA_COLS), lambda l: (l, 0, 0)),
        out_shape=SDS((DEPTH, N_DEV, ADA_COLS), F32),
        compiler_params=_cp(("parallel",)),
    )(c_all, w_ada, b_cols)


def _ada_bwd(ct_pad, dmod_pad):
    def body(c_ref, d_ref, out_ref):
        out_ref[...] = _mm(_bf(_silu(c_ref[...])), _bf(d_ref[...]))

    return pl.pallas_call(
        body, name="ada_bwd", grid=(DEPTH,),
        in_specs=[_full((D_MODEL, LANE)), pl.BlockSpec((None, LANE, ADA_COLS), lambda l: (l, 0, 0))],
        out_specs=pl.BlockSpec((None, D_MODEL, ADA_COLS), lambda l: (l, 0, 0)),
        out_shape=SDS((DEPTH, D_MODEL, ADA_COLS), F32),
        compiler_params=_cp(("parallel",)),
    )(ct_pad, dmod_pad)


def _adamw(parts, w, m, v, name, own=None, layers=None, prev=None):
    n, L, R, C = parts.shape
    lo, hi = layers or (0, L)
    tr = R
    while tr * C * 4 > (1 << 20) and tr % 16 == 0:
        tr //= 2
    first = 1 if own is None else 2

    def body(*refs):
        p_ref = refs[0]
        w_ref, m_ref, v_ref = refs[first:first + 3]
        g_ref, d_ref, mo_ref, vo_ref = refs[-4:]

        def part(k):
            if own is None:
                return p_ref[k].astype(F32)
            me = 4 * lax.axis_index("x") + 2 * lax.axis_index("y") + lax.axis_index("c")
            return jnp.where(me == k, refs[1][...], p_ref[k]).astype(F32)

        g = part(0)
        for k in range(1, n):
            g = g + part(k)
        mn = ADAM_B1 * m_ref[...] + (1.0 - ADAM_B1) * g
        vn = ADAM_B2 * v_ref[...] + (1.0 - ADAM_B2) * (g * g)
        m_hat = mn / (1.0 - ADAM_B1 ** ADAM_STEP)
        v_hat = vn / (1.0 - ADAM_B2 ** ADAM_STEP)
        g_ref[...] = g
        d_ref[...] = -ADAM_LR * (m_hat / (jnp.sqrt(v_hat) + ADAM_EPS) + ADAM_WD * w_ref[...])
        mo_ref[...] = mn
        vo_ref[...] = vn

    blk = pl.BlockSpec((None, tr, C), lambda l, i: (lo + l, i, 0))
    own_blk = [] if own is None else [pl.BlockSpec((None, tr, C), lambda l, i: (l, i, 0))]
    n_blk = 3 if own is None else 4
    return pl.pallas_call(
        body, name=name, grid=(hi - lo, R // tr),
        in_specs=[pl.BlockSpec((n, None, tr, C), lambda l, i: (0, lo + l, i, 0))] + own_blk + [blk] * 3
        + ([] if prev is None else [ANY] * 4),
        out_specs=[blk] * 4,
        out_shape=[SDS((L, R, C), F32)] * 4,
        input_output_aliases={} if prev is None else {1 + n_blk + k: k for k in range(4)},
        compiler_params=_cp(("parallel", "parallel")),
    )(parts, *([] if own is None else [own]), w, m, v, *([] if prev is None else prev))


MESH = pl.DeviceIdType.MESH
ANY = pl.BlockSpec(memory_space=pl.ANY)


def _all_gather(v, name):
    def body(v_ref, out_ref, send_sems, recv_sems, local_sem):
        x, y, c = lax.axis_index("x"), lax.axis_index("y"), lax.axis_index("c")
        me, sibling = (x, y, c), (x, y, 1 - c)
        chips = [(1 - x, y), (x, 1 - y), (1 - x, 1 - y)]

        def slot(px, py, pc):
            return out_ref.at[4 * px + 2 * py + pc]

        def copy(k, block, to, src=None):
            return pltpu.make_async_remote_copy(
                src_ref=slot(*block) if src is None else src, dst_ref=slot(*block),
                send_sem=send_sems.at[k], recv_sem=recv_sems.at[k], device_id=to, device_id_type=MESH)

        mine = pltpu.make_async_copy(v_ref, slot(*me), local_sem)
        mine.start()
        first = [copy(0, me, sibling, src=v_ref)]
        first += [copy(1 + j, me, (*chip, c), src=v_ref) for j, chip in enumerate(chips)]
        for cp in first:
            cp.start()
        passed = [copy(4 + j, (*chip, c), sibling) for j, chip in enumerate(chips)]
        for j, chip in enumerate(chips):
            copy(1 + j, (*chip, c), me).wait_recv()
            passed[j].start()
        copy(0, sibling, me).wait_recv()
        for j, chip in enumerate(chips):
            copy(4 + j, (*chip, 1 - c), me).wait_recv()
        for cp in first + passed:
            cp.wait_send()
        mine.wait()

    return pl.pallas_call(
        body, name=name, in_specs=[ANY], out_specs=ANY,
        out_shape=SDS((N_DEV,) + v.shape, v.dtype),
        scratch_shapes=[pltpu.SemaphoreType.DMA((7,)), pltpu.SemaphoreType.DMA((7,)), pltpu.SemaphoreType.DMA],
    )(v)


def _all_to_all(v, name):
    def body(v_ref, out_ref, send_sems, recv_sems, local_sem):
        x, y, c = lax.axis_index("x"), lax.axis_index("y"), lax.axis_index("c")
        mine_idx = 4 * x + 2 * y + c
        mine = pltpu.make_async_copy(v_ref.at[mine_idx], out_ref.at[mine_idx], local_sem)
        mine.start()
        sends, recvs = [], []
        for k in range(1, N_DEV):
            px = 1 - x if k & 4 else x
            py = 1 - y if k & 2 else y
            pc = 1 - c if k & 1 else c
            peer_idx = 4 * px + 2 * py + pc
            sems = dict(send_sem=send_sems.at[k - 1], recv_sem=recv_sems.at[k - 1], device_id=(px, py, pc),
                        device_id_type=MESH)
            sends.append(pltpu.make_async_remote_copy(src_ref=v_ref.at[peer_idx], dst_ref=out_ref.at[mine_idx], **sems))
            recvs.append(pltpu.make_async_remote_copy(src_ref=v_ref.at[peer_idx], dst_ref=out_ref.at[peer_idx], **sems))
        for cp in sends:
            cp.start()
        for cp in recvs:
            cp.wait_recv()
        for cp in sends:
            cp.wait_send()
        mine.wait()

    return pl.pallas_call(
        body, name=name, in_specs=[ANY], out_specs=ANY,
        out_shape=SDS(v.shape, v.dtype),
        scratch_shapes=[pltpu.SemaphoreType.DMA((7,)), pltpu.SemaphoreType.DMA((7,)), pltpu.SemaphoreType.DMA],
    )(v)


HBM_SPEC = pl.BlockSpec(memory_space=pltpu.HBM)
SEM_SPEC = pl.BlockSpec(memory_space=pltpu.SEMAPHORE)
EFFECT = pltpu.SideEffectType.DATAFLOW_SIDE_EFFECTING


EXCHANGE_PEERS = {"gather": range(1, N_DEV), "scatter": range(1, N_DEV), "chip": (1, 2, 4, 6), "pass": (2, 4, 6)}


def _exchange_copies(srcs, lands, send_sems, recv_sems, mode, layer):
    x, y, c = lax.axis_index("x"), lax.axis_index("y"), lax.axis_index("c")
    me = 4 * x + 2 * y + c
    copies = []
    for a, (src, land) in enumerate(zip(srcs, lands)):
        for k in EXCHANGE_PEERS[mode]:
            px = 1 - x if k & 4 else x
            py = 1 - y if k & 2 else y
            pc = 1 - c if k & 1 else c
            peer = 4 * px + 2 * py + pc
            if mode == "scatter":
                s, d, to = src.at[peer], land.at[me, layer], (px, py, pc)
            elif mode == "pass":
                s, d, to = land.at[peer], land.at[peer], (x, y, 1 - c)
            else:
                s, d, to = src, land.at[me], (px, py, pc)
            n = 7 * a + k - 1
            copies.append(pltpu.make_async_remote_copy(
                src_ref=s, dst_ref=d, send_sem=send_sems.at[n], recv_sem=recv_sems.at[n], device_id=to,
                device_id_type=MESH))
    return copies


def _exchange_start(name, srcs, lands, mode, layer=0, after=None):
    n = len(srcs)

    def body(*refs):
        send_sems, recv_sems = refs[-2 * n - 3], refs[-2 * n - 2]
        for cp in _exchange_copies(refs[:n], refs[n:2 * n], send_sems, recv_sems, mode, layer):
            cp.start()
        refs[-1][...] = jnp.zeros_like(refs[-1])

    arrays = list(srcs) + list(lands)
    sems = pltpu.SemaphoreType.DMA((7 * n,))
    out = pl.pallas_call(
        body, name=name,
        out_shape=(sems, sems, *[pltpu.HBM(v.shape, v.dtype) for v in arrays], SDS((8, LANE), F32)),
        in_specs=[HBM_SPEC] * (2 * n) + ([ANY] if after is not None else []),
        out_specs=(SEM_SPEC, SEM_SPEC, *[HBM_SPEC] * (2 * n), pl.BlockSpec(memory_space=pltpu.VMEM)),
        input_output_aliases={i: 2 + i for i in range(2 * n)},
        compiler_params=pltpu.CompilerParams(has_side_effects=EFFECT),
    )(*[pltpu.with_memory_space_constraint(v, pltpu.HBM) for v in arrays], *([after] if after is not None else []))
    return dict(sems=out[:2], srcs=out[2:2 + n], lands=out[2 + n:2 + 2 * n], token=out[-1], mode=mode,
                layer=layer)


def _exchange_wait(name, st, after, also=()):
    n = len(st["srcs"])

    def body(*refs):
        send_sems, recv_sems = refs[2 * n], refs[2 * n + 1]
        for cp in _exchange_copies(refs[:n], refs[n:2 * n], send_sems, recv_sems, st["mode"], st["layer"]):
            cp.wait_send()
            cp.wait_recv()

    arrays = list(st["srcs"]) + list(st["lands"])
    out = pl.pallas_call(
        body, name=name,
        out_shape=tuple(pltpu.HBM(v.shape, v.dtype) for v in arrays),
        in_specs=[HBM_SPEC] * (2 * n) + [SEM_SPEC, SEM_SPEC] + [ANY] * (1 + len(also)),
        out_specs=tuple([HBM_SPEC] * (2 * n)),
        input_output_aliases={i: i for i in range(2 * n)},
        compiler_params=pltpu.CompilerParams(has_side_effects=EFFECT),
    )(*arrays, *st["sems"], after, *also)
    st["srcs"] = out[:n]
    return out[n:]


_IN_PIECES = ([(1024, 3072)]
              + [r for t in range(4) for r in ((LANE * t, LANE * (t + 1)), (512 + LANE * t, 512 + LANE * (t + 1)))]
              + [(4096, 5632), (3072, 4096), (5632, 5648)])


def _permute_in(w):
    pad = jnp.zeros(w.shape[:-1] + (N_PAD - N_IN,), w.dtype)
    return jnp.concatenate([w[..., a:b] for a, b in _IN_PIECES] + [pad], axis=-1)


def _unpermute_in(g):
    ax = [g[..., OFF_LRU + 2 * LANE * t:OFF_LRU + 2 * LANE * t + LANE] for t in range(4)]
    ag = [g[..., OFF_LRU + 2 * LANE * t + LANE:OFF_LRU + 2 * LANE * (t + 1)] for t in range(4)]
    return jnp.concatenate(ax + ag + [g[..., 0:2048], g[..., OFF_Z:OFF_Z + SSD_W], g[..., OFF_XBC:OFF_XBC + SSD_CONV],
                                      g[..., OFF_Z + SSD_W:OFF_Z + SSD_W + SSD_HEADS]], axis=-1)


SHARD_COLS = N_IN // N_DEV


def _in_segments():
    segs, pos = [], 0
    for a, b in _IN_PIECES:
        for i in range(N_DEV):
            lo, hi = max(a, SHARD_COLS * i), min(b, SHARD_COLS * (i + 1))
            if lo < hi:
                segs.append((i, lo - SHARD_COLS * i, hi - lo, pos + lo - a))
        pos += b - a
    return segs


RELAYOUT_ROWS = 256


def _relayout_in(land, own):
    def body(land_ref, own_ref, out_ref):
        me = 4 * lax.axis_index("x") + 2 * lax.axis_index("y") + lax.axis_index("c")
        out_ref[:, N_IN:N_PAD] = jnp.zeros((RELAYOUT_ROWS, N_PAD - N_IN), BF16)
        for i, j, wd, p in _in_segments():
            out_ref[:, p:p + wd] = jnp.where(me == i, own_ref[:, j:j + wd], land_ref[i, :, j:j + wd])

    return pl.pallas_call(
        body, name="relayout_in", grid=(D_MODEL // RELAYOUT_ROWS,),
        in_specs=[pl.BlockSpec((N_DEV, RELAYOUT_ROWS, SHARD_COLS), lambda r: (0, r, 0)),
                  pl.BlockSpec((RELAYOUT_ROWS, SHARD_COLS), lambda r: (r, 0))],
        out_specs=pl.BlockSpec((RELAYOUT_ROWS, N_PAD), lambda r: (r, 0)),
        out_shape=SDS((D_MODEL, N_PAD), BF16),
        compiler_params=_cp(("parallel",)),
    )(land, own)


def _relayout_grad(g):
    def body(g_ref, out_ref):
        for i, j, wd, p in _in_segments():
            out_ref[i, :, j:j + wd] = g_ref[:, p:p + wd].astype(BF16)

    return pl.pallas_call(
        body, name="relayout_grad", grid=(D_MODEL // RELAYOUT_ROWS,),
        in_specs=[pl.BlockSpec((RELAYOUT_ROWS, N_PAD), lambda r: (r, 0))],
        out_specs=pl.BlockSpec((N_DEV, RELAYOUT_ROWS, SHARD_COLS), lambda r: (0, r, 0)),
        out_shape=SDS((N_DEV, D_MODEL, SHARD_COLS), BF16),
        compiler_params=_cp(("parallel",)),
    )(g)


def _block_diag(w):
    w4 = w.reshape(DEPTH, 4, 2, 64, 64)
    z = jnp.zeros((DEPTH, 4, 64, 64), w.dtype)
    top = jnp.concatenate([w4[:, :, 0], z], axis=-1)
    bot = jnp.concatenate([z, w4[:, :, 1]], axis=-1)
    return jnp.concatenate([top, bot], axis=2).astype(BF16)


def _diag_blocks(g):
    return jnp.stack([g[:, :, :64, :64], g[:, :, 64:, 64:]], axis=2).reshape(DEPTH, 8, 64, 64)


def _pad_lanes(v):
    return jnp.pad(v, ((0, 0), (0, LANE - v.shape[1])))


def _lower_bounds(logits):
    p = jax.nn.softmax(logits, axis=0)
    return p, jnp.cumsum(p, axis=0) - p[0]


def _lower_bounds_bwd(p, dlb):
    dp = jnp.cumsum(dlb[::-1], axis=0)[::-1]
    dp = dp.at[0].add(-jnp.sum(dlb, axis=0))
    return p * (dp - jnp.sum(dp * p, axis=0, keepdims=True))


SMALL = ["norm_w", "b_ada", "lru_conv_b", "lru_wa", "lru_ba", "lru_wx", "lru_bx", "lru_lambda", "hg_lb_logits",
         "hg_norm_w", "ssd_conv_b", "ssd_dt_bias", "ssd_a_log", "ssd_d", "ssd_norm_w", "final_norm_w"]
WEIGHTS = ["norm_w", "w_ada", "b_ada", "w_in", "lru_conv_w", "lru_conv_b", "lru_wa", "lru_ba", "lru_wx", "lru_bx",
           "lru_lambda", "hg_lb_logits", "hg_norm_w", "ssd_conv_w", "ssd_conv_b", "ssd_dt_bias", "ssd_a_log", "ssd_d",
           "ssd_norm_w", "w_out", "final_norm_w"]
INPUTS = ["x", "c"] + WEIGHTS + ["loss_target"] + ["m_" + n for n in WEIGHTS] + ["v_" + n for n in WEIGHTS]
SMALL_ROW = 1024


def _small_rows(like):
    out, off = {}, 0
    for n in SMALL:
        rows = -(-int(np.prod(like[n].shape)) // (8 * SMALL_ROW)) * 8
        out[n] = (off, rows)
        off += rows
    return out, off


def _flatten_small(d, prefix="", last=0.0):
    table, _ = _small_rows({n: d[prefix + n] for n in SMALL})
    pieces = []
    for n in SMALL:
        flat = d[prefix + n].reshape(-1)
        pieces.append(jnp.pad(flat, (0, table[n][1] * SMALL_ROW - flat.shape[0])).reshape(-1, SMALL_ROW))
    return jnp.concatenate(pieces + [jnp.full((8, SMALL_ROW), last, F32)], axis=0)


def _split_small(packed, like):
    table, _ = _small_rows(like)
    out = {}
    for n in SMALL:
        off, rows = table[n]
        size = int(np.prod(like[n].shape))
        out[n] = packed[off:off + rows].reshape(-1)[:size].reshape(like[n].shape)
    return out


def _local_step(x, mod, target, w, fetch, emit):
    S = x.shape[0]
    mall = _bfc(_hg_consts())
    mall_t = _bfc(_hg_consts().T)
    consts = _ssd_consts()
    p_lb, lbs = _lower_bounds(w["hg_lb_logits"])
    no_tok = jnp.zeros((8, LANE), F32)
    wa, wx = _block_diag(w["lru_wa"]), _block_diag(w["lru_wx"])
    ba, bx = w["lru_ba"].reshape(DEPTH, 1, LRU_W), w["lru_bx"].reshape(DEPTH, 1, LRU_W)
    lru_cb, lam, ssd_cb = w["lru_conv_b"][:, None], w["lru_lambda"][:, None], w["ssd_conv_b"][:, None]
    bias, alog = _pad_lanes(w["ssd_dt_bias"]), _pad_lanes(w["ssd_a_log"])
    dskip = jnp.repeat(w["ssd_d"], SSD_P, axis=1)
    saved = []
    for l in range(DEPTH):
        w_in_l, w_out_l, token = fetch(l, x)
        shift, scale, gate = (_Row(mod, l, D_MODEL, k) for k in range(3))
        nw = _Row(w["norm_w"], l)
        u, h = _inproj_fwd(x, nw, scale, shift, w_in_l, no_tok if token is None else token)
        ycat = lax.empty((S, D_INNER), BF16)
        lru_args = (l, u, w["lru_conv_w"], lru_cb, wa, ba, wx, bx, lam)
        ycat, h_lru = _lru_fwd(*lru_args, ycat)
        hg_args = (u, _Row(lbs, l), _Row(w["hg_norm_w"], l), mall)
        ycat, o_b, hg_st = _hg_fwd(*hg_args, ycat)
        xbc = _ssdconv_fwd(l, u, w["ssd_conv_w"], ssd_cb)
        ssd_args = (u, xbc, _Row(bias, l), _Row(alog, l), _Row(dskip, l), _Row(w["ssd_norm_w"], l), consts)
        ycat, y_ssd, ssd_st = _ssd_fwd(*ssd_args, ycat)
        token = fetch(l, y_ssd, late=True)
        x_new, y = _outproj_fwd(ycat, w_out_l, x, gate, no_tok if token is None else token)
        saved.append((x, u, h, ycat, nw, scale, gate, w_in_l, w_out_l, lru_args, h_lru, hg_args, o_b, hg_st, ssd_args,
                      y_ssd, ssd_st, y))
        x = x_new
    dx, red = _loss_head(x, w["final_norm_w"][None, :], target)
    loss = red[1, 0]
    reds = {k: [None] * DEPTH for k in ("in", "gate", "lru", "wa", "wx", "hg", "conv", "ssd")}
    for l in reversed(range(DEPTH)):
        (x, u, h, ycat, nw, scale, gate, w_in_l, w_out_l, lru_args, h_lru, hg_args, o_b, hg_st, ssd_args, y_ssd, ssd_st,
         y) = saved[l]
        dycat, g_out, reds["gate"][l] = _outproj_bwd(dx, y, gate, ycat, w_out_l)
        token = emit(l, "w_out", g_out)
        du = lax.empty((S, N_PAD), BF16)
        du, dxbc, reds["ssd"][l] = _ssd_bwd(*ssd_args, y_ssd, ssd_st, dycat, du, no_tok if token is None else token)
        du, reds["conv"][l] = _ssdconv_bwd(l, u, w["ssd_conv_w"], ssd_cb, dxbc, du)
        du, reds["hg"][l] = _hg_bwd(*hg_args, mall_t, o_b, hg_st, dycat, du)
        du, reds["lru"][l], reds["wa"][l], reds["wx"][l] = _lru_bwd(*lru_args, h_lru, dycat, du)
        token = emit(l, "w_in", functools.partial(_inproj_bwd_w, h, du))
        dx, reds["in"][l] = _inproj_bwd_x(du, w_in_l, x, nw, scale, dx, no_tok if token is None else token)
    r = {k: jnp.stack(v) for k, v in reds.items()}
    g = {n: None for n in WEIGHTS}
    g["final_norm_w"] = red[0]
    g["norm_w"] = r["in"][:, 2]
    dmod = jnp.concatenate([r["in"][:, 0], r["in"][:, 1], r["gate"][:, 0]], axis=1)
    g["lru_conv_w"], g["lru_conv_b"] = r["lru"][:, 0:4], r["lru"][:, 4]
    g["lru_ba"], g["lru_bx"] = r["lru"][:, 5].reshape(DEPTH, 8, 64), r["lru"][:, 6].reshape(DEPTH, 8, 64)
    g["lru_lambda"] = r["lru"][:, 7]
    g["lru_wa"], g["lru_wx"] = _diag_blocks(r["wa"]), _diag_blocks(r["wx"])
    g["hg_norm_w"] = r["hg"][:, 0]
    g["hg_lb_logits"] = _lower_bounds_bwd(p_lb, r["hg"][:, 1])
    g["ssd_conv_w"], g["ssd_conv_b"] = r["conv"][:, 0:4], r["conv"][:, 4]
    g["ssd_norm_w"] = r["ssd"][:, 0]
    g["ssd_d"] = r["ssd"][:, 1].reshape(DEPTH, SSD_HEADS, SSD_P).sum(-1)
    g["ssd_dt_bias"] = r["ssd"][:, 2, :SSD_HEADS]
    g["ssd_a_log"] = -r["ssd"][:, 3, :SSD_HEADS] * jnp.exp(w["ssd_a_log"])
    return loss, dx, dmod, g


def kernel(x, c, norm_w, w_ada, b_ada, w_in, lru_conv_w, lru_conv_b, lru_wa, lru_ba, lru_wx, lru_bx, lru_lambda, hg_lb_logits, hg_norm_w, ssd_conv_w, ssd_conv_b, ssd_dt_bias, ssd_a_log, ssd_d, ssd_norm_w, w_out, final_norm_w, loss_target, m_norm_w, m_w_ada, m_b_ada, m_w_in, m_lru_conv_w, m_lru_conv_b, m_lru_wa, m_lru_ba, m_lru_wx, m_lru_bx, m_lru_lambda, m_hg_lb_logits, m_hg_norm_w, m_ssd_conv_w, m_ssd_conv_b, m_ssd_dt_bias, m_ssd_a_log, m_ssd_d, m_ssd_norm_w, m_w_out, m_final_norm_w, v_norm_w, v_w_ada, v_b_ada, v_w_in, v_lru_conv_w, v_lru_conv_b, v_lru_wa, v_lru_ba, v_lru_wx, v_lru_bx, v_lru_lambda, v_hg_lb_logits, v_hg_norm_w, v_ssd_conv_w, v_ssd_conv_b, v_ssd_dt_bias, v_ssd_a_log, v_ssd_d, v_ssd_norm_w, v_w_out, v_final_norm_w):
    return _step(x, c, norm_w, w_ada, b_ada, w_in, lru_conv_w, lru_conv_b, lru_wa, lru_ba, lru_wx, lru_bx, lru_lambda, hg_lb_logits, hg_norm_w, ssd_conv_w, ssd_conv_b, ssd_dt_bias, ssd_a_log, ssd_d, ssd_norm_w, w_out, final_norm_w, loss_target, m_norm_w, m_w_ada, m_b_ada, m_w_in, m_lru_conv_w, m_lru_conv_b, m_lru_wa, m_lru_ba, m_lru_wx, m_lru_bx, m_lru_lambda, m_hg_lb_logits, m_hg_norm_w, m_ssd_conv_w, m_ssd_conv_b, m_ssd_dt_bias, m_ssd_a_log, m_ssd_d, m_ssd_norm_w, m_w_out, m_final_norm_w, v_norm_w, v_w_ada, v_b_ada, v_w_in, v_lru_conv_w, v_lru_conv_b, v_lru_wa, v_lru_ba, v_lru_wx, v_lru_bx, v_lru_lambda, v_hg_lb_logits, v_hg_norm_w, v_ssd_conv_w, v_ssd_conv_b, v_ssd_dt_bias, v_ssd_a_log, v_ssd_d, v_ssd_norm_w, v_w_out, v_final_norm_w)


def _step(*args):
    a = dict(zip(INPUTS, args, strict=True))
    me = 4 * lax.axis_index("x") + 2 * lax.axis_index("y") + lax.axis_index("c")
    x, target = a["x"][0], a["loss_target"][0]

    c_all = _all_gather(a["c"], "gather_c")[:, 0, :]
    b_cols = lax.dynamic_slice_in_dim(a["b_ada"], me * ADA_COLS, ADA_COLS, axis=1)[:, None, :]
    mod_parts = _all_gather(_ada_fwd(c_all, a["w_ada"], b_cols), "gather_mod")
    mod = lax.dynamic_index_in_dim(mod_parts, me, axis=2, keepdims=False)
    mod = mod.transpose(1, 0, 2).reshape(DEPTH, 3 * D_MODEL)

    w = {n: a[n] for n in SMALL}

    w_in_b = [a["w_in"][l].astype(BF16) for l in range(DEPTH)]
    w_out_b = a["w_out"].astype(BF16)
    conv_own = jnp.concatenate([a["lru_conv_w"], a["ssd_conv_w"]], axis=-1)
    cols, rows_out = N_IN // N_DEV, D_INNER // N_DEV

    def gather_start(l, after):
        srcs = [w_in_b[l], w_out_b[l]] + ([conv_own] if l == 0 else [])
        lands = [lax.empty((N_DEV,) + s.shape, s.dtype) for s in srcs]
        return _exchange_start(f"gather_start_{l}", srcs, lands, "chip", after=after)

    def gather_pass(name, st, after, also=()):
        landed = _exchange_wait(name + "_wait", st, after, also)
        st2 = _exchange_start(name + "_pass", st["srcs"], landed, "pass")
        return _exchange_wait(name + "_passed", st2, after)

    gathers = {0: gather_start(0, mod)}
    passing = {}

    def fetch(l, x_l, late=False):
        if late:
            if l + 1 == DEPTH:
                return None
            landed = _exchange_wait(f"gather_{l + 1}_wait", gathers[l + 1], x_l)
            passing[l + 1] = _exchange_start(f"gather_{l + 1}_pass", gathers[l + 1]["srcs"], landed, "pass")
            return passing[l + 1]["token"]
        if l == 0:
            landed = gather_pass("gather_0", gathers[0], x_l, also=(a["w_in"], a["m_w_in"], a["v_w_in"]))
        else:
            landed = _exchange_wait(f"gather_{l}_passed", passing[l], x_l)
        land_out = lax.dynamic_update_index_in_dim(landed[1], w_out_b[l], me, 0)
        if l == 0:
            conv = lax.dynamic_update_index_in_dim(landed[2], conv_own, me, 0).transpose(1, 2, 0, 3)
            w["lru_conv_w"] = conv[..., :64].reshape(DEPTH, 4, LRU_W)
            w["ssd_conv_w"] = conv[..., 64:].reshape(DEPTH, 4, SSD_CONV)
        token = None
        if l + 1 < DEPTH:
            gathers[l + 1] = gather_start(l + 1, land_out)
            token = gathers[l + 1]["token"]
        return _relayout_in(landed[0], w_in_b[l]), land_out.reshape(D_INNER, D_MODEL), token

    PROJ = ("w_in", "w_out")
    scatters = {}
    lands = [lax.empty((N_DEV, DEPTH, D_MODEL, cols), BF16), lax.empty((N_DEV, DEPTH, rows_out, D_MODEL), BF16)]
    own = [None] * DEPTH

    deferred, g_out = {}, {}

    def emit(l, name, grad, after=None):
        if name == "w_out" and l > 0:
            g_out[l] = grad
            return None
        if name == "w_in" and l == 0 and after is None:
            deferred["w_in"] = grad
            return None
        if name == "w_in":
            grad = grad(jnp.zeros((8, LANE), F32) if after is None else after)
        if l == 0:
            k = PROJ.index(name)
            src = _relayout_grad(grad) if name == "w_in" else grad.reshape(N_DEV, rows_out, D_MODEL)
            st = _exchange_start(f"scatter_start_0_{name}", [src], [lands[k]], "scatter", layer=0, after=after)
            scatters[name] = st
            lands[k] = st["lands"][0]
            return st["token"]
        srcs = [_relayout_grad(grad), g_out[l].reshape(N_DEV, rows_out, D_MODEL)]
        st = _exchange_start(f"scatter_start_{l}", srcs, lands, "scatter", layer=l, after=after)
        scatters[l] = st
        lands[:] = st["lands"]
        return st["token"]

    loss_own, dx, dmod, g = _local_step(x, mod, target, w, fetch, emit)

    def sharded(name, parts, own=None, **kw):
        return _adamw(parts, a[name], a["m_" + name], a["v_" + name], "adamw_" + name + kw.pop("tag", ""), own=own, **kw)

    g["b_ada"] = dmod
    small_own = _flatten_small(g, last=loss_own)
    small_st = _exchange_start("gather_small", [small_own], [lax.empty((N_DEV,) + small_own.shape, F32)], "chip",
                               after=dx)
    big = {}
    after = emit(0, "w_in", deferred["w_in"], after=small_st["token"]) + dx[0:8, 0:LANE]

    def own_slices(st):
        return [lax.dynamic_index_in_dim(s, me, 0, keepdims=False) for s in st["srcs"]]

    for l in reversed(range(1, DEPTH)):
        scatters[l]["lands"] = lands
        lands[:] = _exchange_wait(f"scatter_wait_{l}", scatters[l], after)
        own[l] = own_slices(scatters[l])
    scatters["w_out"]["lands"] = [lands[1]]
    lands[1] = _exchange_wait("scatter_wait_0_w_out", scatters["w_out"], after)[0]
    own[0] = [None, own_slices(scatters["w_out"])[0]]
    big["w_out"] = sharded("w_out", lands[1], jnp.stack([own[l][1] for l in range(DEPTH)]))
    upper = sharded("w_in", lands[0], jnp.stack([own[l][0] for l in range(1, DEPTH)]), layers=(1, DEPTH), tag="_upper")
    after = upper[1][0, 0:8, 0:LANE] + big["w_out"][1][0, 0:8, 0:LANE]
    small = gather_pass("gather_small", small_st, after)[0]
    outs = _adamw(small[:, None], *[_flatten_small(a, p)[None] for p in ("", "m_", "v_")], "adamw_small",
                  own=small_own[None])
    res = [_split_small(o[0], a) for o in outs]
    losses = lax.dynamic_update_index_in_dim(small[:, -1, 0], loss_own, me, 0)
    loss = jnp.sum(losses)

    off = _small_rows(a)[0]["b_ada"][0]
    dmod_all = lax.dynamic_update_index_in_dim(small[:, off:off + DEPTH * 3 * D_MODEL // SMALL_ROW],
                                               dmod.reshape(-1, SMALL_ROW), me, 0)
    dmod_all = dmod_all.reshape(N_DEV, DEPTH, 3 * D_MODEL).transpose(1, 0, 2)
    dmod_cols = lax.dynamic_slice_in_dim(dmod_all, me * ADA_COLS, ADA_COLS, axis=2)
    dmod_pad = jnp.pad(dmod_cols, ((0, 0), (0, LANE - N_DEV), (0, 0)))
    ct_pad = jnp.pad(c_all.T, ((0, 0), (0, LANE - N_DEV)))
    big["w_ada"] = sharded("w_ada", _ada_bwd(ct_pad, dmod_pad)[None])
    g_conv = jnp.concatenate([g["lru_conv_w"].reshape(DEPTH, 4, N_DEV, 64), g["ssd_conv_w"].reshape(DEPTH, 4, N_DEV, 192)],
                             axis=-1).transpose(2, 0, 1, 3)
    conv_parts = _all_to_all(g_conv, "scatter_conv")
    big["lru_conv_w"] = sharded("lru_conv_w", conv_parts[..., :64])
    big["ssd_conv_w"] = sharded("ssd_conv_w", conv_parts[..., 64:])

    after = outs[1] + big["w_ada"][1][0, 0:1, 0:1]
    scatters["w_in"]["lands"] = [lands[0]]
    lands[0] = _exchange_wait("scatter_wait_0_w_in", scatters["w_in"], after)[0]
    big["w_in"] = sharded("w_in", lands[0], own_slices(scatters["w_in"])[0][None], layers=(0, 1), prev=upper)

    out = [loss, dx[None]]
    for k in range(4):
        out += [big[n][k] if n in big else res[k][n] for n in WEIGHTS]
    return tuple(out)
```

```python
import functools

import numpy as np
import jax
import jax.numpy as jnp
from jax import lax
from jax.experimental import pallas as pl
from jax.experimental.pallas import tpu as pltpu

F32 = jnp.float32
BF16 = jnp.bfloat16
SDS = jax.ShapeDtypeStruct

N_DEV = 8
DEPTH = 4
D_MODEL = 1024
D_INNER = 2048
EPS = 1e-6
LRU_W = 512
LRU_C = 8.0
HG_W = 512
HG_CHUNK = 64
HG_HEADS = 4
SSD_W = 1024
SSD_HEADS = 16
SSD_P = 64
SSD_N = 128
SSD_CHUNK = 128
SSD_CONV = 1536
N_IN = 5648
N_PAD = 5760
OFF_HG = 0
OFF_LRU = 2048
OFF_XBC = 3072
OFF_Z = 4608
LANE = 128
VMEM_LIMIT = 56 * 1024 * 1024
NEG = -1e30

ADAM_LR = 0.001
ADAM_B1 = 0.9
ADAM_B2 = 0.999
ADAM_EPS = 1e-08
ADAM_WD = 0.01
ADAM_STEP = 10


def _cp(sem=None):
    return pltpu.CompilerParams(dimension_semantics=sem, vmem_limit_bytes=VMEM_LIMIT)


def _dg(a, b, ca, cb):
    return lax.dot_general(a, b, (((ca,), (cb,)), ((), ())), preferred_element_type=F32)


def _mm(a, b):
    return _dg(a, b, 1, 0)


def _mm_nt(a, b):
    return _dg(a, b, 1, 1)


def _mm_tn(a, b):
    return _dg(a, b, 0, 0)


def _bf(x):
    return x.astype(BF16)


def _f(x):
    return x.astype(F32)


def _split3(x):
    hi = x.astype(BF16)
    r = x - hi.astype(F32)
    mid = r.astype(BF16)
    lo = (r - mid.astype(F32)).astype(BF16)
    return hi, mid, lo


def _sel_r(x, m):
    hi, mid, lo = _split3(x)
    return _mm(hi, m) + _mm(mid, m) + _mm(lo, m)


def _sel_l(m, x):
    hi, mid, lo = _split3(x)
    return _mm(m, hi) + _mm(m, mid) + _mm(m, lo)


def _sel_l2(m, x):
    hi = x.astype(BF16)
    lo = (x - hi.astype(F32)).astype(BF16)
    return _mm(m, hi) + _mm(m, lo)


def _sel_tn(x, m):
    hi, mid, lo = _split3(x)
    return _mm_tn(hi, m) + _mm_tn(mid, m) + _mm_tn(lo, m)


def _sigmoid(x):
    return 1.0 / (1.0 + jnp.exp(-x))


def _silu(x):
    return x * _sigmoid(x)


def _dsilu(x):
    s = _sigmoid(x)
    return s * (1.0 + x * (1.0 - s))


def _softplus(x):
    return jnp.maximum(x, 0.0) + jnp.log(1.0 + jnp.exp(-jnp.abs(x)))


def _expm1(z):
    series = z * (1.0 + z * (1.0 / 2) * (1.0 + z * (1.0 / 3) * (1.0 + z * (1.0 / 4) * (
        1.0 + z * (1.0 / 5) * (1.0 + z * (1.0 / 6) * (1.0 + z * (1.0 / 7)))))))
    return jnp.where(jnp.abs(z) < 0.3, series, jnp.exp(z) - 1.0)


def _iota(shape, dim):
    return lax.broadcasted_iota(jnp.int32, shape, dim)


def _last_row(x, rows):
    return jnp.sum(jnp.where(rows == x.shape[0] - 1, x, 0.0), axis=0, keepdims=True)


def _shift_down(x, d, rows, fill=0.0):
    return jnp.where(rows >= d, pltpu.roll(x, d, 0), fill)


def _shift_up(x, d, rows, fill=0.0):
    n = x.shape[0]
    return jnp.where(rows < n - d, pltpu.roll(x, n - d, 0), fill)


def _conv_fwd(x, cw_ref, cb_ref, rows):
    out = cb_ref[...] + cw_ref[pl.ds(3, 1), :] * x
    for k in range(3):
        out = out + cw_ref[pl.ds(k, 1), :] * _shift_down(x, 3 - k, rows)
    return out


def _conv_bwd(x, dco, cw_ref, rows):
    dx = cw_ref[pl.ds(3, 1), :] * dco
    dws = []
    for k in range(3):
        dx = dx + cw_ref[pl.ds(k, 1), :] * _shift_up(dco, 3 - k, rows)
        dws.append(jnp.sum(dco * _shift_down(x, 3 - k, rows), axis=0, keepdims=True))
    dws.append(jnp.sum(dco * x, axis=0, keepdims=True))
    return dx, dws, jnp.sum(dco, axis=0, keepdims=True)


def _vec(n):
    return pl.BlockSpec((1, n), lambda *_: (0, 0))


class _Row:
    def __init__(self, arr, l, n=None, c=0):
        self.arr, self.l, self.n, self.c = arr[:, None, :], l, n or arr.shape[1], c


def _spec(v):
    if isinstance(v, _Row):
        return pl.BlockSpec((None, 1, v.n), lambda *_: (v.l, 0, v.c))
    return _vec(v.shape[1])


def _arr(v):
    return v.arr if isinstance(v, _Row) else v


def _full(shape):
    nd = len(shape)
    return pl.BlockSpec(shape, lambda *_: (0,) * nd)


def _inproj_fwd(x, nw, scale, shift, w, tok):
    S = x.shape[0]
    tm = min(256, S)

    def body(x_ref, nw_ref, sc_ref, sh_ref, w_ref, tok_ref, u_ref, h_ref):
        del tok_ref
        xv = x_ref[...]
        inv = lax.rsqrt(jnp.mean(xv * xv, axis=-1, keepdims=True) + EPS)
        h = ((xv * inv) * nw_ref[...] * (1.0 + sc_ref[...]) + sh_ref[...]).astype(BF16)
        h_ref[...] = h
        u_ref[...] = _mm(h, w_ref[...])

    return pl.pallas_call(
        body, name="inproj_fwd", grid=(S // tm,),
        in_specs=[pl.BlockSpec((tm, D_MODEL), lambda i: (i, 0)), _spec(nw), _spec(scale), _spec(shift),
                  _full((D_MODEL, N_PAD)), pl.BlockSpec(memory_space=pl.ANY)],
        out_specs=[pl.BlockSpec((tm, N_PAD), lambda i: (i, 0)), pl.BlockSpec((tm, D_MODEL), lambda i: (i, 0))],
        out_shape=[SDS((S, N_PAD), F32), SDS((S, D_MODEL), BF16)],
        compiler_params=_cp(("parallel",)),
    )(x, _arr(nw), _arr(scale), _arr(shift), w, tok)


def _inproj_bwd_x(du, w, x, nw, scale, dxn, tok):
    S = x.shape[0]
    tm = min(256, S)

    def body(du_ref, w_ref, x_ref, nw_ref, sc_ref, dxn_ref, tok_ref, dx_ref, red_ref):
        del tok_ref

        @pl.when(pl.program_id(0) == 0)
        def _():
            red_ref[...] = jnp.zeros_like(red_ref)

        dh = _mm_nt(du_ref[...], w_ref[...])
        xv = x_ref[...]
        inv = lax.rsqrt(jnp.mean(xv * xv, axis=-1, keepdims=True) + EPS)
        xhat = xv * inv
        nwv = nw_ref[...]
        g1 = 1.0 + sc_ref[...]
        dxhat = dh * nwv * g1
        dx = inv * (dxhat - xhat * jnp.mean(dxhat * xhat, axis=-1, keepdims=True))
        dx_ref[...] = dxn_ref[...] + dx
        red_ref[0:1, :] += jnp.sum(dh, axis=0, keepdims=True)
        red_ref[1:2, :] += jnp.sum(dh * xhat * nwv, axis=0, keepdims=True)
        red_ref[2:3, :] += jnp.sum(dh * xhat * g1, axis=0, keepdims=True)

    row = pl.BlockSpec((tm, D_MODEL), lambda i: (i, 0))
    return pl.pallas_call(
        body, name="inproj_bwd_x", grid=(S // tm,),
        in_specs=[pl.BlockSpec((tm, N_PAD), lambda i: (i, 0)), _full((D_MODEL, N_PAD)), row, _spec(nw),
                  _spec(scale), row, pl.BlockSpec(memory_space=pl.ANY)],
        out_specs=[row, _full((8, D_MODEL))],
        out_shape=[SDS((S, D_MODEL), F32), SDS((8, D_MODEL), F32)],
        compiler_params=_cp(("arbitrary",)),
    )(du, w, x, _arr(nw), _arr(scale), dxn, tok)


def _inproj_bwd_w(h, du, tok):
    S = h.shape[0]
    tn = 640

    def body(h_ref, du_ref, tok_ref, gw_ref):
        del tok_ref
        gw_ref[...] = _mm_tn(h_ref[...], _bf(du_ref[...]))

    return pl.pallas_call(
        body, name="inproj_bwd_w", grid=(N_PAD // tn,),
        in_specs=[_full((S, D_MODEL)), pl.BlockSpec((S, tn), lambda j: (0, j)), pl.BlockSpec(memory_space=pl.ANY)],
        out_specs=pl.BlockSpec((D_MODEL, tn), lambda j: (0, j)),
        out_shape=SDS((D_MODEL, N_PAD), F32),
        compiler_params=_cp(("parallel",)),
    )(h, du, tok)


def _scan_block(a, b, rows):
    d = 1
    while d < a.shape[0]:
        a_s = _shift_down(a, d, rows, 1.0)
        b_s = _shift_down(b, d, rows, 0.0)
        b = a * b_s + b
        a = a * a_s
        d *= 2
    return a, b


def _rscan_block(c, g, rows):
    d = 1
    while d < c.shape[0]:
        c_s = _shift_up(c, d, rows, 1.0)
        g_s = _shift_up(g, d, rows, 0.0)
        g = g + c * g_s
        c = c * c_s
        d *= 2
    return c, g


LRU_BLOCK = 128


def _lru_gates(xa, wa_ref, ba_ref, wx_ref, bx_ref, lam_ref):
    sp = _softplus(-lam_ref[...])
    xb = _bf(xa)
    r = _sigmoid(_mm(xb, wa_ref[...]) + ba_ref[...])
    ig = _sigmoid(_mm(xb, wx_ref[...]) + bx_ref[...])
    la = -LRU_C * r * sp
    a = jnp.exp(la)
    mult = jnp.sqrt(-_expm1(2.0 * la))
    return sp, r, ig, la, a, mult


def _lru_specs(S, l):
    t128 = pl.BlockSpec((None, 1, LANE), lambda t: (l, 0, t))
    gate = pl.BlockSpec((None, None, LANE, LANE), lambda t: (l, t, 0, 0))
    return [pl.BlockSpec((S, 2 * LANE), lambda t: (0, OFF_LRU // (2 * LANE) + t)),
            pl.BlockSpec((None, 4, LANE), lambda t: (l, 0, t)), t128, gate, t128, gate, t128, t128]


def _lru_fwd(l, u, cw, cb, wa, ba, wx, bx, lam, ycat):
    S = u.shape[0]
    tb = min(LRU_BLOCK, S)

    def body(u_ref, cw_ref, cb_ref, wa_ref, ba_ref, wx_ref, bx_ref, lam_ref, ycat_in, ycat_ref, h_ref, a_scr, b_scr):
        del ycat_in
        rows = _iota((S, LANE), 0)
        xa = _conv_fwd(_f(u_ref[:, 0:LANE]), cw_ref, cb_ref, rows)
        _, _, ig, _, a, mult = _lru_gates(xa, wa_ref, ba_ref, wx_ref, bx_ref, lam_ref)
        a_scr[...] = a
        b_scr[...] = mult * (ig * xa)
        rows_b = _iota((tb, LANE), 0)

        def blk(j, hprev):
            sl = pl.ds(pl.multiple_of(j * tb, tb), tb)
            acum, hloc = _scan_block(a_scr[sl, :], b_scr[sl, :], rows_b)
            hf = hloc + acum * hprev
            h_ref[sl, :] = hf
            return _last_row(hf, rows_b)

        lax.fori_loop(0, S // tb, blk, jnp.zeros((1, LANE), F32))
        ycat_ref[...] = _bf(h_ref[...] * _silu(_f(u_ref[:, LANE:2 * LANE])))

    col = pl.BlockSpec((S, LANE), lambda t: (0, t))
    return pl.pallas_call(
        body, name="lru_fwd", grid=(LRU_W // LANE,),
        in_specs=_lru_specs(S, l) + [pl.BlockSpec(memory_space=pl.ANY)],
        out_specs=[col, col],
        out_shape=[SDS((S, D_INNER), BF16), SDS((S,LRU_W), F32)],
        scratch_shapes=[pltpu.VMEM((S, LANE), F32), pltpu.VMEM((S, LANE), F32)],
        input_output_aliases={8: 0},
        compiler_params=_cp(("parallel",)),
    )(u, cw, cb, wa, ba, wx, bx, lam, ycat)


def _lru_bwd(l, u, cw, cb, wa, ba, wx, bx, lam, h_lru, dycat, du):
    S = u.shape[0]
    tb = min(LRU_BLOCK, S)

    def body(u_ref, cw_ref, cb_ref, wa_ref, ba_ref, wx_ref, bx_ref, lam_ref, h_ref, dy_ref, du_in,
             du_ref, red_ref, gwa_ref, gwx_ref, c_scr, g_scr, l_scr):
        del du_in
        rows = _iota((S, LANE), 0)
        ax = _f(u_ref[:, 0:LANE])
        ag = _f(u_ref[:, LANE:2 * LANE])
        xa = _conv_fwd(ax, cw_ref, cb_ref, rows)
        sp, r, ig, la, a, mult = _lru_gates(xa, wa_ref, ba_ref, wx_ref, bx_ref, lam_ref)
        h = h_ref[...]
        dy = _f(dy_ref[...])
        du_ref[:, LANE:2 * LANE] = _bf(dy * h * _dsilu(ag))
        c_scr[...] = _shift_up(a, 1, rows, 0.0)
        g_scr[...] = dy * _silu(ag)
        rows_b = _iota((tb, LANE), 0)
        nb = S // tb

        def blk(jj, lnext):
            j = nb - 1 - jj
            sl = pl.ds(pl.multiple_of(j * tb, tb), tb)
            ccum, lloc = _rscan_block(c_scr[sl, :], g_scr[sl, :], rows_b)
            lam_t = lloc + ccum * lnext
            l_scr[sl, :] = lam_t
            return jnp.sum(jnp.where(rows_b == 0, lam_t, 0.0), axis=0, keepdims=True)

        lax.fori_loop(0, nb, blk, jnp.zeros((1, LANE), F32))
        db = l_scr[...]
        da = db * _shift_down(h, 1, rows)
        dmult = db * ig * xa
        dig = db * mult * xa
        dxa = db * mult * ig
        dla = da * a - dmult * (a * a) / mult
        dr = -LRU_C * sp * dla
        dsp = jnp.sum(-LRU_C * r * dla, axis=0, keepdims=True)
        dlam = -dsp * _sigmoid(-lam_ref[...])
        dzr = dr * r * (1.0 - r)
        dzi = dig * ig * (1.0 - ig)
        dzr_b, dzi_b, xa_b = _bf(dzr), _bf(dzi), _bf(xa)
        dxa = dxa + _mm_nt(dzr_b, wa_ref[...]) + _mm_nt(dzi_b, wx_ref[...])
        gwa_ref[...] = _mm_tn(xa_b, dzr_b)
        gwx_ref[...] = _mm_tn(xa_b, dzi_b)
        dax, dws, dcb = _conv_bwd(ax, dxa, cw_ref, rows)
        du_ref[:, 0:LANE] = _bf(dax)
        parts = dws + [dcb, jnp.sum(dzr, axis=0, keepdims=True), jnp.sum(dzi, axis=0, keepdims=True), dlam]
        for n, p in enumerate(parts):
            red_ref[pl.ds(n, 1), :] = p

    col = pl.BlockSpec((S, LANE), lambda t: (0, t))
    gw = pl.BlockSpec((None, LANE, LANE), lambda t: (t, 0, 0))
    return pl.pallas_call(
        body, name="lru_bwd", grid=(LRU_W // LANE,),
        in_specs=_lru_specs(S, l) + [col, col, pl.BlockSpec(memory_space=pl.ANY)],
        out_specs=[pl.BlockSpec((S, 2 * LANE), lambda t: (0, OFF_LRU // (2 * LANE) + t)),
                   pl.BlockSpec((8, LANE), lambda t: (0, t)), gw, gw],
        out_shape=[SDS((S, N_PAD), BF16), SDS((8, LRU_W), F32), SDS((4, LANE, LANE), F32), SDS((4, LANE, LANE), F32)],
        scratch_shapes=[pltpu.VMEM((S, LANE), F32)] * 3,
        input_output_aliases={10: 0},
        compiler_params=_cp(("parallel",)),
    )(u, cw, cb, wa, ba, wx, bx, lam, h_lru, dycat, du)


HG_LEVELS = 6


def _hg_consts():
    C = HG_CHUNK
    t = np.arange(C)[:, None]
    r = np.arange(C)[None, :]
    mats = []
    for l in range(HG_LEVELS):
        b = 1 << l
        upper = (t % (2 * b)) >= b
        anchor = (t // (2 * b)) * 2 * b + b - 1
        mats.append((upper & (r > anchor) & (r <= t)) | ((~upper) & (r > t) & (r <= anchor)))
    mats.append(r <= t)
    mats.append(r > t)
    return np.concatenate(mats, 0).astype(np.float32)


def _hg_factors(hf, lb, mall):
    s = _sigmoid(hf)
    f = lb + (1.0 - lb) * s
    lf = jnp.log(f)
    k = (1.0 - lb) * _sigmoid(-hf)
    e = jnp.exp(_sel_l(mall, lf))
    C = HG_CHUNK
    rows = _iota((C, HG_W), 0)
    eq, ek = [], []
    for l in range(HG_LEVELS):
        el = e[l * C:(l + 1) * C]
        eq.append(jnp.where((lax.shift_right_logical(rows, l) & 1) == 1, el, 0.0))
        ek.append(el - eq[l])
    ecum = e[HG_LEVELS * C:(HG_LEVELS + 1) * C]
    erem = e[(HG_LEVELS + 1) * C:(HG_LEVELS + 2) * C]
    return s, f, k, eq, ek, ecum, erem


def _hg_masks():
    C = HG_CHUNK
    ri, ci = _iota((C, C), 0), _iota((C, C), 1)
    rr = _iota((C, LANE), 0)
    gm = [(lax.shift_right_logical(ri, l + 1) == lax.shift_right_logical(ci, l + 1)).astype(F32)
          for l in range(HG_LEVELS)]
    up = [(lax.shift_right_logical(rr, l) & 1) == 1 for l in range(HG_LEVELS)]
    eye = (ri == ci).astype(F32)
    return gm, up, eye, rr


def _hg_scores(qh, kh, eq, ek, sl, gm, up, eye):
    del up
    qs, ks, qb, kb = [], [], [], []
    p = _mm_nt(_bf(qh), _bf(kh)) * eye
    for l in range(HG_LEVELS):
        qs.append(qh * eq[l][:, sl])
        ks.append(kh * ek[l][:, sl])
        qb.append(_bf(qs[l]))
        kb.append(_bf(ks[l]))
        p = p + _mm_nt(qb[l], kb[l]) * gm[l]
    return p, qs, ks, qb, kb


HG_SUB = 4


def _hg_fwd(u, lb, nw, mall, ycat):
    S = u.shape[0]
    C = HG_CHUNK
    n = S // C
    rows = HG_SUB * C

    def body(u_ref, lb_ref, nw_ref, mall_ref, ycat_in, ycat_ref, o_ref, st_ref, st):
        del ycat_in

        @pl.when(pl.program_id(0) == 0)
        def _():
            st[...] = jnp.zeros_like(st)

        gm, up, eye, rr = _hg_masks()
        for sub in range(HG_SUB):
            r = slice(sub * C, (sub + 1) * C)
            q = _silu(_f(u_ref[r, 0:512]))
            v = u_ref[r, 1024:1536]
            _, _, k, eq, ek, ecum, erem = _hg_factors(_f(u_ref[r, 512:1024]), lb_ref[...], mall_ref[...])
            for h in range(HG_HEADS):
                sl = slice(h * LANE, (h + 1) * LANE)
                qh, kh, vh = q[:, sl], k[:, sl], _bf(v[:, sl])
                p = _hg_scores(qh, kh, eq, ek, sl, gm, up, eye)[0]
                sth = st[h]
                st_ref[sub, h] = sth
                o_ref[r, sl] = _mm(_bf(p), vh) + _mm_nt(_bf(qh * ecum[:, sl]), _bf(sth))
                st[h] = sth * _last_row(ecum[:, sl], rr) + _mm_tn(vh, _bf(kh * erem[:, sl]))
            o = o_ref[r, :]
            inv = lax.rsqrt(jnp.mean(o * o, axis=-1, keepdims=True) + EPS)
            ycat_ref[r, :] = _bf((o * inv) * nw_ref[...] * _silu(_f(u_ref[r, 1536:2048])))

    return pl.pallas_call(
        body, name="hg_fwd", grid=(n // HG_SUB,),
        in_specs=[pl.BlockSpec((rows, 2048), lambda i: (i, 0)), _spec(lb), _spec(nw), _full(mall.shape),
                  pl.BlockSpec(memory_space=pl.ANY)],
        out_specs=[pl.BlockSpec((rows, HG_W), lambda i: (i, 1)), pl.BlockSpec((rows, HG_W), lambda i: (i, 0)),
                   pl.BlockSpec((HG_SUB, HG_HEADS, LANE, LANE), lambda i: (i, 0, 0, 0))],
        out_shape=[SDS((S, D_INNER), BF16), SDS((S,HG_W), F32), SDS((n, HG_HEADS, LANE, LANE), F32)],
        scratch_shapes=[pltpu.VMEM((HG_HEADS, LANE, LANE), F32)],
        input_output_aliases={4: 0},
        compiler_params=_cp(("arbitrary",)),
    )(u, _arr(lb), _arr(nw), mall, ycat)


def _hg_bwd(u, lb, nw, mall, mall_t, o_b, states, dycat, du):
    S = u.shape[0]
    C = HG_CHUNK
    n = S // C
    nb = n // HG_SUB
    rows = HG_SUB * C
    L2 = HG_LEVELS

    def body(u_ref, lb_ref, nw_ref, mall_ref, mallt_ref, o_ref, st_ref, dy_ref, du_in, du_ref, red_ref,
             dst, dlast_s, dq_s, dk_s, dex):
        del du_in

        @pl.when(pl.program_id(0) == 0)
        def _():
            dst[...] = jnp.zeros_like(dst)
            red_ref[...] = jnp.zeros_like(red_ref)

        lb = lb_ref[...]
        nwv = nw_ref[...]
        gm, up, eye, rr = _hg_masks()
        for sub in reversed(range(HG_SUB)):
            r = slice(sub * C, (sub + 1) * C)
            hq, hf, hg = _f(u_ref[r, 0:512]), _f(u_ref[r, 512:1024]), _f(u_ref[r, 1536:2048])
            q = _silu(hq)
            v = u_ref[r, 1024:1536]
            s, f, k, eq, ek, ecum, erem = _hg_factors(hf, lb, mall_ref[...])
            o = o_ref[r, :]
            dy = _f(dy_ref[r, :])
            inv = lax.rsqrt(jnp.mean(o * o, axis=-1, keepdims=True) + EPS)
            ohat = o * inv
            du_ref[r, 1536:2048] = _bf(dy * ohat * nwv * _dsilu(hg))
            dn = dy * _silu(hg)
            red_ref[0:1, :] += jnp.sum(dn * ohat, axis=0, keepdims=True)
            dohat = dn * nwv
            do = inv * (dohat - ohat * jnp.mean(dohat * ohat, axis=-1, keepdims=True))
            for h in range(HG_HEADS):
                sl = slice(h * LANE, (h + 1) * LANE)
                qh, kh, vh, doh = q[:, sl], k[:, sl], _bf(v[:, sl]), _bf(do[:, sl])
                p, qs, ks, qb, kb = _hg_scores(qh, kh, eq, ek, sl, gm, up, eye)
                st_f = st_ref[sub, h]
                sth = _bf(st_f)
                dsth = dst[h]
                dsth_b = _bf(dsth)
                qt = qh * ecum[:, sl]
                kt = kh * erem[:, sl]
                elast = _last_row(ecum[:, sl], rr)
                dp = _mm_nt(doh, vh)
                du_ref[r, 1024 + h * LANE:1024 + (h + 1) * LANE] = _bf(_mm_tn(_bf(p), doh) + _mm_nt(_bf(kt), dsth_b))
                dpe = _bf(dp * eye)
                dqt = _mm(doh, sth)
                dkt = _mm(vh, dsth_b)
                dq = dqt * ecum[:, sl] + _mm(dpe, _bf(kh))
                dk = dkt * erem[:, sl] + _mm_tn(dpe, _bf(qh))
                dex[sub, L2 * C:(L2 + 1) * C, sl] = dqt * qt
                dex[sub, (L2 + 1) * C:(L2 + 2) * C, sl] = dkt * kt
                for l in range(HG_LEVELS):
                    dpl = _bf(dp * gm[l])
                    dql = _mm(dpl, kb[l])
                    dkl = _mm_tn(dpl, qb[l])
                    dq = dq + dql * eq[l][:, sl]
                    dk = dk + dkl * ek[l][:, sl]
                    dex[sub, l * C:(l + 1) * C, sl] = dql * qs[l] + dkl * ks[l]
                dlast_s[sub, :, sl] = jnp.sum(dsth * st_f, axis=0, keepdims=True) * elast
                dst[h] = dsth * elast + _mm_tn(doh, _bf(qt))
                dq_s[sub, :, sl] = dq
                dk_s[sub, :, sl] = dk
            dq = dq_s[sub]
            dk = dk_s[sub]
            dlf = _sel_l2(mallt_ref[...], dex[sub]) + dlast_s[sub]
            du_ref[r, 0:512] = _bf(dq * _dsilu(hq))
            t = (1.0 - s) * (dlf / f - dk)
            du_ref[r, 512:1024] = _bf((1.0 - lb) * s * t)
            red_ref[1:2, :] += jnp.sum(t, axis=0, keepdims=True)

    rev = lambda i: (nb - 1 - i, 0)
    return pl.pallas_call(
        body, name="hg_bwd", grid=(nb,),
        in_specs=[pl.BlockSpec((rows, 2048), rev), _spec(lb), _spec(nw), _full(mall.shape), _full(mall_t.shape),
                  pl.BlockSpec((rows, HG_W), rev),
                  pl.BlockSpec((HG_SUB, HG_HEADS, LANE, LANE), lambda i: (nb - 1 - i, 0, 0, 0)),
                  pl.BlockSpec((rows, HG_W), lambda i: (nb - 1 - i, 1)), pl.BlockSpec(memory_space=pl.ANY)],
        out_specs=[pl.BlockSpec((rows, 2048), rev), pl.BlockSpec((8, HG_W), lambda i: (0, 0))],
        out_shape=[SDS((S, N_PAD), BF16), SDS((8, HG_W), F32)],
        scratch_shapes=[pltpu.VMEM((HG_HEADS, LANE, LANE), F32), pltpu.VMEM((HG_SUB, 1, HG_W), F32),
                        pltpu.VMEM((HG_SUB, C, HG_W), F32), pltpu.VMEM((HG_SUB, C, HG_W), F32),
                        pltpu.VMEM((HG_SUB, (L2 + 2) * C, HG_W), F32)],
        input_output_aliases={8: 0},
        compiler_params=_cp(("arbitrary",)),
    )(u, _arr(lb), _arr(nw), mall, mall_t, o_b, states, dycat, du)


def _ssdconv_fwd(l, u, cw, cb):
    S = u.shape[0]

    def body(u_ref, cw_ref, cb_ref, out_ref):
        rows = _iota((S, LANE), 0)
        out_ref[...] = _silu(_conv_fwd(_f(u_ref[...]), cw_ref, cb_ref, rows))

    return pl.pallas_call(
        body, name="ssdconv_fwd", grid=(SSD_CONV // LANE,),
        in_specs=[pl.BlockSpec((S, LANE), lambda t: (0, OFF_XBC // LANE + t)),
                  pl.BlockSpec((None, 4, LANE), lambda t: (l, 0, t)), pl.BlockSpec((None, 1, LANE), lambda t: (l, 0, t))],
        out_specs=pl.BlockSpec((S, LANE), lambda t: (0, t)),
        out_shape=SDS((S, SSD_CONV), F32),
        compiler_params=_cp(("parallel",)),
    )(u, cw, cb)


def _ssdconv_bwd(l, u, cw, cb, dxbc, du):
    S = u.shape[0]

    def body(u_ref, cw_ref, cb_ref, d_ref, du_in, du_ref, red_ref):
        del du_in
        rows = _iota((S, LANE), 0)
        x = _f(u_ref[...])
        dco = d_ref[...] * _dsilu(_conv_fwd(x, cw_ref, cb_ref, rows))
        dx, dws, dcb = _conv_bwd(x, dco, cw_ref, rows)
        du_ref[...] = _bf(dx)
        for n, p in enumerate(dws + [dcb]):
            red_ref[pl.ds(n, 1), :] = p
        red_ref[pl.ds(5, 3), :] = jnp.zeros((3, LANE), F32)

    ucol = pl.BlockSpec((S, LANE), lambda t: (0, OFF_XBC // LANE + t))
    return pl.pallas_call(
        body, name="ssdconv_bwd", grid=(SSD_CONV // LANE,),
        in_specs=[ucol, pl.BlockSpec((None, 4, LANE), lambda t: (l, 0, t)),
                  pl.BlockSpec((None, 1, LANE), lambda t: (l, 0, t)),
                  pl.BlockSpec((S, LANE), lambda t: (0, t)), pl.BlockSpec(memory_space=pl.ANY)],
        out_specs=[ucol, pl.BlockSpec((8, LANE), lambda t: (0, t))],
        out_shape=[SDS((S, N_PAD), BF16), SDS((8, SSD_CONV), F32)],
        input_output_aliases={4: 0},
        compiler_params=_cp(("parallel",)),
    )(u, cw, cb, dxbc, du)


SSD_SUB = 2


def _ssd_consts():
    e64 = np.zeros((LANE, SSD_W), np.float32)
    for h in range(SSD_HEADS):
        e64[h, h * SSD_P:(h + 1) * SSD_P] = 1.0
    T = SSD_CHUNK
    tril = (np.arange(T)[None, :] <= np.arange(T)[:, None]).astype(np.float32)
    return e64, tril, tril.T.copy()


def _ssd_common(zdt, bias_ref, alog_ref, tril, e64, cum_ref, cumt_ref):
    T = SSD_CHUNK
    lane = _iota((1, LANE), 1)
    a_neg = jnp.where(lane < SSD_HEADS, -jnp.exp(alog_ref[...]), 0.0)
    dtpre = zdt[:, SSD_W:SSD_W + LANE] + bias_ref[...]
    dt = _softplus(dtpre)
    cum = _sel_l(tril, dt * a_neg)
    cum_ref[...] = cum
    cumt_ref[...] = cum.T
    cum_x = _sel_r(cum, e64)
    last_x = _last_row(cum_x, _iota((T, SSD_W), 0))
    ecum_x = jnp.exp(cum_x)
    erem_x = jnp.exp(last_x - cum_x)
    elast_x = jnp.exp(last_x)
    dt_x = _sel_r(dt, e64)
    return a_neg, dtpre, dt, ecum_x, erem_x, elast_x, dt_x


def _ssd_decay(cum_ref, cumt_ref, h, causal):
    T = SSD_CHUNK
    diff = jnp.broadcast_to(cum_ref[:, pl.ds(h, 1)], (T, T)) - cumt_ref[pl.ds(h, 1), :]
    return jnp.exp(jnp.where(causal, diff, NEG))


def _group_norm_fwd(y1, nwv):
    outs, invs = [], []
    for g in range(2):
        seg = y1[:, g * 512:(g + 1) * 512]
        inv = lax.rsqrt(jnp.mean(seg * seg, axis=-1, keepdims=True) + EPS)
        outs.append(seg * inv * nwv[:, g * 512:(g + 1) * 512])
        invs.append(inv)
    return outs, invs


def _ssd_fwd(u, xbc, bias, alog, dskip_x, nw, consts, ycat):
    S = u.shape[0]
    T = SSD_CHUNK
    n = S // T
    rows = SSD_SUB * T
    e64, tril, _ = consts

    def body(u_ref, xbc_ref, bias_ref, alog_ref, dx_ref, nw_ref, e64_ref, tril_ref, ycat_in,
             ycat_ref, y_ref, st_ref, st, cumt, cum_e):
        del ycat_in

        @pl.when(pl.program_id(0) == 0)
        def _():
            st[...] = jnp.zeros_like(st)

        causal = _iota((T, T), 0) >= _iota((T, T), 1)
        lo = _iota((T, LANE), 1) < SSD_P
        for sub in range(SSD_SUB):
            r = slice(sub * T, (sub + 1) * T)
            zdt = _f(u_ref[r, :])
            z = zdt[:, 0:SSD_W]
            xs = xbc_ref[r, 0:SSD_W]
            cum_r, cumt_r = cum_e.at[sub], cumt.at[sub]
            _, _, _, ecum_x, erem_x, elast_x, dt_x = _ssd_common(
                zdt, bias_ref, alog_ref, tril_ref[...], e64_ref[...], cum_r, cumt_r)
            xdt = xs * dt_x
            xrem = xdt * erem_x
            st_ref[sub] = st[...]
            for g in range(2):
                gs = slice(g * 512, (g + 1) * 512)
                bg = _bf(xbc_ref[r, SSD_W + g * LANE:SSD_W + (g + 1) * LANE])
                cg = _bf(xbc_ref[r, SSD_W + 256 + g * LANE:SSD_W + 256 + (g + 1) * LANE])
                cb = _mm_nt(cg, bg)
                yin = _mm(cg, _bf(st[:, gs])) * ecum_x[:, gs]
                for j in range(4):
                    h0 = 8 * g + 2 * j
                    cs = slice(h0 * SSD_P, (h0 + 2) * SSD_P)
                    xp = xdt[:, cs]
                    s0 = _bf(cb * _ssd_decay(cum_r, cumt_r, h0, causal))
                    s1 = _bf(cb * _ssd_decay(cum_r, cumt_r, h0 + 1, causal))
                    y_ref[r, cs] = (_mm(s0, _bf(jnp.where(lo, xp, 0.0))) + _mm(s1, _bf(jnp.where(lo, 0.0, xp)))
                                    + yin[:, j * LANE:(j + 1) * LANE])
                st[:, gs] = st[:, gs] * elast_x[:, gs] + _mm_tn(bg, _bf(xrem[:, gs]))
            y1 = (y_ref[r, :] + dx_ref[...] * xs) * _silu(z)
            outs, _ = _group_norm_fwd(y1, nw_ref[...])
            for g in range(2):
                ycat_ref[r, g * 512:(g + 1) * 512] = _bf(outs[g])

    return pl.pallas_call(
        body, name="ssd_fwd", grid=(n // SSD_SUB,),
        in_specs=[pl.BlockSpec((rows, SSD_W + LANE), lambda i: (i, OFF_Z // (SSD_W + LANE))),
                  pl.BlockSpec((rows, SSD_CONV), lambda i: (i, 0)), _spec(bias), _spec(alog), _spec(dskip_x), _spec(nw),
                  _full(e64.shape), _full(tril.shape), pl.BlockSpec(memory_space=pl.ANY)],
        out_specs=[pl.BlockSpec((rows, SSD_W), lambda i: (i, 1)), pl.BlockSpec((rows, SSD_W), lambda i: (i, 0)),
                   pl.BlockSpec((SSD_SUB, SSD_N, SSD_W), lambda i: (i, 0, 0))],
        out_shape=[SDS((S, D_INNER), BF16), SDS((S,SSD_W), F32), SDS((n, SSD_N, SSD_W), F32)],
        scratch_shapes=[pltpu.VMEM((SSD_N, SSD_W), F32), pltpu.VMEM((SSD_SUB, LANE, T), F32),
                        pltpu.VMEM((SSD_SUB, T, LANE), F32)],
        input_output_aliases={8: 0},
        compiler_params=_cp(("arbitrary",)),
    )(u, xbc, _arr(bias), _arr(alog), _arr(dskip_x), _arr(nw), _bfc(e64), _bfc(tril), ycat)


def _ssd_bwd(u, xbc, bias, alog, dskip_x, nw, consts, y_ssd, states, dycat, du, tok):
    S = u.shape[0]
    T = SSD_CHUNK
    n = S // T
    e64, tril, triu = consts
    e64t = np.ascontiguousarray(e64.T)

    def chunk(u_ref, xbc_ref, bias_ref, alog_ref, dx_ref, nw_ref, e64_ref, e64t_ref, tril_ref, triu_ref,
              y_ref, st_ref, dy_ref, du_ref, dxbc_ref, red_ref, dst, dl_s, cumt, dxdt_s, dy0_s, gb_s, gc_s, cum_e, cs_s):
        zdt = _f(u_ref[...])
        z = zdt[:, 0:SSD_W]
        xs = xbc_ref[:, 0:SSD_W]
        a_neg, dtpre, dt, ecum_x, erem_x, elast_x, dt_x = _ssd_common(
            zdt, bias_ref, alog_ref, tril_ref[...], e64_ref[...], cum_e, cumt)
        causal = _iota((T, T), 0) >= _iota((T, T), 1)
        lo = _iota((T, LANE), 1) < SSD_P
        xdt = xs * dt_x
        xrem = xdt * erem_x
        y = y_ref[...]
        dxv = dx_ref[...]
        nwv = nw_ref[...]
        sz = _silu(z)
        y0 = y + dxv * xs
        y1 = y0 * sz
        for g in range(2):
            gs = slice(g * 512, (g + 1) * 512)
            seg = y1[:, gs]
            inv = lax.rsqrt(jnp.mean(seg * seg, axis=-1, keepdims=True) + EPS)
            shat = seg * inv
            dyg = _f(dy_ref[:, gs])
            red_ref[0:1, gs] += jnp.sum(dyg * shat, axis=0, keepdims=True)
            dsh = dyg * nwv[:, gs]
            dy1g = inv * (dsh - shat * jnp.mean(dsh * shat, axis=-1, keepdims=True))
            du_ref[:, gs] = _bf(dy1g * y0[:, gs] * _dsilu(z[:, gs]))
            dy0_s[:, gs] = dy1g * sz[:, gs]
        dy0 = dy0_s[...]
        red_ref[1:2, :] += jnp.sum(dy0 * xs, axis=0, keepdims=True)
        dyin = dy0 * ecum_x
        lane = _iota((T, LANE), 1)
        dcum = jnp.zeros((T, LANE), F32)

        def decay_grad(h, gm):
            cs_s[pl.ds(h, 1), :] = jnp.sum(gm, axis=0, keepdims=True)
            return jnp.where(lane == h, jnp.sum(gm, axis=1, keepdims=True), 0.0)

        for g in range(2):
            gs = slice(g * 512, (g + 1) * 512)
            bg = _bf(xbc_ref[:, SSD_W + g * LANE:SSD_W + (g + 1) * LANE])
            cg = _bf(xbc_ref[:, SSD_W + 256 + g * LANE:SSD_W + 256 + (g + 1) * LANE])
            cb = _mm_nt(cg, bg)
            dst_f, st_f = dst[:, gs], st_ref[:, gs]
            dstg = _bf(dst_f)
            stg = _bf(st_f)
            dyin_g = _bf(dyin[:, gs])
            xrem_g = _bf(xrem[:, gs])
            dcb = jnp.zeros((T, T), F32)
            dxr = _mm(bg, dstg)
            dxdt_s[:, gs] = dxr * erem_x[:, gs]
            gc_s[:, gs] = dxr * xrem[:, gs]
            gb_s[:, gs] = dyin[:, gs] * _mm(cg, stg)
            dl_s[:, gs] = jnp.sum(dst_f * st_f, axis=0, keepdims=True) * elast_x[:, gs]
            for j in range(4):
                h0 = 8 * g + 2 * j
                cs = slice(h0 * SSD_P, (h0 + 2) * SSD_P)
                xp = xdt[:, cs]
                dyp = dy0[:, cs]
                x_lo, x_hi = _bf(jnp.where(lo, xp, 0.0)), _bf(jnp.where(lo, 0.0, xp))
                d_lo, d_hi = _bf(jnp.where(lo, dyp, 0.0)), _bf(jnp.where(lo, 0.0, dyp))
                l0 = _ssd_decay(cum_e, cumt, h0, causal)
                l1 = _ssd_decay(cum_e, cumt, h0 + 1, causal)
                s0 = cb * l0
                s1 = cb * l1
                ds0 = _mm_nt(d_lo, x_lo)
                ds1 = _mm_nt(d_hi, x_hi)
                dcb = dcb + ds0 * l0 + ds1 * l1
                dxdt_s[:, cs] += _mm_tn(_bf(s0), d_lo) + _mm_tn(_bf(s1), d_hi)
                dcum = dcum + decay_grad(h0, ds0 * s0) + decay_grad(h0 + 1, ds1 * s1)
            dcb_b = _bf(dcb)
            dxbc_ref[:, SSD_W + g * LANE:SSD_W + (g + 1) * LANE] = _mm_tn(dcb_b, cg) + _mm_nt(xrem_g, dstg)
            dxbc_ref[:, SSD_W + 256 + g * LANE:SSD_W + 256 + (g + 1) * LANE] = _mm(dcb_b, bg) + _mm_nt(dyin_g, stg)
            dst[:, gs] = dst_f * elast_x[:, gs] + _mm_tn(cg, dyin_g)
        dxdt = dxdt_s[...]
        dxbc_ref[:, 0:SSD_W] = dxdt * dt_x + dy0 * dxv
        e64t = e64t_ref[...]
        gc = gc_s[...]
        dlast_x = jnp.sum(gc, axis=0, keepdims=True) + dl_s[...]
        dlast = jnp.max(_sel_r(jnp.broadcast_to(dlast_x, (8, SSD_W)), e64t), axis=0, keepdims=True)
        dcum = (dcum - cs_s[...].T + _sel_r(gb_s[...] - gc, e64t)
                + jnp.where(_iota((T, LANE), 0) == T - 1, dlast, 0.0))
        dda = _sel_l(triu_ref[...], dcum)
        ddt = dda * a_neg + _sel_r(dxdt * xs, e64t)
        ddtpre = ddt * _sigmoid(dtpre)
        du_ref[:, SSD_W:SSD_W + LANE] = _bf(jnp.where(lane < SSD_HEADS, ddtpre, 0.0))
        red_ref[2:3, 0:LANE] += jnp.sum(ddtpre, axis=0, keepdims=True)
        red_ref[3:4, 0:LANE] += jnp.sum(dda * dt, axis=0, keepdims=True)

    def body(u_ref, xbc_ref, bias_ref, alog_ref, dx_ref, nw_ref, e64_ref, e64t_ref, tril_ref, triu_ref,
             y_ref, st_ref, dy_ref, du_in, tok_ref, du_ref, dxbc_ref, red_ref, dst, *scratch):
        del du_in, tok_ref

        @pl.when(pl.program_id(0) == 0)
        def _():
            dst[...] = jnp.zeros_like(dst)
            red_ref[...] = jnp.zeros_like(red_ref)
            scratch[-1][...] = jnp.zeros_like(scratch[-1])

        for sub in reversed(range(SSD_SUB)):
            rs = pl.ds(sub * T, T)
            chunk(u_ref.at[rs], xbc_ref.at[rs], bias_ref, alog_ref, dx_ref, nw_ref, e64_ref, e64t_ref, tril_ref, triu_ref,
                  y_ref.at[rs], st_ref.at[sub], dy_ref.at[rs], du_ref.at[rs], dxbc_ref.at[rs], red_ref, dst,
                  *[s.at[sub] for s in scratch])

    nb = n // SSD_SUB
    rows = SSD_SUB * T
    rev = lambda i: (nb - 1 - i, 0)
    sub_scratch = [(1, SSD_W), (LANE, T)] + [(T, SSD_W)] * 4 + [(T, LANE), (LANE, T)]
    return pl.pallas_call(
        body, name="ssd_bwd", grid=(nb,),
        in_specs=[pl.BlockSpec((rows, SSD_W + LANE), lambda i: (nb - 1 - i, OFF_Z // (SSD_W + LANE))),
                  pl.BlockSpec((rows, SSD_CONV), rev), _spec(bias), _spec(alog), _spec(dskip_x), _spec(nw),
                  _full(e64.shape), _full(e64t.shape), _full(tril.shape), _full(triu.shape),
                  pl.BlockSpec((rows, SSD_W), rev), pl.BlockSpec((SSD_SUB, SSD_N, SSD_W), lambda i: (nb - 1 - i, 0, 0)),
                  pl.BlockSpec((rows, SSD_W), lambda i: (nb - 1 - i, 1)), pl.BlockSpec(memory_space=pl.ANY),
                  pl.BlockSpec(memory_space=pl.ANY)],
        out_specs=[pl.BlockSpec((rows, SSD_W + LANE), lambda i: (nb - 1 - i, OFF_Z // (SSD_W + LANE))),
                   pl.BlockSpec((rows, SSD_CONV), rev), pl.BlockSpec((8, SSD_W), lambda i: (0, 0))],
        out_shape=[SDS((S, N_PAD), BF16), SDS((S, SSD_CONV), F32), SDS((8, SSD_W), F32)],
        scratch_shapes=[pltpu.VMEM((SSD_N, SSD_W), F32)] + [pltpu.VMEM((SSD_SUB,) + s, F32) for s in sub_scratch],
        input_output_aliases={13: 0},
        compiler_params=_cp(("arbitrary",)),
    )(u, xbc, _arr(bias), _arr(alog), _arr(dskip_x), _arr(nw), _bfc(e64), _bfc(e64t), _bfc(tril), _bfc(triu), y_ssd,
      states, dycat, du, tok)


def _bfc(a):
    return jnp.asarray(a, BF16)


def _outproj_fwd(ycat, wo, x, gate, tok):
    S = x.shape[0]
    tm = min(512, S)

    def body(yc_ref, wo_ref, x_ref, g_ref, tok_ref, xn_ref, y_ref):
        del tok_ref
        y = _mm(_bf(yc_ref[...]), wo_ref[...])
        y_ref[...] = y
        xn_ref[...] = x_ref[...] + g_ref[...] * y

    row = pl.BlockSpec((tm, D_MODEL), lambda i: (i, 0))
    return pl.pallas_call(
        body, name="outproj_fwd", grid=(S // tm,),
        in_specs=[pl.BlockSpec((tm, D_INNER), lambda i: (i, 0)), _full((D_INNER, D_MODEL)), row, _spec(gate),
                  pl.BlockSpec(memory_space=pl.ANY)],
        out_specs=[row, row],
        out_shape=[SDS((S, D_MODEL), F32), SDS((S, D_MODEL), F32)],
        compiler_params=_cp(("parallel",)),
    )(ycat, wo, x, _arr(gate), tok)


def _outproj_bwd(dxn, y, gate, ycat, wo):
    S = dxn.shape[0]
    tm = min(512, S)

    def body(dx_ref, y_ref, g_ref, yc_ref, wo_ref, dyc_ref, gwo_ref, dg_ref, acc):
        @pl.when(pl.program_id(0) == 0)
        def _():
            acc[...] = jnp.zeros_like(acc)
            dg_ref[...] = jnp.zeros_like(dg_ref)

        dxv = dx_ref[...]
        dy = _bf(dxv * g_ref[...])
        dg_ref[0:1, :] += jnp.sum(dxv * y_ref[...], axis=0, keepdims=True)
        dyc_ref[...] = _mm_nt(dy, wo_ref[...])
        acc[...] += _mm_tn(_bf(yc_ref[...]), dy)

        @pl.when(pl.program_id(0) == pl.num_programs(0) - 1)
        def _():
            gwo_ref[...] = acc[...].astype(BF16)

    row = pl.BlockSpec((tm, D_MODEL), lambda i: (i, 0))
    wide = pl.BlockSpec((tm, D_INNER), lambda i: (i, 0))
    return pl.pallas_call(
        body, name="outproj_bwd", grid=(S // tm,),
        in_specs=[row, row, _spec(gate), wide, _full((D_INNER, D_MODEL))],
        out_specs=[wide, _full((D_INNER, D_MODEL)), _full((8, D_MODEL))],
        out_shape=[SDS((S, D_INNER), F32), SDS((D_INNER, D_MODEL), BF16), SDS((8, D_MODEL), F32)],
        scratch_shapes=[pltpu.VMEM((D_INNER, D_MODEL), F32)],
        compiler_params=_cp(("arbitrary",)),
    )(dxn, y, _arr(gate), ycat, wo)


def _loss_head(x, fw, target):
    S = x.shape[0]
    tm = min(512, S)

    def body(x_ref, fw_ref, t_ref, dx_ref, red_ref):
        @pl.when(pl.program_id(0) == 0)
        def _():
            red_ref[...] = jnp.zeros_like(red_ref)

        xv = x_ref[...]
        fwv = fw_ref[...]
        inv = lax.rsqrt(jnp.mean(xv * xv, axis=-1, keepdims=True) + EPS)
        xhat = xv * inv
        err = xhat * fwv - t_ref[...]
        col = jnp.sum(err * err, axis=0, keepdims=True)
        red_ref[1:2, :] += jnp.broadcast_to(jnp.sum(col, axis=1, keepdims=True) * (0.5 / D_MODEL), (1, D_MODEL))
        dy = err * (1.0 / D_MODEL)
        red_ref[0:1, :] += jnp.sum(dy * xhat, axis=0, keepdims=True)
        dxhat = dy * fwv
        dx_ref[...] = inv * (dxhat - xhat * jnp.mean(dxhat * xhat, axis=-1, keepdims=True))

    row = pl.BlockSpec((tm, D_MODEL), lambda i: (i, 0))
    return pl.pallas_call(
        body, name="loss_head", grid=(S // tm,),
        in_specs=[row, _vec(D_MODEL), row],
        out_specs=[row, _full((8, D_MODEL))],
        out_shape=[SDS((S, D_MODEL), F32), SDS((8, D_MODEL), F32)],
        compiler_params=_cp(("arbitrary",)),
    )(x, fw, target)


ADA_COLS = 3 * D_MODEL // N_DEV


def _ada_fwd(c_all, w_ada, b_cols):
    def body(c_ref, w_ref, b_ref, out_ref):
        out_ref[...] = _mm(_bf(_silu(c_ref[...])), _bf(w_ref[...])) + b_ref[...]

    return pl.pallas_call(
        body, name="ada_fwd", grid=(DEPTH,),
        in_specs=[_full((N_DEV, D_MODEL)), pl.BlockSpec((None, D_MODEL, ADA_COLS), lambda l: (l, 0, 0)),
                  pl.BlockSpec((None, 1, ADA_COLS), lambda l: (l, 0, 0))],
        out_specs=pl.BlockSpec((None, N_DEV, ADA_COLS), lambda l: (l, 0, 0)),
        out_shape=SDS((DEPTH, N_DEV, ADA_COLS), F32),
        compiler_params=_cp(("parallel",)),
    )(c_all, w_ada, b_cols)


def _ada_bwd(ct_pad, dmod_pad):
    def body(c_ref, d_ref, out_ref):
        out_ref[...] = _mm(_bf(_silu(c_ref[...])), _bf(d_ref[...]))

    return pl.pallas_call(
        body, name="ada_bwd", grid=(DEPTH,),
        in_specs=[_full((D_MODEL, LANE)), pl.BlockSpec((None, LANE, ADA_COLS), lambda l: (l, 0, 0))],
        out_specs=pl.BlockSpec((None, D_MODEL, ADA_COLS), lambda l: (l, 0, 0)),
        out_shape=SDS((DEPTH, D_MODEL, ADA_COLS), F32),
        compiler_params=_cp(("parallel",)),
    )(ct_pad, dmod_pad)


def _adamw(parts, w, m, v, name, own=None, layers=None, prev=None):
    n, L, R, C = parts.shape
    lo, hi = layers or (0, L)
    tr = R
    while tr * C * 4 > (1 << 20) and tr % 16 == 0:
        tr //= 2
    first = 1 if own is None else 2

    def body(*refs):
        p_ref = refs[0]
        w_ref, m_ref, v_ref = refs[first:first + 3]
        g_ref, d_ref, mo_ref, vo_ref = refs[-4:]

        def part(k):
            if own is None:
                return p_ref[k].astype(F32)
            me = 4 * lax.axis_index("x") + 2 * lax.axis_index("y") + lax.axis_index("c")
            return jnp.where(me == k, refs[1][...], p_ref[k]).astype(F32)

        g = part(0)
        for k in range(1, n):
            g = g + part(k)
        mn = ADAM_B1 * m_ref[...] + (1.0 - ADAM_B1) * g
        vn = ADAM_B2 * v_ref[...] + (1.0 - ADAM_B2) * (g * g)
        m_hat = mn / (1.0 - ADAM_B1 ** ADAM_STEP)
        v_hat = vn / (1.0 - ADAM_B2 ** ADAM_STEP)
        g_ref[...] = g
        d_ref[...] = -ADAM_LR * (m_hat / (jnp.sqrt(v_hat) + ADAM_EPS) + ADAM_WD * w_ref[...])
        mo_ref[...] = mn
        vo_ref[...] = vn

    blk = pl.BlockSpec((None, tr, C), lambda l, i: (lo + l, i, 0))
    own_blk = [] if own is None else [pl.BlockSpec((None, tr, C), lambda l, i: (l, i, 0))]
    n_blk = 3 if own is None else 4
    return pl.pallas_call(
        body, name=name, grid=(hi - lo, R // tr),
        in_specs=[pl.BlockSpec((n, None, tr, C), lambda l, i: (0, lo + l, i, 0))] + own_blk + [blk] * 3
        + ([] if prev is None else [ANY] * 4),
        out_specs=[blk] * 4,
        out_shape=[SDS((L, R, C), F32)] * 4,
        input_output_aliases={} if prev is None else {1 + n_blk + k: k for k in range(4)},
        compiler_params=_cp(("parallel", "parallel")),
    )(parts, *([] if own is None else [own]), w, m, v, *([] if prev is None else prev))


MESH = pl.DeviceIdType.MESH
ANY = pl.BlockSpec(memory_space=pl.ANY)


def _all_gather(v, name):
    def body(v_ref, out_ref, send_sems, recv_sems, local_sem):
        x, y, c = lax.axis_index("x"), lax.axis_index("y"), lax.axis_index("c")
        me, sibling = (x, y, c), (x, y, 1 - c)
        chips = [(1 - x, y), (x, 1 - y), (1 - x, 1 - y)]

        def slot(px, py, pc):
            return out_ref.at[4 * px + 2 * py + pc]

        def copy(k, block, to, src=None):
            return pltpu.make_async_remote_copy(
                src_ref=slot(*block) if src is None else src, dst_ref=slot(*block),
                send_sem=send_sems.at[k], recv_sem=recv_sems.at[k], device_id=to, device_id_type=MESH)

        mine = pltpu.make_async_copy(v_ref, slot(*me), local_sem)
        mine.start()
        first = [copy(0, me, sibling, src=v_ref)]
        first += [copy(1 + j, me, (*chip, c), src=v_ref) for j, chip in enumerate(chips)]
        for cp in first:
            cp.start()
        passed = [copy(4 + j, (*chip, c), sibling) for j, chip in enumerate(chips)]
        for j, chip in enumerate(chips):
            copy(1 + j, (*chip, c), me).wait_recv()
            passed[j].start()
        copy(0, sibling, me).wait_recv()
        for j, chip in enumerate(chips):
            copy(4 + j, (*chip, 1 - c), me).wait_recv()
        for cp in first + passed:
            cp.wait_send()
        mine.wait()

    return pl.pallas_call(
        body, name=name, in_specs=[ANY], out_specs=ANY,
        out_shape=SDS((N_DEV,) + v.shape, v.dtype),
        scratch_shapes=[pltpu.SemaphoreType.DMA((7,)), pltpu.SemaphoreType.DMA((7,)), pltpu.SemaphoreType.DMA],
    )(v)


def _all_to_all(v, name):
    def body(v_ref, out_ref, send_sems, recv_sems, local_sem):
        x, y, c = lax.axis_index("x"), lax.axis_index("y"), lax.axis_index("c")
        mine_idx = 4 * x + 2 * y + c
        mine = pltpu.make_async_copy(v_ref.at[mine_idx], out_ref.at[mine_idx], local_sem)
        mine.start()
        sends, recvs = [], []
        for k in range(1, N_DEV):
            px = 1 - x if k & 4 else x
            py = 1 - y if k & 2 else y
            pc = 1 - c if k & 1 else c
            peer_idx = 4 * px + 2 * py + pc
            sems = dict(send_sem=send_sems.at[k - 1], recv_sem=recv_sems.at[k - 1], device_id=(px, py, pc),
                        device_id_type=MESH)
            sends.append(pltpu.make_async_remote_copy(src_ref=v_ref.at[peer_idx], dst_ref=out_ref.at[mine_idx], **sems))
            recvs.append(pltpu.make_async_remote_copy(src_ref=v_ref.at[peer_idx], dst_ref=out_ref.at[peer_idx], **sems))
        for cp in sends:
            cp.start()
        for cp in recvs:
            cp.wait_recv()
        for cp in sends:
            cp.wait_send()
        mine.wait()

    return pl.pallas_call(
        body, name=name, in_specs=[ANY], out_specs=ANY,
        out_shape=SDS(v.shape, v.dtype),
        scratch_shapes=[pltpu.SemaphoreType.DMA((7,)), pltpu.SemaphoreType.DMA((7,)), pltpu.SemaphoreType.DMA],
    )(v)


HBM_SPEC = pl.BlockSpec(memory_space=pltpu.HBM)
SEM_SPEC = pl.BlockSpec(memory_space=pltpu.SEMAPHORE)
EFFECT = pltpu.SideEffectType.DATAFLOW_SIDE_EFFECTING


EXCHANGE_PEERS = {"gather": range(1, N_DEV), "scatter": range(1, N_DEV), "chip": (1, 2, 4, 6), "pass": (2, 4, 6)}


def _exchange_copies(srcs, lands, send_sems, recv_sems, mode, layer):
    x, y, c = lax.axis_index("x"), lax.axis_index("y"), lax.axis_index("c")
    me = 4 * x + 2 * y + c
    copies = []
    for a, (src, land) in enumerate(zip(srcs, lands)):
        for k in EXCHANGE_PEERS[mode]:
            px = 1 - x if k & 4 else x
            py = 1 - y if k & 2 else y
            pc = 1 - c if k & 1 else c
            peer = 4 * px + 2 * py + pc
            if mode == "scatter":
                s, d, to = src.at[peer], land.at[me, layer], (px, py, pc)
            elif mode == "pass":
                s, d, to = land.at[peer], land.at[peer], (x, y, 1 - c)
            else:
                s, d, to = src, land.at[me], (px, py, pc)
            n = 7 * a + k - 1
            copies.append(pltpu.make_async_remote_copy(
                src_ref=s, dst_ref=d, send_sem=send_sems.at[n], recv_sem=recv_sems.at[n], device_id=to,
                device_id_type=MESH))
    return copies


def _exchange_start(name, srcs, lands, mode, layer=0, after=None):
    n = len(srcs)

    def body(*refs):
        send_sems, recv_sems = refs[-2 * n - 3], refs[-2 * n - 2]
        for cp in _exchange_copies(refs[:n], refs[n:2 * n], send_sems, recv_sems, mode, layer):
            cp.start()
        refs[-1][...] = jnp.zeros_like(refs[-1])

    arrays = list(srcs) + list(lands)
    sems = pltpu.SemaphoreType.DMA((7 * n,))
    out = pl.pallas_call(
        body, name=name,
        out_shape=(sems, sems, *[pltpu.HBM(v.shape, v.dtype) for v in arrays], SDS((8, LANE), F32)),
        in_specs=[HBM_SPEC] * (2 * n) + ([ANY] if after is not None else []),
        out_specs=(SEM_SPEC, SEM_SPEC, *[HBM_SPEC] * (2 * n), pl.BlockSpec(memory_space=pltpu.VMEM)),
        input_output_aliases={i: 2 + i for i in range(2 * n)},
        compiler_params=pltpu.CompilerParams(has_side_effects=EFFECT),
    )(*[pltpu.with_memory_space_constraint(v, pltpu.HBM) for v in arrays], *([after] if after is not None else []))
    return dict(sems=out[:2], srcs=out[2:2 + n], lands=out[2 + n:2 + 2 * n], token=out[-1], mode=mode,
                layer=layer)


def _exchange_wait(name, st, after, also=()):
    n = len(st["srcs"])

    def body(*refs):
        send_sems, recv_sems = refs[2 * n], refs[2 * n + 1]
        for cp in _exchange_copies(refs[:n], refs[n:2 * n], send_sems, recv_sems, st["mode"], st["layer"]):
            cp.wait_send()
            cp.wait_recv()

    arrays = list(st["srcs"]) + list(st["lands"])
    out = pl.pallas_call(
        body, name=name,
        out_shape=tuple(pltpu.HBM(v.shape, v.dtype) for v in arrays),
        in_specs=[HBM_SPEC] * (2 * n) + [SEM_SPEC, SEM_SPEC] + [ANY] * (1 + len(also)),
        out_specs=tuple([HBM_SPEC] * (2 * n)),
        input_output_aliases={i: i for i in range(2 * n)},
        compiler_params=pltpu.CompilerParams(has_side_effects=EFFECT),
    )(*arrays, *st["sems"], after, *also)
    st["srcs"] = out[:n]
    return out[n:]


_IN_PIECES = ([(1024, 3072)]
              + [r for t in range(4) for r in ((LANE * t, LANE * (t + 1)), (512 + LANE * t, 512 + LANE * (t + 1)))]
              + [(4096, 5632), (3072, 4096), (5632, 5648)])


def _permute_in(w):
    pad = jnp.zeros(w.shape[:-1] + (N_PAD - N_IN,), w.dtype)
    return jnp.concatenate([w[..., a:b] for a, b in _IN_PIECES] + [pad], axis=-1)


def _unpermute_in(g):
    ax = [g[..., OFF_LRU + 2 * LANE * t:OFF_LRU + 2 * LANE * t + LANE] for t in range(4)]
    ag = [g[..., OFF_LRU + 2 * LANE * t + LANE:OFF_LRU + 2 * LANE * (t + 1)] for t in range(4)]
    return jnp.concatenate(ax + ag + [g[..., 0:2048], g[..., OFF_Z:OFF_Z + SSD_W], g[..., OFF_XBC:OFF_XBC + SSD_CONV],
                                      g[..., OFF_Z + SSD_W:OFF_Z + SSD_W + SSD_HEADS]], axis=-1)


SHARD_COLS = N_IN // N_DEV


def _in_segments():
    segs, pos = [], 0
    for a, b in _IN_PIECES:
        for i in range(N_DEV):
            lo, hi = max(a, SHARD_COLS * i), min(b, SHARD_COLS * (i + 1))
            if lo < hi:
                segs.append((i, lo - SHARD_COLS * i, hi - lo, pos + lo - a))
        pos += b - a
    return segs


RELAYOUT_ROWS = 512


def _relayout_in(land, own):
    def body(land_ref, own_ref, out_ref):
        me = 4 * lax.axis_index("x") + 2 * lax.axis_index("y") + lax.axis_index("c")
        out_ref[:, N_IN:N_PAD] = jnp.zeros((RELAYOUT_ROWS, N_PAD - N_IN), BF16)
        for i, j, wd, p in _in_segments():
            out_ref[:, p:p + wd] = jnp.where(me == i, own_ref[:, j:j + wd], land_ref[i, :, j:j + wd])

    return pl.pallas_call(
        body, name="relayout_in", grid=(D_MODEL // RELAYOUT_ROWS,),
        in_specs=[pl.BlockSpec((N_DEV, RELAYOUT_ROWS, SHARD_COLS), lambda r: (0, r, 0)),
                  pl.BlockSpec((RELAYOUT_ROWS, SHARD_COLS), lambda r: (r, 0))],
        out_specs=pl.BlockSpec((RELAYOUT_ROWS, N_PAD), lambda r: (r, 0)),
        out_shape=SDS((D_MODEL, N_PAD), BF16),
        compiler_params=_cp(("parallel",)),
    )(land, own)


def _relayout_grad(g):
    def body(g_ref, out_ref):
        for i, j, wd, p in _in_segments():
            out_ref[i, :, j:j + wd] = g_ref[:, p:p + wd].astype(BF16)

    return pl.pallas_call(
        body, name="relayout_grad", grid=(D_MODEL // RELAYOUT_ROWS,),
        in_specs=[pl.BlockSpec((RELAYOUT_ROWS, N_PAD), lambda r: (r, 0))],
        out_specs=pl.BlockSpec((N_DEV, RELAYOUT_ROWS, SHARD_COLS), lambda r: (0, r, 0)),
        out_shape=SDS((N_DEV, D_MODEL, SHARD_COLS), BF16),
        compiler_params=_cp(("parallel",)),
    )(g)


def _block_diag(w):
    w4 = w.reshape(DEPTH, 4, 2, 64, 64)
    z = jnp.zeros((DEPTH, 4, 64, 64), w.dtype)
    top = jnp.concatenate([w4[:, :, 0], z], axis=-1)
    bot = jnp.concatenate([z, w4[:, :, 1]], axis=-1)
    return jnp.concatenate([top, bot], axis=2).astype(BF16)


def _diag_blocks(g):
    return jnp.stack([g[:, :, :64, :64], g[:, :, 64:, 64:]], axis=2).reshape(DEPTH, 8, 64, 64)


def _pad_lanes(v):
    return jnp.pad(v, ((0, 0), (0, LANE - v.shape[1])))


def _lower_bounds(logits):
    p = jax.nn.softmax(logits, axis=0)
    return p, jnp.cumsum(p, axis=0) - p[0]


def _lower_bounds_bwd(p, dlb):
    dp = jnp.cumsum(dlb[::-1], axis=0)[::-1]
    dp = dp.at[0].add(-jnp.sum(dlb, axis=0))
    return p * (dp - jnp.sum(dp * p, axis=0, keepdims=True))


SMALL = ["norm_w", "b_ada", "lru_conv_b", "lru_wa", "lru_ba", "lru_wx", "lru_bx", "lru_lambda", "hg_lb_logits",
         "hg_norm_w", "ssd_conv_b", "ssd_dt_bias", "ssd_a_log", "ssd_d", "ssd_norm_w", "final_norm_w"]
WEIGHTS = ["norm_w", "w_ada", "b_ada", "w_in", "lru_conv_w", "lru_conv_b", "lru_wa", "lru_ba", "lru_wx", "lru_bx",
           "lru_lambda", "hg_lb_logits", "hg_norm_w", "ssd_conv_w", "ssd_conv_b", "ssd_dt_bias", "ssd_a_log", "ssd_d",
           "ssd_norm_w", "w_out", "final_norm_w"]
INPUTS = ["x", "c"] + WEIGHTS + ["loss_target"] + ["m_" + n for n in WEIGHTS] + ["v_" + n for n in WEIGHTS]
SMALL_ROW = 1024


def _small_rows(like):
    out, off = {}, 0
    for n in SMALL:
        rows = -(-int(np.prod(like[n].shape)) // (8 * SMALL_ROW)) * 8
        out[n] = (off, rows)
        off += rows
    return out, off


def _flatten_small(d, prefix="", last=0.0):
    table, _ = _small_rows({n: d[prefix + n] for n in SMALL})
    pieces = []
    for n in SMALL:
        flat = d[prefix + n].reshape(-1)
        pieces.append(jnp.pad(flat, (0, table[n][1] * SMALL_ROW - flat.shape[0])).reshape(-1, SMALL_ROW))
    return jnp.concatenate(pieces + [jnp.full((8, SMALL_ROW), last, F32)], axis=0)


def _split_small(packed, like):
    table, _ = _small_rows(like)
    out = {}
    for n in SMALL:
        off, rows = table[n]
        size = int(np.prod(like[n].shape))
        out[n] = packed[off:off + rows].reshape(-1)[:size].reshape(like[n].shape)
    return out


def _local_step(x, mod, target, w, fetch, emit):
    S = x.shape[0]
    mall = _bfc(_hg_consts())
    mall_t = _bfc(_hg_consts().T)
    consts = _ssd_consts()
    p_lb, lbs = _lower_bounds(w["hg_lb_logits"])
    no_tok = jnp.zeros((8, LANE), F32)
    wa, wx = _block_diag(w["lru_wa"]), _block_diag(w["lru_wx"])
    ba, bx = w["lru_ba"].reshape(DEPTH, 1, LRU_W), w["lru_bx"].reshape(DEPTH, 1, LRU_W)
    lru_cb, lam, ssd_cb = w["lru_conv_b"][:, None], w["lru_lambda"][:, None], w["ssd_conv_b"][:, None]
    bias, alog = _pad_lanes(w["ssd_dt_bias"]), _pad_lanes(w["ssd_a_log"])
    dskip = jnp.repeat(w["ssd_d"], SSD_P, axis=1)
    saved = []
    for l in range(DEPTH):
        w_in_l, w_out_l, token = fetch(l, x)
        shift, scale, gate = (_Row(mod, l, D_MODEL, k) for k in range(3))
        nw = _Row(w["norm_w"], l)
        u, h = _inproj_fwd(x, nw, scale, shift, w_in_l, no_tok if token is None else token)
        ycat = lax.empty((S, D_INNER), BF16)
        lru_args = (l, u, w["lru_conv_w"], lru_cb, wa, ba, wx, bx, lam)
        ycat, h_lru = _lru_fwd(*lru_args, ycat)
        hg_args = (u, _Row(lbs, l), _Row(w["hg_norm_w"], l), mall)
        ycat, o_b, hg_st = _hg_fwd(*hg_args, ycat)
        xbc = _ssdconv_fwd(l, u, w["ssd_conv_w"], ssd_cb)
        ssd_args = (u, xbc, _Row(bias, l), _Row(alog, l), _Row(dskip, l), _Row(w["ssd_norm_w"], l), consts)
        ycat, y_ssd, ssd_st = _ssd_fwd(*ssd_args, ycat)
        token = fetch(l, y_ssd, late=True)
        x_new, y = _outproj_fwd(ycat, w_out_l, x, gate, no_tok if token is None else token)
        saved.append((x, u, h, ycat, nw, scale, gate, w_in_l, w_out_l, lru_args, h_lru, hg_args, o_b, hg_st, ssd_args,
                      y_ssd, ssd_st, y))
        x = x_new
    dx, red = _loss_head(x, w["final_norm_w"][None, :], target)
    loss = red[1, 0]
    reds = {k: [None] * DEPTH for k in ("in", "gate", "lru", "wa", "wx", "hg", "conv", "ssd")}
    for l in reversed(range(DEPTH)):
        (x, u, h, ycat, nw, scale, gate, w_in_l, w_out_l, lru_args, h_lru, hg_args, o_b, hg_st, ssd_args, y_ssd, ssd_st,
         y) = saved[l]
        dycat, g_out, reds["gate"][l] = _outproj_bwd(dx, y, gate, ycat, w_out_l)
        token = emit(l, "w_out", g_out)
        du = lax.empty((S, N_PAD), BF16)
        du, dxbc, reds["ssd"][l] = _ssd_bwd(*ssd_args, y_ssd, ssd_st, dycat, du, no_tok if token is None else token)
        du, reds["conv"][l] = _ssdconv_bwd(l, u, w["ssd_conv_w"], ssd_cb, dxbc, du)
        du, reds["hg"][l] = _hg_bwd(*hg_args, mall_t, o_b, hg_st, dycat, du)
        du, reds["lru"][l], reds["wa"][l], reds["wx"][l] = _lru_bwd(*lru_args, h_lru, dycat, du)
        token = emit(l, "w_in", functools.partial(_inproj_bwd_w, h, du))
        dx, reds["in"][l] = _inproj_bwd_x(du, w_in_l, x, nw, scale, dx, no_tok if token is None else token)
    r = {k: jnp.stack(v) for k, v in reds.items()}
    g = {n: None for n in WEIGHTS}
    g["final_norm_w"] = red[0]
    g["norm_w"] = r["in"][:, 2]
    dmod = jnp.concatenate([r["in"][:, 0], r["in"][:, 1], r["gate"][:, 0]], axis=1)
    g["lru_conv_w"], g["lru_conv_b"] = r["lru"][:, 0:4], r["lru"][:, 4]
    g["lru_ba"], g["lru_bx"] = r["lru"][:, 5].reshape(DEPTH, 8, 64), r["lru"][:, 6].reshape(DEPTH, 8, 64)
    g["lru_lambda"] = r["lru"][:, 7]
    g["lru_wa"], g["lru_wx"] = _diag_blocks(r["wa"]), _diag_blocks(r["wx"])
    g["hg_norm_w"] = r["hg"][:, 0]
    g["hg_lb_logits"] = _lower_bounds_bwd(p_lb, r["hg"][:, 1])
    g["ssd_conv_w"], g["ssd_conv_b"] = r["conv"][:, 0:4], r["conv"][:, 4]
    g["ssd_norm_w"] = r["ssd"][:, 0]
    g["ssd_d"] = r["ssd"][:, 1].reshape(DEPTH, SSD_HEADS, SSD_P).sum(-1)
    g["ssd_dt_bias"] = r["ssd"][:, 2, :SSD_HEADS]
    g["ssd_a_log"] = -r["ssd"][:, 3, :SSD_HEADS] * jnp.exp(w["ssd_a_log"])
    return loss, dx, dmod, g


def kernel(x, c, norm_w, w_ada, b_ada, w_in, lru_conv_w, lru_conv_b, lru_wa, lru_ba, lru_wx, lru_bx, lru_lambda, hg_lb_logits, hg_norm_w, ssd_conv_w, ssd_conv_b, ssd_dt_bias, ssd_a_log, ssd_d, ssd_norm_w, w_out, final_norm_w, loss_target, m_norm_w, m_w_ada, m_b_ada, m_w_in, m_lru_conv_w, m_lru_conv_b, m_lru_wa, m_lru_ba, m_lru_wx, m_lru_bx, m_lru_lambda, m_hg_lb_logits, m_hg_norm_w, m_ssd_conv_w, m_ssd_conv_b, m_ssd_dt_bias, m_ssd_a_log, m_ssd_d, m_ssd_norm_w, m_w_out, m_final_norm_w, v_norm_w, v_w_ada, v_b_ada, v_w_in, v_lru_conv_w, v_lru_conv_b, v_lru_wa, v_lru_ba, v_lru_wx, v_lru_bx, v_lru_lambda, v_hg_lb_logits, v_hg_norm_w, v_ssd_conv_w, v_ssd_conv_b, v_ssd_dt_bias, v_ssd_a_log, v_ssd_d, v_ssd_norm_w, v_w_out, v_final_norm_w):
    return _step(x, c, norm_w, w_ada, b_ada, w_in, lru_conv_w, lru_conv_b, lru_wa, lru_ba, lru_wx, lru_bx, lru_lambda, hg_lb_logits, hg_norm_w, ssd_conv_w, ssd_conv_b, ssd_dt_bias, ssd_a_log, ssd_d, ssd_norm_w, w_out, final_norm_w, loss_target, m_norm_w, m_w_ada, m_b_ada, m_w_in, m_lru_conv_w, m_lru_conv_b, m_lru_wa, m_lru_ba, m_lru_wx, m_lru_bx, m_lru_lambda, m_hg_lb_logits, m_hg_norm_w, m_ssd_conv_w, m_ssd_conv_b, m_ssd_dt_bias, m_ssd_a_log, m_ssd_d, m_ssd_norm_w, m_w_out, m_final_norm_w, v_norm_w, v_w_ada, v_b_ada, v_w_in, v_lru_conv_w, v_lru_conv_b, v_lru_wa, v_lru_ba, v_lru_wx, v_lru_bx, v_lru_lambda, v_hg_lb_logits, v_hg_norm_w, v_ssd_conv_w, v_ssd_conv_b, v_ssd_dt_bias, v_ssd_a_log, v_ssd_d, v_ssd_norm_w, v_w_out, v_final_norm_w)


def _step(*args):
    a = dict(zip(INPUTS, args, strict=True))
    me = 4 * lax.axis_index("x") + 2 * lax.axis_index("y") + lax.axis_index("c")
    x, target = a["x"][0], a["loss_target"][0]

    c_all = _all_gather(a["c"], "gather_c")[:, 0, :]
    b_cols = lax.dynamic_slice_in_dim(a["b_ada"], me * ADA_COLS, ADA_COLS, axis=1)[:, None, :]
    mod_parts = _all_gather(_ada_fwd(c_all, a["w_ada"], b_cols), "gather_mod")
    mod = lax.dynamic_index_in_dim(mod_parts, me, axis=2, keepdims=False)
    mod = mod.transpose(1, 0, 2).reshape(DEPTH, 3 * D_MODEL)

    w = {n: a[n] for n in SMALL}

    w_in_b = [a["w_in"][l].astype(BF16) for l in range(DEPTH)]
    w_out_b = a["w_out"].astype(BF16)
    conv_own = jnp.concatenate([a["lru_conv_w"], a["ssd_conv_w"]], axis=-1)
    cols, rows_out = N_IN // N_DEV, D_INNER // N_DEV

    def gather_start(l, after):
        srcs = [w_in_b[l], w_out_b[l]] + ([conv_own] if l == 0 else [])
        lands = [lax.empty((N_DEV,) + s.shape, s.dtype) for s in srcs]
        return _exchange_start(f"gather_start_{l}", srcs, lands, "chip", after=after)

    def gather_pass(name, st, after, also=()):
        landed = _exchange_wait(name + "_wait", st, after, also)
        st2 = _exchange_start(name + "_pass", st["srcs"], landed, "pass")
        return _exchange_wait(name + "_passed", st2, after)

    gathers = {0: gather_start(0, mod)}
    passing = {}

    def fetch(l, x_l, late=False):
        if late:
            if l + 1 == DEPTH:
                return None
            landed = _exchange_wait(f"gather_{l + 1}_wait", gathers[l + 1], x_l)
            passing[l + 1] = _exchange_start(f"gather_{l + 1}_pass", gathers[l + 1]["srcs"], landed, "pass")
            return passing[l + 1]["token"]
        if l == 0:
            landed = gather_pass("gather_0", gathers[0], x_l, also=(a["w_in"], a["m_w_in"], a["v_w_in"]))
        else:
            landed = _exchange_wait(f"gather_{l}_passed", passing[l], x_l)
        land_out = lax.dynamic_update_index_in_dim(landed[1], w_out_b[l], me, 0)
        if l == 0:
            conv = lax.dynamic_update_index_in_dim(landed[2], conv_own, me, 0).transpose(1, 2, 0, 3)
            w["lru_conv_w"] = conv[..., :64].reshape(DEPTH, 4, LRU_W)
            w["ssd_conv_w"] = conv[..., 64:].reshape(DEPTH, 4, SSD_CONV)
        token = None
        if l + 1 < DEPTH:
            gathers[l + 1] = gather_start(l + 1, land_out)
            token = gathers[l + 1]["token"]
        return _relayout_in(landed[0], w_in_b[l]), land_out.reshape(D_INNER, D_MODEL), token

    PROJ = ("w_in", "w_out")
    scatters = {}
    lands = [lax.empty((N_DEV, DEPTH, D_MODEL, cols), BF16), lax.empty((N_DEV, DEPTH, rows_out, D_MODEL), BF16)]
    own = [None] * DEPTH

    deferred, g_out = {}, {}

    def emit(l, name, grad, after=None):
        if name == "w_out" and l > 0:
            g_out[l] = grad
            return None
        if name == "w_in" and l == 0 and after is None:
            deferred["w_in"] = grad
            return None
        if name == "w_in":
            grad = grad(jnp.zeros((8, LANE), F32) if after is None else after)
        if l == 0:
            k = PROJ.index(name)
            src = _relayout_grad(grad) if name == "w_in" else grad.reshape(N_DEV, rows_out, D_MODEL)
            st = _exchange_start(f"scatter_start_0_{name}", [src], [lands[k]], "scatter", layer=0, after=after)
            scatters[name] = st
            lands[k] = st["lands"][0]
            return st["token"]
        srcs = [_relayout_grad(grad), g_out[l].reshape(N_DEV, rows_out, D_MODEL)]
        st = _exchange_start(f"scatter_start_{l}", srcs, lands, "scatter", layer=l, after=after)
        scatters[l] = st
        lands[:] = st["lands"]
        return st["token"]

    loss_own, dx, dmod, g = _local_step(x, mod, target, w, fetch, emit)

    def sharded(name, parts, own=None, **kw):
        return _adamw(parts, a[name], a["m_" + name], a["v_" + name], "adamw_" + name + kw.pop("tag", ""), own=own, **kw)

    g["b_ada"] = dmod
    small_own = _flatten_small(g, last=loss_own)
    small_st = _exchange_start("gather_small", [small_own], [lax.empty((N_DEV,) + small_own.shape, F32)], "chip",
                               after=dx)
    big = {}
    after = emit(0, "w_in", deferred["w_in"], after=small_st["token"]) + dx[0:8, 0:LANE]

    def own_slices(st):
        return [lax.dynamic_index_in_dim(s, me, 0, keepdims=False) for s in st["srcs"]]

    for l in reversed(range(1, DEPTH)):
        scatters[l]["lands"] = lands
        lands[:] = _exchange_wait(f"scatter_wait_{l}", scatters[l], after)
        own[l] = own_slices(scatters[l])
    scatters["w_out"]["lands"] = [lands[1]]
    lands[1] = _exchange_wait("scatter_wait_0_w_out", scatters["w_out"], after)[0]
    own[0] = [None, own_slices(scatters["w_out"])[0]]
    big["w_out"] = sharded("w_out", lands[1], jnp.stack([own[l][1] for l in range(DEPTH)]))
    upper = sharded("w_in", lands[0], jnp.stack([own[l][0] for l in range(1, DEPTH)]), layers=(1, DEPTH), tag="_upper")
    after = upper[1][0, 0:8, 0:LANE] + big["w_out"][1][0, 0:8, 0:LANE]
    small = gather_pass("gather_small", small_st, after)[0]
    outs = _adamw(small[:, None], *[_flatten_small(a, p)[None] for p in ("", "m_", "v_")], "adamw_small",
                  own=small_own[None])
    res = [_split_small(o[0], a) for o in outs]
    losses = lax.dynamic_update_index_in_dim(small[:, -1, 0], loss_own, me, 0)
    loss = jnp.sum(losses)

    off = _small_rows(a)[0]["b_ada"][0]
    dmod_all = lax.dynamic_update_index_in_dim(small[:, off:off + DEPTH * 3 * D_MODEL // SMALL_ROW],
                                               dmod.reshape(-1, SMALL_ROW), me, 0)
    dmod_all = dmod_all.reshape(N_DEV, DEPTH, 3 * D_MODEL).transpose(1, 0, 2)
    dmod_cols = lax.dynamic_slice_in_dim(dmod_all, me * ADA_COLS, ADA_COLS, axis=2)
    dmod_pad = jnp.pad(dmod_cols, ((0, 0), (0, LANE - N_DEV), (0, 0)))
    ct_pad = jnp.pad(c_all.T, ((0, 0), (0, LANE - N_DEV)))
    big["w_ada"] = sharded("w_ada", _ada_bwd(ct_pad, dmod_pad)[None])
    g_conv = jnp.concatenate([g["lru_conv_w"].reshape(DEPTH, 4, N_DEV, 64), g["ssd_conv_w"].reshape(DEPTH, 4, N_DEV, 192)],
                             axis=-1).transpose(2, 0, 1, 3)
    conv_parts = _all_to_all(g_conv, "scatter_conv")
    big["lru_conv_w"] = sharded("lru_conv_w", conv_parts[..., :64])
    big["ssd_conv_w"] = sharded("ssd_conv_w", conv_parts[..., 64:])

    after = outs[1] + big["w_ada"][1][0, 0:1, 0:1]
    scatters["w_in"]["lands"] = [lands[0]]
    lands[0] = _exchange_wait("scatter_wait_0_w_in", scatters["w_in"], after)[0]
    big["w_in"] = sharded("w_in", lands[0], own_slices(scatters["w_in"])[0][None], layers=(0, 1), prev=upper)

    out = [loss, dx[None]]
    for k in range(4):
        out += [big[n][k] if n in big else res[k][n] for n in WEIGHTS]
    return tuple(out)
```

```python
import functools

import numpy as np
import jax
import jax.numpy as jnp
from jax import lax
from jax.experimental import pallas as pl
from jax.experimental.pallas import tpu as pltpu

F32 = jnp.float32
BF16 = jnp.bfloat16
SDS = jax.ShapeDtypeStruct

N_DEV = 8
DEPTH = 4
D_MODEL = 1024
D_INNER = 2048
EPS = 1e-6
LRU_W = 512
LRU_C = 8.0
HG_W = 512
HG_CHUNK = 64
HG_HEADS = 4
SSD_W = 1024
SSD_HEADS = 16
SSD_P = 64
SSD_N = 128
SSD_CHUNK = 128
SSD_CONV = 1536
N_IN = 5648
N_PAD = 5760
OFF_HG = 0
OFF_LRU = 2048
OFF_XBC = 3072
OFF_Z = 4608
LANE = 128
VMEM_LIMIT = 56 * 1024 * 1024
NEG = -1e30

ADAM_LR = 0.001
ADAM_B1 = 0.9
ADAM_B2 = 0.999
ADAM_EPS = 1e-08
ADAM_WD = 0.01
ADAM_STEP = 10


def _cp(sem=None):
    return pltpu.CompilerParams(dimension_semantics=sem, vmem_limit_bytes=VMEM_LIMIT)


def _dg(a, b, ca, cb):
    return lax.dot_general(a, b, (((ca,), (cb,)), ((), ())), preferred_element_type=F32)


def _mm(a, b):
    return _dg(a, b, 1, 0)


def _mm_nt(a, b):
    return _dg(a, b, 1, 1)


def _mm_tn(a, b):
    return _dg(a, b, 0, 0)


def _bf(x):
    return x.astype(BF16)


def _f(x):
    return x.astype(F32)


def _split3(x):
    hi = x.astype(BF16)
    r = x - hi.astype(F32)
    mid = r.astype(BF16)
    lo = (r - mid.astype(F32)).astype(BF16)
    return hi, mid, lo


def _sel_r(x, m):
    hi, mid, lo = _split3(x)
    return _mm(hi, m) + _mm(mid, m) + _mm(lo, m)


def _sel_l(m, x):
    hi, mid, lo = _split3(x)
    return _mm(m, hi) + _mm(m, mid) + _mm(m, lo)


def _sel_l2(m, x):
    hi = x.astype(BF16)
    lo = (x - hi.astype(F32)).astype(BF16)
    return _mm(m, hi) + _mm(m, lo)


def _sel_tn(x, m):
    hi, mid, lo = _split3(x)
    return _mm_tn(hi, m) + _mm_tn(mid, m) + _mm_tn(lo, m)


def _sigmoid(x):
    return 1.0 / (1.0 + jnp.exp(-x))


def _silu(x):
    return x * _sigmoid(x)


def _dsilu(x):
    s = _sigmoid(x)
    return s * (1.0 + x * (1.0 - s))


def _softplus(x):
    return jnp.maximum(x, 0.0) + jnp.log(1.0 + jnp.exp(-jnp.abs(x)))


def _expm1(z):
    series = z * (1.0 + z * (1.0 / 2) * (1.0 + z * (1.0 / 3) * (1.0 + z * (1.0 / 4) * (
        1.0 + z * (1.0 / 5) * (1.0 + z * (1.0 / 6) * (1.0 + z * (1.0 / 7)))))))
    return jnp.where(jnp.abs(z) < 0.3, series, jnp.exp(z) - 1.0)


def _iota(shape, dim):
    return lax.broadcasted_iota(jnp.int32, shape, dim)


def _last_row(x, rows):
    return jnp.sum(jnp.where(rows == x.shape[0] - 1, x, 0.0), axis=0, keepdims=True)


def _shift_down(x, d, rows, fill=0.0):
    return jnp.where(rows >= d, pltpu.roll(x, d, 0), fill)


def _shift_up(x, d, rows, fill=0.0):
    n = x.shape[0]
    return jnp.where(rows < n - d, pltpu.roll(x, n - d, 0), fill)


def _conv_fwd(x, cw_ref, cb_ref, rows):
    out = cb_ref[...] + cw_ref[pl.ds(3, 1), :] * x
    for k in range(3):
        out = out + cw_ref[pl.ds(k, 1), :] * _shift_down(x, 3 - k, rows)
    return out


def _conv_bwd(x, dco, cw_ref, rows):
    dx = cw_ref[pl.ds(3, 1), :] * dco
    dws = []
    for k in range(3):
        dx = dx + cw_ref[pl.ds(k, 1), :] * _shift_up(dco, 3 - k, rows)
        dws.append(jnp.sum(dco * _shift_down(x, 3 - k, rows), axis=0, keepdims=True))
    dws.append(jnp.sum(dco * x, axis=0, keepdims=True))
    return dx, dws, jnp.sum(dco, axis=0, keepdims=True)


def _vec(n):
    return pl.BlockSpec((1, n), lambda *_: (0, 0))


class _Row:
    def __init__(self, arr, l, n=None, c=0):
        self.arr, self.l, self.n, self.c = arr[:, None, :], l, n or arr.shape[1], c


def _spec(v):
    if isinstance(v, _Row):
        return pl.BlockSpec((None, 1, v.n), lambda *_: (v.l, 0, v.c))
    return _vec(v.shape[1])


def _arr(v):
    return v.arr if isinstance(v, _Row) else v


def _full(shape):
    nd = len(shape)
    return pl.BlockSpec(shape, lambda *_: (0,) * nd)


def _inproj_fwd(x, nw, scale, shift, w, tok):
    S = x.shape[0]
    tm = min(256, S)

    def body(x_ref, nw_ref, sc_ref, sh_ref, w_ref, tok_ref, u_ref, h_ref):
        del tok_ref
        xv = x_ref[...]
        inv = lax.rsqrt(jnp.mean(xv * xv, axis=-1, keepdims=True) + EPS)
        h = ((xv * inv) * nw_ref[...] * (1.0 + sc_ref[...]) + sh_ref[...]).astype(BF16)
        h_ref[...] = h
        u_ref[...] = _mm(h, w_ref[...])

    return pl.pallas_call(
        body, name="inproj_fwd", grid=(S // tm,),
        in_specs=[pl.BlockSpec((tm, D_MODEL), lambda i: (i, 0)), _spec(nw), _spec(scale), _spec(shift),
                  _full((D_MODEL, N_PAD)), pl.BlockSpec(memory_space=pl.ANY)],
        out_specs=[pl.BlockSpec((tm, N_PAD), lambda i: (i, 0)), pl.BlockSpec((tm, D_MODEL), lambda i: (i, 0))],
        out_shape=[SDS((S, N_PAD), F32), SDS((S, D_MODEL), BF16)],
        compiler_params=_cp(("parallel",)),
    )(x, _arr(nw), _arr(scale), _arr(shift), w, tok)


def _inproj_bwd_x(du, w, x, nw, scale, dxn, tok):
    S = x.shape[0]
    tm = min(256, S)

    def body(du_ref, w_ref, x_ref, nw_ref, sc_ref, dxn_ref, tok_ref, dx_ref, red_ref):
        del tok_ref

        @pl.when(pl.program_id(0) == 0)
        def _():
            red_ref[...] = jnp.zeros_like(red_ref)

        dh = _mm_nt(du_ref[...], w_ref[...])
        xv = x_ref[...]
        inv = lax.rsqrt(jnp.mean(xv * xv, axis=-1, keepdims=True) + EPS)
        xhat = xv * inv
        nwv = nw_ref[...]
        g1 = 1.0 + sc_ref[...]
        dxhat = dh * nwv * g1
        dx = inv * (dxhat - xhat * jnp.mean(dxhat * xhat, axis=-1, keepdims=True))
        dx_ref[...] = dxn_ref[...] + dx
        red_ref[0:1, :] += jnp.sum(dh, axis=0, keepdims=True)
        red_ref[1:2, :] += jnp.sum(dh * xhat * nwv, axis=0, keepdims=True)
        red_ref[2:3, :] += jnp.sum(dh * xhat * g1, axis=0, keepdims=True)

    row = pl.BlockSpec((tm, D_MODEL), lambda i: (i, 0))
    return pl.pallas_call(
        body, name="inproj_bwd_x", grid=(S // tm,),
        in_specs=[pl.BlockSpec((tm, N_PAD), lambda i: (i, 0)), _full((D_MODEL, N_PAD)), row, _spec(nw),
                  _spec(scale), row, pl.BlockSpec(memory_space=pl.ANY)],
        out_specs=[row, _full((8, D_MODEL))],
        out_shape=[SDS((S, D_MODEL), F32), SDS((8, D_MODEL), F32)],
        compiler_params=_cp(("arbitrary",)),
    )(du, w, x, _arr(nw), _arr(scale), dxn, tok)


def _inproj_bwd_w(h, du, tok):
    S = h.shape[0]
    tn = 640

    def body(h_ref, du_ref, tok_ref, gw_ref):
        del tok_ref
        gw_ref[...] = _mm_tn(h_ref[...], _bf(du_ref[...]))

    return pl.pallas_call(
        body, name="inproj_bwd_w", grid=(N_PAD // tn,),
        in_specs=[_full((S, D_MODEL)), pl.BlockSpec((S, tn), lambda j: (0, j)), pl.BlockSpec(memory_space=pl.ANY)],
        out_specs=pl.BlockSpec((D_MODEL, tn), lambda j: (0, j)),
        out_shape=SDS((D_MODEL, N_PAD), F32),
        compiler_params=_cp(("parallel",)),
    )(h, du, tok)


def _scan_block(a, b, rows):
    d = 1
    while d < a.shape[0]:
        a_s = _shift_down(a, d, rows, 1.0)
        b_s = _shift_down(b, d, rows, 0.0)
        b = a * b_s + b
        a = a * a_s
        d *= 2
    return a, b


def _rscan_block(c, g, rows):
    d = 1
    while d < c.shape[0]:
        c_s = _shift_up(c, d, rows, 1.0)
        g_s = _shift_up(g, d, rows, 0.0)
        g = g + c * g_s
        c = c * c_s
        d *= 2
    return c, g


LRU_BLOCK = 64


def _lru_gates(xa, wa_ref, ba_ref, wx_ref, bx_ref, lam_ref):
    sp = _softplus(-lam_ref[...])
    xb = _bf(xa)
    r = _sigmoid(_mm(xb, wa_ref[...]) + ba_ref[...])
    ig = _sigmoid(_mm(xb, wx_ref[...]) + bx_ref[...])
    la = -LRU_C * r * sp
    a = jnp.exp(la)
    mult = jnp.sqrt(-_expm1(2.0 * la))
    return sp, r, ig, la, a, mult


def _lru_specs(S, l):
    t128 = pl.BlockSpec((None, 1, LANE), lambda t: (l, 0, t))
    gate = pl.BlockSpec((None, None, LANE, LANE), lambda t: (l, t, 0, 0))
    return [pl.BlockSpec((S, 2 * LANE), lambda t: (0, OFF_LRU // (2 * LANE) + t)),
            pl.BlockSpec((None, 4, LANE), lambda t: (l, 0, t)), t128, gate, t128, gate, t128, t128]


def _lru_fwd(l, u, cw, cb, wa, ba, wx, bx, lam, ycat):
    S = u.shape[0]
    tb = min(LRU_BLOCK, S)

    def body(u_ref, cw_ref, cb_ref, wa_ref, ba_ref, wx_ref, bx_ref, lam_ref, ycat_in, ycat_ref, h_ref, a_scr, b_scr):
        del ycat_in
        rows = _iota((S, LANE), 0)
        xa = _conv_fwd(_f(u_ref[:, 0:LANE]), cw_ref, cb_ref, rows)
        _, _, ig, _, a, mult = _lru_gates(xa, wa_ref, ba_ref, wx_ref, bx_ref, lam_ref)
        a_scr[...] = a
        b_scr[...] = mult * (ig * xa)
        rows_b = _iota((tb, LANE), 0)

        def blk(j, hprev):
            sl = pl.ds(pl.multiple_of(j * tb, tb), tb)
            acum, hloc = _scan_block(a_scr[sl, :], b_scr[sl, :], rows_b)
            hf = hloc + acum * hprev
            h_ref[sl, :] = hf
            return _last_row(hf, rows_b)

        lax.fori_loop(0, S // tb, blk, jnp.zeros((1, LANE), F32))
        ycat_ref[...] = _bf(h_ref[...] * _silu(_f(u_ref[:, LANE:2 * LANE])))

    col = pl.BlockSpec((S, LANE), lambda t: (0, t))
    return pl.pallas_call(
        body, name="lru_fwd", grid=(LRU_W // LANE,),
        in_specs=_lru_specs(S, l) + [pl.BlockSpec(memory_space=pl.ANY)],
        out_specs=[col, col],
        out_shape=[SDS((S, D_INNER), BF16), SDS((S,LRU_W), F32)],
        scratch_shapes=[pltpu.VMEM((S, LANE), F32), pltpu.VMEM((S, LANE), F32)],
        input_output_aliases={8: 0},
        compiler_params=_cp(("parallel",)),
    )(u, cw, cb, wa, ba, wx, bx, lam, ycat)


def _lru_bwd(l, u, cw, cb, wa, ba, wx, bx, lam, h_lru, dycat, du):
    S = u.shape[0]
    tb = min(LRU_BLOCK, S)

    def body(u_ref, cw_ref, cb_ref, wa_ref, ba_ref, wx_ref, bx_ref, lam_ref, h_ref, dy_ref, du_in,
             du_ref, red_ref, gwa_ref, gwx_ref, c_scr, g_scr, l_scr):
        del du_in
        rows = _iota((S, LANE), 0)
        ax = _f(u_ref[:, 0:LANE])
        ag = _f(u_ref[:, LANE:2 * LANE])
        xa = _conv_fwd(ax, cw_ref, cb_ref, rows)
        sp, r, ig, la, a, mult = _lru_gates(xa, wa_ref, ba_ref, wx_ref, bx_ref, lam_ref)
        h = h_ref[...]
        dy = _f(dy_ref[...])
        du_ref[:, LANE:2 * LANE] = _bf(dy * h * _dsilu(ag))
        c_scr[...] = _shift_up(a, 1, rows, 0.0)
        g_scr[...] = dy * _silu(ag)
        rows_b = _iota((tb, LANE), 0)
        nb = S // tb

        def blk(jj, lnext):
            j = nb - 1 - jj
            sl = pl.ds(pl.multiple_of(j * tb, tb), tb)
            ccum, lloc = _rscan_block(c_scr[sl, :], g_scr[sl, :], rows_b)
            lam_t = lloc + ccum * lnext
            l_scr[sl, :] = lam_t
            return jnp.sum(jnp.where(rows_b == 0, lam_t, 0.0), axis=0, keepdims=True)

        lax.fori_loop(0, nb, blk, jnp.zeros((1, LANE), F32))
        db = l_scr[...]
        da = db * _shift_down(h, 1, rows)
        dmult = db * ig * xa
        dig = db * mult * xa
        dxa = db * mult * ig
        dla = da * a - dmult * (a * a) / mult
        dr = -LRU_C * sp * dla
        dsp = jnp.sum(-LRU_C * r * dla, axis=0, keepdims=True)
        dlam = -dsp * _sigmoid(-lam_ref[...])
        dzr = dr * r * (1.0 - r)
        dzi = dig * ig * (1.0 - ig)
        dzr_b, dzi_b, xa_b = _bf(dzr), _bf(dzi), _bf(xa)
        dxa = dxa + _mm_nt(dzr_b, wa_ref[...]) + _mm_nt(dzi_b, wx_ref[...])
        gwa_ref[...] = _mm_tn(xa_b, dzr_b)
        gwx_ref[...] = _mm_tn(xa_b, dzi_b)
        dax, dws, dcb = _conv_bwd(ax, dxa, cw_ref, rows)
        du_ref[:, 0:LANE] = _bf(dax)
        parts = dws + [dcb, jnp.sum(dzr, axis=0, keepdims=True), jnp.sum(dzi, axis=0, keepdims=True), dlam]
        for n, p in enumerate(parts):
            red_ref[pl.ds(n, 1), :] = p

    col = pl.BlockSpec((S, LANE), lambda t: (0, t))
    gw = pl.BlockSpec((None, LANE, LANE), lambda t: (t, 0, 0))
    return pl.pallas_call(
        body, name="lru_bwd", grid=(LRU_W // LANE,),
        in_specs=_lru_specs(S, l) + [col, col, pl.BlockSpec(memory_space=pl.ANY)],
        out_specs=[pl.BlockSpec((S, 2 * LANE), lambda t: (0, OFF_LRU // (2 * LANE) + t)),
                   pl.BlockSpec((8, LANE), lambda t: (0, t)), gw, gw],
        out_shape=[SDS((S, N_PAD), BF16), SDS((8, LRU_W), F32), SDS((4, LANE, LANE), F32), SDS((4, LANE, LANE), F32)],
        scratch_shapes=[pltpu.VMEM((S, LANE), F32)] * 3,
        input_output_aliases={10: 0},
        compiler_params=_cp(("parallel",)),
    )(u, cw, cb, wa, ba, wx, bx, lam, h_lru, dycat, du)


HG_LEVELS = 6


def _hg_consts():
    C = HG_CHUNK
    t = np.arange(C)[:, None]
    r = np.arange(C)[None, :]
    mats = []
    for l in range(HG_LEVELS):
        b = 1 << l
        upper = (t % (2 * b)) >= b
        anchor = (t // (2 * b)) * 2 * b + b - 1
        mats.append((upper & (r > anchor) & (r <= t)) | ((~upper) & (r > t) & (r <= anchor)))
    mats.append(r <= t)
    mats.append(r > t)
    return np.concatenate(mats, 0).astype(np.float32)


def _hg_factors(hf, lb, mall):
    s = _sigmoid(hf)
    f = lb + (1.0 - lb) * s
    lf = jnp.log(f)
    k = (1.0 - lb) * _sigmoid(-hf)
    e = jnp.exp(_sel_l(mall, lf))
    C = HG_CHUNK
    rows = _iota((C, HG_W), 0)
    eq, ek = [], []
    for l in range(HG_LEVELS):
        el = e[l * C:(l + 1) * C]
        eq.append(jnp.where((lax.shift_right_logical(rows, l) & 1) == 1, el, 0.0))
        ek.append(el - eq[l])
    ecum = e[HG_LEVELS * C:(HG_LEVELS + 1) * C]
    erem = e[(HG_LEVELS + 1) * C:(HG_LEVELS + 2) * C]
    return s, f, k, eq, ek, ecum, erem


def _hg_masks():
    C = HG_CHUNK
    ri, ci = _iota((C, C), 0), _iota((C, C), 1)
    rr = _iota((C, LANE), 0)
    gm = [(lax.shift_right_logical(ri, l + 1) == lax.shift_right_logical(ci, l + 1)).astype(F32)
          for l in range(HG_LEVELS)]
    up = [(lax.shift_right_logical(rr, l) & 1) == 1 for l in range(HG_LEVELS)]
    eye = (ri == ci).astype(F32)
    return gm, up, eye, rr


def _hg_scores(qh, kh, eq, ek, sl, gm, up, eye):
    del up
    qs, ks, qb, kb = [], [], [], []
    p = _mm_nt(_bf(qh), _bf(kh)) * eye
    for l in range(HG_LEVELS):
        qs.append(qh * eq[l][:, sl])
        ks.append(kh * ek[l][:, sl])
        qb.append(_bf(qs[l]))
        kb.append(_bf(ks[l]))
        p = p + _mm_nt(qb[l], kb[l]) * gm[l]
    return p, qs, ks, qb, kb


HG_SUB = 8


def _hg_fwd(u, lb, nw, mall, ycat):
    S = u.shape[0]
    C = HG_CHUNK
    n = S // C
    rows = HG_SUB * C

    def body(u_ref, lb_ref, nw_ref, mall_ref, ycat_in, ycat_ref, o_ref, st_ref, st):
        del ycat_in

        @pl.when(pl.program_id(0) == 0)
        def _():
            st[...] = jnp.zeros_like(st)

        gm, up, eye, rr = _hg_masks()
        for sub in range(HG_SUB):
            r = slice(sub * C, (sub + 1) * C)
            q = _silu(_f(u_ref[r, 0:512]))
            v = u_ref[r, 1024:1536]
            _, _, k, eq, ek, ecum, erem = _hg_factors(_f(u_ref[r, 512:1024]), lb_ref[...], mall_ref[...])
            for h in range(HG_HEADS):
                sl = slice(h * LANE, (h + 1) * LANE)
                qh, kh, vh = q[:, sl], k[:, sl], _bf(v[:, sl])
                p = _hg_scores(qh, kh, eq, ek, sl, gm, up, eye)[0]
                sth = st[h]
                st_ref[sub, h] = sth
                o_ref[r, sl] = _mm(_bf(p), vh) + _mm_nt(_bf(qh * ecum[:, sl]), _bf(sth))
                st[h] = sth * _last_row(ecum[:, sl], rr) + _mm_tn(vh, _bf(kh * erem[:, sl]))
            o = o_ref[r, :]
            inv = lax.rsqrt(jnp.mean(o * o, axis=-1, keepdims=True) + EPS)
            ycat_ref[r, :] = _bf((o * inv) * nw_ref[...] * _silu(_f(u_ref[r, 1536:2048])))

    return pl.pallas_call(
        body, name="hg_fwd", grid=(n // HG_SUB,),
        in_specs=[pl.BlockSpec((rows, 2048), lambda i: (i, 0)), _spec(lb), _spec(nw), _full(mall.shape),
                  pl.BlockSpec(memory_space=pl.ANY)],
        out_specs=[pl.BlockSpec((rows, HG_W), lambda i: (i, 1)), pl.BlockSpec((rows, HG_W), lambda i: (i, 0)),
                   pl.BlockSpec((HG_SUB, HG_HEADS, LANE, LANE), lambda i: (i, 0, 0, 0))],
        out_shape=[SDS((S, D_INNER), BF16), SDS((S,HG_W), F32), SDS((n, HG_HEADS, LANE, LANE), F32)],
        scratch_shapes=[pltpu.VMEM((HG_HEADS, LANE, LANE), F32)],
        input_output_aliases={4: 0},
        compiler_params=_cp(("arbitrary",)),
    )(u, _arr(lb), _arr(nw), mall, ycat)


def _hg_bwd(u, lb, nw, mall, mall_t, o_b, states, dycat, du):
    S = u.shape[0]
    C = HG_CHUNK
    n = S // C
    nb = n // HG_SUB
    rows = HG_SUB * C
    L2 = HG_LEVELS

    def body(u_ref, lb_ref, nw_ref, mall_ref, mallt_ref, o_ref, st_ref, dy_ref, du_in, du_ref, red_ref,
             dst, dlast_s, dq_s, dk_s, dex):
        del du_in

        @pl.when(pl.program_id(0) == 0)
        def _():
            dst[...] = jnp.zeros_like(dst)
            red_ref[...] = jnp.zeros_like(red_ref)

        lb = lb_ref[...]
        nwv = nw_ref[...]
        gm, up, eye, rr = _hg_masks()
        for sub in reversed(range(HG_SUB)):
            r = slice(sub * C, (sub + 1) * C)
            hq, hf, hg = _f(u_ref[r, 0:512]), _f(u_ref[r, 512:1024]), _f(u_ref[r, 1536:2048])
            q = _silu(hq)
            v = u_ref[r, 1024:1536]
            s, f, k, eq, ek, ecum, erem = _hg_factors(hf, lb, mall_ref[...])
            o = o_ref[r, :]
            dy = _f(dy_ref[r, :])
            inv = lax.rsqrt(jnp.mean(o * o, axis=-1, keepdims=True) + EPS)
            ohat = o * inv
            du_ref[r, 1536:2048] = _bf(dy * ohat * nwv * _dsilu(hg))
            dn = dy * _silu(hg)
            red_ref[0:1, :] += jnp.sum(dn * ohat, axis=0, keepdims=True)
            dohat = dn * nwv
            do = inv * (dohat - ohat * jnp.mean(dohat * ohat, axis=-1, keepdims=True))
            for h in range(HG_HEADS):
                sl = slice(h * LANE, (h + 1) * LANE)
                qh, kh, vh, doh = q[:, sl], k[:, sl], _bf(v[:, sl]), _bf(do[:, sl])
                p, qs, ks, qb, kb = _hg_scores(qh, kh, eq, ek, sl, gm, up, eye)
                st_f = st_ref[sub, h]
                sth = _bf(st_f)
                dsth = dst[h]
                dsth_b = _bf(dsth)
                qt = qh * ecum[:, sl]
                kt = kh * erem[:, sl]
                elast = _last_row(ecum[:, sl], rr)
                dp = _mm_nt(doh, vh)
                du_ref[r, 1024 + h * LANE:1024 + (h + 1) * LANE] = _bf(_mm_tn(_bf(p), doh) + _mm_nt(_bf(kt), dsth_b))
                dpe = _bf(dp * eye)
                dqt = _mm(doh, sth)
                dkt = _mm(vh, dsth_b)
                dq = dqt * ecum[:, sl] + _mm(dpe, _bf(kh))
                dk = dkt * erem[:, sl] + _mm_tn(dpe, _bf(qh))
                dex[sub, L2 * C:(L2 + 1) * C, sl] = dqt * qt
                dex[sub, (L2 + 1) * C:(L2 + 2) * C, sl] = dkt * kt
                for l in range(HG_LEVELS):
                    dpl = _bf(dp * gm[l])
                    dql = _mm(dpl, kb[l])
                    dkl = _mm_tn(dpl, qb[l])
                    dq = dq + dql * eq[l][:, sl]
                    dk = dk + dkl * ek[l][:, sl]
                    dex[sub, l * C:(l + 1) * C, sl] = dql * qs[l] + dkl * ks[l]
                dlast_s[sub, :, sl] = jnp.sum(dsth * st_f, axis=0, keepdims=True) * elast
                dst[h] = dsth * elast + _mm_tn(doh, _bf(qt))
                dq_s[sub, :, sl] = dq
                dk_s[sub, :, sl] = dk
            dq = dq_s[sub]
            dk = dk_s[sub]
            dlf = _sel_l2(mallt_ref[...], dex[sub]) + dlast_s[sub]
            du_ref[r, 0:512] = _bf(dq * _dsilu(hq))
            t = (1.0 - s) * (dlf / f - dk)
            du_ref[r, 512:1024] = _bf((1.0 - lb) * s * t)
            red_ref[1:2, :] += jnp.sum(t, axis=0, keepdims=True)

    rev = lambda i: (nb - 1 - i, 0)
    return pl.pallas_call(
        body, name="hg_bwd", grid=(nb,),
        in_specs=[pl.BlockSpec((rows, 2048), rev), _spec(lb), _spec(nw), _full(mall.shape), _full(mall_t.shape),
                  pl.BlockSpec((rows, HG_W), rev),
                  pl.BlockSpec((HG_SUB, HG_HEADS, LANE, LANE), lambda i: (nb - 1 - i, 0, 0, 0)),
                  pl.BlockSpec((rows, HG_W), lambda i: (nb - 1 - i, 1)), pl.BlockSpec(memory_space=pl.ANY)],
        out_specs=[pl.BlockSpec((rows, 2048), rev), pl.BlockSpec((8, HG_W), lambda i: (0, 0))],
        out_shape=[SDS((S, N_PAD), BF16), SDS((8, HG_W), F32)],
        scratch_shapes=[pltpu.VMEM((HG_HEADS, LANE, LANE), F32), pltpu.VMEM((HG_SUB, 1, HG_W), F32),
                        pltpu.VMEM((HG_SUB, C, HG_W), F32), pltpu.VMEM((HG_SUB, C, HG_W), F32),
                        pltpu.VMEM((HG_SUB, (L2 + 2) * C, HG_W), F32)],
        input_output_aliases={8: 0},
        compiler_params=_cp(("arbitrary",)),
    )(u, _arr(lb), _arr(nw), mall, mall_t, o_b, states, dycat, du)


def _ssdconv_fwd(l, u, cw, cb):
    S = u.shape[0]

    def body(u_ref, cw_ref, cb_ref, out_ref):
        rows = _iota((S, LANE), 0)
        out_ref[...] = _silu(_conv_fwd(_f(u_ref[...]), cw_ref, cb_ref, rows))

    return pl.pallas_call(
        body, name="ssdconv_fwd", grid=(SSD_CONV // LANE,),
        in_specs=[pl.BlockSpec((S, LANE), lambda t: (0, OFF_XBC // LANE + t)),
                  pl.BlockSpec((None, 4, LANE), lambda t: (l, 0, t)), pl.BlockSpec((None, 1, LANE), lambda t: (l, 0, t))],
        out_specs=pl.BlockSpec((S, LANE), lambda t: (0, t)),
        out_shape=SDS((S, SSD_CONV), F32),
        compiler_params=_cp(("parallel",)),
    )(u, cw, cb)


def _ssdconv_bwd(l, u, cw, cb, dxbc, du):
    S = u.shape[0]

    def body(u_ref, cw_ref, cb_ref, d_ref, du_in, du_ref, red_ref):
        del du_in
        rows = _iota((S, LANE), 0)
        x = _f(u_ref[...])
        dco = d_ref[...] * _dsilu(_conv_fwd(x, cw_ref, cb_ref, rows))
        dx, dws, dcb = _conv_bwd(x, dco, cw_ref, rows)
        du_ref[...] = _bf(dx)
        for n, p in enumerate(dws + [dcb]):
            red_ref[pl.ds(n, 1), :] = p
        red_ref[pl.ds(5, 3), :] = jnp.zeros((3, LANE), F32)

    ucol = pl.BlockSpec((S, LANE), lambda t: (0, OFF_XBC // LANE + t))
    return pl.pallas_call(
        body, name="ssdconv_bwd", grid=(SSD_CONV // LANE,),
        in_specs=[ucol, pl.BlockSpec((None, 4, LANE), lambda t: (l, 0, t)),
                  pl.BlockSpec((None, 1, LANE), lambda t: (l, 0, t)),
                  pl.BlockSpec((S, LANE), lambda t: (0, t)), pl.BlockSpec(memory_space=pl.ANY)],
        out_specs=[ucol, pl.BlockSpec((8, LANE), lambda t: (0, t))],
        out_shape=[SDS((S, N_PAD), BF16), SDS((8, SSD_CONV), F32)],
        input_output_aliases={4: 0},
        compiler_params=_cp(("parallel",)),
    )(u, cw, cb, dxbc, du)


SSD_SUB = 4


def _ssd_consts():
    e64 = np.zeros((LANE, SSD_W), np.float32)
    for h in range(SSD_HEADS):
        e64[h, h * SSD_P:(h + 1) * SSD_P] = 1.0
    T = SSD_CHUNK
    tril = (np.arange(T)[None, :] <= np.arange(T)[:, None]).astype(np.float32)
    return e64, tril, tril.T.copy()


def _ssd_common(zdt, bias_ref, alog_ref, tril, e64, cum_ref, cumt_ref):
    T = SSD_CHUNK
    lane = _iota((1, LANE), 1)
    a_neg = jnp.where(lane < SSD_HEADS, -jnp.exp(alog_ref[...]), 0.0)
    dtpre = zdt[:, SSD_W:SSD_W + LANE] + bias_ref[...]
    dt = _softplus(dtpre)
    cum = _sel_l(tril, dt * a_neg)
    cum_ref[...] = cum
    cumt_ref[...] = cum.T
    cum_x = _sel_r(cum, e64)
    last_x = _last_row(cum_x, _iota((T, SSD_W), 0))
    ecum_x = jnp.exp(cum_x)
    erem_x = jnp.exp(last_x - cum_x)
    elast_x = jnp.exp(last_x)
    dt_x = _sel_r(dt, e64)
    return a_neg, dtpre, dt, ecum_x, erem_x, elast_x, dt_x


def _ssd_decay(cum_ref, cumt_ref, h, causal):
    T = SSD_CHUNK
    diff = jnp.broadcast_to(cum_ref[:, pl.ds(h, 1)], (T, T)) - cumt_ref[pl.ds(h, 1), :]
    return jnp.exp(jnp.where(causal, diff, NEG))


def _group_norm_fwd(y1, nwv):
    outs, invs = [], []
    for g in range(2):
        seg = y1[:, g * 512:(g + 1) * 512]
        inv = lax.rsqrt(jnp.mean(seg * seg, axis=-1, keepdims=True) + EPS)
        outs.append(seg * inv * nwv[:, g * 512:(g + 1) * 512])
        invs.append(inv)
    return outs, invs


def _ssd_fwd(u, xbc, bias, alog, dskip_x, nw, consts, ycat):
    S = u.shape[0]
    T = SSD_CHUNK
    n = S // T
    rows = SSD_SUB * T
    e64, tril, _ = consts

    def body(u_ref, xbc_ref, bias_ref, alog_ref, dx_ref, nw_ref, e64_ref, tril_ref, ycat_in,
             ycat_ref, y_ref, st_ref, st, cumt, cum_e):
        del ycat_in

        @pl.when(pl.program_id(0) == 0)
        def _():
            st[...] = jnp.zeros_like(st)

        causal = _iota((T, T), 0) >= _iota((T, T), 1)
        lo = _iota((T, LANE), 1) < SSD_P
        for sub in range(SSD_SUB):
            r = slice(sub * T, (sub + 1) * T)
            zdt = _f(u_ref[r, :])
            z = zdt[:, 0:SSD_W]
            xs = xbc_ref[r, 0:SSD_W]
            cum_r, cumt_r = cum_e.at[sub], cumt.at[sub]
            _, _, _, ecum_x, erem_x, elast_x, dt_x = _ssd_common(
                zdt, bias_ref, alog_ref, tril_ref[...], e64_ref[...], cum_r, cumt_r)
            xdt = xs * dt_x
            xrem = xdt * erem_x
            st_ref[sub] = st[...]
            for g in range(2):
                gs = slice(g * 512, (g + 1) * 512)
                bg = _bf(xbc_ref[r, SSD_W + g * LANE:SSD_W + (g + 1) * LANE])
                cg = _bf(xbc_ref[r, SSD_W + 256 + g * LANE:SSD_W + 256 + (g + 1) * LANE])
                cb = _mm_nt(cg, bg)
                yin = _mm(cg, _bf(st[:, gs])) * ecum_x[:, gs]
                for j in range(4):
                    h0 = 8 * g + 2 * j
                    cs = slice(h0 * SSD_P, (h0 + 2) * SSD_P)
                    xp = xdt[:, cs]
                    s0 = _bf(cb * _ssd_decay(cum_r, cumt_r, h0, causal))
                    s1 = _bf(cb * _ssd_decay(cum_r, cumt_r, h0 + 1, causal))
                    y_ref[r, cs] = (_mm(s0, _bf(jnp.where(lo, xp, 0.0))) + _mm(s1, _bf(jnp.where(lo, 0.0, xp)))
                                    + yin[:, j * LANE:(j + 1) * LANE])
                st[:, gs] = st[:, gs] * elast_x[:, gs] + _mm_tn(bg, _bf(xrem[:, gs]))
            y1 = (y_ref[r, :] + dx_ref[...] * xs) * _silu(z)
            outs, _ = _group_norm_fwd(y1, nw_ref[...])
            for g in range(2):
                ycat_ref[r, g * 512:(g + 1) * 512] = _bf(outs[g])

    return pl.pallas_call(
        body, name="ssd_fwd", grid=(n // SSD_SUB,),
        in_specs=[pl.BlockSpec((rows, SSD_W + LANE), lambda i: (i, OFF_Z // (SSD_W + LANE))),
                  pl.BlockSpec((rows, SSD_CONV), lambda i: (i, 0)), _spec(bias), _spec(alog), _spec(dskip_x), _spec(nw),
                  _full(e64.shape), _full(tril.shape), pl.BlockSpec(memory_space=pl.ANY)],
        out_specs=[pl.BlockSpec((rows, SSD_W), lambda i: (i, 1)), pl.BlockSpec((rows, SSD_W), lambda i: (i, 0)),
                   pl.BlockSpec((SSD_SUB, SSD_N, SSD_W), lambda i: (i, 0, 0))],
        out_shape=[SDS((S, D_INNER), BF16), SDS((S,SSD_W), F32), SDS((n, SSD_N, SSD_W), F32)],
        scratch_shapes=[pltpu.VMEM((SSD_N, SSD_W), F32), pltpu.VMEM((SSD_SUB, LANE, T), F32),
                        pltpu.VMEM((SSD_SUB, T, LANE), F32)],
        input_output_aliases={8: 0},
        compiler_params=_cp(("arbitrary",)),
    )(u, xbc, _arr(bias), _arr(alog), _arr(dskip_x), _arr(nw), _bfc(e64), _bfc(tril), ycat)


def _ssd_bwd(u, xbc, bias, alog, dskip_x, nw, consts, y_ssd, states, dycat, du, tok):
    S = u.shape[0]
    T = SSD_CHUNK
    n = S // T
    e64, tril, triu = consts
    e64t = np.ascontiguousarray(e64.T)

    def chunk(u_ref, xbc_ref, bias_ref, alog_ref, dx_ref, nw_ref, e64_ref, e64t_ref, tril_ref, triu_ref,
              y_ref, st_ref, dy_ref, du_ref, dxbc_ref, red_ref, dst, dl_s, cumt, dxdt_s, dy0_s, gb_s, gc_s, cum_e, cs_s):
        zdt = _f(u_ref[...])
        z = zdt[:, 0:SSD_W]
        xs = xbc_ref[:, 0:SSD_W]
        a_neg, dtpre, dt, ecum_x, erem_x, elast_x, dt_x = _ssd_common(
            zdt, bias_ref, alog_ref, tril_ref[...], e64_ref[...], cum_e, cumt)
        causal = _iota((T, T), 0) >= _iota((T, T), 1)
        lo = _iota((T, LANE), 1) < SSD_P
        xdt = xs * dt_x
        xrem = xdt * erem_x
        y = y_ref[...]
        dxv = dx_ref[...]
        nwv = nw_ref[...]
        sz = _silu(z)
        y0 = y + dxv * xs
        y1 = y0 * sz
        for g in range(2):
            gs = slice(g * 512, (g + 1) * 512)
            seg = y1[:, gs]
            inv = lax.rsqrt(jnp.mean(seg * seg, axis=-1, keepdims=True) + EPS)
            shat = seg * inv
            dyg = _f(dy_ref[:, gs])
            red_ref[0:1, gs] += jnp.sum(dyg * shat, axis=0, keepdims=True)
            dsh = dyg * nwv[:, gs]
            dy1g = inv * (dsh - shat * jnp.mean(dsh * shat, axis=-1, keepdims=True))
            du_ref[:, gs] = _bf(dy1g * y0[:, gs] * _dsilu(z[:, gs]))
            dy0_s[:, gs] = dy1g * sz[:, gs]
        dy0 = dy0_s[...]
        red_ref[1:2, :] += jnp.sum(dy0 * xs, axis=0, keepdims=True)
        dyin = dy0 * ecum_x
        lane = _iota((T, LANE), 1)
        dcum = jnp.zeros((T, LANE), F32)

        def decay_grad(h, gm):
            cs_s[pl.ds(h, 1), :] = jnp.sum(gm, axis=0, keepdims=True)
            return jnp.where(lane == h, jnp.sum(gm, axis=1, keepdims=True), 0.0)

        for g in range(2):
            gs = slice(g * 512, (g + 1) * 512)
            bg = _bf(xbc_ref[:, SSD_W + g * LANE:SSD_W + (g + 1) * LANE])
            cg = _bf(xbc_ref[:, SSD_W + 256 + g * LANE:SSD_W + 256 + (g + 1) * LANE])
            cb = _mm_nt(cg, bg)
            dst_f, st_f = dst[:, gs], st_ref[:, gs]
            dstg = _bf(dst_f)
            stg = _bf(st_f)
            dyin_g = _bf(dyin[:, gs])
            xrem_g = _bf(xrem[:, gs])
            dcb = jnp.zeros((T, T), F32)
            dxr = _mm(bg, dstg)
            dxdt_s[:, gs] = dxr * erem_x[:, gs]
            gc_s[:, gs] = dxr * xrem[:, gs]
            gb_s[:, gs] = dyin[:, gs] * _mm(cg, stg)
            dl_s[:, gs] = jnp.sum(dst_f * st_f, axis=0, keepdims=True) * elast_x[:, gs]
            for j in range(4):
                h0 = 8 * g + 2 * j
                cs = slice(h0 * SSD_P, (h0 + 2) * SSD_P)
                xp = xdt[:, cs]
                dyp = dy0[:, cs]
                x_lo, x_hi = _bf(jnp.where(lo, xp, 0.0)), _bf(jnp.where(lo, 0.0, xp))
                d_lo, d_hi = _bf(jnp.where(lo, dyp, 0.0)), _bf(jnp.where(lo, 0.0, dyp))
                l0 = _ssd_decay(cum_e, cumt, h0, causal)
                l1 = _ssd_decay(cum_e, cumt, h0 + 1, causal)
                s0 = cb * l0
                s1 = cb * l1
                ds0 = _mm_nt(d_lo, x_lo)
                ds1 = _mm_nt(d_hi, x_hi)
                dcb = dcb + ds0 * l0 + ds1 * l1
                dxdt_s[:, cs] += _mm_tn(_bf(s0), d_lo) + _mm_tn(_bf(s1), d_hi)
                dcum = dcum + decay_grad(h0, ds0 * s0) + decay_grad(h0 + 1, ds1 * s1)
            dcb_b = _bf(dcb)
            dxbc_ref[:, SSD_W + g * LANE:SSD_W + (g + 1) * LANE] = _mm_tn(dcb_b, cg) + _mm_nt(xrem_g, dstg)
            dxbc_ref[:, SSD_W + 256 + g * LANE:SSD_W + 256 + (g + 1) * LANE] = _mm(dcb_b, bg) + _mm_nt(dyin_g, stg)
            dst[:, gs] = dst_f * elast_x[:, gs] + _mm_tn(cg, dyin_g)
        dxdt = dxdt_s[...]
        dxbc_ref[:, 0:SSD_W] = dxdt * dt_x + dy0 * dxv
        e64t = e64t_ref[...]
        gc = gc_s[...]
        dlast_x = jnp.sum(gc, axis=0, keepdims=True) + dl_s[...]
        dlast = jnp.max(_sel_r(jnp.broadcast_to(dlast_x, (8, SSD_W)), e64t), axis=0, keepdims=True)
        dcum = (dcum - cs_s[...].T + _sel_r(gb_s[...] - gc, e64t)
                + jnp.where(_iota((T, LANE), 0) == T - 1, dlast, 0.0))
        dda = _sel_l(triu_ref[...], dcum)
        ddt = dda * a_neg + _sel_r(dxdt * xs, e64t)
        ddtpre = ddt * _sigmoid(dtpre)
        du_ref[:, SSD_W:SSD_W + LANE] = _bf(jnp.where(lane < SSD_HEADS, ddtpre, 0.0))
        red_ref[2:3, 0:LANE] += jnp.sum(ddtpre, axis=0, keepdims=True)
        red_ref[3:4, 0:LANE] += jnp.sum(dda * dt, axis=0, keepdims=True)

    def body(u_ref, xbc_ref, bias_ref, alog_ref, dx_ref, nw_ref, e64_ref, e64t_ref, tril_ref, triu_ref,
             y_ref, st_ref, dy_ref, du_in, tok_ref, du_ref, dxbc_ref, red_ref, dst, *scratch):
        del du_in, tok_ref

        @pl.when(pl.program_id(0) == 0)
        def _():
            dst[...] = jnp.zeros_like(dst)
            red_ref[...] = jnp.zeros_like(red_ref)
            scratch[-1][...] = jnp.zeros_like(scratch[-1])

        for sub in reversed(range(SSD_SUB)):
            rs = pl.ds(sub * T, T)
            chunk(u_ref.at[rs], xbc_ref.at[rs], bias_ref, alog_ref, dx_ref, nw_ref, e64_ref, e64t_ref, tril_ref, triu_ref,
                  y_ref.at[rs], st_ref.at[sub], dy_ref.at[rs], du_ref.at[rs], dxbc_ref.at[rs], red_ref, dst,
                  *[s.at[sub] for s in scratch])

    nb = n // SSD_SUB
    rows = SSD_SUB * T
    rev = lambda i: (nb - 1 - i, 0)
    sub_scratch = [(1, SSD_W), (LANE, T)] + [(T, SSD_W)] * 4 + [(T, LANE), (LANE, T)]
    return pl.pallas_call(
        body, name="ssd_bwd", grid=(nb,),
        in_specs=[pl.BlockSpec((rows, SSD_W + LANE), lambda i: (nb - 1 - i, OFF_Z // (SSD_W + LANE))),
                  pl.BlockSpec((rows, SSD_CONV), rev), _spec(bias), _spec(alog), _spec(dskip_x), _spec(nw),
                  _full(e64.shape), _full(e64t.shape), _full(tril.shape), _full(triu.shape),
                  pl.BlockSpec((rows, SSD_W), rev), pl.BlockSpec((SSD_SUB, SSD_N, SSD_W), lambda i: (nb - 1 - i, 0, 0)),
                  pl.BlockSpec((rows, SSD_W), lambda i: (nb - 1 - i, 1)), pl.BlockSpec(memory_space=pl.ANY),
                  pl.BlockSpec(memory_space=pl.ANY)],
        out_specs=[pl.BlockSpec((rows, SSD_W + LANE), lambda i: (nb - 1 - i, OFF_Z // (SSD_W + LANE))),
                   pl.BlockSpec((rows, SSD_CONV), rev), pl.BlockSpec((8, SSD_W), lambda i: (0, 0))],
        out_shape=[SDS((S, N_PAD), BF16), SDS((S, SSD_CONV), F32), SDS((8, SSD_W), F32)],
        scratch_shapes=[pltpu.VMEM((SSD_N, SSD_W), F32)] + [pltpu.VMEM((SSD_SUB,) + s, F32) for s in sub_scratch],
        input_output_aliases={13: 0},
        compiler_params=_cp(("arbitrary",)),
    )(u, xbc, _arr(bias), _arr(alog), _arr(dskip_x), _arr(nw), _bfc(e64), _bfc(e64t), _bfc(tril), _bfc(triu), y_ssd,
      states, dycat, du, tok)


def _bfc(a):
    return jnp.asarray(a, BF16)


def _outproj_fwd(ycat, wo, x, gate, tok):
    S = x.shape[0]
    tm = min(512, S)

    def body(yc_ref, wo_ref, x_ref, g_ref, tok_ref, xn_ref, y_ref):
        del tok_ref
        y = _mm(_bf(yc_ref[...]), wo_ref[...])
        y_ref[...] = y
        xn_ref[...] = x_ref[...] + g_ref[...] * y

    row = pl.BlockSpec((tm, D_MODEL), lambda i: (i, 0))
    return pl.pallas_call(
        body, name="outproj_fwd", grid=(S // tm,),
        in_specs=[pl.BlockSpec((tm, D_INNER), lambda i: (i, 0)), _full((D_INNER, D_MODEL)), row, _spec(gate),
                  pl.BlockSpec(memory_space=pl.ANY)],
        out_specs=[row, row],
        out_shape=[SDS((S, D_MODEL), F32), SDS((S, D_MODEL), F32)],
        compiler_params=_cp(("parallel",)),
    )(ycat, wo, x, _arr(gate), tok)


def _outproj_bwd(dxn, y, gate, ycat, wo):
    S = dxn.shape[0]
    tm = min(512, S)

    def body(dx_ref, y_ref, g_ref, yc_ref, wo_ref, dyc_ref, gwo_ref, dg_ref, acc):
        @pl.when(pl.program_id(0) == 0)
        def _():
            acc[...] = jnp.zeros_like(acc)
            dg_ref[...] = jnp.zeros_like(dg_ref)

        dxv = dx_ref[...]
        dy = _bf(dxv * g_ref[...])
        dg_ref[0:1, :] += jnp.sum(dxv * y_ref[...], axis=0, keepdims=True)
        dyc_ref[...] = _mm_nt(dy, wo_ref[...])
        acc[...] += _mm_tn(_bf(yc_ref[...]), dy)

        @pl.when(pl.program_id(0) == pl.num_programs(0) - 1)
        def _():
            gwo_ref[...] = acc[...].astype(BF16)

    row = pl.BlockSpec((tm, D_MODEL), lambda i: (i, 0))
    wide = pl.BlockSpec((tm, D_INNER), lambda i: (i, 0))
    return pl.pallas_call(
        body, name="outproj_bwd", grid=(S // tm,),
        in_specs=[row, row, _spec(gate), wide, _full((D_INNER, D_MODEL))],
        out_specs=[wide, _full((D_INNER, D_MODEL)), _full((8, D_MODEL))],
        out_shape=[SDS((S, D_INNER), F32), SDS((D_INNER, D_MODEL), BF16), SDS((8, D_MODEL), F32)],
        scratch_shapes=[pltpu.VMEM((D_INNER, D_MODEL), F32)],
        compiler_params=_cp(("arbitrary",)),
    )(dxn, y, _arr(gate), ycat, wo)


def _loss_head(x, fw, target):
    S = x.shape[0]
    tm = min(512, S)

    def body(x_ref, fw_ref, t_ref, dx_ref, red_ref):
        @pl.when(pl.program_id(0) == 0)
        def _():
            red_ref[...] = jnp.zeros_like(red_ref)

        xv = x_ref[...]
        fwv = fw_ref[...]
        inv = lax.rsqrt(jnp.mean(xv * xv, axis=-1, keepdims=True) + EPS)
        xhat = xv * inv
        err = xhat * fwv - t_ref[...]
        col = jnp.sum(err * err, axis=0, keepdims=True)
        red_ref[1:2, :] += jnp.broadcast_to(jnp.sum(col, axis=1, keepdims=True) * (0.5 / D_MODEL), (1, D_MODEL))
        dy = err * (1.0 / D_MODEL)
        red_ref[0:1, :] += jnp.sum(dy * xhat, axis=0, keepdims=True)
        dxhat = dy * fwv
        dx_ref[...] = inv * (dxhat - xhat * jnp.mean(dxhat * xhat, axis=-1, keepdims=True))

    row = pl.BlockSpec((tm, D_MODEL), lambda i: (i, 0))
    return pl.pallas_call(
        body, name="loss_head", grid=(S // tm,),
        in_specs=[row, _vec(D_MODEL), row],
        out_specs=[row, _full((8, D_MODEL))],
        out_shape=[SDS((S, D_MODEL), F32), SDS((8, D_MODEL), F32)],
        compiler_params=_cp(("arbitrary",)),
    )(x, fw, target)


ADA_COLS = 3 * D_MODEL // N_DEV


def _ada_fwd(c_all, w_ada, b_cols):
    def body(c_ref, w_ref, b_ref, out_ref):
        out_ref[...] = _mm(_bf(_silu(c_ref[...])), _bf(w_ref[...])) + b_ref[...]

    return pl.pallas_call(
        body, name="ada_fwd", grid=(DEPTH,),
        in_specs=[_full((N_DEV, D_MODEL)), pl.BlockSpec((None, D_MODEL, ADA_COLS), lambda l: (l, 0, 0)),
                  pl.BlockSpec((None, 1, ADA_COLS), lambda l: (l, 0, 0))],
        out_specs=pl.BlockSpec((None, N_DEV, ADA_COLS), lambda l: (l, 0, 0)),
        out_shape=SDS((DEPTH, N_DEV, ADA_COLS), F32),
        compiler_params=_cp(("parallel",)),
    )(c_all, w_ada, b_cols)


def _ada_bwd(ct_pad, dmod_pad):
    def body(c_ref, d_ref, out_ref):
        out_ref[...] = _mm(_bf(_silu(c_ref[...])), _bf(d_ref[...]))

    return pl.pallas_call(
        body, name="ada_bwd", grid=(DEPTH,),
        in_specs=[_full((D_MODEL, LANE)), pl.BlockSpec((None, LANE, ADA_COLS), lambda l: (l, 0, 0))],
        out_specs=pl.BlockSpec((None, D_MODEL, ADA_COLS), lambda l: (l, 0, 0)),
        out_shape=SDS((DEPTH, D_MODEL, ADA_COLS), F32),
        compiler_params=_cp(("parallel",)),
    )(ct_pad, dmod_pad)


def _adamw(parts, w, m, v, name, own=None, layers=None, prev=None):
    n, L, R, C = parts.shape
    lo, hi = layers or (0, L)
    tr = R
    while tr * C * 4 > (1 << 20) and tr % 16 == 0:
        tr //= 2
    first = 1 if own is None else 2

    def body(*refs):
        p_ref = refs[0]
        w_ref, m_ref, v_ref = refs[first:first + 3]
        g_ref, d_ref, mo_ref, vo_ref = refs[-4:]

        def part(k):
            if own is None:
                return p_ref[k].astype(F32)
            me = 4 * lax.axis_index("x") + 2 * lax.axis_index("y") + lax.axis_index("c")
            return jnp.where(me == k, refs[1][...], p_ref[k]).astype(F32)

        g = part(0)
        for k in range(1, n):
            g = g + part(k)
        mn = ADAM_B1 * m_ref[...] + (1.0 - ADAM_B1) * g
        vn = ADAM_B2 * v_ref[...] + (1.0 - ADAM_B2) * (g * g)
        m_hat = mn / (1.0 - ADAM_B1 ** ADAM_STEP)
        v_hat = vn / (1.0 - ADAM_B2 ** ADAM_STEP)
        g_ref[...] = g
        d_ref[...] = -ADAM_LR * (m_hat / (jnp.sqrt(v_hat) + ADAM_EPS) + ADAM_WD * w_ref[...])
        mo_ref[...] = mn
        vo_ref[...] = vn

    blk = pl.BlockSpec((None, tr, C), lambda l, i: (lo + l, i, 0))
    own_blk = [] if own is None else [pl.BlockSpec((None, tr, C), lambda l, i: (l, i, 0))]
    n_blk = 3 if own is None else 4
    return pl.pallas_call(
        body, name=name, grid=(hi - lo, R // tr),
        in_specs=[pl.BlockSpec((n, None, tr, C), lambda l, i: (0, lo + l, i, 0))] + own_blk + [blk] * 3
        + ([] if prev is None else [ANY] * 4),
        out_specs=[blk] * 4,
        out_shape=[SDS((L, R, C), F32)] * 4,
        input_output_aliases={} if prev is None else {1 + n_blk + k: k for k in range(4)},
        compiler_params=_cp(("parallel", "parallel")),
    )(parts, *([] if own is None else [own]), w, m, v, *([] if prev is None else prev))


MESH = pl.DeviceIdType.MESH
ANY = pl.BlockSpec(memory_space=pl.ANY)


def _all_gather(v, name):
    def body(v_ref, out_ref, send_sems, recv_sems, local_sem):
        x, y, c = lax.axis_index("x"), lax.axis_index("y"), lax.axis_index("c")
        me, sibling = (x, y, c), (x, y, 1 - c)
        chips = [(1 - x, y), (x, 1 - y), (1 - x, 1 - y)]

        def slot(px, py, pc):
            return out_ref.at[4 * px + 2 * py + pc]

        def copy(k, block, to, src=None):
            return pltpu.make_async_remote_copy(
                src_ref=slot(*block) if src is None else src, dst_ref=slot(*block),
                send_sem=send_sems.at[k], recv_sem=recv_sems.at[k], device_id=to, device_id_type=MESH)

        mine = pltpu.make_async_copy(v_ref, slot(*me), local_sem)
        mine.start()
        first = [copy(0, me, sibling, src=v_ref)]
        first += [copy(1 + j, me, (*chip, c), src=v_ref) for j, chip in enumerate(chips)]
        for cp in first:
            cp.start()
        passed = [copy(4 + j, (*chip, c), sibling) for j, chip in enumerate(chips)]
        for j, chip in enumerate(chips):
            copy(1 + j, (*chip, c), me).wait_recv()
            passed[j].start()
        copy(0, sibling, me).wait_recv()
        for j, chip in enumerate(chips):
            copy(4 + j, (*chip, 1 - c), me).wait_recv()
        for cp in first + passed:
            cp.wait_send()
        mine.wait()

    return pl.pallas_call(
        body, name=name, in_specs=[ANY], out_specs=ANY,
        out_shape=SDS((N_DEV,) + v.shape, v.dtype),
        scratch_shapes=[pltpu.SemaphoreType.DMA((7,)), pltpu.SemaphoreType.DMA((7,)), pltpu.SemaphoreType.DMA],
    )(v)


def _all_to_all(v, name):
    def body(v_ref, out_ref, send_sems, recv_sems, local_sem):
        x, y, c = lax.axis_index("x"), lax.axis_index("y"), lax.axis_index("c")
        mine_idx = 4 * x + 2 * y + c
        mine = pltpu.make_async_copy(v_ref.at[mine_idx], out_ref.at[mine_idx], local_sem)
        mine.start()
        sends, recvs = [], []
        for k in range(1, N_DEV):
            px = 1 - x if k & 4 else x
            py = 1 - y if k & 2 else y
            pc = 1 - c if k & 1 else c
            peer_idx = 4 * px + 2 * py + pc
            sems = dict(send_sem=send_sems.at[k - 1], recv_sem=recv_sems.at[k - 1], device_id=(px, py, pc),
                        device_id_type=MESH)
            sends.append(pltpu.make_async_remote_copy(src_ref=v_ref.at[peer_idx], dst_ref=out_ref.at[mine_idx], **sems))
            recvs.append(pltpu.make_async_remote_copy(src_ref=v_ref.at[peer_idx], dst_ref=out_ref.at[peer_idx], **sems))
        for cp in sends:
            cp.start()
        for cp in recvs:
            cp.wait_recv()
        for cp in sends:
            cp.wait_send()
        mine.wait()

    return pl.pallas_call(
        body, name=name, in_specs=[ANY], out_specs=ANY,
        out_shape=SDS(v.shape, v.dtype),
        scratch_shapes=[pltpu.SemaphoreType.DMA((7,)), pltpu.SemaphoreType.DMA((7,)), pltpu.SemaphoreType.DMA],
    )(v)


HBM_SPEC = pl.BlockSpec(memory_space=pltpu.HBM)
SEM_SPEC = pl.BlockSpec(memory_space=pltpu.SEMAPHORE)
EFFECT = pltpu.SideEffectType.DATAFLOW_SIDE_EFFECTING


EXCHANGE_PEERS = {"gather": range(1, N_DEV), "scatter": range(1, N_DEV), "chip": (1, 2, 4, 6), "pass": (2, 4, 6)}


def _exchange_copies(srcs, lands, send_sems, recv_sems, mode, layer):
    x, y, c = lax.axis_index("x"), lax.axis_index("y"), lax.axis_index("c")
    me = 4 * x + 2 * y + c
    copies = []
    for a, (src, land) in enumerate(zip(srcs, lands)):
        for k in EXCHANGE_PEERS[mode]:
            px = 1 - x if k & 4 else x
            py = 1 - y if k & 2 else y
            pc = 1 - c if k & 1 else c
            peer = 4 * px + 2 * py + pc
            if mode == "scatter":
                s, d, to = src.at[peer], land.at[me, layer], (px, py, pc)
            elif mode == "pass":
                s, d, to = land.at[peer], land.at[peer], (x, y, 1 - c)
            else:
                s, d, to = src, land.at[me], (px, py, pc)
            n = 7 * a + k - 1
            copies.append(pltpu.make_async_remote_copy(
                src_ref=s, dst_ref=d, send_sem=send_sems.at[n], recv_sem=recv_sems.at[n], device_id=to,
                device_id_type=MESH))
    return copies


def _exchange_start(name, srcs, lands, mode, layer=0, after=None):
    n = len(srcs)

    def body(*refs):
        send_sems, recv_sems = refs[-2 * n - 3], refs[-2 * n - 2]
        for cp in _exchange_copies(refs[:n], refs[n:2 * n], send_sems, recv_sems, mode, layer):
            cp.start()
        refs[-1][...] = jnp.zeros_like(refs[-1])

    arrays = list(srcs) + list(lands)
    sems = pltpu.SemaphoreType.DMA((7 * n,))
    out = pl.pallas_call(
        body, name=name,
        out_shape=(sems, sems, *[pltpu.HBM(v.shape, v.dtype) for v in arrays], SDS((8, LANE), F32)),
        in_specs=[HBM_SPEC] * (2 * n) + ([ANY] if after is not None else []),
        out_specs=(SEM_SPEC, SEM_SPEC, *[HBM_SPEC] * (2 * n), pl.BlockSpec(memory_space=pltpu.VMEM)),
        input_output_aliases={i: 2 + i for i in range(2 * n)},
        compiler_params=pltpu.CompilerParams(has_side_effects=EFFECT),
    )(*[pltpu.with_memory_space_constraint(v, pltpu.HBM) for v in arrays], *([after] if after is not None else []))
    return dict(sems=out[:2], srcs=out[2:2 + n], lands=out[2 + n:2 + 2 * n], token=out[-1], mode=mode,
                layer=layer)


def _exchange_wait(name, st, after, also=()):
    n = len(st["srcs"])

    def body(*refs):
        send_sems, recv_sems = refs[2 * n], refs[2 * n + 1]
        for cp in _exchange_copies(refs[:n], refs[n:2 * n], send_sems, recv_sems, st["mode"], st["layer"]):
            cp.wait_send()
            cp.wait_recv()

    arrays = list(st["srcs"]) + list(st["lands"])
    out = pl.pallas_call(
        body, name=name,
        out_shape=tuple(pltpu.HBM(v.shape, v.dtype) for v in arrays),
        in_specs=[HBM_SPEC] * (2 * n) + [SEM_SPEC, SEM_SPEC] + [ANY] * (1 + len(also)),
        out_specs=tuple([HBM_SPEC] * (2 * n)),
        input_output_aliases={i: i for i in range(2 * n)},
        compiler_params=pltpu.CompilerParams(has_side_effects=EFFECT),
    )(*arrays, *st["sems"], after, *also)
    st["srcs"] = out[:n]
    return out[n:]


_IN_PIECES = ([(1024, 3072)]
              + [r for t in range(4) for r in ((LANE * t, LANE * (t + 1)), (512 + LANE * t, 512 + LANE * (t + 1)))]
              + [(4096, 5632), (3072, 4096), (5632, 5648)])


def _permute_in(w):
    pad = jnp.zeros(w.shape[:-1] + (N_PAD - N_IN,), w.dtype)
    return jnp.concatenate([w[..., a:b] for a, b in _IN_PIECES] + [pad], axis=-1)


def _unpermute_in(g):
    ax = [g[..., OFF_LRU + 2 * LANE * t:OFF_LRU + 2 * LANE * t + LANE] for t in range(4)]
    ag = [g[..., OFF_LRU + 2 * LANE * t + LANE:OFF_LRU + 2 * LANE * (t + 1)] for t in range(4)]
    return jnp.concatenate(ax + ag + [g[..., 0:2048], g[..., OFF_Z:OFF_Z + SSD_W], g[..., OFF_XBC:OFF_XBC + SSD_CONV],
                                      g[..., OFF_Z + SSD_W:OFF_Z + SSD_W + SSD_HEADS]], axis=-1)


SHARD_COLS = N_IN // N_DEV


def _in_segments():
    segs, pos = [], 0
    for a, b in _IN_PIECES:
        for i in range(N_DEV):
            lo, hi = max(a, SHARD_COLS * i), min(b, SHARD_COLS * (i + 1))
            if lo < hi:
                segs.append((i, lo - SHARD_COLS * i, hi - lo, pos + lo - a))
        pos += b - a
    return segs


RELAYOUT_ROWS = 512


def _relayout_in(land, own):
    def body(land_ref, own_ref, out_ref):
        me = 4 * lax.axis_index("x") + 2 * lax.axis_index("y") + lax.axis_index("c")
        out_ref[:, N_IN:N_PAD] = jnp.zeros((RELAYOUT_ROWS, N_PAD - N_IN), BF16)
        for i, j, wd, p in _in_segments():
            out_ref[:, p:p + wd] = jnp.where(me == i, own_ref[:, j:j + wd], land_ref[i, :, j:j + wd])

    return pl.pallas_call(
        body, name="relayout_in", grid=(D_MODEL // RELAYOUT_ROWS,),
        in_specs=[pl.BlockSpec((N_DEV, RELAYOUT_ROWS, SHARD_COLS), lambda r: (0, r, 0)),
                  pl.BlockSpec((RELAYOUT_ROWS, SHARD_COLS), lambda r: (r, 0))],
        out_specs=pl.BlockSpec((RELAYOUT_ROWS, N_PAD), lambda r: (r, 0)),
        out_shape=SDS((D_MODEL, N_PAD), BF16),
        compiler_params=_cp(("parallel",)),
    )(land, own)


def _relayout_grad(g):
    def body(g_ref, out_ref):
        for i, j, wd, p in _in_segments():
            out_ref[i, :, j:j + wd] = g_ref[:, p:p + wd].astype(BF16)

    return pl.pallas_call(
        body, name="relayout_grad", grid=(D_MODEL // RELAYOUT_ROWS,),
        in_specs=[pl.BlockSpec((RELAYOUT_ROWS, N_PAD), lambda r: (r, 0))],
        out_specs=pl.BlockSpec((N_DEV, RELAYOUT_ROWS, SHARD_COLS), lambda r: (0, r, 0)),
        out_shape=SDS((N_DEV, D_MODEL, SHARD_COLS), BF16),
        compiler_params=_cp(("parallel",)),
    )(g)


def _block_diag(w):
    w4 = w.reshape(DEPTH, 4, 2, 64, 64)
    z = jnp.zeros((DEPTH, 4, 64, 64), w.dtype)
    top = jnp.concatenate([w4[:, :, 0], z], axis=-1)
    bot = jnp.concatenate([z, w4[:, :, 1]], axis=-1)
    return jnp.concatenate([top, bot], axis=2).astype(BF16)


def _diag_blocks(g):
    return jnp.stack([g[:, :, :64, :64], g[:, :, 64:, 64:]], axis=2).reshape(DEPTH, 8, 64, 64)


def _pad_lanes(v):
    return jnp.pad(v, ((0, 0), (0, LANE - v.shape[1])))


def _lower_bounds(logits):
    p = jax.nn.softmax(logits, axis=0)
    return p, jnp.cumsum(p, axis=0) - p[0]


def _lower_bounds_bwd(p, dlb):
    dp = jnp.cumsum(dlb[::-1], axis=0)[::-1]
    dp = dp.at[0].add(-jnp.sum(dlb, axis=0))
    return p * (dp - jnp.sum(dp * p, axis=0, keepdims=True))


SMALL = ["norm_w", "b_ada", "lru_conv_b", "lru_wa", "lru_ba", "lru_wx", "lru_bx", "lru_lambda", "hg_lb_logits",
         "hg_norm_w", "ssd_conv_b", "ssd_dt_bias", "ssd_a_log", "ssd_d", "ssd_norm_w", "final_norm_w"]
WEIGHTS = ["norm_w", "w_ada", "b_ada", "w_in", "lru_conv_w", "lru_conv_b", "lru_wa", "lru_ba", "lru_wx", "lru_bx",
           "lru_lambda", "hg_lb_logits", "hg_norm_w", "ssd_conv_w", "ssd_conv_b", "ssd_dt_bias", "ssd_a_log", "ssd_d",
           "ssd_norm_w", "w_out", "final_norm_w"]
INPUTS = ["x", "c"] + WEIGHTS + ["loss_target"] + ["m_" + n for n in WEIGHTS] + ["v_" + n for n in WEIGHTS]
SMALL_ROW = 1024


def _small_rows(like):
    out, off = {}, 0
    for n in SMALL:
        rows = -(-int(np.prod(like[n].shape)) // (8 * SMALL_ROW)) * 8
        out[n] = (off, rows)
        off += rows
    return out, off


def _flatten_small(d, prefix="", last=0.0):
    table, _ = _small_rows({n: d[prefix + n] for n in SMALL})
    pieces = []
    for n in SMALL:
        flat = d[prefix + n].reshape(-1)
        pieces.append(jnp.pad(flat, (0, table[n][1] * SMALL_ROW - flat.shape[0])).reshape(-1, SMALL_ROW))
    return jnp.concatenate(pieces + [jnp.full((8, SMALL_ROW), last, F32)], axis=0)


def _split_small(packed, like):
    table, _ = _small_rows(like)
    out = {}
    for n in SMALL:
        off, rows = table[n]
        size = int(np.prod(like[n].shape))
        out[n] = packed[off:off + rows].reshape(-1)[:size].reshape(like[n].shape)
    return out


def _local_step(x, mod, target, w, fetch, emit):
    S = x.shape[0]
    mall = _bfc(_hg_consts())
    mall_t = _bfc(_hg_consts().T)
    consts = _ssd_consts()
    p_lb, lbs = _lower_bounds(w["hg_lb_logits"])
    no_tok = jnp.zeros((8, LANE), F32)
    wa, wx = _block_diag(w["lru_wa"]), _block_diag(w["lru_wx"])
    ba, bx = w["lru_ba"].reshape(DEPTH, 1, LRU_W), w["lru_bx"].reshape(DEPTH, 1, LRU_W)
    lru_cb, lam, ssd_cb = w["lru_conv_b"][:, None], w["lru_lambda"][:, None], w["ssd_conv_b"][:, None]
    bias, alog = _pad_lanes(w["ssd_dt_bias"]), _pad_lanes(w["ssd_a_log"])
    dskip = jnp.repeat(w["ssd_d"], SSD_P, axis=1)
    saved = []
    for l in range(DEPTH):
        w_in_l, w_out_l, token = fetch(l, x)
        shift, scale, gate = (_Row(mod, l, D_MODEL, k) for k in range(3))
        nw = _Row(w["norm_w"], l)
        u, h = _inproj_fwd(x, nw, scale, shift, w_in_l, no_tok if token is None else token)
        ycat = lax.empty((S, D_INNER), BF16)
        lru_args = (l, u, w["lru_conv_w"], lru_cb, wa, ba, wx, bx, lam)
        ycat, h_lru = _lru_fwd(*lru_args, ycat)
        hg_args = (u, _Row(lbs, l), _Row(w["hg_norm_w"], l), mall)
        ycat, o_b, hg_st = _hg_fwd(*hg_args, ycat)
        xbc = _ssdconv_fwd(l, u, w["ssd_conv_w"], ssd_cb)
        ssd_args = (u, xbc, _Row(bias, l), _Row(alog, l), _Row(dskip, l), _Row(w["ssd_norm_w"], l), consts)
        ycat, y_ssd, ssd_st = _ssd_fwd(*ssd_args, ycat)
        token = fetch(l, y_ssd, late=True)
        x_new, y = _outproj_fwd(ycat, w_out_l, x, gate, no_tok if token is None else token)
        saved.append((x, u, h, ycat, nw, scale, gate, w_in_l, w_out_l, lru_args, h_lru, hg_args, o_b, hg_st, ssd_args,
                      y_ssd, ssd_st, y))
        x = x_new
    dx, red = _loss_head(x, w["final_norm_w"][None, :], target)
    loss = red[1, 0]
    reds = {k: [None] * DEPTH for k in ("in", "gate", "lru", "wa", "wx", "hg", "conv", "ssd")}
    for l in reversed(range(DEPTH)):
        (x, u, h, ycat, nw, scale, gate, w_in_l, w_out_l, lru_args, h_lru, hg_args, o_b, hg_st, ssd_args, y_ssd, ssd_st,
         y) = saved[l]
        dycat, g_out, reds["gate"][l] = _outproj_bwd(dx, y, gate, ycat, w_out_l)
        token = emit(l, "w_out", g_out)
        du = lax.empty((S, N_PAD), BF16)
        du, dxbc, reds["ssd"][l] = _ssd_bwd(*ssd_args, y_ssd, ssd_st, dycat, du, no_tok if token is None else token)
        du, reds["conv"][l] = _ssdconv_bwd(l, u, w["ssd_conv_w"], ssd_cb, dxbc, du)
        du, reds["hg"][l] = _hg_bwd(*hg_args, mall_t, o_b, hg_st, dycat, du)
        du, reds["lru"][l], reds["wa"][l], reds["wx"][l] = _lru_bwd(*lru_args, h_lru, dycat, du)
        token = emit(l, "w_in", functools.partial(_inproj_bwd_w, h, du))
        dx, reds["in"][l] = _inproj_bwd_x(du, w_in_l, x, nw, scale, dx, no_tok if token is None else token)
    r = {k: jnp.stack(v) for k, v in reds.items()}
    g = {n: None for n in WEIGHTS}
    g["final_norm_w"] = red[0]
    g["norm_w"] = r["in"][:, 2]
    dmod = jnp.concatenate([r["in"][:, 0], r["in"][:, 1], r["gate"][:, 0]], axis=1)
    g["lru_conv_w"], g["lru_conv_b"] = r["lru"][:, 0:4], r["lru"][:, 4]
    g["lru_ba"], g["lru_bx"] = r["lru"][:, 5].reshape(DEPTH, 8, 64), r["lru"][:, 6].reshape(DEPTH, 8, 64)
    g["lru_lambda"] = r["lru"][:, 7]
    g["lru_wa"], g["lru_wx"] = _diag_blocks(r["wa"]), _diag_blocks(r["wx"])
    g["hg_norm_w"] = r["hg"][:, 0]
    g["hg_lb_logits"] = _lower_bounds_bwd(p_lb, r["hg"][:, 1])
    g["ssd_conv_w"], g["ssd_conv_b"] = r["conv"][:, 0:4], r["conv"][:, 4]
    g["ssd_norm_w"] = r["ssd"][:, 0]
    g["ssd_d"] = r["ssd"][:, 1].reshape(DEPTH, SSD_HEADS, SSD_P).sum(-1)
    g["ssd_dt_bias"] = r["ssd"][:, 2, :SSD_HEADS]
    g["ssd_a_log"] = -r["ssd"][:, 3, :SSD_HEADS] * jnp.exp(w["ssd_a_log"])
    return loss, dx, dmod, g


def kernel(x, c, norm_w, w_ada, b_ada, w_in, lru_conv_w, lru_conv_b, lru_wa, lru_ba, lru_wx, lru_bx, lru_lambda, hg_lb_logits, hg_norm_w, ssd_conv_w, ssd_conv_b, ssd_dt_bias, ssd_a_log, ssd_d, ssd_norm_w, w_out, final_norm_w, loss_target, m_norm_w, m_w_ada, m_b_ada, m_w_in, m_lru_conv_w, m_lru_conv_b, m_lru_wa, m_lru_ba, m_lru_wx, m_lru_bx, m_lru_lambda, m_hg_lb_logits, m_hg_norm_w, m_ssd_conv_w, m_ssd_conv_b, m_ssd_dt_bias, m_ssd_a_log, m_ssd_d, m_ssd_norm_w, m_w_out, m_final_norm_w, v_norm_w, v_w_ada, v_b_ada, v_w_in, v_lru_conv_w, v_lru_conv_b, v_lru_wa, v_lru_ba, v_lru_wx, v_lru_bx, v_lru_lambda, v_hg_lb_logits, v_hg_norm_w, v_ssd_conv_w, v_ssd_conv_b, v_ssd_dt_bias, v_ssd_a_log, v_ssd_d, v_ssd_norm_w, v_w_out, v_final_norm_w):
    return _step(x, c, norm_w, w_ada, b_ada, w_in, lru_conv_w, lru_conv_b, lru_wa, lru_ba, lru_wx, lru_bx, lru_lambda, hg_lb_logits, hg_norm_w, ssd_conv_w, ssd_conv_b, ssd_dt_bias, ssd_a_log, ssd_d, ssd_norm_w, w_out, final_norm_w, loss_target, m_norm_w, m_w_ada, m_b_ada, m_w_in, m_lru_conv_w, m_lru_conv_b, m_lru_wa, m_lru_ba, m_lru_wx, m_lru_bx, m_lru_lambda, m_hg_lb_logits, m_hg_norm_w, m_ssd_conv_w, m_ssd_conv_b, m_ssd_dt_bias, m_ssd_a_log, m_ssd_d, m_ssd_norm_w, m_w_out, m_final_norm_w, v_norm_w, v_w_ada, v_b_ada, v_w_in, v_lru_conv_w, v_lru_conv_b, v_lru_wa, v_lru_ba, v_lru_wx, v_lru_bx, v_lru_lambda, v_hg_lb_logits, v_hg_norm_w, v_ssd_conv_w, v_ssd_conv_b, v_ssd_dt_bias, v_ssd_a_log, v_ssd_d, v_ssd_norm_w, v_w_out, v_final_norm_w)


def _step(*args):
    a = dict(zip(INPUTS, args, strict=True))
    me = 4 * lax.axis_index("x") + 2 * lax.axis_index("y") + lax.axis_index("c")
    x, target = a["x"][0], a["loss_target"][0]

    c_all = _all_gather(a["c"], "gather_c")[:, 0, :]
    b_cols = lax.dynamic_slice_in_dim(a["b_ada"], me * ADA_COLS, ADA_COLS, axis=1)[:, None, :]
    mod_parts = _all_gather(_ada_fwd(c_all, a["w_ada"], b_cols), "gather_mod")
    mod = lax.dynamic_index_in_dim(mod_parts, me, axis=2, keepdims=False)
    mod = mod.transpose(1, 0, 2).reshape(DEPTH, 3 * D_MODEL)

    w = {n: a[n] for n in SMALL}

    w_in_b = [a["w_in"][l].astype(BF16) for l in range(DEPTH)]
    w_out_b = a["w_out"].astype(BF16)
    conv_own = jnp.concatenate([a["lru_conv_w"], a["ssd_conv_w"]], axis=-1)
    cols, rows_out = N_IN // N_DEV, D_INNER // N_DEV

    def gather_start(l, after):
        srcs = [w_in_b[l], w_out_b[l]] + ([conv_own] if l == 0 else [])
        lands = [lax.empty((N_DEV,) + s.shape, s.dtype) for s in srcs]
        return _exchange_start(f"gather_start_{l}", srcs, lands, "chip", after=after)

    def gather_pass(name, st, after, also=()):
        landed = _exchange_wait(name + "_wait", st, after, also)
        st2 = _exchange_start(name + "_pass", st["srcs"], landed, "pass")
        return _exchange_wait(name + "_passed", st2, after)

    gathers = {0: gather_start(0, mod)}
    passing = {}

    def fetch(l, x_l, late=False):
        if late:
            if l + 1 == DEPTH:
                return None
            landed = _exchange_wait(f"gather_{l + 1}_wait", gathers[l + 1], x_l)
            passing[l + 1] = _exchange_start(f"gather_{l + 1}_pass", gathers[l + 1]["srcs"], landed, "pass")
            return passing[l + 1]["token"]
        if l == 0:
            landed = gather_pass("gather_0", gathers[0], x_l, also=(a["w_in"], a["m_w_in"], a["v_w_in"]))
        else:
            landed = _exchange_wait(f"gather_{l}_passed", passing[l], x_l)
        land_out = lax.dynamic_update_index_in_dim(landed[1], w_out_b[l], me, 0)
        if l == 0:
            conv = lax.dynamic_update_index_in_dim(landed[2], conv_own, me, 0).transpose(1, 2, 0, 3)
            w["lru_conv_w"] = conv[..., :64].reshape(DEPTH, 4, LRU_W)
            w["ssd_conv_w"] = conv[..., 64:].reshape(DEPTH, 4, SSD_CONV)
        token = None
        if l + 1 < DEPTH:
            gathers[l + 1] = gather_start(l + 1, land_out)
            token = gathers[l + 1]["token"]
        return _relayout_in(landed[0], w_in_b[l]), land_out.reshape(D_INNER, D_MODEL), token

    PROJ = ("w_in", "w_out")
    scatters = {}
    lands = [lax.empty((N_DEV, DEPTH, D_MODEL, cols), BF16), lax.empty((N_DEV, DEPTH, rows_out, D_MODEL), BF16)]
    own = [None] * DEPTH

    deferred, g_out = {}, {}

    def emit(l, name, grad, after=None):
        if name == "w_out" and l > 0:
            g_out[l] = grad
            return None
        if name == "w_in" and l == 0 and after is None:
            deferred["w_in"] = grad
            return None
        if name == "w_in":
            grad = grad(jnp.zeros((8, LANE), F32) if after is None else after)
        if l == 0:
            k = PROJ.index(name)
            src = _relayout_grad(grad) if name == "w_in" else grad.reshape(N_DEV, rows_out, D_MODEL)
            st = _exchange_start(f"scatter_start_0_{name}", [src], [lands[k]], "scatter", layer=0, after=after)
            scatters[name] = st
            lands[k] = st["lands"][0]
            return st["token"]
        srcs = [_relayout_grad(grad), g_out[l].reshape(N_DEV, rows_out, D_MODEL)]
        st = _exchange_start(f"scatter_start_{l}", srcs, lands, "scatter", layer=l, after=after)
        scatters[l] = st
        lands[:] = st["lands"]
        return st["token"]

    loss_own, dx, dmod, g = _local_step(x, mod, target, w, fetch, emit)

    def sharded(name, parts, own=None, **kw):
        return _adamw(parts, a[name], a["m_" + name], a["v_" + name], "adamw_" + name + kw.pop("tag", ""), own=own, **kw)

    g["b_ada"] = dmod
    small_own = _flatten_small(g, last=loss_own)
    small_st = _exchange_start("gather_small", [small_own], [lax.empty((N_DEV,) + small_own.shape, F32)], "chip",
                               after=dx)
    big = {}
    after = emit(0, "w_in", deferred["w_in"], after=small_st["token"]) + dx[0:8, 0:LANE]

    def own_slices(st):
        return [lax.dynamic_index_in_dim(s, me, 0, keepdims=False) for s in st["srcs"]]

    for l in reversed(range(1, DEPTH)):
        scatters[l]["lands"] = lands
        lands[:] = _exchange_wait(f"scatter_wait_{l}", scatters[l], after)
        own[l] = own_slices(scatters[l])
    scatters["w_out"]["lands"] = [lands[1]]
    lands[1] = _exchange_wait("scatter_wait_0_w_out", scatters["w_out"], after)[0]
    own[0] = [None, own_slices(scatters["w_out"])[0]]
    big["w_out"] = sharded("w_out", lands[1], jnp.stack([own[l][1] for l in range(DEPTH)]))
    upper = sharded("w_in", lands[0], jnp.stack([own[l][0] for l in range(1, DEPTH)]), layers=(1, DEPTH), tag="_upper")
    after = upper[1][0, 0:8, 0:LANE] + big["w_out"][1][0, 0:8, 0:LANE]
    small = gather_pass("gather_small", small_st, after)[0]
    outs = _adamw(small[:, None], *[_flatten_small(a, p)[None] for p in ("", "m_", "v_")], "adamw_small",
                  own=small_own[None])
    res = [_split_small(o[0], a) for o in outs]
    losses = lax.dynamic_update_index_in_dim(small[:, -1, 0], loss_own, me, 0)
    loss = jnp.sum(losses)

    off = _small_rows(a)[0]["b_ada"][0]
    dmod_all = lax.dynamic_update_index_in_dim(small[:, off:off + DEPTH * 3 * D_MODEL // SMALL_ROW],
                                               dmod.reshape(-1, SMALL_ROW), me, 0)
    dmod_all = dmod_all.reshape(N_DEV, DEPTH, 3 * D_MODEL).transpose(1, 0, 2)
    dmod_cols = lax.dynamic_slice_in_dim(dmod_all, me * ADA_COLS, ADA_COLS, axis=2)
    dmod_pad = jnp.pad(dmod_cols, ((0, 0), (0, LANE - N_DEV), (0, 0)))
    ct_pad = jnp.pad(c_all.T, ((0, 0), (0, LANE - N_DEV)))
    big["w_ada"] = sharded("w_ada", _ada_bwd(ct_pad, dmod_pad)[None])
    g_conv = jnp.concatenate([g["lru_conv_w"].reshape(DEPTH, 4, N_DEV, 64), g["ssd_conv_w"].reshape(DEPTH, 4, N_DEV, 192)],
                             axis=-1).transpose(2, 0, 1, 3)
    conv_parts = _all_to_all(g_conv, "scatter_conv")
    big["lru_conv_w"] = sharded("lru_conv_w", conv_parts[..., :64])
    big["ssd_conv_w"] = sharded("ssd_conv_w", conv_parts[..., 64:])

    after = outs[1] + big["w_ada"][1][0, 0:1, 0:1]
    scatters["w_in"]["lands"] = [lands[0]]
    lands[0] = _exchange_wait("scatter_wait_0_w_in", scatters["w_in"], after)[0]
    big["w_in"] = sharded("w_in", lands[0], own_slices(scatters["w_in"])[0][None], layers=(0, 1), prev=upper)

    out = [loss, dx[None]]
    for k in range(4):
        out += [big[n][k] if n in big else res[k][n] for n in WEIGHTS]
    return tuple(out)
```

```python
import functools

import numpy as np
import jax
import jax.numpy as jnp
from jax import lax
from jax.experimental import pallas as pl
from jax.experimental.pallas import tpu as pltpu

F32 = jnp.float32
BF16 = jnp.bfloat16
SDS = jax.ShapeDtypeStruct

N_DEV = 8
DEPTH = 4
D_MODEL = 1024
D_INNER = 2048
EPS = 1e-6
LRU_W = 512
LRU_C = 8.0
HG_W = 512
HG_CHUNK = 64
HG_HEADS = 4
SSD_W = 1024
SSD_HEADS = 16
SSD_P = 64
SSD_N = 128
SSD_CHUNK = 128
SSD_CONV = 1536
N_IN = 5648
N_PAD = 5760
OFF_HG = 0
OFF_LRU = 2048
OFF_XBC = 3072
OFF_Z = 4608
LANE = 128
VMEM_LIMIT = 56 * 1024 * 1024
NEG = -1e30

ADAM_LR = 0.001
ADAM_B1 = 0.9
ADAM_B2 = 0.999
ADAM_EPS = 1e-08
ADAM_WD = 0.01
ADAM_STEP = 10


def _cp(sem=None):
    return pltpu.CompilerParams(dimension_semantics=sem, vmem_limit_bytes=VMEM_LIMIT)


def _dg(a, b, ca, cb):
    return lax.dot_general(a, b, (((ca,), (cb,)), ((), ())), preferred_element_type=F32)


def _mm(a, b):
    return _dg(a, b, 1, 0)


def _mm_nt(a, b):
    return _dg(a, b, 1, 1)


def _mm_tn(a, b):
    return _dg(a, b, 0, 0)


def _bf(x):
    return x.astype(BF16)


def _f(x):
    return x.astype(F32)


def _split3(x):
    hi = x.astype(BF16)
    r = x - hi.astype(F32)
    mid = r.astype(BF16)
    lo = (r - mid.astype(F32)).astype(BF16)
    return hi, mid, lo


def _sel_r(x, m):
    hi, mid, lo = _split3(x)
    return _mm(hi, m) + _mm(mid, m) + _mm(lo, m)


def _sel_l(m, x):
    hi, mid, lo = _split3(x)
    return _mm(m, hi) + _mm(m, mid) + _mm(m, lo)


def _sel_l2(m, x):
    hi = x.astype(BF16)
    lo = (x - hi.astype(F32)).astype(BF16)
    return _mm(m, hi) + _mm(m, lo)


def _sel_tn(x, m):
    hi, mid, lo = _split3(x)
    return _mm_tn(hi, m) + _mm_tn(mid, m) + _mm_tn(lo, m)


def _sigmoid(x):
    return 1.0 / (1.0 + jnp.exp(-x))


def _silu(x):
    return x * _sigmoid(x)


def _dsilu(x):
    s = _sigmoid(x)
    return s * (1.0 + x * (1.0 - s))


def _softplus(x):
    return jnp.maximum(x, 0.0) + jnp.log(1.0 + jnp.exp(-jnp.abs(x)))


def _expm1(z):
    series = z * (1.0 + z * (1.0 / 2) * (1.0 + z * (1.0 / 3) * (1.0 + z * (1.0 / 4) * (
        1.0 + z * (1.0 / 5) * (1.0 + z * (1.0 / 6) * (1.0 + z * (1.0 / 7)))))))
    return jnp.where(jnp.abs(z) < 0.3, series, jnp.exp(z) - 1.0)


def _iota(shape, dim):
    return lax.broadcasted_iota(jnp.int32, shape, dim)


def _last_row(x, rows):
    return jnp.sum(jnp.where(rows == x.shape[0] - 1, x, 0.0), axis=0, keepdims=True)


def _shift_down(x, d, rows, fill=0.0):
    return jnp.where(rows >= d, pltpu.roll(x, d, 0), fill)


def _shift_up(x, d, rows, fill=0.0):
    n = x.shape[0]
    return jnp.where(rows < n - d, pltpu.roll(x, n - d, 0), fill)


def _conv_fwd(x, cw_ref, cb_ref, rows):
    out = cb_ref[...] + cw_ref[pl.ds(3, 1), :] * x
    for k in range(3):
        out = out + cw_ref[pl.ds(k, 1), :] * _shift_down(x, 3 - k, rows)
    return out


def _conv_bwd(x, dco, cw_ref, rows):
    dx = cw_ref[pl.ds(3, 1), :] * dco
    dws = []
    for k in range(3):
        dx = dx + cw_ref[pl.ds(k, 1), :] * _shift_up(dco, 3 - k, rows)
        dws.append(jnp.sum(dco * _shift_down(x, 3 - k, rows), axis=0, keepdims=True))
    dws.append(jnp.sum(dco * x, axis=0, keepdims=True))
    return dx, dws, jnp.sum(dco, axis=0, keepdims=True)


def _vec(n):
    return pl.BlockSpec((1, n), lambda *_: (0, 0))


class _Row:
    def __init__(self, arr, l, n=None, c=0):
        self.arr, self.l, self.n, self.c = arr[:, None, :], l, n or arr.shape[1], c


def _spec(v):
    if isinstance(v, _Row):
        return pl.BlockSpec((None, 1, v.n), lambda *_: (v.l, 0, v.c))
    return _vec(v.shape[1])


def _arr(v):
    return v.arr if isinstance(v, _Row) else v


def _full(shape):
    nd = len(shape)
    return pl.BlockSpec(shape, lambda *_: (0,) * nd)


def _inproj_fwd(x, nw, scale, shift, w, tok):
    S = x.shape[0]
    tm = min(256, S)

    def body(x_ref, nw_ref, sc_ref, sh_ref, w_ref, tok_ref, u_ref, h_ref):
        del tok_ref
        xv = x_ref[...]
        inv = lax.rsqrt(jnp.mean(xv * xv, axis=-1, keepdims=True) + EPS)
        h = ((xv * inv) * nw_ref[...] * (1.0 + sc_ref[...]) + sh_ref[...]).astype(BF16)
        h_ref[...] = h
        u_ref[...] = _mm(h, w_ref[...])

    return pl.pallas_call(
        body, name="inproj_fwd", grid=(S // tm,),
        in_specs=[pl.BlockSpec((tm, D_MODEL), lambda i: (i, 0)), _spec(nw), _spec(scale), _spec(shift),
                  _full((D_MODEL, N_PAD)), pl.BlockSpec(memory_space=pl.ANY)],
        out_specs=[pl.BlockSpec((tm, N_PAD), lambda i: (i, 0)), pl.BlockSpec((tm, D_MODEL), lambda i: (i, 0))],
        out_shape=[SDS((S, N_PAD), F32), SDS((S, D_MODEL), BF16)],
        compiler_params=_cp(("parallel",)),
    )(x, _arr(nw), _arr(scale), _arr(shift), w, tok)


def _inproj_bwd_x(du, w, x, nw, scale, dxn, tok):
    S = x.shape[0]
    tm = min(256, S)

    def body(du_ref, w_ref, x_ref, nw_ref, sc_ref, dxn_ref, tok_ref, dx_ref, red_ref):
        del tok_ref

        @pl.when(pl.program_id(0) == 0)
        def _():
            red_ref[...] = jnp.zeros_like(red_ref)

        dh = _mm_nt(du_ref[...], w_ref[...])
        xv = x_ref[...]
        inv = lax.rsqrt(jnp.mean(xv * xv, axis=-1, keepdims=True) + EPS)
        xhat = xv * inv
        nwv = nw_ref[...]
        g1 = 1.0 + sc_ref[...]
        dxhat = dh * nwv * g1
        dx = inv * (dxhat - xhat * jnp.mean(dxhat * xhat, axis=-1, keepdims=True))
        dx_ref[...] = dxn_ref[...] + dx
        red_ref[0:1, :] += jnp.sum(dh, axis=0, keepdims=True)
        red_ref[1:2, :] += jnp.sum(dh * xhat * nwv, axis=0, keepdims=True)
        red_ref[2:3, :] += jnp.sum(dh * xhat * g1, axis=0, keepdims=True)

    row = pl.BlockSpec((tm, D_MODEL), lambda i: (i, 0))
    return pl.pallas_call(
        body, name="inproj_bwd_x", grid=(S // tm,),
        in_specs=[pl.BlockSpec((tm, N_PAD), lambda i: (i, 0)), _full((D_MODEL, N_PAD)), row, _spec(nw),
                  _spec(scale), row, pl.BlockSpec(memory_space=pl.ANY)],
        out_specs=[row, _full((8, D_MODEL))],
        out_shape=[SDS((S, D_MODEL), F32), SDS((8, D_MODEL), F32)],
        compiler_params=_cp(("arbitrary",)),
    )(du, w, x, _arr(nw), _arr(scale), dxn, tok)


def _inproj_bwd_w(h, du, tok):
    S = h.shape[0]
    tn = 640

    def body(h_ref, du_ref, tok_ref, gw_ref):
        del tok_ref
        gw_ref[...] = _mm_tn(h_ref[...], _bf(du_ref[...]))

    return pl.pallas_call(
        body, name="inproj_bwd_w", grid=(N_PAD // tn,),
        in_specs=[_full((S, D_MODEL)), pl.BlockSpec((S, tn), lambda j: (0, j)), pl.BlockSpec(memory_space=pl.ANY)],
        out_specs=pl.BlockSpec((D_MODEL, tn), lambda j: (0, j)),
        out_shape=SDS((D_MODEL, N_PAD), F32),
        compiler_params=_cp(("parallel",)),
    )(h, du, tok)


def _scan_block(a, b, rows):
    d = 1
    while d < a.shape[0]:
        a_s = _shift_down(a, d, rows, 1.0)
        b_s = _shift_down(b, d, rows, 0.0)
        b = a * b_s + b
        a = a * a_s
        d *= 2
    return a, b


def _rscan_block(c, g, rows):
    d = 1
    while d < c.shape[0]:
        c_s = _shift_up(c, d, rows, 1.0)
        g_s = _shift_up(g, d, rows, 0.0)
        g = g + c * g_s
        c = c * c_s
        d *= 2
    return c, g


LRU_BLOCK = 128


def _lru_gates(xa, wa_ref, ba_ref, wx_ref, bx_ref, lam_ref):
    sp = _softplus(-lam_ref[...])
    xb = _bf(xa)
    r = _sigmoid(_mm(xb, wa_ref[...]) + ba_ref[...])
    ig = _sigmoid(_mm(xb, wx_ref[...]) + bx_ref[...])
    la = -LRU_C * r * sp
    a = jnp.exp(la)
    mult = jnp.sqrt(-_expm1(2.0 * la))
    return sp, r, ig, la, a, mult


def _lru_specs(S, l):
    t128 = pl.BlockSpec((None, 1, LANE), lambda t: (l, 0, t))
    gate = pl.BlockSpec((None, None, LANE, LANE), lambda t: (l, t, 0, 0))
    return [pl.BlockSpec((S, 2 * LANE), lambda t: (0, OFF_LRU // (2 * LANE) + t)),
            pl.BlockSpec((None, 4, LANE), lambda t: (l, 0, t)), t128, gate, t128, gate, t128, t128]


def _lru_fwd(l, u, cw, cb, wa, ba, wx, bx, lam, ycat):
    S = u.shape[0]
    tb = min(LRU_BLOCK, S)

    def body(u_ref, cw_ref, cb_ref, wa_ref, ba_ref, wx_ref, bx_ref, lam_ref, ycat_in, ycat_ref, h_ref, a_scr, b_scr):
        del ycat_in
        rows = _iota((S, LANE), 0)
        xa = _conv_fwd(_f(u_ref[:, 0:LANE]), cw_ref, cb_ref, rows)
        _, _, ig, _, a, mult = _lru_gates(xa, wa_ref, ba_ref, wx_ref, bx_ref, lam_ref)
        a_scr[...] = a
        b_scr[...] = mult * (ig * xa)
        rows_b = _iota((tb, LANE), 0)

        def blk(j, hprev):
            sl = pl.ds(pl.multiple_of(j * tb, tb), tb)
            acum, hloc = _scan_block(a_scr[sl, :], b_scr[sl, :], rows_b)
            hf = hloc + acum * hprev
            h_ref[sl, :] = hf
            return _last_row(hf, rows_b)

        lax.fori_loop(0, S // tb, blk, jnp.zeros((1, LANE), F32))
        ycat_ref[...] = _bf(h_ref[...] * _silu(_f(u_ref[:, LANE:2 * LANE])))

    col = pl.BlockSpec((S, LANE), lambda t: (0, t))
    return pl.pallas_call(
        body, name="lru_fwd", grid=(LRU_W // LANE,),
        in_specs=_lru_specs(S, l) + [pl.BlockSpec(memory_space=pl.ANY)],
        out_specs=[col, col],
        out_shape=[SDS((S, D_INNER), BF16), SDS((S,LRU_W), F32)],
        scratch_shapes=[pltpu.VMEM((S, LANE), F32), pltpu.VMEM((S, LANE), F32)],
        input_output_aliases={8: 0},
        compiler_params=_cp(("parallel",)),
    )(u, cw, cb, wa, ba, wx, bx, lam, ycat)


def _lru_bwd(l, u, cw, cb, wa, ba, wx, bx, lam, h_lru, dycat, du):
    S = u.shape[0]
    tb = min(LRU_BLOCK, S)

    def body(u_ref, cw_ref, cb_ref, wa_ref, ba_ref, wx_ref, bx_ref, lam_ref, h_ref, dy_ref, du_in,
             du_ref, red_ref, gwa_ref, gwx_ref, c_scr, g_scr, l_scr):
        del du_in
        rows = _iota((S, LANE), 0)
        ax = _f(u_ref[:, 0:LANE])
        ag = _f(u_ref[:, LANE:2 * LANE])
        xa = _conv_fwd(ax, cw_ref, cb_ref, rows)
        sp, r, ig, la, a, mult = _lru_gates(xa, wa_ref, ba_ref, wx_ref, bx_ref, lam_ref)
        h = h_ref[...]
        dy = _f(dy_ref[...])
        du_ref[:, LANE:2 * LANE] = _bf(dy * h * _dsilu(ag))
        c_scr[...] = _shift_up(a, 1, rows, 0.0)
        g_scr[...] = dy * _silu(ag)
        rows_b = _iota((tb, LANE), 0)
        nb = S // tb

        def blk(jj, lnext):
            j = nb - 1 - jj
            sl = pl.ds(pl.multiple_of(j * tb, tb), tb)
            ccum, lloc = _rscan_block(c_scr[sl, :], g_scr[sl, :], rows_b)
            lam_t = lloc + ccum * lnext
            l_scr[sl, :] = lam_t
            return jnp.sum(jnp.where(rows_b == 0, lam_t, 0.0), axis=0, keepdims=True)

        lax.fori_loop(0, nb, blk, jnp.zeros((1, LANE), F32))
        db = l_scr[...]
        da = db * _shift_down(h, 1, rows)
        dmult = db * ig * xa
        dig = db * mult * xa
        dxa = db * mult * ig
        dla = da * a - dmult * (a * a) / mult
        dr = -LRU_C * sp * dla
        dsp = jnp.sum(-LRU_C * r * dla, axis=0, keepdims=True)
        dlam = -dsp * _sigmoid(-lam_ref[...])
        dzr = dr * r * (1.0 - r)
        dzi = dig * ig * (1.0 - ig)
        dzr_b, dzi_b, xa_b = _bf(dzr), _bf(dzi), _bf(xa)
        dxa = dxa + _mm_nt(dzr_b, wa_ref[...]) + _mm_nt(dzi_b, wx_ref[...])
        gwa_ref[...] = _mm_tn(xa_b, dzr_b)
        gwx_ref[...] = _mm_tn(xa_b, dzi_b)
        dax, dws, dcb = _conv_bwd(ax, dxa, cw_ref, rows)
        du_ref[:, 0:LANE] = _bf(dax)
        parts = dws + [dcb, jnp.sum(dzr, axis=0, keepdims=True), jnp.sum(dzi, axis=0, keepdims=True), dlam]
        for n, p in enumerate(parts):
            red_ref[pl.ds(n, 1), :] = p

    col = pl.BlockSpec((S, LANE), lambda t: (0, t))
    gw = pl.BlockSpec((None, LANE, LANE), lambda t: (t, 0, 0))
    return pl.pallas_call(
        body, name="lru_bwd", grid=(LRU_W // LANE,),
        in_specs=_lru_specs(S, l) + [col, col, pl.BlockSpec(memory_space=pl.ANY)],
        out_specs=[pl.BlockSpec((S, 2 * LANE), lambda t: (0, OFF_LRU // (2 * LANE) + t)),
                   pl.BlockSpec((8, LANE), lambda t: (0, t)), gw, gw],
        out_shape=[SDS((S, N_PAD), BF16), SDS((8, LRU_W), F32), SDS((4, LANE, LANE), F32), SDS((4, LANE, LANE), F32)],
        scratch_shapes=[pltpu.VMEM((S, LANE), F32)] * 3,
        input_output_aliases={10: 0},
        compiler_params=_cp(("parallel",)),
    )(u, cw, cb, wa, ba, wx, bx, lam, h_lru, dycat, du)


HG_LEVELS = 6


def _hg_consts():
    C = HG_CHUNK
    t = np.arange(C)[:, None]
    r = np.arange(C)[None, :]
    mats = []
    for l in range(HG_LEVELS):
        b = 1 << l
        upper = (t % (2 * b)) >= b
        anchor = (t // (2 * b)) * 2 * b + b - 1
        mats.append((upper & (r > anchor) & (r <= t)) | ((~upper) & (r > t) & (r <= anchor)))
    mats.append(r <= t)
    mats.append(r > t)
    return np.concatenate(mats, 0).astype(np.float32)


def _hg_factors(hf, lb, mall):
    s = _sigmoid(hf)
    f = lb + (1.0 - lb) * s
    lf = jnp.log(f)
    k = (1.0 - lb) * _sigmoid(-hf)
    e = jnp.exp(_sel_l(mall, lf))
    C = HG_CHUNK
    rows = _iota((C, HG_W), 0)
    eq, ek = [], []
    for l in range(HG_LEVELS):
        el = e[l * C:(l + 1) * C]
        eq.append(jnp.where((lax.shift_right_logical(rows, l) & 1) == 1, el, 0.0))
        ek.append(el - eq[l])
    ecum = e[HG_LEVELS * C:(HG_LEVELS + 1) * C]
    erem = e[(HG_LEVELS + 1) * C:(HG_LEVELS + 2) * C]
    return s, f, k, eq, ek, ecum, erem


def _hg_masks():
    C = HG_CHUNK
    ri, ci = _iota((C, C), 0), _iota((C, C), 1)
    rr = _iota((C, LANE), 0)
    gm = [(lax.shift_right_logical(ri, l + 1) == lax.shift_right_logical(ci, l + 1)).astype(F32)
          for l in range(HG_LEVELS)]
    up = [(lax.shift_right_logical(rr, l) & 1) == 1 for l in range(HG_LEVELS)]
    eye = (ri == ci).astype(F32)
    return gm, up, eye, rr


def _hg_scores(qh, kh, eq, ek, sl, gm, up, eye):
    del up
    qs, ks, qb, kb = [], [], [], []
    p = _mm_nt(_bf(qh), _bf(kh)) * eye
    for l in range(HG_LEVELS):
        qs.append(qh * eq[l][:, sl])
        ks.append(kh * ek[l][:, sl])
        qb.append(_bf(qs[l]))
        kb.append(_bf(ks[l]))
        p = p + _mm_nt(qb[l], kb[l]) * gm[l]
    return p, qs, ks, qb, kb


HG_SUB = 4


def _hg_fwd(u, lb, nw, mall, ycat):
    S = u.shape[0]
    C = HG_CHUNK
    n = S // C
    rows = HG_SUB * C

    def body(u_ref, lb_ref, nw_ref, mall_ref, ycat_in, ycat_ref, o_ref, st_ref, st):
        del ycat_in

        @pl.when(pl.program_id(0) == 0)
        def _():
            st[...] = jnp.zeros_like(st)

        gm, up, eye, rr = _hg_masks()
        for sub in range(HG_SUB):
            r = slice(sub * C, (sub + 1) * C)
            q = _silu(_f(u_ref[r, 0:512]))
            v = u_ref[r, 1024:1536]
            _, _, k, eq, ek, ecum, erem = _hg_factors(_f(u_ref[r, 512:1024]), lb_ref[...], mall_ref[...])
            for h in range(HG_HEADS):
                sl = slice(h * LANE, (h + 1) * LANE)
                qh, kh, vh = q[:, sl], k[:, sl], _bf(v[:, sl])
                p = _hg_scores(qh, kh, eq, ek, sl, gm, up, eye)[0]
                sth = st[h]
                st_ref[sub, h] = sth
                o_ref[r, sl] = _mm(_bf(p), vh) + _mm_nt(_bf(qh * ecum[:, sl]), _bf(sth))
                st[h] = sth * _last_row(ecum[:, sl], rr) + _mm_tn(vh, _bf(kh * erem[:, sl]))
            o = o_ref[r, :]
            inv = lax.rsqrt(jnp.mean(o * o, axis=-1, keepdims=True) + EPS)
            ycat_ref[r, :] = _bf((o * inv) * nw_ref[...] * _silu(_f(u_ref[r, 1536:2048])))

    return pl.pallas_call(
        body, name="hg_fwd", grid=(n // HG_SUB,),
        in_specs=[pl.BlockSpec((rows, 2048), lambda i: (i, 0)), _spec(lb), _spec(nw), _full(mall.shape),
                  pl.BlockSpec(memory_space=pl.ANY)],
        out_specs=[pl.BlockSpec((rows, HG_W), lambda i: (i, 1)), pl.BlockSpec((rows, HG_W), lambda i: (i, 0)),
                   pl.BlockSpec((HG_SUB, HG_HEADS, LANE, LANE), lambda i: (i, 0, 0, 0))],
        out_shape=[SDS((S, D_INNER), BF16), SDS((S,HG_W), F32), SDS((n, HG_HEADS, LANE, LANE), F32)],
        scratch_shapes=[pltpu.VMEM((HG_HEADS, LANE, LANE), F32)],
        input_output_aliases={4: 0},
        compiler_params=_cp(("arbitrary",)),
    )(u, _arr(lb), _arr(nw), mall, ycat)


def _hg_bwd(u, lb, nw, mall, mall_t, o_b, states, dycat, du):
    S = u.shape[0]
    C = HG_CHUNK
    n = S // C
    nb = n // HG_SUB
    rows = HG_SUB * C
    L2 = HG_LEVELS

    def body(u_ref, lb_ref, nw_ref, mall_ref, mallt_ref, o_ref, st_ref, dy_ref, du_in, du_ref, red_ref,
             dst, dlast_s, dq_s, dk_s, dex):
        del du_in

        @pl.when(pl.program_id(0) == 0)
        def _():
            dst[...] = jnp.zeros_like(dst)
            red_ref[...] = jnp.zeros_like(red_ref)

        lb = lb_ref[...]
        nwv = nw_ref[...]
        gm, up, eye, rr = _hg_masks()
        for sub in reversed(range(HG_SUB)):
            r = slice(sub * C, (sub + 1) * C)
            hq, hf, hg = _f(u_ref[r, 0:512]), _f(u_ref[r, 512:1024]), _f(u_ref[r, 1536:2048])
            q = _silu(hq)
            v = u_ref[r, 1024:1536]
            s, f, k, eq, ek, ecum, erem = _hg_factors(hf, lb, mall_ref[...])
            o = o_ref[r, :]
            dy = _f(dy_ref[r, :])
            inv = lax.rsqrt(jnp.mean(o * o, axis=-1, keepdims=True) + EPS)
            ohat = o * inv
            du_ref[r, 1536:2048] = _bf(dy * ohat * nwv * _dsilu(hg))
            dn = dy * _silu(hg)
            red_ref[0:1, :] += jnp.sum(dn * ohat, axis=0, keepdims=True)
            dohat = dn * nwv
            do = inv * (dohat - ohat * jnp.mean(dohat * ohat, axis=-1, keepdims=True))
            for h in range(HG_HEADS):
                sl = slice(h * LANE, (h + 1) * LANE)
                qh, kh, vh, doh = q[:, sl], k[:, sl], _bf(v[:, sl]), _bf(do[:, sl])
                p, qs, ks, qb, kb = _hg_scores(qh, kh, eq, ek, sl, gm, up, eye)
                st_f = st_ref[sub, h]
                sth = _bf(st_f)
                dsth = dst[h]
                dsth_b = _bf(dsth)
                qt = qh * ecum[:, sl]
                kt = kh * erem[:, sl]
                elast = _last_row(ecum[:, sl], rr)
                dp = _mm_nt(doh, vh)
                du_ref[r, 1024 + h * LANE:1024 + (h + 1) * LANE] = _bf(_mm_tn(_bf(p), doh) + _mm_nt(_bf(kt), dsth_b))
                dpe = _bf(dp * eye)
                dqt = _mm(doh, sth)
                dkt = _mm(vh, dsth_b)
                dq = dqt * ecum[:, sl] + _mm(dpe, _bf(kh))
                dk = dkt * erem[:, sl] + _mm_tn(dpe, _bf(qh))
                dex[sub, L2 * C:(L2 + 1) * C, sl] = dqt * qt
                dex[sub, (L2 + 1) * C:(L2 + 2) * C, sl] = dkt * kt
                for l in range(HG_LEVELS):
                    dpl = _bf(dp * gm[l])
                    dql = _mm(dpl, kb[l])
                    dkl = _mm_tn(dpl, qb[l])
                    dq = dq + dql * eq[l][:, sl]
                    dk = dk + dkl * ek[l][:, sl]
                    dex[sub, l * C:(l + 1) * C, sl] = dql * qs[l] + dkl * ks[l]
                dlast_s[sub, :, sl] = jnp.sum(dsth * st_f, axis=0, keepdims=True) * elast
                dst[h] = dsth * elast + _mm_tn(doh, _bf(qt))
                dq_s[sub, :, sl] = dq
                dk_s[sub, :, sl] = dk
            dq = dq_s[sub]
            dk = dk_s[sub]
            dlf = _sel_l2(mallt_ref[...], dex[sub]) + dlast_s[sub]
            du_ref[r, 0:512] = _bf(dq * _dsilu(hq))
            t = (1.0 - s) * (dlf / f - dk)
            du_ref[r, 512:1024] = _bf((1.0 - lb) * s * t)
            red_ref[1:2, :] += jnp.sum(t, axis=0, keepdims=True)

    rev = lambda i: (nb - 1 - i, 0)
    return pl.pallas_call(
        body, name="hg_bwd", grid=(nb,),
        in_specs=[pl.BlockSpec((rows, 2048), rev), _spec(lb), _spec(nw), _full(mall.shape), _full(mall_t.shape),
                  pl.BlockSpec((rows, HG_W), rev),
                  pl.BlockSpec((HG_SUB, HG_HEADS, LANE, LANE), lambda i: (nb - 1 - i, 0, 0, 0)),
                  pl.BlockSpec((rows, HG_W), lambda i: (nb - 1 - i, 1)), pl.BlockSpec(memory_space=pl.ANY)],
        out_specs=[pl.BlockSpec((rows, 2048), rev), pl.BlockSpec((8, HG_W), lambda i: (0, 0))],
        out_shape=[SDS((S, N_PAD), BF16), SDS((8, HG_W), F32)],
        scratch_shapes=[pltpu.VMEM((HG_HEADS, LANE, LANE), F32), pltpu.VMEM((HG_SUB, 1, HG_W), F32),
                        pltpu.VMEM((HG_SUB, C, HG_W), F32), pltpu.VMEM((HG_SUB, C, HG_W), F32),
                        pltpu.VMEM((HG_SUB, (L2 + 2) * C, HG_W), F32)],
        input_output_aliases={8: 0},
        compiler_params=_cp(("arbitrary",)),
    )(u, _arr(lb), _arr(nw), mall, mall_t, o_b, states, dycat, du)


def _ssdconv_fwd(l, u, cw, cb):
    S = u.shape[0]

    def body(u_ref, cw_ref, cb_ref, out_ref):
        rows = _iota((S, LANE), 0)
        out_ref[...] = _silu(_conv_fwd(_f(u_ref[...]), cw_ref, cb_ref, rows))

    return pl.pallas_call(
        body, name="ssdconv_fwd", grid=(SSD_CONV // LANE,),
        in_specs=[pl.BlockSpec((S, LANE), lambda t: (0, OFF_XBC // LANE + t)),
                  pl.BlockSpec((None, 4, LANE), lambda t: (l, 0, t)), pl.BlockSpec((None, 1, LANE), lambda t: (l, 0, t))],
        out_specs=pl.BlockSpec((S, LANE), lambda t: (0, t)),
        out_shape=SDS((S, SSD_CONV), F32),
        compiler_params=_cp(("parallel",)),
    )(u, cw, cb)


def _ssdconv_bwd(l, u, cw, cb, dxbc, du):
    S = u.shape[0]

    def body(u_ref, cw_ref, cb_ref, d_ref, du_in, du_ref, red_ref):
        del du_in
        rows = _iota((S, LANE), 0)
        x = _f(u_ref[...])
        dco = d_ref[...] * _dsilu(_conv_fwd(x, cw_ref, cb_ref, rows))
        dx, dws, dcb = _conv_bwd(x, dco, cw_ref, rows)
        du_ref[...] = _bf(dx)
        for n, p in enumerate(dws + [dcb]):
            red_ref[pl.ds(n, 1), :] = p
        red_ref[pl.ds(5, 3), :] = jnp.zeros((3, LANE), F32)

    ucol = pl.BlockSpec((S, LANE), lambda t: (0, OFF_XBC // LANE + t))
    return pl.pallas_call(
        body, name="ssdconv_bwd", grid=(SSD_CONV // LANE,),
        in_specs=[ucol, pl.BlockSpec((None, 4, LANE), lambda t: (l, 0, t)),
                  pl.BlockSpec((None, 1, LANE), lambda t: (l, 0, t)),
                  pl.BlockSpec((S, LANE), lambda t: (0, t)), pl.BlockSpec(memory_space=pl.ANY)],
        out_specs=[ucol, pl.BlockSpec((8, LANE), lambda t: (0, t))],
        out_shape=[SDS((S, N_PAD), BF16), SDS((8, SSD_CONV), F32)],
        input_output_aliases={4: 0},
        compiler_params=_cp(("parallel",)),
    )(u, cw, cb, dxbc, du)


SSD_SUB = 2


def _ssd_consts():
    e64 = np.zeros((LANE, SSD_W), np.float32)
    for h in range(SSD_HEADS):
        e64[h, h * SSD_P:(h + 1) * SSD_P] = 1.0
    T = SSD_CHUNK
    tril = (np.arange(T)[None, :] <= np.arange(T)[:, None]).astype(np.float32)
    return e64, tril, tril.T.copy()


def _ssd_common(zdt, bias_ref, alog_ref, tril, e64, cum_ref, cumt_ref):
    T = SSD_CHUNK
    lane = _iota((1, LANE), 1)
    a_neg = jnp.where(lane < SSD_HEADS, -jnp.exp(alog_ref[...]), 0.0)
    dtpre = zdt[:, SSD_W:SSD_W + LANE] + bias_ref[...]
    dt = _softplus(dtpre)
    cum = _sel_l(tril, dt * a_neg)
    cum_ref[...] = cum
    cumt_ref[...] = cum.T
    cum_x = _sel_r(cum, e64)
    last_x = _last_row(cum_x, _iota((T, SSD_W), 0))
    ecum_x = jnp.exp(cum_x)
    erem_x = jnp.exp(last_x - cum_x)
    elast_x = jnp.exp(last_x)
    dt_x = _sel_r(dt, e64)
    return a_neg, dtpre, dt, ecum_x, erem_x, elast_x, dt_x


def _ssd_decay(cum_ref, cumt_ref, h, causal):
    T = SSD_CHUNK
    diff = jnp.broadcast_to(cum_ref[:, pl.ds(h, 1)], (T, T)) - cumt_ref[pl.ds(h, 1), :]
    return jnp.exp(jnp.where(causal, diff, NEG))


def _group_norm_fwd(y1, nwv):
    outs, invs = [], []
    for g in range(2):
        seg = y1[:, g * 512:(g + 1) * 512]
        inv = lax.rsqrt(jnp.mean(seg * seg, axis=-1, keepdims=True) + EPS)
        outs.append(seg * inv * nwv[:, g * 512:(g + 1) * 512])
        invs.append(inv)
    return outs, invs


def _ssd_fwd(u, xbc, bias, alog, dskip_x, nw, consts, ycat):
    S = u.shape[0]
    T = SSD_CHUNK
    n = S // T
    rows = SSD_SUB * T
    e64, tril, _ = consts

    def body(u_ref, xbc_ref, bias_ref, alog_ref, dx_ref, nw_ref, e64_ref, tril_ref, ycat_in,
             ycat_ref, y_ref, st_ref, st, cumt, cum_e):
        del ycat_in

        @pl.when(pl.program_id(0) == 0)
        def _():
            st[...] = jnp.zeros_like(st)

        causal = _iota((T, T), 0) >= _iota((T, T), 1)
        lo = _iota((T, LANE), 1) < SSD_P
        for sub in range(SSD_SUB):
            r = slice(sub * T, (sub + 1) * T)
            zdt = _f(u_ref[r, :])
            z = zdt[:, 0:SSD_W]
            xs = xbc_ref[r, 0:SSD_W]
            cum_r, cumt_r = cum_e.at[sub], cumt.at[sub]
            _, _, _, ecum_x, erem_x, elast_x, dt_x = _ssd_common(
                zdt, bias_ref, alog_ref, tril_ref[...], e64_ref[...], cum_r, cumt_r)
            xdt = xs * dt_x
            xrem = xdt * erem_x
            st_ref[sub] = st[...]
            for g in range(2):
                gs = slice(g * 512, (g + 1) * 512)
                bg = _bf(xbc_ref[r, SSD_W + g * LANE:SSD_W + (g + 1) * LANE])
                cg = _bf(xbc_ref[r, SSD_W + 256 + g * LANE:SSD_W + 256 + (g + 1) * LANE])
                cb = _mm_nt(cg, bg)
                yin = _mm(cg, _bf(st[:, gs])) * ecum_x[:, gs]
                for j in range(4):
                    h0 = 8 * g + 2 * j
                    cs = slice(h0 * SSD_P, (h0 + 2) * SSD_P)
                    xp = xdt[:, cs]
                    s0 = _bf(cb * _ssd_decay(cum_r, cumt_r, h0, causal))
                    s1 = _bf(cb * _ssd_decay(cum_r, cumt_r, h0 + 1, causal))
                    y_ref[r, cs] = (_mm(s0, _bf(jnp.where(lo, xp, 0.0))) + _mm(s1, _bf(jnp.where(lo, 0.0, xp)))
                                    + yin[:, j * LANE:(j + 1) * LANE])
                st[:, gs] = st[:, gs] * elast_x[:, gs] + _mm_tn(bg, _bf(xrem[:, gs]))
            y1 = (y_ref[r, :] + dx_ref[...] * xs) * _silu(z)
            outs, _ = _group_norm_fwd(y1, nw_ref[...])
            for g in range(2):
                ycat_ref[r, g * 512:(g + 1) * 512] = _bf(outs[g])

    return pl.pallas_call(
        body, name="ssd_fwd", grid=(n // SSD_SUB,),
        in_specs=[pl.BlockSpec((rows, SSD_W + LANE), lambda i: (i, OFF_Z // (SSD_W + LANE))),
                  pl.BlockSpec((rows, SSD_CONV), lambda i: (i, 0)), _spec(bias), _spec(alog), _spec(dskip_x), _spec(nw),
                  _full(e64.shape), _full(tril.shape), pl.BlockSpec(memory_space=pl.ANY)],
        out_specs=[pl.BlockSpec((rows, SSD_W), lambda i: (i, 1)), pl.BlockSpec((rows, SSD_W), lambda i: (i, 0)),
                   pl.BlockSpec((SSD_SUB, SSD_N, SSD_W), lambda i: (i, 0, 0))],
        out_shape=[SDS((S, D_INNER), BF16), SDS((S,SSD_W), F32), SDS((n, SSD_N, SSD_W), F32)],
        scratch_shapes=[pltpu.VMEM((SSD_N, SSD_W), F32), pltpu.VMEM((SSD_SUB, LANE, T), F32),
                        pltpu.VMEM((SSD_SUB, T, LANE), F32)],
        input_output_aliases={8: 0},
        compiler_params=_cp(("arbitrary",)),
    )(u, xbc, _arr(bias), _arr(alog), _arr(dskip_x), _arr(nw), _bfc(e64), _bfc(tril), ycat)


def _ssd_bwd(u, xbc, bias, alog, dskip_x, nw, consts, y_ssd, states, dycat, du, tok):
    S = u.shape[0]
    T = SSD_CHUNK
    n = S // T
    e64, tril, triu = consts
    e64t = np.ascontiguousarray(e64.T)

    def chunk(u_ref, xbc_ref, bias_ref, alog_ref, dx_ref, nw_ref, e64_ref, e64t_ref, tril_ref, triu_ref,
              y_ref, st_ref, dy_ref, du_ref, dxbc_ref, red_ref, dst, dl_s, cumt, dxdt_s, dy0_s, gb_s, gc_s, cum_e, cs_s):
        zdt = _f(u_ref[...])
        z = zdt[:, 0:SSD_W]
        xs = xbc_ref[:, 0:SSD_W]
        a_neg, dtpre, dt, ecum_x, erem_x, elast_x, dt_x = _ssd_common(
            zdt, bias_ref, alog_ref, tril_ref[...], e64_ref[...], cum_e, cumt)
        causal = _iota((T, T), 0) >= _iota((T, T), 1)
        lo = _iota((T, LANE), 1) < SSD_P
        xdt = xs * dt_x
        xrem = xdt * erem_x
        y = y_ref[...]
        dxv = dx_ref[...]
        nwv = nw_ref[...]
        sz = _silu(z)
        y0 = y + dxv * xs
        y1 = y0 * sz
        for g in range(2):
            gs = slice(g * 512, (g + 1) * 512)
            seg = y1[:, gs]
            inv = lax.rsqrt(jnp.mean(seg * seg, axis=-1, keepdims=True) + EPS)
            shat = seg * inv
            dyg = _f(dy_ref[:, gs])
            red_ref[0:1, gs] += jnp.sum(dyg * shat, axis=0, keepdims=True)
            dsh = dyg * nwv[:, gs]
            dy1g = inv * (dsh - shat * jnp.mean(dsh * shat, axis=-1, keepdims=True))
            du_ref[:, gs] = _bf(dy1g * y0[:, gs] * _dsilu(z[:, gs]))
            dy0_s[:, gs] = dy1g * sz[:, gs]
        dy0 = dy0_s[...]
        red_ref[1:2, :] += jnp.sum(dy0 * xs, axis=0, keepdims=True)
        dyin = dy0 * ecum_x
        lane = _iota((T, LANE), 1)
        dcum = jnp.zeros((T, LANE), F32)

        def decay_grad(h, gm):
            cs_s[pl.ds(h, 1), :] = jnp.sum(gm, axis=0, keepdims=True)
            return jnp.where(lane == h, jnp.sum(gm, axis=1, keepdims=True), 0.0)

        for g in range(2):
            gs = slice(g * 512, (g + 1) * 512)
            bg = _bf(xbc_ref[:, SSD_W + g * LANE:SSD_W + (g + 1) * LANE])
            cg = _bf(xbc_ref[:, SSD_W + 256 + g * LANE:SSD_W + 256 + (g + 1) * LANE])
            cb = _mm_nt(cg, bg)
            dst_f, st_f = dst[:, gs], st_ref[:, gs]
            dstg = _bf(dst_f)
            stg = _bf(st_f)
            dyin_g = _bf(dyin[:, gs])
            xrem_g = _bf(xrem[:, gs])
            dcb = jnp.zeros((T, T), F32)
            dxr = _mm(bg, dstg)
            dxdt_s[:, gs] = dxr * erem_x[:, gs]
            gc_s[:, gs] = dxr * xrem[:, gs]
            gb_s[:, gs] = dyin[:, gs] * _mm(cg, stg)
            dl_s[:, gs] = jnp.sum(dst_f * st_f, axis=0, keepdims=True) * elast_x[:, gs]
            for j in range(4):
                h0 = 8 * g + 2 * j
                cs = slice(h0 * SSD_P, (h0 + 2) * SSD_P)
                xp = xdt[:, cs]
                dyp = dy0[:, cs]
                x_lo, x_hi = _bf(jnp.where(lo, xp, 0.0)), _bf(jnp.where(lo, 0.0, xp))
                d_lo, d_hi = _bf(jnp.where(lo, dyp, 0.0)), _bf(jnp.where(lo, 0.0, dyp))
                l0 = _ssd_decay(cum_e, cumt, h0, causal)
                l1 = _ssd_decay(cum_e, cumt, h0 + 1, causal)
                s0 = cb * l0
                s1 = cb * l1
                ds0 = _mm_nt(d_lo, x_lo)
                ds1 = _mm_nt(d_hi, x_hi)
                dcb = dcb + ds0 * l0 + ds1 * l1
                dxdt_s[:, cs] += _mm_tn(_bf(s0), d_lo) + _mm_tn(_bf(s1), d_hi)
                dcum = dcum + decay_grad(h0, ds0 * s0) + decay_grad(h0 + 1, ds1 * s1)
            dcb_b = _bf(dcb)
            dxbc_ref[:, SSD_W + g * LANE:SSD_W + (g + 1) * LANE] = _mm_tn(dcb_b, cg) + _mm_nt(xrem_g, dstg)
            dxbc_ref[:, SSD_W + 256 + g * LANE:SSD_W + 256 + (g + 1) * LANE] = _mm(dcb_b, bg) + _mm_nt(dyin_g, stg)
            dst[:, gs] = dst_f * elast_x[:, gs] + _mm_tn(cg, dyin_g)
        dxdt = dxdt_s[...]
        dxbc_ref[:, 0:SSD_W] = dxdt * dt_x + dy0 * dxv
        e64t = e64t_ref[...]
        gc = gc_s[...]
        dlast_x = jnp.sum(gc, axis=0, keepdims=True) + dl_s[...]
        dlast = jnp.max(_sel_r(jnp.broadcast_to(dlast_x, (8, SSD_W)), e64t), axis=0, keepdims=True)
        dcum = (dcum - cs_s[...].T + _sel_r(gb_s[...] - gc, e64t)
                + jnp.where(_iota((T, LANE), 0) == T - 1, dlast, 0.0))
        dda = _sel_l(triu_ref[...], dcum)
        ddt = dda * a_neg + _sel_r(dxdt * xs, e64t)
        ddtpre = ddt * _sigmoid(dtpre)
        du_ref[:, SSD_W:SSD_W + LANE] = _bf(jnp.where(lane < SSD_HEADS, ddtpre, 0.0))
        red_ref[2:3, 0:LANE] += jnp.sum(ddtpre, axis=0, keepdims=True)
        red_ref[3:4, 0:LANE] += jnp.sum(dda * dt, axis=0, keepdims=True)

    def body(u_ref, xbc_ref, bias_ref, alog_ref, dx_ref, nw_ref, e64_ref, e64t_ref, tril_ref, triu_ref,
             y_ref, st_ref, dy_ref, du_in, tok_ref, du_ref, dxbc_ref, red_ref, dst, *scratch):
        del du_in, tok_ref

        @pl.when(pl.program_id(0) == 0)
        def _():
            dst[...] = jnp.zeros_like(dst)
            red_ref[...] = jnp.zeros_like(red_ref)
            scratch[-1][...] = jnp.zeros_like(scratch[-1])

        for sub in reversed(range(SSD_SUB)):
            rs = pl.ds(sub * T, T)
            chunk(u_ref.at[rs], xbc_ref.at[rs], bias_ref, alog_ref, dx_ref, nw_ref, e64_ref, e64t_ref, tril_ref, triu_ref,
                  y_ref.at[rs], st_ref.at[sub], dy_ref.at[rs], du_ref.at[rs], dxbc_ref.at[rs], red_ref, dst,
                  *[s.at[sub] for s in scratch])

    nb = n // SSD_SUB
    rows = SSD_SUB * T
    rev = lambda i: (nb - 1 - i, 0)
    sub_scratch = [(1, SSD_W), (LANE, T)] + [(T, SSD_W)] * 4 + [(T, LANE), (LANE, T)]
    return pl.pallas_call(
        body, name="ssd_bwd", grid=(nb,),
        in_specs=[pl.BlockSpec((rows, SSD_W + LANE), lambda i: (nb - 1 - i, OFF_Z // (SSD_W + LANE))),
                  pl.BlockSpec((rows, SSD_CONV), rev), _spec(bias), _spec(alog), _spec(dskip_x), _spec(nw),
                  _full(e64.shape), _full(e64t.shape), _full(tril.shape), _full(triu.shape),
                  pl.BlockSpec((rows, SSD_W), rev), pl.BlockSpec((SSD_SUB, SSD_N, SSD_W), lambda i: (nb - 1 - i, 0, 0)),
                  pl.BlockSpec((rows, SSD_W), lambda i: (nb - 1 - i, 1)), pl.BlockSpec(memory_space=pl.ANY),
                  pl.BlockSpec(memory_space=pl.ANY)],
        out_specs=[pl.BlockSpec((rows, SSD_W + LANE), lambda i: (nb - 1 - i, OFF_Z // (SSD_W + LANE))),
                   pl.BlockSpec((rows, SSD_CONV), rev), pl.BlockSpec((8, SSD_W), lambda i: (0, 0))],
        out_shape=[SDS((S, N_PAD), BF16), SDS((S, SSD_CONV), F32), SDS((8, SSD_W), F32)],
        scratch_shapes=[pltpu.VMEM((SSD_N, SSD_W), F32)] + [pltpu.VMEM((SSD_SUB,) + s, F32) for s in sub_scratch],
        input_output_aliases={13: 0},
        compiler_params=_cp(("arbitrary",)),
    )(u, xbc, _arr(bias), _arr(alog), _arr(dskip_x), _arr(nw), _bfc(e64), _bfc(e64t), _bfc(tril), _bfc(triu), y_ssd,
      states, dycat, du, tok)


def _bfc(a):
    return jnp.asarray(a, BF16)


def _outproj_fwd(ycat, wo, x, gate, tok):
    S = x.shape[0]
    tm = min(512, S)

    def body(yc_ref, wo_ref, x_ref, g_ref, tok_ref, xn_ref, y_ref):
        del tok_ref
        y = _mm(_bf(yc_ref[...]), wo_ref[...])
        y_ref[...] = y
        xn_ref[...] = x_ref[...] + g_ref[...] * y

    row = pl.BlockSpec((tm, D_MODEL), lambda i: (i, 0))
    return pl.pallas_call(
        body, name="outproj_fwd", grid=(S // tm,),
        in_specs=[pl.BlockSpec((tm, D_INNER), lambda i: (i, 0)), _full((D_INNER, D_MODEL)), row, _spec(gate),
                  pl.BlockSpec(memory_space=pl.ANY)],
        out_specs=[row, row],
        out_shape=[SDS((S, D_MODEL), F32), SDS((S, D_MODEL), F32)],
        compiler_params=_cp(("parallel",)),
    )(ycat, wo, x, _arr(gate), tok)


def _outproj_bwd(dxn, y, gate, ycat, wo):
    S = dxn.shape[0]
    tm = min(512, S)

    def body(dx_ref, y_ref, g_ref, yc_ref, wo_ref, dyc_ref, gwo_ref, dg_ref, acc):
        @pl.when(pl.program_id(0) == 0)
        def _():
            acc[...] = jnp.zeros_like(acc)
            dg_ref[...] = jnp.zeros_like(dg_ref)

        dxv = dx_ref[...]
        dy = _bf(dxv * g_ref[...])
        dg_ref[0:1, :] += jnp.sum(dxv * y_ref[...], axis=0, keepdims=True)
        dyc_ref[...] = _mm_nt(dy, wo_ref[...])
        acc[...] += _mm_tn(_bf(yc_ref[...]), dy)

        @pl.when(pl.program_id(0) == pl.num_programs(0) - 1)
        def _():
            gwo_ref[...] = acc[...].astype(BF16)

    row = pl.BlockSpec((tm, D_MODEL), lambda i: (i, 0))
    wide = pl.BlockSpec((tm, D_INNER), lambda i: (i, 0))
    return pl.pallas_call(
        body, name="outproj_bwd", grid=(S // tm,),
        in_specs=[row, row, _spec(gate), wide, _full((D_INNER, D_MODEL))],
        out_specs=[wide, _full((D_INNER, D_MODEL)), _full((8, D_MODEL))],
        out_shape=[SDS((S, D_INNER), F32), SDS((D_INNER, D_MODEL), BF16), SDS((8, D_MODEL), F32)],
        scratch_shapes=[pltpu.VMEM((D_INNER, D_MODEL), F32)],
        compiler_params=_cp(("arbitrary",)),
    )(dxn, y, _arr(gate), ycat, wo)


def _loss_head(x, fw, target):
    S = x.shape[0]
    tm = min(512, S)

    def body(x_ref, fw_ref, t_ref, dx_ref, red_ref):
        @pl.when(pl.program_id(0) == 0)
        def _():
            red_ref[...] = jnp.zeros_like(red_ref)

        xv = x_ref[...]
        fwv = fw_ref[...]
        inv = lax.rsqrt(jnp.mean(xv * xv, axis=-1, keepdims=True) + EPS)
        xhat = xv * inv
        err = xhat * fwv - t_ref[...]
        col = jnp.sum(err * err, axis=0, keepdims=True)
        red_ref[1:2, :] += jnp.broadcast_to(jnp.sum(col, axis=1, keepdims=True) * (0.5 / D_MODEL), (1, D_MODEL))
        dy = err * (1.0 / D_MODEL)
        red_ref[0:1, :] += jnp.sum(dy * xhat, axis=0, keepdims=True)
        dxhat = dy * fwv
        dx_ref[...] = inv * (dxhat - xhat * jnp.mean(dxhat * xhat, axis=-1, keepdims=True))

    row = pl.BlockSpec((tm, D_MODEL), lambda i: (i, 0))
    return pl.pallas_call(
        body, name="loss_head", grid=(S // tm,),
        in_specs=[row, _vec(D_MODEL), row],
        out_specs=[row, _full((8, D_MODEL))],
        out_shape=[SDS((S, D_MODEL), F32), SDS((8, D_MODEL), F32)],
        compiler_params=_cp(("arbitrary",)),
    )(x, fw, target)


ADA_COLS = 3 * D_MODEL // N_DEV


def _ada_fwd(c_all, w_ada, b_cols):
    def body(c_ref, w_ref, b_ref, out_ref):
        out_ref[...] = _mm(_bf(_silu(c_ref[...])), _bf(w_ref[...])) + b_ref[...]

    return pl.pallas_call(
        body, name="ada_fwd", grid=(DEPTH,),
        in_specs=[_full((N_DEV, D_MODEL)), pl.BlockSpec((None, D_MODEL, ADA_COLS), lambda l: (l, 0, 0)),
                  pl.BlockSpec((None, 1, ADA_COLS), lambda l: (l, 0, 0))],
        out_specs=pl.BlockSpec((None, N_DEV, ADA_COLS), lambda l: (l, 0, 0)),
        out_shape=SDS((DEPTH, N_DEV, ADA_COLS), F32),
        compiler_params=_cp(("parallel",)),
    )(c_all, w_ada, b_cols)


def _ada_bwd(ct_pad, dmod_pad):
    def body(c_ref, d_ref, out_ref):
        out_ref[...] = _mm(_bf(_silu(c_ref[...])), _bf(d_ref[...]))

    return pl.pallas_call(
        body, name="ada_bwd", grid=(DEPTH,),
        in_specs=[_full((D_MODEL, LANE)), pl.BlockSpec((None, LANE, ADA_COLS), lambda l: (l, 0, 0))],
        out_specs=pl.BlockSpec((None, D_MODEL, ADA_COLS), lambda l: (l, 0, 0)),
        out_shape=SDS((DEPTH, D_MODEL, ADA_COLS), F32),
        compiler_params=_cp(("parallel",)),
    )(ct_pad, dmod_pad)


def _adamw(parts, w, m, v, name, own=None, layers=None, prev=None):
    n, L, R, C = parts.shape
    lo, hi = layers or (0, L)
    tr = R
    while tr * C * 4 > (1 << 20) and tr % 16 == 0:
        tr //= 2
    first = 1 if own is None else 2

    def body(*refs):
        p_ref = refs[0]
        w_ref, m_ref, v_ref = refs[first:first + 3]
        g_ref, d_ref, mo_ref, vo_ref = refs[-4:]

        def part(k):
            if own is None:
                return p_ref[k].astype(F32)
            me = 4 * lax.axis_index("x") + 2 * lax.axis_index("y") + lax.axis_index("c")
            return jnp.where(me == k, refs[1][...], p_ref[k]).astype(F32)

        g = part(0)
        for k in range(1, n):
            g = g + part(k)
        mn = ADAM_B1 * m_ref[...] + (1.0 - ADAM_B1) * g
        vn = ADAM_B2 * v_ref[...] + (1.0 - ADAM_B2) * (g * g)
        m_hat = mn / (1.0 - ADAM_B1 ** ADAM_STEP)
        v_hat = vn / (1.0 - ADAM_B2 ** ADAM_STEP)
        g_ref[...] = g
        d_ref[...] = -ADAM_LR * (m_hat / (jnp.sqrt(v_hat) + ADAM_EPS) + ADAM_WD * w_ref[...])
        mo_ref[...] = mn
        vo_ref[...] = vn

    blk = pl.BlockSpec((None, tr, C), lambda l, i: (lo + l, i, 0))
    own_blk = [] if own is None else [pl.BlockSpec((None, tr, C), lambda l, i: (l, i, 0))]
    n_blk = 3 if own is None else 4
    return pl.pallas_call(
        body, name=name, grid=(hi - lo, R // tr),
        in_specs=[pl.BlockSpec((n, None, tr, C), lambda l, i: (0, lo + l, i, 0))] + own_blk + [blk] * 3
        + ([] if prev is None else [ANY] * 4),
        out_specs=[blk] * 4,
        out_shape=[SDS((L, R, C), F32)] * 4,
        input_output_aliases={} if prev is None else {1 + n_blk + k: k for k in range(4)},
        compiler_params=_cp(("parallel", "parallel")),
    )(parts, *([] if own is None else [own]), w, m, v, *([] if prev is None else prev))


MESH = pl.DeviceIdType.MESH
ANY = pl.BlockSpec(memory_space=pl.ANY)


def _all_gather(v, name):
    def body(v_ref, out_ref, send_sems, recv_sems, local_sem):
        x, y, c = lax.axis_index("x"), lax.axis_index("y"), lax.axis_index("c")
        me, sibling = (x, y, c), (x, y, 1 - c)
        chips = [(1 - x, y), (x, 1 - y), (1 - x, 1 - y)]

        def slot(px, py, pc):
            return out_ref.at[4 * px + 2 * py + pc]

        def copy(k, block, to, src=None):
            return pltpu.make_async_remote_copy(
                src_ref=slot(*block) if src is None else src, dst_ref=slot(*block),
                send_sem=send_sems.at[k], recv_sem=recv_sems.at[k], device_id=to, device_id_type=MESH)

        mine = pltpu.make_async_copy(v_ref, slot(*me), local_sem)
        mine.start()
        first = [copy(0, me, sibling, src=v_ref)]
        first += [copy(1 + j, me, (*chip, c), src=v_ref) for j, chip in enumerate(chips)]
        for cp in first:
            cp.start()
        passed = [copy(4 + j, (*chip, c), sibling) for j, chip in enumerate(chips)]
        for j, chip in enumerate(chips):
            copy(1 + j, (*chip, c), me).wait_recv()
            passed[j].start()
        copy(0, sibling, me).wait_recv()
        for j, chip in enumerate(chips):
            copy(4 + j, (*chip, 1 - c), me).wait_recv()
        for cp in first + passed:
            cp.wait_send()
        mine.wait()

    return pl.pallas_call(
        body, name=name, in_specs=[ANY], out_specs=ANY,
        out_shape=SDS((N_DEV,) + v.shape, v.dtype),
        scratch_shapes=[pltpu.SemaphoreType.DMA((7,)), pltpu.SemaphoreType.DMA((7,)), pltpu.SemaphoreType.DMA],
    )(v)


def _all_to_all(v, name):
    def body(v_ref, out_ref, send_sems, recv_sems, local_sem):
        x, y, c = lax.axis_index("x"), lax.axis_index("y"), lax.axis_index("c")
        mine_idx = 4 * x + 2 * y + c
        mine = pltpu.make_async_copy(v_ref.at[mine_idx], out_ref.at[mine_idx], local_sem)
        mine.start()
        sends, recvs = [], []
        for k in range(1, N_DEV):
            px = 1 - x if k & 4 else x
            py = 1 - y if k & 2 else y
            pc = 1 - c if k & 1 else c
            peer_idx = 4 * px + 2 * py + pc
            sems = dict(send_sem=send_sems.at[k - 1], recv_sem=recv_sems.at[k - 1], device_id=(px, py, pc),
                        device_id_type=MESH)
            sends.append(pltpu.make_async_remote_copy(src_ref=v_ref.at[peer_idx], dst_ref=out_ref.at[mine_idx], **sems))
            recvs.append(pltpu.make_async_remote_copy(src_ref=v_ref.at[peer_idx], dst_ref=out_ref.at[peer_idx], **sems))
        for cp in sends:
            cp.start()
        for cp in recvs:
            cp.wait_recv()
        for cp in sends:
            cp.wait_send()
        mine.wait()

    return pl.pallas_call(
        body, name=name, in_specs=[ANY], out_specs=ANY,
        out_shape=SDS(v.shape, v.dtype),
        scratch_shapes=[pltpu.SemaphoreType.DMA((7,)), pltpu.SemaphoreType.DMA((7,)), pltpu.SemaphoreType.DMA],
    )(v)


HBM_SPEC = pl.BlockSpec(memory_space=pltpu.HBM)
SEM_SPEC = pl.BlockSpec(memory_space=pltpu.SEMAPHORE)
EFFECT = pltpu.SideEffectType.DATAFLOW_SIDE_EFFECTING


EXCHANGE_PEERS = {"gather": range(1, N_DEV), "scatter": range(1, N_DEV), "chip": (1, 2, 4, 6), "pass": (2, 4, 6)}


def _exchange_copies(srcs, lands, send_sems, recv_sems, mode, layer):
    x, y, c = lax.axis_index("x"), lax.axis_index("y"), lax.axis_index("c")
    me = 4 * x + 2 * y + c
    copies = []
    for a, (src, land) in enumerate(zip(srcs, lands)):
        for k in EXCHANGE_PEERS[mode]:
            px = 1 - x if k & 4 else x
            py = 1 - y if k & 2 else y
            pc = 1 - c if k & 1 else c
            peer = 4 * px + 2 * py + pc
            if mode == "scatter":
                s, d, to = src.at[peer], land.at[me, layer], (px, py, pc)
            elif mode == "pass":
                s, d, to = land.at[peer], land.at[peer], (x, y, 1 - c)
            else:
                s, d, to = src, land.at[me], (px, py, pc)
            n = 7 * a + k - 1
            copies.append(pltpu.make_async_remote_copy(
                src_ref=s, dst_ref=d, send_sem=send_sems.at[n], recv_sem=recv_sems.at[n], device_id=to,
                device_id_type=MESH))
    return copies


def _exchange_start(name, srcs, lands, mode, layer=0, after=None):
    n = len(srcs)

    def body(*refs):
        send_sems, recv_sems = refs[-2 * n - 3], refs[-2 * n - 2]
        for cp in _exchange_copies(refs[:n], refs[n:2 * n], send_sems, recv_sems, mode, layer):
            cp.start()
        refs[-1][...] = jnp.zeros_like(refs[-1])

    arrays = list(srcs) + list(lands)
    sems = pltpu.SemaphoreType.DMA((7 * n,))
    out = pl.pallas_call(
        body, name=name,
        out_shape=(sems, sems, *[pltpu.HBM(v.shape, v.dtype) for v in arrays], SDS((8, LANE), F32)),
        in_specs=[HBM_SPEC] * (2 * n) + ([ANY] if after is not None else []),
        out_specs=(SEM_SPEC, SEM_SPEC, *[HBM_SPEC] * (2 * n), pl.BlockSpec(memory_space=pltpu.VMEM)),
        input_output_aliases={i: 2 + i for i in range(2 * n)},
        compiler_params=pltpu.CompilerParams(has_side_effects=EFFECT),
    )(*[pltpu.with_memory_space_constraint(v, pltpu.HBM) for v in arrays], *([after] if after is not None else []))
    return dict(sems=out[:2], srcs=out[2:2 + n], lands=out[2 + n:2 + 2 * n], token=out[-1], mode=mode,
                layer=layer)


def _exchange_wait(name, st, after, also=()):
    n = len(st["srcs"])

    def body(*refs):
        send_sems, recv_sems = refs[2 * n], refs[2 * n + 1]
        for cp in _exchange_copies(refs[:n], refs[n:2 * n], send_sems, recv_sems, st["mode"], st["layer"]):
            cp.wait_send()
            cp.wait_recv()

    arrays = list(st["srcs"]) + list(st["lands"])
    out = pl.pallas_call(
        body, name=name,
        out_shape=tuple(pltpu.HBM(v.shape, v.dtype) for v in arrays),
        in_specs=[HBM_SPEC] * (2 * n) + [SEM_SPEC, SEM_SPEC] + [ANY] * (1 + len(also)),
        out_specs=tuple([HBM_SPEC] * (2 * n)),
        input_output_aliases={i: i for i in range(2 * n)},
        compiler_params=pltpu.CompilerParams(has_side_effects=EFFECT),
    )(*arrays, *st["sems"], after, *also)
    st["srcs"] = out[:n]
    return out[n:]


_IN_PIECES = ([(1024, 3072)]
              + [r for t in range(4) for r in ((LANE * t, LANE * (t + 1)), (512 + LANE * t, 512 + LANE * (t + 1)))]
              + [(4096, 5632), (3072, 4096), (5632, 5648)])


def _permute_in(w):
    pad = jnp.zeros(w.shape[:-1] + (N_PAD - N_IN,), w.dtype)
    return jnp.concatenate([w[..., a:b] for a, b in _IN_PIECES] + [pad], axis=-1)


def _unpermute_in(g):
    ax = [g[..., OFF_LRU + 2 * LANE * t:OFF_LRU + 2 * LANE * t + LANE] for t in range(4)]
    ag = [g[..., OFF_LRU + 2 * LANE * t + LANE:OFF_LRU + 2 * LANE * (t + 1)] for t in range(4)]
    return jnp.concatenate(ax + ag + [g[..., 0:2048], g[..., OFF_Z:OFF_Z + SSD_W], g[..., OFF_XBC:OFF_XBC + SSD_CONV],
                                      g[..., OFF_Z + SSD_W:OFF_Z + SSD_W + SSD_HEADS]], axis=-1)


SHARD_COLS = N_IN // N_DEV


def _in_segments():
    segs, pos = [], 0
    for a, b in _IN_PIECES:
        for i in range(N_DEV):
            lo, hi = max(a, SHARD_COLS * i), min(b, SHARD_COLS * (i + 1))
            if lo < hi:
                segs.append((i, lo - SHARD_COLS * i, hi - lo, pos + lo - a))
        pos += b - a
    return segs


RELAYOUT_ROWS = 512


def _relayout_in(land, own):
    def body(land_ref, own_ref, out_ref):
        me = 4 * lax.axis_index("x") + 2 * lax.axis_index("y") + lax.axis_index("c")
        out_ref[:, N_IN:N_PAD] = jnp.zeros((RELAYOUT_ROWS, N_PAD - N_IN), BF16)
        for i, j, wd, p in _in_segments():
            out_ref[:, p:p + wd] = jnp.where(me == i, own_ref[:, j:j + wd], land_ref[i, :, j:j + wd])

    return pl.pallas_call(
        body, name="relayout_in", grid=(D_MODEL // RELAYOUT_ROWS,),
        in_specs=[pl.BlockSpec((N_DEV, RELAYOUT_ROWS, SHARD_COLS), lambda r: (0, r, 0)),
                  pl.BlockSpec((RELAYOUT_ROWS, SHARD_COLS), lambda r: (r, 0))],
        out_specs=pl.BlockSpec((RELAYOUT_ROWS, N_PAD), lambda r: (r, 0)),
        out_shape=SDS((D_MODEL, N_PAD), BF16),
        compiler_params=_cp(("parallel",)),
    )(land, own)


def _relayout_grad(g):
    def body(g_ref, out_ref):
        for i, j, wd, p in _in_segments():
            out_ref[i, :, j:j + wd] = g_ref[:, p:p + wd].astype(BF16)

    return pl.pallas_call(
        body, name="relayout_grad", grid=(D_MODEL // RELAYOUT_ROWS,),
        in_specs=[pl.BlockSpec((RELAYOUT_ROWS, N_PAD), lambda r: (r, 0))],
        out_specs=pl.BlockSpec((N_DEV, RELAYOUT_ROWS, SHARD_COLS), lambda r: (0, r, 0)),
        out_shape=SDS((N_DEV, D_MODEL, SHARD_COLS), BF16),
        compiler_params=_cp(("parallel",)),
    )(g)


def _block_diag(w):
    w4 = w.reshape(DEPTH, 4, 2, 64, 64)
    z = jnp.zeros((DEPTH, 4, 64, 64), w.dtype)
    top = jnp.concatenate([w4[:, :, 0], z], axis=-1)
    bot = jnp.concatenate([z, w4[:, :, 1]], axis=-1)
    return jnp.concatenate([top, bot], axis=2).astype(BF16)


def _diag_blocks(g):
    return jnp.stack([g[:, :, :64, :64], g[:, :, 64:, 64:]], axis=2).reshape(DEPTH, 8, 64, 64)


def _pad_lanes(v):
    return jnp.pad(v, ((0, 0), (0, LANE - v.shape[1])))


def _lower_bounds(logits):
    p = jax.nn.softmax(logits, axis=0)
    return p, jnp.cumsum(p, axis=0) - p[0]


def _lower_bounds_bwd(p, dlb):
    dp = jnp.cumsum(dlb[::-1], axis=0)[::-1]
    dp = dp.at[0].add(-jnp.sum(dlb, axis=0))
    return p * (dp - jnp.sum(dp * p, axis=0, keepdims=True))


SMALL = ["norm_w", "b_ada", "lru_conv_b", "lru_wa", "lru_ba", "lru_wx", "lru_bx", "lru_lambda", "hg_lb_logits",
         "hg_norm_w", "ssd_conv_b", "ssd_dt_bias", "ssd_a_log", "ssd_d", "ssd_norm_w", "final_norm_w"]
WEIGHTS = ["norm_w", "w_ada", "b_ada", "w_in", "lru_conv_w", "lru_conv_b", "lru_wa", "lru_ba", "lru_wx", "lru_bx",
           "lru_lambda", "hg_lb_logits", "hg_norm_w", "ssd_conv_w", "ssd_conv_b", "ssd_dt_bias", "ssd_a_log", "ssd_d",
           "ssd_norm_w", "w_out", "final_norm_w"]
INPUTS = ["x", "c"] + WEIGHTS + ["loss_target"] + ["m_" + n for n in WEIGHTS] + ["v_" + n for n in WEIGHTS]
SMALL_ROW = 1024


def _small_rows(like):
    out, off = {}, 0
    for n in SMALL:
        rows = -(-int(np.prod(like[n].shape)) // (8 * SMALL_ROW)) * 8
        out[n] = (off, rows)
        off += rows
    return out, off


def _flatten_small(d, prefix="", last=0.0):
    table, _ = _small_rows({n: d[prefix + n] for n in SMALL})
    pieces = []
    for n in SMALL:
        flat = d[prefix + n].reshape(-1)
        pieces.append(jnp.pad(flat, (0, table[n][1] * SMALL_ROW - flat.shape[0])).reshape(-1, SMALL_ROW))
    return jnp.concatenate(pieces + [jnp.full((8, SMALL_ROW), last, F32)], axis=0)


def _split_small(packed, like):
    table, _ = _small_rows(like)
    out = {}
    for n in SMALL:
        off, rows = table[n]
        size = int(np.prod(like[n].shape))
        out[n] = packed[off:off + rows].reshape(-1)[:size].reshape(like[n].shape)
    return out


def _local_step(x, mod, target, w, fetch, emit):
    S = x.shape[0]
    mall = _bfc(_hg_consts())
    mall_t = _bfc(_hg_consts().T)
    consts = _ssd_consts()
    p_lb, lbs = _lower_bounds(w["hg_lb_logits"])
    no_tok = jnp.zeros((8, LANE), F32)
    wa, wx = _block_diag(w["lru_wa"]), _block_diag(w["lru_wx"])
    ba, bx = w["lru_ba"].reshape(DEPTH, 1, LRU_W), w["lru_bx"].reshape(DEPTH, 1, LRU_W)
    lru_cb, lam, ssd_cb = w["lru_conv_b"][:, None], w["lru_lambda"][:, None], w["ssd_conv_b"][:, None]
    bias, alog = _pad_lanes(w["ssd_dt_bias"]), _pad_lanes(w["ssd_a_log"])
    dskip = jnp.repeat(w["ssd_d"], SSD_P, axis=1)
    saved = []
    for l in range(DEPTH):
        w_in_l, w_out_l, token = fetch(l, x)
        shift, scale, gate = (_Row(mod, l, D_MODEL, k) for k in range(3))
        nw = _Row(w["norm_w"], l)
        u, h = _inproj_fwd(x, nw, scale, shift, w_in_l, no_tok if token is None else token)
        ycat = lax.empty((S, D_INNER), BF16)
        lru_args = (l, u, w["lru_conv_w"], lru_cb, wa, ba, wx, bx, lam)
        ycat, h_lru = _lru_fwd(*lru_args, ycat)
        hg_args = (u, _Row(lbs, l), _Row(w["hg_norm_w"], l), mall)
        ycat, o_b, hg_st = _hg_fwd(*hg_args, ycat)
        xbc = _ssdconv_fwd(l, u, w["ssd_conv_w"], ssd_cb)
        ssd_args = (u, xbc, _Row(bias, l), _Row(alog, l), _Row(dskip, l), _Row(w["ssd_norm_w"], l), consts)
        ycat, y_ssd, ssd_st = _ssd_fwd(*ssd_args, ycat)
        token = fetch(l, y_ssd, late=True)
        x_new, y = _outproj_fwd(ycat, w_out_l, x, gate, no_tok if token is None else token)
        saved.append((x, u, h, ycat, nw, scale, gate, w_in_l, w_out_l, lru_args, h_lru, hg_args, o_b, hg_st, ssd_args,
                      y_ssd, ssd_st, y))
        x = x_new
    dx, red = _loss_head(x, w["final_norm_w"][None, :], target)
    loss = red[1, 0]
    reds = {k: [None] * DEPTH for k in ("in", "gate", "lru", "wa", "wx", "hg", "conv", "ssd")}
    for l in reversed(range(DEPTH)):
        (x, u, h, ycat, nw, scale, gate, w_in_l, w_out_l, lru_args, h_lru, hg_args, o_b, hg_st, ssd_args, y_ssd, ssd_st,
         y) = saved[l]
        dycat, g_out, reds["gate"][l] = _outproj_bwd(dx, y, gate, ycat, w_out_l)
        token = emit(l, "w_out", g_out)
        du = lax.empty((S, N_PAD), BF16)
        du, dxbc, reds["ssd"][l] = _ssd_bwd(*ssd_args, y_ssd, ssd_st, dycat, du, no_tok if token is None else token)
        du, reds["conv"][l] = _ssdconv_bwd(l, u, w["ssd_conv_w"], ssd_cb, dxbc, du)
        du, reds["hg"][l] = _hg_bwd(*hg_args, mall_t, o_b, hg_st, dycat, du)
        du, reds["lru"][l], reds["wa"][l], reds["wx"][l] = _lru_bwd(*lru_args, h_lru, dycat, du)
        token = emit(l, "w_in", functools.partial(_inproj_bwd_w, h, du))
        dx, reds["in"][l] = _inproj_bwd_x(du, w_in_l, x, nw, scale, dx, no_tok if token is None else token)
    r = {k: jnp.stack(v) for k, v in reds.items()}
    g = {n: None for n in WEIGHTS}
    g["final_norm_w"] = red[0]
    g["norm_w"] = r["in"][:, 2]
    dmod = jnp.concatenate([r["in"][:, 0], r["in"][:, 1], r["gate"][:, 0]], axis=1)
    g["lru_conv_w"], g["lru_conv_b"] = r["lru"][:, 0:4], r["lru"][:, 4]
    g["lru_ba"], g["lru_bx"] = r["lru"][:, 5].reshape(DEPTH, 8, 64), r["lru"][:, 6].reshape(DEPTH, 8, 64)
    g["lru_lambda"] = r["lru"][:, 7]
    g["lru_wa"], g["lru_wx"] = _diag_blocks(r["wa"]), _diag_blocks(r["wx"])
    g["hg_norm_w"] = r["hg"][:, 0]
    g["hg_lb_logits"] = _lower_bounds_bwd(p_lb, r["hg"][:, 1])
    g["ssd_conv_w"], g["ssd_conv_b"] = r["conv"][:, 0:4], r["conv"][:, 4]
    g["ssd_norm_w"] = r["ssd"][:, 0]
    g["ssd_d"] = r["ssd"][:, 1].reshape(DEPTH, SSD_HEADS, SSD_P).sum(-1)
    g["ssd_dt_bias"] = r["ssd"][:, 2, :SSD_HEADS]
    g["ssd_a_log"] = -r["ssd"][:, 3, :SSD_HEADS] * jnp.exp(w["ssd_a_log"])
    return loss, dx, dmod, g


def kernel(x, c, norm_w, w_ada, b_ada, w_in, lru_conv_w, lru_conv_b, lru_wa, lru_ba, lru_wx, lru_bx, lru_lambda, hg_lb_logits, hg_norm_w, ssd_conv_w, ssd_conv_b, ssd_dt_bias, ssd_a_log, ssd_d, ssd_norm_w, w_out, final_norm_w, loss_target, m_norm_w, m_w_ada, m_b_ada, m_w_in, m_lru_conv_w, m_lru_conv_b, m_lru_wa, m_lru_ba, m_lru_wx, m_lru_bx, m_lru_lambda, m_hg_lb_logits, m_hg_norm_w, m_ssd_conv_w, m_ssd_conv_b, m_ssd_dt_bias, m_ssd_a_log, m_ssd_d, m_ssd_norm_w, m_w_out, m_final_norm_w, v_norm_w, v_w_ada, v_b_ada, v_w_in, v_lru_conv_w, v_lru_conv_b, v_lru_wa, v_lru_ba, v_lru_wx, v_lru_bx, v_lru_lambda, v_hg_lb_logits, v_hg_norm_w, v_ssd_conv_w, v_ssd_conv_b, v_ssd_dt_bias, v_ssd_a_log, v_ssd_d, v_ssd_norm_w, v_w_out, v_final_norm_w):
    return _step(x, c, norm_w, w_ada, b_ada, w_in, lru_conv_w, lru_conv_b, lru_wa, lru_ba, lru_wx, lru_bx, lru_lambda, hg_lb_logits, hg_norm_w, ssd_conv_w, ssd_conv_b, ssd_dt_bias, ssd_a_log, ssd_d, ssd_norm_w, w_out, final_norm_w, loss_target, m_norm_w, m_w_ada, m_b_ada, m_w_in, m_lru_conv_w, m_lru_conv_b, m_lru_wa, m_lru_ba, m_lru_wx, m_lru_bx, m_lru_lambda, m_hg_lb_logits, m_hg_norm_w, m_ssd_conv_w, m_ssd_conv_b, m_ssd_dt_bias, m_ssd_a_log, m_ssd_d, m_ssd_norm_w, m_w_out, m_final_norm_w, v_norm_w, v_w_ada, v_b_ada, v_w_in, v_lru_conv_w, v_lru_conv_b, v_lru_wa, v_lru_ba, v_lru_wx, v_lru_bx, v_lru_lambda, v_hg_lb_logits, v_hg_norm_w, v_ssd_conv_w, v_ssd_conv_b, v_ssd_dt_bias, v_ssd_a_log, v_ssd_d, v_ssd_norm_w, v_w_out, v_final_norm_w)


def _step(*args):
    a = dict(zip(INPUTS, args, strict=True))
    me = 4 * lax.axis_index("x") + 2 * lax.axis_index("y") + lax.axis_index("c")
    x, target = a["x"][0], a["loss_target"][0]

    c_all = _all_gather(a["c"], "gather_c")[:, 0, :]
    b_cols = lax.dynamic_slice_in_dim(a["b_ada"], me * ADA_COLS, ADA_COLS, axis=1)[:, None, :]
    mod_parts = _all_gather(_ada_fwd(c_all, a["w_ada"], b_cols), "gather_mod")
    mod = lax.dynamic_index_in_dim(mod_parts, me, axis=2, keepdims=False)
    mod = mod.transpose(1, 0, 2).reshape(DEPTH, 3 * D_MODEL)

    w = {n: a[n] for n in SMALL}

    w_in_b = [a["w_in"][l].astype(BF16) for l in range(DEPTH)]
    w_out_b = a["w_out"].astype(BF16)
    conv_own = jnp.concatenate([a["lru_conv_w"], a["ssd_conv_w"]], axis=-1)
    cols, rows_out = N_IN // N_DEV, D_INNER // N_DEV

    def gather_start(l, after):
        srcs = [w_in_b[l], w_out_b[l]] + ([conv_own] if l == 0 else [])
        lands = [lax.empty((N_DEV,) + s.shape, s.dtype) for s in srcs]
        return _exchange_start(f"gather_start_{l}", srcs, lands, "chip", after=after)

    def gather_pass(name, st, after, also=()):
        landed = _exchange_wait(name + "_wait", st, after, also)
        st2 = _exchange_start(name + "_pass", st["srcs"], landed, "pass")
        return _exchange_wait(name + "_passed", st2, after)

    gathers = {0: gather_start(0, c_all)}
    passing = {}

    def fetch(l, x_l, late=False):
        if late:
            if l + 1 == DEPTH:
                return None
            landed = _exchange_wait(f"gather_{l + 1}_wait", gathers[l + 1], x_l)
            passing[l + 1] = _exchange_start(f"gather_{l + 1}_pass", gathers[l + 1]["srcs"], landed, "pass")
            return passing[l + 1]["token"]
        if l == 0:
            landed = gather_pass("gather_0", gathers[0], x_l, also=(a["w_in"], a["m_w_in"], a["v_w_in"]))
        else:
            landed = _exchange_wait(f"gather_{l}_passed", passing[l], x_l)
        land_out = lax.dynamic_update_index_in_dim(landed[1], w_out_b[l], me, 0)
        if l == 0:
            conv = lax.dynamic_update_index_in_dim(landed[2], conv_own, me, 0).transpose(1, 2, 0, 3)
            w["lru_conv_w"] = conv[..., :64].reshape(DEPTH, 4, LRU_W)
            w["ssd_conv_w"] = conv[..., 64:].reshape(DEPTH, 4, SSD_CONV)
        token = None
        if l + 1 < DEPTH:
            gathers[l + 1] = gather_start(l + 1, land_out)
            token = gathers[l + 1]["token"]
        return _relayout_in(landed[0], w_in_b[l]), land_out.reshape(D_INNER, D_MODEL), token

    PROJ = ("w_in", "w_out")
    scatters = {}
    lands = [lax.empty((N_DEV, DEPTH, D_MODEL, cols), BF16), lax.empty((N_DEV, DEPTH, rows_out, D_MODEL), BF16)]
    own = [None] * DEPTH

    deferred, g_out = {}, {}

    def emit(l, name, grad, after=None):
        if name == "w_out" and l > 0:
            g_out[l] = grad
            return None
        if name == "w_in" and l == 0 and after is None:
            deferred["w_in"] = grad
            return None
        if name == "w_in":
            grad = grad(jnp.zeros((8, LANE), F32) if after is None else after)
        if l == 0:
            k = PROJ.index(name)
            src = _relayout_grad(grad) if name == "w_in" else grad.reshape(N_DEV, rows_out, D_MODEL)
            st = _exchange_start(f"scatter_start_0_{name}", [src], [lands[k]], "scatter", layer=0, after=after)
            scatters[name] = st
            lands[k] = st["lands"][0]
            return st["token"]
        srcs = [_relayout_grad(grad), g_out[l].reshape(N_DEV, rows_out, D_MODEL)]
        st = _exchange_start(f"scatter_start_{l}", srcs, lands, "scatter", layer=l, after=after)
        scatters[l] = st
        lands[:] = st["lands"]
        return st["token"]

    loss_own, dx, dmod, g = _local_step(x, mod, target, w, fetch, emit)

    def sharded(name, parts, own=None, **kw):
        return _adamw(parts, a[name], a["m_" + name], a["v_" + name], "adamw_" + name + kw.pop("tag", ""), own=own, **kw)

    g["b_ada"] = dmod
    small_own = _flatten_small(g, last=loss_own)
    small_st = _exchange_start("gather_small", [small_own], [lax.empty((N_DEV,) + small_own.shape, F32)], "chip",
                               after=dx)
    big = {}
    after = emit(0, "w_in", deferred["w_in"], after=small_st["token"]) + dx[0:8, 0:LANE]

    def own_slices(st):
        return [lax.dynamic_index_in_dim(s, me, 0, keepdims=False) for s in st["srcs"]]

    for l in reversed(range(1, DEPTH)):
        scatters[l]["lands"] = lands
        lands[:] = _exchange_wait(f"scatter_wait_{l}", scatters[l], after)
        own[l] = own_slices(scatters[l])
    scatters["w_out"]["lands"] = [lands[1]]
    lands[1] = _exchange_wait("scatter_wait_0_w_out", scatters["w_out"], after)[0]
    own[0] = [None, own_slices(scatters["w_out"])[0]]
    big["w_out"] = sharded("w_out", lands[1], jnp.stack([own[l][1] for l in range(DEPTH)]))
    upper = sharded("w_in", lands[0], jnp.stack([own[l][0] for l in range(1, DEPTH)]), layers=(1, DEPTH), tag="_upper")
    after = upper[1][0, 0:8, 0:LANE] + big["w_out"][1][0, 0:8, 0:LANE]
    small = gather_pass("gather_small", small_st, after)[0]
    outs = _adamw(small[:, None], *[_flatten_small(a, p)[None] for p in ("", "m_", "v_")], "adamw_small",
                  own=small_own[None])
    res = [_split_small(o[0], a) for o in outs]
    losses = lax.dynamic_update_index_in_dim(small[:, -1, 0], loss_own, me, 0)
    loss = jnp.sum(losses)

    off = _small_rows(a)[0]["b_ada"][0]
    dmod_all = lax.dynamic_update_index_in_dim(small[:, off:off + DEPTH * 3 * D_MODEL // SMALL_ROW],
                                               dmod.reshape(-1, SMALL_ROW), me, 0)
    dmod_all = dmod_all.reshape(N_DEV, DEPTH, 3 * D_MODEL).transpose(1, 0, 2)
    dmod_cols = lax.dynamic_slice_in_dim(dmod_all, me * ADA_COLS, ADA_COLS, axis=2)
    dmod_pad = jnp.pad(dmod_cols, ((0, 0), (0, LANE - N_DEV), (0, 0)))
    ct_pad = jnp.pad(c_all.T, ((0, 0), (0, LANE - N_DEV)))
    big["w_ada"] = sharded("w_ada", _ada_bwd(ct_pad, dmod_pad)[None])
    g_conv = jnp.concatenate([g["lru_conv_w"].reshape(DEPTH, 4, N_DEV, 64), g["ssd_conv_w"].reshape(DEPTH, 4, N_DEV, 192)],
                             axis=-1).transpose(2, 0, 1, 3)
    conv_parts = _all_to_all(g_conv, "scatter_conv")
    big["lru_conv_w"] = sharded("lru_conv_w", conv_parts[..., :64])
    big["ssd_conv_w"] = sharded("ssd_conv_w", conv_parts[..., 64:])

    after = outs[1] + big["w_ada"][1][0, 0:1, 0:1]
    scatters["w_in"]["lands"] = [lands[0]]
    lands[0] = _exchange_wait("scatter_wait_0_w_in", scatters["w_in"], after)[0]
    big["w_in"] = sharded("w_in", lands[0], own_slices(scatters["w_in"])[0][None], layers=(0, 1), prev=upper)

    out = [loss, dx[None]]
    for k in range(4):
        out += [big[n][k] if n in big else res[k][n] for n in WEIGHTS]
    return tuple(out)
```

```python
import functools

import numpy as np
import jax
import jax.numpy as jnp
from jax import lax
from jax.experimental import pallas as pl
from jax.experimental.pallas import tpu as pltpu

F32 = jnp.float32
BF16 = jnp.bfloat16
SDS = jax.ShapeDtypeStruct

N_DEV = 8
DEPTH = 4
D_MODEL = 1024
D_INNER = 2048
EPS = 1e-6
LRU_W = 512
LRU_C = 8.0
HG_W = 512
HG_CHUNK = 64
HG_HEADS = 4
SSD_W = 1024
SSD_HEADS = 16
SSD_P = 64
SSD_N = 128
SSD_CHUNK = 128
SSD_CONV = 1536
N_IN = 5648
N_PAD = 5760
OFF_HG = 0
OFF_LRU = 2048
OFF_XBC = 3072
OFF_Z = 4608
LANE = 128
VMEM_LIMIT = 56 * 1024 * 1024
NEG = -1e30

ADAM_LR = 0.001
ADAM_B1 = 0.9
ADAM_B2 = 0.999
ADAM_EPS = 1e-08
ADAM_WD = 0.01
ADAM_STEP = 10


def _cp(sem=None):
    return pltpu.CompilerParams(dimension_semantics=sem, vmem_limit_bytes=VMEM_LIMIT)


def _dg(a, b, ca, cb):
    return lax.dot_general(a, b, (((ca,), (cb,)), ((), ())), preferred_element_type=F32)


def _mm(a, b):
    return _dg(a, b, 1, 0)


def _mm_nt(a, b):
    return _dg(a, b, 1, 1)


def _mm_tn(a, b):
    return _dg(a, b, 0, 0)


def _bf(x):
    return x.astype(BF16)


def _f(x):
    return x.astype(F32)


def _split3(x):
    hi = x.astype(BF16)
    r = x - hi.astype(F32)
    mid = r.astype(BF16)
    lo = (r - mid.astype(F32)).astype(BF16)
    return hi, mid, lo


def _sel_r(x, m):
    hi, mid, lo = _split3(x)
    return _mm(hi, m) + _mm(mid, m) + _mm(lo, m)


def _sel_l(m, x):
    hi, mid, lo = _split3(x)
    return _mm(m, hi) + _mm(m, mid) + _mm(m, lo)


def _sel_l2(m, x):
    hi = x.astype(BF16)
    lo = (x - hi.astype(F32)).astype(BF16)
    return _mm(m, hi) + _mm(m, lo)


def _sel_tn(x, m):
    hi, mid, lo = _split3(x)
    return _mm_tn(hi, m) + _mm_tn(mid, m) + _mm_tn(lo, m)


def _sigmoid(x):
    return 1.0 / (1.0 + jnp.exp(-x))


def _silu(x):
    return x * _sigmoid(x)


def _dsilu(x):
    s = _sigmoid(x)
    return s * (1.0 + x * (1.0 - s))


def _softplus(x):
    return jnp.maximum(x, 0.0) + jnp.log(1.0 + jnp.exp(-jnp.abs(x)))


def _expm1(z):
    series = z * (1.0 + z * (1.0 / 2) * (1.0 + z * (1.0 / 3) * (1.0 + z * (1.0 / 4) * (
        1.0 + z * (1.0 / 5) * (1.0 + z * (1.0 / 6) * (1.0 + z * (1.0 / 7)))))))
    return jnp.where(jnp.abs(z) < 0.3, series, jnp.exp(z) - 1.0)


def _iota(shape, dim):
    return lax.broadcasted_iota(jnp.int32, shape, dim)


def _last_row(x, rows):
    return jnp.sum(jnp.where(rows == x.shape[0] - 1, x, 0.0), axis=0, keepdims=True)


def _shift_down(x, d, rows, fill=0.0):
    return jnp.where(rows >= d, pltpu.roll(x, d, 0), fill)


def _shift_up(x, d, rows, fill=0.0):
    n = x.shape[0]
    return jnp.where(rows < n - d, pltpu.roll(x, n - d, 0), fill)


def _conv_fwd(x, cw_ref, cb_ref, rows):
    out = cb_ref[...] + cw_ref[pl.ds(3, 1), :] * x
    for k in range(3):
        out = out + cw_ref[pl.ds(k, 1), :] * _shift_down(x, 3 - k, rows)
    return out


def _conv_bwd(x, dco, cw_ref, rows):
    dx = cw_ref[pl.ds(3, 1), :] * dco
    dws = []
    for k in range(3):
        dx = dx + cw_ref[pl.ds(k, 1), :] * _shift_up(dco, 3 - k, rows)
        dws.append(jnp.sum(dco * _shift_down(x, 3 - k, rows), axis=0, keepdims=True))
    dws.append(jnp.sum(dco * x, axis=0, keepdims=True))
    return dx, dws, jnp.sum(dco, axis=0, keepdims=True)


def _vec(n):
    return pl.BlockSpec((1, n), lambda *_: (0, 0))


class _Row:
    def __init__(self, arr, l, n=None, c=0):
        self.arr, self.l, self.n, self.c = arr[:, None, :], l, n or arr.shape[1], c


def _spec(v):
    if isinstance(v, _Row):
        return pl.BlockSpec((None, 1, v.n), lambda *_: (v.l, 0, v.c))
    return _vec(v.shape[1])


def _arr(v):
    return v.arr if isinstance(v, _Row) else v


def _full(shape):
    nd = len(shape)
    return pl.BlockSpec(shape, lambda *_: (0,) * nd)


def _inproj_fwd(x, nw, scale, shift, w, tok):
    S = x.shape[0]
    tm = min(256, S)

    def body(x_ref, nw_ref, sc_ref, sh_ref, w_ref, tok_ref, u_ref, h_ref):
        del tok_ref
        xv = x_ref[...]
        inv = lax.rsqrt(jnp.mean(xv * xv, axis=-1, keepdims=True) + EPS)
        h = ((xv * inv) * nw_ref[...] * (1.0 + sc_ref[...]) + sh_ref[...]).astype(BF16)
        h_ref[...] = h
        u_ref[...] = _mm(h, w_ref[...])

    return pl.pallas_call(
        body, name="inproj_fwd", grid=(S // tm,),
        in_specs=[pl.BlockSpec((tm, D_MODEL), lambda i: (i, 0)), _spec(nw), _spec(scale), _spec(shift),
                  _full((D_MODEL, N_PAD)), pl.BlockSpec(memory_space=pl.ANY)],
        out_specs=[pl.BlockSpec((tm, N_PAD), lambda i: (i, 0)), pl.BlockSpec((tm, D_MODEL), lambda i: (i, 0))],
        out_shape=[SDS((S, N_PAD), F32), SDS((S, D_MODEL), BF16)],
        compiler_params=_cp(("parallel",)),
    )(x, _arr(nw), _arr(scale), _arr(shift), w, tok)


def _inproj_bwd_x(du, w, x, nw, scale, dxn, tok):
    S = x.shape[0]
    tm = min(256, S)

    def body(du_ref, w_ref, x_ref, nw_ref, sc_ref, dxn_ref, tok_ref, dx_ref, red_ref):
        del tok_ref

        @pl.when(pl.program_id(0) == 0)
        def _():
            red_ref[...] = jnp.zeros_like(red_ref)

        dh = _mm_nt(du_ref[...], w_ref[...])
        xv = x_ref[...]
        inv = lax.rsqrt(jnp.mean(xv * xv, axis=-1, keepdims=True) + EPS)
        xhat = xv * inv
        nwv = nw_ref[...]
        g1 = 1.0 + sc_ref[...]
        dxhat = dh * nwv * g1
        dx = inv * (dxhat - xhat * jnp.mean(dxhat * xhat, axis=-1, keepdims=True))
        dx_ref[...] = dxn_ref[...] + dx
        red_ref[0:1, :] += jnp.sum(dh, axis=0, keepdims=True)
        red_ref[1:2, :] += jnp.sum(dh * xhat * nwv, axis=0, keepdims=True)
        red_ref[2:3, :] += jnp.sum(dh * xhat * g1, axis=0, keepdims=True)

    row = pl.BlockSpec((tm, D_MODEL), lambda i: (i, 0))
    return pl.pallas_call(
        body, name="inproj_bwd_x", grid=(S // tm,),
        in_specs=[pl.BlockSpec((tm, N_PAD), lambda i: (i, 0)), _full((D_MODEL, N_PAD)), row, _spec(nw),
                  _spec(scale), row, pl.BlockSpec(memory_space=pl.ANY)],
        out_specs=[row, _full((8, D_MODEL))],
        out_shape=[SDS((S, D_MODEL), F32), SDS((8, D_MODEL), F32)],
        compiler_params=_cp(("arbitrary",)),
    )(du, w, x, _arr(nw), _arr(scale), dxn, tok)


def _inproj_bwd_w(h, du, tok):
    S = h.shape[0]
    tn = 640

    def body(h_ref, du_ref, tok_ref, gw_ref):
        del tok_ref
        gw_ref[...] = _mm_tn(h_ref[...], _bf(du_ref[...]))

    return pl.pallas_call(
        body, name="inproj_bwd_w", grid=(N_PAD // tn,),
        in_specs=[_full((S, D_MODEL)), pl.BlockSpec((S, tn), lambda j: (0, j)), pl.BlockSpec(memory_space=pl.ANY)],
        out_specs=pl.BlockSpec((D_MODEL, tn), lambda j: (0, j)),
        out_shape=SDS((D_MODEL, N_PAD), F32),
        compiler_params=_cp(("parallel",)),
    )(h, du, tok)


def _scan_block(a, b, rows):
    d = 1
    while d < a.shape[0]:
        a_s = _shift_down(a, d, rows, 1.0)
        b_s = _shift_down(b, d, rows, 0.0)
        b = a * b_s + b
        a = a * a_s
        d *= 2
    return a, b


def _rscan_block(c, g, rows):
    d = 1
    while d < c.shape[0]:
        c_s = _shift_up(c, d, rows, 1.0)
        g_s = _shift_up(g, d, rows, 0.0)
        g = g + c * g_s
        c = c * c_s
        d *= 2
    return c, g


LRU_BLOCK = 128


def _lru_gates(xa, wa_ref, ba_ref, wx_ref, bx_ref, lam_ref):
    sp = _softplus(-lam_ref[...])
    xb = _bf(xa)
    r = _sigmoid(_mm(xb, wa_ref[...]) + ba_ref[...])
    ig = _sigmoid(_mm(xb, wx_ref[...]) + bx_ref[...])
    la = -LRU_C * r * sp
    a = jnp.exp(la)
    mult = jnp.sqrt(-_expm1(2.0 * la))
    return sp, r, ig, la, a, mult


def _lru_specs(S, l):
    t128 = pl.BlockSpec((None, 1, LANE), lambda t: (l, 0, t))
    gate = pl.BlockSpec((None, None, LANE, LANE), lambda t: (l, t, 0, 0))
    return [pl.BlockSpec((S, 2 * LANE), lambda t: (0, OFF_LRU // (2 * LANE) + t)),
            pl.BlockSpec((None, 4, LANE), lambda t: (l, 0, t)), t128, gate, t128, gate, t128, t128]


def _lru_fwd(l, u, cw, cb, wa, ba, wx, bx, lam, ycat):
    S = u.shape[0]
    tb = min(LRU_BLOCK, S)

    def body(u_ref, cw_ref, cb_ref, wa_ref, ba_ref, wx_ref, bx_ref, lam_ref, ycat_in, ycat_ref, h_ref, a_scr, b_scr):
        del ycat_in
        rows = _iota((S, LANE), 0)
        xa = _conv_fwd(_f(u_ref[:, 0:LANE]), cw_ref, cb_ref, rows)
        _, _, ig, _, a, mult = _lru_gates(xa, wa_ref, ba_ref, wx_ref, bx_ref, lam_ref)
        a_scr[...] = a
        b_scr[...] = mult * (ig * xa)
        rows_b = _iota((tb, LANE), 0)

        def blk(j, hprev):
            sl = pl.ds(pl.multiple_of(j * tb, tb), tb)
            acum, hloc = _scan_block(a_scr[sl, :], b_scr[sl, :], rows_b)
            hf = hloc + acum * hprev
            h_ref[sl, :] = hf
            return _last_row(hf, rows_b)

        lax.fori_loop(0, S // tb, blk, jnp.zeros((1, LANE), F32))
        ycat_ref[...] = _bf(h_ref[...] * _silu(_f(u_ref[:, LANE:2 * LANE])))

    col = pl.BlockSpec((S, LANE), lambda t: (0, t))
    return pl.pallas_call(
        body, name="lru_fwd", grid=(LRU_W // LANE,),
        in_specs=_lru_specs(S, l) + [pl.BlockSpec(memory_space=pl.ANY)],
        out_specs=[col, col],
        out_shape=[SDS((S, D_INNER), BF16), SDS((S,LRU_W), F32)],
        scratch_shapes=[pltpu.VMEM((S, LANE), F32), pltpu.VMEM((S, LANE), F32)],
        input_output_aliases={8: 0},
        compiler_params=_cp(("parallel",)),
    )(u, cw, cb, wa, ba, wx, bx, lam, ycat)


def _lru_bwd(l, u, cw, cb, wa, ba, wx, bx, lam, h_lru, dycat, du):
    S = u.shape[0]
    tb = min(LRU_BLOCK, S)

    def body(u_ref, cw_ref, cb_ref, wa_ref, ba_ref, wx_ref, bx_ref, lam_ref, h_ref, dy_ref, du_in,
             du_ref, red_ref, gwa_ref, gwx_ref, c_scr, g_scr, l_scr):
        del du_in
        rows = _iota((S, LANE), 0)
        ax = _f(u_ref[:, 0:LANE])
        ag = _f(u_ref[:, LANE:2 * LANE])
        xa = _conv_fwd(ax, cw_ref, cb_ref, rows)
        sp, r, ig, la, a, mult = _lru_gates(xa, wa_ref, ba_ref, wx_ref, bx_ref, lam_ref)
        h = h_ref[...]
        dy = _f(dy_ref[...])
        du_ref[:, LANE:2 * LANE] = _bf(dy * h * _dsilu(ag))
        c_scr[...] = _shift_up(a, 1, rows, 0.0)
        g_scr[...] = dy * _silu(ag)
        rows_b = _iota((tb, LANE), 0)
        nb = S // tb

        def blk(jj, lnext):
            j = nb - 1 - jj
            sl = pl.ds(pl.multiple_of(j * tb, tb), tb)
            ccum, lloc = _rscan_block(c_scr[sl, :], g_scr[sl, :], rows_b)
            lam_t = lloc + ccum * lnext
            l_scr[sl, :] = lam_t
            return jnp.sum(jnp.where(rows_b == 0, lam_t, 0.0), axis=0, keepdims=True)

        lax.fori_loop(0, nb, blk, jnp.zeros((1, LANE), F32))
        db = l_scr[...]
        da = db * _shift_down(h, 1, rows)
        dmult = db * ig * xa
        dig = db * mult * xa
        dxa = db * mult * ig
        dla = da * a - dmult * (a * a) / mult
        dr = -LRU_C * sp * dla
        dsp = jnp.sum(-LRU_C * r * dla, axis=0, keepdims=True)
        dlam = -dsp * _sigmoid(-lam_ref[...])
        dzr = dr * r * (1.0 - r)
        dzi = dig * ig * (1.0 - ig)
        dzr_b, dzi_b, xa_b = _bf(dzr), _bf(dzi), _bf(xa)
        dxa = dxa + _mm_nt(dzr_b, wa_ref[...]) + _mm_nt(dzi_b, wx_ref[...])
        gwa_ref[...] = _mm_tn(xa_b, dzr_b)
        gwx_ref[...] = _mm_tn(xa_b, dzi_b)
        dax, dws, dcb = _conv_bwd(ax, dxa, cw_ref, rows)
        du_ref[:, 0:LANE] = _bf(dax)
        parts = dws + [dcb, jnp.sum(dzr, axis=0, keepdims=True), jnp.sum(dzi, axis=0, keepdims=True), dlam]
        for n, p in enumerate(parts):
            red_ref[pl.ds(n, 1), :] = p

    col = pl.BlockSpec((S, LANE), lambda t: (0, t))
    gw = pl.BlockSpec((None, LANE, LANE), lambda t: (t, 0, 0))
    return pl.pallas_call(
        body, name="lru_bwd", grid=(LRU_W // LANE,),
        in_specs=_lru_specs(S, l) + [col, col, pl.BlockSpec(memory_space=pl.ANY)],
        out_specs=[pl.BlockSpec((S, 2 * LANE), lambda t: (0, OFF_LRU // (2 * LANE) + t)),
                   pl.BlockSpec((8, LANE), lambda t: (0, t)), gw, gw],
        out_shape=[SDS((S, N_PAD), BF16), SDS((8, LRU_W), F32), SDS((4, LANE, LANE), F32), SDS((4, LANE, LANE), F32)],
        scratch_shapes=[pltpu.VMEM((S, LANE), F32)] * 3,
        input_output_aliases={10: 0},
        compiler_params=_cp(("parallel",)),
    )(u, cw, cb, wa, ba, wx, bx, lam, h_lru, dycat, du)


HG_LEVELS = 6


def _hg_consts():
    C = HG_CHUNK
    t = np.arange(C)[:, None]
    r = np.arange(C)[None, :]
    mats = []
    for l in range(HG_LEVELS):
        b = 1 << l
        upper = (t % (2 * b)) >= b
        anchor = (t // (2 * b)) * 2 * b + b - 1
        mats.append((upper & (r > anchor) & (r <= t)) | ((~upper) & (r > t) & (r <= anchor)))
    mats.append(r <= t)
    mats.append(r > t)
    return np.concatenate(mats, 0).astype(np.float32)


def _hg_factors(hf, lb, mall):
    s = _sigmoid(hf)
    f = lb + (1.0 - lb) * s
    lf = jnp.log(f)
    k = (1.0 - lb) * _sigmoid(-hf)
    e = jnp.exp(_sel_l(mall, lf))
    C = HG_CHUNK
    rows = _iota((C, hf.shape[1]), 0)
    eq, ek = [], []
    for l in range(HG_LEVELS):
        el = e[l * C:(l + 1) * C]
        eq.append(jnp.where((lax.shift_right_logical(rows, l) & 1) == 1, el, 0.0))
        ek.append(el - eq[l])
    ecum = e[HG_LEVELS * C:(HG_LEVELS + 1) * C]
    erem = e[(HG_LEVELS + 1) * C:(HG_LEVELS + 2) * C]
    return s, f, k, eq, ek, ecum, erem


def _hg_masks():
    C = HG_CHUNK
    ri, ci = _iota((C, C), 0), _iota((C, C), 1)
    rr = _iota((C, LANE), 0)
    gm = [(lax.shift_right_logical(ri, l + 1) == lax.shift_right_logical(ci, l + 1)).astype(F32)
          for l in range(HG_LEVELS)]
    up = [(lax.shift_right_logical(rr, l) & 1) == 1 for l in range(HG_LEVELS)]
    eye = (ri == ci).astype(F32)
    return gm, up, eye, rr


def _hg_scores(qh, kh, eq, ek, sl, gm, up, eye):
    del up
    qs, ks, qb, kb = [], [], [], []
    p = _mm_nt(_bf(qh), _bf(kh)) * eye
    for l in range(HG_LEVELS):
        qs.append(qh * eq[l][:, sl])
        ks.append(kh * ek[l][:, sl])
        qb.append(_bf(qs[l]))
        kb.append(_bf(ks[l]))
        p = p + _mm_nt(qb[l], kb[l]) * gm[l]
    return p, qs, ks, qb, kb


HG_SUB = 4
HG_PAIR = 2


def _hg_fwd(u, lb, nw, mall, ycat):
    S = u.shape[0]
    C = HG_CHUNK
    n = S // C
    rows = HG_SUB * C

    def body(u_ref, lb_ref, nw_ref, mall_ref, ycat_in, ycat_ref, o_ref, st_ref, st):
        del ycat_in

        @pl.when(pl.program_id(0) == 0)
        def _():
            st[...] = jnp.zeros_like(st)

        gm, up, eye, rr = _hg_masks()
        for sub in range(HG_SUB):
            r = slice(sub * C, (sub + 1) * C)
            for pair in range(HG_HEADS // HG_PAIR):
                w0 = pair * HG_PAIR * LANE
                w1 = w0 + HG_PAIR * LANE
                q = _silu(_f(u_ref[r, w0:w1]))
                _, _, k, eq, ek, ecum, erem = _hg_factors(_f(u_ref[r, 512 + w0:512 + w1]), lb_ref[:, w0:w1],
                                                          mall_ref[...])
                for hh in range(HG_PAIR):
                    h = pair * HG_PAIR + hh
                    sl = slice(h * LANE, (h + 1) * LANE)
                    ls = slice(hh * LANE, (hh + 1) * LANE)
                    qh, kh, vh = q[:, ls], k[:, ls], _bf(u_ref[r, 1024 + h * LANE:1024 + (h + 1) * LANE])
                    p = _hg_scores(qh, kh, eq, ek, ls, gm, up, eye)[0]
                    sth = st[h]
                    st_ref[sub, h] = sth
                    o_ref[r, sl] = _mm(_bf(p), vh) + _mm_nt(_bf(qh * ecum[:, ls]), _bf(sth))
                    st[h] = sth * _last_row(ecum[:, ls], rr) + _mm_tn(vh, _bf(kh * erem[:, ls]))
            o = o_ref[r, :]
            inv = lax.rsqrt(jnp.mean(o * o, axis=-1, keepdims=True) + EPS)
            ycat_ref[r, :] = _bf((o * inv) * nw_ref[...] * _silu(_f(u_ref[r, 1536:2048])))

    return pl.pallas_call(
        body, name="hg_fwd", grid=(n // HG_SUB,),
        in_specs=[pl.BlockSpec((rows, 2048), lambda i: (i, 0)), _spec(lb), _spec(nw), _full(mall.shape),
                  pl.BlockSpec(memory_space=pl.ANY)],
        out_specs=[pl.BlockSpec((rows, HG_W), lambda i: (i, 1)), pl.BlockSpec((rows, HG_W), lambda i: (i, 0)),
                   pl.BlockSpec((HG_SUB, HG_HEADS, LANE, LANE), lambda i: (i, 0, 0, 0))],
        out_shape=[SDS((S, D_INNER), BF16), SDS((S,HG_W), F32), SDS((n, HG_HEADS, LANE, LANE), F32)],
        scratch_shapes=[pltpu.VMEM((HG_HEADS, LANE, LANE), F32)],
        input_output_aliases={4: 0},
        compiler_params=_cp(("arbitrary",)),
    )(u, _arr(lb), _arr(nw), mall, ycat)


def _hg_bwd(u, lb, nw, mall, mall_t, o_b, states, dycat, du):
    S = u.shape[0]
    C = HG_CHUNK
    n = S // C
    nb = n // HG_SUB
    rows = HG_SUB * C
    L2 = HG_LEVELS

    def body(u_ref, lb_ref, nw_ref, mall_ref, mallt_ref, o_ref, st_ref, dy_ref, du_in, du_ref, red_ref,
             dst, dlast_s, dq_s, dk_s, dex, do_s):
        del du_in

        @pl.when(pl.program_id(0) == 0)
        def _():
            dst[...] = jnp.zeros_like(dst)
            red_ref[...] = jnp.zeros_like(red_ref)

        nwv = nw_ref[...]
        gm, up, eye, rr = _hg_masks()
        for sub in reversed(range(HG_SUB)):
            r = slice(sub * C, (sub + 1) * C)
            hg = _f(u_ref[r, 1536:2048])
            o = o_ref[r, :]
            dy = _f(dy_ref[r, :])
            inv = lax.rsqrt(jnp.mean(o * o, axis=-1, keepdims=True) + EPS)
            ohat = o * inv
            du_ref[r, 1536:2048] = _bf(dy * ohat * nwv * _dsilu(hg))
            dn = dy * _silu(hg)
            red_ref[0:1, :] += jnp.sum(dn * ohat, axis=0, keepdims=True)
            dohat = dn * nwv
            do_s[sub] = _bf(inv * (dohat - ohat * jnp.mean(dohat * ohat, axis=-1, keepdims=True)))
            for pair in range(HG_HEADS // HG_PAIR):
                w0 = pair * HG_PAIR * LANE
                w1 = w0 + HG_PAIR * LANE
                hq, hf = _f(u_ref[r, w0:w1]), _f(u_ref[r, 512 + w0:512 + w1])
                lbp = lb_ref[:, w0:w1]
                q = _silu(hq)
                s, f, k, eq, ek, ecum, erem = _hg_factors(hf, lbp, mall_ref[...])
                for hh in range(HG_PAIR):
                    h = pair * HG_PAIR + hh
                    sl = slice(h * LANE, (h + 1) * LANE)
                    ls = slice(hh * LANE, (hh + 1) * LANE)
                    qh, kh, vh = q[:, ls], k[:, ls], _bf(u_ref[r, 1024 + h * LANE:1024 + (h + 1) * LANE])
                    doh = do_s[sub, :, sl]
                    p, qs, ks, qb, kb = _hg_scores(qh, kh, eq, ek, ls, gm, up, eye)
                    st_f = st_ref[sub, h]
                    sth = _bf(st_f)
                    dsth = dst[h]
                    dsth_b = _bf(dsth)
                    qt = qh * ecum[:, ls]
                    kt = kh * erem[:, ls]
                    elast = _last_row(ecum[:, ls], rr)
                    dp = _mm_nt(doh, vh)
                    du_ref[r, 1024 + h * LANE:1024 + (h + 1) * LANE] = _bf(_mm_tn(_bf(p), doh)
                                                                           + _mm_nt(_bf(kt), dsth_b))
                    dpe = _bf(dp * eye)
                    dqt = _mm(doh, sth)
                    dkt = _mm(vh, dsth_b)
                    dq = dqt * ecum[:, ls] + _mm(dpe, _bf(kh))
                    dk = dkt * erem[:, ls] + _mm_tn(dpe, _bf(qh))
                    dex[sub, L2 * C:(L2 + 1) * C, sl] = dqt * qt
                    dex[sub, (L2 + 1) * C:(L2 + 2) * C, sl] = dkt * kt
                    for l in range(HG_LEVELS):
                        dpl = _bf(dp * gm[l])
                        dql = _mm(dpl, kb[l])
                        dkl = _mm_tn(dpl, qb[l])
                        dq = dq + dql * eq[l][:, ls]
                        dk = dk + dkl * ek[l][:, ls]
                        dex[sub, l * C:(l + 1) * C, sl] = dql * qs[l] + dkl * ks[l]
                    dlast_s[sub, :, sl] = jnp.sum(dsth * st_f, axis=0, keepdims=True) * elast
                    dst[h] = dsth * elast + _mm_tn(doh, _bf(qt))
                    dq_s[sub, :, sl] = dq
                    dk_s[sub, :, sl] = dk
                dq = dq_s[sub, :, w0:w1]
                dk = dk_s[sub, :, w0:w1]
                dlf = _sel_l2(mallt_ref[...], dex[sub, :, w0:w1]) + dlast_s[sub, :, w0:w1]
                du_ref[r, w0:w1] = _bf(dq * _dsilu(hq))
                t = (1.0 - s) * (dlf / f - dk)
                du_ref[r, 512 + w0:512 + w1] = _bf((1.0 - lbp) * s * t)
                red_ref[1:2, w0:w1] += jnp.sum(t, axis=0, keepdims=True)

    rev = lambda i: (nb - 1 - i, 0)
    return pl.pallas_call(
        body, name="hg_bwd", grid=(nb,),
        in_specs=[pl.BlockSpec((rows, 2048), rev), _spec(lb), _spec(nw), _full(mall.shape), _full(mall_t.shape),
                  pl.BlockSpec((rows, HG_W), rev),
                  pl.BlockSpec((HG_SUB, HG_HEADS, LANE, LANE), lambda i: (nb - 1 - i, 0, 0, 0)),
                  pl.BlockSpec((rows, HG_W), lambda i: (nb - 1 - i, 1)), pl.BlockSpec(memory_space=pl.ANY)],
        out_specs=[pl.BlockSpec((rows, 2048), rev), pl.BlockSpec((8, HG_W), lambda i: (0, 0))],
        out_shape=[SDS((S, N_PAD), BF16), SDS((8, HG_W), F32)],
        scratch_shapes=[pltpu.VMEM((HG_HEADS, LANE, LANE), F32), pltpu.VMEM((HG_SUB, 1, HG_W), F32),
                        pltpu.VMEM((HG_SUB, C, HG_W), F32), pltpu.VMEM((HG_SUB, C, HG_W), F32),
                        pltpu.VMEM((HG_SUB, (L2 + 2) * C, HG_W), F32), pltpu.VMEM((HG_SUB, C, HG_W), BF16)],
        input_output_aliases={8: 0},
        compiler_params=_cp(("arbitrary",)),
    )(u, _arr(lb), _arr(nw), mall, mall_t, o_b, states, dycat, du)


def _ssdconv_fwd(l, u, cw, cb):
    S = u.shape[0]

    def body(u_ref, cw_ref, cb_ref, out_ref):
        rows = _iota((S, LANE), 0)
        out_ref[...] = _silu(_conv_fwd(_f(u_ref[...]), cw_ref, cb_ref, rows))

    return pl.pallas_call(
        body, name="ssdconv_fwd", grid=(SSD_CONV // LANE,),
        in_specs=[pl.BlockSpec((S, LANE), lambda t: (0, OFF_XBC // LANE + t)),
                  pl.BlockSpec((None, 4, LANE), lambda t: (l, 0, t)), pl.BlockSpec((None, 1, LANE), lambda t: (l, 0, t))],
        out_specs=pl.BlockSpec((S, LANE), lambda t: (0, t)),
        out_shape=SDS((S, SSD_CONV), F32),
        compiler_params=_cp(("parallel",)),
    )(u, cw, cb)


def _ssdconv_bwd(l, u, cw, cb, dxbc, du):
    S = u.shape[0]

    def body(u_ref, cw_ref, cb_ref, d_ref, du_in, du_ref, red_ref):
        del du_in
        rows = _iota((S, LANE), 0)
        x = _f(u_ref[...])
        dco = d_ref[...] * _dsilu(_conv_fwd(x, cw_ref, cb_ref, rows))
        dx, dws, dcb = _conv_bwd(x, dco, cw_ref, rows)
        du_ref[...] = _bf(dx)
        for n, p in enumerate(dws + [dcb]):
            red_ref[pl.ds(n, 1), :] = p
        red_ref[pl.ds(5, 3), :] = jnp.zeros((3, LANE), F32)

    ucol = pl.BlockSpec((S, LANE), lambda t: (0, OFF_XBC // LANE + t))
    return pl.pallas_call(
        body, name="ssdconv_bwd", grid=(SSD_CONV // LANE,),
        in_specs=[ucol, pl.BlockSpec((None, 4, LANE), lambda t: (l, 0, t)),
                  pl.BlockSpec((None, 1, LANE), lambda t: (l, 0, t)),
                  pl.BlockSpec((S, LANE), lambda t: (0, t)), pl.BlockSpec(memory_space=pl.ANY)],
        out_specs=[ucol, pl.BlockSpec((8, LANE), lambda t: (0, t))],
        out_shape=[SDS((S, N_PAD), BF16), SDS((8, SSD_CONV), F32)],
        input_output_aliases={4: 0},
        compiler_params=_cp(("parallel",)),
    )(u, cw, cb, dxbc, du)


SSD_SUB = 2


def _ssd_consts():
    e64 = np.zeros((LANE, SSD_W), np.float32)
    for h in range(SSD_HEADS):
        e64[h, h * SSD_P:(h + 1) * SSD_P] = 1.0
    T = SSD_CHUNK
    tril = (np.arange(T)[None, :] <= np.arange(T)[:, None]).astype(np.float32)
    return e64, tril, tril.T.copy()


def _ssd_common(zdt, bias_ref, alog_ref, tril, e64, cum_ref, cumt_ref):
    T = SSD_CHUNK
    lane = _iota((1, LANE), 1)
    a_neg = jnp.where(lane < SSD_HEADS, -jnp.exp(alog_ref[...]), 0.0)
    dtpre = zdt[:, SSD_W:SSD_W + LANE] + bias_ref[...]
    dt = _softplus(dtpre)
    cum = _sel_l(tril, dt * a_neg)
    cum_ref[...] = cum
    cumt_ref[...] = cum.T
    cum_x = _sel_r(cum, e64)
    last_x = _last_row(cum_x, _iota((T, SSD_W), 0))
    ecum_x = jnp.exp(cum_x)
    erem_x = jnp.exp(last_x - cum_x)
    elast_x = jnp.exp(last_x)
    dt_x = _sel_r(dt, e64)
    return a_neg, dtpre, dt, ecum_x, erem_x, elast_x, dt_x


def _ssd_decay(cum_ref, cumt_ref, h, causal):
    T = SSD_CHUNK
    diff = jnp.broadcast_to(cum_ref[:, pl.ds(h, 1)], (T, T)) - cumt_ref[pl.ds(h, 1), :]
    return jnp.exp(jnp.where(causal, diff, NEG))


def _group_norm_fwd(y1, nwv):
    outs, invs = [], []
    for g in range(2):
        seg = y1[:, g * 512:(g + 1) * 512]
        inv = lax.rsqrt(jnp.mean(seg * seg, axis=-1, keepdims=True) + EPS)
        outs.append(seg * inv * nwv[:, g * 512:(g + 1) * 512])
        invs.append(inv)
    return outs, invs


def _ssd_fwd(u, xbc, bias, alog, dskip_x, nw, consts, ycat):
    S = u.shape[0]
    T = SSD_CHUNK
    n = S // T
    rows = SSD_SUB * T
    e64, tril, _ = consts

    def body(u_ref, xbc_ref, bias_ref, alog_ref, dx_ref, nw_ref, e64_ref, tril_ref, ycat_in,
             ycat_ref, y_ref, st_ref, st, cumt, cum_e):
        del ycat_in

        @pl.when(pl.program_id(0) == 0)
        def _():
            st[...] = jnp.zeros_like(st)

        causal = _iota((T, T), 0) >= _iota((T, T), 1)
        lo = _iota((T, LANE), 1) < SSD_P
        for sub in range(SSD_SUB):
            r = slice(sub * T, (sub + 1) * T)
            zdt = _f(u_ref[r, :])
            z = zdt[:, 0:SSD_W]
            xs = xbc_ref[r, 0:SSD_W]
            cum_r, cumt_r = cum_e.at[sub], cumt.at[sub]
            _, _, _, ecum_x, erem_x, elast_x, dt_x = _ssd_common(
                zdt, bias_ref, alog_ref, tril_ref[...], e64_ref[...], cum_r, cumt_r)
            xdt = xs * dt_x
            xrem = xdt * erem_x
            st_ref[sub] = st[...]
            for g in range(2):
                gs = slice(g * 512, (g + 1) * 512)
                bg = _bf(xbc_ref[r, SSD_W + g * LANE:SSD_W + (g + 1) * LANE])
                cg = _bf(xbc_ref[r, SSD_W + 256 + g * LANE:SSD_W + 256 + (g + 1) * LANE])
                cb = _mm_nt(cg, bg)
                yin = _mm(cg, _bf(st[:, gs])) * ecum_x[:, gs]
                for j in range(4):
                    h0 = 8 * g + 2 * j
                    cs = slice(h0 * SSD_P, (h0 + 2) * SSD_P)
                    xp = xdt[:, cs]
                    s0 = _bf(cb * _ssd_decay(cum_r, cumt_r, h0, causal))
                    s1 = _bf(cb * _ssd_decay(cum_r, cumt_r, h0 + 1, causal))
                    y_ref[r, cs] = (_mm(s0, _bf(jnp.where(lo, xp, 0.0))) + _mm(s1, _bf(jnp.where(lo, 0.0, xp)))
                                    + yin[:, j * LANE:(j + 1) * LANE])
                st[:, gs] = st[:, gs] * elast_x[:, gs] + _mm_tn(bg, _bf(xrem[:, gs]))
            y1 = (y_ref[r, :] + dx_ref[...] * xs) * _silu(z)
            outs, _ = _group_norm_fwd(y1, nw_ref[...])
            for g in range(2):
                ycat_ref[r, g * 512:(g + 1) * 512] = _bf(outs[g])

    return pl.pallas_call(
        body, name="ssd_fwd", grid=(n // SSD_SUB,),
        in_specs=[pl.BlockSpec((rows, SSD_W + LANE), lambda i: (i, OFF_Z // (SSD_W + LANE))),
                  pl.BlockSpec((rows, SSD_CONV), lambda i: (i, 0)), _spec(bias), _spec(alog), _spec(dskip_x), _spec(nw),
                  _full(e64.shape), _full(tril.shape), pl.BlockSpec(memory_space=pl.ANY)],
        out_specs=[pl.BlockSpec((rows, SSD_W), lambda i: (i, 1)), pl.BlockSpec((rows, SSD_W), lambda i: (i, 0)),
                   pl.BlockSpec((SSD_SUB, SSD_N, SSD_W), lambda i: (i, 0, 0))],
        out_shape=[SDS((S, D_INNER), BF16), SDS((S,SSD_W), F32), SDS((n, SSD_N, SSD_W), F32)],
        scratch_shapes=[pltpu.VMEM((SSD_N, SSD_W), F32), pltpu.VMEM((SSD_SUB, LANE, T), F32),
                        pltpu.VMEM((SSD_SUB, T, LANE), F32)],
        input_output_aliases={8: 0},
        compiler_params=_cp(("arbitrary",)),
    )(u, xbc, _arr(bias), _arr(alog), _arr(dskip_x), _arr(nw), _bfc(e64), _bfc(tril), ycat)


def _ssd_bwd(u, xbc, bias, alog, dskip_x, nw, consts, y_ssd, states, dycat, du, tok):
    S = u.shape[0]
    T = SSD_CHUNK
    n = S // T
    e64, tril, triu = consts
    e64t = np.ascontiguousarray(e64.T)

    def chunk(u_ref, xbc_ref, bias_ref, alog_ref, dx_ref, nw_ref, e64_ref, e64t_ref, tril_ref, triu_ref,
              y_ref, st_ref, dy_ref, du_ref, dxbc_ref, red_ref, dst, dl_s, cumt, dxdt_s, dy0_s, gb_s, gc_s, cum_e, cs_s):
        zdt = _f(u_ref[...])
        z = zdt[:, 0:SSD_W]
        xs = xbc_ref[:, 0:SSD_W]
        a_neg, dtpre, dt, ecum_x, erem_x, elast_x, dt_x = _ssd_common(
            zdt, bias_ref, alog_ref, tril_ref[...], e64_ref[...], cum_e, cumt)
        causal = _iota((T, T), 0) >= _iota((T, T), 1)
        lo = _iota((T, LANE), 1) < SSD_P
        xdt = xs * dt_x
        xrem = xdt * erem_x
        y = y_ref[...]
        dxv = dx_ref[...]
        nwv = nw_ref[...]
        sz = _silu(z)
        y0 = y + dxv * xs
        y1 = y0 * sz
        for g in range(2):
            gs = slice(g * 512, (g + 1) * 512)
            seg = y1[:, gs]
            inv = lax.rsqrt(jnp.mean(seg * seg, axis=-1, keepdims=True) + EPS)
            shat = seg * inv
            dyg = _f(dy_ref[:, gs])
            red_ref[0:1, gs] += jnp.sum(dyg * shat, axis=0, keepdims=True)
            dsh = dyg * nwv[:, gs]
            dy1g = inv * (dsh - shat * jnp.mean(dsh * shat, axis=-1, keepdims=True))
            du_ref[:, gs] = _bf(dy1g * y0[:, gs] * _dsilu(z[:, gs]))
            dy0_s[:, gs] = dy1g * sz[:, gs]
        dy0 = dy0_s[...]
        red_ref[1:2, :] += jnp.sum(dy0 * xs, axis=0, keepdims=True)
        dyin = dy0 * ecum_x
        lane = _iota((T, LANE), 1)
        dcum = jnp.zeros((T, LANE), F32)

        def decay_grad(h, gm):
            cs_s[pl.ds(h, 1), :] = jnp.sum(gm, axis=0, keepdims=True)
            return jnp.where(lane == h, jnp.sum(gm, axis=1, keepdims=True), 0.0)

        for g in range(2):
            gs = slice(g * 512, (g + 1) * 512)
            bg = _bf(xbc_ref[:, SSD_W + g * LANE:SSD_W + (g + 1) * LANE])
            cg = _bf(xbc_ref[:, SSD_W + 256 + g * LANE:SSD_W + 256 + (g + 1) * LANE])
            cb = _mm_nt(cg, bg)
            dst_f, st_f = dst[:, gs], st_ref[:, gs]
            dstg = _bf(dst_f)
            stg = _bf(st_f)
            dyin_g = _bf(dyin[:, gs])
            xrem_g = _bf(xrem[:, gs])
            dcb = jnp.zeros((T, T), F32)
            dxr = _mm(bg, dstg)
            dxdt_s[:, gs] = dxr * erem_x[:, gs]
            gc_s[:, gs] = dxr * xrem[:, gs]
            gb_s[:, gs] = dyin[:, gs] * _mm(cg, stg)
            dl_s[:, gs] = jnp.sum(dst_f * st_f, axis=0, keepdims=True) * elast_x[:, gs]
            for j in range(4):
                h0 = 8 * g + 2 * j
                cs = slice(h0 * SSD_P, (h0 + 2) * SSD_P)
                xp = xdt[:, cs]
                dyp = dy0[:, cs]
                x_lo, x_hi = _bf(jnp.where(lo, xp, 0.0)), _bf(jnp.where(lo, 0.0, xp))
                d_lo, d_hi = _bf(jnp.where(lo, dyp, 0.0)), _bf(jnp.where(lo, 0.0, dyp))
                l0 = _ssd_decay(cum_e, cumt, h0, causal)
                l1 = _ssd_decay(cum_e, cumt, h0 + 1, causal)
                s0 = cb * l0
                s1 = cb * l1
                ds0 = _mm_nt(d_lo, x_lo)
                ds1 = _mm_nt(d_hi, x_hi)
                dcb = dcb + ds0 * l0 + ds1 * l1
                dxdt_s[:, cs] += _mm_tn(_bf(s0), d_lo) + _mm_tn(_bf(s1), d_hi)
                dcum = dcum + decay_grad(h0, ds0 * s0) + decay_grad(h0 + 1, ds1 * s1)
            dcb_b = _bf(dcb)
            dxbc_ref[:, SSD_W + g * LANE:SSD_W + (g + 1) * LANE] = _mm_tn(dcb_b, cg) + _mm_nt(xrem_g, dstg)
            dxbc_ref[:, SSD_W + 256 + g * LANE:SSD_W + 256 + (g + 1) * LANE] = _mm(dcb_b, bg) + _mm_nt(dyin_g, stg)
            dst[:, gs] = dst_f * elast_x[:, gs] + _mm_tn(cg, dyin_g)
        dxdt = dxdt_s[...]
        dxbc_ref[:, 0:SSD_W] = dxdt * dt_x + dy0 * dxv
        e64t = e64t_ref[...]
        gc = gc_s[...]
        dlast_x = jnp.sum(gc, axis=0, keepdims=True) + dl_s[...]
        dlast = jnp.max(_sel_r(jnp.broadcast_to(dlast_x, (8, SSD_W)), e64t), axis=0, keepdims=True)
        dcum = (dcum - cs_s[...].T + _sel_r(gb_s[...] - gc, e64t)
                + jnp.where(_iota((T, LANE), 0) == T - 1, dlast, 0.0))
        dda = _sel_l(triu_ref[...], dcum)
        ddt = dda * a_neg + _sel_r(dxdt * xs, e64t)
        ddtpre = ddt * _sigmoid(dtpre)
        du_ref[:, SSD_W:SSD_W + LANE] = _bf(jnp.where(lane < SSD_HEADS, ddtpre, 0.0))
        red_ref[2:3, 0:LANE] += jnp.sum(ddtpre, axis=0, keepdims=True)
        red_ref[3:4, 0:LANE] += jnp.sum(dda * dt, axis=0, keepdims=True)

    def body(u_ref, xbc_ref, bias_ref, alog_ref, dx_ref, nw_ref, e64_ref, e64t_ref, tril_ref, triu_ref,
             y_ref, st_ref, dy_ref, du_in, tok_ref, du_ref, dxbc_ref, red_ref, dst, *scratch):
        del du_in, tok_ref

        @pl.when(pl.program_id(0) == 0)
        def _():
            dst[...] = jnp.zeros_like(dst)
            red_ref[...] = jnp.zeros_like(red_ref)
            scratch[-1][...] = jnp.zeros_like(scratch[-1])

        for sub in reversed(range(SSD_SUB)):
            rs = pl.ds(sub * T, T)
            chunk(u_ref.at[rs], xbc_ref.at[rs], bias_ref, alog_ref, dx_ref, nw_ref, e64_ref, e64t_ref, tril_ref, triu_ref,
                  y_ref.at[rs], st_ref.at[sub], dy_ref.at[rs], du_ref.at[rs], dxbc_ref.at[rs], red_ref, dst,
                  *[s.at[sub] for s in scratch])

    nb = n // SSD_SUB
    rows = SSD_SUB * T
    rev = lambda i: (nb - 1 - i, 0)
    sub_scratch = [(1, SSD_W), (LANE, T)] + [(T, SSD_W)] * 4 + [(T, LANE), (LANE, T)]
    return pl.pallas_call(
        body, name="ssd_bwd", grid=(nb,),
        in_specs=[pl.BlockSpec((rows, SSD_W + LANE), lambda i: (nb - 1 - i, OFF_Z // (SSD_W + LANE))),
                  pl.BlockSpec((rows, SSD_CONV), rev), _spec(bias), _spec(alog), _spec(dskip_x), _spec(nw),
                  _full(e64.shape), _full(e64t.shape), _full(tril.shape), _full(triu.shape),
                  pl.BlockSpec((rows, SSD_W), rev), pl.BlockSpec((SSD_SUB, SSD_N, SSD_W), lambda i: (nb - 1 - i, 0, 0)),
                  pl.BlockSpec((rows, SSD_W), lambda i: (nb - 1 - i, 1)), pl.BlockSpec(memory_space=pl.ANY),
                  pl.BlockSpec(memory_space=pl.ANY)],
        out_specs=[pl.BlockSpec((rows, SSD_W + LANE), lambda i: (nb - 1 - i, OFF_Z // (SSD_W + LANE))),
                   pl.BlockSpec((rows, SSD_CONV), rev), pl.BlockSpec((8, SSD_W), lambda i: (0, 0))],
        out_shape=[SDS((S, N_PAD), BF16), SDS((S, SSD_CONV), F32), SDS((8, SSD_W), F32)],
        scratch_shapes=[pltpu.VMEM((SSD_N, SSD_W), F32)] + [pltpu.VMEM((SSD_SUB,) + s, F32) for s in sub_scratch],
        input_output_aliases={13: 0},
        compiler_params=_cp(("arbitrary",)),
    )(u, xbc, _arr(bias), _arr(alog), _arr(dskip_x), _arr(nw), _bfc(e64), _bfc(e64t), _bfc(tril), _bfc(triu), y_ssd,
      states, dycat, du, tok)


def _bfc(a):
    return jnp.asarray(a, BF16)


def _outproj_fwd(ycat, wo, x, gate, tok):
    S = x.shape[0]
    tm = min(512, S)

    def body(yc_ref, wo_ref, x_ref, g_ref, tok_ref, xn_ref, y_ref):
        del tok_ref
        y = _mm(_bf(yc_ref[...]), wo_ref[...])
        y_ref[...] = y
        xn_ref[...] = x_ref[...] + g_ref[...] * y

    row = pl.BlockSpec((tm, D_MODEL), lambda i: (i, 0))
    return pl.pallas_call(
        body, name="outproj_fwd", grid=(S // tm,),
        in_specs=[pl.BlockSpec((tm, D_INNER), lambda i: (i, 0)), _full((D_INNER, D_MODEL)), row, _spec(gate),
                  pl.BlockSpec(memory_space=pl.ANY)],
        out_specs=[row, row],
        out_shape=[SDS((S, D_MODEL), F32), SDS((S, D_MODEL), F32)],
        compiler_params=_cp(("parallel",)),
    )(ycat, wo, x, _arr(gate), tok)


def _outproj_bwd(dxn, y, gate, ycat, wo):
    S = dxn.shape[0]
    tm = min(512, S)

    def body(dx_ref, y_ref, g_ref, yc_ref, wo_ref, dyc_ref, gwo_ref, dg_ref, acc):
        @pl.when(pl.program_id(0) == 0)
        def _():
            acc[...] = jnp.zeros_like(acc)
            dg_ref[...] = jnp.zeros_like(dg_ref)

        dxv = dx_ref[...]
        dy = _bf(dxv * g_ref[...])
        dg_ref[0:1, :] += jnp.sum(dxv * y_ref[...], axis=0, keepdims=True)
        dyc_ref[...] = _mm_nt(dy, wo_ref[...])
        acc[...] += _mm_tn(_bf(yc_ref[...]), dy)

        @pl.when(pl.program_id(0) == pl.num_programs(0) - 1)
        def _():
            gwo_ref[...] = acc[...].astype(BF16)

    row = pl.BlockSpec((tm, D_MODEL), lambda i: (i, 0))
    wide = pl.BlockSpec((tm, D_INNER), lambda i: (i, 0))
    return pl.pallas_call(
        body, name="outproj_bwd", grid=(S // tm,),
        in_specs=[row, row, _spec(gate), wide, _full((D_INNER, D_MODEL))],
        out_specs=[wide, _full((D_INNER, D_MODEL)), _full((8, D_MODEL))],
        out_shape=[SDS((S, D_INNER), F32), SDS((D_INNER, D_MODEL), BF16), SDS((8, D_MODEL), F32)],
        scratch_shapes=[pltpu.VMEM((D_INNER, D_MODEL), F32)],
        compiler_params=_cp(("arbitrary",)),
    )(dxn, y, _arr(gate), ycat, wo)


def _loss_head(x, fw, target):
    S = x.shape[0]
    tm = min(512, S)

    def body(x_ref, fw_ref, t_ref, dx_ref, red_ref):
        @pl.when(pl.program_id(0) == 0)
        def _():
            red_ref[...] = jnp.zeros_like(red_ref)

        xv = x_ref[...]
        fwv = fw_ref[...]
        inv = lax.rsqrt(jnp.mean(xv * xv, axis=-1, keepdims=True) + EPS)
        xhat = xv * inv
        err = xhat * fwv - t_ref[...]
        col = jnp.sum(err * err, axis=0, keepdims=True)
        red_ref[1:2, :] += jnp.broadcast_to(jnp.sum(col, axis=1, keepdims=True) * (0.5 / D_MODEL), (1, D_MODEL))
        dy = err * (1.0 / D_MODEL)
        red_ref[0:1, :] += jnp.sum(dy * xhat, axis=0, keepdims=True)
        dxhat = dy * fwv
        dx_ref[...] = inv * (dxhat - xhat * jnp.mean(dxhat * xhat, axis=-1, keepdims=True))

    row = pl.BlockSpec((tm, D_MODEL), lambda i: (i, 0))
    return pl.pallas_call(
        body, name="loss_head", grid=(S // tm,),
        in_specs=[row, _vec(D_MODEL), row],
        out_specs=[row, _full((8, D_MODEL))],
        out_shape=[SDS((S, D_MODEL), F32), SDS((8, D_MODEL), F32)],
        compiler_params=_cp(("arbitrary",)),
    )(x, fw, target)


ADA_COLS = 3 * D_MODEL // N_DEV


def _ada_fwd(c_all, w_ada, b_cols):
    def body(c_ref, w_ref, b_ref, out_ref):
        out_ref[...] = _mm(_bf(_silu(c_ref[...])), _bf(w_ref[...])) + b_ref[...]

    return pl.pallas_call(
        body, name="ada_fwd", grid=(DEPTH,),
        in_specs=[_full((N_DEV, D_MODEL)), pl.BlockSpec((None, D_MODEL, ADA_COLS), lambda l: (l, 0, 0)),
                  pl.BlockSpec((None, 1, ADA_COLS), lambda l: (l, 0, 0))],
        out_specs=pl.BlockSpec((None, N_DEV, ADA_COLS), lambda l: (l, 0, 0)),
        out_shape=SDS((DEPTH, N_DEV, ADA_COLS), F32),
        compiler_params=_cp(("parallel",)),
    )(c_all, w_ada, b_cols)


def _ada_bwd(ct_pad, dmod_pad):
    def body(c_ref, d_ref, out_ref):
        out_ref[...] = _mm(_bf(_silu(c_ref[...])), _bf(d_ref[...]))

    return pl.pallas_call(
        body, name="ada_bwd", grid=(DEPTH,),
        in_specs=[_full((D_MODEL, LANE)), pl.BlockSpec((None, LANE, ADA_COLS), lambda l: (l, 0, 0))],
        out_specs=pl.BlockSpec((None, D_MODEL, ADA_COLS), lambda l: (l, 0, 0)),
        out_shape=SDS((DEPTH, D_MODEL, ADA_COLS), F32),
        compiler_params=_cp(("parallel",)),
    )(ct_pad, dmod_pad)


def _adamw(parts, w, m, v, name, own=None, layers=None, prev=None):
    n, L, R, C = parts.shape
    lo, hi = layers or (0, L)
    tr = R
    while tr * C * 4 > (1 << 20) and tr % 16 == 0:
        tr //= 2
    first = 1 if own is None else 2

    def body(*refs):
        p_ref = refs[0]
        w_ref, m_ref, v_ref = refs[first:first + 3]
        g_ref, d_ref, mo_ref, vo_ref = refs[-4:]

        def part(k):
            if own is None:
                return p_ref[k].astype(F32)
            me = 4 * lax.axis_index("x") + 2 * lax.axis_index("y") + lax.axis_index("c")
            return jnp.where(me == k, refs[1][...], p_ref[k]).astype(F32)

        g = part(0)
        for k in range(1, n):
            g = g + part(k)
        mn = ADAM_B1 * m_ref[...] + (1.0 - ADAM_B1) * g
        vn = ADAM_B2 * v_ref[...] + (1.0 - ADAM_B2) * (g * g)
        m_hat = mn / (1.0 - ADAM_B1 ** ADAM_STEP)
        v_hat = vn / (1.0 - ADAM_B2 ** ADAM_STEP)
        g_ref[...] = g
        d_ref[...] = -ADAM_LR * (m_hat / (jnp.sqrt(v_hat) + ADAM_EPS) + ADAM_WD * w_ref[...])
        mo_ref[...] = mn
        vo_ref[...] = vn

    blk = pl.BlockSpec((None, tr, C), lambda l, i: (lo + l, i, 0))
    own_blk = [] if own is None else [pl.BlockSpec((None, tr, C), lambda l, i: (l, i, 0))]
    n_blk = 3 if own is None else 4
    return pl.pallas_call(
        body, name=name, grid=(hi - lo, R // tr),
        in_specs=[pl.BlockSpec((n, None, tr, C), lambda l, i: (0, lo + l, i, 0))] + own_blk + [blk] * 3
        + ([] if prev is None else [ANY] * 4),
        out_specs=[blk] * 4,
        out_shape=[SDS((L, R, C), F32)] * 4,
        input_output_aliases={} if prev is None else {1 + n_blk + k: k for k in range(4)},
        compiler_params=_cp(("parallel", "parallel")),
    )(parts, *([] if own is None else [own]), w, m, v, *([] if prev is None else prev))


MESH = pl.DeviceIdType.MESH
ANY = pl.BlockSpec(memory_space=pl.ANY)


def _all_gather(v, name):
    def body(v_ref, out_ref, send_sems, recv_sems, local_sem):
        x, y, c = lax.axis_index("x"), lax.axis_index("y"), lax.axis_index("c")
        me, sibling = (x, y, c), (x, y, 1 - c)
        chips = [(1 - x, y), (x, 1 - y), (1 - x, 1 - y)]

        def slot(px, py, pc):
            return out_ref.at[4 * px + 2 * py + pc]

        def copy(k, block, to, src=None):
            return pltpu.make_async_remote_copy(
                src_ref=slot(*block) if src is None else src, dst_ref=slot(*block),
                send_sem=send_sems.at[k], recv_sem=recv_sems.at[k], device_id=to, device_id_type=MESH)

        mine = pltpu.make_async_copy(v_ref, slot(*me), local_sem)
        mine.start()
        first = [copy(0, me, sibling, src=v_ref)]
        first += [copy(1 + j, me, (*chip, c), src=v_ref) for j, chip in enumerate(chips)]
        for cp in first:
            cp.start()
        passed = [copy(4 + j, (*chip, c), sibling) for j, chip in enumerate(chips)]
        for j, chip in enumerate(chips):
            copy(1 + j, (*chip, c), me).wait_recv()
            passed[j].start()
        copy(0, sibling, me).wait_recv()
        for j, chip in enumerate(chips):
            copy(4 + j, (*chip, 1 - c), me).wait_recv()
        for cp in first + passed:
            cp.wait_send()
        mine.wait()

    return pl.pallas_call(
        body, name=name, in_specs=[ANY], out_specs=ANY,
        out_shape=SDS((N_DEV,) + v.shape, v.dtype),
        scratch_shapes=[pltpu.SemaphoreType.DMA((7,)), pltpu.SemaphoreType.DMA((7,)), pltpu.SemaphoreType.DMA],
    )(v)


def _all_to_all(v, name):
    def body(v_ref, out_ref, send_sems, recv_sems, local_sem):
        x, y, c = lax.axis_index("x"), lax.axis_index("y"), lax.axis_index("c")
        mine_idx = 4 * x + 2 * y + c
        mine = pltpu.make_async_copy(v_ref.at[mine_idx], out_ref.at[mine_idx], local_sem)
        mine.start()
        sends, recvs = [], []
        for k in range(1, N_DEV):
            px = 1 - x if k & 4 else x
            py = 1 - y if k & 2 else y
            pc = 1 - c if k & 1 else c
            peer_idx = 4 * px + 2 * py + pc
            sems = dict(send_sem=send_sems.at[k - 1], recv_sem=recv_sems.at[k - 1], device_id=(px, py, pc),
                        device_id_type=MESH)
            sends.append(pltpu.make_async_remote_copy(src_ref=v_ref.at[peer_idx], dst_ref=out_ref.at[mine_idx], **sems))
            recvs.append(pltpu.make_async_remote_copy(src_ref=v_ref.at[peer_idx], dst_ref=out_ref.at[peer_idx], **sems))
        for cp in sends:
            cp.start()
        for cp in recvs:
            cp.wait_recv()
        for cp in sends:
            cp.wait_send()
        mine.wait()

    return pl.pallas_call(
        body, name=name, in_specs=[ANY], out_specs=ANY,
        out_shape=SDS(v.shape, v.dtype),
        scratch_shapes=[pltpu.SemaphoreType.DMA((7,)), pltpu.SemaphoreType.DMA((7,)), pltpu.SemaphoreType.DMA],
    )(v)


HBM_SPEC = pl.BlockSpec(memory_space=pltpu.HBM)
SEM_SPEC = pl.BlockSpec(memory_space=pltpu.SEMAPHORE)
EFFECT = pltpu.SideEffectType.DATAFLOW_SIDE_EFFECTING


EXCHANGE_PEERS = {"gather": range(1, N_DEV), "scatter": range(1, N_DEV), "chip": (1, 2, 4, 6), "pass": (2, 4, 6)}


def _exchange_copies(srcs, lands, send_sems, recv_sems, mode, layer):
    x, y, c = lax.axis_index("x"), lax.axis_index("y"), lax.axis_index("c")
    me = 4 * x + 2 * y + c
    copies = []
    for a, (src, land) in enumerate(zip(srcs, lands)):
        for k in EXCHANGE_PEERS[mode]:
            px = 1 - x if k & 4 else x
            py = 1 - y if k & 2 else y
            pc = 1 - c if k & 1 else c
            peer = 4 * px + 2 * py + pc
            if mode == "scatter":
                s, d, to = src.at[peer], land.at[me, layer], (px, py, pc)
            elif mode == "pass":
                s, d, to = land.at[peer], land.at[peer], (x, y, 1 - c)
            else:
                s, d, to = src, land.at[me], (px, py, pc)
            n = 7 * a + k - 1
            copies.append(pltpu.make_async_remote_copy(
                src_ref=s, dst_ref=d, send_sem=send_sems.at[n], recv_sem=recv_sems.at[n], device_id=to,
                device_id_type=MESH))
    return copies


def _exchange_start(name, srcs, lands, mode, layer=0, after=None):
    n = len(srcs)

    def body(*refs):
        send_sems, recv_sems = refs[-2 * n - 3], refs[-2 * n - 2]
        for cp in _exchange_copies(refs[:n], refs[n:2 * n], send_sems, recv_sems, mode, layer):
            cp.start()
        refs[-1][...] = jnp.zeros_like(refs[-1])

    arrays = list(srcs) + list(lands)
    sems = pltpu.SemaphoreType.DMA((7 * n,))
    out = pl.pallas_call(
        body, name=name,
        out_shape=(sems, sems, *[pltpu.HBM(v.shape, v.dtype) for v in arrays], SDS((8, LANE), F32)),
        in_specs=[HBM_SPEC] * (2 * n) + ([ANY] if after is not None else []),
        out_specs=(SEM_SPEC, SEM_SPEC, *[HBM_SPEC] * (2 * n), pl.BlockSpec(memory_space=pltpu.VMEM)),
        input_output_aliases={i: 2 + i for i in range(2 * n)},
        compiler_params=pltpu.CompilerParams(has_side_effects=EFFECT),
    )(*[pltpu.with_memory_space_constraint(v, pltpu.HBM) for v in arrays], *([after] if after is not None else []))
    return dict(sems=out[:2], srcs=out[2:2 + n], lands=out[2 + n:2 + 2 * n], token=out[-1], mode=mode,
                layer=layer)


def _exchange_wait(name, st, after, also=()):
    n = len(st["srcs"])

    def body(*refs):
        send_sems, recv_sems = refs[2 * n], refs[2 * n + 1]
        for cp in _exchange_copies(refs[:n], refs[n:2 * n], send_sems, recv_sems, st["mode"], st["layer"]):
            cp.wait_send()
            cp.wait_recv()

    arrays = list(st["srcs"]) + list(st["lands"])
    out = pl.pallas_call(
        body, name=name,
        out_shape=tuple(pltpu.HBM(v.shape, v.dtype) for v in arrays),
        in_specs=[HBM_SPEC] * (2 * n) + [SEM_SPEC, SEM_SPEC] + [ANY] * (1 + len(also)),
        out_specs=tuple([HBM_SPEC] * (2 * n)),
        input_output_aliases={i: i for i in range(2 * n)},
        compiler_params=pltpu.CompilerParams(has_side_effects=EFFECT),
    )(*arrays, *st["sems"], after, *also)
    st["srcs"] = out[:n]
    return out[n:]


_IN_PIECES = ([(1024, 3072)]
              + [r for t in range(4) for r in ((LANE * t, LANE * (t + 1)), (512 + LANE * t, 512 + LANE * (t + 1)))]
              + [(4096, 5632), (3072, 4096), (5632, 5648)])


def _permute_in(w):
    pad = jnp.zeros(w.shape[:-1] + (N_PAD - N_IN,), w.dtype)
    return jnp.concatenate([w[..., a:b] for a, b in _IN_PIECES] + [pad], axis=-1)


def _unpermute_in(g):
    ax = [g[..., OFF_LRU + 2 * LANE * t:OFF_LRU + 2 * LANE * t + LANE] for t in range(4)]
    ag = [g[..., OFF_LRU + 2 * LANE * t + LANE:OFF_LRU + 2 * LANE * (t + 1)] for t in range(4)]
    return jnp.concatenate(ax + ag + [g[..., 0:2048], g[..., OFF_Z:OFF_Z + SSD_W], g[..., OFF_XBC:OFF_XBC + SSD_CONV],
                                      g[..., OFF_Z + SSD_W:OFF_Z + SSD_W + SSD_HEADS]], axis=-1)


SHARD_COLS = N_IN // N_DEV


def _in_segments():
    segs, pos = [], 0
    for a, b in _IN_PIECES:
        for i in range(N_DEV):
            lo, hi = max(a, SHARD_COLS * i), min(b, SHARD_COLS * (i + 1))
            if lo < hi:
                segs.append((i, lo - SHARD_COLS * i, hi - lo, pos + lo - a))
        pos += b - a
    return segs


RELAYOUT_ROWS = 512


def _relayout_in(land, own):
    def body(land_ref, own_ref, out_ref):
        me = 4 * lax.axis_index("x") + 2 * lax.axis_index("y") + lax.axis_index("c")
        out_ref[:, N_IN:N_PAD] = jnp.zeros((RELAYOUT_ROWS, N_PAD - N_IN), BF16)
        for i, j, wd, p in _in_segments():
            out_ref[:, p:p + wd] = jnp.where(me == i, own_ref[:, j:j + wd], land_ref[i, :, j:j + wd])

    return pl.pallas_call(
        body, name="relayout_in", grid=(D_MODEL // RELAYOUT_ROWS,),
        in_specs=[pl.BlockSpec((N_DEV, RELAYOUT_ROWS, SHARD_COLS), lambda r: (0, r, 0)),
                  pl.BlockSpec((RELAYOUT_ROWS, SHARD_COLS), lambda r: (r, 0))],
        out_specs=pl.BlockSpec((RELAYOUT_ROWS, N_PAD), lambda r: (r, 0)),
        out_shape=SDS((D_MODEL, N_PAD), BF16),
        compiler_params=_cp(("parallel",)),
    )(land, own)


def _relayout_grad(g):
    def body(g_ref, out_ref):
        for i, j, wd, p in _in_segments():
            out_ref[i, :, j:j + wd] = g_ref[:, p:p + wd].astype(BF16)

    return pl.pallas_call(
        body, name="relayout_grad", grid=(D_MODEL // RELAYOUT_ROWS,),
        in_specs=[pl.BlockSpec((RELAYOUT_ROWS, N_PAD), lambda r: (r, 0))],
        out_specs=pl.BlockSpec((N_DEV, RELAYOUT_ROWS, SHARD_COLS), lambda r: (0, r, 0)),
        out_shape=SDS((N_DEV, D_MODEL, SHARD_COLS), BF16),
        compiler_params=_cp(("parallel",)),
    )(g)


def _block_diag(w):
    w4 = w.reshape(DEPTH, 4, 2, 64, 64)
    z = jnp.zeros((DEPTH, 4, 64, 64), w.dtype)
    top = jnp.concatenate([w4[:, :, 0], z], axis=-1)
    bot = jnp.concatenate([z, w4[:, :, 1]], axis=-1)
    return jnp.concatenate([top, bot], axis=2).astype(BF16)


def _diag_blocks(g):
    return jnp.stack([g[:, :, :64, :64], g[:, :, 64:, 64:]], axis=2).reshape(DEPTH, 8, 64, 64)


def _pad_lanes(v):
    return jnp.pad(v, ((0, 0), (0, LANE - v.shape[1])))


def _lower_bounds(logits):
    p = jax.nn.softmax(logits, axis=0)
    return p, jnp.cumsum(p, axis=0) - p[0]


def _lower_bounds_bwd(p, dlb):
    dp = jnp.cumsum(dlb[::-1], axis=0)[::-1]
    dp = dp.at[0].add(-jnp.sum(dlb, axis=0))
    return p * (dp - jnp.sum(dp * p, axis=0, keepdims=True))


SMALL = ["norm_w", "b_ada", "lru_conv_b", "lru_wa", "lru_ba", "lru_wx", "lru_bx", "lru_lambda", "hg_lb_logits",
         "hg_norm_w", "ssd_conv_b", "ssd_dt_bias", "ssd_a_log", "ssd_d", "ssd_norm_w", "final_norm_w"]
WEIGHTS = ["norm_w", "w_ada", "b_ada", "w_in", "lru_conv_w", "lru_conv_b", "lru_wa", "lru_ba", "lru_wx", "lru_bx",
           "lru_lambda", "hg_lb_logits", "hg_norm_w", "ssd_conv_w", "ssd_conv_b", "ssd_dt_bias", "ssd_a_log", "ssd_d",
           "ssd_norm_w", "w_out", "final_norm_w"]
INPUTS = ["x", "c"] + WEIGHTS + ["loss_target"] + ["m_" + n for n in WEIGHTS] + ["v_" + n for n in WEIGHTS]
SMALL_ROW = 1024


def _small_rows(like):
    out, off = {}, 0
    for n in SMALL:
        rows = -(-int(np.prod(like[n].shape)) // (8 * SMALL_ROW)) * 8
        out[n] = (off, rows)
        off += rows
    return out, off


def _flatten_small(d, prefix="", last=0.0):
    table, _ = _small_rows({n: d[prefix + n] for n in SMALL})
    pieces = []
    for n in SMALL:
        flat = d[prefix + n].reshape(-1)
        pieces.append(jnp.pad(flat, (0, table[n][1] * SMALL_ROW - flat.shape[0])).reshape(-1, SMALL_ROW))
    return jnp.concatenate(pieces + [jnp.full((8, SMALL_ROW), last, F32)], axis=0)


def _split_small(packed, like):
    table, _ = _small_rows(like)
    out = {}
    for n in SMALL:
        off, rows = table[n]
        size = int(np.prod(like[n].shape))
        out[n] = packed[off:off + rows].reshape(-1)[:size].reshape(like[n].shape)
    return out


def _local_step(x, mod, target, w, fetch, emit):
    S = x.shape[0]
    mall = _bfc(_hg_consts())
    mall_t = _bfc(_hg_consts().T)
    consts = _ssd_consts()
    p_lb, lbs = _lower_bounds(w["hg_lb_logits"])
    no_tok = jnp.zeros((8, LANE), F32)
    wa, wx = _block_diag(w["lru_wa"]), _block_diag(w["lru_wx"])
    ba, bx = w["lru_ba"].reshape(DEPTH, 1, LRU_W), w["lru_bx"].reshape(DEPTH, 1, LRU_W)
    lru_cb, lam, ssd_cb = w["lru_conv_b"][:, None], w["lru_lambda"][:, None], w["ssd_conv_b"][:, None]
    bias, alog = _pad_lanes(w["ssd_dt_bias"]), _pad_lanes(w["ssd_a_log"])
    dskip = jnp.repeat(w["ssd_d"], SSD_P, axis=1)
    saved = []
    for l in range(DEPTH):
        w_in_l, w_out_l, token = fetch(l, x)
        shift, scale, gate = (_Row(mod, l, D_MODEL, k) for k in range(3))
        nw = _Row(w["norm_w"], l)
        u, h = _inproj_fwd(x, nw, scale, shift, w_in_l, no_tok if token is None else token)
        ycat = lax.empty((S, D_INNER), BF16)
        lru_args = (l, u, w["lru_conv_w"], lru_cb, wa, ba, wx, bx, lam)
        ycat, h_lru = _lru_fwd(*lru_args, ycat)
        hg_args = (u, _Row(lbs, l), _Row(w["hg_norm_w"], l), mall)
        ycat, o_b, hg_st = _hg_fwd(*hg_args, ycat)
        xbc = _ssdconv_fwd(l, u, w["ssd_conv_w"], ssd_cb)
        ssd_args = (u, xbc, _Row(bias, l), _Row(alog, l), _Row(dskip, l), _Row(w["ssd_norm_w"], l), consts)
        ycat, y_ssd, ssd_st = _ssd_fwd(*ssd_args, ycat)
        token = fetch(l, y_ssd, late=True)
        x_new, y = _outproj_fwd(ycat, w_out_l, x, gate, no_tok if token is None else token)
        saved.append((x, u, h, ycat, nw, scale, gate, w_in_l, w_out_l, lru_args, h_lru, hg_args, o_b, hg_st, ssd_args,
                      y_ssd, ssd_st, y))
        x = x_new
    dx, red = _loss_head(x, w["final_norm_w"][None, :], target)
    loss = red[1, 0]
    reds = {k: [None] * DEPTH for k in ("in", "gate", "lru", "wa", "wx", "hg", "conv", "ssd")}
    for l in reversed(range(DEPTH)):
        (x, u, h, ycat, nw, scale, gate, w_in_l, w_out_l, lru_args, h_lru, hg_args, o_b, hg_st, ssd_args, y_ssd, ssd_st,
         y) = saved[l]
        dycat, g_out, reds["gate"][l] = _outproj_bwd(dx, y, gate, ycat, w_out_l)
        token = emit(l, "w_out", g_out)
        du = lax.empty((S, N_PAD), BF16)
        du, dxbc, reds["ssd"][l] = _ssd_bwd(*ssd_args, y_ssd, ssd_st, dycat, du, no_tok if token is None else token)
        du, reds["conv"][l] = _ssdconv_bwd(l, u, w["ssd_conv_w"], ssd_cb, dxbc, du)
        du, reds["hg"][l] = _hg_bwd(*hg_args, mall_t, o_b, hg_st, dycat, du)
        du, reds["lru"][l], reds["wa"][l], reds["wx"][l] = _lru_bwd(*lru_args, h_lru, dycat, du)
        token = emit(l, "w_in", functools.partial(_inproj_bwd_w, h, du))
        dx, reds["in"][l] = _inproj_bwd_x(du, w_in_l, x, nw, scale, dx, no_tok if token is None else token)
    r = {k: jnp.stack(v) for k, v in reds.items()}
    g = {n: None for n in WEIGHTS}
    g["final_norm_w"] = red[0]
    g["norm_w"] = r["in"][:, 2]
    dmod = jnp.concatenate([r["in"][:, 0], r["in"][:, 1], r["gate"][:, 0]], axis=1)
    g["lru_conv_w"], g["lru_conv_b"] = r["lru"][:, 0:4], r["lru"][:, 4]
    g["lru_ba"], g["lru_bx"] = r["lru"][:, 5].reshape(DEPTH, 8, 64), r["lru"][:, 6].reshape(DEPTH, 8, 64)
    g["lru_lambda"] = r["lru"][:, 7]
    g["lru_wa"], g["lru_wx"] = _diag_blocks(r["wa"]), _diag_blocks(r["wx"])
    g["hg_norm_w"] = r["hg"][:, 0]
    g["hg_lb_logits"] = _lower_bounds_bwd(p_lb, r["hg"][:, 1])
    g["ssd_conv_w"], g["ssd_conv_b"] = r["conv"][:, 0:4], r["conv"][:, 4]
    g["ssd_norm_w"] = r["ssd"][:, 0]
    g["ssd_d"] = r["ssd"][:, 1].reshape(DEPTH, SSD_HEADS, SSD_P).sum(-1)
    g["ssd_dt_bias"] = r["ssd"][:, 2, :SSD_HEADS]
    g["ssd_a_log"] = -r["ssd"][:, 3, :SSD_HEADS] * jnp.exp(w["ssd_a_log"])
    return loss, dx, dmod, g


def kernel(x, c, norm_w, w_ada, b_ada, w_in, lru_conv_w, lru_conv_b, lru_wa, lru_ba, lru_wx, lru_bx, lru_lambda, hg_lb_logits, hg_norm_w, ssd_conv_w, ssd_conv_b, ssd_dt_bias, ssd_a_log, ssd_d, ssd_norm_w, w_out, final_norm_w, loss_target, m_norm_w, m_w_ada, m_b_ada, m_w_in, m_lru_conv_w, m_lru_conv_b, m_lru_wa, m_lru_ba, m_lru_wx, m_lru_bx, m_lru_lambda, m_hg_lb_logits, m_hg_norm_w, m_ssd_conv_w, m_ssd_conv_b, m_ssd_dt_bias, m_ssd_a_log, m_ssd_d, m_ssd_norm_w, m_w_out, m_final_norm_w, v_norm_w, v_w_ada, v_b_ada, v_w_in, v_lru_conv_w, v_lru_conv_b, v_lru_wa, v_lru_ba, v_lru_wx, v_lru_bx, v_lru_lambda, v_hg_lb_logits, v_hg_norm_w, v_ssd_conv_w, v_ssd_conv_b, v_ssd_dt_bias, v_ssd_a_log, v_ssd_d, v_ssd_norm_w, v_w_out, v_final_norm_w):
    return _step(x, c, norm_w, w_ada, b_ada, w_in, lru_conv_w, lru_conv_b, lru_wa, lru_ba, lru_wx, lru_bx, lru_lambda, hg_lb_logits, hg_norm_w, ssd_conv_w, ssd_conv_b, ssd_dt_bias, ssd_a_log, ssd_d, ssd_norm_w, w_out, final_norm_w, loss_target, m_norm_w, m_w_ada, m_b_ada, m_w_in, m_lru_conv_w, m_lru_conv_b, m_lru_wa, m_lru_ba, m_lru_wx, m_lru_bx, m_lru_lambda, m_hg_lb_logits, m_hg_norm_w, m_ssd_conv_w, m_ssd_conv_b, m_ssd_dt_bias, m_ssd_a_log, m_ssd_d, m_ssd_norm_w, m_w_out, m_final_norm_w, v_norm_w, v_w_ada, v_b_ada, v_w_in, v_lru_conv_w, v_lru_conv_b, v_lru_wa, v_lru_ba, v_lru_wx, v_lru_bx, v_lru_lambda, v_hg_lb_logits, v_hg_norm_w, v_ssd_conv_w, v_ssd_conv_b, v_ssd_dt_bias, v_ssd_a_log, v_ssd_d, v_ssd_norm_w, v_w_out, v_final_norm_w)


def _step(*args):
    a = dict(zip(INPUTS, args, strict=True))
    me = 4 * lax.axis_index("x") + 2 * lax.axis_index("y") + lax.axis_index("c")
    x, target = a["x"][0], a["loss_target"][0]

    c_all = _all_gather(a["c"], "gather_c")[:, 0, :]
    b_cols = lax.dynamic_slice_in_dim(a["b_ada"], me * ADA_COLS, ADA_COLS, axis=1)[:, None, :]
    mod_parts = _all_gather(_ada_fwd(c_all, a["w_ada"], b_cols), "gather_mod")
    mod = lax.dynamic_index_in_dim(mod_parts, me, axis=2, keepdims=False)
    mod = mod.transpose(1, 0, 2).reshape(DEPTH, 3 * D_MODEL)

    w = {n: a[n] for n in SMALL}

    w_in_b = [a["w_in"][l].astype(BF16) for l in range(DEPTH)]
    w_out_b = a["w_out"].astype(BF16)
    conv_own = jnp.concatenate([a["lru_conv_w"], a["ssd_conv_w"]], axis=-1)
    cols, rows_out = N_IN // N_DEV, D_INNER // N_DEV

    def gather_start(l, after):
        srcs = [w_in_b[l], w_out_b[l]] + ([conv_own] if l == 0 else [])
        lands = [lax.empty((N_DEV,) + s.shape, s.dtype) for s in srcs]
        return _exchange_start(f"gather_start_{l}", srcs, lands, "chip", after=after)

    def gather_pass(name, st, after, also=()):
        landed = _exchange_wait(name + "_wait", st, after, also)
        st2 = _exchange_start(name + "_pass", st["srcs"], landed, "pass")
        return _exchange_wait(name + "_passed", st2, after)

    gathers = {0: gather_start(0, mod)}
    passing = {}

    def fetch(l, x_l, late=False):
        if late:
            if l + 1 == DEPTH:
                return None
            landed = _exchange_wait(f"gather_{l + 1}_wait", gathers[l + 1], x_l)
            passing[l + 1] = _exchange_start(f"gather_{l + 1}_pass", gathers[l + 1]["srcs"], landed, "pass")
            return passing[l + 1]["token"]
        if l == 0:
            landed = gather_pass("gather_0", gathers[0], x_l, also=(a["w_in"], a["m_w_in"], a["v_w_in"]))
        else:
            landed = _exchange_wait(f"gather_{l}_passed", passing[l], x_l)
        land_out = lax.dynamic_update_index_in_dim(landed[1], w_out_b[l], me, 0)
        if l == 0:
            conv = lax.dynamic_update_index_in_dim(landed[2], conv_own, me, 0).transpose(1, 2, 0, 3)
            w["lru_conv_w"] = conv[..., :64].reshape(DEPTH, 4, LRU_W)
            w["ssd_conv_w"] = conv[..., 64:].reshape(DEPTH, 4, SSD_CONV)
        token = None
        if l + 1 < DEPTH:
            gathers[l + 1] = gather_start(l + 1, land_out)
            token = gathers[l + 1]["token"]
        return _relayout_in(landed[0], w_in_b[l]), land_out.reshape(D_INNER, D_MODEL), token

    PROJ = ("w_in", "w_out")
    scatters = {}
    lands = [lax.empty((N_DEV, DEPTH, D_MODEL, cols), BF16), lax.empty((N_DEV, DEPTH, rows_out, D_MODEL), BF16)]
    own = [None] * DEPTH

    deferred, g_out = {}, {}

    def emit(l, name, grad, after=None):
        if name == "w_out" and l > 0:
            g_out[l] = grad
            return None
        if name == "w_in" and l == 0 and after is None:
            deferred["w_in"] = grad
            return None
        if name == "w_in":
            grad = grad(jnp.zeros((8, LANE), F32) if after is None else after)
        if l == 0:
            k = PROJ.index(name)
            src = _relayout_grad(grad) if name == "w_in" else grad.reshape(N_DEV, rows_out, D_MODEL)
            st = _exchange_start(f"scatter_start_0_{name}", [src], [lands[k]], "scatter", layer=0, after=after)
            scatters[name] = st
            lands[k] = st["lands"][0]
            return st["token"]
        srcs = [_relayout_grad(grad), g_out[l].reshape(N_DEV, rows_out, D_MODEL)]
        st = _exchange_start(f"scatter_start_{l}", srcs, lands, "scatter", layer=l, after=after)
        scatters[l] = st
        lands[:] = st["lands"]
        return st["token"]

    loss_own, dx, dmod, g = _local_step(x, mod, target, w, fetch, emit)

    def sharded(name, parts, own=None, **kw):
        return _adamw(parts, a[name], a["m_" + name], a["v_" + name], "adamw_" + name + kw.pop("tag", ""), own=own, **kw)

    g["b_ada"] = dmod
    small_own = _flatten_small(g, last=loss_own)
    small_st = _exchange_start("gather_small", [small_own], [lax.empty((N_DEV,) + small_own.shape, F32)], "chip",
                               after=dx)
    big = {}
    after = emit(0, "w_in", deferred["w_in"], after=small_st["token"]) + dx[0:8, 0:LANE]

    def own_slices(st):
        return [lax.dynamic_index_in_dim(s, me, 0, keepdims=False) for s in st["srcs"]]

    for l in reversed(range(1, DEPTH)):
        scatters[l]["lands"] = lands
        lands[:] = _exchange_wait(f"scatter_wait_{l}", scatters[l], after)
        own[l] = own_slices(scatters[l])
    scatters["w_out"]["lands"] = [lands[1]]
    lands[1] = _exchange_wait("scatter_wait_0_w_out", scatters["w_out"], after)[0]
    own[0] = [None, own_slices(scatters["w_out"])[0]]
    big["w_out"] = sharded("w_out", lands[1], jnp.stack([own[l][1] for l in range(DEPTH)]))
    upper = sharded("w_in", lands[0], jnp.stack([own[l][0] for l in range(1, DEPTH)]), layers=(1, DEPTH), tag="_upper")
    after = upper[1][0, 0:8, 0:LANE] + big["w_out"][1][0, 0:8, 0:LANE]
    small = gather_pass("gather_small", small_st, after)[0]
    outs = _adamw(small[:, None], *[_flatten_small(a, p)[None] for p in ("", "m_", "v_")], "adamw_small",
                  own=small_own[None])
    res = [_split_small(o[0], a) for o in outs]
    losses = lax.dynamic_update_index_in_dim(small[:, -1, 0], loss_own, me, 0)
    loss = jnp.sum(losses)

    off = _small_rows(a)[0]["b_ada"][0]
    dmod_all = lax.dynamic_update_index_in_dim(small[:, off:off + DEPTH * 3 * D_MODEL // SMALL_ROW],
                                               dmod.reshape(-1, SMALL_ROW), me, 0)
    dmod_all = dmod_all.reshape(N_DEV, DEPTH, 3 * D_MODEL).transpose(1, 0, 2)
    dmod_cols = lax.dynamic_slice_in_dim(dmod_all, me * ADA_COLS, ADA_COLS, axis=2)
    dmod_pad = jnp.pad(dmod_cols, ((0, 0), (0, LANE - N_DEV), (0, 0)))
    ct_pad = jnp.pad(c_all.T, ((0, 0), (0, LANE - N_DEV)))
    big["w_ada"] = sharded("w_ada", _ada_bwd(ct_pad, dmod_pad)[None])
    g_conv = jnp.concatenate([g["lru_conv_w"].reshape(DEPTH, 4, N_DEV, 64), g["ssd_conv_w"].reshape(DEPTH, 4, N_DEV, 192)],
                             axis=-1).transpose(2, 0, 1, 3)
    conv_parts = _all_to_all(g_conv, "scatter_conv")
    big["lru_conv_w"] = sharded("lru_conv_w", conv_parts[..., :64])
    big["ssd_conv_w"] = sharded("ssd_conv_w", conv_parts[..., 64:])

    after = outs[1] + big["w_ada"][1][0, 0:1, 0:1]
    scatters["w_in"]["lands"] = [lands[0]]
    lands[0] = _exchange_wait("scatter_wait_0_w_in", scatters["w_in"], after)[0]
    big["w_in"] = sharded("w_in", lands[0], own_slices(scatters["w_in"])[0][None], layers=(0, 1), prev=upper)

    out = [loss, dx[None]]
    for k in range(4):
        out += [big[n][k] if n in big else res[k][n] for n in WEIGHTS]
    return tuple(out)
```

```python
import functools

import numpy as np
import jax
import jax.numpy as jnp
from jax import lax
from jax.experimental import pallas as pl
from jax.experimental.pallas import tpu as pltpu

F32 = jnp.float32
BF16 = jnp.bfloat16
SDS = jax.ShapeDtypeStruct

N_DEV = 8
DEPTH = 4
D_MODEL = 1024
D_INNER = 2048
EPS = 1e-6
LRU_W = 512
LRU_C = 8.0
HG_W = 512
HG_CHUNK = 64
HG_HEADS = 4
SSD_W = 1024
SSD_HEADS = 16
SSD_P = 64
SSD_N = 128
SSD_CHUNK = 128
SSD_CONV = 1536
N_IN = 5648
N_PAD = 5760
OFF_HG = 0
OFF_LRU = 2048
OFF_XBC = 3072
OFF_Z = 4608
LANE = 128
VMEM_LIMIT = 56 * 1024 * 1024
NEG = -1e30

ADAM_LR = 0.001
ADAM_B1 = 0.9
ADAM_B2 = 0.999
ADAM_EPS = 1e-08
ADAM_WD = 0.01
ADAM_STEP = 10


def _cp(sem=None):
    return pltpu.CompilerParams(dimension_semantics=sem, vmem_limit_bytes=VMEM_LIMIT)


def _dg(a, b, ca, cb):
    return lax.dot_general(a, b, (((ca,), (cb,)), ((), ())), preferred_element_type=F32)


def _mm(a, b):
    return _dg(a, b, 1, 0)


def _mm_nt(a, b):
    return _dg(a, b, 1, 1)


def _mm_tn(a, b):
    return _dg(a, b, 0, 0)


def _bf(x):
    return x.astype(BF16)


def _f(x):
    return x.astype(F32)


def _split3(x):
    hi = x.astype(BF16)
    r = x - hi.astype(F32)
    mid = r.astype(BF16)
    lo = (r - mid.astype(F32)).astype(BF16)
    return hi, mid, lo


def _sel_r(x, m):
    hi, mid, lo = _split3(x)
    return _mm(hi, m) + _mm(mid, m) + _mm(lo, m)


def _sel_l(m, x):
    hi, mid, lo = _split3(x)
    return _mm(m, hi) + _mm(m, mid) + _mm(m, lo)


def _sel_l2(m, x):
    hi = x.astype(BF16)
    lo = (x - hi.astype(F32)).astype(BF16)
    return _mm(m, hi) + _mm(m, lo)


def _sel_tn(x, m):
    hi, mid, lo = _split3(x)
    return _mm_tn(hi, m) + _mm_tn(mid, m) + _mm_tn(lo, m)


def _sigmoid(x):
    return 1.0 / (1.0 + jnp.exp(-x))


def _silu(x):
    return x * _sigmoid(x)


def _dsilu(x):
    s = _sigmoid(x)
    return s * (1.0 + x * (1.0 - s))


def _softplus(x):
    return jnp.maximum(x, 0.0) + jnp.log(1.0 + jnp.exp(-jnp.abs(x)))


def _expm1(z):
    series = z * (1.0 + z * (1.0 / 2) * (1.0 + z * (1.0 / 3) * (1.0 + z * (1.0 / 4) * (
        1.0 + z * (1.0 / 5) * (1.0 + z * (1.0 / 6) * (1.0 + z * (1.0 / 7)))))))
    return jnp.where(jnp.abs(z) < 0.3, series, jnp.exp(z) - 1.0)


def _iota(shape, dim):
    return lax.broadcasted_iota(jnp.int32, shape, dim)


def _last_row(x, rows):
    return jnp.sum(jnp.where(rows == x.shape[0] - 1, x, 0.0), axis=0, keepdims=True)


def _shift_down(x, d, rows, fill=0.0):
    return jnp.where(rows >= d, pltpu.roll(x, d, 0), fill)


def _shift_up(x, d, rows, fill=0.0):
    n = x.shape[0]
    return jnp.where(rows < n - d, pltpu.roll(x, n - d, 0), fill)


def _conv_fwd(x, cw_ref, cb_ref, rows):
    out = cb_ref[...] + cw_ref[pl.ds(3, 1), :] * x
    for k in range(3):
        out = out + cw_ref[pl.ds(k, 1), :] * _shift_down(x, 3 - k, rows)
    return out


def _conv_bwd(x, dco, cw_ref, rows):
    dx = cw_ref[pl.ds(3, 1), :] * dco
    dws = []
    for k in range(3):
        dx = dx + cw_ref[pl.ds(k, 1), :] * _shift_up(dco, 3 - k, rows)
        dws.append(jnp.sum(dco * _shift_down(x, 3 - k, rows), axis=0, keepdims=True))
    dws.append(jnp.sum(dco * x, axis=0, keepdims=True))
    return dx, dws, jnp.sum(dco, axis=0, keepdims=True)


def _vec(n):
    return pl.BlockSpec((1, n), lambda *_: (0, 0))


class _Row:
    def __init__(self, arr, l, n=None, c=0):
        self.arr, self.l, self.n, self.c = arr[:, None, :], l, n or arr.shape[1], c


def _spec(v):
    if isinstance(v, _Row):
        return pl.BlockSpec((None, 1, v.n), lambda *_: (v.l, 0, v.c))
    return _vec(v.shape[1])


def _arr(v):
    return v.arr if isinstance(v, _Row) else v


def _full(shape):
    nd = len(shape)
    return pl.BlockSpec(shape, lambda *_: (0,) * nd)


def _inproj_fwd(x, nw, scale, shift, w, tok):
    S = x.shape[0]
    tm = min(256, S)

    def body(x_ref, nw_ref, sc_ref, sh_ref, w_ref, tok_ref, u_ref, h_ref):
        del tok_ref
        xv = x_ref[...]
        inv = lax.rsqrt(jnp.mean(xv * xv, axis=-1, keepdims=True) + EPS)
        h = ((xv * inv) * nw_ref[...] * (1.0 + sc_ref[...]) + sh_ref[...]).astype(BF16)
        h_ref[...] = h
        u_ref[...] = _mm(h, w_ref[...])

    return pl.pallas_call(
        body, name="inproj_fwd", grid=(S // tm,),
        in_specs=[pl.BlockSpec((tm, D_MODEL), lambda i: (i, 0)), _spec(nw), _spec(scale), _spec(shift),
                  _full((D_MODEL, N_PAD)), pl.BlockSpec(memory_space=pl.ANY)],
        out_specs=[pl.BlockSpec((tm, N_PAD), lambda i: (i, 0)), pl.BlockSpec((tm, D_MODEL), lambda i: (i, 0))],
        out_shape=[SDS((S, N_PAD), F32), SDS((S, D_MODEL), BF16)],
        compiler_params=_cp(("parallel",)),
    )(x, _arr(nw), _arr(scale), _arr(shift), w, tok)


def _inproj_bwd_x(du, w, x, nw, scale, dxn, tok):
    S = x.shape[0]
    tm = min(256, S)

    def body(du_ref, w_ref, x_ref, nw_ref, sc_ref, dxn_ref, tok_ref, dx_ref, red_ref):
        del tok_ref

        @pl.when(pl.program_id(0) == 0)
        def _():
            red_ref[...] = jnp.zeros_like(red_ref)

        dh = _mm_nt(du_ref[...], w_ref[...])
        xv = x_ref[...]
        inv = lax.rsqrt(jnp.mean(xv * xv, axis=-1, keepdims=True) + EPS)
        xhat = xv * inv
        nwv = nw_ref[...]
        g1 = 1.0 + sc_ref[...]
        dxhat = dh * nwv * g1
        dx = inv * (dxhat - xhat * jnp.mean(dxhat * xhat, axis=-1, keepdims=True))
        dx_ref[...] = dxn_ref[...] + dx
        red_ref[0:1, :] += jnp.sum(dh, axis=0, keepdims=True)
        red_ref[1:2, :] += jnp.sum(dh * xhat * nwv, axis=0, keepdims=True)
        red_ref[2:3, :] += jnp.sum(dh * xhat * g1, axis=0, keepdims=True)

    row = pl.BlockSpec((tm, D_MODEL), lambda i: (i, 0))
    return pl.pallas_call(
        body, name="inproj_bwd_x", grid=(S // tm,),
        in_specs=[pl.BlockSpec((tm, N_PAD), lambda i: (i, 0)), _full((D_MODEL, N_PAD)), row, _spec(nw),
                  _spec(scale), row, pl.BlockSpec(memory_space=pl.ANY)],
        out_specs=[row, _full((8, D_MODEL))],
        out_shape=[SDS((S, D_MODEL), F32), SDS((8, D_MODEL), F32)],
        compiler_params=_cp(("arbitrary",)),
    )(du, w, x, _arr(nw), _arr(scale), dxn, tok)


def _inproj_bwd_w(h, du, tok):
    S = h.shape[0]
    tn = 640

    def body(h_ref, du_ref, tok_ref, gw_ref):
        del tok_ref
        gw_ref[...] = _mm_tn(h_ref[...], _bf(du_ref[...]))

    return pl.pallas_call(
        body, name="inproj_bwd_w", grid=(N_PAD // tn,),
        in_specs=[_full((S, D_MODEL)), pl.BlockSpec((S, tn), lambda j: (0, j)), pl.BlockSpec(memory_space=pl.ANY)],
        out_specs=pl.BlockSpec((D_MODEL, tn), lambda j: (0, j)),
        out_shape=SDS((D_MODEL, N_PAD), F32),
        compiler_params=_cp(("parallel",)),
    )(h, du, tok)


def _scan_block(a, b, rows):
    d = 1
    while d < a.shape[0]:
        a_s = _shift_down(a, d, rows, 1.0)
        b_s = _shift_down(b, d, rows, 0.0)
        b = a * b_s + b
        a = a * a_s
        d *= 2
    return a, b


def _rscan_block(c, g, rows):
    d = 1
    while d < c.shape[0]:
        c_s = _shift_up(c, d, rows, 1.0)
        g_s = _shift_up(g, d, rows, 0.0)
        g = g + c * g_s
        c = c * c_s
        d *= 2
    return c, g


LRU_BLOCK = 128


def _lru_gates(xa, wa_ref, ba_ref, wx_ref, bx_ref, lam_ref):
    sp = _softplus(-lam_ref[...])
    xb = _bf(xa)
    r = _sigmoid(_mm(xb, wa_ref[...]) + ba_ref[...])
    ig = _sigmoid(_mm(xb, wx_ref[...]) + bx_ref[...])
    la = -LRU_C * r * sp
    a = jnp.exp(la)
    mult = jnp.sqrt(-_expm1(2.0 * la))
    return sp, r, ig, la, a, mult


def _lru_specs(S, l):
    t128 = pl.BlockSpec((None, 1, LANE), lambda t: (l, 0, t))
    gate = pl.BlockSpec((None, None, LANE, LANE), lambda t: (l, t, 0, 0))
    return [pl.BlockSpec((S, 2 * LANE), lambda t: (0, OFF_LRU // (2 * LANE) + t)),
            pl.BlockSpec((None, 4, LANE), lambda t: (l, 0, t)), t128, gate, t128, gate, t128, t128]


def _lru_fwd(l, u, cw, cb, wa, ba, wx, bx, lam, ycat):
    S = u.shape[0]
    tb = min(LRU_BLOCK, S)

    def body(u_ref, cw_ref, cb_ref, wa_ref, ba_ref, wx_ref, bx_ref, lam_ref, ycat_in, ycat_ref, h_ref, a_scr, b_scr):
        del ycat_in
        rows = _iota((S, LANE), 0)
        xa = _conv_fwd(_f(u_ref[:, 0:LANE]), cw_ref, cb_ref, rows)
        _, _, ig, _, a, mult = _lru_gates(xa, wa_ref, ba_ref, wx_ref, bx_ref, lam_ref)
        a_scr[...] = a
        b_scr[...] = mult * (ig * xa)
        rows_b = _iota((tb, LANE), 0)

        def blk(j, hprev):
            sl = pl.ds(pl.multiple_of(j * tb, tb), tb)
            acum, hloc = _scan_block(a_scr[sl, :], b_scr[sl, :], rows_b)
            hf = hloc + acum * hprev
            h_ref[sl, :] = hf
            return _last_row(hf, rows_b)

        lax.fori_loop(0, S // tb, blk, jnp.zeros((1, LANE), F32))
        ycat_ref[...] = _bf(h_ref[...] * _silu(_f(u_ref[:, LANE:2 * LANE])))

    col = pl.BlockSpec((S, LANE), lambda t: (0, t))
    return pl.pallas_call(
        body, name="lru_fwd", grid=(LRU_W // LANE,),
        in_specs=_lru_specs(S, l) + [pl.BlockSpec(memory_space=pl.ANY)],
        out_specs=[col, col],
        out_shape=[SDS((S, D_INNER), BF16), SDS((S,LRU_W), F32)],
        scratch_shapes=[pltpu.VMEM((S, LANE), F32), pltpu.VMEM((S, LANE), F32)],
        input_output_aliases={8: 0},
        compiler_params=_cp(("parallel",)),
    )(u, cw, cb, wa, ba, wx, bx, lam, ycat)


def _lru_bwd(l, u, cw, cb, wa, ba, wx, bx, lam, h_lru, dycat, du):
    S = u.shape[0]
    tb = min(LRU_BLOCK, S)

    def body(u_ref, cw_ref, cb_ref, wa_ref, ba_ref, wx_ref, bx_ref, lam_ref, h_ref, dy_ref, du_in,
             du_ref, red_ref, gwa_ref, gwx_ref, c_scr, g_scr, l_scr):
        del du_in
        rows = _iota((S, LANE), 0)
        ax = _f(u_ref[:, 0:LANE])
        ag = _f(u_ref[:, LANE:2 * LANE])
        xa = _conv_fwd(ax, cw_ref, cb_ref, rows)
        sp, r, ig, la, a, mult = _lru_gates(xa, wa_ref, ba_ref, wx_ref, bx_ref, lam_ref)
        h = h_ref[...]
        dy = _f(dy_ref[...])
        du_ref[:, LANE:2 * LANE] = _bf(dy * h * _dsilu(ag))
        c_scr[...] = _shift_up(a, 1, rows, 0.0)
        g_scr[...] = dy * _silu(ag)
        rows_b = _iota((tb, LANE), 0)
        nb = S // tb

        def blk(jj, lnext):
            j = nb - 1 - jj
            sl = pl.ds(pl.multiple_of(j * tb, tb), tb)
            ccum, lloc = _rscan_block(c_scr[sl, :], g_scr[sl, :], rows_b)
            lam_t = lloc + ccum * lnext
            l_scr[sl, :] = lam_t
            return jnp.sum(jnp.where(rows_b == 0, lam_t, 0.0), axis=0, keepdims=True)

        lax.fori_loop(0, nb, blk, jnp.zeros((1, LANE), F32))
        db = l_scr[...]
        da = db * _shift_down(h, 1, rows)
        dmult = db * ig * xa
        dig = db * mult * xa
        dxa = db * mult * ig
        dla = da * a - dmult * (a * a) / mult
        dr = -LRU_C * sp * dla
        dsp = jnp.sum(-LRU_C * r * dla, axis=0, keepdims=True)
        dlam = -dsp * _sigmoid(-lam_ref[...])
        dzr = dr * r * (1.0 - r)
        dzi = dig * ig * (1.0 - ig)
        dzr_b, dzi_b, xa_b = _bf(dzr), _bf(dzi), _bf(xa)
        dxa = dxa + _mm_nt(dzr_b, wa_ref[...]) + _mm_nt(dzi_b, wx_ref[...])
        gwa_ref[...] = _mm_tn(xa_b, dzr_b)
        gwx_ref[...] = _mm_tn(xa_b, dzi_b)
        dax, dws, dcb = _conv_bwd(ax, dxa, cw_ref, rows)
        du_ref[:, 0:LANE] = _bf(dax)
        parts = dws + [dcb, jnp.sum(dzr, axis=0, keepdims=True), jnp.sum(dzi, axis=0, keepdims=True), dlam]
        for n, p in enumerate(parts):
            red_ref[pl.ds(n, 1), :] = p

    col = pl.BlockSpec((S, LANE), lambda t: (0, t))
    gw = pl.BlockSpec((None, LANE, LANE), lambda t: (t, 0, 0))
    return pl.pallas_call(
        body, name="lru_bwd", grid=(LRU_W // LANE,),
        in_specs=_lru_specs(S, l) + [col, col, pl.BlockSpec(memory_space=pl.ANY)],
        out_specs=[pl.BlockSpec((S, 2 * LANE), lambda t: (0, OFF_LRU // (2 * LANE) + t)),
                   pl.BlockSpec((8, LANE), lambda t: (0, t)), gw, gw],
        out_shape=[SDS((S, N_PAD), BF16), SDS((8, LRU_W), F32), SDS((4, LANE, LANE), F32), SDS((4, LANE, LANE), F32)],
        scratch_shapes=[pltpu.VMEM((S, LANE), F32)] * 3,
        input_output_aliases={10: 0},
        compiler_params=_cp(("parallel",)),
    )(u, cw, cb, wa, ba, wx, bx, lam, h_lru, dycat, du)


HG_LEVELS = 6


def _hg_consts():
    C = HG_CHUNK
    t = np.arange(C)[:, None]
    r = np.arange(C)[None, :]
    mats = []
    for l in range(HG_LEVELS):
        b = 1 << l
        upper = (t % (2 * b)) >= b
        anchor = (t // (2 * b)) * 2 * b + b - 1
        mats.append((upper & (r > anchor) & (r <= t)) | ((~upper) & (r > t) & (r <= anchor)))
    mats.append(r <= t)
    mats.append(r > t)
    return np.concatenate(mats, 0).astype(np.float32)


def _hg_factors(hf, lb, mall):
    s = _sigmoid(hf)
    f = lb + (1.0 - lb) * s
    lf = jnp.log(f)
    k = (1.0 - lb) * _sigmoid(-hf)
    e = jnp.exp(_sel_l(mall, lf))
    C = HG_CHUNK
    rows = _iota((C, HG_W), 0)
    eq, ek = [], []
    for l in range(HG_LEVELS):
        el = e[l * C:(l + 1) * C]
        eq.append(jnp.where((lax.shift_right_logical(rows, l) & 1) == 1, el, 0.0))
        ek.append(el - eq[l])
    ecum = e[HG_LEVELS * C:(HG_LEVELS + 1) * C]
    erem = e[(HG_LEVELS + 1) * C:(HG_LEVELS + 2) * C]
    return s, f, k, eq, ek, ecum, erem


def _hg_masks():
    C = HG_CHUNK
    ri, ci = _iota((C, C), 0), _iota((C, C), 1)
    rr = _iota((C, LANE), 0)
    gm = [(lax.shift_right_logical(ri, l + 1) == lax.shift_right_logical(ci, l + 1)).astype(F32)
          for l in range(HG_LEVELS)]
    up = [(lax.shift_right_logical(rr, l) & 1) == 1 for l in range(HG_LEVELS)]
    eye = (ri == ci).astype(F32)
    return gm, up, eye, rr


def _hg_scores(qh, kh, eq, ek, sl, gm, up, eye):
    del up
    qs, ks, qb, kb = [], [], [], []
    p = _mm_nt(_bf(qh), _bf(kh)) * eye
    for l in range(HG_LEVELS):
        qs.append(qh * eq[l][:, sl])
        ks.append(kh * ek[l][:, sl])
        qb.append(_bf(qs[l]))
        kb.append(_bf(ks[l]))
        p = p + _mm_nt(qb[l], kb[l]) * gm[l]
    return p, qs, ks, qb, kb


HG_SUB = 4


def _hg_fwd(u, lb, nw, mall, ycat):
    S = u.shape[0]
    C = HG_CHUNK
    n = S // C
    rows = HG_SUB * C

    def body(u_ref, lb_ref, nw_ref, mall_ref, ycat_in, ycat_ref, o_ref, st_ref, st):
        del ycat_in

        @pl.when(pl.program_id(0) == 0)
        def _():
            st[...] = jnp.zeros_like(st)

        gm, up, eye, rr = _hg_masks()
        for sub in range(HG_SUB):
            r = slice(sub * C, (sub + 1) * C)
            q = _silu(_f(u_ref[r, 0:512]))
            v = u_ref[r, 1024:1536]
            _, _, k, eq, ek, ecum, erem = _hg_factors(_f(u_ref[r, 512:1024]), lb_ref[...], mall_ref[...])
            for h in range(HG_HEADS):
                sl = slice(h * LANE, (h + 1) * LANE)
                qh, kh, vh = q[:, sl], k[:, sl], _bf(v[:, sl])
                p = _hg_scores(qh, kh, eq, ek, sl, gm, up, eye)[0]
                sth = st[h]
                st_ref[sub, h] = sth
                o_ref[r, sl] = _mm(_bf(p), vh) + _mm_nt(_bf(qh * ecum[:, sl]), _bf(sth))
                st[h] = sth * _last_row(ecum[:, sl], rr) + _mm_tn(vh, _bf(kh * erem[:, sl]))
            o = o_ref[r, :]
            inv = lax.rsqrt(jnp.mean(o * o, axis=-1, keepdims=True) + EPS)
            ycat_ref[r, :] = _bf((o * inv) * nw_ref[...] * _silu(_f(u_ref[r, 1536:2048])))

    return pl.pallas_call(
        body, name="hg_fwd", grid=(n // HG_SUB,),
        in_specs=[pl.BlockSpec((rows, 2048), lambda i: (i, 0)), _spec(lb), _spec(nw), _full(mall.shape),
                  pl.BlockSpec(memory_space=pl.ANY)],
        out_specs=[pl.BlockSpec((rows, HG_W), lambda i: (i, 1)), pl.BlockSpec((rows, HG_W), lambda i: (i, 0)),
                   pl.BlockSpec((HG_SUB, HG_HEADS, LANE, LANE), lambda i: (i, 0, 0, 0))],
        out_shape=[SDS((S, D_INNER), BF16), SDS((S,HG_W), F32), SDS((n, HG_HEADS, LANE, LANE), F32)],
        scratch_shapes=[pltpu.VMEM((HG_HEADS, LANE, LANE), F32)],
        input_output_aliases={4: 0},
        compiler_params=_cp(("arbitrary",)),
    )(u, _arr(lb), _arr(nw), mall, ycat)


def _hg_bwd(u, lb, nw, mall, mall_t, o_b, states, dycat, du):
    S = u.shape[0]
    C = HG_CHUNK
    n = S // C
    nb = n // HG_SUB
    rows = HG_SUB * C
    L2 = HG_LEVELS

    def body(u_ref, lb_ref, nw_ref, mall_ref, mallt_ref, o_ref, st_ref, dy_ref, du_in, du_ref, red_ref,
             dst, dlast_s, dq_s, dk_s, dex):
        del du_in

        @pl.when(pl.program_id(0) == 0)
        def _():
            dst[...] = jnp.zeros_like(dst)
            red_ref[...] = jnp.zeros_like(red_ref)

        lb = lb_ref[...]
        nwv = nw_ref[...]
        gm, up, eye, rr = _hg_masks()
        for sub in reversed(range(HG_SUB)):
            r = slice(sub * C, (sub + 1) * C)
            hq, hf, hg = _f(u_ref[r, 0:512]), _f(u_ref[r, 512:1024]), _f(u_ref[r, 1536:2048])
            q = _silu(hq)
            v = u_ref[r, 1024:1536]
            s, f, k, eq, ek, ecum, erem = _hg_factors(hf, lb, mall_ref[...])
            o = o_ref[r, :]
            dy = _f(dy_ref[r, :])
            inv = lax.rsqrt(jnp.mean(o * o, axis=-1, keepdims=True) + EPS)
            ohat = o * inv
            du_ref[r, 1536:2048] = _bf(dy * ohat * nwv * _dsilu(hg))
            dn = dy * _silu(hg)
            red_ref[0:1, :] += jnp.sum(dn * ohat, axis=0, keepdims=True)
            dohat = dn * nwv
            do = inv * (dohat - ohat * jnp.mean(dohat * ohat, axis=-1, keepdims=True))
            for h in range(HG_HEADS):
                sl = slice(h * LANE, (h + 1) * LANE)
                qh, kh, vh, doh = q[:, sl], k[:, sl], _bf(v[:, sl]), _bf(do[:, sl])
                p, qs, ks, qb, kb = _hg_scores(qh, kh, eq, ek, sl, gm, up, eye)
                st_f = st_ref[sub, h]
                sth = _bf(st_f)
                dsth = dst[h]
                dsth_b = _bf(dsth)
                qt = qh * ecum[:, sl]
                kt = kh * erem[:, sl]
                elast = _last_row(ecum[:, sl], rr)
                dp = _mm_nt(doh, vh)
                du_ref[r, 1024 + h * LANE:1024 + (h + 1) * LANE] = _bf(_mm_tn(_bf(p), doh) + _mm_nt(_bf(kt), dsth_b))
                dpe = _bf(dp * eye)
                dqt = _mm(doh, sth)
                dkt = _mm(vh, dsth_b)
                dq = dqt * ecum[:, sl] + _mm(dpe, _bf(kh))
                dk = dkt * erem[:, sl] + _mm_tn(dpe, _bf(qh))
                dex[sub, L2 * C:(L2 + 1) * C, sl] = dqt * qt
                dex[sub, (L2 + 1) * C:(L2 + 2) * C, sl] = dkt * kt
                for l in range(HG_LEVELS):
                    dpl = _bf(dp * gm[l])
                    dql = _mm(dpl, kb[l])
                    dkl = _mm_tn(dpl, qb[l])
                    dq = dq + dql * eq[l][:, sl]
                    dk = dk + dkl * ek[l][:, sl]
                    dex[sub, l * C:(l + 1) * C, sl] = dql * qs[l] + dkl * ks[l]
                dlast_s[sub, :, sl] = jnp.sum(dsth * st_f, axis=0, keepdims=True) * elast
                dst[h] = dsth * elast + _mm_tn(doh, _bf(qt))
                dq_s[sub, :, sl] = dq
                dk_s[sub, :, sl] = dk
            dq = dq_s[sub]
            dk = dk_s[sub]
            dlf = _sel_l2(mallt_ref[...], dex[sub]) + dlast_s[sub]
            du_ref[r, 0:512] = _bf(dq * _dsilu(hq))
            t = (1.0 - s) * (dlf / f - dk)
            du_ref[r, 512:1024] = _bf((1.0 - lb) * s * t)
            red_ref[1:2, :] += jnp.sum(t, axis=0, keepdims=True)

    rev = lambda i: (nb - 1 - i, 0)
    return pl.pallas_call(
        body, name="hg_bwd", grid=(nb,),
        in_specs=[pl.BlockSpec((rows, 2048), rev), _spec(lb), _spec(nw), _full(mall.shape), _full(mall_t.shape),
                  pl.BlockSpec((rows, HG_W), rev),
                  pl.BlockSpec((HG_SUB, HG_HEADS, LANE, LANE), lambda i: (nb - 1 - i, 0, 0, 0)),
                  pl.BlockSpec((rows, HG_W), lambda i: (nb - 1 - i, 1)), pl.BlockSpec(memory_space=pl.ANY)],
        out_specs=[pl.BlockSpec((rows, 2048), rev), pl.BlockSpec((8, HG_W), lambda i: (0, 0))],
        out_shape=[SDS((S, N_PAD), BF16), SDS((8, HG_W), F32)],
        scratch_shapes=[pltpu.VMEM((HG_HEADS, LANE, LANE), F32), pltpu.VMEM((HG_SUB, 1, HG_W), F32),
                        pltpu.VMEM((HG_SUB, C, HG_W), F32), pltpu.VMEM((HG_SUB, C, HG_W), F32),
                        pltpu.VMEM((HG_SUB, (L2 + 2) * C, HG_W), F32)],
        input_output_aliases={8: 0},
        compiler_params=_cp(("arbitrary",)),
    )(u, _arr(lb), _arr(nw), mall, mall_t, o_b, states, dycat, du)


def _ssdconv_fwd(l, u, cw, cb):
    S = u.shape[0]

    def body(u_ref, cw_ref, cb_ref, out_ref):
        rows = _iota((S, LANE), 0)
        out_ref[...] = _silu(_conv_fwd(_f(u_ref[...]), cw_ref, cb_ref, rows))

    return pl.pallas_call(
        body, name="ssdconv_fwd", grid=(SSD_CONV // LANE,),
        in_specs=[pl.BlockSpec((S, LANE), lambda t: (0, OFF_XBC // LANE + t)),
                  pl.BlockSpec((None, 4, LANE), lambda t: (l, 0, t)), pl.BlockSpec((None, 1, LANE), lambda t: (l, 0, t))],
        out_specs=pl.BlockSpec((S, LANE), lambda t: (0, t)),
        out_shape=SDS((S, SSD_CONV), F32),
        compiler_params=_cp(("parallel",)),
    )(u, cw, cb)


def _ssdconv_bwd(l, u, cw, cb, dxbc, du):
    S = u.shape[0]

    def body(u_ref, cw_ref, cb_ref, d_ref, du_in, du_ref, red_ref):
        del du_in
        rows = _iota((S, LANE), 0)
        x = _f(u_ref[...])
        dco = d_ref[...] * _dsilu(_conv_fwd(x, cw_ref, cb_ref, rows))
        dx, dws, dcb = _conv_bwd(x, dco, cw_ref, rows)
        du_ref[...] = _bf(dx)
        for n, p in enumerate(dws + [dcb]):
            red_ref[pl.ds(n, 1), :] = p
        red_ref[pl.ds(5, 3), :] = jnp.zeros((3, LANE), F32)

    ucol = pl.BlockSpec((S, LANE), lambda t: (0, OFF_XBC // LANE + t))
    return pl.pallas_call(
        body, name="ssdconv_bwd", grid=(SSD_CONV // LANE,),
        in_specs=[ucol, pl.BlockSpec((None, 4, LANE), lambda t: (l, 0, t)),
                  pl.BlockSpec((None, 1, LANE), lambda t: (l, 0, t)),
                  pl.BlockSpec((S, LANE), lambda t: (0, t)), pl.BlockSpec(memory_space=pl.ANY)],
        out_specs=[ucol, pl.BlockSpec((8, LANE), lambda t: (0, t))],
        out_shape=[SDS((S, N_PAD), BF16), SDS((8, SSD_CONV), F32)],
        input_output_aliases={4: 0},
        compiler_params=_cp(("parallel",)),
    )(u, cw, cb, dxbc, du)


SSD_SUB = 2


def _ssd_consts():
    e64 = np.zeros((LANE, SSD_W), np.float32)
    for h in range(SSD_HEADS):
        e64[h, h * SSD_P:(h + 1) * SSD_P] = 1.0
    T = SSD_CHUNK
    tril = (np.arange(T)[None, :] <= np.arange(T)[:, None]).astype(np.float32)
    return e64, tril, tril.T.copy()


def _ssd_common(zdt, bias_ref, alog_ref, tril, e64, cum_ref, cumt_ref):
    T = SSD_CHUNK
    lane = _iota((1, LANE), 1)
    a_neg = jnp.where(lane < SSD_HEADS, -jnp.exp(alog_ref[...]), 0.0)
    dtpre = zdt[:, SSD_W:SSD_W + LANE] + bias_ref[...]
    dt = _softplus(dtpre)
    cum = _sel_l(tril, dt * a_neg)
    cum_ref[...] = cum
    cumt_ref[...] = cum.T
    cum_x = _sel_r(cum, e64)
    last_x = _last_row(cum_x, _iota((T, SSD_W), 0))
    ecum_x = jnp.exp(cum_x)
    erem_x = jnp.exp(last_x - cum_x)
    elast_x = jnp.exp(last_x)
    dt_x = _sel_r(dt, e64)
    return a_neg, dtpre, dt, ecum_x, erem_x, elast_x, dt_x


def _ssd_decay(cum_ref, cumt_ref, h, causal):
    T = SSD_CHUNK
    diff = jnp.broadcast_to(cum_ref[:, pl.ds(h, 1)], (T, T)) - cumt_ref[pl.ds(h, 1), :]
    return jnp.exp(jnp.where(causal, diff, NEG))


def _group_norm_fwd(y1, nwv):
    outs, invs = [], []
    for g in range(2):
        seg = y1[:, g * 512:(g + 1) * 512]
        inv = lax.rsqrt(jnp.mean(seg * seg, axis=-1, keepdims=True) + EPS)
        outs.append(seg * inv * nwv[:, g * 512:(g + 1) * 512])
        invs.append(inv)
    return outs, invs


def _ssd_fwd(u, xbc, bias, alog, dskip_x, nw, consts, ycat):
    S = u.shape[0]
    T = SSD_CHUNK
    n = S // T
    rows = SSD_SUB * T
    e64, tril, _ = consts

    def body(u_ref, xbc_ref, bias_ref, alog_ref, dx_ref, nw_ref, e64_ref, tril_ref, ycat_in,
             ycat_ref, y_ref, st_ref, st, cumt, cum_e):
        del ycat_in

        @pl.when(pl.program_id(0) == 0)
        def _():
            st[...] = jnp.zeros_like(st)

        causal = _iota((T, T), 0) >= _iota((T, T), 1)
        lo = _iota((T, LANE), 1) < SSD_P
        for sub in range(SSD_SUB):
            r = slice(sub * T, (sub + 1) * T)
            zdt = _f(u_ref[r, :])
            z = zdt[:, 0:SSD_W]
            xs = xbc_ref[r, 0:SSD_W]
            cum_r, cumt_r = cum_e.at[sub], cumt.at[sub]
            _, _, _, ecum_x, erem_x, elast_x, dt_x = _ssd_common(
                zdt, bias_ref, alog_ref, tril_ref[...], e64_ref[...], cum_r, cumt_r)
            xdt = xs * dt_x
            xrem = xdt * erem_x
            st_ref[sub] = st[...]
            for g in range(2):
                gs = slice(g * 512, (g + 1) * 512)
                bg = _bf(xbc_ref[r, SSD_W + g * LANE:SSD_W + (g + 1) * LANE])
                cg = _bf(xbc_ref[r, SSD_W + 256 + g * LANE:SSD_W + 256 + (g + 1) * LANE])
                cb = _mm_nt(cg, bg)
                yin = _mm(cg, _bf(st[:, gs])) * ecum_x[:, gs]
                for j in range(4):
                    h0 = 8 * g + 2 * j
                    cs = slice(h0 * SSD_P, (h0 + 2) * SSD_P)
                    xp = xdt[:, cs]
                    s0 = _bf(cb * _ssd_decay(cum_r, cumt_r, h0, causal))
                    s1 = _bf(cb * _ssd_decay(cum_r, cumt_r, h0 + 1, causal))
                    y_ref[r, cs] = (_mm(s0, _bf(jnp.where(lo, xp, 0.0))) + _mm(s1, _bf(jnp.where(lo, 0.0, xp)))
                                    + yin[:, j * LANE:(j + 1) * LANE])
                st[:, gs] = st[:, gs] * elast_x[:, gs] + _mm_tn(bg, _bf(xrem[:, gs]))
            y1 = (y_ref[r, :] + dx_ref[...] * xs) * _silu(z)
            outs, _ = _group_norm_fwd(y1, nw_ref[...])
            for g in range(2):
                ycat_ref[r, g * 512:(g + 1) * 512] = _bf(outs[g])

    return pl.pallas_call(
        body, name="ssd_fwd", grid=(n // SSD_SUB,),
        in_specs=[pl.BlockSpec((rows, SSD_W + LANE), lambda i: (i, OFF_Z // (SSD_W + LANE))),
                  pl.BlockSpec((rows, SSD_CONV), lambda i: (i, 0)), _spec(bias), _spec(alog), _spec(dskip_x), _spec(nw),
                  _full(e64.shape), _full(tril.shape), pl.BlockSpec(memory_space=pl.ANY)],
        out_specs=[pl.BlockSpec((rows, SSD_W), lambda i: (i, 1)), pl.BlockSpec((rows, SSD_W), lambda i: (i, 0)),
                   pl.BlockSpec((SSD_SUB, SSD_N, SSD_W), lambda i: (i, 0, 0))],
        out_shape=[SDS((S, D_INNER), BF16), SDS((S,SSD_W), F32), SDS((n, SSD_N, SSD_W), F32)],
        scratch_shapes=[pltpu.VMEM((SSD_N, SSD_W), F32), pltpu.VMEM((SSD_SUB, LANE, T), F32),
                        pltpu.VMEM((SSD_SUB, T, LANE), F32)],
        input_output_aliases={8: 0},
        compiler_params=_cp(("arbitrary",)),
    )(u, xbc, _arr(bias), _arr(alog), _arr(dskip_x), _arr(nw), _bfc(e64), _bfc(tril), ycat)


def _ssd_bwd(u, xbc, bias, alog, dskip_x, nw, consts, y_ssd, states, dycat, du, tok):
    S = u.shape[0]
    T = SSD_CHUNK
    n = S // T
    e64, tril, triu = consts
    e64t = np.ascontiguousarray(e64.T)

    def chunk(u_ref, xbc_ref, bias_ref, alog_ref, dx_ref, nw_ref, e64_ref, e64t_ref, tril_ref, triu_ref,
              y_ref, st_ref, dy_ref, du_ref, dxbc_ref, red_ref, dst, dl_s, cumt, dxdt_s, dy0_s, gb_s, gc_s, cum_e, cs_s):
        zdt = _f(u_ref[...])
        z = zdt[:, 0:SSD_W]
        xs = xbc_ref[:, 0:SSD_W]
        a_neg, dtpre, dt, ecum_x, erem_x, elast_x, dt_x = _ssd_common(
            zdt, bias_ref, alog_ref, tril_ref[...], e64_ref[...], cum_e, cumt)
        causal = _iota((T, T), 0) >= _iota((T, T), 1)
        lo = _iota((T, LANE), 1) < SSD_P
        xdt = xs * dt_x
        xrem = xdt * erem_x
        y = y_ref[...]
        dxv = dx_ref[...]
        nwv = nw_ref[...]
        sz = _silu(z)
        y0 = y + dxv * xs
        y1 = y0 * sz
        for g in range(2):
            gs = slice(g * 512, (g + 1) * 512)
            seg = y1[:, gs]
            inv = lax.rsqrt(jnp.mean(seg * seg, axis=-1, keepdims=True) + EPS)
            shat = seg * inv
            dyg = _f(dy_ref[:, gs])
            red_ref[0:1, gs] += jnp.sum(dyg * shat, axis=0, keepdims=True)
            dsh = dyg * nwv[:, gs]
            dy1g = inv * (dsh - shat * jnp.mean(dsh * shat, axis=-1, keepdims=True))
            du_ref[:, gs] = _bf(dy1g * y0[:, gs] * _dsilu(z[:, gs]))
            dy0_s[:, gs] = dy1g * sz[:, gs]
        dy0 = dy0_s[...]
        red_ref[1:2, :] += jnp.sum(dy0 * xs, axis=0, keepdims=True)
        dyin = dy0 * ecum_x
        lane = _iota((T, LANE), 1)
        dcum = jnp.zeros((T, LANE), F32)

        def decay_grad(h, gm):
            cs_s[pl.ds(h, 1), :] = jnp.sum(gm, axis=0, keepdims=True)
            return jnp.where(lane == h, jnp.sum(gm, axis=1, keepdims=True), 0.0)

        for g in range(2):
            gs = slice(g * 512, (g + 1) * 512)
            bg = _bf(xbc_ref[:, SSD_W + g * LANE:SSD_W + (g + 1) * LANE])
            cg = _bf(xbc_ref[:, SSD_W + 256 + g * LANE:SSD_W + 256 + (g + 1) * LANE])
            cb = _mm_nt(cg, bg)
            dst_f, st_f = dst[:, gs], st_ref[:, gs]
            dstg = _bf(dst_f)
            stg = _bf(st_f)
            dyin_g = _bf(dyin[:, gs])
            xrem_g = _bf(xrem[:, gs])
            dcb = jnp.zeros((T, T), F32)
            dxr = _mm(bg, dstg)
            dxdt_s[:, gs] = dxr * erem_x[:, gs]
            gc_s[:, gs] = dxr * xrem[:, gs]
            gb_s[:, gs] = dyin[:, gs] * _mm(cg, stg)
            dl_s[:, gs] = jnp.sum(dst_f * st_f, axis=0, keepdims=True) * elast_x[:, gs]
            for j in range(4):
                h0 = 8 * g + 2 * j
                cs = slice(h0 * SSD_P, (h0 + 2) * SSD_P)
                xp = xdt[:, cs]
                dyp = dy0[:, cs]
                x_lo, x_hi = _bf(jnp.where(lo, xp, 0.0)), _bf(jnp.where(lo, 0.0, xp))
                d_lo, d_hi = _bf(jnp.where(lo, dyp, 0.0)), _bf(jnp.where(lo, 0.0, dyp))
                l0 = _ssd_decay(cum_e, cumt, h0, causal)
                l1 = _ssd_decay(cum_e, cumt, h0 + 1, causal)
                s0 = cb * l0
                s1 = cb * l1
                ds0 = _mm_nt(d_lo, x_lo)
                ds1 = _mm_nt(d_hi, x_hi)
                dcb = dcb + ds0 * l0 + ds1 * l1
                dxdt_s[:, cs] += _mm_tn(_bf(s0), d_lo) + _mm_tn(_bf(s1), d_hi)
                dcum = dcum + decay_grad(h0, ds0 * s0) + decay_grad(h0 + 1, ds1 * s1)
            dcb_b = _bf(dcb)
            dxbc_ref[:, SSD_W + g * LANE:SSD_W + (g + 1) * LANE] = _mm_tn(dcb_b, cg) + _mm_nt(xrem_g, dstg)
            dxbc_ref[:, SSD_W + 256 + g * LANE:SSD_W + 256 + (g + 1) * LANE] = _mm(dcb_b, bg) + _mm_nt(dyin_g, stg)
            dst[:, gs] = dst_f * elast_x[:, gs] + _mm_tn(cg, dyin_g)
        dxdt = dxdt_s[...]
        dxbc_ref[:, 0:SSD_W] = dxdt * dt_x + dy0 * dxv
        e64t = e64t_ref[...]
        gc = gc_s[...]
        dlast_x = jnp.sum(gc, axis=0, keepdims=True) + dl_s[...]
        dlast = jnp.max(_sel_r(jnp.broadcast_to(dlast_x, (8, SSD_W)), e64t), axis=0, keepdims=True)
        dcum = (dcum - cs_s[...].T + _sel_r(gb_s[...] - gc, e64t)
                + jnp.where(_iota((T, LANE), 0) == T - 1, dlast, 0.0))
        dda = _sel_l(triu_ref[...], dcum)
        ddt = dda * a_neg + _sel_r(dxdt * xs, e64t)
        ddtpre = ddt * _sigmoid(dtpre)
        du_ref[:, SSD_W:SSD_W + LANE] = _bf(jnp.where(lane < SSD_HEADS, ddtpre, 0.0))
        red_ref[2:3, 0:LANE] += jnp.sum(ddtpre, axis=0, keepdims=True)
        red_ref[3:4, 0:LANE] += jnp.sum(dda * dt, axis=0, keepdims=True)

    def body(u_ref, xbc_ref, bias_ref, alog_ref, dx_ref, nw_ref, e64_ref, e64t_ref, tril_ref, triu_ref,
             y_ref, st_ref, dy_ref, du_in, tok_ref, du_ref, dxbc_ref, red_ref, dst, *scratch):
        del du_in, tok_ref

        @pl.when(pl.program_id(0) == 0)
        def _():
            dst[...] = jnp.zeros_like(dst)
            red_ref[...] = jnp.zeros_like(red_ref)
            scratch[-1][...] = jnp.zeros_like(scratch[-1])

        for sub in reversed(range(SSD_SUB)):
            rs = pl.ds(sub * T, T)
            chunk(u_ref.at[rs], xbc_ref.at[rs], bias_ref, alog_ref, dx_ref, nw_ref, e64_ref, e64t_ref, tril_ref, triu_ref,
                  y_ref.at[rs], st_ref.at[sub], dy_ref.at[rs], du_ref.at[rs], dxbc_ref.at[rs], red_ref, dst,
                  *[s.at[sub] for s in scratch])

    nb = n // SSD_SUB
    rows = SSD_SUB * T
    rev = lambda i: (nb - 1 - i, 0)
    sub_scratch = [(1, SSD_W), (LANE, T)] + [(T, SSD_W)] * 4 + [(T, LANE), (LANE, T)]
    return pl.pallas_call(
        body, name="ssd_bwd", grid=(nb,),
        in_specs=[pl.BlockSpec((rows, SSD_W + LANE), lambda i: (nb - 1 - i, OFF_Z // (SSD_W + LANE))),
                  pl.BlockSpec((rows, SSD_CONV), rev), _spec(bias), _spec(alog), _spec(dskip_x), _spec(nw),
                  _full(e64.shape), _full(e64t.shape), _full(tril.shape), _full(triu.shape),
                  pl.BlockSpec((rows, SSD_W), rev), pl.BlockSpec((SSD_SUB, SSD_N, SSD_W), lambda i: (nb - 1 - i, 0, 0)),
                  pl.BlockSpec((rows, SSD_W), lambda i: (nb - 1 - i, 1)), pl.BlockSpec(memory_space=pl.ANY),
                  pl.BlockSpec(memory_space=pl.ANY)],
        out_specs=[pl.BlockSpec((rows, SSD_W + LANE), lambda i: (nb - 1 - i, OFF_Z // (SSD_W + LANE))),
                   pl.BlockSpec((rows, SSD_CONV), rev), pl.BlockSpec((8, SSD_W), lambda i: (0, 0))],
        out_shape=[SDS((S, N_PAD), BF16), SDS((S, SSD_CONV), F32), SDS((8, SSD_W), F32)],
        scratch_shapes=[pltpu.VMEM((SSD_N, SSD_W), F32)] + [pltpu.VMEM((SSD_SUB,) + s, F32) for s in sub_scratch],
        input_output_aliases={13: 0},
        compiler_params=_cp(("arbitrary",)),
    )(u, xbc, _arr(bias), _arr(alog), _arr(dskip_x), _arr(nw), _bfc(e64), _bfc(e64t), _bfc(tril), _bfc(triu), y_ssd,
      states, dycat, du, tok)


def _bfc(a):
    return jnp.asarray(a, BF16)


def _outproj_fwd(ycat, wo, x, gate, tok):
    S = x.shape[0]
    tm = min(512, S)

    def body(yc_ref, wo_ref, x_ref, g_ref, tok_ref, xn_ref, y_ref):
        del tok_ref
        y = _mm(_bf(yc_ref[...]), wo_ref[...])
        y_ref[...] = y
        xn_ref[...] = x_ref[...] + g_ref[...] * y

    row = pl.BlockSpec((tm, D_MODEL), lambda i: (i, 0))
    return pl.pallas_call(
        body, name="outproj_fwd", grid=(S // tm,),
        in_specs=[pl.BlockSpec((tm, D_INNER), lambda i: (i, 0)), _full((D_INNER, D_MODEL)), row, _spec(gate),
                  pl.BlockSpec(memory_space=pl.ANY)],
        out_specs=[row, row],
        out_shape=[SDS((S, D_MODEL), F32), SDS((S, D_MODEL), F32)],
        compiler_params=_cp(("parallel",)),
    )(ycat, wo, x, _arr(gate), tok)


def _outproj_bwd(dxn, y, gate, ycat, wo):
    S = dxn.shape[0]
    tm = min(512, S)

    def body(dx_ref, y_ref, g_ref, yc_ref, wo_ref, dyc_ref, gwo_ref, dg_ref, acc):
        @pl.when(pl.program_id(0) == 0)
        def _():
            acc[...] = jnp.zeros_like(acc)
            dg_ref[...] = jnp.zeros_like(dg_ref)

        dxv = dx_ref[...]
        dy = _bf(dxv * g_ref[...])
        dg_ref[0:1, :] += jnp.sum(dxv * y_ref[...], axis=0, keepdims=True)
        dyc_ref[...] = _mm_nt(dy, wo_ref[...])
        acc[...] += _mm_tn(_bf(yc_ref[...]), dy)

        @pl.when(pl.program_id(0) == pl.num_programs(0) - 1)
        def _():
            gwo_ref[...] = acc[...].astype(BF16)

    row = pl.BlockSpec((tm, D_MODEL), lambda i: (i, 0))
    wide = pl.BlockSpec((tm, D_INNER), lambda i: (i, 0))
    return pl.pallas_call(
        body, name="outproj_bwd", grid=(S // tm,),
        in_specs=[row, row, _spec(gate), wide, _full((D_INNER, D_MODEL))],
        out_specs=[wide, _full((D_INNER, D_MODEL)), _full((8, D_MODEL))],
        out_shape=[SDS((S, D_INNER), F32), SDS((D_INNER, D_MODEL), BF16), SDS((8, D_MODEL), F32)],
        scratch_shapes=[pltpu.VMEM((D_INNER, D_MODEL), F32)],
        compiler_params=_cp(("arbitrary",)),
    )(dxn, y, _arr(gate), ycat, wo)


def _loss_head(x, fw, target):
    S = x.shape[0]
    tm = min(512, S)

    def body(x_ref, fw_ref, t_ref, dx_ref, red_ref):
        @pl.when(pl.program_id(0) == 0)
        def _():
            red_ref[...] = jnp.zeros_like(red_ref)

        xv = x_ref[...]
        fwv = fw_ref[...]
        inv = lax.rsqrt(jnp.mean(xv * xv, axis=-1, keepdims=True) + EPS)
        xhat = xv * inv
        err = xhat * fwv - t_ref[...]
        col = jnp.sum(err * err, axis=0, keepdims=True)
        red_ref[1:2, :] += jnp.broadcast_to(jnp.sum(col, axis=1, keepdims=True) * (0.5 / D_MODEL), (1, D_MODEL))
        dy = err * (1.0 / D_MODEL)
        red_ref[0:1, :] += jnp.sum(dy * xhat, axis=0, keepdims=True)
        dxhat = dy * fwv
        dx_ref[...] = inv * (dxhat - xhat * jnp.mean(dxhat * xhat, axis=-1, keepdims=True))

    row = pl.BlockSpec((tm, D_MODEL), lambda i: (i, 0))
    return pl.pallas_call(
        body, name="loss_head", grid=(S // tm,),
        in_specs=[row, _vec(D_MODEL), row],
        out_specs=[row, _full((8, D_MODEL))],
        out_shape=[SDS((S, D_MODEL), F32), SDS((8, D_MODEL), F32)],
        compiler_params=_cp(("arbitrary",)),
    )(x, fw, target)


ADA_COLS = 3 * D_MODEL // N_DEV


def _ada_fwd(c_all, w_ada, b_cols):
    def body(c_ref, w_ref, b_ref, out_ref):
        out_ref[...] = _mm(_bf(_silu(c_ref[...])), _bf(w_ref[...])) + b_ref[...]

    return pl.pallas_call(
        body, name="ada_fwd", grid=(DEPTH,),
        in_specs=[_full((N_DEV, D_MODEL)), pl.BlockSpec((None, D_MODEL, ADA_COLS), lambda l: (l, 0, 0)),
                  pl.BlockSpec((None, 1, ADA_COLS), lambda l: (l, 0, 0))],
        out_specs=pl.BlockSpec((None, N_DEV, ADA_COLS), lambda l: (l, 0, 0)),
        out_shape=SDS((DEPTH, N_DEV, ADA_COLS), F32),
        compiler_params=_cp(("parallel",)),
    )(c_all, w_ada, b_cols)


def _ada_bwd(ct_pad, dmod_pad):
    def body(c_ref, d_ref, out_ref):
        out_ref[...] = _mm(_bf(_silu(c_ref[...])), _bf(d_ref[...]))

    return pl.pallas_call(
        body, name="ada_bwd", grid=(DEPTH,),
        in_specs=[_full((D_MODEL, LANE)), pl.BlockSpec((None, LANE, ADA_COLS), lambda l: (l, 0, 0))],
        out_specs=pl.BlockSpec((None, D_MODEL, ADA_COLS), lambda l: (l, 0, 0)),
        out_shape=SDS((DEPTH, D_MODEL, ADA_COLS), F32),
        compiler_params=_cp(("parallel",)),
    )(ct_pad, dmod_pad)


def _adamw(parts, w, m, v, name, own=None, layers=None, prev=None):
    n, L, R, C = parts.shape
    lo, hi = layers or (0, L)
    tr = R
    while tr * C * 4 > (1 << 20) and tr % 16 == 0:
        tr //= 2
    first = 1 if own is None else 2

    def body(*refs):
        p_ref = refs[0]
        w_ref, m_ref, v_ref = refs[first:first + 3]
        g_ref, d_ref, mo_ref, vo_ref = refs[-4:]

        def part(k):
            if own is None:
                return p_ref[k].astype(F32)
            me = 4 * lax.axis_index("x") + 2 * lax.axis_index("y") + lax.axis_index("c")
            return jnp.where(me == k, refs[1][...], p_ref[k]).astype(F32)

        g = part(0)
        for k in range(1, n):
            g = g + part(k)
        mn = ADAM_B1 * m_ref[...] + (1.0 - ADAM_B1) * g
        vn = ADAM_B2 * v_ref[...] + (1.0 - ADAM_B2) * (g * g)
        m_hat = mn / (1.0 - ADAM_B1 ** ADAM_STEP)
        v_hat = vn / (1.0 - ADAM_B2 ** ADAM_STEP)
        g_ref[...] = g
        d_ref[...] = -ADAM_LR * (m_hat / (jnp.sqrt(v_hat) + ADAM_EPS) + ADAM_WD * w_ref[...])
        mo_ref[...] = mn
        vo_ref[...] = vn

    blk = pl.BlockSpec((None, tr, C), lambda l, i: (lo + l, i, 0))
    own_blk = [] if own is None else [pl.BlockSpec((None, tr, C), lambda l, i: (l, i, 0))]
    n_blk = 3 if own is None else 4
    return pl.pallas_call(
        body, name=name, grid=(hi - lo, R // tr),
        in_specs=[pl.BlockSpec((n, None, tr, C), lambda l, i: (0, lo + l, i, 0))] + own_blk + [blk] * 3
        + ([] if prev is None else [ANY] * 4),
        out_specs=[blk] * 4,
        out_shape=[SDS((L, R, C), F32)] * 4,
        input_output_aliases={} if prev is None else {1 + n_blk + k: k for k in range(4)},
        compiler_params=_cp(("parallel", "parallel")),
    )(parts, *([] if own is None else [own]), w, m, v, *([] if prev is None else prev))


MESH = pl.DeviceIdType.MESH
ANY = pl.BlockSpec(memory_space=pl.ANY)


def _all_gather(v, name):
    def body(v_ref, out_ref, send_sems, recv_sems, local_sem):
        x, y, c = lax.axis_index("x"), lax.axis_index("y"), lax.axis_index("c")
        me, sibling = (x, y, c), (x, y, 1 - c)
        chips = [(1 - x, y), (x, 1 - y), (1 - x, 1 - y)]

        def slot(px, py, pc):
            return out_ref.at[4 * px + 2 * py + pc]

        def copy(k, block, to, src=None):
            return pltpu.make_async_remote_copy(
                src_ref=slot(*block) if src is None else src, dst_ref=slot(*block),
                send_sem=send_sems.at[k], recv_sem=recv_sems.at[k], device_id=to, device_id_type=MESH)

        mine = pltpu.make_async_copy(v_ref, slot(*me), local_sem)
        mine.start()
        first = [copy(0, me, sibling, src=v_ref)]
        first += [copy(1 + j, me, (*chip, c), src=v_ref) for j, chip in enumerate(chips)]
        for cp in first:
            cp.start()
        passed = [copy(4 + j, (*chip, c), sibling) for j, chip in enumerate(chips)]
        for j, chip in enumerate(chips):
            copy(1 + j, (*chip, c), me).wait_recv()
            passed[j].start()
        copy(0, sibling, me).wait_recv()
        for j, chip in enumerate(chips):
            copy(4 + j, (*chip, 1 - c), me).wait_recv()
        for cp in first + passed:
            cp.wait_send()
        mine.wait()

    return pl.pallas_call(
        body, name=name, in_specs=[ANY], out_specs=ANY,
        out_shape=SDS((N_DEV,) + v.shape, v.dtype),
        scratch_shapes=[pltpu.SemaphoreType.DMA((7,)), pltpu.SemaphoreType.DMA((7,)), pltpu.SemaphoreType.DMA],
    )(v)


def _all_to_all(v, name):
    def body(v_ref, out_ref, send_sems, recv_sems, local_sem):
        x, y, c = lax.axis_index("x"), lax.axis_index("y"), lax.axis_index("c")
        mine_idx = 4 * x + 2 * y + c
        mine = pltpu.make_async_copy(v_ref.at[mine_idx], out_ref.at[mine_idx], local_sem)
        mine.start()
        sends, recvs = [], []
        for k in range(1, N_DEV):
            px = 1 - x if k & 4 else x
            py = 1 - y if k & 2 else y
            pc = 1 - c if k & 1 else c
            peer_idx = 4 * px + 2 * py + pc
            sems = dict(send_sem=send_sems.at[k - 1], recv_sem=recv_sems.at[k - 1], device_id=(px, py, pc),
                        device_id_type=MESH)
            sends.append(pltpu.make_async_remote_copy(src_ref=v_ref.at[peer_idx], dst_ref=out_ref.at[mine_idx], **sems))
            recvs.append(pltpu.make_async_remote_copy(src_ref=v_ref.at[peer_idx], dst_ref=out_ref.at[peer_idx], **sems))
        for cp in sends:
            cp.start()
        for cp in recvs:
            cp.wait_recv()
        for cp in sends:
            cp.wait_send()
        mine.wait()

    return pl.pallas_call(
        body, name=name, in_specs=[ANY], out_specs=ANY,
        out_shape=SDS(v.shape, v.dtype),
        scratch_shapes=[pltpu.SemaphoreType.DMA((7,)), pltpu.SemaphoreType.DMA((7,)), pltpu.SemaphoreType.DMA],
    )(v)


HBM_SPEC = pl.BlockSpec(memory_space=pltpu.HBM)
SEM_SPEC = pl.BlockSpec(memory_space=pltpu.SEMAPHORE)
EFFECT = pltpu.SideEffectType.DATAFLOW_SIDE_EFFECTING


EXCHANGE_PEERS = {"gather": range(1, N_DEV), "scatter": range(1, N_DEV), "chip": (1, 2, 4, 6), "pass": (2, 4, 6)}


def _exchange_copies(srcs, lands, send_sems, recv_sems, mode, layer):
    x, y, c = lax.axis_index("x"), lax.axis_index("y"), lax.axis_index("c")
    me = 4 * x + 2 * y + c
    copies = []
    for a, (src, land) in enumerate(zip(srcs, lands)):
        for k in EXCHANGE_PEERS[mode]:
            px = 1 - x if k & 4 else x
            py = 1 - y if k & 2 else y
            pc = 1 - c if k & 1 else c
            peer = 4 * px + 2 * py + pc
            if mode == "scatter":
                s, d, to = src.at[peer], land.at[me, layer], (px, py, pc)
            elif mode == "pass":
                s, d, to = land.at[peer], land.at[peer], (x, y, 1 - c)
            else:
                s, d, to = src, land.at[me], (px, py, pc)
            n = 7 * a + k - 1
            copies.append(pltpu.make_async_remote_copy(
                src_ref=s, dst_ref=d, send_sem=send_sems.at[n], recv_sem=recv_sems.at[n], device_id=to,
                device_id_type=MESH))
    return copies


def _exchange_start(name, srcs, lands, mode, layer=0, after=None):
    n = len(srcs)

    def body(*refs):
        send_sems, recv_sems = refs[-2 * n - 3], refs[-2 * n - 2]
        for cp in _exchange_copies(refs[:n], refs[n:2 * n], send_sems, recv_sems, mode, layer):
            cp.start()
        refs[-1][...] = jnp.zeros_like(refs[-1])

    arrays = list(srcs) + list(lands)
    sems = pltpu.SemaphoreType.DMA((7 * n,))
    out = pl.pallas_call(
        body, name=name,
        out_shape=(sems, sems, *[pltpu.HBM(v.shape, v.dtype) for v in arrays], SDS((8, LANE), F32)),
        in_specs=[HBM_SPEC] * (2 * n) + ([ANY] if after is not None else []),
        out_specs=(SEM_SPEC, SEM_SPEC, *[HBM_SPEC] * (2 * n), pl.BlockSpec(memory_space=pltpu.VMEM)),
        input_output_aliases={i: 2 + i for i in range(2 * n)},
        compiler_params=pltpu.CompilerParams(has_side_effects=EFFECT),
    )(*[pltpu.with_memory_space_constraint(v, pltpu.HBM) for v in arrays], *([after] if after is not None else []))
    return dict(sems=out[:2], srcs=out[2:2 + n], lands=out[2 + n:2 + 2 * n], token=out[-1], mode=mode,
                layer=layer)


def _exchange_wait(name, st, after, also=()):
    n = len(st["srcs"])

    def body(*refs):
        send_sems, recv_sems = refs[2 * n], refs[2 * n + 1]
        for cp in _exchange_copies(refs[:n], refs[n:2 * n], send_sems, recv_sems, st["mode"], st["layer"]):
            cp.wait_send()
            cp.wait_recv()

    arrays = list(st["srcs"]) + list(st["lands"])
    out = pl.pallas_call(
        body, name=name,
        out_shape=tuple(pltpu.HBM(v.shape, v.dtype) for v in arrays),
        in_specs=[HBM_SPEC] * (2 * n) + [SEM_SPEC, SEM_SPEC] + [ANY] * (1 + len(also)),
        out_specs=tuple([HBM_SPEC] * (2 * n)),
        input_output_aliases={i: i for i in range(2 * n)},
        compiler_params=pltpu.CompilerParams(has_side_effects=EFFECT),
    )(*arrays, *st["sems"], after, *also)
    st["srcs"] = out[:n]
    return out[n:]


def _exchange_relay(name, st, after, also=()):
    n = len(st["srcs"])

    def body(*refs):
        for cp in _exchange_copies(refs[:n], refs[n:2 * n], refs[2 * n], refs[2 * n + 1], st["mode"], st["layer"]):
            cp.wait_send()
            cp.wait_recv()
        send_sems, recv_sems = refs[-2 * n - 3], refs[-2 * n - 2]
        for cp in _exchange_copies(refs[:n], refs[n:2 * n], send_sems, recv_sems, "pass", st["layer"]):
            cp.start()
        refs[-1][...] = jnp.zeros_like(refs[-1])

    arrays = list(st["srcs"]) + list(st["lands"])
    sems = pltpu.SemaphoreType.DMA((7 * n,))
    out = pl.pallas_call(
        body, name=name,
        out_shape=(sems, sems, *[pltpu.HBM(v.shape, v.dtype) for v in arrays], SDS((8, LANE), F32)),
        in_specs=[HBM_SPEC] * (2 * n) + [SEM_SPEC, SEM_SPEC] + [ANY] * (1 + len(also)),
        out_specs=(SEM_SPEC, SEM_SPEC, *[HBM_SPEC] * (2 * n), pl.BlockSpec(memory_space=pltpu.VMEM)),
        input_output_aliases={i: 2 + i for i in range(2 * n)},
        compiler_params=pltpu.CompilerParams(has_side_effects=EFFECT),
    )(*arrays, *st["sems"], after, *also)
    return dict(sems=out[:2], srcs=out[2:2 + n], lands=out[2 + n:2 + 2 * n], token=out[-1], mode="pass",
                layer=st["layer"])


def _exchange_wait_all(name, sts, lands, ids, after):
    counts = [len(st["srcs"]) for st in sts]
    ns, nl = sum(counts), len(lands)

    def body(*refs):
        at = 0
        for e, st in enumerate(sts):
            own_lands = [refs[ns + i] for i in ids[e]]
            send_sems, recv_sems = refs[ns + nl + 2 * e], refs[ns + nl + 2 * e + 1]
            for cp in _exchange_copies(refs[at:at + counts[e]], own_lands, send_sems, recv_sems, st["mode"],
                                       st["layer"]):
                cp.wait_send()
                cp.wait_recv()
            at += counts[e]

    arrays = [s for st in sts for s in st["srcs"]] + list(lands)
    out = pl.pallas_call(
        body, name=name,
        out_shape=tuple(pltpu.HBM(v.shape, v.dtype) for v in arrays),
        in_specs=[HBM_SPEC] * len(arrays) + [SEM_SPEC] * (2 * len(sts)) + [ANY],
        out_specs=tuple([HBM_SPEC] * len(arrays)),
        input_output_aliases={i: i for i in range(len(arrays))},
        compiler_params=pltpu.CompilerParams(has_side_effects=EFFECT),
    )(*arrays, *[s for st in sts for s in st["sems"]], after)
    at = 0
    for e, st in enumerate(sts):
        st["srcs"] = out[at:at + counts[e]]
        at += counts[e]
    return list(out[ns:])


_IN_PIECES = ([(1024, 3072)]
              + [r for t in range(4) for r in ((LANE * t, LANE * (t + 1)), (512 + LANE * t, 512 + LANE * (t + 1)))]
              + [(4096, 5632), (3072, 4096), (5632, 5648)])


def _permute_in(w):
    pad = jnp.zeros(w.shape[:-1] + (N_PAD - N_IN,), w.dtype)
    return jnp.concatenate([w[..., a:b] for a, b in _IN_PIECES] + [pad], axis=-1)


def _unpermute_in(g):
    ax = [g[..., OFF_LRU + 2 * LANE * t:OFF_LRU + 2 * LANE * t + LANE] for t in range(4)]
    ag = [g[..., OFF_LRU + 2 * LANE * t + LANE:OFF_LRU + 2 * LANE * (t + 1)] for t in range(4)]
    return jnp.concatenate(ax + ag + [g[..., 0:2048], g[..., OFF_Z:OFF_Z + SSD_W], g[..., OFF_XBC:OFF_XBC + SSD_CONV],
                                      g[..., OFF_Z + SSD_W:OFF_Z + SSD_W + SSD_HEADS]], axis=-1)


SHARD_COLS = N_IN // N_DEV


def _in_segments():
    segs, pos = [], 0
    for a, b in _IN_PIECES:
        for i in range(N_DEV):
            lo, hi = max(a, SHARD_COLS * i), min(b, SHARD_COLS * (i + 1))
            if lo < hi:
                segs.append((i, lo - SHARD_COLS * i, hi - lo, pos + lo - a))
        pos += b - a
    return segs


RELAYOUT_ROWS = 512


def _relayout_in(land, own):
    def body(land_ref, own_ref, out_ref):
        me = 4 * lax.axis_index("x") + 2 * lax.axis_index("y") + lax.axis_index("c")
        out_ref[:, N_IN:N_PAD] = jnp.zeros((RELAYOUT_ROWS, N_PAD - N_IN), BF16)
        for i, j, wd, p in _in_segments():
            out_ref[:, p:p + wd] = jnp.where(me == i, own_ref[:, j:j + wd], land_ref[i, :, j:j + wd])

    return pl.pallas_call(
        body, name="relayout_in", grid=(D_MODEL // RELAYOUT_ROWS,),
        in_specs=[pl.BlockSpec((N_DEV, RELAYOUT_ROWS, SHARD_COLS), lambda r: (0, r, 0)),
                  pl.BlockSpec((RELAYOUT_ROWS, SHARD_COLS), lambda r: (r, 0))],
        out_specs=pl.BlockSpec((RELAYOUT_ROWS, N_PAD), lambda r: (r, 0)),
        out_shape=SDS((D_MODEL, N_PAD), BF16),
        compiler_params=_cp(("parallel",)),
    )(land, own)


def _relayout_grad(g):
    def body(g_ref, out_ref):
        for i, j, wd, p in _in_segments():
            out_ref[i, :, j:j + wd] = g_ref[:, p:p + wd].astype(BF16)

    return pl.pallas_call(
        body, name="relayout_grad", grid=(D_MODEL // RELAYOUT_ROWS,),
        in_specs=[pl.BlockSpec((RELAYOUT_ROWS, N_PAD), lambda r: (r, 0))],
        out_specs=pl.BlockSpec((N_DEV, RELAYOUT_ROWS, SHARD_COLS), lambda r: (0, r, 0)),
        out_shape=SDS((N_DEV, D_MODEL, SHARD_COLS), BF16),
        compiler_params=_cp(("parallel",)),
    )(g)


def _block_diag(w):
    w4 = w.reshape(DEPTH, 4, 2, 64, 64)
    z = jnp.zeros((DEPTH, 4, 64, 64), w.dtype)
    top = jnp.concatenate([w4[:, :, 0], z], axis=-1)
    bot = jnp.concatenate([z, w4[:, :, 1]], axis=-1)
    return jnp.concatenate([top, bot], axis=2).astype(BF16)


def _diag_blocks(g):
    return jnp.stack([g[:, :, :64, :64], g[:, :, 64:, 64:]], axis=2).reshape(DEPTH, 8, 64, 64)


def _pad_lanes(v):
    return jnp.pad(v, ((0, 0), (0, LANE - v.shape[1])))


def _lower_bounds(logits):
    p = jax.nn.softmax(logits, axis=0)
    return p, jnp.cumsum(p, axis=0) - p[0]


def _lower_bounds_bwd(p, dlb):
    dp = jnp.cumsum(dlb[::-1], axis=0)[::-1]
    dp = dp.at[0].add(-jnp.sum(dlb, axis=0))
    return p * (dp - jnp.sum(dp * p, axis=0, keepdims=True))


SMALL = ["norm_w", "b_ada", "lru_conv_b", "lru_wa", "lru_ba", "lru_wx", "lru_bx", "lru_lambda", "hg_lb_logits",
         "hg_norm_w", "ssd_conv_b", "ssd_dt_bias", "ssd_a_log", "ssd_d", "ssd_norm_w", "final_norm_w"]
WEIGHTS = ["norm_w", "w_ada", "b_ada", "w_in", "lru_conv_w", "lru_conv_b", "lru_wa", "lru_ba", "lru_wx", "lru_bx",
           "lru_lambda", "hg_lb_logits", "hg_norm_w", "ssd_conv_w", "ssd_conv_b", "ssd_dt_bias", "ssd_a_log", "ssd_d",
           "ssd_norm_w", "w_out", "final_norm_w"]
INPUTS = ["x", "c"] + WEIGHTS + ["loss_target"] + ["m_" + n for n in WEIGHTS] + ["v_" + n for n in WEIGHTS]
SMALL_ROW = 1024


def _small_rows(like):
    out, off = {}, 0
    for n in SMALL:
        rows = -(-int(np.prod(like[n].shape)) // (8 * SMALL_ROW)) * 8
        out[n] = (off, rows)
        off += rows
    return out, off


def _flatten_small(d, prefix="", last=0.0):
    table, _ = _small_rows({n: d[prefix + n] for n in SMALL})
    pieces = []
    for n in SMALL:
        flat = d[prefix + n].reshape(-1)
        pieces.append(jnp.pad(flat, (0, table[n][1] * SMALL_ROW - flat.shape[0])).reshape(-1, SMALL_ROW))
    return jnp.concatenate(pieces + [jnp.full((8, SMALL_ROW), last, F32)], axis=0)


def _split_small(packed, like):
    table, _ = _small_rows(like)
    out = {}
    for n in SMALL:
        off, rows = table[n]
        size = int(np.prod(like[n].shape))
        out[n] = packed[off:off + rows].reshape(-1)[:size].reshape(like[n].shape)
    return out


def _local_step(x, mod, target, w, fetch, emit):
    S = x.shape[0]
    mall = _bfc(_hg_consts())
    mall_t = _bfc(_hg_consts().T)
    consts = _ssd_consts()
    p_lb, lbs = _lower_bounds(w["hg_lb_logits"])
    no_tok = jnp.zeros((8, LANE), F32)
    wa, wx = _block_diag(w["lru_wa"]), _block_diag(w["lru_wx"])
    ba, bx = w["lru_ba"].reshape(DEPTH, 1, LRU_W), w["lru_bx"].reshape(DEPTH, 1, LRU_W)
    lru_cb, lam, ssd_cb = w["lru_conv_b"][:, None], w["lru_lambda"][:, None], w["ssd_conv_b"][:, None]
    bias, alog = _pad_lanes(w["ssd_dt_bias"]), _pad_lanes(w["ssd_a_log"])
    dskip = jnp.repeat(w["ssd_d"], SSD_P, axis=1)
    saved = []
    for l in range(DEPTH):
        w_in_l, w_out_l, token = fetch(l, x)
        shift, scale, gate = (_Row(mod, l, D_MODEL, k) for k in range(3))
        nw = _Row(w["norm_w"], l)
        u, h = _inproj_fwd(x, nw, scale, shift, w_in_l, no_tok if token is None else token)
        ycat = lax.empty((S, D_INNER), BF16)
        lru_args = (l, u, w["lru_conv_w"], lru_cb, wa, ba, wx, bx, lam)
        ycat, h_lru = _lru_fwd(*lru_args, ycat)
        hg_args = (u, _Row(lbs, l), _Row(w["hg_norm_w"], l), mall)
        ycat, o_b, hg_st = _hg_fwd(*hg_args, ycat)
        xbc = _ssdconv_fwd(l, u, w["ssd_conv_w"], ssd_cb)
        ssd_args = (u, xbc, _Row(bias, l), _Row(alog, l), _Row(dskip, l), _Row(w["ssd_norm_w"], l), consts)
        ycat, y_ssd, ssd_st = _ssd_fwd(*ssd_args, ycat)
        token = fetch(l, y_ssd, late=True)
        x_new, y = _outproj_fwd(ycat, w_out_l, x, gate, no_tok if token is None else token)
        saved.append((x, u, h, ycat, nw, scale, gate, w_in_l, w_out_l, lru_args, h_lru, hg_args, o_b, hg_st, ssd_args,
                      y_ssd, ssd_st, y))
        x = x_new
    dx, red = _loss_head(x, w["final_norm_w"][None, :], target)
    loss = red[1, 0]
    reds = {k: [None] * DEPTH for k in ("in", "gate", "lru", "wa", "wx", "hg", "conv", "ssd")}
    for l in reversed(range(DEPTH)):
        (x, u, h, ycat, nw, scale, gate, w_in_l, w_out_l, lru_args, h_lru, hg_args, o_b, hg_st, ssd_args, y_ssd, ssd_st,
         y) = saved[l]
        dycat, g_out, reds["gate"][l] = _outproj_bwd(dx, y, gate, ycat, w_out_l)
        token = emit(l, "w_out", g_out)
        du = lax.empty((S, N_PAD), BF16)
        du, dxbc, reds["ssd"][l] = _ssd_bwd(*ssd_args, y_ssd, ssd_st, dycat, du, no_tok if token is None else token)
        du, reds["conv"][l] = _ssdconv_bwd(l, u, w["ssd_conv_w"], ssd_cb, dxbc, du)
        du, reds["hg"][l] = _hg_bwd(*hg_args, mall_t, o_b, hg_st, dycat, du)
        du, reds["lru"][l], reds["wa"][l], reds["wx"][l] = _lru_bwd(*lru_args, h_lru, dycat, du)
        token = emit(l, "w_in", functools.partial(_inproj_bwd_w, h, du))
        dx, reds["in"][l] = _inproj_bwd_x(du, w_in_l, x, nw, scale, dx, no_tok if token is None else token)
    r = {k: jnp.stack(v) for k, v in reds.items()}
    g = {n: None for n in WEIGHTS}
    g["final_norm_w"] = red[0]
    g["norm_w"] = r["in"][:, 2]
    dmod = jnp.concatenate([r["in"][:, 0], r["in"][:, 1], r["gate"][:, 0]], axis=1)
    g["lru_conv_w"], g["lru_conv_b"] = r["lru"][:, 0:4], r["lru"][:, 4]
    g["lru_ba"], g["lru_bx"] = r["lru"][:, 5].reshape(DEPTH, 8, 64), r["lru"][:, 6].reshape(DEPTH, 8, 64)
    g["lru_lambda"] = r["lru"][:, 7]
    g["lru_wa"], g["lru_wx"] = _diag_blocks(r["wa"]), _diag_blocks(r["wx"])
    g["hg_norm_w"] = r["hg"][:, 0]
    g["hg_lb_logits"] = _lower_bounds_bwd(p_lb, r["hg"][:, 1])
    g["ssd_conv_w"], g["ssd_conv_b"] = r["conv"][:, 0:4], r["conv"][:, 4]
    g["ssd_norm_w"] = r["ssd"][:, 0]
    g["ssd_d"] = r["ssd"][:, 1].reshape(DEPTH, SSD_HEADS, SSD_P).sum(-1)
    g["ssd_dt_bias"] = r["ssd"][:, 2, :SSD_HEADS]
    g["ssd_a_log"] = -r["ssd"][:, 3, :SSD_HEADS] * jnp.exp(w["ssd_a_log"])
    return loss, dx, dmod, g


def kernel(x, c, norm_w, w_ada, b_ada, w_in, lru_conv_w, lru_conv_b, lru_wa, lru_ba, lru_wx, lru_bx, lru_lambda, hg_lb_logits, hg_norm_w, ssd_conv_w, ssd_conv_b, ssd_dt_bias, ssd_a_log, ssd_d, ssd_norm_w, w_out, final_norm_w, loss_target, m_norm_w, m_w_ada, m_b_ada, m_w_in, m_lru_conv_w, m_lru_conv_b, m_lru_wa, m_lru_ba, m_lru_wx, m_lru_bx, m_lru_lambda, m_hg_lb_logits, m_hg_norm_w, m_ssd_conv_w, m_ssd_conv_b, m_ssd_dt_bias, m_ssd_a_log, m_ssd_d, m_ssd_norm_w, m_w_out, m_final_norm_w, v_norm_w, v_w_ada, v_b_ada, v_w_in, v_lru_conv_w, v_lru_conv_b, v_lru_wa, v_lru_ba, v_lru_wx, v_lru_bx, v_lru_lambda, v_hg_lb_logits, v_hg_norm_w, v_ssd_conv_w, v_ssd_conv_b, v_ssd_dt_bias, v_ssd_a_log, v_ssd_d, v_ssd_norm_w, v_w_out, v_final_norm_w):
    return _step(x, c, norm_w, w_ada, b_ada, w_in, lru_conv_w, lru_conv_b, lru_wa, lru_ba, lru_wx, lru_bx, lru_lambda, hg_lb_logits, hg_norm_w, ssd_conv_w, ssd_conv_b, ssd_dt_bias, ssd_a_log, ssd_d, ssd_norm_w, w_out, final_norm_w, loss_target, m_norm_w, m_w_ada, m_b_ada, m_w_in, m_lru_conv_w, m_lru_conv_b, m_lru_wa, m_lru_ba, m_lru_wx, m_lru_bx, m_lru_lambda, m_hg_lb_logits, m_hg_norm_w, m_ssd_conv_w, m_ssd_conv_b, m_ssd_dt_bias, m_ssd_a_log, m_ssd_d, m_ssd_norm_w, m_w_out, m_final_norm_w, v_norm_w, v_w_ada, v_b_ada, v_w_in, v_lru_conv_w, v_lru_conv_b, v_lru_wa, v_lru_ba, v_lru_wx, v_lru_bx, v_lru_lambda, v_hg_lb_logits, v_hg_norm_w, v_ssd_conv_w, v_ssd_conv_b, v_ssd_dt_bias, v_ssd_a_log, v_ssd_d, v_ssd_norm_w, v_w_out, v_final_norm_w)


def _step(*args):
    a = dict(zip(INPUTS, args, strict=True))
    me = 4 * lax.axis_index("x") + 2 * lax.axis_index("y") + lax.axis_index("c")
    x, target = a["x"][0], a["loss_target"][0]

    c_all = _all_gather(a["c"], "gather_c")[:, 0, :]
    b_cols = lax.dynamic_slice_in_dim(a["b_ada"], me * ADA_COLS, ADA_COLS, axis=1)[:, None, :]
    mod_parts = _all_gather(_ada_fwd(c_all, a["w_ada"], b_cols), "gather_mod")
    mod = lax.dynamic_index_in_dim(mod_parts, me, axis=2, keepdims=False)
    mod = mod.transpose(1, 0, 2).reshape(DEPTH, 3 * D_MODEL)

    w = {n: a[n] for n in SMALL}

    w_in_b = [a["w_in"][l].astype(BF16) for l in range(DEPTH)]
    w_out_b = a["w_out"].astype(BF16)
    conv_own = jnp.concatenate([a["lru_conv_w"], a["ssd_conv_w"]], axis=-1)
    cols, rows_out = N_IN // N_DEV, D_INNER // N_DEV

    def gather_start(l, after):
        srcs = [w_in_b[l], w_out_b[l]] + ([conv_own] if l == 0 else [])
        lands = [lax.empty((N_DEV,) + s.shape, s.dtype) for s in srcs]
        return _exchange_start(f"gather_start_{l}", srcs, lands, "chip", after=after)

    def gather_pass(name, st, after, also=()):
        return _exchange_wait(name + "_passed", _exchange_relay(name + "_pass", st, after, also), after)

    gathers = {0: gather_start(0, mod)}
    passing = {}

    def fetch(l, x_l, late=False):
        if late:
            if l + 1 == DEPTH:
                return None
            passing[l + 1] = _exchange_relay(f"gather_{l + 1}_pass", gathers[l + 1], x_l)
            return passing[l + 1]["token"]
        if l == 0:
            landed = gather_pass("gather_0", gathers[0], x_l, also=(a["w_in"], a["m_w_in"], a["v_w_in"]))
        else:
            landed = _exchange_wait(f"gather_{l}_passed", passing[l], x_l)
        land_out = lax.dynamic_update_index_in_dim(landed[1], w_out_b[l], me, 0)
        if l == 0:
            conv = lax.dynamic_update_index_in_dim(landed[2], conv_own, me, 0).transpose(1, 2, 0, 3)
            w["lru_conv_w"] = conv[..., :64].reshape(DEPTH, 4, LRU_W)
            w["ssd_conv_w"] = conv[..., 64:].reshape(DEPTH, 4, SSD_CONV)
        token = None
        if l + 1 < DEPTH:
            gathers[l + 1] = gather_start(l + 1, land_out)
            token = gathers[l + 1]["token"]
        return _relayout_in(landed[0], w_in_b[l]), land_out.reshape(D_INNER, D_MODEL), token

    PROJ = ("w_in", "w_out")
    scatters = {}
    lands = [lax.empty((N_DEV, DEPTH, D_MODEL, cols), BF16), lax.empty((N_DEV, DEPTH, rows_out, D_MODEL), BF16)]
    own = [None] * DEPTH

    deferred, g_out = {}, {}

    def emit(l, name, grad, after=None):
        if name == "w_out" and l > 0:
            g_out[l] = grad
            return None
        if name == "w_in" and l == 0 and after is None:
            deferred["w_in"] = grad
            return None
        if name == "w_in":
            grad = grad(jnp.zeros((8, LANE), F32) if after is None else after)
        if l == 0:
            k = PROJ.index(name)
            src = _relayout_grad(grad) if name == "w_in" else grad.reshape(N_DEV, rows_out, D_MODEL)
            st = _exchange_start(f"scatter_start_0_{name}", [src], [lands[k]], "scatter", layer=0, after=after)
            scatters[name] = st
            lands[k] = st["lands"][0]
            return st["token"]
        srcs = [_relayout_grad(grad), g_out[l].reshape(N_DEV, rows_out, D_MODEL)]
        st = _exchange_start(f"scatter_start_{l}", srcs, lands, "scatter", layer=l, after=after)
        scatters[l] = st
        lands[:] = st["lands"]
        return st["token"]

    loss_own, dx, dmod, g = _local_step(x, mod, target, w, fetch, emit)

    def sharded(name, parts, own=None, **kw):
        return _adamw(parts, a[name], a["m_" + name], a["v_" + name], "adamw_" + name + kw.pop("tag", ""), own=own, **kw)

    g["b_ada"] = dmod
    small_own = _flatten_small(g, last=loss_own)
    small_st = _exchange_start("gather_small", [small_own], [lax.empty((N_DEV,) + small_own.shape, F32)], "chip",
                               after=dx)
    big = {}
    after = emit(0, "w_in", deferred["w_in"], after=small_st["token"]) + dx[0:8, 0:LANE]

    def own_slices(st):
        return [lax.dynamic_index_in_dim(s, me, 0, keepdims=False) for s in st["srcs"]]

    upper_sts = [scatters[l] for l in reversed(range(1, DEPTH))] + [scatters["w_out"]]
    lands[:] = _exchange_wait_all("scatter_wait_upper", upper_sts, lands, [(0, 1)] * (DEPTH - 1) + [(1,)], after)
    for l in range(1, DEPTH):
        own[l] = own_slices(scatters[l])
    own[0] = [None, own_slices(scatters["w_out"])[0]]
    big["w_out"] = sharded("w_out", lands[1], jnp.stack([own[l][1] for l in range(DEPTH)]))
    upper = sharded("w_in", lands[0], jnp.stack([own[l][0] for l in range(1, DEPTH)]), layers=(1, DEPTH), tag="_upper")
    after = upper[1][0, 0:8, 0:LANE] + big["w_out"][1][0, 0:8, 0:LANE]
    small = gather_pass("gather_small", small_st, after)[0]
    outs = _adamw(small[:, None], *[_flatten_small(a, p)[None] for p in ("", "m_", "v_")], "adamw_small",
                  own=small_own[None])
    res = [_split_small(o[0], a) for o in outs]
    losses = lax.dynamic_update_index_in_dim(small[:, -1, 0], loss_own, me, 0)
    loss = jnp.sum(losses)

    off = _small_rows(a)[0]["b_ada"][0]
    dmod_all = lax.dynamic_update_index_in_dim(small[:, off:off + DEPTH * 3 * D_MODEL // SMALL_ROW],
                                               dmod.reshape(-1, SMALL_ROW), me, 0)
    dmod_all = dmod_all.reshape(N_DEV, DEPTH, 3 * D_MODEL).transpose(1, 0, 2)
    dmod_cols = lax.dynamic_slice_in_dim(dmod_all, me * ADA_COLS, ADA_COLS, axis=2)
    dmod_pad = jnp.pad(dmod_cols, ((0, 0), (0, LANE - N_DEV), (0, 0)))
    ct_pad = jnp.pad(c_all.T, ((0, 0), (0, LANE - N_DEV)))
    big["w_ada"] = sharded("w_ada", _ada_bwd(ct_pad, dmod_pad)[None])
    g_conv = jnp.concatenate([g["lru_conv_w"].reshape(DEPTH, 4, N_DEV, 64), g["ssd_conv_w"].reshape(DEPTH, 4, N_DEV, 192)],
                             axis=-1).transpose(2, 0, 1, 3)
    conv_parts = _all_to_all(g_conv, "scatter_conv")
    big["lru_conv_w"] = sharded("lru_conv_w", conv_parts[..., :64])
    big["ssd_conv_w"] = sharded("ssd_conv_w", conv_parts[..., 64:])

    after = outs[1] + big["w_ada"][1][0, 0:1, 0:1]
    scatters["w_in"]["lands"] = [lands[0]]
    lands[0] = _exchange_wait("scatter_wait_0_w_in", scatters["w_in"], after)[0]
    big["w_in"] = sharded("w_in", lands[0], own_slices(scatters["w_in"])[0][None], layers=(0, 1), prev=upper)

    out = [loss, dx[None]]
    for k in range(4):
        out += [big[n][k] if n in big else res[k][n] for n in WEIGHTS]
    return tuple(out)
```

```python
import functools

import numpy as np
import jax
import jax.numpy as jnp
from jax import lax
from jax.experimental import pallas as pl
from jax.experimental.pallas import tpu as pltpu

F32 = jnp.float32
BF16 = jnp.bfloat16
SDS = jax.ShapeDtypeStruct

N_DEV = 8
DEPTH = 4
D_MODEL = 1024
D_INNER = 2048
EPS = 1e-6
LRU_W = 512
LRU_C = 8.0
HG_W = 512
HG_CHUNK = 64
HG_HEADS = 4
SSD_W = 1024
SSD_HEADS = 16
SSD_P = 64
SSD_N = 128
SSD_CHUNK = 128
SSD_CONV = 1536
N_IN = 5648
N_PAD = 5760
OFF_HG = 0
OFF_LRU = 2048
OFF_XBC = 3072
OFF_Z = 4608
LANE = 128
VMEM_LIMIT = 56 * 1024 * 1024
NEG = -1e30

ADAM_LR = 0.001
ADAM_B1 = 0.9
ADAM_B2 = 0.999
ADAM_EPS = 1e-08
ADAM_WD = 0.01
ADAM_STEP = 10


def _cp(sem=None):
    return pltpu.CompilerParams(dimension_semantics=sem, vmem_limit_bytes=VMEM_LIMIT)


def _dg(a, b, ca, cb):
    return lax.dot_general(a, b, (((ca,), (cb,)), ((), ())), preferred_element_type=F32)


def _mm(a, b):
    return _dg(a, b, 1, 0)


def _mm_nt(a, b):
    return _dg(a, b, 1, 1)


def _mm_tn(a, b):
    return _dg(a, b, 0, 0)


def _bf(x):
    return x.astype(BF16)


def _f(x):
    return x.astype(F32)


def _split3(x):
    hi = x.astype(BF16)
    r = x - hi.astype(F32)
    mid = r.astype(BF16)
    lo = (r - mid.astype(F32)).astype(BF16)
    return hi, mid, lo


def _sel_r(x, m):
    hi, mid, lo = _split3(x)
    return _mm(hi, m) + _mm(mid, m) + _mm(lo, m)


def _sel_l(m, x):
    hi, mid, lo = _split3(x)
    return _mm(m, hi) + _mm(m, mid) + _mm(m, lo)


def _sel_l2(m, x):
    hi = x.astype(BF16)
    lo = (x - hi.astype(F32)).astype(BF16)
    return _mm(m, hi) + _mm(m, lo)


def _sel_tn(x, m):
    hi, mid, lo = _split3(x)
    return _mm_tn(hi, m) + _mm_tn(mid, m) + _mm_tn(lo, m)


def _sigmoid(x):
    return 1.0 / (1.0 + jnp.exp(-x))


def _silu(x):
    return x * _sigmoid(x)


def _dsilu(x):
    s = _sigmoid(x)
    return s * (1.0 + x * (1.0 - s))


def _softplus(x):
    return jnp.maximum(x, 0.0) + jnp.log(1.0 + jnp.exp(-jnp.abs(x)))


def _expm1(z):
    series = z * (1.0 + z * (1.0 / 2) * (1.0 + z * (1.0 / 3) * (1.0 + z * (1.0 / 4) * (
        1.0 + z * (1.0 / 5) * (1.0 + z * (1.0 / 6) * (1.0 + z * (1.0 / 7)))))))
    return jnp.where(jnp.abs(z) < 0.3, series, jnp.exp(z) - 1.0)


def _iota(shape, dim):
    return lax.broadcasted_iota(jnp.int32, shape, dim)


def _last_row(x, rows):
    return jnp.sum(jnp.where(rows == x.shape[0] - 1, x, 0.0), axis=0, keepdims=True)


def _shift_down(x, d, rows, fill=0.0):
    return jnp.where(rows >= d, pltpu.roll(x, d, 0), fill)


def _shift_up(x, d, rows, fill=0.0):
    n = x.shape[0]
    return jnp.where(rows < n - d, pltpu.roll(x, n - d, 0), fill)


def _conv_fwd(x, cw_ref, cb_ref, rows):
    out = cb_ref[...] + cw_ref[pl.ds(3, 1), :] * x
    for k in range(3):
        out = out + cw_ref[pl.ds(k, 1), :] * _shift_down(x, 3 - k, rows)
    return out


def _conv_bwd(x, dco, cw_ref, rows):
    dx = cw_ref[pl.ds(3, 1), :] * dco
    dws = []
    for k in range(3):
        dx = dx + cw_ref[pl.ds(k, 1), :] * _shift_up(dco, 3 - k, rows)
        dws.append(jnp.sum(dco * _shift_down(x, 3 - k, rows), axis=0, keepdims=True))
    dws.append(jnp.sum(dco * x, axis=0, keepdims=True))
    return dx, dws, jnp.sum(dco, axis=0, keepdims=True)


def _vec(n):
    return pl.BlockSpec((1, n), lambda *_: (0, 0))


class _Row:
    def __init__(self, arr, l, n=None, c=0):
        self.arr, self.l, self.n, self.c = arr[:, None, :], l, n or arr.shape[1], c


def _spec(v):
    if isinstance(v, _Row):
        return pl.BlockSpec((None, 1, v.n), lambda *_: (v.l, 0, v.c))
    return _vec(v.shape[1])


def _arr(v):
    return v.arr if isinstance(v, _Row) else v


def _full(shape):
    nd = len(shape)
    return pl.BlockSpec(shape, lambda *_: (0,) * nd)


def _inproj_fwd(x, nw, scale, shift, w, tok):
    S = x.shape[0]
    tm = min(256, S)

    def body(x_ref, nw_ref, sc_ref, sh_ref, w_ref, tok_ref, u_ref, h_ref):
        del tok_ref
        xv = x_ref[...]
        inv = lax.rsqrt(jnp.mean(xv * xv, axis=-1, keepdims=True) + EPS)
        h = ((xv * inv) * nw_ref[...] * (1.0 + sc_ref[...]) + sh_ref[...]).astype(BF16)
        h_ref[...] = h
        u_ref[...] = _mm(h, w_ref[...])

    return pl.pallas_call(
        body, name="inproj_fwd", grid=(S // tm,),
        in_specs=[pl.BlockSpec((tm, D_MODEL), lambda i: (i, 0)), _spec(nw), _spec(scale), _spec(shift),
                  _full((D_MODEL, N_PAD)), pl.BlockSpec(memory_space=pl.ANY)],
        out_specs=[pl.BlockSpec((tm, N_PAD), lambda i: (i, 0)), pl.BlockSpec((tm, D_MODEL), lambda i: (i, 0))],
        out_shape=[SDS((S, N_PAD), F32), SDS((S, D_MODEL), BF16)],
        compiler_params=_cp(("parallel",)),
    )(x, _arr(nw), _arr(scale), _arr(shift), w, tok)


def _inproj_bwd_x(du, w, x, nw, scale, dxn, tok):
    S = x.shape[0]
    tm = min(256, S)

    def body(du_ref, w_ref, x_ref, nw_ref, sc_ref, dxn_ref, tok_ref, dx_ref, red_ref):
        del tok_ref

        @pl.when(pl.program_id(0) == 0)
        def _():
            red_ref[...] = jnp.zeros_like(red_ref)

        dh = _mm_nt(du_ref[...], w_ref[...])
        xv = x_ref[...]
        inv = lax.rsqrt(jnp.mean(xv * xv, axis=-1, keepdims=True) + EPS)
        xhat = xv * inv
        nwv = nw_ref[...]
        g1 = 1.0 + sc_ref[...]
        dxhat = dh * nwv * g1
        dx = inv * (dxhat - xhat * jnp.mean(dxhat * xhat, axis=-1, keepdims=True))
        dx_ref[...] = dxn_ref[...] + dx
        red_ref[0:1, :] += jnp.sum(dh, axis=0, keepdims=True)
        red_ref[1:2, :] += jnp.sum(dh * xhat * nwv, axis=0, keepdims=True)
        red_ref[2:3, :] += jnp.sum(dh * xhat * g1, axis=0, keepdims=True)

    row = pl.BlockSpec((tm, D_MODEL), lambda i: (i, 0))
    return pl.pallas_call(
        body, name="inproj_bwd_x", grid=(S // tm,),
        in_specs=[pl.BlockSpec((tm, N_PAD), lambda i: (i, 0)), _full((D_MODEL, N_PAD)), row, _spec(nw),
                  _spec(scale), row, pl.BlockSpec(memory_space=pl.ANY)],
        out_specs=[row, _full((8, D_MODEL))],
        out_shape=[SDS((S, D_MODEL), F32), SDS((8, D_MODEL), F32)],
        compiler_params=_cp(("arbitrary",)),
    )(du, w, x, _arr(nw), _arr(scale), dxn, tok)


def _inproj_bwd_w(h, du, tok):
    S = h.shape[0]
    tn = 640

    def body(h_ref, du_ref, tok_ref, gw_ref):
        del tok_ref
        gw_ref[...] = _mm_tn(h_ref[...], _bf(du_ref[...]))

    return pl.pallas_call(
        body, name="inproj_bwd_w", grid=(N_PAD // tn,),
        in_specs=[_full((S, D_MODEL)), pl.BlockSpec((S, tn), lambda j: (0, j)), pl.BlockSpec(memory_space=pl.ANY)],
        out_specs=pl.BlockSpec((D_MODEL, tn), lambda j: (0, j)),
        out_shape=SDS((D_MODEL, N_PAD), F32),
        compiler_params=_cp(("parallel",)),
    )(h, du, tok)


def _scan_block(a, b, rows):
    d = 1
    while d < a.shape[0]:
        a_s = _shift_down(a, d, rows, 1.0)
        b_s = _shift_down(b, d, rows, 0.0)
        b = a * b_s + b
        a = a * a_s
        d *= 2
    return a, b


def _rscan_block(c, g, rows):
    d = 1
    while d < c.shape[0]:
        c_s = _shift_up(c, d, rows, 1.0)
        g_s = _shift_up(g, d, rows, 0.0)
        g = g + c * g_s
        c = c * c_s
        d *= 2
    return c, g


LRU_BLOCK = 128


def _lru_gates(xa, wa_ref, ba_ref, wx_ref, bx_ref, lam_ref):
    sp = _softplus(-lam_ref[...])
    xb = _bf(xa)
    r = _sigmoid(_mm(xb, wa_ref[...]) + ba_ref[...])
    ig = _sigmoid(_mm(xb, wx_ref[...]) + bx_ref[...])
    la = -LRU_C * r * sp
    a = jnp.exp(la)
    mult = jnp.sqrt(-_expm1(2.0 * la))
    return sp, r, ig, la, a, mult


def _lru_specs(S, l):
    t128 = pl.BlockSpec((None, 1, LANE), lambda t: (l, 0, t))
    gate = pl.BlockSpec((None, None, LANE, LANE), lambda t: (l, t, 0, 0))
    return [pl.BlockSpec((S, 2 * LANE), lambda t: (0, OFF_LRU // (2 * LANE) + t)),
            pl.BlockSpec((None, 4, LANE), lambda t: (l, 0, t)), t128, gate, t128, gate, t128, t128]


def _lru_fwd(l, u, cw, cb, wa, ba, wx, bx, lam, ycat):
    S = u.shape[0]
    tb = min(LRU_BLOCK, S)

    def body(u_ref, cw_ref, cb_ref, wa_ref, ba_ref, wx_ref, bx_ref, lam_ref, ycat_in, ycat_ref, h_ref, a_scr, b_scr):
        del ycat_in
        rows = _iota((S, LANE), 0)
        xa = _conv_fwd(_f(u_ref[:, 0:LANE]), cw_ref, cb_ref, rows)
        _, _, ig, _, a, mult = _lru_gates(xa, wa_ref, ba_ref, wx_ref, bx_ref, lam_ref)
        a_scr[...] = a
        b_scr[...] = mult * (ig * xa)
        rows_b = _iota((tb, LANE), 0)

        def blk(j, hprev):
            sl = pl.ds(pl.multiple_of(j * tb, tb), tb)
            acum, hloc = _scan_block(a_scr[sl, :], b_scr[sl, :], rows_b)
            hf = hloc + acum * hprev
            h_ref[sl, :] = hf
            return _last_row(hf, rows_b)

        lax.fori_loop(0, S // tb, blk, jnp.zeros((1, LANE), F32))
        ycat_ref[...] = _bf(h_ref[...] * _silu(_f(u_ref[:, LANE:2 * LANE])))

    col = pl.BlockSpec((S, LANE), lambda t: (0, t))
    return pl.pallas_call(
        body, name="lru_fwd", grid=(LRU_W // LANE,),
        in_specs=_lru_specs(S, l) + [pl.BlockSpec(memory_space=pl.ANY)],
        out_specs=[col, col],
        out_shape=[SDS((S, D_INNER), BF16), SDS((S,LRU_W), F32)],
        scratch_shapes=[pltpu.VMEM((S, LANE), F32), pltpu.VMEM((S, LANE), F32)],
        input_output_aliases={8: 0},
        compiler_params=_cp(("parallel",)),
    )(u, cw, cb, wa, ba, wx, bx, lam, ycat)


def _lru_bwd(l, u, cw, cb, wa, ba, wx, bx, lam, h_lru, dycat, du):
    S = u.shape[0]
    tb = min(LRU_BLOCK, S)

    def body(u_ref, cw_ref, cb_ref, wa_ref, ba_ref, wx_ref, bx_ref, lam_ref, h_ref, dy_ref, du_in,
             du_ref, red_ref, gwa_ref, gwx_ref, c_scr, g_scr, l_scr):
        del du_in
        rows = _iota((S, LANE), 0)
        ax = _f(u_ref[:, 0:LANE])
        ag = _f(u_ref[:, LANE:2 * LANE])
        xa = _conv_fwd(ax, cw_ref, cb_ref, rows)
        sp, r, ig, la, a, mult = _lru_gates(xa, wa_ref, ba_ref, wx_ref, bx_ref, lam_ref)
        h = h_ref[...]
        dy = _f(dy_ref[...])
        du_ref[:, LANE:2 * LANE] = _bf(dy * h * _dsilu(ag))
        c_scr[...] = _shift_up(a, 1, rows, 0.0)
        g_scr[...] = dy * _silu(ag)
        rows_b = _iota((tb, LANE), 0)
        nb = S // tb

        def blk(jj, lnext):
            j = nb - 1 - jj
            sl = pl.ds(pl.multiple_of(j * tb, tb), tb)
            ccum, lloc = _rscan_block(c_scr[sl, :], g_scr[sl, :], rows_b)
            lam_t = lloc + ccum * lnext
            l_scr[sl, :] = lam_t
            return jnp.sum(jnp.where(rows_b == 0, lam_t, 0.0), axis=0, keepdims=True)

        lax.fori_loop(0, nb, blk, jnp.zeros((1, LANE), F32))
        db = l_scr[...]
        da = db * _shift_down(h, 1, rows)
        dmult = db * ig * xa
        dig = db * mult * xa
        dxa = db * mult * ig
        dla = da * a - dmult * (a * a) / mult
        dr = -LRU_C * sp * dla
        dsp = jnp.sum(-LRU_C * r * dla, axis=0, keepdims=True)
        dlam = -dsp * _sigmoid(-lam_ref[...])
        dzr = dr * r * (1.0 - r)
        dzi = dig * ig * (1.0 - ig)
        dzr_b, dzi_b, xa_b = _bf(dzr), _bf(dzi), _bf(xa)
        dxa = dxa + _mm_nt(dzr_b, wa_ref[...]) + _mm_nt(dzi_b, wx_ref[...])
        gwa_ref[...] = _mm_tn(xa_b, dzr_b)
        gwx_ref[...] = _mm_tn(xa_b, dzi_b)
        dax, dws, dcb = _conv_bwd(ax, dxa, cw_ref, rows)
        du_ref[:, 0:LANE] = _bf(dax)
        parts = dws + [dcb, jnp.sum(dzr, axis=0, keepdims=True), jnp.sum(dzi, axis=0, keepdims=True), dlam]
        for n, p in enumerate(parts):
            red_ref[pl.ds(n, 1), :] = p

    col = pl.BlockSpec((S, LANE), lambda t: (0, t))
    gw = pl.BlockSpec((None, LANE, LANE), lambda t: (t, 0, 0))
    return pl.pallas_call(
        body, name="lru_bwd", grid=(LRU_W // LANE,),
        in_specs=_lru_specs(S, l) + [col, col, pl.BlockSpec(memory_space=pl.ANY)],
        out_specs=[pl.BlockSpec((S, 2 * LANE), lambda t: (0, OFF_LRU // (2 * LANE) + t)),
                   pl.BlockSpec((8, LANE), lambda t: (0, t)), gw, gw],
        out_shape=[SDS((S, N_PAD), BF16), SDS((8, LRU_W), F32), SDS((4, LANE, LANE), F32), SDS((4, LANE, LANE), F32)],
        scratch_shapes=[pltpu.VMEM((S, LANE), F32)] * 3,
        input_output_aliases={10: 0},
        compiler_params=_cp(("parallel",)),
    )(u, cw, cb, wa, ba, wx, bx, lam, h_lru, dycat, du)


HG_LEVELS = 6


def _hg_consts():
    C = HG_CHUNK
    t = np.arange(C)[:, None]
    r = np.arange(C)[None, :]
    mats = []
    for l in range(HG_LEVELS):
        b = 1 << l
        upper = (t % (2 * b)) >= b
        anchor = (t // (2 * b)) * 2 * b + b - 1
        mats.append((upper & (r > anchor) & (r <= t)) | ((~upper) & (r > t) & (r <= anchor)))
    mats.append(r <= t)
    mats.append(r > t)
    return np.concatenate(mats, 0).astype(np.float32)


def _hg_factors(hf, lb, mall):
    s = _sigmoid(hf)
    f = lb + (1.0 - lb) * s
    lf = jnp.log(f)
    k = (1.0 - lb) * _sigmoid(-hf)
    e = jnp.exp(_sel_l(mall, lf))
    C = HG_CHUNK
    rows = _iota((C, HG_W), 0)
    eq, ek = [], []
    for l in range(HG_LEVELS):
        el = e[l * C:(l + 1) * C]
        eq.append(jnp.where((lax.shift_right_logical(rows, l) & 1) == 1, el, 0.0))
        ek.append(el - eq[l])
    ecum = e[HG_LEVELS * C:(HG_LEVELS + 1) * C]
    erem = e[(HG_LEVELS + 1) * C:(HG_LEVELS + 2) * C]
    return s, f, k, eq, ek, ecum, erem


def _hg_masks():
    C = HG_CHUNK
    ri, ci = _iota((C, C), 0), _iota((C, C), 1)
    rr = _iota((C, LANE), 0)
    gm = [(lax.shift_right_logical(ri, l + 1) == lax.shift_right_logical(ci, l + 1)).astype(F32)
          for l in range(HG_LEVELS)]
    up = [(lax.shift_right_logical(rr, l) & 1) == 1 for l in range(HG_LEVELS)]
    eye = (ri == ci).astype(F32)
    return gm, up, eye, rr


def _hg_scores(qh, kh, eq, ek, sl, gm, up, eye):
    del up
    qs, ks, qb, kb = [], [], [], []
    p = _mm_nt(_bf(qh), _bf(kh)) * eye
    for l in range(HG_LEVELS):
        qs.append(qh * eq[l][:, sl])
        ks.append(kh * ek[l][:, sl])
        qb.append(_bf(qs[l]))
        kb.append(_bf(ks[l]))
        p = p + _mm_nt(qb[l], kb[l]) * gm[l]
    return p, qs, ks, qb, kb


HG_SUB = 4


def _hg_fwd(u, lb, nw, mall, ycat):
    S = u.shape[0]
    C = HG_CHUNK
    n = S // C
    rows = HG_SUB * C

    def body(u_ref, lb_ref, nw_ref, mall_ref, ycat_in, ycat_ref, o_ref, st_ref, st):
        del ycat_in

        @pl.when(pl.program_id(0) == 0)
        def _():
            st[...] = jnp.zeros_like(st)

        gm, up, eye, rr = _hg_masks()
        for sub in range(HG_SUB):
            r = slice(sub * C, (sub + 1) * C)
            q = _silu(_f(u_ref[r, 0:512]))
            v = u_ref[r, 1024:1536]
            _, _, k, eq, ek, ecum, erem = _hg_factors(_f(u_ref[r, 512:1024]), lb_ref[...], mall_ref[...])
            for h in range(HG_HEADS):
                sl = slice(h * LANE, (h + 1) * LANE)
                qh, kh, vh = q[:, sl], k[:, sl], _bf(v[:, sl])
                p = _hg_scores(qh, kh, eq, ek, sl, gm, up, eye)[0]
                sth = st[h]
                st_ref[sub, h] = sth
                o_ref[r, sl] = _mm(_bf(p), vh) + _mm_nt(_bf(qh * ecum[:, sl]), _bf(sth))
                st[h] = sth * _last_row(ecum[:, sl], rr) + _mm_tn(vh, _bf(kh * erem[:, sl]))
            o = o_ref[r, :]
            inv = lax.rsqrt(jnp.mean(o * o, axis=-1, keepdims=True) + EPS)
            ycat_ref[r, :] = _bf((o * inv) * nw_ref[...] * _silu(_f(u_ref[r, 1536:2048])))

    return pl.pallas_call(
        body, name="hg_fwd", grid=(n // HG_SUB,),
        in_specs=[pl.BlockSpec((rows, 2048), lambda i: (i, 0)), _spec(lb), _spec(nw), _full(mall.shape),
                  pl.BlockSpec(memory_space=pl.ANY)],
        out_specs=[pl.BlockSpec((rows, HG_W), lambda i: (i, 1)), pl.BlockSpec((rows, HG_W), lambda i: (i, 0)),
                   pl.BlockSpec((HG_SUB, HG_HEADS, LANE, LANE), lambda i: (i, 0, 0, 0))],
        out_shape=[SDS((S, D_INNER), BF16), SDS((S,HG_W), F32), SDS((n, HG_HEADS, LANE, LANE), F32)],
        scratch_shapes=[pltpu.VMEM((HG_HEADS, LANE, LANE), F32)],
        input_output_aliases={4: 0},
        compiler_params=_cp(("arbitrary",)),
    )(u, _arr(lb), _arr(nw), mall, ycat)


def _hg_bwd(u, lb, nw, mall, mall_t, o_b, states, dycat, du):
    S = u.shape[0]
    C = HG_CHUNK
    n = S // C
    nb = n // HG_SUB
    rows = HG_SUB * C
    L2 = HG_LEVELS

    def body(u_ref, lb_ref, nw_ref, mall_ref, mallt_ref, o_ref, st_ref, dy_ref, du_in, du_ref, red_ref,
             dst, dlast_s, dq_s, dk_s, dex):
        del du_in

        @pl.when(pl.program_id(0) == 0)
        def _():
            dst[...] = jnp.zeros_like(dst)
            red_ref[...] = jnp.zeros_like(red_ref)

        lb = lb_ref[...]
        nwv = nw_ref[...]
        gm, up, eye, rr = _hg_masks()
        for sub in reversed(range(HG_SUB)):
            r = slice(sub * C, (sub + 1) * C)
            hq, hf, hg = _f(u_ref[r, 0:512]), _f(u_ref[r, 512:1024]), _f(u_ref[r, 1536:2048])
            q = _silu(hq)
            v = u_ref[r, 1024:1536]
            s, f, k, eq, ek, ecum, erem = _hg_factors(hf, lb, mall_ref[...])
            o = o_ref[r, :]
            dy = _f(dy_ref[r, :])
            inv = lax.rsqrt(jnp.mean(o * o, axis=-1, keepdims=True) + EPS)
            ohat = o * inv
            du_ref[r, 1536:2048] = _bf(dy * ohat * nwv * _dsilu(hg))
            dn = dy * _silu(hg)
            red_ref[0:1, :] += jnp.sum(dn * ohat, axis=0, keepdims=True)
            dohat = dn * nwv
            do = inv * (dohat - ohat * jnp.mean(dohat * ohat, axis=-1, keepdims=True))
            for h in range(HG_HEADS):
                sl = slice(h * LANE, (h + 1) * LANE)
                qh, kh, vh, doh = q[:, sl], k[:, sl], _bf(v[:, sl]), _bf(do[:, sl])
                p, qs, ks, qb, kb = _hg_scores(qh, kh, eq, ek, sl, gm, up, eye)
                st_f = st_ref[sub, h]
                sth = _bf(st_f)
                dsth = dst[h]
                dsth_b = _bf(dsth)
                qt = qh * ecum[:, sl]
                kt = kh * erem[:, sl]
                elast = _last_row(ecum[:, sl], rr)
                dp = _mm_nt(doh, vh)
                du_ref[r, 1024 + h * LANE:1024 + (h + 1) * LANE] = _bf(_mm_tn(_bf(p), doh) + _mm_nt(_bf(kt), dsth_b))
                dpe = _bf(dp * eye)
                dqt = _mm(doh, sth)
                dkt = _mm(vh, dsth_b)
                dq = dqt * ecum[:, sl] + _mm(dpe, _bf(kh))
                dk = dkt * erem[:, sl] + _mm_tn(dpe, _bf(qh))
                dex[sub, L2 * C:(L2 + 1) * C, sl] = dqt * qt
                dex[sub, (L2 + 1) * C:(L2 + 2) * C, sl] = dkt * kt
                for l in range(HG_LEVELS):
                    dpl = _bf(dp * gm[l])
                    dql = _mm(dpl, kb[l])
                    dkl = _mm_tn(dpl, qb[l])
                    dq = dq + dql * eq[l][:, sl]
                    dk = dk + dkl * ek[l][:, sl]
                    dex[sub, l * C:(l + 1) * C, sl] = dql * qs[l] + dkl * ks[l]
                dlast_s[sub, :, sl] = jnp.sum(dsth * st_f, axis=0, keepdims=True) * elast
                dst[h] = dsth * elast + _mm_tn(doh, _bf(qt))
                dq_s[sub, :, sl] = dq
                dk_s[sub, :, sl] = dk
            dq = dq_s[sub]
            dk = dk_s[sub]
            dlf = _sel_l2(mallt_ref[...], dex[sub]) + dlast_s[sub]
            du_ref[r, 0:512] = _bf(dq * _dsilu(hq))
            t = (1.0 - s) * (dlf / f - dk)
            du_ref[r, 512:1024] = _bf((1.0 - lb) * s * t)
            red_ref[1:2, :] += jnp.sum(t, axis=0, keepdims=True)

    rev = lambda i: (nb - 1 - i, 0)
    return pl.pallas_call(
        body, name="hg_bwd", grid=(nb,),
        in_specs=[pl.BlockSpec((rows, 2048), rev), _spec(lb), _spec(nw), _full(mall.shape), _full(mall_t.shape),
                  pl.BlockSpec((rows, HG_W), rev),
                  pl.BlockSpec((HG_SUB, HG_HEADS, LANE, LANE), lambda i: (nb - 1 - i, 0, 0, 0)),
                  pl.BlockSpec((rows, HG_W), lambda i: (nb - 1 - i, 1)), pl.BlockSpec(memory_space=pl.ANY)],
        out_specs=[pl.BlockSpec((rows, 2048), rev), pl.BlockSpec((8, HG_W), lambda i: (0, 0))],
        out_shape=[SDS((S, N_PAD), BF16), SDS((8, HG_W), F32)],
        scratch_shapes=[pltpu.VMEM((HG_HEADS, LANE, LANE), F32), pltpu.VMEM((HG_SUB, 1, HG_W), F32),
                        pltpu.VMEM((HG_SUB, C, HG_W), F32), pltpu.VMEM((HG_SUB, C, HG_W), F32),
                        pltpu.VMEM((HG_SUB, (L2 + 2) * C, HG_W), F32)],
        input_output_aliases={8: 0},
        compiler_params=_cp(("arbitrary",)),
    )(u, _arr(lb), _arr(nw), mall, mall_t, o_b, states, dycat, du)


def _ssdconv_fwd(l, u, cw, cb):
    S = u.shape[0]

    def body(u_ref, cw_ref, cb_ref, out_ref):
        rows = _iota((S, LANE), 0)
        out_ref[...] = _silu(_conv_fwd(_f(u_ref[...]), cw_ref, cb_ref, rows))

    return pl.pallas_call(
        body, name="ssdconv_fwd", grid=(SSD_CONV // LANE,),
        in_specs=[pl.BlockSpec((S, LANE), lambda t: (0, OFF_XBC // LANE + t)),
                  pl.BlockSpec((None, 4, LANE), lambda t: (l, 0, t)), pl.BlockSpec((None, 1, LANE), lambda t: (l, 0, t))],
        out_specs=pl.BlockSpec((S, LANE), lambda t: (0, t)),
        out_shape=SDS((S, SSD_CONV), F32),
        compiler_params=_cp(("parallel",)),
    )(u, cw, cb)


def _ssdconv_bwd(l, u, cw, cb, dxbc, du):
    S = u.shape[0]

    def body(u_ref, cw_ref, cb_ref, d_ref, du_in, du_ref, red_ref):
        del du_in
        rows = _iota((S, LANE), 0)
        x = _f(u_ref[...])
        dco = d_ref[...] * _dsilu(_conv_fwd(x, cw_ref, cb_ref, rows))
        dx, dws, dcb = _conv_bwd(x, dco, cw_ref, rows)
        du_ref[...] = _bf(dx)
        for n, p in enumerate(dws + [dcb]):
            red_ref[pl.ds(n, 1), :] = p
        red_ref[pl.ds(5, 3), :] = jnp.zeros((3, LANE), F32)

    ucol = pl.BlockSpec((S, LANE), lambda t: (0, OFF_XBC // LANE + t))
    return pl.pallas_call(
        body, name="ssdconv_bwd", grid=(SSD_CONV // LANE,),
        in_specs=[ucol, pl.BlockSpec((None, 4, LANE), lambda t: (l, 0, t)),
                  pl.BlockSpec((None, 1, LANE), lambda t: (l, 0, t)),
                  pl.BlockSpec((S, LANE), lambda t: (0, t)), pl.BlockSpec(memory_space=pl.ANY)],
        out_specs=[ucol, pl.BlockSpec((8, LANE), lambda t: (0, t))],
        out_shape=[SDS((S, N_PAD), BF16), SDS((8, SSD_CONV), F32)],
        input_output_aliases={4: 0},
        compiler_params=_cp(("parallel",)),
    )(u, cw, cb, dxbc, du)


SSD_SUB = 2


def _ssd_consts():
    e64 = np.zeros((LANE, SSD_W), np.float32)
    for h in range(SSD_HEADS):
        e64[h, h * SSD_P:(h + 1) * SSD_P] = 1.0
    T = SSD_CHUNK
    tril = (np.arange(T)[None, :] <= np.arange(T)[:, None]).astype(np.float32)
    return e64, tril, tril.T.copy()


def _ssd_common(zdt, bias_ref, alog_ref, tril, e64, cum_ref, cumt_ref):
    T = SSD_CHUNK
    lane = _iota((1, LANE), 1)
    a_neg = jnp.where(lane < SSD_HEADS, -jnp.exp(alog_ref[...]), 0.0)
    dtpre = zdt[:, SSD_W:SSD_W + LANE] + bias_ref[...]
    dt = _softplus(dtpre)
    cum = _sel_l(tril, dt * a_neg)
    cum_ref[...] = cum
    cumt_ref[...] = cum.T
    cum_x = _sel_r(cum, e64)
    last_x = _last_row(cum_x, _iota((T, SSD_W), 0))
    ecum_x = jnp.exp(cum_x)
    erem_x = jnp.exp(last_x - cum_x)
    elast_x = jnp.exp(last_x)
    dt_x = _sel_r(dt, e64)
    return a_neg, dtpre, dt, ecum_x, erem_x, elast_x, dt_x


def _ssd_decay(cum_ref, cumt_ref, h, causal):
    T = SSD_CHUNK
    diff = jnp.broadcast_to(cum_ref[:, pl.ds(h, 1)], (T, T)) - cumt_ref[pl.ds(h, 1), :]
    return jnp.exp(jnp.where(causal, diff, NEG))


def _group_norm_fwd(y1, nwv):
    outs, invs = [], []
    for g in range(2):
        seg = y1[:, g * 512:(g + 1) * 512]
        inv = lax.rsqrt(jnp.mean(seg * seg, axis=-1, keepdims=True) + EPS)
        outs.append(seg * inv * nwv[:, g * 512:(g + 1) * 512])
        invs.append(inv)
    return outs, invs


def _ssd_fwd(u, xbc, bias, alog, dskip_x, nw, consts, ycat):
    S = u.shape[0]
    T = SSD_CHUNK
    n = S // T
    rows = SSD_SUB * T
    e64, tril, _ = consts

    def body(u_ref, xbc_ref, bias_ref, alog_ref, dx_ref, nw_ref, e64_ref, tril_ref, ycat_in,
             ycat_ref, y_ref, st_ref, st, cumt, cum_e):
        del ycat_in

        @pl.when(pl.program_id(0) == 0)
        def _():
            st[...] = jnp.zeros_like(st)

        causal = _iota((T, T), 0) >= _iota((T, T), 1)
        lo = _iota((T, LANE), 1) < SSD_P
        for sub in range(SSD_SUB):
            r = slice(sub * T, (sub + 1) * T)
            zdt = _f(u_ref[r, :])
            z = zdt[:, 0:SSD_W]
            xs = xbc_ref[r, 0:SSD_W]
            cum_r, cumt_r = cum_e.at[sub], cumt.at[sub]
            _, _, _, ecum_x, erem_x, elast_x, dt_x = _ssd_common(
                zdt, bias_ref, alog_ref, tril_ref[...], e64_ref[...], cum_r, cumt_r)
            xdt = xs * dt_x
            xrem = xdt * erem_x
            st_ref[sub] = st[...]
            for g in range(2):
                gs = slice(g * 512, (g + 1) * 512)
                bg = _bf(xbc_ref[r, SSD_W + g * LANE:SSD_W + (g + 1) * LANE])
                cg = _bf(xbc_ref[r, SSD_W + 256 + g * LANE:SSD_W + 256 + (g + 1) * LANE])
                cb = _mm_nt(cg, bg)
                yin = _mm(cg, _bf(st[:, gs])) * ecum_x[:, gs]
                for j in range(4):
                    h0 = 8 * g + 2 * j
                    cs = slice(h0 * SSD_P, (h0 + 2) * SSD_P)
                    xp = xdt[:, cs]
                    s0 = _bf(cb * _ssd_decay(cum_r, cumt_r, h0, causal))
                    s1 = _bf(cb * _ssd_decay(cum_r, cumt_r, h0 + 1, causal))
                    y_ref[r, cs] = (_mm(s0, _bf(jnp.where(lo, xp, 0.0))) + _mm(s1, _bf(jnp.where(lo, 0.0, xp)))
                                    + yin[:, j * LANE:(j + 1) * LANE])
                st[:, gs] = st[:, gs] * elast_x[:, gs] + _mm_tn(bg, _bf(xrem[:, gs]))
            y1 = (y_ref[r, :] + dx_ref[...] * xs) * _silu(z)
            outs, _ = _group_norm_fwd(y1, nw_ref[...])
            for g in range(2):
                ycat_ref[r, g * 512:(g + 1) * 512] = _bf(outs[g])

    return pl.pallas_call(
        body, name="ssd_fwd", grid=(n // SSD_SUB,),
        in_specs=[pl.BlockSpec((rows, SSD_W + LANE), lambda i: (i, OFF_Z // (SSD_W + LANE))),
                  pl.BlockSpec((rows, SSD_CONV), lambda i: (i, 0)), _spec(bias), _spec(alog), _spec(dskip_x), _spec(nw),
                  _full(e64.shape), _full(tril.shape), pl.BlockSpec(memory_space=pl.ANY)],
        out_specs=[pl.BlockSpec((rows, SSD_W), lambda i: (i, 1)), pl.BlockSpec((rows, SSD_W), lambda i: (i, 0)),
                   pl.BlockSpec((SSD_SUB, SSD_N, SSD_W), lambda i: (i, 0, 0))],
        out_shape=[SDS((S, D_INNER), BF16), SDS((S,SSD_W), F32), SDS((n, SSD_N, SSD_W), F32)],
        scratch_shapes=[pltpu.VMEM((SSD_N, SSD_W), F32), pltpu.VMEM((SSD_SUB, LANE, T), F32),
                        pltpu.VMEM((SSD_SUB, T, LANE), F32)],
        input_output_aliases={8: 0},
        compiler_params=_cp(("arbitrary",)),
    )(u, xbc, _arr(bias), _arr(alog), _arr(dskip_x), _arr(nw), _bfc(e64), _bfc(tril), ycat)


def _ssd_bwd(u, xbc, bias, alog, dskip_x, nw, consts, y_ssd, states, dycat, du, tok):
    S = u.shape[0]
    T = SSD_CHUNK
    n = S // T
    e64, tril, triu = consts
    e64t = np.ascontiguousarray(e64.T)

    def chunk(u_ref, xbc_ref, bias_ref, alog_ref, dx_ref, nw_ref, e64_ref, e64t_ref, tril_ref, triu_ref,
              y_ref, st_ref, dy_ref, du_ref, dxbc_ref, red_ref, dst, dl_s, cumt, dxdt_s, dy0_s, gb_s, gc_s, cum_e, cs_s):
        zdt = _f(u_ref[...])
        z = zdt[:, 0:SSD_W]
        xs = xbc_ref[:, 0:SSD_W]
        a_neg, dtpre, dt, ecum_x, erem_x, elast_x, dt_x = _ssd_common(
            zdt, bias_ref, alog_ref, tril_ref[...], e64_ref[...], cum_e, cumt)
        causal = _iota((T, T), 0) >= _iota((T, T), 1)
        lo = _iota((T, LANE), 1) < SSD_P
        xdt = xs * dt_x
        xrem = xdt * erem_x
        y = y_ref[...]
        dxv = dx_ref[...]
        nwv = nw_ref[...]
        sz = _silu(z)
        y0 = y + dxv * xs
        y1 = y0 * sz
        for g in range(2):
            gs = slice(g * 512, (g + 1) * 512)
            seg = y1[:, gs]
            inv = lax.rsqrt(jnp.mean(seg * seg, axis=-1, keepdims=True) + EPS)
            shat = seg * inv
            dyg = _f(dy_ref[:, gs])
            red_ref[0:1, gs] += jnp.sum(dyg * shat, axis=0, keepdims=True)
            dsh = dyg * nwv[:, gs]
            dy1g = inv * (dsh - shat * jnp.mean(dsh * shat, axis=-1, keepdims=True))
            du_ref[:, gs] = _bf(dy1g * y0[:, gs] * _dsilu(z[:, gs]))
            dy0_s[:, gs] = dy1g * sz[:, gs]
        dy0 = dy0_s[...]
        red_ref[1:2, :] += jnp.sum(dy0 * xs, axis=0, keepdims=True)
        dyin = dy0 * ecum_x
        lane = _iota((T, LANE), 1)
        dcum = jnp.zeros((T, LANE), F32)

        def decay_grad(h, gm):
            cs_s[pl.ds(h, 1), :] = jnp.sum(gm, axis=0, keepdims=True)
            return jnp.where(lane == h, jnp.sum(gm, axis=1, keepdims=True), 0.0)

        for g in range(2):
            gs = slice(g * 512, (g + 1) * 512)
            bg = _bf(xbc_ref[:, SSD_W + g * LANE:SSD_W + (g + 1) * LANE])
            cg = _bf(xbc_ref[:, SSD_W + 256 + g * LANE:SSD_W + 256 + (g + 1) * LANE])
            cb = _mm_nt(cg, bg)
            dst_f, st_f = dst[:, gs], st_ref[:, gs]
            dstg = _bf(dst_f)
            stg = _bf(st_f)
            dyin_g = _bf(dyin[:, gs])
            xrem_g = _bf(xrem[:, gs])
            dcb = jnp.zeros((T, T), F32)
            dxr = _mm(bg, dstg)
            dxdt_s[:, gs] = dxr * erem_x[:, gs]
            gc_s[:, gs] = dxr * xrem[:, gs]
            gb_s[:, gs] = dyin[:, gs] * _mm(cg, stg)
            dl_s[:, gs] = jnp.sum(dst_f * st_f, axis=0, keepdims=True) * elast_x[:, gs]
            for j in range(4):
                h0 = 8 * g + 2 * j
                cs = slice(h0 * SSD_P, (h0 + 2) * SSD_P)
                xp = xdt[:, cs]
                dyp = dy0[:, cs]
                x_lo, x_hi = _bf(jnp.where(lo, xp, 0.0)), _bf(jnp.where(lo, 0.0, xp))
                d_lo, d_hi = _bf(jnp.where(lo, dyp, 0.0)), _bf(jnp.where(lo, 0.0, dyp))
                l0 = _ssd_decay(cum_e, cumt, h0, causal)
                l1 = _ssd_decay(cum_e, cumt, h0 + 1, causal)
                s0 = cb * l0
                s1 = cb * l1
                ds0 = _mm_nt(d_lo, x_lo)
                ds1 = _mm_nt(d_hi, x_hi)
                dcb = dcb + ds0 * l0 + ds1 * l1
                dxdt_s[:, cs] += _mm_tn(_bf(s0), d_lo) + _mm_tn(_bf(s1), d_hi)
                dcum = dcum + decay_grad(h0, ds0 * s0) + decay_grad(h0 + 1, ds1 * s1)
            dcb_b = _bf(dcb)
            dxbc_ref[:, SSD_W + g * LANE:SSD_W + (g + 1) * LANE] = _mm_tn(dcb_b, cg) + _mm_nt(xrem_g, dstg)
            dxbc_ref[:, SSD_W + 256 + g * LANE:SSD_W + 256 + (g + 1) * LANE] = _mm(dcb_b, bg) + _mm_nt(dyin_g, stg)
            dst[:, gs] = dst_f * elast_x[:, gs] + _mm_tn(cg, dyin_g)
        dxdt = dxdt_s[...]
        dxbc_ref[:, 0:SSD_W] = dxdt * dt_x + dy0 * dxv
        e64t = e64t_ref[...]
        gc = gc_s[...]
        dlast_x = jnp.sum(gc, axis=0, keepdims=True) + dl_s[...]
        dlast = jnp.max(_sel_r(jnp.broadcast_to(dlast_x, (8, SSD_W)), e64t), axis=0, keepdims=True)
        dcum = (dcum - cs_s[...].T + _sel_r(gb_s[...] - gc, e64t)
                + jnp.where(_iota((T, LANE), 0) == T - 1, dlast, 0.0))
        dda = _sel_l(triu_ref[...], dcum)
        ddt = dda * a_neg + _sel_r(dxdt * xs, e64t)
        ddtpre = ddt * _sigmoid(dtpre)
        du_ref[:, SSD_W:SSD_W + LANE] = _bf(jnp.where(lane < SSD_HEADS, ddtpre, 0.0))
        red_ref[2:3, 0:LANE] += jnp.sum(ddtpre, axis=0, keepdims=True)
        red_ref[3:4, 0:LANE] += jnp.sum(dda * dt, axis=0, keepdims=True)

    def body(u_ref, xbc_ref, bias_ref, alog_ref, dx_ref, nw_ref, e64_ref, e64t_ref, tril_ref, triu_ref,
             y_ref, st_ref, dy_ref, du_in, tok_ref, du_ref, dxbc_ref, red_ref, dst, *scratch):
        del du_in, tok_ref

        @pl.when(pl.program_id(0) == 0)
        def _():
            dst[...] = jnp.zeros_like(dst)
            red_ref[...] = jnp.zeros_like(red_ref)
            scratch[-1][...] = jnp.zeros_like(scratch[-1])

        for sub in reversed(range(SSD_SUB)):
            rs = pl.ds(sub * T, T)
            chunk(u_ref.at[rs], xbc_ref.at[rs], bias_ref, alog_ref, dx_ref, nw_ref, e64_ref, e64t_ref, tril_ref, triu_ref,
                  y_ref.at[rs], st_ref.at[sub], dy_ref.at[rs], du_ref.at[rs], dxbc_ref.at[rs], red_ref, dst,
                  *[s.at[sub] for s in scratch])

    nb = n // SSD_SUB
    rows = SSD_SUB * T
    rev = lambda i: (nb - 1 - i, 0)
    sub_scratch = [(1, SSD_W), (LANE, T)] + [(T, SSD_W)] * 4 + [(T, LANE), (LANE, T)]
    return pl.pallas_call(
        body, name="ssd_bwd", grid=(nb,),
        in_specs=[pl.BlockSpec((rows, SSD_W + LANE), lambda i: (nb - 1 - i, OFF_Z // (SSD_W + LANE))),
                  pl.BlockSpec((rows, SSD_CONV), rev), _spec(bias), _spec(alog), _spec(dskip_x), _spec(nw),
                  _full(e64.shape), _full(e64t.shape), _full(tril.shape), _full(triu.shape),
                  pl.BlockSpec((rows, SSD_W), rev), pl.BlockSpec((SSD_SUB, SSD_N, SSD_W), lambda i: (nb - 1 - i, 0, 0)),
                  pl.BlockSpec((rows, SSD_W), lambda i: (nb - 1 - i, 1)), pl.BlockSpec(memory_space=pl.ANY),
                  pl.BlockSpec(memory_space=pl.ANY)],
        out_specs=[pl.BlockSpec((rows, SSD_W + LANE), lambda i: (nb - 1 - i, OFF_Z // (SSD_W + LANE))),
                   pl.BlockSpec((rows, SSD_CONV), rev), pl.BlockSpec((8, SSD_W), lambda i: (0, 0))],
        out_shape=[SDS((S, N_PAD), BF16), SDS((S, SSD_CONV), F32), SDS((8, SSD_W), F32)],
        scratch_shapes=[pltpu.VMEM((SSD_N, SSD_W), F32)] + [pltpu.VMEM((SSD_SUB,) + s, F32) for s in sub_scratch],
        input_output_aliases={13: 0},
        compiler_params=_cp(("arbitrary",)),
    )(u, xbc, _arr(bias), _arr(alog), _arr(dskip_x), _arr(nw), _bfc(e64), _bfc(e64t), _bfc(tril), _bfc(triu), y_ssd,
      states, dycat, du, tok)


def _bfc(a):
    return jnp.asarray(a, BF16)


def _outproj_fwd(ycat, wo, x, gate, tok):
    S = x.shape[0]
    tm = min(512, S)

    def body(yc_ref, wo_ref, x_ref, g_ref, tok_ref, xn_ref, y_ref):
        del tok_ref
        y = _mm(_bf(yc_ref[...]), wo_ref[...])
        y_ref[...] = y
        xn_ref[...] = x_ref[...] + g_ref[...] * y

    row = pl.BlockSpec((tm, D_MODEL), lambda i: (i, 0))
    return pl.pallas_call(
        body, name="outproj_fwd", grid=(S // tm,),
        in_specs=[pl.BlockSpec((tm, D_INNER), lambda i: (i, 0)), _full((D_INNER, D_MODEL)), row, _spec(gate),
                  pl.BlockSpec(memory_space=pl.ANY)],
        out_specs=[row, row],
        out_shape=[SDS((S, D_MODEL), F32), SDS((S, D_MODEL), F32)],
        compiler_params=_cp(("parallel",)),
    )(ycat, wo, x, _arr(gate), tok)


def _outproj_bwd(dxn, y, gate, ycat, wo):
    S = dxn.shape[0]
    tm = min(512, S)

    def body(dx_ref, y_ref, g_ref, yc_ref, wo_ref, dyc_ref, gwo_ref, dg_ref, acc):
        @pl.when(pl.program_id(0) == 0)
        def _():
            acc[...] = jnp.zeros_like(acc)
            dg_ref[...] = jnp.zeros_like(dg_ref)

        dxv = dx_ref[...]
        dy = _bf(dxv * g_ref[...])
        dg_ref[0:1, :] += jnp.sum(dxv * y_ref[...], axis=0, keepdims=True)
        dyc_ref[...] = _mm_nt(dy, wo_ref[...])
        acc[...] += _mm_tn(_bf(yc_ref[...]), dy)

        @pl.when(pl.program_id(0) == pl.num_programs(0) - 1)
        def _():
            gwo_ref[...] = acc[...].astype(BF16)

    row = pl.BlockSpec((tm, D_MODEL), lambda i: (i, 0))
    wide = pl.BlockSpec((tm, D_INNER), lambda i: (i, 0))
    return pl.pallas_call(
        body, name="outproj_bwd", grid=(S // tm,),
        in_specs=[row, row, _spec(gate), wide, _full((D_INNER, D_MODEL))],
        out_specs=[wide, _full((D_INNER, D_MODEL)), _full((8, D_MODEL))],
        out_shape=[SDS((S, D_INNER), F32), SDS((D_INNER, D_MODEL), BF16), SDS((8, D_MODEL), F32)],
        scratch_shapes=[pltpu.VMEM((D_INNER, D_MODEL), F32)],
        compiler_params=_cp(("arbitrary",)),
    )(dxn, y, _arr(gate), ycat, wo)


def _loss_head(x, fw, target):
    S = x.shape[0]
    tm = min(512, S)

    def body(x_ref, fw_ref, t_ref, dx_ref, red_ref):
        @pl.when(pl.program_id(0) == 0)
        def _():
            red_ref[...] = jnp.zeros_like(red_ref)

        xv = x_ref[...]
        fwv = fw_ref[...]
        inv = lax.rsqrt(jnp.mean(xv * xv, axis=-1, keepdims=True) + EPS)
        xhat = xv * inv
        err = xhat * fwv - t_ref[...]
        col = jnp.sum(err * err, axis=0, keepdims=True)
        red_ref[1:2, :] += jnp.broadcast_to(jnp.sum(col, axis=1, keepdims=True) * (0.5 / D_MODEL), (1, D_MODEL))
        dy = err * (1.0 / D_MODEL)
        red_ref[0:1, :] += jnp.sum(dy * xhat, axis=0, keepdims=True)
        dxhat = dy * fwv
        dx_ref[...] = inv * (dxhat - xhat * jnp.mean(dxhat * xhat, axis=-1, keepdims=True))

    row = pl.BlockSpec((tm, D_MODEL), lambda i: (i, 0))
    return pl.pallas_call(
        body, name="loss_head", grid=(S // tm,),
        in_specs=[row, _vec(D_MODEL), row],
        out_specs=[row, _full((8, D_MODEL))],
        out_shape=[SDS((S, D_MODEL), F32), SDS((8, D_MODEL), F32)],
        compiler_params=_cp(("arbitrary",)),
    )(x, fw, target)


ADA_COLS = 3 * D_MODEL // N_DEV


def _ada_fwd(c_all, w_ada, b_cols):
    def body(c_ref, w_ref, b_ref, out_ref):
        out_ref[...] = _mm(_bf(_silu(c_ref[...])), _bf(w_ref[...])) + b_ref[...]

    return pl.pallas_call(
        body, name="ada_fwd", grid=(DEPTH,),
        in_specs=[_full((N_DEV, D_MODEL)), pl.BlockSpec((None, D_MODEL, ADA_COLS), lambda l: (l, 0, 0)),
                  pl.BlockSpec((None, 1, ADA_COLS), lambda l: (l, 0, 0))],
        out_specs=pl.BlockSpec((None, N_DEV, ADA_COLS), lambda l: (l, 0, 0)),
        out_shape=SDS((DEPTH, N_DEV, ADA_COLS), F32),
        compiler_params=_cp(("parallel",)),
    )(c_all, w_ada, b_cols)


def _ada_bwd(ct_pad, dmod_pad):
    def body(c_ref, d_ref, out_ref):
        out_ref[...] = _mm(_bf(_silu(c_ref[...])), _bf(d_ref[...]))

    return pl.pallas_call(
        body, name="ada_bwd", grid=(DEPTH,),
        in_specs=[_full((D_MODEL, LANE)), pl.BlockSpec((None, LANE, ADA_COLS), lambda l: (l, 0, 0))],
        out_specs=pl.BlockSpec((None, D_MODEL, ADA_COLS), lambda l: (l, 0, 0)),
        out_shape=SDS((DEPTH, D_MODEL, ADA_COLS), F32),
        compiler_params=_cp(("parallel",)),
    )(ct_pad, dmod_pad)


def _adamw(parts, w, m, v, name, own=None, layers=None, prev=None):
    n, L, R, C = parts.shape
    lo, hi = layers or (0, L)
    tr = R
    while tr * C * 4 > (1 << 20) and tr % 16 == 0:
        tr //= 2
    first = 1 if own is None else 2

    def body(*refs):
        p_ref = refs[0]
        w_ref, m_ref, v_ref = refs[first:first + 3]
        g_ref, d_ref, mo_ref, vo_ref = refs[-4:]

        def part(k):
            if own is None:
                return p_ref[k].astype(F32)
            me = 4 * lax.axis_index("x") + 2 * lax.axis_index("y") + lax.axis_index("c")
            return jnp.where(me == k, refs[1][...], p_ref[k]).astype(F32)

        g = part(0)
        for k in range(1, n):
            g = g + part(k)
        mn = ADAM_B1 * m_ref[...] + (1.0 - ADAM_B1) * g
        vn = ADAM_B2 * v_ref[...] + (1.0 - ADAM_B2) * (g * g)
        m_hat = mn / (1.0 - ADAM_B1 ** ADAM_STEP)
        v_hat = vn / (1.0 - ADAM_B2 ** ADAM_STEP)
        g_ref[...] = g
        d_ref[...] = -ADAM_LR * (m_hat / (jnp.sqrt(v_hat) + ADAM_EPS) + ADAM_WD * w_ref[...])
        mo_ref[...] = mn
        vo_ref[...] = vn

    blk = pl.BlockSpec((None, tr, C), lambda l, i: (lo + l, i, 0))
    own_blk = [] if own is None else [pl.BlockSpec((None, tr, C), lambda l, i: (l, i, 0))]
    n_blk = 3 if own is None else 4
    return pl.pallas_call(
        body, name=name, grid=(hi - lo, R // tr),
        in_specs=[pl.BlockSpec((n, None, tr, C), lambda l, i: (0, lo + l, i, 0))] + own_blk + [blk] * 3
        + ([] if prev is None else [ANY] * 4),
        out_specs=[blk] * 4,
        out_shape=[SDS((L, R, C), F32)] * 4,
        input_output_aliases={} if prev is None else {1 + n_blk + k: k for k in range(4)},
        compiler_params=_cp(("parallel", "parallel")),
    )(parts, *([] if own is None else [own]), w, m, v, *([] if prev is None else prev))


MESH = pl.DeviceIdType.MESH
ANY = pl.BlockSpec(memory_space=pl.ANY)


def _all_gather(v, name):
    def body(v_ref, out_ref, send_sems, recv_sems, local_sem):
        x, y, c = lax.axis_index("x"), lax.axis_index("y"), lax.axis_index("c")
        me, sibling = (x, y, c), (x, y, 1 - c)
        chips = [(1 - x, y), (x, 1 - y), (1 - x, 1 - y)]

        def slot(px, py, pc):
            return out_ref.at[4 * px + 2 * py + pc]

        def copy(k, block, to, src=None):
            return pltpu.make_async_remote_copy(
                src_ref=slot(*block) if src is None else src, dst_ref=slot(*block),
                send_sem=send_sems.at[k], recv_sem=recv_sems.at[k], device_id=to, device_id_type=MESH)

        mine = pltpu.make_async_copy(v_ref, slot(*me), local_sem)
        mine.start()
        first = [copy(0, me, sibling, src=v_ref)]
        first += [copy(1 + j, me, (*chip, c), src=v_ref) for j, chip in enumerate(chips)]
        for cp in first:
            cp.start()
        passed = [copy(4 + j, (*chip, c), sibling) for j, chip in enumerate(chips)]
        for j, chip in enumerate(chips):
            copy(1 + j, (*chip, c), me).wait_recv()
            passed[j].start()
        copy(0, sibling, me).wait_recv()
        for j, chip in enumerate(chips):
            copy(4 + j, (*chip, 1 - c), me).wait_recv()
        for cp in first + passed:
            cp.wait_send()
        mine.wait()

    return pl.pallas_call(
        body, name=name, in_specs=[ANY], out_specs=ANY,
        out_shape=SDS((N_DEV,) + v.shape, v.dtype),
        scratch_shapes=[pltpu.SemaphoreType.DMA((7,)), pltpu.SemaphoreType.DMA((7,)), pltpu.SemaphoreType.DMA],
    )(v)


def _all_to_all(v, name):
    def body(v_ref, out_ref, send_sems, recv_sems, local_sem):
        x, y, c = lax.axis_index("x"), lax.axis_index("y"), lax.axis_index("c")
        mine_idx = 4 * x + 2 * y + c
        mine = pltpu.make_async_copy(v_ref.at[mine_idx], out_ref.at[mine_idx], local_sem)
        mine.start()
        sends, recvs = [], []
        for k in range(1, N_DEV):
            px = 1 - x if k & 4 else x
            py = 1 - y if k & 2 else y
            pc = 1 - c if k & 1 else c
            peer_idx = 4 * px + 2 * py + pc
            sems = dict(send_sem=send_sems.at[k - 1], recv_sem=recv_sems.at[k - 1], device_id=(px, py, pc),
                        device_id_type=MESH)
            sends.append(pltpu.make_async_remote_copy(src_ref=v_ref.at[peer_idx], dst_ref=out_ref.at[mine_idx], **sems))
            recvs.append(pltpu.make_async_remote_copy(src_ref=v_ref.at[peer_idx], dst_ref=out_ref.at[peer_idx], **sems))
        for cp in sends:
            cp.start()
        for cp in recvs:
            cp.wait_recv()
        for cp in sends:
            cp.wait_send()
        mine.wait()

    return pl.pallas_call(
        body, name=name, in_specs=[ANY], out_specs=ANY,
        out_shape=SDS(v.shape, v.dtype),
        scratch_shapes=[pltpu.SemaphoreType.DMA((7,)), pltpu.SemaphoreType.DMA((7,)), pltpu.SemaphoreType.DMA],
    )(v)


HBM_SPEC = pl.BlockSpec(memory_space=pltpu.HBM)
SEM_SPEC = pl.BlockSpec(memory_space=pltpu.SEMAPHORE)
EFFECT = pltpu.SideEffectType.DATAFLOW_SIDE_EFFECTING


EXCHANGE_PEERS = {"gather": range(1, N_DEV), "scatter": range(1, N_DEV), "chip": (1, 2, 4, 6), "pass": (2, 4, 6)}


def _exchange_copies(srcs, lands, send_sems, recv_sems, mode, layer):
    x, y, c = lax.axis_index("x"), lax.axis_index("y"), lax.axis_index("c")
    me = 4 * x + 2 * y + c
    copies = []
    for a, (src, land) in enumerate(zip(srcs, lands)):
        for k in EXCHANGE_PEERS[mode]:
            px = 1 - x if k & 4 else x
            py = 1 - y if k & 2 else y
            pc = 1 - c if k & 1 else c
            peer = 4 * px + 2 * py + pc
            if mode == "scatter":
                s, d, to = src.at[peer], land.at[me, layer], (px, py, pc)
            elif mode == "pass":
                s, d, to = land.at[peer], land.at[peer], (x, y, 1 - c)
            else:
                s, d, to = src, land.at[me], (px, py, pc)
            n = 7 * a + k - 1
            copies.append(pltpu.make_async_remote_copy(
                src_ref=s, dst_ref=d, send_sem=send_sems.at[n], recv_sem=recv_sems.at[n], device_id=to,
                device_id_type=MESH))
    return copies


def _exchange_start(name, srcs, lands, mode, layer=0, after=None):
    n = len(srcs)

    def body(*refs):
        send_sems, recv_sems = refs[-2 * n - 3], refs[-2 * n - 2]
        for cp in _exchange_copies(refs[:n], refs[n:2 * n], send_sems, recv_sems, mode, layer):
            cp.start()
        refs[-1][...] = jnp.zeros_like(refs[-1])

    arrays = list(srcs) + list(lands)
    sems = pltpu.SemaphoreType.DMA((7 * n,))
    out = pl.pallas_call(
        body, name=name,
        out_shape=(sems, sems, *[pltpu.HBM(v.shape, v.dtype) for v in arrays], SDS((8, LANE), F32)),
        in_specs=[HBM_SPEC] * (2 * n) + ([ANY] if after is not None else []),
        out_specs=(SEM_SPEC, SEM_SPEC, *[HBM_SPEC] * (2 * n), pl.BlockSpec(memory_space=pltpu.VMEM)),
        input_output_aliases={i: 2 + i for i in range(2 * n)},
        compiler_params=pltpu.CompilerParams(has_side_effects=EFFECT),
    )(*[pltpu.with_memory_space_constraint(v, pltpu.HBM) for v in arrays], *([after] if after is not None else []))
    return dict(sems=out[:2], srcs=out[2:2 + n], lands=out[2 + n:2 + 2 * n], token=out[-1], mode=mode,
                layer=layer)


def _exchange_wait(name, st, after, also=()):
    n = len(st["srcs"])

    def body(*refs):
        send_sems, recv_sems = refs[2 * n], refs[2 * n + 1]
        for cp in _exchange_copies(refs[:n], refs[n:2 * n], send_sems, recv_sems, st["mode"], st["layer"]):
            cp.wait_send()
            cp.wait_recv()

    arrays = list(st["srcs"]) + list(st["lands"])
    out = pl.pallas_call(
        body, name=name,
        out_shape=tuple(pltpu.HBM(v.shape, v.dtype) for v in arrays),
        in_specs=[HBM_SPEC] * (2 * n) + [SEM_SPEC, SEM_SPEC] + [ANY] * (1 + len(also)),
        out_specs=tuple([HBM_SPEC] * (2 * n)),
        input_output_aliases={i: i for i in range(2 * n)},
        compiler_params=pltpu.CompilerParams(has_side_effects=EFFECT),
    )(*arrays, *st["sems"], after, *also)
    st["srcs"] = out[:n]
    return out[n:]


def _exchange_relay(name, st, after, also=()):
    n = len(st["srcs"])

    def body(*refs):
        for cp in _exchange_copies(refs[:n], refs[n:2 * n], refs[2 * n], refs[2 * n + 1], st["mode"], st["layer"]):
            cp.wait_send()
            cp.wait_recv()
        send_sems, recv_sems = refs[-2 * n - 3], refs[-2 * n - 2]
        for cp in _exchange_copies(refs[:n], refs[n:2 * n], send_sems, recv_sems, "pass", st["layer"]):
            cp.start()
        refs[-1][...] = jnp.zeros_like(refs[-1])

    arrays = list(st["srcs"]) + list(st["lands"])
    sems = pltpu.SemaphoreType.DMA((7 * n,))
    out = pl.pallas_call(
        body, name=name,
        out_shape=(sems, sems, *[pltpu.HBM(v.shape, v.dtype) for v in arrays], SDS((8, LANE), F32)),
        in_specs=[HBM_SPEC] * (2 * n) + [SEM_SPEC, SEM_SPEC] + [ANY] * (1 + len(also)),
        out_specs=(SEM_SPEC, SEM_SPEC, *[HBM_SPEC] * (2 * n), pl.BlockSpec(memory_space=pltpu.VMEM)),
        input_output_aliases={i: 2 + i for i in range(2 * n)},
        compiler_params=pltpu.CompilerParams(has_side_effects=EFFECT),
    )(*arrays, *st["sems"], after, *also)
    return dict(sems=out[:2], srcs=out[2:2 + n], lands=out[2 + n:2 + 2 * n], token=out[-1], mode="pass",
                layer=st["layer"])


def _exchange_wait_all(name, sts, lands, ids, after):
    counts = [len(st["srcs"]) for st in sts]
    ns, nl = sum(counts), len(lands)

    def body(*refs):
        at = 0
        for e, st in enumerate(sts):
            own_lands = [refs[ns + i] for i in ids[e]]
            send_sems, recv_sems = refs[ns + nl + 2 * e], refs[ns + nl + 2 * e + 1]
            for cp in _exchange_copies(refs[at:at + counts[e]], own_lands, send_sems, recv_sems, st["mode"],
                                       st["layer"]):
                cp.wait_send()
                cp.wait_recv()
            at += counts[e]

    arrays = [s for st in sts for s in st["srcs"]] + list(lands)
    out = pl.pallas_call(
        body, name=name,
        out_shape=tuple(pltpu.HBM(v.shape, v.dtype) for v in arrays),
        in_specs=[HBM_SPEC] * len(arrays) + [SEM_SPEC] * (2 * len(sts)) + [ANY],
        out_specs=tuple([HBM_SPEC] * len(arrays)),
        input_output_aliases={i: i for i in range(len(arrays))},
        compiler_params=pltpu.CompilerParams(has_side_effects=EFFECT),
    )(*arrays, *[s for st in sts for s in st["sems"]], after)
    at = 0
    for e, st in enumerate(sts):
        st["srcs"] = out[at:at + counts[e]]
        at += counts[e]
    return list(out[ns:])


_IN_PIECES = ([(1024, 3072)]
              + [r for t in range(4) for r in ((LANE * t, LANE * (t + 1)), (512 + LANE * t, 512 + LANE * (t + 1)))]
              + [(4096, 5632), (3072, 4096), (5632, 5648)])


def _permute_in(w):
    pad = jnp.zeros(w.shape[:-1] + (N_PAD - N_IN,), w.dtype)
    return jnp.concatenate([w[..., a:b] for a, b in _IN_PIECES] + [pad], axis=-1)


def _unpermute_in(g):
    ax = [g[..., OFF_LRU + 2 * LANE * t:OFF_LRU + 2 * LANE * t + LANE] for t in range(4)]
    ag = [g[..., OFF_LRU + 2 * LANE * t + LANE:OFF_LRU + 2 * LANE * (t + 1)] for t in range(4)]
    return jnp.concatenate(ax + ag + [g[..., 0:2048], g[..., OFF_Z:OFF_Z + SSD_W], g[..., OFF_XBC:OFF_XBC + SSD_CONV],
                                      g[..., OFF_Z + SSD_W:OFF_Z + SSD_W + SSD_HEADS]], axis=-1)


SHARD_COLS = N_IN // N_DEV


def _in_segments():
    segs, pos = [], 0
    for a, b in _IN_PIECES:
        for i in range(N_DEV):
            lo, hi = max(a, SHARD_COLS * i), min(b, SHARD_COLS * (i + 1))
            if lo < hi:
                segs.append((i, lo - SHARD_COLS * i, hi - lo, pos + lo - a))
        pos += b - a
    return segs


RELAYOUT_ROWS = 512


def _relayout_in(land, own):
    def body(land_ref, own_ref, out_ref):
        me = 4 * lax.axis_index("x") + 2 * lax.axis_index("y") + lax.axis_index("c")
        out_ref[:, N_IN:N_PAD] = jnp.zeros((RELAYOUT_ROWS, N_PAD - N_IN), BF16)
        for i, j, wd, p in _in_segments():
            out_ref[:, p:p + wd] = jnp.where(me == i, own_ref[:, j:j + wd], land_ref[i, :, j:j + wd])

    return pl.pallas_call(
        body, name="relayout_in", grid=(D_MODEL // RELAYOUT_ROWS,),
        in_specs=[pl.BlockSpec((N_DEV, RELAYOUT_ROWS, SHARD_COLS), lambda r: (0, r, 0)),
                  pl.BlockSpec((RELAYOUT_ROWS, SHARD_COLS), lambda r: (r, 0))],
        out_specs=pl.BlockSpec((RELAYOUT_ROWS, N_PAD), lambda r: (r, 0)),
        out_shape=SDS((D_MODEL, N_PAD), BF16),
        compiler_params=_cp(("parallel",)),
    )(land, own)


def _relayout_grad(g):
    def body(g_ref, out_ref):
        for i, j, wd, p in _in_segments():
            out_ref[i, :, j:j + wd] = g_ref[:, p:p + wd].astype(BF16)

    return pl.pallas_call(
        body, name="relayout_grad", grid=(D_MODEL // RELAYOUT_ROWS,),
        in_specs=[pl.BlockSpec((RELAYOUT_ROWS, N_PAD), lambda r: (r, 0))],
        out_specs=pl.BlockSpec((N_DEV, RELAYOUT_ROWS, SHARD_COLS), lambda r: (0, r, 0)),
        out_shape=SDS((N_DEV, D_MODEL, SHARD_COLS), BF16),
        compiler_params=_cp(("parallel",)),
    )(g)


def _block_diag(w):
    w4 = w.reshape(DEPTH, 4, 2, 64, 64)
    z = jnp.zeros((DEPTH, 4, 64, 64), w.dtype)
    top = jnp.concatenate([w4[:, :, 0], z], axis=-1)
    bot = jnp.concatenate([z, w4[:, :, 1]], axis=-1)
    return jnp.concatenate([top, bot], axis=2).astype(BF16)


def _diag_blocks(g):
    return jnp.stack([g[:, :, :64, :64], g[:, :, 64:, 64:]], axis=2).reshape(DEPTH, 8, 64, 64)


def _pad_lanes(v):
    return jnp.pad(v, ((0, 0), (0, LANE - v.shape[1])))


def _lower_bounds(logits):
    p = jax.nn.softmax(logits, axis=0)
    return p, jnp.cumsum(p, axis=0) - p[0]


def _lower_bounds_bwd(p, dlb):
    dp = jnp.cumsum(dlb[::-1], axis=0)[::-1]
    dp = dp.at[0].add(-jnp.sum(dlb, axis=0))
    return p * (dp - jnp.sum(dp * p, axis=0, keepdims=True))


SMALL = ["norm_w", "b_ada", "lru_conv_b", "lru_wa", "lru_ba", "lru_wx", "lru_bx", "lru_lambda", "hg_lb_logits",
         "hg_norm_w", "ssd_conv_b", "ssd_dt_bias", "ssd_a_log", "ssd_d", "ssd_norm_w", "final_norm_w"]
WEIGHTS = ["norm_w", "w_ada", "b_ada", "w_in", "lru_conv_w", "lru_conv_b", "lru_wa", "lru_ba", "lru_wx", "lru_bx",
           "lru_lambda", "hg_lb_logits", "hg_norm_w", "ssd_conv_w", "ssd_conv_b", "ssd_dt_bias", "ssd_a_log", "ssd_d",
           "ssd_norm_w", "w_out", "final_norm_w"]
INPUTS = ["x", "c"] + WEIGHTS + ["loss_target"] + ["m_" + n for n in WEIGHTS] + ["v_" + n for n in WEIGHTS]
SMALL_ROW = 1024


def _small_rows(like):
    out, off = {}, 0
    for n in SMALL:
        rows = -(-int(np.prod(like[n].shape)) // (8 * SMALL_ROW)) * 8
        out[n] = (off, rows)
        off += rows
    return out, off


def _flatten_small(d, prefix="", last=0.0):
    table, _ = _small_rows({n: d[prefix + n] for n in SMALL})
    pieces = []
    for n in SMALL:
        flat = d[prefix + n].reshape(-1)
        pieces.append(jnp.pad(flat, (0, table[n][1] * SMALL_ROW - flat.shape[0])).reshape(-1, SMALL_ROW))
    return jnp.concatenate(pieces + [jnp.full((8, SMALL_ROW), last, F32)], axis=0)


def _split_small(packed, like):
    table, _ = _small_rows(like)
    out = {}
    for n in SMALL:
        off, rows = table[n]
        size = int(np.prod(like[n].shape))
        out[n] = packed[off:off + rows].reshape(-1)[:size].reshape(like[n].shape)
    return out


def _local_step(x, mod, target, w, fetch, emit):
    S = x.shape[0]
    mall = _bfc(_hg_consts())
    mall_t = _bfc(_hg_consts().T)
    consts = _ssd_consts()
    p_lb, lbs = _lower_bounds(w["hg_lb_logits"])
    no_tok = jnp.zeros((8, LANE), F32)
    wa, wx = _block_diag(w["lru_wa"]), _block_diag(w["lru_wx"])
    ba, bx = w["lru_ba"].reshape(DEPTH, 1, LRU_W), w["lru_bx"].reshape(DEPTH, 1, LRU_W)
    lru_cb, lam, ssd_cb = w["lru_conv_b"][:, None], w["lru_lambda"][:, None], w["ssd_conv_b"][:, None]
    bias, alog = _pad_lanes(w["ssd_dt_bias"]), _pad_lanes(w["ssd_a_log"])
    dskip = jnp.repeat(w["ssd_d"], SSD_P, axis=1)
    saved = []
    for l in range(DEPTH):
        w_in_l, w_out_l, token = fetch(l, x)
        shift, scale, gate = (_Row(mod, l, D_MODEL, k) for k in range(3))
        nw = _Row(w["norm_w"], l)
        u, h = _inproj_fwd(x, nw, scale, shift, w_in_l, no_tok if token is None else token)
        ycat = lax.empty((S, D_INNER), BF16)
        lru_args = (l, u, w["lru_conv_w"], lru_cb, wa, ba, wx, bx, lam)
        ycat, h_lru = _lru_fwd(*lru_args, ycat)
        hg_args = (u, _Row(lbs, l), _Row(w["hg_norm_w"], l), mall)
        ycat, o_b, hg_st = _hg_fwd(*hg_args, ycat)
        xbc = _ssdconv_fwd(l, u, w["ssd_conv_w"], ssd_cb)
        ssd_args = (u, xbc, _Row(bias, l), _Row(alog, l), _Row(dskip, l), _Row(w["ssd_norm_w"], l), consts)
        ycat, y_ssd, ssd_st = _ssd_fwd(*ssd_args, ycat)
        token = fetch(l, y_ssd, late=True)
        if callable(w_out_l):
            w_out_l = w_out_l()
        x_new, y = _outproj_fwd(ycat, w_out_l, x, gate, no_tok if token is None else token)
        saved.append((x, u, h, ycat, nw, scale, gate, w_in_l, w_out_l, lru_args, h_lru, hg_args, o_b, hg_st, ssd_args,
                      y_ssd, ssd_st, y))
        x = x_new
    dx, red = _loss_head(x, w["final_norm_w"][None, :], target)
    loss = red[1, 0]
    reds = {k: [None] * DEPTH for k in ("in", "gate", "lru", "wa", "wx", "hg", "conv", "ssd")}
    for l in reversed(range(DEPTH)):
        (x, u, h, ycat, nw, scale, gate, w_in_l, w_out_l, lru_args, h_lru, hg_args, o_b, hg_st, ssd_args, y_ssd, ssd_st,
         y) = saved[l]
        dycat, g_out, reds["gate"][l] = _outproj_bwd(dx, y, gate, ycat, w_out_l)
        token = emit(l, "w_out", g_out)
        du = lax.empty((S, N_PAD), BF16)
        du, dxbc, reds["ssd"][l] = _ssd_bwd(*ssd_args, y_ssd, ssd_st, dycat, du, no_tok if token is None else token)
        du, reds["conv"][l] = _ssdconv_bwd(l, u, w["ssd_conv_w"], ssd_cb, dxbc, du)
        du, reds["hg"][l] = _hg_bwd(*hg_args, mall_t, o_b, hg_st, dycat, du)
        du, reds["lru"][l], reds["wa"][l], reds["wx"][l] = _lru_bwd(*lru_args, h_lru, dycat, du)
        token = emit(l, "w_in", functools.partial(_inproj_bwd_w, h, du))
        dx, reds["in"][l] = _inproj_bwd_x(du, w_in_l, x, nw, scale, dx, no_tok if token is None else token)
    r = {k: jnp.stack(v) for k, v in reds.items()}
    g = {n: None for n in WEIGHTS}
    g["final_norm_w"] = red[0]
    g["norm_w"] = r["in"][:, 2]
    dmod = jnp.concatenate([r["in"][:, 0], r["in"][:, 1], r["gate"][:, 0]], axis=1)
    g["lru_conv_w"], g["lru_conv_b"] = r["lru"][:, 0:4], r["lru"][:, 4]
    g["lru_ba"], g["lru_bx"] = r["lru"][:, 5].reshape(DEPTH, 8, 64), r["lru"][:, 6].reshape(DEPTH, 8, 64)
    g["lru_lambda"] = r["lru"][:, 7]
    g["lru_wa"], g["lru_wx"] = _diag_blocks(r["wa"]), _diag_blocks(r["wx"])
    g["hg_norm_w"] = r["hg"][:, 0]
    g["hg_lb_logits"] = _lower_bounds_bwd(p_lb, r["hg"][:, 1])
    g["ssd_conv_w"], g["ssd_conv_b"] = r["conv"][:, 0:4], r["conv"][:, 4]
    g["ssd_norm_w"] = r["ssd"][:, 0]
    g["ssd_d"] = r["ssd"][:, 1].reshape(DEPTH, SSD_HEADS, SSD_P).sum(-1)
    g["ssd_dt_bias"] = r["ssd"][:, 2, :SSD_HEADS]
    g["ssd_a_log"] = -r["ssd"][:, 3, :SSD_HEADS] * jnp.exp(w["ssd_a_log"])
    return loss, dx, dmod, g


def kernel(x, c, norm_w, w_ada, b_ada, w_in, lru_conv_w, lru_conv_b, lru_wa, lru_ba, lru_wx, lru_bx, lru_lambda, hg_lb_logits, hg_norm_w, ssd_conv_w, ssd_conv_b, ssd_dt_bias, ssd_a_log, ssd_d, ssd_norm_w, w_out, final_norm_w, loss_target, m_norm_w, m_w_ada, m_b_ada, m_w_in, m_lru_conv_w, m_lru_conv_b, m_lru_wa, m_lru_ba, m_lru_wx, m_lru_bx, m_lru_lambda, m_hg_lb_logits, m_hg_norm_w, m_ssd_conv_w, m_ssd_conv_b, m_ssd_dt_bias, m_ssd_a_log, m_ssd_d, m_ssd_norm_w, m_w_out, m_final_norm_w, v_norm_w, v_w_ada, v_b_ada, v_w_in, v_lru_conv_w, v_lru_conv_b, v_lru_wa, v_lru_ba, v_lru_wx, v_lru_bx, v_lru_lambda, v_hg_lb_logits, v_hg_norm_w, v_ssd_conv_w, v_ssd_conv_b, v_ssd_dt_bias, v_ssd_a_log, v_ssd_d, v_ssd_norm_w, v_w_out, v_final_norm_w):
    return _step(x, c, norm_w, w_ada, b_ada, w_in, lru_conv_w, lru_conv_b, lru_wa, lru_ba, lru_wx, lru_bx, lru_lambda, hg_lb_logits, hg_norm_w, ssd_conv_w, ssd_conv_b, ssd_dt_bias, ssd_a_log, ssd_d, ssd_norm_w, w_out, final_norm_w, loss_target, m_norm_w, m_w_ada, m_b_ada, m_w_in, m_lru_conv_w, m_lru_conv_b, m_lru_wa, m_lru_ba, m_lru_wx, m_lru_bx, m_lru_lambda, m_hg_lb_logits, m_hg_norm_w, m_ssd_conv_w, m_ssd_conv_b, m_ssd_dt_bias, m_ssd_a_log, m_ssd_d, m_ssd_norm_w, m_w_out, m_final_norm_w, v_norm_w, v_w_ada, v_b_ada, v_w_in, v_lru_conv_w, v_lru_conv_b, v_lru_wa, v_lru_ba, v_lru_wx, v_lru_bx, v_lru_lambda, v_hg_lb_logits, v_hg_norm_w, v_ssd_conv_w, v_ssd_conv_b, v_ssd_dt_bias, v_ssd_a_log, v_ssd_d, v_ssd_norm_w, v_w_out, v_final_norm_w)


def _step(*args):
    a = dict(zip(INPUTS, args, strict=True))
    me = 4 * lax.axis_index("x") + 2 * lax.axis_index("y") + lax.axis_index("c")
    x, target = a["x"][0], a["loss_target"][0]

    c_all = _all_gather(a["c"], "gather_c")[:, 0, :]
    b_cols = lax.dynamic_slice_in_dim(a["b_ada"], me * ADA_COLS, ADA_COLS, axis=1)[:, None, :]
    mod_parts = _all_gather(_ada_fwd(c_all, a["w_ada"], b_cols), "gather_mod")
    mod = lax.dynamic_index_in_dim(mod_parts, me, axis=2, keepdims=False)
    mod = mod.transpose(1, 0, 2).reshape(DEPTH, 3 * D_MODEL)

    w = {n: a[n] for n in SMALL}

    w_in_b = [a["w_in"][l].astype(BF16) for l in range(DEPTH)]
    w_out_b = a["w_out"].astype(BF16)
    conv_own = jnp.concatenate([a["lru_conv_w"], a["ssd_conv_w"]], axis=-1)
    cols, rows_out = N_IN // N_DEV, D_INNER // N_DEV

    def gather_start(l, after):
        srcs = [w_in_b[l], conv_own if l == 0 else w_out_b[l]]
        lands = [lax.empty((N_DEV,) + s.shape, s.dtype) for s in srcs]
        return _exchange_start(f"gather_start_{l}", srcs, lands, "chip", after=after)

    def gather_pass(name, st, after, also=()):
        return _exchange_wait(name + "_passed", _exchange_relay(name + "_pass", st, after, also), after)

    gathers = {0: gather_start(0, mod)}
    passing = {}
    out0 = {"st": _exchange_start("gather_start_0_out", [w_out_b[0]], [lax.empty((N_DEV,) + w_out_b[0].shape, BF16)],
                                  "chip", after=gathers[0]["token"])}

    def fetch(l, x_l, late=False):
        if late:
            if l + 1 == DEPTH:
                return None
            if l == 0:
                out0["st"] = _exchange_relay("gather_0_out_pass", out0["st"], x_l)
                x_l = out0["st"]["token"]
            passing[l + 1] = _exchange_relay(f"gather_{l + 1}_pass", gathers[l + 1], x_l)
            if l == 0:
                landed = _exchange_wait("gather_0_out_passed", out0["st"], passing[1]["token"])
                out0["w"] = lax.dynamic_update_index_in_dim(landed[0], w_out_b[0], me, 0).reshape(D_INNER, D_MODEL)
            return passing[l + 1]["token"]
        if l == 0:
            landed = gather_pass("gather_0", gathers[0], x_l,
                                 also=(a["w_in"], a["m_w_in"], a["v_w_in"], out0["st"]["token"]))
            conv =lax.dynamic_update_index_in_dim(landed[1], conv_own, me, 0).transpose(1, 2, 0, 3)
            w["lru_conv_w"] = conv[..., :64].reshape(DEPTH, 4, LRU_W)
            w["ssd_conv_w"] = conv[..., 64:].reshape(DEPTH, 4, SSD_CONV)
            after, w_out_l = landed[0], lambda: out0["w"]
        else:
            landed = _exchange_wait(f"gather_{l}_passed", passing[l], x_l)
            after = lax.dynamic_update_index_in_dim(landed[1], w_out_b[l], me, 0)
            w_out_l = after.reshape(D_INNER, D_MODEL)
        token = None
        if l + 1 < DEPTH:
            gathers[l + 1] = gather_start(l + 1, after)
            token = gathers[l + 1]["token"]
        return _relayout_in(landed[0], w_in_b[l]), w_out_l, token

    PROJ = ("w_in", "w_out")
    scatters = {}
    lands = [lax.empty((N_DEV, DEPTH, D_MODEL, cols), BF16), lax.empty((N_DEV, DEPTH, rows_out, D_MODEL), BF16)]
    own = [None] * DEPTH

    deferred, g_out = {}, {}

    def emit(l, name, grad, after=None):
        if name == "w_out" and l > 0:
            g_out[l] = grad
            return None
        if name == "w_in" and l == 0 and after is None:
            deferred["w_in"] = grad
            return None
        if name == "w_in":
            grad = grad(jnp.zeros((8, LANE), F32) if after is None else after)
        if l == 0:
            k = PROJ.index(name)
            src = _relayout_grad(grad) if name == "w_in" else grad.reshape(N_DEV, rows_out, D_MODEL)
            st = _exchange_start(f"scatter_start_0_{name}", [src], [lands[k]], "scatter", layer=0, after=after)
            scatters[name] = st
            lands[k] = st["lands"][0]
            return st["token"]
        srcs = [_relayout_grad(grad), g_out[l].reshape(N_DEV, rows_out, D_MODEL)]
        st = _exchange_start(f"scatter_start_{l}", srcs, lands, "scatter", layer=l, after=after)
        scatters[l] = st
        lands[:] = st["lands"]
        return st["token"]

    loss_own, dx, dmod, g = _local_step(x, mod, target, w, fetch, emit)

    def sharded(name, parts, own=None, **kw):
        return _adamw(parts, a[name], a["m_" + name], a["v_" + name], "adamw_" + name + kw.pop("tag", ""), own=own, **kw)

    g["b_ada"] = dmod
    small_own = _flatten_small(g, last=loss_own)
    small_st = _exchange_start("gather_small", [small_own], [lax.empty((N_DEV,) + small_own.shape, F32)], "chip",
                               after=dx)
    big = {}
    after = emit(0, "w_in", deferred["w_in"], after=small_st["token"]) + dx[0:8, 0:LANE]

    def own_slices(st):
        return [lax.dynamic_index_in_dim(s, me, 0, keepdims=False) for s in st["srcs"]]

    upper_sts = [scatters[l] for l in reversed(range(1, DEPTH))] + [scatters["w_out"]]
    lands[:] = _exchange_wait_all("scatter_wait_upper", upper_sts, lands, [(0, 1)] * (DEPTH - 1) + [(1,)], after)
    for l in range(1, DEPTH):
        own[l] = own_slices(scatters[l])
    own[0] = [None, own_slices(scatters["w_out"])[0]]
    big["w_out"] = sharded("w_out", lands[1], jnp.stack([own[l][1] for l in range(DEPTH)]))
    upper = sharded("w_in", lands[0], jnp.stack([own[l][0] for l in range(1, DEPTH)]), layers=(1, DEPTH), tag="_upper")
    after = upper[1][0, 0:8, 0:LANE] + big["w_out"][1][0, 0:8, 0:LANE]
    small = gather_pass("gather_small", small_st, after)[0]
    outs = _adamw(small[:, None], *[_flatten_small(a, p)[None] for p in ("", "m_", "v_")], "adamw_small",
                  own=small_own[None])
    res = [_split_small(o[0], a) for o in outs]
    losses = lax.dynamic_update_index_in_dim(small[:, -1, 0], loss_own, me, 0)
    loss = jnp.sum(losses)

    off = _small_rows(a)[0]["b_ada"][0]
    dmod_all = lax.dynamic_update_index_in_dim(small[:, off:off + DEPTH * 3 * D_MODEL // SMALL_ROW],
                                               dmod.reshape(-1, SMALL_ROW), me, 0)
    dmod_all = dmod_all.reshape(N_DEV, DEPTH, 3 * D_MODEL).transpose(1, 0, 2)
    dmod_cols = lax.dynamic_slice_in_dim(dmod_all, me * ADA_COLS, ADA_COLS, axis=2)
    dmod_pad = jnp.pad(dmod_cols, ((0, 0), (0, LANE - N_DEV), (0, 0)))
    ct_pad = jnp.pad(c_all.T, ((0, 0), (0, LANE - N_DEV)))
    big["w_ada"] = sharded("w_ada", _ada_bwd(ct_pad, dmod_pad)[None])
    g_conv = jnp.concatenate([g["lru_conv_w"].reshape(DEPTH, 4, N_DEV, 64), g["ssd_conv_w"].reshape(DEPTH, 4, N_DEV, 192)],
                             axis=-1).transpose(2, 0, 1, 3)
    conv_parts = _all_to_all(g_conv, "scatter_conv")
    big["lru_conv_w"] = sharded("lru_conv_w", conv_parts[..., :64])
    big["ssd_conv_w"] = sharded("ssd_conv_w", conv_parts[..., 64:])

    after = outs[1] + big["w_ada"][1][0, 0:1, 0:1]
    scatters["w_in"]["lands"] = [lands[0]]
    lands[0] = _exchange_wait("scatter_wait_0_w_in", scatters["w_in"], after)[0]
    big["w_in"] = sharded("w_in", lands[0], own_slices(scatters["w_in"])[0][None], layers=(0, 1), prev=upper)

    out = [loss, dx[None]]
    for k in range(4):
        out += [big[n][k] if n in big else res[k][n] for n in WEIGHTS]
    return tuple(out)
```

```python
import functools

import numpy as np
import jax
import jax.numpy as jnp
from jax import lax
from jax.experimental import pallas as pl
from jax.experimental.pallas import tpu as pltpu

F32 = jnp.float32
BF16 = jnp.bfloat16
SDS = jax.ShapeDtypeStruct

N_DEV = 8
DEPTH = 4
D_MODEL = 1024
D_INNER = 2048
EPS = 1e-6
LRU_W = 512
LRU_C = 8.0
HG_W = 512
HG_CHUNK = 64
HG_HEADS = 4
SSD_W = 1024
SSD_HEADS = 16
SSD_P = 64
SSD_N = 128
SSD_CHUNK = 128
SSD_CONV = 1536
N_IN = 5648
N_PAD = 5760
OFF_HG = 0
OFF_LRU = 2048
OFF_XBC = 3072
OFF_Z = 4608
LANE = 128
VMEM_LIMIT = 56 * 1024 * 1024
NEG = -1e30

ADAM_LR = 0.001
ADAM_B1 = 0.9
ADAM_B2 = 0.999
ADAM_EPS = 1e-08
ADAM_WD = 0.01
ADAM_STEP = 10


def _cp(sem=None):
    return pltpu.CompilerParams(dimension_semantics=sem, vmem_limit_bytes=VMEM_LIMIT)


def _dg(a, b, ca, cb):
    return lax.dot_general(a, b, (((ca,), (cb,)), ((), ())), preferred_element_type=F32)


def _mm(a, b):
    return _dg(a, b, 1, 0)


def _mm_nt(a, b):
    return _dg(a, b, 1, 1)


def _mm_tn(a, b):
    return _dg(a, b, 0, 0)


def _bf(x):
    return x.astype(BF16)


def _f(x):
    return x.astype(F32)


def _split3(x):
    hi = x.astype(BF16)
    r = x - hi.astype(F32)
    mid = r.astype(BF16)
    lo = (r - mid.astype(F32)).astype(BF16)
    return hi, mid, lo


def _sel_r(x, m):
    hi, mid, lo = _split3(x)
    return _mm(hi, m) + _mm(mid, m) + _mm(lo, m)


def _sel_l(m, x):
    hi, mid, lo = _split3(x)
    return _mm(m, hi) + _mm(m, mid) + _mm(m, lo)


def _sel_l2(m, x):
    hi = x.astype(BF16)
    lo = (x - hi.astype(F32)).astype(BF16)
    return _mm(m, hi) + _mm(m, lo)


def _sel_tn(x, m):
    hi, mid, lo = _split3(x)
    return _mm_tn(hi, m) + _mm_tn(mid, m) + _mm_tn(lo, m)


def _sigmoid(x):
    return 1.0 / (1.0 + jnp.exp(-x))


def _silu(x):
    return x * _sigmoid(x)


def _dsilu(x):
    s = _sigmoid(x)
    return s * (1.0 + x * (1.0 - s))


def _softplus(x):
    return jnp.maximum(x, 0.0) + jnp.log(1.0 + jnp.exp(-jnp.abs(x)))


def _expm1(z):
    series = z * (1.0 + z * (1.0 / 2) * (1.0 + z * (1.0 / 3) * (1.0 + z * (1.0 / 4) * (
        1.0 + z * (1.0 / 5) * (1.0 + z * (1.0 / 6) * (1.0 + z * (1.0 / 7)))))))
    return jnp.where(jnp.abs(z) < 0.3, series, jnp.exp(z) - 1.0)


def _iota(shape, dim):
    return lax.broadcasted_iota(jnp.int32, shape, dim)


def _last_row(x, rows):
    return jnp.sum(jnp.where(rows == x.shape[0] - 1, x, 0.0), axis=0, keepdims=True)


def _shift_down(x, d, rows, fill=0.0):
    return jnp.where(rows >= d, pltpu.roll(x, d, 0), fill)


def _shift_up(x, d, rows, fill=0.0):
    n = x.shape[0]
    return jnp.where(rows < n - d, pltpu.roll(x, n - d, 0), fill)


def _conv_fwd(x, cw_ref, cb_ref, rows):
    out = cb_ref[...] + cw_ref[pl.ds(3, 1), :] * x
    for k in range(3):
        out = out + cw_ref[pl.ds(k, 1), :] * _shift_down(x, 3 - k, rows)
    return out


def _conv_bwd(x, dco, cw_ref, rows):
    dx = cw_ref[pl.ds(3, 1), :] * dco
    dws = []
    for k in range(3):
        dx = dx + cw_ref[pl.ds(k, 1), :] * _shift_up(dco, 3 - k, rows)
        dws.append(jnp.sum(dco * _shift_down(x, 3 - k, rows), axis=0, keepdims=True))
    dws.append(jnp.sum(dco * x, axis=0, keepdims=True))
    return dx, dws, jnp.sum(dco, axis=0, keepdims=True)


def _vec(n):
    return pl.BlockSpec((1, n), lambda *_: (0, 0))


class _Row:
    def __init__(self, arr, l, n=None, c=0):
        self.arr, self.l, self.n, self.c = arr[:, None, :], l, n or arr.shape[1], c


def _spec(v):
    if isinstance(v, _Row):
        return pl.BlockSpec((None, 1, v.n), lambda *_: (v.l, 0, v.c))
    return _vec(v.shape[1])


def _arr(v):
    return v.arr if isinstance(v, _Row) else v


def _full(shape):
    nd = len(shape)
    return pl.BlockSpec(shape, lambda *_: (0,) * nd)


def _inproj_fwd(x, nw, scale, shift, w, tok):
    S = x.shape[0]
    tm = min(256, S)

    def body(x_ref, nw_ref, sc_ref, sh_ref, w_ref, tok_ref, u_ref, h_ref):
        del tok_ref
        xv = x_ref[...]
        inv = lax.rsqrt(jnp.mean(xv * xv, axis=-1, keepdims=True) + EPS)
        h = ((xv * inv) * nw_ref[...] * (1.0 + sc_ref[...]) + sh_ref[...]).astype(BF16)
        h_ref[...] = h
        u_ref[...] = _mm(h, w_ref[...])

    return pl.pallas_call(
        body, name="inproj_fwd", grid=(S // tm,),
        in_specs=[pl.BlockSpec((tm, D_MODEL), lambda i: (i, 0)), _spec(nw), _spec(scale), _spec(shift),
                  _full((D_MODEL, N_PAD)), pl.BlockSpec(memory_space=pl.ANY)],
        out_specs=[pl.BlockSpec((tm, N_PAD), lambda i: (i, 0)), pl.BlockSpec((tm, D_MODEL), lambda i: (i, 0))],
        out_shape=[SDS((S, N_PAD), F32), SDS((S, D_MODEL), BF16)],
        compiler_params=_cp(("parallel",)),
    )(x, _arr(nw), _arr(scale), _arr(shift), w, tok)


def _inproj_bwd_x(du, w, x, nw, scale, dxn, tok):
    S = x.shape[0]
    tm = min(256, S)

    def body(du_ref, w_ref, x_ref, nw_ref, sc_ref, dxn_ref, tok_ref, dx_ref, red_ref):
        del tok_ref

        @pl.when(pl.program_id(0) == 0)
        def _():
            red_ref[...] = jnp.zeros_like(red_ref)

        dh = _mm_nt(du_ref[...], w_ref[...])
        xv = x_ref[...]
        inv = lax.rsqrt(jnp.mean(xv * xv, axis=-1, keepdims=True) + EPS)
        xhat = xv * inv
        nwv = nw_ref[...]
        g1 = 1.0 + sc_ref[...]
        dxhat = dh * nwv * g1
        dx = inv * (dxhat - xhat * jnp.mean(dxhat * xhat, axis=-1, keepdims=True))
        dx_ref[...] = dxn_ref[...] + dx
        red_ref[0:1, :] += jnp.sum(dh, axis=0, keepdims=True)
        red_ref[1:2, :] += jnp.sum(dh * xhat * nwv, axis=0, keepdims=True)
        red_ref[2:3, :] += jnp.sum(dh * xhat * g1, axis=0, keepdims=True)

    row = pl.BlockSpec((tm, D_MODEL), lambda i: (i, 0))
    return pl.pallas_call(
        body, name="inproj_bwd_x", grid=(S // tm,),
        in_specs=[pl.BlockSpec((tm, N_PAD), lambda i: (i, 0)), _full((D_MODEL, N_PAD)), row, _spec(nw),
                  _spec(scale), row, pl.BlockSpec(memory_space=pl.ANY)],
        out_specs=[row, _full((8, D_MODEL))],
        out_shape=[SDS((S, D_MODEL), F32), SDS((8, D_MODEL), F32)],
        compiler_params=_cp(("arbitrary",)),
    )(du, w, x, _arr(nw), _arr(scale), dxn, tok)


def _inproj_bwd_w(h, du, tok):
    S = h.shape[0]
    tn = 640

    def body(h_ref, du_ref, tok_ref, gw_ref):
        del tok_ref
        gw_ref[...] = _mm_tn(h_ref[...], _bf(du_ref[...]))

    return pl.pallas_call(
        body, name="inproj_bwd_w", grid=(N_PAD // tn,),
        in_specs=[_full((S, D_MODEL)), pl.BlockSpec((S, tn), lambda j: (0, j)), pl.BlockSpec(memory_space=pl.ANY)],
        out_specs=pl.BlockSpec((D_MODEL, tn), lambda j: (0, j)),
        out_shape=SDS((D_MODEL, N_PAD), F32),
        compiler_params=_cp(("parallel",)),
    )(h, du, tok)


def _scan_block(a, b, rows):
    d = 1
    while d < a.shape[0]:
        a_s = _shift_down(a, d, rows, 1.0)
        b_s = _shift_down(b, d, rows, 0.0)
        b = a * b_s + b
        a = a * a_s
        d *= 2
    return a, b


def _rscan_block(c, g, rows):
    d = 1
    while d < c.shape[0]:
        c_s = _shift_up(c, d, rows, 1.0)
        g_s = _shift_up(g, d, rows, 0.0)
        g = g + c * g_s
        c = c * c_s
        d *= 2
    return c, g


LRU_BLOCK = 128


def _lru_gates(xa, wa_ref, ba_ref, wx_ref, bx_ref, lam_ref):
    sp = _softplus(-lam_ref[...])
    xb = _bf(xa)
    r = _sigmoid(_mm(xb, wa_ref[...]) + ba_ref[...])
    ig = _sigmoid(_mm(xb, wx_ref[...]) + bx_ref[...])
    la = -LRU_C * r * sp
    a = jnp.exp(la)
    mult = jnp.sqrt(-_expm1(2.0 * la))
    return sp, r, ig, la, a, mult


def _lru_specs(S, l):
    t128 = pl.BlockSpec((None, 1, LANE), lambda t: (l, 0, t))
    gate = pl.BlockSpec((None, None, LANE, LANE), lambda t: (l, t, 0, 0))
    return [pl.BlockSpec((S, 2 * LANE), lambda t: (0, OFF_LRU // (2 * LANE) + t)),
            pl.BlockSpec((None, 4, LANE), lambda t: (l, 0, t)), t128, gate, t128, gate, t128, t128]


def _lru_fwd(l, u, cw, cb, wa, ba, wx, bx, lam, ycat):
    S = u.shape[0]
    tb = min(LRU_BLOCK, S)

    def body(u_ref, cw_ref, cb_ref, wa_ref, ba_ref, wx_ref, bx_ref, lam_ref, ycat_in, ycat_ref, h_ref, a_scr, b_scr):
        del ycat_in
        rows = _iota((S, LANE), 0)
        xa = _conv_fwd(_f(u_ref[:, 0:LANE]), cw_ref, cb_ref, rows)
        _, _, ig, _, a, mult = _lru_gates(xa, wa_ref, ba_ref, wx_ref, bx_ref, lam_ref)
        a_scr[...] = a
        b_scr[...] = mult * (ig * xa)
        rows_b = _iota((tb, LANE), 0)

        def blk(j, hprev):
            sl = pl.ds(pl.multiple_of(j * tb, tb), tb)
            acum, hloc = _scan_block(a_scr[sl, :], b_scr[sl, :], rows_b)
            hf = hloc + acum * hprev
            h_ref[sl, :] = hf
            return _last_row(hf, rows_b)

        lax.fori_loop(0, S // tb, blk, jnp.zeros((1, LANE), F32))
        ycat_ref[...] = _bf(h_ref[...] * _silu(_f(u_ref[:, LANE:2 * LANE])))

    col = pl.BlockSpec((S, LANE), lambda t: (0, t))
    return pl.pallas_call(
        body, name="lru_fwd", grid=(LRU_W // LANE,),
        in_specs=_lru_specs(S, l) + [pl.BlockSpec(memory_space=pl.ANY)],
        out_specs=[col, col],
        out_shape=[SDS((S, D_INNER), BF16), SDS((S,LRU_W), F32)],
        scratch_shapes=[pltpu.VMEM((S, LANE), F32), pltpu.VMEM((S, LANE), F32)],
        input_output_aliases={8: 0},
        compiler_params=_cp(("parallel",)),
    )(u, cw, cb, wa, ba, wx, bx, lam, ycat)


def _lru_bwd(l, u, cw, cb, wa, ba, wx, bx, lam, h_lru, dycat, du):
    S = u.shape[0]
    tb = min(LRU_BLOCK, S)

    def body(u_ref, cw_ref, cb_ref, wa_ref, ba_ref, wx_ref, bx_ref, lam_ref, h_ref, dy_ref, du_in,
             du_ref, red_ref, gwa_ref, gwx_ref, c_scr, g_scr, l_scr):
        del du_in
        rows = _iota((S, LANE), 0)
        ax = _f(u_ref[:, 0:LANE])
        ag = _f(u_ref[:, LANE:2 * LANE])
        xa = _conv_fwd(ax, cw_ref, cb_ref, rows)
        sp, r, ig, la, a, mult = _lru_gates(xa, wa_ref, ba_ref, wx_ref, bx_ref, lam_ref)
        h = h_ref[...]
        dy = _f(dy_ref[...])
        du_ref[:, LANE:2 * LANE] = _bf(dy * h * _dsilu(ag))
        c_scr[...] = _shift_up(a, 1, rows, 0.0)
        g_scr[...] = dy * _silu(ag)
        rows_b = _iota((tb, LANE), 0)
        nb = S // tb

        def blk(jj, lnext):
            j = nb - 1 - jj
            sl = pl.ds(pl.multiple_of(j * tb, tb), tb)
            ccum, lloc = _rscan_block(c_scr[sl, :], g_scr[sl, :], rows_b)
            lam_t = lloc + ccum * lnext
            l_scr[sl, :] = lam_t
            return jnp.sum(jnp.where(rows_b == 0, lam_t, 0.0), axis=0, keepdims=True)

        lax.fori_loop(0, nb, blk, jnp.zeros((1, LANE), F32))
        db = l_scr[...]
        da = db * _shift_down(h, 1, rows)
        dmult = db * ig * xa
        dig = db * mult * xa
        dxa = db * mult * ig
        dla = da * a - dmult * (a * a) / mult
        dr = -LRU_C * sp * dla
        dsp = jnp.sum(-LRU_C * r * dla, axis=0, keepdims=True)
        dlam = -dsp * _sigmoid(-lam_ref[...])
        dzr = dr * r * (1.0 - r)
        dzi = dig * ig * (1.0 - ig)
        dzr_b, dzi_b, xa_b = _bf(dzr), _bf(dzi), _bf(xa)
        dxa = dxa + _mm_nt(dzr_b, wa_ref[...]) + _mm_nt(dzi_b, wx_ref[...])
        gwa_ref[...] = _mm_tn(xa_b, dzr_b)
        gwx_ref[...] = _mm_tn(xa_b, dzi_b)
        dax, dws, dcb = _conv_bwd(ax, dxa, cw_ref, rows)
        du_ref[:, 0:LANE] = _bf(dax)
        parts = dws + [dcb, jnp.sum(dzr, axis=0, keepdims=True), jnp.sum(dzi, axis=0, keepdims=True), dlam]
        for n, p in enumerate(parts):
            red_ref[pl.ds(n, 1), :] = p

    col = pl.BlockSpec((S, LANE), lambda t: (0, t))
    gw = pl.BlockSpec((None, LANE, LANE), lambda t: (t, 0, 0))
    return pl.pallas_call(
        body, name="lru_bwd", grid=(LRU_W // LANE,),
        in_specs=_lru_specs(S, l) + [col, col, pl.BlockSpec(memory_space=pl.ANY)],
        out_specs=[pl.BlockSpec((S, 2 * LANE), lambda t: (0, OFF_LRU // (2 * LANE) + t)),
                   pl.BlockSpec((8, LANE), lambda t: (0, t)), gw, gw],
        out_shape=[SDS((S, N_PAD), BF16), SDS((8, LRU_W), F32), SDS((4, LANE, LANE), F32), SDS((4, LANE, LANE), F32)],
        scratch_shapes=[pltpu.VMEM((S, LANE), F32)] * 3,
        input_output_aliases={10: 0},
        compiler_params=_cp(("parallel",)),
    )(u, cw, cb, wa, ba, wx, bx, lam, h_lru, dycat, du)


HG_LEVELS = 6


def _hg_consts():
    C = HG_CHUNK
    t = np.arange(C)[:, None]
    r = np.arange(C)[None, :]
    mats = []
    for l in range(HG_LEVELS):
        b = 1 << l
        upper = (t % (2 * b)) >= b
        anchor = (t // (2 * b)) * 2 * b + b - 1
        mats.append((upper & (r > anchor) & (r <= t)) | ((~upper) & (r > t) & (r <= anchor)))
    mats.append(r <= t)
    mats.append(r > t)
    return np.concatenate(mats, 0).astype(np.float32)


def _hg_factors(hf, lb, mall):
    s = _sigmoid(hf)
    f = lb + (1.0 - lb) * s
    lf = jnp.log(f)
    k = (1.0 - lb) * _sigmoid(-hf)
    e = jnp.exp(_sel_l(mall, lf))
    C = HG_CHUNK
    rows = _iota((C, HG_W), 0)
    eq, ek = [], []
    for l in range(HG_LEVELS):
        el = e[l * C:(l + 1) * C]
        eq.append(jnp.where((lax.shift_right_logical(rows, l) & 1) == 1, el, 0.0))
        ek.append(el - eq[l])
    ecum = e[HG_LEVELS * C:(HG_LEVELS + 1) * C]
    erem = e[(HG_LEVELS + 1) * C:(HG_LEVELS + 2) * C]
    return s, f, k, eq, ek, ecum, erem


def _hg_masks():
    C = HG_CHUNK
    ri, ci = _iota((C, C), 0), _iota((C, C), 1)
    rr = _iota((C, LANE), 0)
    gm = [(lax.shift_right_logical(ri, l + 1) == lax.shift_right_logical(ci, l + 1)).astype(F32)
          for l in range(HG_LEVELS)]
    up = [(lax.shift_right_logical(rr, l) & 1) == 1 for l in range(HG_LEVELS)]
    eye = (ri == ci).astype(F32)
    return gm, up, eye, rr


def _hg_scores(qh, kh, eq, ek, sl, gm, up, eye):
    del up
    qs, ks, qb, kb = [], [], [], []
    p = _mm_nt(_bf(qh), _bf(kh)) * eye
    for l in range(HG_LEVELS):
        qs.append(qh * eq[l][:, sl])
        ks.append(kh * ek[l][:, sl])
        qb.append(_bf(qs[l]))
        kb.append(_bf(ks[l]))
        p = p + _mm_nt(qb[l], kb[l]) * gm[l]
    return p, qs, ks, qb, kb


HG_SUB = 4


def _hg_fwd(u, lb, nw, mall, ycat):
    S = u.shape[0]
    C = HG_CHUNK
    n = S // C
    rows = HG_SUB * C

    def body(u_ref, lb_ref, nw_ref, mall_ref, ycat_in, ycat_ref, o_ref, st_ref, st):
        del ycat_in

        @pl.when(pl.program_id(0) == 0)
        def _():
            st[...] = jnp.zeros_like(st)

        gm, up, eye, rr = _hg_masks()
        for sub in range(HG_SUB):
            r = slice(sub * C, (sub + 1) * C)
            q = _silu(_f(u_ref[r, 0:512]))
            v = u_ref[r, 1024:1536]
            _, _, k, eq, ek, ecum, erem = _hg_factors(_f(u_ref[r, 512:1024]), lb_ref[...], mall_ref[...])
            for h in range(HG_HEADS):
                sl = slice(h * LANE, (h + 1) * LANE)
                qh, kh, vh = q[:, sl], k[:, sl], _bf(v[:, sl])
                p = _hg_scores(qh, kh, eq, ek, sl, gm, up, eye)[0]
                sth = st[h]
                st_ref[sub, h] = sth
                o_ref[r, sl] = _mm(_bf(p), vh) + _mm_nt(_bf(qh * ecum[:, sl]), _bf(sth))
                st[h] = sth * _last_row(ecum[:, sl], rr) + _mm_tn(vh, _bf(kh * erem[:, sl]))
            o = o_ref[r, :]
            inv = lax.rsqrt(jnp.mean(o * o, axis=-1, keepdims=True) + EPS)
            ycat_ref[r, :] = _bf((o * inv) * nw_ref[...] * _silu(_f(u_ref[r, 1536:2048])))

    return pl.pallas_call(
        body, name="hg_fwd", grid=(n // HG_SUB,),
        in_specs=[pl.BlockSpec((rows, 2048), lambda i: (i, 0)), _spec(lb), _spec(nw), _full(mall.shape),
                  pl.BlockSpec(memory_space=pl.ANY)],
        out_specs=[pl.BlockSpec((rows, HG_W), lambda i: (i, 1)), pl.BlockSpec((rows, HG_W), lambda i: (i, 0)),
                   pl.BlockSpec((HG_SUB, HG_HEADS, LANE, LANE), lambda i: (i, 0, 0, 0))],
        out_shape=[SDS((S, D_INNER), BF16), SDS((S,HG_W), F32), SDS((n, HG_HEADS, LANE, LANE), F32)],
        scratch_shapes=[pltpu.VMEM((HG_HEADS, LANE, LANE), F32)],
        input_output_aliases={4: 0},
        compiler_params=_cp(("arbitrary",)),
    )(u, _arr(lb), _arr(nw), mall, ycat)


def _hg_bwd(u, lb, nw, mall, mall_t, o_b, states, dycat, du):
    S = u.shape[0]
    C = HG_CHUNK
    n = S // C
    nb = n // HG_SUB
    rows = HG_SUB * C
    L2 = HG_LEVELS

    def body(u_ref, lb_ref, nw_ref, mall_ref, mallt_ref, o_ref, st_ref, dy_ref, du_in, du_ref, red_ref,
             dst, dlast_s, dq_s, dk_s, dex):
        del du_in

        @pl.when(pl.program_id(0) == 0)
        def _():
            dst[...] = jnp.zeros_like(dst)
            red_ref[...] = jnp.zeros_like(red_ref)

        lb = lb_ref[...]
        nwv = nw_ref[...]
        gm, up, eye, rr = _hg_masks()
        for sub in reversed(range(HG_SUB)):
            r = slice(sub * C, (sub + 1) * C)
            hq, hf, hg = _f(u_ref[r, 0:512]), _f(u_ref[r, 512:1024]), _f(u_ref[r, 1536:2048])
            q = _silu(hq)
            v = u_ref[r, 1024:1536]
            s, f, k, eq, ek, ecum, erem = _hg_factors(hf, lb, mall_ref[...])
            o = o_ref[r, :]
            dy = _f(dy_ref[r, :])
            inv = lax.rsqrt(jnp.mean(o * o, axis=-1, keepdims=True) + EPS)
            ohat = o * inv
            du_ref[r, 1536:2048] = _bf(dy * ohat * nwv * _dsilu(hg))
            dn = dy * _silu(hg)
            red_ref[0:1, :] += jnp.sum(dn * ohat, axis=0, keepdims=True)
            dohat = dn * nwv
            do = inv * (dohat - ohat * jnp.mean(dohat * ohat, axis=-1, keepdims=True))
            for h in range(HG_HEADS):
                sl = slice(h * LANE, (h + 1) * LANE)
                qh, kh, vh, doh = q[:, sl], k[:, sl], _bf(v[:, sl]), _bf(do[:, sl])
                p, qs, ks, qb, kb = _hg_scores(qh, kh, eq, ek, sl, gm, up, eye)
                st_f = st_ref[sub, h]
                sth = _bf(st_f)
                dsth = dst[h]
                dsth_b = _bf(dsth)
                qt = qh * ecum[:, sl]
                kt = kh * erem[:, sl]
                elast = _last_row(ecum[:, sl], rr)
                dp = _mm_nt(doh, vh)
                du_ref[r, 1024 + h * LANE:1024 + (h + 1) * LANE] = _bf(_mm_tn(_bf(p), doh) + _mm_nt(_bf(kt), dsth_b))
                dpe = _bf(dp * eye)
                dqt = _mm(doh, sth)
                dkt = _mm(vh, dsth_b)
                dq = dqt * ecum[:, sl] + _mm(dpe, _bf(kh))
                dk = dkt * erem[:, sl] + _mm_tn(dpe, _bf(qh))
                dex[sub, L2 * C:(L2 + 1) * C, sl] = dqt * qt
                dex[sub, (L2 + 1) * C:(L2 + 2) * C, sl] = dkt * kt
                for l in range(HG_LEVELS):
                    dpl = _bf(dp * gm[l])
                    dql = _mm(dpl, kb[l])
                    dkl = _mm_tn(dpl, qb[l])
                    dq = dq + dql * eq[l][:, sl]
                    dk = dk + dkl * ek[l][:, sl]
                    dex[sub, l * C:(l + 1) * C, sl] = dql * qs[l] + dkl * ks[l]
                dlast_s[sub, :, sl] = jnp.sum(dsth * st_f, axis=0, keepdims=True) * elast
                dst[h] = dsth * elast + _mm_tn(doh, _bf(qt))
                dq_s[sub, :, sl] = dq
                dk_s[sub, :, sl] = dk
            dq = dq_s[sub]
            dk = dk_s[sub]
            dlf = _sel_l2(mallt_ref[...], dex[sub]) + dlast_s[sub]
            du_ref[r, 0:512] = _bf(dq * _dsilu(hq))
            t = (1.0 - s) * (dlf / f - dk)
            du_ref[r, 512:1024] = _bf((1.0 - lb) * s * t)
            red_ref[1:2, :] += jnp.sum(t, axis=0, keepdims=True)

    rev = lambda i: (nb - 1 - i, 0)
    return pl.pallas_call(
        body, name="hg_bwd", grid=(nb,),
        in_specs=[pl.BlockSpec((rows, 2048), rev), _spec(lb), _spec(nw), _full(mall.shape), _full(mall_t.shape),
                  pl.BlockSpec((rows, HG_W), rev),
                  pl.BlockSpec((HG_SUB, HG_HEADS, LANE, LANE), lambda i: (nb - 1 - i, 0, 0, 0)),
                  pl.BlockSpec((rows, HG_W), lambda i: (nb - 1 - i, 1)), pl.BlockSpec(memory_space=pl.ANY)],
        out_specs=[pl.BlockSpec((rows, 2048), rev), pl.BlockSpec((8, HG_W), lambda i: (0, 0))],
        out_shape=[SDS((S, N_PAD), BF16), SDS((8, HG_W), F32)],
        scratch_shapes=[pltpu.VMEM((HG_HEADS, LANE, LANE), F32), pltpu.VMEM((HG_SUB, 1, HG_W), F32),
                        pltpu.VMEM((HG_SUB, C, HG_W), F32), pltpu.VMEM((HG_SUB, C, HG_W), F32),
                        pltpu.VMEM((HG_SUB, (L2 + 2) * C, HG_W), F32)],
        input_output_aliases={8: 0},
        compiler_params=_cp(("arbitrary",)),
    )(u, _arr(lb), _arr(nw), mall, mall_t, o_b, states, dycat, du)


def _ssdconv_fwd(l, u, cw, cb):
    S = u.shape[0]

    def body(u_ref, cw_ref, cb_ref, out_ref):
        rows = _iota((S, LANE), 0)
        out_ref[...] = _silu(_conv_fwd(_f(u_ref[...]), cw_ref, cb_ref, rows))

    return pl.pallas_call(
        body, name="ssdconv_fwd", grid=(SSD_CONV // LANE,),
        in_specs=[pl.BlockSpec((S, LANE), lambda t: (0, OFF_XBC // LANE + t)),
                  pl.BlockSpec((None, 4, LANE), lambda t: (l, 0, t)), pl.BlockSpec((None, 1, LANE), lambda t: (l, 0, t))],
        out_specs=pl.BlockSpec((S, LANE), lambda t: (0, t)),
        out_shape=SDS((S, SSD_CONV), F32),
        compiler_params=_cp(("parallel",)),
    )(u, cw, cb)


def _ssdconv_bwd(l, u, cw, cb, dxbc, du):
    S = u.shape[0]

    def body(u_ref, cw_ref, cb_ref, d_ref, du_in, du_ref, red_ref):
        del du_in
        rows = _iota((S, LANE), 0)
        x = _f(u_ref[...])
        dco = d_ref[...] * _dsilu(_conv_fwd(x, cw_ref, cb_ref, rows))
        dx, dws, dcb = _conv_bwd(x, dco, cw_ref, rows)
        du_ref[...] = _bf(dx)
        for n, p in enumerate(dws + [dcb]):
            red_ref[pl.ds(n, 1), :] = p
        red_ref[pl.ds(5, 3), :] = jnp.zeros((3, LANE), F32)

    ucol = pl.BlockSpec((S, LANE), lambda t: (0, OFF_XBC // LANE + t))
    return pl.pallas_call(
        body, name="ssdconv_bwd", grid=(SSD_CONV // LANE,),
        in_specs=[ucol, pl.BlockSpec((None, 4, LANE), lambda t: (l, 0, t)),
                  pl.BlockSpec((None, 1, LANE), lambda t: (l, 0, t)),
                  pl.BlockSpec((S, LANE), lambda t: (0, t)), pl.BlockSpec(memory_space=pl.ANY)],
        out_specs=[ucol, pl.BlockSpec((8, LANE), lambda t: (0, t))],
        out_shape=[SDS((S, N_PAD), BF16), SDS((8, SSD_CONV), F32)],
        input_output_aliases={4: 0},
        compiler_params=_cp(("parallel",)),
    )(u, cw, cb, dxbc, du)


SSD_SUB = 2


def _ssd_consts():
    e64 = np.zeros((LANE, SSD_W), np.float32)
    for h in range(SSD_HEADS):
        e64[h, h * SSD_P:(h + 1) * SSD_P] = 1.0
    T = SSD_CHUNK
    tril = (np.arange(T)[None, :] <= np.arange(T)[:, None]).astype(np.float32)
    return e64, tril, tril.T.copy()


def _ssd_common(zdt, bias_ref, alog_ref, tril, e64, cum_ref, cumt_ref):
    T = SSD_CHUNK
    lane = _iota((1, LANE), 1)
    a_neg = jnp.where(lane < SSD_HEADS, -jnp.exp(alog_ref[...]), 0.0)
    dtpre = zdt[:, SSD_W:SSD_W + LANE] + bias_ref[...]
    dt = _softplus(dtpre)
    cum = _sel_l(tril, dt * a_neg)
    cum_ref[...] = cum
    cumt_ref[...] = cum.T
    cum_x = _sel_r(cum, e64)
    last_x = _last_row(cum_x, _iota((T, SSD_W), 0))
    ecum_x = jnp.exp(cum_x)
    erem_x = jnp.exp(last_x - cum_x)
    elast_x = jnp.exp(last_x)
    dt_x = _sel_r(dt, e64)
    return a_neg, dtpre, dt, ecum_x, erem_x, elast_x, dt_x


def _ssd_decay(cum_ref, cumt_ref, h, causal):
    T = SSD_CHUNK
    diff = jnp.broadcast_to(cum_ref[:, pl.ds(h, 1)], (T, T)) - cumt_ref[pl.ds(h, 1), :]
    return jnp.exp(jnp.where(causal, diff, NEG))


def _group_norm_fwd(y1, nwv):
    outs, invs = [], []
    for g in range(2):
        seg = y1[:, g * 512:(g + 1) * 512]
        inv = lax.rsqrt(jnp.mean(seg * seg, axis=-1, keepdims=True) + EPS)
        outs.append(seg * inv * nwv[:, g * 512:(g + 1) * 512])
        invs.append(inv)
    return outs, invs


def _ssd_fwd(u, xbc, bias, alog, dskip_x, nw, consts, ycat):
    S = u.shape[0]
    T = SSD_CHUNK
    n = S // T
    rows = SSD_SUB * T
    e64, tril, _ = consts

    def body(u_ref, xbc_ref, bias_ref, alog_ref, dx_ref, nw_ref, e64_ref, tril_ref, ycat_in,
             ycat_ref, y_ref, st_ref, st, cumt, cum_e):
        del ycat_in

        @pl.when(pl.program_id(0) == 0)
        def _():
            st[...] = jnp.zeros_like(st)

        causal = _iota((T, T), 0) >= _iota((T, T), 1)
        lo = _iota((T, LANE), 1) < SSD_P
        for sub in range(SSD_SUB):
            r = slice(sub * T, (sub + 1) * T)
            zdt = _f(u_ref[r, :])
            z = zdt[:, 0:SSD_W]
            xs = xbc_ref[r, 0:SSD_W]
            cum_r, cumt_r = cum_e.at[sub], cumt.at[sub]
            _, _, _, ecum_x, erem_x, elast_x, dt_x = _ssd_common(
                zdt, bias_ref, alog_ref, tril_ref[...], e64_ref[...], cum_r, cumt_r)
            xdt = xs * dt_x
            xrem = xdt * erem_x
            st_ref[sub] = st[...]
            for g in range(2):
                gs = slice(g * 512, (g + 1) * 512)
                bg = _bf(xbc_ref[r, SSD_W + g * LANE:SSD_W + (g + 1) * LANE])
                cg = _bf(xbc_ref[r, SSD_W + 256 + g * LANE:SSD_W + 256 + (g + 1) * LANE])
                cb = _mm_nt(cg, bg)
                yin = _mm(cg, _bf(st[:, gs])) * ecum_x[:, gs]
                for j in range(4):
                    h0 = 8 * g + 2 * j
                    cs = slice(h0 * SSD_P, (h0 + 2) * SSD_P)
                    xp = xdt[:, cs]
                    s0 = _bf(cb * _ssd_decay(cum_r, cumt_r, h0, causal))
                    s1 = _bf(cb * _ssd_decay(cum_r, cumt_r, h0 + 1, causal))
                    y_ref[r, cs] = (_mm(s0, _bf(jnp.where(lo, xp, 0.0))) + _mm(s1, _bf(jnp.where(lo, 0.0, xp)))
                                    + yin[:, j * LANE:(j + 1) * LANE])
                st[:, gs] = st[:, gs] * elast_x[:, gs] + _mm_tn(bg, _bf(xrem[:, gs]))
            y1 = (y_ref[r, :] + dx_ref[...] * xs) * _silu(z)
            outs, _ = _group_norm_fwd(y1, nw_ref[...])
            for g in range(2):
                ycat_ref[r, g * 512:(g + 1) * 512] = _bf(outs[g])

    return pl.pallas_call(
        body, name="ssd_fwd", grid=(n // SSD_SUB,),
        in_specs=[pl.BlockSpec((rows, SSD_W + LANE), lambda i: (i, OFF_Z // (SSD_W + LANE))),
                  pl.BlockSpec((rows, SSD_CONV), lambda i: (i, 0)), _spec(bias), _spec(alog), _spec(dskip_x), _spec(nw),
                  _full(e64.shape), _full(tril.shape), pl.BlockSpec(memory_space=pl.ANY)],
        out_specs=[pl.BlockSpec((rows, SSD_W), lambda i: (i, 1)), pl.BlockSpec((rows, SSD_W), lambda i: (i, 0)),
                   pl.BlockSpec((SSD_SUB, SSD_N, SSD_W), lambda i: (i, 0, 0))],
        out_shape=[SDS((S, D_INNER), BF16), SDS((S,SSD_W), F32), SDS((n, SSD_N, SSD_W), F32)],
        scratch_shapes=[pltpu.VMEM((SSD_N, SSD_W), F32), pltpu.VMEM((SSD_SUB, LANE, T), F32),
                        pltpu.VMEM((SSD_SUB, T, LANE), F32)],
        input_output_aliases={8: 0},
        compiler_params=_cp(("arbitrary",)),
    )(u, xbc, _arr(bias), _arr(alog), _arr(dskip_x), _arr(nw), _bfc(e64), _bfc(tril), ycat)


def _ssd_bwd(u, xbc, bias, alog, dskip_x, nw, consts, y_ssd, states, dycat, du, tok):
    S = u.shape[0]
    T = SSD_CHUNK
    n = S // T
    e64, tril, triu = consts
    e64t = np.ascontiguousarray(e64.T)

    def chunk(u_ref, xbc_ref, bias_ref, alog_ref, dx_ref, nw_ref, e64_ref, e64t_ref, tril_ref, triu_ref,
              y_ref, st_ref, dy_ref, du_ref, dxbc_ref, red_ref, dst, dl_s, cumt, dxdt_s, dy0_s, gb_s, gc_s, cum_e, cs_s):
        zdt = _f(u_ref[...])
        z = zdt[:, 0:SSD_W]
        xs = xbc_ref[:, 0:SSD_W]
        a_neg, dtpre, dt, ecum_x, erem_x, elast_x, dt_x = _ssd_common(
            zdt, bias_ref, alog_ref, tril_ref[...], e64_ref[...], cum_e, cumt)
        causal = _iota((T, T), 0) >= _iota((T, T), 1)
        lo = _iota((T, LANE), 1) < SSD_P
        xdt = xs * dt_x
        xrem = xdt * erem_x
        y = y_ref[...]
        dxv = dx_ref[...]
        nwv = nw_ref[...]
        sz = _silu(z)
        y0 = y + dxv * xs
        y1 = y0 * sz
        for g in range(2):
            gs = slice(g * 512, (g + 1) * 512)
            seg = y1[:, gs]
            inv = lax.rsqrt(jnp.mean(seg * seg, axis=-1, keepdims=True) + EPS)
            shat = seg * inv
            dyg = _f(dy_ref[:, gs])
            red_ref[0:1, gs] += jnp.sum(dyg * shat, axis=0, keepdims=True)
            dsh = dyg * nwv[:, gs]
            dy1g = inv * (dsh - shat * jnp.mean(dsh * shat, axis=-1, keepdims=True))
            du_ref[:, gs] = _bf(dy1g * y0[:, gs] * _dsilu(z[:, gs]))
            dy0_s[:, gs] = dy1g * sz[:, gs]
        dy0 = dy0_s[...]
        red_ref[1:2, :] += jnp.sum(dy0 * xs, axis=0, keepdims=True)
        dyin = dy0 * ecum_x
        lane = _iota((T, LANE), 1)
        dcum = jnp.zeros((T, LANE), F32)

        def decay_grad(h, gm):
            cs_s[pl.ds(h, 1), :] = jnp.sum(gm, axis=0, keepdims=True)
            return jnp.where(lane == h, jnp.sum(gm, axis=1, keepdims=True), 0.0)

        for g in range(2):
            gs = slice(g * 512, (g + 1) * 512)
            bg = _bf(xbc_ref[:, SSD_W + g * LANE:SSD_W + (g + 1) * LANE])
            cg = _bf(xbc_ref[:, SSD_W + 256 + g * LANE:SSD_W + 256 + (g + 1) * LANE])
            cb = _mm_nt(cg, bg)
            dst_f, st_f = dst[:, gs], st_ref[:, gs]
            dstg = _bf(dst_f)
            stg = _bf(st_f)
            dyin_g = _bf(dyin[:, gs])
            xrem_g = _bf(xrem[:, gs])
            dcb = jnp.zeros((T, T), F32)
            dxr = _mm(bg, dstg)
            dxdt_s[:, gs] = dxr * erem_x[:, gs]
            gc_s[:, gs] = dxr * xrem[:, gs]
            gb_s[:, gs] = dyin[:, gs] * _mm(cg, stg)
            dl_s[:, gs] = jnp.sum(dst_f * st_f, axis=0, keepdims=True) * elast_x[:, gs]
            for j in range(4):
                h0 = 8 * g + 2 * j
                cs = slice(h0 * SSD_P, (h0 + 2) * SSD_P)
                xp = xdt[:, cs]
                dyp = dy0[:, cs]
                x_lo, x_hi = _bf(jnp.where(lo, xp, 0.0)), _bf(jnp.where(lo, 0.0, xp))
                d_lo, d_hi = _bf(jnp.where(lo, dyp, 0.0)), _bf(jnp.where(lo, 0.0, dyp))
                l0 = _ssd_decay(cum_e, cumt, h0, causal)
                l1 = _ssd_decay(cum_e, cumt, h0 + 1, causal)
                s0 = cb * l0
                s1 = cb * l1
                ds0 = _mm_nt(d_lo, x_lo)
                ds1 = _mm_nt(d_hi, x_hi)
                dcb = dcb + ds0 * l0 + ds1 * l1
                dxdt_s[:, cs] += _mm_tn(_bf(s0), d_lo) + _mm_tn(_bf(s1), d_hi)
                dcum = dcum + decay_grad(h0, ds0 * s0) + decay_grad(h0 + 1, ds1 * s1)
            dcb_b = _bf(dcb)
            dxbc_ref[:, SSD_W + g * LANE:SSD_W + (g + 1) * LANE] = _mm_tn(dcb_b, cg) + _mm_nt(xrem_g, dstg)
            dxbc_ref[:, SSD_W + 256 + g * LANE:SSD_W + 256 + (g + 1) * LANE] = _mm(dcb_b, bg) + _mm_nt(dyin_g, stg)
            dst[:, gs] = dst_f * elast_x[:, gs] + _mm_tn(cg, dyin_g)
        dxdt = dxdt_s[...]
        dxbc_ref[:, 0:SSD_W] = dxdt * dt_x + dy0 * dxv
        e64t = e64t_ref[...]
        gc = gc_s[...]
        dlast_x = jnp.sum(gc, axis=0, keepdims=True) + dl_s[...]
        dlast = jnp.max(_sel_r(jnp.broadcast_to(dlast_x, (8, SSD_W)), e64t), axis=0, keepdims=True)
        dcum = (dcum - cs_s[...].T + _sel_r(gb_s[...] - gc, e64t)
                + jnp.where(_iota((T, LANE), 0) == T - 1, dlast, 0.0))
        dda = _sel_l(triu_ref[...], dcum)
        ddt = dda * a_neg + _sel_r(dxdt * xs, e64t)
        ddtpre = ddt * _sigmoid(dtpre)
        du_ref[:, SSD_W:SSD_W + LANE] = _bf(jnp.where(lane < SSD_HEADS, ddtpre, 0.0))
        red_ref[2:3, 0:LANE] += jnp.sum(ddtpre, axis=0, keepdims=True)
        red_ref[3:4, 0:LANE] += jnp.sum(dda * dt, axis=0, keepdims=True)

    def body(u_ref, xbc_ref, bias_ref, alog_ref, dx_ref, nw_ref, e64_ref, e64t_ref, tril_ref, triu_ref,
             y_ref, st_ref, dy_ref, du_in, tok_ref, du_ref, dxbc_ref, red_ref, dst, *scratch):
        del du_in, tok_ref

        @pl.when(pl.program_id(0) == 0)
        def _():
            dst[...] = jnp.zeros_like(dst)
            red_ref[...] = jnp.zeros_like(red_ref)
            scratch[-1][...] = jnp.zeros_like(scratch[-1])

        for sub in reversed(range(SSD_SUB)):
            rs = pl.ds(sub * T, T)
            chunk(u_ref.at[rs], xbc_ref.at[rs], bias_ref, alog_ref, dx_ref, nw_ref, e64_ref, e64t_ref, tril_ref, triu_ref,
                  y_ref.at[rs], st_ref.at[sub], dy_ref.at[rs], du_ref.at[rs], dxbc_ref.at[rs], red_ref, dst,
                  *[s.at[sub] for s in scratch])

    nb = n // SSD_SUB
    rows = SSD_SUB * T
    rev = lambda i: (nb - 1 - i, 0)
    sub_scratch = [(1, SSD_W), (LANE, T)] + [(T, SSD_W)] * 4 + [(T, LANE), (LANE, T)]
    return pl.pallas_call(
        body, name="ssd_bwd", grid=(nb,),
        in_specs=[pl.BlockSpec((rows, SSD_W + LANE), lambda i: (nb - 1 - i, OFF_Z // (SSD_W + LANE))),
                  pl.BlockSpec((rows, SSD_CONV), rev), _spec(bias), _spec(alog), _spec(dskip_x), _spec(nw),
                  _full(e64.shape), _full(e64t.shape), _full(tril.shape), _full(triu.shape),
                  pl.BlockSpec((rows, SSD_W), rev), pl.BlockSpec((SSD_SUB, SSD_N, SSD_W), lambda i: (nb - 1 - i, 0, 0)),
                  pl.BlockSpec((rows, SSD_W), lambda i: (nb - 1 - i, 1)), pl.BlockSpec(memory_space=pl.ANY),
                  pl.BlockSpec(memory_space=pl.ANY)],
        out_specs=[pl.BlockSpec((rows, SSD_W + LANE), lambda i: (nb - 1 - i, OFF_Z // (SSD_W + LANE))),
                   pl.BlockSpec((rows, SSD_CONV), rev), pl.BlockSpec((8, SSD_W), lambda i: (0, 0))],
        out_shape=[SDS((S, N_PAD), BF16), SDS((S, SSD_CONV), F32), SDS((8, SSD_W), F32)],
        scratch_shapes=[pltpu.VMEM((SSD_N, SSD_W), F32)] + [pltpu.VMEM((SSD_SUB,) + s, F32) for s in sub_scratch],
        input_output_aliases={13: 0},
        compiler_params=_cp(("arbitrary",)),
    )(u, xbc, _arr(bias), _arr(alog), _arr(dskip_x), _arr(nw), _bfc(e64), _bfc(e64t), _bfc(tril), _bfc(triu), y_ssd,
      states, dycat, du, tok)


def _bfc(a):
    return jnp.asarray(a, BF16)


def _outproj_fwd(ycat, wo, x, gate, tok):
    S = x.shape[0]
    tm = min(512, S)

    def body(yc_ref, wo_ref, x_ref, g_ref, tok_ref, xn_ref, y_ref):
        del tok_ref
        y = _mm(_bf(yc_ref[...]), wo_ref[...])
        y_ref[...] = y
        xn_ref[...] = x_ref[...] + g_ref[...] * y

    row = pl.BlockSpec((tm, D_MODEL), lambda i: (i, 0))
    return pl.pallas_call(
        body, name="outproj_fwd", grid=(S // tm,),
        in_specs=[pl.BlockSpec((tm, D_INNER), lambda i: (i, 0)), _full((D_INNER, D_MODEL)), row, _spec(gate),
                  pl.BlockSpec(memory_space=pl.ANY)],
        out_specs=[row, row],
        out_shape=[SDS((S, D_MODEL), F32), SDS((S, D_MODEL), F32)],
        compiler_params=_cp(("parallel",)),
    )(ycat, wo, x, _arr(gate), tok)


def _outproj_bwd(dxn, y, gate, ycat, wo):
    S = dxn.shape[0]
    tm = min(512, S)

    def body(dx_ref, y_ref, g_ref, yc_ref, wo_ref, dyc_ref, gwo_ref, dg_ref, acc):
        @pl.when(pl.program_id(0) == 0)
        def _():
            acc[...] = jnp.zeros_like(acc)
            dg_ref[...] = jnp.zeros_like(dg_ref)

        dxv = dx_ref[...]
        dy = _bf(dxv * g_ref[...])
        dg_ref[0:1, :] += jnp.sum(dxv * y_ref[...], axis=0, keepdims=True)
        dyc_ref[...] = _mm_nt(dy, wo_ref[...])
        acc[...] += _mm_tn(_bf(yc_ref[...]), dy)

        @pl.when(pl.program_id(0) == pl.num_programs(0) - 1)
        def _():
            gwo_ref[...] = acc[...].astype(BF16)

    row = pl.BlockSpec((tm, D_MODEL), lambda i: (i, 0))
    wide = pl.BlockSpec((tm, D_INNER), lambda i: (i, 0))
    return pl.pallas_call(
        body, name="outproj_bwd", grid=(S // tm,),
        in_specs=[row, row, _spec(gate), wide, _full((D_INNER, D_MODEL))],
        out_specs=[wide, _full((D_INNER, D_MODEL)), _full((8, D_MODEL))],
        out_shape=[SDS((S, D_INNER), F32), SDS((D_INNER, D_MODEL), BF16), SDS((8, D_MODEL), F32)],
        scratch_shapes=[pltpu.VMEM((D_INNER, D_MODEL), F32)],
        compiler_params=_cp(("arbitrary",)),
    )(dxn, y, _arr(gate), ycat, wo)


def _loss_head(x, fw, target):
    S = x.shape[0]
    tm = min(512, S)

    def body(x_ref, fw_ref, t_ref, dx_ref, red_ref):
        @pl.when(pl.program_id(0) == 0)
        def _():
            red_ref[...] = jnp.zeros_like(red_ref)

        xv = x_ref[...]
        fwv = fw_ref[...]
        inv = lax.rsqrt(jnp.mean(xv * xv, axis=-1, keepdims=True) + EPS)
        xhat = xv * inv
        err = xhat * fwv - t_ref[...]
        col = jnp.sum(err * err, axis=0, keepdims=True)
        red_ref[1:2, :] += jnp.broadcast_to(jnp.sum(col, axis=1, keepdims=True) * (0.5 / D_MODEL), (1, D_MODEL))
        dy = err * (1.0 / D_MODEL)
        red_ref[0:1, :] += jnp.sum(dy * xhat, axis=0, keepdims=True)
        dxhat = dy * fwv
        dx_ref[...] = inv * (dxhat - xhat * jnp.mean(dxhat * xhat, axis=-1, keepdims=True))

    row = pl.BlockSpec((tm, D_MODEL), lambda i: (i, 0))
    return pl.pallas_call(
        body, name="loss_head", grid=(S // tm,),
        in_specs=[row, _vec(D_MODEL), row],
        out_specs=[row, _full((8, D_MODEL))],
        out_shape=[SDS((S, D_MODEL), F32), SDS((8, D_MODEL), F32)],
        compiler_params=_cp(("arbitrary",)),
    )(x, fw, target)


ADA_COLS = 3 * D_MODEL // N_DEV


def _ada_fwd(c_all, w_ada, b_cols):
    def body(c_ref, w_ref, b_ref, out_ref):
        out_ref[...] = _mm(_bf(_silu(c_ref[...])), _bf(w_ref[...])) + b_ref[...]

    return pl.pallas_call(
        body, name="ada_fwd", grid=(DEPTH,),
        in_specs=[_full((N_DEV, D_MODEL)), pl.BlockSpec((None, D_MODEL, ADA_COLS), lambda l: (l, 0, 0)),
                  pl.BlockSpec((None, 1, ADA_COLS), lambda l: (l, 0, 0))],
        out_specs=pl.BlockSpec((None, N_DEV, ADA_COLS), lambda l: (l, 0, 0)),
        out_shape=SDS((DEPTH, N_DEV, ADA_COLS), F32),
        compiler_params=_cp(("parallel",)),
    )(c_all, w_ada, b_cols)


def _ada_bwd(ct_pad, dmod_pad):
    def body(c_ref, d_ref, out_ref):
        out_ref[...] = _mm(_bf(_silu(c_ref[...])), _bf(d_ref[...]))

    return pl.pallas_call(
        body, name="ada_bwd", grid=(DEPTH,),
        in_specs=[_full((D_MODEL, LANE)), pl.BlockSpec((None, LANE, ADA_COLS), lambda l: (l, 0, 0))],
        out_specs=pl.BlockSpec((None, D_MODEL, ADA_COLS), lambda l: (l, 0, 0)),
        out_shape=SDS((DEPTH, D_MODEL, ADA_COLS), F32),
        compiler_params=_cp(("parallel",)),
    )(ct_pad, dmod_pad)


def _adamw(parts, w, m, v, name, own=None, layers=None, prev=None):
    n, L, R, C = parts.shape
    lo, hi = layers or (0, L)
    tr = R
    while tr * C * 4 > (1 << 20) and tr % 16 == 0:
        tr //= 2
    first = 1 if own is None else 2

    def body(*refs):
        p_ref = refs[0]
        w_ref, m_ref, v_ref = refs[first:first + 3]
        g_ref, d_ref, mo_ref, vo_ref = refs[-4:]

        def part(k):
            if own is None:
                return p_ref[k].astype(F32)
            me = 4 * lax.axis_index("x") + 2 * lax.axis_index("y") + lax.axis_index("c")
            return jnp.where(me == k, refs[1][...], p_ref[k]).astype(F32)

        g = part(0)
        for k in range(1, n):
            g = g + part(k)
        mn = ADAM_B1 * m_ref[...] + (1.0 - ADAM_B1) * g
        vn = ADAM_B2 * v_ref[...] + (1.0 - ADAM_B2) * (g * g)
        m_hat = mn / (1.0 - ADAM_B1 ** ADAM_STEP)
        v_hat = vn / (1.0 - ADAM_B2 ** ADAM_STEP)
        g_ref[...] = g
        d_ref[...] = -ADAM_LR * (m_hat / (jnp.sqrt(v_hat) + ADAM_EPS) + ADAM_WD * w_ref[...])
        mo_ref[...] = mn
        vo_ref[...] = vn

    blk = pl.BlockSpec((None, tr, C), lambda l, i: (lo + l, i, 0))
    own_blk = [] if own is None else [pl.BlockSpec((None, tr, C), lambda l, i: (l, i, 0))]
    n_blk = 3 if own is None else 4
    return pl.pallas_call(
        body, name=name, grid=(hi - lo, R // tr),
        in_specs=[pl.BlockSpec((n, None, tr, C), lambda l, i: (0, lo + l, i, 0))] + own_blk + [blk] * 3
        + ([] if prev is None else [ANY] * 4),
        out_specs=[blk] * 4,
        out_shape=[SDS((L, R, C), F32)] * 4,
        input_output_aliases={} if prev is None else {1 + n_blk + k: k for k in range(4)},
        compiler_params=_cp(("parallel", "parallel")),
    )(parts, *([] if own is None else [own]), w, m, v, *([] if prev is None else prev))


MESH = pl.DeviceIdType.MESH
ANY = pl.BlockSpec(memory_space=pl.ANY)


def _all_gather(v, name):
    def body(v_ref, out_ref, send_sems, recv_sems, local_sem):
        x, y, c = lax.axis_index("x"), lax.axis_index("y"), lax.axis_index("c")
        me, sibling = (x, y, c), (x, y, 1 - c)
        chips = [(1 - x, y), (x, 1 - y), (1 - x, 1 - y)]

        def slot(px, py, pc):
            return out_ref.at[4 * px + 2 * py + pc]

        def copy(k, block, to, src=None):
            return pltpu.make_async_remote_copy(
                src_ref=slot(*block) if src is None else src, dst_ref=slot(*block),
                send_sem=send_sems.at[k], recv_sem=recv_sems.at[k], device_id=to, device_id_type=MESH)

        mine = pltpu.make_async_copy(v_ref, slot(*me), local_sem)
        mine.start()
        first = [copy(0, me, sibling, src=v_ref)]
        first += [copy(1 + j, me, (*chip, c), src=v_ref) for j, chip in enumerate(chips)]
        for cp in first:
            cp.start()
        passed = [copy(4 + j, (*chip, c), sibling) for j, chip in enumerate(chips)]
        for j, chip in enumerate(chips):
            copy(1 + j, (*chip, c), me).wait_recv()
            passed[j].start()
        copy(0, sibling, me).wait_recv()
        for j, chip in enumerate(chips):
            copy(4 + j, (*chip, 1 - c), me).wait_recv()
        for cp in first + passed:
            cp.wait_send()
        mine.wait()

    return pl.pallas_call(
        body, name=name, in_specs=[ANY], out_specs=ANY,
        out_shape=SDS((N_DEV,) + v.shape, v.dtype),
        scratch_shapes=[pltpu.SemaphoreType.DMA((7,)), pltpu.SemaphoreType.DMA((7,)), pltpu.SemaphoreType.DMA],
    )(v)


def _all_to_all(v, name):
    def body(v_ref, out_ref, send_sems, recv_sems, local_sem):
        x, y, c = lax.axis_index("x"), lax.axis_index("y"), lax.axis_index("c")
        mine_idx = 4 * x + 2 * y + c
        mine = pltpu.make_async_copy(v_ref.at[mine_idx], out_ref.at[mine_idx], local_sem)
        mine.start()
        sends, recvs = [], []
        for k in range(1, N_DEV):
            px = 1 - x if k & 4 else x
            py = 1 - y if k & 2 else y
            pc = 1 - c if k & 1 else c
            peer_idx = 4 * px + 2 * py + pc
            sems = dict(send_sem=send_sems.at[k - 1], recv_sem=recv_sems.at[k - 1], device_id=(px, py, pc),
                        device_id_type=MESH)
            sends.append(pltpu.make_async_remote_copy(src_ref=v_ref.at[peer_idx], dst_ref=out_ref.at[mine_idx], **sems))
            recvs.append(pltpu.make_async_remote_copy(src_ref=v_ref.at[peer_idx], dst_ref=out_ref.at[peer_idx], **sems))
        for cp in sends:
            cp.start()
        for cp in recvs:
            cp.wait_recv()
        for cp in sends:
            cp.wait_send()
        mine.wait()

    return pl.pallas_call(
        body, name=name, in_specs=[ANY], out_specs=ANY,
        out_shape=SDS(v.shape, v.dtype),
        scratch_shapes=[pltpu.SemaphoreType.DMA((7,)), pltpu.SemaphoreType.DMA((7,)), pltpu.SemaphoreType.DMA],
    )(v)


HBM_SPEC = pl.BlockSpec(memory_space=pltpu.HBM)
SEM_SPEC = pl.BlockSpec(memory_space=pltpu.SEMAPHORE)
EFFECT = pltpu.SideEffectType.DATAFLOW_SIDE_EFFECTING


EXCHANGE_PEERS = {"gather": range(1, N_DEV), "scatter": range(1, N_DEV), "chip": (1, 2, 4, 6), "pass": (2, 4, 6)}


def _exchange_copies(srcs, lands, send_sems, recv_sems, mode, layer):
    x, y, c = lax.axis_index("x"), lax.axis_index("y"), lax.axis_index("c")
    me = 4 * x + 2 * y + c
    copies = []
    for a, (src, land) in enumerate(zip(srcs, lands)):
        for k in EXCHANGE_PEERS[mode]:
            px = 1 - x if k & 4 else x
            py = 1 - y if k & 2 else y
            pc = 1 - c if k & 1 else c
            peer = 4 * px + 2 * py + pc
            if mode == "scatter":
                s, d, to = src.at[peer], land.at[me, layer], (px, py, pc)
            elif mode == "pass":
                s, d, to = land.at[peer], land.at[peer], (x, y, 1 - c)
            else:
                s, d, to = src, land.at[me], (px, py, pc)
            n = 7 * a + k - 1
            copies.append(pltpu.make_async_remote_copy(
                src_ref=s, dst_ref=d, send_sem=send_sems.at[n], recv_sem=recv_sems.at[n], device_id=to,
                device_id_type=MESH))
    return copies


def _exchange_start(name, srcs, lands, mode, layer=0, after=None):
    n = len(srcs)

    def body(*refs):
        send_sems, recv_sems = refs[-2 * n - 3], refs[-2 * n - 2]
        for cp in _exchange_copies(refs[:n], refs[n:2 * n], send_sems, recv_sems, mode, layer):
            cp.start()
        refs[-1][...] = jnp.zeros_like(refs[-1])

    arrays = list(srcs) + list(lands)
    sems = pltpu.SemaphoreType.DMA((7 * n,))
    out = pl.pallas_call(
        body, name=name,
        out_shape=(sems, sems, *[pltpu.HBM(v.shape, v.dtype) for v in arrays], SDS((8, LANE), F32)),
        in_specs=[HBM_SPEC] * (2 * n) + ([ANY] if after is not None else []),
        out_specs=(SEM_SPEC, SEM_SPEC, *[HBM_SPEC] * (2 * n), pl.BlockSpec(memory_space=pltpu.VMEM)),
        input_output_aliases={i: 2 + i for i in range(2 * n)},
        compiler_params=pltpu.CompilerParams(has_side_effects=EFFECT),
    )(*[pltpu.with_memory_space_constraint(v, pltpu.HBM) for v in arrays], *([after] if after is not None else []))
    return dict(sems=out[:2], srcs=out[2:2 + n], lands=out[2 + n:2 + 2 * n], token=out[-1], mode=mode,
                layer=layer)


def _exchange_wait(name, st, after, also=()):
    n = len(st["srcs"])

    def body(*refs):
        send_sems, recv_sems = refs[2 * n], refs[2 * n + 1]
        for cp in _exchange_copies(refs[:n], refs[n:2 * n], send_sems, recv_sems, st["mode"], st["layer"]):
            cp.wait_send()
            cp.wait_recv()

    arrays = list(st["srcs"]) + list(st["lands"])
    out = pl.pallas_call(
        body, name=name,
        out_shape=tuple(pltpu.HBM(v.shape, v.dtype) for v in arrays),
        in_specs=[HBM_SPEC] * (2 * n) + [SEM_SPEC, SEM_SPEC] + [ANY] * (1 + len(also)),
        out_specs=tuple([HBM_SPEC] * (2 * n)),
        input_output_aliases={i: i for i in range(2 * n)},
        compiler_params=pltpu.CompilerParams(has_side_effects=EFFECT),
    )(*arrays, *st["sems"], after, *also)
    st["srcs"] = out[:n]
    return out[n:]


def _exchange_relay(name, st, after, also=()):
    n = len(st["srcs"])

    def body(*refs):
        def copies(send_sems, recv_sems, mode):
            keys = [(a, k) for a in range(n) for k in EXCHANGE_PEERS[mode]]
            return dict(zip(keys, _exchange_copies(refs[:n], refs[n:2 * n], send_sems, recv_sems, mode, st["layer"]),
                            strict=True))

        arrived = copies(refs[2 * n], refs[2 * n + 1], st["mode"])
        onward = copies(refs[-2 * n - 3], refs[-2 * n - 2], "pass")
        for key, cp in arrived.items():
            cp.wait_recv()
            if key in onward:
                onward[key].start()
        for cp in arrived.values():
            cp.wait_send()
        refs[-1][...] = jnp.zeros_like(refs[-1])

    arrays = list(st["srcs"]) + list(st["lands"])
    sems = pltpu.SemaphoreType.DMA((7 * n,))
    out = pl.pallas_call(
        body, name=name,
        out_shape=(sems, sems, *[pltpu.HBM(v.shape, v.dtype) for v in arrays], SDS((8, LANE), F32)),
        in_specs=[HBM_SPEC] * (2 * n) + [SEM_SPEC, SEM_SPEC] + [ANY] * (1 + len(also)),
        out_specs=(SEM_SPEC, SEM_SPEC, *[HBM_SPEC] * (2 * n), pl.BlockSpec(memory_space=pltpu.VMEM)),
        input_output_aliases={i: 2 + i for i in range(2 * n)},
        compiler_params=pltpu.CompilerParams(has_side_effects=EFFECT),
    )(*arrays, *st["sems"], after, *also)
    return dict(sems=out[:2], srcs=out[2:2 + n], lands=out[2 + n:2 + 2 * n], token=out[-1], mode="pass",
                layer=st["layer"])


def _exchange_wait_all(name, sts, lands, ids, after):
    counts = [len(st["srcs"]) for st in sts]
    ns, nl = sum(counts), len(lands)

    def body(*refs):
        at = 0
        for e, st in enumerate(sts):
            own_lands = [refs[ns + i] for i in ids[e]]
            send_sems, recv_sems = refs[ns + nl + 2 * e], refs[ns + nl + 2 * e + 1]
            for cp in _exchange_copies(refs[at:at + counts[e]], own_lands, send_sems, recv_sems, st["mode"],
                                       st["layer"]):
                cp.wait_send()
                cp.wait_recv()
            at += counts[e]

    arrays = [s for st in sts for s in st["srcs"]] + list(lands)
    out = pl.pallas_call(
        body, name=name,
        out_shape=tuple(pltpu.HBM(v.shape, v.dtype) for v in arrays),
        in_specs=[HBM_SPEC] * len(arrays) + [SEM_SPEC] * (2 * len(sts)) + [ANY],
        out_specs=tuple([HBM_SPEC] * len(arrays)),
        input_output_aliases={i: i for i in range(len(arrays))},
        compiler_params=pltpu.CompilerParams(has_side_effects=EFFECT),
    )(*arrays, *[s for st in sts for s in st["sems"]], after)
    at = 0
    for e, st in enumerate(sts):
        st["srcs"] = out[at:at + counts[e]]
        at += counts[e]
    return list(out[ns:])


_IN_PIECES = ([(1024, 3072)]
              + [r for t in range(4) for r in ((LANE * t, LANE * (t + 1)), (512 + LANE * t, 512 + LANE * (t + 1)))]
              + [(4096, 5632), (3072, 4096), (5632, 5648)])


def _permute_in(w):
    pad = jnp.zeros(w.shape[:-1] + (N_PAD - N_IN,), w.dtype)
    return jnp.concatenate([w[..., a:b] for a, b in _IN_PIECES] + [pad], axis=-1)


def _unpermute_in(g):
    ax = [g[..., OFF_LRU + 2 * LANE * t:OFF_LRU + 2 * LANE * t + LANE] for t in range(4)]
    ag = [g[..., OFF_LRU + 2 * LANE * t + LANE:OFF_LRU + 2 * LANE * (t + 1)] for t in range(4)]
    return jnp.concatenate(ax + ag + [g[..., 0:2048], g[..., OFF_Z:OFF_Z + SSD_W], g[..., OFF_XBC:OFF_XBC + SSD_CONV],
                                      g[..., OFF_Z + SSD_W:OFF_Z + SSD_W + SSD_HEADS]], axis=-1)


SHARD_COLS = N_IN // N_DEV


def _in_segments():
    segs, pos = [], 0
    for a, b in _IN_PIECES:
        for i in range(N_DEV):
            lo, hi = max(a, SHARD_COLS * i), min(b, SHARD_COLS * (i + 1))
            if lo < hi:
                segs.append((i, lo - SHARD_COLS * i, hi - lo, pos + lo - a))
        pos += b - a
    return segs


RELAYOUT_ROWS = 512


def _relayout_in(land, own):
    def body(land_ref, own_ref, out_ref):
        me = 4 * lax.axis_index("x") + 2 * lax.axis_index("y") + lax.axis_index("c")
        out_ref[:, N_IN:N_PAD] = jnp.zeros((RELAYOUT_ROWS, N_PAD - N_IN), BF16)
        for i, j, wd, p in _in_segments():
            out_ref[:, p:p + wd] = jnp.where(me == i, own_ref[:, j:j + wd], land_ref[i, :, j:j + wd])

    return pl.pallas_call(
        body, name="relayout_in", grid=(D_MODEL // RELAYOUT_ROWS,),
        in_specs=[pl.BlockSpec((N_DEV, RELAYOUT_ROWS, SHARD_COLS), lambda r: (0, r, 0)),
                  pl.BlockSpec((RELAYOUT_ROWS, SHARD_COLS), lambda r: (r, 0))],
        out_specs=pl.BlockSpec((RELAYOUT_ROWS, N_PAD), lambda r: (r, 0)),
        out_shape=SDS((D_MODEL, N_PAD), BF16),
        compiler_params=_cp(("parallel",)),
    )(land, own)


def _relayout_grad(g):
    def body(g_ref, out_ref):
        for i, j, wd, p in _in_segments():
            out_ref[i, :, j:j + wd] = g_ref[:, p:p + wd].astype(BF16)

    return pl.pallas_call(
        body, name="relayout_grad", grid=(D_MODEL // RELAYOUT_ROWS,),
        in_specs=[pl.BlockSpec((RELAYOUT_ROWS, N_PAD), lambda r: (r, 0))],
        out_specs=pl.BlockSpec((N_DEV, RELAYOUT_ROWS, SHARD_COLS), lambda r: (0, r, 0)),
        out_shape=SDS((N_DEV, D_MODEL, SHARD_COLS), BF16),
        compiler_params=_cp(("parallel",)),
    )(g)


def _block_diag(w):
    w4 = w.reshape(DEPTH, 4, 2, 64, 64)
    z = jnp.zeros((DEPTH, 4, 64, 64), w.dtype)
    top = jnp.concatenate([w4[:, :, 0], z], axis=-1)
    bot = jnp.concatenate([z, w4[:, :, 1]], axis=-1)
    return jnp.concatenate([top, bot], axis=2).astype(BF16)


def _diag_blocks(g):
    return jnp.stack([g[:, :, :64, :64], g[:, :, 64:, 64:]], axis=2).reshape(DEPTH, 8, 64, 64)


def _pad_lanes(v):
    return jnp.pad(v, ((0, 0), (0, LANE - v.shape[1])))


def _lower_bounds(logits):
    p = jax.nn.softmax(logits, axis=0)
    return p, jnp.cumsum(p, axis=0) - p[0]


def _lower_bounds_bwd(p, dlb):
    dp = jnp.cumsum(dlb[::-1], axis=0)[::-1]
    dp = dp.at[0].add(-jnp.sum(dlb, axis=0))
    return p * (dp - jnp.sum(dp * p, axis=0, keepdims=True))


SMALL = ["norm_w", "b_ada", "lru_conv_b", "lru_wa", "lru_ba", "lru_wx", "lru_bx", "lru_lambda", "hg_lb_logits",
         "hg_norm_w", "ssd_conv_b", "ssd_dt_bias", "ssd_a_log", "ssd_d", "ssd_norm_w", "final_norm_w"]
WEIGHTS = ["norm_w", "w_ada", "b_ada", "w_in", "lru_conv_w", "lru_conv_b", "lru_wa", "lru_ba", "lru_wx", "lru_bx",
           "lru_lambda", "hg_lb_logits", "hg_norm_w", "ssd_conv_w", "ssd_conv_b", "ssd_dt_bias", "ssd_a_log", "ssd_d",
           "ssd_norm_w", "w_out", "final_norm_w"]
INPUTS = ["x", "c"] + WEIGHTS + ["loss_target"] + ["m_" + n for n in WEIGHTS] + ["v_" + n for n in WEIGHTS]
SMALL_ROW = 1024


def _small_rows(like):
    out, off = {}, 0
    for n in SMALL:
        rows = -(-int(np.prod(like[n].shape)) // (8 * SMALL_ROW)) * 8
        out[n] = (off, rows)
        off += rows
    return out, off


def _flatten_small(d, prefix="", last=0.0):
    table, _ = _small_rows({n: d[prefix + n] for n in SMALL})
    pieces = []
    for n in SMALL:
        flat = d[prefix + n].reshape(-1)
        pieces.append(jnp.pad(flat, (0, table[n][1] * SMALL_ROW - flat.shape[0])).reshape(-1, SMALL_ROW))
    return jnp.concatenate(pieces + [jnp.full((8, SMALL_ROW), last, F32)], axis=0)


def _split_small(packed, like):
    table, _ = _small_rows(like)
    out = {}
    for n in SMALL:
        off, rows = table[n]
        size = int(np.prod(like[n].shape))
        out[n] = packed[off:off + rows].reshape(-1)[:size].reshape(like[n].shape)
    return out


def _local_step(x, mod, target, w, fetch, emit):
    S = x.shape[0]
    mall = _bfc(_hg_consts())
    mall_t = _bfc(_hg_consts().T)
    consts = _ssd_consts()
    p_lb, lbs = _lower_bounds(w["hg_lb_logits"])
    no_tok = jnp.zeros((8, LANE), F32)
    wa, wx = _block_diag(w["lru_wa"]), _block_diag(w["lru_wx"])
    ba, bx = w["lru_ba"].reshape(DEPTH, 1, LRU_W), w["lru_bx"].reshape(DEPTH, 1, LRU_W)
    lru_cb, lam, ssd_cb = w["lru_conv_b"][:, None], w["lru_lambda"][:, None], w["ssd_conv_b"][:, None]
    bias, alog = _pad_lanes(w["ssd_dt_bias"]), _pad_lanes(w["ssd_a_log"])
    dskip = jnp.repeat(w["ssd_d"], SSD_P, axis=1)
    saved = []
    for l in range(DEPTH):
        w_in_l, w_out_l, token = fetch(l, x)
        shift, scale, gate = (_Row(mod, l, D_MODEL, k) for k in range(3))
        nw = _Row(w["norm_w"], l)
        u, h = _inproj_fwd(x, nw, scale, shift, w_in_l, no_tok if token is None else token)
        ycat = lax.empty((S, D_INNER), BF16)
        lru_args = (l, u, w["lru_conv_w"], lru_cb, wa, ba, wx, bx, lam)
        ycat, h_lru = _lru_fwd(*lru_args, ycat)
        hg_args = (u, _Row(lbs, l), _Row(w["hg_norm_w"], l), mall)
        ycat, o_b, hg_st = _hg_fwd(*hg_args, ycat)
        xbc = _ssdconv_fwd(l, u, w["ssd_conv_w"], ssd_cb)
        ssd_args = (u, xbc, _Row(bias, l), _Row(alog, l), _Row(dskip, l), _Row(w["ssd_norm_w"], l), consts)
        ycat, y_ssd, ssd_st = _ssd_fwd(*ssd_args, ycat)
        token = fetch(l, y_ssd, late=True)
        if callable(w_out_l):
            w_out_l = w_out_l()
        x_new, y = _outproj_fwd(ycat, w_out_l, x, gate, no_tok if token is None else token)
        saved.append((x, u, h, ycat, nw, scale, gate, w_in_l, w_out_l, lru_args, h_lru, hg_args, o_b, hg_st, ssd_args,
                      y_ssd, ssd_st, y))
        x = x_new
    dx, red = _loss_head(x, w["final_norm_w"][None, :], target)
    loss = red[1, 0]
    reds = {k: [None] * DEPTH for k in ("in", "gate", "lru", "wa", "wx", "hg", "conv", "ssd")}
    for l in reversed(range(DEPTH)):
        (x, u, h, ycat, nw, scale, gate, w_in_l, w_out_l, lru_args, h_lru, hg_args, o_b, hg_st, ssd_args, y_ssd, ssd_st,
         y) = saved[l]
        dycat, g_out, reds["gate"][l] = _outproj_bwd(dx, y, gate, ycat, w_out_l)
        token = emit(l, "w_out", g_out)
        du = lax.empty((S, N_PAD), BF16)
        du, dxbc, reds["ssd"][l] = _ssd_bwd(*ssd_args, y_ssd, ssd_st, dycat, du, no_tok if token is None else token)
        du, reds["conv"][l] = _ssdconv_bwd(l, u, w["ssd_conv_w"], ssd_cb, dxbc, du)
        du, reds["hg"][l] = _hg_bwd(*hg_args, mall_t, o_b, hg_st, dycat, du)
        du, reds["lru"][l], reds["wa"][l], reds["wx"][l] = _lru_bwd(*lru_args, h_lru, dycat, du)
        token = emit(l, "w_in", functools.partial(_inproj_bwd_w, h, du))
        dx, reds["in"][l] = _inproj_bwd_x(du, w_in_l, x, nw, scale, dx, no_tok if token is None else token)
    r = {k: jnp.stack(v) for k, v in reds.items()}
    g = {n: None for n in WEIGHTS}
    g["final_norm_w"] = red[0]
    g["norm_w"] = r["in"][:, 2]
    dmod = jnp.concatenate([r["in"][:, 0], r["in"][:, 1], r["gate"][:, 0]], axis=1)
    g["lru_conv_w"], g["lru_conv_b"] = r["lru"][:, 0:4], r["lru"][:, 4]
    g["lru_ba"], g["lru_bx"] = r["lru"][:, 5].reshape(DEPTH, 8, 64), r["lru"][:, 6].reshape(DEPTH, 8, 64)
    g["lru_lambda"] = r["lru"][:, 7]
    g["lru_wa"], g["lru_wx"] = _diag_blocks(r["wa"]), _diag_blocks(r["wx"])
    g["hg_norm_w"] = r["hg"][:, 0]
    g["hg_lb_logits"] = _lower_bounds_bwd(p_lb, r["hg"][:, 1])
    g["ssd_conv_w"], g["ssd_conv_b"] = r["conv"][:, 0:4], r["conv"][:, 4]
    g["ssd_norm_w"] = r["ssd"][:, 0]
    g["ssd_d"] = r["ssd"][:, 1].reshape(DEPTH, SSD_HEADS, SSD_P).sum(-1)
    g["ssd_dt_bias"] = r["ssd"][:, 2, :SSD_HEADS]
    g["ssd_a_log"] = -r["ssd"][:, 3, :SSD_HEADS] * jnp.exp(w["ssd_a_log"])
    return loss, dx, dmod, g


def kernel(x, c, norm_w, w_ada, b_ada, w_in, lru_conv_w, lru_conv_b, lru_wa, lru_ba, lru_wx, lru_bx, lru_lambda, hg_lb_logits, hg_norm_w, ssd_conv_w, ssd_conv_b, ssd_dt_bias, ssd_a_log, ssd_d, ssd_norm_w, w_out, final_norm_w, loss_target, m_norm_w, m_w_ada, m_b_ada, m_w_in, m_lru_conv_w, m_lru_conv_b, m_lru_wa, m_lru_ba, m_lru_wx, m_lru_bx, m_lru_lambda, m_hg_lb_logits, m_hg_norm_w, m_ssd_conv_w, m_ssd_conv_b, m_ssd_dt_bias, m_ssd_a_log, m_ssd_d, m_ssd_norm_w, m_w_out, m_final_norm_w, v_norm_w, v_w_ada, v_b_ada, v_w_in, v_lru_conv_w, v_lru_conv_b, v_lru_wa, v_lru_ba, v_lru_wx, v_lru_bx, v_lru_lambda, v_hg_lb_logits, v_hg_norm_w, v_ssd_conv_w, v_ssd_conv_b, v_ssd_dt_bias, v_ssd_a_log, v_ssd_d, v_ssd_norm_w, v_w_out, v_final_norm_w):
    return _step(x, c, norm_w, w_ada, b_ada, w_in, lru_conv_w, lru_conv_b, lru_wa, lru_ba, lru_wx, lru_bx, lru_lambda, hg_lb_logits, hg_norm_w, ssd_conv_w, ssd_conv_b, ssd_dt_bias, ssd_a_log, ssd_d, ssd_norm_w, w_out, final_norm_w, loss_target, m_norm_w, m_w_ada, m_b_ada, m_w_in, m_lru_conv_w, m_lru_conv_b, m_lru_wa, m_lru_ba, m_lru_wx, m_lru_bx, m_lru_lambda, m_hg_lb_logits, m_hg_norm_w, m_ssd_conv_w, m_ssd_conv_b, m_ssd_dt_bias, m_ssd_a_log, m_ssd_d, m_ssd_norm_w, m_w_out, m_final_norm_w, v_norm_w, v_w_ada, v_b_ada, v_w_in, v_lru_conv_w, v_lru_conv_b, v_lru_wa, v_lru_ba, v_lru_wx, v_lru_bx, v_lru_lambda, v_hg_lb_logits, v_hg_norm_w, v_ssd_conv_w, v_ssd_conv_b, v_ssd_dt_bias, v_ssd_a_log, v_ssd_d, v_ssd_norm_w, v_w_out, v_final_norm_w)


def _step(*args):
    a = dict(zip(INPUTS, args, strict=True))
    me = 4 * lax.axis_index("x") + 2 * lax.axis_index("y") + lax.axis_index("c")
    x, target = a["x"][0], a["loss_target"][0]

    c_all = _all_gather(a["c"], "gather_c")[:, 0, :]
    b_cols = lax.dynamic_slice_in_dim(a["b_ada"], me * ADA_COLS, ADA_COLS, axis=1)[:, None, :]
    mod_parts = _all_gather(_ada_fwd(c_all, a["w_ada"], b_cols), "gather_mod")
    mod = lax.dynamic_index_in_dim(mod_parts, me, axis=2, keepdims=False)
    mod = mod.transpose(1, 0, 2).reshape(DEPTH, 3 * D_MODEL)

    w = {n: a[n] for n in SMALL}

    w_in_b = [a["w_in"][l].astype(BF16) for l in range(DEPTH)]
    w_out_b = a["w_out"].astype(BF16)
    conv_own = jnp.concatenate([a["lru_conv_w"], a["ssd_conv_w"]], axis=-1)
    cols, rows_out = N_IN // N_DEV, D_INNER // N_DEV

    def gather_start(l, after):
        srcs = [w_in_b[l], conv_own if l == 0 else w_out_b[l]]
        lands = [lax.empty((N_DEV,) + s.shape, s.dtype) for s in srcs]
        return _exchange_start(f"gather_start_{l}", srcs, lands, "chip", after=after)

    def gather_pass(name, st, after, also=()):
        return _exchange_wait(name + "_passed", _exchange_relay(name + "_pass", st, after, also), after)

    gathers = {0: gather_start(0, mod)}
    passing = {}
    out0 = {"st": _exchange_start("gather_start_0_out", [w_out_b[0]], [lax.empty((N_DEV,) + w_out_b[0].shape, BF16)],
                                  "chip", after=gathers[0]["token"])}

    def fetch(l, x_l, late=False):
        if late:
            if l + 1 == DEPTH:
                return None
            if l == 0:
                out0["st"] = _exchange_relay("gather_0_out_pass", out0["st"], x_l)
                x_l = out0["st"]["token"]
            passing[l + 1] = _exchange_relay(f"gather_{l + 1}_pass", gathers[l + 1], x_l)
            if l == 0:
                landed = _exchange_wait("gather_0_out_passed", out0["st"], passing[1]["token"])
                out0["w"] = lax.dynamic_update_index_in_dim(landed[0], w_out_b[0], me, 0).reshape(D_INNER, D_MODEL)
            return passing[l + 1]["token"]
        if l == 0:
            landed = gather_pass("gather_0", gathers[0], x_l,
                                 also=(a["w_in"], a["m_w_in"], a["v_w_in"], out0["st"]["token"]))
            conv =lax.dynamic_update_index_in_dim(landed[1], conv_own, me, 0).transpose(1, 2, 0, 3)
            w["lru_conv_w"] = conv[..., :64].reshape(DEPTH, 4, LRU_W)
            w["ssd_conv_w"] = conv[..., 64:].reshape(DEPTH, 4, SSD_CONV)
            after, w_out_l = landed[0], lambda: out0["w"]
        else:
            landed = _exchange_wait(f"gather_{l}_passed", passing[l], x_l)
            after = lax.dynamic_update_index_in_dim(landed[1], w_out_b[l], me, 0)
            w_out_l = after.reshape(D_INNER, D_MODEL)
        token = None
        if l + 1 < DEPTH:
            gathers[l + 1] = gather_start(l + 1, after)
            token = gathers[l + 1]["token"]
        return _relayout_in(landed[0], w_in_b[l]), w_out_l, token

    PROJ = ("w_in", "w_out")
    scatters = {}
    lands = [lax.empty((N_DEV, DEPTH, D_MODEL, cols), BF16), lax.empty((N_DEV, DEPTH, rows_out, D_MODEL), BF16)]
    own = [None] * DEPTH

    deferred, g_out = {}, {}

    def emit(l, name, grad, after=None):
        if name == "w_out" and l > 0:
            g_out[l] = grad
            return None
        if name == "w_in" and l == 0 and after is None:
            deferred["w_in"] = grad
            return None
        if name == "w_in":
            grad = grad(jnp.zeros((8, LANE), F32) if after is None else after)
        if l == 0:
            k = PROJ.index(name)
            src = _relayout_grad(grad) if name == "w_in" else grad.reshape(N_DEV, rows_out, D_MODEL)
            st = _exchange_start(f"scatter_start_0_{name}", [src], [lands[k]], "scatter", layer=0, after=after)
            scatters[name] = st
            lands[k] = st["lands"][0]
            return st["token"]
        srcs = [_relayout_grad(grad), g_out[l].reshape(N_DEV, rows_out, D_MODEL)]
        st = _exchange_start(f"scatter_start_{l}", srcs, lands, "scatter", layer=l, after=after)
        scatters[l] = st
        lands[:] = st["lands"]
        return st["token"]

    loss_own, dx, dmod, g = _local_step(x, mod, target, w, fetch, emit)

    def sharded(name, parts, own=None, **kw):
        return _adamw(parts, a[name], a["m_" + name], a["v_" + name], "adamw_" + name + kw.pop("tag", ""), own=own, **kw)

    g["b_ada"] = dmod
    small_own = _flatten_small(g, last=loss_own)
    small_st = _exchange_start("gather_small", [small_own], [lax.empty((N_DEV,) + small_own.shape, F32)], "chip",
                               after=dx)
    big = {}
    after = emit(0, "w_in", deferred["w_in"], after=small_st["token"]) + dx[0:8, 0:LANE]

    def own_slices(st):
        return [lax.dynamic_index_in_dim(s, me, 0, keepdims=False) for s in st["srcs"]]

    upper_sts = [scatters[l] for l in reversed(range(1, DEPTH))] + [scatters["w_out"]]
    lands[:] = _exchange_wait_all("scatter_wait_upper", upper_sts, lands, [(0, 1)] * (DEPTH - 1) + [(1,)], after)
    for l in range(1, DEPTH):
        own[l] = own_slices(scatters[l])
    own[0] = [None, own_slices(scatters["w_out"])[0]]
    big["w_out"] = sharded("w_out", lands[1], jnp.stack([own[l][1] for l in range(DEPTH)]))
    upper = sharded("w_in", lands[0], jnp.stack([own[l][0] for l in range(1, DEPTH)]), layers=(1, DEPTH), tag="_upper")
    after = upper[1][0, 0:8, 0:LANE] + big["w_out"][1][0, 0:8, 0:LANE]
    small = gather_pass("gather_small", small_st, after)[0]
    outs = _adamw(small[:, None], *[_flatten_small(a, p)[None] for p in ("", "m_", "v_")], "adamw_small",
                  own=small_own[None])
    res = [_split_small(o[0], a) for o in outs]
    losses = lax.dynamic_update_index_in_dim(small[:, -1, 0], loss_own, me, 0)
    loss = jnp.sum(losses)

    off = _small_rows(a)[0]["b_ada"][0]
    dmod_all = lax.dynamic_update_index_in_dim(small[:, off:off + DEPTH * 3 * D_MODEL // SMALL_ROW],
                                               dmod.reshape(-1, SMALL_ROW), me, 0)
    dmod_all = dmod_all.reshape(N_DEV, DEPTH, 3 * D_MODEL).transpose(1, 0, 2)
    dmod_cols = lax.dynamic_slice_in_dim(dmod_all, me * ADA_COLS, ADA_COLS, axis=2)
    dmod_pad = jnp.pad(dmod_cols, ((0, 0), (0, LANE - N_DEV), (0, 0)))
    ct_pad = jnp.pad(c_all.T, ((0, 0), (0, LANE - N_DEV)))
    big["w_ada"] = sharded("w_ada", _ada_bwd(ct_pad, dmod_pad)[None])
    g_conv = jnp.concatenate([g["lru_conv_w"].reshape(DEPTH, 4, N_DEV, 64), g["ssd_conv_w"].reshape(DEPTH, 4, N_DEV, 192)],
                             axis=-1).transpose(2, 0, 1, 3)
    conv_parts = _all_to_all(g_conv, "scatter_conv")
    big["lru_conv_w"] = sharded("lru_conv_w", conv_parts[..., :64])
    big["ssd_conv_w"] = sharded("ssd_conv_w", conv_parts[..., 64:])

    after = outs[1] + big["w_ada"][1][0, 0:1, 0:1]
    scatters["w_in"]["lands"] = [lands[0]]
    lands[0] = _exchange_wait("scatter_wait_0_w_in", scatters["w_in"], after)[0]
    big["w_in"] = sharded("w_in", lands[0], own_slices(scatters["w_in"])[0][None], layers=(0, 1), prev=upper)

    out = [loss, dx[None]]
    for k in range(4):
        out += [big[n][k] if n in big else res[k][n] for n in WEIGHTS]
    return tuple(out)
```

```python
import functools

import numpy as np
import jax
import jax.numpy as jnp
from jax import lax
from jax.experimental import pallas as pl
from jax.experimental.pallas import tpu as pltpu

F32 = jnp.float32
BF16 = jnp.bfloat16
SDS = jax.ShapeDtypeStruct

N_DEV = 8
DEPTH = 4
D_MODEL = 1024
D_INNER = 2048
EPS = 1e-6
LRU_W = 512
LRU_C = 8.0
HG_W = 512
HG_CHUNK = 64
HG_HEADS = 4
SSD_W = 1024
SSD_HEADS = 16
SSD_P = 64
SSD_N = 128
SSD_CHUNK = 128
SSD_CONV = 1536
N_IN = 5648
N_PAD = 5760
OFF_HG = 0
OFF_LRU = 2048
OFF_XBC = 3072
OFF_Z = 4608
LANE = 128
VMEM_LIMIT = 56 * 1024 * 1024
NEG = -1e30

ADAM_LR = 0.001
ADAM_B1 = 0.9
ADAM_B2 = 0.999
ADAM_EPS = 1e-08
ADAM_WD = 0.01
ADAM_STEP = 10


def _cp(sem=None):
    return pltpu.CompilerParams(dimension_semantics=sem, vmem_limit_bytes=VMEM_LIMIT)


def _dg(a, b, ca, cb):
    return lax.dot_general(a, b, (((ca,), (cb,)), ((), ())), preferred_element_type=F32)


def _mm(a, b):
    return _dg(a, b, 1, 0)


def _mm_nt(a, b):
    return _dg(a, b, 1, 1)


def _mm_tn(a, b):
    return _dg(a, b, 0, 0)


def _bf(x):
    return x.astype(BF16)


def _f(x):
    return x.astype(F32)


def _split3(x):
    hi = x.astype(BF16)
    r = x - hi.astype(F32)
    mid = r.astype(BF16)
    lo = (r - mid.astype(F32)).astype(BF16)
    return hi, mid, lo


def _sel_r(x, m):
    hi, mid, lo = _split3(x)
    return _mm(hi, m) + _mm(mid, m) + _mm(lo, m)


def _sel_l(m, x):
    hi, mid, lo = _split3(x)
    return _mm(m, hi) + _mm(m, mid) + _mm(m, lo)


def _sel_l2(m, x):
    hi = x.astype(BF16)
    lo = (x - hi.astype(F32)).astype(BF16)
    return _mm(m, hi) + _mm(m, lo)


def _sel_tn(x, m):
    hi, mid, lo = _split3(x)
    return _mm_tn(hi, m) + _mm_tn(mid, m) + _mm_tn(lo, m)


def _sigmoid(x):
    return 1.0 / (1.0 + jnp.exp(-x))


def _silu(x):
    return x * _sigmoid(x)


def _dsilu(x):
    s = _sigmoid(x)
    return s * (1.0 + x * (1.0 - s))


def _softplus(x):
    return jnp.maximum(x, 0.0) + jnp.log(1.0 + jnp.exp(-jnp.abs(x)))


def _expm1(z):
    series = z * (1.0 + z * (1.0 / 2) * (1.0 + z * (1.0 / 3) * (1.0 + z * (1.0 / 4) * (
        1.0 + z * (1.0 / 5) * (1.0 + z * (1.0 / 6) * (1.0 + z * (1.0 / 7)))))))
    return jnp.where(jnp.abs(z) < 0.3, series, jnp.exp(z) - 1.0)


def _iota(shape, dim):
    return lax.broadcasted_iota(jnp.int32, shape, dim)


def _last_row(x, rows):
    return jnp.sum(jnp.where(rows == x.shape[0] - 1, x, 0.0), axis=0, keepdims=True)


def _shift_down(x, d, rows, fill=0.0):
    return jnp.where(rows >= d, pltpu.roll(x, d, 0), fill)


def _shift_up(x, d, rows, fill=0.0):
    n = x.shape[0]
    return jnp.where(rows < n - d, pltpu.roll(x, n - d, 0), fill)


def _conv_fwd(x, cw_ref, cb_ref, rows):
    out = cb_ref[...] + cw_ref[pl.ds(3, 1), :] * x
    for k in range(3):
        out = out + cw_ref[pl.ds(k, 1), :] * _shift_down(x, 3 - k, rows)
    return out


def _conv_bwd(x, dco, cw_ref, rows):
    dx = cw_ref[pl.ds(3, 1), :] * dco
    dws = []
    for k in range(3):
        dx = dx + cw_ref[pl.ds(k, 1), :] * _shift_up(dco, 3 - k, rows)
        dws.append(jnp.sum(dco * _shift_down(x, 3 - k, rows), axis=0, keepdims=True))
    dws.append(jnp.sum(dco * x, axis=0, keepdims=True))
    return dx, dws, jnp.sum(dco, axis=0, keepdims=True)


def _vec(n):
    return pl.BlockSpec((1, n), lambda *_: (0, 0))


class _Row:
    def __init__(self, arr, l, n=None, c=0):
        self.arr, self.l, self.n, self.c = arr[:, None, :], l, n or arr.shape[1], c


def _spec(v):
    if isinstance(v, _Row):
        return pl.BlockSpec((None, 1, v.n), lambda *_: (v.l, 0, v.c))
    return _vec(v.shape[1])


def _arr(v):
    return v.arr if isinstance(v, _Row) else v


def _full(shape):
    nd = len(shape)
    return pl.BlockSpec(shape, lambda *_: (0,) * nd)


def _inproj_fwd(x, nw, scale, shift, w, tok):
    S = x.shape[0]
    tm = min(256, S)

    def body(x_ref, nw_ref, sc_ref, sh_ref, w_ref, tok_ref, u_ref, h_ref):
        del tok_ref
        xv = x_ref[...]
        inv = lax.rsqrt(jnp.mean(xv * xv, axis=-1, keepdims=True) + EPS)
        h = ((xv * inv) * nw_ref[...] * (1.0 + sc_ref[...]) + sh_ref[...]).astype(BF16)
        h_ref[...] = h
        u_ref[...] = _mm(h, w_ref[...])

    return pl.pallas_call(
        body, name="inproj_fwd", grid=(S // tm,),
        in_specs=[pl.BlockSpec((tm, D_MODEL), lambda i: (i, 0)), _spec(nw), _spec(scale), _spec(shift),
                  _full((D_MODEL, N_PAD)), pl.BlockSpec(memory_space=pl.ANY)],
        out_specs=[pl.BlockSpec((tm, N_PAD), lambda i: (i, 0)), pl.BlockSpec((tm, D_MODEL), lambda i: (i, 0))],
        out_shape=[SDS((S, N_PAD), F32), SDS((S, D_MODEL), BF16)],
        compiler_params=_cp(("parallel",)),
    )(x, _arr(nw), _arr(scale), _arr(shift), w, tok)


def _inproj_bwd_x(du, w, x, nw, scale, dxn, tok):
    S = x.shape[0]
    tm = min(256, S)

    def body(du_ref, w_ref, x_ref, nw_ref, sc_ref, dxn_ref, tok_ref, dx_ref, red_ref):
        del tok_ref

        @pl.when(pl.program_id(0) == 0)
        def _():
            red_ref[...] = jnp.zeros_like(red_ref)

        dh = _mm_nt(du_ref[...], w_ref[...])
        xv = x_ref[...]
        inv = lax.rsqrt(jnp.mean(xv * xv, axis=-1, keepdims=True) + EPS)
        xhat = xv * inv
        nwv = nw_ref[...]
        g1 = 1.0 + sc_ref[...]
        dxhat = dh * nwv * g1
        dx = inv * (dxhat - xhat * jnp.mean(dxhat * xhat, axis=-1, keepdims=True))
        dx_ref[...] = dxn_ref[...] + dx
        red_ref[0:1, :] += jnp.sum(dh, axis=0, keepdims=True)
        red_ref[1:2, :] += jnp.sum(dh * xhat * nwv, axis=0, keepdims=True)
        red_ref[2:3, :] += jnp.sum(dh * xhat * g1, axis=0, keepdims=True)

    row = pl.BlockSpec((tm, D_MODEL), lambda i: (i, 0))
    return pl.pallas_call(
        body, name="inproj_bwd_x", grid=(S // tm,),
        in_specs=[pl.BlockSpec((tm, N_PAD), lambda i: (i, 0)), _full((D_MODEL, N_PAD)), row, _spec(nw),
                  _spec(scale), row, pl.BlockSpec(memory_space=pl.ANY)],
        out_specs=[row, _full((8, D_MODEL))],
        out_shape=[SDS((S, D_MODEL), F32), SDS((8, D_MODEL), F32)],
        compiler_params=_cp(("arbitrary",)),
    )(du, w, x, _arr(nw), _arr(scale), dxn, tok)


def _inproj_bwd_w(h, du, tok):
    S = h.shape[0]
    tn = 640

    def body(h_ref, du_ref, tok_ref, gw_ref):
        del tok_ref
        gw_ref[...] = _mm_tn(h_ref[...], _bf(du_ref[...]))

    return pl.pallas_call(
        body, name="inproj_bwd_w", grid=(N_PAD // tn,),
        in_specs=[_full((S, D_MODEL)), pl.BlockSpec((S, tn), lambda j: (0, j)), pl.BlockSpec(memory_space=pl.ANY)],
        out_specs=pl.BlockSpec((D_MODEL, tn), lambda j: (0, j)),
        out_shape=SDS((D_MODEL, N_PAD), F32),
        compiler_params=_cp(("parallel",)),
    )(h, du, tok)


def _scan_block(a, b, rows):
    d = 1
    while d < a.shape[0]:
        a_s = _shift_down(a, d, rows, 1.0)
        b_s = _shift_down(b, d, rows, 0.0)
        b = a * b_s + b
        a = a * a_s
        d *= 2
    return a, b


def _rscan_block(c, g, rows):
    d = 1
    while d < c.shape[0]:
        c_s = _shift_up(c, d, rows, 1.0)
        g_s = _shift_up(g, d, rows, 0.0)
        g = g + c * g_s
        c = c * c_s
        d *= 2
    return c, g


LRU_BLOCK = 128


def _lru_gates(xa, wa_ref, ba_ref, wx_ref, bx_ref, lam_ref):
    sp = _softplus(-lam_ref[...])
    xb = _bf(xa)
    r = _sigmoid(_mm(xb, wa_ref[...]) + ba_ref[...])
    ig = _sigmoid(_mm(xb, wx_ref[...]) + bx_ref[...])
    la = -LRU_C * r * sp
    a = jnp.exp(la)
    mult = jnp.sqrt(-_expm1(2.0 * la))
    return sp, r, ig, la, a, mult


def _lru_specs(S, l):
    t128 = pl.BlockSpec((None, 1, LANE), lambda t: (l, 0, t))
    gate = pl.BlockSpec((None, None, LANE, LANE), lambda t: (l, t, 0, 0))
    return [pl.BlockSpec((S, 2 * LANE), lambda t: (0, OFF_LRU // (2 * LANE) + t)),
            pl.BlockSpec((None, 4, LANE), lambda t: (l, 0, t)), t128, gate, t128, gate, t128, t128]


def _lru_fwd(l, u, cw, cb, wa, ba, wx, bx, lam, ycat):
    S = u.shape[0]
    tb = min(LRU_BLOCK, S)

    def body(u_ref, cw_ref, cb_ref, wa_ref, ba_ref, wx_ref, bx_ref, lam_ref, ycat_in, ycat_ref, h_ref, a_scr, b_scr):
        del ycat_in
        rows = _iota((S, LANE), 0)
        xa = _conv_fwd(_f(u_ref[:, 0:LANE]), cw_ref, cb_ref, rows)
        _, _, ig, _, a, mult = _lru_gates(xa, wa_ref, ba_ref, wx_ref, bx_ref, lam_ref)
        a_scr[...] = a
        b_scr[...] = mult * (ig * xa)
        rows_b = _iota((tb, LANE), 0)

        def blk(j, hprev):
            sl = pl.ds(pl.multiple_of(j * tb, tb), tb)
            acum, hloc = _scan_block(a_scr[sl, :], b_scr[sl, :], rows_b)
            hf = hloc + acum * hprev
            h_ref[sl, :] = hf
            return _last_row(hf, rows_b)

        lax.fori_loop(0, S // tb, blk, jnp.zeros((1, LANE), F32))
        ycat_ref[...] = _bf(h_ref[...] * _silu(_f(u_ref[:, LANE:2 * LANE])))

    col = pl.BlockSpec((S, LANE), lambda t: (0, t))
    return pl.pallas_call(
        body, name="lru_fwd", grid=(LRU_W // LANE,),
        in_specs=_lru_specs(S, l) + [pl.BlockSpec(memory_space=pl.ANY)],
        out_specs=[col, col],
        out_shape=[SDS((S, D_INNER), BF16), SDS((S,LRU_W), F32)],
        scratch_shapes=[pltpu.VMEM((S, LANE), F32), pltpu.VMEM((S, LANE), F32)],
        input_output_aliases={8: 0},
        compiler_params=_cp(("parallel",)),
    )(u, cw, cb, wa, ba, wx, bx, lam, ycat)


def _lru_bwd(l, u, cw, cb, wa, ba, wx, bx, lam, h_lru, dycat, du):
    S = u.shape[0]
    tb = min(LRU_BLOCK, S)

    def body(u_ref, cw_ref, cb_ref, wa_ref, ba_ref, wx_ref, bx_ref, lam_ref, h_ref, dy_ref, du_in,
             du_ref, red_ref, gwa_ref, gwx_ref, c_scr, g_scr, l_scr):
        del du_in
        rows = _iota((S, LANE), 0)
        ax = _f(u_ref[:, 0:LANE])
        ag = _f(u_ref[:, LANE:2 * LANE])
        xa = _conv_fwd(ax, cw_ref, cb_ref, rows)
        sp, r, ig, la, a, mult = _lru_gates(xa, wa_ref, ba_ref, wx_ref, bx_ref, lam_ref)
        h = h_ref[...]
        dy = _f(dy_ref[...])
        du_ref[:, LANE:2 * LANE] = _bf(dy * h * _dsilu(ag))
        c_scr[...] = _shift_up(a, 1, rows, 0.0)
        g_scr[...] = dy * _silu(ag)
        rows_b = _iota((tb, LANE), 0)
        nb = S // tb

        def blk(jj, lnext):
            j = nb - 1 - jj
            sl = pl.ds(pl.multiple_of(j * tb, tb), tb)
            ccum, lloc = _rscan_block(c_scr[sl, :], g_scr[sl, :], rows_b)
            lam_t = lloc + ccum * lnext
            l_scr[sl, :] = lam_t
            return jnp.sum(jnp.where(rows_b == 0, lam_t, 0.0), axis=0, keepdims=True)

        lax.fori_loop(0, nb, blk, jnp.zeros((1, LANE), F32))
        db = l_scr[...]
        da = db * _shift_down(h, 1, rows)
        dmult = db * ig * xa
        dig = db * mult * xa
        dxa = db * mult * ig
        dla = da * a - dmult * (a * a) / mult
        dr = -LRU_C * sp * dla
        dsp = jnp.sum(-LRU_C * r * dla, axis=0, keepdims=True)
        dlam = -dsp * _sigmoid(-lam_ref[...])
        dzr = dr * r * (1.0 - r)
        dzi = dig * ig * (1.0 - ig)
        dzr_b, dzi_b, xa_b = _bf(dzr), _bf(dzi), _bf(xa)
        dxa = dxa + _mm_nt(dzr_b, wa_ref[...]) + _mm_nt(dzi_b, wx_ref[...])
        gwa_ref[...] = _mm_tn(xa_b, dzr_b)
        gwx_ref[...] = _mm_tn(xa_b, dzi_b)
        dax, dws, dcb = _conv_bwd(ax, dxa, cw_ref, rows)
        du_ref[:, 0:LANE] = _bf(dax)
        parts = dws + [dcb, jnp.sum(dzr, axis=0, keepdims=True), jnp.sum(dzi, axis=0, keepdims=True), dlam]
        for n, p in enumerate(parts):
            red_ref[pl.ds(n, 1), :] = p

    col = pl.BlockSpec((S, LANE), lambda t: (0, t))
    gw = pl.BlockSpec((None, LANE, LANE), lambda t: (t, 0, 0))
    return pl.pallas_call(
        body, name="lru_bwd", grid=(LRU_W // LANE,),
        in_specs=_lru_specs(S, l) + [col, col, pl.BlockSpec(memory_space=pl.ANY)],
        out_specs=[pl.BlockSpec((S, 2 * LANE), lambda t: (0, OFF_LRU // (2 * LANE) + t)),
                   pl.BlockSpec((8, LANE), lambda t: (0, t)), gw, gw],
        out_shape=[SDS((S, N_PAD), BF16), SDS((8, LRU_W), F32), SDS((4, LANE, LANE), F32), SDS((4, LANE, LANE), F32)],
        scratch_shapes=[pltpu.VMEM((S, LANE), F32)] * 3,
        input_output_aliases={10: 0},
        compiler_params=_cp(("parallel",)),
    )(u, cw, cb, wa, ba, wx, bx, lam, h_lru, dycat, du)


HG_LEVELS = 6


def _hg_consts():
    C = HG_CHUNK
    t = np.arange(C)[:, None]
    r = np.arange(C)[None, :]
    mats = []
    for l in range(HG_LEVELS):
        b = 1 << l
        upper = (t % (2 * b)) >= b
        anchor = (t // (2 * b)) * 2 * b + b - 1
        mats.append((upper & (r > anchor) & (r <= t)) | ((~upper) & (r > t) & (r <= anchor)))
    mats.append(r <= t)
    mats.append(r > t)
    return np.concatenate(mats, 0).astype(np.float32)


def _hg_factors(hf, lb, mall):
    s = _sigmoid(hf)
    f = lb + (1.0 - lb) * s
    lf = jnp.log(f)
    k = (1.0 - lb) * _sigmoid(-hf)
    e = jnp.exp(_sel_l(mall, lf))
    C = HG_CHUNK
    rows = _iota((C, HG_W), 0)
    eq, ek = [], []
    for l in range(HG_LEVELS):
        el = e[l * C:(l + 1) * C]
        eq.append(jnp.where((lax.shift_right_logical(rows, l) & 1) == 1, el, 0.0))
        ek.append(el - eq[l])
    ecum = e[HG_LEVELS * C:(HG_LEVELS + 1) * C]
    erem = e[(HG_LEVELS + 1) * C:(HG_LEVELS + 2) * C]
    return s, f, k, eq, ek, ecum, erem


def _hg_masks():
    C = HG_CHUNK
    ri, ci = _iota((C, C), 0), _iota((C, C), 1)
    rr = _iota((C, LANE), 0)
    gm = [(lax.shift_right_logical(ri, l + 1) == lax.shift_right_logical(ci, l + 1)).astype(F32)
          for l in range(HG_LEVELS)]
    up = [(lax.shift_right_logical(rr, l) & 1) == 1 for l in range(HG_LEVELS)]
    eye = (ri == ci).astype(F32)
    return gm, up, eye, rr


def _hg_scores(qh, kh, eq, ek, sl, gm, up, eye):
    del up
    qs, ks, qb, kb = [], [], [], []
    p = _mm_nt(_bf(qh), _bf(kh)) * eye
    for l in range(HG_LEVELS):
        qs.append(qh * eq[l][:, sl])
        ks.append(kh * ek[l][:, sl])
        qb.append(_bf(qs[l]))
        kb.append(_bf(ks[l]))
        p = p + _mm_nt(qb[l], kb[l]) * gm[l]
    return p, qs, ks, qb, kb


HG_SUB = 4


def _hg_fwd(u, lb, nw, mall, ycat):
    S = u.shape[0]
    C = HG_CHUNK
    n = S // C
    rows = HG_SUB * C

    def body(u_ref, lb_ref, nw_ref, mall_ref, ycat_in, ycat_ref, o_ref, st_ref, st):
        del ycat_in

        @pl.when(pl.program_id(0) == 0)
        def _():
            st[...] = jnp.zeros_like(st)

        gm, up, eye, rr = _hg_masks()
        for sub in range(HG_SUB):
            r = slice(sub * C, (sub + 1) * C)
            q = _silu(_f(u_ref[r, 0:512]))
            v = u_ref[r, 1024:1536]
            _, _, k, eq, ek, ecum, erem = _hg_factors(_f(u_ref[r, 512:1024]), lb_ref[...], mall_ref[...])
            for h in range(HG_HEADS):
                sl = slice(h * LANE, (h + 1) * LANE)
                qh, kh, vh = q[:, sl], k[:, sl], _bf(v[:, sl])
                p = _hg_scores(qh, kh, eq, ek, sl, gm, up, eye)[0]
                sth = st[h]
                st_ref[sub, h] = sth
                o_ref[r, sl] = _mm(_bf(p), vh) + _mm_nt(_bf(qh * ecum[:, sl]), _bf(sth))
                st[h] = sth * _last_row(ecum[:, sl], rr) + _mm_tn(vh, _bf(kh * erem[:, sl]))
            o = o_ref[r, :]
            inv = lax.rsqrt(jnp.mean(o * o, axis=-1, keepdims=True) + EPS)
            ycat_ref[r, :] = _bf((o * inv) * nw_ref[...] * _silu(_f(u_ref[r, 1536:2048])))

    return pl.pallas_call(
        body, name="hg_fwd", grid=(n // HG_SUB,),
        in_specs=[pl.BlockSpec((rows, 2048), lambda i: (i, 0)), _spec(lb), _spec(nw), _full(mall.shape),
                  pl.BlockSpec(memory_space=pl.ANY)],
        out_specs=[pl.BlockSpec((rows, HG_W), lambda i: (i, 1)), pl.BlockSpec((rows, HG_W), lambda i: (i, 0)),
                   pl.BlockSpec((HG_SUB, HG_HEADS, LANE, LANE), lambda i: (i, 0, 0, 0))],
        out_shape=[SDS((S, D_INNER), BF16), SDS((S,HG_W), F32), SDS((n, HG_HEADS, LANE, LANE), F32)],
        scratch_shapes=[pltpu.VMEM((HG_HEADS, LANE, LANE), F32)],
        input_output_aliases={4: 0},
        compiler_params=_cp(("arbitrary",)),
    )(u, _arr(lb), _arr(nw), mall, ycat)


def _hg_bwd(u, lb, nw, mall, mall_t, o_b, states, dycat, du):
    S = u.shape[0]
    C = HG_CHUNK
    n = S // C
    nb = n // HG_SUB
    rows = HG_SUB * C
    L2 = HG_LEVELS

    def body(u_ref, lb_ref, nw_ref, mall_ref, mallt_ref, o_ref, st_ref, dy_ref, du_in, du_ref, red_ref,
             dst, dlast_s, dq_s, dk_s, dex):
        del du_in

        @pl.when(pl.program_id(0) == 0)
        def _():
            dst[...] = jnp.zeros_like(dst)
            red_ref[...] = jnp.zeros_like(red_ref)

        lb = lb_ref[...]
        nwv = nw_ref[...]
        gm, up, eye, rr = _hg_masks()
        for sub in reversed(range(HG_SUB)):
            r = slice(sub * C, (sub + 1) * C)
            hq, hf, hg = _f(u_ref[r, 0:512]), _f(u_ref[r, 512:1024]), _f(u_ref[r, 1536:2048])
            q = _silu(hq)
            v = u_ref[r, 1024:1536]
            s, f, k, eq, ek, ecum, erem = _hg_factors(hf, lb, mall_ref[...])
            o = o_ref[r, :]
            dy = _f(dy_ref[r, :])
            inv = lax.rsqrt(jnp.mean(o * o, axis=-1, keepdims=True) + EPS)
            ohat = o * inv
            du_ref[r, 1536:2048] = _bf(dy * ohat * nwv * _dsilu(hg))
            dn = dy * _silu(hg)
            red_ref[0:1, :] += jnp.sum(dn * ohat, axis=0, keepdims=True)
            dohat = dn * nwv
            do = inv * (dohat - ohat * jnp.mean(dohat * ohat, axis=-1, keepdims=True))
            for h in range(HG_HEADS):
                sl = slice(h * LANE, (h + 1) * LANE)
                qh, kh, vh, doh = q[:, sl], k[:, sl], _bf(v[:, sl]), _bf(do[:, sl])
                p, qs, ks, qb, kb = _hg_scores(qh, kh, eq, ek, sl, gm, up, eye)
                st_f = st_ref[sub, h]
                sth = _bf(st_f)
                dsth = dst[h]
                dsth_b = _bf(dsth)
                qt = qh * ecum[:, sl]
                kt = kh * erem[:, sl]
                elast = _last_row(ecum[:, sl], rr)
                dp = _mm_nt(doh, vh)
                du_ref[r, 1024 + h * LANE:1024 + (h + 1) * LANE] = _bf(_mm_tn(_bf(p), doh) + _mm_nt(_bf(kt), dsth_b))
                dpe = _bf(dp * eye)
                dqt = _mm(doh, sth)
                dkt = _mm(vh, dsth_b)
                dq = dqt * ecum[:, sl] + _mm(dpe, _bf(kh))
                dk = dkt * erem[:, sl] + _mm_tn(dpe, _bf(qh))
                dex[sub, L2 * C:(L2 + 1) * C, sl] = dqt * qt
                dex[sub, (L2 + 1) * C:(L2 + 2) * C, sl] = dkt * kt
                for l in range(HG_LEVELS):
                    dpl = _bf(dp * gm[l])
                    dql = _mm(dpl, kb[l])
                    dkl = _mm_tn(dpl, qb[l])
                    dq = dq + dql * eq[l][:, sl]
                    dk = dk + dkl * ek[l][:, sl]
                    dex[sub, l * C:(l + 1) * C, sl] = dql * qs[l] + dkl * ks[l]
                dlast_s[sub, :, sl] = jnp.sum(dsth * st_f, axis=0, keepdims=True) * elast
                dst[h] = dsth * elast + _mm_tn(doh, _bf(qt))
                dq_s[sub, :, sl] = dq
                dk_s[sub, :, sl] = dk
            dq = dq_s[sub]
            dk = dk_s[sub]
            dlf = _sel_l2(mallt_ref[...], dex[sub]) + dlast_s[sub]
            du_ref[r, 0:512] = _bf(dq * _dsilu(hq))
            t = (1.0 - s) * (dlf / f - dk)
            du_ref[r, 512:1024] = _bf((1.0 - lb) * s * t)
            red_ref[1:2, :] += jnp.sum(t, axis=0, keepdims=True)

    rev = lambda i: (nb - 1 - i, 0)
    return pl.pallas_call(
        body, name="hg_bwd", grid=(nb,),
        in_specs=[pl.BlockSpec((rows, 2048), rev), _spec(lb), _spec(nw), _full(mall.shape), _full(mall_t.shape),
                  pl.BlockSpec((rows, HG_W), rev),
                  pl.BlockSpec((HG_SUB, HG_HEADS, LANE, LANE), lambda i: (nb - 1 - i, 0, 0, 0)),
                  pl.BlockSpec((rows, HG_W), lambda i: (nb - 1 - i, 1)), pl.BlockSpec(memory_space=pl.ANY)],
        out_specs=[pl.BlockSpec((rows, 2048), rev), pl.BlockSpec((8, HG_W), lambda i: (0, 0))],
        out_shape=[SDS((S, N_PAD), BF16), SDS((8, HG_W), F32)],
        scratch_shapes=[pltpu.VMEM((HG_HEADS, LANE, LANE), F32), pltpu.VMEM((HG_SUB, 1, HG_W), F32),
                        pltpu.VMEM((HG_SUB, C, HG_W), F32), pltpu.VMEM((HG_SUB, C, HG_W), F32),
                        pltpu.VMEM((HG_SUB, (L2 + 2) * C, HG_W), F32)],
        input_output_aliases={8: 0},
        compiler_params=_cp(("arbitrary",)),
    )(u, _arr(lb), _arr(nw), mall, mall_t, o_b, states, dycat, du)


def _ssdconv_fwd(l, u, cw, cb):
    S = u.shape[0]

    def body(u_ref, cw_ref, cb_ref, out_ref):
        rows = _iota((S, LANE), 0)
        out_ref[...] = _silu(_conv_fwd(_f(u_ref[...]), cw_ref, cb_ref, rows))

    return pl.pallas_call(
        body, name="ssdconv_fwd", grid=(SSD_CONV // LANE,),
        in_specs=[pl.BlockSpec((S, LANE), lambda t: (0, OFF_XBC // LANE + t)),
                  pl.BlockSpec((None, 4, LANE), lambda t: (l, 0, t)), pl.BlockSpec((None, 1, LANE), lambda t: (l, 0, t))],
        out_specs=pl.BlockSpec((S, LANE), lambda t: (0, t)),
        out_shape=SDS((S, SSD_CONV), F32),
        compiler_params=_cp(("parallel",)),
    )(u, cw, cb)


def _ssdconv_bwd(l, u, cw, cb, dxbc, du):
    S = u.shape[0]

    def body(u_ref, cw_ref, cb_ref, d_ref, du_in, du_ref, red_ref):
        del du_in
        rows = _iota((S, LANE), 0)
        x = _f(u_ref[...])
        dco = d_ref[...] * _dsilu(_conv_fwd(x, cw_ref, cb_ref, rows))
        dx, dws, dcb = _conv_bwd(x, dco, cw_ref, rows)
        du_ref[...] = _bf(dx)
        for n, p in enumerate(dws + [dcb]):
            red_ref[pl.ds(n, 1), :] = p
        red_ref[pl.ds(5, 3), :] = jnp.zeros((3, LANE), F32)

    ucol = pl.BlockSpec((S, LANE), lambda t: (0, OFF_XBC // LANE + t))
    return pl.pallas_call(
        body, name="ssdconv_bwd", grid=(SSD_CONV // LANE,),
        in_specs=[ucol, pl.BlockSpec((None, 4, LANE), lambda t: (l, 0, t)),
                  pl.BlockSpec((None, 1, LANE), lambda t: (l, 0, t)),
                  pl.BlockSpec((S, LANE), lambda t: (0, t)), pl.BlockSpec(memory_space=pl.ANY)],
        out_specs=[ucol, pl.BlockSpec((8, LANE), lambda t: (0, t))],
        out_shape=[SDS((S, N_PAD), BF16), SDS((8, SSD_CONV), F32)],
        input_output_aliases={4: 0},
        compiler_params=_cp(("parallel",)),
    )(u, cw, cb, dxbc, du)


SSD_SUB = 2


def _ssd_consts():
    e64 = np.zeros((LANE, SSD_W), np.float32)
    for h in range(SSD_HEADS):
        e64[h, h * SSD_P:(h + 1) * SSD_P] = 1.0
    T = SSD_CHUNK
    tril = (np.arange(T)[None, :] <= np.arange(T)[:, None]).astype(np.float32)
    return e64, tril, tril.T.copy()


def _ssd_common(zdt, bias_ref, alog_ref, tril, e64, cum_ref, cumt_ref):
    T = SSD_CHUNK
    lane = _iota((1, LANE), 1)
    a_neg = jnp.where(lane < SSD_HEADS, -jnp.exp(alog_ref[...]), 0.0)
    dtpre = zdt[:, SSD_W:SSD_W + LANE] + bias_ref[...]
    dt = _softplus(dtpre)
    cum = _sel_l(tril, dt * a_neg)
    cum_ref[...] = cum
    cumt_ref[...] = cum.T
    cum_x = _sel_r(cum, e64)
    last_x = _last_row(cum_x, _iota((T, SSD_W), 0))
    ecum_x = jnp.exp(cum_x)
    erem_x = jnp.exp(last_x - cum_x)
    elast_x = jnp.exp(last_x)
    dt_x = _sel_r(dt, e64)
    return a_neg, dtpre, dt, ecum_x, erem_x, elast_x, dt_x


def _ssd_decay(cum_ref, cumt_ref, h, causal):
    T = SSD_CHUNK
    diff = jnp.broadcast_to(cum_ref[:, pl.ds(h, 1)], (T, T)) - cumt_ref[pl.ds(h, 1), :]
    return jnp.exp(jnp.where(causal, diff, NEG))


def _group_norm_fwd(y1, nwv):
    outs, invs = [], []
    for g in range(2):
        seg = y1[:, g * 512:(g + 1) * 512]
        inv = lax.rsqrt(jnp.mean(seg * seg, axis=-1, keepdims=True) + EPS)
        outs.append(seg * inv * nwv[:, g * 512:(g + 1) * 512])
        invs.append(inv)
    return outs, invs


def _ssd_fwd(u, xbc, bias, alog, dskip_x, nw, consts, ycat):
    S = u.shape[0]
    T = SSD_CHUNK
    n = S // T
    rows = SSD_SUB * T
    e64, tril, _ = consts

    def body(u_ref, xbc_ref, bias_ref, alog_ref, dx_ref, nw_ref, e64_ref, tril_ref, ycat_in,
             ycat_ref, y_ref, st_ref, st, cumt, cum_e):
        del ycat_in

        @pl.when(pl.program_id(0) == 0)
        def _():
            st[...] = jnp.zeros_like(st)

        causal = _iota((T, T), 0) >= _iota((T, T), 1)
        lo = _iota((T, LANE), 1) < SSD_P
        for sub in range(SSD_SUB):
            r = slice(sub * T, (sub + 1) * T)
            zdt = _f(u_ref[r, :])
            z = zdt[:, 0:SSD_W]
            xs = xbc_ref[r, 0:SSD_W]
            cum_r, cumt_r = cum_e.at[sub], cumt.at[sub]
            _, _, _, ecum_x, erem_x, elast_x, dt_x = _ssd_common(
                zdt, bias_ref, alog_ref, tril_ref[...], e64_ref[...], cum_r, cumt_r)
            xdt = xs * dt_x
            xrem = xdt * erem_x
            st_ref[sub] = st[...]
            for g in range(2):
                gs = slice(g * 512, (g + 1) * 512)
                bg = _bf(xbc_ref[r, SSD_W + g * LANE:SSD_W + (g + 1) * LANE])
                cg = _bf(xbc_ref[r, SSD_W + 256 + g * LANE:SSD_W + 256 + (g + 1) * LANE])
                cb = _mm_nt(cg, bg)
                yin = _mm(cg, _bf(st[:, gs])) * ecum_x[:, gs]
                for j in range(4):
                    h0 = 8 * g + 2 * j
                    cs = slice(h0 * SSD_P, (h0 + 2) * SSD_P)
                    xp = xdt[:, cs]
                    s0 = _bf(cb * _ssd_decay(cum_r, cumt_r, h0, causal))
                    s1 = _bf(cb * _ssd_decay(cum_r, cumt_r, h0 + 1, causal))
                    y_ref[r, cs] = (_mm(s0, _bf(jnp.where(lo, xp, 0.0))) + _mm(s1, _bf(jnp.where(lo, 0.0, xp)))
                                    + yin[:, j * LANE:(j + 1) * LANE])
                st[:, gs] = st[:, gs] * elast_x[:, gs] + _mm_tn(bg, _bf(xrem[:, gs]))
            y1 = (y_ref[r, :] + dx_ref[...] * xs) * _silu(z)
            outs, _ = _group_norm_fwd(y1, nw_ref[...])
            for g in range(2):
                ycat_ref[r, g * 512:(g + 1) * 512] = _bf(outs[g])

    return pl.pallas_call(
        body, name="ssd_fwd", grid=(n // SSD_SUB,),
        in_specs=[pl.BlockSpec((rows, SSD_W + LANE), lambda i: (i, OFF_Z // (SSD_W + LANE))),
                  pl.BlockSpec((rows, SSD_CONV), lambda i: (i, 0)), _spec(bias), _spec(alog), _spec(dskip_x), _spec(nw),
                  _full(e64.shape), _full(tril.shape), pl.BlockSpec(memory_space=pl.ANY)],
        out_specs=[pl.BlockSpec((rows, SSD_W), lambda i: (i, 1)), pl.BlockSpec((rows, SSD_W), lambda i: (i, 0)),
                   pl.BlockSpec((SSD_SUB, SSD_N, SSD_W), lambda i: (i, 0, 0))],
        out_shape=[SDS((S, D_INNER), BF16), SDS((S,SSD_W), F32), SDS((n, SSD_N, SSD_W), F32)],
        scratch_shapes=[pltpu.VMEM((SSD_N, SSD_W), F32), pltpu.VMEM((SSD_SUB, LANE, T), F32),
                        pltpu.VMEM((SSD_SUB, T, LANE), F32)],
        input_output_aliases={8: 0},
        compiler_params=_cp(("arbitrary",)),
    )(u, xbc, _arr(bias), _arr(alog), _arr(dskip_x), _arr(nw), _bfc(e64), _bfc(tril), ycat)


def _ssd_bwd(u, xbc, bias, alog, dskip_x, nw, consts, y_ssd, states, dycat, du, tok):
    S = u.shape[0]
    T = SSD_CHUNK
    n = S // T
    e64, tril, triu = consts
    e64t = np.ascontiguousarray(e64.T)

    def chunk(u_ref, xbc_ref, bias_ref, alog_ref, dx_ref, nw_ref, e64_ref, e64t_ref, tril_ref, triu_ref,
              y_ref, st_ref, dy_ref, du_ref, dxbc_ref, red_ref, dst, dl_s, cumt, dxdt_s, dy0_s, gb_s, gc_s, cum_e, cs_s):
        zdt = _f(u_ref[...])
        z = zdt[:, 0:SSD_W]
        xs = xbc_ref[:, 0:SSD_W]
        a_neg, dtpre, dt, ecum_x, erem_x, elast_x, dt_x = _ssd_common(
            zdt, bias_ref, alog_ref, tril_ref[...], e64_ref[...], cum_e, cumt)
        causal = _iota((T, T), 0) >= _iota((T, T), 1)
        lo = _iota((T, LANE), 1) < SSD_P
        xdt = xs * dt_x
        xrem = xdt * erem_x
        y = y_ref[...]
        dxv = dx_ref[...]
        nwv = nw_ref[...]
        sz = _silu(z)
        y0 = y + dxv * xs
        y1 = y0 * sz
        for g in range(2):
            gs = slice(g * 512, (g + 1) * 512)
            seg = y1[:, gs]
            inv = lax.rsqrt(jnp.mean(seg * seg, axis=-1, keepdims=True) + EPS)
            shat = seg * inv
            dyg = _f(dy_ref[:, gs])
            red_ref[0:1, gs] += jnp.sum(dyg * shat, axis=0, keepdims=True)
            dsh = dyg * nwv[:, gs]
            dy1g = inv * (dsh - shat * jnp.mean(dsh * shat, axis=-1, keepdims=True))
            du_ref[:, gs] = _bf(dy1g * y0[:, gs] * _dsilu(z[:, gs]))
            dy0_s[:, gs] = dy1g * sz[:, gs]
        dy0 = dy0_s[...]
        red_ref[1:2, :] += jnp.sum(dy0 * xs, axis=0, keepdims=True)
        dyin = dy0 * ecum_x
        lane = _iota((T, LANE), 1)
        dcum = jnp.zeros((T, LANE), F32)

        def decay_grad(h, gm):
            cs_s[pl.ds(h, 1), :] = jnp.sum(gm, axis=0, keepdims=True)
            return jnp.where(lane == h, jnp.sum(gm, axis=1, keepdims=True), 0.0)

        for g in range(2):
            gs = slice(g * 512, (g + 1) * 512)
            bg = _bf(xbc_ref[:, SSD_W + g * LANE:SSD_W + (g + 1) * LANE])
            cg = _bf(xbc_ref[:, SSD_W + 256 + g * LANE:SSD_W + 256 + (g + 1) * LANE])
            cb = _mm_nt(cg, bg)
            dst_f, st_f = dst[:, gs], st_ref[:, gs]
            dstg = _bf(dst_f)
            stg = _bf(st_f)
            dyin_g = _bf(dyin[:, gs])
            xrem_g = _bf(xrem[:, gs])
            dcb = jnp.zeros((T, T), F32)
            dxr = _mm(bg, dstg)
            dxdt_s[:, gs] = dxr * erem_x[:, gs]
            gc_s[:, gs] = dxr * xrem[:, gs]
            gb_s[:, gs] = dyin[:, gs] * _mm(cg, stg)
            dl_s[:, gs] = jnp.sum(dst_f * st_f, axis=0, keepdims=True) * elast_x[:, gs]
            for j in range(4):
                h0 = 8 * g + 2 * j
                cs = slice(h0 * SSD_P, (h0 + 2) * SSD_P)
                xp = xdt[:, cs]
                dyp = dy0[:, cs]
                x_lo, x_hi = _bf(jnp.where(lo, xp, 0.0)), _bf(jnp.where(lo, 0.0, xp))
                d_lo, d_hi = _bf(jnp.where(lo, dyp, 0.0)), _bf(jnp.where(lo, 0.0, dyp))
                l0 = _ssd_decay(cum_e, cumt, h0, causal)
                l1 = _ssd_decay(cum_e, cumt, h0 + 1, causal)
                s0 = cb * l0
                s1 = cb * l1
                ds0 = _mm_nt(d_lo, x_lo)
                ds1 = _mm_nt(d_hi, x_hi)
                dcb = dcb + ds0 * l0 + ds1 * l1
                dxdt_s[:, cs] += _mm_tn(_bf(s0), d_lo) + _mm_tn(_bf(s1), d_hi)
                dcum = dcum + decay_grad(h0, ds0 * s0) + decay_grad(h0 + 1, ds1 * s1)
            dcb_b = _bf(dcb)
            dxbc_ref[:, SSD_W + g * LANE:SSD_W + (g + 1) * LANE] = _mm_tn(dcb_b, cg) + _mm_nt(xrem_g, dstg)
            dxbc_ref[:, SSD_W + 256 + g * LANE:SSD_W + 256 + (g + 1) * LANE] = _mm(dcb_b, bg) + _mm_nt(dyin_g, stg)
            dst[:, gs] = dst_f * elast_x[:, gs] + _mm_tn(cg, dyin_g)
        dxdt = dxdt_s[...]
        dxbc_ref[:, 0:SSD_W] = dxdt * dt_x + dy0 * dxv
        e64t = e64t_ref[...]
        gc = gc_s[...]
        dlast_x = jnp.sum(gc, axis=0, keepdims=True) + dl_s[...]
        dlast = jnp.max(_sel_r(jnp.broadcast_to(dlast_x, (8, SSD_W)), e64t), axis=0, keepdims=True)
        dcum = (dcum - cs_s[...].T + _sel_r(gb_s[...] - gc, e64t)
                + jnp.where(_iota((T, LANE), 0) == T - 1, dlast, 0.0))
        dda = _sel_l(triu_ref[...], dcum)
        ddt = dda * a_neg + _sel_r(dxdt * xs, e64t)
        ddtpre = ddt * _sigmoid(dtpre)
        du_ref[:, SSD_W:SSD_W + LANE] = _bf(jnp.where(lane < SSD_HEADS, ddtpre, 0.0))
        red_ref[2:3, 0:LANE] += jnp.sum(ddtpre, axis=0, keepdims=True)
        red_ref[3:4, 0:LANE] += jnp.sum(dda * dt, axis=0, keepdims=True)

    def body(u_ref, xbc_ref, bias_ref, alog_ref, dx_ref, nw_ref, e64_ref, e64t_ref, tril_ref, triu_ref,
             y_ref, st_ref, dy_ref, du_in, tok_ref, du_ref, dxbc_ref, red_ref, dst, *scratch):
        del du_in, tok_ref

        @pl.when(pl.program_id(0) == 0)
        def _():
            dst[...] = jnp.zeros_like(dst)
            red_ref[...] = jnp.zeros_like(red_ref)
            scratch[-1][...] = jnp.zeros_like(scratch[-1])

        for sub in reversed(range(SSD_SUB)):
            rs = pl.ds(sub * T, T)
            chunk(u_ref.at[rs], xbc_ref.at[rs], bias_ref, alog_ref, dx_ref, nw_ref, e64_ref, e64t_ref, tril_ref, triu_ref,
                  y_ref.at[rs], st_ref.at[sub], dy_ref.at[rs], du_ref.at[rs], dxbc_ref.at[rs], red_ref, dst,
                  *[s.at[sub] for s in scratch])

    nb = n // SSD_SUB
    rows = SSD_SUB * T
    rev = lambda i: (nb - 1 - i, 0)
    sub_scratch = [(1, SSD_W), (LANE, T)] + [(T, SSD_W)] * 4 + [(T, LANE), (LANE, T)]
    return pl.pallas_call(
        body, name="ssd_bwd", grid=(nb,),
        in_specs=[pl.BlockSpec((rows, SSD_W + LANE), lambda i: (nb - 1 - i, OFF_Z // (SSD_W + LANE))),
                  pl.BlockSpec((rows, SSD_CONV), rev), _spec(bias), _spec(alog), _spec(dskip_x), _spec(nw),
                  _full(e64.shape), _full(e64t.shape), _full(tril.shape), _full(triu.shape),
                  pl.BlockSpec((rows, SSD_W), rev), pl.BlockSpec((SSD_SUB, SSD_N, SSD_W), lambda i: (nb - 1 - i, 0, 0)),
                  pl.BlockSpec((rows, SSD_W), lambda i: (nb - 1 - i, 1)), pl.BlockSpec(memory_space=pl.ANY),
                  pl.BlockSpec(memory_space=pl.ANY)],
        out_specs=[pl.BlockSpec((rows, SSD_W + LANE), lambda i: (nb - 1 - i, OFF_Z // (SSD_W + LANE))),
                   pl.BlockSpec((rows, SSD_CONV), rev), pl.BlockSpec((8, SSD_W), lambda i: (0, 0))],
        out_shape=[SDS((S, N_PAD), BF16), SDS((S, SSD_CONV), F32), SDS((8, SSD_W), F32)],
        scratch_shapes=[pltpu.VMEM((SSD_N, SSD_W), F32)] + [pltpu.VMEM((SSD_SUB,) + s, F32) for s in sub_scratch],
        input_output_aliases={13: 0},
        compiler_params=_cp(("arbitrary",)),
    )(u, xbc, _arr(bias), _arr(alog), _arr(dskip_x), _arr(nw), _bfc(e64), _bfc(e64t), _bfc(tril), _bfc(triu), y_ssd,
      states, dycat, du, tok)


def _bfc(a):
    return jnp.asarray(a, BF16)


def _outproj_fwd(ycat, wo, x, gate, tok):
    S = x.shape[0]
    tm = min(512, S)

    def body(yc_ref, wo_ref, x_ref, g_ref, tok_ref, xn_ref, y_ref):
        del tok_ref
        y = _mm(_bf(yc_ref[...]), wo_ref[...])
        y_ref[...] = y
        xn_ref[...] = x_ref[...] + g_ref[...] * y

    row = pl.BlockSpec((tm, D_MODEL), lambda i: (i, 0))
    return pl.pallas_call(
        body, name="outproj_fwd", grid=(S // tm,),
        in_specs=[pl.BlockSpec((tm, D_INNER), lambda i: (i, 0)), _full((D_INNER, D_MODEL)), row, _spec(gate),
                  pl.BlockSpec(memory_space=pl.ANY)],
        out_specs=[row, row],
        out_shape=[SDS((S, D_MODEL), F32), SDS((S, D_MODEL), F32)],
        compiler_params=_cp(("parallel",)),
    )(ycat, wo, x, _arr(gate), tok)


def _outproj_bwd(dxn, y, gate, ycat, wo):
    S = dxn.shape[0]
    tm = min(512, S)

    def body(dx_ref, y_ref, g_ref, yc_ref, wo_ref, dyc_ref, gwo_ref, dg_ref, acc):
        @pl.when(pl.program_id(0) == 0)
        def _():
            acc[...] = jnp.zeros_like(acc)
            dg_ref[...] = jnp.zeros_like(dg_ref)

        dxv = dx_ref[...]
        dy = _bf(dxv * g_ref[...])
        dg_ref[0:1, :] += jnp.sum(dxv * y_ref[...], axis=0, keepdims=True)
        dyc_ref[...] = _mm_nt(dy, wo_ref[...])
        acc[...] += _mm_tn(_bf(yc_ref[...]), dy)

        @pl.when(pl.program_id(0) == pl.num_programs(0) - 1)
        def _():
            gwo_ref[...] = acc[...].astype(BF16)

    row = pl.BlockSpec((tm, D_MODEL), lambda i: (i, 0))
    wide = pl.BlockSpec((tm, D_INNER), lambda i: (i, 0))
    return pl.pallas_call(
        body, name="outproj_bwd", grid=(S // tm,),
        in_specs=[row, row, _spec(gate), wide, _full((D_INNER, D_MODEL))],
        out_specs=[wide, _full((D_INNER, D_MODEL)), _full((8, D_MODEL))],
        out_shape=[SDS((S, D_INNER), F32), SDS((D_INNER, D_MODEL), BF16), SDS((8, D_MODEL), F32)],
        scratch_shapes=[pltpu.VMEM((D_INNER, D_MODEL), F32)],
        compiler_params=_cp(("arbitrary",)),
    )(dxn, y, _arr(gate), ycat, wo)


def _loss_head(x, fw, target):
    S = x.shape[0]
    tm = min(512, S)

    def body(x_ref, fw_ref, t_ref, dx_ref, red_ref):
        @pl.when(pl.program_id(0) == 0)
        def _():
            red_ref[...] = jnp.zeros_like(red_ref)

        xv = x_ref[...]
        fwv = fw_ref[...]
        inv = lax.rsqrt(jnp.mean(xv * xv, axis=-1, keepdims=True) + EPS)
        xhat = xv * inv
        err = xhat * fwv - t_ref[...]
        col = jnp.sum(err * err, axis=0, keepdims=True)
        red_ref[1:2, :] += jnp.broadcast_to(jnp.sum(col, axis=1, keepdims=True) * (0.5 / D_MODEL), (1, D_MODEL))
        dy = err * (1.0 / D_MODEL)
        red_ref[0:1, :] += jnp.sum(dy * xhat, axis=0, keepdims=True)
        dxhat = dy * fwv
        dx_ref[...] = inv * (dxhat - xhat * jnp.mean(dxhat * xhat, axis=-1, keepdims=True))

    row = pl.BlockSpec((tm, D_MODEL), lambda i: (i, 0))
    return pl.pallas_call(
        body, name="loss_head", grid=(S // tm,),
        in_specs=[row, _vec(D_MODEL), row],
        out_specs=[row, _full((8, D_MODEL))],
        out_shape=[SDS((S, D_MODEL), F32), SDS((8, D_MODEL), F32)],
        compiler_params=_cp(("arbitrary",)),
    )(x, fw, target)


ADA_COLS = 3 * D_MODEL // N_DEV


def _ada_fwd(c_all, w_ada, b_cols):
    def body(c_ref, w_ref, b_ref, out_ref):
        out_ref[...] = _mm(_bf(_silu(c_ref[...])), _bf(w_ref[...])) + b_ref[...]

    return pl.pallas_call(
        body, name="ada_fwd", grid=(DEPTH,),
        in_specs=[_full((N_DEV, D_MODEL)), pl.BlockSpec((None, D_MODEL, ADA_COLS), lambda l: (l, 0, 0)),
                  pl.BlockSpec((None, 1, ADA_COLS), lambda l: (l, 0, 0))],
        out_specs=pl.BlockSpec((None, N_DEV, ADA_COLS), lambda l: (l, 0, 0)),
        out_shape=SDS((DEPTH, N_DEV, ADA_COLS), F32),
        compiler_params=_cp(("parallel",)),
    )(c_all, w_ada, b_cols)


def _ada_bwd(ct_pad, dmod_pad):
    def body(c_ref, d_ref, out_ref):
        out_ref[...] = _mm(_bf(_silu(c_ref[...])), _bf(d_ref[...]))

    return pl.pallas_call(
        body, name="ada_bwd", grid=(DEPTH,),
        in_specs=[_full((D_MODEL, LANE)), pl.BlockSpec((None, LANE, ADA_COLS), lambda l: (l, 0, 0))],
        out_specs=pl.BlockSpec((None, D_MODEL, ADA_COLS), lambda l: (l, 0, 0)),
        out_shape=SDS((DEPTH, D_MODEL, ADA_COLS), F32),
        compiler_params=_cp(("parallel",)),
    )(ct_pad, dmod_pad)


def _adamw(parts, w, m, v, name, own=None, layers=None, prev=None):
    n, L, R, C = parts.shape
    lo, hi = layers or (0, L)
    tr = R
    while tr * C * 4 > (1 << 20) and tr % 16 == 0:
        tr //= 2
    first = 1 if own is None else 2

    def body(*refs):
        p_ref = refs[0]
        w_ref, m_ref, v_ref = refs[first:first + 3]
        g_ref, d_ref, mo_ref, vo_ref = refs[-4:]

        def part(k):
            if own is None:
                return p_ref[k].astype(F32)
            me = 4 * lax.axis_index("x") + 2 * lax.axis_index("y") + lax.axis_index("c")
            return jnp.where(me == k, refs[1][...], p_ref[k]).astype(F32)

        g = part(0)
        for k in range(1, n):
            g = g + part(k)
        mn = ADAM_B1 * m_ref[...] + (1.0 - ADAM_B1) * g
        vn = ADAM_B2 * v_ref[...] + (1.0 - ADAM_B2) * (g * g)
        m_hat = mn / (1.0 - ADAM_B1 ** ADAM_STEP)
        v_hat = vn / (1.0 - ADAM_B2 ** ADAM_STEP)
        g_ref[...] = g
        d_ref[...] = -ADAM_LR * (m_hat / (jnp.sqrt(v_hat) + ADAM_EPS) + ADAM_WD * w_ref[...])
        mo_ref[...] = mn
        vo_ref[...] = vn

    blk = pl.BlockSpec((None, tr, C), lambda l, i: (lo + l, i, 0))
    own_blk = [] if own is None else [pl.BlockSpec((None, tr, C), lambda l, i: (l, i, 0))]
    n_blk = 3 if own is None else 4
    return pl.pallas_call(
        body, name=name, grid=(hi - lo, R // tr),
        in_specs=[pl.BlockSpec((n, None, tr, C), lambda l, i: (0, lo + l, i, 0))] + own_blk + [blk] * 3
        + ([] if prev is None else [ANY] * 4),
        out_specs=[blk] * 4,
        out_shape=[SDS((L, R, C), F32)] * 4,
        input_output_aliases={} if prev is None else {1 + n_blk + k: k for k in range(4)},
        compiler_params=_cp(("parallel", "parallel")),
    )(parts, *([] if own is None else [own]), w, m, v, *([] if prev is None else prev))


MESH = pl.DeviceIdType.MESH
ANY = pl.BlockSpec(memory_space=pl.ANY)


def _all_gather(v, name):
    def body(v_ref, out_ref, send_sems, recv_sems, local_sem):
        x, y, c = lax.axis_index("x"), lax.axis_index("y"), lax.axis_index("c")
        me, sibling = (x, y, c), (x, y, 1 - c)
        chips = [(1 - x, y), (x, 1 - y), (1 - x, 1 - y)]

        def slot(px, py, pc):
            return out_ref.at[4 * px + 2 * py + pc]

        def copy(k, block, to, src=None):
            return pltpu.make_async_remote_copy(
                src_ref=slot(*block) if src is None else src, dst_ref=slot(*block),
                send_sem=send_sems.at[k], recv_sem=recv_sems.at[k], device_id=to, device_id_type=MESH)

        mine = pltpu.make_async_copy(v_ref, slot(*me), local_sem)
        mine.start()
        first = [copy(0, me, sibling, src=v_ref)]
        first += [copy(1 + j, me, (*chip, c), src=v_ref) for j, chip in enumerate(chips)]
        for cp in first:
            cp.start()
        passed = [copy(4 + j, (*chip, c), sibling) for j, chip in enumerate(chips)]
        for j, chip in enumerate(chips):
            copy(1 + j, (*chip, c), me).wait_recv()
            passed[j].start()
        copy(0, sibling, me).wait_recv()
        for j, chip in enumerate(chips):
            copy(4 + j, (*chip, 1 - c), me).wait_recv()
        for cp in first + passed:
            cp.wait_send()
        mine.wait()

    return pl.pallas_call(
        body, name=name, in_specs=[ANY], out_specs=ANY,
        out_shape=SDS((N_DEV,) + v.shape, v.dtype),
        scratch_shapes=[pltpu.SemaphoreType.DMA((7,)), pltpu.SemaphoreType.DMA((7,)), pltpu.SemaphoreType.DMA],
    )(v)


def _all_to_all(v, name):
    def body(v_ref, out_ref, send_sems, recv_sems, local_sem):
        x, y, c = lax.axis_index("x"), lax.axis_index("y"), lax.axis_index("c")
        mine_idx = 4 * x + 2 * y + c
        mine = pltpu.make_async_copy(v_ref.at[mine_idx], out_ref.at[mine_idx], local_sem)
        mine.start()
        sends, recvs = [], []
        for k in range(1, N_DEV):
            px = 1 - x if k & 4 else x
            py = 1 - y if k & 2 else y
            pc = 1 - c if k & 1 else c
            peer_idx = 4 * px + 2 * py + pc
            sems = dict(send_sem=send_sems.at[k - 1], recv_sem=recv_sems.at[k - 1], device_id=(px, py, pc),
                        device_id_type=MESH)
            sends.append(pltpu.make_async_remote_copy(src_ref=v_ref.at[peer_idx], dst_ref=out_ref.at[mine_idx], **sems))
            recvs.append(pltpu.make_async_remote_copy(src_ref=v_ref.at[peer_idx], dst_ref=out_ref.at[peer_idx], **sems))
        for cp in sends:
            cp.start()
        for cp in recvs:
            cp.wait_recv()
        for cp in sends:
            cp.wait_send()
        mine.wait()

    return pl.pallas_call(
        body, name=name, in_specs=[ANY], out_specs=ANY,
        out_shape=SDS(v.shape, v.dtype),
        scratch_shapes=[pltpu.SemaphoreType.DMA((7,)), pltpu.SemaphoreType.DMA((7,)), pltpu.SemaphoreType.DMA],
    )(v)


HBM_SPEC = pl.BlockSpec(memory_space=pltpu.HBM)
SEM_SPEC = pl.BlockSpec(memory_space=pltpu.SEMAPHORE)
EFFECT = pltpu.SideEffectType.DATAFLOW_SIDE_EFFECTING


EXCHANGE_PEERS = {"gather": range(1, N_DEV), "scatter": range(1, N_DEV), "chip": (1, 2, 4, 6), "pass": (2, 4, 6)}


def _exchange_copies(srcs, lands, send_sems, recv_sems, mode, layer):
    x, y, c = lax.axis_index("x"), lax.axis_index("y"), lax.axis_index("c")
    me = 4 * x + 2 * y + c
    copies = []
    for a, (src, land) in enumerate(zip(srcs, lands)):
        for k in EXCHANGE_PEERS[mode]:
            px = 1 - x if k & 4 else x
            py = 1 - y if k & 2 else y
            pc = 1 - c if k & 1 else c
            peer = 4 * px + 2 * py + pc
            if mode == "scatter":
                s, d, to = src.at[peer], land.at[me, layer], (px, py, pc)
            elif mode == "pass":
                s, d, to = land.at[peer], land.at[peer], (x, y, 1 - c)
            else:
                s, d, to = src, land.at[me], (px, py, pc)
            n = 7 * a + k - 1
            copies.append(pltpu.make_async_remote_copy(
                src_ref=s, dst_ref=d, send_sem=send_sems.at[n], recv_sem=recv_sems.at[n], device_id=to,
                device_id_type=MESH))
    return copies


def _exchange_start(name, srcs, lands, mode, layer=0, after=None):
    n = len(srcs)

    def body(*refs):
        send_sems, recv_sems = refs[-2 * n - 3], refs[-2 * n - 2]
        for cp in _exchange_copies(refs[:n], refs[n:2 * n], send_sems, recv_sems, mode, layer):
            cp.start()
        refs[-1][...] = jnp.zeros_like(refs[-1])

    arrays = list(srcs) + list(lands)
    sems = pltpu.SemaphoreType.DMA((7 * n,))
    out = pl.pallas_call(
        body, name=name,
        out_shape=(sems, sems, *[pltpu.HBM(v.shape, v.dtype) for v in arrays], SDS((8, LANE), F32)),
        in_specs=[HBM_SPEC] * (2 * n) + ([ANY] if after is not None else []),
        out_specs=(SEM_SPEC, SEM_SPEC, *[HBM_SPEC] * (2 * n), pl.BlockSpec(memory_space=pltpu.VMEM)),
        input_output_aliases={i: 2 + i for i in range(2 * n)},
        compiler_params=pltpu.CompilerParams(has_side_effects=EFFECT),
    )(*[pltpu.with_memory_space_constraint(v, pltpu.HBM) for v in arrays], *([after] if after is not None else []))
    return dict(sems=out[:2], srcs=out[2:2 + n], lands=out[2 + n:2 + 2 * n], token=out[-1], mode=mode,
                layer=layer)


def _exchange_wait(name, st, after, also=()):
    n = len(st["srcs"])

    def body(*refs):
        send_sems, recv_sems = refs[2 * n], refs[2 * n + 1]
        for cp in _exchange_copies(refs[:n], refs[n:2 * n], send_sems, recv_sems, st["mode"], st["layer"]):
            cp.wait_send()
            cp.wait_recv()

    arrays = list(st["srcs"]) + list(st["lands"])
    out = pl.pallas_call(
        body, name=name,
        out_shape=tuple(pltpu.HBM(v.shape, v.dtype) for v in arrays),
        in_specs=[HBM_SPEC] * (2 * n) + [SEM_SPEC, SEM_SPEC] + [ANY] * (1 + len(also)),
        out_specs=tuple([HBM_SPEC] * (2 * n)),
        input_output_aliases={i: i for i in range(2 * n)},
        compiler_params=pltpu.CompilerParams(has_side_effects=EFFECT),
    )(*arrays, *st["sems"], after, *also)
    st["srcs"] = out[:n]
    return out[n:]


def _exchange_relay(name, st, after, also=()):
    n = len(st["srcs"])

    def body(*refs):
        def copies(send_sems, recv_sems, mode):
            keys = [(a, k) for a in range(n) for k in EXCHANGE_PEERS[mode]]
            return dict(zip(keys, _exchange_copies(refs[:n], refs[n:2 * n], send_sems, recv_sems, mode, st["layer"]),
                            strict=True))

        arrived = copies(refs[2 * n], refs[2 * n + 1], st["mode"])
        onward = copies(refs[-2 * n - 3], refs[-2 * n - 2], "pass")
        for key, cp in arrived.items():
            cp.wait_recv()
            if key in onward:
                onward[key].start()
        for cp in arrived.values():
            cp.wait_send()
        refs[-1][...] = jnp.zeros_like(refs[-1])

    arrays = list(st["srcs"]) + list(st["lands"])
    sems = pltpu.SemaphoreType.DMA((7 * n,))
    out = pl.pallas_call(
        body, name=name,
        out_shape=(sems, sems, *[pltpu.HBM(v.shape, v.dtype) for v in arrays], SDS((8, LANE), F32)),
        in_specs=[HBM_SPEC] * (2 * n) + [SEM_SPEC, SEM_SPEC] + [ANY] * (1 + len(also)),
        out_specs=(SEM_SPEC, SEM_SPEC, *[HBM_SPEC] * (2 * n), pl.BlockSpec(memory_space=pltpu.VMEM)),
        input_output_aliases={i: 2 + i for i in range(2 * n)},
        compiler_params=pltpu.CompilerParams(has_side_effects=EFFECT),
    )(*arrays, *st["sems"], after, *also)
    return dict(sems=out[:2], srcs=out[2:2 + n], lands=out[2 + n:2 + 2 * n], token=out[-1], mode="pass",
                layer=st["layer"])


def _exchange_wait_all(name, sts, lands, ids, after):
    counts = [len(st["srcs"]) for st in sts]
    ns, nl = sum(counts), len(lands)

    def body(*refs):
        at = 0
        for e, st in enumerate(sts):
            own_lands = [refs[ns + i] for i in ids[e]]
            send_sems, recv_sems = refs[ns + nl + 2 * e], refs[ns + nl + 2 * e + 1]
            for cp in _exchange_copies(refs[at:at + counts[e]], own_lands, send_sems, recv_sems, st["mode"],
                                       st["layer"]):
                cp.wait_send()
                cp.wait_recv()
            at += counts[e]

    arrays = [s for st in sts for s in st["srcs"]] + list(lands)
    out = pl.pallas_call(
        body, name=name,
        out_shape=tuple(pltpu.HBM(v.shape, v.dtype) for v in arrays),
        in_specs=[HBM_SPEC] * len(arrays) + [SEM_SPEC] * (2 * len(sts)) + [ANY],
        out_specs=tuple([HBM_SPEC] * len(arrays)),
        input_output_aliases={i: i for i in range(len(arrays))},
        compiler_params=pltpu.CompilerParams(has_side_effects=EFFECT),
    )(*arrays, *[s for st in sts for s in st["sems"]], after)
    at = 0
    for e, st in enumerate(sts):
        st["srcs"] = out[at:at + counts[e]]
        at += counts[e]
    return list(out[ns:])


_IN_PIECES = ([(1024, 3072)]
              + [r for t in range(4) for r in ((LANE * t, LANE * (t + 1)), (512 + LANE * t, 512 + LANE * (t + 1)))]
              + [(4096, 5632), (3072, 4096), (5632, 5648)])


def _permute_in(w):
    pad = jnp.zeros(w.shape[:-1] + (N_PAD - N_IN,), w.dtype)
    return jnp.concatenate([w[..., a:b] for a, b in _IN_PIECES] + [pad], axis=-1)


def _unpermute_in(g):
    ax = [g[..., OFF_LRU + 2 * LANE * t:OFF_LRU + 2 * LANE * t + LANE] for t in range(4)]
    ag = [g[..., OFF_LRU + 2 * LANE * t + LANE:OFF_LRU + 2 * LANE * (t + 1)] for t in range(4)]
    return jnp.concatenate(ax + ag + [g[..., 0:2048], g[..., OFF_Z:OFF_Z + SSD_W], g[..., OFF_XBC:OFF_XBC + SSD_CONV],
                                      g[..., OFF_Z + SSD_W:OFF_Z + SSD_W + SSD_HEADS]], axis=-1)


SHARD_COLS = N_IN // N_DEV


def _in_segments():
    segs, pos = [], 0
    for a, b in _IN_PIECES:
        for i in range(N_DEV):
            lo, hi = max(a, SHARD_COLS * i), min(b, SHARD_COLS * (i + 1))
            if lo < hi:
                segs.append((i, lo - SHARD_COLS * i, hi - lo, pos + lo - a))
        pos += b - a
    return segs


RELAYOUT_ROWS = 512


def _relayout_in(land, own):
    def body(land_ref, own_ref, out_ref):
        me = 4 * lax.axis_index("x") + 2 * lax.axis_index("y") + lax.axis_index("c")
        out_ref[:, N_IN:N_PAD] = jnp.zeros((RELAYOUT_ROWS, N_PAD - N_IN), BF16)
        for i, j, wd, p in _in_segments():
            out_ref[:, p:p + wd] = jnp.where(me == i, own_ref[:, j:j + wd], land_ref[i, :, j:j + wd])

    return pl.pallas_call(
        body, name="relayout_in", grid=(D_MODEL // RELAYOUT_ROWS,),
        in_specs=[pl.BlockSpec((N_DEV, RELAYOUT_ROWS, SHARD_COLS), lambda r: (0, r, 0)),
                  pl.BlockSpec((RELAYOUT_ROWS, SHARD_COLS), lambda r: (r, 0))],
        out_specs=pl.BlockSpec((RELAYOUT_ROWS, N_PAD), lambda r: (r, 0)),
        out_shape=SDS((D_MODEL, N_PAD), BF16),
        compiler_params=_cp(("parallel",)),
    )(land, own)


def _relayout_grad(g):
    def body(g_ref, out_ref):
        for i, j, wd, p in _in_segments():
            out_ref[i, :, j:j + wd] = g_ref[:, p:p + wd].astype(BF16)

    return pl.pallas_call(
        body, name="relayout_grad", grid=(D_MODEL // RELAYOUT_ROWS,),
        in_specs=[pl.BlockSpec((RELAYOUT_ROWS, N_PAD), lambda r: (r, 0))],
        out_specs=pl.BlockSpec((N_DEV, RELAYOUT_ROWS, SHARD_COLS), lambda r: (0, r, 0)),
        out_shape=SDS((N_DEV, D_MODEL, SHARD_COLS), BF16),
        compiler_params=_cp(("parallel",)),
    )(g)


def _block_diag(w):
    w4 = w.reshape(DEPTH, 4, 2, 64, 64)
    z = jnp.zeros((DEPTH, 4, 64, 64), w.dtype)
    top = jnp.concatenate([w4[:, :, 0], z], axis=-1)
    bot = jnp.concatenate([z, w4[:, :, 1]], axis=-1)
    return jnp.concatenate([top, bot], axis=2).astype(BF16)


def _diag_blocks(g):
    return jnp.stack([g[:, :, :64, :64], g[:, :, 64:, 64:]], axis=2).reshape(DEPTH, 8, 64, 64)


def _pad_lanes(v):
    return jnp.pad(v, ((0, 0), (0, LANE - v.shape[1])))


def _lower_bounds(logits):
    p = jax.nn.softmax(logits, axis=0)
    return p, jnp.cumsum(p, axis=0) - p[0]


def _lower_bounds_bwd(p, dlb):
    dp = jnp.cumsum(dlb[::-1], axis=0)[::-1]
    dp = dp.at[0].add(-jnp.sum(dlb, axis=0))
    return p * (dp - jnp.sum(dp * p, axis=0, keepdims=True))


SMALL = ["norm_w", "b_ada", "lru_conv_b", "lru_wa", "lru_ba", "lru_wx", "lru_bx", "lru_lambda", "hg_lb_logits",
         "hg_norm_w", "ssd_conv_b", "ssd_dt_bias", "ssd_a_log", "ssd_d", "ssd_norm_w", "final_norm_w"]
WEIGHTS = ["norm_w", "w_ada", "b_ada", "w_in", "lru_conv_w", "lru_conv_b", "lru_wa", "lru_ba", "lru_wx", "lru_bx",
           "lru_lambda", "hg_lb_logits", "hg_norm_w", "ssd_conv_w", "ssd_conv_b", "ssd_dt_bias", "ssd_a_log", "ssd_d",
           "ssd_norm_w", "w_out", "final_norm_w"]
INPUTS = ["x", "c"] + WEIGHTS + ["loss_target"] + ["m_" + n for n in WEIGHTS] + ["v_" + n for n in WEIGHTS]
SMALL_ROW = 1024


def _small_rows(like):
    out, off = {}, 0
    for n in SMALL:
        rows = -(-int(np.prod(like[n].shape)) // (8 * SMALL_ROW)) * 8
        out[n] = (off, rows)
        off += rows
    return out, off


def _flatten_small(d, prefix="", last=0.0):
    table, _ = _small_rows({n: d[prefix + n] for n in SMALL})
    pieces = []
    for n in SMALL:
        flat = d[prefix + n].reshape(-1)
        pieces.append(jnp.pad(flat, (0, table[n][1] * SMALL_ROW - flat.shape[0])).reshape(-1, SMALL_ROW))
    return jnp.concatenate(pieces + [jnp.full((8, SMALL_ROW), last, F32)], axis=0)


def _split_small(packed, like):
    table, _ = _small_rows(like)
    out = {}
    for n in SMALL:
        off, rows = table[n]
        size = int(np.prod(like[n].shape))
        out[n] = packed[off:off + rows].reshape(-1)[:size].reshape(like[n].shape)
    return out


def _local_step(x, mod, target, w, fetch, emit):
    S = x.shape[0]
    mall = _bfc(_hg_consts())
    mall_t = _bfc(_hg_consts().T)
    consts = _ssd_consts()
    p_lb, lbs = _lower_bounds(w["hg_lb_logits"])
    no_tok = jnp.zeros((8, LANE), F32)
    wa, wx = _block_diag(w["lru_wa"]), _block_diag(w["lru_wx"])
    ba, bx = w["lru_ba"].reshape(DEPTH, 1, LRU_W), w["lru_bx"].reshape(DEPTH, 1, LRU_W)
    lru_cb, lam, ssd_cb = w["lru_conv_b"][:, None], w["lru_lambda"][:, None], w["ssd_conv_b"][:, None]
    bias, alog = _pad_lanes(w["ssd_dt_bias"]), _pad_lanes(w["ssd_a_log"])
    dskip = jnp.repeat(w["ssd_d"], SSD_P, axis=1)
    saved = []
    for l in range(DEPTH):
        w_in_l, w_out_l, token = fetch(l, x)
        shift, scale, gate = (_Row(mod, l, D_MODEL, k) for k in range(3))
        nw = _Row(w["norm_w"], l)
        u, h = _inproj_fwd(x, nw, scale, shift, w_in_l, no_tok if token is None else token)
        ycat = lax.empty((S, D_INNER), BF16)
        lru_args = (l, u, w["lru_conv_w"], lru_cb, wa, ba, wx, bx, lam)
        ycat, h_lru = _lru_fwd(*lru_args, ycat)
        hg_args = (u, _Row(lbs, l), _Row(w["hg_norm_w"], l), mall)
        ycat, o_b, hg_st = _hg_fwd(*hg_args, ycat)
        xbc = _ssdconv_fwd(l, u, w["ssd_conv_w"], ssd_cb)
        ssd_args = (u, xbc, _Row(bias, l), _Row(alog, l), _Row(dskip, l), _Row(w["ssd_norm_w"], l), consts)
        ycat, y_ssd, ssd_st = _ssd_fwd(*ssd_args, ycat)
        token = fetch(l, y_ssd, late=True)
        if callable(w_out_l):
            w_out_l = w_out_l()
        x_new, y = _outproj_fwd(ycat, w_out_l, x, gate, no_tok if token is None else token)
        saved.append((x, u, h, ycat, nw, scale, gate, w_in_l, w_out_l, lru_args, h_lru, hg_args, o_b, hg_st, ssd_args,
                      y_ssd, ssd_st, y))
        x = x_new
    dx, red = _loss_head(x, w["final_norm_w"][None, :], target)
    loss = red[1, 0]
    reds = {k: [None] * DEPTH for k in ("in", "gate", "lru", "wa", "wx", "hg", "conv", "ssd")}
    for l in reversed(range(DEPTH)):
        (x, u, h, ycat, nw, scale, gate, w_in_l, w_out_l, lru_args, h_lru, hg_args, o_b, hg_st, ssd_args, y_ssd, ssd_st,
         y) = saved[l]
        dycat, g_out, reds["gate"][l] = _outproj_bwd(dx, y, gate, ycat, w_out_l)
        token = emit(l, "w_out", g_out)
        du = lax.empty((S, N_PAD), BF16)
        du, dxbc, reds["ssd"][l] = _ssd_bwd(*ssd_args, y_ssd, ssd_st, dycat, du, no_tok if token is None else token)
        du, reds["conv"][l] = _ssdconv_bwd(l, u, w["ssd_conv_w"], ssd_cb, dxbc, du)
        du, reds["hg"][l] = _hg_bwd(*hg_args, mall_t, o_b, hg_st, dycat, du)
        du, reds["lru"][l], reds["wa"][l], reds["wx"][l] = _lru_bwd(*lru_args, h_lru, dycat, du)
        token = emit(l, "w_in", functools.partial(_inproj_bwd_w, h, du))
        dx, reds["in"][l] = _inproj_bwd_x(du, w_in_l, x, nw, scale, dx, no_tok if token is None else token)
    r = {k: jnp.stack(v) for k, v in reds.items()}
    g = {n: None for n in WEIGHTS}
    g["final_norm_w"] = red[0]
    g["norm_w"] = r["in"][:, 2]
    dmod = jnp.concatenate([r["in"][:, 0], r["in"][:, 1], r["gate"][:, 0]], axis=1)
    g["lru_conv_w"], g["lru_conv_b"] = r["lru"][:, 0:4], r["lru"][:, 4]
    g["lru_ba"], g["lru_bx"] = r["lru"][:, 5].reshape(DEPTH, 8, 64), r["lru"][:, 6].reshape(DEPTH, 8, 64)
    g["lru_lambda"] = r["lru"][:, 7]
    g["lru_wa"], g["lru_wx"] = _diag_blocks(r["wa"]), _diag_blocks(r["wx"])
    g["hg_norm_w"] = r["hg"][:, 0]
    g["hg_lb_logits"] = _lower_bounds_bwd(p_lb, r["hg"][:, 1])
    g["ssd_conv_w"], g["ssd_conv_b"] = r["conv"][:, 0:4], r["conv"][:, 4]
    g["ssd_norm_w"] = r["ssd"][:, 0]
    g["ssd_d"] = r["ssd"][:, 1].reshape(DEPTH, SSD_HEADS, SSD_P).sum(-1)
    g["ssd_dt_bias"] = r["ssd"][:, 2, :SSD_HEADS]
    g["ssd_a_log"] = -r["ssd"][:, 3, :SSD_HEADS] * jnp.exp(w["ssd_a_log"])
    return loss, dx, dmod, g


def kernel(x, c, norm_w, w_ada, b_ada, w_in, lru_conv_w, lru_conv_b, lru_wa, lru_ba, lru_wx, lru_bx, lru_lambda, hg_lb_logits, hg_norm_w, ssd_conv_w, ssd_conv_b, ssd_dt_bias, ssd_a_log, ssd_d, ssd_norm_w, w_out, final_norm_w, loss_target, m_norm_w, m_w_ada, m_b_ada, m_w_in, m_lru_conv_w, m_lru_conv_b, m_lru_wa, m_lru_ba, m_lru_wx, m_lru_bx, m_lru_lambda, m_hg_lb_logits, m_hg_norm_w, m_ssd_conv_w, m_ssd_conv_b, m_ssd_dt_bias, m_ssd_a_log, m_ssd_d, m_ssd_norm_w, m_w_out, m_final_norm_w, v_norm_w, v_w_ada, v_b_ada, v_w_in, v_lru_conv_w, v_lru_conv_b, v_lru_wa, v_lru_ba, v_lru_wx, v_lru_bx, v_lru_lambda, v_hg_lb_logits, v_hg_norm_w, v_ssd_conv_w, v_ssd_conv_b, v_ssd_dt_bias, v_ssd_a_log, v_ssd_d, v_ssd_norm_w, v_w_out, v_final_norm_w):
    return _step(x, c, norm_w, w_ada, b_ada, w_in, lru_conv_w, lru_conv_b, lru_wa, lru_ba, lru_wx, lru_bx, lru_lambda, hg_lb_logits, hg_norm_w, ssd_conv_w, ssd_conv_b, ssd_dt_bias, ssd_a_log, ssd_d, ssd_norm_w, w_out, final_norm_w, loss_target, m_norm_w, m_w_ada, m_b_ada, m_w_in, m_lru_conv_w, m_lru_conv_b, m_lru_wa, m_lru_ba, m_lru_wx, m_lru_bx, m_lru_lambda, m_hg_lb_logits, m_hg_norm_w, m_ssd_conv_w, m_ssd_conv_b, m_ssd_dt_bias, m_ssd_a_log, m_ssd_d, m_ssd_norm_w, m_w_out, m_final_norm_w, v_norm_w, v_w_ada, v_b_ada, v_w_in, v_lru_conv_w, v_lru_conv_b, v_lru_wa, v_lru_ba, v_lru_wx, v_lru_bx, v_lru_lambda, v_hg_lb_logits, v_hg_norm_w, v_ssd_conv_w, v_ssd_conv_b, v_ssd_dt_bias, v_ssd_a_log, v_ssd_d, v_ssd_norm_w, v_w_out, v_final_norm_w)


def _step(*args):
    a = dict(zip(INPUTS, args, strict=True))
    me = 4 * lax.axis_index("x") + 2 * lax.axis_index("y") + lax.axis_index("c")
    x, target = a["x"][0], a["loss_target"][0]

    c_st = _exchange_start("gather_c", [a["c"]], [lax.empty((N_DEV,) + a["c"].shape, F32)], "gather")
    w_in_0 = (a["w_in"][0] + 0.0 * c_st["token"][0:1, 0:1]).astype(BF16)
    c_land = _exchange_wait("gather_c_wait", c_st, w_in_0)[0]
    c_all = lax.dynamic_update_index_in_dim(c_land, a["c"], me, 0)[:, 0, :]
    b_cols = lax.dynamic_slice_in_dim(a["b_ada"], me * ADA_COLS, ADA_COLS, axis=1)[:, None, :]
    mod_parts = _all_gather(_ada_fwd(c_all, a["w_ada"], b_cols), "gather_mod")
    mod = lax.dynamic_index_in_dim(mod_parts, me, axis=2, keepdims=False)
    mod = mod.transpose(1, 0, 2).reshape(DEPTH, 3 * D_MODEL)

    w = {n: a[n] for n in SMALL}

    w_in_b = [w_in_0] + [a["w_in"][l].astype(BF16) for l in range(1, DEPTH)]
    w_out_b = a["w_out"].astype(BF16)
    conv_own = jnp.concatenate([a["lru_conv_w"], a["ssd_conv_w"]], axis=-1)
    cols, rows_out = N_IN // N_DEV, D_INNER // N_DEV

    def gather_start(l, after):
        srcs = [w_in_b[l], conv_own if l == 0 else w_out_b[l]]
        lands = [lax.empty((N_DEV,) + s.shape, s.dtype) for s in srcs]
        return _exchange_start(f"gather_start_{l}", srcs, lands, "chip", after=after)

    def gather_pass(name, st, after, also=()):
        return _exchange_wait(name + "_passed", _exchange_relay(name + "_pass", st, after, also), after)

    gathers = {0: gather_start(0, mod)}
    passing = {}
    out0 = {"st": _exchange_start("gather_start_0_out", [w_out_b[0]], [lax.empty((N_DEV,) + w_out_b[0].shape, BF16)],
                                  "chip", after=gathers[0]["token"])}

    def fetch(l, x_l, late=False):
        if late:
            if l + 1 == DEPTH:
                return None
            if l == 0:
                out0["st"] = _exchange_relay("gather_0_out_pass", out0["st"], x_l)
                x_l = out0["st"]["token"]
            passing[l + 1] = _exchange_relay(f"gather_{l + 1}_pass", gathers[l + 1], x_l)
            if l == 0:
                landed = _exchange_wait("gather_0_out_passed", out0["st"], passing[1]["token"])
                out0["w"] = lax.dynamic_update_index_in_dim(landed[0], w_out_b[0], me, 0).reshape(D_INNER, D_MODEL)
            return passing[l + 1]["token"]
        if l == 0:
            landed = gather_pass("gather_0", gathers[0], x_l,
                                 also=(a["w_in"], a["m_w_in"], a["v_w_in"], out0["st"]["token"]))
            conv =lax.dynamic_update_index_in_dim(landed[1], conv_own, me, 0).transpose(1, 2, 0, 3)
            w["lru_conv_w"] = conv[..., :64].reshape(DEPTH, 4, LRU_W)
            w["ssd_conv_w"] = conv[..., 64:].reshape(DEPTH, 4, SSD_CONV)
            after, w_out_l = landed[0], lambda: out0["w"]
        else:
            landed = _exchange_wait(f"gather_{l}_passed", passing[l], x_l)
            after = lax.dynamic_update_index_in_dim(landed[1], w_out_b[l], me, 0)
            w_out_l = after.reshape(D_INNER, D_MODEL)
        token = None
        if l + 1 < DEPTH:
            gathers[l + 1] = gather_start(l + 1, after)
            token = gathers[l + 1]["token"]
        return _relayout_in(landed[0], w_in_b[l]), w_out_l, token

    PROJ = ("w_in", "w_out")
    scatters = {}
    lands = [lax.empty((N_DEV, DEPTH, D_MODEL, cols), BF16), lax.empty((N_DEV, DEPTH, rows_out, D_MODEL), BF16)]
    own = [None] * DEPTH

    deferred, g_out = {}, {}

    def emit(l, name, grad, after=None):
        if name == "w_out" and l > 0:
            g_out[l] = grad
            return None
        if name == "w_in" and l == 0 and after is None:
            deferred["w_in"] = grad
            return None
        if name == "w_in":
            grad = grad(jnp.zeros((8, LANE), F32) if after is None else after)
        if l == 0:
            k = PROJ.index(name)
            src = _relayout_grad(grad) if name == "w_in" else grad.reshape(N_DEV, rows_out, D_MODEL)
            st = _exchange_start(f"scatter_start_0_{name}", [src], [lands[k]], "scatter", layer=0, after=after)
            scatters[name] = st
            lands[k] = st["lands"][0]
            return st["token"]
        srcs = [_relayout_grad(grad), g_out[l].reshape(N_DEV, rows_out, D_MODEL)]
        st = _exchange_start(f"scatter_start_{l}", srcs, lands, "scatter", layer=l, after=after)
        scatters[l] = st
        lands[:] = st["lands"]
        return st["token"]

    loss_own, dx, dmod, g = _local_step(x, mod, target, w, fetch, emit)

    def sharded(name, parts, own=None, **kw):
        return _adamw(parts, a[name], a["m_" + name], a["v_" + name], "adamw_" + name + kw.pop("tag", ""), own=own, **kw)

    g["b_ada"] = dmod
    small_own = _flatten_small(g, last=loss_own)
    small_st = _exchange_start("gather_small", [small_own], [lax.empty((N_DEV,) + small_own.shape, F32)], "chip",
                               after=dx)
    big = {}
    after = emit(0, "w_in", deferred["w_in"], after=small_st["token"]) + dx[0:8, 0:LANE]

    def own_slices(st):
        return [lax.dynamic_index_in_dim(s, me, 0, keepdims=False) for s in st["srcs"]]

    upper_sts = [scatters[l] for l in reversed(range(1, DEPTH))] + [scatters["w_out"]]
    lands[:] = _exchange_wait_all("scatter_wait_upper", upper_sts, lands, [(0, 1)] * (DEPTH - 1) + [(1,)], after)
    for l in range(1, DEPTH):
        own[l] = own_slices(scatters[l])
    own[0] = [None, own_slices(scatters["w_out"])[0]]
    big["w_out"] = sharded("w_out", lands[1], jnp.stack([own[l][1] for l in range(DEPTH)]))
    upper = sharded("w_in", lands[0], jnp.stack([own[l][0] for l in range(1, DEPTH)]), layers=(1, DEPTH), tag="_upper")
    after = upper[1][0, 0:8, 0:LANE] + big["w_out"][1][0, 0:8, 0:LANE]
    small = gather_pass("gather_small", small_st, after)[0]
    outs = _adamw(small[:, None], *[_flatten_small(a, p)[None] for p in ("", "m_", "v_")], "adamw_small",
                  own=small_own[None])
    res = [_split_small(o[0], a) for o in outs]
    losses = lax.dynamic_update_index_in_dim(small[:, -1, 0], loss_own, me, 0)
    loss = jnp.sum(losses)

    off = _small_rows(a)[0]["b_ada"][0]
    dmod_all = lax.dynamic_update_index_in_dim(small[:, off:off + DEPTH * 3 * D_MODEL // SMALL_ROW],
                                               dmod.reshape(-1, SMALL_ROW), me, 0)
    dmod_all = dmod_all.reshape(N_DEV, DEPTH, 3 * D_MODEL).transpose(1, 0, 2)
    dmod_cols = lax.dynamic_slice_in_dim(dmod_all, me * ADA_COLS, ADA_COLS, axis=2)
    dmod_pad = jnp.pad(dmod_cols, ((0, 0), (0, LANE - N_DEV), (0, 0)))
    ct_pad = jnp.pad(c_all.T, ((0, 0), (0, LANE - N_DEV)))
    big["w_ada"] = sharded("w_ada", _ada_bwd(ct_pad, dmod_pad)[None])
    g_conv = jnp.concatenate([g["lru_conv_w"].reshape(DEPTH, 4, N_DEV, 64), g["ssd_conv_w"].reshape(DEPTH, 4, N_DEV, 192)],
                             axis=-1).transpose(2, 0, 1, 3)
    conv_parts = _all_to_all(g_conv, "scatter_conv")
    big["lru_conv_w"] = sharded("lru_conv_w", conv_parts[..., :64])
    big["ssd_conv_w"] = sharded("ssd_conv_w", conv_parts[..., 64:])

    after = outs[1] + big["w_ada"][1][0, 0:1, 0:1]
    scatters["w_in"]["lands"] = [lands[0]]
    lands[0] = _exchange_wait("scatter_wait_0_w_in", scatters["w_in"], after)[0]
    big["w_in"] = sharded("w_in", lands[0], own_slices(scatters["w_in"])[0][None], layers=(0, 1), prev=upper)

    out = [loss, dx[None]]
    for k in range(4):
        out += [big[n][k] if n in big else res[k][n] for n in WEIGHTS]
    return tuple(out)
```

```python
import functools

import numpy as np
import jax
import jax.numpy as jnp
from jax import lax
from jax.experimental import pallas as pl
from jax.experimental.pallas import tpu as pltpu

F32 = jnp.float32
BF16 = jnp.bfloat16
SDS = jax.ShapeDtypeStruct

N_DEV = 8
DEPTH = 4
D_MODEL = 1024
D_INNER = 2048
EPS = 1e-6
LRU_W = 512
LRU_C = 8.0
HG_W = 512
HG_CHUNK = 64
HG_HEADS = 4
SSD_W = 1024
SSD_HEADS = 16
SSD_P = 64
SSD_N = 128
SSD_CHUNK = 128
SSD_CONV = 1536
N_IN = 5648
N_PAD = 5760
OFF_HG = 0
OFF_LRU = 2048
OFF_XBC = 3072
OFF_Z = 4608
LANE = 128
VMEM_LIMIT = 56 * 1024 * 1024
NEG = -1e30

ADAM_LR = 0.001
ADAM_B1 = 0.9
ADAM_B2 = 0.999
ADAM_EPS = 1e-08
ADAM_WD = 0.01
ADAM_STEP = 10


def _cp(sem=None):
    return pltpu.CompilerParams(dimension_semantics=sem, vmem_limit_bytes=VMEM_LIMIT)


def _dg(a, b, ca, cb):
    return lax.dot_general(a, b, (((ca,), (cb,)), ((), ())), preferred_element_type=F32)


def _mm(a, b):
    return _dg(a, b, 1, 0)


def _mm_nt(a, b):
    return _dg(a, b, 1, 1)


def _mm_tn(a, b):
    return _dg(a, b, 0, 0)


def _bf(x):
    return x.astype(BF16)


def _f(x):
    return x.astype(F32)


def _split3(x):
    hi = x.astype(BF16)
    r = x - hi.astype(F32)
    mid = r.astype(BF16)
    lo = (r - mid.astype(F32)).astype(BF16)
    return hi, mid, lo


def _sel_r(x, m):
    hi, mid, lo = _split3(x)
    return _mm(hi, m) + _mm(mid, m) + _mm(lo, m)


def _sel_l(m, x):
    hi, mid, lo = _split3(x)
    return _mm(m, hi) + _mm(m, mid) + _mm(m, lo)


def _sel_l2(m, x):
    hi = x.astype(BF16)
    lo = (x - hi.astype(F32)).astype(BF16)
    return _mm(m, hi) + _mm(m, lo)


def _sel_tn(x, m):
    hi, mid, lo = _split3(x)
    return _mm_tn(hi, m) + _mm_tn(mid, m) + _mm_tn(lo, m)


def _sigmoid(x):
    return 1.0 / (1.0 + jnp.exp(-x))


def _silu(x):
    return x * _sigmoid(x)


def _dsilu(x):
    s = _sigmoid(x)
    return s * (1.0 + x * (1.0 - s))


def _softplus(x):
    return jnp.maximum(x, 0.0) + jnp.log(1.0 + jnp.exp(-jnp.abs(x)))


def _expm1(z):
    series = z * (1.0 + z * (1.0 / 2) * (1.0 + z * (1.0 / 3) * (1.0 + z * (1.0 / 4) * (
        1.0 + z * (1.0 / 5) * (1.0 + z * (1.0 / 6) * (1.0 + z * (1.0 / 7)))))))
    return jnp.where(jnp.abs(z) < 0.3, series, jnp.exp(z) - 1.0)


def _iota(shape, dim):
    return lax.broadcasted_iota(jnp.int32, shape, dim)


def _last_row(x, rows):
    return jnp.sum(jnp.where(rows == x.shape[0] - 1, x, 0.0), axis=0, keepdims=True)


def _shift_down(x, d, rows, fill=0.0):
    return jnp.where(rows >= d, pltpu.roll(x, d, 0), fill)


def _shift_up(x, d, rows, fill=0.0):
    n = x.shape[0]
    return jnp.where(rows < n - d, pltpu.roll(x, n - d, 0), fill)


def _conv_fwd(x, cw_ref, cb_ref, rows):
    out = cb_ref[...] + cw_ref[pl.ds(3, 1), :] * x
    for k in range(3):
        out = out + cw_ref[pl.ds(k, 1), :] * _shift_down(x, 3 - k, rows)
    return out


def _conv_bwd(x, dco, cw_ref, rows):
    dx = cw_ref[pl.ds(3, 1), :] * dco
    dws = []
    for k in range(3):
        dx = dx + cw_ref[pl.ds(k, 1), :] * _shift_up(dco, 3 - k, rows)
        dws.append(jnp.sum(dco * _shift_down(x, 3 - k, rows), axis=0, keepdims=True))
    dws.append(jnp.sum(dco * x, axis=0, keepdims=True))
    return dx, dws, jnp.sum(dco, axis=0, keepdims=True)


def _vec(n):
    return pl.BlockSpec((1, n), lambda *_: (0, 0))


class _Row:
    def __init__(self, arr, l, n=None, c=0):
        self.arr, self.l, self.n, self.c = arr[:, None, :], l, n or arr.shape[1], c


def _spec(v):
    if isinstance(v, _Row):
        return pl.BlockSpec((None, 1, v.n), lambda *_: (v.l, 0, v.c))
    return _vec(v.shape[1])


def _arr(v):
    return v.arr if isinstance(v, _Row) else v


def _full(shape):
    nd = len(shape)
    return pl.BlockSpec(shape, lambda *_: (0,) * nd)


def _inproj_fwd(x, nw, scale, shift, w, tok):
    S = x.shape[0]
    tm = min(256, S)

    def body(x_ref, nw_ref, sc_ref, sh_ref, w_ref, tok_ref, u_ref, h_ref):
        del tok_ref
        xv = x_ref[...]
        inv = lax.rsqrt(jnp.mean(xv * xv, axis=-1, keepdims=True) + EPS)
        h = ((xv * inv) * nw_ref[...] * (1.0 + sc_ref[...]) + sh_ref[...]).astype(BF16)
        h_ref[...] = h
        u_ref[...] = _mm(h, w_ref[...])

    return pl.pallas_call(
        body, name="inproj_fwd", grid=(S // tm,),
        in_specs=[pl.BlockSpec((tm, D_MODEL), lambda i: (i, 0)), _spec(nw), _spec(scale), _spec(shift),
                  _full((D_MODEL, N_PAD)), pl.BlockSpec(memory_space=pl.ANY)],
        out_specs=[pl.BlockSpec((tm, N_PAD), lambda i: (i, 0)), pl.BlockSpec((tm, D_MODEL), lambda i: (i, 0))],
        out_shape=[SDS((S, N_PAD), F32), SDS((S, D_MODEL), BF16)],
        compiler_params=_cp(("parallel",)),
    )(x, _arr(nw), _arr(scale), _arr(shift), w, tok)


def _inproj_bwd_x(du, w, x, nw, scale, dxn, tok):
    S = x.shape[0]
    tm = min(256, S)

    def body(du_ref, w_ref, x_ref, nw_ref, sc_ref, dxn_ref, tok_ref, dx_ref, red_ref):
        del tok_ref

        @pl.when(pl.program_id(0) == 0)
        def _():
            red_ref[...] = jnp.zeros_like(red_ref)

        dh = _mm_nt(du_ref[...], w_ref[...])
        xv = x_ref[...]
        inv = lax.rsqrt(jnp.mean(xv * xv, axis=-1, keepdims=True) + EPS)
        xhat = xv * inv
        nwv = nw_ref[...]
        g1 = 1.0 + sc_ref[...]
        dxhat = dh * nwv * g1
        dx = inv * (dxhat - xhat * jnp.mean(dxhat * xhat, axis=-1, keepdims=True))
        dx_ref[...] = dxn_ref[...] + dx
        red_ref[0:1, :] += jnp.sum(dh, axis=0, keepdims=True)
        red_ref[1:2, :] += jnp.sum(dh * xhat * nwv, axis=0, keepdims=True)
        red_ref[2:3, :] += jnp.sum(dh * xhat * g1, axis=0, keepdims=True)

    row = pl.BlockSpec((tm, D_MODEL), lambda i: (i, 0))
    return pl.pallas_call(
        body, name="inproj_bwd_x", grid=(S // tm,),
        in_specs=[pl.BlockSpec((tm, N_PAD), lambda i: (i, 0)), _full((D_MODEL, N_PAD)), row, _spec(nw),
                  _spec(scale), row, pl.BlockSpec(memory_space=pl.ANY)],
        out_specs=[row, _full((8, D_MODEL))],
        out_shape=[SDS((S, D_MODEL), F32), SDS((8, D_MODEL), F32)],
        compiler_params=_cp(("arbitrary",)),
    )(du, w, x, _arr(nw), _arr(scale), dxn, tok)


def _inproj_bwd_w(h, du, tok):
    S = h.shape[0]
    tn = 640

    def body(h_ref, du_ref, tok_ref, gw_ref):
        del tok_ref
        gw_ref[...] = _mm_tn(h_ref[...], _bf(du_ref[...]))

    return pl.pallas_call(
        body, name="inproj_bwd_w", grid=(N_PAD // tn,),
        in_specs=[_full((S, D_MODEL)), pl.BlockSpec((S, tn), lambda j: (0, j)), pl.BlockSpec(memory_space=pl.ANY)],
        out_specs=pl.BlockSpec((D_MODEL, tn), lambda j: (0, j)),
        out_shape=SDS((D_MODEL, N_PAD), F32),
        compiler_params=_cp(("parallel",)),
    )(h, du, tok)


def _scan_block(a, b, rows):
    d = 1
    while d < a.shape[0]:
        a_s = _shift_down(a, d, rows, 1.0)
        b_s = _shift_down(b, d, rows, 0.0)
        b = a * b_s + b
        a = a * a_s
        d *= 2
    return a, b


def _rscan_block(c, g, rows):
    d = 1
    while d < c.shape[0]:
        c_s = _shift_up(c, d, rows, 1.0)
        g_s = _shift_up(g, d, rows, 0.0)
        g = g + c * g_s
        c = c * c_s
        d *= 2
    return c, g


LRU_BLOCK = 128


def _lru_gates(xa, wa_ref, ba_ref, wx_ref, bx_ref, lam_ref):
    sp = _softplus(-lam_ref[...])
    xb = _bf(xa)
    r = _sigmoid(_mm(xb, wa_ref[...]) + ba_ref[...])
    ig = _sigmoid(_mm(xb, wx_ref[...]) + bx_ref[...])
    la = -LRU_C * r * sp
    a = jnp.exp(la)
    mult = jnp.sqrt(-_expm1(2.0 * la))
    return sp, r, ig, la, a, mult


def _lru_specs(S, l):
    t128 = pl.BlockSpec((None, 1, LANE), lambda t: (l, 0, t))
    gate = pl.BlockSpec((None, None, LANE, LANE), lambda t: (l, t, 0, 0))
    return [pl.BlockSpec((S, 2 * LANE), lambda t: (0, OFF_LRU // (2 * LANE) + t)),
            pl.BlockSpec((None, 4, LANE), lambda t: (l, 0, t)), t128, gate, t128, gate, t128, t128]


def _lru_fwd(l, u, cw, cb, wa, ba, wx, bx, lam, ycat):
    S = u.shape[0]
    tb = min(LRU_BLOCK, S)

    def body(u_ref, cw_ref, cb_ref, wa_ref, ba_ref, wx_ref, bx_ref, lam_ref, ycat_in, ycat_ref, h_ref, a_scr, b_scr):
        del ycat_in
        rows = _iota((S, LANE), 0)
        xa = _conv_fwd(_f(u_ref[:, 0:LANE]), cw_ref, cb_ref, rows)
        _, _, ig, _, a, mult = _lru_gates(xa, wa_ref, ba_ref, wx_ref, bx_ref, lam_ref)
        a_scr[...] = a
        b_scr[...] = mult * (ig * xa)
        rows_b = _iota((tb, LANE), 0)

        def blk(j, hprev):
            sl = pl.ds(pl.multiple_of(j * tb, tb), tb)
            acum, hloc = _scan_block(a_scr[sl, :], b_scr[sl, :], rows_b)
            hf = hloc + acum * hprev
            h_ref[sl, :] = hf
            return _last_row(hf, rows_b)

        lax.fori_loop(0, S // tb, blk, jnp.zeros((1, LANE), F32))
        ycat_ref[...] = _bf(h_ref[...] * _silu(_f(u_ref[:, LANE:2 * LANE])))

    col = pl.BlockSpec((S, LANE), lambda t: (0, t))
    return pl.pallas_call(
        body, name="lru_fwd", grid=(LRU_W // LANE,),
        in_specs=_lru_specs(S, l) + [pl.BlockSpec(memory_space=pl.ANY)],
        out_specs=[col, col],
        out_shape=[SDS((S, D_INNER), BF16), SDS((S,LRU_W), F32)],
        scratch_shapes=[pltpu.VMEM((S, LANE), F32), pltpu.VMEM((S, LANE), F32)],
        input_output_aliases={8: 0},
        compiler_params=_cp(("parallel",)),
    )(u, cw, cb, wa, ba, wx, bx, lam, ycat)


def _lru_bwd(l, u, cw, cb, wa, ba, wx, bx, lam, h_lru, dycat, du):
    S = u.shape[0]
    tb = min(LRU_BLOCK, S)

    def body(u_ref, cw_ref, cb_ref, wa_ref, ba_ref, wx_ref, bx_ref, lam_ref, h_ref, dy_ref, du_in,
             du_ref, red_ref, gwa_ref, gwx_ref, c_scr, g_scr, l_scr):
        del du_in
        rows = _iota((S, LANE), 0)
        ax = _f(u_ref[:, 0:LANE])
        ag = _f(u_ref[:, LANE:2 * LANE])
        xa = _conv_fwd(ax, cw_ref, cb_ref, rows)
        sp, r, ig, la, a, mult = _lru_gates(xa, wa_ref, ba_ref, wx_ref, bx_ref, lam_ref)
        h = h_ref[...]
        dy = _f(dy_ref[...])
        du_ref[:, LANE:2 * LANE] = _bf(dy * h * _dsilu(ag))
        c_scr[...] = _shift_up(a, 1, rows, 0.0)
        g_scr[...] = dy * _silu(ag)
        rows_b = _iota((tb, LANE), 0)
        nb = S // tb

        def blk(jj, lnext):
            j = nb - 1 - jj
            sl = pl.ds(pl.multiple_of(j * tb, tb), tb)
            ccum, lloc = _rscan_block(c_scr[sl, :], g_scr[sl, :], rows_b)
            lam_t = lloc + ccum * lnext
            l_scr[sl, :] = lam_t
            return jnp.sum(jnp.where(rows_b == 0, lam_t, 0.0), axis=0, keepdims=True)

        lax.fori_loop(0, nb, blk, jnp.zeros((1, LANE), F32))
        db = l_scr[...]
        da = db * _shift_down(h, 1, rows)
        dmult = db * ig * xa
        dig = db * mult * xa
        dxa = db * mult * ig
        dla = da * a - dmult * (a * a) / mult
        dr = -LRU_C * sp * dla
        dsp = jnp.sum(-LRU_C * r * dla, axis=0, keepdims=True)
        dlam = -dsp * _sigmoid(-lam_ref[...])
        dzr = dr * r * (1.0 - r)
        dzi = dig * ig * (1.0 - ig)
        dzr_b, dzi_b, xa_b = _bf(dzr), _bf(dzi), _bf(xa)
        dxa = dxa + _mm_nt(dzr_b, wa_ref[...]) + _mm_nt(dzi_b, wx_ref[...])
        gwa_ref[...] = _mm_tn(xa_b, dzr_b)
        gwx_ref[...] = _mm_tn(xa_b, dzi_b)
        dax, dws, dcb = _conv_bwd(ax, dxa, cw_ref, rows)
        du_ref[:, 0:LANE] = _bf(dax)
        parts = dws + [dcb, jnp.sum(dzr, axis=0, keepdims=True), jnp.sum(dzi, axis=0, keepdims=True), dlam]
        for n, p in enumerate(parts):
            red_ref[pl.ds(n, 1), :] = p

    col = pl.BlockSpec((S, LANE), lambda t: (0, t))
    gw = pl.BlockSpec((None, LANE, LANE), lambda t: (t, 0, 0))
    return pl.pallas_call(
        body, name="lru_bwd", grid=(LRU_W // LANE,),
        in_specs=_lru_specs(S, l) + [col, col, pl.BlockSpec(memory_space=pl.ANY)],
        out_specs=[pl.BlockSpec((S, 2 * LANE), lambda t: (0, OFF_LRU // (2 * LANE) + t)),
                   pl.BlockSpec((8, LANE), lambda t: (0, t)), gw, gw],
        out_shape=[SDS((S, N_PAD), BF16), SDS((8, LRU_W), F32), SDS((4, LANE, LANE), F32), SDS((4, LANE, LANE), F32)],
        scratch_shapes=[pltpu.VMEM((S, LANE), F32)] * 3,
        input_output_aliases={10: 0},
        compiler_params=_cp(("parallel",)),
    )(u, cw, cb, wa, ba, wx, bx, lam, h_lru, dycat, du)


HG_LEVELS = 6


def _hg_consts():
    C = HG_CHUNK
    t = np.arange(C)[:, None]
    r = np.arange(C)[None, :]
    mats = []
    for l in range(HG_LEVELS):
        b = 1 << l
        upper = (t % (2 * b)) >= b
        anchor = (t // (2 * b)) * 2 * b + b - 1
        mats.append((upper & (r > anchor) & (r <= t)) | ((~upper) & (r > t) & (r <= anchor)))
    mats.append(r <= t)
    mats.append(r > t)
    return np.concatenate(mats, 0).astype(np.float32)


def _hg_factors(hf, lb, mall):
    s = _sigmoid(hf)
    f = lb + (1.0 - lb) * s
    lf = jnp.log(f)
    k = (1.0 - lb) * _sigmoid(-hf)
    e = jnp.exp(_sel_l(mall, lf))
    C = HG_CHUNK
    rows = _iota((C, HG_W), 0)
    eq, ek = [], []
    for l in range(HG_LEVELS):
        el = e[l * C:(l + 1) * C]
        eq.append(jnp.where((lax.shift_right_logical(rows, l) & 1) == 1, el, 0.0))
        ek.append(el - eq[l])
    ecum = e[HG_LEVELS * C:(HG_LEVELS + 1) * C]
    erem = e[(HG_LEVELS + 1) * C:(HG_LEVELS + 2) * C]
    return s, f, k, eq, ek, ecum, erem


def _hg_masks():
    C = HG_CHUNK
    ri, ci = _iota((C, C), 0), _iota((C, C), 1)
    rr = _iota((C, LANE), 0)
    gm = [(lax.shift_right_logical(ri, l + 1) == lax.shift_right_logical(ci, l + 1)).astype(F32)
          for l in range(HG_LEVELS)]
    up = [(lax.shift_right_logical(rr, l) & 1) == 1 for l in range(HG_LEVELS)]
    eye = (ri == ci).astype(F32)
    return gm, up, eye, rr


def _hg_scores(qh, kh, eq, ek, sl, gm, up, eye):
    del up
    qs, ks, qb, kb = [], [], [], []
    p = _mm_nt(_bf(qh), _bf(kh)) * eye
    for l in range(HG_LEVELS):
        qs.append(qh * eq[l][:, sl])
        ks.append(kh * ek[l][:, sl])
        qb.append(_bf(qs[l]))
        kb.append(_bf(ks[l]))
        p = p + _mm_nt(qb[l], kb[l]) * gm[l]
    return p, qs, ks, qb, kb


HG_SUB = 4


def _hg_fwd(u, lb, nw, mall, ycat):
    S = u.shape[0]
    C = HG_CHUNK
    n = S // C
    rows = HG_SUB * C

    def body(u_ref, lb_ref, nw_ref, mall_ref, ycat_in, ycat_ref, o_ref, st_ref, st):
        del ycat_in

        @pl.when(pl.program_id(0) == 0)
        def _():
            st[...] = jnp.zeros_like(st)

        gm, up, eye, rr = _hg_masks()
        for sub in range(HG_SUB):
            r = slice(sub * C, (sub + 1) * C)
            q = _silu(_f(u_ref[r, 0:512]))
            v = u_ref[r, 1024:1536]
            _, _, k, eq, ek, ecum, erem = _hg_factors(_f(u_ref[r, 512:1024]), lb_ref[...], mall_ref[...])
            for h in range(HG_HEADS):
                sl = slice(h * LANE, (h + 1) * LANE)
                qh, kh, vh = q[:, sl], k[:, sl], _bf(v[:, sl])
                p = _hg_scores(qh, kh, eq, ek, sl, gm, up, eye)[0]
                sth = st[h]
                st_ref[sub, h] = sth
                o_ref[r, sl] = _mm(_bf(p), vh) + _mm_nt(_bf(qh * ecum[:, sl]), _bf(sth))
                st[h] = sth * _last_row(ecum[:, sl], rr) + _mm_tn(vh, _bf(kh * erem[:, sl]))
            o = o_ref[r, :]
            inv = lax.rsqrt(jnp.mean(o * o, axis=-1, keepdims=True) + EPS)
            ycat_ref[r, :] = _bf((o * inv) * nw_ref[...] * _silu(_f(u_ref[r, 1536:2048])))

    return pl.pallas_call(
        body, name="hg_fwd", grid=(n // HG_SUB,),
        in_specs=[pl.BlockSpec((rows, 2048), lambda i: (i, 0)), _spec(lb), _spec(nw), _full(mall.shape),
                  pl.BlockSpec(memory_space=pl.ANY)],
        out_specs=[pl.BlockSpec((rows, HG_W), lambda i: (i, 1)), pl.BlockSpec((rows, HG_W), lambda i: (i, 0)),
                   pl.BlockSpec((HG_SUB, HG_HEADS, LANE, LANE), lambda i: (i, 0, 0, 0))],
        out_shape=[SDS((S, D_INNER), BF16), SDS((S,HG_W), F32), SDS((n, HG_HEADS, LANE, LANE), F32)],
        scratch_shapes=[pltpu.VMEM((HG_HEADS, LANE, LANE), F32)],
        input_output_aliases={4: 0},
        compiler_params=_cp(("arbitrary",)),
    )(u, _arr(lb), _arr(nw), mall, ycat)


def _hg_bwd(u, lb, nw, mall, mall_t, o_b, states, dycat, du):
    S = u.shape[0]
    C = HG_CHUNK
    n = S // C
    nb = n // HG_SUB
    rows = HG_SUB * C
    L2 = HG_LEVELS

    def body(u_ref, lb_ref, nw_ref, mall_ref, mallt_ref, o_ref, st_ref, dy_ref, du_in, du_ref, red_ref,
             dst, dlast_s, dq_s, dk_s, dex):
        del du_in

        @pl.when(pl.program_id(0) == 0)
        def _():
            dst[...] = jnp.zeros_like(dst)
            red_ref[...] = jnp.zeros_like(red_ref)

        lb = lb_ref[...]
        nwv = nw_ref[...]
        gm, up, eye, rr = _hg_masks()
        for sub in reversed(range(HG_SUB)):
            r = slice(sub * C, (sub + 1) * C)
            hq, hf, hg = _f(u_ref[r, 0:512]), _f(u_ref[r, 512:1024]), _f(u_ref[r, 1536:2048])
            q = _silu(hq)
            v = u_ref[r, 1024:1536]
            s, f, k, eq, ek, ecum, erem = _hg_factors(hf, lb, mall_ref[...])
            o = o_ref[r, :]
            dy = _f(dy_ref[r, :])
            inv = lax.rsqrt(jnp.mean(o * o, axis=-1, keepdims=True) + EPS)
            ohat = o * inv
            du_ref[r, 1536:2048] = _bf(dy * ohat * nwv * _dsilu(hg))
            dn = dy * _silu(hg)
            red_ref[0:1, :] += jnp.sum(dn * ohat, axis=0, keepdims=True)
            dohat = dn * nwv
            do = inv * (dohat - ohat * jnp.mean(dohat * ohat, axis=-1, keepdims=True))
            for h in range(HG_HEADS):
                sl = slice(h * LANE, (h + 1) * LANE)
                qh, kh, vh, doh = q[:, sl], k[:, sl], _bf(v[:, sl]), _bf(do[:, sl])
                p, qs, ks, qb, kb = _hg_scores(qh, kh, eq, ek, sl, gm, up, eye)
                st_f = st_ref[sub, h]
                sth = _bf(st_f)
                dsth = dst[h]
                dsth_b = _bf(dsth)
                qt = qh * ecum[:, sl]
                kt = kh * erem[:, sl]
                elast = _last_row(ecum[:, sl], rr)
                dp = _mm_nt(doh, vh)
                du_ref[r, 1024 + h * LANE:1024 + (h + 1) * LANE] = _bf(_mm_tn(_bf(p), doh) + _mm_nt(_bf(kt), dsth_b))
                dpe = _bf(dp * eye)
                dqt = _mm(doh, sth)
                dkt = _mm(vh, dsth_b)
                dq = dqt * ecum[:, sl] + _mm(dpe, _bf(kh))
                dk = dkt * erem[:, sl] + _mm_tn(dpe, _bf(qh))
                dex[sub, L2 * C:(L2 + 1) * C, sl] = dqt * qt
                dex[sub, (L2 + 1) * C:(L2 + 2) * C, sl] = dkt * kt
                for l in range(HG_LEVELS):
                    dpl = _bf(dp * gm[l])
                    dql = _mm(dpl, kb[l])
                    dkl = _mm_tn(dpl, qb[l])
                    dq = dq + dql * eq[l][:, sl]
                    dk = dk + dkl * ek[l][:, sl]
                    dex[sub, l * C:(l + 1) * C, sl] = dql * qs[l] + dkl * ks[l]
                dlast_s[sub, :, sl] = jnp.sum(dsth * st_f, axis=0, keepdims=True) * elast
                dst[h] = dsth * elast + _mm_tn(doh, _bf(qt))
                dq_s[sub, :, sl] = dq
                dk_s[sub, :, sl] = dk
            dq = dq_s[sub]
            dk = dk_s[sub]
            dlf = _sel_l2(mallt_ref[...], dex[sub]) + dlast_s[sub]
            du_ref[r, 0:512] = _bf(dq * _dsilu(hq))
            t = (1.0 - s) * (dlf / f - dk)
            du_ref[r, 512:1024] = _bf((1.0 - lb) * s * t)
            red_ref[1:2, :] += jnp.sum(t, axis=0, keepdims=True)

    rev = lambda i: (nb - 1 - i, 0)
    return pl.pallas_call(
        body, name="hg_bwd", grid=(nb,),
        in_specs=[pl.BlockSpec((rows, 2048), rev), _spec(lb), _spec(nw), _full(mall.shape), _full(mall_t.shape),
                  pl.BlockSpec((rows, HG_W), rev),
                  pl.BlockSpec((HG_SUB, HG_HEADS, LANE, LANE), lambda i: (nb - 1 - i, 0, 0, 0)),
                  pl.BlockSpec((rows, HG_W), lambda i: (nb - 1 - i, 1)), pl.BlockSpec(memory_space=pl.ANY)],
        out_specs=[pl.BlockSpec((rows, 2048), rev), pl.BlockSpec((8, HG_W), lambda i: (0, 0))],
        out_shape=[SDS((S, N_PAD), BF16), SDS((8, HG_W), F32)],
        scratch_shapes=[pltpu.VMEM((HG_HEADS, LANE, LANE), F32), pltpu.VMEM((HG_SUB, 1, HG_W), F32),
                        pltpu.VMEM((HG_SUB, C, HG_W), F32), pltpu.VMEM((HG_SUB, C, HG_W), F32),
                        pltpu.VMEM((HG_SUB, (L2 + 2) * C, HG_W), F32)],
        input_output_aliases={8: 0},
        compiler_params=_cp(("arbitrary",)),
    )(u, _arr(lb), _arr(nw), mall, mall_t, o_b, states, dycat, du)


def _ssdconv_fwd(l, u, cw, cb):
    S = u.shape[0]

    def body(u_ref, cw_ref, cb_ref, out_ref):
        rows = _iota((S, LANE), 0)
        out_ref[...] = _silu(_conv_fwd(_f(u_ref[...]), cw_ref, cb_ref, rows))

    return pl.pallas_call(
        body, name="ssdconv_fwd", grid=(SSD_CONV // LANE,),
        in_specs=[pl.BlockSpec((S, LANE), lambda t: (0, OFF_XBC // LANE + t)),
                  pl.BlockSpec((None, 4, LANE), lambda t: (l, 0, t)), pl.BlockSpec((None, 1, LANE), lambda t: (l, 0, t))],
        out_specs=pl.BlockSpec((S, LANE), lambda t: (0, t)),
        out_shape=SDS((S, SSD_CONV), F32),
        compiler_params=_cp(("parallel",)),
    )(u, cw, cb)


def _ssdconv_bwd(l, u, cw, cb, dxbc, du):
    S = u.shape[0]

    def body(u_ref, cw_ref, cb_ref, d_ref, du_in, du_ref, red_ref):
        del du_in
        rows = _iota((S, LANE), 0)
        x = _f(u_ref[...])
        dco = d_ref[...] * _dsilu(_conv_fwd(x, cw_ref, cb_ref, rows))
        dx, dws, dcb = _conv_bwd(x, dco, cw_ref, rows)
        du_ref[...] = _bf(dx)
        for n, p in enumerate(dws + [dcb]):
            red_ref[pl.ds(n, 1), :] = p
        red_ref[pl.ds(5, 3), :] = jnp.zeros((3, LANE), F32)

    ucol = pl.BlockSpec((S, LANE), lambda t: (0, OFF_XBC // LANE + t))
    return pl.pallas_call(
        body, name="ssdconv_bwd", grid=(SSD_CONV // LANE,),
        in_specs=[ucol, pl.BlockSpec((None, 4, LANE), lambda t: (l, 0, t)),
                  pl.BlockSpec((None, 1, LANE), lambda t: (l, 0, t)),
                  pl.BlockSpec((S, LANE), lambda t: (0, t)), pl.BlockSpec(memory_space=pl.ANY)],
        out_specs=[ucol, pl.BlockSpec((8, LANE), lambda t: (0, t))],
        out_shape=[SDS((S, N_PAD), BF16), SDS((8, SSD_CONV), F32)],
        input_output_aliases={4: 0},
        compiler_params=_cp(("parallel",)),
    )(u, cw, cb, dxbc, du)


SSD_SUB = 2


def _ssd_consts():
    e64 = np.zeros((LANE, SSD_W), np.float32)
    for h in range(SSD_HEADS):
        e64[h, h * SSD_P:(h + 1) * SSD_P] = 1.0
    T = SSD_CHUNK
    tril = (np.arange(T)[None, :] <= np.arange(T)[:, None]).astype(np.float32)
    return e64, tril, tril.T.copy()


def _ssd_common(zdt, bias_ref, alog_ref, tril, e64, cum_ref, cumt_ref):
    T = SSD_CHUNK
    lane = _iota((1, LANE), 1)
    a_neg = jnp.where(lane < SSD_HEADS, -jnp.exp(alog_ref[...]), 0.0)
    dtpre = zdt[:, SSD_W:SSD_W + LANE] + bias_ref[...]
    dt = _softplus(dtpre)
    cum = _sel_l(tril, dt * a_neg)
    cum_ref[...] = cum
    cumt_ref[...] = cum.T
    cum_x = _sel_r(cum, e64)
    last_x = _last_row(cum_x, _iota((T, SSD_W), 0))
    ecum_x = jnp.exp(cum_x)
    erem_x = jnp.exp(last_x - cum_x)
    elast_x = jnp.exp(last_x)
    dt_x = _sel_r(dt, e64)
    return a_neg, dtpre, dt, ecum_x, erem_x, elast_x, dt_x


def _ssd_decay(cum_ref, cumt_ref, h, causal):
    T = SSD_CHUNK
    diff = jnp.broadcast_to(cum_ref[:, pl.ds(h, 1)], (T, T)) - cumt_ref[pl.ds(h, 1), :]
    return jnp.exp(jnp.where(causal, diff, NEG))


def _group_norm_fwd(y1, nwv):
    outs, invs = [], []
    for g in range(2):
        seg = y1[:, g * 512:(g + 1) * 512]
        inv = lax.rsqrt(jnp.mean(seg * seg, axis=-1, keepdims=True) + EPS)
        outs.append(seg * inv * nwv[:, g * 512:(g + 1) * 512])
        invs.append(inv)
    return outs, invs


def _ssd_fwd(u, xbc, bias, alog, dskip_x, nw, consts, ycat):
    S = u.shape[0]
    T = SSD_CHUNK
    n = S // T
    rows = SSD_SUB * T
    e64, tril, _ = consts

    def body(u_ref, xbc_ref, bias_ref, alog_ref, dx_ref, nw_ref, e64_ref, tril_ref, ycat_in,
             ycat_ref, y_ref, st_ref, st, cumt, cum_e):
        del ycat_in

        @pl.when(pl.program_id(0) == 0)
        def _():
            st[...] = jnp.zeros_like(st)

        causal = _iota((T, T), 0) >= _iota((T, T), 1)
        lo = _iota((T, LANE), 1) < SSD_P
        for sub in range(SSD_SUB):
            r = slice(sub * T, (sub + 1) * T)
            zdt = _f(u_ref[r, :])
            z = zdt[:, 0:SSD_W]
            xs = xbc_ref[r, 0:SSD_W]
            cum_r, cumt_r = cum_e.at[sub], cumt.at[sub]
            _, _, _, ecum_x, erem_x, elast_x, dt_x = _ssd_common(
                zdt, bias_ref, alog_ref, tril_ref[...], e64_ref[...], cum_r, cumt_r)
            xdt = xs * dt_x
            xrem = xdt * erem_x
            st_ref[sub] = st[...]
            for g in range(2):
                gs = slice(g * 512, (g + 1) * 512)
                bg = _bf(xbc_ref[r, SSD_W + g * LANE:SSD_W + (g + 1) * LANE])
                cg = _bf(xbc_ref[r, SSD_W + 256 + g * LANE:SSD_W + 256 + (g + 1) * LANE])
                cb = _mm_nt(cg, bg)
                yin = _mm(cg, _bf(st[:, gs])) * ecum_x[:, gs]
                for j in range(4):
                    h0 = 8 * g + 2 * j
                    cs = slice(h0 * SSD_P, (h0 + 2) * SSD_P)
                    xp = xdt[:, cs]
                    s0 = _bf(cb * _ssd_decay(cum_r, cumt_r, h0, causal))
                    s1 = _bf(cb * _ssd_decay(cum_r, cumt_r, h0 + 1, causal))
                    y_ref[r, cs] = (_mm(s0, _bf(jnp.where(lo, xp, 0.0))) + _mm(s1, _bf(jnp.where(lo, 0.0, xp)))
                                    + yin[:, j * LANE:(j + 1) * LANE])
                st[:, gs] = st[:, gs] * elast_x[:, gs] + _mm_tn(bg, _bf(xrem[:, gs]))
            y1 = (y_ref[r, :] + dx_ref[...] * xs) * _silu(z)
            outs, _ = _group_norm_fwd(y1, nw_ref[...])
            for g in range(2):
                ycat_ref[r, g * 512:(g + 1) * 512] = _bf(outs[g])

    return pl.pallas_call(
        body, name="ssd_fwd", grid=(n // SSD_SUB,),
        in_specs=[pl.BlockSpec((rows, SSD_W + LANE), lambda i: (i, OFF_Z // (SSD_W + LANE))),
                  pl.BlockSpec((rows, SSD_CONV), lambda i: (i, 0)), _spec(bias), _spec(alog), _spec(dskip_x), _spec(nw),
                  _full(e64.shape), _full(tril.shape), pl.BlockSpec(memory_space=pl.ANY)],
        out_specs=[pl.BlockSpec((rows, SSD_W), lambda i: (i, 1)), pl.BlockSpec((rows, SSD_W), lambda i: (i, 0)),
                   pl.BlockSpec((SSD_SUB, SSD_N, SSD_W), lambda i: (i, 0, 0))],
        out_shape=[SDS((S, D_INNER), BF16), SDS((S,SSD_W), F32), SDS((n, SSD_N, SSD_W), F32)],
        scratch_shapes=[pltpu.VMEM((SSD_N, SSD_W), F32), pltpu.VMEM((SSD_SUB, LANE, T), F32),
                        pltpu.VMEM((SSD_SUB, T, LANE), F32)],
        input_output_aliases={8: 0},
        compiler_params=_cp(("arbitrary",)),
    )(u, xbc, _arr(bias), _arr(alog), _arr(dskip_x), _arr(nw), _bfc(e64), _bfc(tril), ycat)


def _ssd_bwd(u, xbc, bias, alog, dskip_x, nw, consts, y_ssd, states, dycat, du, tok):
    S = u.shape[0]
    T = SSD_CHUNK
    n = S // T
    e64, tril, triu = consts
    e64t = np.ascontiguousarray(e64.T)

    def chunk(u_ref, xbc_ref, bias_ref, alog_ref, dx_ref, nw_ref, e64_ref, e64t_ref, tril_ref, triu_ref,
              y_ref, st_ref, dy_ref, du_ref, dxbc_ref, red_ref, dst, dl_s, cumt, dxdt_s, dy0_s, gb_s, gc_s, cum_e, cs_s):
        zdt = _f(u_ref[...])
        z = zdt[:, 0:SSD_W]
        xs = xbc_ref[:, 0:SSD_W]
        a_neg, dtpre, dt, ecum_x, erem_x, elast_x, dt_x = _ssd_common(
            zdt, bias_ref, alog_ref, tril_ref[...], e64_ref[...], cum_e, cumt)
        causal = _iota((T, T), 0) >= _iota((T, T), 1)
        lo = _iota((T, LANE), 1) < SSD_P
        xdt = xs * dt_x
        xrem = xdt * erem_x
        y = y_ref[...]
        dxv = dx_ref[...]
        nwv = nw_ref[...]
        sz = _silu(z)
        y0 = y + dxv * xs
        y1 = y0 * sz
        for g in range(2):
            gs = slice(g * 512, (g + 1) * 512)
            seg = y1[:, gs]
            inv = lax.rsqrt(jnp.mean(seg * seg, axis=-1, keepdims=True) + EPS)
            shat = seg * inv
            dyg = _f(dy_ref[:, gs])
            red_ref[0:1, gs] += jnp.sum(dyg * shat, axis=0, keepdims=True)
            dsh = dyg * nwv[:, gs]
            dy1g = inv * (dsh - shat * jnp.mean(dsh * shat, axis=-1, keepdims=True))
            du_ref[:, gs] = _bf(dy1g * y0[:, gs] * _dsilu(z[:, gs]))
            dy0_s[:, gs] = dy1g * sz[:, gs]
        dy0 = dy0_s[...]
        red_ref[1:2, :] += jnp.sum(dy0 * xs, axis=0, keepdims=True)
        dyin = dy0 * ecum_x
        lane = _iota((T, LANE), 1)
        dcum = jnp.zeros((T, LANE), F32)

        def decay_grad(h, gm):
            cs_s[pl.ds(h, 1), :] = jnp.sum(gm, axis=0, keepdims=True)
            return jnp.where(lane == h, jnp.sum(gm, axis=1, keepdims=True), 0.0)

        for g in range(2):
            gs = slice(g * 512, (g + 1) * 512)
            bg = _bf(xbc_ref[:, SSD_W + g * LANE:SSD_W + (g + 1) * LANE])
            cg = _bf(xbc_ref[:, SSD_W + 256 + g * LANE:SSD_W + 256 + (g + 1) * LANE])
            cb = _mm_nt(cg, bg)
            dst_f, st_f = dst[:, gs], st_ref[:, gs]
            dstg = _bf(dst_f)
            stg = _bf(st_f)
            dyin_g = _bf(dyin[:, gs])
            xrem_g = _bf(xrem[:, gs])
            dcb = jnp.zeros((T, T), F32)
            dxr = _mm(bg, dstg)
            dxdt_s[:, gs] = dxr * erem_x[:, gs]
            gc_s[:, gs] = dxr * xrem[:, gs]
            gb_s[:, gs] = dyin[:, gs] * _mm(cg, stg)
            dl_s[:, gs] = jnp.sum(dst_f * st_f, axis=0, keepdims=True) * elast_x[:, gs]
            for j in range(4):
                h0 = 8 * g + 2 * j
                cs = slice(h0 * SSD_P, (h0 + 2) * SSD_P)
                xp = xdt[:, cs]
                dyp = dy0[:, cs]
                x_lo, x_hi = _bf(jnp.where(lo, xp, 0.0)), _bf(jnp.where(lo, 0.0, xp))
                d_lo, d_hi = _bf(jnp.where(lo, dyp, 0.0)), _bf(jnp.where(lo, 0.0, dyp))
                l0 = _ssd_decay(cum_e, cumt, h0, causal)
                l1 = _ssd_decay(cum_e, cumt, h0 + 1, causal)
                s0 = cb * l0
                s1 = cb * l1
                ds0 = _mm_nt(d_lo, x_lo)
                ds1 = _mm_nt(d_hi, x_hi)
                dcb = dcb + ds0 * l0 + ds1 * l1
                dxdt_s[:, cs] += _mm_tn(_bf(s0), d_lo) + _mm_tn(_bf(s1), d_hi)
                dcum = dcum + decay_grad(h0, ds0 * s0) + decay_grad(h0 + 1, ds1 * s1)
            dcb_b = _bf(dcb)
            dxbc_ref[:, SSD_W + g * LANE:SSD_W + (g + 1) * LANE] = _mm_tn(dcb_b, cg) + _mm_nt(xrem_g, dstg)
            dxbc_ref[:, SSD_W + 256 + g * LANE:SSD_W + 256 + (g + 1) * LANE] = _mm(dcb_b, bg) + _mm_nt(dyin_g, stg)
            dst[:, gs] = dst_f * elast_x[:, gs] + _mm_tn(cg, dyin_g)
        dxdt = dxdt_s[...]
        dxbc_ref[:, 0:SSD_W] = dxdt * dt_x + dy0 * dxv
        e64t = e64t_ref[...]
        gc = gc_s[...]
        dlast_x = jnp.sum(gc, axis=0, keepdims=True) + dl_s[...]
        dlast = jnp.max(_sel_r(jnp.broadcast_to(dlast_x, (8, SSD_W)), e64t), axis=0, keepdims=True)
        dcum = (dcum - cs_s[...].T + _sel_r(gb_s[...] - gc, e64t)
                + jnp.where(_iota((T, LANE), 0) == T - 1, dlast, 0.0))
        dda = _sel_l(triu_ref[...], dcum)
        ddt = dda * a_neg + _sel_r(dxdt * xs, e64t)
        ddtpre = ddt * _sigmoid(dtpre)
        du_ref[:, SSD_W:SSD_W + LANE] = _bf(jnp.where(lane < SSD_HEADS, ddtpre, 0.0))
        red_ref[2:3, 0:LANE] += jnp.sum(ddtpre, axis=0, keepdims=True)
        red_ref[3:4, 0:LANE] += jnp.sum(dda * dt, axis=0, keepdims=True)

    def body(u_ref, xbc_ref, bias_ref, alog_ref, dx_ref, nw_ref, e64_ref, e64t_ref, tril_ref, triu_ref,
             y_ref, st_ref, dy_ref, du_in, tok_ref, du_ref, dxbc_ref, red_ref, dst, *scratch):
        del du_in, tok_ref

        @pl.when(pl.program_id(0) == 0)
        def _():
            dst[...] = jnp.zeros_like(dst)
            red_ref[...] = jnp.zeros_like(red_ref)
            scratch[-1][...] = jnp.zeros_like(scratch[-1])

        for sub in reversed(range(SSD_SUB)):
            rs = pl.ds(sub * T, T)
            chunk(u_ref.at[rs], xbc_ref.at[rs], bias_ref, alog_ref, dx_ref, nw_ref, e64_ref, e64t_ref, tril_ref, triu_ref,
                  y_ref.at[rs], st_ref.at[sub], dy_ref.at[rs], du_ref.at[rs], dxbc_ref.at[rs], red_ref, dst,
                  *[s.at[sub] for s in scratch])

    nb = n // SSD_SUB
    rows = SSD_SUB * T
    rev = lambda i: (nb - 1 - i, 0)
    sub_scratch = [(1, SSD_W), (LANE, T)] + [(T, SSD_W)] * 4 + [(T, LANE), (LANE, T)]
    return pl.pallas_call(
        body, name="ssd_bwd", grid=(nb,),
        in_specs=[pl.BlockSpec((rows, SSD_W + LANE), lambda i: (nb - 1 - i, OFF_Z // (SSD_W + LANE))),
                  pl.BlockSpec((rows, SSD_CONV), rev), _spec(bias), _spec(alog), _spec(dskip_x), _spec(nw),
                  _full(e64.shape), _full(e64t.shape), _full(tril.shape), _full(triu.shape),
                  pl.BlockSpec((rows, SSD_W), rev), pl.BlockSpec((SSD_SUB, SSD_N, SSD_W), lambda i: (nb - 1 - i, 0, 0)),
                  pl.BlockSpec((rows, SSD_W), lambda i: (nb - 1 - i, 1)), pl.BlockSpec(memory_space=pl.ANY),
                  pl.BlockSpec(memory_space=pl.ANY)],
        out_specs=[pl.BlockSpec((rows, SSD_W + LANE), lambda i: (nb - 1 - i, OFF_Z // (SSD_W + LANE))),
                   pl.BlockSpec((rows, SSD_CONV), rev), pl.BlockSpec((8, SSD_W), lambda i: (0, 0))],
        out_shape=[SDS((S, N_PAD), BF16), SDS((S, SSD_CONV), F32), SDS((8, SSD_W), F32)],
        scratch_shapes=[pltpu.VMEM((SSD_N, SSD_W), F32)] + [pltpu.VMEM((SSD_SUB,) + s, F32) for s in sub_scratch],
        input_output_aliases={13: 0},
        compiler_params=_cp(("arbitrary",)),
    )(u, xbc, _arr(bias), _arr(alog), _arr(dskip_x), _arr(nw), _bfc(e64), _bfc(e64t), _bfc(tril), _bfc(triu), y_ssd,
      states, dycat, du, tok)


def _bfc(a):
    return jnp.asarray(a, BF16)


def _outproj_fwd(ycat, wo, x, gate, tok):
    S = x.shape[0]
    tm = min(512, S)

    def body(yc_ref, wo_ref, x_ref, g_ref, tok_ref, xn_ref, y_ref):
        del tok_ref
        y = _mm(_bf(yc_ref[...]), wo_ref[...])
        y_ref[...] = y
        xn_ref[...] = x_ref[...] + g_ref[...] * y

    row = pl.BlockSpec((tm, D_MODEL), lambda i: (i, 0))
    return pl.pallas_call(
        body, name="outproj_fwd", grid=(S // tm,),
        in_specs=[pl.BlockSpec((tm, D_INNER), lambda i: (i, 0)), _full((D_INNER, D_MODEL)), row, _spec(gate),
                  pl.BlockSpec(memory_space=pl.ANY)],
        out_specs=[row, row],
        out_shape=[SDS((S, D_MODEL), F32), SDS((S, D_MODEL), F32)],
        compiler_params=_cp(("parallel",)),
    )(ycat, wo, x, _arr(gate), tok)


def _outproj_bwd(dxn, y, gate, ycat, wo):
    S = dxn.shape[0]
    tm = min(512, S)

    def body(dx_ref, y_ref, g_ref, yc_ref, wo_ref, dyc_ref, gwo_ref, dg_ref, acc):
        @pl.when(pl.program_id(0) == 0)
        def _():
            acc[...] = jnp.zeros_like(acc)
            dg_ref[...] = jnp.zeros_like(dg_ref)

        dxv = dx_ref[...]
        dy = _bf(dxv * g_ref[...])
        dg_ref[0:1, :] += jnp.sum(dxv * y_ref[...], axis=0, keepdims=True)
        dyc_ref[...] = _mm_nt(dy, wo_ref[...])
        acc[...] += _mm_tn(_bf(yc_ref[...]), dy)

        @pl.when(pl.program_id(0) == pl.num_programs(0) - 1)
        def _():
            gwo_ref[...] = acc[...].astype(BF16)

    row = pl.BlockSpec((tm, D_MODEL), lambda i: (i, 0))
    wide = pl.BlockSpec((tm, D_INNER), lambda i: (i, 0))
    return pl.pallas_call(
        body, name="outproj_bwd", grid=(S // tm,),
        in_specs=[row, row, _spec(gate), wide, _full((D_INNER, D_MODEL))],
        out_specs=[wide, _full((D_INNER, D_MODEL)), _full((8, D_MODEL))],
        out_shape=[SDS((S, D_INNER), F32), SDS((D_INNER, D_MODEL), BF16), SDS((8, D_MODEL), F32)],
        scratch_shapes=[pltpu.VMEM((D_INNER, D_MODEL), F32)],
        compiler_params=_cp(("arbitrary",)),
    )(dxn, y, _arr(gate), ycat, wo)


def _loss_head(x, fw, target):
    S = x.shape[0]
    tm = min(512, S)

    def body(x_ref, fw_ref, t_ref, dx_ref, red_ref):
        @pl.when(pl.program_id(0) == 0)
        def _():
            red_ref[...] = jnp.zeros_like(red_ref)

        xv = x_ref[...]
        fwv = fw_ref[...]
        inv = lax.rsqrt(jnp.mean(xv * xv, axis=-1, keepdims=True) + EPS)
        xhat = xv * inv
        err = xhat * fwv - t_ref[...]
        col = jnp.sum(err * err, axis=0, keepdims=True)
        red_ref[1:2, :] += jnp.broadcast_to(jnp.sum(col, axis=1, keepdims=True) * (0.5 / D_MODEL), (1, D_MODEL))
        dy = err * (1.0 / D_MODEL)
        red_ref[0:1, :] += jnp.sum(dy * xhat, axis=0, keepdims=True)
        dxhat = dy * fwv
        dx_ref[...] = inv * (dxhat - xhat * jnp.mean(dxhat * xhat, axis=-1, keepdims=True))

    row = pl.BlockSpec((tm, D_MODEL), lambda i: (i, 0))
    return pl.pallas_call(
        body, name="loss_head", grid=(S // tm,),
        in_specs=[row, _vec(D_MODEL), row],
        out_specs=[row, _full((8, D_MODEL))],
        out_shape=[SDS((S, D_MODEL), F32), SDS((8, D_MODEL), F32)],
        compiler_params=_cp(("arbitrary",)),
    )(x, fw, target)


ADA_COLS = 3 * D_MODEL // N_DEV


def _ada_fwd(c_all, w_ada, b_cols):
    def body(c_ref, w_ref, b_ref, out_ref):
        out_ref[...] = _mm(_bf(_silu(c_ref[...])), _bf(w_ref[...])) + b_ref[...]

    return pl.pallas_call(
        body, name="ada_fwd", grid=(DEPTH,),
        in_specs=[_full((N_DEV, D_MODEL)), pl.BlockSpec((None, D_MODEL, ADA_COLS), lambda l: (l, 0, 0)),
                  pl.BlockSpec((None, 1, ADA_COLS), lambda l: (l, 0, 0))],
        out_specs=pl.BlockSpec((None, N_DEV, ADA_COLS), lambda l: (l, 0, 0)),
        out_shape=SDS((DEPTH, N_DEV, ADA_COLS), F32),
        compiler_params=_cp(("parallel",)),
    )(c_all, w_ada, b_cols)


def _ada_bwd(ct_pad, dmod_pad):
    def body(c_ref, d_ref, out_ref):
        out_ref[...] = _mm(_bf(_silu(c_ref[...])), _bf(d_ref[...]))

    return pl.pallas_call(
        body, name="ada_bwd", grid=(DEPTH,),
        in_specs=[_full((D_MODEL, LANE)), pl.BlockSpec((None, LANE, ADA_COLS), lambda l: (l, 0, 0))],
        out_specs=pl.BlockSpec((None, D_MODEL, ADA_COLS), lambda l: (l, 0, 0)),
        out_shape=SDS((DEPTH, D_MODEL, ADA_COLS), F32),
        compiler_params=_cp(("parallel",)),
    )(ct_pad, dmod_pad)


def _adamw(parts, w, m, v, name, own=None, layers=None, prev=None):
    n, L, R, C = parts.shape
    lo, hi = layers or (0, L)
    tr = R
    while tr * C * 4 > (1 << 20) and tr % 16 == 0:
        tr //= 2
    first = 1 if own is None else 2

    def body(*refs):
        p_ref = refs[0]
        w_ref, m_ref, v_ref = refs[first:first + 3]
        g_ref, d_ref, mo_ref, vo_ref = refs[-4:]

        def part(k):
            if own is None:
                return p_ref[k].astype(F32)
            me = 4 * lax.axis_index("x") + 2 * lax.axis_index("y") + lax.axis_index("c")
            return jnp.where(me == k, refs[1][...], p_ref[k]).astype(F32)

        g = part(0)
        for k in range(1, n):
            g = g + part(k)
        mn = ADAM_B1 * m_ref[...] + (1.0 - ADAM_B1) * g
        vn = ADAM_B2 * v_ref[...] + (1.0 - ADAM_B2) * (g * g)
        m_hat = mn / (1.0 - ADAM_B1 ** ADAM_STEP)
        v_hat = vn / (1.0 - ADAM_B2 ** ADAM_STEP)
        g_ref[...] = g
        d_ref[...] = -ADAM_LR * (m_hat / (jnp.sqrt(v_hat) + ADAM_EPS) + ADAM_WD * w_ref[...])
        mo_ref[...] = mn
        vo_ref[...] = vn

    blk = pl.BlockSpec((None, tr, C), lambda l, i: (lo + l, i, 0))
    own_blk = [] if own is None else [pl.BlockSpec((None, tr, C), lambda l, i: (l, i, 0))]
    n_blk = 3 if own is None else 4
    return pl.pallas_call(
        body, name=name, grid=(hi - lo, R // tr),
        in_specs=[pl.BlockSpec((n, None, tr, C), lambda l, i: (0, lo + l, i, 0))] + own_blk + [blk] * 3
        + ([] if prev is None else [ANY] * 4),
        out_specs=[blk] * 4,
        out_shape=[SDS((L, R, C), F32)] * 4,
        input_output_aliases={} if prev is None else {1 + n_blk + k: k for k in range(4)},
        compiler_params=_cp(("parallel", "parallel")),
    )(parts, *([] if own is None else [own]), w, m, v, *([] if prev is None else prev))


MESH = pl.DeviceIdType.MESH
ANY = pl.BlockSpec(memory_space=pl.ANY)


def _all_gather(v, name):
    def body(v_ref, out_ref, send_sems, recv_sems, local_sem):
        x, y, c = lax.axis_index("x"), lax.axis_index("y"), lax.axis_index("c")
        me, sibling = (x, y, c), (x, y, 1 - c)
        chips = [(1 - x, y), (x, 1 - y), (1 - x, 1 - y)]

        def slot(px, py, pc):
            return out_ref.at[4 * px + 2 * py + pc]

        def copy(k, block, to, src=None):
            return pltpu.make_async_remote_copy(
                src_ref=slot(*block) if src is None else src, dst_ref=slot(*block),
                send_sem=send_sems.at[k], recv_sem=recv_sems.at[k], device_id=to, device_id_type=MESH)

        mine = pltpu.make_async_copy(v_ref, slot(*me), local_sem)
        mine.start()
        first = [copy(0, me, sibling, src=v_ref)]
        first += [copy(1 + j, me, (*chip, c), src=v_ref) for j, chip in enumerate(chips)]
        for cp in first:
            cp.start()
        passed = [copy(4 + j, (*chip, c), sibling) for j, chip in enumerate(chips)]
        for j, chip in enumerate(chips):
            copy(1 + j, (*chip, c), me).wait_recv()
            passed[j].start()
        copy(0, sibling, me).wait_recv()
        for j, chip in enumerate(chips):
            copy(4 + j, (*chip, 1 - c), me).wait_recv()
        for cp in first + passed:
            cp.wait_send()
        mine.wait()

    return pl.pallas_call(
        body, name=name, in_specs=[ANY], out_specs=ANY,
        out_shape=SDS((N_DEV,) + v.shape, v.dtype),
        scratch_shapes=[pltpu.SemaphoreType.DMA((7,)), pltpu.SemaphoreType.DMA((7,)), pltpu.SemaphoreType.DMA],
    )(v)


def _all_to_all(v, name):
    def body(v_ref, out_ref, send_sems, recv_sems, local_sem):
        x, y, c = lax.axis_index("x"), lax.axis_index("y"), lax.axis_index("c")
        mine_idx = 4 * x + 2 * y + c
        mine = pltpu.make_async_copy(v_ref.at[mine_idx], out_ref.at[mine_idx], local_sem)
        mine.start()
        sends, recvs = [], []
        for k in range(1, N_DEV):
            px = 1 - x if k & 4 else x
            py = 1 - y if k & 2 else y
            pc = 1 - c if k & 1 else c
            peer_idx = 4 * px + 2 * py + pc
            sems = dict(send_sem=send_sems.at[k - 1], recv_sem=recv_sems.at[k - 1], device_id=(px, py, pc),
                        device_id_type=MESH)
            sends.append(pltpu.make_async_remote_copy(src_ref=v_ref.at[peer_idx], dst_ref=out_ref.at[mine_idx], **sems))
            recvs.append(pltpu.make_async_remote_copy(src_ref=v_ref.at[peer_idx], dst_ref=out_ref.at[peer_idx], **sems))
        for cp in sends:
            cp.start()
        for cp in recvs:
            cp.wait_recv()
        for cp in sends:
            cp.wait_send()
        mine.wait()

    return pl.pallas_call(
        body, name=name, in_specs=[ANY], out_specs=ANY,
        out_shape=SDS(v.shape, v.dtype),
        scratch_shapes=[pltpu.SemaphoreType.DMA((7,)), pltpu.SemaphoreType.DMA((7,)), pltpu.SemaphoreType.DMA],
    )(v)


HBM_SPEC = pl.BlockSpec(memory_space=pltpu.HBM)
SEM_SPEC = pl.BlockSpec(memory_space=pltpu.SEMAPHORE)
EFFECT = pltpu.SideEffectType.DATAFLOW_SIDE_EFFECTING


EXCHANGE_PEERS = {"gather": range(1, N_DEV), "scatter": range(1, N_DEV), "chip": (1, 2, 4, 6), "pass": (2, 4, 6)}


def _exchange_copies(srcs, lands, send_sems, recv_sems, mode, layer):
    x, y, c = lax.axis_index("x"), lax.axis_index("y"), lax.axis_index("c")
    me = 4 * x + 2 * y + c
    copies = []
    for a, (src, land) in enumerate(zip(srcs, lands)):
        for k in EXCHANGE_PEERS[mode]:
            px = 1 - x if k & 4 else x
            py = 1 - y if k & 2 else y
            pc = 1 - c if k & 1 else c
            peer = 4 * px + 2 * py + pc
            if mode == "scatter":
                s, d, to = src.at[peer], land.at[me, layer], (px, py, pc)
            elif mode == "pass":
                s, d, to = land.at[peer], land.at[peer], (x, y, 1 - c)
            else:
                s, d, to = src, land.at[me], (px, py, pc)
            n = 7 * a + k - 1
            copies.append(pltpu.make_async_remote_copy(
                src_ref=s, dst_ref=d, send_sem=send_sems.at[n], recv_sem=recv_sems.at[n], device_id=to,
                device_id_type=MESH))
    return copies


def _exchange_start(name, srcs, lands, mode, layer=0, after=None):
    n = len(srcs)

    def body(*refs):
        send_sems, recv_sems = refs[-2 * n - 3], refs[-2 * n - 2]
        for cp in _exchange_copies(refs[:n], refs[n:2 * n], send_sems, recv_sems, mode, layer):
            cp.start()
        refs[-1][...] = jnp.zeros_like(refs[-1])

    arrays = list(srcs) + list(lands)
    sems = pltpu.SemaphoreType.DMA((7 * n,))
    out = pl.pallas_call(
        body, name=name,
        out_shape=(sems, sems, *[pltpu.HBM(v.shape, v.dtype) for v in arrays], SDS((8, LANE), F32)),
        in_specs=[HBM_SPEC] * (2 * n) + ([ANY] if after is not None else []),
        out_specs=(SEM_SPEC, SEM_SPEC, *[HBM_SPEC] * (2 * n), pl.BlockSpec(memory_space=pltpu.VMEM)),
        input_output_aliases={i: 2 + i for i in range(2 * n)},
        compiler_params=pltpu.CompilerParams(has_side_effects=EFFECT),
    )(*[pltpu.with_memory_space_constraint(v, pltpu.HBM) for v in arrays], *([after] if after is not None else []))
    return dict(sems=out[:2], srcs=out[2:2 + n], lands=out[2 + n:2 + 2 * n], token=out[-1], mode=mode,
                layer=layer)


def _exchange_wait(name, st, after, also=()):
    n = len(st["srcs"])

    def body(*refs):
        send_sems, recv_sems = refs[2 * n], refs[2 * n + 1]
        for cp in _exchange_copies(refs[:n], refs[n:2 * n], send_sems, recv_sems, st["mode"], st["layer"]):
            cp.wait_send()
            cp.wait_recv()

    arrays = list(st["srcs"]) + list(st["lands"])
    out = pl.pallas_call(
        body, name=name,
        out_shape=tuple(pltpu.HBM(v.shape, v.dtype) for v in arrays),
        in_specs=[HBM_SPEC] * (2 * n) + [SEM_SPEC, SEM_SPEC] + [ANY] * (1 + len(also)),
        out_specs=tuple([HBM_SPEC] * (2 * n)),
        input_output_aliases={i: i for i in range(2 * n)},
        compiler_params=pltpu.CompilerParams(has_side_effects=EFFECT),
    )(*arrays, *st["sems"], after, *also)
    st["srcs"] = out[:n]
    return out[n:]


def _exchange_relay(name, st, after, also=()):
    n = len(st["srcs"])

    def body(*refs):
        def copies(send_sems, recv_sems, mode):
            keys = [(a, k) for a in range(n) for k in EXCHANGE_PEERS[mode]]
            return dict(zip(keys, _exchange_copies(refs[:n], refs[n:2 * n], send_sems, recv_sems, mode, st["layer"]),
                            strict=True))

        arrived = copies(refs[2 * n], refs[2 * n + 1], st["mode"])
        onward = copies(refs[-2 * n - 3], refs[-2 * n - 2], "pass")
        for key, cp in arrived.items():
            cp.wait_recv()
            if key in onward:
                onward[key].start()
        for cp in arrived.values():
            cp.wait_send()
        refs[-1][...] = jnp.zeros_like(refs[-1])

    arrays = list(st["srcs"]) + list(st["lands"])
    sems = pltpu.SemaphoreType.DMA((7 * n,))
    out = pl.pallas_call(
        body, name=name,
        out_shape=(sems, sems, *[pltpu.HBM(v.shape, v.dtype) for v in arrays], SDS((8, LANE), F32)),
        in_specs=[HBM_SPEC] * (2 * n) + [SEM_SPEC, SEM_SPEC] + [ANY] * (1 + len(also)),
        out_specs=(SEM_SPEC, SEM_SPEC, *[HBM_SPEC] * (2 * n), pl.BlockSpec(memory_space=pltpu.VMEM)),
        input_output_aliases={i: 2 + i for i in range(2 * n)},
        compiler_params=pltpu.CompilerParams(has_side_effects=EFFECT),
    )(*arrays, *st["sems"], after, *also)
    return dict(sems=out[:2], srcs=out[2:2 + n], lands=out[2 + n:2 + 2 * n], token=out[-1], mode="pass",
                layer=st["layer"])


def _exchange_wait_all(name, sts, lands, ids, after):
    counts = [len(st["srcs"]) for st in sts]
    ns, nl = sum(counts), len(lands)

    def body(*refs):
        at = 0
        for e, st in enumerate(sts):
            own_lands = [refs[ns + i] for i in ids[e]]
            send_sems, recv_sems = refs[ns + nl + 2 * e], refs[ns + nl + 2 * e + 1]
            for cp in _exchange_copies(refs[at:at + counts[e]], own_lands, send_sems, recv_sems, st["mode"],
                                       st["layer"]):
                cp.wait_send()
                cp.wait_recv()
            at += counts[e]

    arrays = [s for st in sts for s in st["srcs"]] + list(lands)
    out = pl.pallas_call(
        body, name=name,
        out_shape=tuple(pltpu.HBM(v.shape, v.dtype) for v in arrays),
        in_specs=[HBM_SPEC] * len(arrays) + [SEM_SPEC] * (2 * len(sts)) + [ANY],
        out_specs=tuple([HBM_SPEC] * len(arrays)),
        input_output_aliases={i: i for i in range(len(arrays))},
        compiler_params=pltpu.CompilerParams(has_side_effects=EFFECT),
    )(*arrays, *[s for st in sts for s in st["sems"]], after)
    at = 0
    for e, st in enumerate(sts):
        st["srcs"] = out[at:at + counts[e]]
        at += counts[e]
    return list(out[ns:])


_IN_PIECES = ([(1024, 3072)]
              + [r for t in range(4) for r in ((LANE * t, LANE * (t + 1)), (512 + LANE * t, 512 + LANE * (t + 1)))]
              + [(4096, 5632), (3072, 4096), (5632, 5648)])


def _permute_in(w):
    pad = jnp.zeros(w.shape[:-1] + (N_PAD - N_IN,), w.dtype)
    return jnp.concatenate([w[..., a:b] for a, b in _IN_PIECES] + [pad], axis=-1)


def _unpermute_in(g):
    ax = [g[..., OFF_LRU + 2 * LANE * t:OFF_LRU + 2 * LANE * t + LANE] for t in range(4)]
    ag = [g[..., OFF_LRU + 2 * LANE * t + LANE:OFF_LRU + 2 * LANE * (t + 1)] for t in range(4)]
    return jnp.concatenate(ax + ag + [g[..., 0:2048], g[..., OFF_Z:OFF_Z + SSD_W], g[..., OFF_XBC:OFF_XBC + SSD_CONV],
                                      g[..., OFF_Z + SSD_W:OFF_Z + SSD_W + SSD_HEADS]], axis=-1)


SHARD_COLS = N_IN // N_DEV


def _in_segments():
    segs, pos = [], 0
    for a, b in _IN_PIECES:
        for i in range(N_DEV):
            lo, hi = max(a, SHARD_COLS * i), min(b, SHARD_COLS * (i + 1))
            if lo < hi:
                segs.append((i, lo - SHARD_COLS * i, hi - lo, pos + lo - a))
        pos += b - a
    return segs


RELAYOUT_ROWS = 512


def _relayout_in(land, own):
    def body(land_ref, own_ref, out_ref):
        me = 4 * lax.axis_index("x") + 2 * lax.axis_index("y") + lax.axis_index("c")
        out_ref[:, N_IN:N_PAD] = jnp.zeros((RELAYOUT_ROWS, N_PAD - N_IN), BF16)
        for i, j, wd, p in _in_segments():
            out_ref[:, p:p + wd] = jnp.where(me == i, own_ref[:, j:j + wd], land_ref[i, :, j:j + wd])

    return pl.pallas_call(
        body, name="relayout_in", grid=(D_MODEL // RELAYOUT_ROWS,),
        in_specs=[pl.BlockSpec((N_DEV, RELAYOUT_ROWS, SHARD_COLS), lambda r: (0, r, 0)),
                  pl.BlockSpec((RELAYOUT_ROWS, SHARD_COLS), lambda r: (r, 0))],
        out_specs=pl.BlockSpec((RELAYOUT_ROWS, N_PAD), lambda r: (r, 0)),
        out_shape=SDS((D_MODEL, N_PAD), BF16),
        compiler_params=_cp(("parallel",)),
    )(land, own)


def _relayout_grad(g):
    def body(g_ref, out_ref):
        for i, j, wd, p in _in_segments():
            out_ref[i, :, j:j + wd] = g_ref[:, p:p + wd].astype(BF16)

    return pl.pallas_call(
        body, name="relayout_grad", grid=(D_MODEL // RELAYOUT_ROWS,),
        in_specs=[pl.BlockSpec((RELAYOUT_ROWS, N_PAD), lambda r: (r, 0))],
        out_specs=pl.BlockSpec((N_DEV, RELAYOUT_ROWS, SHARD_COLS), lambda r: (0, r, 0)),
        out_shape=SDS((N_DEV, D_MODEL, SHARD_COLS), BF16),
        compiler_params=_cp(("parallel",)),
    )(g)


def _block_diag(w):
    w4 = w.reshape(DEPTH, 4, 2, 64, 64)
    z = jnp.zeros((DEPTH, 4, 64, 64), w.dtype)
    top = jnp.concatenate([w4[:, :, 0], z], axis=-1)
    bot = jnp.concatenate([z, w4[:, :, 1]], axis=-1)
    return jnp.concatenate([top, bot], axis=2).astype(BF16)


def _diag_blocks(g):
    return jnp.stack([g[:, :, :64, :64], g[:, :, 64:, 64:]], axis=2).reshape(DEPTH, 8, 64, 64)


def _pad_lanes(v):
    return jnp.pad(v, ((0, 0), (0, LANE - v.shape[1])))


def _lower_bounds(logits):
    p = jax.nn.softmax(logits, axis=0)
    return p, jnp.cumsum(p, axis=0) - p[0]


def _lower_bounds_bwd(p, dlb):
    dp = jnp.cumsum(dlb[::-1], axis=0)[::-1]
    dp = dp.at[0].add(-jnp.sum(dlb, axis=0))
    return p * (dp - jnp.sum(dp * p, axis=0, keepdims=True))


SMALL = ["norm_w", "b_ada", "lru_conv_b", "lru_wa", "lru_ba", "lru_wx", "lru_bx", "lru_lambda", "hg_lb_logits",
         "hg_norm_w", "ssd_conv_b", "ssd_dt_bias", "ssd_a_log", "ssd_d", "ssd_norm_w", "final_norm_w"]
WEIGHTS = ["norm_w", "w_ada", "b_ada", "w_in", "lru_conv_w", "lru_conv_b", "lru_wa", "lru_ba", "lru_wx", "lru_bx",
           "lru_lambda", "hg_lb_logits", "hg_norm_w", "ssd_conv_w", "ssd_conv_b", "ssd_dt_bias", "ssd_a_log", "ssd_d",
           "ssd_norm_w", "w_out", "final_norm_w"]
INPUTS = ["x", "c"] + WEIGHTS + ["loss_target"] + ["m_" + n for n in WEIGHTS] + ["v_" + n for n in WEIGHTS]
SMALL_ROW = 1024


def _small_rows(like):
    out, off = {}, 0
    for n in SMALL:
        rows = -(-int(np.prod(like[n].shape)) // (8 * SMALL_ROW)) * 8
        out[n] = (off, rows)
        off += rows
    return out, off


def _flatten_small(d, prefix="", last=0.0):
    table, _ = _small_rows({n: d[prefix + n] for n in SMALL})
    pieces = []
    for n in SMALL:
        flat = d[prefix + n].reshape(-1)
        pieces.append(jnp.pad(flat, (0, table[n][1] * SMALL_ROW - flat.shape[0])).reshape(-1, SMALL_ROW))
    return jnp.concatenate(pieces + [jnp.full((8, SMALL_ROW), last, F32)], axis=0)


def _split_small(packed, like):
    table, _ = _small_rows(like)
    out = {}
    for n in SMALL:
        off, rows = table[n]
        size = int(np.prod(like[n].shape))
        out[n] = packed[off:off + rows].reshape(-1)[:size].reshape(like[n].shape)
    return out


def _local_step(x, mod, target, w, fetch, emit):
    S = x.shape[0]
    mall = _bfc(_hg_consts())
    mall_t = _bfc(_hg_consts().T)
    consts = _ssd_consts()
    p_lb, lbs = _lower_bounds(w["hg_lb_logits"])
    no_tok = jnp.zeros((8, LANE), F32)
    wa, wx = _block_diag(w["lru_wa"]), _block_diag(w["lru_wx"])
    ba, bx = w["lru_ba"].reshape(DEPTH, 1, LRU_W), w["lru_bx"].reshape(DEPTH, 1, LRU_W)
    lru_cb, lam, ssd_cb = w["lru_conv_b"][:, None], w["lru_lambda"][:, None], w["ssd_conv_b"][:, None]
    bias, alog = _pad_lanes(w["ssd_dt_bias"]), _pad_lanes(w["ssd_a_log"])
    dskip = jnp.repeat(w["ssd_d"], SSD_P, axis=1)
    saved = []
    for l in range(DEPTH):
        w_in_l, w_out_l, token = fetch(l, x)
        shift, scale, gate = (_Row(mod, l, D_MODEL, k) for k in range(3))
        nw = _Row(w["norm_w"], l)
        u, h = _inproj_fwd(x, nw, scale, shift, w_in_l, no_tok if token is None else token)
        ycat = lax.empty((S, D_INNER), BF16)
        lru_args = (l, u, w["lru_conv_w"], lru_cb, wa, ba, wx, bx, lam)
        ycat, h_lru = _lru_fwd(*lru_args, ycat)
        hg_args = (u, _Row(lbs, l), _Row(w["hg_norm_w"], l), mall)
        ycat, o_b, hg_st = _hg_fwd(*hg_args, ycat)
        xbc = _ssdconv_fwd(l, u, w["ssd_conv_w"], ssd_cb)
        ssd_args = (u, xbc, _Row(bias, l), _Row(alog, l), _Row(dskip, l), _Row(w["ssd_norm_w"], l), consts)
        ycat, y_ssd, ssd_st = _ssd_fwd(*ssd_args, ycat)
        token = fetch(l, y_ssd, late=True)
        if callable(w_out_l):
            w_out_l = w_out_l()
        x_new, y = _outproj_fwd(ycat, w_out_l, x, gate, no_tok if token is None else token)
        saved.append((x, u, h, ycat, nw, scale, gate, w_in_l, w_out_l, lru_args, h_lru, hg_args, o_b, hg_st, ssd_args,
                      y_ssd, ssd_st, y))
        x = x_new
    dx, red = _loss_head(x, w["final_norm_w"][None, :], target)
    loss = red[1, 0]
    reds = {k: [None] * DEPTH for k in ("in", "gate", "lru", "wa", "wx", "hg", "conv", "ssd")}
    for l in reversed(range(DEPTH)):
        (x, u, h, ycat, nw, scale, gate, w_in_l, w_out_l, lru_args, h_lru, hg_args, o_b, hg_st, ssd_args, y_ssd, ssd_st,
         y) = saved[l]
        dycat, g_out, reds["gate"][l] = _outproj_bwd(dx, y, gate, ycat, w_out_l)
        token = emit(l, "w_out", g_out)
        du = lax.empty((S, N_PAD), BF16)
        du, dxbc, reds["ssd"][l] = _ssd_bwd(*ssd_args, y_ssd, ssd_st, dycat, du, no_tok if token is None else token)
        du, reds["conv"][l] = _ssdconv_bwd(l, u, w["ssd_conv_w"], ssd_cb, dxbc, du)
        du, reds["hg"][l] = _hg_bwd(*hg_args, mall_t, o_b, hg_st, dycat, du)
        du, reds["lru"][l], reds["wa"][l], reds["wx"][l] = _lru_bwd(*lru_args, h_lru, dycat, du)
        token = emit(l, "w_in", functools.partial(_inproj_bwd_w, h, du))
        dx, reds["in"][l] = _inproj_bwd_x(du, w_in_l, x, nw, scale, dx, no_tok if token is None else token)
    r = {k: jnp.stack(v) for k, v in reds.items()}
    g = {n: None for n in WEIGHTS}
    g["final_norm_w"] = red[0]
    g["norm_w"] = r["in"][:, 2]
    dmod = jnp.concatenate([r["in"][:, 0], r["in"][:, 1], r["gate"][:, 0]], axis=1)
    g["lru_conv_w"], g["lru_conv_b"] = r["lru"][:, 0:4], r["lru"][:, 4]
    g["lru_ba"], g["lru_bx"] = r["lru"][:, 5].reshape(DEPTH, 8, 64), r["lru"][:, 6].reshape(DEPTH, 8, 64)
    g["lru_lambda"] = r["lru"][:, 7]
    g["lru_wa"], g["lru_wx"] = _diag_blocks(r["wa"]), _diag_blocks(r["wx"])
    g["hg_norm_w"] = r["hg"][:, 0]
    g["hg_lb_logits"] = _lower_bounds_bwd(p_lb, r["hg"][:, 1])
    g["ssd_conv_w"], g["ssd_conv_b"] = r["conv"][:, 0:4], r["conv"][:, 4]
    g["ssd_norm_w"] = r["ssd"][:, 0]
    g["ssd_d"] = r["ssd"][:, 1].reshape(DEPTH, SSD_HEADS, SSD_P).sum(-1)
    g["ssd_dt_bias"] = r["ssd"][:, 2, :SSD_HEADS]
    g["ssd_a_log"] = -r["ssd"][:, 3, :SSD_HEADS] * jnp.exp(w["ssd_a_log"])
    return loss, dx, dmod, g


def kernel(x, c, norm_w, w_ada, b_ada, w_in, lru_conv_w, lru_conv_b, lru_wa, lru_ba, lru_wx, lru_bx, lru_lambda, hg_lb_logits, hg_norm_w, ssd_conv_w, ssd_conv_b, ssd_dt_bias, ssd_a_log, ssd_d, ssd_norm_w, w_out, final_norm_w, loss_target, m_norm_w, m_w_ada, m_b_ada, m_w_in, m_lru_conv_w, m_lru_conv_b, m_lru_wa, m_lru_ba, m_lru_wx, m_lru_bx, m_lru_lambda, m_hg_lb_logits, m_hg_norm_w, m_ssd_conv_w, m_ssd_conv_b, m_ssd_dt_bias, m_ssd_a_log, m_ssd_d, m_ssd_norm_w, m_w_out, m_final_norm_w, v_norm_w, v_w_ada, v_b_ada, v_w_in, v_lru_conv_w, v_lru_conv_b, v_lru_wa, v_lru_ba, v_lru_wx, v_lru_bx, v_lru_lambda, v_hg_lb_logits, v_hg_norm_w, v_ssd_conv_w, v_ssd_conv_b, v_ssd_dt_bias, v_ssd_a_log, v_ssd_d, v_ssd_norm_w, v_w_out, v_final_norm_w):
    return _step(x, c, norm_w, w_ada, b_ada, w_in, lru_conv_w, lru_conv_b, lru_wa, lru_ba, lru_wx, lru_bx, lru_lambda, hg_lb_logits, hg_norm_w, ssd_conv_w, ssd_conv_b, ssd_dt_bias, ssd_a_log, ssd_d, ssd_norm_w, w_out, final_norm_w, loss_target, m_norm_w, m_w_ada, m_b_ada, m_w_in, m_lru_conv_w, m_lru_conv_b, m_lru_wa, m_lru_ba, m_lru_wx, m_lru_bx, m_lru_lambda, m_hg_lb_logits, m_hg_norm_w, m_ssd_conv_w, m_ssd_conv_b, m_ssd_dt_bias, m_ssd_a_log, m_ssd_d, m_ssd_norm_w, m_w_out, m_final_norm_w, v_norm_w, v_w_ada, v_b_ada, v_w_in, v_lru_conv_w, v_lru_conv_b, v_lru_wa, v_lru_ba, v_lru_wx, v_lru_bx, v_lru_lambda, v_hg_lb_logits, v_hg_norm_w, v_ssd_conv_w, v_ssd_conv_b, v_ssd_dt_bias, v_ssd_a_log, v_ssd_d, v_ssd_norm_w, v_w_out, v_final_norm_w)


def _step(*args):
    a = dict(zip(INPUTS, args, strict=True))
    me = 4 * lax.axis_index("x") + 2 * lax.axis_index("y") + lax.axis_index("c")
    x, target = a["x"][0], a["loss_target"][0]

    c_all = _all_gather(a["c"], "gather_c")[:, 0, :]
    b_cols = lax.dynamic_slice_in_dim(a["b_ada"], me * ADA_COLS, ADA_COLS, axis=1)[:, None, :]
    mod_parts = _all_gather(_ada_fwd(c_all, a["w_ada"], b_cols), "gather_mod")
    mod = lax.dynamic_index_in_dim(mod_parts, me, axis=2, keepdims=False)
    mod = mod.transpose(1, 0, 2).reshape(DEPTH, 3 * D_MODEL)

    w = {n: a[n] for n in SMALL}

    w_in_b = [a["w_in"][l].astype(BF16) for l in range(DEPTH)]
    w_out_b = a["w_out"].astype(BF16)
    conv_own = jnp.concatenate([a["lru_conv_w"], a["ssd_conv_w"]], axis=-1)
    cols, rows_out = N_IN // N_DEV, D_INNER // N_DEV

    def gather_start(l, after):
        srcs = [w_in_b[l], conv_own if l == 0 else w_out_b[l]]
        lands = [lax.empty((N_DEV,) + s.shape, s.dtype) for s in srcs]
        return _exchange_start(f"gather_start_{l}", srcs, lands, "chip", after=after)

    def gather_pass(name, st, after, also=()):
        return _exchange_wait(name + "_passed", _exchange_relay(name + "_pass", st, after, also), after)

    gathers = {0: gather_start(0, mod)}
    passing = {}
    out0 = {"st": _exchange_start("gather_start_0_out", [w_out_b[0]], [lax.empty((N_DEV,) + w_out_b[0].shape, BF16)],
                                  "chip", after=gathers[0]["token"])}

    def fetch(l, x_l, late=False):
        if late:
            if l + 1 == DEPTH:
                return None
            if l == 0:
                out0["st"] = _exchange_relay("gather_0_out_pass", out0["st"], x_l)
                x_l = out0["st"]["token"]
            passing[l + 1] = _exchange_relay(f"gather_{l + 1}_pass", gathers[l + 1], x_l)
            if l == 0:
                landed = _exchange_wait("gather_0_out_passed", out0["st"], passing[1]["token"])
                out0["w"] = lax.dynamic_update_index_in_dim(landed[0], w_out_b[0], me, 0).reshape(D_INNER, D_MODEL)
            return passing[l + 1]["token"]
        if l == 0:
            landed = gather_pass("gather_0", gathers[0], x_l,
                                 also=(a["w_in"], a["m_w_in"], a["v_w_in"], out0["st"]["token"]))
            conv =lax.dynamic_update_index_in_dim(landed[1], conv_own, me, 0).transpose(1, 2, 0, 3)
            w["lru_conv_w"] = conv[..., :64].reshape(DEPTH, 4, LRU_W)
            w["ssd_conv_w"] = conv[..., 64:].reshape(DEPTH, 4, SSD_CONV)
            after, w_out_l = landed[0], lambda: out0["w"]
        else:
            landed = _exchange_wait(f"gather_{l}_passed", passing[l], x_l)
            after = lax.dynamic_update_index_in_dim(landed[1], w_out_b[l], me, 0)
            w_out_l = after.reshape(D_INNER, D_MODEL)
        token = None
        if l + 1 < DEPTH:
            gathers[l + 1] = gather_start(l + 1, after)
            token = gathers[l + 1]["token"]
        return _relayout_in(landed[0], w_in_b[l]), w_out_l, token

    PROJ = ("w_in", "w_out")
    scatters = {}
    lands = [lax.empty((N_DEV, DEPTH, D_MODEL, cols), BF16), lax.empty((N_DEV, DEPTH, rows_out, D_MODEL), BF16)]
    own = [None] * DEPTH

    deferred, g_out = {}, {}

    def emit(l, name, grad, after=None):
        if name == "w_out" and l > 0:
            g_out[l] = grad
            return None
        if name == "w_in" and l == 0 and after is None:
            deferred["w_in"] = grad
            return None
        if name == "w_in":
            grad = grad(jnp.zeros((8, LANE), F32) if after is None else after)
        if l == 0:
            k = PROJ.index(name)
            src = _relayout_grad(grad) if name == "w_in" else grad.reshape(N_DEV, rows_out, D_MODEL)
            st = _exchange_start(f"scatter_start_0_{name}", [src], [lands[k]], "scatter", layer=0, after=after)
            scatters[name] = st
            lands[k] = st["lands"][0]
            return st["token"]
        srcs = [_relayout_grad(grad), g_out[l].reshape(N_DEV, rows_out, D_MODEL)]
        st = _exchange_start(f"scatter_start_{l}", srcs, lands, "scatter", layer=l, after=after)
        scatters[l] = st
        lands[:] = st["lands"]
        return st["token"]

    loss_own, dx, dmod, g = _local_step(x, mod, target, w, fetch, emit)

    def sharded(name, parts, own=None, **kw):
        return _adamw(parts, a[name], a["m_" + name], a["v_" + name], "adamw_" + name + kw.pop("tag", ""), own=own, **kw)

    g["b_ada"] = dmod
    small_own = _flatten_small(g, last=loss_own)
    small_st = _exchange_start("gather_small", [small_own], [lax.empty((N_DEV,) + small_own.shape, F32)], "chip",
                               after=dx)
    big = {}
    after = emit(0, "w_in", deferred["w_in"], after=small_st["token"]) + dx[0:8, 0:LANE]

    def own_slices(st):
        return [lax.dynamic_index_in_dim(s, me, 0, keepdims=False) for s in st["srcs"]]

    upper_sts = [scatters[l] for l in reversed(range(1, DEPTH))] + [scatters["w_out"]]
    lands[:] = _exchange_wait_all("scatter_wait_upper", upper_sts, lands, [(0, 1)] * (DEPTH - 1) + [(1,)], after)
    for l in range(1, DEPTH):
        own[l] = own_slices(scatters[l])
    own[0] = [None, own_slices(scatters["w_out"])[0]]
    small_pass = _exchange_relay("gather_small_pass", small_st, lands[1])
    big["w_out"] = sharded("w_out", lands[1], jnp.stack([own[l][1] for l in range(DEPTH)]))
    upper = sharded("w_in", lands[0], jnp.stack([own[l][0] for l in range(1, DEPTH)]), layers=(1, DEPTH), tag="_upper")
    after = upper[1][0, 0:8, 0:LANE] + big["w_out"][1][0, 0:8, 0:LANE]
    small = _exchange_wait("gather_small_passed", small_pass, after)[0]
    outs = _adamw(small[:, None], *[_flatten_small(a, p)[None] for p in ("", "m_", "v_")], "adamw_small",
                  own=small_own[None])
    res = [_split_small(o[0], a) for o in outs]
    losses = lax.dynamic_update_index_in_dim(small[:, -1, 0], loss_own, me, 0)
    loss = jnp.sum(losses)

    off = _small_rows(a)[0]["b_ada"][0]
    dmod_all = lax.dynamic_update_index_in_dim(small[:, off:off + DEPTH * 3 * D_MODEL // SMALL_ROW],
                                               dmod.reshape(-1, SMALL_ROW), me, 0)
    dmod_all = dmod_all.reshape(N_DEV, DEPTH, 3 * D_MODEL).transpose(1, 0, 2)
    dmod_cols = lax.dynamic_slice_in_dim(dmod_all, me * ADA_COLS, ADA_COLS, axis=2)
    dmod_pad = jnp.pad(dmod_cols, ((0, 0), (0, LANE - N_DEV), (0, 0)))
    ct_pad = jnp.pad(c_all.T, ((0, 0), (0, LANE - N_DEV)))
    big["w_ada"] = sharded("w_ada", _ada_bwd(ct_pad, dmod_pad)[None])
    g_conv = jnp.concatenate([g["lru_conv_w"].reshape(DEPTH, 4, N_DEV, 64), g["ssd_conv_w"].reshape(DEPTH, 4, N_DEV, 192)],
                             axis=-1).transpose(2, 0, 1, 3)
    conv_parts = _all_to_all(g_conv, "scatter_conv")
    big["lru_conv_w"] = sharded("lru_conv_w", conv_parts[..., :64])
    big["ssd_conv_w"] = sharded("ssd_conv_w", conv_parts[..., 64:])

    after = outs[1] + big["w_ada"][1][0, 0:1, 0:1]
    scatters["w_in"]["lands"] = [lands[0]]
    lands[0] = _exchange_wait("scatter_wait_0_w_in", scatters["w_in"], after)[0]
    big["w_in"] = sharded("w_in", lands[0], own_slices(scatters["w_in"])[0][None], layers=(0, 1), prev=upper)

    out = [loss, dx[None]]
    for k in range(4):
        out += [big[n][k] if n in big else res[k][n] for n in WEIGHTS]
    return tuple(out)
```
